```python
import jax, jax.numpy as jnp
from jax import lax
import numpy as np

D_MODEL = 1024
BATCH = 8
SEQ = 8192
DEPTH = 1

PLE_DIM = 256
ATT_HEADS = 8
ATT_KV_HEADS = 2
HEAD_DIM = 64
WINDOW = 128
BLOCK = 128
D_ATT = ATT_HEADS * HEAD_DIM
D_KV = ATT_KV_HEADS * HEAD_DIM
D_RNN = D_MODEL - D_ATT
RNN_BLOCKS = 8
RNN_BLOCK_DIM = D_RNN // RNN_BLOCKS
RNN_CONV = 4
LRU_C = 8.0
D_MIX = D_ATT + D_RNN
D_IN = D_ATT + 2 * D_KV + 2 * D_RNN
D_FF = 3 * D_MODEL
FFN_CONV = 3
LN_EPS = 1e-5
ALPHA = float((2 * DEPTH) ** 0.25)
BETA = float((8 * DEPTH) ** -0.25)

kernel_name = "hymba_swa_sink_rglru_convglu_deepnorm"


def layer_norm(x, g, b):
    xf = x.astype(jnp.float32)
    mu = jnp.mean(xf, axis=-1, keepdims=True)
    var = jnp.mean(jnp.square(xf - mu), axis=-1, keepdims=True)
    y = (xf - mu) * lax.rsqrt(var + LN_EPS)
    return (y * g.astype(jnp.float32) + b.astype(jnp.float32)).astype(x.dtype)


def causal_dwconv(x, w, b):
    width = w.shape[0]
    y = lax.conv_general_dilated(
        x, w[:, None, :].astype(x.dtype), window_strides=(1,),
        padding=[(width - 1, 0)], dimension_numbers=('NWC', 'WIO', 'NWC'),
        feature_group_count=x.shape[-1])
    return y + b.astype(x.dtype)


def sliding_window_sink_attention(q, k, v, sinks):
    B, S = q.shape[0], q.shape[1]
    nb = S // BLOCK
    grp = ATT_HEADS // ATT_KV_HEADS
    qb = q.reshape(B, nb, BLOCK, ATT_KV_HEADS, grp, HEAD_DIM).astype(jnp.float32)

    def band(t):
        tb = t.reshape(B, nb, BLOCK, ATT_KV_HEADS, HEAD_DIM).astype(jnp.float32)
        prev = jnp.pad(tb, ((0, 0), (1, 0), (0, 0), (0, 0), (0, 0)))[:, :-1]
        return jnp.concatenate([prev, tb], axis=2)

    kb, vb = band(k), band(v)
    scores = jnp.einsum('bnqkgd,bnskd->bnkgqs', qb, kb) * (HEAD_DIM ** -0.5)
    qi = jnp.arange(BLOCK)[:, None]
    sj = jnp.arange(2 * BLOCK)[None, :]
    rel = qi + BLOCK - sj
    in_win = (rel >= 0) & (rel < WINDOW)
    key_ok = (jnp.arange(nb)[:, None] * BLOCK - BLOCK + sj) >= 0
    mask = in_win[None] & key_ok[:, None, :]
    scores = jnp.where(mask[None, :, None, None], scores, -jnp.inf)
    sink = sinks.astype(jnp.float32).reshape(ATT_KV_HEADS, grp)[None, None, :, :, None, None]
    sink = jnp.broadcast_to(sink, scores.shape[:-1] + (1,))
    probs = jax.nn.softmax(jnp.concatenate([scores, sink], axis=-1), axis=-1)[..., :-1]
    out = jnp.einsum('bnkgqs,bnskd->bnqkgd', probs, vb)
    return out.reshape(B, S, D_ATT).astype(q.dtype)


def rg_lru(x, w_a, b_a, w_x, b_x, lam):
    B, S, _ = x.shape
    xf = x.astype(jnp.float32)
    xb = xf.reshape(B, S, RNN_BLOCKS, RNN_BLOCK_DIM)
    r = jax.nn.sigmoid(jnp.einsum('bshi,hij->bshj', xb, w_a.astype(jnp.float32)).reshape(B, S, D_RNN)
                       + b_a.astype(jnp.float32))
    i = jax.nn.sigmoid(jnp.einsum('bshi,hij->bshj', xb, w_x.astype(jnp.float32)).reshape(B, S, D_RNN)
                       + b_x.astype(jnp.float32))
    log_a = -LRU_C * r * jax.nn.softplus(-lam.astype(jnp.float32))
    a = jnp.exp(log_a)
    b = jnp.sqrt(-jnp.expm1(2.0 * log_a)) * (i * xf)

    def combine(c1, c2):
        a1, b1 = c1
        a2, b2 = c2
        return a1 * a2, a2 * b1 + b2

    _, h = lax.associative_scan(combine, (a, b), axis=1)
    return h.astype(x.dtype)


def conv_glu_ffn(h, w_up, conv_w, conv_b, w_down):
    up = h @ w_up
    gate, val = jnp.split(up, 2, axis=-1)
    gate = causal_dwconv(gate, conv_w, conv_b)
    return (jax.nn.gelu(gate, approximate=True) * val) @ w_down


def _fwd_setup_inputs(seed: int = 0) -> dict:
    key = jax.random.key(seed)
    ks = jax.random.split(key, 24)
    f32 = jnp.float32
    nrm = lambda k, shape, s: jax.random.normal(k, shape, f32) * s
    L = DEPTH
    u = jax.random.uniform(ks[12], (L, D_RNN), f32, 0.9, 0.999)
    s = u ** (1.0 / LRU_C)
    lru_lambda = jnp.log(s) - jnp.log1p(-s)
    return {
        "x": nrm(ks[0], (BATCH, SEQ, D_MODEL), 1.0),
        "p": nrm(ks[1], (DEPTH, BATCH, SEQ, PLE_DIM), 1.0),
        "w_in": nrm(ks[2], (L, D_MODEL, D_IN), D_MODEL ** -0.5),
        "attn_sinks": nrm(ks[3], (L, ATT_HEADS), 0.5),
        "rnn_conv_w": nrm(ks[4], (L, RNN_CONV, D_RNN), RNN_CONV ** -0.5),
        "rnn_conv_b": nrm(ks[5], (L, D_RNN), 0.01),
        "gate_a_w": nrm(ks[6], (L, RNN_BLOCKS, RNN_BLOCK_DIM, RNN_BLOCK_DIM), RNN_BLOCK_DIM ** -0.5),
        "gate_a_b": nrm(ks[7], (L, D_RNN), 0.01),
        "gate_x_w": nrm(ks[8], (L, RNN_BLOCKS, RNN_BLOCK_DIM, RNN_BLOCK_DIM), RNN_BLOCK_DIM ** -0.5),
        "gate_x_b": nrm(ks[9], (L, D_RNN), 0.01),
        "lru_lambda": lru_lambda,
        "w_out": nrm(ks[10], (L, D_MIX, D_MODEL), BETA * D_MIX ** -0.5),
        "ln1_g": 1.0 + nrm(ks[11], (L, D_MODEL), 0.01),
        "ln1_b": nrm(ks[13], (L, D_MODEL), 0.01),
        "w_ffn_up": nrm(ks[14], (L, D_MODEL, 2 * D_FF), D_MODEL ** -0.5),
        "ffn_conv_w": nrm(ks[15], (L, FFN_CONV, D_FF), FFN_CONV ** -0.5),
        "ffn_conv_b": nrm(ks[16], (L, D_FF), 0.01),
        "w_ffn_down": nrm(ks[17], (L, D_FF, D_MODEL), BETA * D_FF ** -0.5),
        "ple_gate_w": nrm(ks[18], (L, D_MODEL, D_MODEL), D_MODEL ** -0.5),
        "ple_gate_b": nrm(ks[19], (L, D_MODEL), 0.01),
        "ple_proj": nrm(ks[20], (L, PLE_DIM, D_MODEL), BETA * PLE_DIM ** -0.5),
        "ln2_g": 1.0 + nrm(ks[21], (L, D_MODEL), 0.01),
        "ln2_b": nrm(ks[22], (L, D_MODEL), 0.01),
    }


def _fwd_reference(x, p, w_in, attn_sinks, rnn_conv_w, rnn_conv_b, gate_a_w, gate_a_b,
              gate_x_w, gate_x_b, lru_lambda, w_out, ln1_g, ln1_b, w_ffn_up,
              ffn_conv_w, ffn_conv_b, w_ffn_down, ple_gate_w, ple_gate_b, ple_proj,
              ln2_g, ln2_b):
    B, S, _ = x.shape
    splits = [D_ATT, D_ATT + D_KV, D_ATT + 2 * D_KV, D_ATT + 2 * D_KV + D_RNN]
    h = x
    for l in range(DEPTH):
        u = h @ w_in[l]
        q, k, v, xr, gr = jnp.split(u, splits, axis=-1)
        att = sliding_window_sink_attention(
            q.reshape(B, S, ATT_HEADS, HEAD_DIM),
            k.reshape(B, S, ATT_KV_HEADS, HEAD_DIM),
            v.reshape(B, S, ATT_KV_HEADS, HEAD_DIM),
            attn_sinks[l])
        xr = causal_dwconv(xr, rnn_conv_w[l], rnn_conv_b[l])
        rec = rg_lru(xr, gate_a_w[l], gate_a_b[l], gate_x_w[l], gate_x_b[l], lru_lambda[l])
        rec = rec * jax.nn.gelu(gr, approximate=True)
        mix = jnp.concatenate([att, rec], axis=-1) @ w_out[l]
        h = layer_norm(ALPHA * h + mix, ln1_g[l], ln1_b[l])
        ffn = conv_glu_ffn(h, w_ffn_up[l], ffn_conv_w[l], ffn_conv_b[l], w_ffn_down[l])
        ple = jax.nn.sigmoid(h @ ple_gate_w[l] + ple_gate_b[l]) * (p[l] @ ple_proj[l])
        h = layer_norm(ALPHA * h + ffn + ple, ln2_g[l], ln2_b[l])
    return h


import jax as _jax
import jax.numpy as _jnp

TWIN_FORMAT = 'train_step'
FWD_PARAMS = ['x', 'p', 'w_in', 'attn_sinks', 'rnn_conv_w', 'rnn_conv_b', 'gate_a_w', 'gate_a_b', 'gate_x_w', 'gate_x_b', 'lru_lambda', 'w_out', 'ln1_g', 'ln1_b', 'w_ffn_up', 'ffn_conv_w', 'ffn_conv_b', 'w_ffn_down', 'ple_gate_w', 'ple_gate_b', 'ple_proj', 'ln2_g', 'ln2_b']
TWIN_WEIGHTS = ['w_in', 'attn_sinks', 'rnn_conv_w', 'rnn_conv_b', 'gate_a_w', 'gate_a_b', 'gate_x_w', 'gate_x_b', 'lru_lambda', 'w_out', 'ln1_g', 'ln1_b', 'w_ffn_up', 'ffn_conv_w', 'ffn_conv_b', 'w_ffn_down', 'ple_gate_w', 'ple_gate_b', 'ple_proj', 'ln2_g', 'ln2_b']
TWIN_DIFF_INPUT = 'x'
TWIN_INPUTS = ['x', 'p', 'w_in', 'attn_sinks', 'rnn_conv_w', 'rnn_conv_b', 'gate_a_w', 'gate_a_b', 'gate_x_w', 'gate_x_b', 'lru_lambda', 'w_out', 'ln1_g', 'ln1_b', 'w_ffn_up', 'ffn_conv_w', 'ffn_conv_b', 'w_ffn_down', 'ple_gate_w', 'ple_gate_b', 'ple_proj', 'ln2_g', 'ln2_b', 'loss_target', 'm_w_in', 'm_attn_sinks', 'm_rnn_conv_w', 'm_rnn_conv_b', 'm_gate_a_w', 'm_gate_a_b', 'm_gate_x_w', 'm_gate_x_b', 'm_lru_lambda', 'm_w_out', 'm_ln1_g', 'm_ln1_b', 'm_w_ffn_up', 'm_ffn_conv_w', 'm_ffn_conv_b', 'm_w_ffn_down', 'm_ple_gate_w', 'm_ple_gate_b', 'm_ple_proj', 'm_ln2_g', 'm_ln2_b', 'v_w_in', 'v_attn_sinks', 'v_rnn_conv_w', 'v_rnn_conv_b', 'v_gate_a_w', 'v_gate_a_b', 'v_gate_x_w', 'v_gate_x_b', 'v_lru_lambda', 'v_w_out', 'v_ln1_g', 'v_ln1_b', 'v_w_ffn_up', 'v_ffn_conv_w', 'v_ffn_conv_b', 'v_w_ffn_down', 'v_ple_gate_w', 'v_ple_gate_b', 'v_ple_proj', 'v_ln2_g', 'v_ln2_b']
TWIN_OUTPUTS = ['loss', 'grad_x', 'grad_w_in', 'grad_attn_sinks', 'grad_rnn_conv_w', 'grad_rnn_conv_b', 'grad_gate_a_w', 'grad_gate_a_b', 'grad_gate_x_w', 'grad_gate_x_b', 'grad_lru_lambda', 'grad_w_out', 'grad_ln1_g', 'grad_ln1_b', 'grad_w_ffn_up', 'grad_ffn_conv_w', 'grad_ffn_conv_b', 'grad_w_ffn_down', 'grad_ple_gate_w', 'grad_ple_gate_b', 'grad_ple_proj', 'grad_ln2_g', 'grad_ln2_b', 'delta_w_in', 'delta_attn_sinks', 'delta_rnn_conv_w', 'delta_rnn_conv_b', 'delta_gate_a_w', 'delta_gate_a_b', 'delta_gate_x_w', 'delta_gate_x_b', 'delta_lru_lambda', 'delta_w_out', 'delta_ln1_g', 'delta_ln1_b', 'delta_w_ffn_up', 'delta_ffn_conv_w', 'delta_ffn_conv_b', 'delta_w_ffn_down', 'delta_ple_gate_w', 'delta_ple_gate_b', 'delta_ple_proj', 'delta_ln2_g', 'delta_ln2_b', 'new_m_w_in', 'new_m_attn_sinks', 'new_m_rnn_conv_w', 'new_m_rnn_conv_b', 'new_m_gate_a_w', 'new_m_gate_a_b', 'new_m_gate_x_w', 'new_m_gate_x_b', 'new_m_lru_lambda', 'new_m_w_out', 'new_m_ln1_g', 'new_m_ln1_b', 'new_m_w_ffn_up', 'new_m_ffn_conv_w', 'new_m_ffn_conv_b', 'new_m_w_ffn_down', 'new_m_ple_gate_w', 'new_m_ple_gate_b', 'new_m_ple_proj', 'new_m_ln2_g', 'new_m_ln2_b', 'new_v_w_in', 'new_v_attn_sinks', 'new_v_rnn_conv_w', 'new_v_rnn_conv_b', 'new_v_gate_a_w', 'new_v_gate_a_b', 'new_v_gate_x_w', 'new_v_gate_x_b', 'new_v_lru_lambda', 'new_v_w_out', 'new_v_ln1_g', 'new_v_ln1_b', 'new_v_w_ffn_up', 'new_v_ffn_conv_w', 'new_v_ffn_conv_b', 'new_v_w_ffn_down', 'new_v_ple_gate_w', 'new_v_ple_gate_b', 'new_v_ple_proj', 'new_v_ln2_g', 'new_v_ln2_b']
TWIN_LEAF_KINDS = {'loss': 'loss', 'grad_x': 'grad_x', 'grad_w_in': 'grad_w', 'grad_attn_sinks': 'grad_w', 'grad_rnn_conv_w': 'grad_w', 'grad_rnn_conv_b': 'grad_w', 'grad_gate_a_w': 'grad_w', 'grad_gate_a_b': 'grad_w', 'grad_gate_x_w': 'grad_w', 'grad_gate_x_b': 'grad_w', 'grad_lru_lambda': 'grad_w', 'grad_w_out': 'grad_w', 'grad_ln1_g': 'grad_w', 'grad_ln1_b': 'grad_w', 'grad_w_ffn_up': 'grad_w', 'grad_ffn_conv_w': 'grad_w', 'grad_ffn_conv_b': 'grad_w', 'grad_w_ffn_down': 'grad_w', 'grad_ple_gate_w': 'grad_w', 'grad_ple_gate_b': 'grad_w', 'grad_ple_proj': 'grad_w', 'grad_ln2_g': 'grad_w', 'grad_ln2_b': 'grad_w', 'delta_w_in': 'delta_w', 'delta_attn_sinks': 'delta_w', 'delta_rnn_conv_w': 'delta_w', 'delta_rnn_conv_b': 'delta_w', 'delta_gate_a_w': 'delta_w', 'delta_gate_a_b': 'delta_w', 'delta_gate_x_w': 'delta_w', 'delta_gate_x_b': 'delta_w', 'delta_lru_lambda': 'delta_w', 'delta_w_out': 'delta_w', 'delta_ln1_g': 'delta_w', 'delta_ln1_b': 'delta_w', 'delta_w_ffn_up': 'delta_w', 'delta_ffn_conv_w': 'delta_w', 'delta_ffn_conv_b': 'delta_w', 'delta_w_ffn_down': 'delta_w', 'delta_ple_gate_w': 'delta_w', 'delta_ple_gate_b': 'delta_w', 'delta_ple_proj': 'delta_w', 'delta_ln2_g': 'delta_w', 'delta_ln2_b': 'delta_w', 'new_m_w_in': 'new_m', 'new_m_attn_sinks': 'new_m', 'new_m_rnn_conv_w': 'new_m', 'new_m_rnn_conv_b': 'new_m', 'new_m_gate_a_w': 'new_m', 'new_m_gate_a_b': 'new_m', 'new_m_gate_x_w': 'new_m', 'new_m_gate_x_b': 'new_m', 'new_m_lru_lambda': 'new_m', 'new_m_w_out': 'new_m', 'new_m_ln1_g': 'new_m', 'new_m_ln1_b': 'new_m', 'new_m_w_ffn_up': 'new_m', 'new_m_ffn_conv_w': 'new_m', 'new_m_ffn_conv_b': 'new_m', 'new_m_w_ffn_down': 'new_m', 'new_m_ple_gate_w': 'new_m', 'new_m_ple_gate_b': 'new_m', 'new_m_ple_proj': 'new_m', 'new_m_ln2_g': 'new_m', 'new_m_ln2_b': 'new_m', 'new_v_w_in': 'new_v', 'new_v_attn_sinks': 'new_v', 'new_v_rnn_conv_w': 'new_v', 'new_v_rnn_conv_b': 'new_v', 'new_v_gate_a_w': 'new_v', 'new_v_gate_a_b': 'new_v', 'new_v_gate_x_w': 'new_v', 'new_v_gate_x_b': 'new_v', 'new_v_lru_lambda': 'new_v', 'new_v_w_out': 'new_v', 'new_v_ln1_g': 'new_v', 'new_v_ln1_b': 'new_v', 'new_v_w_ffn_up': 'new_v', 'new_v_ffn_conv_w': 'new_v', 'new_v_ffn_conv_b': 'new_v', 'new_v_w_ffn_down': 'new_v', 'new_v_ple_gate_w': 'new_v', 'new_v_ple_gate_b': 'new_v', 'new_v_ple_proj': 'new_v', 'new_v_ln2_g': 'new_v', 'new_v_ln2_b': 'new_v'}


def _forward(args):
    return _fwd_reference(*[args[k] for k in FWD_PARAMS])


def _output_shape():
    out = _jax.eval_shape(lambda: _forward(_fwd_setup_inputs(0)))
    return out.shape, out.dtype

N_MICROBATCH = 1
ADAM_LR = 0.001
ADAM_B1 = 0.9
ADAM_B2 = 0.999
ADAM_EPS = 1e-08
ADAM_WD = 0.01
ADAM_STEP = 10
PER_EXAMPLE_BATCH_AXIS = {'x': 0, 'p': 1, 'loss_target': 0}
SHARED_INPUTS = []
_WEIGHT_DTYPES = {'w_in': _jnp.float32, 'attn_sinks': _jnp.float32, 'rnn_conv_w': _jnp.float32, 'rnn_conv_b': _jnp.float32, 'gate_a_w': _jnp.float32, 'gate_a_b': _jnp.float32, 'gate_x_w': _jnp.float32, 'gate_x_b': _jnp.float32, 'lru_lambda': _jnp.float32, 'w_out': _jnp.float32, 'ln1_g': _jnp.float32, 'ln1_b': _jnp.float32, 'w_ffn_up': _jnp.float32, 'ffn_conv_w': _jnp.float32, 'ffn_conv_b': _jnp.float32, 'w_ffn_down': _jnp.float32, 'ple_gate_w': _jnp.float32, 'ple_gate_b': _jnp.float32, 'ple_proj': _jnp.float32, 'ln2_g': _jnp.float32, 'ln2_b': _jnp.float32}
MOMENT_SCALE = {'w_in': 4.889571e-02, 'attn_sinks': 2.824890e-02, 'rnn_conv_w': 7.009661e-02, 'rnn_conv_b': 8.380698e-01, 'gate_a_w': 3.330268e-02, 'gate_a_b': 2.665183e-02, 'gate_x_w': 5.950741e-02, 'gate_x_b': 3.103484e-02, 'lru_lambda': 4.556266e-02, 'w_out': 8.545966e-02, 'ln1_g': 6.291321e-01, 'ln1_b': 3.737527e-01, 'w_ffn_up': 4.519065e-02, 'ffn_conv_w': 4.509358e-02, 'ffn_conv_b': 4.356926e-02, 'w_ffn_down': 1.293954e-01, 'ple_gate_w': 2.426313e-02, 'ple_gate_b': 2.850425e-02, 'ple_proj': 1.051293e-01, 'ln2_g': 6.395879e+01, 'ln2_b': 1.899320e+00}


def _to_microbatches(a, axis):
    t = _jnp.moveaxis(a, axis, 0)
    t = t.reshape((N_MICROBATCH, t.shape[0] // N_MICROBATCH) + t.shape[1:])
    return _jnp.moveaxis(t, 1, axis + 1)


def setup_inputs(seed: int = 0) -> dict:
    inp = _fwd_setup_inputs(seed)
    key = _jax.random.fold_in(_jax.random.key(seed), 7919)
    shape, _ = _output_shape()
    out = dict(inp)
    out["loss_target"] = _jax.random.normal(_jax.random.fold_in(key, 0), shape, _jnp.float32)
    for i, name in enumerate(TWIN_WEIGHTS):
        w = inp[name].astype(_jnp.float32)
        if MOMENT_SCALE is None:
            s = _jnp.sqrt(_jnp.mean(_jnp.square(w)) + 1e-30)
        else:
            s = MOMENT_SCALE[name]
        km, kv = _jax.random.split(_jax.random.fold_in(key, i + 1))
        out[name] = w
        out["m_" + name] = s * _jax.random.normal(km, w.shape, _jnp.float32)
        out["v_" + name] = (s * s) * _jax.random.uniform(kv, w.shape, _jnp.float32, 0.5, 1.5)
    if N_MICROBATCH > 1:
        for name, axis in PER_EXAMPLE_BATCH_AXIS.items():
            out[name] = _to_microbatches(out[name], axis)
    return {'x': out['x'], 'p': out['p'], 'w_in': out['w_in'], 'attn_sinks': out['attn_sinks'], 'rnn_conv_w': out['rnn_conv_w'], 'rnn_conv_b': out['rnn_conv_b'], 'gate_a_w': out['gate_a_w'], 'gate_a_b': out['gate_a_b'], 'gate_x_w': out['gate_x_w'], 'gate_x_b': out['gate_x_b'], 'lru_lambda': out['lru_lambda'], 'w_out': out['w_out'], 'ln1_g': out['ln1_g'], 'ln1_b': out['ln1_b'], 'w_ffn_up': out['w_ffn_up'], 'ffn_conv_w': out['ffn_conv_w'], 'ffn_conv_b': out['ffn_conv_b'], 'w_ffn_down': out['w_ffn_down'], 'ple_gate_w': out['ple_gate_w'], 'ple_gate_b': out['ple_gate_b'], 'ple_proj': out['ple_proj'], 'ln2_g': out['ln2_g'], 'ln2_b': out['ln2_b'], 'loss_target': out['loss_target'], 'm_w_in': out['m_w_in'], 'm_attn_sinks': out['m_attn_sinks'], 'm_rnn_conv_w': out['m_rnn_conv_w'], 'm_rnn_conv_b': out['m_rnn_conv_b'], 'm_gate_a_w': out['m_gate_a_w'], 'm_gate_a_b': out['m_gate_a_b'], 'm_gate_x_w': out['m_gate_x_w'], 'm_gate_x_b': out['m_gate_x_b'], 'm_lru_lambda': out['m_lru_lambda'], 'm_w_out': out['m_w_out'], 'm_ln1_g': out['m_ln1_g'], 'm_ln1_b': out['m_ln1_b'], 'm_w_ffn_up': out['m_w_ffn_up'], 'm_ffn_conv_w': out['m_ffn_conv_w'], 'm_ffn_conv_b': out['m_ffn_conv_b'], 'm_w_ffn_down': out['m_w_ffn_down'], 'm_ple_gate_w': out['m_ple_gate_w'], 'm_ple_gate_b': out['m_ple_gate_b'], 'm_ple_proj': out['m_ple_proj'], 'm_ln2_g': out['m_ln2_g'], 'm_ln2_b': out['m_ln2_b'], 'v_w_in': out['v_w_in'], 'v_attn_sinks': out['v_attn_sinks'], 'v_rnn_conv_w': out['v_rnn_conv_w'], 'v_rnn_conv_b': out['v_rnn_conv_b'], 'v_gate_a_w': out['v_gate_a_w'], 'v_gate_a_b': out['v_gate_a_b'], 'v_gate_x_w': out['v_gate_x_w'], 'v_gate_x_b': out['v_gate_x_b'], 'v_lru_lambda': out['v_lru_lambda'], 'v_w_out': out['v_w_out'], 'v_ln1_g': out['v_ln1_g'], 'v_ln1_b': out['v_ln1_b'], 'v_w_ffn_up': out['v_w_ffn_up'], 'v_ffn_conv_w': out['v_ffn_conv_w'], 'v_ffn_conv_b': out['v_ffn_conv_b'], 'v_w_ffn_down': out['v_w_ffn_down'], 'v_ple_gate_w': out['v_ple_gate_w'], 'v_ple_gate_b': out['v_ple_gate_b'], 'v_ple_proj': out['v_ple_proj'], 'v_ln2_g': out['v_ln2_g'], 'v_ln2_b': out['v_ln2_b']}


def _loss(weights, diff, rest, loss_target):
    with _jax.named_scope("forward"):
        args = {**rest, TWIN_DIFF_INPUT: diff, **{k: w.astype(_WEIGHT_DTYPES[k]) for k, w in weights.items()}}
        y = _forward(args)
    with _jax.named_scope("loss_head"):
        err = _jnp.square(y.astype(_jnp.float32) - loss_target)
        return 0.5 * _jnp.sum(_jnp.mean(err, axis=-1)) if err.ndim else 0.5 * err


def _adamw(w, g, m, v):
    m = ADAM_B1 * m + (1.0 - ADAM_B1) * g
    v = ADAM_B2 * v + (1.0 - ADAM_B2) * _jnp.square(g)
    m_hat = m / (1.0 - ADAM_B1 ** ADAM_STEP)
    v_hat = v / (1.0 - ADAM_B2 ** ADAM_STEP)
    delta = -ADAM_LR * (m_hat / (_jnp.sqrt(v_hat) + ADAM_EPS) + ADAM_WD * w)
    return delta, m, v


def reference(x, p, w_in, attn_sinks, rnn_conv_w, rnn_conv_b, gate_a_w, gate_a_b, gate_x_w, gate_x_b, lru_lambda, w_out, ln1_g, ln1_b, w_ffn_up, ffn_conv_w, ffn_conv_b, w_ffn_down, ple_gate_w, ple_gate_b, ple_proj, ln2_g, ln2_b, loss_target, m_w_in, m_attn_sinks, m_rnn_conv_w, m_rnn_conv_b, m_gate_a_w, m_gate_a_b, m_gate_x_w, m_gate_x_b, m_lru_lambda, m_w_out, m_ln1_g, m_ln1_b, m_w_ffn_up, m_ffn_conv_w, m_ffn_conv_b, m_w_ffn_down, m_ple_gate_w, m_ple_gate_b, m_ple_proj, m_ln2_g, m_ln2_b, v_w_in, v_attn_sinks, v_rnn_conv_w, v_rnn_conv_b, v_gate_a_w, v_gate_a_b, v_gate_x_w, v_gate_x_b, v_lru_lambda, v_w_out, v_ln1_g, v_ln1_b, v_w_ffn_up, v_ffn_conv_w, v_ffn_conv_b, v_w_ffn_down, v_ple_gate_w, v_ple_gate_b, v_ple_proj, v_ln2_g, v_ln2_b):
    given = dict(x=x, p=p, w_in=w_in, attn_sinks=attn_sinks, rnn_conv_w=rnn_conv_w, rnn_conv_b=rnn_conv_b, gate_a_w=gate_a_w, gate_a_b=gate_a_b, gate_x_w=gate_x_w, gate_x_b=gate_x_b, lru_lambda=lru_lambda, w_out=w_out, ln1_g=ln1_g, ln1_b=ln1_b, w_ffn_up=w_ffn_up, ffn_conv_w=ffn_conv_w, ffn_conv_b=ffn_conv_b, w_ffn_down=w_ffn_down, ple_gate_w=ple_gate_w, ple_gate_b=ple_gate_b, ple_proj=ple_proj, ln2_g=ln2_g, ln2_b=ln2_b, loss_target=loss_target, m_w_in=m_w_in, m_attn_sinks=m_attn_sinks, m_rnn_conv_w=m_rnn_conv_w, m_rnn_conv_b=m_rnn_conv_b, m_gate_a_w=m_gate_a_w, m_gate_a_b=m_gate_a_b, m_gate_x_w=m_gate_x_w, m_gate_x_b=m_gate_x_b, m_lru_lambda=m_lru_lambda, m_w_out=m_w_out, m_ln1_g=m_ln1_g, m_ln1_b=m_ln1_b, m_w_ffn_up=m_w_ffn_up, m_ffn_conv_w=m_ffn_conv_w, m_ffn_conv_b=m_ffn_conv_b, m_w_ffn_down=m_w_ffn_down, m_ple_gate_w=m_ple_gate_w, m_ple_gate_b=m_ple_gate_b, m_ple_proj=m_ple_proj, m_ln2_g=m_ln2_g, m_ln2_b=m_ln2_b, v_w_in=v_w_in, v_attn_sinks=v_attn_sinks, v_rnn_conv_w=v_rnn_conv_w, v_rnn_conv_b=v_rnn_conv_b, v_gate_a_w=v_gate_a_w, v_gate_a_b=v_gate_a_b, v_gate_x_w=v_gate_x_w, v_gate_x_b=v_gate_x_b, v_lru_lambda=v_lru_lambda, v_w_out=v_w_out, v_ln1_g=v_ln1_g, v_ln1_b=v_ln1_b, v_w_ffn_up=v_w_ffn_up, v_ffn_conv_w=v_ffn_conv_w, v_ffn_conv_b=v_ffn_conv_b, v_w_ffn_down=v_w_ffn_down, v_ple_gate_w=v_ple_gate_w, v_ple_gate_b=v_ple_gate_b, v_ple_proj=v_ple_proj, v_ln2_g=v_ln2_g, v_ln2_b=v_ln2_b)
    weights = {n: given[n] for n in TWIN_WEIGHTS}
    shared = {n: given[n] for n in SHARED_INPUTS}
    per_example = {n: given[n] for n in ['x', 'p']}
    grad_fn = _jax.value_and_grad(_loss, argnums=(0, 1))

    def one_microbatch(ex, loss_target):
        ex = dict(ex)
        diff = ex.pop(TWIN_DIFF_INPUT)
        return grad_fn(weights, diff, {**shared, **ex}, loss_target)

    if N_MICROBATCH == 1:
        loss, (grad_w, grad_x) = one_microbatch(per_example, given["loss_target"])
    else:
        def body(carry, xs):
            loss_sum, grad_sum = carry
            l_k, (gw_k, gx_k) = one_microbatch(xs[0], xs[1])
            with _jax.named_scope("update"):
                return (loss_sum + l_k, _jax.tree.map(_jnp.add, grad_sum, gw_k)), gx_k

        init = (_jnp.zeros((), _jnp.float32), _jax.tree.map(_jnp.zeros_like, weights))
        (loss, grad_w), grad_x = _jax.lax.scan(body, init, (per_example, given["loss_target"]))
    with _jax.named_scope("update"):
        delta_w, new_m, new_v = {}, {}, {}
        for n in TWIN_WEIGHTS:
            delta_w[n], new_m[n], new_v[n] = _adamw(weights[n], grad_w[n], given["m_" + n], given["v_" + n])
    return (loss, grad_x, *[grad_w[n] for n in TWIN_WEIGHTS], *[delta_w[n] for n in TWIN_WEIGHTS],
            *[new_m[n] for n in TWIN_WEIGHTS], *[new_v[n] for n in TWIN_WEIGHTS])
```

```python
import functools

import jax
import jax.numpy as jnp
from jax import lax
from jax.experimental import pallas as pl
from jax.experimental.pallas import tpu as pltpu

F32 = jnp.float32
BF16 = jnp.bfloat16

D_MODEL = 1024
D_ATT = 512
D_KV = 128
HEAD_DIM = 64
N_HEADS = 8
N_KV = 2
D_RNN = 512
RNN_BLOCKS = 8
D_IN = 1792
D_FF = 3072
PLE_DIM = 256
QBLK = 128
N_DEV = 8
ALPHA = float(2 ** 0.25)
LN_EPS = 1e-5
LRU_C = 8.0
ADAM_LR, ADAM_B1, ADAM_B2, ADAM_EPS, ADAM_WD, ADAM_STEP = 0.001, 0.9, 0.999, 1e-08, 0.01, 10

V7X_VMEM_LIMIT = 56 * 1024 * 1024
MESH = pl.DeviceIdType.MESH


def _params(*sem, vmem=V7X_VMEM_LIMIT):
    return pltpu.CompilerParams(dimension_semantics=sem or None, vmem_limit_bytes=vmem)


def _resident(shape):
    return pl.BlockSpec(shape, lambda *_: (0,) * len(shape), pipeline_mode=pl.Buffered(1))


def _rows(tb, cols):
    return pl.BlockSpec((tb, cols), lambda i: (i, 0))


def _acc(shape):
    return pl.BlockSpec(shape, lambda *_: (0,) * len(shape))


def _dot(a, b):
    return jnp.dot(a, b, preferred_element_type=F32)


def _dot_nt(a, b):
    return lax.dot_general(a, b, (((1,), (1,)), ((), ())), preferred_element_type=F32)


def _dot_tn(a, b):
    return lax.dot_general(a, b, (((0,), (0,)), ((), ())), preferred_element_type=F32)


def _sigmoid(x):
    return 1.0 / (1.0 + jnp.exp(-x))


_GELU_C = 0.7978845608028654
_GELU_K = 0.044715


def _gelu_and_grad(x):
    t = jnp.tanh(_GELU_C * (x + _GELU_K * x * x * x))
    g = 0.5 * x * (1.0 + t)
    dg = 0.5 * (1.0 + t) + 0.5 * x * (1.0 - t * t) * _GELU_C * (1.0 + 3.0 * _GELU_K * x * x)
    return g, dg


def _gelu(x):
    return 0.5 * x * (1.0 + jnp.tanh(_GELU_C * (x + _GELU_K * x * x * x)))


def _ln_stats(z):
    mu = jnp.mean(z, axis=-1, keepdims=True)
    zc = z - mu
    var = jnp.mean(zc * zc, axis=-1, keepdims=True)
    rstd = lax.rsqrt(var + LN_EPS)
    return zc * rstd, rstd


def _ln_bwd(dy, xhat, rstd, g):
    dxh = dy * g
    m1 = jnp.mean(dxh, axis=-1, keepdims=True)
    m2 = jnp.mean(dxh * xhat, axis=-1, keepdims=True)
    return rstd * (dxh - m1 - xhat * m2)


def _softplus_neg(lam):
    u = jnp.exp(-jnp.abs(lam))
    w = 1.0 + u
    d = w - 1.0
    log1p_u = jnp.where(d == 0.0, u, jnp.log(w) * (u / jnp.where(d == 0.0, 1.0, d)))
    return jnp.maximum(-lam, 0.0) + log1p_u


def _shift_down(x, halo, s):
    xs = pltpu.roll(x, s, 0)
    hs = pltpu.roll(halo, s, 0)
    row8 = lax.broadcasted_iota(jnp.int32, hs.shape, 0)
    first = jnp.where(row8 < s, hs, xs[:8])
    return jnp.concatenate([first, xs[8:]], axis=0)


def _shift_up(x, halo, s):
    n = x.shape[0]
    xs = pltpu.roll(x, n - s, 0)
    hs = pltpu.roll(halo, 8 - s, 0)
    row8 = lax.broadcasted_iota(jnp.int32, hs.shape, 0)
    last = jnp.where(row8 >= 8 - s, hs, xs[n - 8:])
    return jnp.concatenate([xs[:n - 8], last], axis=0)


def _row_sum(x):
    return jnp.sum(x, axis=0, keepdims=True)


def _put_rows(acc_ref, rows):
    row8 = lax.broadcasted_iota(jnp.int32, acc_ref.shape, 0)
    upd = jnp.zeros(acc_ref.shape, F32)
    for r, vec in enumerate(rows):
        upd = jnp.where(row8 == r, vec, upd)
    acc_ref[...] += upd


def _in_proj(x, w_in):
    S = x.shape[0]
    tb = min(512, S)

    def body(x_ref, w_ref, q_ref, k_ref, v_ref, xr_ref, gr_ref):
        u = _dot(x_ref[...].astype(BF16), w_ref[...])
        q_ref[...] = (u[:, :D_ATT] * (HEAD_DIM ** -0.5)).astype(BF16)
        k_ref[...] = u[:, D_ATT:D_ATT + D_KV].astype(BF16)
        v_ref[...] = u[:, D_ATT + D_KV:D_ATT + 2 * D_KV].astype(BF16)
        xr_ref[...] = u[:, D_ATT + 2 * D_KV:D_ATT + 2 * D_KV + D_RNN]
        gr_ref[...] = u[:, D_ATT + 2 * D_KV + D_RNN:]

    return pl.pallas_call(
        body, name="in_proj", grid=(S // tb,),
        in_specs=[_rows(tb, D_MODEL), _resident((D_MODEL, D_IN))],
        out_specs=[_rows(tb, D_ATT), _rows(tb, D_KV), _rows(tb, D_KV), _rows(tb, D_RNN), _rows(tb, D_RNN)],
        out_shape=[jax.ShapeDtypeStruct((S, D_ATT), BF16), jax.ShapeDtypeStruct((S, D_KV), BF16),
                   jax.ShapeDtypeStruct((S, D_KV), BF16), jax.ShapeDtypeStruct((S, D_RNN), F32),
                   jax.ShapeDtypeStruct((S, D_RNN), F32)],
        compiler_params=_params("parallel"),
    )(x, w_in)


def _band_mask(i):
    qi = lax.broadcasted_iota(jnp.int32, (QBLK, 2 * QBLK), 0)
    sj = lax.broadcasted_iota(jnp.int32, (QBLK, 2 * QBLK), 1)
    return (sj > qi) & (sj <= qi + QBLK) & ((sj >= QBLK) | (i > 0))


def _attn_specs():
    cur = lambda i: (i, 0)
    prev = lambda i: (jnp.maximum(i - 1, 0), 0)
    return [pl.BlockSpec((QBLK, D_KV), cur), pl.BlockSpec((QBLK, D_KV), prev),
            pl.BlockSpec((QBLK, D_KV), cur), pl.BlockSpec((QBLK, D_KV), prev)]


def _attn_fwd(q, k, v, sinks):
    S = q.shape[0]

    def body(sink_ref, q_ref, kc_ref, kp_ref, vc_ref, vp_ref, o_ref):
        valid = _band_mask(pl.program_id(0))
        outs = []
        qv = q_ref[...]
        kall = jnp.concatenate([kp_ref[...], kc_ref[...]], axis=0)
        vall = jnp.concatenate([vp_ref[...], vc_ref[...]], axis=0)
        for g in range(N_KV):
            kcat = kall[:, g * HEAD_DIM:(g + 1) * HEAD_DIM]
            vcat = vall[:, g * HEAD_DIM:(g + 1) * HEAD_DIM]
            for hh in range(N_HEADS // N_KV):
                h = g * (N_HEADS // N_KV) + hh
                s = _dot_nt(qv[:, h * HEAD_DIM:(h + 1) * HEAD_DIM], kcat)
                s = jnp.where(valid, s, -1e30)
                sink = sink_ref[h]
                m = jnp.maximum(jnp.max(s, axis=1, keepdims=True), sink)
                p = jnp.exp(s - m)
                l = jnp.sum(p, axis=1, keepdims=True) + jnp.exp(sink - m)
                outs.append(_dot(p.astype(BF16), vcat) / l)
        o_ref[...] = jnp.concatenate(outs, axis=1).astype(BF16)

    return pl.pallas_call(
        body, name="attn_fwd", grid=(S // QBLK,),
        in_specs=[pl.BlockSpec(memory_space=pltpu.SMEM), _rows(QBLK, D_ATT)] + _attn_specs(),
        out_specs=_rows(QBLK, D_ATT),
        out_shape=jax.ShapeDtypeStruct((S, D_ATT), BF16),
        compiler_params=_params("parallel"),
    )(sinks, q, k, k, v, v)


def _w_rows(w_ref):
    return [w_ref[k:k + 1, :] for k in range(w_ref.shape[0])]


def _conv4(x, halo, w, b):
    y = b + w[3] * x
    for s in (1, 2, 3):
        y = y + w[3 - s] * _shift_down(x, halo, s)
    return y


def _rnn_gates(xc, wa, wx, ba, bx, sp):
    xcb = xc.astype(BF16)
    r = _sigmoid(_dot(xcb, wa) + ba)
    ig = _sigmoid(_dot(xcb, wx) + bx)
    la = -LRU_C * r * sp
    a = jnp.exp(la)
    t = jnp.tanh(la)
    f = jnp.sqrt(-2.0 * t / (1.0 - t))
    return r, ig, a, f


def _rnn_fwd(xr, gr, conv_w, conv_b, wa, wx, ba, bx, lam):
    S = xr.shape[0]
    tb = min(256, S)

    def body(xr_ref, gr_ref, cw_ref, cb_ref, wa_ref, wx_ref, ba_ref, bx_ref, lam_ref, rec_ref, h_ref,
             halo_s, hc_s, a_s, b_s):
        @pl.when(pl.program_id(0) == 0)
        def _():
            halo_s[...] = jnp.zeros_like(halo_s)
            hc_s[...] = jnp.zeros_like(hc_s)

        x = xr_ref[...]
        xc = _conv4(x, halo_s[...], _w_rows(cw_ref), cb_ref[...])
        halo_s[...] = x[tb - 8:]
        _, ig, a, f = _rnn_gates(xc, wa_ref[...], wx_ref[...], ba_ref[...], bx_ref[...], _softplus_neg(lam_ref[...]))
        a_s[...] = a
        b_s[...] = f * ig * xc
        row8 = lax.broadcasted_iota(jnp.int32, (8, D_RNN), 0)

        def tile(t, hc):
            o = pl.multiple_of(t * 8, 8)
            at = a_s[pl.ds(o, 8), :]
            bt = b_s[pl.ds(o, 8), :]
            for s in (1, 2, 4):
                keep = row8 >= s
                a_sh = jnp.where(keep, pltpu.roll(at, s, 0), 1.0)
                b_sh = jnp.where(keep, pltpu.roll(bt, s, 0), 0.0)
                bt = at * b_sh + bt
                at = at * a_sh
            ht = at * hc + bt
            b_s[pl.ds(o, 8), :] = ht
            return _row_sum(jnp.where(row8 == 7, ht, 0.0))

        hc_s[0:1, :] = lax.fori_loop(0, tb // 8, tile, hc_s[0:1, :])
        h = b_s[...]
        h_ref[...] = h
        rec_ref[...] = (h * _gelu(gr_ref[...])).astype(BF16)

    vec = _resident((1, D_RNN))
    return pl.pallas_call(
        body, name="rnn_fwd", grid=(S // tb,),
        in_specs=[_rows(tb, D_RNN), _rows(tb, D_RNN), _resident((4, D_RNN)), vec,
                  _resident((D_RNN, D_RNN)), _resident((D_RNN, D_RNN)), vec, vec, vec],
        out_specs=[_rows(tb, D_RNN), _rows(tb, D_RNN)],
        out_shape=[jax.ShapeDtypeStruct((S, D_RNN), BF16), jax.ShapeDtypeStruct((S, D_RNN), F32)],
        scratch_shapes=[pltpu.VMEM((8, D_RNN), F32), pltpu.VMEM((8, D_RNN), F32),
                        pltpu.VMEM((tb, D_RNN), F32), pltpu.VMEM((tb, D_RNN), F32)],
        compiler_params=_params("arbitrary"),
    )(xr, gr, conv_w, conv_b, wa, wx, ba, bx, lam)


def _mix_ln1_up(x, cat, w_out, ln1_g, ln1_b, w_up):
    S = x.shape[0]
    tb = min(256, S)
    nblk, _, wblk = w_up.shape
    half = nblk // 2

    def body(x_ref, cat_ref, wo_ref, g_ref, b_ref, wu_ref, z1_ref, h1_ref, gate_ref, val_ref):
        z1 = ALPHA * x_ref[...] + _dot(cat_ref[...], wo_ref[...])
        z1_ref[...] = z1
        xhat, _ = _ln_stats(z1)
        h1b = (xhat * g_ref[...] + b_ref[...]).astype(BF16)
        h1_ref[...] = h1b
        for j in range(nblk):
            up = _dot(h1b, wu_ref[j]).astype(BF16)
            dst = gate_ref if j < half else val_ref
            jj = j % half
            dst[:, jj * wblk:(jj + 1) * wblk] = up

    vec = _resident((1, D_MODEL))
    return pl.pallas_call(
        body, name="mix_ln1_up", grid=(S // tb,),
        in_specs=[_rows(tb, D_MODEL), _rows(tb, D_MODEL), _resident((D_MODEL, D_MODEL)), vec, vec,
                  _resident(w_up.shape)],
        out_specs=[_rows(tb, D_MODEL), _rows(tb, D_MODEL), _rows(tb, D_FF), _rows(tb, D_FF)],
        out_shape=[jax.ShapeDtypeStruct((S, D_MODEL), F32), jax.ShapeDtypeStruct((S, D_MODEL), BF16),
                   jax.ShapeDtypeStruct((S, D_FF), BF16), jax.ShapeDtypeStruct((S, D_FF), BF16)],
        compiler_params=_params("parallel"),
    )(x, cat, w_out, ln1_g, ln1_b, w_up)


def _tail(gate, val, z1, p, tgt, fcw, fcb, w_down, w_pg, b_pg, w_pp, ln1_g, ln1_b, ln2_g, ln2_b):
    S = z1.shape[0]
    tb = min(256, S)
    t16 = tb // 16

    def body(gc_ref, gp_ref, val_ref, z1_ref, p_ref, t_ref, fcw_ref, fcb_ref, wd_ref, wpg_ref, bpg_ref, wpp_ref,
             g1_ref, b1_ref, g2_ref, b2_ref,
             act_ref, dz2_ref, dpre_ref, dpp_ref, dgc_ref, dval_ref, dh1_ref, acc_ref):
        i = pl.program_id(0)

        @pl.when(i == 0)
        def _():
            acc_ref[...] = jnp.zeros_like(acc_ref)

        gate = gc_ref[...].astype(F32)
        halo = jnp.where(i > 0, gp_ref[...].astype(F32)[8:16], 0.0)
        w = _w_rows(fcw_ref)
        gcv = fcb_ref[...] + w[2] * gate + w[1] * _shift_down(gate, halo, 1) + w[0] * _shift_down(gate, halo, 2)
        gl, dgl = _gelu_and_grad(gcv)
        val = val_ref[...].astype(F32)
        act = (gl * val).astype(BF16)
        act_ref[...] = act
        ffn = _dot(act, wd_ref[...])

        xhat1, _ = _ln_stats(z1_ref[...])
        h1 = xhat1 * g1_ref[...] + b1_ref[...]
        sg = _sigmoid(_dot(h1.astype(BF16), wpg_ref[...]) + bpg_ref[...])
        pp = _dot(p_ref[...].astype(BF16), wpp_ref[...])
        z2 = ALPHA * h1 + ffn + sg * pp
        xhat2, rstd2 = _ln_stats(z2)
        y = xhat2 * g2_ref[...] + b2_ref[...]
        err = y - t_ref[...]
        dy = err * (1.0 / D_MODEL)
        loss = 0.5 * jnp.sum(jnp.sum(err * err, axis=1, keepdims=True), axis=0, keepdims=True) * (1.0 / D_MODEL)
        dz2 = _ln_bwd(dy, xhat2, rstd2, g2_ref[...])
        dz2b = dz2.astype(BF16)
        dz2_ref[...] = dz2b
        dpre = dz2 * pp * sg * (1.0 - sg)
        dpreb = dpre.astype(BF16)
        dpre_ref[...] = dpreb
        dpp_ref[...] = (dz2 * sg).astype(BF16)
        dh1_ref[...] = ALPHA * dz2 + _dot_nt(dpreb, wpg_ref[...])
        dact = _dot_nt(dz2b, wd_ref[...])
        dval_ref[...] = (dact * gl).astype(BF16)
        dgc_ref[...] = (dact * val * dgl).astype(BF16)
        _put_rows(acc_ref, [_row_sum(dy * xhat2), _row_sum(dy), _row_sum(dpre),
                            jnp.broadcast_to(loss, (1, D_MODEL))])

    vec = _resident((1, D_MODEL))
    prev16 = pl.BlockSpec((16, D_FF), lambda i: (jnp.maximum(i * t16 - 1, 0), 0))
    return pl.pallas_call(
        body, name="tail", grid=(S // tb,),
        in_specs=[_rows(tb, D_FF), prev16, _rows(tb, D_FF), _rows(tb, D_MODEL), _rows(tb, PLE_DIM), _rows(tb, D_MODEL),
                  _resident((3, D_FF)), _resident((1, D_FF)), _resident((D_FF, D_MODEL)),
                  _resident((D_MODEL, D_MODEL)), vec, _resident((PLE_DIM, D_MODEL)), vec, vec, vec, vec],
        out_specs=[_rows(tb, D_FF), _rows(tb, D_MODEL), _rows(tb, D_MODEL), _rows(tb, D_MODEL), _rows(tb, D_FF),
                   _rows(tb, D_FF), _rows(tb, D_MODEL), _acc((8, D_MODEL))],
        out_shape=[jax.ShapeDtypeStruct((S, D_FF), BF16), jax.ShapeDtypeStruct((S, D_MODEL), BF16),
                   jax.ShapeDtypeStruct((S, D_MODEL), BF16), jax.ShapeDtypeStruct((S, D_MODEL), BF16),
                   jax.ShapeDtypeStruct((S, D_FF), BF16), jax.ShapeDtypeStruct((S, D_FF), BF16),
                   jax.ShapeDtypeStruct((S, D_MODEL), F32), jax.ShapeDtypeStruct((8, D_MODEL), F32)],
        compiler_params=_params("arbitrary"),
    )(gate, gate, val, z1, p, tgt, fcw, fcb, w_down, w_pg, b_pg, w_pp, ln1_g, ln1_b, ln2_g, ln2_b)


def _tn_matmul(a, b, name, out_dtype=BF16, tm=512, ts=512):
    S, M = a.shape
    N = b.shape[1]
    tm = min(tm, M)
    ts = min(ts, S)
    nk = S // ts

    def body(a_ref, b_ref, o_ref, acc_ref):
        k = pl.program_id(1)

        @pl.when(k == 0)
        def _():
            acc_ref[...] = jnp.zeros_like(acc_ref)

        acc_ref[...] += _dot_tn(a_ref[...].astype(BF16), b_ref[...].astype(BF16))

        @pl.when(k == nk - 1)
        def _():
            o_ref[...] = acc_ref[...].astype(out_dtype)

    return pl.pallas_call(
        body, name=name, grid=(M // tm, nk),
        in_specs=[pl.BlockSpec((ts, tm), lambda i, k: (k, i)), pl.BlockSpec((ts, N), lambda i, k: (k, 0))],
        out_specs=pl.BlockSpec((tm, N), lambda i, k: (i, 0)),
        out_shape=jax.ShapeDtypeStruct((M, N), out_dtype),
        scratch_shapes=[pltpu.VMEM((tm, N), F32)],
        compiler_params=_params("parallel", "arbitrary"),
    )(a, b)


def _up_bwd(dgc, gate, dval, dh1p, z1, w_up, fcw, w_out, ln1_g):
    S = z1.shape[0]
    tb = min(256, S)
    t16 = tb // 16
    n16 = S // 16
    nblk, _, wblk = w_up.shape
    half = nblk // 2
    nsteps = S // tb

    def body(dgc_ref, dgn_ref, gc_ref, gp_ref, dval_ref, dh1p_ref, z1_ref, wu_ref, fcw_ref, wo_ref, g1_ref,
             dgate_ref, dz1_ref, dz1b_ref, datt_ref, drec_ref, accf_ref, accd_ref):
        i = pl.program_id(0)

        @pl.when(i == 0)
        def _():
            accf_ref[...] = jnp.zeros_like(accf_ref)
            accd_ref[...] = jnp.zeros_like(accd_ref)

        dg = dgc_ref[...].astype(F32)
        nxt = jnp.where(i < nsteps - 1, dgn_ref[...].astype(F32)[0:8], 0.0)
        w = _w_rows(fcw_ref)
        dgate = (w[2] * dg + w[1] * _shift_up(dg, nxt, 1) + w[0] * _shift_up(dg, nxt, 2)).astype(BF16)
        dgate_ref[...] = dgate
        gate = gc_ref[...].astype(F32)
        halo = jnp.where(i > 0, gp_ref[...].astype(F32)[8:16], 0.0)
        _put_rows(accf_ref, [_row_sum(dg * _shift_down(gate, halo, 2)), _row_sum(dg * _shift_down(gate, halo, 1)),
                             _row_sum(dg * gate), _row_sum(dg)])

        dh1 = dh1p_ref[...]
        for j in range(nblk):
            src = dgate if j < half else dval_ref[...]
            jj = j % half
            dh1 = dh1 + _dot_nt(src[:, jj * wblk:(jj + 1) * wblk], wu_ref[j])
        xhat1, rstd1 = _ln_stats(z1_ref[...])
        dz1 = _ln_bwd(dh1, xhat1, rstd1, g1_ref[...])
        dz1_ref[...] = dz1
        dz1b = dz1.astype(BF16)
        dz1b_ref[...] = dz1b
        dcat = _dot_nt(dz1b, wo_ref[...])
        datt_ref[...] = dcat[:, :D_ATT].astype(BF16)
        drec_ref[...] = dcat[:, D_ATT:]
        _put_rows(accd_ref, [_row_sum(dh1 * xhat1), _row_sum(dh1)])

    prev16 = pl.BlockSpec((16, D_FF), lambda i: (jnp.maximum(i * t16 - 1, 0), 0))
    next16 = pl.BlockSpec((16, D_FF), lambda i: (jnp.minimum((i + 1) * t16, n16 - 1), 0))
    return pl.pallas_call(
        body, name="up_bwd", grid=(nsteps,),
        in_specs=[_rows(tb, D_FF), next16, _rows(tb, D_FF), prev16, _rows(tb, D_FF), _rows(tb, D_MODEL),
                  _rows(tb, D_MODEL), _resident(w_up.shape), _resident((3, D_FF)), _resident((D_MODEL, D_MODEL)),
                  _resident((1, D_MODEL))],
        out_specs=[_rows(tb, D_FF), _rows(tb, D_MODEL), _rows(tb, D_MODEL), _rows(tb, D_ATT), _rows(tb, D_RNN),
                   _acc((8, D_FF)), _acc((8, D_MODEL))],
        out_shape=[jax.ShapeDtypeStruct((S, D_FF), BF16), jax.ShapeDtypeStruct((S, D_MODEL), F32),
                   jax.ShapeDtypeStruct((S, D_MODEL), BF16), jax.ShapeDtypeStruct((S, D_ATT), BF16),
                   jax.ShapeDtypeStruct((S, D_RNN), F32), jax.ShapeDtypeStruct((8, D_FF), F32),
                   jax.ShapeDtypeStruct((8, D_MODEL), F32)],
        compiler_params=_params("arbitrary"),
    )(dgc, dgc, gate, gate, dval, dh1p, z1, w_up, fcw, w_out, ln1_g)


def _up_grad(h1b, dgate, dval, nblk, wblk):
    S = h1b.shape[0]
    ts = min(512, S)
    nk = S // ts
    half = nblk // 2

    def body(a_ref, g_ref, v_ref, o_ref, acc_ref):
        j = pl.program_id(0)
        k = pl.program_id(1)

        @pl.when(k == 0)
        def _():
            acc_ref[...] = jnp.zeros_like(acc_ref)

        @pl.when(j < half)
        def _():
            acc_ref[...] += _dot_tn(a_ref[...], g_ref[...])

        @pl.when(j >= half)
        def _():
            acc_ref[...] += _dot_tn(a_ref[...], v_ref[...])

        @pl.when(k == nk - 1)
        def _():
            o_ref[0] = acc_ref[...].astype(BF16)

    return pl.pallas_call(
        body, name="up_grad", grid=(nblk, nk),
        in_specs=[pl.BlockSpec((ts, D_MODEL), lambda j, k: (k, 0)),
                  pl.BlockSpec((ts, wblk), lambda j, k: (k, jnp.minimum(j, half - 1))),
                  pl.BlockSpec((ts, wblk), lambda j, k: (k, jnp.maximum(j - half, 0)))],
        out_specs=pl.BlockSpec((1, D_MODEL, wblk), lambda j, k: (j, 0, 0)),
        out_shape=jax.ShapeDtypeStruct((nblk, D_MODEL, wblk), BF16),
        scratch_shapes=[pltpu.VMEM((D_MODEL, wblk), F32)],
        compiler_params=_params("parallel", "arbitrary"),
    )(h1b, dgate, dval)


def _attn_bwd(q, k, v, do, sinks):
    S = q.shape[0]
    grp = N_HEADS // N_KV

    def body(sink_ref, q_ref, kc_ref, kp_ref, vc_ref, vp_ref, do_ref, dq_ref, dkc_ref, dkp_ref, dvc_ref, dvp_ref,
             ds_ref):
        i = pl.program_id(0)

        @pl.when(i == 0)
        def _():
            ds_ref[...] = jnp.zeros_like(ds_ref)

        valid = _band_mask(i)
        row8 = lax.broadcasted_iota(jnp.int32, (8, 128), 0)
        dqs, dks, dvs = [], [], []
        dsink = jnp.zeros((8, 128), F32)
        qv = q_ref[...]
        dov = do_ref[...]
        kall = jnp.concatenate([kp_ref[...], kc_ref[...]], axis=0)
        vall = jnp.concatenate([vp_ref[...], vc_ref[...]], axis=0)
        for g in range(N_KV):
            kcat = kall[:, g * HEAD_DIM:(g + 1) * HEAD_DIM]
            vcat = vall[:, g * HEAD_DIM:(g + 1) * HEAD_DIM]
            dk = jnp.zeros((2 * QBLK, HEAD_DIM), F32)
            dv = jnp.zeros((2 * QBLK, HEAD_DIM), F32)
            for hh in range(grp):
                h = g * grp + hh
                hc = slice(h * HEAD_DIM, (h + 1) * HEAD_DIM)
                qh = qv[:, hc]
                doh = dov[:, hc]
                s = jnp.where(valid, _dot_nt(qh, kcat), -1e30)
                sink = sink_ref[h]
                m = jnp.maximum(jnp.max(s, axis=1, keepdims=True), sink)
                e = jnp.exp(s - m)
                es = jnp.exp(sink - m)
                inv = 1.0 / (jnp.sum(e, axis=1, keepdims=True) + es)
                p = e * inv
                dp = _dot_nt(doh, vcat)
                delta = jnp.sum(p * dp, axis=1, keepdims=True)
                dsc = (p * (dp - delta)).astype(BF16)
                dqs.append(_dot(dsc, kcat) * (HEAD_DIM ** -0.5))
                dk = dk + _dot_tn(dsc, qh)
                dv = dv + _dot_tn(p.astype(BF16), doh)
                tot = jnp.sum(-es * inv * delta, axis=0, keepdims=True)
                dsink = dsink + jnp.where(row8 == h, tot, 0.0)
            dks.append(dk)
            dvs.append(dv)
        dq_ref[...] = jnp.concatenate(dqs, axis=1).astype(BF16)
        dk = jnp.concatenate(dks, axis=1)
        dv = jnp.concatenate(dvs, axis=1)
        dkp_ref[...] = dk[:QBLK]
        dkc_ref[...] = dk[QBLK:]
        dvp_ref[...] = dv[:QBLK]
        dvc_ref[...] = dv[QBLK:]
        ds_ref[...] += dsink

    kvs = jax.ShapeDtypeStruct((S, D_KV), F32)
    return pl.pallas_call(
        body, name="attn_bwd", grid=(S // QBLK,),
        in_specs=[pl.BlockSpec(memory_space=pltpu.SMEM), _rows(QBLK, D_ATT)] + _attn_specs() + [_rows(QBLK, D_ATT)],
        out_specs=[_rows(QBLK, D_ATT), _rows(QBLK, D_KV), _rows(QBLK, D_KV), _rows(QBLK, D_KV), _rows(QBLK, D_KV),
                   _acc((8, 128))],
        out_shape=[jax.ShapeDtypeStruct((S, D_ATT), BF16), kvs, kvs, kvs, kvs, jax.ShapeDtypeStruct((8, 128), F32)],
        compiler_params=_params("arbitrary"),
    )(sinks, q, k, k, v, v, do)


def _rnn_bwd(xr, gr, h, drec, conv_w, conv_b, wa, wx, ba, bx, lam):
    S = xr.shape[0]
    tb = min(256, S)
    t8 = tb // 8
    nsteps = S // tb

    def body(xr_ref, xp_ref, gr_ref, h_ref, hp_ref, drec_ref, cw_ref, cb_ref, wa_ref, wx_ref, ba_ref, bx_ref, lam_ref,
             dxr_ref, dgr_ref, gwa_ref, gwx_ref, acc_ref, carry_s, dxc_halo_s, a_s, d_s):
        i = pl.program_id(0)
        blk = nsteps - 1 - i

        @pl.when(i == 0)
        def _():
            gwa_ref[...] = jnp.zeros_like(gwa_ref)
            gwx_ref[...] = jnp.zeros_like(gwx_ref)
            acc_ref[...] = jnp.zeros_like(acc_ref)
            carry_s[...] = jnp.zeros_like(carry_s)
            dxc_halo_s[...] = jnp.zeros_like(dxc_halo_s)

        x = xr_ref[...]
        xhalo = jnp.where(blk > 0, xp_ref[...], 0.0)
        cw = _w_rows(cw_ref)
        xs = [_shift_down(x, xhalo, 3), _shift_down(x, xhalo, 2), _shift_down(x, xhalo, 1), x]
        xc = cb_ref[...] + cw[0] * xs[0] + cw[1] * xs[1] + cw[2] * xs[2] + cw[3] * xs[3]
        sp = _softplus_neg(lam_ref[...])
        r, ig, a, f = _rnn_gates(xc, wa_ref[...], wx_ref[...], ba_ref[...], bx_ref[...], sp)
        hcur = h_ref[...]
        hprev = _shift_down(hcur, jnp.where(blk > 0, hp_ref[...], 0.0), 1)
        gl, dgl = _gelu_and_grad(gr_ref[...])
        drec = drec_ref[...]
        dgr_ref[...] = (drec * hcur * dgl).astype(BF16)
        a_s[...] = a
        d_s[...] = drec * gl
        row8 = lax.broadcasted_iota(jnp.int32, (8, D_RNN), 0)

        def tile(t, c):
            o = pl.multiple_of((t8 - 1 - t) * 8, 8)
            a8 = a_s[pl.ds(o, 8), :]
            dt = d_s[pl.ds(o, 8), :]
            at = jnp.where(row8 == 7, 1.0, pltpu.roll(a8, 7, 0))
            for s in (1, 2, 4):
                keep = row8 < 8 - s
                a_sh = jnp.where(keep, pltpu.roll(at, 8 - s, 0), 1.0)
                d_sh = jnp.where(keep, pltpu.roll(dt, 8 - s, 0), 0.0)
                dt = at * d_sh + dt
                at = at * a_sh
            lt = at * c + dt
            d_s[pl.ds(o, 8), :] = lt
            return _row_sum(jnp.where(row8 == 0, a8 * lt, 0.0))

        carry_s[0:1, :] = lax.fori_loop(0, t8, tile, carry_s[0:1, :])
        lmb = d_s[...]
        a2 = a * a
        dla = lmb * hprev * a - lmb * ig * xc * (a2 / f)
        di = lmb * f * xc
        dr = dla * (-LRU_C) * sp
        dpa = dr * r * (1.0 - r)
        dpx = di * ig * (1.0 - ig)
        dpab = dpa.astype(BF16)
        dpxb = dpx.astype(BF16)
        xcb = xc.astype(BF16)
        gwa_ref[...] += _dot_tn(xcb, dpab)
        gwx_ref[...] += _dot_tn(xcb, dpxb)
        dxc = lmb * f * ig + _dot_nt(dpab, wa_ref[...]) + _dot_nt(dpxb, wx_ref[...])
        nxt = dxc_halo_s[...]
        dxr = cw[3] * dxc
        for s in (1, 2, 3):
            dxr = dxr + cw[3 - s] * _shift_up(dxc, nxt, s)
        dxr_ref[...] = dxr.astype(BF16)
        dxc_halo_s[...] = dxc[:8]
        dlam = _row_sum(dla * (-LRU_C) * r) * (-1.0 / (1.0 + jnp.exp(lam_ref[...])))
        _put_rows(acc_ref, [_row_sum(dxc * xs[0]), _row_sum(dxc * xs[1]), _row_sum(dxc * xs[2]), _row_sum(dxc * xs[3]),
                            _row_sum(dxc), _row_sum(dpa), _row_sum(dpx), dlam])

    rev = lambda i: (nsteps - 1 - i, 0)
    prev8 = lambda i: (jnp.maximum((nsteps - 1 - i) * t8 - 1, 0), 0)
    blkspec = pl.BlockSpec((tb, D_RNN), rev)
    halo8 = pl.BlockSpec((8, D_RNN), prev8)
    vec = _resident((1, D_RNN))
    return pl.pallas_call(
        body, name="rnn_bwd", grid=(nsteps,),
        in_specs=[blkspec, halo8, blkspec, blkspec, halo8, blkspec, _resident((4, D_RNN)), vec,
                  _resident((D_RNN, D_RNN)), _resident((D_RNN, D_RNN)), vec, vec, vec],
        out_specs=[blkspec, blkspec, _acc((D_RNN, D_RNN)), _acc((D_RNN, D_RNN)), _acc((8, D_RNN))],
        out_shape=[jax.ShapeDtypeStruct((S, D_RNN), BF16), jax.ShapeDtypeStruct((S, D_RNN), BF16),
                   jax.ShapeDtypeStruct((D_RNN, D_RNN), F32), jax.ShapeDtypeStruct((D_RNN, D_RNN), F32),
                   jax.ShapeDtypeStruct((8, D_RNN), F32)],
        scratch_shapes=[pltpu.VMEM((8, D_RNN), F32), pltpu.VMEM((8, D_RNN), F32),
                        pltpu.VMEM((tb, D_RNN), F32), pltpu.VMEM((tb, D_RNN), F32)],
        compiler_params=_params("arbitrary"),
    )(xr, xr, gr, h, h, drec, conv_w, conv_b, wa, wx, ba, bx, lam)


def _in_bwd(dq, dkc, dkp, dvc, dvp, dxr, dgr, dz1, w_in):
    S = dz1.shape[0]
    tb = min(256, S)
    nsteps = S // tb
    nq = S // QBLK
    r = tb // QBLK

    def body(dq_ref, dkc_ref, dkp_ref, dkn_ref, dvc_ref, dvp_ref, dvn_ref, dxr_ref, dgr_ref, dz1_ref, w_ref,
             du_ref, dx_ref):
        i = pl.program_id(0)
        last = i == nsteps - 1

        def shifted(prev_ref, next_ref):
            nxt = jnp.where(last, 0.0, next_ref[...])
            return jnp.concatenate([prev_ref[QBLK:], nxt], axis=0) if r > 1 else nxt

        dk = (dkc_ref[...] + shifted(dkp_ref, dkn_ref)).astype(BF16)
        dv = (dvc_ref[...] + shifted(dvp_ref, dvn_ref)).astype(BF16)
        du = jnp.concatenate([dq_ref[...], dk, dv, dxr_ref[...], dgr_ref[...]], axis=1)
        du_ref[...] = du
        dx_ref[...] = ALPHA * dz1_ref[...] + _dot_nt(du, w_ref[...])

    nextq = pl.BlockSpec((QBLK, D_KV), lambda i: (jnp.minimum((i + 1) * r, nq - 1), 0))
    return pl.pallas_call(
        body, name="in_bwd", grid=(nsteps,),
        in_specs=[_rows(tb, D_ATT), _rows(tb, D_KV), _rows(tb, D_KV), nextq, _rows(tb, D_KV), _rows(tb, D_KV), nextq,
                  _rows(tb, D_RNN), _rows(tb, D_RNN), _rows(tb, D_MODEL), _resident((D_MODEL, D_IN))],
        out_specs=[_rows(tb, D_IN), _rows(tb, D_MODEL)],
        out_shape=[jax.ShapeDtypeStruct((S, D_IN), BF16), jax.ShapeDtypeStruct((S, D_MODEL), F32)],
        compiler_params=_params("parallel"),
    )(dq, dkc, dkp, dkp, dvc, dvp, dvp, dxr, dgr, dz1, w_in)


def _block_diag(w):
    eye = jnp.eye(RNN_BLOCKS, dtype=w.dtype)
    return (w[:, :, None, :] * eye[:, None, :, None]).reshape(D_RNN, D_RNN).astype(BF16)


def _diag_blocks(g):
    g4 = g.reshape(RNN_BLOCKS, HEAD_DIM, RNN_BLOCKS, HEAD_DIM)
    return jnp.stack([g4[b, :, b, :] for b in range(RNN_BLOCKS)])


def _local_step(x, p, tgt, w_in, w_out, w_up_blocks, w_down, w_pg, w_pp, sinks, rcw, rcb, gaw, gab, gxw, gxb, lam,
                ln1_g, ln1_b, fcw, fcb, b_pg, ln2_g, ln2_b):
    wa, wx = _block_diag(gaw), _block_diag(gxw)
    q, k, v, xr, gr = _in_proj(x, w_in)
    att = _attn_fwd(q, k, v, sinks)
    rec, h = _rnn_fwd(xr, gr, rcw, rcb, wa, wx, gab, gxb, lam)
    cat = jnp.concatenate([att, rec], axis=1)
    z1, h1b, gate, val = _mix_ln1_up(x, cat, w_out, ln1_g, ln1_b, w_up_blocks)
    act, dz2b, dpreb, dppb, dgc, dval, dh1p, acc_t = _tail(gate, val, z1, p, tgt, fcw, fcb, w_down, w_pg, b_pg, w_pp,
                                                           ln1_g, ln1_b, ln2_g, ln2_b)
    g_down_t = _tn_matmul(dz2b, act, "down_grad")
    g_pg = _tn_matmul(h1b, dpreb, "pg_grad")
    g_pp = _tn_matmul(p, dppb, "pp_grad", tm=PLE_DIM)
    dgate, dz1, dz1b, datt, drec, acc_f, acc_d = _up_bwd(dgc, gate, dval, dh1p, z1, w_up_blocks, fcw, w_out, ln1_g)
    g_up = _up_grad(h1b, dgate, dval, w_up_blocks.shape[0], w_up_blocks.shape[2])
    g_out = _tn_matmul(cat, dz1b, "out_grad")
    dq, dkc, dkp, dvc, dvp, acc_s = _attn_bwd(q, k, v, datt, sinks)
    dxr, dgr, g_wa, g_wx, acc_r = _rnn_bwd(xr, gr, h, drec, rcw, rcb, wa, wx, gab, gxb, lam)
    du, dx = _in_bwd(dq, dkc, dkp, dvc, dvp, dxr, dgr, dz1, w_in)
    g_in = _tn_matmul(x, du, "in_grad")
    small = {
        "loss": acc_t[3, :1], "ln2_g": acc_t[0], "ln2_b": acc_t[1], "ple_gate_b": acc_t[2],
        "ffn_conv_w": acc_f[0:3].reshape(-1), "ffn_conv_b": acc_f[3], "ln1_g": acc_d[0], "ln1_b": acc_d[1],
        "attn_sinks": acc_s[:, 0], "rnn_conv_w": acc_r[0:4].reshape(-1), "rnn_conv_b": acc_r[4],
        "gate_a_b": acc_r[5], "gate_x_b": acc_r[6], "lru_lambda": acc_r[7],
        "gate_a_w": _diag_blocks(g_wa).reshape(-1), "gate_x_w": _diag_blocks(g_wx).reshape(-1),
    }
    big = {"w_in": g_in, "w_out": g_out, "w_ffn_up": g_up, "w_ffn_down_t": g_down_t, "ple_gate_w": g_pg, "ple_proj": g_pp}
    return dx, big, small


def _place():
    x, y, c = lax.axis_index("x"), lax.axis_index("y"), lax.axis_index("c")
    return x, y, c


def _dev_index(px, py, pc):
    return 4 * px + 2 * py + pc


def _all_gather(shards):
    n = len(shards)

    def body(*refs):
        ins, outs = refs[:n], refs[n:2 * n]
        send_sems, recv_sems, local_sems = refs[2 * n:]
        x, y, c = _place()
        me, sibling = (x, y, c), (x, y, 1 - c)
        chips = [(1 - x, y), (x, 1 - y), (1 - x, 1 - y)]

        def copy(a, k, block, to, src=None):
            rows = outs[a].at[_dev_index(*block)]
            return pltpu.make_async_remote_copy(
                src_ref=rows if src is None else src, dst_ref=rows, send_sem=send_sems.at[a, k],
                recv_sem=recv_sems.at[a, k], device_id=to, device_id_type=MESH)

        mine = [pltpu.make_async_copy(ins[a], outs[a].at[_dev_index(*me)], local_sems.at[a]) for a in range(n)]
        for cp in mine:
            cp.start()
        first = []
        for a in range(n):
            first.append(copy(a, 0, me, sibling, src=ins[a]))
            first += [copy(a, 1 + j, me, (*chip, c), src=ins[a]) for j, chip in enumerate(chips)]
        for cp in first:
            cp.start()
        passed = []
        for j, chip in enumerate(chips):
            for a in range(n):
                copy(a, 1 + j, (*chip, c), me).wait_recv()
                fwd = copy(a, 4 + j, (*chip, c), sibling)
                fwd.start()
                passed.append(fwd)
        for a in range(n):
            copy(a, 0, sibling, me).wait_recv()
            for j, chip in enumerate(chips):
                copy(a, 4 + j, (*chip, 1 - c), me).wait_recv()
        for cp in first + passed:
            cp.wait_send()
        for cp in mine:
            cp.wait()

    any_spec = pl.BlockSpec(memory_space=pl.ANY)
    return pl.pallas_call(
        body, name="weight_gather",
        in_specs=[any_spec] * n, out_specs=[any_spec] * n,
        out_shape=[jax.ShapeDtypeStruct((N_DEV,) + s.shape, s.dtype) for s in shards],
        scratch_shapes=[pltpu.SemaphoreType.DMA((n, 7)), pltpu.SemaphoreType.DMA((n, 7)), pltpu.SemaphoreType.DMA((n,))],
    )(*shards)


def _exchange(blocked):
    n = len(blocked)

    def body(*refs):
        ins, outs = refs[:n], refs[n:2 * n]
        send_sems, recv_sems, local_sems = refs[2 * n:]
        x, y, c = _place()
        me = _dev_index(x, y, c)

        def peer(k):
            return (x ^ (k >> 2), y ^ ((k >> 1) & 1), c ^ (k & 1))

        def copy(a, k):
            to = peer(k)
            return pltpu.make_async_remote_copy(
                src_ref=ins[a].at[_dev_index(*to)], dst_ref=outs[a].at[me], send_sem=send_sems.at[a, k - 1],
                recv_sem=recv_sems.at[a, k - 1], device_id=to, device_id_type=MESH)

        def arrival(a, k):
            return pltpu.make_async_remote_copy(
                src_ref=ins[a].at[me], dst_ref=outs[a].at[_dev_index(*peer(k))], send_sem=send_sems.at[a, k - 1],
                recv_sem=recv_sems.at[a, k - 1], device_id=peer(k), device_id_type=MESH)

        mine = [pltpu.make_async_copy(ins[a].at[me], outs[a].at[me], local_sems.at[a]) for a in range(n)]
        for cp in mine:
            cp.start()
        sent = [copy(a, k) for k in range(1, N_DEV) for a in range(n)]
        for cp in sent:
            cp.start()
        for k in range(1, N_DEV):
            for a in range(n):
                arrival(a, k).wait_recv()
        for cp in sent:
            cp.wait_send()
        for cp in mine:
            cp.wait()

    any_spec = pl.BlockSpec(memory_space=pl.ANY)
    return pl.pallas_call(
        body, name="grad_exchange",
        in_specs=[any_spec] * n, out_specs=[any_spec] * n,
        out_shape=[jax.ShapeDtypeStruct(b.shape, b.dtype) for b in blocked],
        scratch_shapes=[pltpu.SemaphoreType.DMA((n, 7)), pltpu.SemaphoreType.DMA((n, 7)), pltpu.SemaphoreType.DMA((n,))],
    )(*blocked)


def _adamw(w, g, m, v):
    m = ADAM_B1 * m + (1.0 - ADAM_B1) * g
    v = ADAM_B2 * v + (1.0 - ADAM_B2) * (g * g)
    m_hat = m / (1.0 - ADAM_B1 ** ADAM_STEP)
    v_hat = v / (1.0 - ADAM_B2 ** ADAM_STEP)
    delta = -ADAM_LR * (m_hat / (jnp.sqrt(v_hat) + ADAM_EPS) + ADAM_WD * w)
    return delta, m, v


def _sum_adamw(parts, w, m, v, name):
    R, C = w.shape
    rb = 128
    assert R % rb == 0

    def body(p_ref, w_ref, m_ref, v_ref, g_out, d_out, m_out, v_out):
        g = p_ref[0].astype(F32)
        for d in range(1, N_DEV):
            g = g + p_ref[d].astype(F32)
        delta, mn, vn = _adamw(w_ref[...], g, m_ref[...], v_ref[...])
        g_out[...] = g
        d_out[...] = delta
        m_out[...] = mn
        v_out[...] = vn

    blk = _rows(rb, C)
    out = jax.ShapeDtypeStruct((R, C), F32)
    return pl.pallas_call(
        body, name=name, grid=(R // rb,),
        in_specs=[pl.BlockSpec((N_DEV, rb, C), lambda i: (0, i, 0)), blk, blk, blk],
        out_specs=[blk, blk, blk, blk], out_shape=[out, out, out, out],
        compiler_params=_params("parallel"),
    )(parts, w, m, v)


def _sum_parts(parts):
    _, R, C = parts.shape

    def body(p_ref, o_ref):
        g = p_ref[0]
        for d in range(1, N_DEV):
            g = g + p_ref[d]
        o_ref[...] = g

    return pl.pallas_call(body, name="small_sum", out_shape=jax.ShapeDtypeStruct((R, C), F32))(parts)


def _adamw_packed(g, w, m, v):
    def body(g_ref, w_ref, m_ref, v_ref, d_out, m_out, v_out):
        delta, mn, vn = _adamw(w_ref[...], g_ref[...], m_ref[...], v_ref[...])
        d_out[...] = delta
        m_out[...] = mn
        v_out[...] = vn

    out = jax.ShapeDtypeStruct(g.shape, F32)
    return pl.pallas_call(body, name="small_adamw", out_shape=[out, out, out])(g, w, m, v)


def _pack(vectors, rows=None):
    parts, offs, o = [], [], 0
    for vct in vectors:
        n = vct.shape[0]
        pad = -n % 128
        parts.append(jnp.pad(vct, (0, pad)) if pad else vct)
        offs.append(o)
        o += n + pad
    r = o // 128
    rows = rows or -(-r // 8) * 8
    flat = jnp.concatenate(parts + ([jnp.zeros(((rows - r) * 128,), F32)] if rows > r else []))
    return flat.reshape(rows, 128), offs


_SMALL = [("loss", 1), ("attn_sinks", 8), ("rnn_conv_w", 4 * D_RNN), ("rnn_conv_b", D_RNN),
          ("gate_a_w", RNN_BLOCKS * HEAD_DIM * HEAD_DIM), ("gate_a_b", D_RNN),
          ("gate_x_w", RNN_BLOCKS * HEAD_DIM * HEAD_DIM), ("gate_x_b", D_RNN), ("lru_lambda", D_RNN),
          ("ln1_g", D_MODEL), ("ln1_b", D_MODEL), ("ffn_conv_w", 3 * D_FF), ("ffn_conv_b", D_FF),
          ("ple_gate_b", D_MODEL), ("ln2_g", D_MODEL), ("ln2_b", D_MODEL)]


def kernel(x, p, w_in, attn_sinks, rnn_conv_w, rnn_conv_b, gate_a_w, gate_a_b, gate_x_w, gate_x_b, lru_lambda, w_out, ln1_g, ln1_b, w_ffn_up, ffn_conv_w, ffn_conv_b, w_ffn_down, ple_gate_w, ple_gate_b, ple_proj, ln2_g, ln2_b, loss_target, m_w_in, m_attn_sinks, m_rnn_conv_w, m_rnn_conv_b, m_gate_a_w, m_gate_a_b, m_gate_x_w, m_gate_x_b, m_lru_lambda, m_w_out, m_ln1_g, m_ln1_b, m_w_ffn_up, m_ffn_conv_w, m_ffn_conv_b, m_w_ffn_down, m_ple_gate_w, m_ple_gate_b, m_ple_proj, m_ln2_g, m_ln2_b, v_w_in, v_attn_sinks, v_rnn_conv_w, v_rnn_conv_b, v_gate_a_w, v_gate_a_b, v_gate_x_w, v_gate_x_b, v_lru_lambda, v_w_out, v_ln1_g, v_ln1_b, v_w_ffn_up, v_ffn_conv_w, v_ffn_conv_b, v_w_ffn_down, v_ple_gate_w, v_ple_gate_b, v_ple_proj, v_ln2_g, v_ln2_b):
    me = _dev_index(*_place())
    col_blocks = lambda g: g.reshape(g.shape[0], N_DEV, g.shape[1] // N_DEV).transpose(1, 0, 2)
    from_col_blocks = lambda g: g.transpose(1, 0, 2).reshape(g.shape[1], N_DEV * g.shape[2])

    conv_cols = jnp.concatenate([rnn_conv_w[0].reshape(1, -1), ffn_conv_w[0].reshape(1, -1)], axis=1)
    n_rc, n_fc = 4 * D_RNN // N_DEV, 3 * D_FF // N_DEV
    g_in, g_out, g_up, g_down, g_pg, g_pp, g_conv = _all_gather([
        w_in[0].astype(BF16), w_out[0].astype(BF16), w_ffn_up[0].astype(BF16), w_ffn_down[0].astype(BF16),
        ple_gate_w[0].astype(BF16), ple_proj[0].astype(BF16), jnp.broadcast_to(conv_cols, (8, n_rc + n_fc))])
    rcw = from_col_blocks(g_conv[:, 0, :n_rc].reshape(N_DEV, 4, D_RNN // N_DEV))
    fcw = from_col_blocks(g_conv[:, 0, n_rc:].reshape(N_DEV, 3, D_FF // N_DEV))

    dx, big, small = _local_step(
        x[0], p[0, 0], loss_target[0], from_col_blocks(g_in), g_out.reshape(D_MODEL, D_MODEL), g_up,
        g_down.reshape(D_FF, D_MODEL), g_pg.reshape(D_MODEL, D_MODEL), from_col_blocks(g_pp), attn_sinks[0], rcw,
        rnn_conv_b, gate_a_w[0], gate_a_b, gate_x_w[0], gate_x_b, lru_lambda, ln1_g, ln1_b, fcw, ffn_conv_b,
        ple_gate_b, ln2_g, ln2_b)

    packed, offs = _pack([small[n].astype(F32).reshape(-1) for n, _ in _SMALL])
    send = [col_blocks(big["w_in"]), big["w_out"].reshape(N_DEV, D_MODEL // N_DEV, D_MODEL), big["w_ffn_up"],
            big["w_ffn_down_t"].T.reshape(N_DEV, D_FF // N_DEV, D_MODEL),
            big["ple_gate_w"].reshape(N_DEV, D_MODEL // N_DEV, D_MODEL), col_blocks(big["ple_proj"]),
            jnp.broadcast_to(packed, (N_DEV,) + packed.shape)]
    r_in, r_out, r_up, r_down, r_pg, r_pp, r_small = _exchange(send)

    outs = {}
    for name, parts, w, m, v in [("w_in", r_in, w_in, m_w_in, v_w_in), ("w_out", r_out, w_out, m_w_out, v_w_out),
                                 ("w_ffn_up", r_up, w_ffn_up, m_w_ffn_up, v_w_ffn_up),
                                 ("w_ffn_down", r_down, w_ffn_down, m_w_ffn_down, v_w_ffn_down),
                                 ("ple_gate_w", r_pg, ple_gate_w, m_ple_gate_w, v_ple_gate_w),
                                 ("ple_proj", r_pp, ple_proj, m_ple_proj, v_ple_proj)]:
        res = _sum_adamw(parts, w[0], m[0], v[0], "adamw_" + name)
        outs[name] = [r[None] for r in res]

    total = _sum_parts(r_small).reshape(-1)
    gsmall = {n: total[o:o + size] for (n, size), o in zip(_SMALL, offs)}
    loss = gsmall["loss"][0]
    shard = lambda g, rows: lax.dynamic_slice_in_dim(g.reshape(rows, -1), me * (g.shape[0] // rows // N_DEV),
                                                     g.shape[0] // rows // N_DEV, axis=1).reshape(-1)
    gsmall["rnn_conv_w"] = shard(gsmall["rnn_conv_w"], 4)
    gsmall["ffn_conv_w"] = shard(gsmall["ffn_conv_w"], 3)
    given = dict(attn_sinks=(attn_sinks, m_attn_sinks, v_attn_sinks), rnn_conv_w=(rnn_conv_w, m_rnn_conv_w, v_rnn_conv_w),
                 rnn_conv_b=(rnn_conv_b, m_rnn_conv_b, v_rnn_conv_b), gate_a_w=(gate_a_w, m_gate_a_w, v_gate_a_w),
                 gate_a_b=(gate_a_b, m_gate_a_b, v_gate_a_b), gate_x_w=(gate_x_w, m_gate_x_w, v_gate_x_w),
                 gate_x_b=(gate_x_b, m_gate_x_b, v_gate_x_b), lru_lambda=(lru_lambda, m_lru_lambda, v_lru_lambda),
                 ln1_g=(ln1_g, m_ln1_g, v_ln1_g), ln1_b=(ln1_b, m_ln1_b, v_ln1_b),
                 ffn_conv_w=(ffn_conv_w, m_ffn_conv_w, v_ffn_conv_w), ffn_conv_b=(ffn_conv_b, m_ffn_conv_b, v_ffn_conv_b),
                 ple_gate_b=(ple_gate_b, m_ple_gate_b, v_ple_gate_b), ln2_g=(ln2_g, m_ln2_g, v_ln2_g),
                 ln2_b=(ln2_b, m_ln2_b, v_ln2_b))
    names = [n for n, _ in _SMALL if n != "loss"]
    pg, poffs = _pack([gsmall[n] for n in names])
    pw, _ = _pack([given[n][0].reshape(-1) for n in names])
    pm, _ = _pack([given[n][1].reshape(-1) for n in names])
    pv, _ = _pack([given[n][2].reshape(-1) for n in names])
    res = _adamw_packed(pg, pw, pm, pv)
    for n, o in zip(names, poffs):
        shape = given[n][0].shape
        size = given[n][0].size
        outs[n] = [gsmall[n].reshape(shape)] + [r.reshape(-1)[o:o + size].reshape(shape) for r in res]

    order = ["w_in", "attn_sinks", "rnn_conv_w", "rnn_conv_b", "gate_a_w", "gate_a_b", "gate_x_w", "gate_x_b",
             "lru_lambda", "w_out", "ln1_g", "ln1_b", "w_ffn_up", "ffn_conv_w", "ffn_conv_b", "w_ffn_down",
             "ple_gate_w", "ple_gate_b", "ple_proj", "ln2_g", "ln2_b"]
    return (loss, dx[None], *[outs[n][0] for n in order], *[outs[n][1] for n in order],
            *[outs[n][2] for n in order], *[outs[n][3] for n in order])
```

```python
import functools

import jax
import jax.numpy as jnp
from jax import lax
from jax.experimental import pallas as pl
from jax.experimental.pallas import tpu as pltpu

F32 = jnp.float32
BF16 = jnp.bfloat16

D_MODEL = 1024
D_ATT = 512
D_KV = 128
HEAD_DIM = 64
N_HEADS = 8
N_KV = 2
D_RNN = 512
RNN_BLOCKS = 8
D_IN = 1792
D_FF = 3072
PLE_DIM = 256
QBLK = 128
N_DEV = 8
ALPHA = float(2 ** 0.25)
LN_EPS = 1e-5
LRU_C = 8.0
ADAM_LR, ADAM_B1, ADAM_B2, ADAM_EPS, ADAM_WD, ADAM_STEP = 0.001, 0.9, 0.999, 1e-08, 0.01, 10

V7X_VMEM_LIMIT = 56 * 1024 * 1024
MESH = pl.DeviceIdType.MESH


def _params(*sem, vmem=V7X_VMEM_LIMIT):
    return pltpu.CompilerParams(dimension_semantics=sem or None, vmem_limit_bytes=vmem)


def _resident(shape):
    return pl.BlockSpec(shape, lambda *_: (0,) * len(shape), pipeline_mode=pl.Buffered(1))


def _rows(tb, cols):
    return pl.BlockSpec((tb, cols), lambda i: (i, 0))


def _acc(shape):
    return pl.BlockSpec(shape, lambda *_: (0,) * len(shape))


def _dot(a, b):
    return jnp.dot(a, b, preferred_element_type=F32)


def _dot_nt(a, b):
    return lax.dot_general(a, b, (((1,), (1,)), ((), ())), preferred_element_type=F32)


def _dot_tn(a, b):
    return lax.dot_general(a, b, (((0,), (0,)), ((), ())), preferred_element_type=F32)


def _sigmoid(x):
    return 1.0 / (1.0 + jnp.exp(-x))


_GELU_C = 0.7978845608028654
_GELU_K = 0.044715


def _gelu_and_grad(x):
    t = jnp.tanh(_GELU_C * (x + _GELU_K * x * x * x))
    g = 0.5 * x * (1.0 + t)
    dg = 0.5 * (1.0 + t) + 0.5 * x * (1.0 - t * t) * _GELU_C * (1.0 + 3.0 * _GELU_K * x * x)
    return g, dg


def _gelu(x):
    return 0.5 * x * (1.0 + jnp.tanh(_GELU_C * (x + _GELU_K * x * x * x)))


def _ln_stats(z):
    mu = jnp.mean(z, axis=-1, keepdims=True)
    zc = z - mu
    var = jnp.mean(zc * zc, axis=-1, keepdims=True)
    rstd = lax.rsqrt(var + LN_EPS)
    return zc * rstd, rstd


def _ln_bwd(dy, xhat, rstd, g):
    dxh = dy * g
    m1 = jnp.mean(dxh, axis=-1, keepdims=True)
    m2 = jnp.mean(dxh * xhat, axis=-1, keepdims=True)
    return rstd * (dxh - m1 - xhat * m2)


def _softplus_neg(lam):
    u = jnp.exp(-jnp.abs(lam))
    w = 1.0 + u
    d = w - 1.0
    log1p_u = jnp.where(d == 0.0, u, jnp.log(w) * (u / jnp.where(d == 0.0, 1.0, d)))
    return jnp.maximum(-lam, 0.0) + log1p_u


def _shift_down(x, halo, s):
    xs = pltpu.roll(x, s, 0)
    hs = pltpu.roll(halo, s, 0)
    row8 = lax.broadcasted_iota(jnp.int32, hs.shape, 0)
    first = jnp.where(row8 < s, hs, xs[:8])
    return jnp.concatenate([first, xs[8:]], axis=0)


def _shift_up(x, halo, s):
    n = x.shape[0]
    xs = pltpu.roll(x, n - s, 0)
    hs = pltpu.roll(halo, 8 - s, 0)
    row8 = lax.broadcasted_iota(jnp.int32, hs.shape, 0)
    last = jnp.where(row8 >= 8 - s, hs, xs[n - 8:])
    return jnp.concatenate([xs[:n - 8], last], axis=0)


def _row_sum(x):
    return jnp.sum(x, axis=0, keepdims=True)


def _put_rows(acc_ref, rows):
    row8 = lax.broadcasted_iota(jnp.int32, acc_ref.shape, 0)
    upd = jnp.zeros(acc_ref.shape, F32)
    for r, vec in enumerate(rows):
        upd = jnp.where(row8 == r, vec, upd)
    acc_ref[...] += upd


def _place():
    return lax.axis_index("x"), lax.axis_index("y"), lax.axis_index("c")


def _dev_index(px, py, pc):
    return 4 * px + 2 * py + pc


_ANY = pl.BlockSpec(memory_space=pl.ANY)


class _Gather:
    def __init__(self, arrays):
        self.arrays = list(arrays)
        self.n = len(self.arrays)

    def out_shape(self):
        return [jax.ShapeDtypeStruct((N_DEV,) + s.shape, s.dtype) for s in self.arrays]

    def scratch(self):
        return [pltpu.SemaphoreType.DMA((self.n, 7)), pltpu.SemaphoreType.DMA((self.n, 7)),
                pltpu.SemaphoreType.DMA((self.n,))]

    def _parts(self, ins, outs, sems):
        send_sems, recv_sems, local_sems = sems
        x, y, c = _place()
        me, sibling = (x, y, c), (x, y, 1 - c)
        chips = [(1 - x, y), (x, 1 - y), (1 - x, 1 - y)]

        def copy(a, k, block, to, src=None):
            rows = outs[a].at[_dev_index(*block)]
            return pltpu.make_async_remote_copy(
                src_ref=rows if src is None else src, dst_ref=rows, send_sem=send_sems.at[a, k],
                recv_sem=recv_sems.at[a, k], device_id=to, device_id_type=MESH)

        rng = range(self.n)
        mine = [pltpu.make_async_copy(ins[a], outs[a].at[_dev_index(*me)], local_sems.at[a]) for a in rng]
        first = [copy(a, 0, me, sibling, src=ins[a]) for a in rng]
        first += [copy(a, 1 + j, me, (*chip, c), src=ins[a]) for j, chip in enumerate(chips) for a in rng]
        landed = [copy(a, 1 + j, (*chip, c), me) for j, chip in enumerate(chips) for a in rng]
        passed = [copy(a, 4 + j, (*chip, c), sibling) for j, chip in enumerate(chips) for a in rng]
        from_sibling = [copy(a, 0, sibling, me) for a in rng]
        from_sibling += [copy(a, 4 + j, (*chip, 1 - c), me) for j, chip in enumerate(chips) for a in rng]
        return mine, first, landed, passed, from_sibling

    def start(self, ins, outs, sems):
        mine, first, _, _, _ = self._parts(ins, outs, sems)
        for cp in mine + first:
            cp.start()

    def forward(self, ins, outs, sems):
        _, _, landed, passed, _ = self._parts(ins, outs, sems)
        for got, fwd in zip(landed, passed):
            got.wait_recv()
            fwd.start()

    def finish(self, ins, outs, sems):
        mine, first, _, passed, from_sibling = self._parts(ins, outs, sems)
        for cp in from_sibling:
            cp.wait_recv()
        for cp in first + passed:
            cp.wait_send()
        for cp in mine:
            cp.wait()

    def before(self, ins, outs, sems, step, nsteps):
        pl.when(step == 0)(lambda: self.start(ins, outs, sems))
        pl.when(step == (3 * nsteps) // 5)(lambda: self.forward(ins, outs, sems))

    def after(self, ins, outs, sems, step, nsteps):
        pl.when(step == nsteps - 1)(lambda: self.finish(ins, outs, sems))


class _Exchange:
    def __init__(self, arrays):
        self.arrays = list(arrays)
        self.n = len(self.arrays)

    def out_shape(self):
        return [jax.ShapeDtypeStruct(b.shape, b.dtype) for b in self.arrays]

    def scratch(self):
        return [pltpu.SemaphoreType.DMA((self.n, 7)), pltpu.SemaphoreType.DMA((self.n, 7)),
                pltpu.SemaphoreType.DMA((self.n,))]

    def _parts(self, ins, outs, sems):
        send_sems, recv_sems, local_sems = sems
        x, y, c = _place()
        me = _dev_index(x, y, c)
        peers = [(x ^ (k >> 2), y ^ ((k >> 1) & 1), c ^ (k & 1)) for k in range(1, N_DEV)]
        rng = range(self.n)
        mine = [pltpu.make_async_copy(ins[a].at[me], outs[a].at[me], local_sems.at[a]) for a in rng]
        sent = [pltpu.make_async_remote_copy(
            src_ref=ins[a].at[_dev_index(*to)], dst_ref=outs[a].at[me], send_sem=send_sems.at[a, k],
            recv_sem=recv_sems.at[a, k], device_id=to, device_id_type=MESH) for k, to in enumerate(peers) for a in rng]
        arrivals = [pltpu.make_async_remote_copy(
            src_ref=ins[a].at[me], dst_ref=outs[a].at[_dev_index(*frm)], send_sem=send_sems.at[a, k],
            recv_sem=recv_sems.at[a, k], device_id=frm, device_id_type=MESH) for k, frm in enumerate(peers) for a in rng]
        return mine, sent, arrivals

    def start(self, ins, outs, sems):
        mine, sent, _ = self._parts(ins, outs, sems)
        for cp in mine + sent:
            cp.start()

    def finish(self, ins, outs, sems):
        mine, sent, arrivals = self._parts(ins, outs, sems)
        for cp in arrivals:
            cp.wait_recv()
        for cp in sent:
            cp.wait_send()
        for cp in mine:
            cp.wait()

    def before(self, ins, outs, sems, step, nsteps):
        pl.when(step == 0)(lambda: self.start(ins, outs, sems))

    def after(self, ins, outs, sems, step, nsteps):
        pl.when(step == nsteps - 1)(lambda: self.finish(ins, outs, sems))


def _comm_call(comm, name):
    n = comm.n

    def body(*refs):
        ins, outs, sems = refs[:n], refs[n:2 * n], refs[2 * n:]
        comm.start(ins, outs, sems)
        if isinstance(comm, _Gather):
            comm.forward(ins, outs, sems)
        comm.finish(ins, outs, sems)

    return pl.pallas_call(body, name=name, in_specs=[_ANY] * n, out_specs=[_ANY] * n, out_shape=comm.out_shape(),
                          scratch_shapes=comm.scratch())(*comm.arrays)


def _pcall(body, args, *, name, grid, in_specs, out_specs, out_shape, scratch_shapes=(), sem="parallel", comm=None):
    if comm is None:
        res = pl.pallas_call(body, name=name, grid=grid, in_specs=in_specs, out_specs=out_specs, out_shape=out_shape,
                             scratch_shapes=list(scratch_shapes), compiler_params=_params(sem))(*args)
        return res, []
    n_in, n_out, n_scr, n = len(in_specs), len(out_specs), len(scratch_shapes), comm.n
    nsteps = grid[0]

    def hosted(*refs):
        ins, cin = refs[:n_in], refs[n_in:n_in + n]
        o0 = n_in + n
        outs, cout = refs[o0:o0 + n_out], refs[o0 + n_out:o0 + n_out + n]
        s0 = o0 + n_out + n
        scr, sems = refs[s0:s0 + n_scr], refs[s0 + n_scr:]
        step = pl.program_id(0)
        comm.before(cin, cout, sems, step, nsteps)
        body(*ins, *outs, *scr)
        comm.after(cin, cout, sems, step, nsteps)

    res = pl.pallas_call(
        hosted, name=name, grid=grid, in_specs=list(in_specs) + [_ANY] * n, out_specs=list(out_specs) + [_ANY] * n,
        out_shape=list(out_shape) + comm.out_shape(), scratch_shapes=list(scratch_shapes) + comm.scratch(),
        compiler_params=_params("arbitrary"))(*args, *comm.arrays)
    return res[:n_out], res[n_out:]


def _in_proj(x, w_in):
    S = x.shape[0]
    tb = min(512, S)

    def body(x_ref, w_ref, q_ref, k_ref, v_ref, xr_ref, gr_ref):
        u = _dot(x_ref[...].astype(BF16), w_ref[...])
        q_ref[...] = (u[:, :D_ATT] * (HEAD_DIM ** -0.5)).astype(BF16)
        k_ref[...] = u[:, D_ATT:D_ATT + D_KV].astype(BF16)
        v_ref[...] = u[:, D_ATT + D_KV:D_ATT + 2 * D_KV].astype(BF16)
        xr_ref[...] = u[:, D_ATT + 2 * D_KV:D_ATT + 2 * D_KV + D_RNN]
        gr_ref[...] = u[:, D_ATT + 2 * D_KV + D_RNN:]

    return pl.pallas_call(
        body, name="in_proj", grid=(S // tb,),
        in_specs=[_rows(tb, D_MODEL), _resident((D_MODEL, D_IN))],
        out_specs=[_rows(tb, D_ATT), _rows(tb, D_KV), _rows(tb, D_KV), _rows(tb, D_RNN), _rows(tb, D_RNN)],
        out_shape=[jax.ShapeDtypeStruct((S, D_ATT), BF16), jax.ShapeDtypeStruct((S, D_KV), BF16),
                   jax.ShapeDtypeStruct((S, D_KV), BF16), jax.ShapeDtypeStruct((S, D_RNN), F32),
                   jax.ShapeDtypeStruct((S, D_RNN), F32)],
        compiler_params=_params("parallel"),
    )(x, w_in)


def _band_mask(i):
    qi = lax.broadcasted_iota(jnp.int32, (QBLK, 2 * QBLK), 0)
    sj = lax.broadcasted_iota(jnp.int32, (QBLK, 2 * QBLK), 1)
    return (sj > qi) & (sj <= qi + QBLK) & ((sj >= QBLK) | (i > 0))


def _attn_specs():
    cur = lambda i: (i, 0)
    prev = lambda i: (jnp.maximum(i - 1, 0), 0)
    return [pl.BlockSpec((QBLK, D_KV), cur), pl.BlockSpec((QBLK, D_KV), prev),
            pl.BlockSpec((QBLK, D_KV), cur), pl.BlockSpec((QBLK, D_KV), prev)]


def _attn_fwd(q, k, v, sinks, comm=None):
    S = q.shape[0]

    def body(sink_ref, q_ref, kc_ref, kp_ref, vc_ref, vp_ref, o_ref):
        valid = _band_mask(pl.program_id(0))
        outs = []
        qv = q_ref[...]
        kall = jnp.concatenate([kp_ref[...], kc_ref[...]], axis=0)
        vall = jnp.concatenate([vp_ref[...], vc_ref[...]], axis=0)
        for g in range(N_KV):
            kcat = kall[:, g * HEAD_DIM:(g + 1) * HEAD_DIM]
            vcat = vall[:, g * HEAD_DIM:(g + 1) * HEAD_DIM]
            for hh in range(N_HEADS // N_KV):
                h = g * (N_HEADS // N_KV) + hh
                s = _dot_nt(qv[:, h * HEAD_DIM:(h + 1) * HEAD_DIM], kcat)
                s = jnp.where(valid, s, -1e30)
                sink = sink_ref[h]
                m = jnp.maximum(jnp.max(s, axis=1, keepdims=True), sink)
                p = jnp.exp(s - m)
                l = jnp.sum(p, axis=1, keepdims=True) + jnp.exp(sink - m)
                outs.append(_dot(p.astype(BF16), vcat) / l)
        o_ref[...] = jnp.concatenate(outs, axis=1).astype(BF16)

    return _pcall(
        body, (sinks, q, k, k, v, v), name="attn_fwd", grid=(S // QBLK,), comm=comm,
        in_specs=[pl.BlockSpec(memory_space=pltpu.SMEM), _rows(QBLK, D_ATT)] + _attn_specs(),
        out_specs=[_rows(QBLK, D_ATT)], out_shape=[jax.ShapeDtypeStruct((S, D_ATT), BF16)])


def _w_rows(w_ref):
    return [w_ref[k:k + 1, :] for k in range(w_ref.shape[0])]


def _conv4(x, halo, w, b):
    y = b + w[3] * x
    for s in (1, 2, 3):
        y = y + w[3 - s] * _shift_down(x, halo, s)
    return y


def _rnn_gates(xc, wa, wx, ba, bx, sp):
    xcb = xc.astype(BF16)
    r = _sigmoid(_dot(xcb, wa) + ba)
    ig = _sigmoid(_dot(xcb, wx) + bx)
    la = -LRU_C * r * sp
    a = jnp.exp(la)
    t = jnp.tanh(la)
    f = jnp.sqrt(-2.0 * t / (1.0 - t))
    return r, ig, a, f


def _rnn_fwd(xr, gr, conv_w, conv_b, wa, wx, ba, bx, lam, comm=None):
    S = xr.shape[0]
    tb = min(256, S)

    def body(xr_ref, gr_ref, cw_ref, cb_ref, wa_ref, wx_ref, ba_ref, bx_ref, lam_ref, rec_ref, h_ref,
             halo_s, hc_s, a_s, b_s):
        @pl.when(pl.program_id(0) == 0)
        def _():
            halo_s[...] = jnp.zeros_like(halo_s)
            hc_s[...] = jnp.zeros_like(hc_s)

        x = xr_ref[...]
        xc = _conv4(x, halo_s[...], _w_rows(cw_ref), cb_ref[...])
        halo_s[...] = x[tb - 8:]
        _, ig, a, f = _rnn_gates(xc, wa_ref[...], wx_ref[...], ba_ref[...], bx_ref[...], _softplus_neg(lam_ref[...]))
        a_s[...] = a
        b_s[...] = f * ig * xc
        row8 = lax.broadcasted_iota(jnp.int32, (8, D_RNN), 0)

        def tile(t, hc):
            o = pl.multiple_of(t * 8, 8)
            at = a_s[pl.ds(o, 8), :]
            bt = b_s[pl.ds(o, 8), :]
            for s in (1, 2, 4):
                keep = row8 >= s
                a_sh = jnp.where(keep, pltpu.roll(at, s, 0), 1.0)
                b_sh = jnp.where(keep, pltpu.roll(bt, s, 0), 0.0)
                bt = at * b_sh + bt
                at = at * a_sh
            ht = at * hc + bt
            b_s[pl.ds(o, 8), :] = ht
            return _row_sum(jnp.where(row8 == 7, ht, 0.0))

        hc_s[0:1, :] = lax.fori_loop(0, tb // 8, tile, hc_s[0:1, :])
        h = b_s[...]
        h_ref[...] = h
        rec_ref[...] = (h * _gelu(gr_ref[...])).astype(BF16)

    vec = _resident((1, D_RNN))
    return _pcall(
        body, (xr, gr, conv_w, conv_b, wa, wx, ba, bx, lam), name="rnn_fwd", grid=(S // tb,), sem="arbitrary", comm=comm,
        in_specs=[_rows(tb, D_RNN), _rows(tb, D_RNN), _resident((4, D_RNN)), vec,
                  _resident((D_RNN, D_RNN)), _resident((D_RNN, D_RNN)), vec, vec, vec],
        out_specs=[_rows(tb, D_RNN), _rows(tb, D_RNN)],
        out_shape=[jax.ShapeDtypeStruct((S, D_RNN), BF16), jax.ShapeDtypeStruct((S, D_RNN), F32)],
        scratch_shapes=[pltpu.VMEM((8, D_RNN), F32), pltpu.VMEM((8, D_RNN), F32),
                        pltpu.VMEM((tb, D_RNN), F32), pltpu.VMEM((tb, D_RNN), F32)])


def _mix_ln1_up(x, cat, w_out, ln1_g, ln1_b, w_up, comm=None):
    S = x.shape[0]
    tb = min(256, S)
    nblk, _, wblk = w_up.shape
    half = nblk // 2

    def body(x_ref, cat_ref, wo_ref, g_ref, b_ref, wu_ref, z1_ref, h1_ref, gate_ref, val_ref):
        z1 = ALPHA * x_ref[...] + _dot(cat_ref[...], wo_ref[...])
        z1_ref[...] = z1
        xhat, _ = _ln_stats(z1)
        h1b = (xhat * g_ref[...] + b_ref[...]).astype(BF16)
        h1_ref[...] = h1b
        for j in range(nblk):
            up = _dot(h1b, wu_ref[j]).astype(BF16)
            dst = gate_ref if j < half else val_ref
            jj = j % half
            dst[:, jj * wblk:(jj + 1) * wblk] = up

    vec = _resident((1, D_MODEL))
    return _pcall(
        body, (x, cat, w_out, ln1_g, ln1_b, w_up), name="mix_ln1_up", grid=(S // tb,), comm=comm,
        in_specs=[_rows(tb, D_MODEL), _rows(tb, D_MODEL), _resident((D_MODEL, D_MODEL)), vec, vec,
                  _resident(w_up.shape)],
        out_specs=[_rows(tb, D_MODEL), _rows(tb, D_MODEL), _rows(tb, D_FF), _rows(tb, D_FF)],
        out_shape=[jax.ShapeDtypeStruct((S, D_MODEL), F32), jax.ShapeDtypeStruct((S, D_MODEL), BF16),
                   jax.ShapeDtypeStruct((S, D_FF), BF16), jax.ShapeDtypeStruct((S, D_FF), BF16)])


def _tail(gate, val, z1, p, tgt, fcw, fcb, w_down, w_pg, b_pg, w_pp, ln1_g, ln1_b, ln2_g, ln2_b):
    S = z1.shape[0]
    tb = min(256, S)
    t16 = tb // 16

    def body(gc_ref, gp_ref, val_ref, z1_ref, p_ref, t_ref, fcw_ref, fcb_ref, wd_ref, wpg_ref, bpg_ref, wpp_ref,
             g1_ref, b1_ref, g2_ref, b2_ref,
             act_ref, dz2_ref, dpre_ref, dpp_ref, dgc_ref, dval_ref, dh1_ref, acc_ref):
        i = pl.program_id(0)

        @pl.when(i == 0)
        def _():
            acc_ref[...] = jnp.zeros_like(acc_ref)

        gate = gc_ref[...].astype(F32)
        halo = jnp.where(i > 0, gp_ref[...].astype(F32)[8:16], 0.0)
        w = _w_rows(fcw_ref)
        gcv = fcb_ref[...] + w[2] * gate + w[1] * _shift_down(gate, halo, 1) + w[0] * _shift_down(gate, halo, 2)
        gl, dgl = _gelu_and_grad(gcv)
        val = val_ref[...].astype(F32)
        act = (gl * val).astype(BF16)
        act_ref[...] = act
        ffn = _dot(act, wd_ref[...])

        xhat1, _ = _ln_stats(z1_ref[...])
        h1 = xhat1 * g1_ref[...] + b1_ref[...]
        sg = _sigmoid(_dot(h1.astype(BF16), wpg_ref[...]) + bpg_ref[...])
        pp = _dot(p_ref[...].astype(BF16), wpp_ref[...])
        z2 = ALPHA * h1 + ffn + sg * pp
        xhat2, rstd2 = _ln_stats(z2)
        y = xhat2 * g2_ref[...] + b2_ref[...]
        err = y - t_ref[...]
        dy = err * (1.0 / D_MODEL)
        loss = 0.5 * jnp.sum(jnp.sum(err * err, axis=1, keepdims=True), axis=0, keepdims=True) * (1.0 / D_MODEL)
        dz2 = _ln_bwd(dy, xhat2, rstd2, g2_ref[...])
        dz2b = dz2.astype(BF16)
        dz2_ref[...] = dz2b
        dpre = dz2 * pp * sg * (1.0 - sg)
        dpreb = dpre.astype(BF16)
        dpre_ref[...] = dpreb
        dpp_ref[...] = (dz2 * sg).astype(BF16)
        dh1_ref[...] = ALPHA * dz2 + _dot_nt(dpreb, wpg_ref[...])
        dact = _dot_nt(dz2b, wd_ref[...])
        dval_ref[...] = (dact * gl).astype(BF16)
        dgc_ref[...] = (dact * val * dgl).astype(BF16)
        _put_rows(acc_ref, [_row_sum(dy * xhat2), _row_sum(dy), _row_sum(dpre),
                            jnp.broadcast_to(loss, (1, D_MODEL))])

    vec = _resident((1, D_MODEL))
    prev16 = pl.BlockSpec((16, D_FF), lambda i: (jnp.maximum(i * t16 - 1, 0), 0))
    return pl.pallas_call(
        body, name="tail", grid=(S // tb,),
        in_specs=[_rows(tb, D_FF), prev16, _rows(tb, D_FF), _rows(tb, D_MODEL), _rows(tb, PLE_DIM), _rows(tb, D_MODEL),
                  _resident((3, D_FF)), _resident((1, D_FF)), _resident((D_FF, D_MODEL)),
                  _resident((D_MODEL, D_MODEL)), vec, _resident((PLE_DIM, D_MODEL)), vec, vec, vec, vec],
        out_specs=[_rows(tb, D_FF), _rows(tb, D_MODEL), _rows(tb, D_MODEL), _rows(tb, D_MODEL), _rows(tb, D_FF),
                   _rows(tb, D_FF), _rows(tb, D_MODEL), _acc((8, D_MODEL))],
        out_shape=[jax.ShapeDtypeStruct((S, D_FF), BF16), jax.ShapeDtypeStruct((S, D_MODEL), BF16),
                   jax.ShapeDtypeStruct((S, D_MODEL), BF16), jax.ShapeDtypeStruct((S, D_MODEL), BF16),
                   jax.ShapeDtypeStruct((S, D_FF), BF16), jax.ShapeDtypeStruct((S, D_FF), BF16),
                   jax.ShapeDtypeStruct((S, D_MODEL), F32), jax.ShapeDtypeStruct((8, D_MODEL), F32)],
        compiler_params=_params("arbitrary"),
    )(gate, gate, val, z1, p, tgt, fcw, fcb, w_down, w_pg, b_pg, w_pp, ln1_g, ln1_b, ln2_g, ln2_b)


def _tn_matmul(a, b, name, out_dtype=BF16, tm=512, ts=512):
    S, M = a.shape
    N = b.shape[1]
    tm = min(tm, M)
    ts = min(ts, S)
    nk = S // ts

    def body(a_ref, b_ref, o_ref, acc_ref):
        k = pl.program_id(1)

        @pl.when(k == 0)
        def _():
            acc_ref[...] = jnp.zeros_like(acc_ref)

        acc_ref[...] += _dot_tn(a_ref[...].astype(BF16), b_ref[...].astype(BF16))

        @pl.when(k == nk - 1)
        def _():
            o_ref[...] = acc_ref[...].astype(out_dtype)

    return pl.pallas_call(
        body, name=name, grid=(M // tm, nk),
        in_specs=[pl.BlockSpec((ts, tm), lambda i, k: (k, i)), pl.BlockSpec((ts, N), lambda i, k: (k, 0))],
        out_specs=pl.BlockSpec((tm, N), lambda i, k: (i, 0)),
        out_shape=jax.ShapeDtypeStruct((M, N), out_dtype),
        scratch_shapes=[pltpu.VMEM((tm, N), F32)],
        compiler_params=_params("parallel", "arbitrary"),
    )(a, b)


def _up_bwd(dgc, gate, dval, dh1p, z1, w_up, fcw, w_out, ln1_g, comm=None):
    S = z1.shape[0]
    tb = min(256, S)
    t16 = tb // 16
    n16 = S // 16
    nblk, _, wblk = w_up.shape
    half = nblk // 2
    nsteps = S // tb

    def body(dgc_ref, dgn_ref, gc_ref, gp_ref, dval_ref, dh1p_ref, z1_ref, wu_ref, fcw_ref, wo_ref, g1_ref,
             dgate_ref, dz1_ref, dz1b_ref, datt_ref, drec_ref, accf_ref, accd_ref):
        i = pl.program_id(0)

        @pl.when(i == 0)
        def _():
            accf_ref[...] = jnp.zeros_like(accf_ref)
            accd_ref[...] = jnp.zeros_like(accd_ref)

        dg = dgc_ref[...].astype(F32)
        nxt = jnp.where(i < nsteps - 1, dgn_ref[...].astype(F32)[0:8], 0.0)
        w = _w_rows(fcw_ref)
        dgate = (w[2] * dg + w[1] * _shift_up(dg, nxt, 1) + w[0] * _shift_up(dg, nxt, 2)).astype(BF16)
        dgate_ref[...] = dgate
        gate = gc_ref[...].astype(F32)
        halo = jnp.where(i > 0, gp_ref[...].astype(F32)[8:16], 0.0)
        _put_rows(accf_ref, [_row_sum(dg * _shift_down(gate, halo, 2)), _row_sum(dg * _shift_down(gate, halo, 1)),
                             _row_sum(dg * gate), _row_sum(dg)])

        dh1 = dh1p_ref[...]
        for j in range(nblk):
            src = dgate if j < half else dval_ref[...]
            jj = j % half
            dh1 = dh1 + _dot_nt(src[:, jj * wblk:(jj + 1) * wblk], wu_ref[j])
        xhat1, rstd1 = _ln_stats(z1_ref[...])
        dz1 = _ln_bwd(dh1, xhat1, rstd1, g1_ref[...])
        dz1_ref[...] = dz1
        dz1b = dz1.astype(BF16)
        dz1b_ref[...] = dz1b
        dcat = _dot_nt(dz1b, wo_ref[...])
        datt_ref[...] = dcat[:, :D_ATT].astype(BF16)
        drec_ref[...] = dcat[:, D_ATT:]
        _put_rows(accd_ref, [_row_sum(dh1 * xhat1), _row_sum(dh1)])

    prev16 = pl.BlockSpec((16, D_FF), lambda i: (jnp.maximum(i * t16 - 1, 0), 0))
    next16 = pl.BlockSpec((16, D_FF), lambda i: (jnp.minimum((i + 1) * t16, n16 - 1), 0))
    return _pcall(
        body, (dgc, dgc, gate, gate, dval, dh1p, z1, w_up, fcw, w_out, ln1_g), name="up_bwd", grid=(nsteps,),
        sem="arbitrary", comm=comm,
        in_specs=[_rows(tb, D_FF), next16, _rows(tb, D_FF), prev16, _rows(tb, D_FF), _rows(tb, D_MODEL),
                  _rows(tb, D_MODEL), _resident(w_up.shape), _resident((3, D_FF)), _resident((D_MODEL, D_MODEL)),
                  _resident((1, D_MODEL))],
        out_specs=[_rows(tb, D_FF), _rows(tb, D_MODEL), _rows(tb, D_MODEL), _rows(tb, D_ATT), _rows(tb, D_RNN),
                   _acc((8, D_FF)), _acc((8, D_MODEL))],
        out_shape=[jax.ShapeDtypeStruct((S, D_FF), BF16), jax.ShapeDtypeStruct((S, D_MODEL), F32),
                   jax.ShapeDtypeStruct((S, D_MODEL), BF16), jax.ShapeDtypeStruct((S, D_ATT), BF16),
                   jax.ShapeDtypeStruct((S, D_RNN), F32), jax.ShapeDtypeStruct((8, D_FF), F32),
                   jax.ShapeDtypeStruct((8, D_MODEL), F32)])


def _up_grad(h1b, dgate, dval, nblk, wblk):
    S = h1b.shape[0]
    ts = min(512, S)
    nk = S // ts
    half = nblk // 2

    def body(a_ref, g_ref, v_ref, o_ref, acc_ref):
        j = pl.program_id(0)
        k = pl.program_id(1)

        @pl.when(k == 0)
        def _():
            acc_ref[...] = jnp.zeros_like(acc_ref)

        @pl.when(j < half)
        def _():
            acc_ref[...] += _dot_tn(a_ref[...], g_ref[...])

        @pl.when(j >= half)
        def _():
            acc_ref[...] += _dot_tn(a_ref[...], v_ref[...])

        @pl.when(k == nk - 1)
        def _():
            o_ref[0] = acc_ref[...].astype(BF16)

    return pl.pallas_call(
        body, name="up_grad", grid=(nblk, nk),
        in_specs=[pl.BlockSpec((ts, D_MODEL), lambda j, k: (k, 0)),
                  pl.BlockSpec((ts, wblk), lambda j, k: (k, jnp.minimum(j, half - 1))),
                  pl.BlockSpec((ts, wblk), lambda j, k: (k, jnp.maximum(j - half, 0)))],
        out_specs=pl.BlockSpec((1, D_MODEL, wblk), lambda j, k: (j, 0, 0)),
        out_shape=jax.ShapeDtypeStruct((nblk, D_MODEL, wblk), BF16),
        scratch_shapes=[pltpu.VMEM((D_MODEL, wblk), F32)],
        compiler_params=_params("parallel", "arbitrary"),
    )(h1b, dgate, dval)


def _attn_bwd(q, k, v, do, sinks, comm=None):
    S = q.shape[0]
    grp = N_HEADS // N_KV

    def body(sink_ref, q_ref, kc_ref, kp_ref, vc_ref, vp_ref, do_ref, dq_ref, dkc_ref, dkp_ref, dvc_ref, dvp_ref,
             ds_ref):
        i = pl.program_id(0)

        @pl.when(i == 0)
        def _():
            ds_ref[...] = jnp.zeros_like(ds_ref)

        valid = _band_mask(i)
        row8 = lax.broadcasted_iota(jnp.int32, (8, 128), 0)
        dqs, dks, dvs = [], [], []
        dsink = jnp.zeros((8, 128), F32)
        qv = q_ref[...]
        dov = do_ref[...]
        kall = jnp.concatenate([kp_ref[...], kc_ref[...]], axis=0)
        vall = jnp.concatenate([vp_ref[...], vc_ref[...]], axis=0)
        for g in range(N_KV):
            kcat = kall[:, g * HEAD_DIM:(g + 1) * HEAD_DIM]
            vcat = vall[:, g * HEAD_DIM:(g + 1) * HEAD_DIM]
            dk = jnp.zeros((2 * QBLK, HEAD_DIM), F32)
            dv = jnp.zeros((2 * QBLK, HEAD_DIM), F32)
            for hh in range(grp):
                h = g * grp + hh
                hc = slice(h * HEAD_DIM, (h + 1) * HEAD_DIM)
                qh = qv[:, hc]
                doh = dov[:, hc]
                s = jnp.where(valid, _dot_nt(qh, kcat), -1e30)
                sink = sink_ref[h]
                m = jnp.maximum(jnp.max(s, axis=1, keepdims=True), sink)
                e = jnp.exp(s - m)
                es = jnp.exp(sink - m)
                inv = 1.0 / (jnp.sum(e, axis=1, keepdims=True) + es)
                p = e * inv
                dp = _dot_nt(doh, vcat)
                delta = jnp.sum(p * dp, axis=1, keepdims=True)
                dsc = (p * (dp - delta)).astype(BF16)
                dqs.append(_dot(dsc, kcat) * (HEAD_DIM ** -0.5))
                dk = dk + _dot_tn(dsc, qh)
                dv = dv + _dot_tn(p.astype(BF16), doh)
                tot = jnp.sum(-es * inv * delta, axis=0, keepdims=True)
                dsink = dsink + jnp.where(row8 == h, tot, 0.0)
            dks.append(dk)
            dvs.append(dv)
        dq_ref[...] = jnp.concatenate(dqs, axis=1).astype(BF16)
        dk = jnp.concatenate(dks, axis=1)
        dv = jnp.concatenate(dvs, axis=1)
        dkp_ref[...] = dk[:QBLK]
        dkc_ref[...] = dk[QBLK:]
        dvp_ref[...] = dv[:QBLK]
        dvc_ref[...] = dv[QBLK:]
        ds_ref[...] += dsink

    kvs = jax.ShapeDtypeStruct((S, D_KV), F32)
    return _pcall(
        body, (sinks, q, k, k, v, v, do), name="attn_bwd", grid=(S // QBLK,), sem="arbitrary", comm=comm,
        in_specs=[pl.BlockSpec(memory_space=pltpu.SMEM), _rows(QBLK, D_ATT)] + _attn_specs() + [_rows(QBLK, D_ATT)],
        out_specs=[_rows(QBLK, D_ATT), _rows(QBLK, D_KV), _rows(QBLK, D_KV), _rows(QBLK, D_KV), _rows(QBLK, D_KV),
                   _acc((8, 128))],
        out_shape=[jax.ShapeDtypeStruct((S, D_ATT), BF16), kvs, kvs, kvs, kvs, jax.ShapeDtypeStruct((8, 128), F32)])


def _rnn_bwd(xr, gr, h, drec, conv_w, conv_b, wa, wx, ba, bx, lam, comm=None):
    S = xr.shape[0]
    tb = min(256, S)
    t8 = tb // 8
    nsteps = S // tb

    def body(xr_ref, xp_ref, gr_ref, h_ref, hp_ref, drec_ref, cw_ref, cb_ref, wa_ref, wx_ref, ba_ref, bx_ref, lam_ref,
             dxr_ref, dgr_ref, gwa_ref, gwx_ref, acc_ref, carry_s, dxc_halo_s, a_s, d_s):
        i = pl.program_id(0)
        blk = nsteps - 1 - i

        @pl.when(i == 0)
        def _():
            gwa_ref[...] = jnp.zeros_like(gwa_ref)
            gwx_ref[...] = jnp.zeros_like(gwx_ref)
            acc_ref[...] = jnp.zeros_like(acc_ref)
            carry_s[...] = jnp.zeros_like(carry_s)
            dxc_halo_s[...] = jnp.zeros_like(dxc_halo_s)

        x = xr_ref[...]
        xhalo = jnp.where(blk > 0, xp_ref[...], 0.0)
        cw = _w_rows(cw_ref)
        xs = [_shift_down(x, xhalo, 3), _shift_down(x, xhalo, 2), _shift_down(x, xhalo, 1), x]
        xc = cb_ref[...] + cw[0] * xs[0] + cw[1] * xs[1] + cw[2] * xs[2] + cw[3] * xs[3]
        sp = _softplus_neg(lam_ref[...])
        r, ig, a, f = _rnn_gates(xc, wa_ref[...], wx_ref[...], ba_ref[...], bx_ref[...], sp)
        hcur = h_ref[...]
        hprev = _shift_down(hcur, jnp.where(blk > 0, hp_ref[...], 0.0), 1)
        gl, dgl = _gelu_and_grad(gr_ref[...])
        drec = drec_ref[...]
        dgr_ref[...] = (drec * hcur * dgl).astype(BF16)
        a_s[...] = a
        d_s[...] = drec * gl
        row8 = lax.broadcasted_iota(jnp.int32, (8, D_RNN), 0)

        def tile(t, c):
            o = pl.multiple_of((t8 - 1 - t) * 8, 8)
            a8 = a_s[pl.ds(o, 8), :]
            dt = d_s[pl.ds(o, 8), :]
            at = jnp.where(row8 == 7, 1.0, pltpu.roll(a8, 7, 0))
            for s in (1, 2, 4):
                keep = row8 < 8 - s
                a_sh = jnp.where(keep, pltpu.roll(at, 8 - s, 0), 1.0)
                d_sh = jnp.where(keep, pltpu.roll(dt, 8 - s, 0), 0.0)
                dt = at * d_sh + dt
                at = at * a_sh
            lt = at * c + dt
            d_s[pl.ds(o, 8), :] = lt
            return _row_sum(jnp.where(row8 == 0, a8 * lt, 0.0))

        carry_s[0:1, :] = lax.fori_loop(0, t8, tile, carry_s[0:1, :])
        lmb = d_s[...]
        a2 = a * a
        dla = lmb * hprev * a - lmb * ig * xc * (a2 / f)
        di = lmb * f * xc
        dr = dla * (-LRU_C) * sp
        dpa = dr * r * (1.0 - r)
        dpx = di * ig * (1.0 - ig)
        dpab = dpa.astype(BF16)
        dpxb = dpx.astype(BF16)
        xcb = xc.astype(BF16)
        gwa_ref[...] += _dot_tn(xcb, dpab)
        gwx_ref[...] += _dot_tn(xcb, dpxb)
        dxc = lmb * f * ig + _dot_nt(dpab, wa_ref[...]) + _dot_nt(dpxb, wx_ref[...])
        nxt = dxc_halo_s[...]
        dxr = cw[3] * dxc
        for s in (1, 2, 3):
            dxr = dxr + cw[3 - s] * _shift_up(dxc, nxt, s)
        dxr_ref[...] = dxr.astype(BF16)
        dxc_halo_s[...] = dxc[:8]
        dlam = _row_sum(dla * (-LRU_C) * r) * (-1.0 / (1.0 + jnp.exp(lam_ref[...])))
        _put_rows(acc_ref, [_row_sum(dxc * xs[0]), _row_sum(dxc * xs[1]), _row_sum(dxc * xs[2]), _row_sum(dxc * xs[3]),
                            _row_sum(dxc), _row_sum(dpa), _row_sum(dpx), dlam])

    rev = lambda i: (nsteps - 1 - i, 0)
    prev8 = lambda i: (jnp.maximum((nsteps - 1 - i) * t8 - 1, 0), 0)
    blkspec = pl.BlockSpec((tb, D_RNN), rev)
    halo8 = pl.BlockSpec((8, D_RNN), prev8)
    vec = _resident((1, D_RNN))
    return _pcall(
        body, (xr, xr, gr, h, h, drec, conv_w, conv_b, wa, wx, ba, bx, lam), name="rnn_bwd", grid=(nsteps,),
        sem="arbitrary", comm=comm,
        in_specs=[blkspec, halo8, blkspec, blkspec, halo8, blkspec, _resident((4, D_RNN)), vec,
                  _resident((D_RNN, D_RNN)), _resident((D_RNN, D_RNN)), vec, vec, vec],
        out_specs=[blkspec, blkspec, _acc((D_RNN, D_RNN)), _acc((D_RNN, D_RNN)), _acc((8, D_RNN))],
        out_shape=[jax.ShapeDtypeStruct((S, D_RNN), BF16), jax.ShapeDtypeStruct((S, D_RNN), BF16),
                   jax.ShapeDtypeStruct((D_RNN, D_RNN), F32), jax.ShapeDtypeStruct((D_RNN, D_RNN), F32),
                   jax.ShapeDtypeStruct((8, D_RNN), F32)],
        scratch_shapes=[pltpu.VMEM((8, D_RNN), F32), pltpu.VMEM((8, D_RNN), F32),
                        pltpu.VMEM((tb, D_RNN), F32), pltpu.VMEM((tb, D_RNN), F32)])


def _in_bwd(dq, dkc, dkp, dvc, dvp, dxr, dgr, dz1, w_in, comm=None):
    S = dz1.shape[0]
    tb = min(256, S)
    nsteps = S // tb
    nq = S // QBLK
    r = tb // QBLK

    def body(dq_ref, dkc_ref, dkp_ref, dkn_ref, dvc_ref, dvp_ref, dvn_ref, dxr_ref, dgr_ref, dz1_ref, w_ref,
             du_ref, dx_ref):
        i = pl.program_id(0)
        last = i == nsteps - 1

        def shifted(prev_ref, next_ref):
            nxt = jnp.where(last, 0.0, next_ref[...])
            return jnp.concatenate([prev_ref[QBLK:], nxt], axis=0) if r > 1 else nxt

        dk = (dkc_ref[...] + shifted(dkp_ref, dkn_ref)).astype(BF16)
        dv = (dvc_ref[...] + shifted(dvp_ref, dvn_ref)).astype(BF16)
        du = jnp.concatenate([dq_ref[...], dk, dv, dxr_ref[...], dgr_ref[...]], axis=1)
        du_ref[...] = du
        dx_ref[...] = ALPHA * dz1_ref[...] + _dot_nt(du, w_ref[...])

    nextq = pl.BlockSpec((QBLK, D_KV), lambda i: (jnp.minimum((i + 1) * r, nq - 1), 0))
    return _pcall(
        body, (dq, dkc, dkp, dkp, dvc, dvp, dvp, dxr, dgr, dz1, w_in), name="in_bwd", grid=(nsteps,), comm=comm,
        in_specs=[_rows(tb, D_ATT), _rows(tb, D_KV), _rows(tb, D_KV), nextq, _rows(tb, D_KV), _rows(tb, D_KV), nextq,
                  _rows(tb, D_RNN), _rows(tb, D_RNN), _rows(tb, D_MODEL), _resident((D_MODEL, D_IN))],
        out_specs=[_rows(tb, D_IN), _rows(tb, D_MODEL)],
        out_shape=[jax.ShapeDtypeStruct((S, D_IN), BF16), jax.ShapeDtypeStruct((S, D_MODEL), F32)])


def _block_diag(w):
    eye = jnp.eye(RNN_BLOCKS, dtype=w.dtype)
    return (w[:, :, None, :] * eye[:, None, :, None]).reshape(D_RNN, D_RNN).astype(BF16)


def _diag_blocks(g):
    g4 = g.reshape(RNN_BLOCKS, HEAD_DIM, RNN_BLOCKS, HEAD_DIM)
    return jnp.stack([g4[b, :, b, :] for b in range(RNN_BLOCKS)])


def _adamw(w, g, m, v):
    m = ADAM_B1 * m + (1.0 - ADAM_B1) * g
    v = ADAM_B2 * v + (1.0 - ADAM_B2) * (g * g)
    m_hat = m / (1.0 - ADAM_B1 ** ADAM_STEP)
    v_hat = v / (1.0 - ADAM_B2 ** ADAM_STEP)
    delta = -ADAM_LR * (m_hat / (jnp.sqrt(v_hat) + ADAM_EPS) + ADAM_WD * w)
    return delta, m, v


def _sum_adamw(parts, w, m, v, name):
    R, C = w.shape
    rb = 128
    assert R % rb == 0

    def body(p_ref, w_ref, m_ref, v_ref, g_out, d_out, m_out, v_out):
        g = p_ref[0].astype(F32)
        for d in range(1, N_DEV):
            g = g + p_ref[d].astype(F32)
        delta, mn, vn = _adamw(w_ref[...], g, m_ref[...], v_ref[...])
        g_out[...] = g
        d_out[...] = delta
        m_out[...] = mn
        v_out[...] = vn

    blk = _rows(rb, C)
    out = jax.ShapeDtypeStruct((R, C), F32)
    return pl.pallas_call(
        body, name=name, grid=(R // rb,),
        in_specs=[pl.BlockSpec((N_DEV, rb, C), lambda i: (0, i, 0)), blk, blk, blk],
        out_specs=[blk, blk, blk, blk], out_shape=[out, out, out, out],
        compiler_params=_params("parallel"),
    )(parts, w, m, v)


def _sum_parts(parts):
    _, R, C = parts.shape

    def body(p_ref, o_ref):
        g = p_ref[0]
        for d in range(1, N_DEV):
            g = g + p_ref[d]
        o_ref[...] = g

    return pl.pallas_call(body, name="small_sum", out_shape=jax.ShapeDtypeStruct((R, C), F32))(parts)


def _adamw_packed(g, w, m, v):
    def body(g_ref, w_ref, m_ref, v_ref, d_out, m_out, v_out):
        delta, mn, vn = _adamw(w_ref[...], g_ref[...], m_ref[...], v_ref[...])
        d_out[...] = delta
        m_out[...] = mn
        v_out[...] = vn

    out = jax.ShapeDtypeStruct(g.shape, F32)
    return pl.pallas_call(body, name="small_adamw", out_shape=[out, out, out])(g, w, m, v)


def _pack(vectors, rows=None):
    parts, offs, o = [], [], 0
    for vct in vectors:
        n = vct.shape[0]
        pad = -n % 128
        parts.append(jnp.pad(vct, (0, pad)) if pad else vct)
        offs.append(o)
        o += n + pad
    r = o // 128
    rows = rows or -(-r // 8) * 8
    flat = jnp.concatenate(parts + ([jnp.zeros(((rows - r) * 128,), F32)] if rows > r else []))
    return flat.reshape(rows, 128), offs


_SMALL = [("loss", 1), ("attn_sinks", 8), ("rnn_conv_w", 4 * D_RNN), ("rnn_conv_b", D_RNN),
          ("gate_a_w", RNN_BLOCKS * HEAD_DIM * HEAD_DIM), ("gate_a_b", D_RNN),
          ("gate_x_w", RNN_BLOCKS * HEAD_DIM * HEAD_DIM), ("gate_x_b", D_RNN), ("lru_lambda", D_RNN),
          ("ln1_g", D_MODEL), ("ln1_b", D_MODEL), ("ffn_conv_w", 3 * D_FF), ("ffn_conv_b", D_FF),
          ("ple_gate_b", D_MODEL), ("ln2_g", D_MODEL), ("ln2_b", D_MODEL)]


def kernel(x, p, w_in, attn_sinks, rnn_conv_w, rnn_conv_b, gate_a_w, gate_a_b, gate_x_w, gate_x_b, lru_lambda, w_out, ln1_g, ln1_b, w_ffn_up, ffn_conv_w, ffn_conv_b, w_ffn_down, ple_gate_w, ple_gate_b, ple_proj, ln2_g, ln2_b, loss_target, m_w_in, m_attn_sinks, m_rnn_conv_w, m_rnn_conv_b, m_gate_a_w, m_gate_a_b, m_gate_x_w, m_gate_x_b, m_lru_lambda, m_w_out, m_ln1_g, m_ln1_b, m_w_ffn_up, m_ffn_conv_w, m_ffn_conv_b, m_w_ffn_down, m_ple_gate_w, m_ple_gate_b, m_ple_proj, m_ln2_g, m_ln2_b, v_w_in, v_attn_sinks, v_rnn_conv_w, v_rnn_conv_b, v_gate_a_w, v_gate_a_b, v_gate_x_w, v_gate_x_b, v_lru_lambda, v_w_out, v_ln1_g, v_ln1_b, v_w_ffn_up, v_ffn_conv_w, v_ffn_conv_b, v_w_ffn_down, v_ple_gate_w, v_ple_gate_b, v_ple_proj, v_ln2_g, v_ln2_b):
    me = _dev_index(*_place())
    col_blocks = lambda g: g.reshape(g.shape[0], N_DEV, g.shape[1] // N_DEV).transpose(1, 0, 2)
    from_col_blocks = lambda g: g.transpose(1, 0, 2).reshape(g.shape[1], N_DEV * g.shape[2])

    xs, ps, tgt, sinks = x[0], p[0, 0], loss_target[0], attn_sinks[0]
    wa, wx = _block_diag(gate_a_w[0]), _block_diag(gate_x_w[0])
    row_blocks = lambda g: g.reshape(N_DEV, g.shape[0] // N_DEV, g.shape[1])

    conv_cols = jnp.concatenate([rnn_conv_w[0].reshape(1, -1), ffn_conv_w[0].reshape(1, -1)], axis=1)
    n_rc, n_fc = 4 * D_RNN // N_DEV, 3 * D_FF // N_DEV
    g_in, g_conv = _comm_call(_Gather([w_in[0].astype(BF16), jnp.broadcast_to(conv_cols, (8, n_rc + n_fc))]),
                              "gather_w_in")
    w_in_full = from_col_blocks(g_in)
    rcw = from_col_blocks(g_conv[:, 0, :n_rc].reshape(N_DEV, 4, D_RNN // N_DEV))
    fcw = from_col_blocks(g_conv[:, 0, n_rc:].reshape(N_DEV, 3, D_FF // N_DEV))
    w_up_shard = w_ffn_up[0].astype(BF16)
    half_rows = D_MODEL // 2

    q, k, v, xr, gr = _in_proj(xs, w_in_full)
    (att,), (g_out, g_up_top) = _attn_fwd(q, k, v, sinks,
                                          comm=_Gather([w_out[0].astype(BF16), w_up_shard[:half_rows]]))
    (rec, h), (g_up_bot,) = _rnn_fwd(xr, gr, rcw, rnn_conv_b, wa, wx, gate_a_b, gate_x_b, lru_lambda,
                                     comm=_Gather([w_up_shard[half_rows:]]))
    cat = jnp.concatenate([att, rec], axis=1)
    w_out_full = g_out.reshape(D_MODEL, D_MODEL)
    w_up_blocks = jnp.concatenate([g_up_top, g_up_bot], axis=1)
    (z1, h1b, gate, val), (g_down, g_pg, g_pp) = _mix_ln1_up(
        xs, cat, w_out_full, ln1_g, ln1_b, w_up_blocks,
        comm=_Gather([w_ffn_down[0].astype(BF16), ple_gate_w[0].astype(BF16), ple_proj[0].astype(BF16)]))
    act, dz2b, dpreb, dppb, dgc, dval, dh1p, acc_t = _tail(
        gate, val, z1, ps, tgt, fcw, ffn_conv_b, g_down.reshape(D_FF, D_MODEL), g_pg.reshape(D_MODEL, D_MODEL), ple_gate_b,
        from_col_blocks(g_pp), ln1_g, ln1_b, ln2_g, ln2_b)

    gd_t = _tn_matmul(dz2b, act, "down_grad")
    gd_pg = _tn_matmul(h1b, dpreb, "pg_grad")
    gd_pp = _tn_matmul(ps, dppb, "pp_grad", tm=PLE_DIM)
    (dgate, dz1, dz1b, datt, drec, acc_f, acc_d), (r_down, r_pg, r_pp) = _up_bwd(
        dgc, gate, dval, dh1p, z1, w_up_blocks, fcw, w_out_full, ln1_g,
        comm=_Exchange([row_blocks(gd_t.T), row_blocks(gd_pg), col_blocks(gd_pp)]))
    gd_up = _up_grad(h1b, dgate, dval, N_DEV, D_FF * 2 // N_DEV)
    gd_out = _tn_matmul(cat, dz1b, "out_grad")
    (dq, dkc, dkp, dvc, dvp, acc_s), (r_up,) = _attn_bwd(q, k, v, datt, sinks, comm=_Exchange([gd_up]))
    (dxr, dgr, g_wa, g_wx, acc_r), (r_out,) = _rnn_bwd(xr, gr, h, drec, rcw, rnn_conv_b, wa, wx, gate_a_b, gate_x_b,
                                                       lru_lambda, comm=_Exchange([row_blocks(gd_out)]))
    small = {
        "loss": acc_t[3, :1], "ln2_g": acc_t[0], "ln2_b": acc_t[1], "ple_gate_b": acc_t[2],
        "ffn_conv_w": acc_f[0:3].reshape(-1), "ffn_conv_b": acc_f[3], "ln1_g": acc_d[0], "ln1_b": acc_d[1],
        "attn_sinks": acc_s[:, 0], "rnn_conv_w": acc_r[0:4].reshape(-1), "rnn_conv_b": acc_r[4],
        "gate_a_b": acc_r[5], "gate_x_b": acc_r[6], "lru_lambda": acc_r[7],
        "gate_a_w": _diag_blocks(g_wa).reshape(-1), "gate_x_w": _diag_blocks(g_wx).reshape(-1),
    }
    packed, offs = _pack([small[n].astype(F32).reshape(-1) for n, _ in _SMALL])
    (du, dx), (r_small,) = _in_bwd(dq, dkc, dkp, dvc, dvp, dxr, dgr, dz1, w_in_full,
                                   comm=_Exchange([jnp.broadcast_to(packed, (N_DEV,) + packed.shape)]))
    gd_in = _tn_matmul(xs, du, "in_grad")
    (r_in,) = _comm_call(_Exchange([col_blocks(gd_in)]), "exchange_w_in")

    outs = {}
    for name, parts, w, m, v in [("w_in", r_in, w_in, m_w_in, v_w_in), ("w_out", r_out, w_out, m_w_out, v_w_out),
                                 ("w_ffn_up", r_up, w_ffn_up, m_w_ffn_up, v_w_ffn_up),
                                 ("w_ffn_down", r_down, w_ffn_down, m_w_ffn_down, v_w_ffn_down),
                                 ("ple_gate_w", r_pg, ple_gate_w, m_ple_gate_w, v_ple_gate_w),
                                 ("ple_proj", r_pp, ple_proj, m_ple_proj, v_ple_proj)]:
        res = _sum_adamw(parts, w[0], m[0], v[0], "adamw_" + name)
        outs[name] = [r[None] for r in res]

    total = _sum_parts(r_small).reshape(-1)
    gsmall = {n: total[o:o + size] for (n, size), o in zip(_SMALL, offs)}
    loss = gsmall["loss"][0]
    shard = lambda g, rows: lax.dynamic_slice_in_dim(g.reshape(rows, -1), me * (g.shape[0] // rows // N_DEV),
                                                     g.shape[0] // rows // N_DEV, axis=1).reshape(-1)
    gsmall["rnn_conv_w"] = shard(gsmall["rnn_conv_w"], 4)
    gsmall["ffn_conv_w"] = shard(gsmall["ffn_conv_w"], 3)
    given = dict(attn_sinks=(attn_sinks, m_attn_sinks, v_attn_sinks), rnn_conv_w=(rnn_conv_w, m_rnn_conv_w, v_rnn_conv_w),
                 rnn_conv_b=(rnn_conv_b, m_rnn_conv_b, v_rnn_conv_b), gate_a_w=(gate_a_w, m_gate_a_w, v_gate_a_w),
                 gate_a_b=(gate_a_b, m_gate_a_b, v_gate_a_b), gate_x_w=(gate_x_w, m_gate_x_w, v_gate_x_w),
                 gate_x_b=(gate_x_b, m_gate_x_b, v_gate_x_b), lru_lambda=(lru_lambda, m_lru_lambda, v_lru_lambda),
                 ln1_g=(ln1_g, m_ln1_g, v_ln1_g), ln1_b=(ln1_b, m_ln1_b, v_ln1_b),
                 ffn_conv_w=(ffn_conv_w, m_ffn_conv_w, v_ffn_conv_w), ffn_conv_b=(ffn_conv_b, m_ffn_conv_b, v_ffn_conv_b),
                 ple_gate_b=(ple_gate_b, m_ple_gate_b, v_ple_gate_b), ln2_g=(ln2_g, m_ln2_g, v_ln2_g),
                 ln2_b=(ln2_b, m_ln2_b, v_ln2_b))
    names = [n for n, _ in _SMALL if n != "loss"]
    pg, poffs = _pack([gsmall[n] for n in names])
    pw, _ = _pack([given[n][0].reshape(-1) for n in names])
    pm, _ = _pack([given[n][1].reshape(-1) for n in names])
    pv, _ = _pack([given[n][2].reshape(-1) for n in names])
    res = _adamw_packed(pg, pw, pm, pv)
    for n, o in zip(names, poffs):
        shape = given[n][0].shape
        size = given[n][0].size
        outs[n] = [gsmall[n].reshape(shape)] + [r.reshape(-1)[o:o + size].reshape(shape) for r in res]

    order = ["w_in", "attn_sinks", "rnn_conv_w", "rnn_conv_b", "gate_a_w", "gate_a_b", "gate_x_w", "gate_x_b",
             "lru_lambda", "w_out", "ln1_g", "ln1_b", "w_ffn_up", "ffn_conv_w", "ffn_conv_b", "w_ffn_down",
             "ple_gate_w", "ple_gate_b", "ple_proj", "ln2_g", "ln2_b"]
    return (loss, dx[None], *[outs[n][0] for n in order], *[outs[n][1] for n in order],
            *[outs[n][2] for n in order], *[outs[n][3] for n in order])
```

```python
import jax
import jax.numpy as jnp
from jax import lax
from jax.experimental import pallas as pl
from jax.experimental.pallas import tpu as pltpu

F32 = jnp.float32
BF16 = jnp.bfloat16

D_MODEL = 1024
D_ATT = 512
D_KV = 128
HEAD_DIM = 64
N_HEADS = 8
N_KV = 2
D_RNN = 512
RNN_BLOCKS = 8
D_IN = 1792
D_FF = 3072
PLE_DIM = 256
QBLK = 128
N_DEV = 8
ALPHA = float(2 ** 0.25)
LN_EPS = 1e-5
LRU_C = 8.0
ADAM_LR, ADAM_B1, ADAM_B2, ADAM_EPS, ADAM_WD, ADAM_STEP = 0.001, 0.9, 0.999, 1e-08, 0.01, 10

V7X_VMEM_LIMIT = 56 * 1024 * 1024
MESH = pl.DeviceIdType.MESH


def _params(*sem, vmem=V7X_VMEM_LIMIT):
    return pltpu.CompilerParams(dimension_semantics=sem or None, vmem_limit_bytes=vmem)


def _resident(shape):
    return pl.BlockSpec(shape, lambda *_: (0,) * len(shape), pipeline_mode=pl.Buffered(1))


def _rows(tb, cols):
    return pl.BlockSpec((tb, cols), lambda i: (i, 0))


def _acc(shape):
    return pl.BlockSpec(shape, lambda *_: (0,) * len(shape))


def _dot(a, b):
    return jnp.dot(a, b, preferred_element_type=F32)


def _dot_nt(a, b):
    return lax.dot_general(a, b, (((1,), (1,)), ((), ())), preferred_element_type=F32)


def _dot_tn(a, b):
    return lax.dot_general(a, b, (((0,), (0,)), ((), ())), preferred_element_type=F32)


def _sigmoid(x):
    return 1.0 / (1.0 + jnp.exp(-x))


_GELU_C = 0.7978845608028654
_GELU_K = 0.044715


def _gelu_and_grad(x):
    t = jnp.tanh(_GELU_C * (x + _GELU_K * x * x * x))
    g = 0.5 * x * (1.0 + t)
    dg = 0.5 * (1.0 + t) + 0.5 * x * (1.0 - t * t) * _GELU_C * (1.0 + 3.0 * _GELU_K * x * x)
    return g, dg


def _gelu(x):
    return 0.5 * x * (1.0 + jnp.tanh(_GELU_C * (x + _GELU_K * x * x * x)))


def _ln_stats(z):
    mu = jnp.mean(z, axis=-1, keepdims=True)
    zc = z - mu
    var = jnp.mean(zc * zc, axis=-1, keepdims=True)
    rstd = lax.rsqrt(var + LN_EPS)
    return zc * rstd, rstd


def _ln_bwd(dy, xhat, rstd, g):
    dxh = dy * g
    m1 = jnp.mean(dxh, axis=-1, keepdims=True)
    m2 = jnp.mean(dxh * xhat, axis=-1, keepdims=True)
    return rstd * (dxh - m1 - xhat * m2)


def _softplus_neg(lam):
    u = jnp.exp(-jnp.abs(lam))
    w = 1.0 + u
    d = w - 1.0
    log1p_u = jnp.where(d == 0.0, u, jnp.log(w) * (u / jnp.where(d == 0.0, 1.0, d)))
    return jnp.maximum(-lam, 0.0) + log1p_u


def _shift_down(x, halo, s):
    xs = pltpu.roll(x, s, 0)
    hs = pltpu.roll(halo, s, 0)
    row8 = lax.broadcasted_iota(jnp.int32, hs.shape, 0)
    first = jnp.where(row8 < s, hs, xs[:8])
    return jnp.concatenate([first, xs[8:]], axis=0)


def _shift_up(x, halo, s):
    n = x.shape[0]
    xs = pltpu.roll(x, n - s, 0)
    hs = pltpu.roll(halo, 8 - s, 0)
    row8 = lax.broadcasted_iota(jnp.int32, hs.shape, 0)
    last = jnp.where(row8 >= 8 - s, hs, xs[n - 8:])
    return jnp.concatenate([xs[:n - 8], last], axis=0)


def _row_sum(x):
    return jnp.sum(x, axis=0, keepdims=True)


def _put_rows(acc_ref, rows):
    row8 = lax.broadcasted_iota(jnp.int32, acc_ref.shape, 0)
    upd = jnp.zeros(acc_ref.shape, F32)
    for r, vec in enumerate(rows):
        upd = jnp.where(row8 == r, vec, upd)
    acc_ref[...] += upd


def _place():
    return lax.axis_index("x"), lax.axis_index("y"), lax.axis_index("c")


def _dev_index(px, py, pc):
    return 4 * px + 2 * py + pc


_ANY = pl.BlockSpec(memory_space=pl.ANY)


class _Gather:
    def __init__(self, arrays):
        self.arrays = list(arrays)
        self.n = len(self.arrays)

    def out_shape(self):
        return [jax.ShapeDtypeStruct((N_DEV,) + s.shape, s.dtype) for s in self.arrays]

    def scratch(self):
        return [pltpu.SemaphoreType.DMA((self.n, 7)), pltpu.SemaphoreType.DMA((self.n, 7)),
                pltpu.SemaphoreType.DMA((self.n,))]

    def _parts(self, ins, outs, sems):
        send_sems, recv_sems, local_sems = sems
        x, y, c = _place()
        me, sibling = (x, y, c), (x, y, 1 - c)
        chips = [(1 - x, y), (x, 1 - y), (1 - x, 1 - y)]

        def copy(a, k, block, to, src=None):
            rows = outs[a].at[_dev_index(*block)]
            return pltpu.make_async_remote_copy(
                src_ref=rows if src is None else src, dst_ref=rows, send_sem=send_sems.at[a, k],
                recv_sem=recv_sems.at[a, k], device_id=to, device_id_type=MESH)

        rng = range(self.n)
        mine = [pltpu.make_async_copy(ins[a], outs[a].at[_dev_index(*me)], local_sems.at[a]) for a in rng]
        first = [copy(a, 0, me, sibling, src=ins[a]) for a in rng]
        first += [copy(a, 1 + j, me, (*chip, c), src=ins[a]) for j, chip in enumerate(chips) for a in rng]
        landed = [copy(a, 1 + j, (*chip, c), me) for j, chip in enumerate(chips) for a in rng]
        passed = [copy(a, 4 + j, (*chip, c), sibling) for j, chip in enumerate(chips) for a in rng]
        from_sibling = [copy(a, 0, sibling, me) for a in rng]
        from_sibling += [copy(a, 4 + j, (*chip, 1 - c), me) for j, chip in enumerate(chips) for a in rng]
        return mine, first, landed, passed, from_sibling

    def start(self, ins, outs, sems):
        mine, first, _, _, _ = self._parts(ins, outs, sems)
        for cp in mine + first:
            cp.start()

    def forward(self, ins, outs, sems):
        _, _, landed, passed, _ = self._parts(ins, outs, sems)
        for got, fwd in zip(landed, passed):
            got.wait_recv()
            fwd.start()

    def finish(self, ins, outs, sems):
        mine, first, _, passed, from_sibling = self._parts(ins, outs, sems)
        for cp in from_sibling:
            cp.wait_recv()
        for cp in first + passed:
            cp.wait_send()
        for cp in mine:
            cp.wait()

    def before(self, ins, outs, sems, step, nsteps):
        pl.when(step == 0)(lambda: self.start(ins, outs, sems))
        pl.when(step == (3 * nsteps) // 5)(lambda: self.forward(ins, outs, sems))

    def after(self, ins, outs, sems, step, nsteps):
        pl.when(step == nsteps - 1)(lambda: self.finish(ins, outs, sems))


class _Exchange:
    def __init__(self, arrays):
        self.arrays = list(arrays)
        self.n = len(self.arrays)

    def out_shape(self):
        return [jax.ShapeDtypeStruct(b.shape, b.dtype) for b in self.arrays]

    def scratch(self):
        return [pltpu.SemaphoreType.DMA((self.n, 7)), pltpu.SemaphoreType.DMA((self.n, 7)),
                pltpu.SemaphoreType.DMA((self.n,))]

    def _parts(self, ins, outs, sems):
        send_sems, recv_sems, local_sems = sems
        x, y, c = _place()
        me = _dev_index(x, y, c)
        peers = [(x ^ (k >> 2), y ^ ((k >> 1) & 1), c ^ (k & 1)) for k in range(1, N_DEV)]
        rng = range(self.n)
        mine = [pltpu.make_async_copy(ins[a].at[me], outs[a].at[me], local_sems.at[a]) for a in rng]
        sent = [pltpu.make_async_remote_copy(
            src_ref=ins[a].at[_dev_index(*to)], dst_ref=outs[a].at[me], send_sem=send_sems.at[a, k],
            recv_sem=recv_sems.at[a, k], device_id=to, device_id_type=MESH) for k, to in enumerate(peers) for a in rng]
        arrivals = [pltpu.make_async_remote_copy(
            src_ref=ins[a].at[me], dst_ref=outs[a].at[_dev_index(*frm)], send_sem=send_sems.at[a, k],
            recv_sem=recv_sems.at[a, k], device_id=frm, device_id_type=MESH) for k, frm in enumerate(peers) for a in rng]
        return mine, sent, arrivals

    def start(self, ins, outs, sems):
        mine, sent, _ = self._parts(ins, outs, sems)
        for cp in mine + sent:
            cp.start()

    def finish(self, ins, outs, sems):
        mine, sent, arrivals = self._parts(ins, outs, sems)
        for cp in arrivals:
            cp.wait_recv()
        for cp in sent:
            cp.wait_send()
        for cp in mine:
            cp.wait()

    def before(self, ins, outs, sems, step, nsteps):
        pl.when(step == 0)(lambda: self.start(ins, outs, sems))

    def after(self, ins, outs, sems, step, nsteps):
        pl.when(step == nsteps - 1)(lambda: self.finish(ins, outs, sems))


def _comm_call(comm, name):
    n = comm.n

    def body(*refs):
        ins, outs, sems = refs[:n], refs[n:2 * n], refs[2 * n:]
        comm.start(ins, outs, sems)
        if isinstance(comm, _Gather):
            comm.forward(ins, outs, sems)
        comm.finish(ins, outs, sems)

    return pl.pallas_call(body, name=name, in_specs=[_ANY] * n, out_specs=[_ANY] * n, out_shape=comm.out_shape(),
                          scratch_shapes=comm.scratch())(*comm.arrays)


def _pcall(body, args, *, name, grid, in_specs, out_specs, out_shape, scratch_shapes=(), sem="parallel", comm=None):
    if comm is None:
        res = pl.pallas_call(body, name=name, grid=grid, in_specs=in_specs, out_specs=out_specs, out_shape=out_shape,
                             scratch_shapes=list(scratch_shapes), compiler_params=_params(sem))(*args)
        return res, []
    n_in, n_out, n_scr, n = len(in_specs), len(out_specs), len(scratch_shapes), comm.n
    nsteps = grid[0]

    def hosted(*refs):
        ins, cin = refs[:n_in], refs[n_in:n_in + n]
        o0 = n_in + n
        outs, cout = refs[o0:o0 + n_out], refs[o0 + n_out:o0 + n_out + n]
        s0 = o0 + n_out + n
        scr, sems = refs[s0:s0 + n_scr], refs[s0 + n_scr:]
        step = pl.program_id(0)
        comm.before(cin, cout, sems, step, nsteps)
        body(*ins, *outs, *scr)
        comm.after(cin, cout, sems, step, nsteps)

    res = pl.pallas_call(
        hosted, name=name, grid=grid, in_specs=list(in_specs) + [_ANY] * n, out_specs=list(out_specs) + [_ANY] * n,
        out_shape=list(out_shape) + comm.out_shape(), scratch_shapes=list(scratch_shapes) + comm.scratch(),
        compiler_params=_params("arbitrary"))(*args, *comm.arrays)
    return res[:n_out], res[n_out:]


def _in_proj(x, w_in):
    S = x.shape[0]
    tb = min(512, S)

    def body(x_ref, w_ref, q_ref, k_ref, v_ref, xr_ref, gr_ref):
        u = _dot(x_ref[...].astype(BF16), w_ref[...])
        q_ref[...] = (u[:, :D_ATT] * (HEAD_DIM ** -0.5)).astype(BF16)
        k_ref[...] = u[:, D_ATT:D_ATT + D_KV].astype(BF16)
        v_ref[...] = u[:, D_ATT + D_KV:D_ATT + 2 * D_KV].astype(BF16)
        xr_ref[...] = u[:, D_ATT + 2 * D_KV:D_ATT + 2 * D_KV + D_RNN]
        gr_ref[...] = u[:, D_ATT + 2 * D_KV + D_RNN:]

    return pl.pallas_call(
        body, name="in_proj", grid=(S // tb,),
        in_specs=[_rows(tb, D_MODEL), _resident((D_MODEL, D_IN))],
        out_specs=[_rows(tb, D_ATT), _rows(tb, D_KV), _rows(tb, D_KV), _rows(tb, D_RNN), _rows(tb, D_RNN)],
        out_shape=[jax.ShapeDtypeStruct((S, D_ATT), BF16), jax.ShapeDtypeStruct((S, D_KV), BF16),
                   jax.ShapeDtypeStruct((S, D_KV), BF16), jax.ShapeDtypeStruct((S, D_RNN), F32),
                   jax.ShapeDtypeStruct((S, D_RNN), F32)],
        compiler_params=_params("parallel"),
    )(x, w_in)


GROUP = N_HEADS // N_KV


def _band_mask(i):
    qi = lax.broadcasted_iota(jnp.int32, (GROUP * QBLK, 2 * QBLK), 0) & (QBLK - 1)
    sj = lax.broadcasted_iota(jnp.int32, (GROUP * QBLK, 2 * QBLK), 1)
    return (sj > qi) & (sj <= qi + QBLK) & ((sj >= QBLK) | (i > 0))


def _stack_heads(x, g):
    return jnp.concatenate([x[:, (g * GROUP + hh) * HEAD_DIM:(g * GROUP + hh + 1) * HEAD_DIM] for hh in range(GROUP)],
                           axis=0)


def _unstack_heads(x4):
    return [x4[hh * QBLK:(hh + 1) * QBLK] for hh in range(GROUP)]


def _sink_column(sink_ref, g):
    head = lax.broadcasted_iota(jnp.int32, (GROUP * QBLK, 1), 0) // QBLK
    col = jnp.full((GROUP * QBLK, 1), sink_ref[g * GROUP], F32)
    for hh in range(1, GROUP):
        col = jnp.where(head == hh, sink_ref[g * GROUP + hh], col)
    return col


def _attn_specs():
    cur = lambda i: (i, 0)
    prev = lambda i: (jnp.maximum(i - 1, 0), 0)
    return [pl.BlockSpec((QBLK, D_KV), cur), pl.BlockSpec((QBLK, D_KV), prev),
            pl.BlockSpec((QBLK, D_KV), cur), pl.BlockSpec((QBLK, D_KV), prev)]


def _attn_fwd(q, k, v, sinks, comm=None):
    S = q.shape[0]

    def body(sink_ref, q_ref, kc_ref, kp_ref, vc_ref, vp_ref, o_ref):
        valid = _band_mask(pl.program_id(0))
        outs = []
        qv = q_ref[...]
        kall = jnp.concatenate([kp_ref[...], kc_ref[...]], axis=0)
        vall = jnp.concatenate([vp_ref[...], vc_ref[...]], axis=0)
        for g in range(N_KV):
            kcat = kall[:, g * HEAD_DIM:(g + 1) * HEAD_DIM]
            vcat = vall[:, g * HEAD_DIM:(g + 1) * HEAD_DIM]
            s = jnp.where(valid, _dot_nt(_stack_heads(qv, g), kcat), -1e30)
            sink = _sink_column(sink_ref, g)
            m = jnp.maximum(jnp.max(s, axis=1, keepdims=True), sink)
            p = jnp.exp(s - m)
            l = jnp.sum(p, axis=1, keepdims=True) + jnp.exp(sink - m)
            outs += _unstack_heads(_dot(p.astype(BF16), vcat) / l)
        o_ref[...] = jnp.concatenate(outs, axis=1).astype(BF16)

    return _pcall(
        body, (sinks, q, k, k, v, v), name="attn_fwd", grid=(S // QBLK,), comm=comm,
        in_specs=[pl.BlockSpec(memory_space=pltpu.SMEM), _rows(QBLK, D_ATT)] + _attn_specs(),
        out_specs=[_rows(QBLK, D_ATT)], out_shape=[jax.ShapeDtypeStruct((S, D_ATT), BF16)])


def _w_rows(w_ref):
    return [w_ref[k:k + 1, :] for k in range(w_ref.shape[0])]


def _conv4(x, halo, w, b):
    y = b + w[3] * x
    for s in (1, 2, 3):
        y = y + w[3 - s] * _shift_down(x, halo, s)
    return y


def _rnn_gates(xc, wa, wx, ba, bx, sp):
    xcb = xc.astype(BF16)
    r = _sigmoid(_dot(xcb, wa) + ba)
    ig = _sigmoid(_dot(xcb, wx) + bx)
    la = -LRU_C * r * sp
    a = jnp.exp(la)
    t = jnp.tanh(la)
    f = jnp.sqrt(-2.0 * t / (1.0 - t))
    return r, ig, a, f


def _rnn_fwd(xr, gr, conv_w, conv_b, wa, wx, ba, bx, lam, comm=None):
    S = xr.shape[0]
    tb = min(256, S)

    def body(xr_ref, gr_ref, cw_ref, cb_ref, wa_ref, wx_ref, ba_ref, bx_ref, lam_ref, rec_ref, h_ref,
             halo_s, hc_s, a_s, b_s):
        @pl.when(pl.program_id(0) == 0)
        def _():
            halo_s[...] = jnp.zeros_like(halo_s)
            hc_s[...] = jnp.zeros_like(hc_s)

        x = xr_ref[...]
        xc = _conv4(x, halo_s[...], _w_rows(cw_ref), cb_ref[...])
        halo_s[...] = x[tb - 8:]
        _, ig, a, f = _rnn_gates(xc, wa_ref[...], wx_ref[...], ba_ref[...], bx_ref[...], _softplus_neg(lam_ref[...]))
        a_s[...] = a
        b_s[...] = f * ig * xc
        row8 = lax.broadcasted_iota(jnp.int32, (8, D_RNN), 0)

        def tile(t, hc):
            o = pl.multiple_of(t * 8, 8)
            at = a_s[pl.ds(o, 8), :]
            bt = b_s[pl.ds(o, 8), :]
            for s in (1, 2, 4):
                keep = row8 >= s
                a_sh = jnp.where(keep, pltpu.roll(at, s, 0), 1.0)
                b_sh = jnp.where(keep, pltpu.roll(bt, s, 0), 0.0)
                bt = at * b_sh + bt
                at = at * a_sh
            ht = at * hc + bt
            b_s[pl.ds(o, 8), :] = ht
            return _row_sum(jnp.where(row8 == 7, ht, 0.0))

        hc_s[0:1, :] = lax.fori_loop(0, tb // 8, tile, hc_s[0:1, :])
        h = b_s[...]
        h_ref[...] = h
        rec_ref[...] = (h * _gelu(gr_ref[...])).astype(BF16)

    vec = _resident((1, D_RNN))
    return _pcall(
        body, (xr, gr, conv_w, conv_b, wa, wx, ba, bx, lam), name="rnn_fwd", grid=(S // tb,), sem="arbitrary", comm=comm,
        in_specs=[_rows(tb, D_RNN), _rows(tb, D_RNN), _resident((4, D_RNN)), vec,
                  _resident((D_RNN, D_RNN)), _resident((D_RNN, D_RNN)), vec, vec, vec],
        out_specs=[_rows(tb, D_RNN), _rows(tb, D_RNN)],
        out_shape=[jax.ShapeDtypeStruct((S, D_RNN), BF16), jax.ShapeDtypeStruct((S, D_RNN), F32)],
        scratch_shapes=[pltpu.VMEM((8, D_RNN), F32), pltpu.VMEM((8, D_RNN), F32),
                        pltpu.VMEM((tb, D_RNN), F32), pltpu.VMEM((tb, D_RNN), F32)])


def _mix_ln1_up(x, att, rec, w_out, ln1_g, ln1_b, w_up_top, w_up_bot, comm=None):
    S = x.shape[0]
    tb = min(256, S)
    nblk, hrows, wblk = w_up_top.shape
    half = nblk // 2

    def body(x_ref, att_ref, rec_ref, wo_ref, g_ref, b_ref, wt_ref, wb_ref, z1_ref, h1_ref, gate_ref, val_ref):
        z1 = ALPHA * x_ref[...] + _dot(att_ref[...], wo_ref[:D_ATT, :]) + _dot(rec_ref[...], wo_ref[D_ATT:, :])
        z1_ref[...] = z1
        xhat, _ = _ln_stats(z1)
        h1b = (xhat * g_ref[...] + b_ref[...]).astype(BF16)
        h1_ref[...] = h1b
        for j in range(nblk):
            up = (_dot(h1b[:, :hrows], wt_ref[j]) + _dot(h1b[:, hrows:], wb_ref[j])).astype(BF16)
            dst = gate_ref if j < half else val_ref
            jj = j % half
            dst[:, jj * wblk:(jj + 1) * wblk] = up

    vec = _resident((1, D_MODEL))
    return _pcall(
        body, (x, att, rec, w_out, ln1_g, ln1_b, w_up_top, w_up_bot), name="mix_ln1_up", grid=(S // tb,), comm=comm,
        in_specs=[_rows(tb, D_MODEL), _rows(tb, D_ATT), _rows(tb, D_RNN), _resident((D_MODEL, D_MODEL)), vec, vec,
                  _resident(w_up_top.shape), _resident(w_up_bot.shape)],
        out_specs=[_rows(tb, D_MODEL), _rows(tb, D_MODEL), _rows(tb, D_FF), _rows(tb, D_FF)],
        out_shape=[jax.ShapeDtypeStruct((S, D_MODEL), F32), jax.ShapeDtypeStruct((S, D_MODEL), BF16),
                   jax.ShapeDtypeStruct((S, D_FF), BF16), jax.ShapeDtypeStruct((S, D_FF), BF16)])


def _tail(gate, val, z1, p, tgt, fcw, fcb, w_down, w_pg, b_pg, w_pp, ln1_g, ln1_b, ln2_g, ln2_b):
    S = z1.shape[0]
    tb = min(256, S)
    t16 = tb // 16

    def body(gc_ref, gp_ref, val_ref, z1_ref, p_ref, t_ref, fcw_ref, fcb_ref, wd_ref, wpg_ref, bpg_ref, wpp_ref,
             g1_ref, b1_ref, g2_ref, b2_ref,
             act_ref, dz2_ref, dpre_ref, dpp_ref, dgc_ref, dval_ref, dh1_ref, acc_ref):
        i = pl.program_id(0)

        @pl.when(i == 0)
        def _():
            acc_ref[...] = jnp.zeros_like(acc_ref)

        gate = gc_ref[...].astype(F32)
        halo = jnp.where(i > 0, gp_ref[...].astype(F32)[8:16], 0.0)
        w = _w_rows(fcw_ref)
        gcv = fcb_ref[...] + w[2] * gate + w[1] * _shift_down(gate, halo, 1) + w[0] * _shift_down(gate, halo, 2)
        gl, dgl = _gelu_and_grad(gcv)
        val = val_ref[...].astype(F32)
        act = (gl * val).astype(BF16)
        act_ref[...] = act
        ffn = _dot(act, wd_ref[...])

        xhat1, _ = _ln_stats(z1_ref[...])
        h1 = xhat1 * g1_ref[...] + b1_ref[...]
        sg = _sigmoid(_dot(h1.astype(BF16), wpg_ref[...]) + bpg_ref[...])
        pp = _dot(p_ref[...].astype(BF16), wpp_ref[...])
        z2 = ALPHA * h1 + ffn + sg * pp
        xhat2, rstd2 = _ln_stats(z2)
        y = xhat2 * g2_ref[...] + b2_ref[...]
        err = y - t_ref[...]
        dy = err * (1.0 / D_MODEL)
        loss = 0.5 * jnp.sum(jnp.sum(err * err, axis=1, keepdims=True), axis=0, keepdims=True) * (1.0 / D_MODEL)
        dz2 = _ln_bwd(dy, xhat2, rstd2, g2_ref[...])
        dz2b = dz2.astype(BF16)
        dz2_ref[...] = dz2b
        dpre = dz2 * pp * sg * (1.0 - sg)
        dpreb = dpre.astype(BF16)
        dpre_ref[...] = dpreb
        dpp_ref[...] = (dz2 * sg).astype(BF16)
        dh1_ref[...] = ALPHA * dz2 + _dot_nt(dpreb, wpg_ref[...])
        dact = _dot_nt(dz2b, wd_ref[...])
        dval_ref[...] = (dact * gl).astype(BF16)
        dgc_ref[...] = (dact * val * dgl).astype(BF16)
        _put_rows(acc_ref, [_row_sum(dy * xhat2), _row_sum(dy), _row_sum(dpre),
                            jnp.broadcast_to(loss, (1, D_MODEL))])

    vec = _resident((1, D_MODEL))
    prev16 = pl.BlockSpec((16, D_FF), lambda i: (jnp.maximum(i * t16 - 1, 0), 0))
    return pl.pallas_call(
        body, name="tail", grid=(S // tb,),
        in_specs=[_rows(tb, D_FF), prev16, _rows(tb, D_FF), _rows(tb, D_MODEL), _rows(tb, PLE_DIM), _rows(tb, D_MODEL),
                  _resident((3, D_FF)), _resident((1, D_FF)), _resident((D_FF, D_MODEL)),
                  _resident((D_MODEL, D_MODEL)), vec, _resident((PLE_DIM, D_MODEL)), vec, vec, vec, vec],
        out_specs=[_rows(tb, D_FF), _rows(tb, D_MODEL), _rows(tb, D_MODEL), _rows(tb, D_MODEL), _rows(tb, D_FF),
                   _rows(tb, D_FF), _rows(tb, D_MODEL), _acc((8, D_MODEL))],
        out_shape=[jax.ShapeDtypeStruct((S, D_FF), BF16), jax.ShapeDtypeStruct((S, D_MODEL), BF16),
                   jax.ShapeDtypeStruct((S, D_MODEL), BF16), jax.ShapeDtypeStruct((S, D_MODEL), BF16),
                   jax.ShapeDtypeStruct((S, D_FF), BF16), jax.ShapeDtypeStruct((S, D_FF), BF16),
                   jax.ShapeDtypeStruct((S, D_MODEL), F32), jax.ShapeDtypeStruct((8, D_MODEL), F32)],
        compiler_params=_params("arbitrary"),
    )(gate, gate, val, z1, p, tgt, fcw, fcb, w_down, w_pg, b_pg, w_pp, ln1_g, ln1_b, ln2_g, ln2_b)


def _weight_grad(a_list, b_list, name, layout, ts=512):
    S = a_list[0].shape[0]
    ms = [a.shape[1] for a in a_list]
    M, nb, Nb = sum(ms), len(b_list), b_list[0].shape[1]
    ts = min(ts, S)
    nk = S // ts
    per_b = N_DEV // nb
    na = len(a_list)

    def body(*refs):
        a_refs, b_refs, o_ref, acc_ref = refs[:na], refs[na:na + nb], refs[na + nb], refs[na + nb + 1]
        j, k = pl.program_id(0), pl.program_id(1)

        @pl.when(k == 0)
        def _():
            acc_ref[...] = jnp.zeros_like(acc_ref)

        for jj in range(nb):
            @pl.when(j == jj)
            def _():
                b = b_refs[jj][...].astype(BF16)
                off = 0
                for a_ref, m in zip(a_refs, ms):
                    acc_ref[off:off + m, :] += _dot_tn(a_ref[...].astype(BF16), b)
                    off += m

        @pl.when(k == nk - 1)
        def _():
            for d in range(per_b):
                if layout == "rows":
                    o_ref[d] = acc_ref[d * (M // N_DEV):(d + 1) * (M // N_DEV), :].astype(BF16)
                elif layout == "cols":
                    o_ref[d] = acc_ref[:, d * (Nb // per_b):(d + 1) * (Nb // per_b)].astype(BF16)
                else:
                    o_ref[d] = acc_ref[:, d * (Nb // per_b):(d + 1) * (Nb // per_b)].T.astype(BF16)

    def b_index(jj):
        return lambda j, k: (jnp.where(j == jj, k, jnp.where(j < jj, 0, nk - 1)), 0)

    if layout == "rows":
        assert nb == 1
        blk = (N_DEV, M // N_DEV, Nb)
    elif layout == "cols":
        blk = (per_b, M, Nb // per_b)
    else:
        blk = (per_b, Nb // per_b, M)
    return pl.pallas_call(
        body, name=name, grid=(nb, nk),
        in_specs=[pl.BlockSpec((ts, m), lambda j, k: (k, 0)) for m in ms]
        + [pl.BlockSpec((ts, Nb), b_index(jj)) for jj in range(nb)],
        out_specs=pl.BlockSpec(blk, lambda j, k: (j, 0, 0)),
        out_shape=jax.ShapeDtypeStruct((N_DEV,) + blk[1:], BF16),
        scratch_shapes=[pltpu.VMEM((M, Nb), F32)],
        compiler_params=_params("arbitrary", "arbitrary"),
    )(*a_list, *b_list)


def _up_bwd(dgc, gate, dval, dh1p, z1, w_up_top, w_up_bot, fcw, w_out, ln1_g, comm=None):
    S = z1.shape[0]
    tb = min(256, S)
    t16 = tb // 16
    n16 = S // 16
    nblk, hrows, wblk = w_up_top.shape
    half = nblk // 2
    nsteps = S // tb

    def body(dgc_ref, dgn_ref, gc_ref, gp_ref, dval_ref, dh1p_ref, z1_ref, wt_ref, wb_ref, fcw_ref, wo_ref, g1_ref,
             dgate_ref, dz1_ref, dz1b_ref, datt_ref, drec_ref, accf_ref, accd_ref):
        i = pl.program_id(0)

        @pl.when(i == 0)
        def _():
            accf_ref[...] = jnp.zeros_like(accf_ref)
            accd_ref[...] = jnp.zeros_like(accd_ref)

        dg = dgc_ref[...].astype(F32)
        nxt = jnp.where(i < nsteps - 1, dgn_ref[...].astype(F32)[0:8], 0.0)
        w = _w_rows(fcw_ref)
        dgate = (w[2] * dg + w[1] * _shift_up(dg, nxt, 1) + w[0] * _shift_up(dg, nxt, 2)).astype(BF16)
        dgate_ref[...] = dgate
        gate = gc_ref[...].astype(F32)
        halo = jnp.where(i > 0, gp_ref[...].astype(F32)[8:16], 0.0)
        _put_rows(accf_ref, [_row_sum(dg * _shift_down(gate, halo, 2)), _row_sum(dg * _shift_down(gate, halo, 1)),
                             _row_sum(dg * gate), _row_sum(dg)])

        dleft, dright = dh1p_ref[:, :hrows], dh1p_ref[:, hrows:]
        for j in range(nblk):
            src = dgate if j < half else dval_ref[...]
            jj = j % half
            dup = src[:, jj * wblk:(jj + 1) * wblk]
            dleft = dleft + _dot_nt(dup, wt_ref[j])
            dright = dright + _dot_nt(dup, wb_ref[j])
        dh1 = jnp.concatenate([dleft, dright], axis=1)
        xhat1, rstd1 = _ln_stats(z1_ref[...])
        dz1 = _ln_bwd(dh1, xhat1, rstd1, g1_ref[...])
        dz1_ref[...] = dz1
        dz1b = dz1.astype(BF16)
        dz1b_ref[...] = dz1b
        dcat = _dot_nt(dz1b, wo_ref[...])
        datt_ref[...] = dcat[:, :D_ATT].astype(BF16)
        drec_ref[...] = dcat[:, D_ATT:]
        _put_rows(accd_ref, [_row_sum(dh1 * xhat1), _row_sum(dh1)])

    prev16 = pl.BlockSpec((16, D_FF), lambda i: (jnp.maximum(i * t16 - 1, 0), 0))
    next16 = pl.BlockSpec((16, D_FF), lambda i: (jnp.minimum((i + 1) * t16, n16 - 1), 0))
    return _pcall(
        body, (dgc, dgc, gate, gate, dval, dh1p, z1, w_up_top, w_up_bot, fcw, w_out, ln1_g), name="up_bwd",
        grid=(nsteps,), sem="arbitrary", comm=comm,
        in_specs=[_rows(tb, D_FF), next16, _rows(tb, D_FF), prev16, _rows(tb, D_FF), _rows(tb, D_MODEL),
                  _rows(tb, D_MODEL), _resident(w_up_top.shape), _resident(w_up_bot.shape), _resident((3, D_FF)),
                  _resident((D_MODEL, D_MODEL)), _resident((1, D_MODEL))],
        out_specs=[_rows(tb, D_FF), _rows(tb, D_MODEL), _rows(tb, D_MODEL), _rows(tb, D_ATT), _rows(tb, D_RNN),
                   _acc((8, D_FF)), _acc((8, D_MODEL))],
        out_shape=[jax.ShapeDtypeStruct((S, D_FF), BF16), jax.ShapeDtypeStruct((S, D_MODEL), F32),
                   jax.ShapeDtypeStruct((S, D_MODEL), BF16), jax.ShapeDtypeStruct((S, D_ATT), BF16),
                   jax.ShapeDtypeStruct((S, D_RNN), F32), jax.ShapeDtypeStruct((8, D_FF), F32),
                   jax.ShapeDtypeStruct((8, D_MODEL), F32)])


def _attn_bwd(q, k, v, do, sinks, comm=None):
    S = q.shape[0]
    grp = N_HEADS // N_KV

    def body(sink_ref, q_ref, kc_ref, kp_ref, vc_ref, vp_ref, do_ref, dq_ref, dkc_ref, dkp_ref, dvc_ref, dvp_ref,
             ds_ref):
        i = pl.program_id(0)

        @pl.when(i == 0)
        def _():
            ds_ref[...] = jnp.zeros_like(ds_ref)

        valid = _band_mask(i)
        row8 = lax.broadcasted_iota(jnp.int32, (8, 128), 0)
        lane8 = lax.broadcasted_iota(jnp.int32, (8, 128), 1)
        dqs, dks, dvs = [], [], []
        dsink = jnp.zeros((8, 128), F32)
        qv = q_ref[...]
        dov = do_ref[...]
        kall = jnp.concatenate([kp_ref[...], kc_ref[...]], axis=0)
        vall = jnp.concatenate([vp_ref[...], vc_ref[...]], axis=0)
        for g in range(N_KV):
            kcat = kall[:, g * HEAD_DIM:(g + 1) * HEAD_DIM]
            vcat = vall[:, g * HEAD_DIM:(g + 1) * HEAD_DIM]
            q4, do4 = _stack_heads(qv, g), _stack_heads(dov, g)
            s = jnp.where(valid, _dot_nt(q4, kcat), -1e30)
            sink = _sink_column(sink_ref, g)
            m = jnp.maximum(jnp.max(s, axis=1, keepdims=True), sink)
            e = jnp.exp(s - m)
            es = jnp.exp(sink - m)
            inv = 1.0 / (jnp.sum(e, axis=1, keepdims=True) + es)
            p = e * inv
            dp = _dot_nt(do4, vcat)
            delta = jnp.sum(p * dp, axis=1, keepdims=True)
            dsc = (p * (dp - delta)).astype(BF16)
            dqs += _unstack_heads(_dot(dsc, kcat) * (HEAD_DIM ** -0.5))
            dks.append(_dot_tn(dsc, q4))
            dvs.append(_dot_tn(p.astype(BF16), do4))
            for hh, part in enumerate(_unstack_heads(-es * inv * delta)):
                here = (row8 == 0) & (lane8 == g * grp + hh)
                dsink = dsink + jnp.where(here, jnp.sum(part, axis=0, keepdims=True), 0.0)
        dq_ref[...] = jnp.concatenate(dqs, axis=1).astype(BF16)
        dk = jnp.concatenate(dks, axis=1)
        dv = jnp.concatenate(dvs, axis=1)
        dkp_ref[...] = dk[:QBLK]
        dkc_ref[...] = dk[QBLK:]
        dvp_ref[...] = dv[:QBLK]
        dvc_ref[...] = dv[QBLK:]
        ds_ref[...] += dsink

    kvs = jax.ShapeDtypeStruct((S, D_KV), F32)
    return _pcall(
        body, (sinks, q, k, k, v, v, do), name="attn_bwd", grid=(S // QBLK,), sem="arbitrary", comm=comm,
        in_specs=[pl.BlockSpec(memory_space=pltpu.SMEM), _rows(QBLK, D_ATT)] + _attn_specs() + [_rows(QBLK, D_ATT)],
        out_specs=[_rows(QBLK, D_ATT), _rows(QBLK, D_KV), _rows(QBLK, D_KV), _rows(QBLK, D_KV), _rows(QBLK, D_KV),
                   _acc((8, 128))],
        out_shape=[jax.ShapeDtypeStruct((S, D_ATT), BF16), kvs, kvs, kvs, kvs, jax.ShapeDtypeStruct((8, 128), F32)])


def _rnn_bwd(xr, gr, h, drec, conv_w, conv_b, wa, wx, ba, bx, lam, comm=None):
    S = xr.shape[0]
    tb = min(256, S)
    t8 = tb // 8
    nsteps = S // tb

    def body(xr_ref, xp_ref, gr_ref, h_ref, hp_ref, drec_ref, cw_ref, cb_ref, wa_ref, wx_ref, ba_ref, bx_ref, lam_ref,
             dxr_ref, dgr_ref, gwa_ref, gwx_ref, acc_ref, carry_s, dxc_halo_s, a_s, d_s, gwa_s, gwx_s):
        i = pl.program_id(0)
        blk = nsteps - 1 - i

        @pl.when(i == 0)
        def _():
            gwa_s[...] = jnp.zeros_like(gwa_s)
            gwx_s[...] = jnp.zeros_like(gwx_s)
            acc_ref[...] = jnp.zeros_like(acc_ref)
            carry_s[...] = jnp.zeros_like(carry_s)
            dxc_halo_s[...] = jnp.zeros_like(dxc_halo_s)

        x = xr_ref[...]
        xhalo = jnp.where(blk > 0, xp_ref[...], 0.0)
        cw = _w_rows(cw_ref)
        xs = [_shift_down(x, xhalo, 3), _shift_down(x, xhalo, 2), _shift_down(x, xhalo, 1), x]
        xc = cb_ref[...] + cw[0] * xs[0] + cw[1] * xs[1] + cw[2] * xs[2] + cw[3] * xs[3]
        sp = _softplus_neg(lam_ref[...])
        r, ig, a, f = _rnn_gates(xc, wa_ref[...], wx_ref[...], ba_ref[...], bx_ref[...], sp)
        hcur = h_ref[...]
        hprev = _shift_down(hcur, jnp.where(blk > 0, hp_ref[...], 0.0), 1)
        gl, dgl = _gelu_and_grad(gr_ref[...])
        drec = drec_ref[...]
        dgr_ref[...] = (drec * hcur * dgl).astype(BF16)
        a_s[...] = a
        d_s[...] = drec * gl
        row8 = lax.broadcasted_iota(jnp.int32, (8, D_RNN), 0)

        def tile(t, c):
            o = pl.multiple_of((t8 - 1 - t) * 8, 8)
            a8 = a_s[pl.ds(o, 8), :]
            dt = d_s[pl.ds(o, 8), :]
            at = jnp.where(row8 == 7, 1.0, pltpu.roll(a8, 7, 0))
            for s in (1, 2, 4):
                keep = row8 < 8 - s
                a_sh = jnp.where(keep, pltpu.roll(at, 8 - s, 0), 1.0)
                d_sh = jnp.where(keep, pltpu.roll(dt, 8 - s, 0), 0.0)
                dt = at * d_sh + dt
                at = at * a_sh
            lt = at * c + dt
            d_s[pl.ds(o, 8), :] = lt
            return _row_sum(jnp.where(row8 == 0, a8 * lt, 0.0))

        carry_s[0:1, :] = lax.fori_loop(0, t8, tile, carry_s[0:1, :])
        lmb = d_s[...]
        a2 = a * a
        dla = lmb * hprev * a - lmb * ig * xc * (a2 / f)
        di = lmb * f * xc
        dr = dla * (-LRU_C) * sp
        dpa = dr * r * (1.0 - r)
        dpx = di * ig * (1.0 - ig)
        dpab = dpa.astype(BF16)
        dpxb = dpx.astype(BF16)
        xcb = xc.astype(BF16)
        gwa_s[...] += _dot_tn(xcb, dpab)
        gwx_s[...] += _dot_tn(xcb, dpxb)

        @pl.when(i == nsteps - 1)
        def _():
            for dense, out in ((gwa_s[...], gwa_ref), (gwx_s[...], gwx_ref)):
                for b in range(RNN_BLOCKS):
                    rows = slice(b * HEAD_DIM, (b + 1) * HEAD_DIM)
                    out[rows, :] = dense[rows, b * HEAD_DIM:(b + 1) * HEAD_DIM]

        dxc = lmb * f * ig + _dot_nt(dpab, wa_ref[...]) + _dot_nt(dpxb, wx_ref[...])
        nxt = dxc_halo_s[...]
        dxr = cw[3] * dxc
        for s in (1, 2, 3):
            dxr = dxr + cw[3 - s] * _shift_up(dxc, nxt, s)
        dxr_ref[...] = dxr.astype(BF16)
        dxc_halo_s[...] = dxc[:8]
        dlam = _row_sum(dla * (-LRU_C) * r) * (-1.0 / (1.0 + jnp.exp(lam_ref[...])))
        _put_rows(acc_ref, [_row_sum(dxc * xs[0]), _row_sum(dxc * xs[1]), _row_sum(dxc * xs[2]), _row_sum(dxc * xs[3]),
                            _row_sum(dxc), _row_sum(dpa), _row_sum(dpx), dlam])

    rev = lambda i: (nsteps - 1 - i, 0)
    prev8 = lambda i: (jnp.maximum((nsteps - 1 - i) * t8 - 1, 0), 0)
    blkspec = pl.BlockSpec((tb, D_RNN), rev)
    halo8 = pl.BlockSpec((8, D_RNN), prev8)
    vec = _resident((1, D_RNN))
    return _pcall(
        body, (xr, xr, gr, h, h, drec, conv_w, conv_b, wa, wx, ba, bx, lam), name="rnn_bwd", grid=(nsteps,),
        sem="arbitrary", comm=comm,
        in_specs=[blkspec, halo8, blkspec, blkspec, halo8, blkspec, _resident((4, D_RNN)), vec,
                  _resident((D_RNN, D_RNN)), _resident((D_RNN, D_RNN)), vec, vec, vec],
        out_specs=[blkspec, blkspec, _acc((D_RNN, HEAD_DIM)), _acc((D_RNN, HEAD_DIM)), _acc((8, D_RNN))],
        out_shape=[jax.ShapeDtypeStruct((S, D_RNN), BF16), jax.ShapeDtypeStruct((S, D_RNN), BF16),
                   jax.ShapeDtypeStruct((D_RNN, HEAD_DIM), F32), jax.ShapeDtypeStruct((D_RNN, HEAD_DIM), F32),
                   jax.ShapeDtypeStruct((8, D_RNN), F32)],
        scratch_shapes=[pltpu.VMEM((8, D_RNN), F32), pltpu.VMEM((8, D_RNN), F32),
                        pltpu.VMEM((tb, D_RNN), F32), pltpu.VMEM((tb, D_RNN), F32),
                        pltpu.VMEM((D_RNN, D_RNN), F32), pltpu.VMEM((D_RNN, D_RNN), F32)])


def _in_bwd(dq, dkc, dkp, dvc, dvp, dxr, dgr, dz1, w_in, comm=None):
    S = dz1.shape[0]
    tb = min(256, S)
    nsteps = S // tb
    nq = S // QBLK
    r = tb // QBLK

    def body(dq_ref, dkc_ref, dkp_ref, dkn_ref, dvc_ref, dvp_ref, dvn_ref, dxr_ref, dgr_ref, dz1_ref, w_ref,
             du_ref, dx_ref):
        i = pl.program_id(0)
        last = i == nsteps - 1

        def shifted(prev_ref, next_ref):
            nxt = jnp.where(last, 0.0, next_ref[...])
            return jnp.concatenate([prev_ref[QBLK:], nxt], axis=0) if r > 1 else nxt

        dk = (dkc_ref[...] + shifted(dkp_ref, dkn_ref)).astype(BF16)
        dv = (dvc_ref[...] + shifted(dvp_ref, dvn_ref)).astype(BF16)
        du = jnp.concatenate([dq_ref[...], dk, dv, dxr_ref[...], dgr_ref[...]], axis=1)
        du_ref[...] = du
        dx_ref[...] = ALPHA * dz1_ref[...] + _dot_nt(du, w_ref[...])

    nextq = pl.BlockSpec((QBLK, D_KV), lambda i: (jnp.minimum((i + 1) * r, nq - 1), 0))
    return _pcall(
        body, (dq, dkc, dkp, dkp, dvc, dvp, dvp, dxr, dgr, dz1, w_in), name="in_bwd", grid=(nsteps,), comm=comm,
        in_specs=[_rows(tb, D_ATT), _rows(tb, D_KV), _rows(tb, D_KV), nextq, _rows(tb, D_KV), _rows(tb, D_KV), nextq,
                  _rows(tb, D_RNN), _rows(tb, D_RNN), _rows(tb, D_MODEL), _resident((D_MODEL, D_IN))],
        out_specs=[_rows(tb, D_IN), _rows(tb, D_MODEL)],
        out_shape=[jax.ShapeDtypeStruct((S, D_IN), BF16), jax.ShapeDtypeStruct((S, D_MODEL), F32)])


def _block_diag(w):
    eye = jnp.eye(RNN_BLOCKS, dtype=w.dtype)
    return (w[:, :, None, :] * eye[:, None, :, None]).reshape(D_RNN, D_RNN).astype(BF16)


def _adamw(w, g, m, v):
    m = ADAM_B1 * m + (1.0 - ADAM_B1) * g
    v = ADAM_B2 * v + (1.0 - ADAM_B2) * (g * g)
    m_hat = m / (1.0 - ADAM_B1 ** ADAM_STEP)
    v_hat = v / (1.0 - ADAM_B2 ** ADAM_STEP)
    delta = -ADAM_LR * (m_hat / (jnp.sqrt(v_hat) + ADAM_EPS) + ADAM_WD * w)
    return delta, m, v


def _sum_adamw(parts, w, m, v, name):
    R, C = w.shape
    rb = 128
    assert R % rb == 0

    def body(p_ref, w_ref, m_ref, v_ref, g_out, d_out, m_out, v_out):
        g = p_ref[0].astype(F32)
        for d in range(1, N_DEV):
            g = g + p_ref[d].astype(F32)
        delta, mn, vn = _adamw(w_ref[...], g, m_ref[...], v_ref[...])
        g_out[...] = g
        d_out[...] = delta
        m_out[...] = mn
        v_out[...] = vn

    blk = _rows(rb, C)
    out = jax.ShapeDtypeStruct((R, C), F32)
    return pl.pallas_call(
        body, name=name, grid=(R // rb,),
        in_specs=[pl.BlockSpec((N_DEV, rb, C), lambda i: (0, i, 0)), blk, blk, blk],
        out_specs=[blk, blk, blk, blk], out_shape=[out, out, out, out],
        compiler_params=_params("parallel"),
    )(parts, w, m, v)


_SMALL = [("attn_sinks", "s", 0, 1, None), ("rnn_conv_w", "r", 0, 4, "cols"), ("rnn_conv_b", "r", 4, 1, None),
          ("gate_a_w", "a", 0, D_RNN, None), ("gate_a_b", "r", 5, 1, None), ("gate_x_w", "x", 0, D_RNN, None),
          ("gate_x_b", "r", 6, 1, None), ("lru_lambda", "r", 7, 1, None), ("ln1_g", "d", 0, 1, None),
          ("ln1_b", "d", 1, 1, None), ("ffn_conv_w", "f", 0, 3, "cols"), ("ffn_conv_b", "f", 3, 1, None),
          ("ple_gate_b", "t", 2, 1, None), ("ln2_g", "t", 0, 1, None), ("ln2_b", "t", 1, 1, None)]
_LOSS_ROW = 3


def _small_update(gathered, params):
    keys = "tfdsrax"
    flat = [arr for triple in params for arr in triple]
    n_par = len(_SMALL)

    def body(*refs):
        g_refs = dict(zip(keys, refs[:7]))
        p_refs = refs[7:7 + 3 * n_par]
        loss_ref = refs[7 + 3 * n_par]
        o_refs = refs[8 + 3 * n_par:8 + 7 * n_par]
        tot = dict(zip(keys, refs[8 + 7 * n_par:8 + 7 * n_par + 7]))
        tmp_r, tmp_f = refs[8 + 7 * n_par + 7:]
        me = _dev_index(*_place())
        for key in keys:
            s = g_refs[key][0]
            for d in range(1, N_DEV):
                s = s + g_refs[key][d]
            tot[key][...] = s
        loss_ref[...] = tot["t"][_LOSS_ROW:_LOSS_ROW + 1, 0:128]
        for i, (name, key, row, rows, how) in enumerate(_SMALL):
            w_ref, m_ref, v_ref = p_refs[3 * i:3 * i + 3]
            g_out, d_out, m_out, v_out = o_refs[4 * i:4 * i + 4]
            if how == "cols":
                full = tot[key][...]
                width = full.shape[1] // N_DEV
                mine = full[:, :width]
                for d in range(1, N_DEV):
                    mine = jnp.where(me == d, full[:, d * width:(d + 1) * width], mine)
                tmp = tmp_r if key == "r" else tmp_f
                tmp[...] = mine
                g = tmp[row:row + rows, :]
            else:
                g = tot[key][row:row + rows, :]
                g = g[:, :w_ref.shape[1]]
            delta, mn, vn = _adamw(w_ref[...], g, m_ref[...], v_ref[...])
            g_out[...] = g
            d_out[...] = delta
            m_out[...] = mn
            v_out[...] = vn

    outs = [jax.ShapeDtypeStruct((1, 128), F32)]
    for w, _, _ in params:
        outs += [jax.ShapeDtypeStruct(w.shape, F32)] * 4
    scratch = [pltpu.VMEM(g.shape[1:], F32) for g in gathered]
    scratch += [pltpu.VMEM((8, D_RNN // N_DEV), F32), pltpu.VMEM((8, D_FF // N_DEV), F32)]
    res = pl.pallas_call(body, name="small_update", out_shape=outs, scratch_shapes=scratch)(*gathered, *flat)
    return res[0], [res[1 + 4 * i:5 + 4 * i] for i in range(n_par)]


def kernel(x, p, w_in, attn_sinks, rnn_conv_w, rnn_conv_b, gate_a_w, gate_a_b, gate_x_w, gate_x_b, lru_lambda, w_out, ln1_g, ln1_b, w_ffn_up, ffn_conv_w, ffn_conv_b, w_ffn_down, ple_gate_w, ple_gate_b, ple_proj, ln2_g, ln2_b, loss_target, m_w_in, m_attn_sinks, m_rnn_conv_w, m_rnn_conv_b, m_gate_a_w, m_gate_a_b, m_gate_x_w, m_gate_x_b, m_lru_lambda, m_w_out, m_ln1_g, m_ln1_b, m_w_ffn_up, m_ffn_conv_w, m_ffn_conv_b, m_w_ffn_down, m_ple_gate_w, m_ple_gate_b, m_ple_proj, m_ln2_g, m_ln2_b, v_w_in, v_attn_sinks, v_rnn_conv_w, v_rnn_conv_b, v_gate_a_w, v_gate_a_b, v_gate_x_w, v_gate_x_b, v_lru_lambda, v_w_out, v_ln1_g, v_ln1_b, v_w_ffn_up, v_ffn_conv_w, v_ffn_conv_b, v_w_ffn_down, v_ple_gate_w, v_ple_gate_b, v_ple_proj, v_ln2_g, v_ln2_b):
    from_col_blocks = lambda g: g.transpose(1, 0, 2).reshape(g.shape[1], N_DEV * g.shape[2])

    xs, ps, tgt, sinks = x[0], p[0, 0], loss_target[0], attn_sinks[0]
    wa, wx = _block_diag(gate_a_w[0]), _block_diag(gate_x_w[0])

    conv_cols = jnp.concatenate([rnn_conv_w[0].reshape(1, -1), ffn_conv_w[0].reshape(1, -1)], axis=1)
    n_rc, n_fc = 4 * D_RNN // N_DEV, 3 * D_FF // N_DEV
    g_in, g_conv = _comm_call(_Gather([w_in[0].astype(BF16), jnp.broadcast_to(conv_cols, (8, n_rc + n_fc))]),
                              "gather_w_in")
    w_in_full = from_col_blocks(g_in)
    rcw = from_col_blocks(g_conv[:, 0, :n_rc].reshape(N_DEV, 4, D_RNN // N_DEV))
    fcw = from_col_blocks(g_conv[:, 0, n_rc:].reshape(N_DEV, 3, D_FF // N_DEV))
    w_up_shard = w_ffn_up[0].astype(BF16)
    half_rows = D_MODEL // 2

    q, k, v, xr, gr = _in_proj(xs, w_in_full)
    (att,), (g_out, g_up_top) = _attn_fwd(q, k, v, sinks,
                                          comm=_Gather([w_out[0].astype(BF16), w_up_shard[:half_rows]]))
    (rec, h), (g_up_bot,) = _rnn_fwd(xr, gr, rcw, rnn_conv_b, wa, wx, gate_a_b, gate_x_b, lru_lambda,
                                     comm=_Gather([w_up_shard[half_rows:]]))
    w_out_full = g_out.reshape(D_MODEL, D_MODEL)
    (z1, h1b, gate, val), (g_down, g_pg, g_pp) = _mix_ln1_up(
        xs, att, rec, w_out_full, ln1_g, ln1_b, g_up_top, g_up_bot,
        comm=_Gather([w_ffn_down[0].astype(BF16), ple_gate_w[0].astype(BF16), ple_proj[0].astype(BF16)]))
    act, dz2b, dpreb, dppb, dgc, dval, dh1p, acc_t = _tail(
        gate, val, z1, ps, tgt, fcw, ffn_conv_b, g_down.reshape(D_FF, D_MODEL), g_pg.reshape(D_MODEL, D_MODEL), ple_gate_b,
        from_col_blocks(g_pp), ln1_g, ln1_b, ln2_g, ln2_b)

    gd_down = _weight_grad([dz2b], [act], "down_grad", "rows_t")
    gd_pg = _weight_grad([h1b], [dpreb], "pg_grad", "rows", ts=1024)
    gd_pp = _weight_grad([ps], [dppb], "pp_grad", "cols", ts=1024)
    (dgate, dz1, dz1b, datt, drec, acc_f, acc_d), (r_down, r_pg, r_pp) = _up_bwd(
        dgc, gate, dval, dh1p, z1, g_up_top, g_up_bot, fcw, w_out_full, ln1_g, comm=_Exchange([gd_down, gd_pg, gd_pp]))
    gd_up = _weight_grad([h1b], [dgate, dval], "up_grad", "cols")
    gd_out = _weight_grad([att, rec], [dz1b], "out_grad", "rows", ts=1024)
    (dq, dkc, dkp, dvc, dvp, acc_s), (r_up,) = _attn_bwd(q, k, v, datt, sinks, comm=_Exchange([gd_up]))
    (dxr, dgr, g_wa, g_wx, acc_r), (r_out,) = _rnn_bwd(xr, gr, h, drec, rcw, rnn_conv_b, wa, wx, gate_a_b, gate_x_b,
                                                       lru_lambda, comm=_Exchange([gd_out]))
    (du, dx), small_parts = _in_bwd(dq, dkc, dkp, dvc, dvp, dxr, dgr, dz1, w_in_full,
                                    comm=_Gather([acc_t, acc_f, acc_d, acc_s, acc_r, g_wa, g_wx]))
    gd_in = _weight_grad([xs], [du], "in_grad", "cols", ts=1024)
    (r_in,) = _comm_call(_Exchange([gd_in]), "exchange_w_in")

    outs = {}
    for name, parts, w, m, v in [("w_in", r_in, w_in, m_w_in, v_w_in), ("w_out", r_out, w_out, m_w_out, v_w_out),
                                 ("w_ffn_up", r_up, w_ffn_up, m_w_ffn_up, v_w_ffn_up),
                                 ("w_ffn_down", r_down, w_ffn_down, m_w_ffn_down, v_w_ffn_down),
                                 ("ple_gate_w", r_pg, ple_gate_w, m_ple_gate_w, v_ple_gate_w),
                                 ("ple_proj", r_pp, ple_proj, m_ple_proj, v_ple_proj)]:
        res = _sum_adamw(parts, w[0], m[0], v[0], "adamw_" + name)
        outs[name] = [r[None] for r in res]

    given = dict(attn_sinks=(attn_sinks, m_attn_sinks, v_attn_sinks), rnn_conv_w=(rnn_conv_w, m_rnn_conv_w, v_rnn_conv_w),
                 rnn_conv_b=(rnn_conv_b, m_rnn_conv_b, v_rnn_conv_b), gate_a_w=(gate_a_w, m_gate_a_w, v_gate_a_w),
                 gate_a_b=(gate_a_b, m_gate_a_b, v_gate_a_b), gate_x_w=(gate_x_w, m_gate_x_w, v_gate_x_w),
                 gate_x_b=(gate_x_b, m_gate_x_b, v_gate_x_b), lru_lambda=(lru_lambda, m_lru_lambda, v_lru_lambda),
                 ln1_g=(ln1_g, m_ln1_g, v_ln1_g), ln1_b=(ln1_b, m_ln1_b, v_ln1_b),
                 ffn_conv_w=(ffn_conv_w, m_ffn_conv_w, v_ffn_conv_w), ffn_conv_b=(ffn_conv_b, m_ffn_conv_b, v_ffn_conv_b),
                 ple_gate_b=(ple_gate_b, m_ple_gate_b, v_ple_gate_b), ln2_g=(ln2_g, m_ln2_g, v_ln2_g),
                 ln2_b=(ln2_b, m_ln2_b, v_ln2_b))
    as_2d = lambda a: a.reshape(-1, a.shape[-1])
    loss_row, small_res = _small_update(small_parts, [tuple(as_2d(a) for a in given[n]) for n, *_ in _SMALL])
    loss = loss_row[0, 0]
    for (n, *_), res in zip(_SMALL, small_res):
        outs[n] = [r.reshape(given[n][0].shape) for r in res]

    order = ["w_in", "attn_sinks", "rnn_conv_w", "rnn_conv_b", "gate_a_w", "gate_a_b", "gate_x_w", "gate_x_b",
             "lru_lambda", "w_out", "ln1_g", "ln1_b", "w_ffn_up", "ffn_conv_w", "ffn_conv_b", "w_ffn_down",
             "ple_gate_w", "ple_gate_b", "ple_proj", "ln2_g", "ln2_b"]
    return (loss, dx[None], *[outs[n][0] for n in order], *[outs[n][1] for n in order],
            *[outs[n][2] for n in order], *[outs[n][3] for n in order])
```

```python
import jax
import jax.numpy as jnp
from jax import lax
from jax.experimental import pallas as pl
from jax.experimental.pallas import tpu as pltpu

F32 = jnp.float32
BF16 = jnp.bfloat16

D_MODEL = 1024
D_ATT = 512
D_KV = 128
HEAD_DIM = 64
N_HEADS = 8
N_KV = 2
D_RNN = 512
RNN_BLOCKS = 8
D_IN = 1792
D_FF = 3072
PLE_DIM = 256
QBLK = 128
N_DEV = 8
ALPHA = float(2 ** 0.25)
LN_EPS = 1e-5
LRU_C = 8.0
ADAM_LR, ADAM_B1, ADAM_B2, ADAM_EPS, ADAM_WD, ADAM_STEP = 0.001, 0.9, 0.999, 1e-08, 0.01, 10

V7X_VMEM_LIMIT = 56 * 1024 * 1024
MESH = pl.DeviceIdType.MESH


def _params(*sem, vmem=V7X_VMEM_LIMIT):
    return pltpu.CompilerParams(dimension_semantics=sem or None, vmem_limit_bytes=vmem)


def _resident(shape):
    return pl.BlockSpec(shape, lambda *_: (0,) * len(shape), pipeline_mode=pl.Buffered(1))


def _rows(tb, cols):
    return pl.BlockSpec((tb, cols), lambda i: (i, 0))


def _acc(shape):
    return pl.BlockSpec(shape, lambda *_: (0,) * len(shape))


def _dot(a, b):
    return jnp.dot(a, b, preferred_element_type=F32)


def _dot_nt(a, b):
    return lax.dot_general(a, b, (((1,), (1,)), ((), ())), preferred_element_type=F32)


def _dot_tn(a, b):
    return lax.dot_general(a, b, (((0,), (0,)), ((), ())), preferred_element_type=F32)


def _sigmoid(x):
    return 1.0 / (1.0 + jnp.exp(-x))


_GELU_C = 0.7978845608028654
_GELU_K = 0.044715


def _gelu_and_grad(x):
    u = x * x
    t = jnp.tanh(x * (_GELU_C + (_GELU_C * _GELU_K) * u))
    hp = 0.5 + 0.5 * t
    dg = hp + x * (0.5 - 0.5 * (t * t)) * (_GELU_C + (3.0 * _GELU_C * _GELU_K) * u)
    return x * hp, dg


def _gelu(x):
    return 0.5 * x * (1.0 + jnp.tanh(_GELU_C * (x + _GELU_K * x * x * x)))


def _ln_stats(z):
    mu = jnp.mean(z, axis=-1, keepdims=True)
    zc = z - mu
    var = jnp.mean(zc * zc, axis=-1, keepdims=True)
    rstd = lax.rsqrt(var + LN_EPS)
    return zc * rstd, rstd


def _ln_bwd(dy, xhat, rstd, g):
    dxh = dy * g
    m1 = jnp.mean(dxh, axis=-1, keepdims=True)
    m2 = jnp.mean(dxh * xhat, axis=-1, keepdims=True)
    return rstd * (dxh - m1 - xhat * m2)


def _softplus_neg(lam):
    u = jnp.exp(-jnp.abs(lam))
    w = 1.0 + u
    d = w - 1.0
    log1p_u = jnp.where(d == 0.0, u, jnp.log(w) * (u / jnp.where(d == 0.0, 1.0, d)))
    return jnp.maximum(-lam, 0.0) + log1p_u


def _shift_down(x, halo, s):
    xs = pltpu.roll(x, s, 0)
    hs = pltpu.roll(halo, s, 0)
    row8 = lax.broadcasted_iota(jnp.int32, hs.shape, 0)
    first = jnp.where(row8 < s, hs, xs[:8])
    return jnp.concatenate([first, xs[8:]], axis=0)


def _shift_up(x, halo, s):
    n = x.shape[0]
    xs = pltpu.roll(x, n - s, 0)
    hs = pltpu.roll(halo, 8 - s, 0)
    row8 = lax.broadcasted_iota(jnp.int32, hs.shape, 0)
    last = jnp.where(row8 >= 8 - s, hs, xs[n - 8:])
    return jnp.concatenate([xs[:n - 8], last], axis=0)


def _row_sum(x):
    return jnp.sum(x, axis=0, keepdims=True)


def _put_rows(acc_ref, rows):
    row8 = lax.broadcasted_iota(jnp.int32, acc_ref.shape, 0)
    upd = jnp.zeros(acc_ref.shape, F32)
    for r, vec in enumerate(rows):
        upd = jnp.where(row8 == r, vec, upd)
    acc_ref[...] += upd


def _place():
    return lax.axis_index("x"), lax.axis_index("y"), lax.axis_index("c")


def _dev_index(px, py, pc):
    return 4 * px + 2 * py + pc


_ANY = pl.BlockSpec(memory_space=pl.ANY)


class _Gather:
    def __init__(self, arrays):
        self.arrays = list(arrays)
        self.n = len(self.arrays)

    def out_shape(self):
        return [jax.ShapeDtypeStruct((N_DEV,) + s.shape, s.dtype) for s in self.arrays]

    def scratch(self):
        return [pltpu.SemaphoreType.DMA((self.n, 7)), pltpu.SemaphoreType.DMA((self.n, 7)),
                pltpu.SemaphoreType.DMA((self.n,))]

    def _parts(self, ins, outs, sems):
        send_sems, recv_sems, local_sems = sems
        x, y, c = _place()
        me, sibling = (x, y, c), (x, y, 1 - c)
        chips = [(1 - x, y), (x, 1 - y), (1 - x, 1 - y)]

        def copy(a, k, block, to, src=None):
            rows = outs[a].at[_dev_index(*block)]
            return pltpu.make_async_remote_copy(
                src_ref=rows if src is None else src, dst_ref=rows, send_sem=send_sems.at[a, k],
                recv_sem=recv_sems.at[a, k], device_id=to, device_id_type=MESH)

        rng = range(self.n)
        mine = [pltpu.make_async_copy(ins[a], outs[a].at[_dev_index(*me)], local_sems.at[a]) for a in rng]
        first = [copy(a, 0, me, sibling, src=ins[a]) for a in rng]
        first += [copy(a, 1 + j, me, (*chip, c), src=ins[a]) for j, chip in enumerate(chips) for a in rng]
        landed = [copy(a, 1 + j, (*chip, c), me) for j, chip in enumerate(chips) for a in rng]
        passed = [copy(a, 4 + j, (*chip, c), sibling) for j, chip in enumerate(chips) for a in rng]
        from_sibling = [copy(a, 0, sibling, me) for a in rng]
        from_sibling += [copy(a, 4 + j, (*chip, 1 - c), me) for j, chip in enumerate(chips) for a in rng]
        return mine, first, landed, passed, from_sibling

    def start(self, ins, outs, sems):
        mine, first, _, _, _ = self._parts(ins, outs, sems)
        for cp in mine + first:
            cp.start()

    def forward(self, ins, outs, sems):
        _, _, landed, passed, _ = self._parts(ins, outs, sems)
        for got, fwd in zip(landed, passed):
            got.wait_recv()
            fwd.start()

    def finish(self, ins, outs, sems):
        mine, first, _, passed, from_sibling = self._parts(ins, outs, sems)
        for cp in from_sibling:
            cp.wait_recv()
        for cp in first + passed:
            cp.wait_send()
        for cp in mine:
            cp.wait()

    def before(self, ins, outs, sems, step, nsteps):
        pl.when(step == 0)(lambda: self.start(ins, outs, sems))
        pl.when(step == (3 * nsteps) // 5)(lambda: self.forward(ins, outs, sems))

    def after(self, ins, outs, sems, step, nsteps):
        pl.when(step == nsteps - 1)(lambda: self.finish(ins, outs, sems))


class _Exchange:
    def __init__(self, arrays):
        self.arrays = list(arrays)
        self.n = len(self.arrays)

    def out_shape(self):
        return [jax.ShapeDtypeStruct(b.shape, b.dtype) for b in self.arrays]

    def scratch(self):
        return [pltpu.SemaphoreType.DMA((self.n, 7)), pltpu.SemaphoreType.DMA((self.n, 7)),
                pltpu.SemaphoreType.DMA((self.n,))]

    def _parts(self, ins, outs, sems):
        send_sems, recv_sems, local_sems = sems
        x, y, c = _place()
        me = _dev_index(x, y, c)
        peers = [(x ^ (k >> 2), y ^ ((k >> 1) & 1), c ^ (k & 1)) for k in range(1, N_DEV)]
        rng = range(self.n)
        mine = [pltpu.make_async_copy(ins[a].at[me], outs[a].at[me], local_sems.at[a]) for a in rng]
        sent = [pltpu.make_async_remote_copy(
            src_ref=ins[a].at[_dev_index(*to)], dst_ref=outs[a].at[me], send_sem=send_sems.at[a, k],
            recv_sem=recv_sems.at[a, k], device_id=to, device_id_type=MESH) for k, to in enumerate(peers) for a in rng]
        arrivals = [pltpu.make_async_remote_copy(
            src_ref=ins[a].at[me], dst_ref=outs[a].at[_dev_index(*frm)], send_sem=send_sems.at[a, k],
            recv_sem=recv_sems.at[a, k], device_id=frm, device_id_type=MESH) for k, frm in enumerate(peers) for a in rng]
        return mine, sent, arrivals

    def start(self, ins, outs, sems):
        mine, sent, _ = self._parts(ins, outs, sems)
        for cp in mine + sent:
            cp.start()

    def finish(self, ins, outs, sems):
        mine, sent, arrivals = self._parts(ins, outs, sems)
        for cp in arrivals:
            cp.wait_recv()
        for cp in sent:
            cp.wait_send()
        for cp in mine:
            cp.wait()

    def before(self, ins, outs, sems, step, nsteps):
        pl.when(step == 0)(lambda: self.start(ins, outs, sems))

    def after(self, ins, outs, sems, step, nsteps):
        pl.when(step == nsteps - 1)(lambda: self.finish(ins, outs, sems))


def _comm_call(comm, name):
    n = comm.n

    def body(*refs):
        ins, outs, sems = refs[:n], refs[n:2 * n], refs[2 * n:]
        comm.start(ins, outs, sems)
        if isinstance(comm, _Gather):
            comm.forward(ins, outs, sems)
        comm.finish(ins, outs, sems)

    return pl.pallas_call(body, name=name, in_specs=[_ANY] * n, out_specs=[_ANY] * n, out_shape=comm.out_shape(),
                          scratch_shapes=comm.scratch())(*comm.arrays)


def _pcall(body, args, *, name, grid, in_specs, out_specs, out_shape, scratch_shapes=(), sem="parallel", comm=None,
           step_axis=0):
    sem = (sem,) * len(grid) if isinstance(sem, str) else sem
    if comm is None:
        res = pl.pallas_call(body, name=name, grid=grid, in_specs=in_specs, out_specs=out_specs, out_shape=out_shape,
                             scratch_shapes=list(scratch_shapes), compiler_params=_params(*sem))(*args)
        return res, []
    n_in, n_out, n_scr, n = len(in_specs), len(out_specs), len(scratch_shapes), comm.n
    nsteps = grid[step_axis]
    assert all(g == 1 for ax, g in enumerate(grid) if ax != step_axis)

    def hosted(*refs):
        ins, cin = refs[:n_in], refs[n_in:n_in + n]
        o0 = n_in + n
        outs, cout = refs[o0:o0 + n_out], refs[o0 + n_out:o0 + n_out + n]
        s0 = o0 + n_out + n
        scr, sems = refs[s0:s0 + n_scr], refs[s0 + n_scr:]
        step = pl.program_id(step_axis)
        comm.before(cin, cout, sems, step, nsteps)
        body(*ins, *outs, *scr)
        comm.after(cin, cout, sems, step, nsteps)

    res = pl.pallas_call(
        hosted, name=name, grid=grid, in_specs=list(in_specs) + [_ANY] * n, out_specs=list(out_specs) + [_ANY] * n,
        out_shape=list(out_shape) + comm.out_shape(), scratch_shapes=list(scratch_shapes) + comm.scratch(),
        compiler_params=_params(*(("arbitrary",) * len(grid))))(*args, *comm.arrays)
    return res[:n_out], res[n_out:]


def _in_proj(x, w_in):
    S = x.shape[0]
    tb = min(512, S)

    def body(x_ref, w_ref, q_ref, k_ref, v_ref, xr_ref, gr_ref):
        u = _dot(x_ref[...].astype(BF16), w_ref[...])
        q_ref[...] = (u[:, :D_ATT] * (HEAD_DIM ** -0.5)).astype(BF16)
        k_ref[...] = u[:, D_ATT:D_ATT + D_KV].astype(BF16)
        v_ref[...] = u[:, D_ATT + D_KV:D_ATT + 2 * D_KV].astype(BF16)
        xr_ref[...] = u[:, D_ATT + 2 * D_KV:D_ATT + 2 * D_KV + D_RNN]
        gr_ref[...] = u[:, D_ATT + 2 * D_KV + D_RNN:]

    return pl.pallas_call(
        body, name="in_proj", grid=(S // tb,),
        in_specs=[_rows(tb, D_MODEL), _resident((D_MODEL, D_IN))],
        out_specs=[_rows(tb, D_ATT), _rows(tb, D_KV), _rows(tb, D_KV), _rows(tb, D_RNN), _rows(tb, D_RNN)],
        out_shape=[jax.ShapeDtypeStruct((S, D_ATT), BF16), jax.ShapeDtypeStruct((S, D_KV), BF16),
                   jax.ShapeDtypeStruct((S, D_KV), BF16), jax.ShapeDtypeStruct((S, D_RNN), F32),
                   jax.ShapeDtypeStruct((S, D_RNN), F32)],
        compiler_params=_params("parallel"),
    )(x, w_in)


GROUP = N_HEADS // N_KV


def _band_mask(i):
    qi = lax.broadcasted_iota(jnp.int32, (GROUP * QBLK, 2 * QBLK), 0) & (QBLK - 1)
    sj = lax.broadcasted_iota(jnp.int32, (GROUP * QBLK, 2 * QBLK), 1)
    return (sj > qi) & (sj <= qi + QBLK) & ((sj >= QBLK) | (i > 0))


def _stack_heads(x, g):
    return jnp.concatenate([x[:, (g * GROUP + hh) * HEAD_DIM:(g * GROUP + hh + 1) * HEAD_DIM] for hh in range(GROUP)],
                           axis=0)


def _unstack_heads(x4):
    return [x4[hh * QBLK:(hh + 1) * QBLK] for hh in range(GROUP)]


def _sink_column(sink_ref, g):
    head = lax.broadcasted_iota(jnp.int32, (GROUP * QBLK, 1), 0) // QBLK
    col = jnp.full((GROUP * QBLK, 1), sink_ref[g * GROUP], F32)
    for hh in range(1, GROUP):
        col = jnp.where(head == hh, sink_ref[g * GROUP + hh], col)
    return col


def _attn_specs():
    cur = lambda i: (i, 0)
    prev = lambda i: (jnp.maximum(i - 1, 0), 0)
    return [pl.BlockSpec((QBLK, D_KV), cur), pl.BlockSpec((QBLK, D_KV), prev),
            pl.BlockSpec((QBLK, D_KV), cur), pl.BlockSpec((QBLK, D_KV), prev)]


def _attn_fwd(q, k, v, sinks, comm=None):
    S = q.shape[0]

    def body(sink_ref, q_ref, kc_ref, kp_ref, vc_ref, vp_ref, o_ref):
        valid = _band_mask(pl.program_id(0))
        outs = []
        qv = q_ref[...]
        kall = jnp.concatenate([kp_ref[...], kc_ref[...]], axis=0)
        vall = jnp.concatenate([vp_ref[...], vc_ref[...]], axis=0)
        for g in range(N_KV):
            kcat = kall[:, g * HEAD_DIM:(g + 1) * HEAD_DIM]
            vcat = vall[:, g * HEAD_DIM:(g + 1) * HEAD_DIM]
            s = jnp.where(valid, _dot_nt(_stack_heads(qv, g), kcat), -1e30)
            sink = _sink_column(sink_ref, g)
            m = jnp.maximum(jnp.max(s, axis=1, keepdims=True), sink)
            p = jnp.exp(s - m)
            l = jnp.sum(p, axis=1, keepdims=True) + jnp.exp(sink - m)
            outs += _unstack_heads(_dot(p.astype(BF16), vcat) / l)
        o_ref[...] = jnp.concatenate(outs, axis=1).astype(BF16)

    return _pcall(
        body, (sinks, q, k, k, v, v), name="attn_fwd", grid=(S // QBLK,), comm=comm,
        in_specs=[pl.BlockSpec(memory_space=pltpu.SMEM), _rows(QBLK, D_ATT)] + _attn_specs(),
        out_specs=[_rows(QBLK, D_ATT)], out_shape=[jax.ShapeDtypeStruct((S, D_ATT), BF16)])


def _w_rows(w_ref):
    return [w_ref[k:k + 1, :] for k in range(w_ref.shape[0])]


def _conv4(x, halo, w, b):
    y = b + w[3] * x
    for s in (1, 2, 3):
        y = y + w[3 - s] * _shift_down(x, halo, s)
    return y


def _rnn_gates(xc, wa, wx, ba, bx, sp):
    xcb = xc.astype(BF16)
    r = _sigmoid(_dot(xcb, wa) + ba)
    ig = _sigmoid(_dot(xcb, wx) + bx)
    la = -LRU_C * r * sp
    a = jnp.exp(la)
    t = jnp.tanh(la)
    f = jnp.sqrt(-2.0 * t / (1.0 - t))
    return r, ig, a, f


def _rnn_fwd(xr, gr, conv_w, conv_b, wa, wx, ba, bx, lam, comm=None):
    S = xr.shape[0]
    tb = min(256, S)

    def body(xr_ref, gr_ref, cw_ref, cb_ref, wa_ref, wx_ref, ba_ref, bx_ref, lam_ref, rec_ref, h_ref,
             halo_s, hc_s, a_s, b_s):
        @pl.when(pl.program_id(0) == 0)
        def _():
            halo_s[...] = jnp.zeros_like(halo_s)
            hc_s[...] = jnp.zeros_like(hc_s)

        x = xr_ref[...]
        xc = _conv4(x, halo_s[...], _w_rows(cw_ref), cb_ref[...])
        halo_s[...] = x[tb - 8:]
        _, ig, a, f = _rnn_gates(xc, wa_ref[...], wx_ref[...], ba_ref[...], bx_ref[...], _softplus_neg(lam_ref[...]))
        a_s[...] = a
        b_s[...] = f * ig * xc
        row8 = lax.broadcasted_iota(jnp.int32, (8, D_RNN), 0)

        def tile(t, hc):
            o = pl.multiple_of(t * 8, 8)
            at = a_s[pl.ds(o, 8), :]
            bt = b_s[pl.ds(o, 8), :]
            for s in (1, 2, 4):
                keep = row8 >= s
                a_sh = jnp.where(keep, pltpu.roll(at, s, 0), 1.0)
                b_sh = jnp.where(keep, pltpu.roll(bt, s, 0), 0.0)
                bt = at * b_sh + bt
                at = at * a_sh
            ht = at * hc + bt
            b_s[pl.ds(o, 8), :] = ht
            return _row_sum(jnp.where(row8 == 7, ht, 0.0))

        hc_s[0:1, :] = lax.fori_loop(0, tb // 8, tile, hc_s[0:1, :])
        h = b_s[...]
        h_ref[...] = h
        rec_ref[...] = (h * _gelu(gr_ref[...])).astype(BF16)

    vec = _resident((1, D_RNN))
    return _pcall(
        body, (xr, gr, conv_w, conv_b, wa, wx, ba, bx, lam), name="rnn_fwd", grid=(S // tb,), sem="arbitrary", comm=comm,
        in_specs=[_rows(tb, D_RNN), _rows(tb, D_RNN), _resident((4, D_RNN)), vec,
                  _resident((D_RNN, D_RNN)), _resident((D_RNN, D_RNN)), vec, vec, vec],
        out_specs=[_rows(tb, D_RNN), _rows(tb, D_RNN)],
        out_shape=[jax.ShapeDtypeStruct((S, D_RNN), BF16), jax.ShapeDtypeStruct((S, D_RNN), F32)],
        scratch_shapes=[pltpu.VMEM((8, D_RNN), F32), pltpu.VMEM((8, D_RNN), F32),
                        pltpu.VMEM((tb, D_RNN), F32), pltpu.VMEM((tb, D_RNN), F32)])


def _mix_ln1_up(x, att, rec, w_out, ln1_g, ln1_b, w_up_top, w_up_bot, fcw, fcb, comm=None):
    S = x.shape[0]
    tb = min(256, S)
    nblk, hrows, wblk = w_up_top.shape
    half = nblk // 2

    def body(x_ref, att_ref, rec_ref, wo_ref, g_ref, b_ref, wt_ref, wb_ref, fcw_ref, fcb_ref,
             z1_ref, h1_ref, gate_ref, act_ref, gl_ref, vdgl_ref, halo_s):
        @pl.when(pl.program_id(0) == 0)
        def _():
            halo_s[...] = jnp.zeros_like(halo_s)

        z1 = ALPHA * x_ref[...] + _dot(att_ref[...], wo_ref[:D_ATT, :]) + _dot(rec_ref[...], wo_ref[D_ATT:, :])
        z1_ref[...] = z1
        xhat, _ = _ln_stats(z1)
        h1b = (xhat * g_ref[...] + b_ref[...]).astype(BF16)
        h1_ref[...] = h1b
        top, bot = h1b[:, :hrows], h1b[:, hrows:]
        for jj in range(half):
            cols = slice(jj * wblk, (jj + 1) * wblk)
            gate = _dot(top, wt_ref[jj]) + _dot(bot, wb_ref[jj])
            val = _dot(top, wt_ref[jj + half]) + _dot(bot, wb_ref[jj + half])
            halo = halo_s[:, cols]
            conv = (fcb_ref[:, cols] + fcw_ref[2:3, cols] * gate + fcw_ref[1:2, cols] * _shift_down(gate, halo, 1)
                    + fcw_ref[0:1, cols] * _shift_down(gate, halo, 2))
            halo_s[:, cols] = gate[tb - 8:]
            gl, dgl = _gelu_and_grad(conv)
            gate_ref[:, cols] = gate.astype(BF16)
            act_ref[:, cols] = (gl * val).astype(BF16)
            gl_ref[:, cols] = gl.astype(BF16)
            vdgl_ref[:, cols] = (val * dgl).astype(BF16)

    vec = _resident((1, D_MODEL))
    wide = jax.ShapeDtypeStruct((S, D_FF), BF16)
    return _pcall(
        body, (x, att, rec, w_out, ln1_g, ln1_b, w_up_top, w_up_bot, fcw, fcb), name="mix_ln1_up", grid=(S // tb,),
        sem="arbitrary", comm=comm,
        in_specs=[_rows(tb, D_MODEL), _rows(tb, D_ATT), _rows(tb, D_RNN), _resident((D_MODEL, D_MODEL)), vec, vec,
                  _resident(w_up_top.shape), _resident(w_up_bot.shape), _resident((3, D_FF)), _resident((1, D_FF))],
        out_specs=[_rows(tb, D_MODEL), _rows(tb, D_MODEL)] + [_rows(tb, D_FF)] * 4,
        out_shape=[jax.ShapeDtypeStruct((S, D_MODEL), F32), jax.ShapeDtypeStruct((S, D_MODEL), BF16), wide, wide, wide, wide],
        scratch_shapes=[pltpu.VMEM((8, D_FF), F32)])


def _tail(act, gl, vdgl, z1, p, tgt, w_down, w_pg, b_pg, w_pp, ln1_g, ln1_b, ln2_g, ln2_b):
    S = z1.shape[0]
    tb = min(256, S)

    def body(act_ref, gl_ref, vdgl_ref, z1_ref, p_ref, t_ref, wd_ref, wpg_ref, bpg_ref, wpp_ref,
             g1_ref, b1_ref, g2_ref, b2_ref,
             dz2_ref, dpre_ref, dpp_ref, dgc_ref, dval_ref, dh1_ref, acc_ref):
        i = pl.program_id(0)

        @pl.when(i == 0)
        def _():
            acc_ref[...] = jnp.zeros_like(acc_ref)

        ffn = _dot(act_ref[...], wd_ref[...])

        xhat1, _ = _ln_stats(z1_ref[...])
        h1 = xhat1 * g1_ref[...] + b1_ref[...]
        sg = _sigmoid(_dot(h1.astype(BF16), wpg_ref[...]) + bpg_ref[...])
        pp = _dot(p_ref[...].astype(BF16), wpp_ref[...])
        z2 = ALPHA * h1 + ffn + sg * pp
        xhat2, rstd2 = _ln_stats(z2)
        y = xhat2 * g2_ref[...] + b2_ref[...]
        err = y - t_ref[...]
        dy = err * (1.0 / D_MODEL)
        loss = 0.5 * jnp.sum(jnp.sum(err * err, axis=1, keepdims=True), axis=0, keepdims=True) * (1.0 / D_MODEL)
        dz2 = _ln_bwd(dy, xhat2, rstd2, g2_ref[...])
        dz2b = dz2.astype(BF16)
        dz2_ref[...] = dz2b
        dpre = dz2 * pp * sg * (1.0 - sg)
        dpreb = dpre.astype(BF16)
        dpre_ref[...] = dpreb
        dpp_ref[...] = (dz2 * sg).astype(BF16)
        dh1_ref[...] = ALPHA * dz2 + _dot_nt(dpreb, wpg_ref[...])
        dact = _dot_nt(dz2b, wd_ref[...])
        dval_ref[...] = (dact * gl_ref[...].astype(F32)).astype(BF16)
        dgc_ref[...] = (dact * vdgl_ref[...].astype(F32)).astype(BF16)
        _put_rows(acc_ref, [_row_sum(dy * xhat2), _row_sum(dy), _row_sum(dpre),
                            jnp.broadcast_to(loss, (1, D_MODEL))])

    vec = _resident((1, D_MODEL))
    return pl.pallas_call(
        body, name="tail", grid=(S // tb,),
        in_specs=[_rows(tb, D_FF), _rows(tb, D_FF), _rows(tb, D_FF), _rows(tb, D_MODEL), _rows(tb, PLE_DIM),
                  _rows(tb, D_MODEL), _resident((D_FF, D_MODEL)), _resident((D_MODEL, D_MODEL)), vec,
                  _resident((PLE_DIM, D_MODEL)), vec, vec, vec, vec],
        out_specs=[_rows(tb, D_MODEL), _rows(tb, D_MODEL), _rows(tb, D_MODEL), _rows(tb, D_FF),
                   _rows(tb, D_FF), _rows(tb, D_MODEL), _acc((8, D_MODEL))],
        out_shape=[jax.ShapeDtypeStruct((S, D_MODEL), BF16),
                   jax.ShapeDtypeStruct((S, D_MODEL), BF16), jax.ShapeDtypeStruct((S, D_MODEL), BF16),
                   jax.ShapeDtypeStruct((S, D_FF), BF16), jax.ShapeDtypeStruct((S, D_FF), BF16),
                   jax.ShapeDtypeStruct((S, D_MODEL), F32), jax.ShapeDtypeStruct((8, D_MODEL), F32)],
        compiler_params=_params("arbitrary"),
    )(act, gl, vdgl, z1, p, tgt, w_down, w_pg, b_pg, w_pp, ln1_g, ln1_b, ln2_g, ln2_b)


def _weight_grad(a_list, b_list, name, layout, ts=512, comm=None):
    S = a_list[0].shape[0]
    ms = [a.shape[1] for a in a_list]
    M, nb, Nb = sum(ms), len(b_list), b_list[0].shape[1]
    ts = min(ts, S)
    nk = S // ts
    per_b = N_DEV // nb
    na = len(a_list)

    def body(*refs):
        a_refs, b_refs, o_ref, acc_ref = refs[:na], refs[na:na + nb], refs[na + nb], refs[na + nb + 1]
        j, k = pl.program_id(0), pl.program_id(1)

        @pl.when(k == 0)
        def _():
            acc_ref[...] = jnp.zeros_like(acc_ref)

        for jj in range(nb):
            @pl.when(j == jj)
            def _():
                b = b_refs[jj][...].astype(BF16)
                off = 0
                for a_ref, m in zip(a_refs, ms):
                    acc_ref[off:off + m, :] += _dot_tn(a_ref[...].astype(BF16), b)
                    off += m

        @pl.when(k == nk - 1)
        def _():
            for d in range(per_b):
                if layout == "rows":
                    o_ref[d] = acc_ref[d * (M // N_DEV):(d + 1) * (M // N_DEV), :].astype(BF16)
                elif layout == "cols":
                    o_ref[d] = acc_ref[:, d * (Nb // per_b):(d + 1) * (Nb // per_b)].astype(BF16)
                else:
                    o_ref[d] = acc_ref[:, d * (Nb // per_b):(d + 1) * (Nb // per_b)].T.astype(BF16)

    def b_index(jj):
        return lambda j, k: (jnp.where(j == jj, k, jnp.where(j < jj, 0, nk - 1)), 0)

    if layout == "rows":
        assert nb == 1
        blk = (N_DEV, M // N_DEV, Nb)
    elif layout == "cols":
        blk = (per_b, M, Nb // per_b)
    else:
        blk = (per_b, Nb // per_b, M)
    (res,), comm_res = _pcall(
        body, (*a_list, *b_list), name=name, grid=(nb, nk), sem="arbitrary", comm=comm, step_axis=1,
        in_specs=[pl.BlockSpec((ts, m), lambda j, k: (k, 0)) for m in ms]
        + [pl.BlockSpec((ts, Nb), b_index(jj)) for jj in range(nb)],
        out_specs=[pl.BlockSpec(blk, lambda j, k: (j, 0, 0))],
        out_shape=[jax.ShapeDtypeStruct((N_DEV,) + blk[1:], BF16)],
        scratch_shapes=[pltpu.VMEM((M, Nb), F32)])
    return (res, comm_res) if comm is not None else res


def _up_bwd(dgc, gate, dval, dh1p, z1, w_up_top, w_up_bot, fcw, w_out, ln1_g, comm=None):
    S = z1.shape[0]
    tb = min(256, S)
    t16 = tb // 16
    n16 = S // 16
    nblk, hrows, wblk = w_up_top.shape
    half = nblk // 2
    nsteps = S // tb

    def body(dgc_ref, dgn_ref, gc_ref, gp_ref, dval_ref, dh1p_ref, z1_ref, wt_ref, wb_ref, fcw_ref, wo_ref, g1_ref,
             dgate_ref, dz1_ref, dz1b_ref, datt_ref, drec_ref, accf_ref, accd_ref):
        i = pl.program_id(0)

        @pl.when(i == 0)
        def _():
            accf_ref[...] = jnp.zeros_like(accf_ref)
            accd_ref[...] = jnp.zeros_like(accd_ref)

        dg = dgc_ref[...].astype(F32)
        nxt = jnp.where(i < nsteps - 1, dgn_ref[...].astype(F32)[0:8], 0.0)
        w = _w_rows(fcw_ref)
        dgate = (w[2] * dg + w[1] * _shift_up(dg, nxt, 1) + w[0] * _shift_up(dg, nxt, 2)).astype(BF16)
        dgate_ref[...] = dgate
        gate = gc_ref[...].astype(F32)
        halo = jnp.where(i > 0, gp_ref[...].astype(F32)[8:16], 0.0)
        _put_rows(accf_ref, [_row_sum(dg * _shift_down(gate, halo, 2)), _row_sum(dg * _shift_down(gate, halo, 1)),
                             _row_sum(dg * gate), _row_sum(dg)])

        dleft, dright = dh1p_ref[:, :hrows], dh1p_ref[:, hrows:]
        for j in range(nblk):
            src = dgate if j < half else dval_ref[...]
            jj = j % half
            dup = src[:, jj * wblk:(jj + 1) * wblk]
            dleft = dleft + _dot_nt(dup, wt_ref[j])
            dright = dright + _dot_nt(dup, wb_ref[j])
        dh1 = jnp.concatenate([dleft, dright], axis=1)
        xhat1, rstd1 = _ln_stats(z1_ref[...])
        dz1 = _ln_bwd(dh1, xhat1, rstd1, g1_ref[...])
        dz1_ref[...] = dz1
        dz1b = dz1.astype(BF16)
        dz1b_ref[...] = dz1b
        dcat = _dot_nt(dz1b, wo_ref[...])
        datt_ref[...] = dcat[:, :D_ATT].astype(BF16)
        drec_ref[...] = dcat[:, D_ATT:]
        _put_rows(accd_ref, [_row_sum(dh1 * xhat1), _row_sum(dh1)])

    prev16 = pl.BlockSpec((16, D_FF), lambda i: (jnp.maximum(i * t16 - 1, 0), 0))
    next16 = pl.BlockSpec((16, D_FF), lambda i: (jnp.minimum((i + 1) * t16, n16 - 1), 0))
    return _pcall(
        body, (dgc, dgc, gate, gate, dval, dh1p, z1, w_up_top, w_up_bot, fcw, w_out, ln1_g), name="up_bwd",
        grid=(nsteps,), sem="arbitrary", comm=comm,
        in_specs=[_rows(tb, D_FF), next16, _rows(tb, D_FF), prev16, _rows(tb, D_FF), _rows(tb, D_MODEL),
                  _rows(tb, D_MODEL), _resident(w_up_top.shape), _resident(w_up_bot.shape), _resident((3, D_FF)),
                  _resident((D_MODEL, D_MODEL)), _resident((1, D_MODEL))],
        out_specs=[_rows(tb, D_FF), _rows(tb, D_MODEL), _rows(tb, D_MODEL), _rows(tb, D_ATT), _rows(tb, D_RNN),
                   _acc((8, D_FF)), _acc((8, D_MODEL))],
        out_shape=[jax.ShapeDtypeStruct((S, D_FF), BF16), jax.ShapeDtypeStruct((S, D_MODEL), F32),
                   jax.ShapeDtypeStruct((S, D_MODEL), BF16), jax.ShapeDtypeStruct((S, D_ATT), BF16),
                   jax.ShapeDtypeStruct((S, D_RNN), F32), jax.ShapeDtypeStruct((8, D_FF), F32),
                   jax.ShapeDtypeStruct((8, D_MODEL), F32)])


def _attn_bwd(q, k, v, do, sinks, comm=None):
    S = q.shape[0]
    grp = N_HEADS // N_KV

    def body(sink_ref, q_ref, kc_ref, kp_ref, vc_ref, vp_ref, do_ref, dq_ref, dkc_ref, dkp_ref, dvc_ref, dvp_ref,
             ds_ref):
        i = pl.program_id(0)

        @pl.when(i == 0)
        def _():
            ds_ref[...] = jnp.zeros_like(ds_ref)

        valid = _band_mask(i)
        row8 = lax.broadcasted_iota(jnp.int32, (8, 128), 0)
        lane8 = lax.broadcasted_iota(jnp.int32, (8, 128), 1)
        dqs, dks, dvs = [], [], []
        dsink = jnp.zeros((8, 128), F32)
        qv = q_ref[...]
        dov = do_ref[...]
        kall = jnp.concatenate([kp_ref[...], kc_ref[...]], axis=0)
        vall = jnp.concatenate([vp_ref[...], vc_ref[...]], axis=0)
        for g in range(N_KV):
            kcat = kall[:, g * HEAD_DIM:(g + 1) * HEAD_DIM]
            vcat = vall[:, g * HEAD_DIM:(g + 1) * HEAD_DIM]
            q4, do4 = _stack_heads(qv, g), _stack_heads(dov, g)
            s = jnp.where(valid, _dot_nt(q4, kcat), -1e30)
            sink = _sink_column(sink_ref, g)
            m = jnp.maximum(jnp.max(s, axis=1, keepdims=True), sink)
            e = jnp.exp(s - m)
            es = jnp.exp(sink - m)
            inv = 1.0 / (jnp.sum(e, axis=1, keepdims=True) + es)
            p = e * inv
            dp = _dot_nt(do4, vcat)
            delta = jnp.sum(p * dp, axis=1, keepdims=True)
            dsc = (p * (dp - delta)).astype(BF16)
            dqs += _unstack_heads(_dot(dsc, kcat) * (HEAD_DIM ** -0.5))
            dks.append(_dot_tn(dsc, q4))
            dvs.append(_dot_tn(p.astype(BF16), do4))
            for hh, part in enumerate(_unstack_heads(-es * inv * delta)):
                here = (row8 == 0) & (lane8 == g * grp + hh)
                dsink = dsink + jnp.where(here, jnp.sum(part, axis=0, keepdims=True), 0.0)
        dq_ref[...] = jnp.concatenate(dqs, axis=1).astype(BF16)
        dk = jnp.concatenate(dks, axis=1)
        dv = jnp.concatenate(dvs, axis=1)
        dkp_ref[...] = dk[:QBLK]
        dkc_ref[...] = dk[QBLK:]
        dvp_ref[...] = dv[:QBLK]
        dvc_ref[...] = dv[QBLK:]
        ds_ref[...] += dsink

    kvs = jax.ShapeDtypeStruct((S, D_KV), F32)
    return _pcall(
        body, (sinks, q, k, k, v, v, do), name="attn_bwd", grid=(S // QBLK,), sem="arbitrary", comm=comm,
        in_specs=[pl.BlockSpec(memory_space=pltpu.SMEM), _rows(QBLK, D_ATT)] + _attn_specs() + [_rows(QBLK, D_ATT)],
        out_specs=[_rows(QBLK, D_ATT), _rows(QBLK, D_KV), _rows(QBLK, D_KV), _rows(QBLK, D_KV), _rows(QBLK, D_KV),
                   _acc((8, 128))],
        out_shape=[jax.ShapeDtypeStruct((S, D_ATT), BF16), kvs, kvs, kvs, kvs, jax.ShapeDtypeStruct((8, 128), F32)])


def _rnn_bwd(xr, gr, h, drec, conv_w, conv_b, wa, wx, ba, bx, lam, comm=None):
    S = xr.shape[0]
    tb = min(256, S)
    t8 = tb // 8
    nsteps = S // tb

    def body(xr_ref, xp_ref, gr_ref, h_ref, hp_ref, drec_ref, cw_ref, cb_ref, wa_ref, wx_ref, ba_ref, bx_ref, lam_ref,
             dxr_ref, dgr_ref, gwa_ref, gwx_ref, acc_ref, carry_s, dxc_halo_s, a_s, d_s, gwa_s, gwx_s):
        i = pl.program_id(0)
        blk = nsteps - 1 - i

        @pl.when(i == 0)
        def _():
            gwa_s[...] = jnp.zeros_like(gwa_s)
            gwx_s[...] = jnp.zeros_like(gwx_s)
            acc_ref[...] = jnp.zeros_like(acc_ref)
            carry_s[...] = jnp.zeros_like(carry_s)
            dxc_halo_s[...] = jnp.zeros_like(dxc_halo_s)

        x = xr_ref[...]
        xhalo = jnp.where(blk > 0, xp_ref[...], 0.0)
        cw = _w_rows(cw_ref)
        xs = [_shift_down(x, xhalo, 3), _shift_down(x, xhalo, 2), _shift_down(x, xhalo, 1), x]
        xc = cb_ref[...] + cw[0] * xs[0] + cw[1] * xs[1] + cw[2] * xs[2] + cw[3] * xs[3]
        sp = _softplus_neg(lam_ref[...])
        r, ig, a, f = _rnn_gates(xc, wa_ref[...], wx_ref[...], ba_ref[...], bx_ref[...], sp)
        hcur = h_ref[...]
        hprev = _shift_down(hcur, jnp.where(blk > 0, hp_ref[...], 0.0), 1)
        gl, dgl = _gelu_and_grad(gr_ref[...])
        drec = drec_ref[...]
        dgr_ref[...] = (drec * hcur * dgl).astype(BF16)
        a_s[...] = a
        d_s[...] = drec * gl
        row8 = lax.broadcasted_iota(jnp.int32, (8, D_RNN), 0)

        def tile(t, c):
            o = pl.multiple_of((t8 - 1 - t) * 8, 8)
            a8 = a_s[pl.ds(o, 8), :]
            dt = d_s[pl.ds(o, 8), :]
            at = jnp.where(row8 == 7, 1.0, pltpu.roll(a8, 7, 0))
            for s in (1, 2, 4):
                keep = row8 < 8 - s
                a_sh = jnp.where(keep, pltpu.roll(at, 8 - s, 0), 1.0)
                d_sh = jnp.where(keep, pltpu.roll(dt, 8 - s, 0), 0.0)
                dt = at * d_sh + dt
                at = at * a_sh
            lt = at * c + dt
            d_s[pl.ds(o, 8), :] = lt
            return _row_sum(jnp.where(row8 == 0, a8 * lt, 0.0))

        carry_s[0:1, :] = lax.fori_loop(0, t8, tile, carry_s[0:1, :])
        lmb = d_s[...]
        a2 = a * a
        dla = lmb * hprev * a - lmb * ig * xc * (a2 / f)
        di = lmb * f * xc
        dr = dla * (-LRU_C) * sp
        dpa = dr * r * (1.0 - r)
        dpx = di * ig * (1.0 - ig)
        dpab = dpa.astype(BF16)
        dpxb = dpx.astype(BF16)
        xcb = xc.astype(BF16)
        gwa_s[...] += _dot_tn(xcb, dpab)
        gwx_s[...] += _dot_tn(xcb, dpxb)

        @pl.when(i == nsteps - 1)
        def _():
            for dense, out in ((gwa_s[...], gwa_ref), (gwx_s[...], gwx_ref)):
                for b in range(RNN_BLOCKS):
                    rows = slice(b * HEAD_DIM, (b + 1) * HEAD_DIM)
                    out[rows, :] = dense[rows, b * HEAD_DIM:(b + 1) * HEAD_DIM]

        dxc = lmb * f * ig + _dot_nt(dpab, wa_ref[...]) + _dot_nt(dpxb, wx_ref[...])
        nxt = dxc_halo_s[...]
        dxr = cw[3] * dxc
        for s in (1, 2, 3):
            dxr = dxr + cw[3 - s] * _shift_up(dxc, nxt, s)
        dxr_ref[...] = dxr.astype(BF16)
        dxc_halo_s[...] = dxc[:8]
        dlam = _row_sum(dla * (-LRU_C) * r) * (-1.0 / (1.0 + jnp.exp(lam_ref[...])))
        _put_rows(acc_ref, [_row_sum(dxc * xs[0]), _row_sum(dxc * xs[1]), _row_sum(dxc * xs[2]), _row_sum(dxc * xs[3]),
                            _row_sum(dxc), _row_sum(dpa), _row_sum(dpx), dlam])

    rev = lambda i: (nsteps - 1 - i, 0)
    prev8 = lambda i: (jnp.maximum((nsteps - 1 - i) * t8 - 1, 0), 0)
    blkspec = pl.BlockSpec((tb, D_RNN), rev)
    halo8 = pl.BlockSpec((8, D_RNN), prev8)
    vec = _resident((1, D_RNN))
    return _pcall(
        body, (xr, xr, gr, h, h, drec, conv_w, conv_b, wa, wx, ba, bx, lam), name="rnn_bwd", grid=(nsteps,),
        sem="arbitrary", comm=comm,
        in_specs=[blkspec, halo8, blkspec, blkspec, halo8, blkspec, _resident((4, D_RNN)), vec,
                  _resident((D_RNN, D_RNN)), _resident((D_RNN, D_RNN)), vec, vec, vec],
        out_specs=[blkspec, blkspec, _acc((D_RNN, HEAD_DIM)), _acc((D_RNN, HEAD_DIM)), _acc((8, D_RNN))],
        out_shape=[jax.ShapeDtypeStruct((S, D_RNN), BF16), jax.ShapeDtypeStruct((S, D_RNN), BF16),
                   jax.ShapeDtypeStruct((D_RNN, HEAD_DIM), F32), jax.ShapeDtypeStruct((D_RNN, HEAD_DIM), F32),
                   jax.ShapeDtypeStruct((8, D_RNN), F32)],
        scratch_shapes=[pltpu.VMEM((8, D_RNN), F32), pltpu.VMEM((8, D_RNN), F32),
                        pltpu.VMEM((tb, D_RNN), F32), pltpu.VMEM((tb, D_RNN), F32),
                        pltpu.VMEM((D_RNN, D_RNN), F32), pltpu.VMEM((D_RNN, D_RNN), F32)])


def _in_bwd(dq, dkc, dkp, dvc, dvp, dxr, dgr, dz1, w_in, comm=None):
    S = dz1.shape[0]
    tb = min(256, S)
    nsteps = S // tb
    nq = S // QBLK
    r = tb // QBLK

    def body(dq_ref, dkc_ref, dkp_ref, dkn_ref, dvc_ref, dvp_ref, dvn_ref, dxr_ref, dgr_ref, dz1_ref, w_ref,
             du_ref, dx_ref):
        i = pl.program_id(0)
        last = i == nsteps - 1

        def shifted(prev_ref, next_ref):
            nxt = jnp.where(last, 0.0, next_ref[...])
            return jnp.concatenate([prev_ref[QBLK:], nxt], axis=0) if r > 1 else nxt

        dk = (dkc_ref[...] + shifted(dkp_ref, dkn_ref)).astype(BF16)
        dv = (dvc_ref[...] + shifted(dvp_ref, dvn_ref)).astype(BF16)
        du = jnp.concatenate([dq_ref[...], dk, dv, dxr_ref[...], dgr_ref[...]], axis=1)
        du_ref[...] = du
        dx_ref[...] = ALPHA * dz1_ref[...] + _dot_nt(du, w_ref[...])

    nextq = pl.BlockSpec((QBLK, D_KV), lambda i: (jnp.minimum((i + 1) * r, nq - 1), 0))
    return _pcall(
        body, (dq, dkc, dkp, dkp, dvc, dvp, dvp, dxr, dgr, dz1, w_in), name="in_bwd", grid=(nsteps,), comm=comm,
        in_specs=[_rows(tb, D_ATT), _rows(tb, D_KV), _rows(tb, D_KV), nextq, _rows(tb, D_KV), _rows(tb, D_KV), nextq,
                  _rows(tb, D_RNN), _rows(tb, D_RNN), _rows(tb, D_MODEL), _resident((D_MODEL, D_IN))],
        out_specs=[_rows(tb, D_IN), _rows(tb, D_MODEL)],
        out_shape=[jax.ShapeDtypeStruct((S, D_IN), BF16), jax.ShapeDtypeStruct((S, D_MODEL), F32)])


def _block_diag(w):
    eye = jnp.eye(RNN_BLOCKS, dtype=w.dtype)
    return (w[:, :, None, :] * eye[:, None, :, None]).reshape(D_RNN, D_RNN).astype(BF16)


def _adamw(w, g, m, v):
    m = ADAM_B1 * m + (1.0 - ADAM_B1) * g
    v = ADAM_B2 * v + (1.0 - ADAM_B2) * (g * g)
    m_hat = m / (1.0 - ADAM_B1 ** ADAM_STEP)
    v_hat = v / (1.0 - ADAM_B2 ** ADAM_STEP)
    delta = -ADAM_LR * (m_hat / (jnp.sqrt(v_hat) + ADAM_EPS) + ADAM_WD * w)
    return delta, m, v


def _sum_adamw(parts, w, m, v, name):
    R, C = w.shape
    rb = 128
    assert R % rb == 0

    def body(p_ref, w_ref, m_ref, v_ref, g_out, d_out, m_out, v_out):
        g = p_ref[0].astype(F32)
        for d in range(1, N_DEV):
            g = g + p_ref[d].astype(F32)
        delta, mn, vn = _adamw(w_ref[...], g, m_ref[...], v_ref[...])
        g_out[...] = g
        d_out[...] = delta
        m_out[...] = mn
        v_out[...] = vn

    blk = _rows(rb, C)
    out = jax.ShapeDtypeStruct((R, C), F32)
    return pl.pallas_call(
        body, name=name, grid=(R // rb,),
        in_specs=[pl.BlockSpec((N_DEV, rb, C), lambda i: (0, i, 0)), blk, blk, blk],
        out_specs=[blk, blk, blk, blk], out_shape=[out, out, out, out],
        compiler_params=_params("parallel"),
    )(parts, w, m, v)


_SMALL = [("attn_sinks", "s", 0, 1, None), ("rnn_conv_w", "r", 0, 4, "cols"), ("rnn_conv_b", "r", 4, 1, None),
          ("gate_a_w", "a", 0, D_RNN, None), ("gate_a_b", "r", 5, 1, None), ("gate_x_w", "x", 0, D_RNN, None),
          ("gate_x_b", "r", 6, 1, None), ("lru_lambda", "r", 7, 1, None), ("ln1_g", "d", 0, 1, None),
          ("ln1_b", "d", 1, 1, None), ("ffn_conv_w", "f", 0, 3, "cols"), ("ffn_conv_b", "f", 3, 1, None),
          ("ple_gate_b", "t", 2, 1, None), ("ln2_g", "t", 0, 1, None), ("ln2_b", "t", 1, 1, None)]
_LOSS_ROW = 3


def _small_update(gathered, params):
    keys = "tfdsrax"
    flat = [arr for triple in params for arr in triple]
    n_par = len(_SMALL)

    def body(*refs):
        g_refs = dict(zip(keys, refs[:7]))
        p_refs = refs[7:7 + 3 * n_par]
        loss_ref = refs[7 + 3 * n_par]
        o_refs = refs[8 + 3 * n_par:8 + 7 * n_par]
        tot = dict(zip(keys, refs[8 + 7 * n_par:8 + 7 * n_par + 7]))
        tmp_r, tmp_f = refs[8 + 7 * n_par + 7:]
        me = _dev_index(*_place())
        for key in keys:
            s = g_refs[key][0]
            for d in range(1, N_DEV):
                s = s + g_refs[key][d]
            tot[key][...] = s
        loss_ref[...] = tot["t"][_LOSS_ROW:_LOSS_ROW + 1, 0:128]
        for i, (name, key, row, rows, how) in enumerate(_SMALL):
            w_ref, m_ref, v_ref = p_refs[3 * i:3 * i + 3]
            g_out, d_out, m_out, v_out = o_refs[4 * i:4 * i + 4]
            if how == "cols":
                full = tot[key][...]
                width = full.shape[1] // N_DEV
                mine = full[:, :width]
                for d in range(1, N_DEV):
                    mine = jnp.where(me == d, full[:, d * width:(d + 1) * width], mine)
                tmp = tmp_r if key == "r" else tmp_f
                tmp[...] = mine
                g = tmp[row:row + rows, :]
            else:
                g = tot[key][row:row + rows, :]
                g = g[:, :w_ref.shape[1]]
            delta, mn, vn = _adamw(w_ref[...], g, m_ref[...], v_ref[...])
            g_out[...] = g
            d_out[...] = delta
            m_out[...] = mn
            v_out[...] = vn

    outs = [jax.ShapeDtypeStruct((1, 128), F32)]
    for w, _, _ in params:
        outs += [jax.ShapeDtypeStruct(w.shape, F32)] * 4
    scratch = [pltpu.VMEM(g.shape[1:], F32) for g in gathered]
    scratch += [pltpu.VMEM((8, D_RNN // N_DEV), F32), pltpu.VMEM((8, D_FF // N_DEV), F32)]
    res = pl.pallas_call(body, name="small_update", out_shape=outs, scratch_shapes=scratch)(*gathered, *flat)
    return res[0], [res[1 + 4 * i:5 + 4 * i] for i in range(n_par)]


def kernel(x, p, w_in, attn_sinks, rnn_conv_w, rnn_conv_b, gate_a_w, gate_a_b, gate_x_w, gate_x_b, lru_lambda, w_out, ln1_g, ln1_b, w_ffn_up, ffn_conv_w, ffn_conv_b, w_ffn_down, ple_gate_w, ple_gate_b, ple_proj, ln2_g, ln2_b, loss_target, m_w_in, m_attn_sinks, m_rnn_conv_w, m_rnn_conv_b, m_gate_a_w, m_gate_a_b, m_gate_x_w, m_gate_x_b, m_lru_lambda, m_w_out, m_ln1_g, m_ln1_b, m_w_ffn_up, m_ffn_conv_w, m_ffn_conv_b, m_w_ffn_down, m_ple_gate_w, m_ple_gate_b, m_ple_proj, m_ln2_g, m_ln2_b, v_w_in, v_attn_sinks, v_rnn_conv_w, v_rnn_conv_b, v_gate_a_w, v_gate_a_b, v_gate_x_w, v_gate_x_b, v_lru_lambda, v_w_out, v_ln1_g, v_ln1_b, v_w_ffn_up, v_ffn_conv_w, v_ffn_conv_b, v_w_ffn_down, v_ple_gate_w, v_ple_gate_b, v_ple_proj, v_ln2_g, v_ln2_b):
    from_col_blocks = lambda g: g.transpose(1, 0, 2).reshape(g.shape[1], N_DEV * g.shape[2])

    xs, ps, tgt, sinks = x[0], p[0, 0], loss_target[0], attn_sinks[0]
    wa, wx = _block_diag(gate_a_w[0]), _block_diag(gate_x_w[0])

    conv_cols = jnp.concatenate([rnn_conv_w[0].reshape(1, -1), ffn_conv_w[0].reshape(1, -1)], axis=1)
    n_rc, n_fc = 4 * D_RNN // N_DEV, 3 * D_FF // N_DEV
    g_in, g_conv = _comm_call(_Gather([w_in[0].astype(BF16), jnp.broadcast_to(conv_cols, (8, n_rc + n_fc))]),
                              "gather_w_in")
    w_in_full = from_col_blocks(g_in)
    rcw = from_col_blocks(g_conv[:, 0, :n_rc].reshape(N_DEV, 4, D_RNN // N_DEV))
    fcw = from_col_blocks(g_conv[:, 0, n_rc:].reshape(N_DEV, 3, D_FF // N_DEV))
    w_up_shard = w_ffn_up[0].astype(BF16)
    half_rows = D_MODEL // 2

    q, k, v, xr, gr = _in_proj(xs, w_in_full)
    (att,), (g_out, g_up_top) = _attn_fwd(q, k, v, sinks,
                                          comm=_Gather([w_out[0].astype(BF16), w_up_shard[:half_rows]]))
    (rec, h), (g_up_bot,) = _rnn_fwd(xr, gr, rcw, rnn_conv_b, wa, wx, gate_a_b, gate_x_b, lru_lambda,
                                     comm=_Gather([w_up_shard[half_rows:]]))
    w_out_full = g_out.reshape(D_MODEL, D_MODEL)
    (z1, h1b, gate, act, gl, vdgl), (g_down, g_pg, g_pp) = _mix_ln1_up(
        xs, att, rec, w_out_full, ln1_g, ln1_b, g_up_top, g_up_bot, fcw, ffn_conv_b,
        comm=_Gather([w_ffn_down[0].astype(BF16), ple_gate_w[0].astype(BF16), ple_proj[0].astype(BF16)]))
    dz2b, dpreb, dppb, dgc, dval, dh1p, acc_t = _tail(
        act, gl, vdgl, z1, ps, tgt, g_down.reshape(D_FF, D_MODEL), g_pg.reshape(D_MODEL, D_MODEL), ple_gate_b,
        from_col_blocks(g_pp), ln1_g, ln1_b, ln2_g, ln2_b)

    gd_down = _weight_grad([dz2b], [act], "down_grad", "rows_t")
    gd_pg = _weight_grad([h1b], [dpreb], "pg_grad", "rows", ts=1024)
    gd_pp = _weight_grad([ps], [dppb], "pp_grad", "cols", ts=1024)
    (dgate, dz1, dz1b, datt, drec, acc_f, acc_d), (r_down, r_pg, r_pp) = _up_bwd(
        dgc, gate, dval, dh1p, z1, g_up_top, g_up_bot, fcw, w_out_full, ln1_g, comm=_Exchange([gd_down, gd_pg, gd_pp]))
    gd_up = _weight_grad([h1b], [dgate, dval], "up_grad", "cols")
    gd_out = _weight_grad([att, rec], [dz1b], "out_grad", "rows", ts=1024)
    (dq, dkc, dkp, dvc, dvp, acc_s), (r_up,) = _attn_bwd(q, k, v, datt, sinks, comm=_Exchange([gd_up]))
    (dxr, dgr, g_wa, g_wx, acc_r), (r_out,) = _rnn_bwd(xr, gr, h, drec, rcw, rnn_conv_b, wa, wx, gate_a_b, gate_x_b,
                                                       lru_lambda, comm=_Exchange([gd_out]))
    (du, dx), _ = _in_bwd(dq, dkc, dkp, dvc, dvp, dxr, dgr, dz1, w_in_full)
    gd_in, small_parts = _weight_grad([xs], [du], "in_grad", "cols", ts=1024,
                                      comm=_Gather([acc_t, acc_f, acc_d, acc_s, acc_r, g_wa, g_wx]))
    (r_in,) = _comm_call(_Exchange([gd_in]), "exchange_w_in")

    outs = {}
    for name, parts, w, m, v in [("w_in", r_in, w_in, m_w_in, v_w_in), ("w_out", r_out, w_out, m_w_out, v_w_out),
                                 ("w_ffn_up", r_up, w_ffn_up, m_w_ffn_up, v_w_ffn_up),
                                 ("w_ffn_down", r_down, w_ffn_down, m_w_ffn_down, v_w_ffn_down),
                                 ("ple_gate_w", r_pg, ple_gate_w, m_ple_gate_w, v_ple_gate_w),
                                 ("ple_proj", r_pp, ple_proj, m_ple_proj, v_ple_proj)]:
        res = _sum_adamw(parts, w[0], m[0], v[0], "adamw_" + name)
        outs[name] = [r[None] for r in res]

    given = dict(attn_sinks=(attn_sinks, m_attn_sinks, v_attn_sinks), rnn_conv_w=(rnn_conv_w, m_rnn_conv_w, v_rnn_conv_w),
                 rnn_conv_b=(rnn_conv_b, m_rnn_conv_b, v_rnn_conv_b), gate_a_w=(gate_a_w, m_gate_a_w, v_gate_a_w),
                 gate_a_b=(gate_a_b, m_gate_a_b, v_gate_a_b), gate_x_w=(gate_x_w, m_gate_x_w, v_gate_x_w),
                 gate_x_b=(gate_x_b, m_gate_x_b, v_gate_x_b), lru_lambda=(lru_lambda, m_lru_lambda, v_lru_lambda),
                 ln1_g=(ln1_g, m_ln1_g, v_ln1_g), ln1_b=(ln1_b, m_ln1_b, v_ln1_b),
                 ffn_conv_w=(ffn_conv_w, m_ffn_conv_w, v_ffn_conv_w), ffn_conv_b=(ffn_conv_b, m_ffn_conv_b, v_ffn_conv_b),
                 ple_gate_b=(ple_gate_b, m_ple_gate_b, v_ple_gate_b), ln2_g=(ln2_g, m_ln2_g, v_ln2_g),
                 ln2_b=(ln2_b, m_ln2_b, v_ln2_b))
    as_2d = lambda a: a.reshape(-1, a.shape[-1])
    loss_row, small_res = _small_update(small_parts, [tuple(as_2d(a) for a in given[n]) for n, *_ in _SMALL])
    loss = loss_row[0, 0]
    for (n, *_), res in zip(_SMALL, small_res):
        outs[n] = [r.reshape(given[n][0].shape) for r in res]

    order = ["w_in", "attn_sinks", "rnn_conv_w", "rnn_conv_b", "gate_a_w", "gate_a_b", "gate_x_w", "gate_x_b",
             "lru_lambda", "w_out", "ln1_g", "ln1_b", "w_ffn_up", "ffn_conv_w", "ffn_conv_b", "w_ffn_down",
             "ple_gate_w", "ple_gate_b", "ple_proj", "ln2_g", "ln2_b"]
    return (loss, dx[None], *[outs[n][0] for n in order], *[outs[n][1] for n in order],
            *[outs[n][2] for n in order], *[outs[n][3] for n in order])
```

```python
import jax
import jax.numpy as jnp
from jax import lax
from jax.experimental import pallas as pl
from jax.experimental.pallas import tpu as pltpu

F32 = jnp.float32
BF16 = jnp.bfloat16

D_MODEL = 1024
D_ATT = 512
D_KV = 128
HEAD_DIM = 64
N_HEADS = 8
N_KV = 2
D_RNN = 512
RNN_BLOCKS = 8
D_IN = 1792
D_FF = 3072
PLE_DIM = 256
QBLK = 128
N_DEV = 8
ALPHA = float(2 ** 0.25)
LN_EPS = 1e-5
LRU_C = 8.0
ADAM_LR, ADAM_B1, ADAM_B2, ADAM_EPS, ADAM_WD, ADAM_STEP = 0.001, 0.9, 0.999, 1e-08, 0.01, 10

V7X_VMEM_LIMIT = 56 * 1024 * 1024
MESH = pl.DeviceIdType.MESH


def _params(*sem, vmem=V7X_VMEM_LIMIT):
    return pltpu.CompilerParams(dimension_semantics=sem or None, vmem_limit_bytes=vmem)


def _resident(shape):
    return pl.BlockSpec(shape, lambda *_: (0,) * len(shape), pipeline_mode=pl.Buffered(1))


def _rows(tb, cols):
    return pl.BlockSpec((tb, cols), lambda i: (i, 0))


def _acc(shape):
    return pl.BlockSpec(shape, lambda *_: (0,) * len(shape))


def _dot(a, b):
    return jnp.dot(a, b, preferred_element_type=F32)


def _dot_nt(a, b):
    return lax.dot_general(a, b, (((1,), (1,)), ((), ())), preferred_element_type=F32)


def _dot_tn(a, b):
    return lax.dot_general(a, b, (((0,), (0,)), ((), ())), preferred_element_type=F32)


def _sigmoid(x):
    return 1.0 / (1.0 + jnp.exp(-x))


_GELU_C = 0.7978845608028654
_GELU_K = 0.044715


def _gelu_and_grad(x):
    u = x * x
    t = jnp.tanh(x * (_GELU_C + (_GELU_C * _GELU_K) * u))
    hp = 0.5 + 0.5 * t
    dg = hp + x * (0.5 - 0.5 * (t * t)) * (_GELU_C + (3.0 * _GELU_C * _GELU_K) * u)
    return x * hp, dg


def _gelu(x):
    return 0.5 * x * (1.0 + jnp.tanh(_GELU_C * (x + _GELU_K * x * x * x)))


def _ln_stats(z):
    mu = jnp.mean(z, axis=-1, keepdims=True)
    zc = z - mu
    var = jnp.mean(zc * zc, axis=-1, keepdims=True)
    rstd = lax.rsqrt(var + LN_EPS)
    return zc * rstd, rstd


def _ln_bwd(dy, xhat, rstd, g):
    dxh = dy * g
    m1 = jnp.mean(dxh, axis=-1, keepdims=True)
    m2 = jnp.mean(dxh * xhat, axis=-1, keepdims=True)
    return rstd * (dxh - m1 - xhat * m2)


def _softplus_neg(lam):
    u = jnp.exp(-jnp.abs(lam))
    w = 1.0 + u
    d = w - 1.0
    log1p_u = jnp.where(d == 0.0, u, jnp.log(w) * (u / jnp.where(d == 0.0, 1.0, d)))
    return jnp.maximum(-lam, 0.0) + log1p_u


def _shift_down(x, halo, s):
    xs = pltpu.roll(x, s, 0)
    hs = pltpu.roll(halo, s, 0)
    row8 = lax.broadcasted_iota(jnp.int32, hs.shape, 0)
    first = jnp.where(row8 < s, hs, xs[:8])
    return jnp.concatenate([first, xs[8:]], axis=0)


def _shift_up(x, halo, s):
    n = x.shape[0]
    xs = pltpu.roll(x, n - s, 0)
    hs = pltpu.roll(halo, 8 - s, 0)
    row8 = lax.broadcasted_iota(jnp.int32, hs.shape, 0)
    last = jnp.where(row8 >= 8 - s, hs, xs[n - 8:])
    return jnp.concatenate([xs[:n - 8], last], axis=0)


def _row_sum(x):
    return jnp.sum(x, axis=0, keepdims=True)


def _put_rows(acc_ref, rows):
    row8 = lax.broadcasted_iota(jnp.int32, acc_ref.shape, 0)
    upd = jnp.zeros(acc_ref.shape, F32)
    for r, vec in enumerate(rows):
        upd = jnp.where(row8 == r, vec, upd)
    acc_ref[...] += upd


def _place():
    return lax.axis_index("x"), lax.axis_index("y"), lax.axis_index("c")


def _dev_index(px, py, pc):
    return 4 * px + 2 * py + pc


_ANY = pl.BlockSpec(memory_space=pl.ANY)


class _Gather:
    def __init__(self, arrays):
        self.arrays = list(arrays)
        self.n = len(self.arrays)

    def out_shape(self):
        return [jax.ShapeDtypeStruct((N_DEV,) + s.shape, s.dtype) for s in self.arrays]

    def scratch(self):
        return [pltpu.SemaphoreType.DMA((self.n, 7)), pltpu.SemaphoreType.DMA((self.n, 7)),
                pltpu.SemaphoreType.DMA((self.n,))]

    def _parts(self, ins, outs, sems):
        send_sems, recv_sems, local_sems = sems
        x, y, c = _place()
        me, sibling = (x, y, c), (x, y, 1 - c)
        chips = [(1 - x, y), (x, 1 - y), (1 - x, 1 - y)]

        def copy(a, k, block, to, src=None):
            rows = outs[a].at[_dev_index(*block)]
            return pltpu.make_async_remote_copy(
                src_ref=rows if src is None else src, dst_ref=rows, send_sem=send_sems.at[a, k],
                recv_sem=recv_sems.at[a, k], device_id=to, device_id_type=MESH)

        rng = range(self.n)
        mine = [pltpu.make_async_copy(ins[a], outs[a].at[_dev_index(*me)], local_sems.at[a]) for a in rng]
        first = [copy(a, 0, me, sibling, src=ins[a]) for a in rng]
        first += [copy(a, 1 + j, me, (*chip, c), src=ins[a]) for j, chip in enumerate(chips) for a in rng]
        landed = [copy(a, 1 + j, (*chip, c), me) for j, chip in enumerate(chips) for a in rng]
        passed = [copy(a, 4 + j, (*chip, c), sibling) for j, chip in enumerate(chips) for a in rng]
        from_sibling = [copy(a, 0, sibling, me) for a in rng]
        from_sibling += [copy(a, 4 + j, (*chip, 1 - c), me) for j, chip in enumerate(chips) for a in rng]
        return mine, first, landed, passed, from_sibling

    def start(self, ins, outs, sems):
        mine, first, _, _, _ = self._parts(ins, outs, sems)
        for cp in mine + first:
            cp.start()

    def forward(self, ins, outs, sems):
        _, _, landed, passed, _ = self._parts(ins, outs, sems)
        for got, fwd in zip(landed, passed):
            got.wait_recv()
            fwd.start()

    def finish(self, ins, outs, sems):
        mine, first, _, passed, from_sibling = self._parts(ins, outs, sems)
        for cp in from_sibling:
            cp.wait_recv()
        for cp in first + passed:
            cp.wait_send()
        for cp in mine:
            cp.wait()

    def before(self, ins, outs, sems, step, nsteps):
        pl.when(step == 0)(lambda: self.start(ins, outs, sems))
        pl.when(step == (7 * nsteps) // 8)(lambda: self.forward(ins, outs, sems))

    def after(self, ins, outs, sems, step, nsteps):
        pl.when(step == nsteps - 1)(lambda: self.finish(ins, outs, sems))


class _Exchange:
    def __init__(self, arrays):
        self.arrays = list(arrays)
        self.n = len(self.arrays)

    def out_shape(self):
        return [jax.ShapeDtypeStruct(b.shape, b.dtype) for b in self.arrays]

    def scratch(self):
        return [pltpu.SemaphoreType.DMA((self.n, 7)), pltpu.SemaphoreType.DMA((self.n, 7)),
                pltpu.SemaphoreType.DMA((self.n,))]

    def _parts(self, ins, outs, sems):
        send_sems, recv_sems, local_sems = sems
        x, y, c = _place()
        me = _dev_index(x, y, c)
        peers = [(x ^ (k >> 2), y ^ ((k >> 1) & 1), c ^ (k & 1)) for k in range(1, N_DEV)]
        rng = range(self.n)
        mine = [pltpu.make_async_copy(ins[a].at[me], outs[a].at[me], local_sems.at[a]) for a in rng]
        sent = [pltpu.make_async_remote_copy(
            src_ref=ins[a].at[_dev_index(*to)], dst_ref=outs[a].at[me], send_sem=send_sems.at[a, k],
            recv_sem=recv_sems.at[a, k], device_id=to, device_id_type=MESH) for k, to in enumerate(peers) for a in rng]
        arrivals = [pltpu.make_async_remote_copy(
            src_ref=ins[a].at[me], dst_ref=outs[a].at[_dev_index(*frm)], send_sem=send_sems.at[a, k],
            recv_sem=recv_sems.at[a, k], device_id=frm, device_id_type=MESH) for k, frm in enumerate(peers) for a in rng]
        return mine, sent, arrivals

    def start(self, ins, outs, sems):
        mine, sent, _ = self._parts(ins, outs, sems)
        for cp in mine + sent:
            cp.start()

    def finish(self, ins, outs, sems):
        mine, sent, arrivals = self._parts(ins, outs, sems)
        for cp in arrivals:
            cp.wait_recv()
        for cp in sent:
            cp.wait_send()
        for cp in mine:
            cp.wait()

    def before(self, ins, outs, sems, step, nsteps):
        pl.when(step == 0)(lambda: self.start(ins, outs, sems))

    def after(self, ins, outs, sems, step, nsteps):
        pl.when(step == nsteps - 1)(lambda: self.finish(ins, outs, sems))


def _comm_call(comms, name):
    ns = [c.n for c in comms]
    n = sum(ns)

    def body(*refs):
        parts, a, s = [], 0, 2 * n
        for c in comms:
            parts.append((c, refs[a:a + c.n], refs[n + a:n + a + c.n], refs[s:s + 3]))
            a, s = a + c.n, s + 3
        for c, ins, outs, sems in parts:
            c.start(ins, outs, sems)
        for c, ins, outs, sems in parts:
            if isinstance(c, _Gather):
                c.forward(ins, outs, sems)
        for c, ins, outs, sems in parts:
            c.finish(ins, outs, sems)

    res = pl.pallas_call(
        body, name=name, in_specs=[_ANY] * n, out_specs=[_ANY] * n,
        out_shape=[s for c in comms for s in c.out_shape()], scratch_shapes=[s for c in comms for s in c.scratch()],
    )(*[arr for c in comms for arr in c.arrays])
    out, a = [], 0
    for k in ns:
        out.append(res[a:a + k])
        a += k
    return out


def _pcall(body, args, *, name, grid, in_specs, out_specs, out_shape, scratch_shapes=(), sem="parallel", comm=None,
           step_axis=0):
    sem = (sem,) * len(grid) if isinstance(sem, str) else sem
    if comm is None:
        res = pl.pallas_call(body, name=name, grid=grid, in_specs=in_specs, out_specs=out_specs, out_shape=out_shape,
                             scratch_shapes=list(scratch_shapes), compiler_params=_params(*sem))(*args)
        return res, []
    n_in, n_out, n_scr, n = len(in_specs), len(out_specs), len(scratch_shapes), comm.n
    nsteps = grid[step_axis]
    assert all(g == 1 for ax, g in enumerate(grid) if ax != step_axis)

    def hosted(*refs):
        ins, cin = refs[:n_in], refs[n_in:n_in + n]
        o0 = n_in + n
        outs, cout = refs[o0:o0 + n_out], refs[o0 + n_out:o0 + n_out + n]
        s0 = o0 + n_out + n
        scr, sems = refs[s0:s0 + n_scr], refs[s0 + n_scr:]
        step = pl.program_id(step_axis)
        comm.before(cin, cout, sems, step, nsteps)
        body(*ins, *outs, *scr)
        comm.after(cin, cout, sems, step, nsteps)

    res = pl.pallas_call(
        hosted, name=name, grid=grid, in_specs=list(in_specs) + [_ANY] * n, out_specs=list(out_specs) + [_ANY] * n,
        out_shape=list(out_shape) + comm.out_shape(), scratch_shapes=list(scratch_shapes) + comm.scratch(),
        compiler_params=_params(*(("arbitrary",) * len(grid))))(*args, *comm.arrays)
    return res[:n_out], res[n_out:]


def _in_proj(x, w_in):
    S = x.shape[0]
    tb = min(512, S)

    def body(x_ref, w_ref, q_ref, k_ref, v_ref, xr_ref, gr_ref):
        u = _dot(x_ref[...].astype(BF16), w_ref[...])
        q_ref[...] = (u[:, :D_ATT] * (HEAD_DIM ** -0.5)).astype(BF16)
        k_ref[...] = u[:, D_ATT:D_ATT + D_KV].astype(BF16)
        v_ref[...] = u[:, D_ATT + D_KV:D_ATT + 2 * D_KV].astype(BF16)
        xr_ref[...] = u[:, D_ATT + 2 * D_KV:D_ATT + 2 * D_KV + D_RNN]
        gr_ref[...] = u[:, D_ATT + 2 * D_KV + D_RNN:]

    return pl.pallas_call(
        body, name="in_proj", grid=(S // tb,),
        in_specs=[_rows(tb, D_MODEL), _resident((D_MODEL, D_IN))],
        out_specs=[_rows(tb, D_ATT), _rows(tb, D_KV), _rows(tb, D_KV), _rows(tb, D_RNN), _rows(tb, D_RNN)],
        out_shape=[jax.ShapeDtypeStruct((S, D_ATT), BF16), jax.ShapeDtypeStruct((S, D_KV), BF16),
                   jax.ShapeDtypeStruct((S, D_KV), BF16), jax.ShapeDtypeStruct((S, D_RNN), F32),
                   jax.ShapeDtypeStruct((S, D_RNN), F32)],
        compiler_params=_params("parallel"),
    )(x, w_in)


GROUP = N_HEADS // N_KV


def _band_mask(i):
    qi = lax.broadcasted_iota(jnp.int32, (GROUP * QBLK, 2 * QBLK), 0) & (QBLK - 1)
    sj = lax.broadcasted_iota(jnp.int32, (GROUP * QBLK, 2 * QBLK), 1)
    return (sj > qi) & (sj <= qi + QBLK) & ((sj >= QBLK) | (i > 0))


def _stack_heads(x, g):
    return jnp.concatenate([x[:, (g * GROUP + hh) * HEAD_DIM:(g * GROUP + hh + 1) * HEAD_DIM] for hh in range(GROUP)],
                           axis=0)


def _unstack_heads(x4):
    return [x4[hh * QBLK:(hh + 1) * QBLK] for hh in range(GROUP)]


def _sink_column(sink_ref, g):
    head = lax.broadcasted_iota(jnp.int32, (GROUP * QBLK, 1), 0) // QBLK
    col = jnp.full((GROUP * QBLK, 1), sink_ref[g * GROUP], F32)
    for hh in range(1, GROUP):
        col = jnp.where(head == hh, sink_ref[g * GROUP + hh], col)
    return col


def _attn_specs():
    cur = lambda i: (i, 0)
    prev = lambda i: (jnp.maximum(i - 1, 0), 0)
    return [pl.BlockSpec((QBLK, D_KV), cur), pl.BlockSpec((QBLK, D_KV), prev),
            pl.BlockSpec((QBLK, D_KV), cur), pl.BlockSpec((QBLK, D_KV), prev)]


def _attn_fwd(q, k, v, sinks, comm=None):
    S = q.shape[0]

    def body(sink_ref, q_ref, kc_ref, kp_ref, vc_ref, vp_ref, o_ref):
        valid = _band_mask(pl.program_id(0))
        outs = []
        qv = q_ref[...]
        kall = jnp.concatenate([kp_ref[...], kc_ref[...]], axis=0)
        vall = jnp.concatenate([vp_ref[...], vc_ref[...]], axis=0)
        for g in range(N_KV):
            kcat = kall[:, g * HEAD_DIM:(g + 1) * HEAD_DIM]
            vcat = vall[:, g * HEAD_DIM:(g + 1) * HEAD_DIM]
            s = jnp.where(valid, _dot_nt(_stack_heads(qv, g), kcat), -1e30)
            sink = _sink_column(sink_ref, g)
            m = jnp.maximum(jnp.max(s, axis=1, keepdims=True), sink)
            p = jnp.exp(s - m)
            l = jnp.sum(p, axis=1, keepdims=True) + jnp.exp(sink - m)
            outs += _unstack_heads(_dot(p.astype(BF16), vcat) / l)
        o_ref[...] = jnp.concatenate(outs, axis=1).astype(BF16)

    return _pcall(
        body, (sinks, q, k, k, v, v), name="attn_fwd", grid=(S // QBLK,), comm=comm,
        in_specs=[pl.BlockSpec(memory_space=pltpu.SMEM), _rows(QBLK, D_ATT)] + _attn_specs(),
        out_specs=[_rows(QBLK, D_ATT)], out_shape=[jax.ShapeDtypeStruct((S, D_ATT), BF16)])


def _w_rows(w_ref):
    return [w_ref[k:k + 1, :] for k in range(w_ref.shape[0])]


def _conv4(x, halo, w, b):
    y = b + w[3] * x
    for s in (1, 2, 3):
        y = y + w[3 - s] * _shift_down(x, halo, s)
    return y


def _rnn_gates(xc, wa, wx, ba, bx, sp):
    xcb = xc.astype(BF16)
    r = _sigmoid(_dot(xcb, wa) + ba)
    ig = _sigmoid(_dot(xcb, wx) + bx)
    la = -LRU_C * r * sp
    a = jnp.exp(la)
    t = jnp.tanh(la)
    f = jnp.sqrt(-2.0 * t / (1.0 - t))
    return r, ig, a, f


def _rnn_fwd(xr, gr, conv_w, conv_b, wa, wx, ba, bx, lam, comm=None):
    S = xr.shape[0]
    tb = min(256, S)

    def body(xr_ref, gr_ref, cw_ref, cb_ref, wa_ref, wx_ref, ba_ref, bx_ref, lam_ref, rec_ref, h_ref,
             halo_s, hc_s, a_s, b_s):
        @pl.when(pl.program_id(0) == 0)
        def _():
            halo_s[...] = jnp.zeros_like(halo_s)
            hc_s[...] = jnp.zeros_like(hc_s)

        x = xr_ref[...]
        xc = _conv4(x, halo_s[...], _w_rows(cw_ref), cb_ref[...])
        halo_s[...] = x[tb - 8:]
        _, ig, a, f = _rnn_gates(xc, wa_ref[...], wx_ref[...], ba_ref[...], bx_ref[...], _softplus_neg(lam_ref[...]))
        a_s[...] = a
        b_s[...] = f * ig * xc
        row8 = lax.broadcasted_iota(jnp.int32, (8, D_RNN), 0)

        def tile(t, hc):
            o = pl.multiple_of(t * 8, 8)
            at = a_s[pl.ds(o, 8), :]
            bt = b_s[pl.ds(o, 8), :]
            for s in (1, 2, 4):
                keep = row8 >= s
                a_sh = jnp.where(keep, pltpu.roll(at, s, 0), 1.0)
                b_sh = jnp.where(keep, pltpu.roll(bt, s, 0), 0.0)
                bt = at * b_sh + bt
                at = at * a_sh
            ht = at * hc + bt
            b_s[pl.ds(o, 8), :] = ht
            return _row_sum(jnp.where(row8 == 7, ht, 0.0))

        hc_s[0:1, :] = lax.fori_loop(0, tb // 8, tile, hc_s[0:1, :])
        h = b_s[...]
        h_ref[...] = h
        rec_ref[...] = (h * _gelu(gr_ref[...])).astype(BF16)

    vec = _resident((1, D_RNN))
    return _pcall(
        body, (xr, gr, conv_w, conv_b, wa, wx, ba, bx, lam), name="rnn_fwd", grid=(S // tb,), sem="arbitrary", comm=comm,
        in_specs=[_rows(tb, D_RNN), _rows(tb, D_RNN), _resident((4, D_RNN)), vec,
                  _resident((D_RNN, D_RNN)), _resident((D_RNN, D_RNN)), vec, vec, vec],
        out_specs=[_rows(tb, D_RNN), _rows(tb, D_RNN)],
        out_shape=[jax.ShapeDtypeStruct((S, D_RNN), BF16), jax.ShapeDtypeStruct((S, D_RNN), F32)],
        scratch_shapes=[pltpu.VMEM((8, D_RNN), F32), pltpu.VMEM((8, D_RNN), F32),
                        pltpu.VMEM((tb, D_RNN), F32), pltpu.VMEM((tb, D_RNN), F32)])


def _mix_ln1_up(x, att, rec, w_out, ln1_g, ln1_b, w_up, fcw, fcb, comm=None):
    S = x.shape[0]
    tb = min(256, S)
    nblk, _, wblk = w_up.shape
    half = nblk // 2

    def body(x_ref, att_ref, rec_ref, wo_ref, g_ref, b_ref, wu_ref, fcw_ref, fcb_ref,
             z1_ref, h1_ref, h1b_ref, gate_ref, act_ref, gl_ref, vdgl_ref, halo_s):
        @pl.when(pl.program_id(0) == 0)
        def _():
            halo_s[...] = jnp.zeros_like(halo_s)

        z1 = ALPHA * x_ref[...] + _dot(att_ref[...], wo_ref[:D_ATT, :]) + _dot(rec_ref[...], wo_ref[D_ATT:, :])
        z1_ref[...] = z1
        xhat, _ = _ln_stats(z1)
        h1 = xhat * g_ref[...] + b_ref[...]
        h1_ref[...] = h1
        h1b = h1.astype(BF16)
        h1b_ref[...] = h1b
        for jj in range(half):
            cols = slice(jj * wblk, (jj + 1) * wblk)
            gate = _dot(h1b, wu_ref[jj])
            val = _dot(h1b, wu_ref[jj + half])
            halo = halo_s[:, cols]
            conv = (fcb_ref[:, cols] + fcw_ref[2:3, cols] * gate + fcw_ref[1:2, cols] * _shift_down(gate, halo, 1)
                    + fcw_ref[0:1, cols] * _shift_down(gate, halo, 2))
            halo_s[:, cols] = gate[tb - 8:]
            gl, dgl = _gelu_and_grad(conv)
            gate_ref[:, cols] = gate.astype(BF16)
            act_ref[:, cols] = (gl * val).astype(BF16)
            gl_ref[:, cols] = gl.astype(BF16)
            vdgl_ref[:, cols] = (val * dgl).astype(BF16)

    vec = _resident((1, D_MODEL))
    wide = jax.ShapeDtypeStruct((S, D_FF), BF16)
    return _pcall(
        body, (x, att, rec, w_out, ln1_g, ln1_b, w_up, fcw, fcb), name="mix_ln1_up", grid=(S // tb,),
        sem="arbitrary", comm=comm,
        in_specs=[_rows(tb, D_MODEL), _rows(tb, D_ATT), _rows(tb, D_RNN), _resident((D_MODEL, D_MODEL)), vec, vec,
                  _resident(w_up.shape), _resident((3, D_FF)), _resident((1, D_FF))],
        out_specs=[_rows(tb, D_MODEL), _rows(tb, D_MODEL), _rows(tb, D_MODEL)] + [_rows(tb, D_FF)] * 4,
        out_shape=[jax.ShapeDtypeStruct((S, D_MODEL), F32), jax.ShapeDtypeStruct((S, D_MODEL), F32),
                   jax.ShapeDtypeStruct((S, D_MODEL), BF16), wide, wide, wide, wide],
        scratch_shapes=[pltpu.VMEM((8, D_FF), F32)])


def _tail(act, gl, vdgl, h1, h1b, p, tgt, w_down, w_pg, b_pg, w_pp, ln2_g, ln2_b):
    S = h1.shape[0]
    tb = min(256, S)

    def body(act_ref, gl_ref, vdgl_ref, h1_ref, h1b_ref, p_ref, t_ref, wd_ref, wpg_ref, bpg_ref, wpp_ref, g2_ref, b2_ref,
             dz2_ref, dpre_ref, dpp_ref, dgc_ref, dval_ref, dh1_ref, acc_ref):
        i = pl.program_id(0)

        @pl.when(i == 0)
        def _():
            acc_ref[...] = jnp.zeros_like(acc_ref)

        ffn = _dot(act_ref[...], wd_ref[...])
        h1 = h1_ref[...]
        sg = _sigmoid(_dot(h1b_ref[...], wpg_ref[...]) + bpg_ref[...])
        pp = _dot(p_ref[...].astype(BF16), wpp_ref[...])
        z2 = ALPHA * h1 + ffn + sg * pp
        xhat2, rstd2 = _ln_stats(z2)
        y = xhat2 * g2_ref[...] + b2_ref[...]
        err = y - t_ref[...]
        dy = err * (1.0 / D_MODEL)
        loss = 0.5 * jnp.sum(jnp.sum(err * err, axis=1, keepdims=True), axis=0, keepdims=True) * (1.0 / D_MODEL)
        dz2 = _ln_bwd(dy, xhat2, rstd2, g2_ref[...])
        dz2b = dz2.astype(BF16)
        dz2_ref[...] = dz2b
        dpre = dz2 * pp * sg * (1.0 - sg)
        dpreb = dpre.astype(BF16)
        dpre_ref[...] = dpreb
        dpp_ref[...] = (dz2 * sg).astype(BF16)
        dh1_ref[...] = ALPHA * dz2 + _dot_nt(dpreb, wpg_ref[...])
        dactb = _dot_nt(dz2b, wd_ref[...]).astype(BF16)
        dval_ref[...] = dactb * gl_ref[...]
        dgc_ref[...] = dactb * vdgl_ref[...]
        _put_rows(acc_ref, [_row_sum(dy * xhat2), _row_sum(dy), _row_sum(dpre),
                            jnp.broadcast_to(loss, (1, D_MODEL))])

    vec = _resident((1, D_MODEL))
    return pl.pallas_call(
        body, name="tail", grid=(S // tb,),
        in_specs=[_rows(tb, D_FF), _rows(tb, D_FF), _rows(tb, D_FF), _rows(tb, D_MODEL), _rows(tb, D_MODEL),
                  _rows(tb, PLE_DIM), _rows(tb, D_MODEL), _resident((D_FF, D_MODEL)), _resident((D_MODEL, D_MODEL)), vec,
                  _resident((PLE_DIM, D_MODEL)), vec, vec],
        out_specs=[_rows(tb, D_MODEL), _rows(tb, D_MODEL), _rows(tb, D_MODEL), _rows(tb, D_FF),
                   _rows(tb, D_FF), _rows(tb, D_MODEL), _acc((8, D_MODEL))],
        out_shape=[jax.ShapeDtypeStruct((S, D_MODEL), BF16),
                   jax.ShapeDtypeStruct((S, D_MODEL), BF16), jax.ShapeDtypeStruct((S, D_MODEL), BF16),
                   jax.ShapeDtypeStruct((S, D_FF), BF16), jax.ShapeDtypeStruct((S, D_FF), BF16),
                   jax.ShapeDtypeStruct((S, D_MODEL), F32), jax.ShapeDtypeStruct((8, D_MODEL), F32)],
        compiler_params=_params("arbitrary"),
    )(act, gl, vdgl, h1, h1b, p, tgt, w_down, w_pg, b_pg, w_pp, ln2_g, ln2_b)


def _weight_grad(a_list, b_list, name, layout, ts=512, comm=None):
    S = a_list[0].shape[0]
    ms = [a.shape[1] for a in a_list]
    M, nb, Nb = sum(ms), len(b_list), b_list[0].shape[1]
    ts = min(ts, S)
    nk = S // ts
    per_b = N_DEV // nb
    na = len(a_list)

    def body(*refs):
        a_refs, b_refs, o_ref, acc_ref = refs[:na], refs[na:na + nb], refs[na + nb], refs[na + nb + 1]
        j, k = pl.program_id(0), pl.program_id(1)

        @pl.when(k == 0)
        def _():
            acc_ref[...] = jnp.zeros_like(acc_ref)

        for jj in range(nb):
            @pl.when(j == jj)
            def _():
                b = b_refs[jj][...].astype(BF16)
                off = 0
                for a_ref, m in zip(a_refs, ms):
                    acc_ref[off:off + m, :] += _dot_tn(a_ref[...].astype(BF16), b)
                    off += m

        @pl.when(k == nk - 1)
        def _():
            for d in range(per_b):
                if layout == "rows":
                    o_ref[d] = acc_ref[d * (M // N_DEV):(d + 1) * (M // N_DEV), :].astype(BF16)
                elif layout == "cols":
                    o_ref[d] = acc_ref[:, d * (Nb // per_b):(d + 1) * (Nb // per_b)].astype(BF16)
                else:
                    o_ref[d] = acc_ref[:, d * (Nb // per_b):(d + 1) * (Nb // per_b)].T.astype(BF16)

    def b_index(jj):
        return lambda j, k: (jnp.where(j == jj, k, jnp.where(j < jj, 0, nk - 1)), 0)

    if layout == "rows":
        assert nb == 1
        blk = (N_DEV, M // N_DEV, Nb)
    elif layout == "cols":
        blk = (per_b, M, Nb // per_b)
    else:
        blk = (per_b, Nb // per_b, M)
    (res,), comm_res = _pcall(
        body, (*a_list, *b_list), name=name, grid=(nb, nk), sem="arbitrary", comm=comm, step_axis=1,
        in_specs=[pl.BlockSpec((ts, m), lambda j, k: (k, 0)) for m in ms]
        + [pl.BlockSpec((ts, Nb), b_index(jj)) for jj in range(nb)],
        out_specs=[pl.BlockSpec(blk, lambda j, k: (j, 0, 0))],
        out_shape=[jax.ShapeDtypeStruct((N_DEV,) + blk[1:], BF16)],
        scratch_shapes=[pltpu.VMEM((M, Nb), F32)])
    return (res, comm_res) if comm is not None else res


def _up_bwd(dgc, gate, dval, dh1p, z1, w_up, fcw, w_out, ln1_g, comm=None):
    S = z1.shape[0]
    tb = min(256, S)
    t16 = tb // 16
    n16 = S // 16
    nblk, _, wblk = w_up.shape
    half = nblk // 2
    nsteps = S // tb

    def body(dgc_ref, dgn_ref, gc_ref, gp_ref, dval_ref, dh1p_ref, z1_ref, wu_ref, fcw_ref, wo_ref, g1_ref,
             dgate_ref, dz1_ref, dz1b_ref, datt_ref, drec_ref, accf_ref, accd_ref):
        i = pl.program_id(0)

        @pl.when(i == 0)
        def _():
            accf_ref[...] = jnp.zeros_like(accf_ref)
            accd_ref[...] = jnp.zeros_like(accd_ref)

        dg = dgc_ref[...].astype(F32)
        nxt = jnp.where(i < nsteps - 1, dgn_ref[...].astype(F32)[0:8], 0.0)
        w = _w_rows(fcw_ref)
        dgate = (w[2] * dg + w[1] * _shift_up(dg, nxt, 1) + w[0] * _shift_up(dg, nxt, 2)).astype(BF16)
        dgate_ref[...] = dgate
        gate = gc_ref[...].astype(F32)
        halo = jnp.where(i > 0, gp_ref[...].astype(F32)[8:16], 0.0)
        _put_rows(accf_ref, [_row_sum(dg * _shift_down(gate, halo, 2)), _row_sum(dg * _shift_down(gate, halo, 1)),
                             _row_sum(dg * gate), _row_sum(dg)])

        dh1 = dh1p_ref[...]
        for j in range(nblk):
            src = dgate if j < half else dval_ref[...]
            jj = j % half
            dh1 = dh1 + _dot_nt(src[:, jj * wblk:(jj + 1) * wblk], wu_ref[j])
        xhat1, rstd1 = _ln_stats(z1_ref[...])
        dz1 = _ln_bwd(dh1, xhat1, rstd1, g1_ref[...])
        dz1_ref[...] = dz1
        dz1b = dz1.astype(BF16)
        dz1b_ref[...] = dz1b
        dcat = _dot_nt(dz1b, wo_ref[...])
        datt_ref[...] = dcat[:, :D_ATT].astype(BF16)
        drec_ref[...] = dcat[:, D_ATT:]
        _put_rows(accd_ref, [_row_sum(dh1 * xhat1), _row_sum(dh1)])

    prev16 = pl.BlockSpec((16, D_FF), lambda i: (jnp.maximum(i * t16 - 1, 0), 0))
    next16 = pl.BlockSpec((16, D_FF), lambda i: (jnp.minimum((i + 1) * t16, n16 - 1), 0))
    return _pcall(
        body, (dgc, dgc, gate, gate, dval, dh1p, z1, w_up, fcw, w_out, ln1_g), name="up_bwd",
        grid=(nsteps,), sem="arbitrary", comm=comm,
        in_specs=[_rows(tb, D_FF), next16, _rows(tb, D_FF), prev16, _rows(tb, D_FF), _rows(tb, D_MODEL),
                  _rows(tb, D_MODEL), _resident(w_up.shape), _resident((3, D_FF)),
                  _resident((D_MODEL, D_MODEL)), _resident((1, D_MODEL))],
        out_specs=[_rows(tb, D_FF), _rows(tb, D_MODEL), _rows(tb, D_MODEL), _rows(tb, D_ATT), _rows(tb, D_RNN),
                   _acc((8, D_FF)), _acc((8, D_MODEL))],
        out_shape=[jax.ShapeDtypeStruct((S, D_FF), BF16), jax.ShapeDtypeStruct((S, D_MODEL), F32),
                   jax.ShapeDtypeStruct((S, D_MODEL), BF16), jax.ShapeDtypeStruct((S, D_ATT), BF16),
                   jax.ShapeDtypeStruct((S, D_RNN), F32), jax.ShapeDtypeStruct((8, D_FF), F32),
                   jax.ShapeDtypeStruct((8, D_MODEL), F32)])


def _attn_bwd(q, k, v, do, sinks, comm=None):
    S = q.shape[0]
    grp = N_HEADS // N_KV

    def body(sink_ref, q_ref, kc_ref, kp_ref, vc_ref, vp_ref, do_ref, dq_ref, dkc_ref, dkp_ref, dvc_ref, dvp_ref,
             ds_ref):
        i = pl.program_id(0)

        @pl.when(i == 0)
        def _():
            ds_ref[...] = jnp.zeros_like(ds_ref)

        valid = _band_mask(i)
        row8 = lax.broadcasted_iota(jnp.int32, (8, 128), 0)
        lane8 = lax.broadcasted_iota(jnp.int32, (8, 128), 1)
        dqs, dks, dvs = [], [], []
        dsink = jnp.zeros((8, 128), F32)
        qv = q_ref[...]
        dov = do_ref[...]
        kall = jnp.concatenate([kp_ref[...], kc_ref[...]], axis=0)
        vall = jnp.concatenate([vp_ref[...], vc_ref[...]], axis=0)
        for g in range(N_KV):
            kcat = kall[:, g * HEAD_DIM:(g + 1) * HEAD_DIM]
            vcat = vall[:, g * HEAD_DIM:(g + 1) * HEAD_DIM]
            q4, do4 = _stack_heads(qv, g), _stack_heads(dov, g)
            s = jnp.where(valid, _dot_nt(q4, kcat), -1e30)
            sink = _sink_column(sink_ref, g)
            m = jnp.maximum(jnp.max(s, axis=1, keepdims=True), sink)
            e = jnp.exp(s - m)
            es = jnp.exp(sink - m)
            inv = 1.0 / (jnp.sum(e, axis=1, keepdims=True) + es)
            p = e * inv
            dp = _dot_nt(do4, vcat)
            delta = jnp.sum(p * dp, axis=1, keepdims=True)
            dsc = (p * (dp - delta)).astype(BF16)
            dqs += _unstack_heads(_dot(dsc, kcat) * (HEAD_DIM ** -0.5))
            dks.append(_dot_tn(q4, dsc))
            dvs.append(_dot_tn(do4, p.astype(BF16)))
            for hh, part in enumerate(_unstack_heads(-es * inv * delta)):
                here = (row8 == 0) & (lane8 == g * grp + hh)
                dsink = dsink + jnp.where(here, jnp.sum(part, axis=0, keepdims=True), 0.0)
        dq_ref[...] = jnp.concatenate(dqs, axis=1).astype(BF16)
        dk = jnp.concatenate(dks, axis=0).T
        dv = jnp.concatenate(dvs, axis=0).T
        dkp_ref[...] = dk[:QBLK]
        dkc_ref[...] = dk[QBLK:]
        dvp_ref[...] = dv[:QBLK]
        dvc_ref[...] = dv[QBLK:]
        ds_ref[...] += dsink

    kvs = jax.ShapeDtypeStruct((S, D_KV), F32)
    return _pcall(
        body, (sinks, q, k, k, v, v, do), name="attn_bwd", grid=(S // QBLK,), sem="arbitrary", comm=comm,
        in_specs=[pl.BlockSpec(memory_space=pltpu.SMEM), _rows(QBLK, D_ATT)] + _attn_specs() + [_rows(QBLK, D_ATT)],
        out_specs=[_rows(QBLK, D_ATT), _rows(QBLK, D_KV), _rows(QBLK, D_KV), _rows(QBLK, D_KV), _rows(QBLK, D_KV),
                   _acc((8, 128))],
        out_shape=[jax.ShapeDtypeStruct((S, D_ATT), BF16), kvs, kvs, kvs, kvs, jax.ShapeDtypeStruct((8, 128), F32)])


def _rnn_bwd(xr, gr, h, drec, conv_w, conv_b, wa, wx, ba, bx, lam, comm=None):
    S = xr.shape[0]
    tb = min(256, S)
    t8 = tb // 8
    nsteps = S // tb

    def body(xr_ref, xp_ref, gr_ref, h_ref, hp_ref, drec_ref, cw_ref, cb_ref, wa_ref, wx_ref, ba_ref, bx_ref, lam_ref,
             dxr_ref, dgr_ref, gwa_ref, gwx_ref, acc_ref, carry_s, dxc_halo_s, a_s, d_s, gwa_s, gwx_s):
        i = pl.program_id(0)
        blk = nsteps - 1 - i

        @pl.when(i == 0)
        def _():
            gwa_s[...] = jnp.zeros_like(gwa_s)
            gwx_s[...] = jnp.zeros_like(gwx_s)
            acc_ref[...] = jnp.zeros_like(acc_ref)
            carry_s[...] = jnp.zeros_like(carry_s)
            dxc_halo_s[...] = jnp.zeros_like(dxc_halo_s)

        x = xr_ref[...]
        xhalo = jnp.where(blk > 0, xp_ref[...], 0.0)
        cw = _w_rows(cw_ref)
        xs = [_shift_down(x, xhalo, 3), _shift_down(x, xhalo, 2), _shift_down(x, xhalo, 1), x]
        xc = cb_ref[...] + cw[0] * xs[0] + cw[1] * xs[1] + cw[2] * xs[2] + cw[3] * xs[3]
        sp = _softplus_neg(lam_ref[...])
        r, ig, a, f = _rnn_gates(xc, wa_ref[...], wx_ref[...], ba_ref[...], bx_ref[...], sp)
        hcur = h_ref[...]
        hprev = _shift_down(hcur, jnp.where(blk > 0, hp_ref[...], 0.0), 1)
        gl, dgl = _gelu_and_grad(gr_ref[...])
        drec = drec_ref[...]
        dgr_ref[...] = (drec * hcur * dgl).astype(BF16)
        a_s[...] = a
        d_s[...] = drec * gl
        row8 = lax.broadcasted_iota(jnp.int32, (8, D_RNN), 0)

        def tile(t, c):
            o = pl.multiple_of((t8 - 1 - t) * 8, 8)
            a8 = a_s[pl.ds(o, 8), :]
            dt = d_s[pl.ds(o, 8), :]
            at = jnp.where(row8 == 7, 1.0, pltpu.roll(a8, 7, 0))
            for s in (1, 2, 4):
                keep = row8 < 8 - s
                a_sh = jnp.where(keep, pltpu.roll(at, 8 - s, 0), 1.0)
                d_sh = jnp.where(keep, pltpu.roll(dt, 8 - s, 0), 0.0)
                dt = at * d_sh + dt
                at = at * a_sh
            lt = at * c + dt
            d_s[pl.ds(o, 8), :] = lt
            return _row_sum(jnp.where(row8 == 0, a8 * lt, 0.0))

        carry_s[0:1, :] = lax.fori_loop(0, t8, tile, carry_s[0:1, :])
        lmb = d_s[...]
        a2 = a * a
        dla = lmb * hprev * a - lmb * ig * xc * (a2 / f)
        di = lmb * f * xc
        dr = dla * (-LRU_C) * sp
        dpa = dr * r * (1.0 - r)
        dpx = di * ig * (1.0 - ig)
        dpab = dpa.astype(BF16)
        dpxb = dpx.astype(BF16)
        xcb = xc.astype(BF16)
        gwa_s[...] += _dot_tn(xcb, dpab)
        gwx_s[...] += _dot_tn(xcb, dpxb)

        @pl.when(i == nsteps - 1)
        def _():
            for dense, out in ((gwa_s[...], gwa_ref), (gwx_s[...], gwx_ref)):
                for b in range(RNN_BLOCKS):
                    rows = slice(b * HEAD_DIM, (b + 1) * HEAD_DIM)
                    out[rows, :] = dense[rows, b * HEAD_DIM:(b + 1) * HEAD_DIM]

        dxc = lmb * f * ig + _dot_nt(dpab, wa_ref[...]) + _dot_nt(dpxb, wx_ref[...])
        nxt = dxc_halo_s[...]
        dxr = cw[3] * dxc
        for s in (1, 2, 3):
            dxr = dxr + cw[3 - s] * _shift_up(dxc, nxt, s)
        dxr_ref[...] = dxr.astype(BF16)
        dxc_halo_s[...] = dxc[:8]
        dlam = _row_sum(dla * (-LRU_C) * r) * (-1.0 / (1.0 + jnp.exp(lam_ref[...])))
        _put_rows(acc_ref, [_row_sum(dxc * xs[0]), _row_sum(dxc * xs[1]), _row_sum(dxc * xs[2]), _row_sum(dxc * xs[3]),
                            _row_sum(dxc), _row_sum(dpa), _row_sum(dpx), dlam])

    rev = lambda i: (nsteps - 1 - i, 0)
    prev8 = lambda i: (jnp.maximum((nsteps - 1 - i) * t8 - 1, 0), 0)
    blkspec = pl.BlockSpec((tb, D_RNN), rev)
    halo8 = pl.BlockSpec((8, D_RNN), prev8)
    vec = _resident((1, D_RNN))
    return _pcall(
        body, (xr, xr, gr, h, h, drec, conv_w, conv_b, wa, wx, ba, bx, lam), name="rnn_bwd", grid=(nsteps,),
        sem="arbitrary", comm=comm,
        in_specs=[blkspec, halo8, blkspec, blkspec, halo8, blkspec, _resident((4, D_RNN)), vec,
                  _resident((D_RNN, D_RNN)), _resident((D_RNN, D_RNN)), vec, vec, vec],
        out_specs=[blkspec, blkspec, _acc((D_RNN, HEAD_DIM)), _acc((D_RNN, HEAD_DIM)), _acc((8, D_RNN))],
        out_shape=[jax.ShapeDtypeStruct((S, D_RNN), BF16), jax.ShapeDtypeStruct((S, D_RNN), BF16),
                   jax.ShapeDtypeStruct((D_RNN, HEAD_DIM), F32), jax.ShapeDtypeStruct((D_RNN, HEAD_DIM), F32),
                   jax.ShapeDtypeStruct((8, D_RNN), F32)],
        scratch_shapes=[pltpu.VMEM((8, D_RNN), F32), pltpu.VMEM((8, D_RNN), F32),
                        pltpu.VMEM((tb, D_RNN), F32), pltpu.VMEM((tb, D_RNN), F32),
                        pltpu.VMEM((D_RNN, D_RNN), F32), pltpu.VMEM((D_RNN, D_RNN), F32)])


def _in_bwd(dq, dkc, dkp, dvc, dvp, dxr, dgr, dz1, w_in, comm=None):
    S = dz1.shape[0]
    tb = min(512, S)
    nsteps = S // tb
    nq = S // QBLK
    r = tb // QBLK

    def body(dq_ref, dkc_ref, dkp_ref, dkn_ref, dvc_ref, dvp_ref, dvn_ref, dxr_ref, dgr_ref, dz1_ref, w_ref,
             du_ref, dx_ref):
        i = pl.program_id(0)
        last = i == nsteps - 1

        def shifted(prev_ref, next_ref):
            nxt = jnp.where(last, 0.0, next_ref[...])
            return jnp.concatenate([prev_ref[QBLK:], nxt], axis=0) if r > 1 else nxt

        dk = (dkc_ref[...] + shifted(dkp_ref, dkn_ref)).astype(BF16)
        dv = (dvc_ref[...] + shifted(dvp_ref, dvn_ref)).astype(BF16)
        du = jnp.concatenate([dq_ref[...], dk, dv, dxr_ref[...], dgr_ref[...]], axis=1)
        du_ref[...] = du
        dx_ref[...] = ALPHA * dz1_ref[...] + _dot_nt(du, w_ref[...])

    nextq = pl.BlockSpec((QBLK, D_KV), lambda i: (jnp.minimum((i + 1) * r, nq - 1), 0))
    return _pcall(
        body, (dq, dkc, dkp, dkp, dvc, dvp, dvp, dxr, dgr, dz1, w_in), name="in_bwd", grid=(nsteps,), comm=comm,
        in_specs=[_rows(tb, D_ATT), _rows(tb, D_KV), _rows(tb, D_KV), nextq, _rows(tb, D_KV), _rows(tb, D_KV), nextq,
                  _rows(tb, D_RNN), _rows(tb, D_RNN), _rows(tb, D_MODEL), _resident((D_MODEL, D_IN))],
        out_specs=[_rows(tb, D_IN), _rows(tb, D_MODEL)],
        out_shape=[jax.ShapeDtypeStruct((S, D_IN), BF16), jax.ShapeDtypeStruct((S, D_MODEL), F32)])


def _block_diag(w):
    eye = jnp.eye(RNN_BLOCKS, dtype=w.dtype)
    return (w[:, :, None, :] * eye[:, None, :, None]).reshape(D_RNN, D_RNN).astype(BF16)


def _adamw(w, g, m, v):
    m = ADAM_B1 * m + (1.0 - ADAM_B1) * g
    v = ADAM_B2 * v + (1.0 - ADAM_B2) * (g * g)
    m_hat = m / (1.0 - ADAM_B1 ** ADAM_STEP)
    v_hat = v / (1.0 - ADAM_B2 ** ADAM_STEP)
    delta = -ADAM_LR * (m_hat / (jnp.sqrt(v_hat) + ADAM_EPS) + ADAM_WD * w)
    return delta, m, v


def _sum_adamw(parts, w, m, v, name):
    R, C = w.shape
    rb = 128
    assert R % rb == 0

    def body(p_ref, w_ref, m_ref, v_ref, g_out, d_out, m_out, v_out):
        g = p_ref[0].astype(F32)
        for d in range(1, N_DEV):
            g = g + p_ref[d].astype(F32)
        delta, mn, vn = _adamw(w_ref[...], g, m_ref[...], v_ref[...])
        g_out[...] = g
        d_out[...] = delta
        m_out[...] = mn
        v_out[...] = vn

    blk = _rows(rb, C)
    out = jax.ShapeDtypeStruct((R, C), F32)
    return pl.pallas_call(
        body, name=name, grid=(R // rb,),
        in_specs=[pl.BlockSpec((N_DEV, rb, C), lambda i: (0, i, 0)), blk, blk, blk],
        out_specs=[blk, blk, blk, blk], out_shape=[out, out, out, out],
        compiler_params=_params("parallel"),
    )(parts, w, m, v)


_SMALL = [("attn_sinks", "s", 0, 1, None), ("rnn_conv_w", "r", 0, 4, "cols"), ("rnn_conv_b", "r", 4, 1, None),
          ("gate_a_w", "a", 0, D_RNN, None), ("gate_a_b", "r", 5, 1, None), ("gate_x_w", "x", 0, D_RNN, None),
          ("gate_x_b", "r", 6, 1, None), ("lru_lambda", "r", 7, 1, None), ("ln1_g", "d", 0, 1, None),
          ("ln1_b", "d", 1, 1, None), ("ffn_conv_w", "f", 0, 3, "cols"), ("ffn_conv_b", "f", 3, 1, None),
          ("ple_gate_b", "t", 2, 1, None), ("ln2_g", "t", 0, 1, None), ("ln2_b", "t", 1, 1, None)]
_LOSS_ROW = 3


def _small_update(gathered, params):
    keys = "tfdsrax"
    flat = [arr for triple in params for arr in triple]
    n_par = len(_SMALL)

    def body(*refs):
        g_refs = dict(zip(keys, refs[:7]))
        p_refs = refs[7:7 + 3 * n_par]
        loss_ref = refs[7 + 3 * n_par]
        o_refs = refs[8 + 3 * n_par:8 + 7 * n_par]
        tot = dict(zip(keys, refs[8 + 7 * n_par:8 + 7 * n_par + 7]))
        tmp_r, tmp_f = refs[8 + 7 * n_par + 7:]
        me = _dev_index(*_place())
        for key in keys:
            s = g_refs[key][0]
            for d in range(1, N_DEV):
                s = s + g_refs[key][d]
            tot[key][...] = s
        loss_ref[...] = tot["t"][_LOSS_ROW:_LOSS_ROW + 1, 0:128]
        for i, (name, key, row, rows, how) in enumerate(_SMALL):
            w_ref, m_ref, v_ref = p_refs[3 * i:3 * i + 3]
            g_out, d_out, m_out, v_out = o_refs[4 * i:4 * i + 4]
            if how == "cols":
                full = tot[key][...]
                width = full.shape[1] // N_DEV
                mine = full[:, :width]
                for d in range(1, N_DEV):
                    mine = jnp.where(me == d, full[:, d * width:(d + 1) * width], mine)
                tmp = tmp_r if key == "r" else tmp_f
                tmp[...] = mine
                g = tmp[row:row + rows, :]
            else:
                g = tot[key][row:row + rows, :]
                g = g[:, :w_ref.shape[1]]
            delta, mn, vn = _adamw(w_ref[...], g, m_ref[...], v_ref[...])
            g_out[...] = g
            d_out[...] = delta
            m_out[...] = mn
            v_out[...] = vn

    outs = [jax.ShapeDtypeStruct((1, 128), F32)]
    for w, _, _ in params:
        outs += [jax.ShapeDtypeStruct(w.shape, F32)] * 4
    scratch = [pltpu.VMEM(g.shape[1:], F32) for g in gathered]
    scratch += [pltpu.VMEM((8, D_RNN // N_DEV), F32), pltpu.VMEM((8, D_FF // N_DEV), F32)]
    res = pl.pallas_call(body, name="small_update", out_shape=outs, scratch_shapes=scratch)(*gathered, *flat)
    return res[0], [res[1 + 4 * i:5 + 4 * i] for i in range(n_par)]


def kernel(x, p, w_in, attn_sinks, rnn_conv_w, rnn_conv_b, gate_a_w, gate_a_b, gate_x_w, gate_x_b, lru_lambda, w_out, ln1_g, ln1_b, w_ffn_up, ffn_conv_w, ffn_conv_b, w_ffn_down, ple_gate_w, ple_gate_b, ple_proj, ln2_g, ln2_b, loss_target, m_w_in, m_attn_sinks, m_rnn_conv_w, m_rnn_conv_b, m_gate_a_w, m_gate_a_b, m_gate_x_w, m_gate_x_b, m_lru_lambda, m_w_out, m_ln1_g, m_ln1_b, m_w_ffn_up, m_ffn_conv_w, m_ffn_conv_b, m_w_ffn_down, m_ple_gate_w, m_ple_gate_b, m_ple_proj, m_ln2_g, m_ln2_b, v_w_in, v_attn_sinks, v_rnn_conv_w, v_rnn_conv_b, v_gate_a_w, v_gate_a_b, v_gate_x_w, v_gate_x_b, v_lru_lambda, v_w_out, v_ln1_g, v_ln1_b, v_w_ffn_up, v_ffn_conv_w, v_ffn_conv_b, v_w_ffn_down, v_ple_gate_w, v_ple_gate_b, v_ple_proj, v_ln2_g, v_ln2_b):
    from_col_blocks = lambda g: g.transpose(1, 0, 2).reshape(g.shape[1], N_DEV * g.shape[2])

    xs, ps, tgt, sinks = x[0], p[0, 0], loss_target[0], attn_sinks[0]
    wa, wx = _block_diag(gate_a_w[0]), _block_diag(gate_x_w[0])

    conv_cols = jnp.concatenate([rnn_conv_w[0].reshape(1, -1), ffn_conv_w[0].reshape(1, -1)], axis=1)
    n_rc, n_fc = 4 * D_RNN // N_DEV, 3 * D_FF // N_DEV
    ((g_in, g_conv),) = _comm_call([_Gather([w_in[0].astype(BF16), jnp.broadcast_to(conv_cols, (8, n_rc + n_fc))])],
                                   "gather_w_in")
    w_in_full = from_col_blocks(g_in)
    rcw = from_col_blocks(g_conv[:, 0, :n_rc].reshape(N_DEV, 4, D_RNN // N_DEV))
    fcw = from_col_blocks(g_conv[:, 0, n_rc:].reshape(N_DEV, 3, D_FF // N_DEV))

    q, k, v, xr, gr = _in_proj(xs, w_in_full)
    (att,), (w_up,) = _attn_fwd(q, k, v, sinks, comm=_Gather([w_ffn_up[0].astype(BF16)]))
    (rec, h), (g_out, g_down) = _rnn_fwd(xr, gr, rcw, rnn_conv_b, wa, wx, gate_a_b, gate_x_b, lru_lambda,
                                         comm=_Gather([w_out[0].astype(BF16), w_ffn_down[0].astype(BF16)]))
    w_out_full = g_out.reshape(D_MODEL, D_MODEL)
    (z1, h1, h1b, gate, act, gl, vdgl), (g_pg, g_pp) = _mix_ln1_up(
        xs, att, rec, w_out_full, ln1_g, ln1_b, w_up, fcw, ffn_conv_b,
        comm=_Gather([ple_gate_w[0].astype(BF16), ple_proj[0].astype(BF16)]))
    dz2b, dpreb, dppb, dgc, dval, dh1p, acc_t = _tail(
        act, gl, vdgl, h1, h1b, ps, tgt, g_down.reshape(D_FF, D_MODEL), g_pg.reshape(D_MODEL, D_MODEL), ple_gate_b,
        from_col_blocks(g_pp), ln2_g, ln2_b)

    gd_down = _weight_grad([dz2b], [act], "down_grad", "rows_t")
    gd_pg = _weight_grad([h1b], [dpreb], "pg_grad", "rows", ts=1024)
    gd_pp = _weight_grad([ps], [dppb], "pp_grad", "cols", ts=1024)
    (dgate, dz1, dz1b, datt, drec, acc_f, acc_d), (r_down, r_pg, r_pp) = _up_bwd(
        dgc, gate, dval, dh1p, z1, w_up, fcw, w_out_full, ln1_g, comm=_Exchange([gd_down, gd_pg, gd_pp]))
    gd_up = _weight_grad([h1b], [dgate, dval], "up_grad", "cols")
    gd_out = _weight_grad([att, rec], [dz1b], "out_grad", "rows", ts=1024)
    (dq, dkc, dkp, dvc, dvp, acc_s), (r_up,) = _attn_bwd(q, k, v, datt, sinks, comm=_Exchange([gd_up]))
    (dxr, dgr, g_wa, g_wx, acc_r), (r_out,) = _rnn_bwd(xr, gr, h, drec, rcw, rnn_conv_b, wa, wx, gate_a_b, gate_x_b,
                                                       lru_lambda, comm=_Exchange([gd_out]))
    (du, dx), _ = _in_bwd(dq, dkc, dkp, dvc, dvp, dxr, dgr, dz1, w_in_full)
    gd_in = _weight_grad([xs], [du], "in_grad", "cols", ts=1024)
    (r_in,), small_parts = _comm_call([_Exchange([gd_in]), _Gather([acc_t, acc_f, acc_d, acc_s, acc_r, g_wa, g_wx])],
                                      "exchange_w_in")

    outs = {}
    for name, parts, w, m, v in [("w_in", r_in, w_in, m_w_in, v_w_in), ("w_out", r_out, w_out, m_w_out, v_w_out),
                                 ("w_ffn_up", r_up, w_ffn_up, m_w_ffn_up, v_w_ffn_up),
                                 ("w_ffn_down", r_down, w_ffn_down, m_w_ffn_down, v_w_ffn_down),
                                 ("ple_gate_w", r_pg, ple_gate_w, m_ple_gate_w, v_ple_gate_w),
                                 ("ple_proj", r_pp, ple_proj, m_ple_proj, v_ple_proj)]:
        res = _sum_adamw(parts, w[0], m[0], v[0], "adamw_" + name)
        outs[name] = [r[None] for r in res]

    given = dict(attn_sinks=(attn_sinks, m_attn_sinks, v_attn_sinks), rnn_conv_w=(rnn_conv_w, m_rnn_conv_w, v_rnn_conv_w),
                 rnn_conv_b=(rnn_conv_b, m_rnn_conv_b, v_rnn_conv_b), gate_a_w=(gate_a_w, m_gate_a_w, v_gate_a_w),
                 gate_a_b=(gate_a_b, m_gate_a_b, v_gate_a_b), gate_x_w=(gate_x_w, m_gate_x_w, v_gate_x_w),
                 gate_x_b=(gate_x_b, m_gate_x_b, v_gate_x_b), lru_lambda=(lru_lambda, m_lru_lambda, v_lru_lambda),
                 ln1_g=(ln1_g, m_ln1_g, v_ln1_g), ln1_b=(ln1_b, m_ln1_b, v_ln1_b),
                 ffn_conv_w=(ffn_conv_w, m_ffn_conv_w, v_ffn_conv_w), ffn_conv_b=(ffn_conv_b, m_ffn_conv_b, v_ffn_conv_b),
                 ple_gate_b=(ple_gate_b, m_ple_gate_b, v_ple_gate_b), ln2_g=(ln2_g, m_ln2_g, v_ln2_g),
                 ln2_b=(ln2_b, m_ln2_b, v_ln2_b))
    as_2d = lambda a: a.reshape(-1, a.shape[-1])
    loss_row, small_res = _small_update(small_parts, [tuple(as_2d(a) for a in given[n]) for n, *_ in _SMALL])
    loss = loss_row[0, 0]
    for (n, *_), res in zip(_SMALL, small_res):
        outs[n] = [r.reshape(given[n][0].shape) for r in res]

    order = ["w_in", "attn_sinks", "rnn_conv_w", "rnn_conv_b", "gate_a_w", "gate_a_b", "gate_x_w", "gate_x_b",
             "lru_lambda", "w_out", "ln1_g", "ln1_b", "w_ffn_up", "ffn_conv_w", "ffn_conv_b", "w_ffn_down",
             "ple_gate_w", "ple_gate_b", "ple_proj", "ln2_g", "ln2_b"]
    return (loss, dx[None], *[outs[n][0] for n in order], *[outs[n][1] for n in order],
            *[outs[n][2] for n in order], *[outs[n][3] for n in order])
```

```python
import jax
import jax.numpy as jnp
from jax import lax
from jax.experimental import pallas as pl
from jax.experimental.pallas import tpu as pltpu

F32 = jnp.float32
BF16 = jnp.bfloat16

D_MODEL = 1024
D_ATT = 512
D_KV = 128
HEAD_DIM = 64
N_HEADS = 8
N_KV = 2
D_RNN = 512
RNN_BLOCKS = 8
D_IN = 1792
D_FF = 3072
PLE_DIM = 256
QBLK = 128
N_DEV = 8
ALPHA = float(2 ** 0.25)
LN_EPS = 1e-5
LRU_C = 8.0
ADAM_LR, ADAM_B1, ADAM_B2, ADAM_EPS, ADAM_WD, ADAM_STEP = 0.001, 0.9, 0.999, 1e-08, 0.01, 10

V7X_VMEM_LIMIT = 56 * 1024 * 1024
MESH = pl.DeviceIdType.MESH


def _params(*sem, vmem=V7X_VMEM_LIMIT):
    return pltpu.CompilerParams(dimension_semantics=sem or None, vmem_limit_bytes=vmem)


def _resident(shape):
    return pl.BlockSpec(shape, lambda *_: (0,) * len(shape), pipeline_mode=pl.Buffered(1))


def _rows(tb, cols):
    return pl.BlockSpec((tb, cols), lambda i: (i, 0))


def _acc(shape):
    return pl.BlockSpec(shape, lambda *_: (0,) * len(shape))


def _dot(a, b):
    return jnp.dot(a, b, preferred_element_type=F32)


def _dot_nt(a, b):
    return lax.dot_general(a, b, (((1,), (1,)), ((), ())), preferred_element_type=F32)


def _dot_tn(a, b):
    return lax.dot_general(a, b, (((0,), (0,)), ((), ())), preferred_element_type=F32)


def _sigmoid(x):
    return 1.0 / (1.0 + jnp.exp(-x))


_GELU_C = 0.7978845608028654
_GELU_K = 0.044715


def _gelu_and_grad(x):
    u = x * x
    t = jnp.tanh(x * (_GELU_C + (_GELU_C * _GELU_K) * u))
    hp = 0.5 + 0.5 * t
    dg = hp + x * (0.5 - 0.5 * (t * t)) * (_GELU_C + (3.0 * _GELU_C * _GELU_K) * u)
    return x * hp, dg


def _gelu(x):
    return 0.5 * x * (1.0 + jnp.tanh(_GELU_C * (x + _GELU_K * x * x * x)))


def _ln_stats(z):
    mu = jnp.mean(z, axis=-1, keepdims=True)
    zc = z - mu
    var = jnp.mean(zc * zc, axis=-1, keepdims=True)
    rstd = lax.rsqrt(var + LN_EPS)
    return zc * rstd, rstd


def _ln_bwd(dy, xhat, rstd, g):
    dxh = dy * g
    m1 = jnp.mean(dxh, axis=-1, keepdims=True)
    m2 = jnp.mean(dxh * xhat, axis=-1, keepdims=True)
    return rstd * (dxh - m1 - xhat * m2)


def _softplus_neg(lam):
    u = jnp.exp(-jnp.abs(lam))
    w = 1.0 + u
    d = w - 1.0
    log1p_u = jnp.where(d == 0.0, u, jnp.log(w) * (u / jnp.where(d == 0.0, 1.0, d)))
    return jnp.maximum(-lam, 0.0) + log1p_u


def _shift_down(x, halo, s):
    xs = pltpu.roll(x, s, 0)
    hs = pltpu.roll(halo, s, 0)
    row8 = lax.broadcasted_iota(jnp.int32, hs.shape, 0)
    first = jnp.where(row8 < s, hs, xs[:8])
    return jnp.concatenate([first, xs[8:]], axis=0)


def _shift_up(x, halo, s):
    n = x.shape[0]
    xs = pltpu.roll(x, n - s, 0)
    hs = pltpu.roll(halo, 8 - s, 0)
    row8 = lax.broadcasted_iota(jnp.int32, hs.shape, 0)
    last = jnp.where(row8 >= 8 - s, hs, xs[n - 8:])
    return jnp.concatenate([xs[:n - 8], last], axis=0)


def _row_sum(x):
    return jnp.sum(x, axis=0, keepdims=True)


def _put_rows(acc_ref, rows):
    row8 = lax.broadcasted_iota(jnp.int32, acc_ref.shape, 0)
    upd = jnp.zeros(acc_ref.shape, F32)
    for r, vec in enumerate(rows):
        upd = jnp.where(row8 == r, vec, upd)
    acc_ref[...] += upd


def _place():
    return lax.axis_index("x"), lax.axis_index("y"), lax.axis_index("c")


def _dev_index(px, py, pc):
    return 4 * px + 2 * py + pc


_ANY = pl.BlockSpec(memory_space=pl.ANY)


class _Gather:
    def __init__(self, arrays):
        self.arrays = list(arrays)
        self.n = len(self.arrays)

    def out_shape(self):
        return [jax.ShapeDtypeStruct((N_DEV,) + s.shape, s.dtype) for s in self.arrays]

    def scratch(self):
        return [pltpu.SemaphoreType.DMA((self.n, 7)), pltpu.SemaphoreType.DMA((self.n, 7)),
                pltpu.SemaphoreType.DMA((self.n,))]

    def _parts(self, ins, outs, sems):
        send_sems, recv_sems, local_sems = sems
        x, y, c = _place()
        me, sibling = (x, y, c), (x, y, 1 - c)
        chips = [(1 - x, y), (x, 1 - y), (1 - x, 1 - y)]

        def copy(a, k, block, to, src=None):
            rows = outs[a].at[_dev_index(*block)]
            return pltpu.make_async_remote_copy(
                src_ref=rows if src is None else src, dst_ref=rows, send_sem=send_sems.at[a, k],
                recv_sem=recv_sems.at[a, k], device_id=to, device_id_type=MESH)

        rng = range(self.n)
        mine = [pltpu.make_async_copy(ins[a], outs[a].at[_dev_index(*me)], local_sems.at[a]) for a in rng]
        first = [copy(a, 0, me, sibling, src=ins[a]) for a in rng]
        first += [copy(a, 1 + j, me, (*chip, c), src=ins[a]) for j, chip in enumerate(chips) for a in rng]
        landed = [copy(a, 1 + j, (*chip, c), me) for j, chip in enumerate(chips) for a in rng]
        passed = [copy(a, 4 + j, (*chip, c), sibling) for j, chip in enumerate(chips) for a in rng]
        from_sibling = [copy(a, 0, sibling, me) for a in rng]
        from_sibling += [copy(a, 4 + j, (*chip, 1 - c), me) for j, chip in enumerate(chips) for a in rng]
        return mine, first, landed, passed, from_sibling

    def start(self, ins, outs, sems):
        mine, first, _, _, _ = self._parts(ins, outs, sems)
        for cp in mine + first:
            cp.start()

    def forward(self, ins, outs, sems):
        _, _, landed, passed, _ = self._parts(ins, outs, sems)
        for got, fwd in zip(landed, passed):
            got.wait_recv()
            fwd.start()

    def finish(self, ins, outs, sems):
        mine, first, _, passed, from_sibling = self._parts(ins, outs, sems)
        for cp in from_sibling:
            cp.wait_recv()
        for cp in first + passed:
            cp.wait_send()
        for cp in mine:
            cp.wait()

    def before(self, ins, outs, sems, step, nsteps):
        pl.when(step == 0)(lambda: self.start(ins, outs, sems))
        pl.when(step == (7 * nsteps) // 8)(lambda: self.forward(ins, outs, sems))

    def after(self, ins, outs, sems, step, nsteps):
        pl.when(step == nsteps - 1)(lambda: self.finish(ins, outs, sems))


class _Exchange:
    def __init__(self, arrays):
        self.arrays = list(arrays)
        self.n = len(self.arrays)

    def out_shape(self):
        return [jax.ShapeDtypeStruct(b.shape, b.dtype) for b in self.arrays]

    def scratch(self):
        return [pltpu.SemaphoreType.DMA((self.n, 7)), pltpu.SemaphoreType.DMA((self.n, 7)),
                pltpu.SemaphoreType.DMA((self.n,))]

    def _parts(self, ins, outs, sems):
        send_sems, recv_sems, local_sems = sems
        x, y, c = _place()
        me = _dev_index(x, y, c)
        peers = [(x ^ (k >> 2), y ^ ((k >> 1) & 1), c ^ (k & 1)) for k in range(1, N_DEV)]
        rng = range(self.n)
        mine = [pltpu.make_async_copy(ins[a].at[me], outs[a].at[me], local_sems.at[a]) for a in rng]
        sent = [pltpu.make_async_remote_copy(
            src_ref=ins[a].at[_dev_index(*to)], dst_ref=outs[a].at[me], send_sem=send_sems.at[a, k],
            recv_sem=recv_sems.at[a, k], device_id=to, device_id_type=MESH) for k, to in enumerate(peers) for a in rng]
        arrivals = [pltpu.make_async_remote_copy(
            src_ref=ins[a].at[me], dst_ref=outs[a].at[_dev_index(*frm)], send_sem=send_sems.at[a, k],
            recv_sem=recv_sems.at[a, k], device_id=frm, device_id_type=MESH) for k, frm in enumerate(peers) for a in rng]
        return mine, sent, arrivals

    def start(self, ins, outs, sems):
        mine, sent, _ = self._parts(ins, outs, sems)
        for cp in mine + sent:
            cp.start()

    def finish(self, ins, outs, sems):
        mine, sent, arrivals = self._parts(ins, outs, sems)
        for cp in arrivals:
            cp.wait_recv()
        for cp in sent:
            cp.wait_send()
        for cp in mine:
            cp.wait()

    def before(self, ins, outs, sems, step, nsteps):
        pl.when(step == 0)(lambda: self.start(ins, outs, sems))

    def after(self, ins, outs, sems, step, nsteps):
        pl.when(step == nsteps - 1)(lambda: self.finish(ins, outs, sems))


class _Bcast(_Exchange):
    def out_shape(self):
        return [jax.ShapeDtypeStruct((N_DEV,) + s.shape, s.dtype) for s in self.arrays]

    def _parts(self, ins, outs, sems):
        send_sems, recv_sems, local_sems = sems
        x, y, c = _place()
        me = _dev_index(x, y, c)
        peers = [(x ^ (k >> 2), y ^ ((k >> 1) & 1), c ^ (k & 1)) for k in range(1, N_DEV)]
        rng = range(self.n)
        mine = [pltpu.make_async_copy(ins[a], outs[a].at[me], local_sems.at[a]) for a in rng]
        sent = [pltpu.make_async_remote_copy(
            src_ref=ins[a], dst_ref=outs[a].at[me], send_sem=send_sems.at[a, k], recv_sem=recv_sems.at[a, k],
            device_id=to, device_id_type=MESH) for k, to in enumerate(peers) for a in rng]
        arrivals = [pltpu.make_async_remote_copy(
            src_ref=ins[a], dst_ref=outs[a].at[_dev_index(*frm)], send_sem=send_sems.at[a, k],
            recv_sem=recv_sems.at[a, k], device_id=frm, device_id_type=MESH) for k, frm in enumerate(peers) for a in rng]
        return mine, sent, arrivals


def _comm_call(comms, name):
    ns = [c.n for c in comms]
    n = sum(ns)

    def body(*refs):
        parts, a, s = [], 0, 2 * n
        for c in comms:
            parts.append((c, refs[a:a + c.n], refs[n + a:n + a + c.n], refs[s:s + 3]))
            a, s = a + c.n, s + 3
        for c, ins, outs, sems in parts:
            c.start(ins, outs, sems)
        for c, ins, outs, sems in parts:
            if isinstance(c, _Gather):
                c.forward(ins, outs, sems)
        for c, ins, outs, sems in parts:
            c.finish(ins, outs, sems)

    res = pl.pallas_call(
        body, name=name, in_specs=[_ANY] * n, out_specs=[_ANY] * n,
        out_shape=[s for c in comms for s in c.out_shape()], scratch_shapes=[s for c in comms for s in c.scratch()],
    )(*[arr for c in comms for arr in c.arrays])
    out, a = [], 0
    for k in ns:
        out.append(res[a:a + k])
        a += k
    return out


def _pcall(body, args, *, name, grid, in_specs, out_specs, out_shape, scratch_shapes=(), sem="parallel", comm=None,
           step_axis=0):
    sem = (sem,) * len(grid) if isinstance(sem, str) else sem
    if comm is None:
        res = pl.pallas_call(body, name=name, grid=grid, in_specs=in_specs, out_specs=out_specs, out_shape=out_shape,
                             scratch_shapes=list(scratch_shapes), compiler_params=_params(*sem))(*args)
        return res, []
    n_in, n_out, n_scr, n = len(in_specs), len(out_specs), len(scratch_shapes), comm.n
    nsteps = grid[step_axis]
    assert all(g == 1 for ax, g in enumerate(grid) if ax != step_axis)

    def hosted(*refs):
        ins, cin = refs[:n_in], refs[n_in:n_in + n]
        o0 = n_in + n
        outs, cout = refs[o0:o0 + n_out], refs[o0 + n_out:o0 + n_out + n]
        s0 = o0 + n_out + n
        scr, sems = refs[s0:s0 + n_scr], refs[s0 + n_scr:]
        step = pl.program_id(step_axis)
        comm.before(cin, cout, sems, step, nsteps)
        body(*ins, *outs, *scr)
        comm.after(cin, cout, sems, step, nsteps)

    res = pl.pallas_call(
        hosted, name=name, grid=grid, in_specs=list(in_specs) + [_ANY] * n, out_specs=list(out_specs) + [_ANY] * n,
        out_shape=list(out_shape) + comm.out_shape(), scratch_shapes=list(scratch_shapes) + comm.scratch(),
        compiler_params=_params(*(("arbitrary",) * len(grid))))(*args, *comm.arrays)
    return res[:n_out], res[n_out:]


def _in_proj(x, w_in_t):
    S = x.shape[0]
    tb = min(512, S)

    def body(x_ref, w_ref, q_ref, k_ref, v_ref, xr_ref, gr_ref):
        u = _dot_nt(x_ref[...].astype(BF16), w_ref[...])
        q_ref[...] = (u[:, :D_ATT] * (HEAD_DIM ** -0.5)).astype(BF16)
        k_ref[...] = u[:, D_ATT:D_ATT + D_KV].astype(BF16)
        v_ref[...] = u[:, D_ATT + D_KV:D_ATT + 2 * D_KV].astype(BF16)
        xr_ref[...] = u[:, D_ATT + 2 * D_KV:D_ATT + 2 * D_KV + D_RNN]
        gr_ref[...] = u[:, D_ATT + 2 * D_KV + D_RNN:]

    return pl.pallas_call(
        body, name="in_proj", grid=(S // tb,),
        in_specs=[_rows(tb, D_MODEL), _resident((D_IN, D_MODEL))],
        out_specs=[_rows(tb, D_ATT), _rows(tb, D_KV), _rows(tb, D_KV), _rows(tb, D_RNN), _rows(tb, D_RNN)],
        out_shape=[jax.ShapeDtypeStruct((S, D_ATT), BF16), jax.ShapeDtypeStruct((S, D_KV), BF16),
                   jax.ShapeDtypeStruct((S, D_KV), BF16), jax.ShapeDtypeStruct((S, D_RNN), F32),
                   jax.ShapeDtypeStruct((S, D_RNN), F32)],
        compiler_params=_params("parallel"),
    )(x, w_in_t)


GROUP = N_HEADS // N_KV


def _band_mask(i):
    qi = lax.broadcasted_iota(jnp.int32, (GROUP * QBLK, 2 * QBLK), 0) & (QBLK - 1)
    sj = lax.broadcasted_iota(jnp.int32, (GROUP * QBLK, 2 * QBLK), 1)
    return (sj > qi) & (sj <= qi + QBLK) & ((sj >= QBLK) | (i > 0))


def _stack_heads(x, g):
    return jnp.concatenate([x[:, (g * GROUP + hh) * HEAD_DIM:(g * GROUP + hh + 1) * HEAD_DIM] for hh in range(GROUP)],
                           axis=0)


def _unstack_heads(x4):
    return [x4[hh * QBLK:(hh + 1) * QBLK] for hh in range(GROUP)]


def _sink_column(sink_ref, g):
    head = lax.broadcasted_iota(jnp.int32, (GROUP * QBLK, 1), 0) // QBLK
    col = jnp.full((GROUP * QBLK, 1), sink_ref[g * GROUP], F32)
    for hh in range(1, GROUP):
        col = jnp.where(head == hh, sink_ref[g * GROUP + hh], col)
    return col


def _attn_specs():
    cur = lambda i: (i, 0)
    prev = lambda i: (jnp.maximum(i - 1, 0), 0)
    return [pl.BlockSpec((QBLK, D_KV), cur), pl.BlockSpec((QBLK, D_KV), prev),
            pl.BlockSpec((QBLK, D_KV), cur), pl.BlockSpec((QBLK, D_KV), prev)]


def _attn_fwd(q, k, v, sinks, comm=None):
    S = q.shape[0]

    def body(sink_ref, q_ref, kc_ref, kp_ref, vc_ref, vp_ref, o_ref):
        valid = _band_mask(pl.program_id(0))
        outs = []
        qv = q_ref[...]
        kall = jnp.concatenate([kp_ref[...], kc_ref[...]], axis=0)
        vall = jnp.concatenate([vp_ref[...], vc_ref[...]], axis=0)
        for g in range(N_KV):
            kcat = kall[:, g * HEAD_DIM:(g + 1) * HEAD_DIM]
            vcat = vall[:, g * HEAD_DIM:(g + 1) * HEAD_DIM]
            s = jnp.where(valid, _dot_nt(_stack_heads(qv, g), kcat), -1e30)
            sink = _sink_column(sink_ref, g)
            m = jnp.maximum(jnp.max(s, axis=1, keepdims=True), sink)
            p = jnp.exp(s - m)
            l = jnp.sum(p, axis=1, keepdims=True) + jnp.exp(sink - m)
            outs += _unstack_heads(_dot(p.astype(BF16), vcat) / l)
        o_ref[...] = jnp.concatenate(outs, axis=1).astype(BF16)

    return _pcall(
        body, (sinks, q, k, k, v, v), name="attn_fwd", grid=(S // QBLK,), comm=comm,
        in_specs=[pl.BlockSpec(memory_space=pltpu.SMEM), _rows(QBLK, D_ATT)] + _attn_specs(),
        out_specs=[_rows(QBLK, D_ATT)], out_shape=[jax.ShapeDtypeStruct((S, D_ATT), BF16)])


def _w_rows(w_ref):
    return [w_ref[k:k + 1, :] for k in range(w_ref.shape[0])]


def _conv4(x, halo, w, b):
    y = b + w[3] * x
    for s in (1, 2, 3):
        y = y + w[3 - s] * _shift_down(x, halo, s)
    return y


def _rnn_gates(xc, wa, wx, ba, bx, sp):
    xcb = xc.astype(BF16)
    r = _sigmoid(_dot(xcb, wa) + ba)
    ig = _sigmoid(_dot(xcb, wx) + bx)
    la = -LRU_C * r * sp
    a = jnp.exp(la)
    t = jnp.tanh(la)
    f = jnp.sqrt(-2.0 * t / (1.0 - t))
    return r, ig, a, f


def _rnn_fwd(xr, gr, conv_w, conv_b, wa, wx, ba, bx, lam, comm=None):
    S = xr.shape[0]
    tb = min(256, S)

    def body(xr_ref, gr_ref, cw_ref, cb_ref, wa_ref, wx_ref, ba_ref, bx_ref, lam_ref, rec_ref, h_ref,
             halo_s, hc_s, a_s, b_s):
        @pl.when(pl.program_id(0) == 0)
        def _():
            halo_s[...] = jnp.zeros_like(halo_s)
            hc_s[...] = jnp.zeros_like(hc_s)

        x = xr_ref[...]
        xc = _conv4(x, halo_s[...], _w_rows(cw_ref), cb_ref[...])
        halo_s[...] = x[tb - 8:]
        _, ig, a, f = _rnn_gates(xc, wa_ref[...], wx_ref[...], ba_ref[...], bx_ref[...], _softplus_neg(lam_ref[...]))
        a_s[...] = a
        b_s[...] = f * ig * xc
        row8 = lax.broadcasted_iota(jnp.int32, (8, D_RNN), 0)

        def tile(t, hc):
            o = pl.multiple_of(t * 8, 8)
            at = a_s[pl.ds(o, 8), :]
            bt = b_s[pl.ds(o, 8), :]
            for s in (1, 2, 4):
                keep = row8 >= s
                a_sh = jnp.where(keep, pltpu.roll(at, s, 0), 1.0)
                b_sh = jnp.where(keep, pltpu.roll(bt, s, 0), 0.0)
                bt = at * b_sh + bt
                at = at * a_sh
            ht = at * hc + bt
            b_s[pl.ds(o, 8), :] = ht
            return _row_sum(jnp.where(row8 == 7, ht, 0.0))

        hc_s[0:1, :] = lax.fori_loop(0, tb // 8, tile, hc_s[0:1, :])
        h = b_s[...]
        h_ref[...] = h
        rec_ref[...] = (h * _gelu(gr_ref[...])).astype(BF16)

    vec = _resident((1, D_RNN))
    return _pcall(
        body, (xr, gr, conv_w, conv_b, wa, wx, ba, bx, lam), name="rnn_fwd", grid=(S // tb,), sem="arbitrary", comm=comm,
        in_specs=[_rows(tb, D_RNN), _rows(tb, D_RNN), _resident((4, D_RNN)), vec,
                  _resident((D_RNN, D_RNN)), _resident((D_RNN, D_RNN)), vec, vec, vec],
        out_specs=[_rows(tb, D_RNN), _rows(tb, D_RNN)],
        out_shape=[jax.ShapeDtypeStruct((S, D_RNN), BF16), jax.ShapeDtypeStruct((S, D_RNN), F32)],
        scratch_shapes=[pltpu.VMEM((8, D_RNN), F32), pltpu.VMEM((8, D_RNN), F32),
                        pltpu.VMEM((tb, D_RNN), F32), pltpu.VMEM((tb, D_RNN), F32)])


def _mix_ln1_up(x, att, rec, w_out, ln1_g, ln1_b, w_up, fcw, fcb, comm=None):
    S = x.shape[0]
    tb = min(256, S)
    nblk, _, wblk = w_up.shape
    half = nblk // 2

    def body(x_ref, att_ref, rec_ref, wo_ref, g_ref, b_ref, wu_ref, fcw_ref, fcb_ref,
             z1_ref, h1_ref, h1b_ref, gate_ref, act_ref, gl_ref, vdgl_ref, halo_s):
        @pl.when(pl.program_id(0) == 0)
        def _():
            halo_s[...] = jnp.zeros_like(halo_s)

        z1 = ALPHA * x_ref[...] + _dot(att_ref[...], wo_ref[:D_ATT, :]) + _dot(rec_ref[...], wo_ref[D_ATT:, :])
        z1_ref[...] = z1
        xhat, _ = _ln_stats(z1)
        h1 = xhat * g_ref[...] + b_ref[...]
        h1_ref[...] = h1
        h1b = h1.astype(BF16)
        h1b_ref[...] = h1b
        for jj in range(half):
            cols = slice(jj * wblk, (jj + 1) * wblk)
            gate = _dot(h1b, wu_ref[jj])
            val = _dot(h1b, wu_ref[jj + half])
            halo = halo_s[:, cols]
            conv = (fcb_ref[:, cols] + fcw_ref[2:3, cols] * gate + fcw_ref[1:2, cols] * _shift_down(gate, halo, 1)
                    + fcw_ref[0:1, cols] * _shift_down(gate, halo, 2))
            halo_s[:, cols] = gate[tb - 8:]
            gl, dgl = _gelu_and_grad(conv)
            gate_ref[:, cols] = gate.astype(BF16)
            act_ref[:, cols] = (gl * val).astype(BF16)
            gl_ref[:, cols] = gl.astype(BF16)
            vdgl_ref[:, cols] = (val * dgl).astype(BF16)

    vec = _resident((1, D_MODEL))
    wide = jax.ShapeDtypeStruct((S, D_FF), BF16)
    return _pcall(
        body, (x, att, rec, w_out, ln1_g, ln1_b, w_up, fcw, fcb), name="mix_ln1_up", grid=(S // tb,),
        sem="arbitrary", comm=comm,
        in_specs=[_rows(tb, D_MODEL), _rows(tb, D_ATT), _rows(tb, D_RNN), _resident((D_MODEL, D_MODEL)), vec, vec,
                  _resident(w_up.shape), _resident((3, D_FF)), _resident((1, D_FF))],
        out_specs=[_rows(tb, D_MODEL), _rows(tb, D_MODEL), _rows(tb, D_MODEL)] + [_rows(tb, D_FF)] * 4,
        out_shape=[jax.ShapeDtypeStruct((S, D_MODEL), F32), jax.ShapeDtypeStruct((S, D_MODEL), F32),
                   jax.ShapeDtypeStruct((S, D_MODEL), BF16), wide, wide, wide, wide],
        scratch_shapes=[pltpu.VMEM((8, D_FF), F32)])


def _tail(act, gl, vdgl, h1, h1b, p, tgt, w_down, w_pg, b_pg, w_pp, ln2_g, ln2_b):
    S = h1.shape[0]
    tb = min(256, S)

    def body(act_ref, gl_ref, vdgl_ref, h1_ref, h1b_ref, p_ref, t_ref, wd_ref, wpg_ref, bpg_ref, wpp_ref, g2_ref, b2_ref,
             dz2_ref, dpre_ref, dpp_ref, dgc_ref, dval_ref, dh1_ref, acc_ref):
        i = pl.program_id(0)

        @pl.when(i == 0)
        def _():
            acc_ref[...] = jnp.zeros_like(acc_ref)

        ffn = _dot(act_ref[...], wd_ref[...])
        h1 = h1_ref[...]
        sg = _sigmoid(_dot(h1b_ref[...], wpg_ref[...]) + bpg_ref[...])
        pp = _dot(p_ref[...].astype(BF16), wpp_ref[...])
        z2 = ALPHA * h1 + ffn + sg * pp
        xhat2, rstd2 = _ln_stats(z2)
        y = xhat2 * g2_ref[...] + b2_ref[...]
        err = y - t_ref[...]
        dy = err * (1.0 / D_MODEL)
        loss = 0.5 * jnp.sum(jnp.sum(err * err, axis=1, keepdims=True), axis=0, keepdims=True) * (1.0 / D_MODEL)
        dz2 = _ln_bwd(dy, xhat2, rstd2, g2_ref[...])
        dz2b = dz2.astype(BF16)
        dz2_ref[...] = dz2b
        dpre = dz2 * pp * sg * (1.0 - sg)
        dpreb = dpre.astype(BF16)
        dpre_ref[...] = dpreb
        dpp_ref[...] = (dz2 * sg).astype(BF16)
        dh1_ref[...] = ALPHA * dz2 + _dot_nt(dpreb, wpg_ref[...])
        dactb = _dot_nt(dz2b, wd_ref[...]).astype(BF16)
        dval_ref[...] = dactb * gl_ref[...]
        dgc_ref[...] = dactb * vdgl_ref[...]
        _put_rows(acc_ref, [_row_sum(dy * xhat2), _row_sum(dy), _row_sum(dpre),
                            jnp.broadcast_to(loss, (1, D_MODEL))])

    vec = _resident((1, D_MODEL))
    return pl.pallas_call(
        body, name="tail", grid=(S // tb,),
        in_specs=[_rows(tb, D_FF), _rows(tb, D_FF), _rows(tb, D_FF), _rows(tb, D_MODEL), _rows(tb, D_MODEL),
                  _rows(tb, PLE_DIM), _rows(tb, D_MODEL), _resident((D_FF, D_MODEL)), _resident((D_MODEL, D_MODEL)), vec,
                  _resident((PLE_DIM, D_MODEL)), vec, vec],
        out_specs=[_rows(tb, D_MODEL), _rows(tb, D_MODEL), _rows(tb, D_MODEL), _rows(tb, D_FF),
                   _rows(tb, D_FF), _rows(tb, D_MODEL), _acc((8, D_MODEL))],
        out_shape=[jax.ShapeDtypeStruct((S, D_MODEL), BF16),
                   jax.ShapeDtypeStruct((S, D_MODEL), BF16), jax.ShapeDtypeStruct((S, D_MODEL), BF16),
                   jax.ShapeDtypeStruct((S, D_FF), BF16), jax.ShapeDtypeStruct((S, D_FF), BF16),
                   jax.ShapeDtypeStruct((S, D_MODEL), F32), jax.ShapeDtypeStruct((8, D_MODEL), F32)],
        compiler_params=_params("arbitrary"),
    )(act, gl, vdgl, h1, h1b, p, tgt, w_down, w_pg, b_pg, w_pp, ln2_g, ln2_b)


def _weight_grad(a_list, b_list, name, layout, ts=512, comm=None):
    S = a_list[0].shape[0]
    ms = [a.shape[1] for a in a_list]
    M, nb, Nb = sum(ms), len(b_list), b_list[0].shape[1]
    ts = min(ts, S)
    nk = S // ts
    per_b = N_DEV // nb
    na = len(a_list)

    def body(*refs):
        a_refs, b_refs, o_ref, acc_ref = refs[:na], refs[na:na + nb], refs[na + nb], refs[na + nb + 1]
        j, k = pl.program_id(0), pl.program_id(1)

        @pl.when(k == 0)
        def _():
            acc_ref[...] = jnp.zeros_like(acc_ref)

        for jj in range(nb):
            @pl.when(j == jj)
            def _():
                b = b_refs[jj][...].astype(BF16)
                off = 0
                for a_ref, m in zip(a_refs, ms):
                    acc_ref[off:off + m, :] += _dot_tn(a_ref[...].astype(BF16), b)
                    off += m

        @pl.when(k == nk - 1)
        def _():
            for d in range(per_b):
                if layout == "rows":
                    o_ref[d] = acc_ref[d * (M // N_DEV):(d + 1) * (M // N_DEV), :].astype(BF16)
                elif layout == "cols":
                    o_ref[d] = acc_ref[:, d * (Nb // per_b):(d + 1) * (Nb // per_b)].astype(BF16)
                else:
                    o_ref[d] = acc_ref[:, d * (Nb // per_b):(d + 1) * (Nb // per_b)].T.astype(BF16)

    def b_index(jj):
        return lambda j, k: (jnp.where(j == jj, k, jnp.where(j < jj, 0, nk - 1)), 0)

    if layout == "rows":
        assert nb == 1
        blk = (N_DEV, M // N_DEV, Nb)
    elif layout == "cols":
        blk = (per_b, M, Nb // per_b)
    else:
        blk = (per_b, Nb // per_b, M)
    (res,), comm_res = _pcall(
        body, (*a_list, *b_list), name=name, grid=(nb, nk), sem="arbitrary", comm=comm, step_axis=1,
        in_specs=[pl.BlockSpec((ts, m), lambda j, k: (k, 0)) for m in ms]
        + [pl.BlockSpec((ts, Nb), b_index(jj)) for jj in range(nb)],
        out_specs=[pl.BlockSpec(blk, lambda j, k: (j, 0, 0))],
        out_shape=[jax.ShapeDtypeStruct((N_DEV,) + blk[1:], BF16)],
        scratch_shapes=[pltpu.VMEM((M, Nb), F32)])
    return (res, comm_res) if comm is not None else res


def _up_bwd(dgc, gate, dval, dh1p, z1, w_up, fcw, w_out, ln1_g, comm=None):
    S = z1.shape[0]
    tb = min(256, S)
    t16 = tb // 16
    n16 = S // 16
    nblk, _, wblk = w_up.shape
    half = nblk // 2
    nsteps = S // tb

    def body(dgc_ref, dgn_ref, gc_ref, gp_ref, dval_ref, dh1p_ref, z1_ref, wu_ref, fcw_ref, wo_ref, g1_ref,
             dgate_ref, dz1_ref, dz1b_ref, datt_ref, drec_ref, accf_ref, accd_ref):
        i = pl.program_id(0)

        @pl.when(i == 0)
        def _():
            accf_ref[...] = jnp.zeros_like(accf_ref)
            accd_ref[...] = jnp.zeros_like(accd_ref)

        dg = dgc_ref[...].astype(F32)
        nxt = jnp.where(i < nsteps - 1, dgn_ref[...].astype(F32)[0:8], 0.0)
        w = _w_rows(fcw_ref)
        dgate = (w[2] * dg + w[1] * _shift_up(dg, nxt, 1) + w[0] * _shift_up(dg, nxt, 2)).astype(BF16)
        dgate_ref[...] = dgate
        gate = gc_ref[...].astype(F32)
        halo = jnp.where(i > 0, gp_ref[...].astype(F32)[8:16], 0.0)
        _put_rows(accf_ref, [_row_sum(dg * _shift_down(gate, halo, 2)), _row_sum(dg * _shift_down(gate, halo, 1)),
                             _row_sum(dg * gate), _row_sum(dg)])

        dh1 = dh1p_ref[...]
        for j in range(nblk):
            src = dgate if j < half else dval_ref[...]
            jj = j % half
            dh1 = dh1 + _dot_nt(src[:, jj * wblk:(jj + 1) * wblk], wu_ref[j])
        xhat1, rstd1 = _ln_stats(z1_ref[...])
        dz1 = _ln_bwd(dh1, xhat1, rstd1, g1_ref[...])
        dz1_ref[...] = dz1
        dz1b = dz1.astype(BF16)
        dz1b_ref[...] = dz1b
        dcat = _dot_nt(dz1b, wo_ref[...])
        datt_ref[...] = dcat[:, :D_ATT].astype(BF16)
        drec_ref[...] = dcat[:, D_ATT:]
        _put_rows(accd_ref, [_row_sum(dh1 * xhat1), _row_sum(dh1)])

    prev16 = pl.BlockSpec((16, D_FF), lambda i: (jnp.maximum(i * t16 - 1, 0), 0))
    next16 = pl.BlockSpec((16, D_FF), lambda i: (jnp.minimum((i + 1) * t16, n16 - 1), 0))
    return _pcall(
        body, (dgc, dgc, gate, gate, dval, dh1p, z1, w_up, fcw, w_out, ln1_g), name="up_bwd",
        grid=(nsteps,), sem="arbitrary", comm=comm,
        in_specs=[_rows(tb, D_FF), next16, _rows(tb, D_FF), prev16, _rows(tb, D_FF), _rows(tb, D_MODEL),
                  _rows(tb, D_MODEL), _resident(w_up.shape), _resident((3, D_FF)),
                  _resident((D_MODEL, D_MODEL)), _resident((1, D_MODEL))],
        out_specs=[_rows(tb, D_FF), _rows(tb, D_MODEL), _rows(tb, D_MODEL), _rows(tb, D_ATT), _rows(tb, D_RNN),
                   _acc((8, D_FF)), _acc((8, D_MODEL))],
        out_shape=[jax.ShapeDtypeStruct((S, D_FF), BF16), jax.ShapeDtypeStruct((S, D_MODEL), F32),
                   jax.ShapeDtypeStruct((S, D_MODEL), BF16), jax.ShapeDtypeStruct((S, D_ATT), BF16),
                   jax.ShapeDtypeStruct((S, D_RNN), F32), jax.ShapeDtypeStruct((8, D_FF), F32),
                   jax.ShapeDtypeStruct((8, D_MODEL), F32)])


def _attn_bwd(q, k, v, do, sinks, comm=None):
    S = q.shape[0]
    grp = N_HEADS // N_KV

    def body(sink_ref, q_ref, kc_ref, kp_ref, vc_ref, vp_ref, do_ref, dq_ref, dkc_ref, dkp_ref, dvc_ref, dvp_ref,
             ds_ref):
        i = pl.program_id(0)

        @pl.when(i == 0)
        def _():
            ds_ref[...] = jnp.zeros_like(ds_ref)

        valid = _band_mask(i)
        row8 = lax.broadcasted_iota(jnp.int32, (8, 128), 0)
        lane8 = lax.broadcasted_iota(jnp.int32, (8, 128), 1)
        dqs, dks, dvs = [], [], []
        dsink = jnp.zeros((8, 128), F32)
        qv = q_ref[...]
        dov = do_ref[...]
        kall = jnp.concatenate([kp_ref[...], kc_ref[...]], axis=0)
        vall = jnp.concatenate([vp_ref[...], vc_ref[...]], axis=0)
        for g in range(N_KV):
            kcat = kall[:, g * HEAD_DIM:(g + 1) * HEAD_DIM]
            vcat = vall[:, g * HEAD_DIM:(g + 1) * HEAD_DIM]
            q4, do4 = _stack_heads(qv, g), _stack_heads(dov, g)
            s = jnp.where(valid, _dot_nt(q4, kcat), -1e30)
            sink = _sink_column(sink_ref, g)
            m = jnp.maximum(jnp.max(s, axis=1, keepdims=True), sink)
            e = jnp.exp(s - m)
            es = jnp.exp(sink - m)
            inv = 1.0 / (jnp.sum(e, axis=1, keepdims=True) + es)
            p = e * inv
            dp = _dot_nt(do4, vcat)
            delta = jnp.sum(p * dp, axis=1, keepdims=True)
            dsc = (p * (dp - delta)).astype(BF16)
            dqs += _unstack_heads(_dot(dsc, kcat) * (HEAD_DIM ** -0.5))
            dks.append(_dot_tn(q4, dsc))
            dvs.append(_dot_tn(do4, p.astype(BF16)))
            for hh, part in enumerate(_unstack_heads(-es * inv * delta)):
                here = (row8 == 0) & (lane8 == g * grp + hh)
                dsink = dsink + jnp.where(here, jnp.sum(part, axis=0, keepdims=True), 0.0)
        dq_ref[...] = jnp.concatenate(dqs, axis=1).astype(BF16)
        dk = jnp.concatenate(dks, axis=0).T
        dv = jnp.concatenate(dvs, axis=0).T
        dkp_ref[...] = dk[:QBLK]
        dkc_ref[...] = dk[QBLK:]
        dvp_ref[...] = dv[:QBLK]
        dvc_ref[...] = dv[QBLK:]
        ds_ref[...] += dsink

    kvs = jax.ShapeDtypeStruct((S, D_KV), F32)
    return _pcall(
        body, (sinks, q, k, k, v, v, do), name="attn_bwd", grid=(S // QBLK,), sem="arbitrary", comm=comm,
        in_specs=[pl.BlockSpec(memory_space=pltpu.SMEM), _rows(QBLK, D_ATT)] + _attn_specs() + [_rows(QBLK, D_ATT)],
        out_specs=[_rows(QBLK, D_ATT), _rows(QBLK, D_KV), _rows(QBLK, D_KV), _rows(QBLK, D_KV), _rows(QBLK, D_KV),
                   _acc((8, 128))],
        out_shape=[jax.ShapeDtypeStruct((S, D_ATT), BF16), kvs, kvs, kvs, kvs, jax.ShapeDtypeStruct((8, 128), F32)])


def _rnn_bwd(xr, gr, h, drec, conv_w, conv_b, wa, wx, ba, bx, lam, comm=None):
    S = xr.shape[0]
    tb = min(256, S)
    t8 = tb // 8
    nsteps = S // tb

    def body(xr_ref, xp_ref, gr_ref, h_ref, hp_ref, drec_ref, cw_ref, cb_ref, wa_ref, wx_ref, ba_ref, bx_ref, lam_ref,
             dxr_ref, dgr_ref, gwa_ref, gwx_ref, acc_ref, carry_s, dxc_halo_s, a_s, d_s, gwa_s, gwx_s):
        i = pl.program_id(0)
        blk = nsteps - 1 - i

        @pl.when(i == 0)
        def _():
            gwa_s[...] = jnp.zeros_like(gwa_s)
            gwx_s[...] = jnp.zeros_like(gwx_s)
            acc_ref[...] = jnp.zeros_like(acc_ref)
            carry_s[...] = jnp.zeros_like(carry_s)
            dxc_halo_s[...] = jnp.zeros_like(dxc_halo_s)

        x = xr_ref[...]
        xhalo = jnp.where(blk > 0, xp_ref[...], 0.0)
        cw = _w_rows(cw_ref)
        xs = [_shift_down(x, xhalo, 3), _shift_down(x, xhalo, 2), _shift_down(x, xhalo, 1), x]
        xc = cb_ref[...] + cw[0] * xs[0] + cw[1] * xs[1] + cw[2] * xs[2] + cw[3] * xs[3]
        sp = _softplus_neg(lam_ref[...])
        r, ig, a, f = _rnn_gates(xc, wa_ref[...], wx_ref[...], ba_ref[...], bx_ref[...], sp)
        hcur = h_ref[...]
        hprev = _shift_down(hcur, jnp.where(blk > 0, hp_ref[...], 0.0), 1)
        gl, dgl = _gelu_and_grad(gr_ref[...])
        drec = drec_ref[...]
        dgr_ref[...] = (drec * hcur * dgl).astype(BF16)
        a_s[...] = a
        d_s[...] = drec * gl
        row8 = lax.broadcasted_iota(jnp.int32, (8, D_RNN), 0)

        def tile(t, c):
            o = pl.multiple_of((t8 - 1 - t) * 8, 8)
            a8 = a_s[pl.ds(o, 8), :]
            dt = d_s[pl.ds(o, 8), :]
            at = jnp.where(row8 == 7, 1.0, pltpu.roll(a8, 7, 0))
            for s in (1, 2, 4):
                keep = row8 < 8 - s
                a_sh = jnp.where(keep, pltpu.roll(at, 8 - s, 0), 1.0)
                d_sh = jnp.where(keep, pltpu.roll(dt, 8 - s, 0), 0.0)
                dt = at * d_sh + dt
                at = at * a_sh
            lt = at * c + dt
            d_s[pl.ds(o, 8), :] = lt
            return _row_sum(jnp.where(row8 == 0, a8 * lt, 0.0))

        carry_s[0:1, :] = lax.fori_loop(0, t8, tile, carry_s[0:1, :])
        lmb = d_s[...]
        a2 = a * a
        dla = lmb * hprev * a - lmb * ig * xc * (a2 / f)
        di = lmb * f * xc
        dr = dla * (-LRU_C) * sp
        dpa = dr * r * (1.0 - r)
        dpx = di * ig * (1.0 - ig)
        dpab = dpa.astype(BF16)
        dpxb = dpx.astype(BF16)
        xcb = xc.astype(BF16)
        gwa_s[...] += _dot_tn(xcb, dpab)
        gwx_s[...] += _dot_tn(xcb, dpxb)

        @pl.when(i == nsteps - 1)
        def _():
            for dense, out in ((gwa_s[...], gwa_ref), (gwx_s[...], gwx_ref)):
                for b in range(RNN_BLOCKS):
                    rows = slice(b * HEAD_DIM, (b + 1) * HEAD_DIM)
                    out[rows, :] = dense[rows, b * HEAD_DIM:(b + 1) * HEAD_DIM]

        dxc = lmb * f * ig + _dot_nt(dpab, wa_ref[...]) + _dot_nt(dpxb, wx_ref[...])
        nxt = dxc_halo_s[...]
        dxr = cw[3] * dxc
        for s in (1, 2, 3):
            dxr = dxr + cw[3 - s] * _shift_up(dxc, nxt, s)
        dxr_ref[...] = dxr.astype(BF16)
        dxc_halo_s[...] = dxc[:8]
        dlam = _row_sum(dla * (-LRU_C) * r) * (-1.0 / (1.0 + jnp.exp(lam_ref[...])))
        _put_rows(acc_ref, [_row_sum(dxc * xs[0]), _row_sum(dxc * xs[1]), _row_sum(dxc * xs[2]), _row_sum(dxc * xs[3]),
                            _row_sum(dxc), _row_sum(dpa), _row_sum(dpx), dlam])

    rev = lambda i: (nsteps - 1 - i, 0)
    prev8 = lambda i: (jnp.maximum((nsteps - 1 - i) * t8 - 1, 0), 0)
    blkspec = pl.BlockSpec((tb, D_RNN), rev)
    halo8 = pl.BlockSpec((8, D_RNN), prev8)
    vec = _resident((1, D_RNN))
    return _pcall(
        body, (xr, xr, gr, h, h, drec, conv_w, conv_b, wa, wx, ba, bx, lam), name="rnn_bwd", grid=(nsteps,),
        sem="arbitrary", comm=comm,
        in_specs=[blkspec, halo8, blkspec, blkspec, halo8, blkspec, _resident((4, D_RNN)), vec,
                  _resident((D_RNN, D_RNN)), _resident((D_RNN, D_RNN)), vec, vec, vec],
        out_specs=[blkspec, blkspec, _acc((D_RNN, HEAD_DIM)), _acc((D_RNN, HEAD_DIM)), _acc((8, D_RNN))],
        out_shape=[jax.ShapeDtypeStruct((S, D_RNN), BF16), jax.ShapeDtypeStruct((S, D_RNN), BF16),
                   jax.ShapeDtypeStruct((D_RNN, HEAD_DIM), F32), jax.ShapeDtypeStruct((D_RNN, HEAD_DIM), F32),
                   jax.ShapeDtypeStruct((8, D_RNN), F32)],
        scratch_shapes=[pltpu.VMEM((8, D_RNN), F32), pltpu.VMEM((8, D_RNN), F32),
                        pltpu.VMEM((tb, D_RNN), F32), pltpu.VMEM((tb, D_RNN), F32),
                        pltpu.VMEM((D_RNN, D_RNN), F32), pltpu.VMEM((D_RNN, D_RNN), F32)])


def _in_bwd(dq, dkc, dkp, dvc, dvp, dxr, dgr, dz1, w_in, comm=None):
    S = dz1.shape[0]
    tb = min(512, S)
    nsteps = S // tb
    nq = S // QBLK
    r = tb // QBLK

    def body(dq_ref, dkc_ref, dkp_ref, dkn_ref, dvc_ref, dvp_ref, dvn_ref, dxr_ref, dgr_ref, dz1_ref, w_ref,
             du_ref, dx_ref):
        i = pl.program_id(0)
        last = i == nsteps - 1

        def shifted(prev_ref, next_ref):
            nxt = jnp.where(last, 0.0, next_ref[...])
            return jnp.concatenate([prev_ref[QBLK:], nxt], axis=0) if r > 1 else nxt

        dk = (dkc_ref[...] + shifted(dkp_ref, dkn_ref)).astype(BF16)
        dv = (dvc_ref[...] + shifted(dvp_ref, dvn_ref)).astype(BF16)
        du = jnp.concatenate([dq_ref[...], dk, dv, dxr_ref[...], dgr_ref[...]], axis=1)
        du_ref[...] = du
        dx_ref[...] = ALPHA * dz1_ref[...] + _dot(du, w_ref[...])

    nextq = pl.BlockSpec((QBLK, D_KV), lambda i: (jnp.minimum((i + 1) * r, nq - 1), 0))
    return _pcall(
        body, (dq, dkc, dkp, dkp, dvc, dvp, dvp, dxr, dgr, dz1, w_in), name="in_bwd", grid=(nsteps,), comm=comm,
        in_specs=[_rows(tb, D_ATT), _rows(tb, D_KV), _rows(tb, D_KV), nextq, _rows(tb, D_KV), _rows(tb, D_KV), nextq,
                  _rows(tb, D_RNN), _rows(tb, D_RNN), _rows(tb, D_MODEL), _resident((D_IN, D_MODEL))],
        out_specs=[_rows(tb, D_IN), _rows(tb, D_MODEL)],
        out_shape=[jax.ShapeDtypeStruct((S, D_IN), BF16), jax.ShapeDtypeStruct((S, D_MODEL), F32)])


def _block_diag(w):
    eye = jnp.eye(RNN_BLOCKS, dtype=w.dtype)
    return (w[:, :, None, :] * eye[:, None, :, None]).reshape(D_RNN, D_RNN).astype(BF16)


def _adamw(w, g, m, v):
    m = ADAM_B1 * m + (1.0 - ADAM_B1) * g
    v = ADAM_B2 * v + (1.0 - ADAM_B2) * (g * g)
    m_hat = m / (1.0 - ADAM_B1 ** ADAM_STEP)
    v_hat = v / (1.0 - ADAM_B2 ** ADAM_STEP)
    delta = -ADAM_LR * (m_hat / (jnp.sqrt(v_hat) + ADAM_EPS) + ADAM_WD * w)
    return delta, m, v


def _sum_adamw(parts, w, m, v, name):
    R, C = w.shape
    rb = R if R <= 256 else 128
    assert R % rb == 0

    def body(p_ref, w_ref, m_ref, v_ref, g_out, d_out, m_out, v_out):
        g = p_ref[0].astype(F32)
        for d in range(1, N_DEV):
            g = g + p_ref[d].astype(F32)
        delta, mn, vn = _adamw(w_ref[...], g, m_ref[...], v_ref[...])
        g_out[...] = g
        d_out[...] = delta
        m_out[...] = mn
        v_out[...] = vn

    blk = _rows(rb, C)
    out = jax.ShapeDtypeStruct((R, C), F32)
    return pl.pallas_call(
        body, name=name, grid=(R // rb,),
        in_specs=[pl.BlockSpec((N_DEV, rb, C), lambda i: (0, i, 0)), blk, blk, blk],
        out_specs=[blk, blk, blk, blk], out_shape=[out, out, out, out],
        compiler_params=_params("parallel"),
    )(parts, w, m, v)


_SMALL = [("attn_sinks", "s", 0, 1, None), ("rnn_conv_w", "r", 0, 4, "cols"), ("rnn_conv_b", "r", 4, 1, None),
          ("gate_a_w", "a", 0, D_RNN, None), ("gate_a_b", "r", 5, 1, None), ("gate_x_w", "x", 0, D_RNN, None),
          ("gate_x_b", "r", 6, 1, None), ("lru_lambda", "r", 7, 1, None), ("ln1_g", "d", 0, 1, None),
          ("ln1_b", "d", 1, 1, None), ("ffn_conv_w", "f", 0, 3, "cols"), ("ffn_conv_b", "f", 3, 1, None),
          ("ple_gate_b", "t", 2, 1, None), ("ln2_g", "t", 0, 1, None), ("ln2_b", "t", 1, 1, None)]
_LOSS_ROW = 3


def _small_update(gathered, params):
    keys = "tfdsrax"
    flat = [arr for triple in params for arr in triple]
    n_par = len(_SMALL)

    def body(*refs):
        g_refs = dict(zip(keys, refs[:7]))
        p_refs = refs[7:7 + 3 * n_par]
        loss_ref = refs[7 + 3 * n_par]
        o_refs = refs[8 + 3 * n_par:8 + 7 * n_par]
        tot = dict(zip(keys, refs[8 + 7 * n_par:8 + 7 * n_par + 7]))
        tmp_r, tmp_f = refs[8 + 7 * n_par + 7:]
        me = _dev_index(*_place())
        for key in keys:
            s = g_refs[key][0]
            for d in range(1, N_DEV):
                s = s + g_refs[key][d]
            tot[key][...] = s
        loss_ref[...] = tot["t"][_LOSS_ROW:_LOSS_ROW + 1, 0:128]
        for i, (name, key, row, rows, how) in enumerate(_SMALL):
            w_ref, m_ref, v_ref = p_refs[3 * i:3 * i + 3]
            g_out, d_out, m_out, v_out = o_refs[4 * i:4 * i + 4]
            if how == "cols":
                full = tot[key][...]
                width = full.shape[1] // N_DEV
                mine = full[:, :width]
                for d in range(1, N_DEV):
                    mine = jnp.where(me == d, full[:, d * width:(d + 1) * width], mine)
                tmp = tmp_r if key == "r" else tmp_f
                tmp[...] = mine
                g = tmp[row:row + rows, :]
            else:
                g = tot[key][row:row + rows, :]
                g = g[:, :w_ref.shape[1]]
            delta, mn, vn = _adamw(w_ref[...], g, m_ref[...], v_ref[...])
            g_out[...] = g
            d_out[...] = delta
            m_out[...] = mn
            v_out[...] = vn

    outs = [jax.ShapeDtypeStruct((1, 128), F32)]
    for w, _, _ in params:
        outs += [jax.ShapeDtypeStruct(w.shape, F32)] * 4
    scratch = [pltpu.VMEM(g.shape[1:], F32) for g in gathered]
    scratch += [pltpu.VMEM((8, D_RNN // N_DEV), F32), pltpu.VMEM((8, D_FF // N_DEV), F32)]
    res = pl.pallas_call(body, name="small_update", out_shape=outs, scratch_shapes=scratch)(*gathered, *flat)
    return res[0], [res[1 + 4 * i:5 + 4 * i] for i in range(n_par)]


def kernel(x, p, w_in, attn_sinks, rnn_conv_w, rnn_conv_b, gate_a_w, gate_a_b, gate_x_w, gate_x_b, lru_lambda, w_out, ln1_g, ln1_b, w_ffn_up, ffn_conv_w, ffn_conv_b, w_ffn_down, ple_gate_w, ple_gate_b, ple_proj, ln2_g, ln2_b, loss_target, m_w_in, m_attn_sinks, m_rnn_conv_w, m_rnn_conv_b, m_gate_a_w, m_gate_a_b, m_gate_x_w, m_gate_x_b, m_lru_lambda, m_w_out, m_ln1_g, m_ln1_b, m_w_ffn_up, m_ffn_conv_w, m_ffn_conv_b, m_w_ffn_down, m_ple_gate_w, m_ple_gate_b, m_ple_proj, m_ln2_g, m_ln2_b, v_w_in, v_attn_sinks, v_rnn_conv_w, v_rnn_conv_b, v_gate_a_w, v_gate_a_b, v_gate_x_w, v_gate_x_b, v_lru_lambda, v_w_out, v_ln1_g, v_ln1_b, v_w_ffn_up, v_ffn_conv_w, v_ffn_conv_b, v_w_ffn_down, v_ple_gate_w, v_ple_gate_b, v_ple_proj, v_ln2_g, v_ln2_b):
    from_col_blocks = lambda g: g.transpose(1, 0, 2).reshape(g.shape[1], N_DEV * g.shape[2])

    xs, ps, tgt, sinks = x[0], p[0, 0], loss_target[0], attn_sinks[0]
    wa, wx = _block_diag(gate_a_w[0]), _block_diag(gate_x_w[0])

    conv_cols = jnp.concatenate([rnn_conv_w[0].reshape(1, -1), ffn_conv_w[0].reshape(1, -1)], axis=1)
    n_rc, n_fc = 4 * D_RNN // N_DEV, 3 * D_FF // N_DEV
    ((g_in, g_conv),) = _comm_call([_Gather([w_in[0].T.astype(BF16), jnp.broadcast_to(conv_cols, (8, n_rc + n_fc))])],
                                   "gather_w_in")
    w_in_full = g_in.reshape(D_IN, D_MODEL)
    rcw = from_col_blocks(g_conv[:, 0, :n_rc].reshape(N_DEV, 4, D_RNN // N_DEV))
    fcw = from_col_blocks(g_conv[:, 0, n_rc:].reshape(N_DEV, 3, D_FF // N_DEV))

    q, k, v, xr, gr = _in_proj(xs, w_in_full)
    (att,), (w_up,) = _attn_fwd(q, k, v, sinks, comm=_Gather([w_ffn_up[0].astype(BF16)]))
    (rec, h), (g_out, g_down) = _rnn_fwd(xr, gr, rcw, rnn_conv_b, wa, wx, gate_a_b, gate_x_b, lru_lambda,
                                         comm=_Gather([w_out[0].astype(BF16), w_ffn_down[0].astype(BF16)]))
    w_out_full = g_out.reshape(D_MODEL, D_MODEL)
    (z1, h1, h1b, gate, act, gl, vdgl), (g_pg, g_pp) = _mix_ln1_up(
        xs, att, rec, w_out_full, ln1_g, ln1_b, w_up, fcw, ffn_conv_b,
        comm=_Gather([ple_gate_w[0].astype(BF16), ple_proj[0].astype(BF16)]))
    dz2b, dpreb, dppb, dgc, dval, dh1p, acc_t = _tail(
        act, gl, vdgl, h1, h1b, ps, tgt, g_down.reshape(D_FF, D_MODEL), g_pg.reshape(D_MODEL, D_MODEL), ple_gate_b,
        from_col_blocks(g_pp), ln2_g, ln2_b)

    gd_down = _weight_grad([dz2b], [act], "down_grad", "rows_t")
    gd_pg = _weight_grad([h1b], [dpreb], "pg_grad", "rows", ts=1024)
    gd_pp = _weight_grad([ps], [dppb], "pp_grad", "cols", ts=1024)
    (dgate, dz1, dz1b, datt, drec, acc_f, acc_d), (r_down, r_pg, r_pp) = _up_bwd(
        dgc, gate, dval, dh1p, z1, w_up, fcw, w_out_full, ln1_g, comm=_Exchange([gd_down, gd_pg, gd_pp]))
    gd_up = _weight_grad([h1b], [dgate, dval], "up_grad", "cols")
    gd_out = _weight_grad([att, rec], [dz1b], "out_grad", "rows", ts=1024)
    (dq, dkc, dkp, dvc, dvp, acc_s), (r_up,) = _attn_bwd(q, k, v, datt, sinks, comm=_Exchange([gd_up]))
    (dxr, dgr, g_wa, g_wx, acc_r), (r_out,) = _rnn_bwd(xr, gr, h, drec, rcw, rnn_conv_b, wa, wx, gate_a_b, gate_x_b,
                                                       lru_lambda, comm=_Exchange([gd_out]))
    (du, dx), _ = _in_bwd(dq, dkc, dkp, dvc, dvp, dxr, dgr, dz1, w_in_full)
    gd_in = _weight_grad([du], [xs], "in_grad", "rows", ts=1024)
    small_parts, (r_in,) = _comm_call([_Bcast([acc_t, acc_f, acc_d, acc_s, acc_r, g_wa, g_wx]), _Exchange([gd_in])],
                                      "exchange_w_in")

    outs = {}
    res = _sum_adamw(r_in, w_in[0].T, m_w_in[0].T, v_w_in[0].T, "adamw_w_in")
    outs["w_in"] = [r.T[None] for r in res]
    for name, parts, w, m, v in [("w_out", r_out, w_out, m_w_out, v_w_out),
                                 ("w_ffn_up", r_up, w_ffn_up, m_w_ffn_up, v_w_ffn_up),
                                 ("w_ffn_down", r_down, w_ffn_down, m_w_ffn_down, v_w_ffn_down),
                                 ("ple_gate_w", r_pg, ple_gate_w, m_ple_gate_w, v_ple_gate_w),
                                 ("ple_proj", r_pp, ple_proj, m_ple_proj, v_ple_proj)]:
        res = _sum_adamw(parts, w[0], m[0], v[0], "adamw_" + name)
        outs[name] = [r[None] for r in res]

    given = dict(attn_sinks=(attn_sinks, m_attn_sinks, v_attn_sinks), rnn_conv_w=(rnn_conv_w, m_rnn_conv_w, v_rnn_conv_w),
                 rnn_conv_b=(rnn_conv_b, m_rnn_conv_b, v_rnn_conv_b), gate_a_w=(gate_a_w, m_gate_a_w, v_gate_a_w),
                 gate_a_b=(gate_a_b, m_gate_a_b, v_gate_a_b), gate_x_w=(gate_x_w, m_gate_x_w, v_gate_x_w),
                 gate_x_b=(gate_x_b, m_gate_x_b, v_gate_x_b), lru_lambda=(lru_lambda, m_lru_lambda, v_lru_lambda),
                 ln1_g=(ln1_g, m_ln1_g, v_ln1_g), ln1_b=(ln1_b, m_ln1_b, v_ln1_b),
                 ffn_conv_w=(ffn_conv_w, m_ffn_conv_w, v_ffn_conv_w), ffn_conv_b=(ffn_conv_b, m_ffn_conv_b, v_ffn_conv_b),
                 ple_gate_b=(ple_gate_b, m_ple_gate_b, v_ple_gate_b), ln2_g=(ln2_g, m_ln2_g, v_ln2_g),
                 ln2_b=(ln2_b, m_ln2_b, v_ln2_b))
    as_2d = lambda a: a.reshape(-1, a.shape[-1])
    loss_row, small_res = _small_update(small_parts, [tuple(as_2d(a) for a in given[n]) for n, *_ in _SMALL])
    loss = loss_row[0, 0]
    for (n, *_), res in zip(_SMALL, small_res):
        outs[n] = [r.reshape(given[n][0].shape) for r in res]

    order = ["w_in", "attn_sinks", "rnn_conv_w", "rnn_conv_b", "gate_a_w", "gate_a_b", "gate_x_w", "gate_x_b",
             "lru_lambda", "w_out", "ln1_g", "ln1_b", "w_ffn_up", "ffn_conv_w", "ffn_conv_b", "w_ffn_down",
             "ple_gate_w", "ple_gate_b", "ple_proj", "ln2_g", "ln2_b"]
    return (loss, dx[None], *[outs[n][0] for n in order], *[outs[n][1] for n in order],
            *[outs[n][2] for n in order], *[outs[n][3] for n in order])
```

```python
import jax
import jax.numpy as jnp
from jax import lax
from jax.experimental import pallas as pl
from jax.experimental.pallas import tpu as pltpu

F32 = jnp.float32
BF16 = jnp.bfloat16

D_MODEL = 1024
D_ATT = 512
D_KV = 128
HEAD_DIM = 64
N_HEADS = 8
N_KV = 2
D_RNN = 512
RNN_BLOCKS = 8
D_IN = 1792
D_FF = 3072
PLE_DIM = 256
QBLK = 128
N_DEV = 8
ALPHA = float(2 ** 0.25)
LN_EPS = 1e-5
LRU_C = 8.0
ADAM_LR, ADAM_B1, ADAM_B2, ADAM_EPS, ADAM_WD, ADAM_STEP = 0.001, 0.9, 0.999, 1e-08, 0.01, 10

V7X_VMEM_LIMIT = 56 * 1024 * 1024
MESH = pl.DeviceIdType.MESH


def _params(*sem, vmem=V7X_VMEM_LIMIT):
    return pltpu.CompilerParams(dimension_semantics=sem or None, vmem_limit_bytes=vmem)


def _resident(shape):
    return pl.BlockSpec(shape, lambda *_: (0,) * len(shape), pipeline_mode=pl.Buffered(1))


def _rows(tb, cols):
    return pl.BlockSpec((tb, cols), lambda i: (i, 0))


def _acc(shape):
    return pl.BlockSpec(shape, lambda *_: (0,) * len(shape))


def _dot(a, b):
    return jnp.dot(a, b, preferred_element_type=F32)


def _dot_nt(a, b):
    return lax.dot_general(a, b, (((1,), (1,)), ((), ())), preferred_element_type=F32)


def _dot_tn(a, b):
    return lax.dot_general(a, b, (((0,), (0,)), ((), ())), preferred_element_type=F32)


def _sigmoid(x):
    return 1.0 / (1.0 + jnp.exp(-x))


_GELU_C = 0.7978845608028654
_GELU_K = 0.044715


def _gelu_and_grad(x):
    u = x * x
    t = jnp.tanh(x * (_GELU_C + (_GELU_C * _GELU_K) * u))
    hp = 0.5 + 0.5 * t
    dg = hp + x * (0.5 - 0.5 * (t * t)) * (_GELU_C + (3.0 * _GELU_C * _GELU_K) * u)
    return x * hp, dg


def _gelu(x):
    return 0.5 * x * (1.0 + jnp.tanh(_GELU_C * (x + _GELU_K * x * x * x)))


def _ln_stats(z):
    mu = jnp.mean(z, axis=-1, keepdims=True)
    zc = z - mu
    var = jnp.mean(zc * zc, axis=-1, keepdims=True)
    rstd = lax.rsqrt(var + LN_EPS)
    return zc * rstd, rstd


def _ln_bwd(dy, xhat, rstd, g):
    dxh = dy * g
    m1 = jnp.mean(dxh, axis=-1, keepdims=True)
    m2 = jnp.mean(dxh * xhat, axis=-1, keepdims=True)
    return rstd * (dxh - m1 - xhat * m2)


def _softplus_neg(lam):
    u = jnp.exp(-jnp.abs(lam))
    w = 1.0 + u
    d = w - 1.0
    log1p_u = jnp.where(d == 0.0, u, jnp.log(w) * (u / jnp.where(d == 0.0, 1.0, d)))
    return jnp.maximum(-lam, 0.0) + log1p_u


def _shift_down(x, halo, s):
    xs = pltpu.roll(x, s, 0)
    hs = pltpu.roll(halo, s, 0)
    row8 = lax.broadcasted_iota(jnp.int32, hs.shape, 0)
    first = jnp.where(row8 < s, hs, xs[:8])
    return jnp.concatenate([first, xs[8:]], axis=0)


def _shift_up(x, halo, s):
    n = x.shape[0]
    xs = pltpu.roll(x, n - s, 0)
    hs = pltpu.roll(halo, 8 - s, 0)
    row8 = lax.broadcasted_iota(jnp.int32, hs.shape, 0)
    last = jnp.where(row8 >= 8 - s, hs, xs[n - 8:])
    return jnp.concatenate([xs[:n - 8], last], axis=0)


def _row_sum(x):
    return jnp.sum(x, axis=0, keepdims=True)


def _put_rows(acc_ref, rows):
    row8 = lax.broadcasted_iota(jnp.int32, acc_ref.shape, 0)
    upd = jnp.zeros(acc_ref.shape, F32)
    for r, vec in enumerate(rows):
        upd = jnp.where(row8 == r, vec, upd)
    acc_ref[...] += upd


def _place():
    return lax.axis_index("x"), lax.axis_index("y"), lax.axis_index("c")


def _dev_index(px, py, pc):
    return 4 * px + 2 * py + pc


_ANY = pl.BlockSpec(memory_space=pl.ANY)


class _Gather:
    def __init__(self, arrays):
        self.arrays = list(arrays)
        self.n = len(self.arrays)

    def out_shape(self):
        return [jax.ShapeDtypeStruct((N_DEV,) + s.shape, s.dtype) for s in self.arrays]

    def scratch(self):
        return [pltpu.SemaphoreType.DMA((self.n, 7)), pltpu.SemaphoreType.DMA((self.n, 7)),
                pltpu.SemaphoreType.DMA((self.n,))]

    def _parts(self, ins, outs, sems):
        send_sems, recv_sems, local_sems = sems
        x, y, c = _place()
        me, sibling = (x, y, c), (x, y, 1 - c)
        chips = [(1 - x, y), (x, 1 - y), (1 - x, 1 - y)]

        def copy(a, k, block, to, src=None):
            rows = outs[a].at[_dev_index(*block)]
            return pltpu.make_async_remote_copy(
                src_ref=rows if src is None else src, dst_ref=rows, send_sem=send_sems.at[a, k],
                recv_sem=recv_sems.at[a, k], device_id=to, device_id_type=MESH)

        rng = range(self.n)
        mine = [pltpu.make_async_copy(ins[a], outs[a].at[_dev_index(*me)], local_sems.at[a]) for a in rng]
        first = [copy(a, 0, me, sibling, src=ins[a]) for a in rng]
        first += [copy(a, 1 + j, me, (*chip, c), src=ins[a]) for j, chip in enumerate(chips) for a in rng]
        landed = [copy(a, 1 + j, (*chip, c), me) for j, chip in enumerate(chips) for a in rng]
        passed = [copy(a, 4 + j, (*chip, c), sibling) for j, chip in enumerate(chips) for a in rng]
        from_sibling = [copy(a, 0, sibling, me) for a in rng]
        from_sibling += [copy(a, 4 + j, (*chip, 1 - c), me) for j, chip in enumerate(chips) for a in rng]
        return mine, first, landed, passed, from_sibling

    def start(self, ins, outs, sems):
        mine, first, _, _, _ = self._parts(ins, outs, sems)
        for cp in mine + first:
            cp.start()

    def forward(self, ins, outs, sems):
        _, _, landed, passed, _ = self._parts(ins, outs, sems)
        for got, fwd in zip(landed, passed):
            got.wait_recv()
            fwd.start()

    def finish(self, ins, outs, sems):
        mine, first, _, passed, from_sibling = self._parts(ins, outs, sems)
        for cp in from_sibling:
            cp.wait_recv()
        for cp in first + passed:
            cp.wait_send()
        for cp in mine:
            cp.wait()

    def before(self, ins, outs, sems, step, nsteps):
        pl.when(step == 0)(lambda: self.start(ins, outs, sems))
        pl.when(step == (7 * nsteps) // 8)(lambda: self.forward(ins, outs, sems))

    def after(self, ins, outs, sems, step, nsteps):
        pl.when(step == nsteps - 1)(lambda: self.finish(ins, outs, sems))


class _Exchange:
    def __init__(self, arrays):
        self.arrays = list(arrays)
        self.n = len(self.arrays)

    def out_shape(self):
        return [jax.ShapeDtypeStruct(b.shape, b.dtype) for b in self.arrays]

    def scratch(self):
        return [pltpu.SemaphoreType.DMA((self.n, 7)), pltpu.SemaphoreType.DMA((self.n, 7)),
                pltpu.SemaphoreType.DMA((self.n,))]

    def _parts(self, ins, outs, sems):
        send_sems, recv_sems, local_sems = sems
        x, y, c = _place()
        me = _dev_index(x, y, c)
        peers = [(x ^ (k >> 2), y ^ ((k >> 1) & 1), c ^ (k & 1)) for k in range(1, N_DEV)]
        rng = range(self.n)
        mine = [pltpu.make_async_copy(ins[a].at[me], outs[a].at[me], local_sems.at[a]) for a in rng]
        sent = [pltpu.make_async_remote_copy(
            src_ref=ins[a].at[_dev_index(*to)], dst_ref=outs[a].at[me], send_sem=send_sems.at[a, k],
            recv_sem=recv_sems.at[a, k], device_id=to, device_id_type=MESH) for k, to in enumerate(peers) for a in rng]
        arrivals = [pltpu.make_async_remote_copy(
            src_ref=ins[a].at[me], dst_ref=outs[a].at[_dev_index(*frm)], send_sem=send_sems.at[a, k],
            recv_sem=recv_sems.at[a, k], device_id=frm, device_id_type=MESH) for k, frm in enumerate(peers) for a in rng]
        return mine, sent, arrivals

    def start(self, ins, outs, sems):
        mine, sent, _ = self._parts(ins, outs, sems)
        for cp in mine + sent:
            cp.start()

    def finish(self, ins, outs, sems):
        mine, sent, arrivals = self._parts(ins, outs, sems)
        for cp in arrivals:
            cp.wait_recv()
        for cp in sent:
            cp.wait_send()
        for cp in mine:
            cp.wait()

    def before(self, ins, outs, sems, step, nsteps):
        pl.when(step == 0)(lambda: self.start(ins, outs, sems))

    def after(self, ins, outs, sems, step, nsteps):
        pl.when(step == nsteps - 1)(lambda: self.finish(ins, outs, sems))


class _Bcast(_Exchange):
    def out_shape(self):
        return [jax.ShapeDtypeStruct((N_DEV,) + s.shape, s.dtype) for s in self.arrays]

    def _parts(self, ins, outs, sems):
        send_sems, recv_sems, local_sems = sems
        x, y, c = _place()
        me = _dev_index(x, y, c)
        peers = [(x ^ (k >> 2), y ^ ((k >> 1) & 1), c ^ (k & 1)) for k in range(1, N_DEV)]
        rng = range(self.n)
        mine = [pltpu.make_async_copy(ins[a], outs[a].at[me], local_sems.at[a]) for a in rng]
        sent = [pltpu.make_async_remote_copy(
            src_ref=ins[a], dst_ref=outs[a].at[me], send_sem=send_sems.at[a, k], recv_sem=recv_sems.at[a, k],
            device_id=to, device_id_type=MESH) for k, to in enumerate(peers) for a in rng]
        arrivals = [pltpu.make_async_remote_copy(
            src_ref=ins[a], dst_ref=outs[a].at[_dev_index(*frm)], send_sem=send_sems.at[a, k],
            recv_sem=recv_sems.at[a, k], device_id=frm, device_id_type=MESH) for k, frm in enumerate(peers) for a in rng]
        return mine, sent, arrivals


def _comm_call(comms, name):
    ns = [c.n for c in comms]
    n = sum(ns)

    def body(*refs):
        parts, a, s = [], 0, 2 * n
        for c in comms:
            parts.append((c, refs[a:a + c.n], refs[n + a:n + a + c.n], refs[s:s + 3]))
            a, s = a + c.n, s + 3
        for c, ins, outs, sems in parts:
            c.start(ins, outs, sems)
        for c, ins, outs, sems in parts:
            if isinstance(c, _Gather):
                c.forward(ins, outs, sems)
        for c, ins, outs, sems in parts:
            c.finish(ins, outs, sems)

    res = pl.pallas_call(
        body, name=name, in_specs=[_ANY] * n, out_specs=[_ANY] * n,
        out_shape=[s for c in comms for s in c.out_shape()], scratch_shapes=[s for c in comms for s in c.scratch()],
    )(*[arr for c in comms for arr in c.arrays])
    out, a = [], 0
    for k in ns:
        out.append(res[a:a + k])
        a += k
    return out


def _pcall(body, args, *, name, grid, in_specs, out_specs, out_shape, scratch_shapes=(), sem="parallel", comm=None,
           step_axis=0):
    sem = (sem,) * len(grid) if isinstance(sem, str) else sem
    if comm is None:
        res = pl.pallas_call(body, name=name, grid=grid, in_specs=in_specs, out_specs=out_specs, out_shape=out_shape,
                             scratch_shapes=list(scratch_shapes), compiler_params=_params(*sem))(*args)
        return res, []
    n_in, n_out, n_scr, n = len(in_specs), len(out_specs), len(scratch_shapes), comm.n
    nsteps = grid[step_axis]
    assert all(g == 1 for ax, g in enumerate(grid) if ax != step_axis)

    def hosted(*refs):
        ins, cin = refs[:n_in], refs[n_in:n_in + n]
        o0 = n_in + n
        outs, cout = refs[o0:o0 + n_out], refs[o0 + n_out:o0 + n_out + n]
        s0 = o0 + n_out + n
        scr, sems = refs[s0:s0 + n_scr], refs[s0 + n_scr:]
        step = pl.program_id(step_axis)
        comm.before(cin, cout, sems, step, nsteps)
        body(*ins, *outs, *scr)
        comm.after(cin, cout, sems, step, nsteps)

    res = pl.pallas_call(
        hosted, name=name, grid=grid, in_specs=list(in_specs) + [_ANY] * n, out_specs=list(out_specs) + [_ANY] * n,
        out_shape=list(out_shape) + comm.out_shape(), scratch_shapes=list(scratch_shapes) + comm.scratch(),
        compiler_params=_params(*(("arbitrary",) * len(grid))))(*args, *comm.arrays)
    return res[:n_out], res[n_out:]


def _in_proj(x, w_in_t, comm=None):
    S = x.shape[0]
    tb = min(512, S)

    def body(x_ref, w_ref, q_ref, k_ref, v_ref, xr_ref, gr_ref):
        u = _dot_nt(x_ref[...].astype(BF16), w_ref[...])
        q_ref[...] = (u[:, :D_ATT] * (HEAD_DIM ** -0.5)).astype(BF16)
        k_ref[...] = u[:, D_ATT:D_ATT + D_KV].astype(BF16)
        v_ref[...] = u[:, D_ATT + D_KV:D_ATT + 2 * D_KV].astype(BF16)
        xr_ref[...] = u[:, D_ATT + 2 * D_KV:D_ATT + 2 * D_KV + D_RNN]
        gr_ref[...] = u[:, D_ATT + 2 * D_KV + D_RNN:]

    return _pcall(
        body, (x, w_in_t), name="in_proj", grid=(S // tb,), comm=comm,
        in_specs=[_rows(tb, D_MODEL), _resident((D_IN, D_MODEL))],
        out_specs=[_rows(tb, D_ATT), _rows(tb, D_KV), _rows(tb, D_KV), _rows(tb, D_RNN), _rows(tb, D_RNN)],
        out_shape=[jax.ShapeDtypeStruct((S, D_ATT), BF16), jax.ShapeDtypeStruct((S, D_KV), BF16),
                   jax.ShapeDtypeStruct((S, D_KV), BF16), jax.ShapeDtypeStruct((S, D_RNN), F32),
                   jax.ShapeDtypeStruct((S, D_RNN), F32)])


GROUP = N_HEADS // N_KV


def _band_mask(i):
    qi = lax.broadcasted_iota(jnp.int32, (GROUP * QBLK, 2 * QBLK), 0) & (QBLK - 1)
    sj = lax.broadcasted_iota(jnp.int32, (GROUP * QBLK, 2 * QBLK), 1)
    return (sj > qi) & (sj <= qi + QBLK) & ((sj >= QBLK) | (i > 0))


def _stack_heads(x, g):
    return jnp.concatenate([x[:, (g * GROUP + hh) * HEAD_DIM:(g * GROUP + hh + 1) * HEAD_DIM] for hh in range(GROUP)],
                           axis=0)


def _unstack_heads(x4):
    return [x4[hh * QBLK:(hh + 1) * QBLK] for hh in range(GROUP)]


def _sink_column(sink_ref, g):
    head = lax.broadcasted_iota(jnp.int32, (GROUP * QBLK, 1), 0) // QBLK
    col = jnp.full((GROUP * QBLK, 1), sink_ref[g * GROUP], F32)
    for hh in range(1, GROUP):
        col = jnp.where(head == hh, sink_ref[g * GROUP + hh], col)
    return col


def _attn_specs():
    cur = lambda i: (i, 0)
    prev = lambda i: (jnp.maximum(i - 1, 0), 0)
    return [pl.BlockSpec((QBLK, D_KV), cur), pl.BlockSpec((QBLK, D_KV), prev),
            pl.BlockSpec((QBLK, D_KV), cur), pl.BlockSpec((QBLK, D_KV), prev)]


def _attn_fwd(q, k, v, sinks, comm=None):
    S = q.shape[0]

    def body(sink_ref, q_ref, kc_ref, kp_ref, vc_ref, vp_ref, o_ref):
        valid = _band_mask(pl.program_id(0))
        outs = []
        qv = q_ref[...]
        kall = jnp.concatenate([kp_ref[...], kc_ref[...]], axis=0)
        vall = jnp.concatenate([vp_ref[...], vc_ref[...]], axis=0)
        for g in range(N_KV):
            kcat = kall[:, g * HEAD_DIM:(g + 1) * HEAD_DIM]
            vcat = vall[:, g * HEAD_DIM:(g + 1) * HEAD_DIM]
            s = jnp.where(valid, _dot_nt(_stack_heads(qv, g), kcat), -1e30)
            sink = _sink_column(sink_ref, g)
            m = jnp.maximum(jnp.max(s, axis=1, keepdims=True), sink)
            p = jnp.exp(s - m)
            l = jnp.sum(p, axis=1, keepdims=True) + jnp.exp(sink - m)
            outs += _unstack_heads(_dot(p.astype(BF16), vcat) / l)
        o_ref[...] = jnp.concatenate(outs, axis=1).astype(BF16)

    return _pcall(
        body, (sinks, q, k, k, v, v), name="attn_fwd", grid=(S // QBLK,), comm=comm,
        in_specs=[pl.BlockSpec(memory_space=pltpu.SMEM), _rows(QBLK, D_ATT)] + _attn_specs(),
        out_specs=[_rows(QBLK, D_ATT)], out_shape=[jax.ShapeDtypeStruct((S, D_ATT), BF16)])


def _w_rows(w_ref):
    return [w_ref[k:k + 1, :] for k in range(w_ref.shape[0])]


def _conv4(x, halo, w, b):
    y = b + w[3] * x
    for s in (1, 2, 3):
        y = y + w[3 - s] * _shift_down(x, halo, s)
    return y


def _rnn_gates(xc, wa, wx, ba, bx, sp):
    xcb = xc.astype(BF16)
    r = _sigmoid(_dot(xcb, wa) + ba)
    ig = _sigmoid(_dot(xcb, wx) + bx)
    la = -LRU_C * r * sp
    a = jnp.exp(la)
    t = jnp.tanh(la)
    f = jnp.sqrt(-2.0 * t / (1.0 - t))
    return r, ig, a, f


def _rnn_fwd(xr, gr, conv_w, conv_b, wa, wx, ba, bx, lam, comm=None):
    S = xr.shape[0]
    tb = min(256, S)

    def body(xr_ref, gr_ref, cw_ref, cb_ref, wa_ref, wx_ref, ba_ref, bx_ref, lam_ref, rec_ref, h_ref,
             halo_s, hc_s, a_s, b_s):
        @pl.when(pl.program_id(0) == 0)
        def _():
            halo_s[...] = jnp.zeros_like(halo_s)
            hc_s[...] = jnp.zeros_like(hc_s)

        x = xr_ref[...]
        xc = _conv4(x, halo_s[...], _w_rows(cw_ref), cb_ref[...])
        halo_s[...] = x[tb - 8:]
        _, ig, a, f = _rnn_gates(xc, wa_ref[...], wx_ref[...], ba_ref[...], bx_ref[...], _softplus_neg(lam_ref[...]))
        a_s[...] = a
        b_s[...] = f * ig * xc
        row8 = lax.broadcasted_iota(jnp.int32, (8, D_RNN), 0)

        def tile(t, hc):
            o = pl.multiple_of(t * 8, 8)
            at = a_s[pl.ds(o, 8), :]
            bt = b_s[pl.ds(o, 8), :]
            for s in (1, 2, 4):
                keep = row8 >= s
                a_sh = jnp.where(keep, pltpu.roll(at, s, 0), 1.0)
                b_sh = jnp.where(keep, pltpu.roll(bt, s, 0), 0.0)
                bt = at * b_sh + bt
                at = at * a_sh
            ht = at * hc + bt
            b_s[pl.ds(o, 8), :] = ht
            return _row_sum(jnp.where(row8 == 7, ht, 0.0))

        hc_s[0:1, :] = lax.fori_loop(0, tb // 8, tile, hc_s[0:1, :])
        h = b_s[...]
        h_ref[...] = h
        rec_ref[...] = (h * _gelu(gr_ref[...])).astype(BF16)

    vec = _resident((1, D_RNN))
    return _pcall(
        body, (xr, gr, conv_w, conv_b, wa, wx, ba, bx, lam), name="rnn_fwd", grid=(S // tb,), sem="arbitrary", comm=comm,
        in_specs=[_rows(tb, D_RNN), _rows(tb, D_RNN), _resident((4, D_RNN)), vec,
                  _resident((D_RNN, D_RNN)), _resident((D_RNN, D_RNN)), vec, vec, vec],
        out_specs=[_rows(tb, D_RNN), _rows(tb, D_RNN)],
        out_shape=[jax.ShapeDtypeStruct((S, D_RNN), BF16), jax.ShapeDtypeStruct((S, D_RNN), F32)],
        scratch_shapes=[pltpu.VMEM((8, D_RNN), F32), pltpu.VMEM((8, D_RNN), F32),
                        pltpu.VMEM((tb, D_RNN), F32), pltpu.VMEM((tb, D_RNN), F32)])


def _mix_ln1_up(x, att, rec, w_out, ln1_g, ln1_b, w_up, fcw, fcb, comm=None):
    S = x.shape[0]
    tb = min(256, S)
    nblk, _, wblk = w_up.shape
    half = nblk // 2

    def body(x_ref, att_ref, rec_ref, wo_ref, g_ref, b_ref, wu_ref, fcw_ref, fcb_ref,
             z1_ref, h1_ref, h1b_ref, gate_ref, act_ref, gl_ref, vdgl_ref, halo_s):
        @pl.when(pl.program_id(0) == 0)
        def _():
            halo_s[...] = jnp.zeros_like(halo_s)

        z1 = ALPHA * x_ref[...] + _dot(att_ref[...], wo_ref[:D_ATT, :]) + _dot(rec_ref[...], wo_ref[D_ATT:, :])
        z1_ref[...] = z1
        xhat, _ = _ln_stats(z1)
        h1 = xhat * g_ref[...] + b_ref[...]
        h1_ref[...] = h1
        h1b = h1.astype(BF16)
        h1b_ref[...] = h1b
        for jj in range(half):
            cols = slice(jj * wblk, (jj + 1) * wblk)
            gate = _dot(h1b, wu_ref[jj])
            val = _dot(h1b, wu_ref[jj + half])
            halo = halo_s[:, cols]
            conv = (fcb_ref[:, cols] + fcw_ref[2:3, cols] * gate + fcw_ref[1:2, cols] * _shift_down(gate, halo, 1)
                    + fcw_ref[0:1, cols] * _shift_down(gate, halo, 2))
            halo_s[:, cols] = gate[tb - 8:]
            gl, dgl = _gelu_and_grad(conv)
            gate_ref[:, cols] = gate.astype(BF16)
            act_ref[:, cols] = (gl * val).astype(BF16)
            gl_ref[:, cols] = gl.astype(BF16)
            vdgl_ref[:, cols] = (val * dgl).astype(BF16)

    vec = _resident((1, D_MODEL))
    wide = jax.ShapeDtypeStruct((S, D_FF), BF16)
    return _pcall(
        body, (x, att, rec, w_out, ln1_g, ln1_b, w_up, fcw, fcb), name="mix_ln1_up", grid=(S // tb,),
        sem="arbitrary", comm=comm,
        in_specs=[_rows(tb, D_MODEL), _rows(tb, D_ATT), _rows(tb, D_RNN), _resident((D_MODEL, D_MODEL)), vec, vec,
                  _resident(w_up.shape), _resident((3, D_FF)), _resident((1, D_FF))],
        out_specs=[_rows(tb, D_MODEL), _rows(tb, D_MODEL), _rows(tb, D_MODEL)] + [_rows(tb, D_FF)] * 4,
        out_shape=[jax.ShapeDtypeStruct((S, D_MODEL), F32), jax.ShapeDtypeStruct((S, D_MODEL), F32),
                   jax.ShapeDtypeStruct((S, D_MODEL), BF16), wide, wide, wide, wide],
        scratch_shapes=[pltpu.VMEM((8, D_FF), F32)])


def _tail(act, gl, vdgl, h1, h1b, p, tgt, w_down, w_pg, b_pg, w_pp, ln2_g, ln2_b):
    S = h1.shape[0]
    tb = min(256, S)

    def body(act_ref, gl_ref, vdgl_ref, h1_ref, h1b_ref, p_ref, t_ref, wd_ref, wpg_ref, bpg_ref, wpp_ref, g2_ref, b2_ref,
             dz2_ref, dpre_ref, dpp_ref, dgc_ref, dval_ref, dh1_ref, acc_ref):
        i = pl.program_id(0)

        @pl.when(i == 0)
        def _():
            acc_ref[...] = jnp.zeros_like(acc_ref)

        ffn = _dot(act_ref[...], wd_ref[...])
        h1 = h1_ref[...]
        sg = _sigmoid(_dot(h1b_ref[...], wpg_ref[...]) + bpg_ref[...])
        pp = _dot(p_ref[...].astype(BF16), wpp_ref[...])
        z2 = ALPHA * h1 + ffn + sg * pp
        xhat2, rstd2 = _ln_stats(z2)
        y = xhat2 * g2_ref[...] + b2_ref[...]
        err = y - t_ref[...]
        dy = err * (1.0 / D_MODEL)
        loss = 0.5 * jnp.sum(jnp.sum(err * err, axis=1, keepdims=True), axis=0, keepdims=True) * (1.0 / D_MODEL)
        dz2 = _ln_bwd(dy, xhat2, rstd2, g2_ref[...])
        dz2b = dz2.astype(BF16)
        dz2_ref[...] = dz2b
        dpre = dz2 * pp * sg * (1.0 - sg)
        dpreb = dpre.astype(BF16)
        dpre_ref[...] = dpreb
        dpp_ref[...] = (dz2 * sg).astype(BF16)
        dh1_ref[...] = ALPHA * dz2 + _dot_nt(dpreb, wpg_ref[...])
        dactb = _dot_nt(dz2b, wd_ref[...]).astype(BF16)
        dval_ref[...] = dactb * gl_ref[...]
        dgc_ref[...] = dactb * vdgl_ref[...]
        _put_rows(acc_ref, [_row_sum(dy * xhat2), _row_sum(dy), _row_sum(dpre),
                            jnp.broadcast_to(loss, (1, D_MODEL))])

    vec = _resident((1, D_MODEL))
    return pl.pallas_call(
        body, name="tail", grid=(S // tb,),
        in_specs=[_rows(tb, D_FF), _rows(tb, D_FF), _rows(tb, D_FF), _rows(tb, D_MODEL), _rows(tb, D_MODEL),
                  _rows(tb, PLE_DIM), _rows(tb, D_MODEL), _resident((D_FF, D_MODEL)), _resident((D_MODEL, D_MODEL)), vec,
                  _resident((PLE_DIM, D_MODEL)), vec, vec],
        out_specs=[_rows(tb, D_MODEL), _rows(tb, D_MODEL), _rows(tb, D_MODEL), _rows(tb, D_FF),
                   _rows(tb, D_FF), _rows(tb, D_MODEL), _acc((8, D_MODEL))],
        out_shape=[jax.ShapeDtypeStruct((S, D_MODEL), BF16),
                   jax.ShapeDtypeStruct((S, D_MODEL), BF16), jax.ShapeDtypeStruct((S, D_MODEL), BF16),
                   jax.ShapeDtypeStruct((S, D_FF), BF16), jax.ShapeDtypeStruct((S, D_FF), BF16),
                   jax.ShapeDtypeStruct((S, D_MODEL), F32), jax.ShapeDtypeStruct((8, D_MODEL), F32)],
        compiler_params=_params("arbitrary"),
    )(act, gl, vdgl, h1, h1b, p, tgt, w_down, w_pg, b_pg, w_pp, ln2_g, ln2_b)


def _weight_grad(a_list, b_list, name, layout, ts=512, comm=None):
    S = a_list[0].shape[0]
    ms = [a.shape[1] for a in a_list]
    M, nb, Nb = sum(ms), len(b_list), b_list[0].shape[1]
    ts = min(ts, S)
    nk = S // ts
    per_b = N_DEV // nb
    na = len(a_list)

    def body(*refs):
        a_refs, b_refs, o_ref, acc_ref = refs[:na], refs[na:na + nb], refs[na + nb], refs[na + nb + 1]
        j, k = pl.program_id(0), pl.program_id(1)

        @pl.when(k == 0)
        def _():
            acc_ref[...] = jnp.zeros_like(acc_ref)

        for jj in range(nb):
            @pl.when(j == jj)
            def _():
                b = b_refs[jj][...].astype(BF16)
                off = 0
                for a_ref, m in zip(a_refs, ms):
                    acc_ref[off:off + m, :] += _dot_tn(a_ref[...].astype(BF16), b)
                    off += m

        @pl.when(k == nk - 1)
        def _():
            for d in range(per_b):
                if layout == "rows":
                    o_ref[d] = acc_ref[d * (M // N_DEV):(d + 1) * (M // N_DEV), :].astype(BF16)
                elif layout == "cols":
                    o_ref[d] = acc_ref[:, d * (Nb // per_b):(d + 1) * (Nb // per_b)].astype(BF16)
                else:
                    o_ref[d] = acc_ref[:, d * (Nb // per_b):(d + 1) * (Nb // per_b)].T.astype(BF16)

    def b_index(jj):
        return lambda j, k: (jnp.where(j == jj, k, jnp.where(j < jj, 0, nk - 1)), 0)

    if layout == "rows":
        assert nb == 1
        blk = (N_DEV, M // N_DEV, Nb)
    elif layout == "cols":
        blk = (per_b, M, Nb // per_b)
    else:
        blk = (per_b, Nb // per_b, M)
    (res,), comm_res = _pcall(
        body, (*a_list, *b_list), name=name, grid=(nb, nk), sem="arbitrary", comm=comm, step_axis=1,
        in_specs=[pl.BlockSpec((ts, m), lambda j, k: (k, 0)) for m in ms]
        + [pl.BlockSpec((ts, Nb), b_index(jj)) for jj in range(nb)],
        out_specs=[pl.BlockSpec(blk, lambda j, k: (j, 0, 0))],
        out_shape=[jax.ShapeDtypeStruct((N_DEV,) + blk[1:], BF16)],
        scratch_shapes=[pltpu.VMEM((M, Nb), F32)])
    return (res, comm_res) if comm is not None else res


def _up_bwd(dgc, gate, dval, dh1p, z1, w_up, fcw, w_out, ln1_g, comm=None):
    S = z1.shape[0]
    tb = min(256, S)
    t16 = tb // 16
    n16 = S // 16
    nblk, _, wblk = w_up.shape
    half = nblk // 2
    nsteps = S // tb

    def body(dgc_ref, dgn_ref, gc_ref, gp_ref, dval_ref, dh1p_ref, z1_ref, wu_ref, fcw_ref, wo_ref, g1_ref,
             dgate_ref, dz1_ref, dz1b_ref, datt_ref, drec_ref, accf_ref, accd_ref):
        i = pl.program_id(0)

        @pl.when(i == 0)
        def _():
            accf_ref[...] = jnp.zeros_like(accf_ref)
            accd_ref[...] = jnp.zeros_like(accd_ref)

        dg = dgc_ref[...].astype(F32)
        nxt = jnp.where(i < nsteps - 1, dgn_ref[...].astype(F32)[0:8], 0.0)
        w = _w_rows(fcw_ref)
        dgate = (w[2] * dg + w[1] * _shift_up(dg, nxt, 1) + w[0] * _shift_up(dg, nxt, 2)).astype(BF16)
        dgate_ref[...] = dgate
        gate = gc_ref[...].astype(F32)
        halo = jnp.where(i > 0, gp_ref[...].astype(F32)[8:16], 0.0)
        _put_rows(accf_ref, [_row_sum(dg * _shift_down(gate, halo, 2)), _row_sum(dg * _shift_down(gate, halo, 1)),
                             _row_sum(dg * gate), _row_sum(dg)])

        dh1 = dh1p_ref[...]
        for j in range(nblk):
            src = dgate if j < half else dval_ref[...]
            jj = j % half
            dh1 = dh1 + _dot_nt(src[:, jj * wblk:(jj + 1) * wblk], wu_ref[j])
        xhat1, rstd1 = _ln_stats(z1_ref[...])
        dz1 = _ln_bwd(dh1, xhat1, rstd1, g1_ref[...])
        dz1_ref[...] = dz1
        dz1b = dz1.astype(BF16)
        dz1b_ref[...] = dz1b
        dcat = _dot_nt(dz1b, wo_ref[...])
        datt_ref[...] = dcat[:, :D_ATT].astype(BF16)
        drec_ref[...] = dcat[:, D_ATT:]
        _put_rows(accd_ref, [_row_sum(dh1 * xhat1), _row_sum(dh1)])

    prev16 = pl.BlockSpec((16, D_FF), lambda i: (jnp.maximum(i * t16 - 1, 0), 0))
    next16 = pl.BlockSpec((16, D_FF), lambda i: (jnp.minimum((i + 1) * t16, n16 - 1), 0))
    return _pcall(
        body, (dgc, dgc, gate, gate, dval, dh1p, z1, w_up, fcw, w_out, ln1_g), name="up_bwd",
        grid=(nsteps,), sem="arbitrary", comm=comm,
        in_specs=[_rows(tb, D_FF), next16, _rows(tb, D_FF), prev16, _rows(tb, D_FF), _rows(tb, D_MODEL),
                  _rows(tb, D_MODEL), _resident(w_up.shape), _resident((3, D_FF)),
                  _resident((D_MODEL, D_MODEL)), _resident((1, D_MODEL))],
        out_specs=[_rows(tb, D_FF), _rows(tb, D_MODEL), _rows(tb, D_MODEL), _rows(tb, D_ATT), _rows(tb, D_RNN),
                   _acc((8, D_FF)), _acc((8, D_MODEL))],
        out_shape=[jax.ShapeDtypeStruct((S, D_FF), BF16), jax.ShapeDtypeStruct((S, D_MODEL), F32),
                   jax.ShapeDtypeStruct((S, D_MODEL), BF16), jax.ShapeDtypeStruct((S, D_ATT), BF16),
                   jax.ShapeDtypeStruct((S, D_RNN), F32), jax.ShapeDtypeStruct((8, D_FF), F32),
                   jax.ShapeDtypeStruct((8, D_MODEL), F32)])


def _attn_bwd(q, k, v, do, sinks, comm=None):
    S = q.shape[0]
    grp = N_HEADS // N_KV

    def body(sink_ref, q_ref, kc_ref, kp_ref, vc_ref, vp_ref, do_ref, dq_ref, dkc_ref, dkp_ref, dvc_ref, dvp_ref,
             ds_ref):
        i = pl.program_id(0)

        @pl.when(i == 0)
        def _():
            ds_ref[...] = jnp.zeros_like(ds_ref)

        valid = _band_mask(i)
        row8 = lax.broadcasted_iota(jnp.int32, (8, 128), 0)
        lane8 = lax.broadcasted_iota(jnp.int32, (8, 128), 1)
        dqs, dks, dvs = [], [], []
        dsink = jnp.zeros((8, 128), F32)
        qv = q_ref[...]
        dov = do_ref[...]
        kall = jnp.concatenate([kp_ref[...], kc_ref[...]], axis=0)
        vall = jnp.concatenate([vp_ref[...], vc_ref[...]], axis=0)
        for g in range(N_KV):
            kcat = kall[:, g * HEAD_DIM:(g + 1) * HEAD_DIM]
            vcat = vall[:, g * HEAD_DIM:(g + 1) * HEAD_DIM]
            q4, do4 = _stack_heads(qv, g), _stack_heads(dov, g)
            s = jnp.where(valid, _dot_nt(q4, kcat), -1e30)
            sink = _sink_column(sink_ref, g)
            m = jnp.maximum(jnp.max(s, axis=1, keepdims=True), sink)
            e = jnp.exp(s - m)
            es = jnp.exp(sink - m)
            inv = 1.0 / (jnp.sum(e, axis=1, keepdims=True) + es)
            p = e * inv
            dp = _dot_nt(do4, vcat)
            delta = jnp.sum(p * dp, axis=1, keepdims=True)
            dsc = (p * (dp - delta)).astype(BF16)
            dqs += _unstack_heads(_dot(dsc, kcat) * (HEAD_DIM ** -0.5))
            dks.append(_dot_tn(q4, dsc))
            dvs.append(_dot_tn(do4, p.astype(BF16)))
            for hh, part in enumerate(_unstack_heads(-es * inv * delta)):
                here = (row8 == 0) & (lane8 == g * grp + hh)
                dsink = dsink + jnp.where(here, jnp.sum(part, axis=0, keepdims=True), 0.0)
        dq_ref[...] = jnp.concatenate(dqs, axis=1).astype(BF16)
        dk = jnp.concatenate(dks, axis=0).T
        dv = jnp.concatenate(dvs, axis=0).T
        dkp_ref[...] = dk[:QBLK]
        dkc_ref[...] = dk[QBLK:]
        dvp_ref[...] = dv[:QBLK]
        dvc_ref[...] = dv[QBLK:]
        ds_ref[...] += dsink

    kvs = jax.ShapeDtypeStruct((S, D_KV), F32)
    return _pcall(
        body, (sinks, q, k, k, v, v, do), name="attn_bwd", grid=(S // QBLK,), sem="arbitrary", comm=comm,
        in_specs=[pl.BlockSpec(memory_space=pltpu.SMEM), _rows(QBLK, D_ATT)] + _attn_specs() + [_rows(QBLK, D_ATT)],
        out_specs=[_rows(QBLK, D_ATT), _rows(QBLK, D_KV), _rows(QBLK, D_KV), _rows(QBLK, D_KV), _rows(QBLK, D_KV),
                   _acc((8, 128))],
        out_shape=[jax.ShapeDtypeStruct((S, D_ATT), BF16), kvs, kvs, kvs, kvs, jax.ShapeDtypeStruct((8, 128), F32)])


def _rnn_bwd(xr, gr, h, drec, conv_w, conv_b, wa, wx, ba, bx, lam, comm=None):
    S = xr.shape[0]
    tb = min(256, S)
    t8 = tb // 8
    nsteps = S // tb

    def body(xr_ref, xp_ref, gr_ref, h_ref, hp_ref, drec_ref, cw_ref, cb_ref, wa_ref, wx_ref, ba_ref, bx_ref, lam_ref,
             dxr_ref, dgr_ref, gwa_ref, gwx_ref, acc_ref, carry_s, dxc_halo_s, a_s, d_s, gwa_s, gwx_s):
        i = pl.program_id(0)
        blk = nsteps - 1 - i

        @pl.when(i == 0)
        def _():
            gwa_s[...] = jnp.zeros_like(gwa_s)
            gwx_s[...] = jnp.zeros_like(gwx_s)
            acc_ref[...] = jnp.zeros_like(acc_ref)
            carry_s[...] = jnp.zeros_like(carry_s)
            dxc_halo_s[...] = jnp.zeros_like(dxc_halo_s)

        x = xr_ref[...]
        xhalo = jnp.where(blk > 0, xp_ref[...], 0.0)
        cw = _w_rows(cw_ref)
        xs = [_shift_down(x, xhalo, 3), _shift_down(x, xhalo, 2), _shift_down(x, xhalo, 1), x]
        xc = cb_ref[...] + cw[0] * xs[0] + cw[1] * xs[1] + cw[2] * xs[2] + cw[3] * xs[3]
        sp = _softplus_neg(lam_ref[...])
        r, ig, a, f = _rnn_gates(xc, wa_ref[...], wx_ref[...], ba_ref[...], bx_ref[...], sp)
        hcur = h_ref[...]
        hprev = _shift_down(hcur, jnp.where(blk > 0, hp_ref[...], 0.0), 1)
        gl, dgl = _gelu_and_grad(gr_ref[...])
        drec = drec_ref[...]
        dgr_ref[...] = (drec * hcur * dgl).astype(BF16)
        a_s[...] = a
        d_s[...] = drec * gl
        row8 = lax.broadcasted_iota(jnp.int32, (8, D_RNN), 0)

        def tile(t, c):
            o = pl.multiple_of((t8 - 1 - t) * 8, 8)
            a8 = a_s[pl.ds(o, 8), :]
            dt = d_s[pl.ds(o, 8), :]
            at = jnp.where(row8 == 7, 1.0, pltpu.roll(a8, 7, 0))
            for s in (1, 2, 4):
                keep = row8 < 8 - s
                a_sh = jnp.where(keep, pltpu.roll(at, 8 - s, 0), 1.0)
                d_sh = jnp.where(keep, pltpu.roll(dt, 8 - s, 0), 0.0)
                dt = at * d_sh + dt
                at = at * a_sh
            lt = at * c + dt
            d_s[pl.ds(o, 8), :] = lt
            return _row_sum(jnp.where(row8 == 0, a8 * lt, 0.0))

        carry_s[0:1, :] = lax.fori_loop(0, t8, tile, carry_s[0:1, :])
        lmb = d_s[...]
        a2 = a * a
        dla = lmb * hprev * a - lmb * ig * xc * (a2 / f)
        di = lmb * f * xc
        dr = dla * (-LRU_C) * sp
        dpa = dr * r * (1.0 - r)
        dpx = di * ig * (1.0 - ig)
        dpab = dpa.astype(BF16)
        dpxb = dpx.astype(BF16)
        xcb = xc.astype(BF16)
        gwa_s[...] += _dot_tn(xcb, dpab)
        gwx_s[...] += _dot_tn(xcb, dpxb)

        @pl.when(i == nsteps - 1)
        def _():
            for dense, out in ((gwa_s[...], gwa_ref), (gwx_s[...], gwx_ref)):
                for b in range(RNN_BLOCKS):
                    rows = slice(b * HEAD_DIM, (b + 1) * HEAD_DIM)
                    out[rows, :] = dense[rows, b * HEAD_DIM:(b + 1) * HEAD_DIM]

        dxc = lmb * f * ig + _dot_nt(dpab, wa_ref[...]) + _dot_nt(dpxb, wx_ref[...])
        nxt = dxc_halo_s[...]
        dxr = cw[3] * dxc
        for s in (1, 2, 3):
            dxr = dxr + cw[3 - s] * _shift_up(dxc, nxt, s)
        dxr_ref[...] = dxr.astype(BF16)
        dxc_halo_s[...] = dxc[:8]
        dlam = _row_sum(dla * (-LRU_C) * r) * (-1.0 / (1.0 + jnp.exp(lam_ref[...])))
        _put_rows(acc_ref, [_row_sum(dxc * xs[0]), _row_sum(dxc * xs[1]), _row_sum(dxc * xs[2]), _row_sum(dxc * xs[3]),
                            _row_sum(dxc), _row_sum(dpa), _row_sum(dpx), dlam])

    rev = lambda i: (nsteps - 1 - i, 0)
    prev8 = lambda i: (jnp.maximum((nsteps - 1 - i) * t8 - 1, 0), 0)
    blkspec = pl.BlockSpec((tb, D_RNN), rev)
    halo8 = pl.BlockSpec((8, D_RNN), prev8)
    vec = _resident((1, D_RNN))
    return _pcall(
        body, (xr, xr, gr, h, h, drec, conv_w, conv_b, wa, wx, ba, bx, lam), name="rnn_bwd", grid=(nsteps,),
        sem="arbitrary", comm=comm,
        in_specs=[blkspec, halo8, blkspec, blkspec, halo8, blkspec, _resident((4, D_RNN)), vec,
                  _resident((D_RNN, D_RNN)), _resident((D_RNN, D_RNN)), vec, vec, vec],
        out_specs=[blkspec, blkspec, _acc((D_RNN, HEAD_DIM)), _acc((D_RNN, HEAD_DIM)), _acc((8, D_RNN))],
        out_shape=[jax.ShapeDtypeStruct((S, D_RNN), BF16), jax.ShapeDtypeStruct((S, D_RNN), BF16),
                   jax.ShapeDtypeStruct((D_RNN, HEAD_DIM), F32), jax.ShapeDtypeStruct((D_RNN, HEAD_DIM), F32),
                   jax.ShapeDtypeStruct((8, D_RNN), F32)],
        scratch_shapes=[pltpu.VMEM((8, D_RNN), F32), pltpu.VMEM((8, D_RNN), F32),
                        pltpu.VMEM((tb, D_RNN), F32), pltpu.VMEM((tb, D_RNN), F32),
                        pltpu.VMEM((D_RNN, D_RNN), F32), pltpu.VMEM((D_RNN, D_RNN), F32)])


def _in_bwd(dq, dkc, dkp, dvc, dvp, dxr, dgr, dz1, w_in, comm=None):
    S = dz1.shape[0]
    tb = min(512, S)
    nsteps = S // tb
    nq = S // QBLK
    r = tb // QBLK

    def body(dq_ref, dkc_ref, dkp_ref, dkn_ref, dvc_ref, dvp_ref, dvn_ref, dxr_ref, dgr_ref, dz1_ref, w_ref,
             du_ref, dx_ref):
        i = pl.program_id(0)
        last = i == nsteps - 1

        def shifted(prev_ref, next_ref):
            nxt = jnp.where(last, 0.0, next_ref[...])
            return jnp.concatenate([prev_ref[QBLK:], nxt], axis=0) if r > 1 else nxt

        dk = (dkc_ref[...] + shifted(dkp_ref, dkn_ref)).astype(BF16)
        dv = (dvc_ref[...] + shifted(dvp_ref, dvn_ref)).astype(BF16)
        du = jnp.concatenate([dq_ref[...], dk, dv, dxr_ref[...], dgr_ref[...]], axis=1)
        du_ref[...] = du
        dx_ref[...] = ALPHA * dz1_ref[...] + _dot(du, w_ref[...])

    nextq = pl.BlockSpec((QBLK, D_KV), lambda i: (jnp.minimum((i + 1) * r, nq - 1), 0))
    return _pcall(
        body, (dq, dkc, dkp, dkp, dvc, dvp, dvp, dxr, dgr, dz1, w_in), name="in_bwd", grid=(nsteps,), comm=comm,
        in_specs=[_rows(tb, D_ATT), _rows(tb, D_KV), _rows(tb, D_KV), nextq, _rows(tb, D_KV), _rows(tb, D_KV), nextq,
                  _rows(tb, D_RNN), _rows(tb, D_RNN), _rows(tb, D_MODEL), _resident((D_IN, D_MODEL))],
        out_specs=[_rows(tb, D_IN), _rows(tb, D_MODEL)],
        out_shape=[jax.ShapeDtypeStruct((S, D_IN), BF16), jax.ShapeDtypeStruct((S, D_MODEL), F32)])


def _block_diag(w):
    eye = jnp.eye(RNN_BLOCKS, dtype=w.dtype)
    return (w[:, :, None, :] * eye[:, None, :, None]).reshape(D_RNN, D_RNN).astype(BF16)


def _adamw(w, g, m, v):
    m = ADAM_B1 * m + (1.0 - ADAM_B1) * g
    v = ADAM_B2 * v + (1.0 - ADAM_B2) * (g * g)
    m_hat = m / (1.0 - ADAM_B1 ** ADAM_STEP)
    v_hat = v / (1.0 - ADAM_B2 ** ADAM_STEP)
    delta = -ADAM_LR * (m_hat / (jnp.sqrt(v_hat) + ADAM_EPS) + ADAM_WD * w)
    return delta, m, v


def _sum_adamw(parts, w, m, v, name):
    R, C = w.shape
    rb = R if R <= 256 else 128
    assert R % rb == 0

    def body(p_ref, w_ref, m_ref, v_ref, g_out, d_out, m_out, v_out):
        g = p_ref[0].astype(F32)
        for d in range(1, N_DEV):
            g = g + p_ref[d].astype(F32)
        delta, mn, vn = _adamw(w_ref[...], g, m_ref[...], v_ref[...])
        g_out[...] = g
        d_out[...] = delta
        m_out[...] = mn
        v_out[...] = vn

    blk = _rows(rb, C)
    out = jax.ShapeDtypeStruct((R, C), F32)
    return pl.pallas_call(
        body, name=name, grid=(R // rb,),
        in_specs=[pl.BlockSpec((N_DEV, rb, C), lambda i: (0, i, 0)), blk, blk, blk],
        out_specs=[blk, blk, blk, blk], out_shape=[out, out, out, out],
        compiler_params=_params("parallel"),
    )(parts, w, m, v)


_SMALL = [("attn_sinks", "s", 0, 1, None), ("rnn_conv_w", "r", 0, 4, "cols"), ("rnn_conv_b", "r", 4, 1, None),
          ("gate_a_w", "a", 0, D_RNN, None), ("gate_a_b", "r", 5, 1, None), ("gate_x_w", "x", 0, D_RNN, None),
          ("gate_x_b", "r", 6, 1, None), ("lru_lambda", "r", 7, 1, None), ("ln1_g", "d", 0, 1, None),
          ("ln1_b", "d", 1, 1, None), ("ffn_conv_w", "f", 0, 3, "cols"), ("ffn_conv_b", "f", 3, 1, None),
          ("ple_gate_b", "t", 2, 1, None), ("ln2_g", "t", 0, 1, None), ("ln2_b", "t", 1, 1, None)]
_LOSS_ROW = 3


_ACC_COLS = {"t": (0, D_MODEL), "f": (D_MODEL, D_FF), "d": (D_MODEL + D_FF, D_MODEL), "s": (2 * D_MODEL + D_FF, 128),
             "r": (2 * D_MODEL + D_FF + 128, D_RNN)}
_ACC_WIDTH = 2 * D_MODEL + D_FF + 128 + D_RNN


def _small_update(rows_all, gates_all, params):
    flat = [arr for triple in params for arr in triple]
    n_par = len(_SMALL)

    def body(*refs):
        rows_ref, gates_ref = refs[:2]
        p_refs = refs[2:2 + 3 * n_par]
        loss_ref = refs[2 + 3 * n_par]
        o_refs = refs[3 + 3 * n_par:3 + 7 * n_par]
        rows_s, tmp_r, tmp_f = refs[3 + 7 * n_par:]
        me = _dev_index(*_place())
        rows_sum, gates_sum = rows_ref[0], gates_ref[0]
        for d in range(1, N_DEV):
            rows_sum = rows_sum + rows_ref[d]
            gates_sum = gates_sum + gates_ref[d]
        rows_s[...] = rows_sum
        t0 = _ACC_COLS["t"][0]
        loss_ref[...] = rows_s[_LOSS_ROW:_LOSS_ROW + 1, t0:t0 + 128]
        for i, (name, key, row, rows, how) in enumerate(_SMALL):
            w_ref, m_ref, v_ref = p_refs[3 * i:3 * i + 3]
            g_out, d_out, m_out, v_out = o_refs[4 * i:4 * i + 4]
            if key == "a":
                g = gates_sum[:, :HEAD_DIM]
            elif key == "x":
                g = gates_sum[:, HEAD_DIM:]
            elif how == "cols":
                c0, width = _ACC_COLS[key]
                full = rows_s[:, c0:c0 + width]
                shard = width // N_DEV
                mine = full[:, :shard]
                for d in range(1, N_DEV):
                    mine = jnp.where(me == d, full[:, d * shard:(d + 1) * shard], mine)
                tmp = tmp_r if key == "r" else tmp_f
                tmp[...] = mine
                g = tmp[row:row + rows, :]
            else:
                c0, width = _ACC_COLS[key]
                g = rows_s[row:row + rows, c0:c0 + width][:, :w_ref.shape[1]]
            delta, mn, vn = _adamw(w_ref[...], g, m_ref[...], v_ref[...])
            g_out[...] = g
            d_out[...] = delta
            m_out[...] = mn
            v_out[...] = vn

    outs = [jax.ShapeDtypeStruct((1, 128), F32)]
    for w, _, _ in params:
        outs += [jax.ShapeDtypeStruct(w.shape, F32)] * 4
    scratch = [pltpu.VMEM((8, _ACC_WIDTH), F32), pltpu.VMEM((8, D_RNN // N_DEV), F32), pltpu.VMEM((8, D_FF // N_DEV), F32)]
    res = pl.pallas_call(body, name="small_update", out_shape=outs, scratch_shapes=scratch)(rows_all, gates_all, *flat)
    return res[0], [res[1 + 4 * i:5 + 4 * i] for i in range(n_par)]


def kernel(x, p, w_in, attn_sinks, rnn_conv_w, rnn_conv_b, gate_a_w, gate_a_b, gate_x_w, gate_x_b, lru_lambda, w_out, ln1_g, ln1_b, w_ffn_up, ffn_conv_w, ffn_conv_b, w_ffn_down, ple_gate_w, ple_gate_b, ple_proj, ln2_g, ln2_b, loss_target, m_w_in, m_attn_sinks, m_rnn_conv_w, m_rnn_conv_b, m_gate_a_w, m_gate_a_b, m_gate_x_w, m_gate_x_b, m_lru_lambda, m_w_out, m_ln1_g, m_ln1_b, m_w_ffn_up, m_ffn_conv_w, m_ffn_conv_b, m_w_ffn_down, m_ple_gate_w, m_ple_gate_b, m_ple_proj, m_ln2_g, m_ln2_b, v_w_in, v_attn_sinks, v_rnn_conv_w, v_rnn_conv_b, v_gate_a_w, v_gate_a_b, v_gate_x_w, v_gate_x_b, v_lru_lambda, v_w_out, v_ln1_g, v_ln1_b, v_w_ffn_up, v_ffn_conv_w, v_ffn_conv_b, v_w_ffn_down, v_ple_gate_w, v_ple_gate_b, v_ple_proj, v_ln2_g, v_ln2_b):
    from_col_blocks = lambda g: g.transpose(1, 0, 2).reshape(g.shape[1], N_DEV * g.shape[2])

    xs, ps, tgt, sinks = x[0], p[0, 0], loss_target[0], attn_sinks[0]
    wa, wx = _block_diag(gate_a_w[0]), _block_diag(gate_x_w[0])

    conv_cols = jnp.concatenate([rnn_conv_w[0].reshape(1, -1), ffn_conv_w[0].reshape(1, -1)], axis=1)
    n_rc, n_fc = 4 * D_RNN // N_DEV, 3 * D_FF // N_DEV
    ((g_in,),) = _comm_call([_Gather([w_in[0].T.astype(BF16)])], "gather_w_in")
    w_in_full = g_in.reshape(D_IN, D_MODEL)

    (q, k, v, xr, gr), (g_conv,) = _in_proj(xs, w_in_full, comm=_Bcast([jnp.broadcast_to(conv_cols, (8, n_rc + n_fc))]))
    rcw = from_col_blocks(g_conv[:, 0, :n_rc].reshape(N_DEV, 4, D_RNN // N_DEV))
    fcw = from_col_blocks(g_conv[:, 0, n_rc:].reshape(N_DEV, 3, D_FF // N_DEV))
    (att,), (w_up,) = _attn_fwd(q, k, v, sinks, comm=_Gather([w_ffn_up[0].astype(BF16)]))
    (rec, h), (g_out, g_down) = _rnn_fwd(xr, gr, rcw, rnn_conv_b, wa, wx, gate_a_b, gate_x_b, lru_lambda,
                                         comm=_Gather([w_out[0].astype(BF16), w_ffn_down[0].astype(BF16)]))
    w_out_full = g_out.reshape(D_MODEL, D_MODEL)
    (z1, h1, h1b, gate, act, gl, vdgl), (g_pg, g_pp) = _mix_ln1_up(
        xs, att, rec, w_out_full, ln1_g, ln1_b, w_up, fcw, ffn_conv_b,
        comm=_Gather([ple_gate_w[0].astype(BF16), ple_proj[0].astype(BF16)]))
    dz2b, dpreb, dppb, dgc, dval, dh1p, acc_t = _tail(
        act, gl, vdgl, h1, h1b, ps, tgt, g_down.reshape(D_FF, D_MODEL), g_pg.reshape(D_MODEL, D_MODEL), ple_gate_b,
        from_col_blocks(g_pp), ln2_g, ln2_b)

    gd_down = _weight_grad([dz2b], [act], "down_grad", "rows_t")
    gd_pg = _weight_grad([h1b], [dpreb], "pg_grad", "rows", ts=1024)
    gd_pp = _weight_grad([ps], [dppb], "pp_grad", "cols", ts=1024)
    (dgate, dz1, dz1b, datt, drec, acc_f, acc_d), (r_down, r_pg, r_pp) = _up_bwd(
        dgc, gate, dval, dh1p, z1, w_up, fcw, w_out_full, ln1_g, comm=_Exchange([gd_down, gd_pg, gd_pp]))
    gd_up = _weight_grad([h1b], [dgate, dval], "up_grad", "cols")
    gd_out = _weight_grad([att, rec], [dz1b], "out_grad", "rows", ts=1024)
    (dq, dkc, dkp, dvc, dvp, acc_s), (r_up,) = _attn_bwd(q, k, v, datt, sinks, comm=_Exchange([gd_up]))
    (dxr, dgr, g_wa, g_wx, acc_r), (r_out,) = _rnn_bwd(xr, gr, h, drec, rcw, rnn_conv_b, wa, wx, gate_a_b, gate_x_b,
                                                       lru_lambda, comm=_Exchange([gd_out]))
    (du, dx), _ = _in_bwd(dq, dkc, dkp, dvc, dvp, dxr, dgr, dz1, w_in_full)
    gd_in = _weight_grad([du], [xs], "in_grad", "rows", ts=1024)
    acc_rows = jnp.concatenate([acc_t, acc_f, acc_d, acc_s, acc_r], axis=1)
    small_parts, (r_in,) = _comm_call([_Bcast([acc_rows, jnp.concatenate([g_wa, g_wx], axis=1)]), _Exchange([gd_in])],
                                      "exchange_w_in")

    outs = {}
    res = _sum_adamw(r_in, w_in[0].T, m_w_in[0].T, v_w_in[0].T, "adamw_w_in")
    outs["w_in"] = [r.T[None] for r in res]
    for name, parts, w, m, v in [("w_out", r_out, w_out, m_w_out, v_w_out),
                                 ("w_ffn_up", r_up, w_ffn_up, m_w_ffn_up, v_w_ffn_up),
                                 ("w_ffn_down", r_down, w_ffn_down, m_w_ffn_down, v_w_ffn_down),
                                 ("ple_gate_w", r_pg, ple_gate_w, m_ple_gate_w, v_ple_gate_w),
                                 ("ple_proj", r_pp, ple_proj, m_ple_proj, v_ple_proj)]:
        res = _sum_adamw(parts, w[0], m[0], v[0], "adamw_" + name)
        outs[name] = [r[None] for r in res]

    given = dict(attn_sinks=(attn_sinks, m_attn_sinks, v_attn_sinks), rnn_conv_w=(rnn_conv_w, m_rnn_conv_w, v_rnn_conv_w),
                 rnn_conv_b=(rnn_conv_b, m_rnn_conv_b, v_rnn_conv_b), gate_a_w=(gate_a_w, m_gate_a_w, v_gate_a_w),
                 gate_a_b=(gate_a_b, m_gate_a_b, v_gate_a_b), gate_x_w=(gate_x_w, m_gate_x_w, v_gate_x_w),
                 gate_x_b=(gate_x_b, m_gate_x_b, v_gate_x_b), lru_lambda=(lru_lambda, m_lru_lambda, v_lru_lambda),
                 ln1_g=(ln1_g, m_ln1_g, v_ln1_g), ln1_b=(ln1_b, m_ln1_b, v_ln1_b),
                 ffn_conv_w=(ffn_conv_w, m_ffn_conv_w, v_ffn_conv_w), ffn_conv_b=(ffn_conv_b, m_ffn_conv_b, v_ffn_conv_b),
                 ple_gate_b=(ple_gate_b, m_ple_gate_b, v_ple_gate_b), ln2_g=(ln2_g, m_ln2_g, v_ln2_g),
                 ln2_b=(ln2_b, m_ln2_b, v_ln2_b))
    as_2d = lambda a: a.reshape(-1, a.shape[-1])
    loss_row, small_res = _small_update(*small_parts, [tuple(as_2d(a) for a in given[n]) for n, *_ in _SMALL])
    loss = loss_row[0, 0]
    for (n, *_), res in zip(_SMALL, small_res):
        outs[n] = [r.reshape(given[n][0].shape) for r in res]

    order = ["w_in", "attn_sinks", "rnn_conv_w", "rnn_conv_b", "gate_a_w", "gate_a_b", "gate_x_w", "gate_x_b",
             "lru_lambda", "w_out", "ln1_g", "ln1_b", "w_ffn_up", "ffn_conv_w", "ffn_conv_b", "w_ffn_down",
             "ple_gate_w", "ple_gate_b", "ple_proj", "ln2_g", "ln2_b"]
    return (loss, dx[None], *[outs[n][0] for n in order], *[outs[n][1] for n in order],
            *[outs[n][2] for n in order], *[outs[n][3] for n in order])
```

```python
import jax
import jax.numpy as jnp
from jax import lax
from jax.experimental import pallas as pl
from jax.experimental.pallas import tpu as pltpu

F32 = jnp.float32
BF16 = jnp.bfloat16

D_MODEL = 1024
D_ATT = 512
D_KV = 128
HEAD_DIM = 64
N_HEADS = 8
N_KV = 2
D_RNN = 512
RNN_BLOCKS = 8
D_IN = 1792
D_FF = 3072
PLE_DIM = 256
QBLK = 128
N_DEV = 8
ALPHA = float(2 ** 0.25)
LN_EPS = 1e-5
LRU_C = 8.0
ADAM_LR, ADAM_B1, ADAM_B2, ADAM_EPS, ADAM_WD, ADAM_STEP = 0.001, 0.9, 0.999, 1e-08, 0.01, 10

V7X_VMEM_LIMIT = 56 * 1024 * 1024
MESH = pl.DeviceIdType.MESH


def _params(*sem, vmem=V7X_VMEM_LIMIT):
    return pltpu.CompilerParams(dimension_semantics=sem or None, vmem_limit_bytes=vmem)


def _resident(shape):
    return pl.BlockSpec(shape, lambda *_: (0,) * len(shape), pipeline_mode=pl.Buffered(1))


def _rows(tb, cols):
    return pl.BlockSpec((tb, cols), lambda i: (i, 0))


def _acc(shape):
    return pl.BlockSpec(shape, lambda *_: (0,) * len(shape))


def _dot(a, b):
    return jnp.dot(a, b, preferred_element_type=F32)


def _dot_nt(a, b):
    return lax.dot_general(a, b, (((1,), (1,)), ((), ())), preferred_element_type=F32)


def _dot_tn(a, b):
    return lax.dot_general(a, b, (((0,), (0,)), ((), ())), preferred_element_type=F32)


def _sigmoid(x):
    return 1.0 / (1.0 + jnp.exp(-x))


_GELU_C = 0.7978845608028654
_GELU_K = 0.044715


def _gelu_and_grad(x):
    u = x * x
    t = jnp.tanh(x * (_GELU_C + (_GELU_C * _GELU_K) * u))
    hp = 0.5 + 0.5 * t
    dg = hp + x * (0.5 - 0.5 * (t * t)) * (_GELU_C + (3.0 * _GELU_C * _GELU_K) * u)
    return x * hp, dg


def _gelu(x):
    return 0.5 * x * (1.0 + jnp.tanh(_GELU_C * (x + _GELU_K * x * x * x)))


def _ln_stats(z):
    mu = jnp.mean(z, axis=-1, keepdims=True)
    zc = z - mu
    var = jnp.mean(zc * zc, axis=-1, keepdims=True)
    rstd = lax.rsqrt(var + LN_EPS)
    return zc * rstd, rstd


def _ln_bwd(dy, xhat, rstd, g):
    dxh = dy * g
    m1 = jnp.mean(dxh, axis=-1, keepdims=True)
    m2 = jnp.mean(dxh * xhat, axis=-1, keepdims=True)
    return rstd * (dxh - m1 - xhat * m2)


def _softplus_neg(lam):
    u = jnp.exp(-jnp.abs(lam))
    w = 1.0 + u
    d = w - 1.0
    log1p_u = jnp.where(d == 0.0, u, jnp.log(w) * (u / jnp.where(d == 0.0, 1.0, d)))
    return jnp.maximum(-lam, 0.0) + log1p_u


def _shift_down(x, halo, s):
    xs = pltpu.roll(x, s, 0)
    hs = pltpu.roll(halo, s, 0)
    row8 = lax.broadcasted_iota(jnp.int32, hs.shape, 0)
    first = jnp.where(row8 < s, hs, xs[:8])
    return jnp.concatenate([first, xs[8:]], axis=0)


def _shift_up(x, halo, s):
    n = x.shape[0]
    xs = pltpu.roll(x, n - s, 0)
    hs = pltpu.roll(halo, 8 - s, 0)
    row8 = lax.broadcasted_iota(jnp.int32, hs.shape, 0)
    last = jnp.where(row8 >= 8 - s, hs, xs[n - 8:])
    return jnp.concatenate([xs[:n - 8], last], axis=0)


def _row_sum(x):
    return jnp.sum(x, axis=0, keepdims=True)


def _put_rows(acc_ref, rows):
    row8 = lax.broadcasted_iota(jnp.int32, acc_ref.shape, 0)
    upd = jnp.zeros(acc_ref.shape, F32)
    for r, vec in enumerate(rows):
        upd = jnp.where(row8 == r, vec, upd)
    acc_ref[...] += upd


def _place():
    return lax.axis_index("x"), lax.axis_index("y"), lax.axis_index("c")


def _dev_index(px, py, pc):
    return 4 * px + 2 * py + pc


_ANY = pl.BlockSpec(memory_space=pl.ANY)


class _Gather:
    def __init__(self, arrays):
        self.arrays = list(arrays)
        self.n = len(self.arrays)

    def out_shape(self):
        return [jax.ShapeDtypeStruct((N_DEV,) + s.shape, s.dtype) for s in self.arrays]

    def scratch(self):
        return [pltpu.SemaphoreType.DMA((self.n, 7)), pltpu.SemaphoreType.DMA((self.n, 7)),
                pltpu.SemaphoreType.DMA((self.n,))]

    def _parts(self, ins, outs, sems):
        send_sems, recv_sems, local_sems = sems
        x, y, c = _place()
        me, sibling = (x, y, c), (x, y, 1 - c)
        chips = [(1 - x, y), (x, 1 - y), (1 - x, 1 - y)]

        def copy(a, k, block, to, src=None):
            rows = outs[a].at[_dev_index(*block)]
            return pltpu.make_async_remote_copy(
                src_ref=rows if src is None else src, dst_ref=rows, send_sem=send_sems.at[a, k],
                recv_sem=recv_sems.at[a, k], device_id=to, device_id_type=MESH)

        rng = range(self.n)
        mine = [pltpu.make_async_copy(ins[a], outs[a].at[_dev_index(*me)], local_sems.at[a]) for a in rng]
        first = [copy(a, 0, me, sibling, src=ins[a]) for a in rng]
        first += [copy(a, 1 + j, me, (*chip, c), src=ins[a]) for j, chip in enumerate(chips) for a in rng]
        landed = [copy(a, 1 + j, (*chip, c), me) for j, chip in enumerate(chips) for a in rng]
        passed = [copy(a, 4 + j, (*chip, c), sibling) for j, chip in enumerate(chips) for a in rng]
        from_sibling = [copy(a, 0, sibling, me) for a in rng]
        from_sibling += [copy(a, 4 + j, (*chip, 1 - c), me) for j, chip in enumerate(chips) for a in rng]
        return mine, first, landed, passed, from_sibling

    def start(self, ins, outs, sems):
        mine, first, _, _, _ = self._parts(ins, outs, sems)
        for cp in mine + first:
            cp.start()

    def forward(self, ins, outs, sems):
        _, _, landed, passed, _ = self._parts(ins, outs, sems)
        for got, fwd in zip(landed, passed):
            got.wait_recv()
            fwd.start()

    def finish(self, ins, outs, sems):
        mine, first, _, passed, from_sibling = self._parts(ins, outs, sems)
        for cp in from_sibling:
            cp.wait_recv()
        for cp in first + passed:
            cp.wait_send()
        for cp in mine:
            cp.wait()

    def before(self, ins, outs, sems, step, nsteps):
        pl.when(step == 0)(lambda: self.start(ins, outs, sems))
        pl.when(step == (7 * nsteps) // 8)(lambda: self.forward(ins, outs, sems))

    def after(self, ins, outs, sems, step, nsteps):
        pl.when(step == nsteps - 1)(lambda: self.finish(ins, outs, sems))


class _Exchange:
    def __init__(self, arrays):
        self.arrays = list(arrays)
        self.n = len(self.arrays)

    def out_shape(self):
        return [jax.ShapeDtypeStruct(b.shape, b.dtype) for b in self.arrays]

    def scratch(self):
        return [pltpu.SemaphoreType.DMA((self.n, 7)), pltpu.SemaphoreType.DMA((self.n, 7)),
                pltpu.SemaphoreType.DMA((self.n,))]

    def _parts(self, ins, outs, sems):
        send_sems, recv_sems, local_sems = sems
        x, y, c = _place()
        me = _dev_index(x, y, c)
        peers = [(x ^ (k >> 2), y ^ ((k >> 1) & 1), c ^ (k & 1)) for k in range(1, N_DEV)]
        rng = range(self.n)
        mine = [pltpu.make_async_copy(ins[a].at[me], outs[a].at[me], local_sems.at[a]) for a in rng]
        sent = [pltpu.make_async_remote_copy(
            src_ref=ins[a].at[_dev_index(*to)], dst_ref=outs[a].at[me], send_sem=send_sems.at[a, k],
            recv_sem=recv_sems.at[a, k], device_id=to, device_id_type=MESH) for k, to in enumerate(peers) for a in rng]
        arrivals = [pltpu.make_async_remote_copy(
            src_ref=ins[a].at[me], dst_ref=outs[a].at[_dev_index(*frm)], send_sem=send_sems.at[a, k],
            recv_sem=recv_sems.at[a, k], device_id=frm, device_id_type=MESH) for k, frm in enumerate(peers) for a in rng]
        return mine, sent, arrivals

    def start(self, ins, outs, sems):
        mine, sent, _ = self._parts(ins, outs, sems)
        for cp in mine + sent:
            cp.start()

    def finish(self, ins, outs, sems):
        mine, sent, arrivals = self._parts(ins, outs, sems)
        for cp in arrivals:
            cp.wait_recv()
        for cp in sent:
            cp.wait_send()
        for cp in mine:
            cp.wait()

    def before(self, ins, outs, sems, step, nsteps):
        pl.when(step == 0)(lambda: self.start(ins, outs, sems))

    def after(self, ins, outs, sems, step, nsteps):
        pl.when(step == nsteps - 1)(lambda: self.finish(ins, outs, sems))


class _Bcast(_Exchange):
    def out_shape(self):
        return [jax.ShapeDtypeStruct((N_DEV,) + s.shape, s.dtype) for s in self.arrays]

    def _parts(self, ins, outs, sems):
        send_sems, recv_sems, local_sems = sems
        x, y, c = _place()
        me = _dev_index(x, y, c)
        peers = [(x ^ (k >> 2), y ^ ((k >> 1) & 1), c ^ (k & 1)) for k in range(1, N_DEV)]
        rng = range(self.n)
        mine = [pltpu.make_async_copy(ins[a], outs[a].at[me], local_sems.at[a]) for a in rng]
        sent = [pltpu.make_async_remote_copy(
            src_ref=ins[a], dst_ref=outs[a].at[me], send_sem=send_sems.at[a, k], recv_sem=recv_sems.at[a, k],
            device_id=to, device_id_type=MESH) for k, to in enumerate(peers) for a in rng]
        arrivals = [pltpu.make_async_remote_copy(
            src_ref=ins[a], dst_ref=outs[a].at[_dev_index(*frm)], send_sem=send_sems.at[a, k],
            recv_sem=recv_sems.at[a, k], device_id=frm, device_id_type=MESH) for k, frm in enumerate(peers) for a in rng]
        return mine, sent, arrivals


def _comm_call(comms, name):
    ns = [c.n for c in comms]
    n = sum(ns)

    def body(*refs):
        parts, a, s = [], 0, 2 * n
        for c in comms:
            parts.append((c, refs[a:a + c.n], refs[n + a:n + a + c.n], refs[s:s + 3]))
            a, s = a + c.n, s + 3
        for c, ins, outs, sems in parts:
            c.start(ins, outs, sems)
        for c, ins, outs, sems in parts:
            if isinstance(c, _Gather):
                c.forward(ins, outs, sems)
        for c, ins, outs, sems in parts:
            c.finish(ins, outs, sems)

    res = pl.pallas_call(
        body, name=name, in_specs=[_ANY] * n, out_specs=[_ANY] * n,
        out_shape=[s for c in comms for s in c.out_shape()], scratch_shapes=[s for c in comms for s in c.scratch()],
    )(*[arr for c in comms for arr in c.arrays])
    out, a = [], 0
    for k in ns:
        out.append(res[a:a + k])
        a += k
    return out


def _pcall(body, args, *, name, grid, in_specs, out_specs, out_shape, scratch_shapes=(), sem="parallel", comm=None,
           step_axis=0):
    sem = (sem,) * len(grid) if isinstance(sem, str) else sem
    if comm is None:
        res = pl.pallas_call(body, name=name, grid=grid, in_specs=in_specs, out_specs=out_specs, out_shape=out_shape,
                             scratch_shapes=list(scratch_shapes), compiler_params=_params(*sem))(*args)
        return res, []
    n_in, n_out, n_scr, n = len(in_specs), len(out_specs), len(scratch_shapes), comm.n
    nsteps = grid[step_axis]
    assert all(g == 1 for ax, g in enumerate(grid) if ax != step_axis)

    def hosted(*refs):
        ins, cin = refs[:n_in], refs[n_in:n_in + n]
        o0 = n_in + n
        outs, cout = refs[o0:o0 + n_out], refs[o0 + n_out:o0 + n_out + n]
        s0 = o0 + n_out + n
        scr, sems = refs[s0:s0 + n_scr], refs[s0 + n_scr:]
        step = pl.program_id(step_axis)
        comm.before(cin, cout, sems, step, nsteps)
        body(*ins, *outs, *scr)
        comm.after(cin, cout, sems, step, nsteps)

    res = pl.pallas_call(
        hosted, name=name, grid=grid, in_specs=list(in_specs) + [_ANY] * n, out_specs=list(out_specs) + [_ANY] * n,
        out_shape=list(out_shape) + comm.out_shape(), scratch_shapes=list(scratch_shapes) + comm.scratch(),
        compiler_params=_params(*(("arbitrary",) * len(grid))))(*args, *comm.arrays)
    return res[:n_out], res[n_out:]


def _in_proj(x, w_in_t, comm=None):
    S = x.shape[0]
    tb = min(512, S)

    def body(x_ref, w_ref, q_ref, k_ref, v_ref, xr_ref, gr_ref):
        u = _dot_nt(x_ref[...].astype(BF16), w_ref[...])
        q_ref[...] = (u[:, :D_ATT] * (HEAD_DIM ** -0.5)).astype(BF16)
        k_ref[...] = u[:, D_ATT:D_ATT + D_KV].astype(BF16)
        v_ref[...] = u[:, D_ATT + D_KV:D_ATT + 2 * D_KV].astype(BF16)
        xr_ref[...] = u[:, D_ATT + 2 * D_KV:D_ATT + 2 * D_KV + D_RNN]
        gr_ref[...] = u[:, D_ATT + 2 * D_KV + D_RNN:]

    return _pcall(
        body, (x, w_in_t), name="in_proj", grid=(S // tb,), comm=comm,
        in_specs=[_rows(tb, D_MODEL), _resident((D_IN, D_MODEL))],
        out_specs=[_rows(tb, D_ATT), _rows(tb, D_KV), _rows(tb, D_KV), _rows(tb, D_RNN), _rows(tb, D_RNN)],
        out_shape=[jax.ShapeDtypeStruct((S, D_ATT), BF16), jax.ShapeDtypeStruct((S, D_KV), BF16),
                   jax.ShapeDtypeStruct((S, D_KV), BF16), jax.ShapeDtypeStruct((S, D_RNN), F32),
                   jax.ShapeDtypeStruct((S, D_RNN), F32)])


GROUP = N_HEADS // N_KV


def _band_mask(i):
    qi = lax.broadcasted_iota(jnp.int32, (GROUP * QBLK, 2 * QBLK), 0) & (QBLK - 1)
    sj = lax.broadcasted_iota(jnp.int32, (GROUP * QBLK, 2 * QBLK), 1)
    return (sj > qi) & (sj <= qi + QBLK) & ((sj >= QBLK) | (i > 0))


def _stack_heads(x, g):
    return jnp.concatenate([x[:, (g * GROUP + hh) * HEAD_DIM:(g * GROUP + hh + 1) * HEAD_DIM] for hh in range(GROUP)],
                           axis=0)


def _unstack_heads(x4):
    return [x4[hh * QBLK:(hh + 1) * QBLK] for hh in range(GROUP)]


def _sink_column(sink_ref, g):
    head = lax.broadcasted_iota(jnp.int32, (GROUP * QBLK, 1), 0) // QBLK
    col = jnp.full((GROUP * QBLK, 1), sink_ref[g * GROUP], F32)
    for hh in range(1, GROUP):
        col = jnp.where(head == hh, sink_ref[g * GROUP + hh], col)
    return col


def _attn_specs():
    cur = lambda i: (i, 0)
    prev = lambda i: (jnp.maximum(i - 1, 0), 0)
    return [pl.BlockSpec((QBLK, D_KV), cur), pl.BlockSpec((QBLK, D_KV), prev),
            pl.BlockSpec((QBLK, D_KV), cur), pl.BlockSpec((QBLK, D_KV), prev)]


def _attn_fwd(q, k, v, sinks, comm=None):
    S = q.shape[0]

    def body(sink_ref, q_ref, kc_ref, kp_ref, vc_ref, vp_ref, o_ref):
        valid = _band_mask(pl.program_id(0))
        outs = []
        qv = q_ref[...]
        kall = jnp.concatenate([kp_ref[...], kc_ref[...]], axis=0)
        vall = jnp.concatenate([vp_ref[...], vc_ref[...]], axis=0)
        for g in range(N_KV):
            kcat = kall[:, g * HEAD_DIM:(g + 1) * HEAD_DIM]
            vcat = vall[:, g * HEAD_DIM:(g + 1) * HEAD_DIM]
            s = jnp.where(valid, _dot_nt(_stack_heads(qv, g), kcat), -1e30)
            sink = _sink_column(sink_ref, g)
            m = jnp.maximum(jnp.max(s, axis=1, keepdims=True), sink)
            p = jnp.exp(s - m)
            l = jnp.sum(p, axis=1, keepdims=True) + jnp.exp(sink - m)
            outs += _unstack_heads(_dot(p.astype(BF16), vcat) / l)
        o_ref[...] = jnp.concatenate(outs, axis=1).astype(BF16)

    return _pcall(
        body, (sinks, q, k, k, v, v), name="attn_fwd", grid=(S // QBLK,), comm=comm,
        in_specs=[pl.BlockSpec(memory_space=pltpu.SMEM), _rows(QBLK, D_ATT)] + _attn_specs(),
        out_specs=[_rows(QBLK, D_ATT)], out_shape=[jax.ShapeDtypeStruct((S, D_ATT), BF16)])


def _w_rows(w_ref):
    return [w_ref[k:k + 1, :] for k in range(w_ref.shape[0])]


def _conv4(x, halo, w, b):
    y = b + w[3] * x
    for s in (1, 2, 3):
        y = y + w[3 - s] * _shift_down(x, halo, s)
    return y


def _rnn_gates(xc, wa, wx, ba, bx, sp):
    xcb = xc.astype(BF16)
    r = _sigmoid(_dot(xcb, wa) + ba)
    ig = _sigmoid(_dot(xcb, wx) + bx)
    la = -LRU_C * r * sp
    a = jnp.exp(la)
    t = jnp.tanh(la)
    f = jnp.sqrt(-2.0 * t / (1.0 - t))
    return r, ig, a, f


def _rnn_fwd(xr, gr, conv_w, conv_b, wa, wx, ba, bx, lam, comm=None):
    S = xr.shape[0]
    tb = min(256, S)

    def body(xr_ref, gr_ref, cw_ref, cb_ref, wa_ref, wx_ref, ba_ref, bx_ref, lam_ref, rec_ref, h_ref,
             halo_s, hc_s, a_s, b_s):
        @pl.when(pl.program_id(0) == 0)
        def _():
            halo_s[...] = jnp.zeros_like(halo_s)
            hc_s[...] = jnp.zeros_like(hc_s)

        x = xr_ref[...]
        xc = _conv4(x, halo_s[...], _w_rows(cw_ref), cb_ref[...])
        halo_s[...] = x[tb - 8:]
        _, ig, a, f = _rnn_gates(xc, wa_ref[...], wx_ref[...], ba_ref[...], bx_ref[...], _softplus_neg(lam_ref[...]))
        a_s[...] = a
        b_s[...] = f * ig * xc
        row8 = lax.broadcasted_iota(jnp.int32, (8, D_RNN), 0)

        def tile(t, hc):
            o = pl.multiple_of(t * 8, 8)
            at = a_s[pl.ds(o, 8), :]
            bt = b_s[pl.ds(o, 8), :]
            for s in (1, 2, 4):
                keep = row8 >= s
                a_sh = jnp.where(keep, pltpu.roll(at, s, 0), 1.0)
                b_sh = jnp.where(keep, pltpu.roll(bt, s, 0), 0.0)
                bt = at * b_sh + bt
                at = at * a_sh
            ht = at * hc + bt
            b_s[pl.ds(o, 8), :] = ht
            return _row_sum(jnp.where(row8 == 7, ht, 0.0))

        hc_s[0:1, :] = lax.fori_loop(0, tb // 8, tile, hc_s[0:1, :])
        h = b_s[...]
        h_ref[...] = h
        rec_ref[...] = (h * _gelu(gr_ref[...])).astype(BF16)

    vec = _resident((1, D_RNN))
    return _pcall(
        body, (xr, gr, conv_w, conv_b, wa, wx, ba, bx, lam), name="rnn_fwd", grid=(S // tb,), sem="arbitrary", comm=comm,
        in_specs=[_rows(tb, D_RNN), _rows(tb, D_RNN), _resident((4, D_RNN)), vec,
                  _resident((D_RNN, D_RNN)), _resident((D_RNN, D_RNN)), vec, vec, vec],
        out_specs=[_rows(tb, D_RNN), _rows(tb, D_RNN)],
        out_shape=[jax.ShapeDtypeStruct((S, D_RNN), BF16), jax.ShapeDtypeStruct((S, D_RNN), F32)],
        scratch_shapes=[pltpu.VMEM((8, D_RNN), F32), pltpu.VMEM((8, D_RNN), F32),
                        pltpu.VMEM((tb, D_RNN), F32), pltpu.VMEM((tb, D_RNN), F32)])


def _mix_ln1_up(x, att, rec, w_out, ln1_g, ln1_b, w_up, fcw, fcb, comm=None):
    S = x.shape[0]
    tb = min(256, S)
    nblk, _, wblk = w_up.shape
    half = nblk // 2

    def body(x_ref, att_ref, rec_ref, wo_ref, g_ref, b_ref, wu_ref, fcw_ref, fcb_ref,
             z1_ref, h1_ref, h1b_ref, gate_ref, act_ref, gl_ref, vdgl_ref, halo_s):
        @pl.when(pl.program_id(0) == 0)
        def _():
            halo_s[...] = jnp.zeros_like(halo_s)

        z1 = ALPHA * x_ref[...] + _dot(att_ref[...], wo_ref[:D_ATT, :]) + _dot(rec_ref[...], wo_ref[D_ATT:, :])
        z1_ref[...] = z1
        xhat, _ = _ln_stats(z1)
        h1 = xhat * g_ref[...] + b_ref[...]
        h1_ref[...] = h1
        h1b = h1.astype(BF16)
        h1b_ref[...] = h1b
        for jj in range(half):
            cols = slice(jj * wblk, (jj + 1) * wblk)
            gate = _dot(h1b, wu_ref[jj])
            val = _dot(h1b, wu_ref[jj + half])
            halo = halo_s[:, cols]
            conv = (fcb_ref[:, cols] + fcw_ref[2:3, cols] * gate + fcw_ref[1:2, cols] * _shift_down(gate, halo, 1)
                    + fcw_ref[0:1, cols] * _shift_down(gate, halo, 2))
            halo_s[:, cols] = gate[tb - 8:]
            gl, dgl = _gelu_and_grad(conv)
            gate_ref[:, cols] = gate.astype(BF16)
            act_ref[:, cols] = (gl * val).astype(BF16)
            gl_ref[:, cols] = gl.astype(BF16)
            vdgl_ref[:, cols] = (val * dgl).astype(BF16)

    vec = _resident((1, D_MODEL))
    wide = jax.ShapeDtypeStruct((S, D_FF), BF16)
    return _pcall(
        body, (x, att, rec, w_out, ln1_g, ln1_b, w_up, fcw, fcb), name="mix_ln1_up", grid=(S // tb,),
        sem="arbitrary", comm=comm,
        in_specs=[_rows(tb, D_MODEL), _rows(tb, D_ATT), _rows(tb, D_RNN), _resident((D_MODEL, D_MODEL)), vec, vec,
                  _resident(w_up.shape), _resident((3, D_FF)), _resident((1, D_FF))],
        out_specs=[_rows(tb, D_MODEL), _rows(tb, D_MODEL), _rows(tb, D_MODEL)] + [_rows(tb, D_FF)] * 4,
        out_shape=[jax.ShapeDtypeStruct((S, D_MODEL), F32), jax.ShapeDtypeStruct((S, D_MODEL), F32),
                   jax.ShapeDtypeStruct((S, D_MODEL), BF16), wide, wide, wide, wide],
        scratch_shapes=[pltpu.VMEM((8, D_FF), F32)])


def _tail(act, gl, vdgl, h1, h1b, p, tgt, w_down, w_pg, b_pg, w_pp, ln2_g, ln2_b):
    S = h1.shape[0]
    tb = min(256, S)

    def body(act_ref, gl_ref, vdgl_ref, h1_ref, h1b_ref, p_ref, t_ref, wd_ref, wpg_ref, bpg_ref, wpp_ref, g2_ref, b2_ref,
             dz2_ref, dpre_ref, dpp_ref, dgc_ref, dval_ref, dh1_ref, acc_ref):
        i = pl.program_id(0)

        @pl.when(i == 0)
        def _():
            acc_ref[...] = jnp.zeros_like(acc_ref)

        ffn = _dot(act_ref[...], wd_ref[...])
        h1 = h1_ref[...]
        sg = _sigmoid(_dot(h1b_ref[...], wpg_ref[...]) + bpg_ref[...])
        pp = _dot(p_ref[...].astype(BF16), wpp_ref[...])
        z2 = ALPHA * h1 + ffn + sg * pp
        xhat2, rstd2 = _ln_stats(z2)
        y = xhat2 * g2_ref[...] + b2_ref[...]
        err = y - t_ref[...]
        dy = err * (1.0 / D_MODEL)
        loss = 0.5 * jnp.sum(jnp.sum(err * err, axis=1, keepdims=True), axis=0, keepdims=True) * (1.0 / D_MODEL)
        dz2 = _ln_bwd(dy, xhat2, rstd2, g2_ref[...])
        dz2b = dz2.astype(BF16)
        dz2_ref[...] = dz2b
        dpre = dz2 * pp * sg * (1.0 - sg)
        dpreb = dpre.astype(BF16)
        dpre_ref[...] = dpreb
        dpp_ref[...] = (dz2 * sg).astype(BF16)
        dh1_ref[...] = ALPHA * dz2 + _dot_nt(dpreb, wpg_ref[...])
        dactb = _dot_nt(dz2b, wd_ref[...]).astype(BF16)
        dval_ref[...] = dactb * gl_ref[...]
        dgc_ref[...] = dactb * vdgl_ref[...]
        _put_rows(acc_ref, [_row_sum(dy * xhat2), _row_sum(dy), _row_sum(dpre),
                            jnp.broadcast_to(loss, (1, D_MODEL))])

    vec = _resident((1, D_MODEL))
    return pl.pallas_call(
        body, name="tail", grid=(S // tb,),
        in_specs=[_rows(tb, D_FF), _rows(tb, D_FF), _rows(tb, D_FF), _rows(tb, D_MODEL), _rows(tb, D_MODEL),
                  _rows(tb, PLE_DIM), _rows(tb, D_MODEL), _resident((D_FF, D_MODEL)), _resident((D_MODEL, D_MODEL)), vec,
                  _resident((PLE_DIM, D_MODEL)), vec, vec],
        out_specs=[_rows(tb, D_MODEL), _rows(tb, D_MODEL), _rows(tb, D_MODEL), _rows(tb, D_FF),
                   _rows(tb, D_FF), _rows(tb, D_MODEL), _acc((8, D_MODEL))],
        out_shape=[jax.ShapeDtypeStruct((S, D_MODEL), BF16),
                   jax.ShapeDtypeStruct((S, D_MODEL), BF16), jax.ShapeDtypeStruct((S, D_MODEL), BF16),
                   jax.ShapeDtypeStruct((S, D_FF), BF16), jax.ShapeDtypeStruct((S, D_FF), BF16),
                   jax.ShapeDtypeStruct((S, D_MODEL), F32), jax.ShapeDtypeStruct((8, D_MODEL), F32)],
        compiler_params=_params("arbitrary"),
    )(act, gl, vdgl, h1, h1b, p, tgt, w_down, w_pg, b_pg, w_pp, ln2_g, ln2_b)


def _weight_grad(a_list, b_list, name, layout, ts=512, comm=None, b_window=None):
    S = a_list[0].shape[0]
    ms = [a.shape[1] for a in a_list]
    M, nb = sum(ms), len(b_list)
    win, Nb = b_window if b_window else (0, b_list[0].shape[1])
    ts = min(ts, S)
    nk = S // ts
    per_b = N_DEV // nb
    na = len(a_list)

    def body(*refs):
        a_refs, b_refs, o_ref, acc_ref = refs[:na], refs[na:na + nb], refs[na + nb], refs[na + nb + 1]
        j, k = pl.program_id(0), pl.program_id(1)

        @pl.when(k == 0)
        def _():
            acc_ref[...] = jnp.zeros_like(acc_ref)

        for jj in range(nb):
            @pl.when(j == jj)
            def _():
                b = b_refs[jj][...].astype(BF16)
                off = 0
                for a_ref, m in zip(a_refs, ms):
                    acc_ref[off:off + m, :] += _dot_tn(a_ref[...].astype(BF16), b)
                    off += m

        @pl.when(k == nk - 1)
        def _():
            for d in range(per_b):
                if layout == "rows":
                    o_ref[d] = acc_ref[d * (M // N_DEV):(d + 1) * (M // N_DEV), :].astype(BF16)
                elif layout == "cols":
                    o_ref[d] = acc_ref[:, d * (Nb // per_b):(d + 1) * (Nb // per_b)].astype(BF16)
                else:
                    o_ref[d] = acc_ref[:, d * (Nb // per_b):(d + 1) * (Nb // per_b)].T.astype(BF16)

    def b_index(jj):
        return lambda j, k: (jnp.where(j == jj, k, jnp.where(j < jj, 0, nk - 1)), win)

    if layout == "rows":
        assert nb == 1
        blk = (N_DEV, M // N_DEV, Nb)
    elif layout == "cols":
        blk = (per_b, M, Nb // per_b)
    else:
        blk = (per_b, Nb // per_b, M)
    (res,), comm_res = _pcall(
        body, (*a_list, *b_list), name=name, grid=(nb, nk), sem="arbitrary", comm=comm, step_axis=1,
        in_specs=[pl.BlockSpec((ts, m), lambda j, k: (k, 0)) for m in ms]
        + [pl.BlockSpec((ts, Nb), b_index(jj)) for jj in range(nb)],
        out_specs=[pl.BlockSpec(blk, lambda j, k: (j, 0, 0))],
        out_shape=[jax.ShapeDtypeStruct((N_DEV,) + blk[1:], BF16)],
        scratch_shapes=[pltpu.VMEM((M, Nb), F32)])
    return (res, comm_res) if comm is not None else res


def _up_bwd(dgc, gate, dval, dh1p, z1, w_up, fcw, w_out, ln1_g, comm=None):
    S = z1.shape[0]
    tb = min(256, S)
    t16 = tb // 16
    n16 = S // 16
    nblk, _, wblk = w_up.shape
    half = nblk // 2
    nsteps = S // tb

    def body(dgc_ref, dgn_ref, gc_ref, gp_ref, dval_ref, dh1p_ref, z1_ref, wu_ref, fcw_ref, wo_ref, g1_ref,
             dgate_ref, dz1_ref, dz1b_ref, datt_ref, drec_ref, accf_ref, accd_ref):
        i = pl.program_id(0)

        @pl.when(i == 0)
        def _():
            accf_ref[...] = jnp.zeros_like(accf_ref)
            accd_ref[...] = jnp.zeros_like(accd_ref)

        dg = dgc_ref[...].astype(F32)
        nxt = jnp.where(i < nsteps - 1, dgn_ref[...].astype(F32)[0:8], 0.0)
        w = _w_rows(fcw_ref)
        dgate = (w[2] * dg + w[1] * _shift_up(dg, nxt, 1) + w[0] * _shift_up(dg, nxt, 2)).astype(BF16)
        dgate_ref[...] = dgate
        gate = gc_ref[...].astype(F32)
        halo = jnp.where(i > 0, gp_ref[...].astype(F32)[8:16], 0.0)
        _put_rows(accf_ref, [_row_sum(dg * _shift_down(gate, halo, 2)), _row_sum(dg * _shift_down(gate, halo, 1)),
                             _row_sum(dg * gate), _row_sum(dg)])

        dh1 = dh1p_ref[...]
        for j in range(nblk):
            src = dgate if j < half else dval_ref[...]
            jj = j % half
            dh1 = dh1 + _dot_nt(src[:, jj * wblk:(jj + 1) * wblk], wu_ref[j])
        xhat1, rstd1 = _ln_stats(z1_ref[...])
        dz1 = _ln_bwd(dh1, xhat1, rstd1, g1_ref[...])
        dz1_ref[...] = dz1
        dz1b = dz1.astype(BF16)
        dz1b_ref[...] = dz1b
        dcat = _dot_nt(dz1b, wo_ref[...])
        datt_ref[...] = dcat[:, :D_ATT].astype(BF16)
        drec_ref[...] = dcat[:, D_ATT:]
        _put_rows(accd_ref, [_row_sum(dh1 * xhat1), _row_sum(dh1)])

    prev16 = pl.BlockSpec((16, D_FF), lambda i: (jnp.maximum(i * t16 - 1, 0), 0))
    next16 = pl.BlockSpec((16, D_FF), lambda i: (jnp.minimum((i + 1) * t16, n16 - 1), 0))
    return _pcall(
        body, (dgc, dgc, gate, gate, dval, dh1p, z1, w_up, fcw, w_out, ln1_g), name="up_bwd",
        grid=(nsteps,), sem="arbitrary", comm=comm,
        in_specs=[_rows(tb, D_FF), next16, _rows(tb, D_FF), prev16, _rows(tb, D_FF), _rows(tb, D_MODEL),
                  _rows(tb, D_MODEL), _resident(w_up.shape), _resident((3, D_FF)),
                  _resident((D_MODEL, D_MODEL)), _resident((1, D_MODEL))],
        out_specs=[_rows(tb, D_FF), _rows(tb, D_MODEL), _rows(tb, D_MODEL), _rows(tb, D_ATT), _rows(tb, D_RNN),
                   _acc((8, D_FF)), _acc((8, D_MODEL))],
        out_shape=[jax.ShapeDtypeStruct((S, D_FF), BF16), jax.ShapeDtypeStruct((S, D_MODEL), F32),
                   jax.ShapeDtypeStruct((S, D_MODEL), BF16), jax.ShapeDtypeStruct((S, D_ATT), BF16),
                   jax.ShapeDtypeStruct((S, D_RNN), F32), jax.ShapeDtypeStruct((8, D_FF), F32),
                   jax.ShapeDtypeStruct((8, D_MODEL), F32)])


def _attn_bwd(q, k, v, do, sinks, comm=None):
    S = q.shape[0]
    grp = N_HEADS // N_KV

    def body(sink_ref, q_ref, kc_ref, kp_ref, vc_ref, vp_ref, do_ref, dq_ref, dkc_ref, dkp_ref, dvc_ref, dvp_ref,
             ds_ref):
        i = pl.program_id(0)

        @pl.when(i == 0)
        def _():
            ds_ref[...] = jnp.zeros_like(ds_ref)

        valid = _band_mask(i)
        row8 = lax.broadcasted_iota(jnp.int32, (8, 128), 0)
        lane8 = lax.broadcasted_iota(jnp.int32, (8, 128), 1)
        dqs, dks, dvs = [], [], []
        dsink = jnp.zeros((8, 128), F32)
        qv = q_ref[...]
        dov = do_ref[...]
        kall = jnp.concatenate([kp_ref[...], kc_ref[...]], axis=0)
        vall = jnp.concatenate([vp_ref[...], vc_ref[...]], axis=0)
        for g in range(N_KV):
            kcat = kall[:, g * HEAD_DIM:(g + 1) * HEAD_DIM]
            vcat = vall[:, g * HEAD_DIM:(g + 1) * HEAD_DIM]
            q4, do4 = _stack_heads(qv, g), _stack_heads(dov, g)
            s = jnp.where(valid, _dot_nt(q4, kcat), -1e30)
            sink = _sink_column(sink_ref, g)
            m = jnp.maximum(jnp.max(s, axis=1, keepdims=True), sink)
            e = jnp.exp(s - m)
            es = jnp.exp(sink - m)
            inv = 1.0 / (jnp.sum(e, axis=1, keepdims=True) + es)
            p = e * inv
            dp = _dot_nt(do4, vcat)
            delta = jnp.sum(p * dp, axis=1, keepdims=True)
            dsc = (p * (dp - delta)).astype(BF16)
            dqs += _unstack_heads(_dot(dsc, kcat) * (HEAD_DIM ** -0.5))
            dks.append(_dot_tn(q4, dsc))
            dvs.append(_dot_tn(do4, p.astype(BF16)))
            for hh, part in enumerate(_unstack_heads(-es * inv * delta)):
                here = (row8 == 0) & (lane8 == g * grp + hh)
                dsink = dsink + jnp.where(here, jnp.sum(part, axis=0, keepdims=True), 0.0)
        dq_ref[...] = jnp.concatenate(dqs, axis=1).astype(BF16)
        dk = jnp.concatenate(dks, axis=0).T
        dv = jnp.concatenate(dvs, axis=0).T
        dkp_ref[...] = dk[:QBLK]
        dkc_ref[...] = dk[QBLK:]
        dvp_ref[...] = dv[:QBLK]
        dvc_ref[...] = dv[QBLK:]
        ds_ref[...] += dsink

    kvs = jax.ShapeDtypeStruct((S, D_KV), F32)
    return _pcall(
        body, (sinks, q, k, k, v, v, do), name="attn_bwd", grid=(S // QBLK,), sem="arbitrary", comm=comm,
        in_specs=[pl.BlockSpec(memory_space=pltpu.SMEM), _rows(QBLK, D_ATT)] + _attn_specs() + [_rows(QBLK, D_ATT)],
        out_specs=[_rows(QBLK, D_ATT), _rows(QBLK, D_KV), _rows(QBLK, D_KV), _rows(QBLK, D_KV), _rows(QBLK, D_KV),
                   _acc((8, 128))],
        out_shape=[jax.ShapeDtypeStruct((S, D_ATT), BF16), kvs, kvs, kvs, kvs, jax.ShapeDtypeStruct((8, 128), F32)])


def _rnn_bwd(xr, gr, h, drec, conv_w, conv_b, wa, wx, ba, bx, lam, comm=None):
    S = xr.shape[0]
    tb = min(256, S)
    t8 = tb // 8
    nsteps = S // tb

    def body(xr_ref, xp_ref, gr_ref, h_ref, hp_ref, drec_ref, cw_ref, cb_ref, wa_ref, wx_ref, ba_ref, bx_ref, lam_ref,
             dxr_ref, dgr_ref, gwa_ref, gwx_ref, acc_ref, carry_s, dxc_halo_s, a_s, d_s, gwa_s, gwx_s):
        i = pl.program_id(0)
        blk = nsteps - 1 - i

        @pl.when(i == 0)
        def _():
            gwa_s[...] = jnp.zeros_like(gwa_s)
            gwx_s[...] = jnp.zeros_like(gwx_s)
            acc_ref[...] = jnp.zeros_like(acc_ref)
            carry_s[...] = jnp.zeros_like(carry_s)
            dxc_halo_s[...] = jnp.zeros_like(dxc_halo_s)

        x = xr_ref[...]
        xhalo = jnp.where(blk > 0, xp_ref[...], 0.0)
        cw = _w_rows(cw_ref)
        xs = [_shift_down(x, xhalo, 3), _shift_down(x, xhalo, 2), _shift_down(x, xhalo, 1), x]
        xc = cb_ref[...] + cw[0] * xs[0] + cw[1] * xs[1] + cw[2] * xs[2] + cw[3] * xs[3]
        sp = _softplus_neg(lam_ref[...])
        r, ig, a, f = _rnn_gates(xc, wa_ref[...], wx_ref[...], ba_ref[...], bx_ref[...], sp)
        hcur = h_ref[...]
        hprev = _shift_down(hcur, jnp.where(blk > 0, hp_ref[...], 0.0), 1)
        gl, dgl = _gelu_and_grad(gr_ref[...])
        drec = drec_ref[...]
        dgr_ref[...] = (drec * hcur * dgl).astype(BF16)
        a_s[...] = a
        d_s[...] = drec * gl
        row8 = lax.broadcasted_iota(jnp.int32, (8, D_RNN), 0)

        def tile(t, c):
            o = pl.multiple_of((t8 - 1 - t) * 8, 8)
            a8 = a_s[pl.ds(o, 8), :]
            dt = d_s[pl.ds(o, 8), :]
            at = jnp.where(row8 == 7, 1.0, pltpu.roll(a8, 7, 0))
            for s in (1, 2, 4):
                keep = row8 < 8 - s
                a_sh = jnp.where(keep, pltpu.roll(at, 8 - s, 0), 1.0)
                d_sh = jnp.where(keep, pltpu.roll(dt, 8 - s, 0), 0.0)
                dt = at * d_sh + dt
                at = at * a_sh
            lt = at * c + dt
            d_s[pl.ds(o, 8), :] = lt
            return _row_sum(jnp.where(row8 == 0, a8 * lt, 0.0))

        carry_s[0:1, :] = lax.fori_loop(0, t8, tile, carry_s[0:1, :])
        lmb = d_s[...]
        a2 = a * a
        dla = lmb * hprev * a - lmb * ig * xc * (a2 / f)
        di = lmb * f * xc
        dr = dla * (-LRU_C) * sp
        dpa = dr * r * (1.0 - r)
        dpx = di * ig * (1.0 - ig)
        dpab = dpa.astype(BF16)
        dpxb = dpx.astype(BF16)
        xcb = xc.astype(BF16)
        gwa_s[...] += _dot_tn(xcb, dpab)
        gwx_s[...] += _dot_tn(xcb, dpxb)

        @pl.when(i == nsteps - 1)
        def _():
            for dense, out in ((gwa_s[...], gwa_ref), (gwx_s[...], gwx_ref)):
                for b in range(RNN_BLOCKS):
                    rows = slice(b * HEAD_DIM, (b + 1) * HEAD_DIM)
                    out[rows, :] = dense[rows, b * HEAD_DIM:(b + 1) * HEAD_DIM]

        dxc = lmb * f * ig + _dot_nt(dpab, wa_ref[...]) + _dot_nt(dpxb, wx_ref[...])
        nxt = dxc_halo_s[...]
        dxr = cw[3] * dxc
        for s in (1, 2, 3):
            dxr = dxr + cw[3 - s] * _shift_up(dxc, nxt, s)
        dxr_ref[...] = dxr.astype(BF16)
        dxc_halo_s[...] = dxc[:8]
        dlam = _row_sum(dla * (-LRU_C) * r) * (-1.0 / (1.0 + jnp.exp(lam_ref[...])))
        _put_rows(acc_ref, [_row_sum(dxc * xs[0]), _row_sum(dxc * xs[1]), _row_sum(dxc * xs[2]), _row_sum(dxc * xs[3]),
                            _row_sum(dxc), _row_sum(dpa), _row_sum(dpx), dlam])

    rev = lambda i: (nsteps - 1 - i, 0)
    prev8 = lambda i: (jnp.maximum((nsteps - 1 - i) * t8 - 1, 0), 0)
    blkspec = pl.BlockSpec((tb, D_RNN), rev)
    halo8 = pl.BlockSpec((8, D_RNN), prev8)
    vec = _resident((1, D_RNN))
    return _pcall(
        body, (xr, xr, gr, h, h, drec, conv_w, conv_b, wa, wx, ba, bx, lam), name="rnn_bwd", grid=(nsteps,),
        sem="arbitrary", comm=comm,
        in_specs=[blkspec, halo8, blkspec, blkspec, halo8, blkspec, _resident((4, D_RNN)), vec,
                  _resident((D_RNN, D_RNN)), _resident((D_RNN, D_RNN)), vec, vec, vec],
        out_specs=[blkspec, blkspec, _acc((D_RNN, HEAD_DIM)), _acc((D_RNN, HEAD_DIM)), _acc((8, D_RNN))],
        out_shape=[jax.ShapeDtypeStruct((S, D_RNN), BF16), jax.ShapeDtypeStruct((S, D_RNN), BF16),
                   jax.ShapeDtypeStruct((D_RNN, HEAD_DIM), F32), jax.ShapeDtypeStruct((D_RNN, HEAD_DIM), F32),
                   jax.ShapeDtypeStruct((8, D_RNN), F32)],
        scratch_shapes=[pltpu.VMEM((8, D_RNN), F32), pltpu.VMEM((8, D_RNN), F32),
                        pltpu.VMEM((tb, D_RNN), F32), pltpu.VMEM((tb, D_RNN), F32),
                        pltpu.VMEM((D_RNN, D_RNN), F32), pltpu.VMEM((D_RNN, D_RNN), F32)])


def _in_bwd(dq, dkc, dkp, dvc, dvp, dxr, dgr, dz1, w_in, comm=None):
    S = dz1.shape[0]
    tb = min(512, S)
    nsteps = S // tb
    nq = S // QBLK
    r = tb // QBLK

    def body(dq_ref, dkc_ref, dkp_ref, dkn_ref, dvc_ref, dvp_ref, dvn_ref, dxr_ref, dgr_ref, dz1_ref, w_ref,
             du_ref, dx_ref):
        i = pl.program_id(0)
        last = i == nsteps - 1

        def shifted(prev_ref, next_ref):
            nxt = jnp.where(last, 0.0, next_ref[...])
            return jnp.concatenate([prev_ref[QBLK:], nxt], axis=0) if r > 1 else nxt

        dk = (dkc_ref[...] + shifted(dkp_ref, dkn_ref)).astype(BF16)
        dv = (dvc_ref[...] + shifted(dvp_ref, dvn_ref)).astype(BF16)
        du = jnp.concatenate([dq_ref[...], dk, dv, dxr_ref[...], dgr_ref[...]], axis=1)
        du_ref[...] = du
        dx_ref[...] = ALPHA * dz1_ref[...] + _dot(du, w_ref[...])

    nextq = pl.BlockSpec((QBLK, D_KV), lambda i: (jnp.minimum((i + 1) * r, nq - 1), 0))
    return _pcall(
        body, (dq, dkc, dkp, dkp, dvc, dvp, dvp, dxr, dgr, dz1, w_in), name="in_bwd", grid=(nsteps,), comm=comm,
        in_specs=[_rows(tb, D_ATT), _rows(tb, D_KV), _rows(tb, D_KV), nextq, _rows(tb, D_KV), _rows(tb, D_KV), nextq,
                  _rows(tb, D_RNN), _rows(tb, D_RNN), _rows(tb, D_MODEL), _resident((D_IN, D_MODEL))],
        out_specs=[_rows(tb, D_IN), _rows(tb, D_MODEL)],
        out_shape=[jax.ShapeDtypeStruct((S, D_IN), BF16), jax.ShapeDtypeStruct((S, D_MODEL), F32)])


def _block_diag(w):
    eye = jnp.eye(RNN_BLOCKS, dtype=w.dtype)
    return (w[:, :, None, :] * eye[:, None, :, None]).reshape(D_RNN, D_RNN).astype(BF16)


def _adamw(w, g, m, v):
    m = ADAM_B1 * m + (1.0 - ADAM_B1) * g
    v = ADAM_B2 * v + (1.0 - ADAM_B2) * (g * g)
    m_hat = m / (1.0 - ADAM_B1 ** ADAM_STEP)
    v_hat = v / (1.0 - ADAM_B2 ** ADAM_STEP)
    delta = -ADAM_LR * (m_hat / (jnp.sqrt(v_hat) + ADAM_EPS) + ADAM_WD * w)
    return delta, m, v


def _sum_adamw(parts, w, m, v, name):
    R, C = w.shape
    rb = R if R <= 256 else 128
    assert R % rb == 0

    def body(p_ref, w_ref, m_ref, v_ref, g_out, d_out, m_out, v_out):
        g = p_ref[0].astype(F32)
        for d in range(1, N_DEV):
            g = g + p_ref[d].astype(F32)
        delta, mn, vn = _adamw(w_ref[...], g, m_ref[...], v_ref[...])
        g_out[...] = g
        d_out[...] = delta
        m_out[...] = mn
        v_out[...] = vn

    blk = _rows(rb, C)
    out = jax.ShapeDtypeStruct((R, C), F32)
    return pl.pallas_call(
        body, name=name, grid=(R // rb,),
        in_specs=[pl.BlockSpec((N_DEV, rb, C), lambda i: (0, i, 0)), blk, blk, blk],
        out_specs=[blk, blk, blk, blk], out_shape=[out, out, out, out],
        compiler_params=_params("parallel"),
    )(parts, w, m, v)


_SMALL = [("attn_sinks", "s", 0, 1, None), ("rnn_conv_w", "r", 0, 4, "cols"), ("rnn_conv_b", "r", 4, 1, None),
          ("gate_a_w", "a", 0, D_RNN, None), ("gate_a_b", "r", 5, 1, None), ("gate_x_w", "x", 0, D_RNN, None),
          ("gate_x_b", "r", 6, 1, None), ("lru_lambda", "r", 7, 1, None), ("ln1_g", "d", 0, 1, None),
          ("ln1_b", "d", 1, 1, None), ("ffn_conv_w", "f", 0, 3, "cols"), ("ffn_conv_b", "f", 3, 1, None),
          ("ple_gate_b", "t", 2, 1, None), ("ln2_g", "t", 0, 1, None), ("ln2_b", "t", 1, 1, None)]
_LOSS_ROW = 3


_ACC_COLS = {"t": (0, D_MODEL), "f": (D_MODEL, D_FF), "d": (D_MODEL + D_FF, D_MODEL), "s": (2 * D_MODEL + D_FF, 128),
             "r": (2 * D_MODEL + D_FF + 128, D_RNN)}
_ACC_WIDTH = 2 * D_MODEL + D_FF + 128 + D_RNN


def _small_update(rows_all, gates_all, params):
    flat = [arr for triple in params for arr in triple]
    n_par = len(_SMALL)

    def body(*refs):
        rows_ref, gates_ref = refs[:2]
        p_refs = refs[2:2 + 3 * n_par]
        loss_ref = refs[2 + 3 * n_par]
        o_refs = refs[3 + 3 * n_par:3 + 7 * n_par]
        rows_s, tmp_r, tmp_f = refs[3 + 7 * n_par:]
        me = _dev_index(*_place())
        rows_sum, gates_sum = rows_ref[0], gates_ref[0]
        for d in range(1, N_DEV):
            rows_sum = rows_sum + rows_ref[d]
            gates_sum = gates_sum + gates_ref[d]
        rows_s[...] = rows_sum
        t0 = _ACC_COLS["t"][0]
        loss_ref[...] = rows_s[_LOSS_ROW:_LOSS_ROW + 1, t0:t0 + 128]
        for i, (name, key, row, rows, how) in enumerate(_SMALL):
            w_ref, m_ref, v_ref = p_refs[3 * i:3 * i + 3]
            g_out, d_out, m_out, v_out = o_refs[4 * i:4 * i + 4]
            if key == "a":
                g = gates_sum[:, :HEAD_DIM]
            elif key == "x":
                g = gates_sum[:, HEAD_DIM:]
            elif how == "cols":
                c0, width = _ACC_COLS[key]
                full = rows_s[:, c0:c0 + width]
                shard = width // N_DEV
                mine = full[:, :shard]
                for d in range(1, N_DEV):
                    mine = jnp.where(me == d, full[:, d * shard:(d + 1) * shard], mine)
                tmp = tmp_r if key == "r" else tmp_f
                tmp[...] = mine
                g = tmp[row:row + rows, :]
            else:
                c0, width = _ACC_COLS[key]
                g = rows_s[row:row + rows, c0:c0 + width][:, :w_ref.shape[1]]
            delta, mn, vn = _adamw(w_ref[...], g, m_ref[...], v_ref[...])
            g_out[...] = g
            d_out[...] = delta
            m_out[...] = mn
            v_out[...] = vn

    outs = [jax.ShapeDtypeStruct((1, 128), F32)]
    for w, _, _ in params:
        outs += [jax.ShapeDtypeStruct(w.shape, F32)] * 4
    scratch = [pltpu.VMEM((8, _ACC_WIDTH), F32), pltpu.VMEM((8, D_RNN // N_DEV), F32), pltpu.VMEM((8, D_FF // N_DEV), F32)]
    res = pl.pallas_call(body, name="small_update", out_shape=outs, scratch_shapes=scratch)(rows_all, gates_all, *flat)
    return res[0], [res[1 + 4 * i:5 + 4 * i] for i in range(n_par)]


def kernel(x, p, w_in, attn_sinks, rnn_conv_w, rnn_conv_b, gate_a_w, gate_a_b, gate_x_w, gate_x_b, lru_lambda, w_out, ln1_g, ln1_b, w_ffn_up, ffn_conv_w, ffn_conv_b, w_ffn_down, ple_gate_w, ple_gate_b, ple_proj, ln2_g, ln2_b, loss_target, m_w_in, m_attn_sinks, m_rnn_conv_w, m_rnn_conv_b, m_gate_a_w, m_gate_a_b, m_gate_x_w, m_gate_x_b, m_lru_lambda, m_w_out, m_ln1_g, m_ln1_b, m_w_ffn_up, m_ffn_conv_w, m_ffn_conv_b, m_w_ffn_down, m_ple_gate_w, m_ple_gate_b, m_ple_proj, m_ln2_g, m_ln2_b, v_w_in, v_attn_sinks, v_rnn_conv_w, v_rnn_conv_b, v_gate_a_w, v_gate_a_b, v_gate_x_w, v_gate_x_b, v_lru_lambda, v_w_out, v_ln1_g, v_ln1_b, v_w_ffn_up, v_ffn_conv_w, v_ffn_conv_b, v_w_ffn_down, v_ple_gate_w, v_ple_gate_b, v_ple_proj, v_ln2_g, v_ln2_b):
    from_col_blocks = lambda g: g.transpose(1, 0, 2).reshape(g.shape[1], N_DEV * g.shape[2])

    xs, ps, tgt, sinks = x[0], p[0, 0], loss_target[0], attn_sinks[0]
    wa, wx = _block_diag(gate_a_w[0]), _block_diag(gate_x_w[0])

    conv_cols = jnp.concatenate([rnn_conv_w[0].reshape(1, -1), ffn_conv_w[0].reshape(1, -1)], axis=1)
    n_rc, n_fc = 4 * D_RNN // N_DEV, 3 * D_FF // N_DEV
    ((g_in,),) = _comm_call([_Gather([w_in[0].T.astype(BF16)])], "gather_w_in")
    w_in_full = g_in.reshape(D_IN, D_MODEL)

    (q, k, v, xr, gr), (g_conv,) = _in_proj(xs, w_in_full, comm=_Bcast([jnp.broadcast_to(conv_cols, (8, n_rc + n_fc))]))
    rcw = from_col_blocks(g_conv[:, 0, :n_rc].reshape(N_DEV, 4, D_RNN // N_DEV))
    fcw = from_col_blocks(g_conv[:, 0, n_rc:].reshape(N_DEV, 3, D_FF // N_DEV))
    (att,), (w_up,) = _attn_fwd(q, k, v, sinks, comm=_Gather([w_ffn_up[0].astype(BF16)]))
    (rec, h), (g_out, g_down) = _rnn_fwd(xr, gr, rcw, rnn_conv_b, wa, wx, gate_a_b, gate_x_b, lru_lambda,
                                         comm=_Gather([w_out[0].astype(BF16), w_ffn_down[0].astype(BF16)]))
    w_out_full = g_out.reshape(D_MODEL, D_MODEL)
    (z1, h1, h1b, gate, act, gl, vdgl), (g_pg, g_pp) = _mix_ln1_up(
        xs, att, rec, w_out_full, ln1_g, ln1_b, w_up, fcw, ffn_conv_b,
        comm=_Gather([ple_gate_w[0].astype(BF16), ple_proj[0].astype(BF16)]))
    dz2b, dpreb, dppb, dgc, dval, dh1p, acc_t = _tail(
        act, gl, vdgl, h1, h1b, ps, tgt, g_down.reshape(D_FF, D_MODEL), g_pg.reshape(D_MODEL, D_MODEL), ple_gate_b,
        from_col_blocks(g_pp), ln2_g, ln2_b)

    gd_down = _weight_grad([dz2b], [act], "down_grad", "rows_t")
    gd_pg = _weight_grad([h1b], [dpreb], "pg_grad", "rows", ts=1024)
    gd_pp = _weight_grad([ps], [dppb], "pp_grad", "cols", ts=1024)
    (dgate, dz1, dz1b, datt, drec, acc_f, acc_d), (r_down, r_pg, r_pp) = _up_bwd(
        dgc, gate, dval, dh1p, z1, w_up, fcw, w_out_full, ln1_g, comm=_Exchange([gd_down, gd_pg, gd_pp]))
    gd_up = _weight_grad([h1b], [dgate, dval], "up_grad", "cols")
    gd_out = _weight_grad([att, rec], [dz1b], "out_grad", "rows", ts=1024)
    (dq, dkc, dkp, dvc, dvp, acc_s), (r_up,) = _attn_bwd(q, k, v, datt, sinks, comm=_Exchange([gd_up]))
    (dxr, dgr, g_wa, g_wx, acc_r), (r_out,) = _rnn_bwd(xr, gr, h, drec, rcw, rnn_conv_b, wa, wx, gate_a_b, gate_x_b,
                                                       lru_lambda, comm=_Exchange([gd_out]))
    (du, dx), _ = _in_bwd(dq, dkc, dkp, dvc, dvp, dxr, dgr, dz1, w_in_full)
    acc_rows = jnp.concatenate([acc_t, acc_f, acc_d, acc_s, acc_r], axis=1)
    gd_in_a, small_parts = _weight_grad([du], [xs], "in_grad_a", "rows", ts=1024, b_window=(0, D_MODEL // 2),
                                        comm=_Bcast([acc_rows, jnp.concatenate([g_wa, g_wx], axis=1)]))
    gd_in_b, (r_in_a,) = _weight_grad([du], [xs], "in_grad_b", "rows", ts=1024, b_window=(1, D_MODEL // 2),
                                      comm=_Exchange([gd_in_a]))
    ((r_in_b,),) = _comm_call([_Exchange([gd_in_b])], "exchange_w_in")
    r_in = jnp.concatenate([r_in_a, r_in_b], axis=2)

    outs = {}
    res = _sum_adamw(r_in, w_in[0].T, m_w_in[0].T, v_w_in[0].T, "adamw_w_in")
    outs["w_in"] = [r.T[None] for r in res]
    for name, parts, w, m, v in [("w_out", r_out, w_out, m_w_out, v_w_out),
                                 ("w_ffn_up", r_up, w_ffn_up, m_w_ffn_up, v_w_ffn_up),
                                 ("w_ffn_down", r_down, w_ffn_down, m_w_ffn_down, v_w_ffn_down),
                                 ("ple_gate_w", r_pg, ple_gate_w, m_ple_gate_w, v_ple_gate_w),
                                 ("ple_proj", r_pp, ple_proj, m_ple_proj, v_ple_proj)]:
        res = _sum_adamw(parts, w[0], m[0], v[0], "adamw_" + name)
        outs[name] = [r[None] for r in res]

    given = dict(attn_sinks=(attn_sinks, m_attn_sinks, v_attn_sinks), rnn_conv_w=(rnn_conv_w, m_rnn_conv_w, v_rnn_conv_w),
                 rnn_conv_b=(rnn_conv_b, m_rnn_conv_b, v_rnn_conv_b), gate_a_w=(gate_a_w, m_gate_a_w, v_gate_a_w),
                 gate_a_b=(gate_a_b, m_gate_a_b, v_gate_a_b), gate_x_w=(gate_x_w, m_gate_x_w, v_gate_x_w),
                 gate_x_b=(gate_x_b, m_gate_x_b, v_gate_x_b), lru_lambda=(lru_lambda, m_lru_lambda, v_lru_lambda),
                 ln1_g=(ln1_g, m_ln1_g, v_ln1_g), ln1_b=(ln1_b, m_ln1_b, v_ln1_b),
                 ffn_conv_w=(ffn_conv_w, m_ffn_conv_w, v_ffn_conv_w), ffn_conv_b=(ffn_conv_b, m_ffn_conv_b, v_ffn_conv_b),
                 ple_gate_b=(ple_gate_b, m_ple_gate_b, v_ple_gate_b), ln2_g=(ln2_g, m_ln2_g, v_ln2_g),
                 ln2_b=(ln2_b, m_ln2_b, v_ln2_b))
    as_2d = lambda a: a.reshape(-1, a.shape[-1])
    loss_row, small_res = _small_update(*small_parts, [tuple(as_2d(a) for a in given[n]) for n, *_ in _SMALL])
    loss = loss_row[0, 0]
    for (n, *_), res in zip(_SMALL, small_res):
        outs[n] = [r.reshape(given[n][0].shape) for r in res]

    order = ["w_in", "attn_sinks", "rnn_conv_w", "rnn_conv_b", "gate_a_w", "gate_a_b", "gate_x_w", "gate_x_b",
             "lru_lambda", "w_out", "ln1_g", "ln1_b", "w_ffn_up", "ffn_conv_w", "ffn_conv_b", "w_ffn_down",
             "ple_gate_w", "ple_gate_b", "ple_proj", "ln2_g", "ln2_b"]
    return (loss, dx[None], *[outs[n][0] for n in order], *[outs[n][1] for n in order],
            *[outs[n][2] for n in order], *[outs[n][3] for n in order])
```

```python
import jax
import jax.numpy as jnp
from jax import lax
from jax.experimental import pallas as pl
from jax.experimental.pallas import tpu as pltpu

F32 = jnp.float32
BF16 = jnp.bfloat16

D_MODEL = 1024
D_ATT = 512
D_KV = 128
HEAD_DIM = 64
N_HEADS = 8
N_KV = 2
D_RNN = 512
RNN_BLOCKS = 8
D_IN = 1792
D_FF = 3072
PLE_DIM = 256
QBLK = 128
N_DEV = 8
ALPHA = float(2 ** 0.25)
LN_EPS = 1e-5
LRU_C = 8.0
ADAM_LR, ADAM_B1, ADAM_B2, ADAM_EPS, ADAM_WD, ADAM_STEP = 0.001, 0.9, 0.999, 1e-08, 0.01, 10

V7X_VMEM_LIMIT = 56 * 1024 * 1024
MESH = pl.DeviceIdType.MESH


def _params(*sem, vmem=V7X_VMEM_LIMIT):
    return pltpu.CompilerParams(dimension_semantics=sem or None, vmem_limit_bytes=vmem)


def _resident(shape):
    return pl.BlockSpec(shape, lambda *_: (0,) * len(shape), pipeline_mode=pl.Buffered(1))


def _rows(tb, cols):
    return pl.BlockSpec((tb, cols), lambda i: (i, 0))


def _acc(shape):
    return pl.BlockSpec(shape, lambda *_: (0,) * len(shape))


def _dot(a, b):
    return jnp.dot(a, b, preferred_element_type=F32)


def _dot_nt(a, b):
    return lax.dot_general(a, b, (((1,), (1,)), ((), ())), preferred_element_type=F32)


def _dot_tn(a, b):
    return lax.dot_general(a, b, (((0,), (0,)), ((), ())), preferred_element_type=F32)


def _sigmoid(x):
    return 1.0 / (1.0 + jnp.exp(-x))


_GELU_C = 0.7978845608028654
_GELU_K = 0.044715


def _gelu_and_grad(x):
    u = x * x
    t = jnp.tanh(x * (_GELU_C + (_GELU_C * _GELU_K) * u))
    hp = 0.5 + 0.5 * t
    dg = hp + x * (0.5 - 0.5 * (t * t)) * (_GELU_C + (3.0 * _GELU_C * _GELU_K) * u)
    return x * hp, dg


def _gelu(x):
    return 0.5 * x * (1.0 + jnp.tanh(_GELU_C * (x + _GELU_K * x * x * x)))


def _ln_stats(z):
    mu = jnp.mean(z, axis=-1, keepdims=True)
    zc = z - mu
    var = jnp.mean(zc * zc, axis=-1, keepdims=True)
    rstd = lax.rsqrt(var + LN_EPS)
    return zc * rstd, rstd


def _ln_bwd(dy, xhat, rstd, g):
    dxh = dy * g
    m1 = jnp.mean(dxh, axis=-1, keepdims=True)
    m2 = jnp.mean(dxh * xhat, axis=-1, keepdims=True)
    return rstd * (dxh - m1 - xhat * m2)


def _softplus_neg(lam):
    u = jnp.exp(-jnp.abs(lam))
    w = 1.0 + u
    d = w - 1.0
    log1p_u = jnp.where(d == 0.0, u, jnp.log(w) * (u / jnp.where(d == 0.0, 1.0, d)))
    return jnp.maximum(-lam, 0.0) + log1p_u


def _shift_down(x, halo, s):
    xs = pltpu.roll(x, s, 0)
    hs = pltpu.roll(halo, s, 0)
    row8 = lax.broadcasted_iota(jnp.int32, hs.shape, 0)
    first = jnp.where(row8 < s, hs, xs[:8])
    return jnp.concatenate([first, xs[8:]], axis=0)


def _shift_up(x, halo, s):
    n = x.shape[0]
    xs = pltpu.roll(x, n - s, 0)
    hs = pltpu.roll(halo, 8 - s, 0)
    row8 = lax.broadcasted_iota(jnp.int32, hs.shape, 0)
    last = jnp.where(row8 >= 8 - s, hs, xs[n - 8:])
    return jnp.concatenate([xs[:n - 8], last], axis=0)


def _row_sum(x):
    return jnp.sum(x, axis=0, keepdims=True)


def _put_rows(acc_ref, rows):
    row8 = lax.broadcasted_iota(jnp.int32, acc_ref.shape, 0)
    upd = jnp.zeros(acc_ref.shape, F32)
    for r, vec in enumerate(rows):
        upd = jnp.where(row8 == r, vec, upd)
    acc_ref[...] += upd


def _place():
    return lax.axis_index("x"), lax.axis_index("y"), lax.axis_index("c")


def _dev_index(px, py, pc):
    return 4 * px + 2 * py + pc


_ANY = pl.BlockSpec(memory_space=pl.ANY)


class _Gather:
    def __init__(self, arrays):
        self.arrays = list(arrays)
        self.n = len(self.arrays)

    def out_shape(self):
        return [jax.ShapeDtypeStruct((N_DEV,) + s.shape, s.dtype) for s in self.arrays]

    def scratch(self):
        return [pltpu.SemaphoreType.DMA((self.n, 7)), pltpu.SemaphoreType.DMA((self.n, 7)),
                pltpu.SemaphoreType.DMA((self.n,))]

    def _parts(self, ins, outs, sems):
        send_sems, recv_sems, local_sems = sems
        x, y, c = _place()
        me, sibling = (x, y, c), (x, y, 1 - c)
        chips = [(1 - x, y), (x, 1 - y), (1 - x, 1 - y)]

        def copy(a, k, block, to, src=None):
            rows = outs[a].at[_dev_index(*block)]
            return pltpu.make_async_remote_copy(
                src_ref=rows if src is None else src, dst_ref=rows, send_sem=send_sems.at[a, k],
                recv_sem=recv_sems.at[a, k], device_id=to, device_id_type=MESH)

        rng = range(self.n)
        mine = [pltpu.make_async_copy(ins[a], outs[a].at[_dev_index(*me)], local_sems.at[a]) for a in rng]
        first = [copy(a, 0, me, sibling, src=ins[a]) for a in rng]
        first += [copy(a, 1 + j, me, (*chip, c), src=ins[a]) for j, chip in enumerate(chips) for a in rng]
        landed = [copy(a, 1 + j, (*chip, c), me) for j, chip in enumerate(chips) for a in rng]
        passed = [copy(a, 4 + j, (*chip, c), sibling) for j, chip in enumerate(chips) for a in rng]
        from_sibling = [copy(a, 0, sibling, me) for a in rng]
        from_sibling += [copy(a, 4 + j, (*chip, 1 - c), me) for j, chip in enumerate(chips) for a in rng]
        return mine, first, landed, passed, from_sibling

    def start(self, ins, outs, sems):
        mine, first, _, _, _ = self._parts(ins, outs, sems)
        for cp in mine + first:
            cp.start()

    def forward(self, ins, outs, sems):
        _, _, landed, passed, _ = self._parts(ins, outs, sems)
        for got, fwd in zip(landed, passed):
            got.wait_recv()
            fwd.start()

    def finish(self, ins, outs, sems):
        mine, first, _, passed, from_sibling = self._parts(ins, outs, sems)
        for cp in from_sibling:
            cp.wait_recv()
        for cp in first + passed:
            cp.wait_send()
        for cp in mine:
            cp.wait()

    def before(self, ins, outs, sems, step, nsteps):
        pl.when(step == 0)(lambda: self.start(ins, outs, sems))
        pl.when(step == (7 * nsteps) // 8)(lambda: self.forward(ins, outs, sems))

    def after(self, ins, outs, sems, step, nsteps):
        pl.when(step == nsteps - 1)(lambda: self.finish(ins, outs, sems))


class _Exchange:
    def __init__(self, arrays):
        self.arrays = list(arrays)
        self.n = len(self.arrays)

    def out_shape(self):
        return [jax.ShapeDtypeStruct(b.shape, b.dtype) for b in self.arrays]

    def scratch(self):
        return [pltpu.SemaphoreType.DMA((self.n, 7)), pltpu.SemaphoreType.DMA((self.n, 7)),
                pltpu.SemaphoreType.DMA((self.n,))]

    def _parts(self, ins, outs, sems):
        send_sems, recv_sems, local_sems = sems
        x, y, c = _place()
        me = _dev_index(x, y, c)
        peers = [(x ^ (k >> 2), y ^ ((k >> 1) & 1), c ^ (k & 1)) for k in range(1, N_DEV)]
        rng = range(self.n)
        mine = [pltpu.make_async_copy(ins[a].at[me], outs[a].at[me], local_sems.at[a]) for a in rng]
        sent = [pltpu.make_async_remote_copy(
            src_ref=ins[a].at[_dev_index(*to)], dst_ref=outs[a].at[me], send_sem=send_sems.at[a, k],
            recv_sem=recv_sems.at[a, k], device_id=to, device_id_type=MESH) for k, to in enumerate(peers) for a in rng]
        arrivals = [pltpu.make_async_remote_copy(
            src_ref=ins[a].at[me], dst_ref=outs[a].at[_dev_index(*frm)], send_sem=send_sems.at[a, k],
            recv_sem=recv_sems.at[a, k], device_id=frm, device_id_type=MESH) for k, frm in enumerate(peers) for a in rng]
        return mine, sent, arrivals

    def start(self, ins, outs, sems):
        mine, sent, _ = self._parts(ins, outs, sems)
        for cp in mine + sent:
            cp.start()

    def finish(self, ins, outs, sems):
        mine, sent, arrivals = self._parts(ins, outs, sems)
        for cp in arrivals:
            cp.wait_recv()
        for cp in sent:
            cp.wait_send()
        for cp in mine:
            cp.wait()

    def before(self, ins, outs, sems, step, nsteps):
        pl.when(step == 0)(lambda: self.start(ins, outs, sems))

    def after(self, ins, outs, sems, step, nsteps):
        pl.when(step == nsteps - 1)(lambda: self.finish(ins, outs, sems))


class _Bcast(_Exchange):
    def out_shape(self):
        return [jax.ShapeDtypeStruct((N_DEV,) + s.shape, s.dtype) for s in self.arrays]

    def _parts(self, ins, outs, sems):
        send_sems, recv_sems, local_sems = sems
        x, y, c = _place()
        me = _dev_index(x, y, c)
        peers = [(x ^ (k >> 2), y ^ ((k >> 1) & 1), c ^ (k & 1)) for k in range(1, N_DEV)]
        rng = range(self.n)
        mine = [pltpu.make_async_copy(ins[a], outs[a].at[me], local_sems.at[a]) for a in rng]
        sent = [pltpu.make_async_remote_copy(
            src_ref=ins[a], dst_ref=outs[a].at[me], send_sem=send_sems.at[a, k], recv_sem=recv_sems.at[a, k],
            device_id=to, device_id_type=MESH) for k, to in enumerate(peers) for a in rng]
        arrivals = [pltpu.make_async_remote_copy(
            src_ref=ins[a], dst_ref=outs[a].at[_dev_index(*frm)], send_sem=send_sems.at[a, k],
            recv_sem=recv_sems.at[a, k], device_id=frm, device_id_type=MESH) for k, frm in enumerate(peers) for a in rng]
        return mine, sent, arrivals


def _comm_call(comms, name):
    ns = [c.n for c in comms]
    n = sum(ns)

    def body(*refs):
        parts, a, s = [], 0, 2 * n
        for c in comms:
            parts.append((c, refs[a:a + c.n], refs[n + a:n + a + c.n], refs[s:s + 3]))
            a, s = a + c.n, s + 3
        for c, ins, outs, sems in parts:
            c.start(ins, outs, sems)
        for c, ins, outs, sems in parts:
            if isinstance(c, _Gather):
                c.forward(ins, outs, sems)
        for c, ins, outs, sems in parts:
            c.finish(ins, outs, sems)

    res = pl.pallas_call(
        body, name=name, in_specs=[_ANY] * n, out_specs=[_ANY] * n,
        out_shape=[s for c in comms for s in c.out_shape()], scratch_shapes=[s for c in comms for s in c.scratch()],
    )(*[arr for c in comms for arr in c.arrays])
    out, a = [], 0
    for k in ns:
        out.append(res[a:a + k])
        a += k
    return out


def _pcall(body, args, *, name, grid, in_specs, out_specs, out_shape, scratch_shapes=(), sem="parallel", comm=None,
           step_axis=0):
    sem = (sem,) * len(grid) if isinstance(sem, str) else sem
    if comm is None:
        res = pl.pallas_call(body, name=name, grid=grid, in_specs=in_specs, out_specs=out_specs, out_shape=out_shape,
                             scratch_shapes=list(scratch_shapes), compiler_params=_params(*sem))(*args)
        return res, []
    n_in, n_out, n_scr, n = len(in_specs), len(out_specs), len(scratch_shapes), comm.n
    nsteps = grid[step_axis]
    assert all(g == 1 for ax, g in enumerate(grid) if ax != step_axis)

    def hosted(*refs):
        ins, cin = refs[:n_in], refs[n_in:n_in + n]
        o0 = n_in + n
        outs, cout = refs[o0:o0 + n_out], refs[o0 + n_out:o0 + n_out + n]
        s0 = o0 + n_out + n
        scr, sems = refs[s0:s0 + n_scr], refs[s0 + n_scr:]
        step = pl.program_id(step_axis)
        comm.before(cin, cout, sems, step, nsteps)
        body(*ins, *outs, *scr)
        comm.after(cin, cout, sems, step, nsteps)

    res = pl.pallas_call(
        hosted, name=name, grid=grid, in_specs=list(in_specs) + [_ANY] * n, out_specs=list(out_specs) + [_ANY] * n,
        out_shape=list(out_shape) + comm.out_shape(), scratch_shapes=list(scratch_shapes) + comm.scratch(),
        compiler_params=_params(*(("arbitrary",) * len(grid))))(*args, *comm.arrays)
    return res[:n_out], res[n_out:]


def _in_proj(x, w_in_t, comm=None):
    S = x.shape[0]
    tb = min(512, S)

    def body(x_ref, w_ref, q_ref, k_ref, v_ref, xr_ref, gr_ref):
        u = _dot_nt(x_ref[...].astype(BF16), w_ref[...])
        q_ref[...] = (u[:, :D_ATT] * (HEAD_DIM ** -0.5)).astype(BF16)
        k_ref[...] = u[:, D_ATT:D_ATT + D_KV].astype(BF16)
        v_ref[...] = u[:, D_ATT + D_KV:D_ATT + 2 * D_KV].astype(BF16)
        xr_ref[...] = u[:, D_ATT + 2 * D_KV:D_ATT + 2 * D_KV + D_RNN]
        gr_ref[...] = u[:, D_ATT + 2 * D_KV + D_RNN:]

    return _pcall(
        body, (x, w_in_t), name="in_proj", grid=(S // tb,), comm=comm,
        in_specs=[_rows(tb, D_MODEL), _resident((D_IN, D_MODEL))],
        out_specs=[_rows(tb, D_ATT), _rows(tb, D_KV), _rows(tb, D_KV), _rows(tb, D_RNN), _rows(tb, D_RNN)],
        out_shape=[jax.ShapeDtypeStruct((S, D_ATT), BF16), jax.ShapeDtypeStruct((S, D_KV), BF16),
                   jax.ShapeDtypeStruct((S, D_KV), BF16), jax.ShapeDtypeStruct((S, D_RNN), F32),
                   jax.ShapeDtypeStruct((S, D_RNN), F32)])


GROUP = N_HEADS // N_KV


def _band_mask(i):
    qi = lax.broadcasted_iota(jnp.int32, (GROUP * QBLK, 2 * QBLK), 0) & (QBLK - 1)
    sj = lax.broadcasted_iota(jnp.int32, (GROUP * QBLK, 2 * QBLK), 1)
    return (sj > qi) & (sj <= qi + QBLK) & ((sj >= QBLK) | (i > 0))


def _stack_heads(x, g):
    return jnp.concatenate([x[:, (g * GROUP + hh) * HEAD_DIM:(g * GROUP + hh + 1) * HEAD_DIM] for hh in range(GROUP)],
                           axis=0)


def _unstack_heads(x4):
    return [x4[hh * QBLK:(hh + 1) * QBLK] for hh in range(GROUP)]


def _sink_column(sink_ref, g):
    head = lax.broadcasted_iota(jnp.int32, (GROUP * QBLK, 1), 0) // QBLK
    col = jnp.full((GROUP * QBLK, 1), sink_ref[g * GROUP], F32)
    for hh in range(1, GROUP):
        col = jnp.where(head == hh, sink_ref[g * GROUP + hh], col)
    return col


def _attn_specs():
    cur = lambda i: (i, 0)
    prev = lambda i: (jnp.maximum(i - 1, 0), 0)
    return [pl.BlockSpec((QBLK, D_KV), cur), pl.BlockSpec((QBLK, D_KV), prev),
            pl.BlockSpec((QBLK, D_KV), cur), pl.BlockSpec((QBLK, D_KV), prev)]


def _attn_fwd(q, k, v, sinks, comm=None):
    S = q.shape[0]

    def body(sink_ref, q_ref, kc_ref, kp_ref, vc_ref, vp_ref, o_ref):
        valid = _band_mask(pl.program_id(0))
        outs = []
        qv = q_ref[...]
        kall = jnp.concatenate([kp_ref[...], kc_ref[...]], axis=0)
        vall = jnp.concatenate([vp_ref[...], vc_ref[...]], axis=0)
        for g in range(N_KV):
            kcat = kall[:, g * HEAD_DIM:(g + 1) * HEAD_DIM]
            vcat = vall[:, g * HEAD_DIM:(g + 1) * HEAD_DIM]
            s = jnp.where(valid, _dot_nt(_stack_heads(qv, g), kcat), -1e30)
            sink = _sink_column(sink_ref, g)
            m = jnp.maximum(jnp.max(s, axis=1, keepdims=True), sink)
            p = jnp.exp(s - m)
            l = jnp.sum(p, axis=1, keepdims=True) + jnp.exp(sink - m)
            outs += _unstack_heads(_dot(p.astype(BF16), vcat) / l)
        o_ref[...] = jnp.concatenate(outs, axis=1).astype(BF16)

    return _pcall(
        body, (sinks, q, k, k, v, v), name="attn_fwd", grid=(S // QBLK,), comm=comm,
        in_specs=[pl.BlockSpec(memory_space=pltpu.SMEM), _rows(QBLK, D_ATT)] + _attn_specs(),
        out_specs=[_rows(QBLK, D_ATT)], out_shape=[jax.ShapeDtypeStruct((S, D_ATT), BF16)])


def _w_rows(w_ref):
    return [w_ref[k:k + 1, :] for k in range(w_ref.shape[0])]


def _conv4(x, halo, w, b):
    y = b + w[3] * x
    for s in (1, 2, 3):
        y = y + w[3 - s] * _shift_down(x, halo, s)
    return y


def _rnn_gates(xc, wa, wx, ba, bx, sp):
    xcb = xc.astype(BF16)
    r = _sigmoid(_dot(xcb, wa) + ba)
    ig = _sigmoid(_dot(xcb, wx) + bx)
    la = -LRU_C * r * sp
    a = jnp.exp(la)
    t = jnp.tanh(la)
    f = jnp.sqrt(-2.0 * t / (1.0 - t))
    return r, ig, a, f


def _rnn_fwd(xr, gr, conv_w, conv_b, wa, wx, ba, bx, lam, comm=None):
    S = xr.shape[0]
    tb = min(256, S)

    def body(xr_ref, gr_ref, cw_ref, cb_ref, wa_ref, wx_ref, ba_ref, bx_ref, lam_ref, rec_ref, h_ref,
             xc_ref, r_ref, ig_ref, a_ref, f_ref, halo_s, hc_s, a_s, b_s):
        @pl.when(pl.program_id(0) == 0)
        def _():
            halo_s[...] = jnp.zeros_like(halo_s)
            hc_s[...] = jnp.zeros_like(hc_s)

        x = xr_ref[...]
        xc = _conv4(x, halo_s[...], _w_rows(cw_ref), cb_ref[...])
        halo_s[...] = x[tb - 8:]
        r, ig, a, f = _rnn_gates(xc, wa_ref[...], wx_ref[...], ba_ref[...], bx_ref[...], _softplus_neg(lam_ref[...]))
        xc_ref[...] = xc
        r_ref[...] = r
        ig_ref[...] = ig
        a_ref[...] = a
        f_ref[...] = f
        a_s[...] = a
        b_s[...] = f * ig * xc
        row8 = lax.broadcasted_iota(jnp.int32, (8, D_RNN), 0)

        def tile(t, hc):
            o = pl.multiple_of(t * 8, 8)
            at = a_s[pl.ds(o, 8), :]
            bt = b_s[pl.ds(o, 8), :]
            for s in (1, 2, 4):
                keep = row8 >= s
                a_sh = jnp.where(keep, pltpu.roll(at, s, 0), 1.0)
                b_sh = jnp.where(keep, pltpu.roll(bt, s, 0), 0.0)
                bt = at * b_sh + bt
                at = at * a_sh
            ht = at * hc + bt
            b_s[pl.ds(o, 8), :] = ht
            return _row_sum(jnp.where(row8 == 7, ht, 0.0))

        hc_s[0:1, :] = lax.fori_loop(0, tb // 8, tile, hc_s[0:1, :], unroll=2)
        h = b_s[...]
        h_ref[...] = h
        rec_ref[...] = (h * _gelu(gr_ref[...])).astype(BF16)

    vec = _resident((1, D_RNN))
    kept = jax.ShapeDtypeStruct((S, D_RNN), F32)
    return _pcall(
        body, (xr, gr, conv_w, conv_b, wa, wx, ba, bx, lam), name="rnn_fwd", grid=(S // tb,), sem="arbitrary", comm=comm,
        in_specs=[_rows(tb, D_RNN), _rows(tb, D_RNN), _resident((4, D_RNN)), vec,
                  _resident((D_RNN, D_RNN)), _resident((D_RNN, D_RNN)), vec, vec, vec],
        out_specs=[_rows(tb, D_RNN)] * 7,
        out_shape=[jax.ShapeDtypeStruct((S, D_RNN), BF16), kept, kept, kept, kept, kept, kept],
        scratch_shapes=[pltpu.VMEM((8, D_RNN), F32), pltpu.VMEM((8, D_RNN), F32),
                        pltpu.VMEM((tb, D_RNN), F32), pltpu.VMEM((tb, D_RNN), F32)])


def _mix_ln1_up(x, att, rec, w_out, ln1_g, ln1_b, w_up, fcw, fcb, comm=None):
    S = x.shape[0]
    tb = min(256, S)
    nblk, _, wblk = w_up.shape
    half = nblk // 2

    def body(x_ref, att_ref, rec_ref, wo_ref, g_ref, b_ref, wu_ref, fcw_ref, fcb_ref,
             z1_ref, h1_ref, h1b_ref, gate_ref, act_ref, gl_ref, vdgl_ref, halo_s):
        @pl.when(pl.program_id(0) == 0)
        def _():
            halo_s[...] = jnp.zeros_like(halo_s)

        z1 = ALPHA * x_ref[...] + _dot(att_ref[...], wo_ref[:D_ATT, :]) + _dot(rec_ref[...], wo_ref[D_ATT:, :])
        z1_ref[...] = z1
        xhat, _ = _ln_stats(z1)
        h1 = xhat * g_ref[...] + b_ref[...]
        h1_ref[...] = h1
        h1b = h1.astype(BF16)
        h1b_ref[...] = h1b
        for jj in range(half):
            cols = slice(jj * wblk, (jj + 1) * wblk)
            gate = _dot(h1b, wu_ref[jj])
            val = _dot(h1b, wu_ref[jj + half])
            halo = halo_s[:, cols]
            conv = (fcb_ref[:, cols] + fcw_ref[2:3, cols] * gate + fcw_ref[1:2, cols] * _shift_down(gate, halo, 1)
                    + fcw_ref[0:1, cols] * _shift_down(gate, halo, 2))
            halo_s[:, cols] = gate[tb - 8:]
            gl, dgl = _gelu_and_grad(conv)
            gate_ref[:, cols] = gate.astype(BF16)
            act_ref[:, cols] = (gl * val).astype(BF16)
            gl_ref[:, cols] = gl.astype(BF16)
            vdgl_ref[:, cols] = (val * dgl).astype(BF16)

    vec = _resident((1, D_MODEL))
    wide = jax.ShapeDtypeStruct((S, D_FF), BF16)
    return _pcall(
        body, (x, att, rec, w_out, ln1_g, ln1_b, w_up, fcw, fcb), name="mix_ln1_up", grid=(S // tb,),
        sem="arbitrary", comm=comm,
        in_specs=[_rows(tb, D_MODEL), _rows(tb, D_ATT), _rows(tb, D_RNN), _resident((D_MODEL, D_MODEL)), vec, vec,
                  _resident(w_up.shape), _resident((3, D_FF)), _resident((1, D_FF))],
        out_specs=[_rows(tb, D_MODEL), _rows(tb, D_MODEL), _rows(tb, D_MODEL)] + [_rows(tb, D_FF)] * 4,
        out_shape=[jax.ShapeDtypeStruct((S, D_MODEL), F32), jax.ShapeDtypeStruct((S, D_MODEL), F32),
                   jax.ShapeDtypeStruct((S, D_MODEL), BF16), wide, wide, wide, wide],
        scratch_shapes=[pltpu.VMEM((8, D_FF), F32)])


def _tail(act, gl, vdgl, h1, h1b, p, tgt, w_down, w_pg, b_pg, w_pp, ln2_g, ln2_b):
    S = h1.shape[0]
    tb = min(256, S)

    def body(act_ref, gl_ref, vdgl_ref, h1_ref, h1b_ref, p_ref, t_ref, wd_ref, wpg_ref, bpg_ref, wpp_ref, g2_ref, b2_ref,
             dz2_ref, dpre_ref, dpp_ref, dgc_ref, dval_ref, dh1_ref, acc_ref):
        i = pl.program_id(0)

        @pl.when(i == 0)
        def _():
            acc_ref[...] = jnp.zeros_like(acc_ref)

        ffn = _dot(act_ref[...], wd_ref[...])
        h1 = h1_ref[...]
        sg = _sigmoid(_dot(h1b_ref[...], wpg_ref[...]) + bpg_ref[...])
        pp = _dot(p_ref[...].astype(BF16), wpp_ref[...])
        z2 = ALPHA * h1 + ffn + sg * pp
        xhat2, rstd2 = _ln_stats(z2)
        y = xhat2 * g2_ref[...] + b2_ref[...]
        err = y - t_ref[...]
        dy = err * (1.0 / D_MODEL)
        loss = 0.5 * jnp.sum(jnp.sum(err * err, axis=1, keepdims=True), axis=0, keepdims=True) * (1.0 / D_MODEL)
        dz2 = _ln_bwd(dy, xhat2, rstd2, g2_ref[...])
        dz2b = dz2.astype(BF16)
        dz2_ref[...] = dz2b
        dpre = dz2 * pp * sg * (1.0 - sg)
        dpreb = dpre.astype(BF16)
        dpre_ref[...] = dpreb
        dpp_ref[...] = (dz2 * sg).astype(BF16)
        dh1_ref[...] = ALPHA * dz2 + _dot_nt(dpreb, wpg_ref[...])
        dactb = _dot_nt(dz2b, wd_ref[...]).astype(BF16)
        dval_ref[...] = dactb * gl_ref[...]
        dgc_ref[...] = dactb * vdgl_ref[...]
        _put_rows(acc_ref, [_row_sum(dy * xhat2), _row_sum(dy), _row_sum(dpre),
                            jnp.broadcast_to(loss, (1, D_MODEL))])

    vec = _resident((1, D_MODEL))
    return pl.pallas_call(
        body, name="tail", grid=(S // tb,),
        in_specs=[_rows(tb, D_FF), _rows(tb, D_FF), _rows(tb, D_FF), _rows(tb, D_MODEL), _rows(tb, D_MODEL),
                  _rows(tb, PLE_DIM), _rows(tb, D_MODEL), _resident((D_FF, D_MODEL)), _resident((D_MODEL, D_MODEL)), vec,
                  _resident((PLE_DIM, D_MODEL)), vec, vec],
        out_specs=[_rows(tb, D_MODEL), _rows(tb, D_MODEL), _rows(tb, D_MODEL), _rows(tb, D_FF),
                   _rows(tb, D_FF), _rows(tb, D_MODEL), _acc((8, D_MODEL))],
        out_shape=[jax.ShapeDtypeStruct((S, D_MODEL), BF16),
                   jax.ShapeDtypeStruct((S, D_MODEL), BF16), jax.ShapeDtypeStruct((S, D_MODEL), BF16),
                   jax.ShapeDtypeStruct((S, D_FF), BF16), jax.ShapeDtypeStruct((S, D_FF), BF16),
                   jax.ShapeDtypeStruct((S, D_MODEL), F32), jax.ShapeDtypeStruct((8, D_MODEL), F32)],
        compiler_params=_params("arbitrary"),
    )(act, gl, vdgl, h1, h1b, p, tgt, w_down, w_pg, b_pg, w_pp, ln2_g, ln2_b)


def _weight_grad(a_list, b_list, name, layout, ts=512, comm=None, b_window=None):
    S = a_list[0].shape[0]
    ms = [a.shape[1] for a in a_list]
    M, nb = sum(ms), len(b_list)
    win, Nb = b_window if b_window else (0, b_list[0].shape[1])
    ts = min(ts, S)
    nk = S // ts
    per_b = N_DEV // nb
    na = len(a_list)

    def body(*refs):
        a_refs, b_refs, o_ref, acc_ref = refs[:na], refs[na:na + nb], refs[na + nb], refs[na + nb + 1]
        j, k = pl.program_id(0), pl.program_id(1)

        @pl.when(k == 0)
        def _():
            acc_ref[...] = jnp.zeros_like(acc_ref)

        for jj in range(nb):
            @pl.when(j == jj)
            def _():
                b = b_refs[jj][...].astype(BF16)
                off = 0
                for a_ref, m in zip(a_refs, ms):
                    acc_ref[off:off + m, :] += _dot_tn(a_ref[...].astype(BF16), b)
                    off += m

        @pl.when(k == nk - 1)
        def _():
            for d in range(per_b):
                if layout == "rows":
                    o_ref[d] = acc_ref[d * (M // N_DEV):(d + 1) * (M // N_DEV), :].astype(BF16)
                elif layout == "cols":
                    o_ref[d] = acc_ref[:, d * (Nb // per_b):(d + 1) * (Nb // per_b)].astype(BF16)
                else:
                    o_ref[d] = acc_ref[:, d * (Nb // per_b):(d + 1) * (Nb // per_b)].T.astype(BF16)

    def b_index(jj):
        return lambda j, k: (jnp.where(j == jj, k, jnp.where(j < jj, 0, nk - 1)), win)

    if layout == "rows":
        assert nb == 1
        blk = (N_DEV, M // N_DEV, Nb)
    elif layout == "cols":
        blk = (per_b, M, Nb // per_b)
    else:
        blk = (per_b, Nb // per_b, M)
    (res,), comm_res = _pcall(
        body, (*a_list, *b_list), name=name, grid=(nb, nk), sem="arbitrary", comm=comm, step_axis=1,
        in_specs=[pl.BlockSpec((ts, m), lambda j, k: (k, 0)) for m in ms]
        + [pl.BlockSpec((ts, Nb), b_index(jj)) for jj in range(nb)],
        out_specs=[pl.BlockSpec(blk, lambda j, k: (j, 0, 0))],
        out_shape=[jax.ShapeDtypeStruct((N_DEV,) + blk[1:], BF16)],
        scratch_shapes=[pltpu.VMEM((M, Nb), F32)])
    return (res, comm_res) if comm is not None else res


def _up_bwd(dgc, gate, dval, dh1p, z1, w_up, fcw, w_out, ln1_g, comm=None):
    S = z1.shape[0]
    tb = min(256, S)
    t16 = tb // 16
    n16 = S // 16
    nblk, _, wblk = w_up.shape
    half = nblk // 2
    nsteps = S // tb

    def body(dgc_ref, dgn_ref, gc_ref, gp_ref, dval_ref, dh1p_ref, z1_ref, wu_ref, fcw_ref, wo_ref, g1_ref,
             dgate_ref, dz1_ref, dz1b_ref, datt_ref, drec_ref, accf_ref, accd_ref):
        i = pl.program_id(0)

        @pl.when(i == 0)
        def _():
            accf_ref[...] = jnp.zeros_like(accf_ref)
            accd_ref[...] = jnp.zeros_like(accd_ref)

        dg = dgc_ref[...].astype(F32)
        nxt = jnp.where(i < nsteps - 1, dgn_ref[...].astype(F32)[0:8], 0.0)
        w = _w_rows(fcw_ref)
        dgate = (w[2] * dg + w[1] * _shift_up(dg, nxt, 1) + w[0] * _shift_up(dg, nxt, 2)).astype(BF16)
        dgate_ref[...] = dgate
        gate = gc_ref[...].astype(F32)
        halo = jnp.where(i > 0, gp_ref[...].astype(F32)[8:16], 0.0)
        _put_rows(accf_ref, [_row_sum(dg * _shift_down(gate, halo, 2)), _row_sum(dg * _shift_down(gate, halo, 1)),
                             _row_sum(dg * gate), _row_sum(dg)])

        dh1 = dh1p_ref[...]
        for j in range(nblk):
            src = dgate if j < half else dval_ref[...]
            jj = j % half
            dh1 = dh1 + _dot_nt(src[:, jj * wblk:(jj + 1) * wblk], wu_ref[j])
        xhat1, rstd1 = _ln_stats(z1_ref[...])
        dz1 = _ln_bwd(dh1, xhat1, rstd1, g1_ref[...])
        dz1_ref[...] = dz1
        dz1b = dz1.astype(BF16)
        dz1b_ref[...] = dz1b
        dcat = _dot_nt(dz1b, wo_ref[...])
        datt_ref[...] = dcat[:, :D_ATT].astype(BF16)
        drec_ref[...] = dcat[:, D_ATT:]
        _put_rows(accd_ref, [_row_sum(dh1 * xhat1), _row_sum(dh1)])

    prev16 = pl.BlockSpec((16, D_FF), lambda i: (jnp.maximum(i * t16 - 1, 0), 0))
    next16 = pl.BlockSpec((16, D_FF), lambda i: (jnp.minimum((i + 1) * t16, n16 - 1), 0))
    return _pcall(
        body, (dgc, dgc, gate, gate, dval, dh1p, z1, w_up, fcw, w_out, ln1_g), name="up_bwd",
        grid=(nsteps,), sem="arbitrary", comm=comm,
        in_specs=[_rows(tb, D_FF), next16, _rows(tb, D_FF), prev16, _rows(tb, D_FF), _rows(tb, D_MODEL),
                  _rows(tb, D_MODEL), _resident(w_up.shape), _resident((3, D_FF)),
                  _resident((D_MODEL, D_MODEL)), _resident((1, D_MODEL))],
        out_specs=[_rows(tb, D_FF), _rows(tb, D_MODEL), _rows(tb, D_MODEL), _rows(tb, D_ATT), _rows(tb, D_RNN),
                   _acc((8, D_FF)), _acc((8, D_MODEL))],
        out_shape=[jax.ShapeDtypeStruct((S, D_FF), BF16), jax.ShapeDtypeStruct((S, D_MODEL), F32),
                   jax.ShapeDtypeStruct((S, D_MODEL), BF16), jax.ShapeDtypeStruct((S, D_ATT), BF16),
                   jax.ShapeDtypeStruct((S, D_RNN), F32), jax.ShapeDtypeStruct((8, D_FF), F32),
                   jax.ShapeDtypeStruct((8, D_MODEL), F32)])


def _attn_bwd(q, k, v, do, sinks, comm=None):
    S = q.shape[0]
    grp = N_HEADS // N_KV

    def body(sink_ref, q_ref, kc_ref, kp_ref, vc_ref, vp_ref, do_ref, dq_ref, dkc_ref, dkp_ref, dvc_ref, dvp_ref,
             ds_ref):
        i = pl.program_id(0)

        @pl.when(i == 0)
        def _():
            ds_ref[...] = jnp.zeros_like(ds_ref)

        valid = _band_mask(i)
        row8 = lax.broadcasted_iota(jnp.int32, (8, 128), 0)
        lane8 = lax.broadcasted_iota(jnp.int32, (8, 128), 1)
        dqs, dks, dvs = [], [], []
        dsink = jnp.zeros((8, 128), F32)
        qv = q_ref[...]
        dov = do_ref[...]
        kall = jnp.concatenate([kp_ref[...], kc_ref[...]], axis=0)
        vall = jnp.concatenate([vp_ref[...], vc_ref[...]], axis=0)
        for g in range(N_KV):
            kcat = kall[:, g * HEAD_DIM:(g + 1) * HEAD_DIM]
            vcat = vall[:, g * HEAD_DIM:(g + 1) * HEAD_DIM]
            q4, do4 = _stack_heads(qv, g), _stack_heads(dov, g)
            s = jnp.where(valid, _dot_nt(q4, kcat), -1e30)
            sink = _sink_column(sink_ref, g)
            m = jnp.maximum(jnp.max(s, axis=1, keepdims=True), sink)
            e = jnp.exp(s - m)
            es = jnp.exp(sink - m)
            inv = 1.0 / (jnp.sum(e, axis=1, keepdims=True) + es)
            p = e * inv
            dp = _dot_nt(do4, vcat)
            delta = jnp.sum(p * dp, axis=1, keepdims=True)
            dsc = (p * (dp - delta)).astype(BF16)
            dqs += _unstack_heads(_dot(dsc, kcat) * (HEAD_DIM ** -0.5))
            dks.append(_dot_tn(q4, dsc))
            dvs.append(_dot_tn(do4, p.astype(BF16)))
            for hh, part in enumerate(_unstack_heads(-es * inv * delta)):
                here = (row8 == 0) & (lane8 == g * grp + hh)
                dsink = dsink + jnp.where(here, jnp.sum(part, axis=0, keepdims=True), 0.0)
        dq_ref[...] = jnp.concatenate(dqs, axis=1).astype(BF16)
        dk = jnp.concatenate(dks, axis=0).T
        dv = jnp.concatenate(dvs, axis=0).T
        dkp_ref[...] = dk[:QBLK]
        dkc_ref[...] = dk[QBLK:]
        dvp_ref[...] = dv[:QBLK]
        dvc_ref[...] = dv[QBLK:]
        ds_ref[...] += dsink

    kvs = jax.ShapeDtypeStruct((S, D_KV), F32)
    return _pcall(
        body, (sinks, q, k, k, v, v, do), name="attn_bwd", grid=(S // QBLK,), sem="arbitrary", comm=comm,
        in_specs=[pl.BlockSpec(memory_space=pltpu.SMEM), _rows(QBLK, D_ATT)] + _attn_specs() + [_rows(QBLK, D_ATT)],
        out_specs=[_rows(QBLK, D_ATT), _rows(QBLK, D_KV), _rows(QBLK, D_KV), _rows(QBLK, D_KV), _rows(QBLK, D_KV),
                   _acc((8, 128))],
        out_shape=[jax.ShapeDtypeStruct((S, D_ATT), BF16), kvs, kvs, kvs, kvs, jax.ShapeDtypeStruct((8, 128), F32)])


def _rnn_bwd(xr, gr, h, kept, drec, conv_w, wa, wx, lam, comm=None):
    S = xr.shape[0]
    tb = min(256, S)
    t8 = tb // 8
    nsteps = S // tb

    def body(xr_ref, xp_ref, gr_ref, h_ref, hp_ref, xc_ref, r_ref, ig_ref, a_ref, f_ref, drec_ref, cw_ref, wa_ref, wx_ref,
             lam_ref, dxr_ref, dgr_ref, gwa_ref, gwx_ref, acc_ref, carry_s, dxc_halo_s, d_s, gwa_s, gwx_s):
        i = pl.program_id(0)
        blk = nsteps - 1 - i

        @pl.when(i == 0)
        def _():
            gwa_s[...] = jnp.zeros_like(gwa_s)
            gwx_s[...] = jnp.zeros_like(gwx_s)
            acc_ref[...] = jnp.zeros_like(acc_ref)
            carry_s[...] = jnp.zeros_like(carry_s)
            dxc_halo_s[...] = jnp.zeros_like(dxc_halo_s)

        x = xr_ref[...]
        xhalo = jnp.where(blk > 0, xp_ref[...], 0.0)
        cw = _w_rows(cw_ref)
        xs = [_shift_down(x, xhalo, 3), _shift_down(x, xhalo, 2), _shift_down(x, xhalo, 1), x]
        xc, r, ig, a, f = xc_ref[...], r_ref[...], ig_ref[...], a_ref[...], f_ref[...]
        sp = _softplus_neg(lam_ref[...])
        hcur = h_ref[...]
        hprev = _shift_down(hcur, jnp.where(blk > 0, hp_ref[...], 0.0), 1)
        gl, dgl = _gelu_and_grad(gr_ref[...])
        drec = drec_ref[...]
        dgr_ref[...] = (drec * hcur * dgl).astype(BF16)
        d_s[...] = drec * gl
        row8 = lax.broadcasted_iota(jnp.int32, (8, D_RNN), 0)

        def tile(t, c):
            o = pl.multiple_of((t8 - 1 - t) * 8, 8)
            a8 = a_ref[pl.ds(o, 8), :]
            dt = d_s[pl.ds(o, 8), :]
            at = jnp.where(row8 == 7, 1.0, pltpu.roll(a8, 7, 0))
            for s in (1, 2, 4):
                keep = row8 < 8 - s
                a_sh = jnp.where(keep, pltpu.roll(at, 8 - s, 0), 1.0)
                d_sh = jnp.where(keep, pltpu.roll(dt, 8 - s, 0), 0.0)
                dt = at * d_sh + dt
                at = at * a_sh
            lt = at * c + dt
            d_s[pl.ds(o, 8), :] = lt
            return _row_sum(jnp.where(row8 == 0, a8 * lt, 0.0))

        carry_s[0:1, :] = lax.fori_loop(0, t8, tile, carry_s[0:1, :], unroll=2)
        lmb = d_s[...]
        a2 = a * a
        dla = lmb * hprev * a - lmb * ig * xc * (a2 / f)
        di = lmb * f * xc
        dr = dla * (-LRU_C) * sp
        dpa = dr * r * (1.0 - r)
        dpx = di * ig * (1.0 - ig)
        dpab = dpa.astype(BF16)
        dpxb = dpx.astype(BF16)
        xcb = xc.astype(BF16)
        gwa_s[...] += _dot_tn(xcb, dpab)
        gwx_s[...] += _dot_tn(xcb, dpxb)

        @pl.when(i == nsteps - 1)
        def _():
            for dense, out in ((gwa_s[...], gwa_ref), (gwx_s[...], gwx_ref)):
                for b in range(RNN_BLOCKS):
                    rows = slice(b * HEAD_DIM, (b + 1) * HEAD_DIM)
                    out[rows, :] = dense[rows, b * HEAD_DIM:(b + 1) * HEAD_DIM]

        dxc = lmb * f * ig + _dot_nt(dpab, wa_ref[...]) + _dot_nt(dpxb, wx_ref[...])
        nxt = dxc_halo_s[...]
        dxr = cw[3] * dxc
        for s in (1, 2, 3):
            dxr = dxr + cw[3 - s] * _shift_up(dxc, nxt, s)
        dxr_ref[...] = dxr.astype(BF16)
        dxc_halo_s[...] = dxc[:8]
        dlam = _row_sum(dla * (-LRU_C) * r) * (-1.0 / (1.0 + jnp.exp(lam_ref[...])))
        _put_rows(acc_ref, [_row_sum(dxc * xs[0]), _row_sum(dxc * xs[1]), _row_sum(dxc * xs[2]), _row_sum(dxc * xs[3]),
                            _row_sum(dxc), _row_sum(dpa), _row_sum(dpx), dlam])

    rev = lambda i: (nsteps - 1 - i, 0)
    prev8 = lambda i: (jnp.maximum((nsteps - 1 - i) * t8 - 1, 0), 0)
    blkspec = pl.BlockSpec((tb, D_RNN), rev)
    halo8 = pl.BlockSpec((8, D_RNN), prev8)
    vec = _resident((1, D_RNN))
    return _pcall(
        body, (xr, xr, gr, h, h, *kept, drec, conv_w, wa, wx, lam), name="rnn_bwd", grid=(nsteps,),
        sem="arbitrary", comm=comm,
        in_specs=[blkspec, halo8, blkspec, blkspec, halo8] + [blkspec] * 6
        + [_resident((4, D_RNN)), _resident((D_RNN, D_RNN)), _resident((D_RNN, D_RNN)), vec],
        out_specs=[blkspec, blkspec, _acc((D_RNN, HEAD_DIM)), _acc((D_RNN, HEAD_DIM)), _acc((8, D_RNN))],
        out_shape=[jax.ShapeDtypeStruct((S, D_RNN), BF16), jax.ShapeDtypeStruct((S, D_RNN), BF16),
                   jax.ShapeDtypeStruct((D_RNN, HEAD_DIM), F32), jax.ShapeDtypeStruct((D_RNN, HEAD_DIM), F32),
                   jax.ShapeDtypeStruct((8, D_RNN), F32)],
        scratch_shapes=[pltpu.VMEM((8, D_RNN), F32), pltpu.VMEM((8, D_RNN), F32), pltpu.VMEM((tb, D_RNN), F32),
                        pltpu.VMEM((D_RNN, D_RNN), F32), pltpu.VMEM((D_RNN, D_RNN), F32)])


def _in_bwd(dq, dkc, dkp, dvc, dvp, dxr, dgr, dz1, w_in, comm=None):
    S = dz1.shape[0]
    tb = min(512, S)
    nsteps = S // tb
    nq = S // QBLK
    r = tb // QBLK

    def body(dq_ref, dkc_ref, dkp_ref, dkn_ref, dvc_ref, dvp_ref, dvn_ref, dxr_ref, dgr_ref, dz1_ref, w_ref,
             du_ref, dx_ref):
        i = pl.program_id(0)
        last = i == nsteps - 1

        def shifted(prev_ref, next_ref):
            nxt = jnp.where(last, 0.0, next_ref[...])
            return jnp.concatenate([prev_ref[QBLK:], nxt], axis=0) if r > 1 else nxt

        dk = (dkc_ref[...] + shifted(dkp_ref, dkn_ref)).astype(BF16)
        dv = (dvc_ref[...] + shifted(dvp_ref, dvn_ref)).astype(BF16)
        du = jnp.concatenate([dq_ref[...], dk, dv, dxr_ref[...], dgr_ref[...]], axis=1)
        du_ref[...] = du
        dx_ref[...] = ALPHA * dz1_ref[...] + _dot(du, w_ref[...])

    nextq = pl.BlockSpec((QBLK, D_KV), lambda i: (jnp.minimum((i + 1) * r, nq - 1), 0))
    return _pcall(
        body, (dq, dkc, dkp, dkp, dvc, dvp, dvp, dxr, dgr, dz1, w_in), name="in_bwd", grid=(nsteps,), comm=comm,
        in_specs=[_rows(tb, D_ATT), _rows(tb, D_KV), _rows(tb, D_KV), nextq, _rows(tb, D_KV), _rows(tb, D_KV), nextq,
                  _rows(tb, D_RNN), _rows(tb, D_RNN), _rows(tb, D_MODEL), _resident((D_IN, D_MODEL))],
        out_specs=[_rows(tb, D_IN), _rows(tb, D_MODEL)],
        out_shape=[jax.ShapeDtypeStruct((S, D_IN), BF16), jax.ShapeDtypeStruct((S, D_MODEL), F32)])


def _block_diag(w):
    eye = jnp.eye(RNN_BLOCKS, dtype=w.dtype)
    return (w[:, :, None, :] * eye[:, None, :, None]).reshape(D_RNN, D_RNN).astype(BF16)


def _adamw(w, g, m, v):
    m = ADAM_B1 * m + (1.0 - ADAM_B1) * g
    v = ADAM_B2 * v + (1.0 - ADAM_B2) * (g * g)
    m_hat = m / (1.0 - ADAM_B1 ** ADAM_STEP)
    v_hat = v / (1.0 - ADAM_B2 ** ADAM_STEP)
    delta = -ADAM_LR * (m_hat / (jnp.sqrt(v_hat) + ADAM_EPS) + ADAM_WD * w)
    return delta, m, v


def _sum_adamw(parts, w, m, v, name):
    R, C = w.shape
    rb = R if R <= 256 else 128
    assert R % rb == 0

    def body(p_ref, w_ref, m_ref, v_ref, g_out, d_out, m_out, v_out):
        g = p_ref[0].astype(F32)
        for d in range(1, N_DEV):
            g = g + p_ref[d].astype(F32)
        delta, mn, vn = _adamw(w_ref[...], g, m_ref[...], v_ref[...])
        g_out[...] = g
        d_out[...] = delta
        m_out[...] = mn
        v_out[...] = vn

    blk = _rows(rb, C)
    out = jax.ShapeDtypeStruct((R, C), F32)
    return pl.pallas_call(
        body, name=name, grid=(R // rb,),
        in_specs=[pl.BlockSpec((N_DEV, rb, C), lambda i: (0, i, 0)), blk, blk, blk],
        out_specs=[blk, blk, blk, blk], out_shape=[out, out, out, out],
        compiler_params=_params("parallel"),
    )(parts, w, m, v)


_SMALL = [("attn_sinks", "s", 0, 1, None), ("rnn_conv_w", "r", 0, 4, "cols"), ("rnn_conv_b", "r", 4, 1, None),
          ("gate_a_w", "a", 0, D_RNN, None), ("gate_a_b", "r", 5, 1, None), ("gate_x_w", "x", 0, D_RNN, None),
          ("gate_x_b", "r", 6, 1, None), ("lru_lambda", "r", 7, 1, None), ("ln1_g", "d", 0, 1, None),
          ("ln1_b", "d", 1, 1, None), ("ffn_conv_w", "f", 0, 3, "cols"), ("ffn_conv_b", "f", 3, 1, None),
          ("ple_gate_b", "t", 2, 1, None), ("ln2_g", "t", 0, 1, None), ("ln2_b", "t", 1, 1, None)]
_LOSS_ROW = 3


_ACC_COLS = {"t": (0, D_MODEL), "f": (D_MODEL, D_FF), "d": (D_MODEL + D_FF, D_MODEL), "s": (2 * D_MODEL + D_FF, 128),
             "r": (2 * D_MODEL + D_FF + 128, D_RNN)}
_ACC_WIDTH = 2 * D_MODEL + D_FF + 128 + D_RNN


def _small_update(rows_all, gates_all, params):
    flat = [arr for triple in params for arr in triple]
    n_par = len(_SMALL)

    def body(*refs):
        rows_ref, gates_ref = refs[:2]
        p_refs = refs[2:2 + 3 * n_par]
        loss_ref = refs[2 + 3 * n_par]
        o_refs = refs[3 + 3 * n_par:3 + 7 * n_par]
        rows_s, tmp_r, tmp_f = refs[3 + 7 * n_par:]
        me = _dev_index(*_place())
        rows_sum, gates_sum = rows_ref[0], gates_ref[0]
        for d in range(1, N_DEV):
            rows_sum = rows_sum + rows_ref[d]
            gates_sum = gates_sum + gates_ref[d]
        rows_s[...] = rows_sum
        t0 = _ACC_COLS["t"][0]
        loss_ref[...] = rows_s[_LOSS_ROW:_LOSS_ROW + 1, t0:t0 + 128]
        for i, (name, key, row, rows, how) in enumerate(_SMALL):
            w_ref, m_ref, v_ref = p_refs[3 * i:3 * i + 3]
            g_out, d_out, m_out, v_out = o_refs[4 * i:4 * i + 4]
            if key == "a":
                g = gates_sum[:, :HEAD_DIM]
            elif key == "x":
                g = gates_sum[:, HEAD_DIM:]
            elif how == "cols":
                c0, width = _ACC_COLS[key]
                full = rows_s[:, c0:c0 + width]
                shard = width // N_DEV
                mine = full[:, :shard]
                for d in range(1, N_DEV):
                    mine = jnp.where(me == d, full[:, d * shard:(d + 1) * shard], mine)
                tmp = tmp_r if key == "r" else tmp_f
                tmp[...] = mine
                g = tmp[row:row + rows, :]
            else:
                c0, width = _ACC_COLS[key]
                g = rows_s[row:row + rows, c0:c0 + width][:, :w_ref.shape[1]]
            delta, mn, vn = _adamw(w_ref[...], g, m_ref[...], v_ref[...])
            g_out[...] = g
            d_out[...] = delta
            m_out[...] = mn
            v_out[...] = vn

    outs = [jax.ShapeDtypeStruct((1, 128), F32)]
    for w, _, _ in params:
        outs += [jax.ShapeDtypeStruct(w.shape, F32)] * 4
    scratch = [pltpu.VMEM((8, _ACC_WIDTH), F32), pltpu.VMEM((8, D_RNN // N_DEV), F32), pltpu.VMEM((8, D_FF // N_DEV), F32)]
    res = pl.pallas_call(body, name="small_update", out_shape=outs, scratch_shapes=scratch)(rows_all, gates_all, *flat)
    return res[0], [res[1 + 4 * i:5 + 4 * i] for i in range(n_par)]


def kernel(x, p, w_in, attn_sinks, rnn_conv_w, rnn_conv_b, gate_a_w, gate_a_b, gate_x_w, gate_x_b, lru_lambda, w_out, ln1_g, ln1_b, w_ffn_up, ffn_conv_w, ffn_conv_b, w_ffn_down, ple_gate_w, ple_gate_b, ple_proj, ln2_g, ln2_b, loss_target, m_w_in, m_attn_sinks, m_rnn_conv_w, m_rnn_conv_b, m_gate_a_w, m_gate_a_b, m_gate_x_w, m_gate_x_b, m_lru_lambda, m_w_out, m_ln1_g, m_ln1_b, m_w_ffn_up, m_ffn_conv_w, m_ffn_conv_b, m_w_ffn_down, m_ple_gate_w, m_ple_gate_b, m_ple_proj, m_ln2_g, m_ln2_b, v_w_in, v_attn_sinks, v_rnn_conv_w, v_rnn_conv_b, v_gate_a_w, v_gate_a_b, v_gate_x_w, v_gate_x_b, v_lru_lambda, v_w_out, v_ln1_g, v_ln1_b, v_w_ffn_up, v_ffn_conv_w, v_ffn_conv_b, v_w_ffn_down, v_ple_gate_w, v_ple_gate_b, v_ple_proj, v_ln2_g, v_ln2_b):
    from_col_blocks = lambda g: g.transpose(1, 0, 2).reshape(g.shape[1], N_DEV * g.shape[2])

    xs, ps, tgt, sinks = x[0], p[0, 0], loss_target[0], attn_sinks[0]
    wa, wx = _block_diag(gate_a_w[0]), _block_diag(gate_x_w[0])

    conv_cols = jnp.concatenate([rnn_conv_w[0].reshape(1, -1), ffn_conv_w[0].reshape(1, -1)], axis=1)
    n_rc, n_fc = 4 * D_RNN // N_DEV, 3 * D_FF // N_DEV
    ((g_in,),) = _comm_call([_Gather([w_in[0].T.astype(BF16)])], "gather_w_in")
    w_in_full = g_in.reshape(D_IN, D_MODEL)

    (q, k, v, xr, gr), (g_conv,) = _in_proj(xs, w_in_full, comm=_Bcast([jnp.broadcast_to(conv_cols, (8, n_rc + n_fc))]))
    rcw = from_col_blocks(g_conv[:, 0, :n_rc].reshape(N_DEV, 4, D_RNN // N_DEV))
    fcw = from_col_blocks(g_conv[:, 0, n_rc:].reshape(N_DEV, 3, D_FF // N_DEV))
    (att,), (w_up,) = _attn_fwd(q, k, v, sinks, comm=_Gather([w_ffn_up[0].astype(BF16)]))
    (rec, h, *kept), (g_out, g_down) = _rnn_fwd(xr, gr, rcw, rnn_conv_b, wa, wx, gate_a_b, gate_x_b, lru_lambda,
                                         comm=_Gather([w_out[0].astype(BF16), w_ffn_down[0].astype(BF16)]))
    w_out_full = g_out.reshape(D_MODEL, D_MODEL)
    (z1, h1, h1b, gate, act, gl, vdgl), (g_pg, g_pp) = _mix_ln1_up(
        xs, att, rec, w_out_full, ln1_g, ln1_b, w_up, fcw, ffn_conv_b,
        comm=_Gather([ple_gate_w[0].astype(BF16), ple_proj[0].astype(BF16)]))
    dz2b, dpreb, dppb, dgc, dval, dh1p, acc_t = _tail(
        act, gl, vdgl, h1, h1b, ps, tgt, g_down.reshape(D_FF, D_MODEL), g_pg.reshape(D_MODEL, D_MODEL), ple_gate_b,
        from_col_blocks(g_pp), ln2_g, ln2_b)

    gd_down = _weight_grad([dz2b], [act], "down_grad", "rows_t")
    gd_pg = _weight_grad([h1b], [dpreb], "pg_grad", "rows", ts=1024)
    gd_pp = _weight_grad([ps], [dppb], "pp_grad", "cols", ts=1024)
    (dgate, dz1, dz1b, datt, drec, acc_f, acc_d), (r_down, r_pg, r_pp) = _up_bwd(
        dgc, gate, dval, dh1p, z1, w_up, fcw, w_out_full, ln1_g, comm=_Exchange([gd_down, gd_pg, gd_pp]))
    gd_up = _weight_grad([h1b], [dgate, dval], "up_grad", "cols")
    gd_out = _weight_grad([att, rec], [dz1b], "out_grad", "rows", ts=1024)
    (dq, dkc, dkp, dvc, dvp, acc_s), (r_up,) = _attn_bwd(q, k, v, datt, sinks, comm=_Exchange([gd_up]))
    (dxr, dgr, g_wa, g_wx, acc_r), (r_out,) = _rnn_bwd(xr, gr, h, kept, drec, rcw, wa, wx, lru_lambda,
                                                       comm=_Exchange([gd_out]))
    (du, dx), _ = _in_bwd(dq, dkc, dkp, dvc, dvp, dxr, dgr, dz1, w_in_full)
    acc_rows = jnp.concatenate([acc_t, acc_f, acc_d, acc_s, acc_r], axis=1)
    gd_in_a, small_parts = _weight_grad([du], [xs], "in_grad_a", "rows", ts=1024, b_window=(0, D_MODEL // 2),
                                        comm=_Bcast([acc_rows, jnp.concatenate([g_wa, g_wx], axis=1)]))
    gd_in_b, (r_in_a,) = _weight_grad([du], [xs], "in_grad_b", "rows", ts=1024, b_window=(1, D_MODEL // 2),
                                      comm=_Exchange([gd_in_a]))
    ((r_in_b,),) = _comm_call([_Exchange([gd_in_b])], "exchange_w_in")
    r_in = jnp.concatenate([r_in_a, r_in_b], axis=2)

    outs = {}
    res = _sum_adamw(r_in, w_in[0].T, m_w_in[0].T, v_w_in[0].T, "adamw_w_in")
    outs["w_in"] = [r.T[None] for r in res]
    for name, parts, w, m, v in [("w_out", r_out, w_out, m_w_out, v_w_out),
                                 ("w_ffn_up", r_up, w_ffn_up, m_w_ffn_up, v_w_ffn_up),
                                 ("w_ffn_down", r_down, w_ffn_down, m_w_ffn_down, v_w_ffn_down),
                                 ("ple_gate_w", r_pg, ple_gate_w, m_ple_gate_w, v_ple_gate_w),
                                 ("ple_proj", r_pp, ple_proj, m_ple_proj, v_ple_proj)]:
        res = _sum_adamw(parts, w[0], m[0], v[0], "adamw_" + name)
        outs[name] = [r[None] for r in res]

    given = dict(attn_sinks=(attn_sinks, m_attn_sinks, v_attn_sinks), rnn_conv_w=(rnn_conv_w, m_rnn_conv_w, v_rnn_conv_w),
                 rnn_conv_b=(rnn_conv_b, m_rnn_conv_b, v_rnn_conv_b), gate_a_w=(gate_a_w, m_gate_a_w, v_gate_a_w),
                 gate_a_b=(gate_a_b, m_gate_a_b, v_gate_a_b), gate_x_w=(gate_x_w, m_gate_x_w, v_gate_x_w),
                 gate_x_b=(gate_x_b, m_gate_x_b, v_gate_x_b), lru_lambda=(lru_lambda, m_lru_lambda, v_lru_lambda),
                 ln1_g=(ln1_g, m_ln1_g, v_ln1_g), ln1_b=(ln1_b, m_ln1_b, v_ln1_b),
                 ffn_conv_w=(ffn_conv_w, m_ffn_conv_w, v_ffn_conv_w), ffn_conv_b=(ffn_conv_b, m_ffn_conv_b, v_ffn_conv_b),
                 ple_gate_b=(ple_gate_b, m_ple_gate_b, v_ple_gate_b), ln2_g=(ln2_g, m_ln2_g, v_ln2_g),
                 ln2_b=(ln2_b, m_ln2_b, v_ln2_b))
    as_2d = lambda a: a.reshape(-1, a.shape[-1])
    loss_row, small_res = _small_update(*small_parts, [tuple(as_2d(a) for a in given[n]) for n, *_ in _SMALL])
    loss = loss_row[0, 0]
    for (n, *_), res in zip(_SMALL, small_res):
        outs[n] = [r.reshape(given[n][0].shape) for r in res]

    order = ["w_in", "attn_sinks", "rnn_conv_w", "rnn_conv_b", "gate_a_w", "gate_a_b", "gate_x_w", "gate_x_b",
             "lru_lambda", "w_out", "ln1_g", "ln1_b", "w_ffn_up", "ffn_conv_w", "ffn_conv_b", "w_ffn_down",
             "ple_gate_w", "ple_gate_b", "ple_proj", "ln2_g", "ln2_b"]
    return (loss, dx[None], *[outs[n][0] for n in order], *[outs[n][1] for n in order],
            *[outs[n][2] for n in order], *[outs[n][3] for n in order])
```

```python
import jax
import jax.numpy as jnp
from jax import lax
from jax.experimental import pallas as pl
from jax.experimental.pallas import tpu as pltpu

F32 = jnp.float32
BF16 = jnp.bfloat16

D_MODEL = 1024
D_ATT = 512
D_KV = 128
HEAD_DIM = 64
N_HEADS = 8
N_KV = 2
D_RNN = 512
RNN_BLOCKS = 8
D_IN = 1792
D_FF = 3072
PLE_DIM = 256
QBLK = 128
N_DEV = 8
ALPHA = float(2 ** 0.25)
LN_EPS = 1e-5
LRU_C = 8.0
ADAM_LR, ADAM_B1, ADAM_B2, ADAM_EPS, ADAM_WD, ADAM_STEP = 0.001, 0.9, 0.999, 1e-08, 0.01, 10

V7X_VMEM_LIMIT = 56 * 1024 * 1024
MESH = pl.DeviceIdType.MESH


def _params(*sem, vmem=V7X_VMEM_LIMIT):
    return pltpu.CompilerParams(dimension_semantics=sem or None, vmem_limit_bytes=vmem)


def _resident(shape):
    return pl.BlockSpec(shape, lambda *_: (0,) * len(shape), pipeline_mode=pl.Buffered(1))


def _rows(tb, cols):
    return pl.BlockSpec((tb, cols), lambda i: (i, 0))


def _acc(shape):
    return pl.BlockSpec(shape, lambda *_: (0,) * len(shape))


def _dot(a, b):
    return jnp.dot(a, b, preferred_element_type=F32)


def _dot_nt(a, b):
    return lax.dot_general(a, b, (((1,), (1,)), ((), ())), preferred_element_type=F32)


def _dot_tn(a, b):
    return lax.dot_general(a, b, (((0,), (0,)), ((), ())), preferred_element_type=F32)


def _sigmoid(x):
    return 1.0 / (1.0 + jnp.exp(-x))


_GELU_C = 0.7978845608028654
_GELU_K = 0.044715


def _gelu_and_grad(x):
    u = x * x
    t = jnp.tanh(x * (_GELU_C + (_GELU_C * _GELU_K) * u))
    hp = 0.5 + 0.5 * t
    dg = hp + x * (0.5 - 0.5 * (t * t)) * (_GELU_C + (3.0 * _GELU_C * _GELU_K) * u)
    return x * hp, dg


def _gelu(x):
    return 0.5 * x * (1.0 + jnp.tanh(_GELU_C * (x + _GELU_K * x * x * x)))


def _ln_stats(z):
    mu = jnp.mean(z, axis=-1, keepdims=True)
    zc = z - mu
    var = jnp.mean(zc * zc, axis=-1, keepdims=True)
    rstd = lax.rsqrt(var + LN_EPS)
    return zc * rstd, rstd


def _ln_bwd(dy, xhat, rstd, g):
    dxh = dy * g
    m1 = jnp.mean(dxh, axis=-1, keepdims=True)
    m2 = jnp.mean(dxh * xhat, axis=-1, keepdims=True)
    return rstd * (dxh - m1 - xhat * m2)


def _softplus_neg(lam):
    u = jnp.exp(-jnp.abs(lam))
    w = 1.0 + u
    d = w - 1.0
    log1p_u = jnp.where(d == 0.0, u, jnp.log(w) * (u / jnp.where(d == 0.0, 1.0, d)))
    return jnp.maximum(-lam, 0.0) + log1p_u


def _shift_down(x, halo, s):
    xs = pltpu.roll(x, s, 0)
    hs = pltpu.roll(halo, s, 0)
    row8 = lax.broadcasted_iota(jnp.int32, hs.shape, 0)
    first = jnp.where(row8 < s, hs, xs[:8])
    return jnp.concatenate([first, xs[8:]], axis=0)


def _shift_up(x, halo, s):
    n = x.shape[0]
    xs = pltpu.roll(x, n - s, 0)
    hs = pltpu.roll(halo, 8 - s, 0)
    row8 = lax.broadcasted_iota(jnp.int32, hs.shape, 0)
    last = jnp.where(row8 >= 8 - s, hs, xs[n - 8:])
    return jnp.concatenate([xs[:n - 8], last], axis=0)


def _row_sum(x):
    return jnp.sum(x, axis=0, keepdims=True)


def _put_rows(acc_ref, rows):
    row8 = lax.broadcasted_iota(jnp.int32, acc_ref.shape, 0)
    upd = jnp.zeros(acc_ref.shape, F32)
    for r, vec in enumerate(rows):
        upd = jnp.where(row8 == r, vec, upd)
    acc_ref[...] += upd


def _place():
    return lax.axis_index("x"), lax.axis_index("y"), lax.axis_index("c")


def _dev_index(px, py, pc):
    return 4 * px + 2 * py + pc


_ANY = pl.BlockSpec(memory_space=pl.ANY)


class _Gather:
    def __init__(self, arrays):
        self.arrays = list(arrays)
        self.n = len(self.arrays)

    def out_shape(self):
        return [jax.ShapeDtypeStruct((N_DEV,) + s.shape, s.dtype) for s in self.arrays]

    def scratch(self):
        return [pltpu.SemaphoreType.DMA((self.n, 7)), pltpu.SemaphoreType.DMA((self.n, 7)),
                pltpu.SemaphoreType.DMA((self.n,))]

    def _parts(self, ins, outs, sems):
        send_sems, recv_sems, local_sems = sems
        x, y, c = _place()
        me, sibling = (x, y, c), (x, y, 1 - c)
        chips = [(1 - x, y), (x, 1 - y), (1 - x, 1 - y)]

        def copy(a, k, block, to, src=None):
            rows = outs[a].at[_dev_index(*block)]
            return pltpu.make_async_remote_copy(
                src_ref=rows if src is None else src, dst_ref=rows, send_sem=send_sems.at[a, k],
                recv_sem=recv_sems.at[a, k], device_id=to, device_id_type=MESH)

        rng = range(self.n)
        mine = [pltpu.make_async_copy(ins[a], outs[a].at[_dev_index(*me)], local_sems.at[a]) for a in rng]
        first = [copy(a, 0, me, sibling, src=ins[a]) for a in rng]
        first += [copy(a, 1 + j, me, (*chip, c), src=ins[a]) for j, chip in enumerate(chips) for a in rng]
        landed = [copy(a, 1 + j, (*chip, c), me) for j, chip in enumerate(chips) for a in rng]
        passed = [copy(a, 4 + j, (*chip, c), sibling) for j, chip in enumerate(chips) for a in rng]
        from_sibling = [copy(a, 0, sibling, me) for a in rng]
        from_sibling += [copy(a, 4 + j, (*chip, 1 - c), me) for j, chip in enumerate(chips) for a in rng]
        return mine, first, landed, passed, from_sibling

    def start(self, ins, outs, sems):
        mine, first, _, _, _ = self._parts(ins, outs, sems)
        for cp in mine + first:
            cp.start()

    def forward(self, ins, outs, sems):
        _, _, landed, passed, _ = self._parts(ins, outs, sems)
        for got, fwd in zip(landed, passed):
            got.wait_recv()
            fwd.start()

    def finish(self, ins, outs, sems):
        mine, first, _, passed, from_sibling = self._parts(ins, outs, sems)
        for cp in from_sibling:
            cp.wait_recv()
        for cp in first + passed:
            cp.wait_send()
        for cp in mine:
            cp.wait()

    def before(self, ins, outs, sems, step, nsteps):
        pl.when(step == 0)(lambda: self.start(ins, outs, sems))
        pl.when(step == (7 * nsteps) // 8)(lambda: self.forward(ins, outs, sems))

    def after(self, ins, outs, sems, step, nsteps):
        pl.when(step == nsteps - 1)(lambda: self.finish(ins, outs, sems))


class _Exchange:
    def __init__(self, arrays):
        self.arrays = list(arrays)
        self.n = len(self.arrays)

    def out_shape(self):
        return [jax.ShapeDtypeStruct(b.shape, b.dtype) for b in self.arrays]

    def scratch(self):
        return [pltpu.SemaphoreType.DMA((self.n, 7)), pltpu.SemaphoreType.DMA((self.n, 7)),
                pltpu.SemaphoreType.DMA((self.n,))]

    def _parts(self, ins, outs, sems):
        send_sems, recv_sems, local_sems = sems
        x, y, c = _place()
        me = _dev_index(x, y, c)
        peers = [(x ^ (k >> 2), y ^ ((k >> 1) & 1), c ^ (k & 1)) for k in range(1, N_DEV)]
        rng = range(self.n)
        mine = [pltpu.make_async_copy(ins[a].at[me], outs[a].at[me], local_sems.at[a]) for a in rng]
        sent = [pltpu.make_async_remote_copy(
            src_ref=ins[a].at[_dev_index(*to)], dst_ref=outs[a].at[me], send_sem=send_sems.at[a, k],
            recv_sem=recv_sems.at[a, k], device_id=to, device_id_type=MESH) for k, to in enumerate(peers) for a in rng]
        arrivals = [pltpu.make_async_remote_copy(
            src_ref=ins[a].at[me], dst_ref=outs[a].at[_dev_index(*frm)], send_sem=send_sems.at[a, k],
            recv_sem=recv_sems.at[a, k], device_id=frm, device_id_type=MESH) for k, frm in enumerate(peers) for a in rng]
        return mine, sent, arrivals

    def start(self, ins, outs, sems):
        mine, sent, _ = self._parts(ins, outs, sems)
        for cp in mine + sent:
            cp.start()

    def finish(self, ins, outs, sems):
        mine, sent, arrivals = self._parts(ins, outs, sems)
        for cp in arrivals:
            cp.wait_recv()
        for cp in sent:
            cp.wait_send()
        for cp in mine:
            cp.wait()

    def before(self, ins, outs, sems, step, nsteps):
        pl.when(step == 0)(lambda: self.start(ins, outs, sems))

    def after(self, ins, outs, sems, step, nsteps):
        pl.when(step == nsteps - 1)(lambda: self.finish(ins, outs, sems))


class _Bcast(_Exchange):
    def out_shape(self):
        return [jax.ShapeDtypeStruct((N_DEV,) + s.shape, s.dtype) for s in self.arrays]

    def _parts(self, ins, outs, sems):
        send_sems, recv_sems, local_sems = sems
        x, y, c = _place()
        me = _dev_index(x, y, c)
        peers = [(x ^ (k >> 2), y ^ ((k >> 1) & 1), c ^ (k & 1)) for k in range(1, N_DEV)]
        rng = range(self.n)
        mine = [pltpu.make_async_copy(ins[a], outs[a].at[me], local_sems.at[a]) for a in rng]
        sent = [pltpu.make_async_remote_copy(
            src_ref=ins[a], dst_ref=outs[a].at[me], send_sem=send_sems.at[a, k], recv_sem=recv_sems.at[a, k],
            device_id=to, device_id_type=MESH) for k, to in enumerate(peers) for a in rng]
        arrivals = [pltpu.make_async_remote_copy(
            src_ref=ins[a], dst_ref=outs[a].at[_dev_index(*frm)], send_sem=send_sems.at[a, k],
            recv_sem=recv_sems.at[a, k], device_id=frm, device_id_type=MESH) for k, frm in enumerate(peers) for a in rng]
        return mine, sent, arrivals


def _comm_call(comms, name):
    ns = [c.n for c in comms]
    n = sum(ns)

    def body(*refs):
        parts, a, s = [], 0, 2 * n
        for c in comms:
            parts.append((c, refs[a:a + c.n], refs[n + a:n + a + c.n], refs[s:s + 3]))
            a, s = a + c.n, s + 3
        for c, ins, outs, sems in parts:
            c.start(ins, outs, sems)
        for c, ins, outs, sems in parts:
            if isinstance(c, _Gather):
                c.forward(ins, outs, sems)
        for c, ins, outs, sems in parts:
            c.finish(ins, outs, sems)

    res = pl.pallas_call(
        body, name=name, in_specs=[_ANY] * n, out_specs=[_ANY] * n,
        out_shape=[s for c in comms for s in c.out_shape()], scratch_shapes=[s for c in comms for s in c.scratch()],
    )(*[arr for c in comms for arr in c.arrays])
    out, a = [], 0
    for k in ns:
        out.append(res[a:a + k])
        a += k
    return out


def _pcall(body, args, *, name, grid, in_specs, out_specs, out_shape, scratch_shapes=(), sem="parallel", comm=None,
           step_axis=0):
    sem = (sem,) * len(grid) if isinstance(sem, str) else sem
    if comm is None:
        res = pl.pallas_call(body, name=name, grid=grid, in_specs=in_specs, out_specs=out_specs, out_shape=out_shape,
                             scratch_shapes=list(scratch_shapes), compiler_params=_params(*sem))(*args)
        return res, []
    n_in, n_out, n_scr, n = len(in_specs), len(out_specs), len(scratch_shapes), comm.n
    nsteps = grid[step_axis]
    assert all(g == 1 for ax, g in enumerate(grid) if ax != step_axis)

    def hosted(*refs):
        ins, cin = refs[:n_in], refs[n_in:n_in + n]
        o0 = n_in + n
        outs, cout = refs[o0:o0 + n_out], refs[o0 + n_out:o0 + n_out + n]
        s0 = o0 + n_out + n
        scr, sems = refs[s0:s0 + n_scr], refs[s0 + n_scr:]
        step = pl.program_id(step_axis)
        comm.before(cin, cout, sems, step, nsteps)
        body(*ins, *outs, *scr)
        comm.after(cin, cout, sems, step, nsteps)

    res = pl.pallas_call(
        hosted, name=name, grid=grid, in_specs=list(in_specs) + [_ANY] * n, out_specs=list(out_specs) + [_ANY] * n,
        out_shape=list(out_shape) + comm.out_shape(), scratch_shapes=list(scratch_shapes) + comm.scratch(),
        compiler_params=_params(*(("arbitrary",) * len(grid))))(*args, *comm.arrays)
    return res[:n_out], res[n_out:]


def _in_proj(x, w_in_t, comm=None):
    S = x.shape[0]
    tb = min(512, S)

    def body(x_ref, w_ref, q_ref, k_ref, v_ref, xr_ref, gr_ref):
        u = _dot_nt(x_ref[...].astype(BF16), w_ref[...])
        q_ref[...] = (u[:, :D_ATT] * (HEAD_DIM ** -0.5)).astype(BF16)
        k_ref[...] = u[:, D_ATT:D_ATT + D_KV].astype(BF16)
        v_ref[...] = u[:, D_ATT + D_KV:D_ATT + 2 * D_KV].astype(BF16)
        xr_ref[...] = u[:, D_ATT + 2 * D_KV:D_ATT + 2 * D_KV + D_RNN]
        gr_ref[...] = u[:, D_ATT + 2 * D_KV + D_RNN:]

    return _pcall(
        body, (x, w_in_t), name="in_proj", grid=(S // tb,), comm=comm,
        in_specs=[_rows(tb, D_MODEL), _resident((D_IN, D_MODEL))],
        out_specs=[_rows(tb, D_ATT), _rows(tb, D_KV), _rows(tb, D_KV), _rows(tb, D_RNN), _rows(tb, D_RNN)],
        out_shape=[jax.ShapeDtypeStruct((S, D_ATT), BF16), jax.ShapeDtypeStruct((S, D_KV), BF16),
                   jax.ShapeDtypeStruct((S, D_KV), BF16), jax.ShapeDtypeStruct((S, D_RNN), F32),
                   jax.ShapeDtypeStruct((S, D_RNN), F32)])


GROUP = N_HEADS // N_KV


def _band_mask(i):
    qi = lax.broadcasted_iota(jnp.int32, (GROUP * QBLK, 2 * QBLK), 0) & (QBLK - 1)
    sj = lax.broadcasted_iota(jnp.int32, (GROUP * QBLK, 2 * QBLK), 1)
    return (sj > qi) & (sj <= qi + QBLK) & ((sj >= QBLK) | (i > 0))


def _stack_heads(x, g):
    return jnp.concatenate([x[:, (g * GROUP + hh) * HEAD_DIM:(g * GROUP + hh + 1) * HEAD_DIM] for hh in range(GROUP)],
                           axis=0)


def _unstack_heads(x4):
    return [x4[hh * QBLK:(hh + 1) * QBLK] for hh in range(GROUP)]


def _sink_column(sink_ref, g):
    head = lax.broadcasted_iota(jnp.int32, (GROUP * QBLK, 1), 0) // QBLK
    col = jnp.full((GROUP * QBLK, 1), sink_ref[g * GROUP], F32)
    for hh in range(1, GROUP):
        col = jnp.where(head == hh, sink_ref[g * GROUP + hh], col)
    return col


ATT_STEP = 4


def _attn_specs(nq=1):
    cur = lambda i: (i, 0)
    prev = lambda i: (jnp.maximum(nq * i - 1, 0), 0)
    return [pl.BlockSpec((nq * QBLK, D_KV), cur), pl.BlockSpec((QBLK, D_KV), prev),
            pl.BlockSpec((nq * QBLK, D_KV), cur), pl.BlockSpec((QBLK, D_KV), prev)]


def _attn_fwd(q, k, v, sinks, comm=None):
    S = q.shape[0]
    nq = min(ATT_STEP, S // QBLK)

    def body(sink_ref, q_ref, kc_ref, kp_ref, vc_ref, vp_ref, o_ref, lse_ref):
        first = pl.program_id(0) * nq
        kall = jnp.concatenate([kp_ref[...], kc_ref[...]], axis=0)
        vall = jnp.concatenate([vp_ref[...], vc_ref[...]], axis=0)
        for b in range(nq):
            valid = _band_mask(first + b)
            rows = slice(b * QBLK, (b + 1) * QBLK)
            keys = slice(b * QBLK, (b + 2) * QBLK)
            qv = q_ref[rows, :]
            outs = []
            for g in range(N_KV):
                kcat = kall[keys, g * HEAD_DIM:(g + 1) * HEAD_DIM]
                vcat = vall[keys, g * HEAD_DIM:(g + 1) * HEAD_DIM]
                s = jnp.where(valid, _dot_nt(_stack_heads(qv, g), kcat), -1e30)
                sink = _sink_column(sink_ref, g)
                m = jnp.maximum(jnp.max(s, axis=1, keepdims=True), sink)
                p = jnp.exp(s - m)
                l = jnp.sum(p, axis=1, keepdims=True) + jnp.exp(sink - m)
                outs += _unstack_heads(_dot(p.astype(BF16), vcat) / l)
                lse_ref[(b * N_KV + g) * GROUP * QBLK:(b * N_KV + g + 1) * GROUP * QBLK, :] = m + jnp.log(l)
            o_ref[rows, :] = jnp.concatenate(outs, axis=1).astype(BF16)

    lse_rows = nq * N_HEADS * QBLK
    return _pcall(
        body, (sinks, q, k, k, v, v), name="attn_fwd", grid=(S // (nq * QBLK),), comm=comm,
        in_specs=[pl.BlockSpec(memory_space=pltpu.SMEM), _rows(nq * QBLK, D_ATT)] + _attn_specs(nq),
        out_specs=[_rows(nq * QBLK, D_ATT), _rows(lse_rows, 1)],
        out_shape=[jax.ShapeDtypeStruct((S, D_ATT), BF16), jax.ShapeDtypeStruct((S * N_HEADS, 1), F32)])


def _w_rows(w_ref):
    return [w_ref[k:k + 1, :] for k in range(w_ref.shape[0])]


def _conv4(x, halo, w, b):
    y = b + w[3] * x
    for s in (1, 2, 3):
        y = y + w[3 - s] * _shift_down(x, halo, s)
    return y


def _rnn_gates(xc, wa, wx, ba, bx, sp):
    xcb = xc.astype(BF16)
    r = _sigmoid(_dot(xcb, wa) + ba)
    ig = _sigmoid(_dot(xcb, wx) + bx)
    la = -LRU_C * r * sp
    a = jnp.exp(la)
    t = jnp.tanh(la)
    f = jnp.sqrt(-2.0 * t / (1.0 - t))
    return r, ig, a, f


def _rnn_fwd(xr, gr, conv_w, conv_b, wa, wx, ba, bx, lam, comm=None):
    S = xr.shape[0]
    tb = min(256, S)

    def body(xr_ref, gr_ref, cw_ref, cb_ref, wa_ref, wx_ref, ba_ref, bx_ref, lam_ref, rec_ref, h_ref,
             xc_ref, r_ref, ig_ref, a_ref, f_ref, halo_s, hc_s, a_s, b_s):
        @pl.when(pl.program_id(0) == 0)
        def _():
            halo_s[...] = jnp.zeros_like(halo_s)
            hc_s[...] = jnp.zeros_like(hc_s)

        x = xr_ref[...]
        xc = _conv4(x, halo_s[...], _w_rows(cw_ref), cb_ref[...])
        halo_s[...] = x[tb - 8:]
        r, ig, a, f = _rnn_gates(xc, wa_ref[...], wx_ref[...], ba_ref[...], bx_ref[...], _softplus_neg(lam_ref[...]))
        xc_ref[...] = xc
        r_ref[...] = r
        ig_ref[...] = ig
        a_ref[...] = a
        f_ref[...] = f
        a_s[...] = a
        b_s[...] = f * ig * xc
        row8 = lax.broadcasted_iota(jnp.int32, (8, D_RNN), 0)

        def tile(t, hc):
            o = pl.multiple_of(t * 8, 8)
            at = a_s[pl.ds(o, 8), :]
            bt = b_s[pl.ds(o, 8), :]
            for s in (1, 2, 4):
                keep = row8 >= s
                a_sh = jnp.where(keep, pltpu.roll(at, s, 0), 1.0)
                b_sh = jnp.where(keep, pltpu.roll(bt, s, 0), 0.0)
                bt = at * b_sh + bt
                at = at * a_sh
            ht = at * hc + bt
            b_s[pl.ds(o, 8), :] = ht
            return _row_sum(jnp.where(row8 == 7, ht, 0.0))

        hc_s[0:1, :] = lax.fori_loop(0, tb // 8, tile, hc_s[0:1, :], unroll=2)
        h = b_s[...]
        h_ref[...] = h
        rec_ref[...] = (h * _gelu(gr_ref[...])).astype(BF16)

    vec = _resident((1, D_RNN))
    kept = jax.ShapeDtypeStruct((S, D_RNN), F32)
    return _pcall(
        body, (xr, gr, conv_w, conv_b, wa, wx, ba, bx, lam), name="rnn_fwd", grid=(S // tb,), sem="arbitrary", comm=comm,
        in_specs=[_rows(tb, D_RNN), _rows(tb, D_RNN), _resident((4, D_RNN)), vec,
                  _resident((D_RNN, D_RNN)), _resident((D_RNN, D_RNN)), vec, vec, vec],
        out_specs=[_rows(tb, D_RNN)] * 7,
        out_shape=[jax.ShapeDtypeStruct((S, D_RNN), BF16), kept, kept, kept, kept, kept, kept],
        scratch_shapes=[pltpu.VMEM((8, D_RNN), F32), pltpu.VMEM((8, D_RNN), F32),
                        pltpu.VMEM((tb, D_RNN), F32), pltpu.VMEM((tb, D_RNN), F32)])


def _mix_ln1_up(x, att, rec, w_out, ln1_g, ln1_b, w_up, fcw, fcb, comm=None):
    S = x.shape[0]
    tb = min(256, S)
    nblk, _, wblk = w_up.shape
    half = nblk // 2

    def body(x_ref, att_ref, rec_ref, wo_ref, g_ref, b_ref, wu_ref, fcw_ref, fcb_ref,
             z1_ref, h1_ref, h1b_ref, gate_ref, act_ref, gl_ref, vdgl_ref, halo_s):
        @pl.when(pl.program_id(0) == 0)
        def _():
            halo_s[...] = jnp.zeros_like(halo_s)

        z1 = ALPHA * x_ref[...] + _dot(att_ref[...], wo_ref[:D_ATT, :]) + _dot(rec_ref[...], wo_ref[D_ATT:, :])
        z1_ref[...] = z1
        xhat, _ = _ln_stats(z1)
        h1 = xhat * g_ref[...] + b_ref[...]
        h1_ref[...] = h1
        h1b = h1.astype(BF16)
        h1b_ref[...] = h1b
        for jj in range(half):
            cols = slice(jj * wblk, (jj + 1) * wblk)
            gate = _dot(h1b, wu_ref[jj])
            val = _dot(h1b, wu_ref[jj + half])
            halo = halo_s[:, cols]
            conv = (fcb_ref[:, cols] + fcw_ref[2:3, cols] * gate + fcw_ref[1:2, cols] * _shift_down(gate, halo, 1)
                    + fcw_ref[0:1, cols] * _shift_down(gate, halo, 2))
            halo_s[:, cols] = gate[tb - 8:]
            gl, dgl = _gelu_and_grad(conv)
            gate_ref[:, cols] = gate.astype(BF16)
            act_ref[:, cols] = (gl * val).astype(BF16)
            gl_ref[:, cols] = gl.astype(BF16)
            vdgl_ref[:, cols] = (val * dgl).astype(BF16)

    vec = _resident((1, D_MODEL))
    wide = jax.ShapeDtypeStruct((S, D_FF), BF16)
    return _pcall(
        body, (x, att, rec, w_out, ln1_g, ln1_b, w_up, fcw, fcb), name="mix_ln1_up", grid=(S // tb,),
        sem="arbitrary", comm=comm,
        in_specs=[_rows(tb, D_MODEL), _rows(tb, D_ATT), _rows(tb, D_RNN), _resident((D_MODEL, D_MODEL)), vec, vec,
                  _resident(w_up.shape), _resident((3, D_FF)), _resident((1, D_FF))],
        out_specs=[_rows(tb, D_MODEL), _rows(tb, D_MODEL), _rows(tb, D_MODEL)] + [_rows(tb, D_FF)] * 4,
        out_shape=[jax.ShapeDtypeStruct((S, D_MODEL), F32), jax.ShapeDtypeStruct((S, D_MODEL), F32),
                   jax.ShapeDtypeStruct((S, D_MODEL), BF16), wide, wide, wide, wide],
        scratch_shapes=[pltpu.VMEM((8, D_FF), F32)])


def _tail(act, gl, vdgl, h1, h1b, p, tgt, w_down, w_pg, b_pg, w_pp, ln2_g, ln2_b):
    S = h1.shape[0]
    tb = min(256, S)

    def body(act_ref, gl_ref, vdgl_ref, h1_ref, h1b_ref, p_ref, t_ref, wd_ref, wpg_ref, bpg_ref, wpp_ref, g2_ref, b2_ref,
             dz2_ref, dpre_ref, dpp_ref, dgc_ref, dval_ref, dh1_ref, acc_ref):
        i = pl.program_id(0)

        @pl.when(i == 0)
        def _():
            acc_ref[...] = jnp.zeros_like(acc_ref)

        ffn = _dot(act_ref[...], wd_ref[...])
        h1 = h1_ref[...]
        sg = _sigmoid(_dot(h1b_ref[...], wpg_ref[...]) + bpg_ref[...])
        pp = _dot(p_ref[...].astype(BF16), wpp_ref[...])
        z2 = ALPHA * h1 + ffn + sg * pp
        xhat2, rstd2 = _ln_stats(z2)
        y = xhat2 * g2_ref[...] + b2_ref[...]
        err = y - t_ref[...]
        dy = err * (1.0 / D_MODEL)
        loss = 0.5 * jnp.sum(jnp.sum(err * err, axis=1, keepdims=True), axis=0, keepdims=True) * (1.0 / D_MODEL)
        dz2 = _ln_bwd(dy, xhat2, rstd2, g2_ref[...])
        dz2b = dz2.astype(BF16)
        dz2_ref[...] = dz2b
        dpre = dz2 * pp * sg * (1.0 - sg)
        dpreb = dpre.astype(BF16)
        dpre_ref[...] = dpreb
        dpp_ref[...] = (dz2 * sg).astype(BF16)
        dh1_ref[...] = ALPHA * dz2 + _dot_nt(dpreb, wpg_ref[...])
        dactb = _dot_nt(dz2b, wd_ref[...]).astype(BF16)
        dval_ref[...] = dactb * gl_ref[...]
        dgc_ref[...] = dactb * vdgl_ref[...]
        _put_rows(acc_ref, [_row_sum(dy * xhat2), _row_sum(dy), _row_sum(dpre),
                            jnp.broadcast_to(loss, (1, D_MODEL))])

    vec = _resident((1, D_MODEL))
    return pl.pallas_call(
        body, name="tail", grid=(S // tb,),
        in_specs=[_rows(tb, D_FF), _rows(tb, D_FF), _rows(tb, D_FF), _rows(tb, D_MODEL), _rows(tb, D_MODEL),
                  _rows(tb, PLE_DIM), _rows(tb, D_MODEL), _resident((D_FF, D_MODEL)), _resident((D_MODEL, D_MODEL)), vec,
                  _resident((PLE_DIM, D_MODEL)), vec, vec],
        out_specs=[_rows(tb, D_MODEL), _rows(tb, D_MODEL), _rows(tb, D_MODEL), _rows(tb, D_FF),
                   _rows(tb, D_FF), _rows(tb, D_MODEL), _acc((8, D_MODEL))],
        out_shape=[jax.ShapeDtypeStruct((S, D_MODEL), BF16),
                   jax.ShapeDtypeStruct((S, D_MODEL), BF16), jax.ShapeDtypeStruct((S, D_MODEL), BF16),
                   jax.ShapeDtypeStruct((S, D_FF), BF16), jax.ShapeDtypeStruct((S, D_FF), BF16),
                   jax.ShapeDtypeStruct((S, D_MODEL), F32), jax.ShapeDtypeStruct((8, D_MODEL), F32)],
        compiler_params=_params("arbitrary"),
    )(act, gl, vdgl, h1, h1b, p, tgt, w_down, w_pg, b_pg, w_pp, ln2_g, ln2_b)


def _weight_grad(a_list, b_list, name, layout, ts=512, comm=None, b_window=None):
    S = a_list[0].shape[0]
    ms = [a.shape[1] for a in a_list]
    M, nb = sum(ms), len(b_list)
    win, Nb = b_window if b_window else (0, b_list[0].shape[1])
    ts = min(ts, S)
    nk = S // ts
    per_b = N_DEV // nb
    na = len(a_list)

    def body(*refs):
        a_refs, b_refs, o_ref, acc_ref = refs[:na], refs[na:na + nb], refs[na + nb], refs[na + nb + 1]
        j, k = pl.program_id(0), pl.program_id(1)

        @pl.when(k == 0)
        def _():
            acc_ref[...] = jnp.zeros_like(acc_ref)

        for jj in range(nb):
            @pl.when(j == jj)
            def _():
                b = b_refs[jj][...].astype(BF16)
                off = 0
                for a_ref, m in zip(a_refs, ms):
                    acc_ref[off:off + m, :] += _dot_tn(a_ref[...].astype(BF16), b)
                    off += m

        @pl.when(k == nk - 1)
        def _():
            for d in range(per_b):
                if layout == "rows":
                    o_ref[d] = acc_ref[d * (M // N_DEV):(d + 1) * (M // N_DEV), :].astype(BF16)
                elif layout == "cols":
                    o_ref[d] = acc_ref[:, d * (Nb // per_b):(d + 1) * (Nb // per_b)].astype(BF16)
                else:
                    o_ref[d] = acc_ref[:, d * (Nb // per_b):(d + 1) * (Nb // per_b)].T.astype(BF16)

    def b_index(jj):
        return lambda j, k: (jnp.where(j == jj, k, jnp.where(j < jj, 0, nk - 1)), win)

    if layout == "rows":
        assert nb == 1
        blk = (N_DEV, M // N_DEV, Nb)
    elif layout == "cols":
        blk = (per_b, M, Nb // per_b)
    else:
        blk = (per_b, Nb // per_b, M)
    (res,), comm_res = _pcall(
        body, (*a_list, *b_list), name=name, grid=(nb, nk), sem="arbitrary", comm=comm, step_axis=1,
        in_specs=[pl.BlockSpec((ts, m), lambda j, k: (k, 0)) for m in ms]
        + [pl.BlockSpec((ts, Nb), b_index(jj)) for jj in range(nb)],
        out_specs=[pl.BlockSpec(blk, lambda j, k: (j, 0, 0))],
        out_shape=[jax.ShapeDtypeStruct((N_DEV,) + blk[1:], BF16)],
        scratch_shapes=[pltpu.VMEM((M, Nb), F32)])
    return (res, comm_res) if comm is not None else res


def _up_bwd(dgc, gate, dval, dh1p, z1, w_up, fcw, w_out, ln1_g, comm=None):
    S = z1.shape[0]
    tb = min(256, S)
    t16 = tb // 16
    n16 = S // 16
    nblk, _, wblk = w_up.shape
    half = nblk // 2
    nsteps = S // tb

    def body(dgc_ref, dgn_ref, gc_ref, gp_ref, dval_ref, dh1p_ref, z1_ref, wu_ref, fcw_ref, wo_ref, g1_ref,
             dgate_ref, dz1_ref, dz1b_ref, datt_ref, drec_ref, accf_ref, accd_ref):
        i = pl.program_id(0)

        @pl.when(i == 0)
        def _():
            accf_ref[...] = jnp.zeros_like(accf_ref)
            accd_ref[...] = jnp.zeros_like(accd_ref)

        dg = dgc_ref[...].astype(F32)
        nxt = jnp.where(i < nsteps - 1, dgn_ref[...].astype(F32)[0:8], 0.0)
        w = _w_rows(fcw_ref)
        dgate = (w[2] * dg + w[1] * _shift_up(dg, nxt, 1) + w[0] * _shift_up(dg, nxt, 2)).astype(BF16)
        dgate_ref[...] = dgate
        gate = gc_ref[...].astype(F32)
        halo = jnp.where(i > 0, gp_ref[...].astype(F32)[8:16], 0.0)
        _put_rows(accf_ref, [_row_sum(dg * _shift_down(gate, halo, 2)), _row_sum(dg * _shift_down(gate, halo, 1)),
                             _row_sum(dg * gate), _row_sum(dg)])

        dh1 = dh1p_ref[...]
        for j in range(nblk):
            src = dgate if j < half else dval_ref[...]
            jj = j % half
            dh1 = dh1 + _dot_nt(src[:, jj * wblk:(jj + 1) * wblk], wu_ref[j])
        xhat1, rstd1 = _ln_stats(z1_ref[...])
        dz1 = _ln_bwd(dh1, xhat1, rstd1, g1_ref[...])
        dz1_ref[...] = dz1
        dz1b = dz1.astype(BF16)
        dz1b_ref[...] = dz1b
        dcat = _dot_nt(dz1b, wo_ref[...])
        datt_ref[...] = dcat[:, :D_ATT].astype(BF16)
        drec_ref[...] = dcat[:, D_ATT:]
        _put_rows(accd_ref, [_row_sum(dh1 * xhat1), _row_sum(dh1)])

    prev16 = pl.BlockSpec((16, D_FF), lambda i: (jnp.maximum(i * t16 - 1, 0), 0))
    next16 = pl.BlockSpec((16, D_FF), lambda i: (jnp.minimum((i + 1) * t16, n16 - 1), 0))
    return _pcall(
        body, (dgc, dgc, gate, gate, dval, dh1p, z1, w_up, fcw, w_out, ln1_g), name="up_bwd",
        grid=(nsteps,), sem="arbitrary", comm=comm,
        in_specs=[_rows(tb, D_FF), next16, _rows(tb, D_FF), prev16, _rows(tb, D_FF), _rows(tb, D_MODEL),
                  _rows(tb, D_MODEL), _resident(w_up.shape), _resident((3, D_FF)),
                  _resident((D_MODEL, D_MODEL)), _resident((1, D_MODEL))],
        out_specs=[_rows(tb, D_FF), _rows(tb, D_MODEL), _rows(tb, D_MODEL), _rows(tb, D_ATT), _rows(tb, D_RNN),
                   _acc((8, D_FF)), _acc((8, D_MODEL))],
        out_shape=[jax.ShapeDtypeStruct((S, D_FF), BF16), jax.ShapeDtypeStruct((S, D_MODEL), F32),
                   jax.ShapeDtypeStruct((S, D_MODEL), BF16), jax.ShapeDtypeStruct((S, D_ATT), BF16),
                   jax.ShapeDtypeStruct((S, D_RNN), F32), jax.ShapeDtypeStruct((8, D_FF), F32),
                   jax.ShapeDtypeStruct((8, D_MODEL), F32)])


def _attn_bwd(q, k, v, lse, do, sinks, comm=None):
    S = q.shape[0]
    grp = N_HEADS // N_KV
    nq = min(ATT_STEP, S // QBLK)

    def body(sink_ref, q_ref, kc_ref, kp_ref, vc_ref, vp_ref, do_ref, lse_ref, dq_ref, dkc_ref, dkp_ref, dvc_ref, dvp_ref,
             ds_ref):
        i = pl.program_id(0)

        @pl.when(i == 0)
        def _():
            ds_ref[...] = jnp.zeros_like(ds_ref)

        row8 = lax.broadcasted_iota(jnp.int32, (8, 128), 0)
        lane8 = lax.broadcasted_iota(jnp.int32, (8, 128), 1)
        dsink = jnp.zeros((8, 128), F32)
        kall = jnp.concatenate([kp_ref[...], kc_ref[...]], axis=0)
        vall = jnp.concatenate([vp_ref[...], vc_ref[...]], axis=0)
        dk_t = [jnp.zeros((D_KV, QBLK), F32) for _ in range(nq + 1)]
        dv_t = [jnp.zeros((D_KV, QBLK), F32) for _ in range(nq + 1)]
        for b in range(nq):
            valid = _band_mask(i * nq + b)
            rows = slice(b * QBLK, (b + 1) * QBLK)
            keys = slice(b * QBLK, (b + 2) * QBLK)
            qv, dov = q_ref[rows, :], do_ref[rows, :]
            dqs, dks, dvs = [], [], []
            for g in range(N_KV):
                kcat = kall[keys, g * HEAD_DIM:(g + 1) * HEAD_DIM]
                vcat = vall[keys, g * HEAD_DIM:(g + 1) * HEAD_DIM]
                q4, do4 = _stack_heads(qv, g), _stack_heads(dov, g)
                s = jnp.where(valid, _dot_nt(q4, kcat), -1e30)
                lse = lse_ref[(b * N_KV + g) * GROUP * QBLK:(b * N_KV + g + 1) * GROUP * QBLK, :]
                p = jnp.exp(s - lse)
                p_sink = jnp.exp(_sink_column(sink_ref, g) - lse)
                dp = _dot_nt(do4, vcat)
                delta = jnp.sum(p * dp, axis=1, keepdims=True)
                dsc = (p * (dp - delta)).astype(BF16)
                dqs += _unstack_heads(_dot(dsc, kcat) * (HEAD_DIM ** -0.5))
                dks.append(_dot_tn(q4, dsc))
                dvs.append(_dot_tn(do4, p.astype(BF16)))
                for hh, part in enumerate(_unstack_heads(-p_sink * delta)):
                    here = (row8 == 0) & (lane8 == g * grp + hh)
                    dsink = dsink + jnp.where(here, jnp.sum(part, axis=0, keepdims=True), 0.0)
            dq_ref[rows, :] = jnp.concatenate(dqs, axis=1).astype(BF16)
            dk2, dv2 = jnp.concatenate(dks, axis=0), jnp.concatenate(dvs, axis=0)
            dk_t[b], dk_t[b + 1] = dk_t[b] + dk2[:, :QBLK], dk_t[b + 1] + dk2[:, QBLK:]
            dv_t[b], dv_t[b + 1] = dv_t[b] + dv2[:, :QBLK], dv_t[b + 1] + dv2[:, QBLK:]
        dkp_ref[...] = dk_t[0].T
        dvp_ref[...] = dv_t[0].T
        for b in range(nq):
            dkc_ref[b * QBLK:(b + 1) * QBLK, :] = dk_t[b + 1].T
            dvc_ref[b * QBLK:(b + 1) * QBLK, :] = dv_t[b + 1].T
        ds_ref[...] += dsink

    nsteps = S // (nq * QBLK)
    cur = jax.ShapeDtypeStruct((S, D_KV), F32)
    prev = jax.ShapeDtypeStruct((nsteps * QBLK, D_KV), F32)
    big = _rows(nq * QBLK, D_ATT)
    return _pcall(
        body, (sinks, q, k, k, v, v, do, lse), name="attn_bwd", grid=(nsteps,), sem="arbitrary", comm=comm,
        in_specs=[pl.BlockSpec(memory_space=pltpu.SMEM), big] + _attn_specs(nq) + [big, _rows(nq * N_HEADS * QBLK, 1)],
        out_specs=[big, _rows(nq * QBLK, D_KV), _rows(QBLK, D_KV), _rows(nq * QBLK, D_KV), _rows(QBLK, D_KV),
                   _acc((8, 128))],
        out_shape=[jax.ShapeDtypeStruct((S, D_ATT), BF16), cur, prev, cur, prev, jax.ShapeDtypeStruct((8, 128), F32)])


def _rnn_bwd(xr, gr, h, kept, drec, conv_w, wa, wx, lam, comm=None):
    S = xr.shape[0]
    tb = min(256, S)
    t8 = tb // 8
    nsteps = S // tb

    def body(xr_ref, xp_ref, gr_ref, h_ref, hp_ref, xc_ref, r_ref, ig_ref, a_ref, f_ref, drec_ref, cw_ref, wa_ref, wx_ref,
             lam_ref, dxr_ref, dgr_ref, gwa_ref, gwx_ref, acc_ref, carry_s, dxc_halo_s, d_s, gwa_s, gwx_s):
        i = pl.program_id(0)
        blk = nsteps - 1 - i

        @pl.when(i == 0)
        def _():
            gwa_s[...] = jnp.zeros_like(gwa_s)
            gwx_s[...] = jnp.zeros_like(gwx_s)
            acc_ref[...] = jnp.zeros_like(acc_ref)
            carry_s[...] = jnp.zeros_like(carry_s)
            dxc_halo_s[...] = jnp.zeros_like(dxc_halo_s)

        x = xr_ref[...]
        xhalo = jnp.where(blk > 0, xp_ref[...], 0.0)
        cw = _w_rows(cw_ref)
        xs = [_shift_down(x, xhalo, 3), _shift_down(x, xhalo, 2), _shift_down(x, xhalo, 1), x]
        xc, r, ig, a, f = xc_ref[...], r_ref[...], ig_ref[...], a_ref[...], f_ref[...]
        sp = _softplus_neg(lam_ref[...])
        hcur = h_ref[...]
        hprev = _shift_down(hcur, jnp.where(blk > 0, hp_ref[...], 0.0), 1)
        gl, dgl = _gelu_and_grad(gr_ref[...])
        drec = drec_ref[...]
        dgr_ref[...] = (drec * hcur * dgl).astype(BF16)
        d_s[...] = drec * gl
        row8 = lax.broadcasted_iota(jnp.int32, (8, D_RNN), 0)

        def tile(t, c):
            o = pl.multiple_of((t8 - 1 - t) * 8, 8)
            a8 = a_ref[pl.ds(o, 8), :]
            dt = d_s[pl.ds(o, 8), :]
            at = jnp.where(row8 == 7, 1.0, pltpu.roll(a8, 7, 0))
            for s in (1, 2, 4):
                keep = row8 < 8 - s
                a_sh = jnp.where(keep, pltpu.roll(at, 8 - s, 0), 1.0)
                d_sh = jnp.where(keep, pltpu.roll(dt, 8 - s, 0), 0.0)
                dt = at * d_sh + dt
                at = at * a_sh
            lt = at * c + dt
            d_s[pl.ds(o, 8), :] = lt
            return _row_sum(jnp.where(row8 == 0, a8 * lt, 0.0))

        carry_s[0:1, :] = lax.fori_loop(0, t8, tile, carry_s[0:1, :], unroll=2)
        lmb = d_s[...]
        a2 = a * a
        dla = lmb * hprev * a - lmb * ig * xc * (a2 / f)
        di = lmb * f * xc
        dr = dla * (-LRU_C) * sp
        dpa = dr * r * (1.0 - r)
        dpx = di * ig * (1.0 - ig)
        dpab = dpa.astype(BF16)
        dpxb = dpx.astype(BF16)
        xcb = xc.astype(BF16)
        gwa_s[...] += _dot_tn(xcb, dpab)
        gwx_s[...] += _dot_tn(xcb, dpxb)

        @pl.when(i == nsteps - 1)
        def _():
            for dense, out in ((gwa_s[...], gwa_ref), (gwx_s[...], gwx_ref)):
                for b in range(RNN_BLOCKS):
                    rows = slice(b * HEAD_DIM, (b + 1) * HEAD_DIM)
                    out[rows, :] = dense[rows, b * HEAD_DIM:(b + 1) * HEAD_DIM]

        dxc = lmb * f * ig + _dot_nt(dpab, wa_ref[...]) + _dot_nt(dpxb, wx_ref[...])
        nxt = dxc_halo_s[...]
        dxr = cw[3] * dxc
        for s in (1, 2, 3):
            dxr = dxr + cw[3 - s] * _shift_up(dxc, nxt, s)
        dxr_ref[...] = dxr.astype(BF16)
        dxc_halo_s[...] = dxc[:8]
        dlam = _row_sum(dla * (-LRU_C) * r) * (-1.0 / (1.0 + jnp.exp(lam_ref[...])))
        _put_rows(acc_ref, [_row_sum(dxc * xs[0]), _row_sum(dxc * xs[1]), _row_sum(dxc * xs[2]), _row_sum(dxc * xs[3]),
                            _row_sum(dxc), _row_sum(dpa), _row_sum(dpx), dlam])

    rev = lambda i: (nsteps - 1 - i, 0)
    prev8 = lambda i: (jnp.maximum((nsteps - 1 - i) * t8 - 1, 0), 0)
    blkspec = pl.BlockSpec((tb, D_RNN), rev)
    halo8 = pl.BlockSpec((8, D_RNN), prev8)
    vec = _resident((1, D_RNN))
    return _pcall(
        body, (xr, xr, gr, h, h, *kept, drec, conv_w, wa, wx, lam), name="rnn_bwd", grid=(nsteps,),
        sem="arbitrary", comm=comm,
        in_specs=[blkspec, halo8, blkspec, blkspec, halo8] + [blkspec] * 6
        + [_resident((4, D_RNN)), _resident((D_RNN, D_RNN)), _resident((D_RNN, D_RNN)), vec],
        out_specs=[blkspec, blkspec, _acc((D_RNN, HEAD_DIM)), _acc((D_RNN, HEAD_DIM)), _acc((8, D_RNN))],
        out_shape=[jax.ShapeDtypeStruct((S, D_RNN), BF16), jax.ShapeDtypeStruct((S, D_RNN), BF16),
                   jax.ShapeDtypeStruct((D_RNN, HEAD_DIM), F32), jax.ShapeDtypeStruct((D_RNN, HEAD_DIM), F32),
                   jax.ShapeDtypeStruct((8, D_RNN), F32)],
        scratch_shapes=[pltpu.VMEM((8, D_RNN), F32), pltpu.VMEM((8, D_RNN), F32), pltpu.VMEM((tb, D_RNN), F32),
                        pltpu.VMEM((D_RNN, D_RNN), F32), pltpu.VMEM((D_RNN, D_RNN), F32)])


def _in_bwd(dq, dkc, dkp, dvc, dvp, dxr, dgr, dz1, w_in, comm=None):
    S = dz1.shape[0]
    tb = min(ATT_STEP * QBLK, S)
    nsteps = S // tb

    def body(dq_ref, dkc_ref, dkn_ref, dvc_ref, dvn_ref, dxr_ref, dgr_ref, dz1_ref, w_ref, du_ref, dx_ref):
        last = pl.program_id(0) == nsteps - 1

        def total(cur_ref, next_ref):
            nxt = jnp.where(last, 0.0, next_ref[...])
            tail = cur_ref[tb - QBLK:, :] + nxt
            return jnp.concatenate([cur_ref[:tb - QBLK, :], tail], axis=0) if tb > QBLK else tail

        dk = total(dkc_ref, dkn_ref).astype(BF16)
        dv = total(dvc_ref, dvn_ref).astype(BF16)
        du = jnp.concatenate([dq_ref[...], dk, dv, dxr_ref[...], dgr_ref[...]], axis=1)
        du_ref[...] = du
        dx_ref[...] = ALPHA * dz1_ref[...] + _dot(du, w_ref[...])

    nextp = pl.BlockSpec((QBLK, D_KV), lambda i: (jnp.minimum(i + 1, nsteps - 1), 0))
    return _pcall(
        body, (dq, dkc, dkp, dvc, dvp, dxr, dgr, dz1, w_in), name="in_bwd", grid=(nsteps,), comm=comm,
        in_specs=[_rows(tb, D_ATT), _rows(tb, D_KV), nextp, _rows(tb, D_KV), nextp,
                  _rows(tb, D_RNN), _rows(tb, D_RNN), _rows(tb, D_MODEL), _resident((D_IN, D_MODEL))],
        out_specs=[_rows(tb, D_IN), _rows(tb, D_MODEL)],
        out_shape=[jax.ShapeDtypeStruct((S, D_IN), BF16), jax.ShapeDtypeStruct((S, D_MODEL), F32)])


def _block_diag(w):
    eye = jnp.eye(RNN_BLOCKS, dtype=w.dtype)
    return (w[:, :, None, :] * eye[:, None, :, None]).reshape(D_RNN, D_RNN).astype(BF16)


def _adamw(w, g, m, v):
    m = ADAM_B1 * m + (1.0 - ADAM_B1) * g
    v = ADAM_B2 * v + (1.0 - ADAM_B2) * (g * g)
    m_hat = m / (1.0 - ADAM_B1 ** ADAM_STEP)
    v_hat = v / (1.0 - ADAM_B2 ** ADAM_STEP)
    delta = -ADAM_LR * (m_hat / (jnp.sqrt(v_hat) + ADAM_EPS) + ADAM_WD * w)
    return delta, m, v


def _sum_adamw(parts, w, m, v, name):
    R, C = w.shape
    rb = R if R <= 256 else 128
    assert R % rb == 0

    def body(p_ref, w_ref, m_ref, v_ref, g_out, d_out, m_out, v_out):
        g = p_ref[0].astype(F32)
        for d in range(1, N_DEV):
            g = g + p_ref[d].astype(F32)
        delta, mn, vn = _adamw(w_ref[...], g, m_ref[...], v_ref[...])
        g_out[...] = g
        d_out[...] = delta
        m_out[...] = mn
        v_out[...] = vn

    blk = _rows(rb, C)
    out = jax.ShapeDtypeStruct((R, C), F32)
    return pl.pallas_call(
        body, name=name, grid=(R // rb,),
        in_specs=[pl.BlockSpec((N_DEV, rb, C), lambda i: (0, i, 0)), blk, blk, blk],
        out_specs=[blk, blk, blk, blk], out_shape=[out, out, out, out],
        compiler_params=_params("parallel"),
    )(parts, w, m, v)


_SMALL = [("attn_sinks", "s", 0, 1, None), ("rnn_conv_w", "r", 0, 4, "cols"), ("rnn_conv_b", "r", 4, 1, None),
          ("gate_a_w", "a", 0, D_RNN, None), ("gate_a_b", "r", 5, 1, None), ("gate_x_w", "x", 0, D_RNN, None),
          ("gate_x_b", "r", 6, 1, None), ("lru_lambda", "r", 7, 1, None), ("ln1_g", "d", 0, 1, None),
          ("ln1_b", "d", 1, 1, None), ("ffn_conv_w", "f", 0, 3, "cols"), ("ffn_conv_b", "f", 3, 1, None),
          ("ple_gate_b", "t", 2, 1, None), ("ln2_g", "t", 0, 1, None), ("ln2_b", "t", 1, 1, None)]
_LOSS_ROW = 3


_ACC_COLS = {"t": (0, D_MODEL), "f": (D_MODEL, D_FF), "d": (D_MODEL + D_FF, D_MODEL), "s": (2 * D_MODEL + D_FF, 128),
             "r": (2 * D_MODEL + D_FF + 128, D_RNN)}
_ACC_WIDTH = 2 * D_MODEL + D_FF + 128 + D_RNN


def _small_update(rows_all, gates_all, params):
    flat = [arr for triple in params for arr in triple]
    n_par = len(_SMALL)

    def body(*refs):
        rows_ref, gates_ref = refs[:2]
        p_refs = refs[2:2 + 3 * n_par]
        loss_ref = refs[2 + 3 * n_par]
        o_refs = refs[3 + 3 * n_par:3 + 7 * n_par]
        rows_s, tmp_r, tmp_f = refs[3 + 7 * n_par:]
        me = _dev_index(*_place())
        rows_sum, gates_sum = rows_ref[0], gates_ref[0]
        for d in range(1, N_DEV):
            rows_sum = rows_sum + rows_ref[d]
            gates_sum = gates_sum + gates_ref[d]
        rows_s[...] = rows_sum
        t0 = _ACC_COLS["t"][0]
        loss_ref[...] = rows_s[_LOSS_ROW:_LOSS_ROW + 1, t0:t0 + 128]
        for i, (name, key, row, rows, how) in enumerate(_SMALL):
            w_ref, m_ref, v_ref = p_refs[3 * i:3 * i + 3]
            g_out, d_out, m_out, v_out = o_refs[4 * i:4 * i + 4]
            if key == "a":
                g = gates_sum[:, :HEAD_DIM]
            elif key == "x":
                g = gates_sum[:, HEAD_DIM:]
            elif how == "cols":
                c0, width = _ACC_COLS[key]
                full = rows_s[:, c0:c0 + width]
                shard = width // N_DEV
                mine = full[:, :shard]
                for d in range(1, N_DEV):
                    mine = jnp.where(me == d, full[:, d * shard:(d + 1) * shard], mine)
                tmp = tmp_r if key == "r" else tmp_f
                tmp[...] = mine
                g = tmp[row:row + rows, :]
            else:
                c0, width = _ACC_COLS[key]
                g = rows_s[row:row + rows, c0:c0 + width][:, :w_ref.shape[1]]
            delta, mn, vn = _adamw(w_ref[...], g, m_ref[...], v_ref[...])
            g_out[...] = g
            d_out[...] = delta
            m_out[...] = mn
            v_out[...] = vn

    outs = [jax.ShapeDtypeStruct((1, 128), F32)]
    for w, _, _ in params:
        outs += [jax.ShapeDtypeStruct(w.shape, F32)] * 4
    scratch = [pltpu.VMEM((8, _ACC_WIDTH), F32), pltpu.VMEM((8, D_RNN // N_DEV), F32), pltpu.VMEM((8, D_FF // N_DEV), F32)]
    res = pl.pallas_call(body, name="small_update", out_shape=outs, scratch_shapes=scratch)(rows_all, gates_all, *flat)
    return res[0], [res[1 + 4 * i:5 + 4 * i] for i in range(n_par)]


def kernel(x, p, w_in, attn_sinks, rnn_conv_w, rnn_conv_b, gate_a_w, gate_a_b, gate_x_w, gate_x_b, lru_lambda, w_out, ln1_g, ln1_b, w_ffn_up, ffn_conv_w, ffn_conv_b, w_ffn_down, ple_gate_w, ple_gate_b, ple_proj, ln2_g, ln2_b, loss_target, m_w_in, m_attn_sinks, m_rnn_conv_w, m_rnn_conv_b, m_gate_a_w, m_gate_a_b, m_gate_x_w, m_gate_x_b, m_lru_lambda, m_w_out, m_ln1_g, m_ln1_b, m_w_ffn_up, m_ffn_conv_w, m_ffn_conv_b, m_w_ffn_down, m_ple_gate_w, m_ple_gate_b, m_ple_proj, m_ln2_g, m_ln2_b, v_w_in, v_attn_sinks, v_rnn_conv_w, v_rnn_conv_b, v_gate_a_w, v_gate_a_b, v_gate_x_w, v_gate_x_b, v_lru_lambda, v_w_out, v_ln1_g, v_ln1_b, v_w_ffn_up, v_ffn_conv_w, v_ffn_conv_b, v_w_ffn_down, v_ple_gate_w, v_ple_gate_b, v_ple_proj, v_ln2_g, v_ln2_b):
    from_col_blocks = lambda g: g.transpose(1, 0, 2).reshape(g.shape[1], N_DEV * g.shape[2])

    xs, ps, tgt, sinks = x[0], p[0, 0], loss_target[0], attn_sinks[0]
    wa, wx = _block_diag(gate_a_w[0]), _block_diag(gate_x_w[0])

    conv_cols = jnp.concatenate([rnn_conv_w[0].reshape(1, -1), ffn_conv_w[0].reshape(1, -1)], axis=1)
    n_rc, n_fc = 4 * D_RNN // N_DEV, 3 * D_FF // N_DEV
    ((g_in,),) = _comm_call([_Gather([w_in[0].T.astype(BF16)])], "gather_w_in")
    w_in_full = g_in.reshape(D_IN, D_MODEL)

    (q, k, v, xr, gr), (g_conv,) = _in_proj(xs, w_in_full, comm=_Bcast([jnp.broadcast_to(conv_cols, (8, n_rc + n_fc))]))
    rcw = from_col_blocks(g_conv[:, 0, :n_rc].reshape(N_DEV, 4, D_RNN // N_DEV))
    fcw = from_col_blocks(g_conv[:, 0, n_rc:].reshape(N_DEV, 3, D_FF // N_DEV))
    (att, lse), (w_up,) = _attn_fwd(q, k, v, sinks, comm=_Gather([w_ffn_up[0].astype(BF16)]))
    (rec, h, *kept), (g_out, g_down) = _rnn_fwd(xr, gr, rcw, rnn_conv_b, wa, wx, gate_a_b, gate_x_b, lru_lambda,
                                         comm=_Gather([w_out[0].astype(BF16), w_ffn_down[0].astype(BF16)]))
    w_out_full = g_out.reshape(D_MODEL, D_MODEL)
    (z1, h1, h1b, gate, act, gl, vdgl), (g_pg, g_pp) = _mix_ln1_up(
        xs, att, rec, w_out_full, ln1_g, ln1_b, w_up, fcw, ffn_conv_b,
        comm=_Gather([ple_gate_w[0].astype(BF16), ple_proj[0].astype(BF16)]))
    dz2b, dpreb, dppb, dgc, dval, dh1p, acc_t = _tail(
        act, gl, vdgl, h1, h1b, ps, tgt, g_down.reshape(D_FF, D_MODEL), g_pg.reshape(D_MODEL, D_MODEL), ple_gate_b,
        from_col_blocks(g_pp), ln2_g, ln2_b)

    gd_down = _weight_grad([dz2b], [act], "down_grad", "rows_t")
    gd_pg = _weight_grad([h1b], [dpreb], "pg_grad", "rows", ts=1024)
    gd_pp = _weight_grad([ps], [dppb], "pp_grad", "cols", ts=1024)
    (dgate, dz1, dz1b, datt, drec, acc_f, acc_d), (r_down, r_pg, r_pp) = _up_bwd(
        dgc, gate, dval, dh1p, z1, w_up, fcw, w_out_full, ln1_g, comm=_Exchange([gd_down, gd_pg, gd_pp]))
    gd_up = _weight_grad([h1b], [dgate, dval], "up_grad", "cols")
    gd_out = _weight_grad([att, rec], [dz1b], "out_grad", "rows", ts=1024)
    (dq, dkc, dkp, dvc, dvp, acc_s), (r_up,) = _attn_bwd(q, k, v, lse, datt, sinks, comm=_Exchange([gd_up]))
    (dxr, dgr, g_wa, g_wx, acc_r), (r_out,) = _rnn_bwd(xr, gr, h, kept, drec, rcw, wa, wx, lru_lambda,
                                                       comm=_Exchange([gd_out]))
    (du, dx), _ = _in_bwd(dq, dkc, dkp, dvc, dvp, dxr, dgr, dz1, w_in_full)
    acc_rows = jnp.concatenate([acc_t, acc_f, acc_d, acc_s, acc_r], axis=1)
    gd_in_a, small_parts = _weight_grad([du], [xs], "in_grad_a", "rows", ts=1024, b_window=(0, D_MODEL // 2),
                                        comm=_Bcast([acc_rows, jnp.concatenate([g_wa, g_wx], axis=1)]))
    gd_in_b, (r_in_a,) = _weight_grad([du], [xs], "in_grad_b", "rows", ts=1024, b_window=(1, D_MODEL // 2),
                                      comm=_Exchange([gd_in_a]))
    ((r_in_b,),) = _comm_call([_Exchange([gd_in_b])], "exchange_w_in")
    r_in = jnp.concatenate([r_in_a, r_in_b], axis=2)

    outs = {}
    res = _sum_adamw(r_in, w_in[0].T, m_w_in[0].T, v_w_in[0].T, "adamw_w_in")
    outs["w_in"] = [r.T[None] for r in res]
    for name, parts, w, m, v in [("w_out", r_out, w_out, m_w_out, v_w_out),
                                 ("w_ffn_up", r_up, w_ffn_up, m_w_ffn_up, v_w_ffn_up),
                                 ("w_ffn_down", r_down, w_ffn_down, m_w_ffn_down, v_w_ffn_down),
                                 ("ple_gate_w", r_pg, ple_gate_w, m_ple_gate_w, v_ple_gate_w),
                                 ("ple_proj", r_pp, ple_proj, m_ple_proj, v_ple_proj)]:
        res = _sum_adamw(parts, w[0], m[0], v[0], "adamw_" + name)
        outs[name] = [r[None] for r in res]

    given = dict(attn_sinks=(attn_sinks, m_attn_sinks, v_attn_sinks), rnn_conv_w=(rnn_conv_w, m_rnn_conv_w, v_rnn_conv_w),
                 rnn_conv_b=(rnn_conv_b, m_rnn_conv_b, v_rnn_conv_b), gate_a_w=(gate_a_w, m_gate_a_w, v_gate_a_w),
                 gate_a_b=(gate_a_b, m_gate_a_b, v_gate_a_b), gate_x_w=(gate_x_w, m_gate_x_w, v_gate_x_w),
                 gate_x_b=(gate_x_b, m_gate_x_b, v_gate_x_b), lru_lambda=(lru_lambda, m_lru_lambda, v_lru_lambda),
                 ln1_g=(ln1_g, m_ln1_g, v_ln1_g), ln1_b=(ln1_b, m_ln1_b, v_ln1_b),
                 ffn_conv_w=(ffn_conv_w, m_ffn_conv_w, v_ffn_conv_w), ffn_conv_b=(ffn_conv_b, m_ffn_conv_b, v_ffn_conv_b),
                 ple_gate_b=(ple_gate_b, m_ple_gate_b, v_ple_gate_b), ln2_g=(ln2_g, m_ln2_g, v_ln2_g),
                 ln2_b=(ln2_b, m_ln2_b, v_ln2_b))
    as_2d = lambda a: a.reshape(-1, a.shape[-1])
    loss_row, small_res = _small_update(*small_parts, [tuple(as_2d(a) for a in given[n]) for n, *_ in _SMALL])
    loss = loss_row[0, 0]
    for (n, *_), res in zip(_SMALL, small_res):
        outs[n] = [r.reshape(given[n][0].shape) for r in res]

    order = ["w_in", "attn_sinks", "rnn_conv_w", "rnn_conv_b", "gate_a_w", "gate_a_b", "gate_x_w", "gate_x_b",
             "lru_lambda", "w_out", "ln1_g", "ln1_b", "w_ffn_up", "ffn_conv_w", "ffn_conv_b", "w_ffn_down",
             "ple_gate_w", "ple_gate_b", "ple_proj", "ln2_g", "ln2_b"]
    return (loss, dx[None], *[outs[n][0] for n in order], *[outs[n][1] for n in order],
            *[outs[n][2] for n in order], *[outs[n][3] for n in order])
```

```python
import jax
import jax.numpy as jnp
from jax import lax
from jax.experimental import pallas as pl
from jax.experimental.pallas import tpu as pltpu

F32 = jnp.float32
BF16 = jnp.bfloat16

D_MODEL = 1024
D_ATT = 512
D_KV = 128
HEAD_DIM = 64
N_HEADS = 8
N_KV = 2
D_RNN = 512
RNN_BLOCKS = 8
D_IN = 1792
D_FF = 3072
PLE_DIM = 256
QBLK = 128
N_DEV = 8
ALPHA = float(2 ** 0.25)
LN_EPS = 1e-5
LRU_C = 8.0
ADAM_LR, ADAM_B1, ADAM_B2, ADAM_EPS, ADAM_WD, ADAM_STEP = 0.001, 0.9, 0.999, 1e-08, 0.01, 10

V7X_VMEM_LIMIT = 56 * 1024 * 1024
MESH = pl.DeviceIdType.MESH


def _params(*sem, vmem=V7X_VMEM_LIMIT):
    return pltpu.CompilerParams(dimension_semantics=sem or None, vmem_limit_bytes=vmem)


def _resident(shape):
    return pl.BlockSpec(shape, lambda *_: (0,) * len(shape), pipeline_mode=pl.Buffered(1))


def _rows(tb, cols):
    return pl.BlockSpec((tb, cols), lambda i: (i, 0))


def _acc(shape):
    return pl.BlockSpec(shape, lambda *_: (0,) * len(shape))


def _dot(a, b):
    return jnp.dot(a, b, preferred_element_type=F32)


def _dot_nt(a, b):
    return lax.dot_general(a, b, (((1,), (1,)), ((), ())), preferred_element_type=F32)


def _dot_tn(a, b):
    return lax.dot_general(a, b, (((0,), (0,)), ((), ())), preferred_element_type=F32)


def _sigmoid(x):
    return 1.0 / (1.0 + jnp.exp(-x))


_GELU_C = 0.7978845608028654
_GELU_K = 0.044715


def _gelu_and_grad(x):
    u = x * x
    t = jnp.tanh(x * (_GELU_C + (_GELU_C * _GELU_K) * u))
    hp = 0.5 + 0.5 * t
    dg = hp + x * (0.5 - 0.5 * (t * t)) * (_GELU_C + (3.0 * _GELU_C * _GELU_K) * u)
    return x * hp, dg


def _gelu(x):
    return 0.5 * x * (1.0 + jnp.tanh(_GELU_C * (x + _GELU_K * x * x * x)))


def _ln_stats(z):
    mu = jnp.mean(z, axis=-1, keepdims=True)
    zc = z - mu
    var = jnp.mean(zc * zc, axis=-1, keepdims=True)
    rstd = lax.rsqrt(var + LN_EPS)
    return zc * rstd, rstd


def _ln_bwd(dy, xhat, rstd, g):
    dxh = dy * g
    m1 = jnp.mean(dxh, axis=-1, keepdims=True)
    m2 = jnp.mean(dxh * xhat, axis=-1, keepdims=True)
    return rstd * (dxh - m1 - xhat * m2)


def _softplus_neg(lam):
    u = jnp.exp(-jnp.abs(lam))
    w = 1.0 + u
    d = w - 1.0
    log1p_u = jnp.where(d == 0.0, u, jnp.log(w) * (u / jnp.where(d == 0.0, 1.0, d)))
    return jnp.maximum(-lam, 0.0) + log1p_u


def _shift_down(x, halo, s):
    xs = pltpu.roll(x, s, 0)
    hs = pltpu.roll(halo, s, 0)
    row8 = lax.broadcasted_iota(jnp.int32, hs.shape, 0)
    first = jnp.where(row8 < s, hs, xs[:8])
    return jnp.concatenate([first, xs[8:]], axis=0)


def _shift_up(x, halo, s):
    n = x.shape[0]
    xs = pltpu.roll(x, n - s, 0)
    hs = pltpu.roll(halo, 8 - s, 0)
    row8 = lax.broadcasted_iota(jnp.int32, hs.shape, 0)
    last = jnp.where(row8 >= 8 - s, hs, xs[n - 8:])
    return jnp.concatenate([xs[:n - 8], last], axis=0)


def _row_sum(x):
    return jnp.sum(x, axis=0, keepdims=True)


def _put_rows(acc_ref, rows):
    row8 = lax.broadcasted_iota(jnp.int32, acc_ref.shape, 0)
    upd = jnp.zeros(acc_ref.shape, F32)
    for r, vec in enumerate(rows):
        upd = jnp.where(row8 == r, vec, upd)
    acc_ref[...] += upd


def _place():
    return lax.axis_index("x"), lax.axis_index("y"), lax.axis_index("c")


def _dev_index(px, py, pc):
    return 4 * px + 2 * py + pc


_ANY = pl.BlockSpec(memory_space=pl.ANY)


class _Gather:
    def __init__(self, arrays):
        self.arrays = list(arrays)
        self.n = len(self.arrays)

    def out_shape(self):
        return [jax.ShapeDtypeStruct((N_DEV,) + s.shape, s.dtype) for s in self.arrays]

    def scratch(self):
        return [pltpu.SemaphoreType.DMA((self.n, 7)), pltpu.SemaphoreType.DMA((self.n, 7)),
                pltpu.SemaphoreType.DMA((self.n,))]

    def _parts(self, ins, outs, sems):
        send_sems, recv_sems, local_sems = sems
        x, y, c = _place()
        me, sibling = (x, y, c), (x, y, 1 - c)
        chips = [(1 - x, y), (x, 1 - y), (1 - x, 1 - y)]

        def copy(a, k, block, to, src=None):
            rows = outs[a].at[_dev_index(*block)]
            return pltpu.make_async_remote_copy(
                src_ref=rows if src is None else src, dst_ref=rows, send_sem=send_sems.at[a, k],
                recv_sem=recv_sems.at[a, k], device_id=to, device_id_type=MESH)

        rng = range(self.n)
        mine = [pltpu.make_async_copy(ins[a], outs[a].at[_dev_index(*me)], local_sems.at[a]) for a in rng]
        first = [copy(a, 0, me, sibling, src=ins[a]) for a in rng]
        first += [copy(a, 1 + j, me, (*chip, c), src=ins[a]) for j, chip in enumerate(chips) for a in rng]
        landed = [copy(a, 1 + j, (*chip, c), me) for j, chip in enumerate(chips) for a in rng]
        passed = [copy(a, 4 + j, (*chip, c), sibling) for j, chip in enumerate(chips) for a in rng]
        from_sibling = [copy(a, 0, sibling, me) for a in rng]
        from_sibling += [copy(a, 4 + j, (*chip, 1 - c), me) for j, chip in enumerate(chips) for a in rng]
        return mine, first, landed, passed, from_sibling

    def start(self, ins, outs, sems):
        mine, first, _, _, _ = self._parts(ins, outs, sems)
        for cp in mine + first:
            cp.start()

    def forward(self, ins, outs, sems):
        _, _, landed, passed, _ = self._parts(ins, outs, sems)
        for got, fwd in zip(landed, passed):
            got.wait_recv()
            fwd.start()

    def finish(self, ins, outs, sems):
        mine, first, _, passed, from_sibling = self._parts(ins, outs, sems)
        for cp in from_sibling:
            cp.wait_recv()
        for cp in first + passed:
            cp.wait_send()
        for cp in mine:
            cp.wait()

    def before(self, ins, outs, sems, step, nsteps):
        pl.when(step == 0)(lambda: self.start(ins, outs, sems))
        pl.when(step == (7 * nsteps) // 8)(lambda: self.forward(ins, outs, sems))

    def after(self, ins, outs, sems, step, nsteps):
        pl.when(step == nsteps - 1)(lambda: self.finish(ins, outs, sems))


class _Exchange:
    def __init__(self, arrays):
        self.arrays = list(arrays)
        self.n = len(self.arrays)

    def out_shape(self):
        return [jax.ShapeDtypeStruct(b.shape, b.dtype) for b in self.arrays]

    def scratch(self):
        return [pltpu.SemaphoreType.DMA((self.n, 7)), pltpu.SemaphoreType.DMA((self.n, 7)),
                pltpu.SemaphoreType.DMA((self.n,))]

    def _parts(self, ins, outs, sems):
        send_sems, recv_sems, local_sems = sems
        x, y, c = _place()
        me = _dev_index(x, y, c)
        peers = [(x ^ (k >> 2), y ^ ((k >> 1) & 1), c ^ (k & 1)) for k in range(1, N_DEV)]
        rng = range(self.n)
        mine = [pltpu.make_async_copy(ins[a].at[me], outs[a].at[me], local_sems.at[a]) for a in rng]
        sent = [pltpu.make_async_remote_copy(
            src_ref=ins[a].at[_dev_index(*to)], dst_ref=outs[a].at[me], send_sem=send_sems.at[a, k],
            recv_sem=recv_sems.at[a, k], device_id=to, device_id_type=MESH) for k, to in enumerate(peers) for a in rng]
        arrivals = [pltpu.make_async_remote_copy(
            src_ref=ins[a].at[me], dst_ref=outs[a].at[_dev_index(*frm)], send_sem=send_sems.at[a, k],
            recv_sem=recv_sems.at[a, k], device_id=frm, device_id_type=MESH) for k, frm in enumerate(peers) for a in rng]
        return mine, sent, arrivals

    def start(self, ins, outs, sems):
        mine, sent, _ = self._parts(ins, outs, sems)
        for cp in mine + sent:
            cp.start()

    def finish(self, ins, outs, sems):
        mine, sent, arrivals = self._parts(ins, outs, sems)
        for cp in arrivals:
            cp.wait_recv()
        for cp in sent:
            cp.wait_send()
        for cp in mine:
            cp.wait()

    def before(self, ins, outs, sems, step, nsteps):
        pl.when(step == 0)(lambda: self.start(ins, outs, sems))

    def after(self, ins, outs, sems, step, nsteps):
        pl.when(step == nsteps - 1)(lambda: self.finish(ins, outs, sems))


class _Bcast(_Exchange):
    def out_shape(self):
        return [jax.ShapeDtypeStruct((N_DEV,) + s.shape, s.dtype) for s in self.arrays]

    def _parts(self, ins, outs, sems):
        send_sems, recv_sems, local_sems = sems
        x, y, c = _place()
        me = _dev_index(x, y, c)
        peers = [(x ^ (k >> 2), y ^ ((k >> 1) & 1), c ^ (k & 1)) for k in range(1, N_DEV)]
        rng = range(self.n)
        mine = [pltpu.make_async_copy(ins[a], outs[a].at[me], local_sems.at[a]) for a in rng]
        sent = [pltpu.make_async_remote_copy(
            src_ref=ins[a], dst_ref=outs[a].at[me], send_sem=send_sems.at[a, k], recv_sem=recv_sems.at[a, k],
            device_id=to, device_id_type=MESH) for k, to in enumerate(peers) for a in rng]
        arrivals = [pltpu.make_async_remote_copy(
            src_ref=ins[a], dst_ref=outs[a].at[_dev_index(*frm)], send_sem=send_sems.at[a, k],
            recv_sem=recv_sems.at[a, k], device_id=frm, device_id_type=MESH) for k, frm in enumerate(peers) for a in rng]
        return mine, sent, arrivals


def _comm_call(comms, name):
    ns = [c.n for c in comms]
    n = sum(ns)

    def body(*refs):
        parts, a, s = [], 0, 2 * n
        for c in comms:
            parts.append((c, refs[a:a + c.n], refs[n + a:n + a + c.n], refs[s:s + 3]))
            a, s = a + c.n, s + 3
        for c, ins, outs, sems in parts:
            c.start(ins, outs, sems)
        for c, ins, outs, sems in parts:
            if isinstance(c, _Gather):
                c.forward(ins, outs, sems)
        for c, ins, outs, sems in parts:
            c.finish(ins, outs, sems)

    res = pl.pallas_call(
        body, name=name, in_specs=[_ANY] * n, out_specs=[_ANY] * n,
        out_shape=[s for c in comms for s in c.out_shape()], scratch_shapes=[s for c in comms for s in c.scratch()],
    )(*[arr for c in comms for arr in c.arrays])
    out, a = [], 0
    for k in ns:
        out.append(res[a:a + k])
        a += k
    return out


def _pcall(body, args, *, name, grid, in_specs, out_specs, out_shape, scratch_shapes=(), sem="parallel", comm=None,
           step_axis=0):
    sem = (sem,) * len(grid) if isinstance(sem, str) else sem
    if comm is None:
        res = pl.pallas_call(body, name=name, grid=grid, in_specs=in_specs, out_specs=out_specs, out_shape=out_shape,
                             scratch_shapes=list(scratch_shapes), compiler_params=_params(*sem))(*args)
        return res, []
    n_in, n_out, n_scr, n = len(in_specs), len(out_specs), len(scratch_shapes), comm.n
    nsteps = grid[step_axis]
    assert all(g == 1 for ax, g in enumerate(grid) if ax != step_axis)

    def hosted(*refs):
        ins, cin = refs[:n_in], refs[n_in:n_in + n]
        o0 = n_in + n
        outs, cout = refs[o0:o0 + n_out], refs[o0 + n_out:o0 + n_out + n]
        s0 = o0 + n_out + n
        scr, sems = refs[s0:s0 + n_scr], refs[s0 + n_scr:]
        step = pl.program_id(step_axis)
        comm.before(cin, cout, sems, step, nsteps)
        body(*ins, *outs, *scr)
        comm.after(cin, cout, sems, step, nsteps)

    res = pl.pallas_call(
        hosted, name=name, grid=grid, in_specs=list(in_specs) + [_ANY] * n, out_specs=list(out_specs) + [_ANY] * n,
        out_shape=list(out_shape) + comm.out_shape(), scratch_shapes=list(scratch_shapes) + comm.scratch(),
        compiler_params=_params(*(("arbitrary",) * len(grid))))(*args, *comm.arrays)
    return res[:n_out], res[n_out:]


def _in_proj(x, w_in_t, comm=None):
    S = x.shape[0]
    tb = min(512, S)

    def body(x_ref, w_ref, q_ref, k_ref, v_ref, xr_ref, gr_ref):
        u = _dot_nt(x_ref[...].astype(BF16), w_ref[...])
        q_ref[...] = (u[:, :D_ATT] * (HEAD_DIM ** -0.5)).astype(BF16)
        k_ref[...] = u[:, D_ATT:D_ATT + D_KV].astype(BF16)
        v_ref[...] = u[:, D_ATT + D_KV:D_ATT + 2 * D_KV].astype(BF16)
        xr_ref[...] = u[:, D_ATT + 2 * D_KV:D_ATT + 2 * D_KV + D_RNN]
        gr_ref[...] = u[:, D_ATT + 2 * D_KV + D_RNN:]

    return _pcall(
        body, (x, w_in_t), name="in_proj", grid=(S // tb,), comm=comm,
        in_specs=[_rows(tb, D_MODEL), _resident((D_IN, D_MODEL))],
        out_specs=[_rows(tb, D_ATT), _rows(tb, D_KV), _rows(tb, D_KV), _rows(tb, D_RNN), _rows(tb, D_RNN)],
        out_shape=[jax.ShapeDtypeStruct((S, D_ATT), BF16), jax.ShapeDtypeStruct((S, D_KV), BF16),
                   jax.ShapeDtypeStruct((S, D_KV), BF16), jax.ShapeDtypeStruct((S, D_RNN), F32),
                   jax.ShapeDtypeStruct((S, D_RNN), F32)])


GROUP = N_HEADS // N_KV


def _band_mask(i):
    qi = lax.broadcasted_iota(jnp.int32, (GROUP * QBLK, 2 * QBLK), 0) & (QBLK - 1)
    sj = lax.broadcasted_iota(jnp.int32, (GROUP * QBLK, 2 * QBLK), 1)
    return (sj > qi) & (sj <= qi + QBLK) & ((sj >= QBLK) | (i > 0))


def _stack_heads(x, g):
    return jnp.concatenate([x[:, (g * GROUP + hh) * HEAD_DIM:(g * GROUP + hh + 1) * HEAD_DIM] for hh in range(GROUP)],
                           axis=0)


def _unstack_heads(x4):
    return [x4[hh * QBLK:(hh + 1) * QBLK] for hh in range(GROUP)]


def _sink_column(sink_ref, g):
    head = lax.broadcasted_iota(jnp.int32, (GROUP * QBLK, 1), 0) // QBLK
    col = jnp.full((GROUP * QBLK, 1), sink_ref[g * GROUP], F32)
    for hh in range(1, GROUP):
        col = jnp.where(head == hh, sink_ref[g * GROUP + hh], col)
    return col


ATT_STEP = 4


def _attn_specs(nq=1):
    cur = lambda i: (i, 0)
    prev = lambda i: (jnp.maximum(nq * i - 1, 0), 0)
    return [pl.BlockSpec((nq * QBLK, D_KV), cur), pl.BlockSpec((QBLK, D_KV), prev),
            pl.BlockSpec((nq * QBLK, D_KV), cur), pl.BlockSpec((QBLK, D_KV), prev)]


def _attn_fwd(q, k, v, sinks, comm=None):
    S = q.shape[0]
    nq = min(ATT_STEP, S // QBLK)

    def body(sink_ref, q_ref, kc_ref, kp_ref, vc_ref, vp_ref, o_ref, lse_ref):
        first = pl.program_id(0) * nq
        kall = jnp.concatenate([kp_ref[...], kc_ref[...]], axis=0)
        vall = jnp.concatenate([vp_ref[...], vc_ref[...]], axis=0)
        for b in range(nq):
            valid = _band_mask(first + b)
            rows = slice(b * QBLK, (b + 1) * QBLK)
            keys = slice(b * QBLK, (b + 2) * QBLK)
            qv = q_ref[rows, :]
            outs = []
            for g in range(N_KV):
                kcat = kall[keys, g * HEAD_DIM:(g + 1) * HEAD_DIM]
                vcat = vall[keys, g * HEAD_DIM:(g + 1) * HEAD_DIM]
                s = jnp.where(valid, _dot_nt(_stack_heads(qv, g), kcat), -1e30)
                sink = _sink_column(sink_ref, g)
                m = jnp.maximum(jnp.max(s, axis=1, keepdims=True), sink)
                p = jnp.exp(s - m)
                l = jnp.sum(p, axis=1, keepdims=True) + jnp.exp(sink - m)
                outs += _unstack_heads(_dot(p.astype(BF16), vcat) / l)
                lse_ref[(b * N_KV + g) * GROUP * QBLK:(b * N_KV + g + 1) * GROUP * QBLK, :] = m + jnp.log(l)
            o_ref[rows, :] = jnp.concatenate(outs, axis=1).astype(BF16)

    lse_rows = nq * N_HEADS * QBLK
    return _pcall(
        body, (sinks, q, k, k, v, v), name="attn_fwd", grid=(S // (nq * QBLK),), comm=comm,
        in_specs=[pl.BlockSpec(memory_space=pltpu.SMEM), _rows(nq * QBLK, D_ATT)] + _attn_specs(nq),
        out_specs=[_rows(nq * QBLK, D_ATT), _rows(lse_rows, 1)],
        out_shape=[jax.ShapeDtypeStruct((S, D_ATT), BF16), jax.ShapeDtypeStruct((S * N_HEADS, 1), F32)])


def _w_rows(w_ref):
    return [w_ref[k:k + 1, :] for k in range(w_ref.shape[0])]


def _conv4(x, halo, w, b):
    y = b + w[3] * x
    for s in (1, 2, 3):
        y = y + w[3 - s] * _shift_down(x, halo, s)
    return y


def _rnn_gates(xc, wa, wx, ba, bx, sp):
    xcb = xc.astype(BF16)
    r = _sigmoid(_dot(xcb, wa) + ba)
    ig = _sigmoid(_dot(xcb, wx) + bx)
    la = -LRU_C * r * sp
    a = jnp.exp(la)
    t = jnp.tanh(la)
    f = jnp.sqrt(-2.0 * t / (1.0 - t))
    return r, ig, a, f


def _rnn_fwd(xr, gr, conv_w, conv_b, wa, wx, ba, bx, lam, comm=None):
    S = xr.shape[0]
    tb = min(256, S)

    def body(xr_ref, gr_ref, cw_ref, cb_ref, wa_ref, wx_ref, ba_ref, bx_ref, lam_ref, rec_ref, h_ref,
             xc_ref, r_ref, ig_ref, a_ref, f_ref, halo_s, hc_s, a_s, b_s):
        @pl.when(pl.program_id(0) == 0)
        def _():
            halo_s[...] = jnp.zeros_like(halo_s)
            hc_s[...] = jnp.zeros_like(hc_s)

        x = xr_ref[...]
        xc = _conv4(x, halo_s[...], _w_rows(cw_ref), cb_ref[...])
        halo_s[...] = x[tb - 8:]
        r, ig, a, f = _rnn_gates(xc, wa_ref[...], wx_ref[...], ba_ref[...], bx_ref[...], _softplus_neg(lam_ref[...]))
        xc_ref[...] = xc
        r_ref[...] = r
        ig_ref[...] = ig
        a_ref[...] = a
        f_ref[...] = f
        a_s[...] = a
        b_s[...] = f * ig * xc
        row8 = lax.broadcasted_iota(jnp.int32, (8, D_RNN), 0)

        def tile(t, hc):
            o = pl.multiple_of(t * 8, 8)
            at = a_s[pl.ds(o, 8), :]
            bt = b_s[pl.ds(o, 8), :]
            for s in (1, 2, 4):
                keep = row8 >= s
                a_sh = jnp.where(keep, pltpu.roll(at, s, 0), 1.0)
                b_sh = jnp.where(keep, pltpu.roll(bt, s, 0), 0.0)
                bt = at * b_sh + bt
                at = at * a_sh
            ht = at * hc + bt
            b_s[pl.ds(o, 8), :] = ht
            return _row_sum(jnp.where(row8 == 7, ht, 0.0))

        hc_s[0:1, :] = lax.fori_loop(0, tb // 8, tile, hc_s[0:1, :], unroll=2)
        h = b_s[...]
        h_ref[...] = h
        rec_ref[...] = (h * _gelu(gr_ref[...])).astype(BF16)

    vec = _resident((1, D_RNN))
    kept = jax.ShapeDtypeStruct((S, D_RNN), F32)
    return _pcall(
        body, (xr, gr, conv_w, conv_b, wa, wx, ba, bx, lam), name="rnn_fwd", grid=(S // tb,), sem="arbitrary", comm=comm,
        in_specs=[_rows(tb, D_RNN), _rows(tb, D_RNN), _resident((4, D_RNN)), vec,
                  _resident((D_RNN, D_RNN)), _resident((D_RNN, D_RNN)), vec, vec, vec],
        out_specs=[_rows(tb, D_RNN)] * 7,
        out_shape=[jax.ShapeDtypeStruct((S, D_RNN), BF16), kept, kept, kept, kept, kept, kept],
        scratch_shapes=[pltpu.VMEM((8, D_RNN), F32), pltpu.VMEM((8, D_RNN), F32),
                        pltpu.VMEM((tb, D_RNN), F32), pltpu.VMEM((tb, D_RNN), F32)])


def _mix_ln1_up(x, att, rec, w_out, ln1_g, ln1_b, w_up, fcw, fcb, comm=None):
    S = x.shape[0]
    tb = min(256, S)
    nblk, _, wblk = w_up.shape
    half = nblk // 2

    def body(x_ref, att_ref, rec_ref, wo_ref, g_ref, b_ref, wu_ref, fcw_ref, fcb_ref,
             z1_ref, h1_ref, h1b_ref, gate_ref, act_ref, gl_ref, vdgl_ref, halo_s):
        @pl.when(pl.program_id(0) == 0)
        def _():
            halo_s[...] = jnp.zeros_like(halo_s)

        z1 = ALPHA * x_ref[...] + _dot(att_ref[...], wo_ref[:D_ATT, :]) + _dot(rec_ref[...], wo_ref[D_ATT:, :])
        z1_ref[...] = z1
        xhat, _ = _ln_stats(z1)
        h1 = xhat * g_ref[...] + b_ref[...]
        h1_ref[...] = h1
        h1b = h1.astype(BF16)
        h1b_ref[...] = h1b
        for jj in range(half):
            cols = slice(jj * wblk, (jj + 1) * wblk)
            gate = _dot(h1b, wu_ref[jj])
            val = _dot(h1b, wu_ref[jj + half])
            halo = halo_s[:, cols]
            conv = (fcb_ref[:, cols] + fcw_ref[2:3, cols] * gate + fcw_ref[1:2, cols] * _shift_down(gate, halo, 1)
                    + fcw_ref[0:1, cols] * _shift_down(gate, halo, 2))
            halo_s[:, cols] = gate[tb - 8:]
            gl, dgl = _gelu_and_grad(conv)
            gate_ref[:, cols] = gate.astype(BF16)
            act_ref[:, cols] = (gl * val).astype(BF16)
            gl_ref[:, cols] = gl.astype(BF16)
            vdgl_ref[:, cols] = (val * dgl).astype(BF16)

    vec = _resident((1, D_MODEL))
    wide = jax.ShapeDtypeStruct((S, D_FF), BF16)
    return _pcall(
        body, (x, att, rec, w_out, ln1_g, ln1_b, w_up, fcw, fcb), name="mix_ln1_up", grid=(S // tb,),
        sem="arbitrary", comm=comm,
        in_specs=[_rows(tb, D_MODEL), _rows(tb, D_ATT), _rows(tb, D_RNN), _resident((D_MODEL, D_MODEL)), vec, vec,
                  _resident(w_up.shape), _resident((3, D_FF)), _resident((1, D_FF))],
        out_specs=[_rows(tb, D_MODEL), _rows(tb, D_MODEL), _rows(tb, D_MODEL)] + [_rows(tb, D_FF)] * 4,
        out_shape=[jax.ShapeDtypeStruct((S, D_MODEL), F32), jax.ShapeDtypeStruct((S, D_MODEL), F32),
                   jax.ShapeDtypeStruct((S, D_MODEL), BF16), wide, wide, wide, wide],
        scratch_shapes=[pltpu.VMEM((8, D_FF), F32)])


def _tail(act, gl, vdgl, h1, h1b, p, tgt, w_down, w_pg, b_pg, w_pp, ln2_g, ln2_b):
    S = h1.shape[0]
    tb = min(256, S)

    def body(act_ref, gl_ref, vdgl_ref, h1_ref, h1b_ref, p_ref, t_ref, wd_ref, wpg_ref, bpg_ref, wpp_ref, g2_ref, b2_ref,
             dz2_ref, dpre_ref, dpp_ref, dgc_ref, dval_ref, dh1_ref, acc_ref):
        i = pl.program_id(0)

        @pl.when(i == 0)
        def _():
            acc_ref[...] = jnp.zeros_like(acc_ref)

        ffn = _dot(act_ref[...], wd_ref[...])
        h1 = h1_ref[...]
        sg = _sigmoid(_dot(h1b_ref[...], wpg_ref[...]) + bpg_ref[...])
        pp = _dot(p_ref[...].astype(BF16), wpp_ref[...])
        z2 = ALPHA * h1 + ffn + sg * pp
        xhat2, rstd2 = _ln_stats(z2)
        y = xhat2 * g2_ref[...] + b2_ref[...]
        err = y - t_ref[...]
        dy = err * (1.0 / D_MODEL)
        loss = 0.5 * jnp.sum(jnp.sum(err * err, axis=1, keepdims=True), axis=0, keepdims=True) * (1.0 / D_MODEL)
        dz2 = _ln_bwd(dy, xhat2, rstd2, g2_ref[...])
        dz2b = dz2.astype(BF16)
        dz2_ref[...] = dz2b
        dpre = dz2 * pp * sg * (1.0 - sg)
        dpreb = dpre.astype(BF16)
        dpre_ref[...] = dpreb
        dpp_ref[...] = (dz2 * sg).astype(BF16)
        dh1_ref[...] = ALPHA * dz2 + _dot_nt(dpreb, wpg_ref[...])
        dactb = _dot_nt(dz2b, wd_ref[...]).astype(BF16)
        dval_ref[...] = dactb * gl_ref[...]
        dgc_ref[...] = dactb * vdgl_ref[...]
        _put_rows(acc_ref, [_row_sum(dy * xhat2), _row_sum(dy), _row_sum(dpre),
                            jnp.broadcast_to(loss, (1, D_MODEL))])

    vec = _resident((1, D_MODEL))
    return pl.pallas_call(
        body, name="tail", grid=(S // tb,),
        in_specs=[_rows(tb, D_FF), _rows(tb, D_FF), _rows(tb, D_FF), _rows(tb, D_MODEL), _rows(tb, D_MODEL),
                  _rows(tb, PLE_DIM), _rows(tb, D_MODEL), _resident((D_FF, D_MODEL)), _resident((D_MODEL, D_MODEL)), vec,
                  _resident((PLE_DIM, D_MODEL)), vec, vec],
        out_specs=[_rows(tb, D_MODEL), _rows(tb, D_MODEL), _rows(tb, D_MODEL), _rows(tb, D_FF),
                   _rows(tb, D_FF), _rows(tb, D_MODEL), _acc((8, D_MODEL))],
        out_shape=[jax.ShapeDtypeStruct((S, D_MODEL), BF16),
                   jax.ShapeDtypeStruct((S, D_MODEL), BF16), jax.ShapeDtypeStruct((S, D_MODEL), BF16),
                   jax.ShapeDtypeStruct((S, D_FF), BF16), jax.ShapeDtypeStruct((S, D_FF), BF16),
                   jax.ShapeDtypeStruct((S, D_MODEL), F32), jax.ShapeDtypeStruct((8, D_MODEL), F32)],
        compiler_params=_params("arbitrary"),
    )(act, gl, vdgl, h1, h1b, p, tgt, w_down, w_pg, b_pg, w_pp, ln2_g, ln2_b)


def _weight_grad(a_list, b_list, name, layout, ts=512, comm=None, b_window=None, halves=False):
    S = a_list[0].shape[0]
    ms = [a.shape[1] for a in a_list]
    M, nb = sum(ms), len(b_list)
    win, Nb = b_window if b_window else (0, b_list[0].shape[1])
    ts = min(ts, S)
    nk = S // ts
    per_b = N_DEV // nb
    na = len(a_list)

    n_out = 2 if halves else 1
    assert layout == "cols" or not halves

    def body(*refs):
        a_refs, b_refs, o_refs, acc_ref = refs[:na], refs[na:na + nb], refs[na + nb:na + nb + n_out], refs[-1]
        o_ref = o_refs[0]
        j, k = pl.program_id(0), pl.program_id(1)

        @pl.when(k == 0)
        def _():
            acc_ref[...] = jnp.zeros_like(acc_ref)

        for jj in range(nb):
            @pl.when(j == jj)
            def _():
                b = b_refs[jj][...].astype(BF16)
                off = 0
                for a_ref, m in zip(a_refs, ms):
                    acc_ref[off:off + m, :] += _dot_tn(a_ref[...].astype(BF16), b)
                    off += m

        @pl.when(k == nk - 1)
        def _():
            for d in range(per_b):
                if layout == "rows":
                    o_ref[d] = acc_ref[d * (M // N_DEV):(d + 1) * (M // N_DEV), :].astype(BF16)
                elif layout == "cols" and halves:
                    for o_half, r0 in zip(o_refs, (0, M // 2)):
                        o_half[d] = acc_ref[r0:r0 + M // 2, d * (Nb // per_b):(d + 1) * (Nb // per_b)].astype(BF16)
                elif layout == "cols":
                    o_ref[d] = acc_ref[:, d * (Nb // per_b):(d + 1) * (Nb // per_b)].astype(BF16)
                else:
                    o_ref[d] = acc_ref[:, d * (Nb // per_b):(d + 1) * (Nb // per_b)].T.astype(BF16)

    def b_index(jj):
        return lambda j, k: (jnp.where(j == jj, k, jnp.where(j < jj, 0, nk - 1)), win)

    if layout == "rows":
        assert nb == 1
        blk = (N_DEV, M // N_DEV, Nb)
    elif layout == "cols":
        blk = (per_b, M // n_out, Nb // per_b)
    else:
        blk = (per_b, Nb // per_b, M)
    res, comm_res = _pcall(
        body, (*a_list, *b_list), name=name, grid=(nb, nk), sem="arbitrary", comm=comm, step_axis=1,
        in_specs=[pl.BlockSpec((ts, m), lambda j, k: (k, 0)) for m in ms]
        + [pl.BlockSpec((ts, Nb), b_index(jj)) for jj in range(nb)],
        out_specs=[pl.BlockSpec(blk, lambda j, k: (j, 0, 0))] * n_out,
        out_shape=[jax.ShapeDtypeStruct((N_DEV,) + blk[1:], BF16)] * n_out,
        scratch_shapes=[pltpu.VMEM((M, Nb), F32)])
    res = res if halves else res[0]
    return (res, comm_res) if comm is not None else res


def _up_bwd(dgc, gate, dval, dh1p, z1, w_up, fcw, w_out, ln1_g, comm=None):
    S = z1.shape[0]
    tb = min(256, S)
    t16 = tb // 16
    n16 = S // 16
    nblk, _, wblk = w_up.shape
    half = nblk // 2
    nsteps = S // tb

    def body(dgc_ref, dgn_ref, gc_ref, gp_ref, dval_ref, dh1p_ref, z1_ref, wu_ref, fcw_ref, wo_ref, g1_ref,
             dgate_ref, dz1_ref, dz1b_ref, datt_ref, drec_ref, accf_ref, accd_ref):
        i = pl.program_id(0)

        @pl.when(i == 0)
        def _():
            accf_ref[...] = jnp.zeros_like(accf_ref)
            accd_ref[...] = jnp.zeros_like(accd_ref)

        dg = dgc_ref[...].astype(F32)
        nxt = jnp.where(i < nsteps - 1, dgn_ref[...].astype(F32)[0:8], 0.0)
        w = _w_rows(fcw_ref)
        dgate = (w[2] * dg + w[1] * _shift_up(dg, nxt, 1) + w[0] * _shift_up(dg, nxt, 2)).astype(BF16)
        dgate_ref[...] = dgate
        gate = gc_ref[...].astype(F32)
        halo = jnp.where(i > 0, gp_ref[...].astype(F32)[8:16], 0.0)
        _put_rows(accf_ref, [_row_sum(dg * _shift_down(gate, halo, 2)), _row_sum(dg * _shift_down(gate, halo, 1)),
                             _row_sum(dg * gate), _row_sum(dg)])

        dh1 = dh1p_ref[...]
        for j in range(nblk):
            src = dgate if j < half else dval_ref[...]
            jj = j % half
            dh1 = dh1 + _dot_nt(src[:, jj * wblk:(jj + 1) * wblk], wu_ref[j])
        xhat1, rstd1 = _ln_stats(z1_ref[...])
        dz1 = _ln_bwd(dh1, xhat1, rstd1, g1_ref[...])
        dz1_ref[...] = dz1
        dz1b = dz1.astype(BF16)
        dz1b_ref[...] = dz1b
        dcat = _dot_nt(dz1b, wo_ref[...])
        datt_ref[...] = dcat[:, :D_ATT].astype(BF16)
        drec_ref[...] = dcat[:, D_ATT:]
        _put_rows(accd_ref, [_row_sum(dh1 * xhat1), _row_sum(dh1)])

    prev16 = pl.BlockSpec((16, D_FF), lambda i: (jnp.maximum(i * t16 - 1, 0), 0))
    next16 = pl.BlockSpec((16, D_FF), lambda i: (jnp.minimum((i + 1) * t16, n16 - 1), 0))
    return _pcall(
        body, (dgc, dgc, gate, gate, dval, dh1p, z1, w_up, fcw, w_out, ln1_g), name="up_bwd",
        grid=(nsteps,), sem="arbitrary", comm=comm,
        in_specs=[_rows(tb, D_FF), next16, _rows(tb, D_FF), prev16, _rows(tb, D_FF), _rows(tb, D_MODEL),
                  _rows(tb, D_MODEL), _resident(w_up.shape), _resident((3, D_FF)),
                  _resident((D_MODEL, D_MODEL)), _resident((1, D_MODEL))],
        out_specs=[_rows(tb, D_FF), _rows(tb, D_MODEL), _rows(tb, D_MODEL), _rows(tb, D_ATT), _rows(tb, D_RNN),
                   _acc((8, D_FF)), _acc((8, D_MODEL))],
        out_shape=[jax.ShapeDtypeStruct((S, D_FF), BF16), jax.ShapeDtypeStruct((S, D_MODEL), F32),
                   jax.ShapeDtypeStruct((S, D_MODEL), BF16), jax.ShapeDtypeStruct((S, D_ATT), BF16),
                   jax.ShapeDtypeStruct((S, D_RNN), F32), jax.ShapeDtypeStruct((8, D_FF), F32),
                   jax.ShapeDtypeStruct((8, D_MODEL), F32)])


def _attn_bwd(q, k, v, lse, do, sinks, comm=None):
    S = q.shape[0]
    grp = N_HEADS // N_KV
    nq = min(ATT_STEP, S // QBLK)

    def body(sink_ref, q_ref, kc_ref, kp_ref, vc_ref, vp_ref, do_ref, lse_ref, dq_ref, dkc_ref, dkp_ref, dvc_ref, dvp_ref,
             ds_ref):
        i = pl.program_id(0)

        @pl.when(i == 0)
        def _():
            ds_ref[...] = jnp.zeros_like(ds_ref)

        row8 = lax.broadcasted_iota(jnp.int32, (8, 128), 0)
        lane8 = lax.broadcasted_iota(jnp.int32, (8, 128), 1)
        dsink = jnp.zeros((8, 128), F32)
        kall = jnp.concatenate([kp_ref[...], kc_ref[...]], axis=0)
        vall = jnp.concatenate([vp_ref[...], vc_ref[...]], axis=0)
        dk_t = [jnp.zeros((D_KV, QBLK), F32) for _ in range(nq + 1)]
        dv_t = [jnp.zeros((D_KV, QBLK), F32) for _ in range(nq + 1)]
        for b in range(nq):
            valid = _band_mask(i * nq + b)
            rows = slice(b * QBLK, (b + 1) * QBLK)
            keys = slice(b * QBLK, (b + 2) * QBLK)
            qv, dov = q_ref[rows, :], do_ref[rows, :]
            dqs, dks, dvs = [], [], []
            for g in range(N_KV):
                kcat = kall[keys, g * HEAD_DIM:(g + 1) * HEAD_DIM]
                vcat = vall[keys, g * HEAD_DIM:(g + 1) * HEAD_DIM]
                q4, do4 = _stack_heads(qv, g), _stack_heads(dov, g)
                s = jnp.where(valid, _dot_nt(q4, kcat), -1e30)
                lse = lse_ref[(b * N_KV + g) * GROUP * QBLK:(b * N_KV + g + 1) * GROUP * QBLK, :]
                p = jnp.exp(s - lse)
                p_sink = jnp.exp(_sink_column(sink_ref, g) - lse)
                dp = _dot_nt(do4, vcat)
                delta = jnp.sum(p * dp, axis=1, keepdims=True)
                dsc = (p * (dp - delta)).astype(BF16)
                dqs += _unstack_heads(_dot(dsc, kcat) * (HEAD_DIM ** -0.5))
                dks.append(_dot_tn(q4, dsc))
                dvs.append(_dot_tn(do4, p.astype(BF16)))
                for hh, part in enumerate(_unstack_heads(-p_sink * delta)):
                    here = (row8 == 0) & (lane8 == g * grp + hh)
                    dsink = dsink + jnp.where(here, jnp.sum(part, axis=0, keepdims=True), 0.0)
            dq_ref[rows, :] = jnp.concatenate(dqs, axis=1).astype(BF16)
            dk2, dv2 = jnp.concatenate(dks, axis=0), jnp.concatenate(dvs, axis=0)
            dk_t[b], dk_t[b + 1] = dk_t[b] + dk2[:, :QBLK], dk_t[b + 1] + dk2[:, QBLK:]
            dv_t[b], dv_t[b + 1] = dv_t[b] + dv2[:, :QBLK], dv_t[b + 1] + dv2[:, QBLK:]
        dkp_ref[...] = dk_t[0].T
        dvp_ref[...] = dv_t[0].T
        for b in range(nq):
            dkc_ref[b * QBLK:(b + 1) * QBLK, :] = dk_t[b + 1].T
            dvc_ref[b * QBLK:(b + 1) * QBLK, :] = dv_t[b + 1].T
        ds_ref[...] += dsink

    nsteps = S // (nq * QBLK)
    cur = jax.ShapeDtypeStruct((S, D_KV), F32)
    prev = jax.ShapeDtypeStruct((nsteps * QBLK, D_KV), F32)
    big = _rows(nq * QBLK, D_ATT)
    return _pcall(
        body, (sinks, q, k, k, v, v, do, lse), name="attn_bwd", grid=(nsteps,), sem="arbitrary", comm=comm,
        in_specs=[pl.BlockSpec(memory_space=pltpu.SMEM), big] + _attn_specs(nq) + [big, _rows(nq * N_HEADS * QBLK, 1)],
        out_specs=[big, _rows(nq * QBLK, D_KV), _rows(QBLK, D_KV), _rows(nq * QBLK, D_KV), _rows(QBLK, D_KV),
                   _acc((8, 128))],
        out_shape=[jax.ShapeDtypeStruct((S, D_ATT), BF16), cur, prev, cur, prev, jax.ShapeDtypeStruct((8, 128), F32)])


def _rnn_bwd(xr, gr, h, kept, drec, conv_w, wa, wx, lam, comm=None):
    S = xr.shape[0]
    tb = min(256, S)
    t8 = tb // 8
    nsteps = S // tb

    def body(xr_ref, xp_ref, gr_ref, h_ref, hp_ref, xc_ref, r_ref, ig_ref, a_ref, f_ref, drec_ref, cw_ref, wa_ref, wx_ref,
             lam_ref, dxr_ref, dgr_ref, gwa_ref, gwx_ref, acc_ref, carry_s, dxc_halo_s, d_s, gwa_s, gwx_s):
        i = pl.program_id(0)
        blk = nsteps - 1 - i

        @pl.when(i == 0)
        def _():
            gwa_s[...] = jnp.zeros_like(gwa_s)
            gwx_s[...] = jnp.zeros_like(gwx_s)
            acc_ref[...] = jnp.zeros_like(acc_ref)
            carry_s[...] = jnp.zeros_like(carry_s)
            dxc_halo_s[...] = jnp.zeros_like(dxc_halo_s)

        x = xr_ref[...]
        xhalo = jnp.where(blk > 0, xp_ref[...], 0.0)
        cw = _w_rows(cw_ref)
        xs = [_shift_down(x, xhalo, 3), _shift_down(x, xhalo, 2), _shift_down(x, xhalo, 1), x]
        xc, r, ig, a, f = xc_ref[...], r_ref[...], ig_ref[...], a_ref[...], f_ref[...]
        sp = _softplus_neg(lam_ref[...])
        hcur = h_ref[...]
        hprev = _shift_down(hcur, jnp.where(blk > 0, hp_ref[...], 0.0), 1)
        gl, dgl = _gelu_and_grad(gr_ref[...])
        drec = drec_ref[...]
        dgr_ref[...] = (drec * hcur * dgl).astype(BF16)
        d_s[...] = drec * gl
        row8 = lax.broadcasted_iota(jnp.int32, (8, D_RNN), 0)

        def tile(t, c):
            o = pl.multiple_of((t8 - 1 - t) * 8, 8)
            a8 = a_ref[pl.ds(o, 8), :]
            dt = d_s[pl.ds(o, 8), :]
            at = jnp.where(row8 == 7, 1.0, pltpu.roll(a8, 7, 0))
            for s in (1, 2, 4):
                keep = row8 < 8 - s
                a_sh = jnp.where(keep, pltpu.roll(at, 8 - s, 0), 1.0)
                d_sh = jnp.where(keep, pltpu.roll(dt, 8 - s, 0), 0.0)
                dt = at * d_sh + dt
                at = at * a_sh
            lt = at * c + dt
            d_s[pl.ds(o, 8), :] = lt
            return _row_sum(jnp.where(row8 == 0, a8 * lt, 0.0))

        carry_s[0:1, :] = lax.fori_loop(0, t8, tile, carry_s[0:1, :], unroll=2)
        lmb = d_s[...]
        a2 = a * a
        dla = lmb * hprev * a - lmb * ig * xc * (a2 / f)
        di = lmb * f * xc
        dr = dla * (-LRU_C) * sp
        dpa = dr * r * (1.0 - r)
        dpx = di * ig * (1.0 - ig)
        dpab = dpa.astype(BF16)
        dpxb = dpx.astype(BF16)
        xcb = xc.astype(BF16)
        gwa_s[...] += _dot_tn(xcb, dpab)
        gwx_s[...] += _dot_tn(xcb, dpxb)

        @pl.when(i == nsteps - 1)
        def _():
            for dense, out in ((gwa_s[...], gwa_ref), (gwx_s[...], gwx_ref)):
                for b in range(RNN_BLOCKS):
                    rows = slice(b * HEAD_DIM, (b + 1) * HEAD_DIM)
                    out[rows, :] = dense[rows, b * HEAD_DIM:(b + 1) * HEAD_DIM]

        dxc = lmb * f * ig + _dot_nt(dpab, wa_ref[...]) + _dot_nt(dpxb, wx_ref[...])
        nxt = dxc_halo_s[...]
        dxr = cw[3] * dxc
        for s in (1, 2, 3):
            dxr = dxr + cw[3 - s] * _shift_up(dxc, nxt, s)
        dxr_ref[...] = dxr.astype(BF16)
        dxc_halo_s[...] = dxc[:8]
        dlam = _row_sum(dla * (-LRU_C) * r) * (-1.0 / (1.0 + jnp.exp(lam_ref[...])))
        _put_rows(acc_ref, [_row_sum(dxc * xs[0]), _row_sum(dxc * xs[1]), _row_sum(dxc * xs[2]), _row_sum(dxc * xs[3]),
                            _row_sum(dxc), _row_sum(dpa), _row_sum(dpx), dlam])

    rev = lambda i: (nsteps - 1 - i, 0)
    prev8 = lambda i: (jnp.maximum((nsteps - 1 - i) * t8 - 1, 0), 0)
    blkspec = pl.BlockSpec((tb, D_RNN), rev)
    halo8 = pl.BlockSpec((8, D_RNN), prev8)
    vec = _resident((1, D_RNN))
    return _pcall(
        body, (xr, xr, gr, h, h, *kept, drec, conv_w, wa, wx, lam), name="rnn_bwd", grid=(nsteps,),
        sem="arbitrary", comm=comm,
        in_specs=[blkspec, halo8, blkspec, blkspec, halo8] + [blkspec] * 6
        + [_resident((4, D_RNN)), _resident((D_RNN, D_RNN)), _resident((D_RNN, D_RNN)), vec],
        out_specs=[blkspec, blkspec, _acc((D_RNN, HEAD_DIM)), _acc((D_RNN, HEAD_DIM)), _acc((8, D_RNN))],
        out_shape=[jax.ShapeDtypeStruct((S, D_RNN), BF16), jax.ShapeDtypeStruct((S, D_RNN), BF16),
                   jax.ShapeDtypeStruct((D_RNN, HEAD_DIM), F32), jax.ShapeDtypeStruct((D_RNN, HEAD_DIM), F32),
                   jax.ShapeDtypeStruct((8, D_RNN), F32)],
        scratch_shapes=[pltpu.VMEM((8, D_RNN), F32), pltpu.VMEM((8, D_RNN), F32), pltpu.VMEM((tb, D_RNN), F32),
                        pltpu.VMEM((D_RNN, D_RNN), F32), pltpu.VMEM((D_RNN, D_RNN), F32)])


def _in_bwd(dq, dkc, dkp, dvc, dvp, dxr, dgr, dz1, w_in, comm=None):
    S = dz1.shape[0]
    tb = min(ATT_STEP * QBLK, S)
    nsteps = S // tb

    def body(dq_ref, dkc_ref, dkn_ref, dvc_ref, dvn_ref, dxr_ref, dgr_ref, dz1_ref, w_ref, du_ref, dx_ref):
        last = pl.program_id(0) == nsteps - 1

        def total(cur_ref, next_ref):
            nxt = jnp.where(last, 0.0, next_ref[...])
            tail = cur_ref[tb - QBLK:, :] + nxt
            return jnp.concatenate([cur_ref[:tb - QBLK, :], tail], axis=0) if tb > QBLK else tail

        dk = total(dkc_ref, dkn_ref).astype(BF16)
        dv = total(dvc_ref, dvn_ref).astype(BF16)
        du = jnp.concatenate([dq_ref[...], dk, dv, dxr_ref[...], dgr_ref[...]], axis=1)
        du_ref[...] = du
        dx_ref[...] = ALPHA * dz1_ref[...] + _dot(du, w_ref[...])

    nextp = pl.BlockSpec((QBLK, D_KV), lambda i: (jnp.minimum(i + 1, nsteps - 1), 0))
    return _pcall(
        body, (dq, dkc, dkp, dvc, dvp, dxr, dgr, dz1, w_in), name="in_bwd", grid=(nsteps,), comm=comm,
        in_specs=[_rows(tb, D_ATT), _rows(tb, D_KV), nextp, _rows(tb, D_KV), nextp,
                  _rows(tb, D_RNN), _rows(tb, D_RNN), _rows(tb, D_MODEL), _resident((D_IN, D_MODEL))],
        out_specs=[_rows(tb, D_IN), _rows(tb, D_MODEL)],
        out_shape=[jax.ShapeDtypeStruct((S, D_IN), BF16), jax.ShapeDtypeStruct((S, D_MODEL), F32)])


def _block_diag(w):
    eye = jnp.eye(RNN_BLOCKS, dtype=w.dtype)
    return (w[:, :, None, :] * eye[:, None, :, None]).reshape(D_RNN, D_RNN).astype(BF16)


def _adamw(w, g, m, v):
    m = ADAM_B1 * m + (1.0 - ADAM_B1) * g
    v = ADAM_B2 * v + (1.0 - ADAM_B2) * (g * g)
    m_hat = m / (1.0 - ADAM_B1 ** ADAM_STEP)
    v_hat = v / (1.0 - ADAM_B2 ** ADAM_STEP)
    delta = -ADAM_LR * (m_hat / (jnp.sqrt(v_hat) + ADAM_EPS) + ADAM_WD * w)
    return delta, m, v


def _sum_adamw(parts, w, m, v, name):
    parts = parts if isinstance(parts, (list, tuple)) else [parts]
    R, C = w.shape
    rb = R if R <= 256 else 128
    per = parts[0].shape[1] // rb
    assert R % rb == 0 and parts[0].shape[1] % rb == 0
    n = len(parts)

    def body(*refs):
        p_refs = refs[:n]
        w_ref, m_ref, v_ref, g_out, d_out, m_out, v_out = refs[n:]
        which = pl.program_id(0) // per

        def total(p_ref):
            g = p_ref[0].astype(F32)
            for d in range(1, N_DEV):
                g = g + p_ref[d].astype(F32)
            return g

        g = total(p_refs[0])
        for j in range(1, n):
            g = jnp.where(which == j, total(p_refs[j]), g)
        delta, mn, vn = _adamw(w_ref[...], g, m_ref[...], v_ref[...])
        g_out[...] = g
        d_out[...] = delta
        m_out[...] = mn
        v_out[...] = vn

    def part_spec(j):
        return pl.BlockSpec((N_DEV, rb, C), lambda i: (0, jnp.clip(i - j * per, 0, per - 1), 0))

    blk = _rows(rb, C)
    out = jax.ShapeDtypeStruct((R, C), F32)
    return pl.pallas_call(
        body, name=name, grid=(R // rb,),
        in_specs=[part_spec(j) for j in range(n)] + [blk, blk, blk],
        out_specs=[blk, blk, blk, blk], out_shape=[out, out, out, out],
        compiler_params=_params("parallel"),
    )(*parts, w, m, v)


_SMALL = [("attn_sinks", "s", 0, 1, None), ("rnn_conv_w", "r", 0, 4, "cols"), ("rnn_conv_b", "r", 4, 1, None),
          ("gate_a_w", "a", 0, D_RNN, None), ("gate_a_b", "r", 5, 1, None), ("gate_x_w", "x", 0, D_RNN, None),
          ("gate_x_b", "r", 6, 1, None), ("lru_lambda", "r", 7, 1, None), ("ln1_g", "d", 0, 1, None),
          ("ln1_b", "d", 1, 1, None), ("ffn_conv_w", "f", 0, 3, "cols"), ("ffn_conv_b", "f", 3, 1, None),
          ("ple_gate_b", "t", 2, 1, None), ("ln2_g", "t", 0, 1, None), ("ln2_b", "t", 1, 1, None)]
_LOSS_ROW = 3


_ACC_COLS = {"t": (0, D_MODEL), "f": (D_MODEL, D_FF), "d": (D_MODEL + D_FF, D_MODEL), "s": (2 * D_MODEL + D_FF, 128),
             "r": (2 * D_MODEL + D_FF + 128, D_RNN)}
_ACC_WIDTH = 2 * D_MODEL + D_FF + 128 + D_RNN


def _small_update(rows_all, gates_all, params):
    flat = [arr for triple in params for arr in triple]
    n_par = len(_SMALL)

    def body(*refs):
        rows_ref, gates_ref = refs[:2]
        p_refs = refs[2:2 + 3 * n_par]
        loss_ref = refs[2 + 3 * n_par]
        o_refs = refs[3 + 3 * n_par:3 + 7 * n_par]
        rows_s, tmp_r, tmp_f = refs[3 + 7 * n_par:]
        me = _dev_index(*_place())
        rows_sum, gates_sum = rows_ref[0], gates_ref[0]
        for d in range(1, N_DEV):
            rows_sum = rows_sum + rows_ref[d]
            gates_sum = gates_sum + gates_ref[d]
        rows_s[...] = rows_sum
        t0 = _ACC_COLS["t"][0]
        loss_ref[...] = rows_s[_LOSS_ROW:_LOSS_ROW + 1, t0:t0 + 128]
        for i, (name, key, row, rows, how) in enumerate(_SMALL):
            w_ref, m_ref, v_ref = p_refs[3 * i:3 * i + 3]
            g_out, d_out, m_out, v_out = o_refs[4 * i:4 * i + 4]
            if key == "a":
                g = gates_sum[:, :HEAD_DIM]
            elif key == "x":
                g = gates_sum[:, HEAD_DIM:]
            elif how == "cols":
                c0, width = _ACC_COLS[key]
                full = rows_s[:, c0:c0 + width]
                shard = width // N_DEV
                mine = full[:, :shard]
                for d in range(1, N_DEV):
                    mine = jnp.where(me == d, full[:, d * shard:(d + 1) * shard], mine)
                tmp = tmp_r if key == "r" else tmp_f
                tmp[...] = mine
                g = tmp[row:row + rows, :]
            else:
                c0, width = _ACC_COLS[key]
                g = rows_s[row:row + rows, c0:c0 + width][:, :w_ref.shape[1]]
            delta, mn, vn = _adamw(w_ref[...], g, m_ref[...], v_ref[...])
            g_out[...] = g
            d_out[...] = delta
            m_out[...] = mn
            v_out[...] = vn

    outs = [jax.ShapeDtypeStruct((1, 128), F32)]
    for w, _, _ in params:
        outs += [jax.ShapeDtypeStruct(w.shape, F32)] * 4
    scratch = [pltpu.VMEM((8, _ACC_WIDTH), F32), pltpu.VMEM((8, D_RNN // N_DEV), F32), pltpu.VMEM((8, D_FF // N_DEV), F32)]
    res = pl.pallas_call(body, name="small_update", out_shape=outs, scratch_shapes=scratch)(rows_all, gates_all, *flat)
    return res[0], [res[1 + 4 * i:5 + 4 * i] for i in range(n_par)]


def kernel(x, p, w_in, attn_sinks, rnn_conv_w, rnn_conv_b, gate_a_w, gate_a_b, gate_x_w, gate_x_b, lru_lambda, w_out, ln1_g, ln1_b, w_ffn_up, ffn_conv_w, ffn_conv_b, w_ffn_down, ple_gate_w, ple_gate_b, ple_proj, ln2_g, ln2_b, loss_target, m_w_in, m_attn_sinks, m_rnn_conv_w, m_rnn_conv_b, m_gate_a_w, m_gate_a_b, m_gate_x_w, m_gate_x_b, m_lru_lambda, m_w_out, m_ln1_g, m_ln1_b, m_w_ffn_up, m_ffn_conv_w, m_ffn_conv_b, m_w_ffn_down, m_ple_gate_w, m_ple_gate_b, m_ple_proj, m_ln2_g, m_ln2_b, v_w_in, v_attn_sinks, v_rnn_conv_w, v_rnn_conv_b, v_gate_a_w, v_gate_a_b, v_gate_x_w, v_gate_x_b, v_lru_lambda, v_w_out, v_ln1_g, v_ln1_b, v_w_ffn_up, v_ffn_conv_w, v_ffn_conv_b, v_w_ffn_down, v_ple_gate_w, v_ple_gate_b, v_ple_proj, v_ln2_g, v_ln2_b):
    from_col_blocks = lambda g: g.transpose(1, 0, 2).reshape(g.shape[1], N_DEV * g.shape[2])

    xs, ps, tgt, sinks = x[0], p[0, 0], loss_target[0], attn_sinks[0]
    wa, wx = _block_diag(gate_a_w[0]), _block_diag(gate_x_w[0])

    conv_cols = jnp.concatenate([rnn_conv_w[0].reshape(1, -1), ffn_conv_w[0].reshape(1, -1)], axis=1)
    n_rc, n_fc = 4 * D_RNN // N_DEV, 3 * D_FF // N_DEV
    ((g_in,),) = _comm_call([_Gather([w_in[0].T.astype(BF16)])], "gather_w_in")
    w_in_full = g_in.reshape(D_IN, D_MODEL)

    (q, k, v, xr, gr), (g_conv,) = _in_proj(xs, w_in_full, comm=_Bcast([jnp.broadcast_to(conv_cols, (8, n_rc + n_fc))]))
    rcw = from_col_blocks(g_conv[:, 0, :n_rc].reshape(N_DEV, 4, D_RNN // N_DEV))
    fcw = from_col_blocks(g_conv[:, 0, n_rc:].reshape(N_DEV, 3, D_FF // N_DEV))
    (att, lse), (g_out, g_down) = _attn_fwd(q, k, v, sinks,
                                            comm=_Gather([w_out[0].astype(BF16), w_ffn_down[0].astype(BF16)]))
    (rec, h, *kept), (w_up,) = _rnn_fwd(xr, gr, rcw, rnn_conv_b, wa, wx, gate_a_b, gate_x_b, lru_lambda,
                                        comm=_Gather([w_ffn_up[0].astype(BF16)]))
    w_out_full = g_out.reshape(D_MODEL, D_MODEL)
    (z1, h1, h1b, gate, act, gl, vdgl), (g_pg, g_pp) = _mix_ln1_up(
        xs, att, rec, w_out_full, ln1_g, ln1_b, w_up, fcw, ffn_conv_b,
        comm=_Gather([ple_gate_w[0].astype(BF16), ple_proj[0].astype(BF16)]))
    dz2b, dpreb, dppb, dgc, dval, dh1p, acc_t = _tail(
        act, gl, vdgl, h1, h1b, ps, tgt, g_down.reshape(D_FF, D_MODEL), g_pg.reshape(D_MODEL, D_MODEL), ple_gate_b,
        from_col_blocks(g_pp), ln2_g, ln2_b)

    gd_down = _weight_grad([dz2b], [act], "down_grad", "rows_t")
    gd_pg = _weight_grad([h1b], [dpreb], "pg_grad", "rows", ts=1024)
    gd_pp = _weight_grad([ps], [dppb], "pp_grad", "cols", ts=1024)
    (dgate, dz1, dz1b, datt, drec, acc_f, acc_d), (r_down, r_pg, r_pp) = _up_bwd(
        dgc, gate, dval, dh1p, z1, w_up, fcw, w_out_full, ln1_g, comm=_Exchange([gd_down, gd_pg, gd_pp]))
    gd_up_top, gd_up_bot = _weight_grad([h1b], [dgate, dval], "up_grad", "cols", halves=True)
    gd_out = _weight_grad([att, rec], [dz1b], "out_grad", "rows", ts=1024)
    (dq, dkc, dkp, dvc, dvp, acc_s), (r_up_top,) = _attn_bwd(q, k, v, lse, datt, sinks, comm=_Exchange([gd_up_top]))
    (dxr, dgr, g_wa, g_wx, acc_r), (r_up_bot, r_out) = _rnn_bwd(xr, gr, h, kept, drec, rcw, wa, wx, lru_lambda,
                                                                comm=_Exchange([gd_up_bot, gd_out]))
    (du, dx), _ = _in_bwd(dq, dkc, dkp, dvc, dvp, dxr, dgr, dz1, w_in_full)
    acc_rows = jnp.concatenate([acc_t, acc_f, acc_d, acc_s, acc_r], axis=1)
    gd_in_a, small_parts = _weight_grad([du], [xs], "in_grad_a", "rows", ts=1024, b_window=(0, D_MODEL // 2),
                                        comm=_Bcast([acc_rows, jnp.concatenate([g_wa, g_wx], axis=1)]))
    gd_in_b, (r_in_a,) = _weight_grad([du], [xs], "in_grad_b", "rows", ts=1024, b_window=(1, D_MODEL // 2),
                                      comm=_Exchange([gd_in_a]))
    ((r_in_b,),) = _comm_call([_Exchange([gd_in_b])], "exchange_w_in")
    r_in = jnp.concatenate([r_in_a, r_in_b], axis=2)

    outs = {}
    res = _sum_adamw(r_in, w_in[0].T, m_w_in[0].T, v_w_in[0].T, "adamw_w_in")
    outs["w_in"] = [r.T[None] for r in res]
    for name, parts, w, m, v in [("w_out", r_out, w_out, m_w_out, v_w_out),
                                 ("w_ffn_up", [r_up_top, r_up_bot], w_ffn_up, m_w_ffn_up, v_w_ffn_up),
                                 ("w_ffn_down", r_down, w_ffn_down, m_w_ffn_down, v_w_ffn_down),
                                 ("ple_gate_w", r_pg, ple_gate_w, m_ple_gate_w, v_ple_gate_w),
                                 ("ple_proj", r_pp, ple_proj, m_ple_proj, v_ple_proj)]:
        res = _sum_adamw(parts, w[0], m[0], v[0], "adamw_" + name)
        outs[name] = [r[None] for r in res]

    given = dict(attn_sinks=(attn_sinks, m_attn_sinks, v_attn_sinks), rnn_conv_w=(rnn_conv_w, m_rnn_conv_w, v_rnn_conv_w),
                 rnn_conv_b=(rnn_conv_b, m_rnn_conv_b, v_rnn_conv_b), gate_a_w=(gate_a_w, m_gate_a_w, v_gate_a_w),
                 gate_a_b=(gate_a_b, m_gate_a_b, v_gate_a_b), gate_x_w=(gate_x_w, m_gate_x_w, v_gate_x_w),
                 gate_x_b=(gate_x_b, m_gate_x_b, v_gate_x_b), lru_lambda=(lru_lambda, m_lru_lambda, v_lru_lambda),
                 ln1_g=(ln1_g, m_ln1_g, v_ln1_g), ln1_b=(ln1_b, m_ln1_b, v_ln1_b),
                 ffn_conv_w=(ffn_conv_w, m_ffn_conv_w, v_ffn_conv_w), ffn_conv_b=(ffn_conv_b, m_ffn_conv_b, v_ffn_conv_b),
                 ple_gate_b=(ple_gate_b, m_ple_gate_b, v_ple_gate_b), ln2_g=(ln2_g, m_ln2_g, v_ln2_g),
                 ln2_b=(ln2_b, m_ln2_b, v_ln2_b))
    as_2d = lambda a: a.reshape(-1, a.shape[-1])
    loss_row, small_res = _small_update(*small_parts, [tuple(as_2d(a) for a in given[n]) for n, *_ in _SMALL])
    loss = loss_row[0, 0]
    for (n, *_), res in zip(_SMALL, small_res):
        outs[n] = [r.reshape(given[n][0].shape) for r in res]

    order = ["w_in", "attn_sinks", "rnn_conv_w", "rnn_conv_b", "gate_a_w", "gate_a_b", "gate_x_w", "gate_x_b",
             "lru_lambda", "w_out", "ln1_g", "ln1_b", "w_ffn_up", "ffn_conv_w", "ffn_conv_b", "w_ffn_down",
             "ple_gate_w", "ple_gate_b", "ple_proj", "ln2_g", "ln2_b"]
    return (loss, dx[None], *[outs[n][0] for n in order], *[outs[n][1] for n in order],
            *[outs[n][2] for n in order], *[outs[n][3] for n in order])
```

```python
import jax
import jax.numpy as jnp
from jax import lax
from jax.experimental import pallas as pl
from jax.experimental.pallas import tpu as pltpu

F32 = jnp.float32
BF16 = jnp.bfloat16

D_MODEL = 1024
D_ATT = 512
D_KV = 128
HEAD_DIM = 64
N_HEADS = 8
N_KV = 2
D_RNN = 512
RNN_BLOCKS = 8
D_IN = 1792
D_FF = 3072
PLE_DIM = 256
QBLK = 128
N_DEV = 8
ALPHA = float(2 ** 0.25)
LN_EPS = 1e-5
LRU_C = 8.0
ADAM_LR, ADAM_B1, ADAM_B2, ADAM_EPS, ADAM_WD, ADAM_STEP = 0.001, 0.9, 0.999, 1e-08, 0.01, 10

V7X_VMEM_LIMIT = 56 * 1024 * 1024
MESH = pl.DeviceIdType.MESH


def _params(*sem, vmem=V7X_VMEM_LIMIT):
    return pltpu.CompilerParams(dimension_semantics=sem or None, vmem_limit_bytes=vmem)


def _resident(shape):
    return pl.BlockSpec(shape, lambda *_: (0,) * len(shape), pipeline_mode=pl.Buffered(1))


def _rows(tb, cols):
    return pl.BlockSpec((tb, cols), lambda i: (i, 0))


def _acc(shape):
    return pl.BlockSpec(shape, lambda *_: (0,) * len(shape))


def _dot(a, b):
    return jnp.dot(a, b, preferred_element_type=F32)


def _dot_nt(a, b):
    return lax.dot_general(a, b, (((1,), (1,)), ((), ())), preferred_element_type=F32)


def _dot_tn(a, b):
    return lax.dot_general(a, b, (((0,), (0,)), ((), ())), preferred_element_type=F32)


def _sigmoid(x):
    return 1.0 / (1.0 + jnp.exp(-x))


_GELU_C = 0.7978845608028654
_GELU_K = 0.044715


def _gelu_and_grad(x):
    u = x * x
    t = jnp.tanh(x * (_GELU_C + (_GELU_C * _GELU_K) * u))
    hp = 0.5 + 0.5 * t
    dg = hp + x * (0.5 - 0.5 * (t * t)) * (_GELU_C + (3.0 * _GELU_C * _GELU_K) * u)
    return x * hp, dg


def _gelu(x):
    return 0.5 * x * (1.0 + jnp.tanh(_GELU_C * (x + _GELU_K * x * x * x)))


def _ln_stats(z):
    mu = jnp.mean(z, axis=-1, keepdims=True)
    zc = z - mu
    var = jnp.mean(zc * zc, axis=-1, keepdims=True)
    rstd = lax.rsqrt(var + LN_EPS)
    return zc * rstd, rstd


def _ln_bwd(dy, xhat, rstd, g):
    dxh = dy * g
    m1 = jnp.mean(dxh, axis=-1, keepdims=True)
    m2 = jnp.mean(dxh * xhat, axis=-1, keepdims=True)
    return rstd * (dxh - m1 - xhat * m2)


def _softplus_neg(lam):
    u = jnp.exp(-jnp.abs(lam))
    w = 1.0 + u
    d = w - 1.0
    log1p_u = jnp.where(d == 0.0, u, jnp.log(w) * (u / jnp.where(d == 0.0, 1.0, d)))
    return jnp.maximum(-lam, 0.0) + log1p_u


def _shift_down(x, halo, s):
    xs = pltpu.roll(x, s, 0)
    hs = pltpu.roll(halo, s, 0)
    row8 = lax.broadcasted_iota(jnp.int32, hs.shape, 0)
    first = jnp.where(row8 < s, hs, xs[:8])
    return jnp.concatenate([first, xs[8:]], axis=0)


def _shift_up(x, halo, s):
    n = x.shape[0]
    xs = pltpu.roll(x, n - s, 0)
    hs = pltpu.roll(halo, 8 - s, 0)
    row8 = lax.broadcasted_iota(jnp.int32, hs.shape, 0)
    last = jnp.where(row8 >= 8 - s, hs, xs[n - 8:])
    return jnp.concatenate([xs[:n - 8], last], axis=0)


def _row_sum(x):
    return jnp.sum(x, axis=0, keepdims=True)


def _put_rows(acc_ref, rows):
    row8 = lax.broadcasted_iota(jnp.int32, acc_ref.shape, 0)
    upd = jnp.zeros(acc_ref.shape, F32)
    for r, vec in enumerate(rows):
        upd = jnp.where(row8 == r, vec, upd)
    acc_ref[...] += upd


def _place():
    return lax.axis_index("x"), lax.axis_index("y"), lax.axis_index("c")


def _dev_index(px, py, pc):
    return 4 * px + 2 * py + pc


_ANY = pl.BlockSpec(memory_space=pl.ANY)


class _Gather:
    def __init__(self, arrays):
        self.arrays = list(arrays)
        self.n = len(self.arrays)

    def out_shape(self):
        return [jax.ShapeDtypeStruct((N_DEV,) + s.shape, s.dtype) for s in self.arrays]

    def scratch(self):
        return [pltpu.SemaphoreType.DMA((self.n, 7)), pltpu.SemaphoreType.DMA((self.n, 7)),
                pltpu.SemaphoreType.DMA((self.n,))]

    def _parts(self, ins, outs, sems):
        send_sems, recv_sems, local_sems = sems
        x, y, c = _place()
        me, sibling = (x, y, c), (x, y, 1 - c)
        chips = [(1 - x, y), (x, 1 - y), (1 - x, 1 - y)]

        def copy(a, k, block, to, src=None):
            rows = outs[a].at[_dev_index(*block)]
            return pltpu.make_async_remote_copy(
                src_ref=rows if src is None else src, dst_ref=rows, send_sem=send_sems.at[a, k],
                recv_sem=recv_sems.at[a, k], device_id=to, device_id_type=MESH)

        rng = range(self.n)
        mine = [pltpu.make_async_copy(ins[a], outs[a].at[_dev_index(*me)], local_sems.at[a]) for a in rng]
        first = [copy(a, 0, me, sibling, src=ins[a]) for a in rng]
        first += [copy(a, 1 + j, me, (*chip, c), src=ins[a]) for j, chip in enumerate(chips) for a in rng]
        landed = [copy(a, 1 + j, (*chip, c), me) for j, chip in enumerate(chips) for a in rng]
        passed = [copy(a, 4 + j, (*chip, c), sibling) for j, chip in enumerate(chips) for a in rng]
        from_sibling = [copy(a, 0, sibling, me) for a in rng]
        from_sibling += [copy(a, 4 + j, (*chip, 1 - c), me) for j, chip in enumerate(chips) for a in rng]
        return mine, first, landed, passed, from_sibling

    def start(self, ins, outs, sems):
        mine, first, _, _, _ = self._parts(ins, outs, sems)
        for cp in mine + first:
            cp.start()

    def forward(self, ins, outs, sems):
        _, _, landed, passed, _ = self._parts(ins, outs, sems)
        for got, fwd in zip(landed, passed):
            got.wait_recv()
            fwd.start()

    def finish(self, ins, outs, sems):
        mine, first, _, passed, from_sibling = self._parts(ins, outs, sems)
        for cp in from_sibling:
            cp.wait_recv()
        for cp in first + passed:
            cp.wait_send()
        for cp in mine:
            cp.wait()

    def before(self, ins, outs, sems, step, nsteps):
        pl.when(step == 0)(lambda: self.start(ins, outs, sems))
        pl.when(step == (7 * nsteps) // 8)(lambda: self.forward(ins, outs, sems))

    def after(self, ins, outs, sems, step, nsteps):
        pl.when(step == nsteps - 1)(lambda: self.finish(ins, outs, sems))


class _Exchange:
    def __init__(self, arrays):
        self.arrays = list(arrays)
        self.n = len(self.arrays)

    def out_shape(self):
        return [jax.ShapeDtypeStruct(b.shape, b.dtype) for b in self.arrays]

    def scratch(self):
        return [pltpu.SemaphoreType.DMA((self.n, 7)), pltpu.SemaphoreType.DMA((self.n, 7)),
                pltpu.SemaphoreType.DMA((self.n,))]

    def _parts(self, ins, outs, sems):
        send_sems, recv_sems, local_sems = sems
        x, y, c = _place()
        me = _dev_index(x, y, c)
        peers = [(x ^ (k >> 2), y ^ ((k >> 1) & 1), c ^ (k & 1)) for k in range(1, N_DEV)]
        rng = range(self.n)
        mine = [pltpu.make_async_copy(ins[a].at[me], outs[a].at[me], local_sems.at[a]) for a in rng]
        sent = [pltpu.make_async_remote_copy(
            src_ref=ins[a].at[_dev_index(*to)], dst_ref=outs[a].at[me], send_sem=send_sems.at[a, k],
            recv_sem=recv_sems.at[a, k], device_id=to, device_id_type=MESH) for k, to in enumerate(peers) for a in rng]
        arrivals = [pltpu.make_async_remote_copy(
            src_ref=ins[a].at[me], dst_ref=outs[a].at[_dev_index(*frm)], send_sem=send_sems.at[a, k],
            recv_sem=recv_sems.at[a, k], device_id=frm, device_id_type=MESH) for k, frm in enumerate(peers) for a in rng]
        return mine, sent, arrivals

    def start(self, ins, outs, sems):
        mine, sent, _ = self._parts(ins, outs, sems)
        for cp in mine + sent:
            cp.start()

    def finish(self, ins, outs, sems):
        mine, sent, arrivals = self._parts(ins, outs, sems)
        for cp in arrivals:
            cp.wait_recv()
        for cp in sent:
            cp.wait_send()
        for cp in mine:
            cp.wait()

    def before(self, ins, outs, sems, step, nsteps):
        pl.when(step == 0)(lambda: self.start(ins, outs, sems))

    def after(self, ins, outs, sems, step, nsteps):
        pl.when(step == nsteps - 1)(lambda: self.finish(ins, outs, sems))


class _Bcast(_Exchange):
    def out_shape(self):
        return [jax.ShapeDtypeStruct((N_DEV,) + s.shape, s.dtype) for s in self.arrays]

    def _parts(self, ins, outs, sems):
        send_sems, recv_sems, local_sems = sems
        x, y, c = _place()
        me = _dev_index(x, y, c)
        peers = [(x ^ (k >> 2), y ^ ((k >> 1) & 1), c ^ (k & 1)) for k in range(1, N_DEV)]
        rng = range(self.n)
        mine = [pltpu.make_async_copy(ins[a], outs[a].at[me], local_sems.at[a]) for a in rng]
        sent = [pltpu.make_async_remote_copy(
            src_ref=ins[a], dst_ref=outs[a].at[me], send_sem=send_sems.at[a, k], recv_sem=recv_sems.at[a, k],
            device_id=to, device_id_type=MESH) for k, to in enumerate(peers) for a in rng]
        arrivals = [pltpu.make_async_remote_copy(
            src_ref=ins[a], dst_ref=outs[a].at[_dev_index(*frm)], send_sem=send_sems.at[a, k],
            recv_sem=recv_sems.at[a, k], device_id=frm, device_id_type=MESH) for k, frm in enumerate(peers) for a in rng]
        return mine, sent, arrivals


class _Multi:
    def __init__(self, comms):
        self.comms = list(comms)
        self.arrays = [arr for c in self.comms for arr in c.arrays]
        self.n = len(self.arrays)

    def out_shape(self):
        return [s for c in self.comms for s in c.out_shape()]

    def scratch(self):
        return [s for c in self.comms for s in c.scratch()]

    def _each(self, ins, outs, sems):
        a = 0
        for j, c in enumerate(self.comms):
            yield c, ins[a:a + c.n], outs[a:a + c.n], sems[3 * j:3 * j + 3]
            a += c.n

    def before(self, ins, outs, sems, step, nsteps):
        for c, ci, co, cs in self._each(ins, outs, sems):
            c.before(ci, co, cs, step, nsteps)

    def after(self, ins, outs, sems, step, nsteps):
        for c, ci, co, cs in self._each(ins, outs, sems):
            c.after(ci, co, cs, step, nsteps)


def _comm_call(comms, name):
    ns = [c.n for c in comms]
    n = sum(ns)

    def body(*refs):
        parts, a, s = [], 0, 2 * n
        for c in comms:
            parts.append((c, refs[a:a + c.n], refs[n + a:n + a + c.n], refs[s:s + 3]))
            a, s = a + c.n, s + 3
        for c, ins, outs, sems in parts:
            c.start(ins, outs, sems)
        for c, ins, outs, sems in parts:
            if isinstance(c, _Gather):
                c.forward(ins, outs, sems)
        for c, ins, outs, sems in parts:
            c.finish(ins, outs, sems)

    res = pl.pallas_call(
        body, name=name, in_specs=[_ANY] * n, out_specs=[_ANY] * n,
        out_shape=[s for c in comms for s in c.out_shape()], scratch_shapes=[s for c in comms for s in c.scratch()],
    )(*[arr for c in comms for arr in c.arrays])
    out, a = [], 0
    for k in ns:
        out.append(res[a:a + k])
        a += k
    return out


def _pcall(body, args, *, name, grid, in_specs, out_specs, out_shape, scratch_shapes=(), sem="parallel", comm=None,
           step_axis=0):
    sem = (sem,) * len(grid) if isinstance(sem, str) else sem
    if comm is None:
        res = pl.pallas_call(body, name=name, grid=grid, in_specs=in_specs, out_specs=out_specs, out_shape=out_shape,
                             scratch_shapes=list(scratch_shapes), compiler_params=_params(*sem))(*args)
        return res, []
    n_in, n_out, n_scr, n = len(in_specs), len(out_specs), len(scratch_shapes), comm.n
    nsteps = grid[step_axis]
    assert all(g == 1 for ax, g in enumerate(grid) if ax != step_axis)

    def hosted(*refs):
        ins, cin = refs[:n_in], refs[n_in:n_in + n]
        o0 = n_in + n
        outs, cout = refs[o0:o0 + n_out], refs[o0 + n_out:o0 + n_out + n]
        s0 = o0 + n_out + n
        scr, sems = refs[s0:s0 + n_scr], refs[s0 + n_scr:]
        step = pl.program_id(step_axis)
        comm.before(cin, cout, sems, step, nsteps)
        body(*ins, *outs, *scr)
        comm.after(cin, cout, sems, step, nsteps)

    res = pl.pallas_call(
        hosted, name=name, grid=grid, in_specs=list(in_specs) + [_ANY] * n, out_specs=list(out_specs) + [_ANY] * n,
        out_shape=list(out_shape) + comm.out_shape(), scratch_shapes=list(scratch_shapes) + comm.scratch(),
        compiler_params=_params(*(("arbitrary",) * len(grid))))(*args, *comm.arrays)
    return res[:n_out], res[n_out:]


def _in_proj(x, w_in_t, comm=None):
    S = x.shape[0]
    tb = min(512, S)

    def body(x_ref, w_ref, q_ref, k_ref, v_ref, xr_ref, gr_ref):
        u = _dot_nt(x_ref[...].astype(BF16), w_ref[...])
        q_ref[...] = (u[:, :D_ATT] * (HEAD_DIM ** -0.5)).astype(BF16)
        k_ref[...] = u[:, D_ATT:D_ATT + D_KV].astype(BF16)
        v_ref[...] = u[:, D_ATT + D_KV:D_ATT + 2 * D_KV].astype(BF16)
        xr_ref[...] = u[:, D_ATT + 2 * D_KV:D_ATT + 2 * D_KV + D_RNN]
        gr_ref[...] = u[:, D_ATT + 2 * D_KV + D_RNN:]

    return _pcall(
        body, (x, w_in_t), name="in_proj", grid=(S // tb,), comm=comm,
        in_specs=[_rows(tb, D_MODEL), _resident((D_IN, D_MODEL))],
        out_specs=[_rows(tb, D_ATT), _rows(tb, D_KV), _rows(tb, D_KV), _rows(tb, D_RNN), _rows(tb, D_RNN)],
        out_shape=[jax.ShapeDtypeStruct((S, D_ATT), BF16), jax.ShapeDtypeStruct((S, D_KV), BF16),
                   jax.ShapeDtypeStruct((S, D_KV), BF16), jax.ShapeDtypeStruct((S, D_RNN), F32),
                   jax.ShapeDtypeStruct((S, D_RNN), F32)])


GROUP = N_HEADS // N_KV


def _band_mask(i):
    qi = lax.broadcasted_iota(jnp.int32, (GROUP * QBLK, 2 * QBLK), 0) & (QBLK - 1)
    sj = lax.broadcasted_iota(jnp.int32, (GROUP * QBLK, 2 * QBLK), 1)
    return (sj > qi) & (sj <= qi + QBLK) & ((sj >= QBLK) | (i > 0))


def _stack_heads(x, g):
    return jnp.concatenate([x[:, (g * GROUP + hh) * HEAD_DIM:(g * GROUP + hh + 1) * HEAD_DIM] for hh in range(GROUP)],
                           axis=0)


def _unstack_heads(x4):
    return [x4[hh * QBLK:(hh + 1) * QBLK] for hh in range(GROUP)]


def _sink_column(sink_ref, g):
    head = lax.broadcasted_iota(jnp.int32, (GROUP * QBLK, 1), 0) // QBLK
    col = jnp.full((GROUP * QBLK, 1), sink_ref[g * GROUP], F32)
    for hh in range(1, GROUP):
        col = jnp.where(head == hh, sink_ref[g * GROUP + hh], col)
    return col


ATT_STEP = 4


def _attn_specs(nq=1):
    cur = lambda i: (i, 0)
    prev = lambda i: (jnp.maximum(nq * i - 1, 0), 0)
    return [pl.BlockSpec((nq * QBLK, D_KV), cur), pl.BlockSpec((QBLK, D_KV), prev),
            pl.BlockSpec((nq * QBLK, D_KV), cur), pl.BlockSpec((QBLK, D_KV), prev)]


def _attn_fwd(q, k, v, sinks, comm=None):
    S = q.shape[0]
    nq = min(ATT_STEP, S // QBLK)

    def body(sink_ref, q_ref, kc_ref, kp_ref, vc_ref, vp_ref, o_ref, lse_ref):
        first = pl.program_id(0) * nq
        kall = jnp.concatenate([kp_ref[...], kc_ref[...]], axis=0)
        vall = jnp.concatenate([vp_ref[...], vc_ref[...]], axis=0)
        for b in range(nq):
            valid = _band_mask(first + b)
            rows = slice(b * QBLK, (b + 1) * QBLK)
            keys = slice(b * QBLK, (b + 2) * QBLK)
            qv = q_ref[rows, :]
            outs = []
            for g in range(N_KV):
                kcat = kall[keys, g * HEAD_DIM:(g + 1) * HEAD_DIM]
                vcat = vall[keys, g * HEAD_DIM:(g + 1) * HEAD_DIM]
                s = jnp.where(valid, _dot_nt(_stack_heads(qv, g), kcat), -1e30)
                sink = _sink_column(sink_ref, g)
                m = jnp.maximum(jnp.max(s, axis=1, keepdims=True), sink)
                p = jnp.exp(s - m)
                l = jnp.sum(p, axis=1, keepdims=True) + jnp.exp(sink - m)
                outs += _unstack_heads(_dot(p.astype(BF16), vcat) / l)
                lse_ref[(b * N_KV + g) * GROUP * QBLK:(b * N_KV + g + 1) * GROUP * QBLK, :] = m + jnp.log(l)
            o_ref[rows, :] = jnp.concatenate(outs, axis=1).astype(BF16)

    lse_rows = nq * N_HEADS * QBLK
    return _pcall(
        body, (sinks, q, k, k, v, v), name="attn_fwd", grid=(S // (nq * QBLK),), comm=comm,
        in_specs=[pl.BlockSpec(memory_space=pltpu.SMEM), _rows(nq * QBLK, D_ATT)] + _attn_specs(nq),
        out_specs=[_rows(nq * QBLK, D_ATT), _rows(lse_rows, 1)],
        out_shape=[jax.ShapeDtypeStruct((S, D_ATT), BF16), jax.ShapeDtypeStruct((S * N_HEADS, 1), F32)])


def _w_rows(w_ref):
    return [w_ref[k:k + 1, :] for k in range(w_ref.shape[0])]


def _conv4(x, halo, w, b):
    y = b + w[3] * x
    for s in (1, 2, 3):
        y = y + w[3 - s] * _shift_down(x, halo, s)
    return y


def _rnn_gates(xc, wa, wx, ba, bx, sp):
    xcb = xc.astype(BF16)
    r = _sigmoid(_dot(xcb, wa) + ba)
    ig = _sigmoid(_dot(xcb, wx) + bx)
    la = -LRU_C * r * sp
    a = jnp.exp(la)
    t = jnp.tanh(la)
    f = jnp.sqrt(-2.0 * t / (1.0 - t))
    return r, ig, a, f


def _rnn_fwd(xr, gr, conv_w, conv_b, wa, wx, ba, bx, lam, comm=None):
    S = xr.shape[0]
    tb = min(256, S)

    def body(xr_ref, gr_ref, cw_ref, cb_ref, wa_ref, wx_ref, ba_ref, bx_ref, lam_ref, rec_ref, h_ref,
             xc_ref, r_ref, ig_ref, a_ref, f_ref, halo_s, hc_s, a_s, b_s):
        @pl.when(pl.program_id(0) == 0)
        def _():
            halo_s[...] = jnp.zeros_like(halo_s)
            hc_s[...] = jnp.zeros_like(hc_s)

        x = xr_ref[...]
        xc = _conv4(x, halo_s[...], _w_rows(cw_ref), cb_ref[...])
        halo_s[...] = x[tb - 8:]
        r, ig, a, f = _rnn_gates(xc, wa_ref[...], wx_ref[...], ba_ref[...], bx_ref[...], _softplus_neg(lam_ref[...]))
        xc_ref[...] = xc
        r_ref[...] = r
        ig_ref[...] = ig
        a_ref[...] = a
        f_ref[...] = f
        a_s[...] = a
        b_s[...] = f * ig * xc
        row8 = lax.broadcasted_iota(jnp.int32, (8, D_RNN), 0)

        def tile(t, hc):
            o = pl.multiple_of(t * 8, 8)
            at = a_s[pl.ds(o, 8), :]
            bt = b_s[pl.ds(o, 8), :]
            for s in (1, 2, 4):
                keep = row8 >= s
                a_sh = jnp.where(keep, pltpu.roll(at, s, 0), 1.0)
                b_sh = jnp.where(keep, pltpu.roll(bt, s, 0), 0.0)
                bt = at * b_sh + bt
                at = at * a_sh
            ht = at * hc + bt
            b_s[pl.ds(o, 8), :] = ht
            return _row_sum(jnp.where(row8 == 7, ht, 0.0))

        hc_s[0:1, :] = lax.fori_loop(0, tb // 8, tile, hc_s[0:1, :], unroll=2)
        h = b_s[...]
        h_ref[...] = h
        rec_ref[...] = (h * _gelu(gr_ref[...])).astype(BF16)

    vec = _resident((1, D_RNN))
    kept = jax.ShapeDtypeStruct((S, D_RNN), F32)
    return _pcall(
        body, (xr, gr, conv_w, conv_b, wa, wx, ba, bx, lam), name="rnn_fwd", grid=(S // tb,), sem="arbitrary", comm=comm,
        in_specs=[_rows(tb, D_RNN), _rows(tb, D_RNN), _resident((4, D_RNN)), vec,
                  _resident((D_RNN, D_RNN)), _resident((D_RNN, D_RNN)), vec, vec, vec],
        out_specs=[_rows(tb, D_RNN)] * 7,
        out_shape=[jax.ShapeDtypeStruct((S, D_RNN), BF16), kept, kept, kept, kept, kept, kept],
        scratch_shapes=[pltpu.VMEM((8, D_RNN), F32), pltpu.VMEM((8, D_RNN), F32),
                        pltpu.VMEM((tb, D_RNN), F32), pltpu.VMEM((tb, D_RNN), F32)])


def _mix_ln1_up(x, att, rec, w_out, ln1_g, ln1_b, w_up, fcw, fcb, comm=None):
    S = x.shape[0]
    tb = min(256, S)
    nblk, _, wblk = w_up.shape
    half = nblk // 2

    def body(x_ref, att_ref, rec_ref, wo_ref, g_ref, b_ref, wu_ref, fcw_ref, fcb_ref,
             z1_ref, h1_ref, h1b_ref, gate_ref, act_ref, gl_ref, vdgl_ref, halo_s):
        @pl.when(pl.program_id(0) == 0)
        def _():
            halo_s[...] = jnp.zeros_like(halo_s)

        z1 = ALPHA * x_ref[...] + _dot(att_ref[...], wo_ref[:D_ATT, :]) + _dot(rec_ref[...], wo_ref[D_ATT:, :])
        z1_ref[...] = z1
        xhat, _ = _ln_stats(z1)
        h1 = xhat * g_ref[...] + b_ref[...]
        h1_ref[...] = h1
        h1b = h1.astype(BF16)
        h1b_ref[...] = h1b
        for jj in range(half):
            cols = slice(jj * wblk, (jj + 1) * wblk)
            gate = _dot(h1b, wu_ref[jj])
            val = _dot(h1b, wu_ref[jj + half])
            halo = halo_s[:, cols]
            conv = (fcb_ref[:, cols] + fcw_ref[2:3, cols] * gate + fcw_ref[1:2, cols] * _shift_down(gate, halo, 1)
                    + fcw_ref[0:1, cols] * _shift_down(gate, halo, 2))
            halo_s[:, cols] = gate[tb - 8:]
            gl, dgl = _gelu_and_grad(conv)
            gate_ref[:, cols] = gate.astype(BF16)
            act_ref[:, cols] = (gl * val).astype(BF16)
            gl_ref[:, cols] = gl.astype(BF16)
            vdgl_ref[:, cols] = (val * dgl).astype(BF16)

    vec = _resident((1, D_MODEL))
    wide = jax.ShapeDtypeStruct((S, D_FF), BF16)
    return _pcall(
        body, (x, att, rec, w_out, ln1_g, ln1_b, w_up, fcw, fcb), name="mix_ln1_up", grid=(S // tb,),
        sem="arbitrary", comm=comm,
        in_specs=[_rows(tb, D_MODEL), _rows(tb, D_ATT), _rows(tb, D_RNN), _resident((D_MODEL, D_MODEL)), vec, vec,
                  _resident(w_up.shape), _resident((3, D_FF)), _resident((1, D_FF))],
        out_specs=[_rows(tb, D_MODEL), _rows(tb, D_MODEL), _rows(tb, D_MODEL)] + [_rows(tb, D_FF)] * 4,
        out_shape=[jax.ShapeDtypeStruct((S, D_MODEL), F32), jax.ShapeDtypeStruct((S, D_MODEL), F32),
                   jax.ShapeDtypeStruct((S, D_MODEL), BF16), wide, wide, wide, wide],
        scratch_shapes=[pltpu.VMEM((8, D_FF), F32)])


def _tail(act, gl, vdgl, h1, h1b, p, tgt, w_down, w_pg, b_pg, w_pp, ln2_g, ln2_b):
    S = h1.shape[0]
    tb = min(256, S)

    def body(act_ref, gl_ref, vdgl_ref, h1_ref, h1b_ref, p_ref, t_ref, wd_ref, wpg_ref, bpg_ref, wpp_ref, g2_ref, b2_ref,
             dz2_ref, dpre_ref, dpp_ref, dgc_ref, dval_ref, dh1_ref, acc_ref):
        i = pl.program_id(0)

        @pl.when(i == 0)
        def _():
            acc_ref[...] = jnp.zeros_like(acc_ref)

        ffn = _dot(act_ref[...], wd_ref[...])
        h1 = h1_ref[...]
        sg = _sigmoid(_dot(h1b_ref[...], wpg_ref[...]) + bpg_ref[...])
        pp = _dot(p_ref[...].astype(BF16), wpp_ref[...])
        z2 = ALPHA * h1 + ffn + sg * pp
        xhat2, rstd2 = _ln_stats(z2)
        y = xhat2 * g2_ref[...] + b2_ref[...]
        err = y - t_ref[...]
        dy = err * (1.0 / D_MODEL)
        loss = 0.5 * jnp.sum(jnp.sum(err * err, axis=1, keepdims=True), axis=0, keepdims=True) * (1.0 / D_MODEL)
        dz2 = _ln_bwd(dy, xhat2, rstd2, g2_ref[...])
        dz2b = dz2.astype(BF16)
        dz2_ref[...] = dz2b
        dpre = dz2 * pp * sg * (1.0 - sg)
        dpreb = dpre.astype(BF16)
        dpre_ref[...] = dpreb
        dpp_ref[...] = (dz2 * sg).astype(BF16)
        dh1_ref[...] = ALPHA * dz2 + _dot_nt(dpreb, wpg_ref[...])
        dactb = _dot_nt(dz2b, wd_ref[...]).astype(BF16)
        dval_ref[...] = dactb * gl_ref[...]
        dgc_ref[...] = dactb * vdgl_ref[...]
        _put_rows(acc_ref, [_row_sum(dy * xhat2), _row_sum(dy), _row_sum(dpre),
                            jnp.broadcast_to(loss, (1, D_MODEL))])

    vec = _resident((1, D_MODEL))
    return pl.pallas_call(
        body, name="tail", grid=(S // tb,),
        in_specs=[_rows(tb, D_FF), _rows(tb, D_FF), _rows(tb, D_FF), _rows(tb, D_MODEL), _rows(tb, D_MODEL),
                  _rows(tb, PLE_DIM), _rows(tb, D_MODEL), _resident((D_FF, D_MODEL)), _resident((D_MODEL, D_MODEL)), vec,
                  _resident((PLE_DIM, D_MODEL)), vec, vec],
        out_specs=[_rows(tb, D_MODEL), _rows(tb, D_MODEL), _rows(tb, D_MODEL), _rows(tb, D_FF),
                   _rows(tb, D_FF), _rows(tb, D_MODEL), _acc((8, D_MODEL))],
        out_shape=[jax.ShapeDtypeStruct((S, D_MODEL), BF16),
                   jax.ShapeDtypeStruct((S, D_MODEL), BF16), jax.ShapeDtypeStruct((S, D_MODEL), BF16),
                   jax.ShapeDtypeStruct((S, D_FF), BF16), jax.ShapeDtypeStruct((S, D_FF), BF16),
                   jax.ShapeDtypeStruct((S, D_MODEL), F32), jax.ShapeDtypeStruct((8, D_MODEL), F32)],
        compiler_params=_params("arbitrary"),
    )(act, gl, vdgl, h1, h1b, p, tgt, w_down, w_pg, b_pg, w_pp, ln2_g, ln2_b)


def _weight_grad(a_list, b_list, name, layout, ts=512, comm=None, b_window=None, halves=False):
    S = a_list[0].shape[0]
    ms = [a.shape[1] for a in a_list]
    M, nb = sum(ms), len(b_list)
    win, Nb = b_window if b_window else (0, b_list[0].shape[1])
    ts = min(ts, S)
    nk = S // ts
    per_b = N_DEV // nb
    na = len(a_list)

    n_out = 2 if halves else 1
    assert layout == "cols" or not halves

    def body(*refs):
        a_refs, b_refs, o_refs, acc_ref = refs[:na], refs[na:na + nb], refs[na + nb:na + nb + n_out], refs[-1]
        o_ref = o_refs[0]
        j, k = pl.program_id(0), pl.program_id(1)

        @pl.when(k == 0)
        def _():
            acc_ref[...] = jnp.zeros_like(acc_ref)

        for jj in range(nb):
            @pl.when(j == jj)
            def _():
                b = b_refs[jj][...].astype(BF16)
                off = 0
                for a_ref, m in zip(a_refs, ms):
                    acc_ref[off:off + m, :] += _dot_tn(a_ref[...].astype(BF16), b)
                    off += m

        @pl.when(k == nk - 1)
        def _():
            for d in range(per_b):
                if layout == "rows":
                    o_ref[d] = acc_ref[d * (M // N_DEV):(d + 1) * (M // N_DEV), :].astype(BF16)
                elif layout == "cols" and halves:
                    for o_half, r0 in zip(o_refs, (0, M // 2)):
                        o_half[d] = acc_ref[r0:r0 + M // 2, d * (Nb // per_b):(d + 1) * (Nb // per_b)].astype(BF16)
                elif layout == "cols":
                    o_ref[d] = acc_ref[:, d * (Nb // per_b):(d + 1) * (Nb // per_b)].astype(BF16)
                else:
                    o_ref[d] = acc_ref[:, d * (Nb // per_b):(d + 1) * (Nb // per_b)].T.astype(BF16)

    def b_index(jj):
        return lambda j, k: (jnp.where(j == jj, k, jnp.where(j < jj, 0, nk - 1)), win)

    if layout == "rows":
        assert nb == 1
        blk = (N_DEV, M // N_DEV, Nb)
    elif layout == "cols":
        blk = (per_b, M // n_out, Nb // per_b)
    else:
        blk = (per_b, Nb // per_b, M)
    res, comm_res = _pcall(
        body, (*a_list, *b_list), name=name, grid=(nb, nk), sem="arbitrary", comm=comm, step_axis=1,
        in_specs=[pl.BlockSpec((ts, m), lambda j, k: (k, 0)) for m in ms]
        + [pl.BlockSpec((ts, Nb), b_index(jj)) for jj in range(nb)],
        out_specs=[pl.BlockSpec(blk, lambda j, k: (j, 0, 0))] * n_out,
        out_shape=[jax.ShapeDtypeStruct((N_DEV,) + blk[1:], BF16)] * n_out,
        scratch_shapes=[pltpu.VMEM((M, Nb), F32)])
    res = res if halves else res[0]
    return (res, comm_res) if comm is not None else res


def _up_bwd(dgc, gate, dval, dh1p, z1, w_up, fcw, w_out, ln1_g, comm=None):
    S = z1.shape[0]
    tb = min(256, S)
    t16 = tb // 16
    n16 = S // 16
    nblk, _, wblk = w_up.shape
    half = nblk // 2
    nsteps = S // tb

    def body(dgc_ref, dgn_ref, gc_ref, dval_ref, dh1p_ref, z1_ref, wu_ref, fcw_ref, wo_ref, g1_ref,
             dgate_ref, dz1_ref, dz1b_ref, datt_ref, drec_ref, accf_ref, accd_ref):
        i = pl.program_id(0)

        @pl.when(i == 0)
        def _():
            accf_ref[...] = jnp.zeros_like(accf_ref)
            accd_ref[...] = jnp.zeros_like(accd_ref)

        dg = dgc_ref[...].astype(F32)
        nxt = jnp.where(i < nsteps - 1, dgn_ref[...].astype(F32)[0:8], 0.0)
        w = _w_rows(fcw_ref)
        up1, up2 = _shift_up(dg, nxt, 1), _shift_up(dg, nxt, 2)
        dgate = (w[2] * dg + w[1] * up1 + w[0] * up2).astype(BF16)
        dgate_ref[...] = dgate
        gate = gc_ref[...].astype(F32)
        _put_rows(accf_ref, [_row_sum(up2 * gate), _row_sum(up1 * gate), _row_sum(dg * gate), _row_sum(dg)])

        dh1 = dh1p_ref[...]
        for j in range(nblk):
            src = dgate if j < half else dval_ref[...]
            jj = j % half
            dh1 = dh1 + _dot_nt(src[:, jj * wblk:(jj + 1) * wblk], wu_ref[j])
        xhat1, rstd1 = _ln_stats(z1_ref[...])
        dz1 = _ln_bwd(dh1, xhat1, rstd1, g1_ref[...])
        dz1_ref[...] = dz1
        dz1b = dz1.astype(BF16)
        dz1b_ref[...] = dz1b
        dcat = _dot_nt(dz1b, wo_ref[...])
        datt_ref[...] = dcat[:, :D_ATT].astype(BF16)
        drec_ref[...] = dcat[:, D_ATT:]
        _put_rows(accd_ref, [_row_sum(dh1 * xhat1), _row_sum(dh1)])

    next16 = pl.BlockSpec((16, D_FF), lambda i: (jnp.minimum((i + 1) * t16, n16 - 1), 0))
    return _pcall(
        body, (dgc, dgc, gate, dval, dh1p, z1, w_up, fcw, w_out, ln1_g), name="up_bwd",
        grid=(nsteps,), sem="arbitrary", comm=comm,
        in_specs=[_rows(tb, D_FF), next16, _rows(tb, D_FF), _rows(tb, D_FF), _rows(tb, D_MODEL),
                  _rows(tb, D_MODEL), _resident(w_up.shape), _resident((3, D_FF)),
                  _resident((D_MODEL, D_MODEL)), _resident((1, D_MODEL))],
        out_specs=[_rows(tb, D_FF), _rows(tb, D_MODEL), _rows(tb, D_MODEL), _rows(tb, D_ATT), _rows(tb, D_RNN),
                   _acc((8, D_FF)), _acc((8, D_MODEL))],
        out_shape=[jax.ShapeDtypeStruct((S, D_FF), BF16), jax.ShapeDtypeStruct((S, D_MODEL), F32),
                   jax.ShapeDtypeStruct((S, D_MODEL), BF16), jax.ShapeDtypeStruct((S, D_ATT), BF16),
                   jax.ShapeDtypeStruct((S, D_RNN), F32), jax.ShapeDtypeStruct((8, D_FF), F32),
                   jax.ShapeDtypeStruct((8, D_MODEL), F32)])


def _attn_bwd(q, k, v, lse, do, sinks, comm=None):
    S = q.shape[0]
    grp = N_HEADS // N_KV
    nq = min(ATT_STEP, S // QBLK)

    def body(sink_ref, q_ref, kc_ref, kp_ref, vc_ref, vp_ref, do_ref, lse_ref, dq_ref, dkc_ref, dkp_ref, dvc_ref, dvp_ref,
             ds_ref):
        i = pl.program_id(0)

        @pl.when(i == 0)
        def _():
            ds_ref[...] = jnp.zeros_like(ds_ref)

        row8 = lax.broadcasted_iota(jnp.int32, (8, 128), 0)
        lane8 = lax.broadcasted_iota(jnp.int32, (8, 128), 1)
        dsink = jnp.zeros((8, 128), F32)
        kall = jnp.concatenate([kp_ref[...], kc_ref[...]], axis=0)
        vall = jnp.concatenate([vp_ref[...], vc_ref[...]], axis=0)
        dk_t = [jnp.zeros((D_KV, QBLK), F32) for _ in range(nq + 1)]
        dv_t = [jnp.zeros((D_KV, QBLK), F32) for _ in range(nq + 1)]
        for b in range(nq):
            valid = _band_mask(i * nq + b)
            rows = slice(b * QBLK, (b + 1) * QBLK)
            keys = slice(b * QBLK, (b + 2) * QBLK)
            qv, dov = q_ref[rows, :], do_ref[rows, :]
            dqs, dks, dvs = [], [], []
            for g in range(N_KV):
                kcat = kall[keys, g * HEAD_DIM:(g + 1) * HEAD_DIM]
                vcat = vall[keys, g * HEAD_DIM:(g + 1) * HEAD_DIM]
                q4, do4 = _stack_heads(qv, g), _stack_heads(dov, g)
                s = jnp.where(valid, _dot_nt(q4, kcat), -1e30)
                lse = lse_ref[(b * N_KV + g) * GROUP * QBLK:(b * N_KV + g + 1) * GROUP * QBLK, :]
                p = jnp.exp(s - lse)
                p_sink = jnp.exp(_sink_column(sink_ref, g) - lse)
                dp = _dot_nt(do4, vcat)
                delta = jnp.sum(p * dp, axis=1, keepdims=True)
                dsc = (p * (dp - delta)).astype(BF16)
                dqs += _unstack_heads(_dot(dsc, kcat) * (HEAD_DIM ** -0.5))
                dks.append(_dot_tn(q4, dsc))
                dvs.append(_dot_tn(do4, p.astype(BF16)))
                for hh, part in enumerate(_unstack_heads(-p_sink * delta)):
                    here = (row8 == 0) & (lane8 == g * grp + hh)
                    dsink = dsink + jnp.where(here, jnp.sum(part, axis=0, keepdims=True), 0.0)
            dq_ref[rows, :] = jnp.concatenate(dqs, axis=1).astype(BF16)
            dk2, dv2 = jnp.concatenate(dks, axis=0), jnp.concatenate(dvs, axis=0)
            dk_t[b], dk_t[b + 1] = dk_t[b] + dk2[:, :QBLK], dk_t[b + 1] + dk2[:, QBLK:]
            dv_t[b], dv_t[b + 1] = dv_t[b] + dv2[:, :QBLK], dv_t[b + 1] + dv2[:, QBLK:]
        dkp_ref[...] = dk_t[0].T
        dvp_ref[...] = dv_t[0].T
        for b in range(nq):
            dkc_ref[b * QBLK:(b + 1) * QBLK, :] = dk_t[b + 1].T
            dvc_ref[b * QBLK:(b + 1) * QBLK, :] = dv_t[b + 1].T
        ds_ref[...] += dsink

    nsteps = S // (nq * QBLK)
    cur = jax.ShapeDtypeStruct((S, D_KV), F32)
    prev = jax.ShapeDtypeStruct((nsteps * QBLK, D_KV), F32)
    big = _rows(nq * QBLK, D_ATT)
    return _pcall(
        body, (sinks, q, k, k, v, v, do, lse), name="attn_bwd", grid=(nsteps,), sem="arbitrary", comm=comm,
        in_specs=[pl.BlockSpec(memory_space=pltpu.SMEM), big] + _attn_specs(nq) + [big, _rows(nq * N_HEADS * QBLK, 1)],
        out_specs=[big, _rows(nq * QBLK, D_KV), _rows(QBLK, D_KV), _rows(nq * QBLK, D_KV), _rows(QBLK, D_KV),
                   _acc((8, 128))],
        out_shape=[jax.ShapeDtypeStruct((S, D_ATT), BF16), cur, prev, cur, prev, jax.ShapeDtypeStruct((8, 128), F32)])


def _rnn_bwd(xr, gr, h, kept, drec, conv_w, wa, wx, lam, comm=None):
    S = xr.shape[0]
    tb = min(256, S)
    t8 = tb // 8
    nsteps = S // tb

    def body(xr_ref, xp_ref, gr_ref, h_ref, hp_ref, xc_ref, r_ref, ig_ref, a_ref, f_ref, drec_ref, cw_ref, wa_ref, wx_ref,
             lam_ref, dxr_ref, dgr_ref, gwa_ref, gwx_ref, acc_ref, carry_s, dxc_halo_s, d_s, gwa_s, gwx_s):
        i = pl.program_id(0)
        blk = nsteps - 1 - i

        @pl.when(i == 0)
        def _():
            gwa_s[...] = jnp.zeros_like(gwa_s)
            gwx_s[...] = jnp.zeros_like(gwx_s)
            acc_ref[...] = jnp.zeros_like(acc_ref)
            carry_s[...] = jnp.zeros_like(carry_s)
            dxc_halo_s[...] = jnp.zeros_like(dxc_halo_s)

        x = xr_ref[...]
        xhalo = jnp.where(blk > 0, xp_ref[...], 0.0)
        cw = _w_rows(cw_ref)
        xs = [_shift_down(x, xhalo, 3), _shift_down(x, xhalo, 2), _shift_down(x, xhalo, 1), x]
        xc, r, ig, a, f = xc_ref[...], r_ref[...], ig_ref[...], a_ref[...], f_ref[...]
        sp = _softplus_neg(lam_ref[...])
        hcur = h_ref[...]
        hprev = _shift_down(hcur, jnp.where(blk > 0, hp_ref[...], 0.0), 1)
        gl, dgl = _gelu_and_grad(gr_ref[...])
        drec = drec_ref[...]
        dgr_ref[...] = (drec * hcur * dgl).astype(BF16)
        d_s[...] = drec * gl
        row8 = lax.broadcasted_iota(jnp.int32, (8, D_RNN), 0)

        def tile(t, c):
            o = pl.multiple_of((t8 - 1 - t) * 8, 8)
            a8 = a_ref[pl.ds(o, 8), :]
            dt = d_s[pl.ds(o, 8), :]
            at = jnp.where(row8 == 7, 1.0, pltpu.roll(a8, 7, 0))
            for s in (1, 2, 4):
                keep = row8 < 8 - s
                a_sh = jnp.where(keep, pltpu.roll(at, 8 - s, 0), 1.0)
                d_sh = jnp.where(keep, pltpu.roll(dt, 8 - s, 0), 0.0)
                dt = at * d_sh + dt
                at = at * a_sh
            lt = at * c + dt
            d_s[pl.ds(o, 8), :] = lt
            return _row_sum(jnp.where(row8 == 0, a8 * lt, 0.0))

        carry_s[0:1, :] = lax.fori_loop(0, t8, tile, carry_s[0:1, :], unroll=2)
        lmb = d_s[...]
        a2 = a * a
        dla = lmb * hprev * a - lmb * ig * xc * (a2 / f)
        di = lmb * f * xc
        dr = dla * (-LRU_C) * sp
        dpa = dr * r * (1.0 - r)
        dpx = di * ig * (1.0 - ig)
        dpab = dpa.astype(BF16)
        dpxb = dpx.astype(BF16)
        xcb = xc.astype(BF16)
        gwa_s[...] += _dot_tn(xcb, dpab)
        gwx_s[...] += _dot_tn(xcb, dpxb)

        @pl.when(i == nsteps - 1)
        def _():
            for dense, out in ((gwa_s[...], gwa_ref), (gwx_s[...], gwx_ref)):
                for b in range(RNN_BLOCKS):
                    rows = slice(b * HEAD_DIM, (b + 1) * HEAD_DIM)
                    out[rows, :] = dense[rows, b * HEAD_DIM:(b + 1) * HEAD_DIM]

        dxc = lmb * f * ig + _dot_nt(dpab, wa_ref[...]) + _dot_nt(dpxb, wx_ref[...])
        nxt = dxc_halo_s[...]
        dxr = cw[3] * dxc
        for s in (1, 2, 3):
            dxr = dxr + cw[3 - s] * _shift_up(dxc, nxt, s)
        dxr_ref[...] = dxr.astype(BF16)
        dxc_halo_s[...] = dxc[:8]
        dlam = _row_sum(dla * (-LRU_C) * r) * (-1.0 / (1.0 + jnp.exp(lam_ref[...])))
        _put_rows(acc_ref, [_row_sum(dxc * xs[0]), _row_sum(dxc * xs[1]), _row_sum(dxc * xs[2]), _row_sum(dxc * xs[3]),
                            _row_sum(dxc), _row_sum(dpa), _row_sum(dpx), dlam])

    rev = lambda i: (nsteps - 1 - i, 0)
    prev8 = lambda i: (jnp.maximum((nsteps - 1 - i) * t8 - 1, 0), 0)
    blkspec = pl.BlockSpec((tb, D_RNN), rev)
    halo8 = pl.BlockSpec((8, D_RNN), prev8)
    vec = _resident((1, D_RNN))
    return _pcall(
        body, (xr, xr, gr, h, h, *kept, drec, conv_w, wa, wx, lam), name="rnn_bwd", grid=(nsteps,),
        sem="arbitrary", comm=comm,
        in_specs=[blkspec, halo8, blkspec, blkspec, halo8] + [blkspec] * 6
        + [_resident((4, D_RNN)), _resident((D_RNN, D_RNN)), _resident((D_RNN, D_RNN)), vec],
        out_specs=[blkspec, blkspec, _acc((D_RNN, HEAD_DIM)), _acc((D_RNN, HEAD_DIM)), _acc((8, D_RNN))],
        out_shape=[jax.ShapeDtypeStruct((S, D_RNN), BF16), jax.ShapeDtypeStruct((S, D_RNN), BF16),
                   jax.ShapeDtypeStruct((D_RNN, HEAD_DIM), F32), jax.ShapeDtypeStruct((D_RNN, HEAD_DIM), F32),
                   jax.ShapeDtypeStruct((8, D_RNN), F32)],
        scratch_shapes=[pltpu.VMEM((8, D_RNN), F32), pltpu.VMEM((8, D_RNN), F32), pltpu.VMEM((tb, D_RNN), F32),
                        pltpu.VMEM((D_RNN, D_RNN), F32), pltpu.VMEM((D_RNN, D_RNN), F32)])


def _in_bwd(dq, dkc, dkp, dvc, dvp, dxr, dgr, dz1, w_in, comm=None):
    S = dz1.shape[0]
    tb = min(ATT_STEP * QBLK, S)
    nsteps = S // tb

    def body(dq_ref, dkc_ref, dkn_ref, dvc_ref, dvn_ref, dxr_ref, dgr_ref, dz1_ref, w_ref, du_ref, dx_ref):
        last = pl.program_id(0) == nsteps - 1

        def total(cur_ref, next_ref):
            nxt = jnp.where(last, 0.0, next_ref[...])
            tail = cur_ref[tb - QBLK:, :] + nxt
            return jnp.concatenate([cur_ref[:tb - QBLK, :], tail], axis=0) if tb > QBLK else tail

        dk = total(dkc_ref, dkn_ref).astype(BF16)
        dv = total(dvc_ref, dvn_ref).astype(BF16)
        du = jnp.concatenate([dq_ref[...], dk, dv, dxr_ref[...], dgr_ref[...]], axis=1)
        du_ref[...] = du
        dx_ref[...] = ALPHA * dz1_ref[...] + _dot(du, w_ref[...])

    nextp = pl.BlockSpec((QBLK, D_KV), lambda i: (jnp.minimum(i + 1, nsteps - 1), 0))
    return _pcall(
        body, (dq, dkc, dkp, dvc, dvp, dxr, dgr, dz1, w_in), name="in_bwd", grid=(nsteps,), comm=comm,
        in_specs=[_rows(tb, D_ATT), _rows(tb, D_KV), nextp, _rows(tb, D_KV), nextp,
                  _rows(tb, D_RNN), _rows(tb, D_RNN), _rows(tb, D_MODEL), _resident((D_IN, D_MODEL))],
        out_specs=[_rows(tb, D_IN), _rows(tb, D_MODEL)],
        out_shape=[jax.ShapeDtypeStruct((S, D_IN), BF16), jax.ShapeDtypeStruct((S, D_MODEL), F32)])


def _block_diag(w):
    eye = jnp.eye(RNN_BLOCKS, dtype=w.dtype)
    return (w[:, :, None, :] * eye[:, None, :, None]).reshape(D_RNN, D_RNN).astype(BF16)


def _adamw(w, g, m, v):
    m = ADAM_B1 * m + (1.0 - ADAM_B1) * g
    v = ADAM_B2 * v + (1.0 - ADAM_B2) * (g * g)
    m_hat = m / (1.0 - ADAM_B1 ** ADAM_STEP)
    v_hat = v / (1.0 - ADAM_B2 ** ADAM_STEP)
    delta = -ADAM_LR * (m_hat / (jnp.sqrt(v_hat) + ADAM_EPS) + ADAM_WD * w)
    return delta, m, v


def _sum_adamw(parts, w, m, v, name):
    parts = parts if isinstance(parts, (list, tuple)) else [parts]
    R, C = w.shape
    rb = R if R <= 256 else 128
    per = parts[0].shape[1] // rb
    assert R % rb == 0 and parts[0].shape[1] % rb == 0
    n = len(parts)

    def body(*refs):
        p_refs = refs[:n]
        w_ref, m_ref, v_ref, g_out, d_out, m_out, v_out = refs[n:]
        which = pl.program_id(0) // per

        def total(p_ref):
            g = p_ref[0].astype(F32)
            for d in range(1, N_DEV):
                g = g + p_ref[d].astype(F32)
            return g

        g = total(p_refs[0])
        for j in range(1, n):
            g = jnp.where(which == j, total(p_refs[j]), g)
        delta, mn, vn = _adamw(w_ref[...], g, m_ref[...], v_ref[...])
        g_out[...] = g
        d_out[...] = delta
        m_out[...] = mn
        v_out[...] = vn

    def part_spec(j):
        return pl.BlockSpec((N_DEV, rb, C), lambda i: (0, jnp.clip(i - j * per, 0, per - 1), 0))

    blk = _rows(rb, C)
    out = jax.ShapeDtypeStruct((R, C), F32)
    return pl.pallas_call(
        body, name=name, grid=(R // rb,),
        in_specs=[part_spec(j) for j in range(n)] + [blk, blk, blk],
        out_specs=[blk, blk, blk, blk], out_shape=[out, out, out, out],
        compiler_params=_params("parallel"),
    )(*parts, w, m, v)


_SMALL = [("attn_sinks", "s", 0, 1, None), ("rnn_conv_w", "r", 0, 4, "cols"), ("rnn_conv_b", "r", 4, 1, None),
          ("gate_a_w", "a", 0, D_RNN, None), ("gate_a_b", "r", 5, 1, None), ("gate_x_w", "x", 0, D_RNN, None),
          ("gate_x_b", "r", 6, 1, None), ("lru_lambda", "r", 7, 1, None), ("ln1_g", "d", 0, 1, None),
          ("ln1_b", "d", 1, 1, None), ("ffn_conv_w", "f", 0, 3, "cols"), ("ffn_conv_b", "f", 3, 1, None),
          ("ple_gate_b", "t", 2, 1, None), ("ln2_g", "t", 0, 1, None), ("ln2_b", "t", 1, 1, None)]
_LOSS_ROW = 3


_ACC_COLS = {"t": (0, D_MODEL), "f": (D_MODEL, D_FF), "d": (D_MODEL + D_FF, D_MODEL), "s": (2 * D_MODEL + D_FF, 128),
             "r": (2 * D_MODEL + D_FF + 128, D_RNN)}
_ACC_WIDTH = 2 * D_MODEL + D_FF + 128 + D_RNN


def _small_update(rows_all, gates_all, params):
    flat = [arr for triple in params for arr in triple]
    n_par = len(_SMALL)

    def body(*refs):
        rows_ref, gates_ref = refs[:2]
        p_refs = refs[2:2 + 3 * n_par]
        loss_ref = refs[2 + 3 * n_par]
        o_refs = refs[3 + 3 * n_par:3 + 7 * n_par]
        rows_s, tmp_r, tmp_f = refs[3 + 7 * n_par:]
        me = _dev_index(*_place())
        rows_sum, gates_sum = rows_ref[0], gates_ref[0]
        for d in range(1, N_DEV):
            rows_sum = rows_sum + rows_ref[d]
            gates_sum = gates_sum + gates_ref[d]
        rows_s[...] = rows_sum
        t0 = _ACC_COLS["t"][0]
        loss_ref[...] = rows_s[_LOSS_ROW:_LOSS_ROW + 1, t0:t0 + 128]
        for i, (name, key, row, rows, how) in enumerate(_SMALL):
            w_ref, m_ref, v_ref = p_refs[3 * i:3 * i + 3]
            g_out, d_out, m_out, v_out = o_refs[4 * i:4 * i + 4]
            if key == "a":
                g = gates_sum[:, :HEAD_DIM]
            elif key == "x":
                g = gates_sum[:, HEAD_DIM:]
            elif how == "cols":
                c0, width = _ACC_COLS[key]
                full = rows_s[:, c0:c0 + width]
                shard = width // N_DEV
                mine = full[:, :shard]
                for d in range(1, N_DEV):
                    mine = jnp.where(me == d, full[:, d * shard:(d + 1) * shard], mine)
                tmp = tmp_r if key == "r" else tmp_f
                tmp[...] = mine
                g = tmp[row:row + rows, :]
            else:
                c0, width = _ACC_COLS[key]
                g = rows_s[row:row + rows, c0:c0 + width][:, :w_ref.shape[1]]
            delta, mn, vn = _adamw(w_ref[...], g, m_ref[...], v_ref[...])
            g_out[...] = g
            d_out[...] = delta
            m_out[...] = mn
            v_out[...] = vn

    outs = [jax.ShapeDtypeStruct((1, 128), F32)]
    for w, _, _ in params:
        outs += [jax.ShapeDtypeStruct(w.shape, F32)] * 4
    scratch = [pltpu.VMEM((8, _ACC_WIDTH), F32), pltpu.VMEM((8, D_RNN // N_DEV), F32), pltpu.VMEM((8, D_FF // N_DEV), F32)]
    res = pl.pallas_call(body, name="small_update", out_shape=outs, scratch_shapes=scratch)(rows_all, gates_all, *flat)
    return res[0], [res[1 + 4 * i:5 + 4 * i] for i in range(n_par)]


def kernel(x, p, w_in, attn_sinks, rnn_conv_w, rnn_conv_b, gate_a_w, gate_a_b, gate_x_w, gate_x_b, lru_lambda, w_out, ln1_g, ln1_b, w_ffn_up, ffn_conv_w, ffn_conv_b, w_ffn_down, ple_gate_w, ple_gate_b, ple_proj, ln2_g, ln2_b, loss_target, m_w_in, m_attn_sinks, m_rnn_conv_w, m_rnn_conv_b, m_gate_a_w, m_gate_a_b, m_gate_x_w, m_gate_x_b, m_lru_lambda, m_w_out, m_ln1_g, m_ln1_b, m_w_ffn_up, m_ffn_conv_w, m_ffn_conv_b, m_w_ffn_down, m_ple_gate_w, m_ple_gate_b, m_ple_proj, m_ln2_g, m_ln2_b, v_w_in, v_attn_sinks, v_rnn_conv_w, v_rnn_conv_b, v_gate_a_w, v_gate_a_b, v_gate_x_w, v_gate_x_b, v_lru_lambda, v_w_out, v_ln1_g, v_ln1_b, v_w_ffn_up, v_ffn_conv_w, v_ffn_conv_b, v_w_ffn_down, v_ple_gate_w, v_ple_gate_b, v_ple_proj, v_ln2_g, v_ln2_b):
    from_col_blocks = lambda g: g.transpose(1, 0, 2).reshape(g.shape[1], N_DEV * g.shape[2])

    xs, ps, tgt, sinks = x[0], p[0, 0], loss_target[0], attn_sinks[0]
    wa, wx = _block_diag(gate_a_w[0]), _block_diag(gate_x_w[0])

    conv_cols = jnp.concatenate([rnn_conv_w[0].reshape(1, -1), ffn_conv_w[0].reshape(1, -1)], axis=1)
    n_rc, n_fc = 4 * D_RNN // N_DEV, 3 * D_FF // N_DEV
    ((g_in,),) = _comm_call([_Gather([w_in[0].T.astype(BF16)])], "gather_w_in")
    w_in_full = g_in.reshape(D_IN, D_MODEL)

    (q, k, v, xr, gr), (g_conv,) = _in_proj(xs, w_in_full, comm=_Bcast([jnp.broadcast_to(conv_cols, (8, n_rc + n_fc))]))
    rcw = from_col_blocks(g_conv[:, 0, :n_rc].reshape(N_DEV, 4, D_RNN // N_DEV))
    fcw = from_col_blocks(g_conv[:, 0, n_rc:].reshape(N_DEV, 3, D_FF // N_DEV))
    (att, lse), (g_out, g_down) = _attn_fwd(q, k, v, sinks,
                                            comm=_Gather([w_out[0].astype(BF16), w_ffn_down[0].astype(BF16)]))
    (rec, h, *kept), (w_up,) = _rnn_fwd(xr, gr, rcw, rnn_conv_b, wa, wx, gate_a_b, gate_x_b, lru_lambda,
                                        comm=_Gather([w_ffn_up[0].astype(BF16)]))
    w_out_full = g_out.reshape(D_MODEL, D_MODEL)
    (z1, h1, h1b, gate, act, gl, vdgl), (g_pg, g_pp) = _mix_ln1_up(
        xs, att, rec, w_out_full, ln1_g, ln1_b, w_up, fcw, ffn_conv_b,
        comm=_Gather([ple_gate_w[0].astype(BF16), ple_proj[0].astype(BF16)]))
    dz2b, dpreb, dppb, dgc, dval, dh1p, acc_t = _tail(
        act, gl, vdgl, h1, h1b, ps, tgt, g_down.reshape(D_FF, D_MODEL), g_pg.reshape(D_MODEL, D_MODEL), ple_gate_b,
        from_col_blocks(g_pp), ln2_g, ln2_b)

    gd_down = _weight_grad([dz2b], [act], "down_grad", "rows_t")
    gd_pg = _weight_grad([h1b], [dpreb], "pg_grad", "rows", ts=1024)
    gd_pp = _weight_grad([ps], [dppb], "pp_grad", "cols", ts=1024)
    (dgate, dz1, dz1b, datt, drec, acc_f, acc_d), (r_down, r_pg, r_pp) = _up_bwd(
        dgc, gate, dval, dh1p, z1, w_up, fcw, w_out_full, ln1_g, comm=_Exchange([gd_down, gd_pg, gd_pp]))
    gd_up_top, gd_up_bot = _weight_grad([h1b], [dgate, dval], "up_grad", "cols", halves=True)
    gd_out = _weight_grad([att, rec], [dz1b], "out_grad", "rows", ts=1024)
    (dq, dkc, dkp, dvc, dvp, acc_s), (r_up_top,) = _attn_bwd(q, k, v, lse, datt, sinks, comm=_Exchange([gd_up_top]))
    early = jnp.concatenate([acc_t, acc_f, acc_d], axis=1)
    (dxr, dgr, g_wa, g_wx, acc_r), (r_up_bot, r_out, early_all) = _rnn_bwd(
        xr, gr, h, kept, drec, rcw, wa, wx, lru_lambda, comm=_Multi([_Exchange([gd_up_bot, gd_out]), _Bcast([early])]))
    (du, dx), _ = _in_bwd(dq, dkc, dkp, dvc, dvp, dxr, dgr, dz1, w_in_full)
    lanes = D_RNN // 128
    late = jnp.concatenate([g_wa, g_wx], axis=1)
    late = jnp.concatenate([late, acc_s, acc_r.reshape(8, lanes, 128).transpose(1, 0, 2).reshape(8 * lanes, 128)], axis=0)
    gd_in_a, (late_all,) = _weight_grad([du], [xs], "in_grad_a", "rows", ts=1024, b_window=(0, D_MODEL // 2),
                                        comm=_Bcast([late]))
    gd_in_b, (r_in_a,) = _weight_grad([du], [xs], "in_grad_b", "rows", ts=1024, b_window=(1, D_MODEL // 2),
                                      comm=_Exchange([gd_in_a]))
    ((r_in_b,),) = _comm_call([_Exchange([gd_in_b])], "exchange_w_in")
    r_in = jnp.concatenate([r_in_a, r_in_b], axis=2)
    acc_r_all = late_all[:, D_RNN + 8:].reshape(N_DEV, lanes, 8, 128).transpose(0, 2, 1, 3).reshape(N_DEV, 8, D_RNN)
    small_parts = (jnp.concatenate([early_all, late_all[:, D_RNN:D_RNN + 8], acc_r_all], axis=2),
                   late_all[:, :D_RNN])

    outs = {}
    res = _sum_adamw(r_in, w_in[0].T, m_w_in[0].T, v_w_in[0].T, "adamw_w_in")
    outs["w_in"] = [r.T[None] for r in res]
    for name, parts, w, m, v in [("w_out", r_out, w_out, m_w_out, v_w_out),
                                 ("w_ffn_up", [r_up_top, r_up_bot], w_ffn_up, m_w_ffn_up, v_w_ffn_up),
                                 ("w_ffn_down", r_down, w_ffn_down, m_w_ffn_down, v_w_ffn_down),
                                 ("ple_gate_w", r_pg, ple_gate_w, m_ple_gate_w, v_ple_gate_w),
                                 ("ple_proj", r_pp, ple_proj, m_ple_proj, v_ple_proj)]:
        res = _sum_adamw(parts, w[0], m[0], v[0], "adamw_" + name)
        outs[name] = [r[None] for r in res]

    given = dict(attn_sinks=(attn_sinks, m_attn_sinks, v_attn_sinks), rnn_conv_w=(rnn_conv_w, m_rnn_conv_w, v_rnn_conv_w),
                 rnn_conv_b=(rnn_conv_b, m_rnn_conv_b, v_rnn_conv_b), gate_a_w=(gate_a_w, m_gate_a_w, v_gate_a_w),
                 gate_a_b=(gate_a_b, m_gate_a_b, v_gate_a_b), gate_x_w=(gate_x_w, m_gate_x_w, v_gate_x_w),
                 gate_x_b=(gate_x_b, m_gate_x_b, v_gate_x_b), lru_lambda=(lru_lambda, m_lru_lambda, v_lru_lambda),
                 ln1_g=(ln1_g, m_ln1_g, v_ln1_g), ln1_b=(ln1_b, m_ln1_b, v_ln1_b),
                 ffn_conv_w=(ffn_conv_w, m_ffn_conv_w, v_ffn_conv_w), ffn_conv_b=(ffn_conv_b, m_ffn_conv_b, v_ffn_conv_b),
                 ple_gate_b=(ple_gate_b, m_ple_gate_b, v_ple_gate_b), ln2_g=(ln2_g, m_ln2_g, v_ln2_g),
                 ln2_b=(ln2_b, m_ln2_b, v_ln2_b))
    as_2d = lambda a: a.reshape(-1, a.shape[-1])
    loss_row, small_res = _small_update(*small_parts, [tuple(as_2d(a) for a in given[n]) for n, *_ in _SMALL])
    loss = loss_row[0, 0]
    for (n, *_), res in zip(_SMALL, small_res):
        outs[n] = [r.reshape(given[n][0].shape) for r in res]

    order = ["w_in", "attn_sinks", "rnn_conv_w", "rnn_conv_b", "gate_a_w", "gate_a_b", "gate_x_w", "gate_x_b",
             "lru_lambda", "w_out", "ln1_g", "ln1_b", "w_ffn_up", "ffn_conv_w", "ffn_conv_b", "w_ffn_down",
             "ple_gate_w", "ple_gate_b", "ple_proj", "ln2_g", "ln2_b"]
    return (loss, dx[None], *[outs[n][0] for n in order], *[outs[n][1] for n in order],
            *[outs[n][2] for n in order], *[outs[n][3] for n in order])
```

```python
import jax
import jax.numpy as jnp
from jax import lax
from jax.experimental import pallas as pl
from jax.experimental.pallas import tpu as pltpu

F32 = jnp.float32
BF16 = jnp.bfloat16

D_MODEL = 1024
D_ATT = 512
D_KV = 128
HEAD_DIM = 64
N_HEADS = 8
N_KV = 2
D_RNN = 512
RNN_BLOCKS = 8
D_IN = 1792
D_FF = 3072
PLE_DIM = 256
QBLK = 128
N_DEV = 8
ALPHA = float(2 ** 0.25)
LN_EPS = 1e-5
LRU_C = 8.0
ADAM_LR, ADAM_B1, ADAM_B2, ADAM_EPS, ADAM_WD, ADAM_STEP = 0.001, 0.9, 0.999, 1e-08, 0.01, 10

V7X_VMEM_LIMIT = 56 * 1024 * 1024
MESH = pl.DeviceIdType.MESH


def _params(*sem, vmem=V7X_VMEM_LIMIT):
    return pltpu.CompilerParams(dimension_semantics=sem or None, vmem_limit_bytes=vmem)


def _resident(shape):
    return pl.BlockSpec(shape, lambda *_: (0,) * len(shape), pipeline_mode=pl.Buffered(1))


def _rows(tb, cols):
    return pl.BlockSpec((tb, cols), lambda i: (i, 0))


def _acc(shape):
    return pl.BlockSpec(shape, lambda *_: (0,) * len(shape))


def _dot(a, b):
    return jnp.dot(a, b, preferred_element_type=F32)


def _dot_nt(a, b):
    return lax.dot_general(a, b, (((1,), (1,)), ((), ())), preferred_element_type=F32)


def _dot_tn(a, b):
    return lax.dot_general(a, b, (((0,), (0,)), ((), ())), preferred_element_type=F32)


def _sigmoid(x):
    return 1.0 / (1.0 + jnp.exp(-x))


_GELU_C = 0.7978845608028654
_GELU_K = 0.044715


def _gelu_and_grad(x):
    u = x * x
    t = jnp.tanh(x * (_GELU_C + (_GELU_C * _GELU_K) * u))
    hp = 0.5 + 0.5 * t
    dg = hp + x * (0.5 - 0.5 * (t * t)) * (_GELU_C + (3.0 * _GELU_C * _GELU_K) * u)
    return x * hp, dg


def _gelu(x):
    return 0.5 * x * (1.0 + jnp.tanh(_GELU_C * (x + _GELU_K * x * x * x)))


def _ln_stats(z):
    mu = jnp.mean(z, axis=-1, keepdims=True)
    zc = z - mu
    var = jnp.mean(zc * zc, axis=-1, keepdims=True)
    rstd = lax.rsqrt(var + LN_EPS)
    return zc * rstd, rstd


def _ln_bwd(dy, xhat, rstd, g):
    dxh = dy * g
    m1 = jnp.mean(dxh, axis=-1, keepdims=True)
    m2 = jnp.mean(dxh * xhat, axis=-1, keepdims=True)
    return rstd * (dxh - m1 - xhat * m2)


def _softplus_neg(lam):
    u = jnp.exp(-jnp.abs(lam))
    w = 1.0 + u
    d = w - 1.0
    log1p_u = jnp.where(d == 0.0, u, jnp.log(w) * (u / jnp.where(d == 0.0, 1.0, d)))
    return jnp.maximum(-lam, 0.0) + log1p_u


def _shift_down(x, halo, s):
    xs = pltpu.roll(x, s, 0)
    hs = pltpu.roll(halo, s, 0)
    row8 = lax.broadcasted_iota(jnp.int32, hs.shape, 0)
    first = jnp.where(row8 < s, hs, xs[:8])
    return jnp.concatenate([first, xs[8:]], axis=0)


def _shift_up(x, halo, s):
    n = x.shape[0]
    xs = pltpu.roll(x, n - s, 0)
    hs = pltpu.roll(halo, 8 - s, 0)
    row8 = lax.broadcasted_iota(jnp.int32, hs.shape, 0)
    last = jnp.where(row8 >= 8 - s, hs, xs[n - 8:])
    return jnp.concatenate([xs[:n - 8], last], axis=0)


def _row_sum(x):
    return jnp.sum(x, axis=0, keepdims=True)


def _put_rows(acc_ref, rows):
    row8 = lax.broadcasted_iota(jnp.int32, acc_ref.shape, 0)
    upd = jnp.zeros(acc_ref.shape, F32)
    for r, vec in enumerate(rows):
        upd = jnp.where(row8 == r, vec, upd)
    acc_ref[...] += upd


def _place():
    return lax.axis_index("x"), lax.axis_index("y"), lax.axis_index("c")


def _dev_index(px, py, pc):
    return 4 * px + 2 * py + pc


_ANY = pl.BlockSpec(memory_space=pl.ANY)


class _Gather:
    def __init__(self, arrays, forward_at=0.875):
        self.arrays = list(arrays)
        self.n = len(self.arrays)
        self.forward_at = forward_at

    def out_shape(self):
        return [jax.ShapeDtypeStruct((N_DEV,) + s.shape, s.dtype) for s in self.arrays]

    def scratch(self):
        return [pltpu.SemaphoreType.DMA((self.n, 7)), pltpu.SemaphoreType.DMA((self.n, 7)),
                pltpu.SemaphoreType.DMA((self.n,))]

    def _parts(self, ins, outs, sems):
        send_sems, recv_sems, local_sems = sems
        x, y, c = _place()
        me, sibling = (x, y, c), (x, y, 1 - c)
        chips = [(1 - x, y), (x, 1 - y), (1 - x, 1 - y)]

        def copy(a, k, block, to, src=None):
            rows = outs[a].at[_dev_index(*block)]
            return pltpu.make_async_remote_copy(
                src_ref=rows if src is None else src, dst_ref=rows, send_sem=send_sems.at[a, k],
                recv_sem=recv_sems.at[a, k], device_id=to, device_id_type=MESH)

        rng = range(self.n)
        mine = [pltpu.make_async_copy(ins[a], outs[a].at[_dev_index(*me)], local_sems.at[a]) for a in rng]
        first = [copy(a, 0, me, sibling, src=ins[a]) for a in rng]
        first += [copy(a, 1 + j, me, (*chip, c), src=ins[a]) for j, chip in enumerate(chips) for a in rng]
        landed = [copy(a, 1 + j, (*chip, c), me) for j, chip in enumerate(chips) for a in rng]
        passed = [copy(a, 4 + j, (*chip, c), sibling) for j, chip in enumerate(chips) for a in rng]
        from_sibling = [copy(a, 0, sibling, me) for a in rng]
        from_sibling += [copy(a, 4 + j, (*chip, 1 - c), me) for j, chip in enumerate(chips) for a in rng]
        return mine, first, landed, passed, from_sibling

    def start(self, ins, outs, sems):
        mine, first, _, _, _ = self._parts(ins, outs, sems)
        for cp in mine + first:
            cp.start()

    def forward(self, ins, outs, sems):
        _, _, landed, passed, _ = self._parts(ins, outs, sems)
        for got, fwd in zip(landed, passed):
            got.wait_recv()
            fwd.start()

    def finish(self, ins, outs, sems):
        mine, first, _, passed, from_sibling = self._parts(ins, outs, sems)
        for cp in from_sibling:
            cp.wait_recv()
        for cp in first + passed:
            cp.wait_send()
        for cp in mine:
            cp.wait()

    def before(self, ins, outs, sems, step, nsteps):
        pl.when(step == 0)(lambda: self.start(ins, outs, sems))
        pl.when(step == min(int(self.forward_at * nsteps), nsteps - 1))(lambda: self.forward(ins, outs, sems))

    def after(self, ins, outs, sems, step, nsteps):
        pl.when(step == nsteps - 1)(lambda: self.finish(ins, outs, sems))


class _Exchange:
    def __init__(self, arrays):
        self.arrays = list(arrays)
        self.n = len(self.arrays)

    def out_shape(self):
        return [jax.ShapeDtypeStruct(b.shape, b.dtype) for b in self.arrays]

    def scratch(self):
        return [pltpu.SemaphoreType.DMA((self.n, 7)), pltpu.SemaphoreType.DMA((self.n, 7)),
                pltpu.SemaphoreType.DMA((self.n,))]

    def _parts(self, ins, outs, sems):
        send_sems, recv_sems, local_sems = sems
        x, y, c = _place()
        me = _dev_index(x, y, c)
        peers = [(x ^ (k >> 2), y ^ ((k >> 1) & 1), c ^ (k & 1)) for k in range(1, N_DEV)]
        rng = range(self.n)
        mine = [pltpu.make_async_copy(ins[a].at[me], outs[a].at[me], local_sems.at[a]) for a in rng]
        sent = [pltpu.make_async_remote_copy(
            src_ref=ins[a].at[_dev_index(*to)], dst_ref=outs[a].at[me], send_sem=send_sems.at[a, k],
            recv_sem=recv_sems.at[a, k], device_id=to, device_id_type=MESH) for k, to in enumerate(peers) for a in rng]
        arrivals = [pltpu.make_async_remote_copy(
            src_ref=ins[a].at[me], dst_ref=outs[a].at[_dev_index(*frm)], send_sem=send_sems.at[a, k],
            recv_sem=recv_sems.at[a, k], device_id=frm, device_id_type=MESH) for k, frm in enumerate(peers) for a in rng]
        return mine, sent, arrivals

    def start(self, ins, outs, sems):
        mine, sent, _ = self._parts(ins, outs, sems)
        for cp in mine + sent:
            cp.start()

    def finish(self, ins, outs, sems):
        mine, sent, arrivals = self._parts(ins, outs, sems)
        for cp in arrivals:
            cp.wait_recv()
        for cp in sent:
            cp.wait_send()
        for cp in mine:
            cp.wait()

    def before(self, ins, outs, sems, step, nsteps):
        pl.when(step == 0)(lambda: self.start(ins, outs, sems))

    def after(self, ins, outs, sems, step, nsteps):
        pl.when(step == nsteps - 1)(lambda: self.finish(ins, outs, sems))


class _Bcast(_Exchange):
    def out_shape(self):
        return [jax.ShapeDtypeStruct((N_DEV,) + s.shape, s.dtype) for s in self.arrays]

    def _parts(self, ins, outs, sems):
        send_sems, recv_sems, local_sems = sems
        x, y, c = _place()
        me = _dev_index(x, y, c)
        peers = [(x ^ (k >> 2), y ^ ((k >> 1) & 1), c ^ (k & 1)) for k in range(1, N_DEV)]
        rng = range(self.n)
        mine = [pltpu.make_async_copy(ins[a], outs[a].at[me], local_sems.at[a]) for a in rng]
        sent = [pltpu.make_async_remote_copy(
            src_ref=ins[a], dst_ref=outs[a].at[me], send_sem=send_sems.at[a, k], recv_sem=recv_sems.at[a, k],
            device_id=to, device_id_type=MESH) for k, to in enumerate(peers) for a in rng]
        arrivals = [pltpu.make_async_remote_copy(
            src_ref=ins[a], dst_ref=outs[a].at[_dev_index(*frm)], send_sem=send_sems.at[a, k],
            recv_sem=recv_sems.at[a, k], device_id=frm, device_id_type=MESH) for k, frm in enumerate(peers) for a in rng]
        return mine, sent, arrivals


class _Multi:
    def __init__(self, comms):
        self.comms = list(comms)
        self.arrays = [arr for c in self.comms for arr in c.arrays]
        self.n = len(self.arrays)

    def out_shape(self):
        return [s for c in self.comms for s in c.out_shape()]

    def scratch(self):
        return [s for c in self.comms for s in c.scratch()]

    def _each(self, ins, outs, sems):
        a = 0
        for j, c in enumerate(self.comms):
            yield c, ins[a:a + c.n], outs[a:a + c.n], sems[3 * j:3 * j + 3]
            a += c.n

    def before(self, ins, outs, sems, step, nsteps):
        for c, ci, co, cs in self._each(ins, outs, sems):
            c.before(ci, co, cs, step, nsteps)

    def after(self, ins, outs, sems, step, nsteps):
        for c, ci, co, cs in self._each(ins, outs, sems):
            c.after(ci, co, cs, step, nsteps)


def _comm_call(comms, name):
    ns = [c.n for c in comms]
    n = sum(ns)

    def body(*refs):
        parts, a, s = [], 0, 2 * n
        for c in comms:
            parts.append((c, refs[a:a + c.n], refs[n + a:n + a + c.n], refs[s:s + 3]))
            a, s = a + c.n, s + 3
        for c, ins, outs, sems in parts:
            c.start(ins, outs, sems)
        for c, ins, outs, sems in parts:
            if isinstance(c, _Gather):
                c.forward(ins, outs, sems)
        for c, ins, outs, sems in parts:
            c.finish(ins, outs, sems)

    res = pl.pallas_call(
        body, name=name, in_specs=[_ANY] * n, out_specs=[_ANY] * n,
        out_shape=[s for c in comms for s in c.out_shape()], scratch_shapes=[s for c in comms for s in c.scratch()],
    )(*[arr for c in comms for arr in c.arrays])
    out, a = [], 0
    for k in ns:
        out.append(res[a:a + k])
        a += k
    return out


def _pcall(body, args, *, name, grid, in_specs, out_specs, out_shape, scratch_shapes=(), sem="parallel", comm=None,
           step_axis=0):
    sem = (sem,) * len(grid) if isinstance(sem, str) else sem
    if comm is None:
        res = pl.pallas_call(body, name=name, grid=grid, in_specs=in_specs, out_specs=out_specs, out_shape=out_shape,
                             scratch_shapes=list(scratch_shapes), compiler_params=_params(*sem))(*args)
        return res, []
    n_in, n_out, n_scr, n = len(in_specs), len(out_specs), len(scratch_shapes), comm.n
    nsteps = grid[step_axis]
    assert all(g == 1 for ax, g in enumerate(grid) if ax != step_axis)

    def hosted(*refs):
        ins, cin = refs[:n_in], refs[n_in:n_in + n]
        o0 = n_in + n
        outs, cout = refs[o0:o0 + n_out], refs[o0 + n_out:o0 + n_out + n]
        s0 = o0 + n_out + n
        scr, sems = refs[s0:s0 + n_scr], refs[s0 + n_scr:]
        step = pl.program_id(step_axis)
        comm.before(cin, cout, sems, step, nsteps)
        body(*ins, *outs, *scr)
        comm.after(cin, cout, sems, step, nsteps)

    res = pl.pallas_call(
        hosted, name=name, grid=grid, in_specs=list(in_specs) + [_ANY] * n, out_specs=list(out_specs) + [_ANY] * n,
        out_shape=list(out_shape) + comm.out_shape(), scratch_shapes=list(scratch_shapes) + comm.scratch(),
        compiler_params=_params(*(("arbitrary",) * len(grid))))(*args, *comm.arrays)
    return res[:n_out], res[n_out:]


def _in_proj(x, w_in_t, comm=None):
    S = x.shape[0]
    tb = min(512, S)

    def body(x_ref, w_ref, q_ref, k_ref, v_ref, xr_ref, gr_ref):
        u = _dot_nt(x_ref[...].astype(BF16), w_ref[...])
        q_ref[...] = (u[:, :D_ATT] * (HEAD_DIM ** -0.5)).astype(BF16)
        k_ref[...] = u[:, D_ATT:D_ATT + D_KV].astype(BF16)
        v_ref[...] = u[:, D_ATT + D_KV:D_ATT + 2 * D_KV].astype(BF16)
        xr_ref[...] = u[:, D_ATT + 2 * D_KV:D_ATT + 2 * D_KV + D_RNN]
        gr_ref[...] = u[:, D_ATT + 2 * D_KV + D_RNN:]

    return _pcall(
        body, (x, w_in_t), name="in_proj", grid=(S // tb,), comm=comm,
        in_specs=[_rows(tb, D_MODEL), _resident((D_IN, D_MODEL))],
        out_specs=[_rows(tb, D_ATT), _rows(tb, D_KV), _rows(tb, D_KV), _rows(tb, D_RNN), _rows(tb, D_RNN)],
        out_shape=[jax.ShapeDtypeStruct((S, D_ATT), BF16), jax.ShapeDtypeStruct((S, D_KV), BF16),
                   jax.ShapeDtypeStruct((S, D_KV), BF16), jax.ShapeDtypeStruct((S, D_RNN), F32),
                   jax.ShapeDtypeStruct((S, D_RNN), F32)])


GROUP = N_HEADS // N_KV


def _band_mask(i):
    qi = lax.broadcasted_iota(jnp.int32, (GROUP * QBLK, 2 * QBLK), 0) & (QBLK - 1)
    sj = lax.broadcasted_iota(jnp.int32, (GROUP * QBLK, 2 * QBLK), 1)
    return (sj > qi) & (sj <= qi + QBLK) & ((sj >= QBLK) | (i > 0))


def _stack_heads(x, g):
    return jnp.concatenate([x[:, (g * GROUP + hh) * HEAD_DIM:(g * GROUP + hh + 1) * HEAD_DIM] for hh in range(GROUP)],
                           axis=0)


def _unstack_heads(x4):
    return [x4[hh * QBLK:(hh + 1) * QBLK] for hh in range(GROUP)]


def _sink_column(sink_ref, g):
    head = lax.broadcasted_iota(jnp.int32, (GROUP * QBLK, 1), 0) // QBLK
    col = jnp.full((GROUP * QBLK, 1), sink_ref[g * GROUP], F32)
    for hh in range(1, GROUP):
        col = jnp.where(head == hh, sink_ref[g * GROUP + hh], col)
    return col


ATT_STEP = 4
IN_GRAD_PARTS = 4


def _attn_specs(nq=1):
    cur = lambda i: (i, 0)
    prev = lambda i: (jnp.maximum(nq * i - 1, 0), 0)
    return [pl.BlockSpec((nq * QBLK, D_KV), cur), pl.BlockSpec((QBLK, D_KV), prev),
            pl.BlockSpec((nq * QBLK, D_KV), cur), pl.BlockSpec((QBLK, D_KV), prev)]


def _attn_fwd(q, k, v, sinks, comm=None):
    S = q.shape[0]
    nq = min(ATT_STEP, S // QBLK)

    def body(sink_ref, q_ref, kc_ref, kp_ref, vc_ref, vp_ref, o_ref, lse_ref):
        first = pl.program_id(0) * nq
        kall = jnp.concatenate([kp_ref[...], kc_ref[...]], axis=0)
        vall = jnp.concatenate([vp_ref[...], vc_ref[...]], axis=0)
        for b in range(nq):
            valid = _band_mask(first + b)
            rows = slice(b * QBLK, (b + 1) * QBLK)
            keys = slice(b * QBLK, (b + 2) * QBLK)
            qv = q_ref[rows, :]
            outs = []
            for g in range(N_KV):
                kcat = kall[keys, g * HEAD_DIM:(g + 1) * HEAD_DIM]
                vcat = vall[keys, g * HEAD_DIM:(g + 1) * HEAD_DIM]
                s = jnp.where(valid, _dot_nt(_stack_heads(qv, g), kcat), -1e30)
                sink = _sink_column(sink_ref, g)
                m = jnp.maximum(jnp.max(s, axis=1, keepdims=True), sink)
                p = jnp.exp(s - m)
                l = jnp.sum(p, axis=1, keepdims=True) + jnp.exp(sink - m)
                outs += _unstack_heads(_dot(p.astype(BF16), vcat) / l)
                lse_ref[(b * N_KV + g) * GROUP * QBLK:(b * N_KV + g + 1) * GROUP * QBLK, :] = m + jnp.log(l)
            o_ref[rows, :] = jnp.concatenate(outs, axis=1).astype(BF16)

    lse_rows = nq * N_HEADS * QBLK
    return _pcall(
        body, (sinks, q, k, k, v, v), name="attn_fwd", grid=(S // (nq * QBLK),), comm=comm,
        in_specs=[pl.BlockSpec(memory_space=pltpu.SMEM), _rows(nq * QBLK, D_ATT)] + _attn_specs(nq),
        out_specs=[_rows(nq * QBLK, D_ATT), _rows(lse_rows, 1)],
        out_shape=[jax.ShapeDtypeStruct((S, D_ATT), BF16), jax.ShapeDtypeStruct((S * N_HEADS, 1), F32)])


def _w_rows(w_ref):
    return [w_ref[k:k + 1, :] for k in range(w_ref.shape[0])]


def _conv4(x, halo, w, b):
    y = b + w[3] * x
    for s in (1, 2, 3):
        y = y + w[3 - s] * _shift_down(x, halo, s)
    return y


def _rnn_gates(xc, wa, wx, ba, bx, sp):
    xcb = xc.astype(BF16)
    r = _sigmoid(_dot(xcb, wa) + ba)
    ig = _sigmoid(_dot(xcb, wx) + bx)
    la = -LRU_C * r * sp
    a = jnp.exp(la)
    t = jnp.tanh(la)
    f = jnp.sqrt(-2.0 * t / (1.0 - t))
    return r, ig, a, f


def _rnn_fwd(xr, gr, conv_w, conv_b, wa, wx, ba, bx, lam, comm=None):
    S = xr.shape[0]
    tb = min(256, S)

    def body(xr_ref, gr_ref, cw_ref, cb_ref, wa_ref, wx_ref, ba_ref, bx_ref, lam_ref, rec_ref, h_ref,
             xc_ref, r_ref, ig_ref, a_ref, f_ref, halo_s, hc_s, a_s, b_s):
        @pl.when(pl.program_id(0) == 0)
        def _():
            halo_s[...] = jnp.zeros_like(halo_s)
            hc_s[...] = jnp.zeros_like(hc_s)

        x = xr_ref[...]
        xc = _conv4(x, halo_s[...], _w_rows(cw_ref), cb_ref[...])
        halo_s[...] = x[tb - 8:]
        r, ig, a, f = _rnn_gates(xc, wa_ref[...], wx_ref[...], ba_ref[...], bx_ref[...], _softplus_neg(lam_ref[...]))
        xc_ref[...] = xc
        r_ref[...] = r
        ig_ref[...] = ig
        a_ref[...] = a
        f_ref[...] = f
        a_s[...] = a
        b_s[...] = f * ig * xc
        row8 = lax.broadcasted_iota(jnp.int32, (8, D_RNN), 0)

        def tile(t, hc):
            o = pl.multiple_of(t * 8, 8)
            at = a_s[pl.ds(o, 8), :]
            bt = b_s[pl.ds(o, 8), :]
            for s in (1, 2, 4):
                keep = row8 >= s
                a_sh = jnp.where(keep, pltpu.roll(at, s, 0), 1.0)
                b_sh = jnp.where(keep, pltpu.roll(bt, s, 0), 0.0)
                bt = at * b_sh + bt
                at = at * a_sh
            ht = at * hc + bt
            b_s[pl.ds(o, 8), :] = ht
            return _row_sum(jnp.where(row8 == 7, ht, 0.0))

        hc_s[0:1, :] = lax.fori_loop(0, tb // 8, tile, hc_s[0:1, :], unroll=2)
        h = b_s[...]
        h_ref[...] = h
        rec_ref[...] = (h * _gelu(gr_ref[...])).astype(BF16)

    vec = _resident((1, D_RNN))
    kept = jax.ShapeDtypeStruct((S, D_RNN), F32)
    return _pcall(
        body, (xr, gr, conv_w, conv_b, wa, wx, ba, bx, lam), name="rnn_fwd", grid=(S // tb,), sem="arbitrary", comm=comm,
        in_specs=[_rows(tb, D_RNN), _rows(tb, D_RNN), _resident((4, D_RNN)), vec,
                  _resident((D_RNN, D_RNN)), _resident((D_RNN, D_RNN)), vec, vec, vec],
        out_specs=[_rows(tb, D_RNN)] * 7,
        out_shape=[jax.ShapeDtypeStruct((S, D_RNN), BF16), kept, kept, kept, kept, kept, kept],
        scratch_shapes=[pltpu.VMEM((8, D_RNN), F32), pltpu.VMEM((8, D_RNN), F32),
                        pltpu.VMEM((tb, D_RNN), F32), pltpu.VMEM((tb, D_RNN), F32)])


def _mix_ln1_up(x, att, rec, w_out, ln1_g, ln1_b, w_up, fcw, fcb, comm=None):
    S = x.shape[0]
    tb = min(256, S)
    nblk, _, wblk = w_up.shape
    half = nblk // 2

    def body(x_ref, att_ref, rec_ref, wo_ref, g_ref, b_ref, wu_ref, fcw_ref, fcb_ref,
             z1_ref, h1_ref, h1b_ref, gate_ref, act_ref, gl_ref, vdgl_ref, halo_s):
        @pl.when(pl.program_id(0) == 0)
        def _():
            halo_s[...] = jnp.zeros_like(halo_s)

        z1 = ALPHA * x_ref[...] + _dot(att_ref[...], wo_ref[:D_ATT, :]) + _dot(rec_ref[...], wo_ref[D_ATT:, :])
        z1_ref[...] = z1
        xhat, _ = _ln_stats(z1)
        h1 = xhat * g_ref[...] + b_ref[...]
        h1_ref[...] = h1
        h1b = h1.astype(BF16)
        h1b_ref[...] = h1b
        for jj in range(half):
            cols = slice(jj * wblk, (jj + 1) * wblk)
            gate = _dot(h1b, wu_ref[jj])
            val = _dot(h1b, wu_ref[jj + half])
            halo = halo_s[:, cols]
            conv = (fcb_ref[:, cols] + fcw_ref[2:3, cols] * gate + fcw_ref[1:2, cols] * _shift_down(gate, halo, 1)
                    + fcw_ref[0:1, cols] * _shift_down(gate, halo, 2))
            halo_s[:, cols] = gate[tb - 8:]
            gl, dgl = _gelu_and_grad(conv)
            gate_ref[:, cols] = gate.astype(BF16)
            act_ref[:, cols] = (gl * val).astype(BF16)
            gl_ref[:, cols] = gl.astype(BF16)
            vdgl_ref[:, cols] = (val * dgl).astype(BF16)

    vec = _resident((1, D_MODEL))
    wide = jax.ShapeDtypeStruct((S, D_FF), BF16)
    return _pcall(
        body, (x, att, rec, w_out, ln1_g, ln1_b, w_up, fcw, fcb), name="mix_ln1_up", grid=(S // tb,),
        sem="arbitrary", comm=comm,
        in_specs=[_rows(tb, D_MODEL), _rows(tb, D_ATT), _rows(tb, D_RNN), _resident((D_MODEL, D_MODEL)), vec, vec,
                  _resident(w_up.shape), _resident((3, D_FF)), _resident((1, D_FF))],
        out_specs=[_rows(tb, D_MODEL), _rows(tb, D_MODEL), _rows(tb, D_MODEL)] + [_rows(tb, D_FF)] * 4,
        out_shape=[jax.ShapeDtypeStruct((S, D_MODEL), F32), jax.ShapeDtypeStruct((S, D_MODEL), F32),
                   jax.ShapeDtypeStruct((S, D_MODEL), BF16), wide, wide, wide, wide],
        scratch_shapes=[pltpu.VMEM((8, D_FF), F32)])


def _tail(act, gl, vdgl, h1, h1b, p, tgt, w_down, w_pg, b_pg, w_pp, ln2_g, ln2_b):
    S = h1.shape[0]
    tb = min(256, S)

    def body(act_ref, gl_ref, vdgl_ref, h1_ref, h1b_ref, p_ref, t_ref, wd_ref, wpg_ref, bpg_ref, wpp_ref, g2_ref, b2_ref,
             dz2_ref, dpre_ref, dpp_ref, dgc_ref, dval_ref, dh1_ref, acc_ref):
        i = pl.program_id(0)

        @pl.when(i == 0)
        def _():
            acc_ref[...] = jnp.zeros_like(acc_ref)

        ffn = _dot(act_ref[...], wd_ref[...])
        h1 = h1_ref[...]
        sg = _sigmoid(_dot(h1b_ref[...], wpg_ref[...]) + bpg_ref[...])
        pp = _dot(p_ref[...].astype(BF16), wpp_ref[...])
        z2 = ALPHA * h1 + ffn + sg * pp
        xhat2, rstd2 = _ln_stats(z2)
        y = xhat2 * g2_ref[...] + b2_ref[...]
        err = y - t_ref[...]
        dy = err * (1.0 / D_MODEL)
        loss = 0.5 * jnp.sum(jnp.sum(err * err, axis=1, keepdims=True), axis=0, keepdims=True) * (1.0 / D_MODEL)
        dz2 = _ln_bwd(dy, xhat2, rstd2, g2_ref[...])
        dz2b = dz2.astype(BF16)
        dz2_ref[...] = dz2b
        dpre = dz2 * pp * sg * (1.0 - sg)
        dpreb = dpre.astype(BF16)
        dpre_ref[...] = dpreb
        dpp_ref[...] = (dz2 * sg).astype(BF16)
        dh1_ref[...] = ALPHA * dz2 + _dot_nt(dpreb, wpg_ref[...])
        dactb = _dot_nt(dz2b, wd_ref[...]).astype(BF16)
        dval_ref[...] = dactb * gl_ref[...]
        dgc_ref[...] = dactb * vdgl_ref[...]
        _put_rows(acc_ref, [_row_sum(dy * xhat2), _row_sum(dy), _row_sum(dpre),
                            jnp.broadcast_to(loss, (1, D_MODEL))])

    vec = _resident((1, D_MODEL))
    return pl.pallas_call(
        body, name="tail", grid=(S // tb,),
        in_specs=[_rows(tb, D_FF), _rows(tb, D_FF), _rows(tb, D_FF), _rows(tb, D_MODEL), _rows(tb, D_MODEL),
                  _rows(tb, PLE_DIM), _rows(tb, D_MODEL), _resident((D_FF, D_MODEL)), _resident((D_MODEL, D_MODEL)), vec,
                  _resident((PLE_DIM, D_MODEL)), vec, vec],
        out_specs=[_rows(tb, D_MODEL), _rows(tb, D_MODEL), _rows(tb, D_MODEL), _rows(tb, D_FF),
                   _rows(tb, D_FF), _rows(tb, D_MODEL), _acc((8, D_MODEL))],
        out_shape=[jax.ShapeDtypeStruct((S, D_MODEL), BF16),
                   jax.ShapeDtypeStruct((S, D_MODEL), BF16), jax.ShapeDtypeStruct((S, D_MODEL), BF16),
                   jax.ShapeDtypeStruct((S, D_FF), BF16), jax.ShapeDtypeStruct((S, D_FF), BF16),
                   jax.ShapeDtypeStruct((S, D_MODEL), F32), jax.ShapeDtypeStruct((8, D_MODEL), F32)],
        compiler_params=_params("arbitrary"),
    )(act, gl, vdgl, h1, h1b, p, tgt, w_down, w_pg, b_pg, w_pp, ln2_g, ln2_b)


def _weight_grad(a_list, b_list, name, layout, ts=512, comm=None, b_window=None, halves=False):
    S = a_list[0].shape[0]
    ms = [a.shape[1] for a in a_list]
    M, nb = sum(ms), len(b_list)
    win, Nb = b_window if b_window else (0, b_list[0].shape[1])
    ts = min(ts, S)
    nk = S // ts
    per_b = N_DEV // nb
    na = len(a_list)

    n_out = 2 if halves else 1
    assert layout == "cols" or not halves

    def body(*refs):
        a_refs, b_refs, o_refs, acc_ref = refs[:na], refs[na:na + nb], refs[na + nb:na + nb + n_out], refs[-1]
        o_ref = o_refs[0]
        j, k = pl.program_id(0), pl.program_id(1)

        @pl.when(k == 0)
        def _():
            acc_ref[...] = jnp.zeros_like(acc_ref)

        for jj in range(nb):
            @pl.when(j == jj)
            def _():
                b = b_refs[jj][...].astype(BF16)
                off = 0
                for a_ref, m in zip(a_refs, ms):
                    acc_ref[off:off + m, :] += _dot_tn(a_ref[...].astype(BF16), b)
                    off += m

        @pl.when(k == nk - 1)
        def _():
            for d in range(per_b):
                if layout == "rows":
                    o_ref[d] = acc_ref[d * (M // N_DEV):(d + 1) * (M // N_DEV), :].astype(BF16)
                elif layout == "cols" and halves:
                    for o_half, r0 in zip(o_refs, (0, M // 2)):
                        o_half[d] = acc_ref[r0:r0 + M // 2, d * (Nb // per_b):(d + 1) * (Nb // per_b)].astype(BF16)
                elif layout == "cols":
                    o_ref[d] = acc_ref[:, d * (Nb // per_b):(d + 1) * (Nb // per_b)].astype(BF16)
                else:
                    o_ref[d] = acc_ref[:, d * (Nb // per_b):(d + 1) * (Nb // per_b)].T.astype(BF16)

    def b_index(jj):
        return lambda j, k: (jnp.where(j == jj, k, jnp.where(j < jj, 0, nk - 1)), win)

    if layout == "rows":
        assert nb == 1
        blk = (N_DEV, M // N_DEV, Nb)
    elif layout == "cols":
        blk = (per_b, M // n_out, Nb // per_b)
    else:
        blk = (per_b, Nb // per_b, M)
    res, comm_res = _pcall(
        body, (*a_list, *b_list), name=name, grid=(nb, nk), sem="arbitrary", comm=comm, step_axis=1,
        in_specs=[pl.BlockSpec((ts, m), lambda j, k: (k, 0)) for m in ms]
        + [pl.BlockSpec((ts, Nb), b_index(jj)) for jj in range(nb)],
        out_specs=[pl.BlockSpec(blk, lambda j, k: (j, 0, 0))] * n_out,
        out_shape=[jax.ShapeDtypeStruct((N_DEV,) + blk[1:], BF16)] * n_out,
        scratch_shapes=[pltpu.VMEM((M, Nb), F32)])
    res = res if halves else res[0]
    return (res, comm_res) if comm is not None else res


def _up_bwd(dgc, gate, dval, dh1p, z1, w_up, fcw, w_out, ln1_g, comm=None):
    S = z1.shape[0]
    tb = min(256, S)
    t16 = tb // 16
    n16 = S // 16
    nblk, _, wblk = w_up.shape
    half = nblk // 2
    nsteps = S // tb

    def body(dgc_ref, dgn_ref, gc_ref, dval_ref, dh1p_ref, z1_ref, wu_ref, fcw_ref, wo_ref, g1_ref,
             dgate_ref, dz1_ref, dz1b_ref, datt_ref, drec_ref, accf_ref, accd_ref):
        i = pl.program_id(0)

        @pl.when(i == 0)
        def _():
            accf_ref[...] = jnp.zeros_like(accf_ref)
            accd_ref[...] = jnp.zeros_like(accd_ref)

        dg = dgc_ref[...].astype(F32)
        nxt = jnp.where(i < nsteps - 1, dgn_ref[...].astype(F32)[0:8], 0.0)
        w = _w_rows(fcw_ref)
        up1, up2 = _shift_up(dg, nxt, 1), _shift_up(dg, nxt, 2)
        dgate = (w[2] * dg + w[1] * up1 + w[0] * up2).astype(BF16)
        dgate_ref[...] = dgate
        gate = gc_ref[...].astype(F32)
        _put_rows(accf_ref, [_row_sum(up2 * gate), _row_sum(up1 * gate), _row_sum(dg * gate), _row_sum(dg)])

        dh1 = dh1p_ref[...]
        for j in range(nblk):
            src = dgate if j < half else dval_ref[...]
            jj = j % half
            dh1 = dh1 + _dot_nt(src[:, jj * wblk:(jj + 1) * wblk], wu_ref[j])
        xhat1, rstd1 = _ln_stats(z1_ref[...])
        dz1 = _ln_bwd(dh1, xhat1, rstd1, g1_ref[...])
        dz1_ref[...] = dz1
        dz1b = dz1.astype(BF16)
        dz1b_ref[...] = dz1b
        dcat = _dot_nt(dz1b, wo_ref[...])
        datt_ref[...] = dcat[:, :D_ATT].astype(BF16)
        drec_ref[...] = dcat[:, D_ATT:]
        _put_rows(accd_ref, [_row_sum(dh1 * xhat1), _row_sum(dh1)])

    next16 = pl.BlockSpec((16, D_FF), lambda i: (jnp.minimum((i + 1) * t16, n16 - 1), 0))
    return _pcall(
        body, (dgc, dgc, gate, dval, dh1p, z1, w_up, fcw, w_out, ln1_g), name="up_bwd",
        grid=(nsteps,), sem="arbitrary", comm=comm,
        in_specs=[_rows(tb, D_FF), next16, _rows(tb, D_FF), _rows(tb, D_FF), _rows(tb, D_MODEL),
                  _rows(tb, D_MODEL), _resident(w_up.shape), _resident((3, D_FF)),
                  _resident((D_MODEL, D_MODEL)), _resident((1, D_MODEL))],
        out_specs=[_rows(tb, D_FF), _rows(tb, D_MODEL), _rows(tb, D_MODEL), _rows(tb, D_ATT), _rows(tb, D_RNN),
                   _acc((8, D_FF)), _acc((8, D_MODEL))],
        out_shape=[jax.ShapeDtypeStruct((S, D_FF), BF16), jax.ShapeDtypeStruct((S, D_MODEL), F32),
                   jax.ShapeDtypeStruct((S, D_MODEL), BF16), jax.ShapeDtypeStruct((S, D_ATT), BF16),
                   jax.ShapeDtypeStruct((S, D_RNN), F32), jax.ShapeDtypeStruct((8, D_FF), F32),
                   jax.ShapeDtypeStruct((8, D_MODEL), F32)])


def _attn_bwd(q, k, v, lse, do, sinks, comm=None):
    S = q.shape[0]
    grp = N_HEADS // N_KV
    nq = min(ATT_STEP, S // QBLK)

    def body(sink_ref, q_ref, kc_ref, kp_ref, vc_ref, vp_ref, do_ref, lse_ref, dq_ref, dkc_ref, dkp_ref, dvc_ref, dvp_ref,
             ds_ref):
        i = pl.program_id(0)

        @pl.when(i == 0)
        def _():
            ds_ref[...] = jnp.zeros_like(ds_ref)

        row8 = lax.broadcasted_iota(jnp.int32, (8, 128), 0)
        lane8 = lax.broadcasted_iota(jnp.int32, (8, 128), 1)
        dsink = jnp.zeros((8, 128), F32)
        kall = jnp.concatenate([kp_ref[...], kc_ref[...]], axis=0)
        vall = jnp.concatenate([vp_ref[...], vc_ref[...]], axis=0)
        dk_t = [jnp.zeros((D_KV, QBLK), F32) for _ in range(nq + 1)]
        dv_t = [jnp.zeros((D_KV, QBLK), F32) for _ in range(nq + 1)]
        for b in range(nq):
            valid = _band_mask(i * nq + b)
            rows = slice(b * QBLK, (b + 1) * QBLK)
            keys = slice(b * QBLK, (b + 2) * QBLK)
            qv, dov = q_ref[rows, :], do_ref[rows, :]
            dqs, dks, dvs = [], [], []
            for g in range(N_KV):
                kcat = kall[keys, g * HEAD_DIM:(g + 1) * HEAD_DIM]
                vcat = vall[keys, g * HEAD_DIM:(g + 1) * HEAD_DIM]
                q4, do4 = _stack_heads(qv, g), _stack_heads(dov, g)
                s = jnp.where(valid, _dot_nt(q4, kcat), -1e30)
                lse = lse_ref[(b * N_KV + g) * GROUP * QBLK:(b * N_KV + g + 1) * GROUP * QBLK, :]
                p = jnp.exp(s - lse)
                p_sink = jnp.exp(_sink_column(sink_ref, g) - lse)
                dp = _dot_nt(do4, vcat)
                delta = jnp.sum(p * dp, axis=1, keepdims=True)
                dsc = (p * (dp - delta)).astype(BF16)
                dqs += _unstack_heads(_dot(dsc, kcat) * (HEAD_DIM ** -0.5))
                dks.append(_dot_tn(q4, dsc))
                dvs.append(_dot_tn(do4, p.astype(BF16)))
                for hh, part in enumerate(_unstack_heads(-p_sink * delta)):
                    here = (row8 == 0) & (lane8 == g * grp + hh)
                    dsink = dsink + jnp.where(here, jnp.sum(part, axis=0, keepdims=True), 0.0)
            dq_ref[rows, :] = jnp.concatenate(dqs, axis=1).astype(BF16)
            dk2, dv2 = jnp.concatenate(dks, axis=0), jnp.concatenate(dvs, axis=0)
            dk_t[b], dk_t[b + 1] = dk_t[b] + dk2[:, :QBLK], dk_t[b + 1] + dk2[:, QBLK:]
            dv_t[b], dv_t[b + 1] = dv_t[b] + dv2[:, :QBLK], dv_t[b + 1] + dv2[:, QBLK:]
        dkp_ref[...] = dk_t[0].T
        dvp_ref[...] = dv_t[0].T
        for b in range(nq):
            dkc_ref[b * QBLK:(b + 1) * QBLK, :] = dk_t[b + 1].T
            dvc_ref[b * QBLK:(b + 1) * QBLK, :] = dv_t[b + 1].T
        ds_ref[...] += dsink

    nsteps = S // (nq * QBLK)
    cur = jax.ShapeDtypeStruct((S, D_KV), F32)
    prev = jax.ShapeDtypeStruct((nsteps * QBLK, D_KV), F32)
    big = _rows(nq * QBLK, D_ATT)
    return _pcall(
        body, (sinks, q, k, k, v, v, do, lse), name="attn_bwd", grid=(nsteps,), sem="arbitrary", comm=comm,
        in_specs=[pl.BlockSpec(memory_space=pltpu.SMEM), big] + _attn_specs(nq) + [big, _rows(nq * N_HEADS * QBLK, 1)],
        out_specs=[big, _rows(nq * QBLK, D_KV), _rows(QBLK, D_KV), _rows(nq * QBLK, D_KV), _rows(QBLK, D_KV),
                   _acc((8, 128))],
        out_shape=[jax.ShapeDtypeStruct((S, D_ATT), BF16), cur, prev, cur, prev, jax.ShapeDtypeStruct((8, 128), F32)])


def _rnn_bwd(xr, gr, h, kept, drec, conv_w, wa, wx, lam, comm=None):
    S = xr.shape[0]
    tb = min(256, S)
    t8 = tb // 8
    nsteps = S // tb

    def body(xr_ref, xp_ref, gr_ref, h_ref, hp_ref, xc_ref, r_ref, ig_ref, a_ref, f_ref, drec_ref, cw_ref, wa_ref, wx_ref,
             lam_ref, dxr_ref, dgr_ref, gwa_ref, gwx_ref, acc_ref, carry_s, dxc_halo_s, d_s, gwa_s, gwx_s):
        i = pl.program_id(0)
        blk = nsteps - 1 - i

        @pl.when(i == 0)
        def _():
            gwa_s[...] = jnp.zeros_like(gwa_s)
            gwx_s[...] = jnp.zeros_like(gwx_s)
            acc_ref[...] = jnp.zeros_like(acc_ref)
            carry_s[...] = jnp.zeros_like(carry_s)
            dxc_halo_s[...] = jnp.zeros_like(dxc_halo_s)

        x = xr_ref[...]
        xhalo = jnp.where(blk > 0, xp_ref[...], 0.0)
        cw = _w_rows(cw_ref)
        xs = [_shift_down(x, xhalo, 3), _shift_down(x, xhalo, 2), _shift_down(x, xhalo, 1), x]
        xc, r, ig, a, f = xc_ref[...], r_ref[...], ig_ref[...], a_ref[...], f_ref[...]
        sp = _softplus_neg(lam_ref[...])
        hcur = h_ref[...]
        hprev = _shift_down(hcur, jnp.where(blk > 0, hp_ref[...], 0.0), 1)
        gl, dgl = _gelu_and_grad(gr_ref[...])
        drec = drec_ref[...]
        dgr_ref[...] = (drec * hcur * dgl).astype(BF16)
        d_s[...] = drec * gl
        row8 = lax.broadcasted_iota(jnp.int32, (8, D_RNN), 0)

        def tile(t, c):
            o = pl.multiple_of((t8 - 1 - t) * 8, 8)
            a8 = a_ref[pl.ds(o, 8), :]
            dt = d_s[pl.ds(o, 8), :]
            at = jnp.where(row8 == 7, 1.0, pltpu.roll(a8, 7, 0))
            for s in (1, 2, 4):
                keep = row8 < 8 - s
                a_sh = jnp.where(keep, pltpu.roll(at, 8 - s, 0), 1.0)
                d_sh = jnp.where(keep, pltpu.roll(dt, 8 - s, 0), 0.0)
                dt = at * d_sh + dt
                at = at * a_sh
            lt = at * c + dt
            d_s[pl.ds(o, 8), :] = lt
            return _row_sum(jnp.where(row8 == 0, a8 * lt, 0.0))

        carry_s[0:1, :] = lax.fori_loop(0, t8, tile, carry_s[0:1, :], unroll=2)
        lmb = d_s[...]
        a2 = a * a
        dla = lmb * hprev * a - lmb * ig * xc * (a2 / f)
        di = lmb * f * xc
        dr = dla * (-LRU_C) * sp
        dpa = dr * r * (1.0 - r)
        dpx = di * ig * (1.0 - ig)
        dpab = dpa.astype(BF16)
        dpxb = dpx.astype(BF16)
        xcb = xc.astype(BF16)
        gwa_s[...] += _dot_tn(xcb, dpab)
        gwx_s[...] += _dot_tn(xcb, dpxb)

        @pl.when(i == nsteps - 1)
        def _():
            for dense, out in ((gwa_s[...], gwa_ref), (gwx_s[...], gwx_ref)):
                for b in range(RNN_BLOCKS):
                    rows = slice(b * HEAD_DIM, (b + 1) * HEAD_DIM)
                    out[rows, :] = dense[rows, b * HEAD_DIM:(b + 1) * HEAD_DIM]

        dxc = lmb * f * ig + _dot_nt(dpab, wa_ref[...]) + _dot_nt(dpxb, wx_ref[...])
        nxt = dxc_halo_s[...]
        dxr = cw[3] * dxc
        for s in (1, 2, 3):
            dxr = dxr + cw[3 - s] * _shift_up(dxc, nxt, s)
        dxr_ref[...] = dxr.astype(BF16)
        dxc_halo_s[...] = dxc[:8]
        dlam = _row_sum(dla * (-LRU_C) * r) * (-1.0 / (1.0 + jnp.exp(lam_ref[...])))
        _put_rows(acc_ref, [_row_sum(dxc * xs[0]), _row_sum(dxc * xs[1]), _row_sum(dxc * xs[2]), _row_sum(dxc * xs[3]),
                            _row_sum(dxc), _row_sum(dpa), _row_sum(dpx), dlam])

    rev = lambda i: (nsteps - 1 - i, 0)
    prev8 = lambda i: (jnp.maximum((nsteps - 1 - i) * t8 - 1, 0), 0)
    blkspec = pl.BlockSpec((tb, D_RNN), rev)
    halo8 = pl.BlockSpec((8, D_RNN), prev8)
    vec = _resident((1, D_RNN))
    return _pcall(
        body, (xr, xr, gr, h, h, *kept, drec, conv_w, wa, wx, lam), name="rnn_bwd", grid=(nsteps,),
        sem="arbitrary", comm=comm,
        in_specs=[blkspec, halo8, blkspec, blkspec, halo8] + [blkspec] * 6
        + [_resident((4, D_RNN)), _resident((D_RNN, D_RNN)), _resident((D_RNN, D_RNN)), vec],
        out_specs=[blkspec, blkspec, _acc((D_RNN, HEAD_DIM)), _acc((D_RNN, HEAD_DIM)), _acc((8, D_RNN))],
        out_shape=[jax.ShapeDtypeStruct((S, D_RNN), BF16), jax.ShapeDtypeStruct((S, D_RNN), BF16),
                   jax.ShapeDtypeStruct((D_RNN, HEAD_DIM), F32), jax.ShapeDtypeStruct((D_RNN, HEAD_DIM), F32),
                   jax.ShapeDtypeStruct((8, D_RNN), F32)],
        scratch_shapes=[pltpu.VMEM((8, D_RNN), F32), pltpu.VMEM((8, D_RNN), F32), pltpu.VMEM((tb, D_RNN), F32),
                        pltpu.VMEM((D_RNN, D_RNN), F32), pltpu.VMEM((D_RNN, D_RNN), F32)])


def _in_bwd(dq, dkc, dkp, dvc, dvp, dxr, dgr, dz1, w_in, comm=None):
    S = dz1.shape[0]
    tb = min(ATT_STEP * QBLK, S)
    nsteps = S // tb

    def body(dq_ref, dkc_ref, dkn_ref, dvc_ref, dvn_ref, dxr_ref, dgr_ref, dz1_ref, w_ref, du_ref, dx_ref):
        last = pl.program_id(0) == nsteps - 1

        def total(cur_ref, next_ref):
            nxt = jnp.where(last, 0.0, next_ref[...])
            tail = cur_ref[tb - QBLK:, :] + nxt
            return jnp.concatenate([cur_ref[:tb - QBLK, :], tail], axis=0) if tb > QBLK else tail

        dk = total(dkc_ref, dkn_ref).astype(BF16)
        dv = total(dvc_ref, dvn_ref).astype(BF16)
        du = jnp.concatenate([dq_ref[...], dk, dv, dxr_ref[...], dgr_ref[...]], axis=1)
        du_ref[...] = du
        dx_ref[...] = ALPHA * dz1_ref[...] + _dot(du, w_ref[...])

    nextp = pl.BlockSpec((QBLK, D_KV), lambda i: (jnp.minimum(i + 1, nsteps - 1), 0))
    return _pcall(
        body, (dq, dkc, dkp, dvc, dvp, dxr, dgr, dz1, w_in), name="in_bwd", grid=(nsteps,), comm=comm,
        in_specs=[_rows(tb, D_ATT), _rows(tb, D_KV), nextp, _rows(tb, D_KV), nextp,
                  _rows(tb, D_RNN), _rows(tb, D_RNN), _rows(tb, D_MODEL), _resident((D_IN, D_MODEL))],
        out_specs=[_rows(tb, D_IN), _rows(tb, D_MODEL)],
        out_shape=[jax.ShapeDtypeStruct((S, D_IN), BF16), jax.ShapeDtypeStruct((S, D_MODEL), F32)])


def _block_diag(w):
    eye = jnp.eye(RNN_BLOCKS, dtype=w.dtype)
    return (w[:, :, None, :] * eye[:, None, :, None]).reshape(D_RNN, D_RNN).astype(BF16)


def _adamw(w, g, m, v):
    m = ADAM_B1 * m + (1.0 - ADAM_B1) * g
    v = ADAM_B2 * v + (1.0 - ADAM_B2) * (g * g)
    m_hat = m / (1.0 - ADAM_B1 ** ADAM_STEP)
    v_hat = v / (1.0 - ADAM_B2 ** ADAM_STEP)
    delta = -ADAM_LR * (m_hat / (jnp.sqrt(v_hat) + ADAM_EPS) + ADAM_WD * w)
    return delta, m, v


def _sum_adamw(parts, w, m, v, name):
    parts = parts if isinstance(parts, (list, tuple)) else [parts]
    R, C = w.shape
    rb = R if R <= 256 else 128
    per = parts[0].shape[1] // rb
    assert R % rb == 0 and parts[0].shape[1] % rb == 0
    n = len(parts)

    def body(*refs):
        p_refs = refs[:n]
        w_ref, m_ref, v_ref, g_out, d_out, m_out, v_out = refs[n:]
        which = pl.program_id(0) // per

        def total(p_ref):
            g = p_ref[0].astype(F32)
            for d in range(1, N_DEV):
                g = g + p_ref[d].astype(F32)
            return g

        g = total(p_refs[0])
        for j in range(1, n):
            g = jnp.where(which == j, total(p_refs[j]), g)
        delta, mn, vn = _adamw(w_ref[...], g, m_ref[...], v_ref[...])
        g_out[...] = g
        d_out[...] = delta
        m_out[...] = mn
        v_out[...] = vn

    def part_spec(j):
        return pl.BlockSpec((N_DEV, rb, C), lambda i: (0, jnp.clip(i - j * per, 0, per - 1), 0))

    blk = _rows(rb, C)
    out = jax.ShapeDtypeStruct((R, C), F32)
    return pl.pallas_call(
        body, name=name, grid=(R // rb,),
        in_specs=[part_spec(j) for j in range(n)] + [blk, blk, blk],
        out_specs=[blk, blk, blk, blk], out_shape=[out, out, out, out],
        compiler_params=_params("parallel"),
    )(*parts, w, m, v)


_SMALL = [("attn_sinks", "s", 0, 1, None), ("rnn_conv_w", "r", 0, 4, "cols"), ("rnn_conv_b", "r", 4, 1, None),
          ("gate_a_w", "a", 0, D_RNN, None), ("gate_a_b", "r", 5, 1, None), ("gate_x_w", "x", 0, D_RNN, None),
          ("gate_x_b", "r", 6, 1, None), ("lru_lambda", "r", 7, 1, None), ("ln1_g", "d", 0, 1, None),
          ("ln1_b", "d", 1, 1, None), ("ffn_conv_w", "f", 0, 3, "cols"), ("ffn_conv_b", "f", 3, 1, None),
          ("ple_gate_b", "t", 2, 1, None), ("ln2_g", "t", 0, 1, None), ("ln2_b", "t", 1, 1, None)]
_LOSS_ROW = 3


_ACC_COLS = {"t": (0, D_MODEL), "f": (D_MODEL, D_FF), "d": (D_MODEL + D_FF, D_MODEL), "s": (2 * D_MODEL + D_FF, 128),
             "r": (2 * D_MODEL + D_FF + 128, D_RNN)}
_ACC_WIDTH = 2 * D_MODEL + D_FF + 128 + D_RNN


def _small_update(rows_all, gates_all, params):
    flat = [arr for triple in params for arr in triple]
    n_par = len(_SMALL)

    def body(*refs):
        rows_ref, gates_ref = refs[:2]
        p_refs = refs[2:2 + 3 * n_par]
        loss_ref = refs[2 + 3 * n_par]
        o_refs = refs[3 + 3 * n_par:3 + 7 * n_par]
        rows_s, tmp_r, tmp_f = refs[3 + 7 * n_par:]
        me = _dev_index(*_place())
        rows_sum, gates_sum = rows_ref[0], gates_ref[0]
        for d in range(1, N_DEV):
            rows_sum = rows_sum + rows_ref[d]
            gates_sum = gates_sum + gates_ref[d]
        rows_s[...] = rows_sum
        t0 = _ACC_COLS["t"][0]
        loss_ref[...] = rows_s[_LOSS_ROW:_LOSS_ROW + 1, t0:t0 + 128]
        for i, (name, key, row, rows, how) in enumerate(_SMALL):
            w_ref, m_ref, v_ref = p_refs[3 * i:3 * i + 3]
            g_out, d_out, m_out, v_out = o_refs[4 * i:4 * i + 4]
            if key == "a":
                g = gates_sum[:, :HEAD_DIM]
            elif key == "x":
                g = gates_sum[:, HEAD_DIM:]
            elif how == "cols":
                c0, width = _ACC_COLS[key]
                full = rows_s[:, c0:c0 + width]
                shard = width // N_DEV
                mine = full[:, :shard]
                for d in range(1, N_DEV):
                    mine = jnp.where(me == d, full[:, d * shard:(d + 1) * shard], mine)
                tmp = tmp_r if key == "r" else tmp_f
                tmp[...] = mine
                g = tmp[row:row + rows, :]
            else:
                c0, width = _ACC_COLS[key]
                g = rows_s[row:row + rows, c0:c0 + width][:, :w_ref.shape[1]]
            delta, mn, vn = _adamw(w_ref[...], g, m_ref[...], v_ref[...])
            g_out[...] = g
            d_out[...] = delta
            m_out[...] = mn
            v_out[...] = vn

    outs = [jax.ShapeDtypeStruct((1, 128), F32)]
    for w, _, _ in params:
        outs += [jax.ShapeDtypeStruct(w.shape, F32)] * 4
    scratch = [pltpu.VMEM((8, _ACC_WIDTH), F32), pltpu.VMEM((8, D_RNN // N_DEV), F32), pltpu.VMEM((8, D_FF // N_DEV), F32)]
    res = pl.pallas_call(body, name="small_update", out_shape=outs, scratch_shapes=scratch)(rows_all, gates_all, *flat)
    return res[0], [res[1 + 4 * i:5 + 4 * i] for i in range(n_par)]


def kernel(x, p, w_in, attn_sinks, rnn_conv_w, rnn_conv_b, gate_a_w, gate_a_b, gate_x_w, gate_x_b, lru_lambda, w_out, ln1_g, ln1_b, w_ffn_up, ffn_conv_w, ffn_conv_b, w_ffn_down, ple_gate_w, ple_gate_b, ple_proj, ln2_g, ln2_b, loss_target, m_w_in, m_attn_sinks, m_rnn_conv_w, m_rnn_conv_b, m_gate_a_w, m_gate_a_b, m_gate_x_w, m_gate_x_b, m_lru_lambda, m_w_out, m_ln1_g, m_ln1_b, m_w_ffn_up, m_ffn_conv_w, m_ffn_conv_b, m_w_ffn_down, m_ple_gate_w, m_ple_gate_b, m_ple_proj, m_ln2_g, m_ln2_b, v_w_in, v_attn_sinks, v_rnn_conv_w, v_rnn_conv_b, v_gate_a_w, v_gate_a_b, v_gate_x_w, v_gate_x_b, v_lru_lambda, v_w_out, v_ln1_g, v_ln1_b, v_w_ffn_up, v_ffn_conv_w, v_ffn_conv_b, v_w_ffn_down, v_ple_gate_w, v_ple_gate_b, v_ple_proj, v_ln2_g, v_ln2_b):
    from_col_blocks = lambda g: g.transpose(1, 0, 2).reshape(g.shape[1], N_DEV * g.shape[2])

    xs, ps, tgt, sinks = x[0], p[0, 0], loss_target[0], attn_sinks[0]
    wa, wx = _block_diag(gate_a_w[0]), _block_diag(gate_x_w[0])

    conv_cols = jnp.concatenate([rnn_conv_w[0].reshape(1, -1), ffn_conv_w[0].reshape(1, -1)], axis=1)
    n_rc, n_fc = 4 * D_RNN // N_DEV, 3 * D_FF // N_DEV
    ((g_in,),) = _comm_call([_Gather([w_in[0].T.astype(BF16)])], "gather_w_in")
    w_in_full = g_in.reshape(D_IN, D_MODEL)

    (q, k, v, xr, gr), (g_conv,) = _in_proj(xs, w_in_full, comm=_Bcast([jnp.broadcast_to(conv_cols, (8, n_rc + n_fc))]))
    rcw = from_col_blocks(g_conv[:, 0, :n_rc].reshape(N_DEV, 4, D_RNN // N_DEV))
    fcw = from_col_blocks(g_conv[:, 0, n_rc:].reshape(N_DEV, 3, D_FF // N_DEV))
    (att, lse), (g_out, g_down) = _attn_fwd(q, k, v, sinks,
                                            comm=_Gather([w_out[0].astype(BF16), w_ffn_down[0].astype(BF16)]))
    (rec, h, *kept), (w_up,) = _rnn_fwd(xr, gr, rcw, rnn_conv_b, wa, wx, gate_a_b, gate_x_b, lru_lambda,
                                        comm=_Gather([w_ffn_up[0].astype(BF16)], forward_at=0.97))
    w_out_full = g_out.reshape(D_MODEL, D_MODEL)
    (z1, h1, h1b, gate, act, gl, vdgl), (g_pg, g_pp) = _mix_ln1_up(
        xs, att, rec, w_out_full, ln1_g, ln1_b, w_up, fcw, ffn_conv_b,
        comm=_Gather([ple_gate_w[0].astype(BF16), ple_proj[0].astype(BF16)]))
    dz2b, dpreb, dppb, dgc, dval, dh1p, acc_t = _tail(
        act, gl, vdgl, h1, h1b, ps, tgt, g_down.reshape(D_FF, D_MODEL), g_pg.reshape(D_MODEL, D_MODEL), ple_gate_b,
        from_col_blocks(g_pp), ln2_g, ln2_b)

    gd_down = _weight_grad([dz2b], [act], "down_grad", "rows_t")
    gd_pg = _weight_grad([h1b], [dpreb], "pg_grad", "rows", ts=1024)
    gd_pp = _weight_grad([ps], [dppb], "pp_grad", "cols", ts=1024)
    (dgate, dz1, dz1b, datt, drec, acc_f, acc_d), (r_down, r_pg, r_pp) = _up_bwd(
        dgc, gate, dval, dh1p, z1, w_up, fcw, w_out_full, ln1_g, comm=_Exchange([gd_down, gd_pg, gd_pp]))
    gd_up_top, gd_up_bot = _weight_grad([h1b], [dgate, dval], "up_grad", "cols", halves=True)
    gd_out = _weight_grad([att, rec], [dz1b], "out_grad", "rows", ts=1024)
    (dq, dkc, dkp, dvc, dvp, acc_s), (r_up_top,) = _attn_bwd(q, k, v, lse, datt, sinks, comm=_Exchange([gd_up_top]))
    early = jnp.concatenate([acc_t, acc_f, acc_d], axis=1)
    (dxr, dgr, g_wa, g_wx, acc_r), (r_up_bot, r_out, early_all) = _rnn_bwd(
        xr, gr, h, kept, drec, rcw, wa, wx, lru_lambda, comm=_Multi([_Exchange([gd_up_bot, gd_out]), _Bcast([early])]))
    (du, dx), _ = _in_bwd(dq, dkc, dkp, dvc, dvp, dxr, dgr, dz1, w_in_full)
    lanes = D_RNN // 128
    late = jnp.concatenate([g_wa, g_wx], axis=1)
    late = jnp.concatenate([late, acc_s, acc_r.reshape(8, lanes, 128).transpose(1, 0, 2).reshape(8 * lanes, 128)], axis=0)
    width = D_MODEL // IN_GRAD_PARTS
    comm, r_parts = _Bcast([late]), []
    for part in range(IN_GRAD_PARTS):
        gd_part, got = _weight_grad([du], [xs], f"in_grad_{part}", "rows", ts=1024, b_window=(part, width), comm=comm)
        if part == 0:
            (late_all,) = got
        else:
            r_parts += got
        comm = _Exchange([gd_part])
    r_parts += _comm_call([comm], "exchange_w_in")[0]
    r_in = jnp.concatenate(r_parts, axis=2)
    acc_r_all = late_all[:, D_RNN + 8:].reshape(N_DEV, lanes, 8, 128).transpose(0, 2, 1, 3).reshape(N_DEV, 8, D_RNN)
    small_parts = (jnp.concatenate([early_all, late_all[:, D_RNN:D_RNN + 8], acc_r_all], axis=2),
                   late_all[:, :D_RNN])

    outs = {}
    res = _sum_adamw(r_in, w_in[0].T, m_w_in[0].T, v_w_in[0].T, "adamw_w_in")
    outs["w_in"] = [r.T[None] for r in res]
    for name, parts, w, m, v in [("w_out", r_out, w_out, m_w_out, v_w_out),
                                 ("w_ffn_up", [r_up_top, r_up_bot], w_ffn_up, m_w_ffn_up, v_w_ffn_up),
                                 ("w_ffn_down", r_down, w_ffn_down, m_w_ffn_down, v_w_ffn_down),
                                 ("ple_gate_w", r_pg, ple_gate_w, m_ple_gate_w, v_ple_gate_w),
                                 ("ple_proj", r_pp, ple_proj, m_ple_proj, v_ple_proj)]:
        res = _sum_adamw(parts, w[0], m[0], v[0], "adamw_" + name)
        outs[name] = [r[None] for r in res]

    given = dict(attn_sinks=(attn_sinks, m_attn_sinks, v_attn_sinks), rnn_conv_w=(rnn_conv_w, m_rnn_conv_w, v_rnn_conv_w),
                 rnn_conv_b=(rnn_conv_b, m_rnn_conv_b, v_rnn_conv_b), gate_a_w=(gate_a_w, m_gate_a_w, v_gate_a_w),
                 gate_a_b=(gate_a_b, m_gate_a_b, v_gate_a_b), gate_x_w=(gate_x_w, m_gate_x_w, v_gate_x_w),
                 gate_x_b=(gate_x_b, m_gate_x_b, v_gate_x_b), lru_lambda=(lru_lambda, m_lru_lambda, v_lru_lambda),
                 ln1_g=(ln1_g, m_ln1_g, v_ln1_g), ln1_b=(ln1_b, m_ln1_b, v_ln1_b),
                 ffn_conv_w=(ffn_conv_w, m_ffn_conv_w, v_ffn_conv_w), ffn_conv_b=(ffn_conv_b, m_ffn_conv_b, v_ffn_conv_b),
                 ple_gate_b=(ple_gate_b, m_ple_gate_b, v_ple_gate_b), ln2_g=(ln2_g, m_ln2_g, v_ln2_g),
                 ln2_b=(ln2_b, m_ln2_b, v_ln2_b))
    as_2d = lambda a: a.reshape(-1, a.shape[-1])
    loss_row, small_res = _small_update(*small_parts, [tuple(as_2d(a) for a in given[n]) for n, *_ in _SMALL])
    loss = loss_row[0, 0]
    for (n, *_), res in zip(_SMALL, small_res):
        outs[n] = [r.reshape(given[n][0].shape) for r in res]

    order = ["w_in", "attn_sinks", "rnn_conv_w", "rnn_conv_b", "gate_a_w", "gate_a_b", "gate_x_w", "gate_x_b",
             "lru_lambda", "w_out", "ln1_g", "ln1_b", "w_ffn_up", "ffn_conv_w", "ffn_conv_b", "w_ffn_down",
             "ple_gate_w", "ple_gate_b", "ple_proj", "ln2_g", "ln2_b"]
    return (loss, dx[None], *[outs[n][0] for n in order], *[outs[n][1] for n in order],
            *[outs[n][2] for n in order], *[outs[n][3] for n in order])
```

```python
import jax
import jax.numpy as jnp
from jax import lax
from jax.experimental import pallas as pl
from jax.experimental.pallas import tpu as pltpu

F32 = jnp.float32
BF16 = jnp.bfloat16

D_MODEL = 1024
D_ATT = 512
D_KV = 128
HEAD_DIM = 64
N_HEADS = 8
N_KV = 2
D_RNN = 512
RNN_BLOCKS = 8
D_IN = 1792
D_FF = 3072
PLE_DIM = 256
QBLK = 128
N_DEV = 8
ALPHA = float(2 ** 0.25)
LN_EPS = 1e-5
LRU_C = 8.0
ADAM_LR, ADAM_B1, ADAM_B2, ADAM_EPS, ADAM_WD, ADAM_STEP = 0.001, 0.9, 0.999, 1e-08, 0.01, 10

V7X_VMEM_LIMIT = 56 * 1024 * 1024
MESH = pl.DeviceIdType.MESH


def _params(*sem, vmem=V7X_VMEM_LIMIT):
    return pltpu.CompilerParams(dimension_semantics=sem or None, vmem_limit_bytes=vmem)


def _resident(shape):
    return pl.BlockSpec(shape, lambda *_: (0,) * len(shape), pipeline_mode=pl.Buffered(1))


def _rows(tb, cols):
    return pl.BlockSpec((tb, cols), lambda i: (i, 0))


def _acc(shape):
    return pl.BlockSpec(shape, lambda *_: (0,) * len(shape))


def _dot(a, b):
    return jnp.dot(a, b, preferred_element_type=F32)


def _dot_nt(a, b):
    return lax.dot_general(a, b, (((1,), (1,)), ((), ())), preferred_element_type=F32)


def _dot_tn(a, b):
    return lax.dot_general(a, b, (((0,), (0,)), ((), ())), preferred_element_type=F32)


def _sigmoid(x):
    return 1.0 / (1.0 + jnp.exp(-x))


_GELU_C = 0.7978845608028654
_GELU_K = 0.044715


def _gelu_and_grad(x):
    u = x * x
    t = jnp.tanh(x * (_GELU_C + (_GELU_C * _GELU_K) * u))
    hp = 0.5 + 0.5 * t
    dg = hp + x * (0.5 - 0.5 * (t * t)) * (_GELU_C + (3.0 * _GELU_C * _GELU_K) * u)
    return x * hp, dg


def _gelu(x):
    return 0.5 * x * (1.0 + jnp.tanh(_GELU_C * (x + _GELU_K * x * x * x)))


def _ln_stats(z):
    mu = jnp.mean(z, axis=-1, keepdims=True)
    zc = z - mu
    var = jnp.mean(zc * zc, axis=-1, keepdims=True)
    rstd = lax.rsqrt(var + LN_EPS)
    return zc * rstd, rstd


def _ln_bwd(dy, xhat, rstd, g):
    dxh = dy * g
    m1 = jnp.mean(dxh, axis=-1, keepdims=True)
    m2 = jnp.mean(dxh * xhat, axis=-1, keepdims=True)
    return rstd * (dxh - m1 - xhat * m2)


def _softplus_neg(lam):
    u = jnp.exp(-jnp.abs(lam))
    w = 1.0 + u
    d = w - 1.0
    log1p_u = jnp.where(d == 0.0, u, jnp.log(w) * (u / jnp.where(d == 0.0, 1.0, d)))
    return jnp.maximum(-lam, 0.0) + log1p_u


def _shift_down(x, halo, s):
    xs = pltpu.roll(x, s, 0)
    hs = pltpu.roll(halo, s, 0)
    row8 = lax.broadcasted_iota(jnp.int32, hs.shape, 0)
    first = jnp.where(row8 < s, hs, xs[:8])
    return jnp.concatenate([first, xs[8:]], axis=0)


def _shift_up(x, halo, s):
    n = x.shape[0]
    xs = pltpu.roll(x, n - s, 0)
    hs = pltpu.roll(halo, 8 - s, 0)
    row8 = lax.broadcasted_iota(jnp.int32, hs.shape, 0)
    last = jnp.where(row8 >= 8 - s, hs, xs[n - 8:])
    return jnp.concatenate([xs[:n - 8], last], axis=0)


def _row_sum(x):
    return jnp.sum(x, axis=0, keepdims=True)


def _put_rows(acc_ref, rows):
    row8 = lax.broadcasted_iota(jnp.int32, acc_ref.shape, 0)
    upd = jnp.zeros(acc_ref.shape, F32)
    for r, vec in enumerate(rows):
        upd = jnp.where(row8 == r, vec, upd)
    acc_ref[...] += upd


def _place():
    return lax.axis_index("x"), lax.axis_index("y"), lax.axis_index("c")


def _dev_index(px, py, pc):
    return 4 * px + 2 * py + pc


_ANY = pl.BlockSpec(memory_space=pl.ANY)


class _Gather:
    def __init__(self, arrays, forward_at=0.875):
        self.arrays = list(arrays)
        self.n = len(self.arrays)
        self.forward_at = forward_at

    def out_shape(self):
        return [jax.ShapeDtypeStruct((N_DEV,) + s.shape, s.dtype) for s in self.arrays]

    def scratch(self):
        return [pltpu.SemaphoreType.DMA((self.n, 7)), pltpu.SemaphoreType.DMA((self.n, 7)),
                pltpu.SemaphoreType.DMA((self.n,))]

    def _parts(self, ins, outs, sems):
        send_sems, recv_sems, local_sems = sems
        x, y, c = _place()
        me, sibling = (x, y, c), (x, y, 1 - c)
        chips = [(1 - x, y), (x, 1 - y), (1 - x, 1 - y)]

        def copy(a, k, block, to, src=None):
            rows = outs[a].at[_dev_index(*block)]
            return pltpu.make_async_remote_copy(
                src_ref=rows if src is None else src, dst_ref=rows, send_sem=send_sems.at[a, k],
                recv_sem=recv_sems.at[a, k], device_id=to, device_id_type=MESH)

        rng = range(self.n)
        mine = [pltpu.make_async_copy(ins[a], outs[a].at[_dev_index(*me)], local_sems.at[a]) for a in rng]
        first = [copy(a, 0, me, sibling, src=ins[a]) for a in rng]
        first += [copy(a, 1 + j, me, (*chip, c), src=ins[a]) for j, chip in enumerate(chips) for a in rng]
        landed = [copy(a, 1 + j, (*chip, c), me) for j, chip in enumerate(chips) for a in rng]
        passed = [copy(a, 4 + j, (*chip, c), sibling) for j, chip in enumerate(chips) for a in rng]
        from_sibling = [copy(a, 0, sibling, me) for a in rng]
        from_sibling += [copy(a, 4 + j, (*chip, 1 - c), me) for j, chip in enumerate(chips) for a in rng]
        return mine, first, landed, passed, from_sibling

    def start(self, ins, outs, sems):
        mine, first, _, _, _ = self._parts(ins, outs, sems)
        for cp in mine + first:
            cp.start()

    def forward(self, ins, outs, sems):
        _, _, landed, passed, _ = self._parts(ins, outs, sems)
        for got, fwd in zip(landed, passed):
            got.wait_recv()
            fwd.start()

    def finish(self, ins, outs, sems):
        mine, first, _, passed, from_sibling = self._parts(ins, outs, sems)
        for cp in from_sibling:
            cp.wait_recv()
        for cp in first + passed:
            cp.wait_send()
        for cp in mine:
            cp.wait()

    def before(self, ins, outs, sems, step, nsteps):
        pl.when(step == 0)(lambda: self.start(ins, outs, sems))
        pl.when(step == min(int(self.forward_at * nsteps), nsteps - 1))(lambda: self.forward(ins, outs, sems))

    def after(self, ins, outs, sems, step, nsteps):
        pl.when(step == nsteps - 1)(lambda: self.finish(ins, outs, sems))


class _Exchange:
    def __init__(self, arrays):
        self.arrays = list(arrays)
        self.n = len(self.arrays)

    def out_shape(self):
        return [jax.ShapeDtypeStruct(b.shape, b.dtype) for b in self.arrays]

    def scratch(self):
        return [pltpu.SemaphoreType.DMA((self.n, 7)), pltpu.SemaphoreType.DMA((self.n, 7)),
                pltpu.SemaphoreType.DMA((self.n,))]

    def _parts(self, ins, outs, sems):
        send_sems, recv_sems, local_sems = sems
        x, y, c = _place()
        me = _dev_index(x, y, c)
        peers = [(x ^ (k >> 2), y ^ ((k >> 1) & 1), c ^ (k & 1)) for k in range(1, N_DEV)]
        rng = range(self.n)
        mine = [pltpu.make_async_copy(ins[a].at[me], outs[a].at[me], local_sems.at[a]) for a in rng]
        sent = [pltpu.make_async_remote_copy(
            src_ref=ins[a].at[_dev_index(*to)], dst_ref=outs[a].at[me], send_sem=send_sems.at[a, k],
            recv_sem=recv_sems.at[a, k], device_id=to, device_id_type=MESH) for k, to in enumerate(peers) for a in rng]
        arrivals = [pltpu.make_async_remote_copy(
            src_ref=ins[a].at[me], dst_ref=outs[a].at[_dev_index(*frm)], send_sem=send_sems.at[a, k],
            recv_sem=recv_sems.at[a, k], device_id=frm, device_id_type=MESH) for k, frm in enumerate(peers) for a in rng]
        return mine, sent, arrivals

    def start(self, ins, outs, sems):
        mine, sent, _ = self._parts(ins, outs, sems)
        for cp in mine + sent:
            cp.start()

    def finish(self, ins, outs, sems):
        mine, sent, arrivals = self._parts(ins, outs, sems)
        for cp in arrivals:
            cp.wait_recv()
        for cp in sent:
            cp.wait_send()
        for cp in mine:
            cp.wait()

    def before(self, ins, outs, sems, step, nsteps):
        pl.when(step == 0)(lambda: self.start(ins, outs, sems))

    def after(self, ins, outs, sems, step, nsteps):
        pl.when(step == nsteps - 1)(lambda: self.finish(ins, outs, sems))


class _Bcast(_Exchange):
    def out_shape(self):
        return [jax.ShapeDtypeStruct((N_DEV,) + s.shape, s.dtype) for s in self.arrays]

    def _parts(self, ins, outs, sems):
        send_sems, recv_sems, local_sems = sems
        x, y, c = _place()
        me = _dev_index(x, y, c)
        peers = [(x ^ (k >> 2), y ^ ((k >> 1) & 1), c ^ (k & 1)) for k in range(1, N_DEV)]
        rng = range(self.n)
        mine = [pltpu.make_async_copy(ins[a], outs[a].at[me], local_sems.at[a]) for a in rng]
        sent = [pltpu.make_async_remote_copy(
            src_ref=ins[a], dst_ref=outs[a].at[me], send_sem=send_sems.at[a, k], recv_sem=recv_sems.at[a, k],
            device_id=to, device_id_type=MESH) for k, to in enumerate(peers) for a in rng]
        arrivals = [pltpu.make_async_remote_copy(
            src_ref=ins[a], dst_ref=outs[a].at[_dev_index(*frm)], send_sem=send_sems.at[a, k],
            recv_sem=recv_sems.at[a, k], device_id=frm, device_id_type=MESH) for k, frm in enumerate(peers) for a in rng]
        return mine, sent, arrivals


class _Multi:
    def __init__(self, comms):
        self.comms = list(comms)
        self.arrays = [arr for c in self.comms for arr in c.arrays]
        self.n = len(self.arrays)

    def out_shape(self):
        return [s for c in self.comms for s in c.out_shape()]

    def scratch(self):
        return [s for c in self.comms for s in c.scratch()]

    def _each(self, ins, outs, sems):
        a = 0
        for j, c in enumerate(self.comms):
            yield c, ins[a:a + c.n], outs[a:a + c.n], sems[3 * j:3 * j + 3]
            a += c.n

    def before(self, ins, outs, sems, step, nsteps):
        for c, ci, co, cs in self._each(ins, outs, sems):
            c.before(ci, co, cs, step, nsteps)

    def after(self, ins, outs, sems, step, nsteps):
        for c, ci, co, cs in self._each(ins, outs, sems):
            c.after(ci, co, cs, step, nsteps)


def _comm_call(comms, name):
    ns = [c.n for c in comms]
    n = sum(ns)

    def body(*refs):
        parts, a, s = [], 0, 2 * n
        for c in comms:
            parts.append((c, refs[a:a + c.n], refs[n + a:n + a + c.n], refs[s:s + 3]))
            a, s = a + c.n, s + 3
        for c, ins, outs, sems in parts:
            c.start(ins, outs, sems)
        for c, ins, outs, sems in parts:
            if isinstance(c, _Gather):
                c.forward(ins, outs, sems)
        for c, ins, outs, sems in parts:
            c.finish(ins, outs, sems)

    res = pl.pallas_call(
        body, name=name, in_specs=[_ANY] * n, out_specs=[_ANY] * n,
        out_shape=[s for c in comms for s in c.out_shape()], scratch_shapes=[s for c in comms for s in c.scratch()],
    )(*[arr for c in comms for arr in c.arrays])
    out, a = [], 0
    for k in ns:
        out.append(res[a:a + k])
        a += k
    return out


def _pcall(body, args, *, name, grid, in_specs, out_specs, out_shape, scratch_shapes=(), sem="parallel", comm=None,
           step_axis=0):
    sem = (sem,) * len(grid) if isinstance(sem, str) else sem
    if comm is None:
        res = pl.pallas_call(body, name=name, grid=grid, in_specs=in_specs, out_specs=out_specs, out_shape=out_shape,
                             scratch_shapes=list(scratch_shapes), compiler_params=_params(*sem))(*args)
        return res, []
    n_in, n_out, n_scr, n = len(in_specs), len(out_specs), len(scratch_shapes), comm.n
    nsteps = grid[step_axis]
    assert all(g == 1 for ax, g in enumerate(grid) if ax != step_axis)

    def hosted(*refs):
        ins, cin = refs[:n_in], refs[n_in:n_in + n]
        o0 = n_in + n
        outs, cout = refs[o0:o0 + n_out], refs[o0 + n_out:o0 + n_out + n]
        s0 = o0 + n_out + n
        scr, sems = refs[s0:s0 + n_scr], refs[s0 + n_scr:]
        step = pl.program_id(step_axis)
        comm.before(cin, cout, sems, step, nsteps)
        body(*ins, *outs, *scr)
        comm.after(cin, cout, sems, step, nsteps)

    res = pl.pallas_call(
        hosted, name=name, grid=grid, in_specs=list(in_specs) + [_ANY] * n, out_specs=list(out_specs) + [_ANY] * n,
        out_shape=list(out_shape) + comm.out_shape(), scratch_shapes=list(scratch_shapes) + comm.scratch(),
        compiler_params=_params(*(("arbitrary",) * len(grid))))(*args, *comm.arrays)
    return res[:n_out], res[n_out:]


def _in_proj(x, w_in_t, comm=None):
    S = x.shape[0]
    tb = min(512, S)

    def body(x_ref, w_ref, q_ref, k_ref, v_ref, xr_ref, gr_ref):
        u = _dot_nt(x_ref[...].astype(BF16), w_ref[...])
        q_ref[...] = (u[:, :D_ATT] * (HEAD_DIM ** -0.5)).astype(BF16)
        k_ref[...] = u[:, D_ATT:D_ATT + D_KV].astype(BF16)
        v_ref[...] = u[:, D_ATT + D_KV:D_ATT + 2 * D_KV].astype(BF16)
        xr_ref[...] = u[:, D_ATT + 2 * D_KV:D_ATT + 2 * D_KV + D_RNN]
        gr_ref[...] = u[:, D_ATT + 2 * D_KV + D_RNN:]

    return _pcall(
        body, (x, w_in_t), name="in_proj", grid=(S // tb,), comm=comm,
        in_specs=[_rows(tb, D_MODEL), _resident((D_IN, D_MODEL))],
        out_specs=[_rows(tb, D_ATT), _rows(tb, D_KV), _rows(tb, D_KV), _rows(tb, D_RNN), _rows(tb, D_RNN)],
        out_shape=[jax.ShapeDtypeStruct((S, D_ATT), BF16), jax.ShapeDtypeStruct((S, D_KV), BF16),
                   jax.ShapeDtypeStruct((S, D_KV), BF16), jax.ShapeDtypeStruct((S, D_RNN), F32),
                   jax.ShapeDtypeStruct((S, D_RNN), F32)])


GROUP = N_HEADS // N_KV


def _band_mask(i):
    qi = lax.broadcasted_iota(jnp.int32, (GROUP * QBLK, 2 * QBLK), 0) & (QBLK - 1)
    sj = lax.broadcasted_iota(jnp.int32, (GROUP * QBLK, 2 * QBLK), 1)
    return (sj > qi) & (sj <= qi + QBLK) & ((sj >= QBLK) | (i > 0))


def _stack_heads(x, g):
    return jnp.concatenate([x[:, (g * GROUP + hh) * HEAD_DIM:(g * GROUP + hh + 1) * HEAD_DIM] for hh in range(GROUP)],
                           axis=0)


def _unstack_heads(x4):
    return [x4[hh * QBLK:(hh + 1) * QBLK] for hh in range(GROUP)]


def _sink_column(sink_ref, g):
    head = lax.broadcasted_iota(jnp.int32, (GROUP * QBLK, 1), 0) // QBLK
    col = jnp.full((GROUP * QBLK, 1), sink_ref[g * GROUP], F32)
    for hh in range(1, GROUP):
        col = jnp.where(head == hh, sink_ref[g * GROUP + hh], col)
    return col


ATT_STEP = 4
IN_GRAD_PARTS = 2


def _attn_specs(nq=1):
    cur = lambda i: (i, 0)
    prev = lambda i: (jnp.maximum(nq * i - 1, 0), 0)
    return [pl.BlockSpec((nq * QBLK, D_KV), cur), pl.BlockSpec((QBLK, D_KV), prev),
            pl.BlockSpec((nq * QBLK, D_KV), cur), pl.BlockSpec((QBLK, D_KV), prev)]


def _attn_fwd(q, k, v, sinks, comm=None):
    S = q.shape[0]
    nq = min(ATT_STEP, S // QBLK)

    def body(sink_ref, q_ref, kc_ref, kp_ref, vc_ref, vp_ref, o_ref, lse_ref):
        first = pl.program_id(0) * nq
        kall = jnp.concatenate([kp_ref[...], kc_ref[...]], axis=0)
        vall = jnp.concatenate([vp_ref[...], vc_ref[...]], axis=0)
        for b in range(nq):
            valid = _band_mask(first + b)
            rows = slice(b * QBLK, (b + 1) * QBLK)
            keys = slice(b * QBLK, (b + 2) * QBLK)
            qv = q_ref[rows, :]
            outs = []
            for g in range(N_KV):
                kcat = kall[keys, g * HEAD_DIM:(g + 1) * HEAD_DIM]
                vcat = vall[keys, g * HEAD_DIM:(g + 1) * HEAD_DIM]
                s = jnp.where(valid, _dot_nt(_stack_heads(qv, g), kcat), -1e30)
                sink = _sink_column(sink_ref, g)
                m = jnp.maximum(jnp.max(s, axis=1, keepdims=True), sink)
                p = jnp.exp(s - m)
                l = jnp.sum(p, axis=1, keepdims=True) + jnp.exp(sink - m)
                outs += _unstack_heads(_dot(p.astype(BF16), vcat) / l)
                lse_ref[(b * N_KV + g) * GROUP * QBLK:(b * N_KV + g + 1) * GROUP * QBLK, :] = m + jnp.log(l)
            o_ref[rows, :] = jnp.concatenate(outs, axis=1).astype(BF16)

    lse_rows = nq * N_HEADS * QBLK
    return _pcall(
        body, (sinks, q, k, k, v, v), name="attn_fwd", grid=(S // (nq * QBLK),), comm=comm,
        in_specs=[pl.BlockSpec(memory_space=pltpu.SMEM), _rows(nq * QBLK, D_ATT)] + _attn_specs(nq),
        out_specs=[_rows(nq * QBLK, D_ATT), _rows(lse_rows, 1)],
        out_shape=[jax.ShapeDtypeStruct((S, D_ATT), BF16), jax.ShapeDtypeStruct((S * N_HEADS, 1), F32)])


def _w_rows(w_ref):
    return [w_ref[k:k + 1, :] for k in range(w_ref.shape[0])]


def _conv4(x, halo, w, b):
    y = b + w[3] * x
    for s in (1, 2, 3):
        y = y + w[3 - s] * _shift_down(x, halo, s)
    return y


def _rnn_gates(xc, wa, wx, ba, bx, sp):
    xcb = xc.astype(BF16)
    r = _sigmoid(_dot(xcb, wa) + ba)
    ig = _sigmoid(_dot(xcb, wx) + bx)
    la = -LRU_C * r * sp
    a = jnp.exp(la)
    t = jnp.tanh(la)
    f = jnp.sqrt(-2.0 * t / (1.0 - t))
    return r, ig, a, f


def _rnn_fwd(xr, gr, conv_w, conv_b, wa, wx, ba, bx, lam, comm=None):
    S = xr.shape[0]
    tb = min(256, S)

    def body(xr_ref, gr_ref, cw_ref, cb_ref, wa_ref, wx_ref, ba_ref, bx_ref, lam_ref, rec_ref, h_ref,
             xc_ref, r_ref, ig_ref, a_ref, f_ref, halo_s, hc_s, a_s, b_s):
        @pl.when(pl.program_id(0) == 0)
        def _():
            halo_s[...] = jnp.zeros_like(halo_s)
            hc_s[...] = jnp.zeros_like(hc_s)

        x = xr_ref[...]
        xc = _conv4(x, halo_s[...], _w_rows(cw_ref), cb_ref[...])
        halo_s[...] = x[tb - 8:]
        r, ig, a, f = _rnn_gates(xc, wa_ref[...], wx_ref[...], ba_ref[...], bx_ref[...], _softplus_neg(lam_ref[...]))
        xc_ref[...] = xc
        r_ref[...] = r
        ig_ref[...] = ig
        a_ref[...] = a
        f_ref[...] = f
        a_s[...] = a
        b_s[...] = f * ig * xc
        row8 = lax.broadcasted_iota(jnp.int32, (8, D_RNN), 0)

        def tile(t, hc):
            o = pl.multiple_of(t * 8, 8)
            at = a_s[pl.ds(o, 8), :]
            bt = b_s[pl.ds(o, 8), :]
            for s in (1, 2, 4):
                keep = row8 >= s
                a_sh = jnp.where(keep, pltpu.roll(at, s, 0), 1.0)
                b_sh = jnp.where(keep, pltpu.roll(bt, s, 0), 0.0)
                bt = at * b_sh + bt
                at = at * a_sh
            ht = at * hc + bt
            b_s[pl.ds(o, 8), :] = ht
            return _row_sum(jnp.where(row8 == 7, ht, 0.0))

        hc_s[0:1, :] = lax.fori_loop(0, tb // 8, tile, hc_s[0:1, :], unroll=2)
        h = b_s[...]
        h_ref[...] = h
        rec_ref[...] = (h * _gelu(gr_ref[...])).astype(BF16)

    vec = _resident((1, D_RNN))
    kept = jax.ShapeDtypeStruct((S, D_RNN), F32)
    return _pcall(
        body, (xr, gr, conv_w, conv_b, wa, wx, ba, bx, lam), name="rnn_fwd", grid=(S // tb,), sem="arbitrary", comm=comm,
        in_specs=[_rows(tb, D_RNN), _rows(tb, D_RNN), _resident((4, D_RNN)), vec,
                  _resident((D_RNN, D_RNN)), _resident((D_RNN, D_RNN)), vec, vec, vec],
        out_specs=[_rows(tb, D_RNN)] * 7,
        out_shape=[jax.ShapeDtypeStruct((S, D_RNN), BF16), kept, kept, kept, kept, kept, kept],
        scratch_shapes=[pltpu.VMEM((8, D_RNN), F32), pltpu.VMEM((8, D_RNN), F32),
                        pltpu.VMEM((tb, D_RNN), F32), pltpu.VMEM((tb, D_RNN), F32)])


def _mix_ln1_up(x, att, rec, w_out, ln1_g, ln1_b, w_up, fcw, fcb, comm=None):
    S = x.shape[0]
    tb = min(256, S)
    nblk, _, wblk = w_up.shape
    half = nblk // 2

    def body(x_ref, att_ref, rec_ref, wo_ref, g_ref, b_ref, wu_ref, fcw_ref, fcb_ref,
             z1_ref, h1_ref, h1b_ref, gate_ref, act_ref, gl_ref, vdgl_ref, halo_s):
        @pl.when(pl.program_id(0) == 0)
        def _():
            halo_s[...] = jnp.zeros_like(halo_s)

        z1 = ALPHA * x_ref[...] + _dot(att_ref[...], wo_ref[:D_ATT, :]) + _dot(rec_ref[...], wo_ref[D_ATT:, :])
        z1_ref[...] = z1
        xhat, _ = _ln_stats(z1)
        h1 = xhat * g_ref[...] + b_ref[...]
        h1_ref[...] = h1
        h1b = h1.astype(BF16)
        h1b_ref[...] = h1b
        for jj in range(half):
            cols = slice(jj * wblk, (jj + 1) * wblk)
            gate = _dot(h1b, wu_ref[jj])
            val = _dot(h1b, wu_ref[jj + half])
            halo = halo_s[:, cols]
            conv = (fcb_ref[:, cols] + fcw_ref[2:3, cols] * gate + fcw_ref[1:2, cols] * _shift_down(gate, halo, 1)
                    + fcw_ref[0:1, cols] * _shift_down(gate, halo, 2))
            halo_s[:, cols] = gate[tb - 8:]
            gl, dgl = _gelu_and_grad(conv)
            gate_ref[:, cols] = gate.astype(BF16)
            act_ref[:, cols] = (gl * val).astype(BF16)
            gl_ref[:, cols] = gl.astype(BF16)
            vdgl_ref[:, cols] = (val * dgl).astype(BF16)

    vec = _resident((1, D_MODEL))
    wide = jax.ShapeDtypeStruct((S, D_FF), BF16)
    return _pcall(
        body, (x, att, rec, w_out, ln1_g, ln1_b, w_up, fcw, fcb), name="mix_ln1_up", grid=(S // tb,),
        sem="arbitrary", comm=comm,
        in_specs=[_rows(tb, D_MODEL), _rows(tb, D_ATT), _rows(tb, D_RNN), _resident((D_MODEL, D_MODEL)), vec, vec,
                  _resident(w_up.shape), _resident((3, D_FF)), _resident((1, D_FF))],
        out_specs=[_rows(tb, D_MODEL), _rows(tb, D_MODEL), _rows(tb, D_MODEL)] + [_rows(tb, D_FF)] * 4,
        out_shape=[jax.ShapeDtypeStruct((S, D_MODEL), F32), jax.ShapeDtypeStruct((S, D_MODEL), F32),
                   jax.ShapeDtypeStruct((S, D_MODEL), BF16), wide, wide, wide, wide],
        scratch_shapes=[pltpu.VMEM((8, D_FF), F32)])


def _tail(act, gl, vdgl, h1, h1b, p, tgt, w_down, w_pg, b_pg, w_pp, ln2_g, ln2_b):
    S = h1.shape[0]
    tb = min(256, S)

    def body(act_ref, gl_ref, vdgl_ref, h1_ref, h1b_ref, p_ref, t_ref, wd_ref, wpg_ref, bpg_ref, wpp_ref, g2_ref, b2_ref,
             dz2_ref, dpre_ref, dpp_ref, dgc_ref, dval_ref, dh1_ref, acc_ref):
        i = pl.program_id(0)

        @pl.when(i == 0)
        def _():
            acc_ref[...] = jnp.zeros_like(acc_ref)

        ffn = _dot(act_ref[...], wd_ref[...])
        h1 = h1_ref[...]
        sg = _sigmoid(_dot(h1b_ref[...], wpg_ref[...]) + bpg_ref[...])
        pp = _dot(p_ref[...].astype(BF16), wpp_ref[...])
        z2 = ALPHA * h1 + ffn + sg * pp
        xhat2, rstd2 = _ln_stats(z2)
        y = xhat2 * g2_ref[...] + b2_ref[...]
        err = y - t_ref[...]
        dy = err * (1.0 / D_MODEL)
        loss = 0.5 * jnp.sum(jnp.sum(err * err, axis=1, keepdims=True), axis=0, keepdims=True) * (1.0 / D_MODEL)
        dz2 = _ln_bwd(dy, xhat2, rstd2, g2_ref[...])
        dz2b = dz2.astype(BF16)
        dz2_ref[...] = dz2b
        dpre = dz2 * pp * sg * (1.0 - sg)
        dpreb = dpre.astype(BF16)
        dpre_ref[...] = dpreb
        dpp_ref[...] = (dz2 * sg).astype(BF16)
        dh1_ref[...] = ALPHA * dz2 + _dot_nt(dpreb, wpg_ref[...])
        dactb = _dot_nt(dz2b, wd_ref[...]).astype(BF16)
        dval_ref[...] = dactb * gl_ref[...]
        dgc_ref[...] = dactb * vdgl_ref[...]
        _put_rows(acc_ref, [_row_sum(dy * xhat2), _row_sum(dy), _row_sum(dpre),
                            jnp.broadcast_to(loss, (1, D_MODEL))])

    vec = _resident((1, D_MODEL))
    return pl.pallas_call(
        body, name="tail", grid=(S // tb,),
        in_specs=[_rows(tb, D_FF), _rows(tb, D_FF), _rows(tb, D_FF), _rows(tb, D_MODEL), _rows(tb, D_MODEL),
                  _rows(tb, PLE_DIM), _rows(tb, D_MODEL), _resident((D_FF, D_MODEL)), _resident((D_MODEL, D_MODEL)), vec,
                  _resident((PLE_DIM, D_MODEL)), vec, vec],
        out_specs=[_rows(tb, D_MODEL), _rows(tb, D_MODEL), _rows(tb, D_MODEL), _rows(tb, D_FF),
                   _rows(tb, D_FF), _rows(tb, D_MODEL), _acc((8, D_MODEL))],
        out_shape=[jax.ShapeDtypeStruct((S, D_MODEL), BF16),
                   jax.ShapeDtypeStruct((S, D_MODEL), BF16), jax.ShapeDtypeStruct((S, D_MODEL), BF16),
                   jax.ShapeDtypeStruct((S, D_FF), BF16), jax.ShapeDtypeStruct((S, D_FF), BF16),
                   jax.ShapeDtypeStruct((S, D_MODEL), F32), jax.ShapeDtypeStruct((8, D_MODEL), F32)],
        compiler_params=_params("arbitrary"),
    )(act, gl, vdgl, h1, h1b, p, tgt, w_down, w_pg, b_pg, w_pp, ln2_g, ln2_b)


def _weight_grad(a_list, b_list, name, layout, ts=512, comm=None, b_window=None, halves=False):
    S = a_list[0].shape[0]
    ms = [a.shape[1] for a in a_list]
    M, nb = sum(ms), len(b_list)
    win, Nb = b_window if b_window else (0, b_list[0].shape[1])
    ts = min(ts, S)
    nk = S // ts
    per_b = N_DEV // nb
    na = len(a_list)

    n_out = 2 if halves else 1
    assert layout == "cols" or not halves

    def body(*refs):
        a_refs, b_refs, o_refs, acc_ref = refs[:na], refs[na:na + nb], refs[na + nb:na + nb + n_out], refs[-1]
        o_ref = o_refs[0]
        j, k = pl.program_id(0), pl.program_id(1)

        @pl.when(k == 0)
        def _():
            acc_ref[...] = jnp.zeros_like(acc_ref)

        for jj in range(nb):
            @pl.when(j == jj)
            def _():
                b = b_refs[jj][...].astype(BF16)
                off = 0
                for a_ref, m in zip(a_refs, ms):
                    acc_ref[off:off + m, :] += _dot_tn(a_ref[...].astype(BF16), b)
                    off += m

        @pl.when(k == nk - 1)
        def _():
            for d in range(per_b):
                if layout == "rows":
                    o_ref[d] = acc_ref[d * (M // N_DEV):(d + 1) * (M // N_DEV), :].astype(BF16)
                elif layout == "cols" and halves:
                    for o_half, r0 in zip(o_refs, (0, M // 2)):
                        o_half[d] = acc_ref[r0:r0 + M // 2, d * (Nb // per_b):(d + 1) * (Nb // per_b)].astype(BF16)
                elif layout == "cols":
                    o_ref[d] = acc_ref[:, d * (Nb // per_b):(d + 1) * (Nb // per_b)].astype(BF16)
                else:
                    o_ref[d] = acc_ref[:, d * (Nb // per_b):(d + 1) * (Nb // per_b)].T.astype(BF16)

    def b_index(jj):
        return lambda j, k: (jnp.where(j == jj, k, jnp.where(j < jj, 0, nk - 1)), win)

    if layout == "rows":
        assert nb == 1
        blk = (N_DEV, M // N_DEV, Nb)
    elif layout == "cols":
        blk = (per_b, M // n_out, Nb // per_b)
    else:
        blk = (per_b, Nb // per_b, M)
    res, comm_res = _pcall(
        body, (*a_list, *b_list), name=name, grid=(nb, nk), sem="arbitrary", comm=comm, step_axis=1,
        in_specs=[pl.BlockSpec((ts, m), lambda j, k: (k, 0)) for m in ms]
        + [pl.BlockSpec((ts, Nb), b_index(jj)) for jj in range(nb)],
        out_specs=[pl.BlockSpec(blk, lambda j, k: (j, 0, 0))] * n_out,
        out_shape=[jax.ShapeDtypeStruct((N_DEV,) + blk[1:], BF16)] * n_out,
        scratch_shapes=[pltpu.VMEM((M, Nb), F32)])
    res = res if halves else res[0]
    return (res, comm_res) if comm is not None else res


def _up_bwd(dgc, gate, dval, dh1p, z1, w_up, fcw, w_out, ln1_g, comm=None):
    S = z1.shape[0]
    tb = min(256, S)
    t16 = tb // 16
    n16 = S // 16
    nblk, _, wblk = w_up.shape
    half = nblk // 2
    nsteps = S // tb

    def body(dgc_ref, dgn_ref, gc_ref, dval_ref, dh1p_ref, z1_ref, wu_ref, fcw_ref, wo_ref, g1_ref,
             dgate_ref, dz1_ref, dz1b_ref, datt_ref, drec_ref, accf_ref, accd_ref):
        i = pl.program_id(0)

        @pl.when(i == 0)
        def _():
            accf_ref[...] = jnp.zeros_like(accf_ref)
            accd_ref[...] = jnp.zeros_like(accd_ref)

        dg = dgc_ref[...].astype(F32)
        nxt = jnp.where(i < nsteps - 1, dgn_ref[...].astype(F32)[0:8], 0.0)
        w = _w_rows(fcw_ref)
        up1, up2 = _shift_up(dg, nxt, 1), _shift_up(dg, nxt, 2)
        dgate = (w[2] * dg + w[1] * up1 + w[0] * up2).astype(BF16)
        dgate_ref[...] = dgate
        gate = gc_ref[...].astype(F32)
        _put_rows(accf_ref, [_row_sum(up2 * gate), _row_sum(up1 * gate), _row_sum(dg * gate), _row_sum(dg)])

        dh1 = dh1p_ref[...]
        for j in range(nblk):
            src = dgate if j < half else dval_ref[...]
            jj = j % half
            dh1 = dh1 + _dot_nt(src[:, jj * wblk:(jj + 1) * wblk], wu_ref[j])
        xhat1, rstd1 = _ln_stats(z1_ref[...])
        dz1 = _ln_bwd(dh1, xhat1, rstd1, g1_ref[...])
        dz1_ref[...] = dz1
        dz1b = dz1.astype(BF16)
        dz1b_ref[...] = dz1b
        dcat = _dot_nt(dz1b, wo_ref[...])
        datt_ref[...] = dcat[:, :D_ATT].astype(BF16)
        drec_ref[...] = dcat[:, D_ATT:]
        _put_rows(accd_ref, [_row_sum(dh1 * xhat1), _row_sum(dh1)])

    next16 = pl.BlockSpec((16, D_FF), lambda i: (jnp.minimum((i + 1) * t16, n16 - 1), 0))
    return _pcall(
        body, (dgc, dgc, gate, dval, dh1p, z1, w_up, fcw, w_out, ln1_g), name="up_bwd",
        grid=(nsteps,), sem="arbitrary", comm=comm,
        in_specs=[_rows(tb, D_FF), next16, _rows(tb, D_FF), _rows(tb, D_FF), _rows(tb, D_MODEL),
                  _rows(tb, D_MODEL), _resident(w_up.shape), _resident((3, D_FF)),
                  _resident((D_MODEL, D_MODEL)), _resident((1, D_MODEL))],
        out_specs=[_rows(tb, D_FF), _rows(tb, D_MODEL), _rows(tb, D_MODEL), _rows(tb, D_ATT), _rows(tb, D_RNN),
                   _acc((8, D_FF)), _acc((8, D_MODEL))],
        out_shape=[jax.ShapeDtypeStruct((S, D_FF), BF16), jax.ShapeDtypeStruct((S, D_MODEL), F32),
                   jax.ShapeDtypeStruct((S, D_MODEL), BF16), jax.ShapeDtypeStruct((S, D_ATT), BF16),
                   jax.ShapeDtypeStruct((S, D_RNN), F32), jax.ShapeDtypeStruct((8, D_FF), F32),
                   jax.ShapeDtypeStruct((8, D_MODEL), F32)])


def _attn_bwd(q, k, v, lse, do, sinks, comm=None):
    S = q.shape[0]
    grp = N_HEADS // N_KV
    nq = min(ATT_STEP, S // QBLK)

    def body(sink_ref, q_ref, kc_ref, kp_ref, vc_ref, vp_ref, do_ref, lse_ref, dq_ref, dkc_ref, dkp_ref, dvc_ref, dvp_ref,
             ds_ref):
        i = pl.program_id(0)

        @pl.when(i == 0)
        def _():
            ds_ref[...] = jnp.zeros_like(ds_ref)

        row8 = lax.broadcasted_iota(jnp.int32, (8, 128), 0)
        lane8 = lax.broadcasted_iota(jnp.int32, (8, 128), 1)
        dsink = jnp.zeros((8, 128), F32)
        kall = jnp.concatenate([kp_ref[...], kc_ref[...]], axis=0)
        vall = jnp.concatenate([vp_ref[...], vc_ref[...]], axis=0)
        dk_t = [jnp.zeros((D_KV, QBLK), F32) for _ in range(nq + 1)]
        dv_t = [jnp.zeros((D_KV, QBLK), F32) for _ in range(nq + 1)]
        for b in range(nq):
            valid = _band_mask(i * nq + b)
            rows = slice(b * QBLK, (b + 1) * QBLK)
            keys = slice(b * QBLK, (b + 2) * QBLK)
            qv, dov = q_ref[rows, :], do_ref[rows, :]
            dqs, dks, dvs = [], [], []
            for g in range(N_KV):
                kcat = kall[keys, g * HEAD_DIM:(g + 1) * HEAD_DIM]
                vcat = vall[keys, g * HEAD_DIM:(g + 1) * HEAD_DIM]
                q4, do4 = _stack_heads(qv, g), _stack_heads(dov, g)
                s = jnp.where(valid, _dot_nt(q4, kcat), -1e30)
                lse = lse_ref[(b * N_KV + g) * GROUP * QBLK:(b * N_KV + g + 1) * GROUP * QBLK, :]
                p = jnp.exp(s - lse)
                p_sink = jnp.exp(_sink_column(sink_ref, g) - lse)
                dp = _dot_nt(do4, vcat)
                delta = jnp.sum(p * dp, axis=1, keepdims=True)
                dsc = (p * (dp - delta)).astype(BF16)
                dqs += _unstack_heads(_dot(dsc, kcat) * (HEAD_DIM ** -0.5))
                dks.append(_dot_tn(q4, dsc))
                dvs.append(_dot_tn(do4, p.astype(BF16)))
                for hh, part in enumerate(_unstack_heads(-p_sink * delta)):
                    here = (row8 == 0) & (lane8 == g * grp + hh)
                    dsink = dsink + jnp.where(here, jnp.sum(part, axis=0, keepdims=True), 0.0)
            dq_ref[rows, :] = jnp.concatenate(dqs, axis=1).astype(BF16)
            dk2, dv2 = jnp.concatenate(dks, axis=0), jnp.concatenate(dvs, axis=0)
            dk_t[b], dk_t[b + 1] = dk_t[b] + dk2[:, :QBLK], dk_t[b + 1] + dk2[:, QBLK:]
            dv_t[b], dv_t[b + 1] = dv_t[b] + dv2[:, :QBLK], dv_t[b + 1] + dv2[:, QBLK:]
        dkp_ref[...] = dk_t[0].T
        dvp_ref[...] = dv_t[0].T
        for b in range(nq):
            dkc_ref[b * QBLK:(b + 1) * QBLK, :] = dk_t[b + 1].T
            dvc_ref[b * QBLK:(b + 1) * QBLK, :] = dv_t[b + 1].T
        ds_ref[...] += dsink

    nsteps = S // (nq * QBLK)
    cur = jax.ShapeDtypeStruct((S, D_KV), F32)
    prev = jax.ShapeDtypeStruct((nsteps * QBLK, D_KV), F32)
    big = _rows(nq * QBLK, D_ATT)
    return _pcall(
        body, (sinks, q, k, k, v, v, do, lse), name="attn_bwd", grid=(nsteps,), sem="arbitrary", comm=comm,
        in_specs=[pl.BlockSpec(memory_space=pltpu.SMEM), big] + _attn_specs(nq) + [big, _rows(nq * N_HEADS * QBLK, 1)],
        out_specs=[big, _rows(nq * QBLK, D_KV), _rows(QBLK, D_KV), _rows(nq * QBLK, D_KV), _rows(QBLK, D_KV),
                   _acc((8, 128))],
        out_shape=[jax.ShapeDtypeStruct((S, D_ATT), BF16), cur, prev, cur, prev, jax.ShapeDtypeStruct((8, 128), F32)])


def _rnn_bwd(xr, gr, h, kept, drec, conv_w, wa, wx, lam, comm=None):
    S = xr.shape[0]
    tb = min(256, S)
    t8 = tb // 8
    nsteps = S // tb

    def body(xr_ref, xp_ref, gr_ref, h_ref, hp_ref, xc_ref, r_ref, ig_ref, a_ref, f_ref, drec_ref, cw_ref, wa_ref, wx_ref,
             lam_ref, dxr_ref, dgr_ref, gwa_ref, gwx_ref, acc_ref, carry_s, dxc_halo_s, d_s, gwa_s, gwx_s):
        i = pl.program_id(0)
        blk = nsteps - 1 - i

        @pl.when(i == 0)
        def _():
            gwa_s[...] = jnp.zeros_like(gwa_s)
            gwx_s[...] = jnp.zeros_like(gwx_s)
            acc_ref[...] = jnp.zeros_like(acc_ref)
            carry_s[...] = jnp.zeros_like(carry_s)
            dxc_halo_s[...] = jnp.zeros_like(dxc_halo_s)

        x = xr_ref[...]
        xhalo = jnp.where(blk > 0, xp_ref[...], 0.0)
        cw = _w_rows(cw_ref)
        xs = [_shift_down(x, xhalo, 3), _shift_down(x, xhalo, 2), _shift_down(x, xhalo, 1), x]
        xc, r, ig, a, f = xc_ref[...], r_ref[...], ig_ref[...], a_ref[...], f_ref[...]
        sp = _softplus_neg(lam_ref[...])
        hcur = h_ref[...]
        hprev = _shift_down(hcur, jnp.where(blk > 0, hp_ref[...], 0.0), 1)
        gl, dgl = _gelu_and_grad(gr_ref[...])
        drec = drec_ref[...]
        dgr_ref[...] = (drec * hcur * dgl).astype(BF16)
        d_s[...] = drec * gl
        row8 = lax.broadcasted_iota(jnp.int32, (8, D_RNN), 0)

        def tile(t, c):
            o = pl.multiple_of((t8 - 1 - t) * 8, 8)
            a8 = a_ref[pl.ds(o, 8), :]
            dt = d_s[pl.ds(o, 8), :]
            at = jnp.where(row8 == 7, 1.0, pltpu.roll(a8, 7, 0))
            for s in (1, 2, 4):
                keep = row8 < 8 - s
                a_sh = jnp.where(keep, pltpu.roll(at, 8 - s, 0), 1.0)
                d_sh = jnp.where(keep, pltpu.roll(dt, 8 - s, 0), 0.0)
                dt = at * d_sh + dt
                at = at * a_sh
            lt = at * c + dt
            d_s[pl.ds(o, 8), :] = lt
            return _row_sum(jnp.where(row8 == 0, a8 * lt, 0.0))

        carry_s[0:1, :] = lax.fori_loop(0, t8, tile, carry_s[0:1, :], unroll=2)
        lmb = d_s[...]
        a2 = a * a
        dla = lmb * hprev * a - lmb * ig * xc * (a2 / f)
        di = lmb * f * xc
        dr = dla * (-LRU_C) * sp
        dpa = dr * r * (1.0 - r)
        dpx = di * ig * (1.0 - ig)
        dpab = dpa.astype(BF16)
        dpxb = dpx.astype(BF16)
        xcb = xc.astype(BF16)
        gwa_s[...] += _dot_tn(xcb, dpab)
        gwx_s[...] += _dot_tn(xcb, dpxb)

        @pl.when(i == nsteps - 1)
        def _():
            for dense, out in ((gwa_s[...], gwa_ref), (gwx_s[...], gwx_ref)):
                for b in range(RNN_BLOCKS):
                    rows = slice(b * HEAD_DIM, (b + 1) * HEAD_DIM)
                    out[rows, :] = dense[rows, b * HEAD_DIM:(b + 1) * HEAD_DIM]

        dxc = lmb * f * ig + _dot_nt(dpab, wa_ref[...]) + _dot_nt(dpxb, wx_ref[...])
        nxt = dxc_halo_s[...]
        dxr = cw[3] * dxc
        for s in (1, 2, 3):
            dxr = dxr + cw[3 - s] * _shift_up(dxc, nxt, s)
        dxr_ref[...] = dxr.astype(BF16)
        dxc_halo_s[...] = dxc[:8]
        dlam = _row_sum(dla * (-LRU_C) * r) * (-1.0 / (1.0 + jnp.exp(lam_ref[...])))
        _put_rows(acc_ref, [_row_sum(dxc * xs[0]), _row_sum(dxc * xs[1]), _row_sum(dxc * xs[2]), _row_sum(dxc * xs[3]),
                            _row_sum(dxc), _row_sum(dpa), _row_sum(dpx), dlam])

    rev = lambda i: (nsteps - 1 - i, 0)
    prev8 = lambda i: (jnp.maximum((nsteps - 1 - i) * t8 - 1, 0), 0)
    blkspec = pl.BlockSpec((tb, D_RNN), rev)
    halo8 = pl.BlockSpec((8, D_RNN), prev8)
    vec = _resident((1, D_RNN))
    return _pcall(
        body, (xr, xr, gr, h, h, *kept, drec, conv_w, wa, wx, lam), name="rnn_bwd", grid=(nsteps,),
        sem="arbitrary", comm=comm,
        in_specs=[blkspec, halo8, blkspec, blkspec, halo8] + [blkspec] * 6
        + [_resident((4, D_RNN)), _resident((D_RNN, D_RNN)), _resident((D_RNN, D_RNN)), vec],
        out_specs=[blkspec, blkspec, _acc((D_RNN, HEAD_DIM)), _acc((D_RNN, HEAD_DIM)), _acc((8, D_RNN))],
        out_shape=[jax.ShapeDtypeStruct((S, D_RNN), BF16), jax.ShapeDtypeStruct((S, D_RNN), BF16),
                   jax.ShapeDtypeStruct((D_RNN, HEAD_DIM), F32), jax.ShapeDtypeStruct((D_RNN, HEAD_DIM), F32),
                   jax.ShapeDtypeStruct((8, D_RNN), F32)],
        scratch_shapes=[pltpu.VMEM((8, D_RNN), F32), pltpu.VMEM((8, D_RNN), F32), pltpu.VMEM((tb, D_RNN), F32),
                        pltpu.VMEM((D_RNN, D_RNN), F32), pltpu.VMEM((D_RNN, D_RNN), F32)])


def _in_bwd(dq, dkc, dkp, dvc, dvp, dxr, dgr, dz1, w_in, comm=None):
    S = dz1.shape[0]
    tb = min(ATT_STEP * QBLK, S)
    nsteps = S // tb

    def body(dq_ref, dkc_ref, dkn_ref, dvc_ref, dvn_ref, dxr_ref, dgr_ref, dz1_ref, w_ref, du_ref, dx_ref):
        last = pl.program_id(0) == nsteps - 1

        def total(cur_ref, next_ref):
            nxt = jnp.where(last, 0.0, next_ref[...])
            tail = cur_ref[tb - QBLK:, :] + nxt
            return jnp.concatenate([cur_ref[:tb - QBLK, :], tail], axis=0) if tb > QBLK else tail

        dk = total(dkc_ref, dkn_ref).astype(BF16)
        dv = total(dvc_ref, dvn_ref).astype(BF16)
        du = jnp.concatenate([dq_ref[...], dk, dv, dxr_ref[...], dgr_ref[...]], axis=1)
        du_ref[...] = du
        dx_ref[...] = ALPHA * dz1_ref[...] + _dot(du, w_ref[...])

    nextp = pl.BlockSpec((QBLK, D_KV), lambda i: (jnp.minimum(i + 1, nsteps - 1), 0))
    return _pcall(
        body, (dq, dkc, dkp, dvc, dvp, dxr, dgr, dz1, w_in), name="in_bwd", grid=(nsteps,), comm=comm,
        in_specs=[_rows(tb, D_ATT), _rows(tb, D_KV), nextp, _rows(tb, D_KV), nextp,
                  _rows(tb, D_RNN), _rows(tb, D_RNN), _rows(tb, D_MODEL), _resident((D_IN, D_MODEL))],
        out_specs=[_rows(tb, D_IN), _rows(tb, D_MODEL)],
        out_shape=[jax.ShapeDtypeStruct((S, D_IN), BF16), jax.ShapeDtypeStruct((S, D_MODEL), F32)])


def _block_diag(w):
    eye = jnp.eye(RNN_BLOCKS, dtype=w.dtype)
    return (w[:, :, None, :] * eye[:, None, :, None]).reshape(D_RNN, D_RNN).astype(BF16)


def _adamw(w, g, m, v):
    m = ADAM_B1 * m + (1.0 - ADAM_B1) * g
    v = ADAM_B2 * v + (1.0 - ADAM_B2) * (g * g)
    m_hat = m / (1.0 - ADAM_B1 ** ADAM_STEP)
    v_hat = v / (1.0 - ADAM_B2 ** ADAM_STEP)
    delta = -ADAM_LR * (m_hat / (jnp.sqrt(v_hat) + ADAM_EPS) + ADAM_WD * w)
    return delta, m, v


def _sum_adamw(parts, w, m, v, name):
    parts = parts if isinstance(parts, (list, tuple)) else [parts]
    R, C = w.shape
    rb = R if R <= 256 else 128
    per = parts[0].shape[1] // rb
    assert R % rb == 0 and parts[0].shape[1] % rb == 0
    n = len(parts)

    def body(*refs):
        p_refs = refs[:n]
        w_ref, m_ref, v_ref, g_out, d_out, m_out, v_out = refs[n:]
        which = pl.program_id(0) // per

        def total(p_ref):
            g = p_ref[0].astype(F32)
            for d in range(1, N_DEV):
                g = g + p_ref[d].astype(F32)
            return g

        g = total(p_refs[0])
        for j in range(1, n):
            g = jnp.where(which == j, total(p_refs[j]), g)
        delta, mn, vn = _adamw(w_ref[...], g, m_ref[...], v_ref[...])
        g_out[...] = g
        d_out[...] = delta
        m_out[...] = mn
        v_out[...] = vn

    def part_spec(j):
        return pl.BlockSpec((N_DEV, rb, C), lambda i: (0, jnp.clip(i - j * per, 0, per - 1), 0))

    blk = _rows(rb, C)
    out = jax.ShapeDtypeStruct((R, C), F32)
    return pl.pallas_call(
        body, name=name, grid=(R // rb,),
        in_specs=[part_spec(j) for j in range(n)] + [blk, blk, blk],
        out_specs=[blk, blk, blk, blk], out_shape=[out, out, out, out],
        compiler_params=_params("parallel"),
    )(*parts, w, m, v)


_SMALL = [("attn_sinks", "s", 0, 1, None), ("rnn_conv_w", "r", 0, 4, "cols"), ("rnn_conv_b", "r", 4, 1, None),
          ("gate_a_w", "a", 0, D_RNN, None), ("gate_a_b", "r", 5, 1, None), ("gate_x_w", "x", 0, D_RNN, None),
          ("gate_x_b", "r", 6, 1, None), ("lru_lambda", "r", 7, 1, None), ("ln1_g", "d", 0, 1, None),
          ("ln1_b", "d", 1, 1, None), ("ffn_conv_w", "f", 0, 3, "cols"), ("ffn_conv_b", "f", 3, 1, None),
          ("ple_gate_b", "t", 2, 1, None), ("ln2_g", "t", 0, 1, None), ("ln2_b", "t", 1, 1, None)]
_LOSS_ROW = 3


_ACC_COLS = {"t": (0, D_MODEL), "f": (D_MODEL, D_FF), "d": (D_MODEL + D_FF, D_MODEL), "s": (2 * D_MODEL + D_FF, 128),
             "r": (2 * D_MODEL + D_FF + 128, D_RNN)}
_ACC_WIDTH = 2 * D_MODEL + D_FF + 128 + D_RNN


def _small_update(rows_all, gates_all, params):
    flat = [arr for triple in params for arr in triple]
    n_par = len(_SMALL)

    def body(*refs):
        rows_ref, gates_ref = refs[:2]
        p_refs = refs[2:2 + 3 * n_par]
        loss_ref = refs[2 + 3 * n_par]
        o_refs = refs[3 + 3 * n_par:3 + 7 * n_par]
        rows_s, tmp_r, tmp_f = refs[3 + 7 * n_par:]
        me = _dev_index(*_place())
        rows_sum, gates_sum = rows_ref[0], gates_ref[0]
        for d in range(1, N_DEV):
            rows_sum = rows_sum + rows_ref[d]
            gates_sum = gates_sum + gates_ref[d]
        rows_s[...] = rows_sum
        t0 = _ACC_COLS["t"][0]
        loss_ref[...] = rows_s[_LOSS_ROW:_LOSS_ROW + 1, t0:t0 + 128]
        for i, (name, key, row, rows, how) in enumerate(_SMALL):
            w_ref, m_ref, v_ref = p_refs[3 * i:3 * i + 3]
            g_out, d_out, m_out, v_out = o_refs[4 * i:4 * i + 4]
            if key == "a":
                g = gates_sum[:, :HEAD_DIM]
            elif key == "x":
                g = gates_sum[:, HEAD_DIM:]
            elif how == "cols":
                c0, width = _ACC_COLS[key]
                full = rows_s[:, c0:c0 + width]
                shard = width // N_DEV
                mine = full[:, :shard]
                for d in range(1, N_DEV):
                    mine = jnp.where(me == d, full[:, d * shard:(d + 1) * shard], mine)
                tmp = tmp_r if key == "r" else tmp_f
                tmp[...] = mine
                g = tmp[row:row + rows, :]
            else:
                c0, width = _ACC_COLS[key]
                g = rows_s[row:row + rows, c0:c0 + width][:, :w_ref.shape[1]]
            delta, mn, vn = _adamw(w_ref[...], g, m_ref[...], v_ref[...])
            g_out[...] = g
            d_out[...] = delta
            m_out[...] = mn
            v_out[...] = vn

    outs = [jax.ShapeDtypeStruct((1, 128), F32)]
    for w, _, _ in params:
        outs += [jax.ShapeDtypeStruct(w.shape, F32)] * 4
    scratch = [pltpu.VMEM((8, _ACC_WIDTH), F32), pltpu.VMEM((8, D_RNN // N_DEV), F32), pltpu.VMEM((8, D_FF // N_DEV), F32)]
    res = pl.pallas_call(body, name="small_update", out_shape=outs, scratch_shapes=scratch)(rows_all, gates_all, *flat)
    return res[0], [res[1 + 4 * i:5 + 4 * i] for i in range(n_par)]


def kernel(x, p, w_in, attn_sinks, rnn_conv_w, rnn_conv_b, gate_a_w, gate_a_b, gate_x_w, gate_x_b, lru_lambda, w_out, ln1_g, ln1_b, w_ffn_up, ffn_conv_w, ffn_conv_b, w_ffn_down, ple_gate_w, ple_gate_b, ple_proj, ln2_g, ln2_b, loss_target, m_w_in, m_attn_sinks, m_rnn_conv_w, m_rnn_conv_b, m_gate_a_w, m_gate_a_b, m_gate_x_w, m_gate_x_b, m_lru_lambda, m_w_out, m_ln1_g, m_ln1_b, m_w_ffn_up, m_ffn_conv_w, m_ffn_conv_b, m_w_ffn_down, m_ple_gate_w, m_ple_gate_b, m_ple_proj, m_ln2_g, m_ln2_b, v_w_in, v_attn_sinks, v_rnn_conv_w, v_rnn_conv_b, v_gate_a_w, v_gate_a_b, v_gate_x_w, v_gate_x_b, v_lru_lambda, v_w_out, v_ln1_g, v_ln1_b, v_w_ffn_up, v_ffn_conv_w, v_ffn_conv_b, v_w_ffn_down, v_ple_gate_w, v_ple_gate_b, v_ple_proj, v_ln2_g, v_ln2_b):
    from_col_blocks = lambda g: g.transpose(1, 0, 2).reshape(g.shape[1], N_DEV * g.shape[2])

    xs, ps, tgt, sinks = x[0], p[0, 0], loss_target[0], attn_sinks[0]
    wa, wx = _block_diag(gate_a_w[0]), _block_diag(gate_x_w[0])

    conv_cols = jnp.concatenate([rnn_conv_w[0].reshape(1, -1), ffn_conv_w[0].reshape(1, -1)], axis=1)
    n_rc, n_fc = 4 * D_RNN // N_DEV, 3 * D_FF // N_DEV
    ((g_in,),) = _comm_call([_Gather([w_in[0].T.astype(BF16)])], "gather_w_in")
    w_in_full = g_in.reshape(D_IN, D_MODEL)

    (q, k, v, xr, gr), (g_conv,) = _in_proj(xs, w_in_full, comm=_Bcast([jnp.broadcast_to(conv_cols, (8, n_rc + n_fc))]))
    rcw = from_col_blocks(g_conv[:, 0, :n_rc].reshape(N_DEV, 4, D_RNN // N_DEV))
    fcw = from_col_blocks(g_conv[:, 0, n_rc:].reshape(N_DEV, 3, D_FF // N_DEV))
    (att, lse), (g_out,) = _attn_fwd(q, k, v, sinks, comm=_Gather([w_out[0].astype(BF16)]))
    (rec, h, *kept), (w_up,) = _rnn_fwd(xr, gr, rcw, rnn_conv_b, wa, wx, gate_a_b, gate_x_b, lru_lambda,
                                        comm=_Gather([w_ffn_up[0].astype(BF16)], forward_at=0.97))
    w_out_full = g_out.reshape(D_MODEL, D_MODEL)
    (z1, h1, h1b, gate, act, gl, vdgl), (g_down, g_pg, g_pp) = _mix_ln1_up(
        xs, att, rec, w_out_full, ln1_g, ln1_b, w_up, fcw, ffn_conv_b,
        comm=_Gather([w_ffn_down[0].astype(BF16), ple_gate_w[0].astype(BF16), ple_proj[0].astype(BF16)]))
    dz2b, dpreb, dppb, dgc, dval, dh1p, acc_t = _tail(
        act, gl, vdgl, h1, h1b, ps, tgt, g_down.reshape(D_FF, D_MODEL), g_pg.reshape(D_MODEL, D_MODEL), ple_gate_b,
        from_col_blocks(g_pp), ln2_g, ln2_b)

    gd_down = _weight_grad([dz2b], [act], "down_grad", "rows_t", ts=1024)
    gd_pg = _weight_grad([h1b], [dpreb], "pg_grad", "rows", ts=1024)
    gd_pp = _weight_grad([ps], [dppb], "pp_grad", "cols", ts=1024)
    (dgate, dz1, dz1b, datt, drec, acc_f, acc_d), (r_down, r_pg, r_pp) = _up_bwd(
        dgc, gate, dval, dh1p, z1, w_up, fcw, w_out_full, ln1_g, comm=_Exchange([gd_down, gd_pg, gd_pp]))
    gd_up_top, gd_up_bot = _weight_grad([h1b], [dgate, dval], "up_grad", "cols", halves=True)
    gd_out = _weight_grad([att, rec], [dz1b], "out_grad", "rows", ts=1024)
    (dq, dkc, dkp, dvc, dvp, acc_s), (r_up_top,) = _attn_bwd(q, k, v, lse, datt, sinks, comm=_Exchange([gd_up_top]))
    early = jnp.concatenate([acc_t, acc_f, acc_d], axis=1)
    (dxr, dgr, g_wa, g_wx, acc_r), (r_up_bot, r_out, early_all) = _rnn_bwd(
        xr, gr, h, kept, drec, rcw, wa, wx, lru_lambda, comm=_Multi([_Exchange([gd_up_bot, gd_out]), _Bcast([early])]))
    (du, dx), _ = _in_bwd(dq, dkc, dkp, dvc, dvp, dxr, dgr, dz1, w_in_full)
    lanes = D_RNN // 128
    late = jnp.concatenate([g_wa, g_wx], axis=1)
    late = jnp.concatenate([late, acc_s, acc_r.reshape(8, lanes, 128).transpose(1, 0, 2).reshape(8 * lanes, 128)], axis=0)
    width = D_MODEL // IN_GRAD_PARTS
    comm, r_parts = _Bcast([late]), []
    for part in range(IN_GRAD_PARTS):
        gd_part, got = _weight_grad([du], [xs], f"in_grad_{part}", "rows", ts=1024, b_window=(part, width), comm=comm)
        if part == 0:
            (late_all,) = got
        else:
            r_parts += got
        comm = _Exchange([gd_part])
    r_parts += _comm_call([comm], "exchange_w_in")[0]
    r_in = jnp.concatenate(r_parts, axis=2)
    acc_r_all = late_all[:, D_RNN + 8:].reshape(N_DEV, lanes, 8, 128).transpose(0, 2, 1, 3).reshape(N_DEV, 8, D_RNN)
    small_parts = (jnp.concatenate([early_all, late_all[:, D_RNN:D_RNN + 8], acc_r_all], axis=2),
                   late_all[:, :D_RNN])

    outs = {}
    res = _sum_adamw(r_in, w_in[0].T, m_w_in[0].T, v_w_in[0].T, "adamw_w_in")
    outs["w_in"] = [r.T[None] for r in res]
    for name, parts, w, m, v in [("w_out", r_out, w_out, m_w_out, v_w_out),
                                 ("w_ffn_up", [r_up_top, r_up_bot], w_ffn_up, m_w_ffn_up, v_w_ffn_up),
                                 ("w_ffn_down", r_down, w_ffn_down, m_w_ffn_down, v_w_ffn_down),
                                 ("ple_gate_w", r_pg, ple_gate_w, m_ple_gate_w, v_ple_gate_w),
                                 ("ple_proj", r_pp, ple_proj, m_ple_proj, v_ple_proj)]:
        res = _sum_adamw(parts, w[0], m[0], v[0], "adamw_" + name)
        outs[name] = [r[None] for r in res]

    given = dict(attn_sinks=(attn_sinks, m_attn_sinks, v_attn_sinks), rnn_conv_w=(rnn_conv_w, m_rnn_conv_w, v_rnn_conv_w),
                 rnn_conv_b=(rnn_conv_b, m_rnn_conv_b, v_rnn_conv_b), gate_a_w=(gate_a_w, m_gate_a_w, v_gate_a_w),
                 gate_a_b=(gate_a_b, m_gate_a_b, v_gate_a_b), gate_x_w=(gate_x_w, m_gate_x_w, v_gate_x_w),
                 gate_x_b=(gate_x_b, m_gate_x_b, v_gate_x_b), lru_lambda=(lru_lambda, m_lru_lambda, v_lru_lambda),
                 ln1_g=(ln1_g, m_ln1_g, v_ln1_g), ln1_b=(ln1_b, m_ln1_b, v_ln1_b),
                 ffn_conv_w=(ffn_conv_w, m_ffn_conv_w, v_ffn_conv_w), ffn_conv_b=(ffn_conv_b, m_ffn_conv_b, v_ffn_conv_b),
                 ple_gate_b=(ple_gate_b, m_ple_gate_b, v_ple_gate_b), ln2_g=(ln2_g, m_ln2_g, v_ln2_g),
                 ln2_b=(ln2_b, m_ln2_b, v_ln2_b))
    as_2d = lambda a: a.reshape(-1, a.shape[-1])
    loss_row, small_res = _small_update(*small_parts, [tuple(as_2d(a) for a in given[n]) for n, *_ in _SMALL])
    loss = loss_row[0, 0]
    for (n, *_), res in zip(_SMALL, small_res):
        outs[n] = [r.reshape(given[n][0].shape) for r in res]

    order = ["w_in", "attn_sinks", "rnn_conv_w", "rnn_conv_b", "gate_a_w", "gate_a_b", "gate_x_w", "gate_x_b",
             "lru_lambda", "w_out", "ln1_g", "ln1_b", "w_ffn_up", "ffn_conv_w", "ffn_conv_b", "w_ffn_down",
             "ple_gate_w", "ple_gate_b", "ple_proj", "ln2_g", "ln2_b"]
    return (loss, dx[None], *[outs[n][0] for n in order], *[outs[n][1] for n in order],
            *[outs[n][2] for n in order], *[outs[n][3] for n in order])
```

```python
import jax
import jax.numpy as jnp
from jax import lax
from jax.experimental import pallas as pl
from jax.experimental.pallas import tpu as pltpu

F32 = jnp.float32
BF16 = jnp.bfloat16

D_MODEL = 1024
D_ATT = 512
D_KV = 128
HEAD_DIM = 64
N_HEADS = 8
N_KV = 2
D_RNN = 512
RNN_BLOCKS = 8
D_IN = 1792
D_FF = 3072
PLE_DIM = 256
QBLK = 128
N_DEV = 8
ALPHA = float(2 ** 0.25)
LN_EPS = 1e-5
LRU_C = 8.0
ADAM_LR, ADAM_B1, ADAM_B2, ADAM_EPS, ADAM_WD, ADAM_STEP = 0.001, 0.9, 0.999, 1e-08, 0.01, 10

V7X_VMEM_LIMIT = 56 * 1024 * 1024
MESH = pl.DeviceIdType.MESH


def _params(*sem, vmem=V7X_VMEM_LIMIT):
    return pltpu.CompilerParams(dimension_semantics=sem or None, vmem_limit_bytes=vmem)


def _resident(shape):
    return pl.BlockSpec(shape, lambda *_: (0,) * len(shape), pipeline_mode=pl.Buffered(1))


def _rows(tb, cols):
    return pl.BlockSpec((tb, cols), lambda i: (i, 0))


def _acc(shape):
    return pl.BlockSpec(shape, lambda *_: (0,) * len(shape))


def _dot(a, b):
    return jnp.dot(a, b, preferred_element_type=F32)


def _dot_nt(a, b):
    return lax.dot_general(a, b, (((1,), (1,)), ((), ())), preferred_element_type=F32)


def _dot_tn(a, b):
    return lax.dot_general(a, b, (((0,), (0,)), ((), ())), preferred_element_type=F32)


def _sigmoid(x):
    return 1.0 / (1.0 + jnp.exp(-x))


_GELU_C = 0.7978845608028654
_GELU_K = 0.044715


def _gelu_and_grad(x):
    u = x * x
    t = jnp.tanh(x * (_GELU_C + (_GELU_C * _GELU_K) * u))
    hp = 0.5 + 0.5 * t
    dg = hp + x * (0.5 - 0.5 * (t * t)) * (_GELU_C + (3.0 * _GELU_C * _GELU_K) * u)
    return x * hp, dg


def _gelu(x):
    return 0.5 * x * (1.0 + jnp.tanh(_GELU_C * (x + _GELU_K * x * x * x)))


def _ln_stats(z):
    mu = jnp.mean(z, axis=-1, keepdims=True)
    zc = z - mu
    var = jnp.mean(zc * zc, axis=-1, keepdims=True)
    rstd = lax.rsqrt(var + LN_EPS)
    return zc * rstd, rstd


def _ln_bwd(dy, xhat, rstd, g):
    dxh = dy * g
    m1 = jnp.mean(dxh, axis=-1, keepdims=True)
    m2 = jnp.mean(dxh * xhat, axis=-1, keepdims=True)
    return rstd * (dxh - m1 - xhat * m2)


def _softplus_neg(lam):
    u = jnp.exp(-jnp.abs(lam))
    w = 1.0 + u
    d = w - 1.0
    log1p_u = jnp.where(d == 0.0, u, jnp.log(w) * (u / jnp.where(d == 0.0, 1.0, d)))
    return jnp.maximum(-lam, 0.0) + log1p_u


def _shift_down(x, halo, s):
    xs = pltpu.roll(x, s, 0)
    hs = pltpu.roll(halo, s, 0)
    row8 = lax.broadcasted_iota(jnp.int32, hs.shape, 0)
    first = jnp.where(row8 < s, hs, xs[:8])
    return jnp.concatenate([first, xs[8:]], axis=0)


def _shift_up(x, halo, s):
    n = x.shape[0]
    xs = pltpu.roll(x, n - s, 0)
    hs = pltpu.roll(halo, 8 - s, 0)
    row8 = lax.broadcasted_iota(jnp.int32, hs.shape, 0)
    last = jnp.where(row8 >= 8 - s, hs, xs[n - 8:])
    return jnp.concatenate([xs[:n - 8], last], axis=0)


def _row_sum(x):
    return jnp.sum(x, axis=0, keepdims=True)


def _put_rows(acc_ref, rows):
    row8 = lax.broadcasted_iota(jnp.int32, acc_ref.shape, 0)
    upd = jnp.zeros(acc_ref.shape, F32)
    for r, vec in enumerate(rows):
        upd = jnp.where(row8 == r, vec, upd)
    acc_ref[...] += upd


def _place():
    return lax.axis_index("x"), lax.axis_index("y"), lax.axis_index("c")


def _dev_index(px, py, pc):
    return 4 * px + 2 * py + pc


_ANY = pl.BlockSpec(memory_space=pl.ANY)


class _Gather:
    def __init__(self, arrays):
        self.arrays = list(arrays)
        self.n = len(self.arrays)

    def out_shape(self):
        return [jax.ShapeDtypeStruct((N_DEV,) + s.shape, s.dtype) for s in self.arrays]

    def scratch(self):
        return [pltpu.SemaphoreType.DMA((self.n, 7)), pltpu.SemaphoreType.DMA((self.n, 7)),
                pltpu.SemaphoreType.DMA((self.n,))]

    def _parts(self, ins, outs, sems):
        send_sems, recv_sems, local_sems = sems
        x, y, c = _place()
        me, sibling = (x, y, c), (x, y, 1 - c)
        chips = [(1 - x, y), (x, 1 - y), (1 - x, 1 - y)]

        def copy(a, k, block, to, src=None):
            rows = outs[a].at[_dev_index(*block)]
            return pltpu.make_async_remote_copy(
                src_ref=rows if src is None else src, dst_ref=rows, send_sem=send_sems.at[a, k],
                recv_sem=recv_sems.at[a, k], device_id=to, device_id_type=MESH)

        rng = range(self.n)
        mine = [pltpu.make_async_copy(ins[a], outs[a].at[_dev_index(*me)], local_sems.at[a]) for a in rng]
        first = [copy(a, 0, me, sibling, src=ins[a]) for a in rng]
        first += [copy(a, 1 + j, me, (*chip, c), src=ins[a]) for j, chip in enumerate(chips) for a in rng]
        landed = [copy(a, 1 + j, (*chip, c), me) for j, chip in enumerate(chips) for a in rng]
        passed = [copy(a, 4 + j, (*chip, c), sibling) for j, chip in enumerate(chips) for a in rng]
        from_sibling = [copy(a, 0, sibling, me) for a in rng]
        from_sibling += [copy(a, 4 + j, (*chip, 1 - c), me) for j, chip in enumerate(chips) for a in rng]
        return mine, first, landed, passed, from_sibling

    def start(self, ins, outs, sems):
        mine, first, _, _, _ = self._parts(ins, outs, sems)
        for cp in mine + first:
            cp.start()

    def forward(self, ins, outs, sems):
        _, _, landed, passed, _ = self._parts(ins, outs, sems)
        for got, fwd in zip(landed, passed):
            got.wait_recv()
            fwd.start()

    def finish(self, ins, outs, sems):
        mine, first, _, passed, from_sibling = self._parts(ins, outs, sems)
        for cp in from_sibling:
            cp.wait_recv()
        for cp in first + passed:
            cp.wait_send()
        for cp in mine:
            cp.wait()

    def before(self, ins, outs, sems, step, nsteps):
        pl.when(step == 0)(lambda: self.start(ins, outs, sems))
        pl.when(step == (7 * nsteps) // 8)(lambda: self.forward(ins, outs, sems))

    def after(self, ins, outs, sems, step, nsteps):
        pl.when(step == nsteps - 1)(lambda: self.finish(ins, outs, sems))


class _Exchange:
    def __init__(self, arrays):
        self.arrays = list(arrays)
        self.n = len(self.arrays)

    def out_shape(self):
        return [jax.ShapeDtypeStruct(b.shape, b.dtype) for b in self.arrays]

    def scratch(self):
        return [pltpu.SemaphoreType.DMA((self.n, 7)), pltpu.SemaphoreType.DMA((self.n, 7)),
                pltpu.SemaphoreType.DMA((self.n,))]

    def _parts(self, ins, outs, sems):
        send_sems, recv_sems, local_sems = sems
        x, y, c = _place()
        me = _dev_index(x, y, c)
        peers = [(x ^ (k >> 2), y ^ ((k >> 1) & 1), c ^ (k & 1)) for k in range(1, N_DEV)]
        rng = range(self.n)
        mine = [pltpu.make_async_copy(ins[a].at[me], outs[a].at[me], local_sems.at[a]) for a in rng]
        sent = [pltpu.make_async_remote_copy(
            src_ref=ins[a].at[_dev_index(*to)], dst_ref=outs[a].at[me], send_sem=send_sems.at[a, k],
            recv_sem=recv_sems.at[a, k], device_id=to, device_id_type=MESH) for k, to in enumerate(peers) for a in rng]
        arrivals = [pltpu.make_async_remote_copy(
            src_ref=ins[a].at[me], dst_ref=outs[a].at[_dev_index(*frm)], send_sem=send_sems.at[a, k],
            recv_sem=recv_sems.at[a, k], device_id=frm, device_id_type=MESH) for k, frm in enumerate(peers) for a in rng]
        return mine, sent, arrivals

    def start(self, ins, outs, sems):
        mine, sent, _ = self._parts(ins, outs, sems)
        for cp in mine + sent:
            cp.start()

    def finish(self, ins, outs, sems):
        mine, sent, arrivals = self._parts(ins, outs, sems)
        for cp in arrivals:
            cp.wait_recv()
        for cp in sent:
            cp.wait_send()
        for cp in mine:
            cp.wait()

    def before(self, ins, outs, sems, step, nsteps):
        pl.when(step == 0)(lambda: self.start(ins, outs, sems))

    def after(self, ins, outs, sems, step, nsteps):
        pl.when(step == nsteps - 1)(lambda: self.finish(ins, outs, sems))


class _Bcast(_Exchange):
    def out_shape(self):
        return [jax.ShapeDtypeStruct((N_DEV,) + s.shape, s.dtype) for s in self.arrays]

    def _parts(self, ins, outs, sems):
        send_sems, recv_sems, local_sems = sems
        x, y, c = _place()
        me = _dev_index(x, y, c)
        peers = [(x ^ (k >> 2), y ^ ((k >> 1) & 1), c ^ (k & 1)) for k in range(1, N_DEV)]
        rng = range(self.n)
        mine = [pltpu.make_async_copy(ins[a], outs[a].at[me], local_sems.at[a]) for a in rng]
        sent = [pltpu.make_async_remote_copy(
            src_ref=ins[a], dst_ref=outs[a].at[me], send_sem=send_sems.at[a, k], recv_sem=recv_sems.at[a, k],
            device_id=to, device_id_type=MESH) for k, to in enumerate(peers) for a in rng]
        arrivals = [pltpu.make_async_remote_copy(
            src_ref=ins[a], dst_ref=outs[a].at[_dev_index(*frm)], send_sem=send_sems.at[a, k],
            recv_sem=recv_sems.at[a, k], device_id=frm, device_id_type=MESH) for k, frm in enumerate(peers) for a in rng]
        return mine, sent, arrivals


class _Multi:
    def __init__(self, comms):
        self.comms = list(comms)
        self.arrays = [arr for c in self.comms for arr in c.arrays]
        self.n = len(self.arrays)

    def out_shape(self):
        return [s for c in self.comms for s in c.out_shape()]

    def scratch(self):
        return [s for c in self.comms for s in c.scratch()]

    def _each(self, ins, outs, sems):
        a = 0
        for j, c in enumerate(self.comms):
            yield c, ins[a:a + c.n], outs[a:a + c.n], sems[3 * j:3 * j + 3]
            a += c.n

    def before(self, ins, outs, sems, step, nsteps):
        for c, ci, co, cs in self._each(ins, outs, sems):
            c.before(ci, co, cs, step, nsteps)

    def after(self, ins, outs, sems, step, nsteps):
        for c, ci, co, cs in self._each(ins, outs, sems):
            c.after(ci, co, cs, step, nsteps)


def _comm_call(comms, name):
    ns = [c.n for c in comms]
    n = sum(ns)

    def body(*refs):
        parts, a, s = [], 0, 2 * n
        for c in comms:
            parts.append((c, refs[a:a + c.n], refs[n + a:n + a + c.n], refs[s:s + 3]))
            a, s = a + c.n, s + 3
        for c, ins, outs, sems in parts:
            c.start(ins, outs, sems)
        for c, ins, outs, sems in parts:
            if isinstance(c, _Gather):
                c.forward(ins, outs, sems)
        for c, ins, outs, sems in parts:
            c.finish(ins, outs, sems)

    res = pl.pallas_call(
        body, name=name, in_specs=[_ANY] * n, out_specs=[_ANY] * n,
        out_shape=[s for c in comms for s in c.out_shape()], scratch_shapes=[s for c in comms for s in c.scratch()],
    )(*[arr for c in comms for arr in c.arrays])
    out, a = [], 0
    for k in ns:
        out.append(res[a:a + k])
        a += k
    return out


def _pcall(body, args, *, name, grid, in_specs, out_specs, out_shape, scratch_shapes=(), sem="parallel", comm=None,
           step_axis=0):
    sem = (sem,) * len(grid) if isinstance(sem, str) else sem
    if comm is None:
        res = pl.pallas_call(body, name=name, grid=grid, in_specs=in_specs, out_specs=out_specs, out_shape=out_shape,
                             scratch_shapes=list(scratch_shapes), compiler_params=_params(*sem))(*args)
        return res, []
    n_in, n_out, n_scr, n = len(in_specs), len(out_specs), len(scratch_shapes), comm.n
    nsteps = grid[step_axis]
    assert all(g == 1 for ax, g in enumerate(grid) if ax != step_axis)

    def hosted(*refs):
        ins, cin = refs[:n_in], refs[n_in:n_in + n]
        o0 = n_in + n
        outs, cout = refs[o0:o0 + n_out], refs[o0 + n_out:o0 + n_out + n]
        s0 = o0 + n_out + n
        scr, sems = refs[s0:s0 + n_scr], refs[s0 + n_scr:]
        step = pl.program_id(step_axis)
        comm.before(cin, cout, sems, step, nsteps)
        body(*ins, *outs, *scr)
        comm.after(cin, cout, sems, step, nsteps)

    res = pl.pallas_call(
        hosted, name=name, grid=grid, in_specs=list(in_specs) + [_ANY] * n, out_specs=list(out_specs) + [_ANY] * n,
        out_shape=list(out_shape) + comm.out_shape(), scratch_shapes=list(scratch_shapes) + comm.scratch(),
        compiler_params=_params(*(("arbitrary",) * len(grid))))(*args, *comm.arrays)
    return res[:n_out], res[n_out:]


def _in_proj(x, w_in_t, comm=None):
    S = x.shape[0]
    tb = min(512, S)

    def body(x_ref, w_ref, q_ref, k_ref, v_ref, xr_ref, gr_ref):
        u = _dot_nt(x_ref[...].astype(BF16), w_ref[...])
        q_ref[...] = (u[:, :D_ATT] * (HEAD_DIM ** -0.5)).astype(BF16)
        k_ref[...] = u[:, D_ATT:D_ATT + D_KV].astype(BF16)
        v_ref[...] = u[:, D_ATT + D_KV:D_ATT + 2 * D_KV].astype(BF16)
        xr_ref[...] = u[:, D_ATT + 2 * D_KV:D_ATT + 2 * D_KV + D_RNN]
        gr_ref[...] = u[:, D_ATT + 2 * D_KV + D_RNN:]

    return _pcall(
        body, (x, w_in_t), name="in_proj", grid=(S // tb,), comm=comm,
        in_specs=[_rows(tb, D_MODEL), _resident((D_IN, D_MODEL))],
        out_specs=[_rows(tb, D_ATT), _rows(tb, D_KV), _rows(tb, D_KV), _rows(tb, D_RNN), _rows(tb, D_RNN)],
        out_shape=[jax.ShapeDtypeStruct((S, D_ATT), BF16), jax.ShapeDtypeStruct((S, D_KV), BF16),
                   jax.ShapeDtypeStruct((S, D_KV), BF16), jax.ShapeDtypeStruct((S, D_RNN), F32),
                   jax.ShapeDtypeStruct((S, D_RNN), F32)])


GROUP = N_HEADS // N_KV


def _band_mask(i):
    qi = lax.broadcasted_iota(jnp.int32, (GROUP * QBLK, 2 * QBLK), 0) & (QBLK - 1)
    sj = lax.broadcasted_iota(jnp.int32, (GROUP * QBLK, 2 * QBLK), 1)
    return (sj > qi) & (sj <= qi + QBLK) & ((sj >= QBLK) | (i > 0))


def _stack_heads(x, g):
    return jnp.concatenate([x[:, (g * GROUP + hh) * HEAD_DIM:(g * GROUP + hh + 1) * HEAD_DIM] for hh in range(GROUP)],
                           axis=0)


def _unstack_heads(x4):
    return [x4[hh * QBLK:(hh + 1) * QBLK] for hh in range(GROUP)]


def _sink_column(sink_ref, g):
    head = lax.broadcasted_iota(jnp.int32, (GROUP * QBLK, 1), 0) // QBLK
    col = jnp.full((GROUP * QBLK, 1), sink_ref[g * GROUP], F32)
    for hh in range(1, GROUP):
        col = jnp.where(head == hh, sink_ref[g * GROUP + hh], col)
    return col


ATT_STEP = 4
IN_GRAD_PARTS = 2


def _attn_specs(nq=1):
    cur = lambda i: (i, 0)
    prev = lambda i: (jnp.maximum(nq * i - 1, 0), 0)
    return [pl.BlockSpec((nq * QBLK, D_KV), cur), pl.BlockSpec((QBLK, D_KV), prev),
            pl.BlockSpec((nq * QBLK, D_KV), cur), pl.BlockSpec((QBLK, D_KV), prev)]


def _attn_fwd(q, k, v, sinks, comm=None):
    S = q.shape[0]
    nq = min(ATT_STEP, S // QBLK)

    def body(sink_ref, q_ref, kc_ref, kp_ref, vc_ref, vp_ref, o_ref, lse_ref):
        first = pl.program_id(0) * nq
        kall = jnp.concatenate([kp_ref[...], kc_ref[...]], axis=0)
        vall = jnp.concatenate([vp_ref[...], vc_ref[...]], axis=0)
        for b in range(nq):
            valid = _band_mask(first + b)
            rows = slice(b * QBLK, (b + 1) * QBLK)
            keys = slice(b * QBLK, (b + 2) * QBLK)
            qv = q_ref[rows, :]
            outs = []
            for g in range(N_KV):
                kcat = kall[keys, g * HEAD_DIM:(g + 1) * HEAD_DIM]
                vcat = vall[keys, g * HEAD_DIM:(g + 1) * HEAD_DIM]
                s = jnp.where(valid, _dot_nt(_stack_heads(qv, g), kcat), -1e30)
                sink = _sink_column(sink_ref, g)
                m = jnp.maximum(jnp.max(s, axis=1, keepdims=True), sink)
                p = jnp.exp(s - m)
                l = jnp.sum(p, axis=1, keepdims=True) + jnp.exp(sink - m)
                outs += _unstack_heads(_dot(p.astype(BF16), vcat) / l)
                lse_ref[(b * N_KV + g) * GROUP * QBLK:(b * N_KV + g + 1) * GROUP * QBLK, :] = m + jnp.log(l)
            o_ref[rows, :] = jnp.concatenate(outs, axis=1).astype(BF16)

    lse_rows = nq * N_HEADS * QBLK
    return _pcall(
        body, (sinks, q, k, k, v, v), name="attn_fwd", grid=(S // (nq * QBLK),), comm=comm,
        in_specs=[pl.BlockSpec(memory_space=pltpu.SMEM), _rows(nq * QBLK, D_ATT)] + _attn_specs(nq),
        out_specs=[_rows(nq * QBLK, D_ATT), _rows(lse_rows, 1)],
        out_shape=[jax.ShapeDtypeStruct((S, D_ATT), BF16), jax.ShapeDtypeStruct((S * N_HEADS, 1), F32)])


def _w_rows(w_ref):
    return [w_ref[k:k + 1, :] for k in range(w_ref.shape[0])]


def _conv4(x, halo, w, b):
    y = b + w[3] * x
    for s in (1, 2, 3):
        y = y + w[3 - s] * _shift_down(x, halo, s)
    return y


def _rnn_gates(xc, wa, wx, ba, bx, sp):
    xcb = xc.astype(BF16)
    r = _sigmoid(_dot(xcb, wa) + ba)
    ig = _sigmoid(_dot(xcb, wx) + bx)
    la = -LRU_C * r * sp
    a = jnp.exp(la)
    t = jnp.tanh(la)
    f = jnp.sqrt(-2.0 * t / (1.0 - t))
    return r, ig, a, f


def _rnn_fwd(xr, gr, conv_w, conv_b, wa, wx, ba, bx, lam, comm=None):
    S = xr.shape[0]
    tb = min(256, S)

    def body(xr_ref, gr_ref, cw_ref, cb_ref, wa_ref, wx_ref, ba_ref, bx_ref, lam_ref, rec_ref, h_ref,
             xc_ref, r_ref, ig_ref, a_ref, f_ref, halo_s, hc_s, a_s, b_s):
        @pl.when(pl.program_id(0) == 0)
        def _():
            halo_s[...] = jnp.zeros_like(halo_s)
            hc_s[...] = jnp.zeros_like(hc_s)

        x = xr_ref[...]
        xc = _conv4(x, halo_s[...], _w_rows(cw_ref), cb_ref[...])
        halo_s[...] = x[tb - 8:]
        r, ig, a, f = _rnn_gates(xc, wa_ref[...], wx_ref[...], ba_ref[...], bx_ref[...], _softplus_neg(lam_ref[...]))
        xc_ref[...] = xc
        r_ref[...] = r
        ig_ref[...] = ig
        a_ref[...] = a
        f_ref[...] = f
        a_s[...] = a
        b_s[...] = f * ig * xc
        row8 = lax.broadcasted_iota(jnp.int32, (8, D_RNN), 0)

        def tile(t, hc):
            o = pl.multiple_of(t * 8, 8)
            at = a_s[pl.ds(o, 8), :]
            bt = b_s[pl.ds(o, 8), :]
            for s in (1, 2, 4):
                keep = row8 >= s
                a_sh = jnp.where(keep, pltpu.roll(at, s, 0), 1.0)
                b_sh = jnp.where(keep, pltpu.roll(bt, s, 0), 0.0)
                bt = at * b_sh + bt
                at = at * a_sh
            ht = at * hc + bt
            b_s[pl.ds(o, 8), :] = ht
            return _row_sum(jnp.where(row8 == 7, ht, 0.0))

        hc_s[0:1, :] = lax.fori_loop(0, tb // 8, tile, hc_s[0:1, :], unroll=2)
        h = b_s[...]
        h_ref[...] = h
        rec_ref[...] = (h * _gelu(gr_ref[...])).astype(BF16)

    vec = _resident((1, D_RNN))
    kept = jax.ShapeDtypeStruct((S, D_RNN), F32)
    return _pcall(
        body, (xr, gr, conv_w, conv_b, wa, wx, ba, bx, lam), name="rnn_fwd", grid=(S // tb,), sem="arbitrary", comm=comm,
        in_specs=[_rows(tb, D_RNN), _rows(tb, D_RNN), _resident((4, D_RNN)), vec,
                  _resident((D_RNN, D_RNN)), _resident((D_RNN, D_RNN)), vec, vec, vec],
        out_specs=[_rows(tb, D_RNN)] * 7,
        out_shape=[jax.ShapeDtypeStruct((S, D_RNN), BF16), kept, kept, kept, kept, kept, kept],
        scratch_shapes=[pltpu.VMEM((8, D_RNN), F32), pltpu.VMEM((8, D_RNN), F32),
                        pltpu.VMEM((tb, D_RNN), F32), pltpu.VMEM((tb, D_RNN), F32)])


def _mix_ln1_up(x, att, rec, w_out, ln1_g, ln1_b, w_up, fcw, fcb, comm=None):
    S = x.shape[0]
    tb = min(256, S)
    nblk, _, wblk = w_up.shape
    half = nblk // 2

    def body(x_ref, att_ref, rec_ref, wo_ref, g_ref, b_ref, wu_ref, fcw_ref, fcb_ref,
             z1_ref, h1_ref, h1b_ref, gate_ref, act_ref, gl_ref, vdgl_ref, halo_s):
        @pl.when(pl.program_id(0) == 0)
        def _():
            halo_s[...] = jnp.zeros_like(halo_s)

        z1 = ALPHA * x_ref[...] + _dot(att_ref[...], wo_ref[:D_ATT, :]) + _dot(rec_ref[...], wo_ref[D_ATT:, :])
        z1_ref[...] = z1
        xhat, _ = _ln_stats(z1)
        h1 = xhat * g_ref[...] + b_ref[...]
        h1_ref[...] = h1
        h1b = h1.astype(BF16)
        h1b_ref[...] = h1b
        for jj in range(half):
            cols = slice(jj * wblk, (jj + 1) * wblk)
            gate = _dot(h1b, wu_ref[jj])
            val = _dot(h1b, wu_ref[jj + half])
            halo = halo_s[:, cols]
            conv = (fcb_ref[:, cols] + fcw_ref[2:3, cols] * gate + fcw_ref[1:2, cols] * _shift_down(gate, halo, 1)
                    + fcw_ref[0:1, cols] * _shift_down(gate, halo, 2))
            halo_s[:, cols] = gate[tb - 8:]
            gl, dgl = _gelu_and_grad(conv)
            gate_ref[:, cols] = gate.astype(BF16)
            act_ref[:, cols] = (gl * val).astype(BF16)
            gl_ref[:, cols] = gl.astype(BF16)
            vdgl_ref[:, cols] = (val * dgl).astype(BF16)

    vec = _resident((1, D_MODEL))
    wide = jax.ShapeDtypeStruct((S, D_FF), BF16)
    return _pcall(
        body, (x, att, rec, w_out, ln1_g, ln1_b, w_up, fcw, fcb), name="mix_ln1_up", grid=(S // tb,),
        sem="arbitrary", comm=comm,
        in_specs=[_rows(tb, D_MODEL), _rows(tb, D_ATT), _rows(tb, D_RNN), _resident((D_MODEL, D_MODEL)), vec, vec,
                  _resident(w_up.shape), _resident((3, D_FF)), _resident((1, D_FF))],
        out_specs=[_rows(tb, D_MODEL), _rows(tb, D_MODEL), _rows(tb, D_MODEL)] + [_rows(tb, D_FF)] * 4,
        out_shape=[jax.ShapeDtypeStruct((S, D_MODEL), F32), jax.ShapeDtypeStruct((S, D_MODEL), F32),
                   jax.ShapeDtypeStruct((S, D_MODEL), BF16), wide, wide, wide, wide],
        scratch_shapes=[pltpu.VMEM((8, D_FF), F32)])


def _tail(act, gl, vdgl, h1, h1b, p, tgt, w_down, w_pg, b_pg, w_pp, ln2_g, ln2_b):
    S = h1.shape[0]
    tb = min(256, S)

    def body(act_ref, gl_ref, vdgl_ref, h1_ref, h1b_ref, p_ref, t_ref, wd_ref, wpg_ref, bpg_ref, wpp_ref, g2_ref, b2_ref,
             dz2_ref, dpre_ref, dpp_ref, dgc_ref, dval_ref, dh1_ref, acc_ref):
        i = pl.program_id(0)

        @pl.when(i == 0)
        def _():
            acc_ref[...] = jnp.zeros_like(acc_ref)

        ffn = _dot(act_ref[...], wd_ref[...])
        h1 = h1_ref[...]
        sg = _sigmoid(_dot(h1b_ref[...], wpg_ref[...]) + bpg_ref[...])
        pp = _dot(p_ref[...].astype(BF16), wpp_ref[...])
        z2 = ALPHA * h1 + ffn + sg * pp
        xhat2, rstd2 = _ln_stats(z2)
        y = xhat2 * g2_ref[...] + b2_ref[...]
        err = y - t_ref[...]
        dy = err * (1.0 / D_MODEL)
        loss = 0.5 * jnp.sum(jnp.sum(err * err, axis=1, keepdims=True), axis=0, keepdims=True) * (1.0 / D_MODEL)
        dz2 = _ln_bwd(dy, xhat2, rstd2, g2_ref[...])
        dz2b = dz2.astype(BF16)
        dz2_ref[...] = dz2b
        dpre = dz2 * pp * sg * (1.0 - sg)
        dpreb = dpre.astype(BF16)
        dpre_ref[...] = dpreb
        dpp_ref[...] = (dz2 * sg).astype(BF16)
        dh1_ref[...] = ALPHA * dz2 + _dot_nt(dpreb, wpg_ref[...])
        dactb = _dot_nt(dz2b, wd_ref[...]).astype(BF16)
        dval_ref[...] = dactb * gl_ref[...]
        dgc_ref[...] = dactb * vdgl_ref[...]
        _put_rows(acc_ref, [_row_sum(dy * xhat2), _row_sum(dy), _row_sum(dpre),
                            jnp.broadcast_to(loss, (1, D_MODEL))])

    vec = _resident((1, D_MODEL))
    return pl.pallas_call(
        body, name="tail", grid=(S // tb,),
        in_specs=[_rows(tb, D_FF), _rows(tb, D_FF), _rows(tb, D_FF), _rows(tb, D_MODEL), _rows(tb, D_MODEL),
                  _rows(tb, PLE_DIM), _rows(tb, D_MODEL), _resident((D_FF, D_MODEL)), _resident((D_MODEL, D_MODEL)), vec,
                  _resident((PLE_DIM, D_MODEL)), vec, vec],
        out_specs=[_rows(tb, D_MODEL), _rows(tb, D_MODEL), _rows(tb, D_MODEL), _rows(tb, D_FF),
                   _rows(tb, D_FF), _rows(tb, D_MODEL), _acc((8, D_MODEL))],
        out_shape=[jax.ShapeDtypeStruct((S, D_MODEL), BF16),
                   jax.ShapeDtypeStruct((S, D_MODEL), BF16), jax.ShapeDtypeStruct((S, D_MODEL), BF16),
                   jax.ShapeDtypeStruct((S, D_FF), BF16), jax.ShapeDtypeStruct((S, D_FF), BF16),
                   jax.ShapeDtypeStruct((S, D_MODEL), F32), jax.ShapeDtypeStruct((8, D_MODEL), F32)],
        compiler_params=_params("arbitrary"),
    )(act, gl, vdgl, h1, h1b, p, tgt, w_down, w_pg, b_pg, w_pp, ln2_g, ln2_b)


def _weight_grad(a_list, b_list, name, layout, ts=512, comm=None, b_window=None, halves=False):
    S = a_list[0].shape[0]
    ms = [a.shape[1] for a in a_list]
    M, nb = sum(ms), len(b_list)
    win, Nb = b_window if b_window else (0, b_list[0].shape[1])
    ts = min(ts, S)
    nk = S // ts
    per_b = N_DEV // nb
    na = len(a_list)

    n_out = 2 if halves else 1
    assert layout == "cols" or not halves

    def body(*refs):
        a_refs, b_refs, o_refs, acc_ref = refs[:na], refs[na:na + nb], refs[na + nb:na + nb + n_out], refs[-1]
        o_ref = o_refs[0]
        j, k = pl.program_id(0), pl.program_id(1)

        @pl.when(k == 0)
        def _():
            acc_ref[...] = jnp.zeros_like(acc_ref)

        for jj in range(nb):
            @pl.when(j == jj)
            def _():
                b = b_refs[jj][...].astype(BF16)
                off = 0
                for a_ref, m in zip(a_refs, ms):
                    acc_ref[off:off + m, :] += _dot_tn(a_ref[...].astype(BF16), b)
                    off += m

        @pl.when(k == nk - 1)
        def _():
            for d in range(per_b):
                if layout == "rows":
                    o_ref[d] = acc_ref[d * (M // N_DEV):(d + 1) * (M // N_DEV), :].astype(BF16)
                elif layout == "cols" and halves:
                    for o_half, r0 in zip(o_refs, (0, M // 2)):
                        o_half[d] = acc_ref[r0:r0 + M // 2, d * (Nb // per_b):(d + 1) * (Nb // per_b)].astype(BF16)
                elif layout == "cols":
                    o_ref[d] = acc_ref[:, d * (Nb // per_b):(d + 1) * (Nb // per_b)].astype(BF16)
                else:
                    o_ref[d] = acc_ref[:, d * (Nb // per_b):(d + 1) * (Nb // per_b)].T.astype(BF16)

    def b_index(jj):
        return lambda j, k: (jnp.where(j == jj, k, jnp.where(j < jj, 0, nk - 1)), win)

    if layout == "rows":
        assert nb == 1
        blk = (N_DEV, M // N_DEV, Nb)
    elif layout == "cols":
        blk = (per_b, M // n_out, Nb // per_b)
    else:
        blk = (per_b, Nb // per_b, M)
    res, comm_res = _pcall(
        body, (*a_list, *b_list), name=name, grid=(nb, nk), sem="arbitrary", comm=comm, step_axis=1,
        in_specs=[pl.BlockSpec((ts, m), lambda j, k: (k, 0)) for m in ms]
        + [pl.BlockSpec((ts, Nb), b_index(jj)) for jj in range(nb)],
        out_specs=[pl.BlockSpec(blk, lambda j, k: (j, 0, 0))] * n_out,
        out_shape=[jax.ShapeDtypeStruct((N_DEV,) + blk[1:], BF16)] * n_out,
        scratch_shapes=[pltpu.VMEM((M, Nb), F32)])
    res = res if halves else res[0]
    return (res, comm_res) if comm is not None else res


def _up_bwd(dgc, gate, dval, dh1p, z1, w_up, fcw, w_out, ln1_g, comm=None):
    S = z1.shape[0]
    tb = min(256, S)
    t16 = tb // 16
    n16 = S // 16
    nblk, _, wblk = w_up.shape
    half = nblk // 2
    nsteps = S // tb

    def body(dgc_ref, dgn_ref, gc_ref, dval_ref, dh1p_ref, z1_ref, wu_ref, fcw_ref, wo_ref, g1_ref,
             dgate_ref, dz1_ref, dz1b_ref, datt_ref, drec_ref, accf_ref, accd_ref):
        i = pl.program_id(0)

        @pl.when(i == 0)
        def _():
            accf_ref[...] = jnp.zeros_like(accf_ref)
            accd_ref[...] = jnp.zeros_like(accd_ref)

        dg = dgc_ref[...].astype(F32)
        nxt = jnp.where(i < nsteps - 1, dgn_ref[...].astype(F32)[0:8], 0.0)
        w = _w_rows(fcw_ref)
        up1, up2 = _shift_up(dg, nxt, 1), _shift_up(dg, nxt, 2)
        dgate = (w[2] * dg + w[1] * up1 + w[0] * up2).astype(BF16)
        dgate_ref[...] = dgate
        gate = gc_ref[...].astype(F32)
        _put_rows(accf_ref, [_row_sum(up2 * gate), _row_sum(up1 * gate), _row_sum(dg * gate), _row_sum(dg)])

        dh1 = dh1p_ref[...]
        for j in range(nblk):
            src = dgate if j < half else dval_ref[...]
            jj = j % half
            dh1 = dh1 + _dot_nt(src[:, jj * wblk:(jj + 1) * wblk], wu_ref[j])
        xhat1, rstd1 = _ln_stats(z1_ref[...])
        dz1 = _ln_bwd(dh1, xhat1, rstd1, g1_ref[...])
        dz1_ref[...] = dz1
        dz1b = dz1.astype(BF16)
        dz1b_ref[...] = dz1b
        dcat = _dot_nt(dz1b, wo_ref[...])
        datt_ref[...] = dcat[:, :D_ATT].astype(BF16)
        drec_ref[...] = dcat[:, D_ATT:]
        _put_rows(accd_ref, [_row_sum(dh1 * xhat1), _row_sum(dh1)])

    next16 = pl.BlockSpec((16, D_FF), lambda i: (jnp.minimum((i + 1) * t16, n16 - 1), 0))
    return _pcall(
        body, (dgc, dgc, gate, dval, dh1p, z1, w_up, fcw, w_out, ln1_g), name="up_bwd",
        grid=(nsteps,), sem="arbitrary", comm=comm,
        in_specs=[_rows(tb, D_FF), next16, _rows(tb, D_FF), _rows(tb, D_FF), _rows(tb, D_MODEL),
                  _rows(tb, D_MODEL), _resident(w_up.shape), _resident((3, D_FF)),
                  _resident((D_MODEL, D_MODEL)), _resident((1, D_MODEL))],
        out_specs=[_rows(tb, D_FF), _rows(tb, D_MODEL), _rows(tb, D_MODEL), _rows(tb, D_ATT), _rows(tb, D_RNN),
                   _acc((8, D_FF)), _acc((8, D_MODEL))],
        out_shape=[jax.ShapeDtypeStruct((S, D_FF), BF16), jax.ShapeDtypeStruct((S, D_MODEL), F32),
                   jax.ShapeDtypeStruct((S, D_MODEL), BF16), jax.ShapeDtypeStruct((S, D_ATT), BF16),
                   jax.ShapeDtypeStruct((S, D_RNN), F32), jax.ShapeDtypeStruct((8, D_FF), F32),
                   jax.ShapeDtypeStruct((8, D_MODEL), F32)])


def _attn_bwd(q, k, v, lse, do, sinks, comm=None):
    S = q.shape[0]
    grp = N_HEADS // N_KV
    nq = min(ATT_STEP, S // QBLK)

    def body(sink_ref, q_ref, kc_ref, kp_ref, vc_ref, vp_ref, do_ref, lse_ref, dq_ref, dkc_ref, dkp_ref, dvc_ref, dvp_ref,
             ds_ref):
        i = pl.program_id(0)

        @pl.when(i == 0)
        def _():
            ds_ref[...] = jnp.zeros_like(ds_ref)

        row8 = lax.broadcasted_iota(jnp.int32, (8, 128), 0)
        lane8 = lax.broadcasted_iota(jnp.int32, (8, 128), 1)
        dsink = jnp.zeros((8, 128), F32)
        kall = jnp.concatenate([kp_ref[...], kc_ref[...]], axis=0)
        vall = jnp.concatenate([vp_ref[...], vc_ref[...]], axis=0)
        dk_t = [jnp.zeros((D_KV, QBLK), F32) for _ in range(nq + 1)]
        dv_t = [jnp.zeros((D_KV, QBLK), F32) for _ in range(nq + 1)]
        for b in range(nq):
            valid = _band_mask(i * nq + b)
            rows = slice(b * QBLK, (b + 1) * QBLK)
            keys = slice(b * QBLK, (b + 2) * QBLK)
            qv, dov = q_ref[rows, :], do_ref[rows, :]
            dqs, dks, dvs = [], [], []
            for g in range(N_KV):
                kcat = kall[keys, g * HEAD_DIM:(g + 1) * HEAD_DIM]
                vcat = vall[keys, g * HEAD_DIM:(g + 1) * HEAD_DIM]
                q4, do4 = _stack_heads(qv, g), _stack_heads(dov, g)
                s = jnp.where(valid, _dot_nt(q4, kcat), -1e30)
                lse = lse_ref[(b * N_KV + g) * GROUP * QBLK:(b * N_KV + g + 1) * GROUP * QBLK, :]
                p = jnp.exp(s - lse)
                p_sink = jnp.exp(_sink_column(sink_ref, g) - lse)
                dp = _dot_nt(do4, vcat)
                delta = jnp.sum(p * dp, axis=1, keepdims=True)
                dsc = (p * (dp - delta)).astype(BF16)
                dqs += _unstack_heads(_dot(dsc, kcat) * (HEAD_DIM ** -0.5))
                dks.append(_dot_tn(q4, dsc))
                dvs.append(_dot_tn(do4, p.astype(BF16)))
                for hh, part in enumerate(_unstack_heads(-p_sink * delta)):
                    here = (row8 == 0) & (lane8 == g * grp + hh)
                    dsink = dsink + jnp.where(here, jnp.sum(part, axis=0, keepdims=True), 0.0)
            dq_ref[rows, :] = jnp.concatenate(dqs, axis=1).astype(BF16)
            dk2, dv2 = jnp.concatenate(dks, axis=0), jnp.concatenate(dvs, axis=0)
            dk_t[b], dk_t[b + 1] = dk_t[b] + dk2[:, :QBLK], dk_t[b + 1] + dk2[:, QBLK:]
            dv_t[b], dv_t[b + 1] = dv_t[b] + dv2[:, :QBLK], dv_t[b + 1] + dv2[:, QBLK:]
        dkp_ref[...] = dk_t[0].T
        dvp_ref[...] = dv_t[0].T
        for b in range(nq):
            dkc_ref[b * QBLK:(b + 1) * QBLK, :] = dk_t[b + 1].T
            dvc_ref[b * QBLK:(b + 1) * QBLK, :] = dv_t[b + 1].T
        ds_ref[...] += dsink

    nsteps = S // (nq * QBLK)
    cur = jax.ShapeDtypeStruct((S, D_KV), F32)
    prev = jax.ShapeDtypeStruct((nsteps * QBLK, D_KV), F32)
    big = _rows(nq * QBLK, D_ATT)
    return _pcall(
        body, (sinks, q, k, k, v, v, do, lse), name="attn_bwd", grid=(nsteps,), sem="arbitrary", comm=comm,
        in_specs=[pl.BlockSpec(memory_space=pltpu.SMEM), big] + _attn_specs(nq) + [big, _rows(nq * N_HEADS * QBLK, 1)],
        out_specs=[big, _rows(nq * QBLK, D_KV), _rows(QBLK, D_KV), _rows(nq * QBLK, D_KV), _rows(QBLK, D_KV),
                   _acc((8, 128))],
        out_shape=[jax.ShapeDtypeStruct((S, D_ATT), BF16), cur, prev, cur, prev, jax.ShapeDtypeStruct((8, 128), F32)])


def _rnn_bwd(xr, gr, h, kept, drec, conv_w, wa, wx, lam, comm=None):
    S = xr.shape[0]
    tb = min(256, S)
    t8 = tb // 8
    nsteps = S // tb

    def body(xr_ref, xp_ref, gr_ref, h_ref, hp_ref, xc_ref, r_ref, ig_ref, a_ref, f_ref, drec_ref, cw_ref, wa_ref, wx_ref,
             lam_ref, dxr_ref, dgr_ref, gwa_ref, gwx_ref, acc_ref, carry_s, dxc_halo_s, d_s, gwa_s, gwx_s):
        i = pl.program_id(0)
        blk = nsteps - 1 - i

        @pl.when(i == 0)
        def _():
            gwa_s[...] = jnp.zeros_like(gwa_s)
            gwx_s[...] = jnp.zeros_like(gwx_s)
            acc_ref[...] = jnp.zeros_like(acc_ref)
            carry_s[...] = jnp.zeros_like(carry_s)
            dxc_halo_s[...] = jnp.zeros_like(dxc_halo_s)

        x = xr_ref[...]
        xhalo = jnp.where(blk > 0, xp_ref[...], 0.0)
        cw = _w_rows(cw_ref)
        xs = [_shift_down(x, xhalo, 3), _shift_down(x, xhalo, 2), _shift_down(x, xhalo, 1), x]
        xc, r, ig, a, f = xc_ref[...], r_ref[...], ig_ref[...], a_ref[...], f_ref[...]
        sp = _softplus_neg(lam_ref[...])
        hcur = h_ref[...]
        hprev = _shift_down(hcur, jnp.where(blk > 0, hp_ref[...], 0.0), 1)
        gl, dgl = _gelu_and_grad(gr_ref[...])
        drec = drec_ref[...]
        dgr_ref[...] = (drec * hcur * dgl).astype(BF16)
        d_s[...] = drec * gl
        row8 = lax.broadcasted_iota(jnp.int32, (8, D_RNN), 0)

        def tile(t, c):
            o = pl.multiple_of((t8 - 1 - t) * 8, 8)
            a8 = a_ref[pl.ds(o, 8), :]
            dt = d_s[pl.ds(o, 8), :]
            at = jnp.where(row8 == 7, 1.0, pltpu.roll(a8, 7, 0))
            for s in (1, 2, 4):
                keep = row8 < 8 - s
                a_sh = jnp.where(keep, pltpu.roll(at, 8 - s, 0), 1.0)
                d_sh = jnp.where(keep, pltpu.roll(dt, 8 - s, 0), 0.0)
                dt = at * d_sh + dt
                at = at * a_sh
            lt = at * c + dt
            d_s[pl.ds(o, 8), :] = lt
            return _row_sum(jnp.where(row8 == 0, a8 * lt, 0.0))

        carry_s[0:1, :] = lax.fori_loop(0, t8, tile, carry_s[0:1, :], unroll=2)
        lmb = d_s[...]
        a2 = a * a
        dla = lmb * hprev * a - lmb * ig * xc * (a2 / f)
        di = lmb * f * xc
        dr = dla * (-LRU_C) * sp
        dpa = dr * r * (1.0 - r)
        dpx = di * ig * (1.0 - ig)
        dpab = dpa.astype(BF16)
        dpxb = dpx.astype(BF16)
        xcb = xc.astype(BF16)
        gwa_s[...] += _dot_tn(xcb, dpab)
        gwx_s[...] += _dot_tn(xcb, dpxb)

        @pl.when(i == nsteps - 1)
        def _():
            for dense, out in ((gwa_s[...], gwa_ref), (gwx_s[...], gwx_ref)):
                for b in range(RNN_BLOCKS):
                    rows = slice(b * HEAD_DIM, (b + 1) * HEAD_DIM)
                    out[rows, :] = dense[rows, b * HEAD_DIM:(b + 1) * HEAD_DIM]

        dxc = lmb * f * ig + _dot_nt(dpab, wa_ref[...]) + _dot_nt(dpxb, wx_ref[...])
        nxt = dxc_halo_s[...]
        dxr = cw[3] * dxc
        for s in (1, 2, 3):
            dxr = dxr + cw[3 - s] * _shift_up(dxc, nxt, s)
        dxr_ref[...] = dxr.astype(BF16)
        dxc_halo_s[...] = dxc[:8]
        dlam = _row_sum(dla * (-LRU_C) * r) * (-1.0 / (1.0 + jnp.exp(lam_ref[...])))
        _put_rows(acc_ref, [_row_sum(dxc * xs[0]), _row_sum(dxc * xs[1]), _row_sum(dxc * xs[2]), _row_sum(dxc * xs[3]),
                            _row_sum(dxc), _row_sum(dpa), _row_sum(dpx), dlam])

    rev = lambda i: (nsteps - 1 - i, 0)
    prev8 = lambda i: (jnp.maximum((nsteps - 1 - i) * t8 - 1, 0), 0)
    blkspec = pl.BlockSpec((tb, D_RNN), rev)
    halo8 = pl.BlockSpec((8, D_RNN), prev8)
    vec = _resident((1, D_RNN))
    return _pcall(
        body, (xr, xr, gr, h, h, *kept, drec, conv_w, wa, wx, lam), name="rnn_bwd", grid=(nsteps,),
        sem="arbitrary", comm=comm,
        in_specs=[blkspec, halo8, blkspec, blkspec, halo8] + [blkspec] * 6
        + [_resident((4, D_RNN)), _resident((D_RNN, D_RNN)), _resident((D_RNN, D_RNN)), vec],
        out_specs=[blkspec, blkspec, _acc((D_RNN, HEAD_DIM)), _acc((D_RNN, HEAD_DIM)), _acc((8, D_RNN))],
        out_shape=[jax.ShapeDtypeStruct((S, D_RNN), BF16), jax.ShapeDtypeStruct((S, D_RNN), BF16),
                   jax.ShapeDtypeStruct((D_RNN, HEAD_DIM), F32), jax.ShapeDtypeStruct((D_RNN, HEAD_DIM), F32),
                   jax.ShapeDtypeStruct((8, D_RNN), F32)],
        scratch_shapes=[pltpu.VMEM((8, D_RNN), F32), pltpu.VMEM((8, D_RNN), F32), pltpu.VMEM((tb, D_RNN), F32),
                        pltpu.VMEM((D_RNN, D_RNN), F32), pltpu.VMEM((D_RNN, D_RNN), F32)])


def _in_bwd(dq, dkc, dkp, dvc, dvp, dxr, dgr, dz1, w_in, comm=None):
    S = dz1.shape[0]
    tb = min(ATT_STEP * QBLK, S)
    nsteps = S // tb

    def body(dq_ref, dkc_ref, dkn_ref, dvc_ref, dvn_ref, dxr_ref, dgr_ref, dz1_ref, w_ref, du_ref, dx_ref):
        last = pl.program_id(0) == nsteps - 1

        def total(cur_ref, next_ref):
            nxt = jnp.where(last, 0.0, next_ref[...])
            tail = cur_ref[tb - QBLK:, :] + nxt
            return jnp.concatenate([cur_ref[:tb - QBLK, :], tail], axis=0) if tb > QBLK else tail

        dk = total(dkc_ref, dkn_ref).astype(BF16)
        dv = total(dvc_ref, dvn_ref).astype(BF16)
        du = jnp.concatenate([dq_ref[...], dk, dv, dxr_ref[...], dgr_ref[...]], axis=1)
        du_ref[...] = du
        dx_ref[...] = ALPHA * dz1_ref[...] + _dot(du, w_ref[...])

    nextp = pl.BlockSpec((QBLK, D_KV), lambda i: (jnp.minimum(i + 1, nsteps - 1), 0))
    return _pcall(
        body, (dq, dkc, dkp, dvc, dvp, dxr, dgr, dz1, w_in), name="in_bwd", grid=(nsteps,), comm=comm,
        in_specs=[_rows(tb, D_ATT), _rows(tb, D_KV), nextp, _rows(tb, D_KV), nextp,
                  _rows(tb, D_RNN), _rows(tb, D_RNN), _rows(tb, D_MODEL), _resident((D_IN, D_MODEL))],
        out_specs=[_rows(tb, D_IN), _rows(tb, D_MODEL)],
        out_shape=[jax.ShapeDtypeStruct((S, D_IN), BF16), jax.ShapeDtypeStruct((S, D_MODEL), F32)])


def _block_diag(w):
    eye = jnp.eye(RNN_BLOCKS, dtype=w.dtype)
    return (w[:, :, None, :] * eye[:, None, :, None]).reshape(D_RNN, D_RNN).astype(BF16)


def _adamw(w, g, m, v):
    m = ADAM_B1 * m + (1.0 - ADAM_B1) * g
    v = ADAM_B2 * v + (1.0 - ADAM_B2) * (g * g)
    m_hat = m / (1.0 - ADAM_B1 ** ADAM_STEP)
    v_hat = v / (1.0 - ADAM_B2 ** ADAM_STEP)
    delta = -ADAM_LR * (m_hat / (jnp.sqrt(v_hat) + ADAM_EPS) + ADAM_WD * w)
    return delta, m, v


def _sum_adamw(parts, w, m, v, name):
    parts = parts if isinstance(parts, (list, tuple)) else [parts]
    R, C = w.shape
    rb = R if R <= 256 else (256 if parts[0].shape[1] % 256 == 0 else 128)
    per = parts[0].shape[1] // rb
    assert R % rb == 0 and parts[0].shape[1] % rb == 0
    n = len(parts)

    def body(*refs):
        p_refs = refs[:n]
        w_ref, m_ref, v_ref, g_out, d_out, m_out, v_out = refs[n:]
        which = pl.program_id(0) // per

        def total(p_ref):
            g = p_ref[0].astype(F32)
            for d in range(1, N_DEV):
                g = g + p_ref[d].astype(F32)
            return g

        g = total(p_refs[0])
        for j in range(1, n):
            g = jnp.where(which == j, total(p_refs[j]), g)
        delta, mn, vn = _adamw(w_ref[...], g, m_ref[...], v_ref[...])
        g_out[...] = g
        d_out[...] = delta
        m_out[...] = mn
        v_out[...] = vn

    def part_spec(j):
        return pl.BlockSpec((N_DEV, rb, C), lambda i: (0, jnp.clip(i - j * per, 0, per - 1), 0))

    blk = _rows(rb, C)
    out = jax.ShapeDtypeStruct((R, C), F32)
    return pl.pallas_call(
        body, name=name, grid=(R // rb,),
        in_specs=[part_spec(j) for j in range(n)] + [blk, blk, blk],
        out_specs=[blk, blk, blk, blk], out_shape=[out, out, out, out],
        compiler_params=_params("parallel"),
    )(*parts, w, m, v)


_SMALL = [("attn_sinks", "s", 0, 1, None), ("rnn_conv_w", "r", 0, 4, "cols"), ("rnn_conv_b", "r", 4, 1, None),
          ("gate_a_w", "a", 0, D_RNN, None), ("gate_a_b", "r", 5, 1, None), ("gate_x_w", "x", 0, D_RNN, None),
          ("gate_x_b", "r", 6, 1, None), ("lru_lambda", "r", 7, 1, None), ("ln1_g", "d", 0, 1, None),
          ("ln1_b", "d", 1, 1, None), ("ffn_conv_w", "f", 0, 3, "cols"), ("ffn_conv_b", "f", 3, 1, None),
          ("ple_gate_b", "t", 2, 1, None), ("ln2_g", "t", 0, 1, None), ("ln2_b", "t", 1, 1, None)]
_LOSS_ROW = 3


_ACC_COLS = {"t": (0, D_MODEL), "f": (D_MODEL, D_FF), "d": (D_MODEL + D_FF, D_MODEL), "s": (2 * D_MODEL + D_FF, 128),
             "r": (2 * D_MODEL + D_FF + 128, D_RNN)}
_ACC_WIDTH = 2 * D_MODEL + D_FF + 128 + D_RNN


def _small_update(rows_all, gates_all, params):
    flat = [arr for triple in params for arr in triple]
    n_par = len(_SMALL)

    def body(*refs):
        rows_ref, gates_ref = refs[:2]
        p_refs = refs[2:2 + 3 * n_par]
        loss_ref = refs[2 + 3 * n_par]
        o_refs = refs[3 + 3 * n_par:3 + 7 * n_par]
        rows_s, tmp_r, tmp_f = refs[3 + 7 * n_par:]
        me = _dev_index(*_place())
        rows_sum, gates_sum = rows_ref[0], gates_ref[0]
        for d in range(1, N_DEV):
            rows_sum = rows_sum + rows_ref[d]
            gates_sum = gates_sum + gates_ref[d]
        rows_s[...] = rows_sum
        t0 = _ACC_COLS["t"][0]
        loss_ref[...] = rows_s[_LOSS_ROW:_LOSS_ROW + 1, t0:t0 + 128]
        for i, (name, key, row, rows, how) in enumerate(_SMALL):
            w_ref, m_ref, v_ref = p_refs[3 * i:3 * i + 3]
            g_out, d_out, m_out, v_out = o_refs[4 * i:4 * i + 4]
            if key == "a":
                g = gates_sum[:, :HEAD_DIM]
            elif key == "x":
                g = gates_sum[:, HEAD_DIM:]
            elif how == "cols":
                c0, width = _ACC_COLS[key]
                full = rows_s[:, c0:c0 + width]
                shard = width // N_DEV
                mine = full[:, :shard]
                for d in range(1, N_DEV):
                    mine = jnp.where(me == d, full[:, d * shard:(d + 1) * shard], mine)
                tmp = tmp_r if key == "r" else tmp_f
                tmp[...] = mine
                g = tmp[row:row + rows, :]
            else:
                c0, width = _ACC_COLS[key]
                g = rows_s[row:row + rows, c0:c0 + width][:, :w_ref.shape[1]]
            delta, mn, vn = _adamw(w_ref[...], g, m_ref[...], v_ref[...])
            g_out[...] = g
            d_out[...] = delta
            m_out[...] = mn
            v_out[...] = vn

    outs = [jax.ShapeDtypeStruct((1, 128), F32)]
    for w, _, _ in params:
        outs += [jax.ShapeDtypeStruct(w.shape, F32)] * 4
    scratch = [pltpu.VMEM((8, _ACC_WIDTH), F32), pltpu.VMEM((8, D_RNN // N_DEV), F32), pltpu.VMEM((8, D_FF // N_DEV), F32)]
    res = pl.pallas_call(body, name="small_update", out_shape=outs, scratch_shapes=scratch)(rows_all, gates_all, *flat)
    return res[0], [res[1 + 4 * i:5 + 4 * i] for i in range(n_par)]


def kernel(x, p, w_in, attn_sinks, rnn_conv_w, rnn_conv_b, gate_a_w, gate_a_b, gate_x_w, gate_x_b, lru_lambda, w_out, ln1_g, ln1_b, w_ffn_up, ffn_conv_w, ffn_conv_b, w_ffn_down, ple_gate_w, ple_gate_b, ple_proj, ln2_g, ln2_b, loss_target, m_w_in, m_attn_sinks, m_rnn_conv_w, m_rnn_conv_b, m_gate_a_w, m_gate_a_b, m_gate_x_w, m_gate_x_b, m_lru_lambda, m_w_out, m_ln1_g, m_ln1_b, m_w_ffn_up, m_ffn_conv_w, m_ffn_conv_b, m_w_ffn_down, m_ple_gate_w, m_ple_gate_b, m_ple_proj, m_ln2_g, m_ln2_b, v_w_in, v_attn_sinks, v_rnn_conv_w, v_rnn_conv_b, v_gate_a_w, v_gate_a_b, v_gate_x_w, v_gate_x_b, v_lru_lambda, v_w_out, v_ln1_g, v_ln1_b, v_w_ffn_up, v_ffn_conv_w, v_ffn_conv_b, v_w_ffn_down, v_ple_gate_w, v_ple_gate_b, v_ple_proj, v_ln2_g, v_ln2_b):
    from_col_blocks = lambda g: g.transpose(1, 0, 2).reshape(g.shape[1], N_DEV * g.shape[2])

    xs, ps, tgt, sinks = x[0], p[0, 0], loss_target[0], attn_sinks[0]
    wa, wx = _block_diag(gate_a_w[0]), _block_diag(gate_x_w[0])

    conv_cols = jnp.concatenate([rnn_conv_w[0].reshape(1, -1), ffn_conv_w[0].reshape(1, -1)], axis=1)
    n_rc, n_fc = 4 * D_RNN // N_DEV, 3 * D_FF // N_DEV
    ((g_in,),) = _comm_call([_Gather([w_in[0].T.astype(BF16)])], "gather_w_in")
    w_in_full = g_in.reshape(D_IN, D_MODEL)

    (q, k, v, xr, gr), _ = _in_proj(xs, w_in_full)
    (att, lse), (g_out, g_conv) = _attn_fwd(
        q, k, v, sinks,
        comm=_Multi([_Gather([w_out[0].astype(BF16)]), _Bcast([jnp.broadcast_to(conv_cols, (8, n_rc + n_fc))])]))
    rcw = from_col_blocks(g_conv[:, 0, :n_rc].reshape(N_DEV, 4, D_RNN // N_DEV))
    fcw = from_col_blocks(g_conv[:, 0, n_rc:].reshape(N_DEV, 3, D_FF // N_DEV))
    (rec, h, *kept), (w_up,) = _rnn_fwd(xr, gr, rcw, rnn_conv_b, wa, wx, gate_a_b, gate_x_b, lru_lambda,
                                        comm=_Gather([w_ffn_up[0].astype(BF16)]))
    w_out_full = g_out.reshape(D_MODEL, D_MODEL)
    (z1, h1, h1b, gate, act, gl, vdgl), (g_down, g_pg, g_pp) = _mix_ln1_up(
        xs, att, rec, w_out_full, ln1_g, ln1_b, w_up, fcw, ffn_conv_b,
        comm=_Gather([w_ffn_down[0].astype(BF16), ple_gate_w[0].astype(BF16), ple_proj[0].astype(BF16)]))
    dz2b, dpreb, dppb, dgc, dval, dh1p, acc_t = _tail(
        act, gl, vdgl, h1, h1b, ps, tgt, g_down.reshape(D_FF, D_MODEL), g_pg.reshape(D_MODEL, D_MODEL), ple_gate_b,
        from_col_blocks(g_pp), ln2_g, ln2_b)

    gd_down = _weight_grad([dz2b], [act], "down_grad", "rows_t", ts=1024)
    gd_pg = _weight_grad([h1b], [dpreb], "pg_grad", "rows", ts=1024)
    gd_pp = _weight_grad([ps], [dppb], "pp_grad", "cols", ts=1024)
    (dgate, dz1, dz1b, datt, drec, acc_f, acc_d), (r_down, r_pg, r_pp) = _up_bwd(
        dgc, gate, dval, dh1p, z1, w_up, fcw, w_out_full, ln1_g, comm=_Exchange([gd_down, gd_pg, gd_pp]))
    gd_up_top, gd_up_bot = _weight_grad([h1b], [dgate, dval], "up_grad", "cols", halves=True)
    gd_out = _weight_grad([att, rec], [dz1b], "out_grad", "rows", ts=1024)
    (dq, dkc, dkp, dvc, dvp, acc_s), (r_up_top,) = _attn_bwd(q, k, v, lse, datt, sinks, comm=_Exchange([gd_up_top]))
    early = jnp.concatenate([acc_t, acc_f, acc_d], axis=1)
    (dxr, dgr, g_wa, g_wx, acc_r), (r_up_bot, r_out, early_all) = _rnn_bwd(
        xr, gr, h, kept, drec, rcw, wa, wx, lru_lambda, comm=_Multi([_Exchange([gd_up_bot, gd_out]), _Bcast([early])]))
    (du, dx), _ = _in_bwd(dq, dkc, dkp, dvc, dvp, dxr, dgr, dz1, w_in_full)
    lanes = D_RNN // 128
    late = jnp.concatenate([g_wa, g_wx], axis=1)
    late = jnp.concatenate([late, acc_s, acc_r.reshape(8, lanes, 128).transpose(1, 0, 2).reshape(8 * lanes, 128)], axis=0)
    width = D_MODEL // IN_GRAD_PARTS
    comm, r_parts = _Bcast([late]), []
    for part in range(IN_GRAD_PARTS):
        gd_part, got = _weight_grad([du], [xs], f"in_grad_{part}", "rows", ts=1024, b_window=(part, width), comm=comm)
        if part == 0:
            (late_all,) = got
        else:
            r_parts += got
        comm = _Exchange([gd_part])
    r_parts += _comm_call([comm], "exchange_w_in")[0]
    r_in = jnp.concatenate(r_parts, axis=2)
    acc_r_all = late_all[:, D_RNN + 8:].reshape(N_DEV, lanes, 8, 128).transpose(0, 2, 1, 3).reshape(N_DEV, 8, D_RNN)
    small_parts = (jnp.concatenate([early_all, late_all[:, D_RNN:D_RNN + 8], acc_r_all], axis=2),
                   late_all[:, :D_RNN])

    outs = {}
    res = _sum_adamw(r_in, w_in[0].T, m_w_in[0].T, v_w_in[0].T, "adamw_w_in")
    outs["w_in"] = [r.T[None] for r in res]
    for name, parts, w, m, v in [("w_out", r_out, w_out, m_w_out, v_w_out),
                                 ("w_ffn_up", [r_up_top, r_up_bot], w_ffn_up, m_w_ffn_up, v_w_ffn_up),
                                 ("w_ffn_down", r_down, w_ffn_down, m_w_ffn_down, v_w_ffn_down),
                                 ("ple_gate_w", r_pg, ple_gate_w, m_ple_gate_w, v_ple_gate_w),
                                 ("ple_proj", r_pp, ple_proj, m_ple_proj, v_ple_proj)]:
        res = _sum_adamw(parts, w[0], m[0], v[0], "adamw_" + name)
        outs[name] = [r[None] for r in res]

    given = dict(attn_sinks=(attn_sinks, m_attn_sinks, v_attn_sinks), rnn_conv_w=(rnn_conv_w, m_rnn_conv_w, v_rnn_conv_w),
                 rnn_conv_b=(rnn_conv_b, m_rnn_conv_b, v_rnn_conv_b), gate_a_w=(gate_a_w, m_gate_a_w, v_gate_a_w),
                 gate_a_b=(gate_a_b, m_gate_a_b, v_gate_a_b), gate_x_w=(gate_x_w, m_gate_x_w, v_gate_x_w),
                 gate_x_b=(gate_x_b, m_gate_x_b, v_gate_x_b), lru_lambda=(lru_lambda, m_lru_lambda, v_lru_lambda),
                 ln1_g=(ln1_g, m_ln1_g, v_ln1_g), ln1_b=(ln1_b, m_ln1_b, v_ln1_b),
                 ffn_conv_w=(ffn_conv_w, m_ffn_conv_w, v_ffn_conv_w), ffn_conv_b=(ffn_conv_b, m_ffn_conv_b, v_ffn_conv_b),
                 ple_gate_b=(ple_gate_b, m_ple_gate_b, v_ple_gate_b), ln2_g=(ln2_g, m_ln2_g, v_ln2_g),
                 ln2_b=(ln2_b, m_ln2_b, v_ln2_b))
    as_2d = lambda a: a.reshape(-1, a.shape[-1])
    loss_row, small_res = _small_update(*small_parts, [tuple(as_2d(a) for a in given[n]) for n, *_ in _SMALL])
    loss = loss_row[0, 0]
    for (n, *_), res in zip(_SMALL, small_res):
        outs[n] = [r.reshape(given[n][0].shape) for r in res]

    order = ["w_in", "attn_sinks", "rnn_conv_w", "rnn_conv_b", "gate_a_w", "gate_a_b", "gate_x_w", "gate_x_b",
             "lru_lambda", "w_out", "ln1_g", "ln1_b", "w_ffn_up", "ffn_conv_w", "ffn_conv_b", "w_ffn_down",
             "ple_gate_w", "ple_gate_b", "ple_proj", "ln2_g", "ln2_b"]
    return (loss, dx[None], *[outs[n][0] for n in order], *[outs[n][1] for n in order],
            *[outs[n][2] for n in order], *[outs[n][3] for n in order])
```

```python
import jax
import jax.numpy as jnp
from jax import lax
from jax.experimental import pallas as pl
from jax.experimental.pallas import tpu as pltpu

F32 = jnp.float32
BF16 = jnp.bfloat16

D_MODEL = 1024
D_ATT = 512
D_KV = 128
HEAD_DIM = 64
N_HEADS = 8
N_KV = 2
D_RNN = 512
RNN_BLOCKS = 8
D_IN = 1792
D_FF = 3072
PLE_DIM = 256
QBLK = 128
N_DEV = 8
ALPHA = float(2 ** 0.25)
LN_EPS = 1e-5
LRU_C = 8.0
ADAM_LR, ADAM_B1, ADAM_B2, ADAM_EPS, ADAM_WD, ADAM_STEP = 0.001, 0.9, 0.999, 1e-08, 0.01, 10

V7X_VMEM_LIMIT = 56 * 1024 * 1024
MESH = pl.DeviceIdType.MESH


def _params(*sem, vmem=V7X_VMEM_LIMIT):
    return pltpu.CompilerParams(dimension_semantics=sem or None, vmem_limit_bytes=vmem)


def _resident(shape):
    return pl.BlockSpec(shape, lambda *_: (0,) * len(shape), pipeline_mode=pl.Buffered(1))


def _rows(tb, cols):
    return pl.BlockSpec((tb, cols), lambda i: (i, 0))


def _acc(shape):
    return pl.BlockSpec(shape, lambda *_: (0,) * len(shape))


def _dot(a, b):
    return jnp.dot(a, b, preferred_element_type=F32)


def _dot_nt(a, b):
    return lax.dot_general(a, b, (((1,), (1,)), ((), ())), preferred_element_type=F32)


def _dot_tn(a, b):
    return lax.dot_general(a, b, (((0,), (0,)), ((), ())), preferred_element_type=F32)


def _sigmoid(x):
    return 1.0 / (1.0 + jnp.exp(-x))


_GELU_C = 0.7978845608028654
_GELU_K = 0.044715


def _gelu_and_grad(x):
    u = x * x
    t = jnp.tanh(x * (_GELU_C + (_GELU_C * _GELU_K) * u))
    hp = 0.5 + 0.5 * t
    dg = hp + x * (0.5 - 0.5 * (t * t)) * (_GELU_C + (3.0 * _GELU_C * _GELU_K) * u)
    return x * hp, dg


def _gelu(x):
    return 0.5 * x * (1.0 + jnp.tanh(_GELU_C * (x + _GELU_K * x * x * x)))


def _ln_stats(z):
    mu = jnp.mean(z, axis=-1, keepdims=True)
    zc = z - mu
    var = jnp.mean(zc * zc, axis=-1, keepdims=True)
    rstd = lax.rsqrt(var + LN_EPS)
    return zc * rstd, rstd


def _ln_bwd(dy, xhat, rstd, g):
    dxh = dy * g
    m1 = jnp.mean(dxh, axis=-1, keepdims=True)
    m2 = jnp.mean(dxh * xhat, axis=-1, keepdims=True)
    return rstd * (dxh - m1 - xhat * m2)


def _softplus_neg(lam):
    u = jnp.exp(-jnp.abs(lam))
    w = 1.0 + u
    d = w - 1.0
    log1p_u = jnp.where(d == 0.0, u, jnp.log(w) * (u / jnp.where(d == 0.0, 1.0, d)))
    return jnp.maximum(-lam, 0.0) + log1p_u


def _shift_down(x, halo, s):
    xs = pltpu.roll(x, s, 0)
    hs = pltpu.roll(halo, s, 0)
    row8 = lax.broadcasted_iota(jnp.int32, hs.shape, 0)
    first = jnp.where(row8 < s, hs, xs[:8])
    return jnp.concatenate([first, xs[8:]], axis=0)


def _shift_up(x, halo, s):
    n = x.shape[0]
    xs = pltpu.roll(x, n - s, 0)
    hs = pltpu.roll(halo, 8 - s, 0)
    row8 = lax.broadcasted_iota(jnp.int32, hs.shape, 0)
    last = jnp.where(row8 >= 8 - s, hs, xs[n - 8:])
    return jnp.concatenate([xs[:n - 8], last], axis=0)


def _row_sum(x):
    return jnp.sum(x, axis=0, keepdims=True)


def _put_rows(acc_ref, rows):
    row8 = lax.broadcasted_iota(jnp.int32, acc_ref.shape, 0)
    upd = jnp.zeros(acc_ref.shape, F32)
    for r, vec in enumerate(rows):
        upd = jnp.where(row8 == r, vec, upd)
    acc_ref[...] += upd


def _place():
    return lax.axis_index("x"), lax.axis_index("y"), lax.axis_index("c")


def _dev_index(px, py, pc):
    return 4 * px + 2 * py + pc


_ANY = pl.BlockSpec(memory_space=pl.ANY)


class _Gather:
    def __init__(self, arrays):
        self.arrays = list(arrays)
        self.n = len(self.arrays)

    def out_shape(self):
        return [jax.ShapeDtypeStruct((N_DEV,) + s.shape, s.dtype) for s in self.arrays]

    def scratch(self):
        return [pltpu.SemaphoreType.DMA((self.n, 7)), pltpu.SemaphoreType.DMA((self.n, 7)),
                pltpu.SemaphoreType.DMA((self.n,))]

    def _parts(self, ins, outs, sems):
        send_sems, recv_sems, local_sems = sems
        x, y, c = _place()
        me, sibling = (x, y, c), (x, y, 1 - c)
        chips = [(1 - x, y), (x, 1 - y), (1 - x, 1 - y)]

        def copy(a, k, block, to, src=None):
            rows = outs[a].at[_dev_index(*block)]
            return pltpu.make_async_remote_copy(
                src_ref=rows if src is None else src, dst_ref=rows, send_sem=send_sems.at[a, k],
                recv_sem=recv_sems.at[a, k], device_id=to, device_id_type=MESH)

        rng = range(self.n)
        mine = [pltpu.make_async_copy(ins[a], outs[a].at[_dev_index(*me)], local_sems.at[a]) for a in rng]
        first = [copy(a, 0, me, sibling, src=ins[a]) for a in rng]
        first += [copy(a, 1 + j, me, (*chip, c), src=ins[a]) for j, chip in enumerate(chips) for a in rng]
        landed = [copy(a, 1 + j, (*chip, c), me) for j, chip in enumerate(chips) for a in rng]
        passed = [copy(a, 4 + j, (*chip, c), sibling) for j, chip in enumerate(chips) for a in rng]
        from_sibling = [copy(a, 0, sibling, me) for a in rng]
        from_sibling += [copy(a, 4 + j, (*chip, 1 - c), me) for j, chip in enumerate(chips) for a in rng]
        return mine, first, landed, passed, from_sibling

    def start(self, ins, outs, sems):
        mine, first, _, _, _ = self._parts(ins, outs, sems)
        for cp in mine + first:
            cp.start()

    def forward(self, ins, outs, sems):
        _, _, landed, passed, _ = self._parts(ins, outs, sems)
        for got, fwd in zip(landed, passed):
            got.wait_recv()
            fwd.start()

    def finish(self, ins, outs, sems):
        mine, first, _, passed, from_sibling = self._parts(ins, outs, sems)
        for cp in from_sibling:
            cp.wait_recv()
        for cp in first + passed:
            cp.wait_send()
        for cp in mine:
            cp.wait()

    def before(self, ins, outs, sems, step, nsteps):
        pl.when(step == 0)(lambda: self.start(ins, outs, sems))
        pl.when(step == (7 * nsteps) // 8)(lambda: self.forward(ins, outs, sems))

    def after(self, ins, outs, sems, step, nsteps):
        pl.when(step == nsteps - 1)(lambda: self.finish(ins, outs, sems))


class _Exchange:
    def __init__(self, arrays):
        self.arrays = list(arrays)
        self.n = len(self.arrays)

    def out_shape(self):
        return [jax.ShapeDtypeStruct(b.shape, b.dtype) for b in self.arrays]

    def scratch(self):
        return [pltpu.SemaphoreType.DMA((self.n, 7)), pltpu.SemaphoreType.DMA((self.n, 7)),
                pltpu.SemaphoreType.DMA((self.n,))]

    def _parts(self, ins, outs, sems):
        send_sems, recv_sems, local_sems = sems
        x, y, c = _place()
        me = _dev_index(x, y, c)
        peers = [(x ^ (k >> 2), y ^ ((k >> 1) & 1), c ^ (k & 1)) for k in range(1, N_DEV)]
        rng = range(self.n)
        mine = [pltpu.make_async_copy(ins[a].at[me], outs[a].at[me], local_sems.at[a]) for a in rng]
        sent = [pltpu.make_async_remote_copy(
            src_ref=ins[a].at[_dev_index(*to)], dst_ref=outs[a].at[me], send_sem=send_sems.at[a, k],
            recv_sem=recv_sems.at[a, k], device_id=to, device_id_type=MESH) for k, to in enumerate(peers) for a in rng]
        arrivals = [pltpu.make_async_remote_copy(
            src_ref=ins[a].at[me], dst_ref=outs[a].at[_dev_index(*frm)], send_sem=send_sems.at[a, k],
            recv_sem=recv_sems.at[a, k], device_id=frm, device_id_type=MESH) for k, frm in enumerate(peers) for a in rng]
        return mine, sent, arrivals

    def start(self, ins, outs, sems):
        mine, sent, _ = self._parts(ins, outs, sems)
        for cp in mine + sent:
            cp.start()

    def finish(self, ins, outs, sems):
        mine, sent, arrivals = self._parts(ins, outs, sems)
        for cp in arrivals:
            cp.wait_recv()
        for cp in sent:
            cp.wait_send()
        for cp in mine:
            cp.wait()

    def before(self, ins, outs, sems, step, nsteps):
        pl.when(step == 0)(lambda: self.start(ins, outs, sems))

    def after(self, ins, outs, sems, step, nsteps):
        pl.when(step == nsteps - 1)(lambda: self.finish(ins, outs, sems))


class _Bcast(_Exchange):
    def out_shape(self):
        return [jax.ShapeDtypeStruct((N_DEV,) + s.shape, s.dtype) for s in self.arrays]

    def _parts(self, ins, outs, sems):
        send_sems, recv_sems, local_sems = sems
        x, y, c = _place()
        me = _dev_index(x, y, c)
        peers = [(x ^ (k >> 2), y ^ ((k >> 1) & 1), c ^ (k & 1)) for k in range(1, N_DEV)]
        rng = range(self.n)
        mine = [pltpu.make_async_copy(ins[a], outs[a].at[me], local_sems.at[a]) for a in rng]
        sent = [pltpu.make_async_remote_copy(
            src_ref=ins[a], dst_ref=outs[a].at[me], send_sem=send_sems.at[a, k], recv_sem=recv_sems.at[a, k],
            device_id=to, device_id_type=MESH) for k, to in enumerate(peers) for a in rng]
        arrivals = [pltpu.make_async_remote_copy(
            src_ref=ins[a], dst_ref=outs[a].at[_dev_index(*frm)], send_sem=send_sems.at[a, k],
            recv_sem=recv_sems.at[a, k], device_id=frm, device_id_type=MESH) for k, frm in enumerate(peers) for a in rng]
        return mine, sent, arrivals


class _Multi:
    def __init__(self, comms):
        self.comms = list(comms)
        self.arrays = [arr for c in self.comms for arr in c.arrays]
        self.n = len(self.arrays)

    def out_shape(self):
        return [s for c in self.comms for s in c.out_shape()]

    def scratch(self):
        return [s for c in self.comms for s in c.scratch()]

    def _each(self, ins, outs, sems):
        a = 0
        for j, c in enumerate(self.comms):
            yield c, ins[a:a + c.n], outs[a:a + c.n], sems[3 * j:3 * j + 3]
            a += c.n

    def before(self, ins, outs, sems, step, nsteps):
        for c, ci, co, cs in self._each(ins, outs, sems):
            c.before(ci, co, cs, step, nsteps)

    def after(self, ins, outs, sems, step, nsteps):
        for c, ci, co, cs in self._each(ins, outs, sems):
            c.after(ci, co, cs, step, nsteps)


def _comm_call(comms, name):
    ns = [c.n for c in comms]
    n = sum(ns)

    def body(*refs):
        parts, a, s = [], 0, 2 * n
        for c in comms:
            parts.append((c, refs[a:a + c.n], refs[n + a:n + a + c.n], refs[s:s + 3]))
            a, s = a + c.n, s + 3
        for c, ins, outs, sems in parts:
            c.start(ins, outs, sems)
        for c, ins, outs, sems in parts:
            if isinstance(c, _Gather):
                c.forward(ins, outs, sems)
        for c, ins, outs, sems in parts:
            c.finish(ins, outs, sems)

    res = pl.pallas_call(
        body, name=name, in_specs=[_ANY] * n, out_specs=[_ANY] * n,
        out_shape=[s for c in comms for s in c.out_shape()], scratch_shapes=[s for c in comms for s in c.scratch()],
    )(*[arr for c in comms for arr in c.arrays])
    out, a = [], 0
    for k in ns:
        out.append(res[a:a + k])
        a += k
    return out


def _pcall(body, args, *, name, grid, in_specs, out_specs, out_shape, scratch_shapes=(), sem="parallel", comm=None,
           step_axis=0):
    sem = (sem,) * len(grid) if isinstance(sem, str) else sem
    if comm is None:
        res = pl.pallas_call(body, name=name, grid=grid, in_specs=in_specs, out_specs=out_specs, out_shape=out_shape,
                             scratch_shapes=list(scratch_shapes), compiler_params=_params(*sem))(*args)
        return res, []
    n_in, n_out, n_scr, n = len(in_specs), len(out_specs), len(scratch_shapes), comm.n
    nsteps = grid[step_axis]
    assert all(g == 1 for ax, g in enumerate(grid) if ax != step_axis)

    def hosted(*refs):
        ins, cin = refs[:n_in], refs[n_in:n_in + n]
        o0 = n_in + n
        outs, cout = refs[o0:o0 + n_out], refs[o0 + n_out:o0 + n_out + n]
        s0 = o0 + n_out + n
        scr, sems = refs[s0:s0 + n_scr], refs[s0 + n_scr:]
        step = pl.program_id(step_axis)
        comm.before(cin, cout, sems, step, nsteps)
        body(*ins, *outs, *scr)
        comm.after(cin, cout, sems, step, nsteps)

    res = pl.pallas_call(
        hosted, name=name, grid=grid, in_specs=list(in_specs) + [_ANY] * n, out_specs=list(out_specs) + [_ANY] * n,
        out_shape=list(out_shape) + comm.out_shape(), scratch_shapes=list(scratch_shapes) + comm.scratch(),
        compiler_params=_params(*(("arbitrary",) * len(grid))))(*args, *comm.arrays)
    return res[:n_out], res[n_out:]


def _in_proj(x, w_in_t, comm=None):
    S = x.shape[0]
    tb = min(512, S)

    def body(x_ref, w_ref, q_ref, k_ref, v_ref, xr_ref, gr_ref):
        u = _dot_nt(x_ref[...].astype(BF16), w_ref[...])
        q_ref[...] = (u[:, :D_ATT] * (HEAD_DIM ** -0.5)).astype(BF16)
        k_ref[...] = u[:, D_ATT:D_ATT + D_KV].astype(BF16)
        v_ref[...] = u[:, D_ATT + D_KV:D_ATT + 2 * D_KV].astype(BF16)
        xr_ref[...] = u[:, D_ATT + 2 * D_KV:D_ATT + 2 * D_KV + D_RNN]
        gr_ref[...] = u[:, D_ATT + 2 * D_KV + D_RNN:]

    return _pcall(
        body, (x, w_in_t), name="in_proj", grid=(S // tb,), comm=comm,
        in_specs=[_rows(tb, D_MODEL), _resident((D_IN, D_MODEL))],
        out_specs=[_rows(tb, D_ATT), _rows(tb, D_KV), _rows(tb, D_KV), _rows(tb, D_RNN), _rows(tb, D_RNN)],
        out_shape=[jax.ShapeDtypeStruct((S, D_ATT), BF16), jax.ShapeDtypeStruct((S, D_KV), BF16),
                   jax.ShapeDtypeStruct((S, D_KV), BF16), jax.ShapeDtypeStruct((S, D_RNN), F32),
                   jax.ShapeDtypeStruct((S, D_RNN), F32)])


GROUP = N_HEADS // N_KV


def _band_mask(i):
    qi = lax.broadcasted_iota(jnp.int32, (GROUP * QBLK, 2 * QBLK), 0) & (QBLK - 1)
    sj = lax.broadcasted_iota(jnp.int32, (GROUP * QBLK, 2 * QBLK), 1)
    return (sj > qi) & (sj <= qi + QBLK) & ((sj >= QBLK) | (i > 0))


def _stack_heads(x, g):
    return jnp.concatenate([x[:, (g * GROUP + hh) * HEAD_DIM:(g * GROUP + hh + 1) * HEAD_DIM] for hh in range(GROUP)],
                           axis=0)


def _unstack_heads(x4):
    return [x4[hh * QBLK:(hh + 1) * QBLK] for hh in range(GROUP)]


def _sink_column(sink_ref, g):
    head = lax.broadcasted_iota(jnp.int32, (GROUP * QBLK, 1), 0) // QBLK
    col = jnp.full((GROUP * QBLK, 1), sink_ref[g * GROUP], F32)
    for hh in range(1, GROUP):
        col = jnp.where(head == hh, sink_ref[g * GROUP + hh], col)
    return col


ATT_STEP = 4
IN_GRAD_PARTS = 2


def _attn_specs(nq=1):
    cur = lambda i: (i, 0)
    prev = lambda i: (jnp.maximum(nq * i - 1, 0), 0)
    return [pl.BlockSpec((nq * QBLK, D_KV), cur), pl.BlockSpec((QBLK, D_KV), prev),
            pl.BlockSpec((nq * QBLK, D_KV), cur), pl.BlockSpec((QBLK, D_KV), prev)]


def _attn_fwd(q, k, v, sinks, comm=None):
    S = q.shape[0]
    nq = min(ATT_STEP, S // QBLK)

    def body(sink_ref, q_ref, kc_ref, kp_ref, vc_ref, vp_ref, o_ref, lse_ref):
        first = pl.program_id(0) * nq
        kall = jnp.concatenate([kp_ref[...], kc_ref[...]], axis=0)
        vall = jnp.concatenate([vp_ref[...], vc_ref[...]], axis=0)
        for b in range(nq):
            valid = _band_mask(first + b)
            rows = slice(b * QBLK, (b + 1) * QBLK)
            keys = slice(b * QBLK, (b + 2) * QBLK)
            qv = q_ref[rows, :]
            outs = []
            for g in range(N_KV):
                kcat = kall[keys, g * HEAD_DIM:(g + 1) * HEAD_DIM]
                vcat = vall[keys, g * HEAD_DIM:(g + 1) * HEAD_DIM]
                s = jnp.where(valid, _dot_nt(_stack_heads(qv, g), kcat), -1e30)
                sink = _sink_column(sink_ref, g)
                m = jnp.maximum(jnp.max(s, axis=1, keepdims=True), sink)
                p = jnp.exp(s - m)
                l = jnp.sum(p, axis=1, keepdims=True) + jnp.exp(sink - m)
                outs += _unstack_heads(_dot(p.astype(BF16), vcat) / l)
                lse_ref[(b * N_KV + g) * GROUP * QBLK:(b * N_KV + g + 1) * GROUP * QBLK, :] = m + jnp.log(l)
            o_ref[rows, :] = jnp.concatenate(outs, axis=1).astype(BF16)

    lse_rows = nq * N_HEADS * QBLK
    return _pcall(
        body, (sinks, q, k, k, v, v), name="attn_fwd", grid=(S // (nq * QBLK),), comm=comm,
        in_specs=[pl.BlockSpec(memory_space=pltpu.SMEM), _rows(nq * QBLK, D_ATT)] + _attn_specs(nq),
        out_specs=[_rows(nq * QBLK, D_ATT), _rows(lse_rows, 1)],
        out_shape=[jax.ShapeDtypeStruct((S, D_ATT), BF16), jax.ShapeDtypeStruct((S * N_HEADS, 1), F32)])


def _w_rows(w_ref):
    return [w_ref[k:k + 1, :] for k in range(w_ref.shape[0])]


def _conv4(x, halo, w, b):
    y = b + w[3] * x
    for s in (1, 2, 3):
        y = y + w[3 - s] * _shift_down(x, halo, s)
    return y


def _rnn_gates(xc, wa, wx, ba, bx, sp):
    xcb = xc.astype(BF16)
    r = _sigmoid(_dot(xcb, wa) + ba)
    ig = _sigmoid(_dot(xcb, wx) + bx)
    la = -LRU_C * r * sp
    a = jnp.exp(la)
    t = jnp.tanh(la)
    f = jnp.sqrt(-2.0 * t / (1.0 - t))
    return r, ig, a, f


def _rnn_fwd(xr, gr, conv_w, conv_b, wa, wx, ba, bx, lam, comm=None):
    S = xr.shape[0]
    tb = min(256, S)

    def body(xr_ref, gr_ref, cw_ref, cb_ref, wa_ref, wx_ref, ba_ref, bx_ref, lam_ref, rec_ref, h_ref,
             xc_ref, r_ref, ig_ref, a_ref, f_ref, halo_s, hc_s, a_s, b_s):
        @pl.when(pl.program_id(0) == 0)
        def _():
            halo_s[...] = jnp.zeros_like(halo_s)
            hc_s[...] = jnp.zeros_like(hc_s)

        x = xr_ref[...]
        xc = _conv4(x, halo_s[...], _w_rows(cw_ref), cb_ref[...])
        halo_s[...] = x[tb - 8:]
        r, ig, a, f = _rnn_gates(xc, wa_ref[...], wx_ref[...], ba_ref[...], bx_ref[...], _softplus_neg(lam_ref[...]))
        xc_ref[...] = xc
        r_ref[...] = r
        ig_ref[...] = ig
        a_ref[...] = a
        f_ref[...] = f
        a_s[...] = a
        b_s[...] = f * ig * xc
        row8 = lax.broadcasted_iota(jnp.int32, (8, D_RNN), 0)

        def tile(t, hc):
            o = pl.multiple_of(t * 8, 8)
            at = a_s[pl.ds(o, 8), :]
            bt = b_s[pl.ds(o, 8), :]
            for s in (1, 2, 4):
                keep = row8 >= s
                a_sh = jnp.where(keep, pltpu.roll(at, s, 0), 1.0)
                b_sh = jnp.where(keep, pltpu.roll(bt, s, 0), 0.0)
                bt = at * b_sh + bt
                at = at * a_sh
            ht = at * hc + bt
            b_s[pl.ds(o, 8), :] = ht
            return _row_sum(jnp.where(row8 == 7, ht, 0.0))

        hc_s[0:1, :] = lax.fori_loop(0, tb // 8, tile, hc_s[0:1, :], unroll=2)
        h = b_s[...]
        h_ref[...] = h
        rec_ref[...] = (h * _gelu(gr_ref[...])).astype(BF16)

    vec = _resident((1, D_RNN))
    kept = jax.ShapeDtypeStruct((S, D_RNN), F32)
    return _pcall(
        body, (xr, gr, conv_w, conv_b, wa, wx, ba, bx, lam), name="rnn_fwd", grid=(S // tb,), sem="arbitrary", comm=comm,
        in_specs=[_rows(tb, D_RNN), _rows(tb, D_RNN), _resident((4, D_RNN)), vec,
                  _resident((D_RNN, D_RNN)), _resident((D_RNN, D_RNN)), vec, vec, vec],
        out_specs=[_rows(tb, D_RNN)] * 7,
        out_shape=[jax.ShapeDtypeStruct((S, D_RNN), BF16), kept, kept, kept, kept, kept, kept],
        scratch_shapes=[pltpu.VMEM((8, D_RNN), F32), pltpu.VMEM((8, D_RNN), F32),
                        pltpu.VMEM((tb, D_RNN), F32), pltpu.VMEM((tb, D_RNN), F32)])


def _mix_ln1_up(x, att, rec, w_out, ln1_g, ln1_b, w_up, fcw, fcb, comm=None):
    S = x.shape[0]
    tb = min(256, S)
    nblk, _, wblk = w_up.shape
    half = nblk // 2

    def body(x_ref, att_ref, rec_ref, wo_ref, g_ref, b_ref, wu_ref, fcw_ref, fcb_ref,
             z1_ref, h1_ref, h1b_ref, gate_ref, act_ref, gl_ref, vdgl_ref, halo_s):
        @pl.when(pl.program_id(0) == 0)
        def _():
            halo_s[...] = jnp.zeros_like(halo_s)

        z1 = ALPHA * x_ref[...] + _dot(att_ref[...], wo_ref[:D_ATT, :]) + _dot(rec_ref[...], wo_ref[D_ATT:, :])
        z1_ref[...] = z1
        xhat, _ = _ln_stats(z1)
        h1 = xhat * g_ref[...] + b_ref[...]
        h1_ref[...] = h1
        h1b = h1.astype(BF16)
        h1b_ref[...] = h1b
        for jj in range(half):
            cols = slice(jj * wblk, (jj + 1) * wblk)
            gate = _dot(h1b, wu_ref[jj])
            val = _dot(h1b, wu_ref[jj + half])
            halo = halo_s[:, cols]
            conv = (fcb_ref[:, cols] + fcw_ref[2:3, cols] * gate + fcw_ref[1:2, cols] * _shift_down(gate, halo, 1)
                    + fcw_ref[0:1, cols] * _shift_down(gate, halo, 2))
            halo_s[:, cols] = gate[tb - 8:]
            gl, dgl = _gelu_and_grad(conv)
            gate_ref[:, cols] = gate.astype(BF16)
            act_ref[:, cols] = (gl * val).astype(BF16)
            gl_ref[:, cols] = gl.astype(BF16)
            vdgl_ref[:, cols] = (val * dgl).astype(BF16)

    vec = _resident((1, D_MODEL))
    wide = jax.ShapeDtypeStruct((S, D_FF), BF16)
    return _pcall(
        body, (x, att, rec, w_out, ln1_g, ln1_b, w_up, fcw, fcb), name="mix_ln1_up", grid=(S // tb,),
        sem="arbitrary", comm=comm,
        in_specs=[_rows(tb, D_MODEL), _rows(tb, D_ATT), _rows(tb, D_RNN), _resident((D_MODEL, D_MODEL)), vec, vec,
                  _resident(w_up.shape), _resident((3, D_FF)), _resident((1, D_FF))],
        out_specs=[_rows(tb, D_MODEL), _rows(tb, D_MODEL), _rows(tb, D_MODEL)] + [_rows(tb, D_FF)] * 4,
        out_shape=[jax.ShapeDtypeStruct((S, D_MODEL), F32), jax.ShapeDtypeStruct((S, D_MODEL), F32),
                   jax.ShapeDtypeStruct((S, D_MODEL), BF16), wide, wide, wide, wide],
        scratch_shapes=[pltpu.VMEM((8, D_FF), F32)])


def _tail(act, gl, vdgl, h1, h1b, p, tgt, w_down, w_pg, b_pg, w_pp, ln2_g, ln2_b):
    S = h1.shape[0]
    tb = min(256, S)

    def body(act_ref, gl_ref, vdgl_ref, h1_ref, h1b_ref, p_ref, t_ref, wd_ref, wpg_ref, bpg_ref, wpp_ref, g2_ref, b2_ref,
             dz2_ref, dpre_ref, dpp_ref, dgc_ref, dval_ref, dh1_ref, acc_ref):
        i = pl.program_id(0)

        @pl.when(i == 0)
        def _():
            acc_ref[...] = jnp.zeros_like(acc_ref)

        ffn = _dot(act_ref[...], wd_ref[...])
        h1 = h1_ref[...]
        sg = _sigmoid(_dot(h1b_ref[...], wpg_ref[...]) + bpg_ref[...])
        pp = _dot(p_ref[...].astype(BF16), wpp_ref[...])
        z2 = ALPHA * h1 + ffn + sg * pp
        xhat2, rstd2 = _ln_stats(z2)
        y = xhat2 * g2_ref[...] + b2_ref[...]
        err = y - t_ref[...]
        dy = err * (1.0 / D_MODEL)
        loss = 0.5 * jnp.sum(jnp.sum(err * err, axis=1, keepdims=True), axis=0, keepdims=True) * (1.0 / D_MODEL)
        dz2 = _ln_bwd(dy, xhat2, rstd2, g2_ref[...])
        dz2b = dz2.astype(BF16)
        dz2_ref[...] = dz2b
        dpre = dz2 * pp * sg * (1.0 - sg)
        dpreb = dpre.astype(BF16)
        dpre_ref[...] = dpreb
        dpp_ref[...] = (dz2 * sg).astype(BF16)
        dh1_ref[...] = ALPHA * dz2 + _dot_nt(dpreb, wpg_ref[...])
        dactb = _dot_nt(dz2b, wd_ref[...]).astype(BF16)
        dval_ref[...] = dactb * gl_ref[...]
        dgc_ref[...] = dactb * vdgl_ref[...]
        _put_rows(acc_ref, [_row_sum(dy * xhat2), _row_sum(dy), _row_sum(dpre),
                            jnp.broadcast_to(loss, (1, D_MODEL))])

    vec = _resident((1, D_MODEL))
    return pl.pallas_call(
        body, name="tail", grid=(S // tb,),
        in_specs=[_rows(tb, D_FF), _rows(tb, D_FF), _rows(tb, D_FF), _rows(tb, D_MODEL), _rows(tb, D_MODEL),
                  _rows(tb, PLE_DIM), _rows(tb, D_MODEL), _resident((D_FF, D_MODEL)), _resident((D_MODEL, D_MODEL)), vec,
                  _resident((PLE_DIM, D_MODEL)), vec, vec],
        out_specs=[_rows(tb, D_MODEL), _rows(tb, D_MODEL), _rows(tb, D_MODEL), _rows(tb, D_FF),
                   _rows(tb, D_FF), _rows(tb, D_MODEL), _acc((8, D_MODEL))],
        out_shape=[jax.ShapeDtypeStruct((S, D_MODEL), BF16),
                   jax.ShapeDtypeStruct((S, D_MODEL), BF16), jax.ShapeDtypeStruct((S, D_MODEL), BF16),
                   jax.ShapeDtypeStruct((S, D_FF), BF16), jax.ShapeDtypeStruct((S, D_FF), BF16),
                   jax.ShapeDtypeStruct((S, D_MODEL), F32), jax.ShapeDtypeStruct((8, D_MODEL), F32)],
        compiler_params=_params("arbitrary"),
    )(act, gl, vdgl, h1, h1b, p, tgt, w_down, w_pg, b_pg, w_pp, ln2_g, ln2_b)


def _weight_grad(a_list, b_list, name, layout, ts=512, comm=None, b_window=None, halves=False):
    S = a_list[0].shape[0]
    ms = [a.shape[1] for a in a_list]
    M, nb = sum(ms), len(b_list)
    win, Nb = b_window if b_window else (0, b_list[0].shape[1])
    ts = min(ts, S)
    nk = S // ts
    per_b = N_DEV // nb
    na = len(a_list)

    n_out = 2 if halves else 1
    assert layout == "cols" or not halves

    def body(*refs):
        a_refs, b_refs, o_refs, acc_ref = refs[:na], refs[na:na + nb], refs[na + nb:na + nb + n_out], refs[-1]
        o_ref = o_refs[0]
        j, k = pl.program_id(0), pl.program_id(1)

        @pl.when(k == 0)
        def _():
            acc_ref[...] = jnp.zeros_like(acc_ref)

        for jj in range(nb):
            @pl.when(j == jj)
            def _():
                b = b_refs[jj][...].astype(BF16)
                off = 0
                for a_ref, m in zip(a_refs, ms):
                    acc_ref[off:off + m, :] += _dot_tn(a_ref[...].astype(BF16), b)
                    off += m

        @pl.when(k == nk - 1)
        def _():
            for d in range(per_b):
                if layout == "rows":
                    o_ref[d] = acc_ref[d * (M // N_DEV):(d + 1) * (M // N_DEV), :].astype(BF16)
                elif layout == "cols" and halves:
                    for o_half, r0 in zip(o_refs, (0, M // 2)):
                        o_half[d] = acc_ref[r0:r0 + M // 2, d * (Nb // per_b):(d + 1) * (Nb // per_b)].astype(BF16)
                elif layout == "cols":
                    o_ref[d] = acc_ref[:, d * (Nb // per_b):(d + 1) * (Nb // per_b)].astype(BF16)
                else:
                    o_ref[d] = acc_ref[:, d * (Nb // per_b):(d + 1) * (Nb // per_b)].T.astype(BF16)

    def b_index(jj):
        return lambda j, k: (jnp.where(j == jj, k, jnp.where(j < jj, 0, nk - 1)), win)

    if layout == "rows":
        assert nb == 1
        blk = (N_DEV, M // N_DEV, Nb)
    elif layout == "cols":
        blk = (per_b, M // n_out, Nb // per_b)
    else:
        blk = (per_b, Nb // per_b, M)
    res, comm_res = _pcall(
        body, (*a_list, *b_list), name=name, grid=(nb, nk), sem="arbitrary", comm=comm, step_axis=1,
        in_specs=[pl.BlockSpec((ts, m), lambda j, k: (k, 0)) for m in ms]
        + [pl.BlockSpec((ts, Nb), b_index(jj)) for jj in range(nb)],
        out_specs=[pl.BlockSpec(blk, lambda j, k: (j, 0, 0))] * n_out,
        out_shape=[jax.ShapeDtypeStruct((N_DEV,) + blk[1:], BF16)] * n_out,
        scratch_shapes=[pltpu.VMEM((M, Nb), F32)])
    res = res if halves else res[0]
    return (res, comm_res) if comm is not None else res


def _up_bwd(dgc, gate, dval, dh1p, z1, w_up, fcw, w_out, ln1_g, comm=None):
    S = z1.shape[0]
    tb = min(256, S)
    t16 = tb // 16
    n16 = S // 16
    nblk, _, wblk = w_up.shape
    half = nblk // 2
    nsteps = S // tb

    def body(dgc_ref, dgn_ref, gc_ref, dval_ref, dh1p_ref, z1_ref, wu_ref, fcw_ref, wo_ref, g1_ref,
             dgate_ref, dz1_ref, dz1b_ref, datt_ref, drec_ref, accf_ref, accd_ref):
        i = pl.program_id(0)

        @pl.when(i == 0)
        def _():
            accf_ref[...] = jnp.zeros_like(accf_ref)
            accd_ref[...] = jnp.zeros_like(accd_ref)

        dg = dgc_ref[...].astype(F32)
        nxt = jnp.where(i < nsteps - 1, dgn_ref[...].astype(F32)[0:8], 0.0)
        w = _w_rows(fcw_ref)
        up1, up2 = _shift_up(dg, nxt, 1), _shift_up(dg, nxt, 2)
        dgate = (w[2] * dg + w[1] * up1 + w[0] * up2).astype(BF16)
        dgate_ref[...] = dgate
        gate = gc_ref[...].astype(F32)
        _put_rows(accf_ref, [_row_sum(up2 * gate), _row_sum(up1 * gate), _row_sum(dg * gate), _row_sum(dg)])

        dh1 = dh1p_ref[...]
        for j in range(nblk):
            src = dgate if j < half else dval_ref[...]
            jj = j % half
            dh1 = dh1 + _dot_nt(src[:, jj * wblk:(jj + 1) * wblk], wu_ref[j])
        xhat1, rstd1 = _ln_stats(z1_ref[...])
        dz1 = _ln_bwd(dh1, xhat1, rstd1, g1_ref[...])
        dz1_ref[...] = dz1
        dz1b = dz1.astype(BF16)
        dz1b_ref[...] = dz1b
        dcat = _dot_nt(dz1b, wo_ref[...])
        datt_ref[...] = dcat[:, :D_ATT].astype(BF16)
        drec_ref[...] = dcat[:, D_ATT:]
        _put_rows(accd_ref, [_row_sum(dh1 * xhat1), _row_sum(dh1)])

    next16 = pl.BlockSpec((16, D_FF), lambda i: (jnp.minimum((i + 1) * t16, n16 - 1), 0))
    return _pcall(
        body, (dgc, dgc, gate, dval, dh1p, z1, w_up, fcw, w_out, ln1_g), name="up_bwd",
        grid=(nsteps,), sem="arbitrary", comm=comm,
        in_specs=[_rows(tb, D_FF), next16, _rows(tb, D_FF), _rows(tb, D_FF), _rows(tb, D_MODEL),
                  _rows(tb, D_MODEL), _resident(w_up.shape), _resident((3, D_FF)),
                  _resident((D_MODEL, D_MODEL)), _resident((1, D_MODEL))],
        out_specs=[_rows(tb, D_FF), _rows(tb, D_MODEL), _rows(tb, D_MODEL), _rows(tb, D_ATT), _rows(tb, D_RNN),
                   _acc((8, D_FF)), _acc((8, D_MODEL))],
        out_shape=[jax.ShapeDtypeStruct((S, D_FF), BF16), jax.ShapeDtypeStruct((S, D_MODEL), F32),
                   jax.ShapeDtypeStruct((S, D_MODEL), BF16), jax.ShapeDtypeStruct((S, D_ATT), BF16),
                   jax.ShapeDtypeStruct((S, D_RNN), F32), jax.ShapeDtypeStruct((8, D_FF), F32),
                   jax.ShapeDtypeStruct((8, D_MODEL), F32)])


def _attn_bwd(q, k, v, lse, do, sinks, comm=None):
    S = q.shape[0]
    grp = N_HEADS // N_KV
    nq = min(ATT_STEP, S // QBLK)

    def body(sink_ref, q_ref, kc_ref, kp_ref, vc_ref, vp_ref, do_ref, lse_ref, dq_ref, dkc_ref, dkp_ref, dvc_ref, dvp_ref,
             ds_ref):
        i = pl.program_id(0)

        @pl.when(i == 0)
        def _():
            ds_ref[...] = jnp.zeros_like(ds_ref)

        row8 = lax.broadcasted_iota(jnp.int32, (8, 128), 0)
        lane8 = lax.broadcasted_iota(jnp.int32, (8, 128), 1)
        dsink = jnp.zeros((8, 128), F32)
        kall = jnp.concatenate([kp_ref[...], kc_ref[...]], axis=0)
        vall = jnp.concatenate([vp_ref[...], vc_ref[...]], axis=0)
        dk_t = [jnp.zeros((D_KV, QBLK), F32) for _ in range(nq + 1)]
        dv_t = [jnp.zeros((D_KV, QBLK), F32) for _ in range(nq + 1)]
        for b in range(nq):
            valid = _band_mask(i * nq + b)
            rows = slice(b * QBLK, (b + 1) * QBLK)
            keys = slice(b * QBLK, (b + 2) * QBLK)
            qv, dov = q_ref[rows, :], do_ref[rows, :]
            dqs, dks, dvs = [], [], []
            for g in range(N_KV):
                kcat = kall[keys, g * HEAD_DIM:(g + 1) * HEAD_DIM]
                vcat = vall[keys, g * HEAD_DIM:(g + 1) * HEAD_DIM]
                q4, do4 = _stack_heads(qv, g), _stack_heads(dov, g)
                s = jnp.where(valid, _dot_nt(q4, kcat), -1e30)
                lse = lse_ref[(b * N_KV + g) * GROUP * QBLK:(b * N_KV + g + 1) * GROUP * QBLK, :]
                p = jnp.exp(s - lse)
                p_sink = jnp.exp(_sink_column(sink_ref, g) - lse)
                dp = _dot_nt(do4, vcat)
                delta = jnp.sum(p * dp, axis=1, keepdims=True)
                dsc = (p * (dp - delta)).astype(BF16)
                dqs += _unstack_heads(_dot(dsc, kcat) * (HEAD_DIM ** -0.5))
                dks.append(_dot_tn(q4, dsc))
                dvs.append(_dot_tn(do4, p.astype(BF16)))
                for hh, part in enumerate(_unstack_heads(-p_sink * delta)):
                    here = (row8 == 0) & (lane8 == g * grp + hh)
                    dsink = dsink + jnp.where(here, jnp.sum(part, axis=0, keepdims=True), 0.0)
            dq_ref[rows, :] = jnp.concatenate(dqs, axis=1).astype(BF16)
            dk2, dv2 = jnp.concatenate(dks, axis=0), jnp.concatenate(dvs, axis=0)
            dk_t[b], dk_t[b + 1] = dk_t[b] + dk2[:, :QBLK], dk_t[b + 1] + dk2[:, QBLK:]
            dv_t[b], dv_t[b + 1] = dv_t[b] + dv2[:, :QBLK], dv_t[b + 1] + dv2[:, QBLK:]
        dkp_ref[...] = dk_t[0].T
        dvp_ref[...] = dv_t[0].T
        for b in range(nq):
            dkc_ref[b * QBLK:(b + 1) * QBLK, :] = dk_t[b + 1].T
            dvc_ref[b * QBLK:(b + 1) * QBLK, :] = dv_t[b + 1].T
        ds_ref[...] += dsink

    nsteps = S // (nq * QBLK)
    cur = jax.ShapeDtypeStruct((S, D_KV), F32)
    prev = jax.ShapeDtypeStruct((nsteps * QBLK, D_KV), F32)
    big = _rows(nq * QBLK, D_ATT)
    return _pcall(
        body, (sinks, q, k, k, v, v, do, lse), name="attn_bwd", grid=(nsteps,), sem="arbitrary", comm=comm,
        in_specs=[pl.BlockSpec(memory_space=pltpu.SMEM), big] + _attn_specs(nq) + [big, _rows(nq * N_HEADS * QBLK, 1)],
        out_specs=[big, _rows(nq * QBLK, D_KV), _rows(QBLK, D_KV), _rows(nq * QBLK, D_KV), _rows(QBLK, D_KV),
                   _acc((8, 128))],
        out_shape=[jax.ShapeDtypeStruct((S, D_ATT), BF16), cur, prev, cur, prev, jax.ShapeDtypeStruct((8, 128), F32)])


def _rnn_bwd(xr, gr, h, kept, drec, conv_w, wa, wx, lam, comm=None):
    S = xr.shape[0]
    tb = min(256, S)
    t8 = tb // 8
    nsteps = S // tb

    def body(xr_ref, xp_ref, gr_ref, h_ref, hp_ref, xc_ref, r_ref, ig_ref, a_ref, f_ref, drec_ref, cw_ref, wa_ref, wx_ref,
             lam_ref, dxr_ref, dgr_ref, gwa_ref, gwx_ref, acc_ref, carry_s, dxc_halo_s, d_s, gwa_s, gwx_s):
        i = pl.program_id(0)
        blk = nsteps - 1 - i

        @pl.when(i == 0)
        def _():
            gwa_s[...] = jnp.zeros_like(gwa_s)
            gwx_s[...] = jnp.zeros_like(gwx_s)
            acc_ref[...] = jnp.zeros_like(acc_ref)
            carry_s[...] = jnp.zeros_like(carry_s)
            dxc_halo_s[...] = jnp.zeros_like(dxc_halo_s)

        x = xr_ref[...]
        xhalo = jnp.where(blk > 0, xp_ref[...], 0.0)
        cw = _w_rows(cw_ref)
        xs = [_shift_down(x, xhalo, 3), _shift_down(x, xhalo, 2), _shift_down(x, xhalo, 1), x]
        xc, r, ig, a, f = xc_ref[...], r_ref[...], ig_ref[...], a_ref[...], f_ref[...]
        sp = _softplus_neg(lam_ref[...])
        hcur = h_ref[...]
        hprev = _shift_down(hcur, jnp.where(blk > 0, hp_ref[...], 0.0), 1)
        gl, dgl = _gelu_and_grad(gr_ref[...])
        drec = drec_ref[...]
        dgr_ref[...] = (drec * hcur * dgl).astype(BF16)
        d_s[...] = drec * gl
        row8 = lax.broadcasted_iota(jnp.int32, (8, D_RNN), 0)

        def tile(t, c):
            o = pl.multiple_of((t8 - 1 - t) * 8, 8)
            a8 = a_ref[pl.ds(o, 8), :]
            dt = d_s[pl.ds(o, 8), :]
            at = jnp.where(row8 == 7, 1.0, pltpu.roll(a8, 7, 0))
            for s in (1, 2, 4):
                keep = row8 < 8 - s
                a_sh = jnp.where(keep, pltpu.roll(at, 8 - s, 0), 1.0)
                d_sh = jnp.where(keep, pltpu.roll(dt, 8 - s, 0), 0.0)
                dt = at * d_sh + dt
                at = at * a_sh
            lt = at * c + dt
            d_s[pl.ds(o, 8), :] = lt
            return _row_sum(jnp.where(row8 == 0, a8 * lt, 0.0))

        carry_s[0:1, :] = lax.fori_loop(0, t8, tile, carry_s[0:1, :], unroll=2)
        lmb = d_s[...]
        a2 = a * a
        dla = lmb * hprev * a - lmb * ig * xc * (a2 / f)
        di = lmb * f * xc
        dr = dla * (-LRU_C) * sp
        dpa = dr * r * (1.0 - r)
        dpx = di * ig * (1.0 - ig)
        dpab = dpa.astype(BF16)
        dpxb = dpx.astype(BF16)
        xcb = xc.astype(BF16)
        gwa_s[...] += _dot_tn(xcb, dpab)
        gwx_s[...] += _dot_tn(xcb, dpxb)

        @pl.when(i == nsteps - 1)
        def _():
            for dense, out in ((gwa_s[...], gwa_ref), (gwx_s[...], gwx_ref)):
                for b in range(RNN_BLOCKS):
                    rows = slice(b * HEAD_DIM, (b + 1) * HEAD_DIM)
                    out[rows, :] = dense[rows, b * HEAD_DIM:(b + 1) * HEAD_DIM]

        dxc = lmb * f * ig + _dot_nt(dpab, wa_ref[...]) + _dot_nt(dpxb, wx_ref[...])
        nxt = dxc_halo_s[...]
        dxr = cw[3] * dxc
        for s in (1, 2, 3):
            dxr = dxr + cw[3 - s] * _shift_up(dxc, nxt, s)
        dxr_ref[...] = dxr.astype(BF16)
        dxc_halo_s[...] = dxc[:8]
        dlam = _row_sum(dla * (-LRU_C) * r) * (-1.0 / (1.0 + jnp.exp(lam_ref[...])))
        _put_rows(acc_ref, [_row_sum(dxc * xs[0]), _row_sum(dxc * xs[1]), _row_sum(dxc * xs[2]), _row_sum(dxc * xs[3]),
                            _row_sum(dxc), _row_sum(dpa), _row_sum(dpx), dlam])

    rev = lambda i: (nsteps - 1 - i, 0)
    prev8 = lambda i: (jnp.maximum((nsteps - 1 - i) * t8 - 1, 0), 0)
    blkspec = pl.BlockSpec((tb, D_RNN), rev)
    halo8 = pl.BlockSpec((8, D_RNN), prev8)
    vec = _resident((1, D_RNN))
    return _pcall(
        body, (xr, xr, gr, h, h, *kept, drec, conv_w, wa, wx, lam), name="rnn_bwd", grid=(nsteps,),
        sem="arbitrary", comm=comm,
        in_specs=[blkspec, halo8, blkspec, blkspec, halo8] + [blkspec] * 6
        + [_resident((4, D_RNN)), _resident((D_RNN, D_RNN)), _resident((D_RNN, D_RNN)), vec],
        out_specs=[blkspec, blkspec, _acc((D_RNN, HEAD_DIM)), _acc((D_RNN, HEAD_DIM)), _acc((8, D_RNN))],
        out_shape=[jax.ShapeDtypeStruct((S, D_RNN), BF16), jax.ShapeDtypeStruct((S, D_RNN), BF16),
                   jax.ShapeDtypeStruct((D_RNN, HEAD_DIM), F32), jax.ShapeDtypeStruct((D_RNN, HEAD_DIM), F32),
                   jax.ShapeDtypeStruct((8, D_RNN), F32)],
        scratch_shapes=[pltpu.VMEM((8, D_RNN), F32), pltpu.VMEM((8, D_RNN), F32), pltpu.VMEM((tb, D_RNN), F32),
                        pltpu.VMEM((D_RNN, D_RNN), F32), pltpu.VMEM((D_RNN, D_RNN), F32)])


def _in_bwd(dq, dkc, dkp, dvc, dvp, dxr, dgr, dz1, w_in, comm=None):
    S = dz1.shape[0]
    tb = min(ATT_STEP * QBLK, S)
    nsteps = S // tb

    def body(dq_ref, dkc_ref, dkn_ref, dvc_ref, dvn_ref, dxr_ref, dgr_ref, dz1_ref, w_ref, dkv_ref, dx_ref):
        last = pl.program_id(0) == nsteps - 1

        def total(cur_ref, next_ref):
            nxt = jnp.where(last, 0.0, next_ref[...])
            tail = cur_ref[tb - QBLK:, :] + nxt
            return jnp.concatenate([cur_ref[:tb - QBLK, :], tail], axis=0) if tb > QBLK else tail

        dkv = jnp.concatenate([total(dkc_ref, dkn_ref), total(dvc_ref, dvn_ref)], axis=1).astype(BF16)
        dkv_ref[...] = dkv
        du = jnp.concatenate([dq_ref[...], dkv, dxr_ref[...], dgr_ref[...]], axis=1)
        dx_ref[...] = ALPHA * dz1_ref[...] + _dot(du, w_ref[...])

    nextp = pl.BlockSpec((QBLK, D_KV), lambda i: (jnp.minimum(i + 1, nsteps - 1), 0))
    return _pcall(
        body, (dq, dkc, dkp, dvc, dvp, dxr, dgr, dz1, w_in), name="in_bwd", grid=(nsteps,), comm=comm,
        in_specs=[_rows(tb, D_ATT), _rows(tb, D_KV), nextp, _rows(tb, D_KV), nextp,
                  _rows(tb, D_RNN), _rows(tb, D_RNN), _rows(tb, D_MODEL), _resident((D_IN, D_MODEL))],
        out_specs=[_rows(tb, 2 * D_KV), _rows(tb, D_MODEL)],
        out_shape=[jax.ShapeDtypeStruct((S, 2 * D_KV), BF16), jax.ShapeDtypeStruct((S, D_MODEL), F32)])


def _block_diag(w):
    eye = jnp.eye(RNN_BLOCKS, dtype=w.dtype)
    return (w[:, :, None, :] * eye[:, None, :, None]).reshape(D_RNN, D_RNN).astype(BF16)


def _adamw(w, g, m, v):
    m = ADAM_B1 * m + (1.0 - ADAM_B1) * g
    v = ADAM_B2 * v + (1.0 - ADAM_B2) * (g * g)
    m_hat = m / (1.0 - ADAM_B1 ** ADAM_STEP)
    v_hat = v / (1.0 - ADAM_B2 ** ADAM_STEP)
    delta = -ADAM_LR * (m_hat / (jnp.sqrt(v_hat) + ADAM_EPS) + ADAM_WD * w)
    return delta, m, v


def _sum_adamw(parts, w, m, v, name):
    parts = parts if isinstance(parts, (list, tuple)) else [parts]
    R, C = w.shape
    rb = R if R <= 256 else (256 if parts[0].shape[1] % 256 == 0 else 128)
    per = parts[0].shape[1] // rb
    assert R % rb == 0 and parts[0].shape[1] % rb == 0
    n = len(parts)

    def body(*refs):
        p_refs = refs[:n]
        w_ref, m_ref, v_ref, g_out, d_out, m_out, v_out = refs[n:]
        which = pl.program_id(0) // per

        def total(p_ref):
            g = p_ref[0].astype(F32)
            for d in range(1, N_DEV):
                g = g + p_ref[d].astype(F32)
            return g

        g = total(p_refs[0])
        for j in range(1, n):
            g = jnp.where(which == j, total(p_refs[j]), g)
        delta, mn, vn = _adamw(w_ref[...], g, m_ref[...], v_ref[...])
        g_out[...] = g
        d_out[...] = delta
        m_out[...] = mn
        v_out[...] = vn

    def part_spec(j):
        return pl.BlockSpec((N_DEV, rb, C), lambda i: (0, jnp.clip(i - j * per, 0, per - 1), 0))

    blk = _rows(rb, C)
    out = jax.ShapeDtypeStruct((R, C), F32)
    return pl.pallas_call(
        body, name=name, grid=(R // rb,),
        in_specs=[part_spec(j) for j in range(n)] + [blk, blk, blk],
        out_specs=[blk, blk, blk, blk], out_shape=[out, out, out, out],
        compiler_params=_params("parallel"),
    )(*parts, w, m, v)


_SMALL = [("attn_sinks", "s", 0, 1, None), ("rnn_conv_w", "r", 0, 4, "cols"), ("rnn_conv_b", "r", 4, 1, None),
          ("gate_a_w", "a", 0, D_RNN, None), ("gate_a_b", "r", 5, 1, None), ("gate_x_w", "x", 0, D_RNN, None),
          ("gate_x_b", "r", 6, 1, None), ("lru_lambda", "r", 7, 1, None), ("ln1_g", "d", 0, 1, None),
          ("ln1_b", "d", 1, 1, None), ("ffn_conv_w", "f", 0, 3, "cols"), ("ffn_conv_b", "f", 3, 1, None),
          ("ple_gate_b", "t", 2, 1, None), ("ln2_g", "t", 0, 1, None), ("ln2_b", "t", 1, 1, None)]
_LOSS_ROW = 3


_ACC_COLS = {"t": (0, D_MODEL), "f": (D_MODEL, D_FF), "d": (D_MODEL + D_FF, D_MODEL), "s": (2 * D_MODEL + D_FF, 128),
             "r": (2 * D_MODEL + D_FF + 128, D_RNN)}
_ACC_WIDTH = 2 * D_MODEL + D_FF + 128 + D_RNN


def _small_update(rows_all, gates_all, params):
    flat = [arr for triple in params for arr in triple]
    n_par = len(_SMALL)

    def body(*refs):
        rows_ref, gates_ref = refs[:2]
        p_refs = refs[2:2 + 3 * n_par]
        loss_ref = refs[2 + 3 * n_par]
        o_refs = refs[3 + 3 * n_par:3 + 7 * n_par]
        rows_s, tmp_r, tmp_f = refs[3 + 7 * n_par:]
        me = _dev_index(*_place())
        rows_sum, gates_sum = rows_ref[0], gates_ref[0]
        for d in range(1, N_DEV):
            rows_sum = rows_sum + rows_ref[d]
            gates_sum = gates_sum + gates_ref[d]
        rows_s[...] = rows_sum
        t0 = _ACC_COLS["t"][0]
        loss_ref[...] = rows_s[_LOSS_ROW:_LOSS_ROW + 1, t0:t0 + 128]
        for i, (name, key, row, rows, how) in enumerate(_SMALL):
            w_ref, m_ref, v_ref = p_refs[3 * i:3 * i + 3]
            g_out, d_out, m_out, v_out = o_refs[4 * i:4 * i + 4]
            if key == "a":
                g = gates_sum[:, :HEAD_DIM]
            elif key == "x":
                g = gates_sum[:, HEAD_DIM:]
            elif how == "cols":
                c0, width = _ACC_COLS[key]
                full = rows_s[:, c0:c0 + width]
                shard = width // N_DEV
                mine = full[:, :shard]
                for d in range(1, N_DEV):
                    mine = jnp.where(me == d, full[:, d * shard:(d + 1) * shard], mine)
                tmp = tmp_r if key == "r" else tmp_f
                tmp[...] = mine
                g = tmp[row:row + rows, :]
            else:
                c0, width = _ACC_COLS[key]
                g = rows_s[row:row + rows, c0:c0 + width][:, :w_ref.shape[1]]
            delta, mn, vn = _adamw(w_ref[...], g, m_ref[...], v_ref[...])
            g_out[...] = g
            d_out[...] = delta
            m_out[...] = mn
            v_out[...] = vn

    outs = [jax.ShapeDtypeStruct((1, 128), F32)]
    for w, _, _ in params:
        outs += [jax.ShapeDtypeStruct(w.shape, F32)] * 4
    scratch = [pltpu.VMEM((8, _ACC_WIDTH), F32), pltpu.VMEM((8, D_RNN // N_DEV), F32), pltpu.VMEM((8, D_FF // N_DEV), F32)]
    res = pl.pallas_call(body, name="small_update", out_shape=outs, scratch_shapes=scratch)(rows_all, gates_all, *flat)
    return res[0], [res[1 + 4 * i:5 + 4 * i] for i in range(n_par)]


def kernel(x, p, w_in, attn_sinks, rnn_conv_w, rnn_conv_b, gate_a_w, gate_a_b, gate_x_w, gate_x_b, lru_lambda, w_out, ln1_g, ln1_b, w_ffn_up, ffn_conv_w, ffn_conv_b, w_ffn_down, ple_gate_w, ple_gate_b, ple_proj, ln2_g, ln2_b, loss_target, m_w_in, m_attn_sinks, m_rnn_conv_w, m_rnn_conv_b, m_gate_a_w, m_gate_a_b, m_gate_x_w, m_gate_x_b, m_lru_lambda, m_w_out, m_ln1_g, m_ln1_b, m_w_ffn_up, m_ffn_conv_w, m_ffn_conv_b, m_w_ffn_down, m_ple_gate_w, m_ple_gate_b, m_ple_proj, m_ln2_g, m_ln2_b, v_w_in, v_attn_sinks, v_rnn_conv_w, v_rnn_conv_b, v_gate_a_w, v_gate_a_b, v_gate_x_w, v_gate_x_b, v_lru_lambda, v_w_out, v_ln1_g, v_ln1_b, v_w_ffn_up, v_ffn_conv_w, v_ffn_conv_b, v_w_ffn_down, v_ple_gate_w, v_ple_gate_b, v_ple_proj, v_ln2_g, v_ln2_b):
    from_col_blocks = lambda g: g.transpose(1, 0, 2).reshape(g.shape[1], N_DEV * g.shape[2])

    xs, ps, tgt, sinks = x[0], p[0, 0], loss_target[0], attn_sinks[0]
    wa, wx = _block_diag(gate_a_w[0]), _block_diag(gate_x_w[0])

    conv_cols = jnp.concatenate([rnn_conv_w[0].reshape(1, -1), ffn_conv_w[0].reshape(1, -1)], axis=1)
    n_rc, n_fc = 4 * D_RNN // N_DEV, 3 * D_FF // N_DEV
    ((g_in,),) = _comm_call([_Gather([w_in[0].T.astype(BF16)])], "gather_w_in")
    w_in_full = g_in.reshape(D_IN, D_MODEL)

    (q, k, v, xr, gr), _ = _in_proj(xs, w_in_full)
    (att, lse), (g_out, g_conv) = _attn_fwd(
        q, k, v, sinks,
        comm=_Multi([_Gather([w_out[0].astype(BF16)]), _Bcast([jnp.broadcast_to(conv_cols, (8, n_rc + n_fc))])]))
    rcw = from_col_blocks(g_conv[:, 0, :n_rc].reshape(N_DEV, 4, D_RNN // N_DEV))
    fcw = from_col_blocks(g_conv[:, 0, n_rc:].reshape(N_DEV, 3, D_FF // N_DEV))
    (rec, h, *kept), (w_up,) = _rnn_fwd(xr, gr, rcw, rnn_conv_b, wa, wx, gate_a_b, gate_x_b, lru_lambda,
                                        comm=_Gather([w_ffn_up[0].astype(BF16)]))
    w_out_full = g_out.reshape(D_MODEL, D_MODEL)
    (z1, h1, h1b, gate, act, gl, vdgl), (g_down, g_pg, g_pp) = _mix_ln1_up(
        xs, att, rec, w_out_full, ln1_g, ln1_b, w_up, fcw, ffn_conv_b,
        comm=_Gather([w_ffn_down[0].astype(BF16), ple_gate_w[0].astype(BF16), ple_proj[0].astype(BF16)]))
    dz2b, dpreb, dppb, dgc, dval, dh1p, acc_t = _tail(
        act, gl, vdgl, h1, h1b, ps, tgt, g_down.reshape(D_FF, D_MODEL), g_pg.reshape(D_MODEL, D_MODEL), ple_gate_b,
        from_col_blocks(g_pp), ln2_g, ln2_b)

    gd_down = _weight_grad([dz2b], [act], "down_grad", "rows_t", ts=1024)
    gd_pg = _weight_grad([h1b], [dpreb], "pg_grad", "rows", ts=1024)
    gd_pp = _weight_grad([ps], [dppb], "pp_grad", "cols", ts=1024)
    (dgate, dz1, dz1b, datt, drec, acc_f, acc_d), (r_down, r_pg, r_pp) = _up_bwd(
        dgc, gate, dval, dh1p, z1, w_up, fcw, w_out_full, ln1_g, comm=_Exchange([gd_down, gd_pg, gd_pp]))
    gd_up_top, gd_up_bot = _weight_grad([h1b], [dgate, dval], "up_grad", "cols", halves=True)
    gd_out = _weight_grad([att, rec], [dz1b], "out_grad", "rows", ts=1024)
    (dq, dkc, dkp, dvc, dvp, acc_s), (r_up_top,) = _attn_bwd(q, k, v, lse, datt, sinks, comm=_Exchange([gd_up_top]))
    early = jnp.concatenate([acc_t, acc_f, acc_d], axis=1)
    (dxr, dgr, g_wa, g_wx, acc_r), (r_up_bot, r_out, early_all) = _rnn_bwd(
        xr, gr, h, kept, drec, rcw, wa, wx, lru_lambda, comm=_Multi([_Exchange([gd_up_bot, gd_out]), _Bcast([early])]))
    (dkv, dx), _ = _in_bwd(dq, dkc, dkp, dvc, dvp, dxr, dgr, dz1, w_in_full)
    du_parts = [dq, dkv, dxr, dgr]
    lanes = D_RNN // 128
    late = jnp.concatenate([g_wa, g_wx], axis=1)
    late = jnp.concatenate([late, acc_s, acc_r.reshape(8, lanes, 128).transpose(1, 0, 2).reshape(8 * lanes, 128)], axis=0)
    width = D_MODEL // IN_GRAD_PARTS
    comm, r_parts = _Bcast([late]), []
    for part in range(IN_GRAD_PARTS):
        gd_part, got = _weight_grad(du_parts, [xs], f"in_grad_{part}", "rows", ts=1024, b_window=(part, width), comm=comm)
        if part == 0:
            (late_all,) = got
        else:
            r_parts += got
        comm = _Exchange([gd_part])
    r_parts += _comm_call([comm], "exchange_w_in")[0]
    r_in = jnp.concatenate(r_parts, axis=2)
    acc_r_all = late_all[:, D_RNN + 8:].reshape(N_DEV, lanes, 8, 128).transpose(0, 2, 1, 3).reshape(N_DEV, 8, D_RNN)
    small_parts = (jnp.concatenate([early_all, late_all[:, D_RNN:D_RNN + 8], acc_r_all], axis=2),
                   late_all[:, :D_RNN])

    outs = {}
    res = _sum_adamw(r_in, w_in[0].T, m_w_in[0].T, v_w_in[0].T, "adamw_w_in")
    outs["w_in"] = [r.T[None] for r in res]
    for name, parts, w, m, v in [("w_out", r_out, w_out, m_w_out, v_w_out),
                                 ("w_ffn_up", [r_up_top, r_up_bot], w_ffn_up, m_w_ffn_up, v_w_ffn_up),
                                 ("w_ffn_down", r_down, w_ffn_down, m_w_ffn_down, v_w_ffn_down),
                                 ("ple_gate_w", r_pg, ple_gate_w, m_ple_gate_w, v_ple_gate_w),
                                 ("ple_proj", r_pp, ple_proj, m_ple_proj, v_ple_proj)]:
        res = _sum_adamw(parts, w[0], m[0], v[0], "adamw_" + name)
        outs[name] = [r[None] for r in res]

    given = dict(attn_sinks=(attn_sinks, m_attn_sinks, v_attn_sinks), rnn_conv_w=(rnn_conv_w, m_rnn_conv_w, v_rnn_conv_w),
                 rnn_conv_b=(rnn_conv_b, m_rnn_conv_b, v_rnn_conv_b), gate_a_w=(gate_a_w, m_gate_a_w, v_gate_a_w),
                 gate_a_b=(gate_a_b, m_gate_a_b, v_gate_a_b), gate_x_w=(gate_x_w, m_gate_x_w, v_gate_x_w),
                 gate_x_b=(gate_x_b, m_gate_x_b, v_gate_x_b), lru_lambda=(lru_lambda, m_lru_lambda, v_lru_lambda),
                 ln1_g=(ln1_g, m_ln1_g, v_ln1_g), ln1_b=(ln1_b, m_ln1_b, v_ln1_b),
                 ffn_conv_w=(ffn_conv_w, m_ffn_conv_w, v_ffn_conv_w), ffn_conv_b=(ffn_conv_b, m_ffn_conv_b, v_ffn_conv_b),
                 ple_gate_b=(ple_gate_b, m_ple_gate_b, v_ple_gate_b), ln2_g=(ln2_g, m_ln2_g, v_ln2_g),
                 ln2_b=(ln2_b, m_ln2_b, v_ln2_b))
    as_2d = lambda a: a.reshape(-1, a.shape[-1])
    loss_row, small_res = _small_update(*small_parts, [tuple(as_2d(a) for a in given[n]) for n, *_ in _SMALL])
    loss = loss_row[0, 0]
    for (n, *_), res in zip(_SMALL, small_res):
        outs[n] = [r.reshape(given[n][0].shape) for r in res]

    order = ["w_in", "attn_sinks", "rnn_conv_w", "rnn_conv_b", "gate_a_w", "gate_a_b", "gate_x_w", "gate_x_b",
             "lru_lambda", "w_out", "ln1_g", "ln1_b", "w_ffn_up", "ffn_conv_w", "ffn_conv_b", "w_ffn_down",
             "ple_gate_w", "ple_gate_b", "ple_proj", "ln2_g", "ln2_b"]
    return (loss, dx[None], *[outs[n][0] for n in order], *[outs[n][1] for n in order],
            *[outs[n][2] for n in order], *[outs[n][3] for n in order])
```

```python
import jax
import jax.numpy as jnp
from jax import lax
from jax.experimental import pallas as pl
from jax.experimental.pallas import tpu as pltpu

F32 = jnp.float32
BF16 = jnp.bfloat16

D_MODEL = 1024
D_ATT = 512
D_KV = 128
HEAD_DIM = 64
N_HEADS = 8
N_KV = 2
D_RNN = 512
RNN_BLOCKS = 8
D_IN = 1792
D_FF = 3072
PLE_DIM = 256
QBLK = 128
N_DEV = 8
ALPHA = float(2 ** 0.25)
LN_EPS = 1e-5
LRU_C = 8.0
ADAM_LR, ADAM_B1, ADAM_B2, ADAM_EPS, ADAM_WD, ADAM_STEP = 0.001, 0.9, 0.999, 1e-08, 0.01, 10

V7X_VMEM_LIMIT = 56 * 1024 * 1024
MESH = pl.DeviceIdType.MESH


def _params(*sem, vmem=V7X_VMEM_LIMIT):
    return pltpu.CompilerParams(dimension_semantics=sem or None, vmem_limit_bytes=vmem)


def _resident(shape):
    return pl.BlockSpec(shape, lambda *_: (0,) * len(shape), pipeline_mode=pl.Buffered(1))


def _rows(tb, cols):
    return pl.BlockSpec((tb, cols), lambda i: (i, 0))


def _acc(shape):
    return pl.BlockSpec(shape, lambda *_: (0,) * len(shape))


def _dot(a, b):
    return jnp.dot(a, b, preferred_element_type=F32)


def _dot_nt(a, b):
    return lax.dot_general(a, b, (((1,), (1,)), ((), ())), preferred_element_type=F32)


def _dot_tn(a, b):
    return lax.dot_general(a, b, (((0,), (0,)), ((), ())), preferred_element_type=F32)


def _sigmoid(x):
    return 1.0 / (1.0 + jnp.exp(-x))


_GELU_C = 0.7978845608028654
_GELU_K = 0.044715


def _gelu_and_grad(x):
    u = x * x
    t = jnp.tanh(x * (_GELU_C + (_GELU_C * _GELU_K) * u))
    hp = 0.5 + 0.5 * t
    dg = hp + x * (0.5 - 0.5 * (t * t)) * (_GELU_C + (3.0 * _GELU_C * _GELU_K) * u)
    return x * hp, dg


def _gelu(x):
    return 0.5 * x * (1.0 + jnp.tanh(_GELU_C * (x + _GELU_K * x * x * x)))


def _ln_stats(z):
    mu = jnp.mean(z, axis=-1, keepdims=True)
    zc = z - mu
    var = jnp.mean(zc * zc, axis=-1, keepdims=True)
    rstd = lax.rsqrt(var + LN_EPS)
    return zc * rstd, rstd


def _ln_bwd(dy, xhat, rstd, g):
    dxh = dy * g
    m1 = jnp.mean(dxh, axis=-1, keepdims=True)
    m2 = jnp.mean(dxh * xhat, axis=-1, keepdims=True)
    return rstd * (dxh - m1 - xhat * m2)


def _softplus_neg(lam):
    u = jnp.exp(-jnp.abs(lam))
    w = 1.0 + u
    d = w - 1.0
    log1p_u = jnp.where(d == 0.0, u, jnp.log(w) * (u / jnp.where(d == 0.0, 1.0, d)))
    return jnp.maximum(-lam, 0.0) + log1p_u


def _shift_down(x, halo, s):
    xs = pltpu.roll(x, s, 0)
    hs = pltpu.roll(halo, s, 0)
    row8 = lax.broadcasted_iota(jnp.int32, hs.shape, 0)
    first = jnp.where(row8 < s, hs, xs[:8])
    return jnp.concatenate([first, xs[8:]], axis=0)


def _shift_up(x, halo, s):
    n = x.shape[0]
    xs = pltpu.roll(x, n - s, 0)
    hs = pltpu.roll(halo, 8 - s, 0)
    row8 = lax.broadcasted_iota(jnp.int32, hs.shape, 0)
    last = jnp.where(row8 >= 8 - s, hs, xs[n - 8:])
    return jnp.concatenate([xs[:n - 8], last], axis=0)


def _row_sum(x):
    return jnp.sum(x, axis=0, keepdims=True)


def _put_rows(acc_ref, rows):
    row8 = lax.broadcasted_iota(jnp.int32, acc_ref.shape, 0)
    upd = jnp.zeros(acc_ref.shape, F32)
    for r, vec in enumerate(rows):
        upd = jnp.where(row8 == r, vec, upd)
    acc_ref[...] += upd


def _place():
    return lax.axis_index("x"), lax.axis_index("y"), lax.axis_index("c")


def _dev_index(px, py, pc):
    return 4 * px + 2 * py + pc


_ANY = pl.BlockSpec(memory_space=pl.ANY)


class _Gather:
    def __init__(self, arrays):
        self.arrays = list(arrays)
        self.n = len(self.arrays)

    def out_shape(self):
        return [jax.ShapeDtypeStruct((N_DEV,) + s.shape, s.dtype) for s in self.arrays]

    def scratch(self):
        return [pltpu.SemaphoreType.DMA((self.n, 7)), pltpu.SemaphoreType.DMA((self.n, 7)),
                pltpu.SemaphoreType.DMA((self.n,))]

    def _parts(self, ins, outs, sems):
        send_sems, recv_sems, local_sems = sems
        x, y, c = _place()
        me, sibling = (x, y, c), (x, y, 1 - c)
        chips = [(1 - x, y), (x, 1 - y), (1 - x, 1 - y)]

        def copy(a, k, block, to, src=None):
            rows = outs[a].at[_dev_index(*block)]
            return pltpu.make_async_remote_copy(
                src_ref=rows if src is None else src, dst_ref=rows, send_sem=send_sems.at[a, k],
                recv_sem=recv_sems.at[a, k], device_id=to, device_id_type=MESH)

        rng = range(self.n)
        mine = [pltpu.make_async_copy(ins[a], outs[a].at[_dev_index(*me)], local_sems.at[a]) for a in rng]
        first = [copy(a, 0, me, sibling, src=ins[a]) for a in rng]
        first += [copy(a, 1 + j, me, (*chip, c), src=ins[a]) for j, chip in enumerate(chips) for a in rng]
        landed = [copy(a, 1 + j, (*chip, c), me) for j, chip in enumerate(chips) for a in rng]
        passed = [copy(a, 4 + j, (*chip, c), sibling) for j, chip in enumerate(chips) for a in rng]
        from_sibling = [copy(a, 0, sibling, me) for a in rng]
        from_sibling += [copy(a, 4 + j, (*chip, 1 - c), me) for j, chip in enumerate(chips) for a in rng]
        return mine, first, landed, passed, from_sibling

    def start(self, ins, outs, sems):
        mine, first, _, _, _ = self._parts(ins, outs, sems)
        for cp in mine + first:
            cp.start()

    def forward(self, ins, outs, sems):
        _, _, landed, passed, _ = self._parts(ins, outs, sems)
        for got, fwd in zip(landed, passed):
            got.wait_recv()
            fwd.start()

    def finish(self, ins, outs, sems):
        mine, first, _, passed, from_sibling = self._parts(ins, outs, sems)
        for cp in from_sibling:
            cp.wait_recv()
        for cp in first + passed:
            cp.wait_send()
        for cp in mine:
            cp.wait()

    def before(self, ins, outs, sems, step, nsteps):
        pl.when(step == 0)(lambda: self.start(ins, outs, sems))
        pl.when(step == (7 * nsteps) // 8)(lambda: self.forward(ins, outs, sems))

    def after(self, ins, outs, sems, step, nsteps):
        pl.when(step == nsteps - 1)(lambda: self.finish(ins, outs, sems))


class _Exchange:
    def __init__(self, arrays):
        self.arrays = list(arrays)
        self.n = len(self.arrays)

    def out_shape(self):
        return [jax.ShapeDtypeStruct(b.shape, b.dtype) for b in self.arrays]

    def scratch(self):
        return [pltpu.SemaphoreType.DMA((self.n, 7)), pltpu.SemaphoreType.DMA((self.n, 7)),
                pltpu.SemaphoreType.DMA((self.n,))]

    def _parts(self, ins, outs, sems):
        send_sems, recv_sems, local_sems = sems
        x, y, c = _place()
        me = _dev_index(x, y, c)
        peers = [(x ^ (k >> 2), y ^ ((k >> 1) & 1), c ^ (k & 1)) for k in range(1, N_DEV)]
        rng = range(self.n)
        mine = [pltpu.make_async_copy(ins[a].at[me], outs[a].at[me], local_sems.at[a]) for a in rng]
        sent = [pltpu.make_async_remote_copy(
            src_ref=ins[a].at[_dev_index(*to)], dst_ref=outs[a].at[me], send_sem=send_sems.at[a, k],
            recv_sem=recv_sems.at[a, k], device_id=to, device_id_type=MESH) for k, to in enumerate(peers) for a in rng]
        arrivals = [pltpu.make_async_remote_copy(
            src_ref=ins[a].at[me], dst_ref=outs[a].at[_dev_index(*frm)], send_sem=send_sems.at[a, k],
            recv_sem=recv_sems.at[a, k], device_id=frm, device_id_type=MESH) for k, frm in enumerate(peers) for a in rng]
        return mine, sent, arrivals

    def start(self, ins, outs, sems):
        mine, sent, _ = self._parts(ins, outs, sems)
        for cp in mine + sent:
            cp.start()

    def finish(self, ins, outs, sems):
        mine, sent, arrivals = self._parts(ins, outs, sems)
        for cp in arrivals:
            cp.wait_recv()
        for cp in sent:
            cp.wait_send()
        for cp in mine:
            cp.wait()

    def before(self, ins, outs, sems, step, nsteps):
        pl.when(step == 0)(lambda: self.start(ins, outs, sems))

    def after(self, ins, outs, sems, step, nsteps):
        pl.when(step == nsteps - 1)(lambda: self.finish(ins, outs, sems))


class _Bcast(_Exchange):
    def out_shape(self):
        return [jax.ShapeDtypeStruct((N_DEV,) + s.shape, s.dtype) for s in self.arrays]

    def _parts(self, ins, outs, sems):
        send_sems, recv_sems, local_sems = sems
        x, y, c = _place()
        me = _dev_index(x, y, c)
        peers = [(x ^ (k >> 2), y ^ ((k >> 1) & 1), c ^ (k & 1)) for k in range(1, N_DEV)]
        rng = range(self.n)
        mine = [pltpu.make_async_copy(ins[a], outs[a].at[me], local_sems.at[a]) for a in rng]
        sent = [pltpu.make_async_remote_copy(
            src_ref=ins[a], dst_ref=outs[a].at[me], send_sem=send_sems.at[a, k], recv_sem=recv_sems.at[a, k],
            device_id=to, device_id_type=MESH) for k, to in enumerate(peers) for a in rng]
        arrivals = [pltpu.make_async_remote_copy(
            src_ref=ins[a], dst_ref=outs[a].at[_dev_index(*frm)], send_sem=send_sems.at[a, k],
            recv_sem=recv_sems.at[a, k], device_id=frm, device_id_type=MESH) for k, frm in enumerate(peers) for a in rng]
        return mine, sent, arrivals


class _Multi:
    def __init__(self, comms):
        self.comms = list(comms)
        self.arrays = [arr for c in self.comms for arr in c.arrays]
        self.n = len(self.arrays)

    def out_shape(self):
        return [s for c in self.comms for s in c.out_shape()]

    def scratch(self):
        return [s for c in self.comms for s in c.scratch()]

    def _each(self, ins, outs, sems):
        a = 0
        for j, c in enumerate(self.comms):
            yield c, ins[a:a + c.n], outs[a:a + c.n], sems[3 * j:3 * j + 3]
            a += c.n

    def before(self, ins, outs, sems, step, nsteps):
        for c, ci, co, cs in self._each(ins, outs, sems):
            c.before(ci, co, cs, step, nsteps)

    def after(self, ins, outs, sems, step, nsteps):
        for c, ci, co, cs in self._each(ins, outs, sems):
            c.after(ci, co, cs, step, nsteps)


def _comm_call(comms, name):
    ns = [c.n for c in comms]
    n = sum(ns)

    def body(*refs):
        parts, a, s = [], 0, 2 * n
        for c in comms:
            parts.append((c, refs[a:a + c.n], refs[n + a:n + a + c.n], refs[s:s + 3]))
            a, s = a + c.n, s + 3
        for c, ins, outs, sems in parts:
            c.start(ins, outs, sems)
        for c, ins, outs, sems in parts:
            if isinstance(c, _Gather):
                c.forward(ins, outs, sems)
        for c, ins, outs, sems in parts:
            c.finish(ins, outs, sems)

    res = pl.pallas_call(
        body, name=name, in_specs=[_ANY] * n, out_specs=[_ANY] * n,
        out_shape=[s for c in comms for s in c.out_shape()], scratch_shapes=[s for c in comms for s in c.scratch()],
    )(*[arr for c in comms for arr in c.arrays])
    out, a = [], 0
    for k in ns:
        out.append(res[a:a + k])
        a += k
    return out


def _pcall(body, args, *, name, grid, in_specs, out_specs, out_shape, scratch_shapes=(), sem="parallel", comm=None,
           step_axis=0):
    sem = (sem,) * len(grid) if isinstance(sem, str) else sem
    if comm is None:
        res = pl.pallas_call(body, name=name, grid=grid, in_specs=in_specs, out_specs=out_specs, out_shape=out_shape,
                             scratch_shapes=list(scratch_shapes), compiler_params=_params(*sem))(*args)
        return res, []
    n_in, n_out, n_scr, n = len(in_specs), len(out_specs), len(scratch_shapes), comm.n
    nsteps = grid[step_axis]
    assert all(g == 1 for ax, g in enumerate(grid) if ax != step_axis)

    def hosted(*refs):
        ins, cin = refs[:n_in], refs[n_in:n_in + n]
        o0 = n_in + n
        outs, cout = refs[o0:o0 + n_out], refs[o0 + n_out:o0 + n_out + n]
        s0 = o0 + n_out + n
        scr, sems = refs[s0:s0 + n_scr], refs[s0 + n_scr:]
        step = pl.program_id(step_axis)
        comm.before(cin, cout, sems, step, nsteps)
        body(*ins, *outs, *scr)
        comm.after(cin, cout, sems, step, nsteps)

    res = pl.pallas_call(
        hosted, name=name, grid=grid, in_specs=list(in_specs) + [_ANY] * n, out_specs=list(out_specs) + [_ANY] * n,
        out_shape=list(out_shape) + comm.out_shape(), scratch_shapes=list(scratch_shapes) + comm.scratch(),
        compiler_params=_params(*(("arbitrary",) * len(grid))))(*args, *comm.arrays)
    return res[:n_out], res[n_out:]


def _in_proj(x, w_in_t, comm=None):
    S = x.shape[0]
    tb = min(512, S)

    def body(x_ref, w_ref, q_ref, k_ref, v_ref, xr_ref, gr_ref):
        u = _dot_nt(x_ref[...].astype(BF16), w_ref[...])
        q_ref[...] = (u[:, :D_ATT] * (HEAD_DIM ** -0.5)).astype(BF16)
        k_ref[...] = u[:, D_ATT:D_ATT + D_KV].astype(BF16)
        v_ref[...] = u[:, D_ATT + D_KV:D_ATT + 2 * D_KV].astype(BF16)
        xr_ref[...] = u[:, D_ATT + 2 * D_KV:D_ATT + 2 * D_KV + D_RNN]
        gr_ref[...] = u[:, D_ATT + 2 * D_KV + D_RNN:]

    return _pcall(
        body, (x, w_in_t), name="in_proj", grid=(S // tb,), comm=comm,
        in_specs=[_rows(tb, D_MODEL), _resident((D_IN, D_MODEL))],
        out_specs=[_rows(tb, D_ATT), _rows(tb, D_KV), _rows(tb, D_KV), _rows(tb, D_RNN), _rows(tb, D_RNN)],
        out_shape=[jax.ShapeDtypeStruct((S, D_ATT), BF16), jax.ShapeDtypeStruct((S, D_KV), BF16),
                   jax.ShapeDtypeStruct((S, D_KV), BF16), jax.ShapeDtypeStruct((S, D_RNN), F32),
                   jax.ShapeDtypeStruct((S, D_RNN), F32)])


GROUP = N_HEADS // N_KV


def _band_mask(i):
    qi = lax.broadcasted_iota(jnp.int32, (GROUP * QBLK, 2 * QBLK), 0) & (QBLK - 1)
    sj = lax.broadcasted_iota(jnp.int32, (GROUP * QBLK, 2 * QBLK), 1)
    return (sj > qi) & (sj <= qi + QBLK) & ((sj >= QBLK) | (i > 0))


def _stack_heads(x, g):
    return jnp.concatenate([x[:, (g * GROUP + hh) * HEAD_DIM:(g * GROUP + hh + 1) * HEAD_DIM] for hh in range(GROUP)],
                           axis=0)


def _unstack_heads(x4):
    return [x4[hh * QBLK:(hh + 1) * QBLK] for hh in range(GROUP)]


def _sink_column(sink_ref, g):
    head = lax.broadcasted_iota(jnp.int32, (GROUP * QBLK, 1), 0) // QBLK
    col = jnp.full((GROUP * QBLK, 1), sink_ref[g * GROUP], F32)
    for hh in range(1, GROUP):
        col = jnp.where(head == hh, sink_ref[g * GROUP + hh], col)
    return col


ATT_STEP = 4
IN_GRAD_PARTS = 2


def _attn_specs(nq=1):
    cur = lambda i: (i, 0)
    prev = lambda i: (jnp.maximum(nq * i - 1, 0), 0)
    return [pl.BlockSpec((nq * QBLK, D_KV), cur), pl.BlockSpec((QBLK, D_KV), prev),
            pl.BlockSpec((nq * QBLK, D_KV), cur), pl.BlockSpec((QBLK, D_KV), prev)]


def _attn_fwd(q, k, v, sinks, comm=None):
    S = q.shape[0]
    nq = min(ATT_STEP, S // QBLK)

    def body(sink_ref, q_ref, kc_ref, kp_ref, vc_ref, vp_ref, o_ref, lse_ref):
        first = pl.program_id(0) * nq
        kall = jnp.concatenate([kp_ref[...], kc_ref[...]], axis=0)
        vall = jnp.concatenate([vp_ref[...], vc_ref[...]], axis=0)
        for b in range(nq):
            valid = _band_mask(first + b)
            rows = slice(b * QBLK, (b + 1) * QBLK)
            keys = slice(b * QBLK, (b + 2) * QBLK)
            qv = q_ref[rows, :]
            outs = []
            for g in range(N_KV):
                kcat = kall[keys, g * HEAD_DIM:(g + 1) * HEAD_DIM]
                vcat = vall[keys, g * HEAD_DIM:(g + 1) * HEAD_DIM]
                s = jnp.where(valid, _dot_nt(_stack_heads(qv, g), kcat), -1e30)
                sink = _sink_column(sink_ref, g)
                m = jnp.maximum(jnp.max(s, axis=1, keepdims=True), sink)
                p = jnp.exp(s - m)
                l = jnp.sum(p, axis=1, keepdims=True) + jnp.exp(sink - m)
                outs += _unstack_heads(_dot(p.astype(BF16), vcat) / l)
                lse_ref[(b * N_KV + g) * GROUP * QBLK:(b * N_KV + g + 1) * GROUP * QBLK, :] = m + jnp.log(l)
            o_ref[rows, :] = jnp.concatenate(outs, axis=1).astype(BF16)

    lse_rows = nq * N_HEADS * QBLK
    return _pcall(
        body, (sinks, q, k, k, v, v), name="attn_fwd", grid=(S // (nq * QBLK),), comm=comm,
        in_specs=[pl.BlockSpec(memory_space=pltpu.SMEM), _rows(nq * QBLK, D_ATT)] + _attn_specs(nq),
        out_specs=[_rows(nq * QBLK, D_ATT), _rows(lse_rows, 1)],
        out_shape=[jax.ShapeDtypeStruct((S, D_ATT), BF16), jax.ShapeDtypeStruct((S * N_HEADS, 1), F32)])


def _w_rows(w_ref):
    return [w_ref[k:k + 1, :] for k in range(w_ref.shape[0])]


def _conv4(x, halo, w, b):
    y = b + w[3] * x
    for s in (1, 2, 3):
        y = y + w[3 - s] * _shift_down(x, halo, s)
    return y


def _rnn_gates(xc, wa, wx, ba, bx, sp):
    xcb = xc.astype(BF16)
    r = _sigmoid(_dot(xcb, wa) + ba)
    ig = _sigmoid(_dot(xcb, wx) + bx)
    la = -LRU_C * r * sp
    a = jnp.exp(la)
    t = jnp.tanh(la)
    f = jnp.sqrt(-2.0 * t / (1.0 - t))
    return r, ig, a, f


def _rnn_fwd(xr, gr, conv_w, conv_b, wa, wx, ba, bx, lam, comm=None):
    S = xr.shape[0]
    tb = min(256, S)

    def body(xr_ref, gr_ref, cw_ref, cb_ref, wa_ref, wx_ref, ba_ref, bx_ref, lam_ref, rec_ref, h_ref,
             xc_ref, r_ref, ig_ref, a_ref, f_ref, halo_s, hc_s, a_s, b_s):
        @pl.when(pl.program_id(0) == 0)
        def _():
            halo_s[...] = jnp.zeros_like(halo_s)
            hc_s[...] = jnp.zeros_like(hc_s)

        x = xr_ref[...]
        xc = _conv4(x, halo_s[...], _w_rows(cw_ref), cb_ref[...])
        halo_s[...] = x[tb - 8:]
        r, ig, a, f = _rnn_gates(xc, wa_ref[...], wx_ref[...], ba_ref[...], bx_ref[...], _softplus_neg(lam_ref[...]))
        xc_ref[...] = xc
        r_ref[...] = r
        ig_ref[...] = ig
        a_ref[...] = a
        f_ref[...] = f
        a_s[...] = a
        b_s[...] = f * ig * xc
        row8 = lax.broadcasted_iota(jnp.int32, (8, D_RNN), 0)

        def tile(t, hc):
            o = pl.multiple_of(t * 8, 8)
            at = a_s[pl.ds(o, 8), :]
            bt = b_s[pl.ds(o, 8), :]
            for s in (1, 2, 4):
                keep = row8 >= s
                a_sh = jnp.where(keep, pltpu.roll(at, s, 0), 1.0)
                b_sh = jnp.where(keep, pltpu.roll(bt, s, 0), 0.0)
                bt = at * b_sh + bt
                at = at * a_sh
            ht = at * hc + bt
            b_s[pl.ds(o, 8), :] = ht
            return _row_sum(jnp.where(row8 == 7, ht, 0.0))

        hc_s[0:1, :] = lax.fori_loop(0, tb // 8, tile, hc_s[0:1, :], unroll=2)
        h = b_s[...]
        h_ref[...] = h
        rec_ref[...] = (h * _gelu(gr_ref[...])).astype(BF16)

    vec = _resident((1, D_RNN))
    kept = jax.ShapeDtypeStruct((S, D_RNN), F32)
    return _pcall(
        body, (xr, gr, conv_w, conv_b, wa, wx, ba, bx, lam), name="rnn_fwd", grid=(S // tb,), sem="arbitrary", comm=comm,
        in_specs=[_rows(tb, D_RNN), _rows(tb, D_RNN), _resident((4, D_RNN)), vec,
                  _resident((D_RNN, D_RNN)), _resident((D_RNN, D_RNN)), vec, vec, vec],
        out_specs=[_rows(tb, D_RNN)] * 7,
        out_shape=[jax.ShapeDtypeStruct((S, D_RNN), BF16), kept, kept, kept, kept, kept, kept],
        scratch_shapes=[pltpu.VMEM((8, D_RNN), F32), pltpu.VMEM((8, D_RNN), F32),
                        pltpu.VMEM((tb, D_RNN), F32), pltpu.VMEM((tb, D_RNN), F32)])


def _mix_ln1_up(x, att, rec, w_out, ln1_g, ln1_b, w_up, fcw, fcb, comm=None):
    S = x.shape[0]
    tb = min(256, S)
    nblk, _, wblk = w_up.shape
    half = nblk // 2

    def body(x_ref, att_ref, rec_ref, wo_ref, g_ref, b_ref, wu_ref, fcw_ref, fcb_ref,
             z1_ref, h1_ref, h1b_ref, gate_ref, act_ref, gl_ref, vdgl_ref, halo_s):
        @pl.when(pl.program_id(0) == 0)
        def _():
            halo_s[...] = jnp.zeros_like(halo_s)

        z1 = ALPHA * x_ref[...] + _dot(att_ref[...], wo_ref[:D_ATT, :]) + _dot(rec_ref[...], wo_ref[D_ATT:, :])
        z1_ref[...] = z1
        xhat, _ = _ln_stats(z1)
        h1 = xhat * g_ref[...] + b_ref[...]
        h1_ref[...] = h1
        h1b = h1.astype(BF16)
        h1b_ref[...] = h1b
        for jj in range(half):
            cols = slice(jj * wblk, (jj + 1) * wblk)
            gate = _dot(h1b, wu_ref[jj])
            val = _dot(h1b, wu_ref[jj + half])
            halo = halo_s[:, cols]
            conv = (fcb_ref[:, cols] + fcw_ref[2:3, cols] * gate + fcw_ref[1:2, cols] * _shift_down(gate, halo, 1)
                    + fcw_ref[0:1, cols] * _shift_down(gate, halo, 2))
            halo_s[:, cols] = gate[tb - 8:]
            gl, dgl = _gelu_and_grad(conv)
            gate_ref[:, cols] = gate.astype(BF16)
            act_ref[:, cols] = (gl * val).astype(BF16)
            gl_ref[:, cols] = gl.astype(BF16)
            vdgl_ref[:, cols] = (val * dgl).astype(BF16)

    vec = _resident((1, D_MODEL))
    wide = jax.ShapeDtypeStruct((S, D_FF), BF16)
    return _pcall(
        body, (x, att, rec, w_out, ln1_g, ln1_b, w_up, fcw, fcb), name="mix_ln1_up", grid=(S // tb,),
        sem="arbitrary", comm=comm,
        in_specs=[_rows(tb, D_MODEL), _rows(tb, D_ATT), _rows(tb, D_RNN), _resident((D_MODEL, D_MODEL)), vec, vec,
                  _resident(w_up.shape), _resident((3, D_FF)), _resident((1, D_FF))],
        out_specs=[_rows(tb, D_MODEL), _rows(tb, D_MODEL), _rows(tb, D_MODEL)] + [_rows(tb, D_FF)] * 4,
        out_shape=[jax.ShapeDtypeStruct((S, D_MODEL), F32), jax.ShapeDtypeStruct((S, D_MODEL), F32),
                   jax.ShapeDtypeStruct((S, D_MODEL), BF16), wide, wide, wide, wide],
        scratch_shapes=[pltpu.VMEM((8, D_FF), F32)])


def _tail(act, gl, vdgl, h1, h1b, p, tgt, w_down, w_pg, b_pg, w_pp, ln2_g, ln2_b):
    S = h1.shape[0]
    tb = min(256, S)

    def body(act_ref, gl_ref, vdgl_ref, h1_ref, h1b_ref, p_ref, t_ref, wd_ref, wpg_ref, bpg_ref, wpp_ref, g2_ref, b2_ref,
             dz2_ref, dpre_ref, dpp_ref, dgc_ref, dval_ref, dh1_ref, acc_ref):
        i = pl.program_id(0)

        @pl.when(i == 0)
        def _():
            acc_ref[...] = jnp.zeros_like(acc_ref)

        ffn = _dot(act_ref[...], wd_ref[...])
        h1 = h1_ref[...]
        sg = _sigmoid(_dot(h1b_ref[...], wpg_ref[...]) + bpg_ref[...])
        pp = _dot(p_ref[...].astype(BF16), wpp_ref[...])
        z2 = ALPHA * h1 + ffn + sg * pp
        xhat2, rstd2 = _ln_stats(z2)
        y = xhat2 * g2_ref[...] + b2_ref[...]
        err = y - t_ref[...]
        dy = err * (1.0 / D_MODEL)
        loss = 0.5 * jnp.sum(jnp.sum(err * err, axis=1, keepdims=True), axis=0, keepdims=True) * (1.0 / D_MODEL)
        dz2 = _ln_bwd(dy, xhat2, rstd2, g2_ref[...])
        dz2b = dz2.astype(BF16)
        dz2_ref[...] = dz2b
        dpre = dz2 * pp * sg * (1.0 - sg)
        dpreb = dpre.astype(BF16)
        dpre_ref[...] = dpreb
        dpp_ref[...] = (dz2 * sg).astype(BF16)
        dh1_ref[...] = ALPHA * dz2 + _dot_nt(dpreb, wpg_ref[...])
        dactb = _dot_nt(dz2b, wd_ref[...]).astype(BF16)
        dval_ref[...] = dactb * gl_ref[...]
        dgc_ref[...] = dactb * vdgl_ref[...]
        _put_rows(acc_ref, [_row_sum(dy * xhat2), _row_sum(dy), _row_sum(dpre),
                            jnp.broadcast_to(loss, (1, D_MODEL))])

    vec = _resident((1, D_MODEL))
    return pl.pallas_call(
        body, name="tail", grid=(S // tb,),
        in_specs=[_rows(tb, D_FF), _rows(tb, D_FF), _rows(tb, D_FF), _rows(tb, D_MODEL), _rows(tb, D_MODEL),
                  _rows(tb, PLE_DIM), _rows(tb, D_MODEL), _resident((D_FF, D_MODEL)), _resident((D_MODEL, D_MODEL)), vec,
                  _resident((PLE_DIM, D_MODEL)), vec, vec],
        out_specs=[_rows(tb, D_MODEL), _rows(tb, D_MODEL), _rows(tb, D_MODEL), _rows(tb, D_FF),
                   _rows(tb, D_FF), _rows(tb, D_MODEL), _acc((8, D_MODEL))],
        out_shape=[jax.ShapeDtypeStruct((S, D_MODEL), BF16),
                   jax.ShapeDtypeStruct((S, D_MODEL), BF16), jax.ShapeDtypeStruct((S, D_MODEL), BF16),
                   jax.ShapeDtypeStruct((S, D_FF), BF16), jax.ShapeDtypeStruct((S, D_FF), BF16),
                   jax.ShapeDtypeStruct((S, D_MODEL), F32), jax.ShapeDtypeStruct((8, D_MODEL), F32)],
        compiler_params=_params("arbitrary"),
    )(act, gl, vdgl, h1, h1b, p, tgt, w_down, w_pg, b_pg, w_pp, ln2_g, ln2_b)


def _weight_grad(a_list, b_list, name, layout, ts=512, comm=None, b_window=None, halves=False):
    S = a_list[0].shape[0]
    ms = [a.shape[1] for a in a_list]
    M, nb = sum(ms), len(b_list)
    win, Nb = b_window if b_window else (0, b_list[0].shape[1])
    ts = min(ts, S)
    nk = S // ts
    per_b = N_DEV // nb
    na = len(a_list)

    n_out = 2 if halves else 1
    assert layout == "cols" or not halves

    def body(*refs):
        a_refs, b_refs, o_refs, acc_ref = refs[:na], refs[na:na + nb], refs[na + nb:na + nb + n_out], refs[-1]
        o_ref = o_refs[0]
        j, k = pl.program_id(0), pl.program_id(1)

        @pl.when(k == 0)
        def _():
            acc_ref[...] = jnp.zeros_like(acc_ref)

        for jj in range(nb):
            @pl.when(j == jj)
            def _():
                b = b_refs[jj][...].astype(BF16)
                off = 0
                for a_ref, m in zip(a_refs, ms):
                    acc_ref[off:off + m, :] += _dot_tn(a_ref[...].astype(BF16), b)
                    off += m

        @pl.when(k == nk - 1)
        def _():
            for d in range(per_b):
                if layout == "rows":
                    o_ref[d] = acc_ref[d * (M // N_DEV):(d + 1) * (M // N_DEV), :].astype(BF16)
                elif layout == "cols" and halves:
                    for o_half, r0 in zip(o_refs, (0, M // 2)):
                        o_half[d] = acc_ref[r0:r0 + M // 2, d * (Nb // per_b):(d + 1) * (Nb // per_b)].astype(BF16)
                elif layout == "cols":
                    o_ref[d] = acc_ref[:, d * (Nb // per_b):(d + 1) * (Nb // per_b)].astype(BF16)
                else:
                    o_ref[d] = acc_ref[:, d * (Nb // per_b):(d + 1) * (Nb // per_b)].T.astype(BF16)

    def b_index(jj):
        return lambda j, k: (jnp.where(j == jj, k, jnp.where(j < jj, 0, nk - 1)), win)

    if layout == "rows":
        assert nb == 1
        blk = (N_DEV, M // N_DEV, Nb)
    elif layout == "cols":
        blk = (per_b, M // n_out, Nb // per_b)
    else:
        blk = (per_b, Nb // per_b, M)
    res, comm_res = _pcall(
        body, (*a_list, *b_list), name=name, grid=(nb, nk), sem="arbitrary", comm=comm, step_axis=1,
        in_specs=[pl.BlockSpec((ts, m), lambda j, k: (k, 0)) for m in ms]
        + [pl.BlockSpec((ts, Nb), b_index(jj)) for jj in range(nb)],
        out_specs=[pl.BlockSpec(blk, lambda j, k: (j, 0, 0))] * n_out,
        out_shape=[jax.ShapeDtypeStruct((N_DEV,) + blk[1:], BF16)] * n_out,
        scratch_shapes=[pltpu.VMEM((M, Nb), F32)])
    res = res if halves else res[0]
    return (res, comm_res) if comm is not None else res


def _up_bwd(dgc, gate, dval, dh1p, z1, w_up, fcw, w_out, ln1_g, comm=None):
    S = z1.shape[0]
    tb = min(256, S)
    t16 = tb // 16
    n16 = S // 16
    nblk, _, wblk = w_up.shape
    half = nblk // 2
    nsteps = S // tb

    def body(dgc_ref, dgn_ref, gc_ref, dval_ref, dh1p_ref, z1_ref, wu_ref, fcw_ref, wo_ref, g1_ref,
             dgate_ref, dz1_ref, dz1b_ref, datt_ref, drec_ref, accf_ref, accd_ref):
        i = pl.program_id(0)

        @pl.when(i == 0)
        def _():
            accf_ref[...] = jnp.zeros_like(accf_ref)
            accd_ref[...] = jnp.zeros_like(accd_ref)

        dg = dgc_ref[...].astype(F32)
        nxt = jnp.where(i < nsteps - 1, dgn_ref[...].astype(F32)[0:8], 0.0)
        w = _w_rows(fcw_ref)
        up1, up2 = _shift_up(dg, nxt, 1), _shift_up(dg, nxt, 2)
        dgate = (w[2] * dg + w[1] * up1 + w[0] * up2).astype(BF16)
        dgate_ref[...] = dgate
        gate = gc_ref[...].astype(F32)
        _put_rows(accf_ref, [_row_sum(up2 * gate), _row_sum(up1 * gate), _row_sum(dg * gate), _row_sum(dg)])

        dh1 = dh1p_ref[...]
        for j in range(nblk):
            src = dgate if j < half else dval_ref[...]
            jj = j % half
            dh1 = dh1 + _dot_nt(src[:, jj * wblk:(jj + 1) * wblk], wu_ref[j])
        xhat1, rstd1 = _ln_stats(z1_ref[...])
        dz1 = _ln_bwd(dh1, xhat1, rstd1, g1_ref[...])
        dz1_ref[...] = dz1
        dz1b = dz1.astype(BF16)
        dz1b_ref[...] = dz1b
        dcat = _dot_nt(dz1b, wo_ref[...])
        datt_ref[...] = dcat[:, :D_ATT].astype(BF16)
        drec_ref[...] = dcat[:, D_ATT:]
        _put_rows(accd_ref, [_row_sum(dh1 * xhat1), _row_sum(dh1)])

    next16 = pl.BlockSpec((16, D_FF), lambda i: (jnp.minimum((i + 1) * t16, n16 - 1), 0))
    return _pcall(
        body, (dgc, dgc, gate, dval, dh1p, z1, w_up, fcw, w_out, ln1_g), name="up_bwd",
        grid=(nsteps,), sem="arbitrary", comm=comm,
        in_specs=[_rows(tb, D_FF), next16, _rows(tb, D_FF), _rows(tb, D_FF), _rows(tb, D_MODEL),
                  _rows(tb, D_MODEL), _resident(w_up.shape), _resident((3, D_FF)),
                  _resident((D_MODEL, D_MODEL)), _resident((1, D_MODEL))],
        out_specs=[_rows(tb, D_FF), _rows(tb, D_MODEL), _rows(tb, D_MODEL), _rows(tb, D_ATT), _rows(tb, D_RNN),
                   _acc((8, D_FF)), _acc((8, D_MODEL))],
        out_shape=[jax.ShapeDtypeStruct((S, D_FF), BF16), jax.ShapeDtypeStruct((S, D_MODEL), F32),
                   jax.ShapeDtypeStruct((S, D_MODEL), BF16), jax.ShapeDtypeStruct((S, D_ATT), BF16),
                   jax.ShapeDtypeStruct((S, D_RNN), F32), jax.ShapeDtypeStruct((8, D_FF), F32),
                   jax.ShapeDtypeStruct((8, D_MODEL), F32)])


def _attn_bwd(q, k, v, lse, do, sinks, comm=None):
    S = q.shape[0]
    grp = N_HEADS // N_KV
    nq = min(ATT_STEP, S // QBLK)

    def body(sink_ref, q_ref, kc_ref, kp_ref, vc_ref, vp_ref, do_ref, lse_ref, dq_ref, dkc_ref, dkp_ref, dvc_ref, dvp_ref,
             ds_ref):
        i = pl.program_id(0)

        @pl.when(i == 0)
        def _():
            ds_ref[...] = jnp.zeros_like(ds_ref)

        row8 = lax.broadcasted_iota(jnp.int32, (8, 128), 0)
        lane8 = lax.broadcasted_iota(jnp.int32, (8, 128), 1)
        dsink = jnp.zeros((8, 128), F32)
        kall = jnp.concatenate([kp_ref[...], kc_ref[...]], axis=0)
        vall = jnp.concatenate([vp_ref[...], vc_ref[...]], axis=0)
        dk_t = [jnp.zeros((D_KV, QBLK), F32) for _ in range(nq + 1)]
        dv_t = [jnp.zeros((D_KV, QBLK), F32) for _ in range(nq + 1)]
        for b in range(nq):
            valid = _band_mask(i * nq + b)
            rows = slice(b * QBLK, (b + 1) * QBLK)
            keys = slice(b * QBLK, (b + 2) * QBLK)
            qv, dov = q_ref[rows, :], do_ref[rows, :]
            dqs, dks, dvs = [], [], []
            for g in range(N_KV):
                kcat = kall[keys, g * HEAD_DIM:(g + 1) * HEAD_DIM]
                vcat = vall[keys, g * HEAD_DIM:(g + 1) * HEAD_DIM]
                q4, do4 = _stack_heads(qv, g), _stack_heads(dov, g)
                s = jnp.where(valid, _dot_nt(q4, kcat), -1e30)
                lse = lse_ref[(b * N_KV + g) * GROUP * QBLK:(b * N_KV + g + 1) * GROUP * QBLK, :]
                p = jnp.exp(s - lse)
                p_sink = jnp.exp(_sink_column(sink_ref, g) - lse)
                dp = _dot_nt(do4, vcat)
                delta = jnp.sum(p * dp, axis=1, keepdims=True)
                dsc = (p * (dp - delta)).astype(BF16)
                dqs += _unstack_heads(_dot(dsc, kcat) * (HEAD_DIM ** -0.5))
                dks.append(_dot_tn(q4, dsc))
                dvs.append(_dot_tn(do4, p.astype(BF16)))
                for hh, part in enumerate(_unstack_heads(-p_sink * delta)):
                    here = (row8 == 0) & (lane8 == g * grp + hh)
                    dsink = dsink + jnp.where(here, jnp.sum(part, axis=0, keepdims=True), 0.0)
            dq_ref[rows, :] = jnp.concatenate(dqs, axis=1).astype(BF16)
            dk2, dv2 = jnp.concatenate(dks, axis=0), jnp.concatenate(dvs, axis=0)
            dk_t[b], dk_t[b + 1] = dk_t[b] + dk2[:, :QBLK], dk_t[b + 1] + dk2[:, QBLK:]
            dv_t[b], dv_t[b + 1] = dv_t[b] + dv2[:, :QBLK], dv_t[b + 1] + dv2[:, QBLK:]
        dkp_ref[...] = dk_t[0].T
        dvp_ref[...] = dv_t[0].T
        for b in range(nq):
            dkc_ref[b * QBLK:(b + 1) * QBLK, :] = dk_t[b + 1].T
            dvc_ref[b * QBLK:(b + 1) * QBLK, :] = dv_t[b + 1].T
        ds_ref[...] += dsink

    nsteps = S // (nq * QBLK)
    cur = jax.ShapeDtypeStruct((S, D_KV), F32)
    prev = jax.ShapeDtypeStruct((nsteps * QBLK, D_KV), F32)
    big = _rows(nq * QBLK, D_ATT)
    return _pcall(
        body, (sinks, q, k, k, v, v, do, lse), name="attn_bwd", grid=(nsteps,), sem="arbitrary", comm=comm,
        in_specs=[pl.BlockSpec(memory_space=pltpu.SMEM), big] + _attn_specs(nq) + [big, _rows(nq * N_HEADS * QBLK, 1)],
        out_specs=[big, _rows(nq * QBLK, D_KV), _rows(QBLK, D_KV), _rows(nq * QBLK, D_KV), _rows(QBLK, D_KV),
                   _acc((8, 128))],
        out_shape=[jax.ShapeDtypeStruct((S, D_ATT), BF16), cur, prev, cur, prev, jax.ShapeDtypeStruct((8, 128), F32)])


def _rnn_bwd(xr, gr, h, kept, drec, conv_w, wa, wx, lam, comm=None):
    S = xr.shape[0]
    tb = min(256, S)
    t8 = tb // 8
    nsteps = S // tb

    def body(xr_ref, xp_ref, gr_ref, h_ref, hp_ref, xc_ref, r_ref, ig_ref, a_ref, f_ref, drec_ref, cw_ref, wa_ref, wx_ref,
             lam_ref, dxr_ref, dgr_ref, gwa_ref, gwx_ref, acc_ref, carry_s, dxc_halo_s, d_s, gwa_s, gwx_s):
        i = pl.program_id(0)
        blk = nsteps - 1 - i

        @pl.when(i == 0)
        def _():
            gwa_s[...] = jnp.zeros_like(gwa_s)
            gwx_s[...] = jnp.zeros_like(gwx_s)
            acc_ref[...] = jnp.zeros_like(acc_ref)
            carry_s[...] = jnp.zeros_like(carry_s)
            dxc_halo_s[...] = jnp.zeros_like(dxc_halo_s)

        x = xr_ref[...]
        xhalo = jnp.where(blk > 0, xp_ref[...], 0.0)
        cw = _w_rows(cw_ref)
        xs = [_shift_down(x, xhalo, 3), _shift_down(x, xhalo, 2), _shift_down(x, xhalo, 1), x]
        xc, r, ig, a, f = xc_ref[...], r_ref[...], ig_ref[...], a_ref[...], f_ref[...]
        sp = _softplus_neg(lam_ref[...])
        hcur = h_ref[...]
        hprev = _shift_down(hcur, jnp.where(blk > 0, hp_ref[...], 0.0), 1)
        gl, dgl = _gelu_and_grad(gr_ref[...])
        drec = drec_ref[...]
        dgr_ref[...] = (drec * hcur * dgl).astype(BF16)
        d_s[...] = drec * gl
        row8 = lax.broadcasted_iota(jnp.int32, (8, D_RNN), 0)

        def tile(t, c):
            o = pl.multiple_of((t8 - 1 - t) * 8, 8)
            a8 = a_ref[pl.ds(o, 8), :]
            dt = d_s[pl.ds(o, 8), :]
            at = jnp.where(row8 == 7, 1.0, pltpu.roll(a8, 7, 0))
            for s in (1, 2, 4):
                keep = row8 < 8 - s
                a_sh = jnp.where(keep, pltpu.roll(at, 8 - s, 0), 1.0)
                d_sh = jnp.where(keep, pltpu.roll(dt, 8 - s, 0), 0.0)
                dt = at * d_sh + dt
                at = at * a_sh
            lt = at * c + dt
            d_s[pl.ds(o, 8), :] = lt
            return _row_sum(jnp.where(row8 == 0, a8 * lt, 0.0))

        carry_s[0:1, :] = lax.fori_loop(0, t8, tile, carry_s[0:1, :], unroll=2)
        lmb = d_s[...]
        a2 = a * a
        dla = lmb * hprev * a - lmb * ig * xc * (a2 / f)
        di = lmb * f * xc
        dr = dla * (-LRU_C) * sp
        dpa = dr * r * (1.0 - r)
        dpx = di * ig * (1.0 - ig)
        dpab = dpa.astype(BF16)
        dpxb = dpx.astype(BF16)
        xcb = xc.astype(BF16)
        gwa_s[...] += _dot_tn(xcb, dpab)
        gwx_s[...] += _dot_tn(xcb, dpxb)

        @pl.when(i == nsteps - 1)
        def _():
            for dense, out in ((gwa_s[...], gwa_ref), (gwx_s[...], gwx_ref)):
                for b in range(RNN_BLOCKS):
                    rows = slice(b * HEAD_DIM, (b + 1) * HEAD_DIM)
                    out[rows, :] = dense[rows, b * HEAD_DIM:(b + 1) * HEAD_DIM]

        dxc = lmb * f * ig + _dot_nt(dpab, wa_ref[...]) + _dot_nt(dpxb, wx_ref[...])
        nxt = dxc_halo_s[...]
        dxr = cw[3] * dxc
        for s in (1, 2, 3):
            dxr = dxr + cw[3 - s] * _shift_up(dxc, nxt, s)
        dxr_ref[...] = dxr.astype(BF16)
        dxc_halo_s[...] = dxc[:8]
        dlam = _row_sum(dla * (-LRU_C) * r) * (-1.0 / (1.0 + jnp.exp(lam_ref[...])))
        _put_rows(acc_ref, [_row_sum(dxc * xs[0]), _row_sum(dxc * xs[1]), _row_sum(dxc * xs[2]), _row_sum(dxc * xs[3]),
                            _row_sum(dxc), _row_sum(dpa), _row_sum(dpx), dlam])

    rev = lambda i: (nsteps - 1 - i, 0)
    prev8 = lambda i: (jnp.maximum((nsteps - 1 - i) * t8 - 1, 0), 0)
    blkspec = pl.BlockSpec((tb, D_RNN), rev)
    halo8 = pl.BlockSpec((8, D_RNN), prev8)
    vec = _resident((1, D_RNN))
    return _pcall(
        body, (xr, xr, gr, h, h, *kept, drec, conv_w, wa, wx, lam), name="rnn_bwd", grid=(nsteps,),
        sem="arbitrary", comm=comm,
        in_specs=[blkspec, halo8, blkspec, blkspec, halo8] + [blkspec] * 6
        + [_resident((4, D_RNN)), _resident((D_RNN, D_RNN)), _resident((D_RNN, D_RNN)), vec],
        out_specs=[blkspec, blkspec, _acc((D_RNN, HEAD_DIM)), _acc((D_RNN, HEAD_DIM)), _acc((8, D_RNN))],
        out_shape=[jax.ShapeDtypeStruct((S, D_RNN), BF16), jax.ShapeDtypeStruct((S, D_RNN), BF16),
                   jax.ShapeDtypeStruct((D_RNN, HEAD_DIM), F32), jax.ShapeDtypeStruct((D_RNN, HEAD_DIM), F32),
                   jax.ShapeDtypeStruct((8, D_RNN), F32)],
        scratch_shapes=[pltpu.VMEM((8, D_RNN), F32), pltpu.VMEM((8, D_RNN), F32), pltpu.VMEM((tb, D_RNN), F32),
                        pltpu.VMEM((D_RNN, D_RNN), F32), pltpu.VMEM((D_RNN, D_RNN), F32)])


def _in_bwd(dq, dkc, dkp, dvc, dvp, dxr, dgr, dz1, w_in, comm=None):
    S = dz1.shape[0]
    tb = min(ATT_STEP * QBLK, S)
    nsteps = S // tb

    def body(dq_ref, dkc_ref, dkn_ref, dvc_ref, dvn_ref, dxr_ref, dgr_ref, dz1_ref, w_ref, dkv_ref, dx_ref):
        last = pl.program_id(0) == nsteps - 1

        def total(cur_ref, next_ref):
            nxt = jnp.where(last, 0.0, next_ref[...])
            tail = cur_ref[tb - QBLK:, :] + nxt
            return jnp.concatenate([cur_ref[:tb - QBLK, :], tail], axis=0) if tb > QBLK else tail

        dkv = jnp.concatenate([total(dkc_ref, dkn_ref), total(dvc_ref, dvn_ref)], axis=1).astype(BF16)
        dkv_ref[...] = dkv
        du = jnp.concatenate([dq_ref[...], dkv, dxr_ref[...], dgr_ref[...]], axis=1)
        dx_ref[...] = ALPHA * dz1_ref[...] + _dot(du, w_ref[...])

    nextp = pl.BlockSpec((QBLK, D_KV), lambda i: (jnp.minimum(i + 1, nsteps - 1), 0))
    return _pcall(
        body, (dq, dkc, dkp, dvc, dvp, dxr, dgr, dz1, w_in), name="in_bwd", grid=(nsteps,), comm=comm,
        in_specs=[_rows(tb, D_ATT), _rows(tb, D_KV), nextp, _rows(tb, D_KV), nextp,
                  _rows(tb, D_RNN), _rows(tb, D_RNN), _rows(tb, D_MODEL), _resident((D_IN, D_MODEL))],
        out_specs=[_rows(tb, 2 * D_KV), _rows(tb, D_MODEL)],
        out_shape=[jax.ShapeDtypeStruct((S, 2 * D_KV), BF16), jax.ShapeDtypeStruct((S, D_MODEL), F32)])


def _block_diag(w):
    eye = jnp.eye(RNN_BLOCKS, dtype=w.dtype)
    return (w[:, :, None, :] * eye[:, None, :, None]).reshape(D_RNN, D_RNN).astype(BF16)


def _adamw(w, g, m, v):
    m = ADAM_B1 * m + (1.0 - ADAM_B1) * g
    v = ADAM_B2 * v + (1.0 - ADAM_B2) * (g * g)
    m_hat = m / (1.0 - ADAM_B1 ** ADAM_STEP)
    v_hat = v / (1.0 - ADAM_B2 ** ADAM_STEP)
    delta = -ADAM_LR * (m_hat / (jnp.sqrt(v_hat) + ADAM_EPS) + ADAM_WD * w)
    return delta, m, v


def _sum_adamw(parts, w, m, v, name):
    parts = parts if isinstance(parts, (list, tuple)) else [parts]
    R, C = w.shape
    rows = parts[0].shape[1]
    rb = max(b for b in range(16, 257, 16) if rows % b == 0 and R // b >= min(4, R // 16))
    per = rows // rb
    n = len(parts)

    def body(*refs):
        p_refs = refs[:n]
        w_ref, m_ref, v_ref, g_out, d_out, m_out, v_out = refs[n:]
        which = pl.program_id(0) // per

        def total(p_ref):
            g = p_ref[0].astype(F32)
            for d in range(1, N_DEV):
                g = g + p_ref[d].astype(F32)
            return g

        g = total(p_refs[0])
        for j in range(1, n):
            g = jnp.where(which == j, total(p_refs[j]), g)
        delta, mn, vn = _adamw(w_ref[...], g, m_ref[...], v_ref[...])
        g_out[...] = g
        d_out[...] = delta
        m_out[...] = mn
        v_out[...] = vn

    def part_spec(j):
        return pl.BlockSpec((N_DEV, rb, C), lambda i: (0, jnp.clip(i - j * per, 0, per - 1), 0))

    blk = _rows(rb, C)
    out = jax.ShapeDtypeStruct((R, C), F32)
    return pl.pallas_call(
        body, name=name, grid=(R // rb,),
        in_specs=[part_spec(j) for j in range(n)] + [blk, blk, blk],
        out_specs=[blk, blk, blk, blk], out_shape=[out, out, out, out],
        compiler_params=_params("parallel"),
    )(*parts, w, m, v)


_SMALL = [("attn_sinks", "s", 0, 1, None), ("rnn_conv_w", "r", 0, 4, "cols"), ("rnn_conv_b", "r", 4, 1, None),
          ("gate_a_w", "a", 0, D_RNN, None), ("gate_a_b", "r", 5, 1, None), ("gate_x_w", "x", 0, D_RNN, None),
          ("gate_x_b", "r", 6, 1, None), ("lru_lambda", "r", 7, 1, None), ("ln1_g", "d", 0, 1, None),
          ("ln1_b", "d", 1, 1, None), ("ffn_conv_w", "f", 0, 3, "cols"), ("ffn_conv_b", "f", 3, 1, None),
          ("ple_gate_b", "t", 2, 1, None), ("ln2_g", "t", 0, 1, None), ("ln2_b", "t", 1, 1, None)]
_LOSS_ROW = 3


_ACC_COLS = {"t": (0, D_MODEL), "f": (D_MODEL, D_FF), "d": (D_MODEL + D_FF, D_MODEL), "s": (2 * D_MODEL + D_FF, 128),
             "r": (2 * D_MODEL + D_FF + 128, D_RNN)}
_ACC_WIDTH = 2 * D_MODEL + D_FF + 128 + D_RNN


def _small_update(rows_all, gates_all, params):
    flat = [arr for triple in params for arr in triple]
    n_par = len(_SMALL)

    def body(*refs):
        rows_ref, gates_ref = refs[:2]
        p_refs = refs[2:2 + 3 * n_par]
        loss_ref = refs[2 + 3 * n_par]
        o_refs = refs[3 + 3 * n_par:3 + 7 * n_par]
        rows_s, tmp_r, tmp_f = refs[3 + 7 * n_par:]
        me = _dev_index(*_place())
        rows_sum, gates_sum = rows_ref[0], gates_ref[0]
        for d in range(1, N_DEV):
            rows_sum = rows_sum + rows_ref[d]
            gates_sum = gates_sum + gates_ref[d]
        rows_s[...] = rows_sum
        t0 = _ACC_COLS["t"][0]
        loss_ref[...] = rows_s[_LOSS_ROW:_LOSS_ROW + 1, t0:t0 + 128]
        for i, (name, key, row, rows, how) in enumerate(_SMALL):
            w_ref, m_ref, v_ref = p_refs[3 * i:3 * i + 3]
            g_out, d_out, m_out, v_out = o_refs[4 * i:4 * i + 4]
            if key == "a":
                g = gates_sum[:, :HEAD_DIM]
            elif key == "x":
                g = gates_sum[:, HEAD_DIM:]
            elif how == "cols":
                c0, width = _ACC_COLS[key]
                full = rows_s[:, c0:c0 + width]
                shard = width // N_DEV
                mine = full[:, :shard]
                for d in range(1, N_DEV):
                    mine = jnp.where(me == d, full[:, d * shard:(d + 1) * shard], mine)
                tmp = tmp_r if key == "r" else tmp_f
                tmp[...] = mine
                g = tmp[row:row + rows, :]
            else:
                c0, width = _ACC_COLS[key]
                g = rows_s[row:row + rows, c0:c0 + width][:, :w_ref.shape[1]]
            delta, mn, vn = _adamw(w_ref[...], g, m_ref[...], v_ref[...])
            g_out[...] = g
            d_out[...] = delta
            m_out[...] = mn
            v_out[...] = vn

    outs = [jax.ShapeDtypeStruct((1, 128), F32)]
    for w, _, _ in params:
        outs += [jax.ShapeDtypeStruct(w.shape, F32)] * 4
    scratch = [pltpu.VMEM((8, _ACC_WIDTH), F32), pltpu.VMEM((8, D_RNN // N_DEV), F32), pltpu.VMEM((8, D_FF // N_DEV), F32)]
    res = pl.pallas_call(body, name="small_update", out_shape=outs, scratch_shapes=scratch)(rows_all, gates_all, *flat)
    return res[0], [res[1 + 4 * i:5 + 4 * i] for i in range(n_par)]


def kernel(x, p, w_in, attn_sinks, rnn_conv_w, rnn_conv_b, gate_a_w, gate_a_b, gate_x_w, gate_x_b, lru_lambda, w_out, ln1_g, ln1_b, w_ffn_up, ffn_conv_w, ffn_conv_b, w_ffn_down, ple_gate_w, ple_gate_b, ple_proj, ln2_g, ln2_b, loss_target, m_w_in, m_attn_sinks, m_rnn_conv_w, m_rnn_conv_b, m_gate_a_w, m_gate_a_b, m_gate_x_w, m_gate_x_b, m_lru_lambda, m_w_out, m_ln1_g, m_ln1_b, m_w_ffn_up, m_ffn_conv_w, m_ffn_conv_b, m_w_ffn_down, m_ple_gate_w, m_ple_gate_b, m_ple_proj, m_ln2_g, m_ln2_b, v_w_in, v_attn_sinks, v_rnn_conv_w, v_rnn_conv_b, v_gate_a_w, v_gate_a_b, v_gate_x_w, v_gate_x_b, v_lru_lambda, v_w_out, v_ln1_g, v_ln1_b, v_w_ffn_up, v_ffn_conv_w, v_ffn_conv_b, v_w_ffn_down, v_ple_gate_w, v_ple_gate_b, v_ple_proj, v_ln2_g, v_ln2_b):
    from_col_blocks = lambda g: g.transpose(1, 0, 2).reshape(g.shape[1], N_DEV * g.shape[2])

    xs, ps, tgt, sinks = x[0], p[0, 0], loss_target[0], attn_sinks[0]
    wa, wx = _block_diag(gate_a_w[0]), _block_diag(gate_x_w[0])

    conv_cols = jnp.concatenate([rnn_conv_w[0].reshape(1, -1), ffn_conv_w[0].reshape(1, -1)], axis=1)
    n_rc, n_fc = 4 * D_RNN // N_DEV, 3 * D_FF // N_DEV
    ((g_in,),) = _comm_call([_Gather([w_in[0].T.astype(BF16)])], "gather_w_in")
    w_in_full = g_in.reshape(D_IN, D_MODEL)

    (q, k, v, xr, gr), _ = _in_proj(xs, w_in_full)
    (att, lse), (g_out, g_conv) = _attn_fwd(
        q, k, v, sinks,
        comm=_Bcast([w_out[0].astype(BF16), jnp.broadcast_to(conv_cols, (8, n_rc + n_fc))]))
    rcw = from_col_blocks(g_conv[:, 0, :n_rc].reshape(N_DEV, 4, D_RNN // N_DEV))
    fcw = from_col_blocks(g_conv[:, 0, n_rc:].reshape(N_DEV, 3, D_FF // N_DEV))
    (rec, h, *kept), (w_up,) = _rnn_fwd(xr, gr, rcw, rnn_conv_b, wa, wx, gate_a_b, gate_x_b, lru_lambda,
                                        comm=_Gather([w_ffn_up[0].astype(BF16)]))
    w_out_full = g_out.reshape(D_MODEL, D_MODEL)
    (z1, h1, h1b, gate, act, gl, vdgl), (g_down, g_pg, g_pp) = _mix_ln1_up(
        xs, att, rec, w_out_full, ln1_g, ln1_b, w_up, fcw, ffn_conv_b,
        comm=_Bcast([w_ffn_down[0].astype(BF16), ple_gate_w[0].astype(BF16), ple_proj[0].astype(BF16)]))
    dz2b, dpreb, dppb, dgc, dval, dh1p, acc_t = _tail(
        act, gl, vdgl, h1, h1b, ps, tgt, g_down.reshape(D_FF, D_MODEL), g_pg.reshape(D_MODEL, D_MODEL), ple_gate_b,
        from_col_blocks(g_pp), ln2_g, ln2_b)

    gd_down = _weight_grad([dz2b], [act], "down_grad", "rows_t", ts=1024)
    gd_pg = _weight_grad([h1b], [dpreb], "pg_grad", "rows", ts=1024)
    gd_pp = _weight_grad([ps], [dppb], "pp_grad", "cols", ts=1024)
    (dgate, dz1, dz1b, datt, drec, acc_f, acc_d), (r_down, r_pg, r_pp) = _up_bwd(
        dgc, gate, dval, dh1p, z1, w_up, fcw, w_out_full, ln1_g, comm=_Exchange([gd_down, gd_pg, gd_pp]))
    gd_up_top, gd_up_bot = _weight_grad([h1b], [dgate, dval], "up_grad", "cols", halves=True)
    gd_out = _weight_grad([att, rec], [dz1b], "out_grad", "rows", ts=1024)
    (dq, dkc, dkp, dvc, dvp, acc_s), (r_up_top,) = _attn_bwd(q, k, v, lse, datt, sinks, comm=_Exchange([gd_up_top]))
    early = jnp.concatenate([acc_t, acc_f, acc_d], axis=1)
    (dxr, dgr, g_wa, g_wx, acc_r), (r_up_bot, r_out, early_all) = _rnn_bwd(
        xr, gr, h, kept, drec, rcw, wa, wx, lru_lambda, comm=_Multi([_Exchange([gd_up_bot, gd_out]), _Bcast([early])]))
    (dkv, dx), _ = _in_bwd(dq, dkc, dkp, dvc, dvp, dxr, dgr, dz1, w_in_full)
    du_parts = [dq, dkv, dxr, dgr]
    lanes = D_RNN // 128
    late = jnp.concatenate([g_wa, g_wx], axis=1)
    late = jnp.concatenate([late, acc_s, acc_r.reshape(8, lanes, 128).transpose(1, 0, 2).reshape(8 * lanes, 128)], axis=0)
    width = D_MODEL // IN_GRAD_PARTS
    comm, r_parts = _Bcast([late]), []
    for part in range(IN_GRAD_PARTS):
        gd_part, got = _weight_grad(du_parts, [xs], f"in_grad_{part}", "rows", ts=1024, b_window=(part, width), comm=comm)
        if part == 0:
            (late_all,) = got
        else:
            r_parts += got
        comm = _Exchange([gd_part])
    r_parts += _comm_call([comm], "exchange_w_in")[0]
    r_in = jnp.concatenate(r_parts, axis=2)
    acc_r_all = late_all[:, D_RNN + 8:].reshape(N_DEV, lanes, 8, 128).transpose(0, 2, 1, 3).reshape(N_DEV, 8, D_RNN)
    small_parts = (jnp.concatenate([early_all, late_all[:, D_RNN:D_RNN + 8], acc_r_all], axis=2),
                   late_all[:, :D_RNN])

    outs = {}
    res = _sum_adamw(r_in, w_in[0].T, m_w_in[0].T, v_w_in[0].T, "adamw_w_in")
    outs["w_in"] = [r.T[None] for r in res]
    for name, parts, w, m, v in [("w_out", r_out, w_out, m_w_out, v_w_out),
                                 ("w_ffn_up", [r_up_top, r_up_bot], w_ffn_up, m_w_ffn_up, v_w_ffn_up),
                                 ("w_ffn_down", r_down, w_ffn_down, m_w_ffn_down, v_w_ffn_down),
                                 ("ple_gate_w", r_pg, ple_gate_w, m_ple_gate_w, v_ple_gate_w),
                                 ("ple_proj", r_pp, ple_proj, m_ple_proj, v_ple_proj)]:
        res = _sum_adamw(parts, w[0], m[0], v[0], "adamw_" + name)
        outs[name] = [r[None] for r in res]

    given = dict(attn_sinks=(attn_sinks, m_attn_sinks, v_attn_sinks), rnn_conv_w=(rnn_conv_w, m_rnn_conv_w, v_rnn_conv_w),
                 rnn_conv_b=(rnn_conv_b, m_rnn_conv_b, v_rnn_conv_b), gate_a_w=(gate_a_w, m_gate_a_w, v_gate_a_w),
                 gate_a_b=(gate_a_b, m_gate_a_b, v_gate_a_b), gate_x_w=(gate_x_w, m_gate_x_w, v_gate_x_w),
                 gate_x_b=(gate_x_b, m_gate_x_b, v_gate_x_b), lru_lambda=(lru_lambda, m_lru_lambda, v_lru_lambda),
                 ln1_g=(ln1_g, m_ln1_g, v_ln1_g), ln1_b=(ln1_b, m_ln1_b, v_ln1_b),
                 ffn_conv_w=(ffn_conv_w, m_ffn_conv_w, v_ffn_conv_w), ffn_conv_b=(ffn_conv_b, m_ffn_conv_b, v_ffn_conv_b),
                 ple_gate_b=(ple_gate_b, m_ple_gate_b, v_ple_gate_b), ln2_g=(ln2_g, m_ln2_g, v_ln2_g),
                 ln2_b=(ln2_b, m_ln2_b, v_ln2_b))
    as_2d = lambda a: a.reshape(-1, a.shape[-1])
    loss_row, small_res = _small_update(*small_parts, [tuple(as_2d(a) for a in given[n]) for n, *_ in _SMALL])
    loss = loss_row[0, 0]
    for (n, *_), res in zip(_SMALL, small_res):
        outs[n] = [r.reshape(given[n][0].shape) for r in res]

    order = ["w_in", "attn_sinks", "rnn_conv_w", "rnn_conv_b", "gate_a_w", "gate_a_b", "gate_x_w", "gate_x_b",
             "lru_lambda", "w_out", "ln1_g", "ln1_b", "w_ffn_up", "ffn_conv_w", "ffn_conv_b", "w_ffn_down",
             "ple_gate_w", "ple_gate_b", "ple_proj", "ln2_g", "ln2_b"]
    return (loss, dx[None], *[outs[n][0] for n in order], *[outs[n][1] for n in order],
            *[outs[n][2] for n in order], *[outs[n][3] for n in order])
```

```python
import jax
import jax.numpy as jnp
from jax import lax
from jax.experimental import pallas as pl
from jax.experimental.pallas import tpu as pltpu

F32 = jnp.float32
BF16 = jnp.bfloat16

D_MODEL = 1024
D_ATT = 512
D_KV = 128
HEAD_DIM = 64
N_HEADS = 8
N_KV = 2
D_RNN = 512
RNN_BLOCKS = 8
D_IN = 1792
D_FF = 3072
PLE_DIM = 256
QBLK = 128
N_DEV = 8
ALPHA = float(2 ** 0.25)
LN_EPS = 1e-5
LRU_C = 8.0
ADAM_LR, ADAM_B1, ADAM_B2, ADAM_EPS, ADAM_WD, ADAM_STEP = 0.001, 0.9, 0.999, 1e-08, 0.01, 10

V7X_VMEM_LIMIT = 56 * 1024 * 1024
MESH = pl.DeviceIdType.MESH


def _params(*sem, vmem=V7X_VMEM_LIMIT):
    return pltpu.CompilerParams(dimension_semantics=sem or None, vmem_limit_bytes=vmem)


def _resident(shape):
    return pl.BlockSpec(shape, lambda *_: (0,) * len(shape), pipeline_mode=pl.Buffered(1))


def _rows(tb, cols):
    return pl.BlockSpec((tb, cols), lambda i: (i, 0))


def _acc(shape):
    return pl.BlockSpec(shape, lambda *_: (0,) * len(shape))


def _dot(a, b):
    return jnp.dot(a, b, preferred_element_type=F32)


def _dot_nt(a, b):
    return lax.dot_general(a, b, (((1,), (1,)), ((), ())), preferred_element_type=F32)


def _dot_tn(a, b):
    return lax.dot_general(a, b, (((0,), (0,)), ((), ())), preferred_element_type=F32)


def _sigmoid(x):
    return 1.0 / (1.0 + jnp.exp(-x))


_GELU_C = 0.7978845608028654
_GELU_K = 0.044715


def _gelu_and_grad(x):
    u = x * x
    t = jnp.tanh(x * (_GELU_C + (_GELU_C * _GELU_K) * u))
    hp = 0.5 + 0.5 * t
    dg = hp + x * (0.5 - 0.5 * (t * t)) * (_GELU_C + (3.0 * _GELU_C * _GELU_K) * u)
    return x * hp, dg


def _gelu(x):
    return 0.5 * x * (1.0 + jnp.tanh(_GELU_C * (x + _GELU_K * x * x * x)))


def _ln_stats(z):
    mu = jnp.mean(z, axis=-1, keepdims=True)
    zc = z - mu
    var = jnp.mean(zc * zc, axis=-1, keepdims=True)
    rstd = lax.rsqrt(var + LN_EPS)
    return zc * rstd, rstd


def _ln_bwd(dy, xhat, rstd, g):
    dxh = dy * g
    m1 = jnp.mean(dxh, axis=-1, keepdims=True)
    m2 = jnp.mean(dxh * xhat, axis=-1, keepdims=True)
    return rstd * (dxh - m1 - xhat * m2)


def _softplus_neg(lam):
    u = jnp.exp(-jnp.abs(lam))
    w = 1.0 + u
    d = w - 1.0
    log1p_u = jnp.where(d == 0.0, u, jnp.log(w) * (u / jnp.where(d == 0.0, 1.0, d)))
    return jnp.maximum(-lam, 0.0) + log1p_u


def _shift_down(x, halo, s):
    xs = pltpu.roll(x, s, 0)
    hs = pltpu.roll(halo, s, 0)
    row8 = lax.broadcasted_iota(jnp.int32, hs.shape, 0)
    first = jnp.where(row8 < s, hs, xs[:8])
    return jnp.concatenate([first, xs[8:]], axis=0)


def _shift_up(x, halo, s):
    n = x.shape[0]
    xs = pltpu.roll(x, n - s, 0)
    hs = pltpu.roll(halo, 8 - s, 0)
    row8 = lax.broadcasted_iota(jnp.int32, hs.shape, 0)
    last = jnp.where(row8 >= 8 - s, hs, xs[n - 8:])
    return jnp.concatenate([xs[:n - 8], last], axis=0)


def _row_sum(x):
    return jnp.sum(x, axis=0, keepdims=True)


def _put_rows(acc_ref, rows):
    row8 = lax.broadcasted_iota(jnp.int32, acc_ref.shape, 0)
    upd = jnp.zeros(acc_ref.shape, F32)
    for r, vec in enumerate(rows):
        upd = jnp.where(row8 == r, vec, upd)
    acc_ref[...] += upd


def _place():
    return lax.axis_index("x"), lax.axis_index("y"), lax.axis_index("c")


def _dev_index(px, py, pc):
    return 4 * px + 2 * py + pc


_ANY = pl.BlockSpec(memory_space=pl.ANY)


class _Gather:
    def __init__(self, arrays):
        self.arrays = list(arrays)
        self.n = len(self.arrays)

    def out_shape(self):
        return [jax.ShapeDtypeStruct((N_DEV,) + s.shape, s.dtype) for s in self.arrays]

    def scratch(self):
        return [pltpu.SemaphoreType.DMA((self.n, 7)), pltpu.SemaphoreType.DMA((self.n, 7)),
                pltpu.SemaphoreType.DMA((self.n,))]

    def _parts(self, ins, outs, sems):
        send_sems, recv_sems, local_sems = sems
        x, y, c = _place()
        me, sibling = (x, y, c), (x, y, 1 - c)
        chips = [(1 - x, y), (x, 1 - y), (1 - x, 1 - y)]

        def copy(a, k, block, to, src=None):
            rows = outs[a].at[_dev_index(*block)]
            return pltpu.make_async_remote_copy(
                src_ref=rows if src is None else src, dst_ref=rows, send_sem=send_sems.at[a, k],
                recv_sem=recv_sems.at[a, k], device_id=to, device_id_type=MESH)

        rng = range(self.n)
        mine = [pltpu.make_async_copy(ins[a], outs[a].at[_dev_index(*me)], local_sems.at[a]) for a in rng]
        first = [copy(a, 0, me, sibling, src=ins[a]) for a in rng]
        first += [copy(a, 1 + j, me, (*chip, c), src=ins[a]) for j, chip in enumerate(chips) for a in rng]
        landed = [copy(a, 1 + j, (*chip, c), me) for j, chip in enumerate(chips) for a in rng]
        passed = [copy(a, 4 + j, (*chip, c), sibling) for j, chip in enumerate(chips) for a in rng]
        from_sibling = [copy(a, 0, sibling, me) for a in rng]
        from_sibling += [copy(a, 4 + j, (*chip, 1 - c), me) for j, chip in enumerate(chips) for a in rng]
        return mine, first, landed, passed, from_sibling

    def start(self, ins, outs, sems):
        mine, first, _, _, _ = self._parts(ins, outs, sems)
        for cp in mine + first:
            cp.start()

    def forward(self, ins, outs, sems):
        _, _, landed, passed, _ = self._parts(ins, outs, sems)
        for got, fwd in zip(landed, passed):
            got.wait_recv()
            fwd.start()

    def finish(self, ins, outs, sems):
        mine, first, _, passed, from_sibling = self._parts(ins, outs, sems)
        for cp in from_sibling:
            cp.wait_recv()
        for cp in first + passed:
            cp.wait_send()
        for cp in mine:
            cp.wait()

    def before(self, ins, outs, sems, step, nsteps):
        pl.when(step == 0)(lambda: self.start(ins, outs, sems))
        pl.when(step == (7 * nsteps) // 8)(lambda: self.forward(ins, outs, sems))

    def after(self, ins, outs, sems, step, nsteps):
        pl.when(step == nsteps - 1)(lambda: self.finish(ins, outs, sems))


class _Exchange:
    def __init__(self, arrays):
        self.arrays = list(arrays)
        self.n = len(self.arrays)

    def out_shape(self):
        return [jax.ShapeDtypeStruct(b.shape, b.dtype) for b in self.arrays]

    def scratch(self):
        return [pltpu.SemaphoreType.DMA((self.n, 7)), pltpu.SemaphoreType.DMA((self.n, 7)),
                pltpu.SemaphoreType.DMA((self.n,))]

    def _parts(self, ins, outs, sems):
        send_sems, recv_sems, local_sems = sems
        x, y, c = _place()
        me = _dev_index(x, y, c)
        peers = [(x ^ (k >> 2), y ^ ((k >> 1) & 1), c ^ (k & 1)) for k in range(1, N_DEV)]
        rng = range(self.n)
        mine = [pltpu.make_async_copy(ins[a].at[me], outs[a].at[me], local_sems.at[a]) for a in rng]
        sent = [pltpu.make_async_remote_copy(
            src_ref=ins[a].at[_dev_index(*to)], dst_ref=outs[a].at[me], send_sem=send_sems.at[a, k],
            recv_sem=recv_sems.at[a, k], device_id=to, device_id_type=MESH) for k, to in enumerate(peers) for a in rng]
        arrivals = [pltpu.make_async_remote_copy(
            src_ref=ins[a].at[me], dst_ref=outs[a].at[_dev_index(*frm)], send_sem=send_sems.at[a, k],
            recv_sem=recv_sems.at[a, k], device_id=frm, device_id_type=MESH) for k, frm in enumerate(peers) for a in rng]
        return mine, sent, arrivals

    def start(self, ins, outs, sems):
        mine, sent, _ = self._parts(ins, outs, sems)
        for cp in mine + sent:
            cp.start()

    def finish(self, ins, outs, sems):
        mine, sent, arrivals = self._parts(ins, outs, sems)
        for cp in arrivals:
            cp.wait_recv()
        for cp in sent:
            cp.wait_send()
        for cp in mine:
            cp.wait()

    def before(self, ins, outs, sems, step, nsteps):
        pl.when(step == 0)(lambda: self.start(ins, outs, sems))

    def after(self, ins, outs, sems, step, nsteps):
        pl.when(step == nsteps - 1)(lambda: self.finish(ins, outs, sems))


class _Bcast(_Exchange):
    def out_shape(self):
        return [jax.ShapeDtypeStruct((N_DEV,) + s.shape, s.dtype) for s in self.arrays]

    def _parts(self, ins, outs, sems):
        send_sems, recv_sems, local_sems = sems
        x, y, c = _place()
        me = _dev_index(x, y, c)
        peers = [(x ^ (k >> 2), y ^ ((k >> 1) & 1), c ^ (k & 1)) for k in range(1, N_DEV)]
        rng = range(self.n)
        mine = [pltpu.make_async_copy(ins[a], outs[a].at[me], local_sems.at[a]) for a in rng]
        sent = [pltpu.make_async_remote_copy(
            src_ref=ins[a], dst_ref=outs[a].at[me], send_sem=send_sems.at[a, k], recv_sem=recv_sems.at[a, k],
            device_id=to, device_id_type=MESH) for k, to in enumerate(peers) for a in rng]
        arrivals = [pltpu.make_async_remote_copy(
            src_ref=ins[a], dst_ref=outs[a].at[_dev_index(*frm)], send_sem=send_sems.at[a, k],
            recv_sem=recv_sems.at[a, k], device_id=frm, device_id_type=MESH) for k, frm in enumerate(peers) for a in rng]
        return mine, sent, arrivals


class _Multi:
    def __init__(self, comms):
        self.comms = list(comms)
        self.arrays = [arr for c in self.comms for arr in c.arrays]
        self.n = len(self.arrays)

    def out_shape(self):
        return [s for c in self.comms for s in c.out_shape()]

    def scratch(self):
        return [s for c in self.comms for s in c.scratch()]

    def _each(self, ins, outs, sems):
        a = 0
        for j, c in enumerate(self.comms):
            yield c, ins[a:a + c.n], outs[a:a + c.n], sems[3 * j:3 * j + 3]
            a += c.n

    def before(self, ins, outs, sems, step, nsteps):
        for c, ci, co, cs in self._each(ins, outs, sems):
            c.before(ci, co, cs, step, nsteps)

    def after(self, ins, outs, sems, step, nsteps):
        for c, ci, co, cs in self._each(ins, outs, sems):
            c.after(ci, co, cs, step, nsteps)


def _comm_call(comms, name):
    ns = [c.n for c in comms]
    n = sum(ns)

    def body(*refs):
        parts, a, s = [], 0, 2 * n
        for c in comms:
            parts.append((c, refs[a:a + c.n], refs[n + a:n + a + c.n], refs[s:s + 3]))
            a, s = a + c.n, s + 3
        for c, ins, outs, sems in parts:
            c.start(ins, outs, sems)
        for c, ins, outs, sems in parts:
            if isinstance(c, _Gather):
                c.forward(ins, outs, sems)
        for c, ins, outs, sems in parts:
            c.finish(ins, outs, sems)

    res = pl.pallas_call(
        body, name=name, in_specs=[_ANY] * n, out_specs=[_ANY] * n,
        out_shape=[s for c in comms for s in c.out_shape()], scratch_shapes=[s for c in comms for s in c.scratch()],
    )(*[arr for c in comms for arr in c.arrays])
    out, a = [], 0
    for k in ns:
        out.append(res[a:a + k])
        a += k
    return out


def _pcall(body, args, *, name, grid, in_specs, out_specs, out_shape, scratch_shapes=(), sem="parallel", comm=None,
           step_axis=0):
    sem = (sem,) * len(grid) if isinstance(sem, str) else sem
    if comm is None:
        res = pl.pallas_call(body, name=name, grid=grid, in_specs=in_specs, out_specs=out_specs, out_shape=out_shape,
                             scratch_shapes=list(scratch_shapes), compiler_params=_params(*sem))(*args)
        return res, []
    n_in, n_out, n_scr, n = len(in_specs), len(out_specs), len(scratch_shapes), comm.n
    nsteps = grid[step_axis]
    assert all(g == 1 for ax, g in enumerate(grid) if ax != step_axis)

    def hosted(*refs):
        ins, cin = refs[:n_in], refs[n_in:n_in + n]
        o0 = n_in + n
        outs, cout = refs[o0:o0 + n_out], refs[o0 + n_out:o0 + n_out + n]
        s0 = o0 + n_out + n
        scr, sems = refs[s0:s0 + n_scr], refs[s0 + n_scr:]
        step = pl.program_id(step_axis)
        comm.before(cin, cout, sems, step, nsteps)
        body(*ins, *outs, *scr)
        comm.after(cin, cout, sems, step, nsteps)

    res = pl.pallas_call(
        hosted, name=name, grid=grid, in_specs=list(in_specs) + [_ANY] * n, out_specs=list(out_specs) + [_ANY] * n,
        out_shape=list(out_shape) + comm.out_shape(), scratch_shapes=list(scratch_shapes) + comm.scratch(),
        compiler_params=_params(*(("arbitrary",) * len(grid))))(*args, *comm.arrays)
    return res[:n_out], res[n_out:]


def _in_proj(x, w_in_t, comm=None):
    S = x.shape[0]
    tb = min(512, S)

    def body(x_ref, w_ref, q_ref, k_ref, v_ref, xr_ref, gr_ref):
        u = _dot_nt(x_ref[...].astype(BF16), w_ref[...])
        q_ref[...] = (u[:, :D_ATT] * (HEAD_DIM ** -0.5)).astype(BF16)
        k_ref[...] = u[:, D_ATT:D_ATT + D_KV].astype(BF16)
        v_ref[...] = u[:, D_ATT + D_KV:D_ATT + 2 * D_KV].astype(BF16)
        xr_ref[...] = u[:, D_ATT + 2 * D_KV:D_ATT + 2 * D_KV + D_RNN]
        gr_ref[...] = u[:, D_ATT + 2 * D_KV + D_RNN:]

    return _pcall(
        body, (x, w_in_t), name="in_proj", grid=(S // tb,), comm=comm,
        in_specs=[_rows(tb, D_MODEL), _resident((D_IN, D_MODEL))],
        out_specs=[_rows(tb, D_ATT), _rows(tb, D_KV), _rows(tb, D_KV), _rows(tb, D_RNN), _rows(tb, D_RNN)],
        out_shape=[jax.ShapeDtypeStruct((S, D_ATT), BF16), jax.ShapeDtypeStruct((S, D_KV), BF16),
                   jax.ShapeDtypeStruct((S, D_KV), BF16), jax.ShapeDtypeStruct((S, D_RNN), F32),
                   jax.ShapeDtypeStruct((S, D_RNN), F32)])


GROUP = N_HEADS // N_KV


def _band_mask(i):
    qi = lax.broadcasted_iota(jnp.int32, (GROUP * QBLK, 2 * QBLK), 0) & (QBLK - 1)
    sj = lax.broadcasted_iota(jnp.int32, (GROUP * QBLK, 2 * QBLK), 1)
    return (sj > qi) & (sj <= qi + QBLK) & ((sj >= QBLK) | (i > 0))


def _stack_heads(x, g):
    return jnp.concatenate([x[:, (g * GROUP + hh) * HEAD_DIM:(g * GROUP + hh + 1) * HEAD_DIM] for hh in range(GROUP)],
                           axis=0)


def _unstack_heads(x4):
    return [x4[hh * QBLK:(hh + 1) * QBLK] for hh in range(GROUP)]


def _sink_column(sink_ref, g):
    head = lax.broadcasted_iota(jnp.int32, (GROUP * QBLK, 1), 0) // QBLK
    col = jnp.full((GROUP * QBLK, 1), sink_ref[g * GROUP], F32)
    for hh in range(1, GROUP):
        col = jnp.where(head == hh, sink_ref[g * GROUP + hh], col)
    return col


ATT_STEP = 4
IN_GRAD_PARTS = 2


def _attn_specs(nq=1):
    cur = lambda i: (i, 0)
    prev = lambda i: (jnp.maximum(nq * i - 1, 0), 0)
    return [pl.BlockSpec((nq * QBLK, D_KV), cur), pl.BlockSpec((QBLK, D_KV), prev),
            pl.BlockSpec((nq * QBLK, D_KV), cur), pl.BlockSpec((QBLK, D_KV), prev)]


def _attn_fwd(q, k, v, sinks, comm=None):
    S = q.shape[0]
    nq = min(ATT_STEP, S // QBLK)

    def body(sink_ref, q_ref, kc_ref, kp_ref, vc_ref, vp_ref, o_ref, lse_ref):
        first = pl.program_id(0) * nq
        kall = jnp.concatenate([kp_ref[...], kc_ref[...]], axis=0)
        vall = jnp.concatenate([vp_ref[...], vc_ref[...]], axis=0)
        for b in range(nq):
            valid = _band_mask(first + b)
            rows = slice(b * QBLK, (b + 1) * QBLK)
            keys = slice(b * QBLK, (b + 2) * QBLK)
            qv = q_ref[rows, :]
            outs = []
            for g in range(N_KV):
                kcat = kall[keys, g * HEAD_DIM:(g + 1) * HEAD_DIM]
                vcat = vall[keys, g * HEAD_DIM:(g + 1) * HEAD_DIM]
                s = jnp.where(valid, _dot_nt(_stack_heads(qv, g), kcat), -1e30)
                sink = _sink_column(sink_ref, g)
                m = jnp.maximum(jnp.max(s, axis=1, keepdims=True), sink)
                p = jnp.exp(s - m)
                l = jnp.sum(p, axis=1, keepdims=True) + jnp.exp(sink - m)
                outs += _unstack_heads(_dot(p.astype(BF16), vcat) / l)
                lse_ref[(b * N_KV + g) * GROUP * QBLK:(b * N_KV + g + 1) * GROUP * QBLK, :] = m + jnp.log(l)
            o_ref[rows, :] = jnp.concatenate(outs, axis=1).astype(BF16)

    lse_rows = nq * N_HEADS * QBLK
    return _pcall(
        body, (sinks, q, k, k, v, v), name="attn_fwd", grid=(S // (nq * QBLK),), comm=comm,
        in_specs=[pl.BlockSpec(memory_space=pltpu.SMEM), _rows(nq * QBLK, D_ATT)] + _attn_specs(nq),
        out_specs=[_rows(nq * QBLK, D_ATT), _rows(lse_rows, 1)],
        out_shape=[jax.ShapeDtypeStruct((S, D_ATT), BF16), jax.ShapeDtypeStruct((S * N_HEADS, 1), F32)])


def _w_rows(w_ref):
    return [w_ref[k:k + 1, :] for k in range(w_ref.shape[0])]


def _conv4(x, halo, w, b):
    y = b + w[3] * x
    for s in (1, 2, 3):
        y = y + w[3 - s] * _shift_down(x, halo, s)
    return y


def _rnn_gates(xc, wa, wx, ba, bx, sp):
    xcb = xc.astype(BF16)
    r = _sigmoid(_dot(xcb, wa) + ba)
    ig = _sigmoid(_dot(xcb, wx) + bx)
    la = -LRU_C * r * sp
    a = jnp.exp(la)
    t = jnp.tanh(la)
    f = jnp.sqrt(-2.0 * t / (1.0 - t))
    return r, ig, a, f


def _rnn_fwd(xr, gr, conv_w, conv_b, wa, wx, ba, bx, lam, comm=None):
    S = xr.shape[0]
    tb = min(256, S)

    def body(xr_ref, gr_ref, cw_ref, cb_ref, wa_ref, wx_ref, ba_ref, bx_ref, lam_ref, rec_ref, h_ref,
             xc_ref, r_ref, ig_ref, a_ref, f_ref, halo_s, hc_s, a_s, b_s):
        @pl.when(pl.program_id(0) == 0)
        def _():
            halo_s[...] = jnp.zeros_like(halo_s)
            hc_s[...] = jnp.zeros_like(hc_s)

        x = xr_ref[...]
        xc = _conv4(x, halo_s[...], _w_rows(cw_ref), cb_ref[...])
        halo_s[...] = x[tb - 8:]
        r, ig, a, f = _rnn_gates(xc, wa_ref[...], wx_ref[...], ba_ref[...], bx_ref[...], _softplus_neg(lam_ref[...]))
        xc_ref[...] = xc
        r_ref[...] = r
        ig_ref[...] = ig
        a_ref[...] = a
        f_ref[...] = f
        a_s[...] = a
        b_s[...] = f * ig * xc
        row8 = lax.broadcasted_iota(jnp.int32, (8, D_RNN), 0)

        def tile(t, hc):
            o = pl.multiple_of(t * 8, 8)
            at = a_s[pl.ds(o, 8), :]
            bt = b_s[pl.ds(o, 8), :]
            for s in (1, 2, 4):
                keep = row8 >= s
                a_sh = jnp.where(keep, pltpu.roll(at, s, 0), 1.0)
                b_sh = jnp.where(keep, pltpu.roll(bt, s, 0), 0.0)
                bt = at * b_sh + bt
                at = at * a_sh
            ht = at * hc + bt
            b_s[pl.ds(o, 8), :] = ht
            return _row_sum(jnp.where(row8 == 7, ht, 0.0))

        hc_s[0:1, :] = lax.fori_loop(0, tb // 8, tile, hc_s[0:1, :], unroll=2)
        h = b_s[...]
        h_ref[...] = h
        rec_ref[...] = (h * _gelu(gr_ref[...])).astype(BF16)

    vec = _resident((1, D_RNN))
    kept = jax.ShapeDtypeStruct((S, D_RNN), F32)
    return _pcall(
        body, (xr, gr, conv_w, conv_b, wa, wx, ba, bx, lam), name="rnn_fwd", grid=(S // tb,), sem="arbitrary", comm=comm,
        in_specs=[_rows(tb, D_RNN), _rows(tb, D_RNN), _resident((4, D_RNN)), vec,
                  _resident((D_RNN, D_RNN)), _resident((D_RNN, D_RNN)), vec, vec, vec],
        out_specs=[_rows(tb, D_RNN)] * 7,
        out_shape=[jax.ShapeDtypeStruct((S, D_RNN), BF16), kept, kept, kept, kept, kept, kept],
        scratch_shapes=[pltpu.VMEM((8, D_RNN), F32), pltpu.VMEM((8, D_RNN), F32),
                        pltpu.VMEM((tb, D_RNN), F32), pltpu.VMEM((tb, D_RNN), F32)])


def _mix_ln1_up(x, att, rec, w_out, ln1_g, ln1_b, w_up, fcw, fcb, comm=None):
    S = x.shape[0]
    tb = min(256, S)
    nblk, _, wblk = w_up.shape
    half = nblk // 2

    def body(x_ref, att_ref, rec_ref, wo_ref, g_ref, b_ref, wu_ref, fcw_ref, fcb_ref,
             z1_ref, h1_ref, h1b_ref, gate_ref, act_ref, gl_ref, vdgl_ref, halo_s):
        @pl.when(pl.program_id(0) == 0)
        def _():
            halo_s[...] = jnp.zeros_like(halo_s)

        z1 = ALPHA * x_ref[...] + _dot(att_ref[...], wo_ref[:D_ATT, :]) + _dot(rec_ref[...], wo_ref[D_ATT:, :])
        z1_ref[...] = z1
        xhat, _ = _ln_stats(z1)
        h1 = xhat * g_ref[...] + b_ref[...]
        h1_ref[...] = h1
        h1b = h1.astype(BF16)
        h1b_ref[...] = h1b
        for jj in range(half):
            cols = slice(jj * wblk, (jj + 1) * wblk)
            gate = _dot(h1b, wu_ref[jj])
            val = _dot(h1b, wu_ref[jj + half])
            halo = halo_s[:, cols]
            conv = (fcb_ref[:, cols] + fcw_ref[2:3, cols] * gate + fcw_ref[1:2, cols] * _shift_down(gate, halo, 1)
                    + fcw_ref[0:1, cols] * _shift_down(gate, halo, 2))
            halo_s[:, cols] = gate[tb - 8:]
            gl, dgl = _gelu_and_grad(conv)
            gate_ref[:, cols] = gate.astype(BF16)
            act_ref[:, cols] = (gl * val).astype(BF16)
            gl_ref[:, cols] = gl.astype(BF16)
            vdgl_ref[:, cols] = (val * dgl).astype(BF16)

    vec = _resident((1, D_MODEL))
    wide = jax.ShapeDtypeStruct((S, D_FF), BF16)
    return _pcall(
        body, (x, att, rec, w_out, ln1_g, ln1_b, w_up, fcw, fcb), name="mix_ln1_up", grid=(S // tb,),
        sem="arbitrary", comm=comm,
        in_specs=[_rows(tb, D_MODEL), _rows(tb, D_ATT), _rows(tb, D_RNN), _resident((D_MODEL, D_MODEL)), vec, vec,
                  _resident(w_up.shape), _resident((3, D_FF)), _resident((1, D_FF))],
        out_specs=[_rows(tb, D_MODEL), _rows(tb, D_MODEL), _rows(tb, D_MODEL)] + [_rows(tb, D_FF)] * 4,
        out_shape=[jax.ShapeDtypeStruct((S, D_MODEL), F32), jax.ShapeDtypeStruct((S, D_MODEL), F32),
                   jax.ShapeDtypeStruct((S, D_MODEL), BF16), wide, wide, wide, wide],
        scratch_shapes=[pltpu.VMEM((8, D_FF), F32)])


def _tail(act, gl, vdgl, h1, h1b, p, tgt, w_down, w_pg, b_pg, w_pp, ln2_g, ln2_b):
    S = h1.shape[0]
    tb = min(256, S)

    def body(act_ref, gl_ref, vdgl_ref, h1_ref, h1b_ref, p_ref, t_ref, wd_ref, wpg_ref, bpg_ref, wpp_ref, g2_ref, b2_ref,
             dz2_ref, dpre_ref, dpp_ref, dgc_ref, dval_ref, dh1_ref, acc_ref):
        i = pl.program_id(0)

        @pl.when(i == 0)
        def _():
            acc_ref[...] = jnp.zeros_like(acc_ref)

        ffn = _dot(act_ref[...], wd_ref[...])
        h1 = h1_ref[...]
        sg = _sigmoid(_dot(h1b_ref[...], wpg_ref[...]) + bpg_ref[...])
        pp = _dot(p_ref[...].astype(BF16), wpp_ref[...])
        z2 = ALPHA * h1 + ffn + sg * pp
        xhat2, rstd2 = _ln_stats(z2)
        y = xhat2 * g2_ref[...] + b2_ref[...]
        err = y - t_ref[...]
        dy = err * (1.0 / D_MODEL)
        loss = 0.5 * jnp.sum(jnp.sum(err * err, axis=1, keepdims=True), axis=0, keepdims=True) * (1.0 / D_MODEL)
        dz2 = _ln_bwd(dy, xhat2, rstd2, g2_ref[...])
        dz2b = dz2.astype(BF16)
        dz2_ref[...] = dz2b
        dpre = dz2 * pp * sg * (1.0 - sg)
        dpreb = dpre.astype(BF16)
        dpre_ref[...] = dpreb
        dpp_ref[...] = (dz2 * sg).astype(BF16)
        dh1_ref[...] = ALPHA * dz2 + _dot_nt(dpreb, wpg_ref[...])
        dactb = _dot_nt(dz2b, wd_ref[...]).astype(BF16)
        dval_ref[...] = dactb * gl_ref[...]
        dgc_ref[...] = dactb * vdgl_ref[...]
        _put_rows(acc_ref, [_row_sum(dy * xhat2), _row_sum(dy), _row_sum(dpre),
                            jnp.broadcast_to(loss, (1, D_MODEL))])

    vec = _resident((1, D_MODEL))
    return pl.pallas_call(
        body, name="tail", grid=(S // tb,),
        in_specs=[_rows(tb, D_FF), _rows(tb, D_FF), _rows(tb, D_FF), _rows(tb, D_MODEL), _rows(tb, D_MODEL),
                  _rows(tb, PLE_DIM), _rows(tb, D_MODEL), _resident((D_FF, D_MODEL)), _resident((D_MODEL, D_MODEL)), vec,
                  _resident((PLE_DIM, D_MODEL)), vec, vec],
        out_specs=[_rows(tb, D_MODEL), _rows(tb, D_MODEL), _rows(tb, D_MODEL), _rows(tb, D_FF),
                   _rows(tb, D_FF), _rows(tb, D_MODEL), _acc((8, D_MODEL))],
        out_shape=[jax.ShapeDtypeStruct((S, D_MODEL), BF16),
                   jax.ShapeDtypeStruct((S, D_MODEL), BF16), jax.ShapeDtypeStruct((S, D_MODEL), BF16),
                   jax.ShapeDtypeStruct((S, D_FF), BF16), jax.ShapeDtypeStruct((S, D_FF), BF16),
                   jax.ShapeDtypeStruct((S, D_MODEL), F32), jax.ShapeDtypeStruct((8, D_MODEL), F32)],
        compiler_params=_params("arbitrary"),
    )(act, gl, vdgl, h1, h1b, p, tgt, w_down, w_pg, b_pg, w_pp, ln2_g, ln2_b)


def _weight_grad(a_list, b_list, name, layout, ts=512, comm=None, b_window=None, halves=False):
    S = a_list[0].shape[0]
    ms = [a.shape[1] for a in a_list]
    M, nb = sum(ms), len(b_list)
    win, Nb = b_window if b_window else (0, b_list[0].shape[1])
    ts = min(ts, S)
    nk = S // ts
    per_b = N_DEV // nb
    na = len(a_list)

    n_out = 2 if halves else 1
    assert layout == "cols" or not halves

    def body(*refs):
        a_refs, b_refs, o_refs, acc_ref = refs[:na], refs[na:na + nb], refs[na + nb:na + nb + n_out], refs[-1]
        o_ref = o_refs[0]
        j, k = pl.program_id(0), pl.program_id(1)

        @pl.when(k == 0)
        def _():
            acc_ref[...] = jnp.zeros_like(acc_ref)

        for jj in range(nb):
            @pl.when(j == jj)
            def _():
                b = b_refs[jj][...].astype(BF16)
                off = 0
                for a_ref, m in zip(a_refs, ms):
                    acc_ref[off:off + m, :] += _dot_tn(a_ref[...].astype(BF16), b)
                    off += m

        @pl.when(k == nk - 1)
        def _():
            for d in range(per_b):
                if layout == "rows":
                    o_ref[d] = acc_ref[d * (M // N_DEV):(d + 1) * (M // N_DEV), :].astype(BF16)
                elif layout == "cols" and halves:
                    for o_half, r0 in zip(o_refs, (0, M // 2)):
                        o_half[d] = acc_ref[r0:r0 + M // 2, d * (Nb // per_b):(d + 1) * (Nb // per_b)].astype(BF16)
                elif layout == "cols":
                    o_ref[d] = acc_ref[:, d * (Nb // per_b):(d + 1) * (Nb // per_b)].astype(BF16)
                else:
                    o_ref[d] = acc_ref[:, d * (Nb // per_b):(d + 1) * (Nb // per_b)].T.astype(BF16)

    def b_index(jj):
        return lambda j, k: (jnp.where(j == jj, k, jnp.where(j < jj, 0, nk - 1)), win)

    if layout == "rows":
        assert nb == 1
        blk = (N_DEV, M // N_DEV, Nb)
    elif layout == "cols":
        blk = (per_b, M // n_out, Nb // per_b)
    else:
        blk = (per_b, Nb // per_b, M)
    res, comm_res = _pcall(
        body, (*a_list, *b_list), name=name, grid=(nb, nk), sem="arbitrary", comm=comm, step_axis=1,
        in_specs=[pl.BlockSpec((ts, m), lambda j, k: (k, 0)) for m in ms]
        + [pl.BlockSpec((ts, Nb), b_index(jj)) for jj in range(nb)],
        out_specs=[pl.BlockSpec(blk, lambda j, k: (j, 0, 0))] * n_out,
        out_shape=[jax.ShapeDtypeStruct((N_DEV,) + blk[1:], BF16)] * n_out,
        scratch_shapes=[pltpu.VMEM((M, Nb), F32)])
    res = res if halves else res[0]
    return (res, comm_res) if comm is not None else res


def _up_bwd(dgc, gate, dval, dh1p, z1, w_up, fcw, w_out, ln1_g, comm=None):
    S = z1.shape[0]
    tb = min(256, S)
    t16 = tb // 16
    n16 = S // 16
    nblk, _, wblk = w_up.shape
    half = nblk // 2
    nsteps = S // tb

    def body(dgc_ref, dgn_ref, gc_ref, dval_ref, dh1p_ref, z1_ref, wu_ref, fcw_ref, wo_ref, g1_ref,
             dgate_ref, dz1_ref, dz1b_ref, datt_ref, drec_ref, accf_ref, accd_ref):
        i = pl.program_id(0)

        @pl.when(i == 0)
        def _():
            accf_ref[...] = jnp.zeros_like(accf_ref)
            accd_ref[...] = jnp.zeros_like(accd_ref)

        dg = dgc_ref[...].astype(F32)
        nxt = jnp.where(i < nsteps - 1, dgn_ref[...].astype(F32)[0:8], 0.0)
        w = _w_rows(fcw_ref)
        up1, up2 = _shift_up(dg, nxt, 1), _shift_up(dg, nxt, 2)
        dgate = (w[2] * dg + w[1] * up1 + w[0] * up2).astype(BF16)
        dgate_ref[...] = dgate
        gate = gc_ref[...].astype(F32)
        _put_rows(accf_ref, [_row_sum(up2 * gate), _row_sum(up1 * gate), _row_sum(dg * gate), _row_sum(dg)])

        dh1 = dh1p_ref[...]
        for j in range(nblk):
            src = dgate if j < half else dval_ref[...]
            jj = j % half
            dh1 = dh1 + _dot_nt(src[:, jj * wblk:(jj + 1) * wblk], wu_ref[j])
        xhat1, rstd1 = _ln_stats(z1_ref[...])
        dz1 = _ln_bwd(dh1, xhat1, rstd1, g1_ref[...])
        dz1_ref[...] = dz1
        dz1b = dz1.astype(BF16)
        dz1b_ref[...] = dz1b
        dcat = _dot_nt(dz1b, wo_ref[...])
        datt_ref[...] = dcat[:, :D_ATT].astype(BF16)
        drec_ref[...] = dcat[:, D_ATT:]
        _put_rows(accd_ref, [_row_sum(dh1 * xhat1), _row_sum(dh1)])

    next16 = pl.BlockSpec((16, D_FF), lambda i: (jnp.minimum((i + 1) * t16, n16 - 1), 0))
    return _pcall(
        body, (dgc, dgc, gate, dval, dh1p, z1, w_up, fcw, w_out, ln1_g), name="up_bwd",
        grid=(nsteps,), sem="arbitrary", comm=comm,
        in_specs=[_rows(tb, D_FF), next16, _rows(tb, D_FF), _rows(tb, D_FF), _rows(tb, D_MODEL),
                  _rows(tb, D_MODEL), _resident(w_up.shape), _resident((3, D_FF)),
                  _resident((D_MODEL, D_MODEL)), _resident((1, D_MODEL))],
        out_specs=[_rows(tb, D_FF), _rows(tb, D_MODEL), _rows(tb, D_MODEL), _rows(tb, D_ATT), _rows(tb, D_RNN),
                   _acc((8, D_FF)), _acc((8, D_MODEL))],
        out_shape=[jax.ShapeDtypeStruct((S, D_FF), BF16), jax.ShapeDtypeStruct((S, D_MODEL), F32),
                   jax.ShapeDtypeStruct((S, D_MODEL), BF16), jax.ShapeDtypeStruct((S, D_ATT), BF16),
                   jax.ShapeDtypeStruct((S, D_RNN), F32), jax.ShapeDtypeStruct((8, D_FF), F32),
                   jax.ShapeDtypeStruct((8, D_MODEL), F32)])


def _attn_bwd(q, k, v, lse, do, sinks, comm=None):
    S = q.shape[0]
    grp = N_HEADS // N_KV
    nq = min(ATT_STEP, S // QBLK)

    def body(sink_ref, q_ref, kc_ref, kp_ref, vc_ref, vp_ref, do_ref, lse_ref, dq_ref, dkc_ref, dkp_ref, dvc_ref, dvp_ref,
             ds_ref):
        i = pl.program_id(0)

        @pl.when(i == 0)
        def _():
            ds_ref[...] = jnp.zeros_like(ds_ref)

        row8 = lax.broadcasted_iota(jnp.int32, (8, 128), 0)
        lane8 = lax.broadcasted_iota(jnp.int32, (8, 128), 1)
        dsink = jnp.zeros((8, 128), F32)
        kall = jnp.concatenate([kp_ref[...], kc_ref[...]], axis=0)
        vall = jnp.concatenate([vp_ref[...], vc_ref[...]], axis=0)
        dk_t = [jnp.zeros((D_KV, QBLK), F32) for _ in range(nq + 1)]
        dv_t = [jnp.zeros((D_KV, QBLK), F32) for _ in range(nq + 1)]
        for b in range(nq):
            valid = _band_mask(i * nq + b)
            rows = slice(b * QBLK, (b + 1) * QBLK)
            keys = slice(b * QBLK, (b + 2) * QBLK)
            qv, dov = q_ref[rows, :], do_ref[rows, :]
            dqs, dks, dvs = [], [], []
            for g in range(N_KV):
                kcat = kall[keys, g * HEAD_DIM:(g + 1) * HEAD_DIM]
                vcat = vall[keys, g * HEAD_DIM:(g + 1) * HEAD_DIM]
                q4, do4 = _stack_heads(qv, g), _stack_heads(dov, g)
                s = jnp.where(valid, _dot_nt(q4, kcat), -1e30)
                lse = lse_ref[(b * N_KV + g) * GROUP * QBLK:(b * N_KV + g + 1) * GROUP * QBLK, :]
                p = jnp.exp(s - lse)
                p_sink = jnp.exp(_sink_column(sink_ref, g) - lse)
                dp = _dot_nt(do4, vcat)
                delta = jnp.sum(p * dp, axis=1, keepdims=True)
                dsc = (p * (dp - delta)).astype(BF16)
                dqs += _unstack_heads(_dot(dsc, kcat) * (HEAD_DIM ** -0.5))
                dks.append(_dot_tn(q4, dsc))
                dvs.append(_dot_tn(do4, p.astype(BF16)))
                for hh, part in enumerate(_unstack_heads(-p_sink * delta)):
                    here = (row8 == 0) & (lane8 == g * grp + hh)
                    dsink = dsink + jnp.where(here, jnp.sum(part, axis=0, keepdims=True), 0.0)
            dq_ref[rows, :] = jnp.concatenate(dqs, axis=1).astype(BF16)
            dk2, dv2 = jnp.concatenate(dks, axis=0), jnp.concatenate(dvs, axis=0)
            dk_t[b], dk_t[b + 1] = dk_t[b] + dk2[:, :QBLK], dk_t[b + 1] + dk2[:, QBLK:]
            dv_t[b], dv_t[b + 1] = dv_t[b] + dv2[:, :QBLK], dv_t[b + 1] + dv2[:, QBLK:]
        dkp_ref[...] = dk_t[0].T
        dvp_ref[...] = dv_t[0].T
        for b in range(nq):
            dkc_ref[b * QBLK:(b + 1) * QBLK, :] = dk_t[b + 1].T
            dvc_ref[b * QBLK:(b + 1) * QBLK, :] = dv_t[b + 1].T
        ds_ref[...] += dsink

    nsteps = S // (nq * QBLK)
    cur = jax.ShapeDtypeStruct((S, D_KV), F32)
    prev = jax.ShapeDtypeStruct((nsteps * QBLK, D_KV), F32)
    big = _rows(nq * QBLK, D_ATT)
    return _pcall(
        body, (sinks, q, k, k, v, v, do, lse), name="attn_bwd", grid=(nsteps,), sem="arbitrary", comm=comm,
        in_specs=[pl.BlockSpec(memory_space=pltpu.SMEM), big] + _attn_specs(nq) + [big, _rows(nq * N_HEADS * QBLK, 1)],
        out_specs=[big, _rows(nq * QBLK, D_KV), _rows(QBLK, D_KV), _rows(nq * QBLK, D_KV), _rows(QBLK, D_KV),
                   _acc((8, 128))],
        out_shape=[jax.ShapeDtypeStruct((S, D_ATT), BF16), cur, prev, cur, prev, jax.ShapeDtypeStruct((8, 128), F32)])


def _rnn_bwd(xr, gr, h, kept, drec, conv_w, wa, wx, lam, comm=None):
    S = xr.shape[0]
    tb = min(256, S)
    t8 = tb // 8
    nsteps = S // tb

    def body(xr_ref, xp_ref, gr_ref, h_ref, hp_ref, xc_ref, r_ref, ig_ref, a_ref, f_ref, drec_ref, cw_ref, wa_ref, wx_ref,
             lam_ref, dxr_ref, dgr_ref, gwa_ref, gwx_ref, acc_ref, carry_s, dxc_halo_s, d_s, gwa_s, gwx_s):
        i = pl.program_id(0)
        blk = nsteps - 1 - i

        @pl.when(i == 0)
        def _():
            gwa_s[...] = jnp.zeros_like(gwa_s)
            gwx_s[...] = jnp.zeros_like(gwx_s)
            acc_ref[...] = jnp.zeros_like(acc_ref)
            carry_s[...] = jnp.zeros_like(carry_s)
            dxc_halo_s[...] = jnp.zeros_like(dxc_halo_s)

        x = xr_ref[...]
        xhalo = jnp.where(blk > 0, xp_ref[...], 0.0)
        cw = _w_rows(cw_ref)
        xs = [_shift_down(x, xhalo, 3), _shift_down(x, xhalo, 2), _shift_down(x, xhalo, 1), x]
        xc, r, ig, a, f = xc_ref[...], r_ref[...], ig_ref[...], a_ref[...], f_ref[...]
        sp = _softplus_neg(lam_ref[...])
        hcur = h_ref[...]
        hprev = _shift_down(hcur, jnp.where(blk > 0, hp_ref[...], 0.0), 1)
        gl, dgl = _gelu_and_grad(gr_ref[...])
        drec = drec_ref[...]
        dgr_ref[...] = (drec * hcur * dgl).astype(BF16)
        d_s[...] = drec * gl
        row8 = lax.broadcasted_iota(jnp.int32, (8, D_RNN), 0)

        def tile(t, c):
            o = pl.multiple_of((t8 - 1 - t) * 8, 8)
            a8 = a_ref[pl.ds(o, 8), :]
            dt = d_s[pl.ds(o, 8), :]
            at = jnp.where(row8 == 7, 1.0, pltpu.roll(a8, 7, 0))
            for s in (1, 2, 4):
                keep = row8 < 8 - s
                a_sh = jnp.where(keep, pltpu.roll(at, 8 - s, 0), 1.0)
                d_sh = jnp.where(keep, pltpu.roll(dt, 8 - s, 0), 0.0)
                dt = at * d_sh + dt
                at = at * a_sh
            lt = at * c + dt
            d_s[pl.ds(o, 8), :] = lt
            return _row_sum(jnp.where(row8 == 0, a8 * lt, 0.0))

        carry_s[0:1, :] = lax.fori_loop(0, t8, tile, carry_s[0:1, :], unroll=2)
        lmb = d_s[...]
        a2 = a * a
        dla = lmb * hprev * a - lmb * ig * xc * (a2 / f)
        di = lmb * f * xc
        dr = dla * (-LRU_C) * sp
        dpa = dr * r * (1.0 - r)
        dpx = di * ig * (1.0 - ig)
        dpab = dpa.astype(BF16)
        dpxb = dpx.astype(BF16)
        xcb = xc.astype(BF16)
        gwa_s[...] += _dot_tn(xcb, dpab)
        gwx_s[...] += _dot_tn(xcb, dpxb)

        @pl.when(i == nsteps - 1)
        def _():
            for dense, out in ((gwa_s[...], gwa_ref), (gwx_s[...], gwx_ref)):
                for b in range(RNN_BLOCKS):
                    rows = slice(b * HEAD_DIM, (b + 1) * HEAD_DIM)
                    out[rows, :] = dense[rows, b * HEAD_DIM:(b + 1) * HEAD_DIM]

        dxc = lmb * f * ig + _dot_nt(dpab, wa_ref[...]) + _dot_nt(dpxb, wx_ref[...])
        nxt = dxc_halo_s[...]
        dxr = cw[3] * dxc
        for s in (1, 2, 3):
            dxr = dxr + cw[3 - s] * _shift_up(dxc, nxt, s)
        dxr_ref[...] = dxr.astype(BF16)
        dxc_halo_s[...] = dxc[:8]
        dlam = _row_sum(dla * (-LRU_C) * r) * (-1.0 / (1.0 + jnp.exp(lam_ref[...])))
        _put_rows(acc_ref, [_row_sum(dxc * xs[0]), _row_sum(dxc * xs[1]), _row_sum(dxc * xs[2]), _row_sum(dxc * xs[3]),
                            _row_sum(dxc), _row_sum(dpa), _row_sum(dpx), dlam])

    rev = lambda i: (nsteps - 1 - i, 0)
    prev8 = lambda i: (jnp.maximum((nsteps - 1 - i) * t8 - 1, 0), 0)
    blkspec = pl.BlockSpec((tb, D_RNN), rev)
    halo8 = pl.BlockSpec((8, D_RNN), prev8)
    vec = _resident((1, D_RNN))
    return _pcall(
        body, (xr, xr, gr, h, h, *kept, drec, conv_w, wa, wx, lam), name="rnn_bwd", grid=(nsteps,),
        sem="arbitrary", comm=comm,
        in_specs=[blkspec, halo8, blkspec, blkspec, halo8] + [blkspec] * 6
        + [_resident((4, D_RNN)), _resident((D_RNN, D_RNN)), _resident((D_RNN, D_RNN)), vec],
        out_specs=[blkspec, blkspec, _acc((D_RNN, HEAD_DIM)), _acc((D_RNN, HEAD_DIM)), _acc((8, D_RNN))],
        out_shape=[jax.ShapeDtypeStruct((S, D_RNN), BF16), jax.ShapeDtypeStruct((S, D_RNN), BF16),
                   jax.ShapeDtypeStruct((D_RNN, HEAD_DIM), F32), jax.ShapeDtypeStruct((D_RNN, HEAD_DIM), F32),
                   jax.ShapeDtypeStruct((8, D_RNN), F32)],
        scratch_shapes=[pltpu.VMEM((8, D_RNN), F32), pltpu.VMEM((8, D_RNN), F32), pltpu.VMEM((tb, D_RNN), F32),
                        pltpu.VMEM((D_RNN, D_RNN), F32), pltpu.VMEM((D_RNN, D_RNN), F32)])


def _in_bwd(dq, dkc, dkp, dvc, dvp, dxr, dgr, dz1, w_in, comm=None):
    S = dz1.shape[0]
    tb = min(ATT_STEP * QBLK, S)
    nsteps = S // tb

    def body(dq_ref, dkc_ref, dkn_ref, dvc_ref, dvn_ref, dxr_ref, dgr_ref, dz1_ref, w_ref, dkv_ref, dx_ref):
        last = pl.program_id(0) == nsteps - 1

        def total(cur_ref, next_ref):
            nxt = jnp.where(last, 0.0, next_ref[...])
            tail = cur_ref[tb - QBLK:, :] + nxt
            return jnp.concatenate([cur_ref[:tb - QBLK, :], tail], axis=0) if tb > QBLK else tail

        dkv = jnp.concatenate([total(dkc_ref, dkn_ref), total(dvc_ref, dvn_ref)], axis=1).astype(BF16)
        dkv_ref[...] = dkv
        du = jnp.concatenate([dq_ref[...], dkv, dxr_ref[...], dgr_ref[...]], axis=1)
        dx_ref[...] = ALPHA * dz1_ref[...] + _dot(du, w_ref[...])

    nextp = pl.BlockSpec((QBLK, D_KV), lambda i: (jnp.minimum(i + 1, nsteps - 1), 0))
    return _pcall(
        body, (dq, dkc, dkp, dvc, dvp, dxr, dgr, dz1, w_in), name="in_bwd", grid=(nsteps,), comm=comm,
        in_specs=[_rows(tb, D_ATT), _rows(tb, D_KV), nextp, _rows(tb, D_KV), nextp,
                  _rows(tb, D_RNN), _rows(tb, D_RNN), _rows(tb, D_MODEL), _resident((D_IN, D_MODEL))],
        out_specs=[_rows(tb, 2 * D_KV), _rows(tb, D_MODEL)],
        out_shape=[jax.ShapeDtypeStruct((S, 2 * D_KV), BF16), jax.ShapeDtypeStruct((S, D_MODEL), F32)])


def _block_diag(w):
    eye = jnp.eye(RNN_BLOCKS, dtype=w.dtype)
    return (w[:, :, None, :] * eye[:, None, :, None]).reshape(D_RNN, D_RNN).astype(BF16)


def _adamw(w, g, m, v):
    m = ADAM_B1 * m + (1.0 - ADAM_B1) * g
    v = ADAM_B2 * v + (1.0 - ADAM_B2) * (g * g)
    m_hat = m / (1.0 - ADAM_B1 ** ADAM_STEP)
    v_hat = v / (1.0 - ADAM_B2 ** ADAM_STEP)
    delta = -ADAM_LR * (m_hat / (jnp.sqrt(v_hat) + ADAM_EPS) + ADAM_WD * w)
    return delta, m, v


def _sum_adamw(parts, w, m, v, name):
    parts = parts if isinstance(parts, (list, tuple)) else [parts]
    R, C = w.shape
    rb = R if R <= 256 else (256 if parts[0].shape[1] % 256 == 0 else 128)
    per = parts[0].shape[1] // rb
    assert R % rb == 0 and parts[0].shape[1] % rb == 0
    n = len(parts)

    def body(*refs):
        p_refs = refs[:n]
        w_ref, m_ref, v_ref, g_out, d_out, m_out, v_out = refs[n:]
        which = pl.program_id(0) // per

        def total(p_ref):
            g = p_ref[0].astype(F32)
            for d in range(1, N_DEV):
                g = g + p_ref[d].astype(F32)
            return g

        g = total(p_refs[0])
        for j in range(1, n):
            g = jnp.where(which == j, total(p_refs[j]), g)
        delta, mn, vn = _adamw(w_ref[...], g, m_ref[...], v_ref[...])
        g_out[...] = g
        d_out[...] = delta
        m_out[...] = mn
        v_out[...] = vn

    def part_spec(j):
        return pl.BlockSpec((N_DEV, rb, C), lambda i: (0, jnp.clip(i - j * per, 0, per - 1), 0))

    blk = _rows(rb, C)
    out = jax.ShapeDtypeStruct((R, C), F32)
    return pl.pallas_call(
        body, name=name, grid=(R // rb,),
        in_specs=[part_spec(j) for j in range(n)] + [blk, blk, blk],
        out_specs=[blk, blk, blk, blk], out_shape=[out, out, out, out],
        compiler_params=_params("parallel"),
    )(*parts, w, m, v)


_SMALL = [("attn_sinks", "s", 0, 1, None), ("rnn_conv_w", "r", 0, 4, "cols"), ("rnn_conv_b", "r", 4, 1, None),
          ("gate_a_w", "a", 0, D_RNN, None), ("gate_a_b", "r", 5, 1, None), ("gate_x_w", "x", 0, D_RNN, None),
          ("gate_x_b", "r", 6, 1, None), ("lru_lambda", "r", 7, 1, None), ("ln1_g", "d", 0, 1, None),
          ("ln1_b", "d", 1, 1, None), ("ffn_conv_w", "f", 0, 3, "cols"), ("ffn_conv_b", "f", 3, 1, None),
          ("ple_gate_b", "t", 2, 1, None), ("ln2_g", "t", 0, 1, None), ("ln2_b", "t", 1, 1, None)]
_LOSS_ROW = 3


_ACC_COLS = {"t": (0, D_MODEL), "f": (D_MODEL, D_FF), "d": (D_MODEL + D_FF, D_MODEL), "s": (2 * D_MODEL + D_FF, 128),
             "r": (2 * D_MODEL + D_FF + 128, D_RNN)}
_ACC_WIDTH = 2 * D_MODEL + D_FF + 128 + D_RNN


def _small_update(rows_all, gates_all, params):
    flat = [arr for triple in params for arr in triple]
    n_par = len(_SMALL)

    def body(*refs):
        rows_ref, gates_ref = refs[:2]
        p_refs = refs[2:2 + 3 * n_par]
        loss_ref = refs[2 + 3 * n_par]
        o_refs = refs[3 + 3 * n_par:3 + 7 * n_par]
        rows_s, tmp_r, tmp_f = refs[3 + 7 * n_par:]
        me = _dev_index(*_place())
        rows_sum, gates_sum = rows_ref[0], gates_ref[0]
        for d in range(1, N_DEV):
            rows_sum = rows_sum + rows_ref[d]
            gates_sum = gates_sum + gates_ref[d]
        rows_s[...] = rows_sum
        t0 = _ACC_COLS["t"][0]
        loss_ref[...] = rows_s[_LOSS_ROW:_LOSS_ROW + 1, t0:t0 + 128]
        for i, (name, key, row, rows, how) in enumerate(_SMALL):
            w_ref, m_ref, v_ref = p_refs[3 * i:3 * i + 3]
            g_out, d_out, m_out, v_out = o_refs[4 * i:4 * i + 4]
            if key == "a":
                g = gates_sum[:, :HEAD_DIM]
            elif key == "x":
                g = gates_sum[:, HEAD_DIM:]
            elif how == "cols":
                c0, width = _ACC_COLS[key]
                full = rows_s[:, c0:c0 + width]
                shard = width // N_DEV
                mine = full[:, :shard]
                for d in range(1, N_DEV):
                    mine = jnp.where(me == d, full[:, d * shard:(d + 1) * shard], mine)
                tmp = tmp_r if key == "r" else tmp_f
                tmp[...] = mine
                g = tmp[row:row + rows, :]
            else:
                c0, width = _ACC_COLS[key]
                g = rows_s[row:row + rows, c0:c0 + width][:, :w_ref.shape[1]]
            delta, mn, vn = _adamw(w_ref[...], g, m_ref[...], v_ref[...])
            g_out[...] = g
            d_out[...] = delta
            m_out[...] = mn
            v_out[...] = vn

    outs = [jax.ShapeDtypeStruct((1, 128), F32)]
    for w, _, _ in params:
        outs += [jax.ShapeDtypeStruct(w.shape, F32)] * 4
    scratch = [pltpu.VMEM((8, _ACC_WIDTH), F32), pltpu.VMEM((8, D_RNN // N_DEV), F32), pltpu.VMEM((8, D_FF // N_DEV), F32)]
    res = pl.pallas_call(body, name="small_update", out_shape=outs, scratch_shapes=scratch)(rows_all, gates_all, *flat)
    return res[0], [res[1 + 4 * i:5 + 4 * i] for i in range(n_par)]


def kernel(x, p, w_in, attn_sinks, rnn_conv_w, rnn_conv_b, gate_a_w, gate_a_b, gate_x_w, gate_x_b, lru_lambda, w_out, ln1_g, ln1_b, w_ffn_up, ffn_conv_w, ffn_conv_b, w_ffn_down, ple_gate_w, ple_gate_b, ple_proj, ln2_g, ln2_b, loss_target, m_w_in, m_attn_sinks, m_rnn_conv_w, m_rnn_conv_b, m_gate_a_w, m_gate_a_b, m_gate_x_w, m_gate_x_b, m_lru_lambda, m_w_out, m_ln1_g, m_ln1_b, m_w_ffn_up, m_ffn_conv_w, m_ffn_conv_b, m_w_ffn_down, m_ple_gate_w, m_ple_gate_b, m_ple_proj, m_ln2_g, m_ln2_b, v_w_in, v_attn_sinks, v_rnn_conv_w, v_rnn_conv_b, v_gate_a_w, v_gate_a_b, v_gate_x_w, v_gate_x_b, v_lru_lambda, v_w_out, v_ln1_g, v_ln1_b, v_w_ffn_up, v_ffn_conv_w, v_ffn_conv_b, v_w_ffn_down, v_ple_gate_w, v_ple_gate_b, v_ple_proj, v_ln2_g, v_ln2_b):
    from_col_blocks = lambda g: g.transpose(1, 0, 2).reshape(g.shape[1], N_DEV * g.shape[2])

    xs, ps, tgt, sinks = x[0], p[0, 0], loss_target[0], attn_sinks[0]
    wa, wx = _block_diag(gate_a_w[0]), _block_diag(gate_x_w[0])

    conv_cols = jnp.concatenate([rnn_conv_w[0].reshape(1, -1), ffn_conv_w[0].reshape(1, -1)], axis=1)
    n_rc, n_fc = 4 * D_RNN // N_DEV, 3 * D_FF // N_DEV
    ((g_in,),) = _comm_call([_Gather([w_in[0].T.astype(BF16)])], "gather_w_in")
    w_in_full = g_in.reshape(D_IN, D_MODEL)

    (q, k, v, xr, gr), _ = _in_proj(xs, w_in_full)
    (att, lse), (g_out, g_conv) = _attn_fwd(
        q, k, v, sinks,
        comm=_Multi([_Gather([w_out[0].astype(BF16)]), _Bcast([jnp.broadcast_to(conv_cols, (8, n_rc + n_fc))])]))
    rcw = from_col_blocks(g_conv[:, 0, :n_rc].reshape(N_DEV, 4, D_RNN // N_DEV))
    fcw = from_col_blocks(g_conv[:, 0, n_rc:].reshape(N_DEV, 3, D_FF // N_DEV))
    (rec, h, *kept), (w_up,) = _rnn_fwd(xr, gr, rcw, rnn_conv_b, wa, wx, gate_a_b, gate_x_b, lru_lambda,
                                        comm=_Gather([w_ffn_up[0].astype(BF16)]))
    w_out_full = g_out.reshape(D_MODEL, D_MODEL)
    (z1, h1, h1b, gate, act, gl, vdgl), (g_down, g_pg, g_pp) = _mix_ln1_up(
        xs, att, rec, w_out_full, ln1_g, ln1_b, w_up, fcw, ffn_conv_b,
        comm=_Gather([w_ffn_down[0].astype(BF16), ple_gate_w[0].astype(BF16), ple_proj[0].astype(BF16)]))
    dz2b, dpreb, dppb, dgc, dval, dh1p, acc_t = _tail(
        act, gl, vdgl, h1, h1b, ps, tgt, g_down.reshape(D_FF, D_MODEL), g_pg.reshape(D_MODEL, D_MODEL), ple_gate_b,
        from_col_blocks(g_pp), ln2_g, ln2_b)

    gd_down = _weight_grad([dz2b], [act], "down_grad", "rows_t", ts=1024)
    gd_pg = _weight_grad([h1b], [dpreb], "pg_grad", "rows", ts=1024)
    gd_pp = _weight_grad([ps], [dppb], "pp_grad", "cols", ts=1024)
    (dgate, dz1, dz1b, datt, drec, acc_f, acc_d), (r_down, r_pg, r_pp) = _up_bwd(
        dgc, gate, dval, dh1p, z1, w_up, fcw, w_out_full, ln1_g, comm=_Exchange([gd_down, gd_pg, gd_pp]))
    gd_up_top, gd_up_bot = _weight_grad([h1b], [dgate, dval], "up_grad", "cols", halves=True)
    gd_out = _weight_grad([att, rec], [dz1b], "out_grad", "rows", ts=1024)
    (dq, dkc, dkp, dvc, dvp, acc_s), (r_up_top,) = _attn_bwd(q, k, v, lse, datt, sinks, comm=_Exchange([gd_up_top]))
    early = jnp.concatenate([acc_t, acc_f, acc_d], axis=1)
    (dxr, dgr, g_wa, g_wx, acc_r), (r_up_bot, r_out, early_all) = _rnn_bwd(
        xr, gr, h, kept, drec, rcw, wa, wx, lru_lambda, comm=_Multi([_Exchange([gd_up_bot, gd_out]), _Bcast([early])]))
    (dkv, dx), _ = _in_bwd(dq, dkc, dkp, dvc, dvp, dxr, dgr, dz1, w_in_full)
    du_parts = [dq, dkv, dxr, dgr]
    lanes = D_RNN // 128
    late = jnp.concatenate([g_wa, g_wx], axis=1)
    late = jnp.concatenate([late, acc_s, acc_r.reshape(8, lanes, 128).transpose(1, 0, 2).reshape(8 * lanes, 128)], axis=0)
    width = D_MODEL // IN_GRAD_PARTS
    gd_part = _weight_grad(du_parts, [xs], "in_grad_0", "rows", ts=1024, b_window=(0, width))
    comm, r_parts = _Multi([_Exchange([gd_part]), _Bcast([late])]), []
    for part in range(1, IN_GRAD_PARTS):
        gd_part, got = _weight_grad(du_parts, [xs], f"in_grad_{part}", "rows", ts=1024, b_window=(part, width), comm=comm)
        got = list(got)
        if part == 1:
            late_all = got.pop()
        r_parts += got
        comm = _Exchange([gd_part])
    r_parts += _comm_call([comm], "exchange_w_in")[0]
    r_in = jnp.concatenate(r_parts, axis=2)
    acc_r_all = late_all[:, D_RNN + 8:].reshape(N_DEV, lanes, 8, 128).transpose(0, 2, 1, 3).reshape(N_DEV, 8, D_RNN)
    small_parts = (jnp.concatenate([early_all, late_all[:, D_RNN:D_RNN + 8], acc_r_all], axis=2),
                   late_all[:, :D_RNN])

    outs = {}
    res = _sum_adamw(r_in, w_in[0].T, m_w_in[0].T, v_w_in[0].T, "adamw_w_in")
    outs["w_in"] = [r.T[None] for r in res]
    for name, parts, w, m, v in [("w_out", r_out, w_out, m_w_out, v_w_out),
                                 ("w_ffn_up", [r_up_top, r_up_bot], w_ffn_up, m_w_ffn_up, v_w_ffn_up),
                                 ("w_ffn_down", r_down, w_ffn_down, m_w_ffn_down, v_w_ffn_down),
                                 ("ple_gate_w", r_pg, ple_gate_w, m_ple_gate_w, v_ple_gate_w),
                                 ("ple_proj", r_pp, ple_proj, m_ple_proj, v_ple_proj)]:
        res = _sum_adamw(parts, w[0], m[0], v[0], "adamw_" + name)
        outs[name] = [r[None] for r in res]

    given = dict(attn_sinks=(attn_sinks, m_attn_sinks, v_attn_sinks), rnn_conv_w=(rnn_conv_w, m_rnn_conv_w, v_rnn_conv_w),
                 rnn_conv_b=(rnn_conv_b, m_rnn_conv_b, v_rnn_conv_b), gate_a_w=(gate_a_w, m_gate_a_w, v_gate_a_w),
                 gate_a_b=(gate_a_b, m_gate_a_b, v_gate_a_b), gate_x_w=(gate_x_w, m_gate_x_w, v_gate_x_w),
                 gate_x_b=(gate_x_b, m_gate_x_b, v_gate_x_b), lru_lambda=(lru_lambda, m_lru_lambda, v_lru_lambda),
                 ln1_g=(ln1_g, m_ln1_g, v_ln1_g), ln1_b=(ln1_b, m_ln1_b, v_ln1_b),
                 ffn_conv_w=(ffn_conv_w, m_ffn_conv_w, v_ffn_conv_w), ffn_conv_b=(ffn_conv_b, m_ffn_conv_b, v_ffn_conv_b),
                 ple_gate_b=(ple_gate_b, m_ple_gate_b, v_ple_gate_b), ln2_g=(ln2_g, m_ln2_g, v_ln2_g),
                 ln2_b=(ln2_b, m_ln2_b, v_ln2_b))
    as_2d = lambda a: a.reshape(-1, a.shape[-1])
    loss_row, small_res = _small_update(*small_parts, [tuple(as_2d(a) for a in given[n]) for n, *_ in _SMALL])
    loss = loss_row[0, 0]
    for (n, *_), res in zip(_SMALL, small_res):
        outs[n] = [r.reshape(given[n][0].shape) for r in res]

    order = ["w_in", "attn_sinks", "rnn_conv_w", "rnn_conv_b", "gate_a_w", "gate_a_b", "gate_x_w", "gate_x_b",
             "lru_lambda", "w_out", "ln1_g", "ln1_b", "w_ffn_up", "ffn_conv_w", "ffn_conv_b", "w_ffn_down",
             "ple_gate_w", "ple_gate_b", "ple_proj", "ln2_g", "ln2_b"]
    return (loss, dx[None], *[outs[n][0] for n in order], *[outs[n][1] for n in order],
            *[outs[n][2] for n in order], *[outs[n][3] for n in order])
```

```python
import jax
import jax.numpy as jnp
from jax import lax
from jax.experimental import pallas as pl
from jax.experimental.pallas import tpu as pltpu

F32 = jnp.float32
BF16 = jnp.bfloat16

D_MODEL = 1024
D_ATT = 512
D_KV = 128
HEAD_DIM = 64
N_HEADS = 8
N_KV = 2
D_RNN = 512
RNN_BLOCKS = 8
D_IN = 1792
D_FF = 3072
PLE_DIM = 256
QBLK = 128
N_DEV = 8
ALPHA = float(2 ** 0.25)
LN_EPS = 1e-5
LRU_C = 8.0
ADAM_LR, ADAM_B1, ADAM_B2, ADAM_EPS, ADAM_WD, ADAM_STEP = 0.001, 0.9, 0.999, 1e-08, 0.01, 10

V7X_VMEM_LIMIT = 56 * 1024 * 1024
MESH = pl.DeviceIdType.MESH


def _params(*sem, vmem=V7X_VMEM_LIMIT):
    return pltpu.CompilerParams(dimension_semantics=sem or None, vmem_limit_bytes=vmem)


def _resident(shape):
    return pl.BlockSpec(shape, lambda *_: (0,) * len(shape), pipeline_mode=pl.Buffered(1))


def _rows(tb, cols):
    return pl.BlockSpec((tb, cols), lambda i: (i, 0))


def _acc(shape):
    return pl.BlockSpec(shape, lambda *_: (0,) * len(shape))


def _dot(a, b):
    return jnp.dot(a, b, preferred_element_type=F32)


def _dot_nt(a, b):
    return lax.dot_general(a, b, (((1,), (1,)), ((), ())), preferred_element_type=F32)


def _dot_tn(a, b):
    return lax.dot_general(a, b, (((0,), (0,)), ((), ())), preferred_element_type=F32)


def _sigmoid(x):
    return 1.0 / (1.0 + jnp.exp(-x))


_GELU_C = 0.7978845608028654
_GELU_K = 0.044715


def _gelu_and_grad(x):
    u = x * x
    t = jnp.tanh(x * (_GELU_C + (_GELU_C * _GELU_K) * u))
    hp = 0.5 + 0.5 * t
    dg = hp + x * (0.5 - 0.5 * (t * t)) * (_GELU_C + (3.0 * _GELU_C * _GELU_K) * u)
    return x * hp, dg


def _gelu(x):
    return 0.5 * x * (1.0 + jnp.tanh(_GELU_C * (x + _GELU_K * x * x * x)))


def _ln_stats(z):
    mu = jnp.mean(z, axis=-1, keepdims=True)
    zc = z - mu
    var = jnp.mean(zc * zc, axis=-1, keepdims=True)
    rstd = lax.rsqrt(var + LN_EPS)
    return zc * rstd, rstd


def _ln_bwd(dy, xhat, rstd, g):
    dxh = dy * g
    m1 = jnp.mean(dxh, axis=-1, keepdims=True)
    m2 = jnp.mean(dxh * xhat, axis=-1, keepdims=True)
    return rstd * (dxh - m1 - xhat * m2)


def _softplus_neg(lam):
    u = jnp.exp(-jnp.abs(lam))
    w = 1.0 + u
    d = w - 1.0
    log1p_u = jnp.where(d == 0.0, u, jnp.log(w) * (u / jnp.where(d == 0.0, 1.0, d)))
    return jnp.maximum(-lam, 0.0) + log1p_u


def _shift_down(x, halo, s):
    xs = pltpu.roll(x, s, 0)
    hs = pltpu.roll(halo, s, 0)
    row8 = lax.broadcasted_iota(jnp.int32, hs.shape, 0)
    first = jnp.where(row8 < s, hs, xs[:8])
    return jnp.concatenate([first, xs[8:]], axis=0)


def _shift_up(x, halo, s):
    n = x.shape[0]
    xs = pltpu.roll(x, n - s, 0)
    hs = pltpu.roll(halo, 8 - s, 0)
    row8 = lax.broadcasted_iota(jnp.int32, hs.shape, 0)
    last = jnp.where(row8 >= 8 - s, hs, xs[n - 8:])
    return jnp.concatenate([xs[:n - 8], last], axis=0)


def _row_sum(x):
    return jnp.sum(x, axis=0, keepdims=True)


def _put_rows(acc_ref, rows):
    row8 = lax.broadcasted_iota(jnp.int32, acc_ref.shape, 0)
    upd = jnp.zeros(acc_ref.shape, F32)
    for r, vec in enumerate(rows):
        upd = jnp.where(row8 == r, vec, upd)
    acc_ref[...] += upd


def _place():
    return lax.axis_index("x"), lax.axis_index("y"), lax.axis_index("c")


def _dev_index(px, py, pc):
    return 4 * px + 2 * py + pc


_ANY = pl.BlockSpec(memory_space=pl.ANY)


class _Gather:
    def __init__(self, arrays):
        self.arrays = list(arrays)
        self.n = len(self.arrays)

    def out_shape(self):
        return [jax.ShapeDtypeStruct((N_DEV,) + s.shape, s.dtype) for s in self.arrays]

    def scratch(self):
        return [pltpu.SemaphoreType.DMA((self.n, 7)), pltpu.SemaphoreType.DMA((self.n, 7)),
                pltpu.SemaphoreType.DMA((self.n,))]

    def _parts(self, ins, outs, sems):
        send_sems, recv_sems, local_sems = sems
        x, y, c = _place()
        me, sibling = (x, y, c), (x, y, 1 - c)
        chips = [(1 - x, y), (x, 1 - y), (1 - x, 1 - y)]

        def copy(a, k, block, to, src=None):
            rows = outs[a].at[_dev_index(*block)]
            return pltpu.make_async_remote_copy(
                src_ref=rows if src is None else src, dst_ref=rows, send_sem=send_sems.at[a, k],
                recv_sem=recv_sems.at[a, k], device_id=to, device_id_type=MESH)

        rng = range(self.n)
        mine = [pltpu.make_async_copy(ins[a], outs[a].at[_dev_index(*me)], local_sems.at[a]) for a in rng]
        first = [copy(a, 0, me, sibling, src=ins[a]) for a in rng]
        first += [copy(a, 1 + j, me, (*chip, c), src=ins[a]) for j, chip in enumerate(chips) for a in rng]
        landed = [copy(a, 1 + j, (*chip, c), me) for j, chip in enumerate(chips) for a in rng]
        passed = [copy(a, 4 + j, (*chip, c), sibling) for j, chip in enumerate(chips) for a in rng]
        from_sibling = [copy(a, 0, sibling, me) for a in rng]
        from_sibling += [copy(a, 4 + j, (*chip, 1 - c), me) for j, chip in enumerate(chips) for a in rng]
        return mine, first, landed, passed, from_sibling

    def start(self, ins, outs, sems):
        mine, first, _, _, _ = self._parts(ins, outs, sems)
        for cp in mine + first:
            cp.start()

    def forward(self, ins, outs, sems):
        _, _, landed, passed, _ = self._parts(ins, outs, sems)
        for got, fwd in zip(landed, passed):
            got.wait_recv()
            fwd.start()

    def finish(self, ins, outs, sems):
        mine, first, _, passed, from_sibling = self._parts(ins, outs, sems)
        for cp in from_sibling:
            cp.wait_recv()
        for cp in first + passed:
            cp.wait_send()
        for cp in mine:
            cp.wait()

    def before(self, ins, outs, sems, step, nsteps):
        pl.when(step == 0)(lambda: self.start(ins, outs, sems))
        pl.when(step == (7 * nsteps) // 8)(lambda: self.forward(ins, outs, sems))

    def after(self, ins, outs, sems, step, nsteps):
        pl.when(step == nsteps - 1)(lambda: self.finish(ins, outs, sems))


class _Exchange:
    def __init__(self, arrays):
        self.arrays = list(arrays)
        self.n = len(self.arrays)

    def out_shape(self):
        return [jax.ShapeDtypeStruct(b.shape, b.dtype) for b in self.arrays]

    def scratch(self):
        return [pltpu.SemaphoreType.DMA((self.n, 7)), pltpu.SemaphoreType.DMA((self.n, 7)),
                pltpu.SemaphoreType.DMA((self.n,))]

    def _parts(self, ins, outs, sems):
        send_sems, recv_sems, local_sems = sems
        x, y, c = _place()
        me = _dev_index(x, y, c)
        peers = [(x ^ (k >> 2), y ^ ((k >> 1) & 1), c ^ (k & 1)) for k in range(1, N_DEV)]
        rng = range(self.n)
        mine = [pltpu.make_async_copy(ins[a].at[me], outs[a].at[me], local_sems.at[a]) for a in rng]
        sent = [pltpu.make_async_remote_copy(
            src_ref=ins[a].at[_dev_index(*to)], dst_ref=outs[a].at[me], send_sem=send_sems.at[a, k],
            recv_sem=recv_sems.at[a, k], device_id=to, device_id_type=MESH) for k, to in enumerate(peers) for a in rng]
        arrivals = [pltpu.make_async_remote_copy(
            src_ref=ins[a].at[me], dst_ref=outs[a].at[_dev_index(*frm)], send_sem=send_sems.at[a, k],
            recv_sem=recv_sems.at[a, k], device_id=frm, device_id_type=MESH) for k, frm in enumerate(peers) for a in rng]
        return mine, sent, arrivals

    def start(self, ins, outs, sems):
        mine, sent, _ = self._parts(ins, outs, sems)
        for cp in mine + sent:
            cp.start()

    def finish(self, ins, outs, sems):
        mine, sent, arrivals = self._parts(ins, outs, sems)
        for cp in arrivals:
            cp.wait_recv()
        for cp in sent:
            cp.wait_send()
        for cp in mine:
            cp.wait()

    def before(self, ins, outs, sems, step, nsteps):
        pl.when(step == 0)(lambda: self.start(ins, outs, sems))

    def after(self, ins, outs, sems, step, nsteps):
        pl.when(step == nsteps - 1)(lambda: self.finish(ins, outs, sems))


class _Bcast(_Exchange):
    def out_shape(self):
        return [jax.ShapeDtypeStruct((N_DEV,) + s.shape, s.dtype) for s in self.arrays]

    def _parts(self, ins, outs, sems):
        send_sems, recv_sems, local_sems = sems
        x, y, c = _place()
        me = _dev_index(x, y, c)
        peers = [(x ^ (k >> 2), y ^ ((k >> 1) & 1), c ^ (k & 1)) for k in range(1, N_DEV)]
        rng = range(self.n)
        mine = [pltpu.make_async_copy(ins[a], outs[a].at[me], local_sems.at[a]) for a in rng]
        sent = [pltpu.make_async_remote_copy(
            src_ref=ins[a], dst_ref=outs[a].at[me], send_sem=send_sems.at[a, k], recv_sem=recv_sems.at[a, k],
            device_id=to, device_id_type=MESH) for k, to in enumerate(peers) for a in rng]
        arrivals = [pltpu.make_async_remote_copy(
            src_ref=ins[a], dst_ref=outs[a].at[_dev_index(*frm)], send_sem=send_sems.at[a, k],
            recv_sem=recv_sems.at[a, k], device_id=frm, device_id_type=MESH) for k, frm in enumerate(peers) for a in rng]
        return mine, sent, arrivals


class _Multi:
    def __init__(self, comms):
        self.comms = list(comms)
        self.arrays = [arr for c in self.comms for arr in c.arrays]
        self.n = len(self.arrays)

    def out_shape(self):
        return [s for c in self.comms for s in c.out_shape()]

    def scratch(self):
        return [s for c in self.comms for s in c.scratch()]

    def _each(self, ins, outs, sems):
        a = 0
        for j, c in enumerate(self.comms):
            yield c, ins[a:a + c.n], outs[a:a + c.n], sems[3 * j:3 * j + 3]
            a += c.n

    def before(self, ins, outs, sems, step, nsteps):
        for c, ci, co, cs in self._each(ins, outs, sems):
            c.before(ci, co, cs, step, nsteps)

    def after(self, ins, outs, sems, step, nsteps):
        for c, ci, co, cs in self._each(ins, outs, sems):
            c.after(ci, co, cs, step, nsteps)


def _comm_call(comms, name):
    ns = [c.n for c in comms]
    n = sum(ns)

    def body(*refs):
        parts, a, s = [], 0, 2 * n
        for c in comms:
            parts.append((c, refs[a:a + c.n], refs[n + a:n + a + c.n], refs[s:s + 3]))
            a, s = a + c.n, s + 3
        for c, ins, outs, sems in parts:
            c.start(ins, outs, sems)
        for c, ins, outs, sems in parts:
            if isinstance(c, _Gather):
                c.forward(ins, outs, sems)
        for c, ins, outs, sems in parts:
            c.finish(ins, outs, sems)

    res = pl.pallas_call(
        body, name=name, in_specs=[_ANY] * n, out_specs=[_ANY] * n,
        out_shape=[s for c in comms for s in c.out_shape()], scratch_shapes=[s for c in comms for s in c.scratch()],
    )(*[arr for c in comms for arr in c.arrays])
    out, a = [], 0
    for k in ns:
        out.append(res[a:a + k])
        a += k
    return out


def _pcall(body, args, *, name, grid, in_specs, out_specs, out_shape, scratch_shapes=(), sem="parallel", comm=None,
           step_axis=0):
    sem = (sem,) * len(grid) if isinstance(sem, str) else sem
    if comm is None:
        res = pl.pallas_call(body, name=name, grid=grid, in_specs=in_specs, out_specs=out_specs, out_shape=out_shape,
                             scratch_shapes=list(scratch_shapes), compiler_params=_params(*sem))(*args)
        return res, []
    n_in, n_out, n_scr, n = len(in_specs), len(out_specs), len(scratch_shapes), comm.n
    nsteps = grid[step_axis]
    assert all(g == 1 for ax, g in enumerate(grid) if ax != step_axis)

    def hosted(*refs):
        ins, cin = refs[:n_in], refs[n_in:n_in + n]
        o0 = n_in + n
        outs, cout = refs[o0:o0 + n_out], refs[o0 + n_out:o0 + n_out + n]
        s0 = o0 + n_out + n
        scr, sems = refs[s0:s0 + n_scr], refs[s0 + n_scr:]
        step = pl.program_id(step_axis)
        comm.before(cin, cout, sems, step, nsteps)
        body(*ins, *outs, *scr)
        comm.after(cin, cout, sems, step, nsteps)

    res = pl.pallas_call(
        hosted, name=name, grid=grid, in_specs=list(in_specs) + [_ANY] * n, out_specs=list(out_specs) + [_ANY] * n,
        out_shape=list(out_shape) + comm.out_shape(), scratch_shapes=list(scratch_shapes) + comm.scratch(),
        compiler_params=_params(*(("arbitrary",) * len(grid))))(*args, *comm.arrays)
    return res[:n_out], res[n_out:]


def _in_proj(x, w_in_t, comm=None):
    S = x.shape[0]
    tb = min(512, S)

    def body(x_ref, w_ref, q_ref, k_ref, v_ref, xr_ref, gr_ref):
        u = _dot_nt(x_ref[...].astype(BF16), w_ref[...])
        q_ref[...] = (u[:, :D_ATT] * (HEAD_DIM ** -0.5)).astype(BF16)
        k_ref[...] = u[:, D_ATT:D_ATT + D_KV].astype(BF16)
        v_ref[...] = u[:, D_ATT + D_KV:D_ATT + 2 * D_KV].astype(BF16)
        xr_ref[...] = u[:, D_ATT + 2 * D_KV:D_ATT + 2 * D_KV + D_RNN]
        gr_ref[...] = u[:, D_ATT + 2 * D_KV + D_RNN:]

    return _pcall(
        body, (x, w_in_t), name="in_proj", grid=(S // tb,), comm=comm,
        in_specs=[_rows(tb, D_MODEL), _resident((D_IN, D_MODEL))],
        out_specs=[_rows(tb, D_ATT), _rows(tb, D_KV), _rows(tb, D_KV), _rows(tb, D_RNN), _rows(tb, D_RNN)],
        out_shape=[jax.ShapeDtypeStruct((S, D_ATT), BF16), jax.ShapeDtypeStruct((S, D_KV), BF16),
                   jax.ShapeDtypeStruct((S, D_KV), BF16), jax.ShapeDtypeStruct((S, D_RNN), F32),
                   jax.ShapeDtypeStruct((S, D_RNN), F32)])


GROUP = N_HEADS // N_KV


def _band_mask(i):
    qi = lax.broadcasted_iota(jnp.int32, (GROUP * QBLK, 2 * QBLK), 0) & (QBLK - 1)
    sj = lax.broadcasted_iota(jnp.int32, (GROUP * QBLK, 2 * QBLK), 1)
    return (sj > qi) & (sj <= qi + QBLK) & ((sj >= QBLK) | (i > 0))


def _stack_heads(x, g):
    return jnp.concatenate([x[:, (g * GROUP + hh) * HEAD_DIM:(g * GROUP + hh + 1) * HEAD_DIM] for hh in range(GROUP)],
                           axis=0)


def _unstack_heads(x4):
    return [x4[hh * QBLK:(hh + 1) * QBLK] for hh in range(GROUP)]


def _sink_column(sink_ref, g):
    head = lax.broadcasted_iota(jnp.int32, (GROUP * QBLK, 1), 0) // QBLK
    col = jnp.full((GROUP * QBLK, 1), sink_ref[g * GROUP], F32)
    for hh in range(1, GROUP):
        col = jnp.where(head == hh, sink_ref[g * GROUP + hh], col)
    return col


ATT_STEP = 4
IN_GRAD_PARTS = 2


def _attn_specs(nq=1):
    cur = lambda i: (i, 0)
    prev = lambda i: (jnp.maximum(nq * i - 1, 0), 0)
    return [pl.BlockSpec((nq * QBLK, D_KV), cur), pl.BlockSpec((QBLK, D_KV), prev),
            pl.BlockSpec((nq * QBLK, D_KV), cur), pl.BlockSpec((QBLK, D_KV), prev)]


def _attn_fwd(q, k, v, sinks, comm=None):
    S = q.shape[0]
    nq = min(ATT_STEP, S // QBLK)

    def body(sink_ref, q_ref, kc_ref, kp_ref, vc_ref, vp_ref, o_ref, lse_ref):
        first = pl.program_id(0) * nq
        kall = jnp.concatenate([kp_ref[...], kc_ref[...]], axis=0)
        vall = jnp.concatenate([vp_ref[...], vc_ref[...]], axis=0)
        for b in range(nq):
            valid = _band_mask(first + b)
            rows = slice(b * QBLK, (b + 1) * QBLK)
            keys = slice(b * QBLK, (b + 2) * QBLK)
            qv = q_ref[rows, :]
            outs = []
            for g in range(N_KV):
                kcat = kall[keys, g * HEAD_DIM:(g + 1) * HEAD_DIM]
                vcat = vall[keys, g * HEAD_DIM:(g + 1) * HEAD_DIM]
                s = jnp.where(valid, _dot_nt(_stack_heads(qv, g), kcat), -1e30)
                sink = _sink_column(sink_ref, g)
                m = jnp.maximum(jnp.max(s, axis=1, keepdims=True), sink)
                p = jnp.exp(s - m)
                l = jnp.sum(p, axis=1, keepdims=True) + jnp.exp(sink - m)
                outs += _unstack_heads(_dot(p.astype(BF16), vcat) / l)
                lse_ref[(b * N_KV + g) * GROUP * QBLK:(b * N_KV + g + 1) * GROUP * QBLK, :] = m + jnp.log(l)
            o_ref[rows, :] = jnp.concatenate(outs, axis=1).astype(BF16)

    lse_rows = nq * N_HEADS * QBLK
    return _pcall(
        body, (sinks, q, k, k, v, v), name="attn_fwd", grid=(S // (nq * QBLK),), comm=comm,
        in_specs=[pl.BlockSpec(memory_space=pltpu.SMEM), _rows(nq * QBLK, D_ATT)] + _attn_specs(nq),
        out_specs=[_rows(nq * QBLK, D_ATT), _rows(lse_rows, 1)],
        out_shape=[jax.ShapeDtypeStruct((S, D_ATT), BF16), jax.ShapeDtypeStruct((S * N_HEADS, 1), F32)])


def _w_rows(w_ref):
    return [w_ref[k:k + 1, :] for k in range(w_ref.shape[0])]


def _conv4(x, halo, w, b):
    y = b + w[3] * x
    for s in (1, 2, 3):
        y = y + w[3 - s] * _shift_down(x, halo, s)
    return y


def _rnn_gates(xc, wa, wx, ba, bx, sp):
    xcb = xc.astype(BF16)
    r = _sigmoid(_dot(xcb, wa) + ba)
    ig = _sigmoid(_dot(xcb, wx) + bx)
    la = -LRU_C * r * sp
    a = jnp.exp(la)
    t = jnp.tanh(la)
    f = jnp.sqrt(-2.0 * t / (1.0 - t))
    return r, ig, a, f


def _rnn_fwd(xr, gr, conv_w, conv_b, wa, wx, ba, bx, lam, comm=None):
    S = xr.shape[0]
    tb = min(256, S)

    def body(xr_ref, gr_ref, cw_ref, cb_ref, wa_ref, wx_ref, ba_ref, bx_ref, lam_ref, rec_ref, h_ref,
             xc_ref, r_ref, ig_ref, a_ref, f_ref, halo_s, hc_s, a_s, b_s):
        @pl.when(pl.program_id(0) == 0)
        def _():
            halo_s[...] = jnp.zeros_like(halo_s)
            hc_s[...] = jnp.zeros_like(hc_s)

        x = xr_ref[...]
        xc = _conv4(x, halo_s[...], _w_rows(cw_ref), cb_ref[...])
        halo_s[...] = x[tb - 8:]
        r, ig, a, f = _rnn_gates(xc, wa_ref[...], wx_ref[...], ba_ref[...], bx_ref[...], _softplus_neg(lam_ref[...]))
        xc_ref[...] = xc
        r_ref[...] = r
        ig_ref[...] = ig
        a_ref[...] = a
        f_ref[...] = f
        a_s[...] = a
        b_s[...] = f * ig * xc
        row8 = lax.broadcasted_iota(jnp.int32, (8, D_RNN), 0)

        def tile(t, hc):
            o = pl.multiple_of(t * 8, 8)
            at = a_s[pl.ds(o, 8), :]
            bt = b_s[pl.ds(o, 8), :]
            for s in (1, 2, 4):
                keep = row8 >= s
                a_sh = jnp.where(keep, pltpu.roll(at, s, 0), 1.0)
                b_sh = jnp.where(keep, pltpu.roll(bt, s, 0), 0.0)
                bt = at * b_sh + bt
                at = at * a_sh
            ht = at * hc + bt
            b_s[pl.ds(o, 8), :] = ht
            return _row_sum(jnp.where(row8 == 7, ht, 0.0))

        hc_s[0:1, :] = lax.fori_loop(0, tb // 8, tile, hc_s[0:1, :], unroll=2)
        h = b_s[...]
        h_ref[...] = h
        rec_ref[...] = (h * _gelu(gr_ref[...])).astype(BF16)

    vec = _resident((1, D_RNN))
    kept = jax.ShapeDtypeStruct((S, D_RNN), F32)
    return _pcall(
        body, (xr, gr, conv_w, conv_b, wa, wx, ba, bx, lam), name="rnn_fwd", grid=(S // tb,), sem="arbitrary", comm=comm,
        in_specs=[_rows(tb, D_RNN), _rows(tb, D_RNN), _resident((4, D_RNN)), vec,
                  _resident((D_RNN, D_RNN)), _resident((D_RNN, D_RNN)), vec, vec, vec],
        out_specs=[_rows(tb, D_RNN)] * 7,
        out_shape=[jax.ShapeDtypeStruct((S, D_RNN), BF16), kept, kept, kept, kept, kept, kept],
        scratch_shapes=[pltpu.VMEM((8, D_RNN), F32), pltpu.VMEM((8, D_RNN), F32),
                        pltpu.VMEM((tb, D_RNN), F32), pltpu.VMEM((tb, D_RNN), F32)])


def _mix_ln1_up(x, att, rec, w_out, ln1_g, ln1_b, w_up, fcw, fcb, comm=None):
    S = x.shape[0]
    tb = min(256, S)
    nblk, _, wblk = w_up.shape
    half = nblk // 2

    def body(x_ref, att_ref, rec_ref, wo_ref, g_ref, b_ref, wu_ref, fcw_ref, fcb_ref,
             z1_ref, h1b_ref, gate_ref, act_ref, gl_ref, vdgl_ref, halo_s):
        @pl.when(pl.program_id(0) == 0)
        def _():
            halo_s[...] = jnp.zeros_like(halo_s)

        z1 = ALPHA * x_ref[...] + _dot(att_ref[...], wo_ref[:D_ATT, :]) + _dot(rec_ref[...], wo_ref[D_ATT:, :])
        z1_ref[...] = z1
        xhat, _ = _ln_stats(z1)
        h1b = (xhat * g_ref[...] + b_ref[...]).astype(BF16)
        h1b_ref[...] = h1b
        for jj in range(half):
            cols = slice(jj * wblk, (jj + 1) * wblk)
            gate = _dot(h1b, wu_ref[jj])
            val = _dot(h1b, wu_ref[jj + half])
            halo = halo_s[:, cols]
            conv = (fcb_ref[:, cols] + fcw_ref[2:3, cols] * gate + fcw_ref[1:2, cols] * _shift_down(gate, halo, 1)
                    + fcw_ref[0:1, cols] * _shift_down(gate, halo, 2))
            halo_s[:, cols] = gate[tb - 8:]
            gl, dgl = _gelu_and_grad(conv)
            gate_ref[:, cols] = gate.astype(BF16)
            act_ref[:, cols] = (gl * val).astype(BF16)
            gl_ref[:, cols] = gl.astype(BF16)
            vdgl_ref[:, cols] = (val * dgl).astype(BF16)

    vec = _resident((1, D_MODEL))
    wide = jax.ShapeDtypeStruct((S, D_FF), BF16)
    return _pcall(
        body, (x, att, rec, w_out, ln1_g, ln1_b, w_up, fcw, fcb), name="mix_ln1_up", grid=(S // tb,),
        sem="arbitrary", comm=comm,
        in_specs=[_rows(tb, D_MODEL), _rows(tb, D_ATT), _rows(tb, D_RNN), _resident((D_MODEL, D_MODEL)), vec, vec,
                  _resident(w_up.shape), _resident((3, D_FF)), _resident((1, D_FF))],
        out_specs=[_rows(tb, D_MODEL), _rows(tb, D_MODEL)] + [_rows(tb, D_FF)] * 4,
        out_shape=[jax.ShapeDtypeStruct((S, D_MODEL), F32), jax.ShapeDtypeStruct((S, D_MODEL), BF16), wide, wide, wide, wide],
        scratch_shapes=[pltpu.VMEM((8, D_FF), F32)])


def _tail(act, gl, vdgl, z1, h1b, p, tgt, w_down, w_pg, b_pg, w_pp, ln1_g, ln1_b, ln2_g, ln2_b):
    S = z1.shape[0]
    tb = min(256, S)

    def body(act_ref, gl_ref, vdgl_ref, z1_ref, h1b_ref, p_ref, t_ref, wd_ref, wpg_ref, bpg_ref, wpp_ref,
             g1_ref, b1_ref, g2_ref, b2_ref, dz2_ref, dpre_ref, dpp_ref, dgc_ref, dval_ref, dh1_ref, acc_ref):
        i = pl.program_id(0)

        @pl.when(i == 0)
        def _():
            acc_ref[...] = jnp.zeros_like(acc_ref)

        ffn = _dot(act_ref[...], wd_ref[...])
        xhat1, _ = _ln_stats(z1_ref[...])
        h1 = xhat1 * g1_ref[...] + b1_ref[...]
        sg = _sigmoid(_dot(h1b_ref[...], wpg_ref[...]) + bpg_ref[...])
        pp = _dot(p_ref[...].astype(BF16), wpp_ref[...])
        z2 = ALPHA * h1 + ffn + sg * pp
        xhat2, rstd2 = _ln_stats(z2)
        y = xhat2 * g2_ref[...] + b2_ref[...]
        err = y - t_ref[...]
        dy = err * (1.0 / D_MODEL)
        loss = 0.5 * jnp.sum(jnp.sum(err * err, axis=1, keepdims=True), axis=0, keepdims=True) * (1.0 / D_MODEL)
        dz2 = _ln_bwd(dy, xhat2, rstd2, g2_ref[...])
        dz2b = dz2.astype(BF16)
        dz2_ref[...] = dz2b
        dpre = dz2 * pp * sg * (1.0 - sg)
        dpreb = dpre.astype(BF16)
        dpre_ref[...] = dpreb
        dpp_ref[...] = (dz2 * sg).astype(BF16)
        dh1_ref[...] = ALPHA * dz2 + _dot_nt(dpreb, wpg_ref[...])
        dactb = _dot_nt(dz2b, wd_ref[...]).astype(BF16)
        dval_ref[...] = dactb * gl_ref[...]
        dgc_ref[...] = dactb * vdgl_ref[...]
        _put_rows(acc_ref, [_row_sum(dy * xhat2), _row_sum(dy), _row_sum(dpre),
                            jnp.broadcast_to(loss, (1, D_MODEL))])

    vec = _resident((1, D_MODEL))
    return pl.pallas_call(
        body, name="tail", grid=(S // tb,),
        in_specs=[_rows(tb, D_FF), _rows(tb, D_FF), _rows(tb, D_FF), _rows(tb, D_MODEL), _rows(tb, D_MODEL),
                  _rows(tb, PLE_DIM), _rows(tb, D_MODEL), _resident((D_FF, D_MODEL)), _resident((D_MODEL, D_MODEL)), vec,
                  _resident((PLE_DIM, D_MODEL)), vec, vec, vec, vec],
        out_specs=[_rows(tb, D_MODEL), _rows(tb, D_MODEL), _rows(tb, D_MODEL), _rows(tb, D_FF),
                   _rows(tb, D_FF), _rows(tb, D_MODEL), _acc((8, D_MODEL))],
        out_shape=[jax.ShapeDtypeStruct((S, D_MODEL), BF16),
                   jax.ShapeDtypeStruct((S, D_MODEL), BF16), jax.ShapeDtypeStruct((S, D_MODEL), BF16),
                   jax.ShapeDtypeStruct((S, D_FF), BF16), jax.ShapeDtypeStruct((S, D_FF), BF16),
                   jax.ShapeDtypeStruct((S, D_MODEL), F32), jax.ShapeDtypeStruct((8, D_MODEL), F32)],
        compiler_params=_params("arbitrary"),
    )(act, gl, vdgl, z1, h1b, p, tgt, w_down, w_pg, b_pg, w_pp, ln1_g, ln1_b, ln2_g, ln2_b)


def _weight_grad(a_list, b_list, name, layout, ts=512, comm=None, b_window=None, halves=False):
    S = a_list[0].shape[0]
    ms = [a.shape[1] for a in a_list]
    M, nb = sum(ms), len(b_list)
    win, Nb = b_window if b_window else (0, b_list[0].shape[1])
    ts = min(ts, S)
    nk = S // ts
    per_b = N_DEV // nb
    na = len(a_list)

    n_out = 2 if halves else 1
    assert layout == "cols" or not halves

    def body(*refs):
        a_refs, b_refs, o_refs, acc_ref = refs[:na], refs[na:na + nb], refs[na + nb:na + nb + n_out], refs[-1]
        o_ref = o_refs[0]
        j, k = pl.program_id(0), pl.program_id(1)

        @pl.when(k == 0)
        def _():
            acc_ref[...] = jnp.zeros_like(acc_ref)

        for jj in range(nb):
            @pl.when(j == jj)
            def _():
                b = b_refs[jj][...].astype(BF16)
                off = 0
                for a_ref, m in zip(a_refs, ms):
                    acc_ref[off:off + m, :] += _dot_tn(a_ref[...].astype(BF16), b)
                    off += m

        @pl.when(k == nk - 1)
        def _():
            for d in range(per_b):
                if layout == "rows":
                    o_ref[d] = acc_ref[d * (M // N_DEV):(d + 1) * (M // N_DEV), :].astype(BF16)
                elif layout == "cols" and halves:
                    for o_half, r0 in zip(o_refs, (0, M // 2)):
                        o_half[d] = acc_ref[r0:r0 + M // 2, d * (Nb // per_b):(d + 1) * (Nb // per_b)].astype(BF16)
                elif layout == "cols":
                    o_ref[d] = acc_ref[:, d * (Nb // per_b):(d + 1) * (Nb // per_b)].astype(BF16)
                else:
                    o_ref[d] = acc_ref[:, d * (Nb // per_b):(d + 1) * (Nb // per_b)].T.astype(BF16)

    def b_index(jj):
        return lambda j, k: (jnp.where(j == jj, k, jnp.where(j < jj, 0, nk - 1)), win)

    if layout == "rows":
        assert nb == 1
        blk = (N_DEV, M // N_DEV, Nb)
    elif layout == "cols":
        blk = (per_b, M // n_out, Nb // per_b)
    else:
        blk = (per_b, Nb // per_b, M)
    res, comm_res = _pcall(
        body, (*a_list, *b_list), name=name, grid=(nb, nk), sem="arbitrary", comm=comm, step_axis=1,
        in_specs=[pl.BlockSpec((ts, m), lambda j, k: (k, 0)) for m in ms]
        + [pl.BlockSpec((ts, Nb), b_index(jj)) for jj in range(nb)],
        out_specs=[pl.BlockSpec(blk, lambda j, k: (j, 0, 0))] * n_out,
        out_shape=[jax.ShapeDtypeStruct((N_DEV,) + blk[1:], BF16)] * n_out,
        scratch_shapes=[pltpu.VMEM((M, Nb), F32)])
    res = res if halves else res[0]
    return (res, comm_res) if comm is not None else res


def _up_bwd(dgc, gate, dval, dh1p, z1, w_up, fcw, w_out, ln1_g, comm=None):
    S = z1.shape[0]
    tb = min(256, S)
    t16 = tb // 16
    n16 = S // 16
    nblk, _, wblk = w_up.shape
    half = nblk // 2
    nsteps = S // tb

    def body(dgc_ref, dgn_ref, gc_ref, dval_ref, dh1p_ref, z1_ref, wu_ref, fcw_ref, wo_ref, g1_ref,
             dgate_ref, dz1_ref, dz1b_ref, datt_ref, drec_ref, accf_ref, accd_ref):
        i = pl.program_id(0)

        @pl.when(i == 0)
        def _():
            accf_ref[...] = jnp.zeros_like(accf_ref)
            accd_ref[...] = jnp.zeros_like(accd_ref)

        dg = dgc_ref[...].astype(F32)
        nxt = jnp.where(i < nsteps - 1, dgn_ref[...].astype(F32)[0:8], 0.0)
        w = _w_rows(fcw_ref)
        up1, up2 = _shift_up(dg, nxt, 1), _shift_up(dg, nxt, 2)
        dgate = (w[2] * dg + w[1] * up1 + w[0] * up2).astype(BF16)
        dgate_ref[...] = dgate
        gate = gc_ref[...].astype(F32)
        _put_rows(accf_ref, [_row_sum(up2 * gate), _row_sum(up1 * gate), _row_sum(dg * gate), _row_sum(dg)])

        dh1 = dh1p_ref[...]
        for j in range(nblk):
            src = dgate if j < half else dval_ref[...]
            jj = j % half
            dh1 = dh1 + _dot_nt(src[:, jj * wblk:(jj + 1) * wblk], wu_ref[j])
        xhat1, rstd1 = _ln_stats(z1_ref[...])
        dz1 = _ln_bwd(dh1, xhat1, rstd1, g1_ref[...])
        dz1_ref[...] = dz1
        dz1b = dz1.astype(BF16)
        dz1b_ref[...] = dz1b
        dcat = _dot_nt(dz1b, wo_ref[...])
        datt_ref[...] = dcat[:, :D_ATT].astype(BF16)
        drec_ref[...] = dcat[:, D_ATT:]
        _put_rows(accd_ref, [_row_sum(dh1 * xhat1), _row_sum(dh1)])

    next16 = pl.BlockSpec((16, D_FF), lambda i: (jnp.minimum((i + 1) * t16, n16 - 1), 0))
    return _pcall(
        body, (dgc, dgc, gate, dval, dh1p, z1, w_up, fcw, w_out, ln1_g), name="up_bwd",
        grid=(nsteps,), sem="arbitrary", comm=comm,
        in_specs=[_rows(tb, D_FF), next16, _rows(tb, D_FF), _rows(tb, D_FF), _rows(tb, D_MODEL),
                  _rows(tb, D_MODEL), _resident(w_up.shape), _resident((3, D_FF)),
                  _resident((D_MODEL, D_MODEL)), _resident((1, D_MODEL))],
        out_specs=[_rows(tb, D_FF), _rows(tb, D_MODEL), _rows(tb, D_MODEL), _rows(tb, D_ATT), _rows(tb, D_RNN),
                   _acc((8, D_FF)), _acc((8, D_MODEL))],
        out_shape=[jax.ShapeDtypeStruct((S, D_FF), BF16), jax.ShapeDtypeStruct((S, D_MODEL), F32),
                   jax.ShapeDtypeStruct((S, D_MODEL), BF16), jax.ShapeDtypeStruct((S, D_ATT), BF16),
                   jax.ShapeDtypeStruct((S, D_RNN), F32), jax.ShapeDtypeStruct((8, D_FF), F32),
                   jax.ShapeDtypeStruct((8, D_MODEL), F32)])


def _attn_bwd(q, k, v, lse, do, sinks, comm=None):
    S = q.shape[0]
    grp = N_HEADS // N_KV
    nq = min(ATT_STEP, S // QBLK)

    def body(sink_ref, q_ref, kc_ref, kp_ref, vc_ref, vp_ref, do_ref, lse_ref, dq_ref, dkc_ref, dkp_ref, dvc_ref, dvp_ref,
             ds_ref):
        i = pl.program_id(0)

        @pl.when(i == 0)
        def _():
            ds_ref[...] = jnp.zeros_like(ds_ref)

        row8 = lax.broadcasted_iota(jnp.int32, (8, 128), 0)
        lane8 = lax.broadcasted_iota(jnp.int32, (8, 128), 1)
        dsink = jnp.zeros((8, 128), F32)
        kall = jnp.concatenate([kp_ref[...], kc_ref[...]], axis=0)
        vall = jnp.concatenate([vp_ref[...], vc_ref[...]], axis=0)
        dk_t = [jnp.zeros((D_KV, QBLK), F32) for _ in range(nq + 1)]
        dv_t = [jnp.zeros((D_KV, QBLK), F32) for _ in range(nq + 1)]
        for b in range(nq):
            valid = _band_mask(i * nq + b)
            rows = slice(b * QBLK, (b + 1) * QBLK)
            keys = slice(b * QBLK, (b + 2) * QBLK)
            qv, dov = q_ref[rows, :], do_ref[rows, :]
            dqs, dks, dvs = [], [], []
            for g in range(N_KV):
                kcat = kall[keys, g * HEAD_DIM:(g + 1) * HEAD_DIM]
                vcat = vall[keys, g * HEAD_DIM:(g + 1) * HEAD_DIM]
                q4, do4 = _stack_heads(qv, g), _stack_heads(dov, g)
                s = jnp.where(valid, _dot_nt(q4, kcat), -1e30)
                lse = lse_ref[(b * N_KV + g) * GROUP * QBLK:(b * N_KV + g + 1) * GROUP * QBLK, :]
                p = jnp.exp(s - lse)
                p_sink = jnp.exp(_sink_column(sink_ref, g) - lse)
                dp = _dot_nt(do4, vcat)
                delta = jnp.sum(p * dp, axis=1, keepdims=True)
                dsc = (p * (dp - delta)).astype(BF16)
                dqs += _unstack_heads(_dot(dsc, kcat) * (HEAD_DIM ** -0.5))
                dks.append(_dot_tn(q4, dsc))
                dvs.append(_dot_tn(do4, p.astype(BF16)))
                for hh, part in enumerate(_unstack_heads(-p_sink * delta)):
                    here = (row8 == 0) & (lane8 == g * grp + hh)
                    dsink = dsink + jnp.where(here, jnp.sum(part, axis=0, keepdims=True), 0.0)
            dq_ref[rows, :] = jnp.concatenate(dqs, axis=1).astype(BF16)
            dk2, dv2 = jnp.concatenate(dks, axis=0), jnp.concatenate(dvs, axis=0)
            dk_t[b], dk_t[b + 1] = dk_t[b] + dk2[:, :QBLK], dk_t[b + 1] + dk2[:, QBLK:]
            dv_t[b], dv_t[b + 1] = dv_t[b] + dv2[:, :QBLK], dv_t[b + 1] + dv2[:, QBLK:]
        dkp_ref[...] = dk_t[0].T
        dvp_ref[...] = dv_t[0].T
        for b in range(nq):
            dkc_ref[b * QBLK:(b + 1) * QBLK, :] = dk_t[b + 1].T
            dvc_ref[b * QBLK:(b + 1) * QBLK, :] = dv_t[b + 1].T
        ds_ref[...] += dsink

    nsteps = S // (nq * QBLK)
    cur = jax.ShapeDtypeStruct((S, D_KV), F32)
    prev = jax.ShapeDtypeStruct((nsteps * QBLK, D_KV), F32)
    big = _rows(nq * QBLK, D_ATT)
    return _pcall(
        body, (sinks, q, k, k, v, v, do, lse), name="attn_bwd", grid=(nsteps,), sem="arbitrary", comm=comm,
        in_specs=[pl.BlockSpec(memory_space=pltpu.SMEM), big] + _attn_specs(nq) + [big, _rows(nq * N_HEADS * QBLK, 1)],
        out_specs=[big, _rows(nq * QBLK, D_KV), _rows(QBLK, D_KV), _rows(nq * QBLK, D_KV), _rows(QBLK, D_KV),
                   _acc((8, 128))],
        out_shape=[jax.ShapeDtypeStruct((S, D_ATT), BF16), cur, prev, cur, prev, jax.ShapeDtypeStruct((8, 128), F32)])


def _rnn_bwd(xr, gr, h, kept, drec, conv_w, wa, wx, lam, comm=None):
    S = xr.shape[0]
    tb = min(256, S)
    t8 = tb // 8
    nsteps = S // tb

    def body(xr_ref, xp_ref, gr_ref, h_ref, hp_ref, xc_ref, r_ref, ig_ref, a_ref, f_ref, drec_ref, cw_ref, wa_ref, wx_ref,
             lam_ref, dxr_ref, dgr_ref, gwa_ref, gwx_ref, acc_ref, carry_s, dxc_halo_s, d_s, gwa_s, gwx_s):
        i = pl.program_id(0)
        blk = nsteps - 1 - i

        @pl.when(i == 0)
        def _():
            gwa_s[...] = jnp.zeros_like(gwa_s)
            gwx_s[...] = jnp.zeros_like(gwx_s)
            acc_ref[...] = jnp.zeros_like(acc_ref)
            carry_s[...] = jnp.zeros_like(carry_s)
            dxc_halo_s[...] = jnp.zeros_like(dxc_halo_s)

        x = xr_ref[...]
        xhalo = jnp.where(blk > 0, xp_ref[...], 0.0)
        cw = _w_rows(cw_ref)
        xs = [_shift_down(x, xhalo, 3), _shift_down(x, xhalo, 2), _shift_down(x, xhalo, 1), x]
        xc, r, ig, a, f = xc_ref[...], r_ref[...], ig_ref[...], a_ref[...], f_ref[...]
        sp = _softplus_neg(lam_ref[...])
        hcur = h_ref[...]
        hprev = _shift_down(hcur, jnp.where(blk > 0, hp_ref[...], 0.0), 1)
        gl, dgl = _gelu_and_grad(gr_ref[...])
        drec = drec_ref[...]
        dgr_ref[...] = (drec * hcur * dgl).astype(BF16)
        d_s[...] = drec * gl
        row8 = lax.broadcasted_iota(jnp.int32, (8, D_RNN), 0)

        def tile(t, c):
            o = pl.multiple_of((t8 - 1 - t) * 8, 8)
            a8 = a_ref[pl.ds(o, 8), :]
            dt = d_s[pl.ds(o, 8), :]
            at = jnp.where(row8 == 7, 1.0, pltpu.roll(a8, 7, 0))
            for s in (1, 2, 4):
                keep = row8 < 8 - s
                a_sh = jnp.where(keep, pltpu.roll(at, 8 - s, 0), 1.0)
                d_sh = jnp.where(keep, pltpu.roll(dt, 8 - s, 0), 0.0)
                dt = at * d_sh + dt
                at = at * a_sh
            lt = at * c + dt
            d_s[pl.ds(o, 8), :] = lt
            return _row_sum(jnp.where(row8 == 0, a8 * lt, 0.0))

        carry_s[0:1, :] = lax.fori_loop(0, t8, tile, carry_s[0:1, :], unroll=2)
        lmb = d_s[...]
        a2 = a * a
        dla = lmb * hprev * a - lmb * ig * xc * (a2 / f)
        di = lmb * f * xc
        dr = dla * (-LRU_C) * sp
        dpa = dr * r * (1.0 - r)
        dpx = di * ig * (1.0 - ig)
        dpab = dpa.astype(BF16)
        dpxb = dpx.astype(BF16)
        xcb = xc.astype(BF16)
        gwa_s[...] += _dot_tn(xcb, dpab)
        gwx_s[...] += _dot_tn(xcb, dpxb)

        @pl.when(i == nsteps - 1)
        def _():
            for dense, out in ((gwa_s[...], gwa_ref), (gwx_s[...], gwx_ref)):
                for b in range(RNN_BLOCKS):
                    rows = slice(b * HEAD_DIM, (b + 1) * HEAD_DIM)
                    out[rows, :] = dense[rows, b * HEAD_DIM:(b + 1) * HEAD_DIM]

        dxc = lmb * f * ig + _dot_nt(dpab, wa_ref[...]) + _dot_nt(dpxb, wx_ref[...])
        nxt = dxc_halo_s[...]
        dxr = cw[3] * dxc
        for s in (1, 2, 3):
            dxr = dxr + cw[3 - s] * _shift_up(dxc, nxt, s)
        dxr_ref[...] = dxr.astype(BF16)
        dxc_halo_s[...] = dxc[:8]
        dlam = _row_sum(dla * (-LRU_C) * r) * (-1.0 / (1.0 + jnp.exp(lam_ref[...])))
        _put_rows(acc_ref, [_row_sum(dxc * xs[0]), _row_sum(dxc * xs[1]), _row_sum(dxc * xs[2]), _row_sum(dxc * xs[3]),
                            _row_sum(dxc), _row_sum(dpa), _row_sum(dpx), dlam])

    rev = lambda i: (nsteps - 1 - i, 0)
    prev8 = lambda i: (jnp.maximum((nsteps - 1 - i) * t8 - 1, 0), 0)
    blkspec = pl.BlockSpec((tb, D_RNN), rev)
    halo8 = pl.BlockSpec((8, D_RNN), prev8)
    vec = _resident((1, D_RNN))
    return _pcall(
        body, (xr, xr, gr, h, h, *kept, drec, conv_w, wa, wx, lam), name="rnn_bwd", grid=(nsteps,),
        sem="arbitrary", comm=comm,
        in_specs=[blkspec, halo8, blkspec, blkspec, halo8] + [blkspec] * 6
        + [_resident((4, D_RNN)), _resident((D_RNN, D_RNN)), _resident((D_RNN, D_RNN)), vec],
        out_specs=[blkspec, blkspec, _acc((D_RNN, HEAD_DIM)), _acc((D_RNN, HEAD_DIM)), _acc((8, D_RNN))],
        out_shape=[jax.ShapeDtypeStruct((S, D_RNN), BF16), jax.ShapeDtypeStruct((S, D_RNN), BF16),
                   jax.ShapeDtypeStruct((D_RNN, HEAD_DIM), F32), jax.ShapeDtypeStruct((D_RNN, HEAD_DIM), F32),
                   jax.ShapeDtypeStruct((8, D_RNN), F32)],
        scratch_shapes=[pltpu.VMEM((8, D_RNN), F32), pltpu.VMEM((8, D_RNN), F32), pltpu.VMEM((tb, D_RNN), F32),
                        pltpu.VMEM((D_RNN, D_RNN), F32), pltpu.VMEM((D_RNN, D_RNN), F32)])


def _in_bwd(dq, dkc, dkp, dvc, dvp, dxr, dgr, dz1, w_in, comm=None):
    S = dz1.shape[0]
    tb = min(ATT_STEP * QBLK, S)
    nsteps = S // tb

    def body(dq_ref, dkc_ref, dkn_ref, dvc_ref, dvn_ref, dxr_ref, dgr_ref, dz1_ref, w_ref, dkv_ref, dx_ref):
        last = pl.program_id(0) == nsteps - 1

        def total(cur_ref, next_ref):
            nxt = jnp.where(last, 0.0, next_ref[...])
            tail = cur_ref[tb - QBLK:, :] + nxt
            return jnp.concatenate([cur_ref[:tb - QBLK, :], tail], axis=0) if tb > QBLK else tail

        dkv = jnp.concatenate([total(dkc_ref, dkn_ref), total(dvc_ref, dvn_ref)], axis=1).astype(BF16)
        dkv_ref[...] = dkv
        du = jnp.concatenate([dq_ref[...], dkv, dxr_ref[...], dgr_ref[...]], axis=1)
        dx_ref[...] = ALPHA * dz1_ref[...] + _dot(du, w_ref[...])

    nextp = pl.BlockSpec((QBLK, D_KV), lambda i: (jnp.minimum(i + 1, nsteps - 1), 0))
    return _pcall(
        body, (dq, dkc, dkp, dvc, dvp, dxr, dgr, dz1, w_in), name="in_bwd", grid=(nsteps,), comm=comm,
        in_specs=[_rows(tb, D_ATT), _rows(tb, D_KV), nextp, _rows(tb, D_KV), nextp,
                  _rows(tb, D_RNN), _rows(tb, D_RNN), _rows(tb, D_MODEL), _resident((D_IN, D_MODEL))],
        out_specs=[_rows(tb, 2 * D_KV), _rows(tb, D_MODEL)],
        out_shape=[jax.ShapeDtypeStruct((S, 2 * D_KV), BF16), jax.ShapeDtypeStruct((S, D_MODEL), F32)])


def _block_diag(w):
    eye = jnp.eye(RNN_BLOCKS, dtype=w.dtype)
    return (w[:, :, None, :] * eye[:, None, :, None]).reshape(D_RNN, D_RNN).astype(BF16)


def _adamw(w, g, m, v):
    m = ADAM_B1 * m + (1.0 - ADAM_B1) * g
    v = ADAM_B2 * v + (1.0 - ADAM_B2) * (g * g)
    m_hat = m / (1.0 - ADAM_B1 ** ADAM_STEP)
    v_hat = v / (1.0 - ADAM_B2 ** ADAM_STEP)
    delta = -ADAM_LR * (m_hat / (jnp.sqrt(v_hat) + ADAM_EPS) + ADAM_WD * w)
    return delta, m, v


def _sum_adamw(parts, w, m, v, name):
    parts = parts if isinstance(parts, (list, tuple)) else [parts]
    R, C = w.shape
    rb = R if R <= 256 else (256 if parts[0].shape[1] % 256 == 0 else 128)
    per = parts[0].shape[1] // rb
    assert R % rb == 0 and parts[0].shape[1] % rb == 0
    n = len(parts)

    def body(*refs):
        p_refs = refs[:n]
        w_ref, m_ref, v_ref, g_out, d_out, m_out, v_out = refs[n:]
        which = pl.program_id(0) // per

        def total(p_ref):
            g = p_ref[0].astype(F32)
            for d in range(1, N_DEV):
                g = g + p_ref[d].astype(F32)
            return g

        g = total(p_refs[0])
        for j in range(1, n):
            g = jnp.where(which == j, total(p_refs[j]), g)
        delta, mn, vn = _adamw(w_ref[...], g, m_ref[...], v_ref[...])
        g_out[...] = g
        d_out[...] = delta
        m_out[...] = mn
        v_out[...] = vn

    def part_spec(j):
        return pl.BlockSpec((N_DEV, rb, C), lambda i: (0, jnp.clip(i - j * per, 0, per - 1), 0))

    blk = _rows(rb, C)
    out = jax.ShapeDtypeStruct((R, C), F32)
    return pl.pallas_call(
        body, name=name, grid=(R // rb,),
        in_specs=[part_spec(j) for j in range(n)] + [blk, blk, blk],
        out_specs=[blk, blk, blk, blk], out_shape=[out, out, out, out],
        compiler_params=_params("parallel"),
    )(*parts, w, m, v)


_SMALL = [("attn_sinks", "s", 0, 1, None), ("rnn_conv_w", "r", 0, 4, "cols"), ("rnn_conv_b", "r", 4, 1, None),
          ("gate_a_w", "a", 0, D_RNN, None), ("gate_a_b", "r", 5, 1, None), ("gate_x_w", "x", 0, D_RNN, None),
          ("gate_x_b", "r", 6, 1, None), ("lru_lambda", "r", 7, 1, None), ("ln1_g", "d", 0, 1, None),
          ("ln1_b", "d", 1, 1, None), ("ffn_conv_w", "f", 0, 3, "cols"), ("ffn_conv_b", "f", 3, 1, None),
          ("ple_gate_b", "t", 2, 1, None), ("ln2_g", "t", 0, 1, None), ("ln2_b", "t", 1, 1, None)]
_LOSS_ROW = 3


_ACC_COLS = {"t": (0, D_MODEL), "f": (D_MODEL, D_FF), "d": (D_MODEL + D_FF, D_MODEL), "s": (2 * D_MODEL + D_FF, 128),
             "r": (2 * D_MODEL + D_FF + 128, D_RNN)}
_ACC_WIDTH = 2 * D_MODEL + D_FF + 128 + D_RNN


def _small_update(rows_all, gates_all, params):
    flat = [arr for triple in params for arr in triple]
    n_par = len(_SMALL)

    def body(*refs):
        rows_ref, gates_ref = refs[:2]
        p_refs = refs[2:2 + 3 * n_par]
        loss_ref = refs[2 + 3 * n_par]
        o_refs = refs[3 + 3 * n_par:3 + 7 * n_par]
        rows_s, tmp_r, tmp_f = refs[3 + 7 * n_par:]
        me = _dev_index(*_place())
        rows_sum, gates_sum = rows_ref[0], gates_ref[0]
        for d in range(1, N_DEV):
            rows_sum = rows_sum + rows_ref[d]
            gates_sum = gates_sum + gates_ref[d]
        rows_s[...] = rows_sum
        t0 = _ACC_COLS["t"][0]
        loss_ref[...] = rows_s[_LOSS_ROW:_LOSS_ROW + 1, t0:t0 + 128]
        for i, (name, key, row, rows, how) in enumerate(_SMALL):
            w_ref, m_ref, v_ref = p_refs[3 * i:3 * i + 3]
            g_out, d_out, m_out, v_out = o_refs[4 * i:4 * i + 4]
            if key == "a":
                g = gates_sum[:, :HEAD_DIM]
            elif key == "x":
                g = gates_sum[:, HEAD_DIM:]
            elif how == "cols":
                c0, width = _ACC_COLS[key]
                full = rows_s[:, c0:c0 + width]
                shard = width // N_DEV
                mine = full[:, :shard]
                for d in range(1, N_DEV):
                    mine = jnp.where(me == d, full[:, d * shard:(d + 1) * shard], mine)
                tmp = tmp_r if key == "r" else tmp_f
                tmp[...] = mine
                g = tmp[row:row + rows, :]
            else:
                c0, width = _ACC_COLS[key]
                g = rows_s[row:row + rows, c0:c0 + width][:, :w_ref.shape[1]]
            delta, mn, vn = _adamw(w_ref[...], g, m_ref[...], v_ref[...])
            g_out[...] = g
            d_out[...] = delta
            m_out[...] = mn
            v_out[...] = vn

    outs = [jax.ShapeDtypeStruct((1, 128), F32)]
    for w, _, _ in params:
        outs += [jax.ShapeDtypeStruct(w.shape, F32)] * 4
    scratch = [pltpu.VMEM((8, _ACC_WIDTH), F32), pltpu.VMEM((8, D_RNN // N_DEV), F32), pltpu.VMEM((8, D_FF // N_DEV), F32)]
    res = pl.pallas_call(body, name="small_update", out_shape=outs, scratch_shapes=scratch)(rows_all, gates_all, *flat)
    return res[0], [res[1 + 4 * i:5 + 4 * i] for i in range(n_par)]


def kernel(x, p, w_in, attn_sinks, rnn_conv_w, rnn_conv_b, gate_a_w, gate_a_b, gate_x_w, gate_x_b, lru_lambda, w_out, ln1_g, ln1_b, w_ffn_up, ffn_conv_w, ffn_conv_b, w_ffn_down, ple_gate_w, ple_gate_b, ple_proj, ln2_g, ln2_b, loss_target, m_w_in, m_attn_sinks, m_rnn_conv_w, m_rnn_conv_b, m_gate_a_w, m_gate_a_b, m_gate_x_w, m_gate_x_b, m_lru_lambda, m_w_out, m_ln1_g, m_ln1_b, m_w_ffn_up, m_ffn_conv_w, m_ffn_conv_b, m_w_ffn_down, m_ple_gate_w, m_ple_gate_b, m_ple_proj, m_ln2_g, m_ln2_b, v_w_in, v_attn_sinks, v_rnn_conv_w, v_rnn_conv_b, v_gate_a_w, v_gate_a_b, v_gate_x_w, v_gate_x_b, v_lru_lambda, v_w_out, v_ln1_g, v_ln1_b, v_w_ffn_up, v_ffn_conv_w, v_ffn_conv_b, v_w_ffn_down, v_ple_gate_w, v_ple_gate_b, v_ple_proj, v_ln2_g, v_ln2_b):
    from_col_blocks = lambda g: g.transpose(1, 0, 2).reshape(g.shape[1], N_DEV * g.shape[2])

    xs, ps, tgt, sinks = x[0], p[0, 0], loss_target[0], attn_sinks[0]
    wa, wx = _block_diag(gate_a_w[0]), _block_diag(gate_x_w[0])

    conv_cols = jnp.concatenate([rnn_conv_w[0].reshape(1, -1), ffn_conv_w[0].reshape(1, -1)], axis=1)
    n_rc, n_fc = 4 * D_RNN // N_DEV, 3 * D_FF // N_DEV
    ((g_in,),) = _comm_call([_Gather([w_in[0].T.astype(BF16)])], "gather_w_in")
    w_in_full = g_in.reshape(D_IN, D_MODEL)

    (q, k, v, xr, gr), _ = _in_proj(xs, w_in_full)
    (att, lse), (g_out, g_conv) = _attn_fwd(
        q, k, v, sinks,
        comm=_Multi([_Gather([w_out[0].astype(BF16)]), _Bcast([jnp.broadcast_to(conv_cols, (8, n_rc + n_fc))])]))
    rcw = from_col_blocks(g_conv[:, 0, :n_rc].reshape(N_DEV, 4, D_RNN // N_DEV))
    fcw = from_col_blocks(g_conv[:, 0, n_rc:].reshape(N_DEV, 3, D_FF // N_DEV))
    (rec, h, *kept), (w_up,) = _rnn_fwd(xr, gr, rcw, rnn_conv_b, wa, wx, gate_a_b, gate_x_b, lru_lambda,
                                        comm=_Gather([w_ffn_up[0].astype(BF16)]))
    w_out_full = g_out.reshape(D_MODEL, D_MODEL)
    (z1, h1b, gate, act, gl, vdgl), (g_down, g_pg, g_pp) = _mix_ln1_up(
        xs, att, rec, w_out_full, ln1_g, ln1_b, w_up, fcw, ffn_conv_b,
        comm=_Gather([w_ffn_down[0].astype(BF16), ple_gate_w[0].astype(BF16), ple_proj[0].astype(BF16)]))
    dz2b, dpreb, dppb, dgc, dval, dh1p, acc_t = _tail(
        act, gl, vdgl, z1, h1b, ps, tgt, g_down.reshape(D_FF, D_MODEL), g_pg.reshape(D_MODEL, D_MODEL), ple_gate_b,
        from_col_blocks(g_pp), ln1_g, ln1_b, ln2_g, ln2_b)

    gd_down = _weight_grad([dz2b], [act], "down_grad", "rows_t", ts=1024)
    gd_pg = _weight_grad([h1b], [dpreb], "pg_grad", "rows", ts=1024)
    gd_pp = _weight_grad([ps], [dppb], "pp_grad", "cols", ts=1024)
    (dgate, dz1, dz1b, datt, drec, acc_f, acc_d), (r_down, r_pg, r_pp) = _up_bwd(
        dgc, gate, dval, dh1p, z1, w_up, fcw, w_out_full, ln1_g, comm=_Exchange([gd_down, gd_pg, gd_pp]))
    gd_up_top, gd_up_bot = _weight_grad([h1b], [dgate, dval], "up_grad", "cols", halves=True)
    gd_out = _weight_grad([att, rec], [dz1b], "out_grad", "rows", ts=1024)
    (dq, dkc, dkp, dvc, dvp, acc_s), (r_up_top,) = _attn_bwd(q, k, v, lse, datt, sinks, comm=_Exchange([gd_up_top]))
    early = jnp.concatenate([acc_t, acc_f, acc_d], axis=1)
    (dxr, dgr, g_wa, g_wx, acc_r), (r_up_bot, r_out, early_all) = _rnn_bwd(
        xr, gr, h, kept, drec, rcw, wa, wx, lru_lambda, comm=_Multi([_Exchange([gd_up_bot, gd_out]), _Bcast([early])]))
    (dkv, dx), _ = _in_bwd(dq, dkc, dkp, dvc, dvp, dxr, dgr, dz1, w_in_full)
    du_parts = [dq, dkv, dxr, dgr]
    lanes = D_RNN // 128
    late = jnp.concatenate([g_wa, g_wx], axis=1)
    late = jnp.concatenate([late, acc_s, acc_r.reshape(8, lanes, 128).transpose(1, 0, 2).reshape(8 * lanes, 128)], axis=0)
    width = D_MODEL // IN_GRAD_PARTS
    comm, r_parts = _Bcast([late]), []
    for part in range(IN_GRAD_PARTS):
        gd_part, got = _weight_grad(du_parts, [xs], f"in_grad_{part}", "rows", ts=1024, b_window=(part, width), comm=comm)
        if part == 0:
            (late_all,) = got
        else:
            r_parts += got
        comm = _Exchange([gd_part])
    r_parts += _comm_call([comm], "exchange_w_in")[0]
    r_in = jnp.concatenate(r_parts, axis=2)
    acc_r_all = late_all[:, D_RNN + 8:].reshape(N_DEV, lanes, 8, 128).transpose(0, 2, 1, 3).reshape(N_DEV, 8, D_RNN)
    small_parts = (jnp.concatenate([early_all, late_all[:, D_RNN:D_RNN + 8], acc_r_all], axis=2),
                   late_all[:, :D_RNN])

    outs = {}
    res = _sum_adamw(r_in, w_in[0].T, m_w_in[0].T, v_w_in[0].T, "adamw_w_in")
    outs["w_in"] = [r.T[None] for r in res]
    for name, parts, w, m, v in [("w_out", r_out, w_out, m_w_out, v_w_out),
                                 ("w_ffn_up", [r_up_top, r_up_bot], w_ffn_up, m_w_ffn_up, v_w_ffn_up),
                                 ("w_ffn_down", r_down, w_ffn_down, m_w_ffn_down, v_w_ffn_down),
                                 ("ple_gate_w", r_pg, ple_gate_w, m_ple_gate_w, v_ple_gate_w),
                                 ("ple_proj", r_pp, ple_proj, m_ple_proj, v_ple_proj)]:
        res = _sum_adamw(parts, w[0], m[0], v[0], "adamw_" + name)
        outs[name] = [r[None] for r in res]

    given = dict(attn_sinks=(attn_sinks, m_attn_sinks, v_attn_sinks), rnn_conv_w=(rnn_conv_w, m_rnn_conv_w, v_rnn_conv_w),
                 rnn_conv_b=(rnn_conv_b, m_rnn_conv_b, v_rnn_conv_b), gate_a_w=(gate_a_w, m_gate_a_w, v_gate_a_w),
                 gate_a_b=(gate_a_b, m_gate_a_b, v_gate_a_b), gate_x_w=(gate_x_w, m_gate_x_w, v_gate_x_w),
                 gate_x_b=(gate_x_b, m_gate_x_b, v_gate_x_b), lru_lambda=(lru_lambda, m_lru_lambda, v_lru_lambda),
                 ln1_g=(ln1_g, m_ln1_g, v_ln1_g), ln1_b=(ln1_b, m_ln1_b, v_ln1_b),
                 ffn_conv_w=(ffn_conv_w, m_ffn_conv_w, v_ffn_conv_w), ffn_conv_b=(ffn_conv_b, m_ffn_conv_b, v_ffn_conv_b),
                 ple_gate_b=(ple_gate_b, m_ple_gate_b, v_ple_gate_b), ln2_g=(ln2_g, m_ln2_g, v_ln2_g),
                 ln2_b=(ln2_b, m_ln2_b, v_ln2_b))
    as_2d = lambda a: a.reshape(-1, a.shape[-1])
    loss_row, small_res = _small_update(*small_parts, [tuple(as_2d(a) for a in given[n]) for n, *_ in _SMALL])
    loss = loss_row[0, 0]
    for (n, *_), res in zip(_SMALL, small_res):
        outs[n] = [r.reshape(given[n][0].shape) for r in res]

    order = ["w_in", "attn_sinks", "rnn_conv_w", "rnn_conv_b", "gate_a_w", "gate_a_b", "gate_x_w", "gate_x_b",
             "lru_lambda", "w_out", "ln1_g", "ln1_b", "w_ffn_up", "ffn_conv_w", "ffn_conv_b", "w_ffn_down",
             "ple_gate_w", "ple_gate_b", "ple_proj", "ln2_g", "ln2_b"]
    return (loss, dx[None], *[outs[n][0] for n in order], *[outs[n][1] for n in order],
            *[outs[n][2] for n in order], *[outs[n][3] for n in order])
```

```python
import jax
import jax.numpy as jnp
from jax import lax
from jax.experimental import pallas as pl
from jax.experimental.pallas import tpu as pltpu

F32 = jnp.float32
BF16 = jnp.bfloat16

D_MODEL = 1024
D_ATT = 512
D_KV = 128
HEAD_DIM = 64
N_HEADS = 8
N_KV = 2
D_RNN = 512
RNN_BLOCKS = 8
D_IN = 1792
D_FF = 3072
PLE_DIM = 256
QBLK = 128
N_DEV = 8
ALPHA = float(2 ** 0.25)
LN_EPS = 1e-5
LRU_C = 8.0
ADAM_LR, ADAM_B1, ADAM_B2, ADAM_EPS, ADAM_WD, ADAM_STEP = 0.001, 0.9, 0.999, 1e-08, 0.01, 10

V7X_VMEM_LIMIT = 56 * 1024 * 1024
MESH = pl.DeviceIdType.MESH


def _params(*sem, vmem=V7X_VMEM_LIMIT):
    return pltpu.CompilerParams(dimension_semantics=sem or None, vmem_limit_bytes=vmem)


def _resident(shape):
    return pl.BlockSpec(shape, lambda *_: (0,) * len(shape), pipeline_mode=pl.Buffered(1))


def _rows(tb, cols):
    return pl.BlockSpec((tb, cols), lambda i: (i, 0))


def _acc(shape):
    return pl.BlockSpec(shape, lambda *_: (0,) * len(shape))


def _dot(a, b):
    return jnp.dot(a, b, preferred_element_type=F32)


def _dot_nt(a, b):
    return lax.dot_general(a, b, (((1,), (1,)), ((), ())), preferred_element_type=F32)


def _dot_tn(a, b):
    return lax.dot_general(a, b, (((0,), (0,)), ((), ())), preferred_element_type=F32)


def _sigmoid(x):
    return 1.0 / (1.0 + jnp.exp(-x))


_GELU_C = 0.7978845608028654
_GELU_K = 0.044715


def _gelu_and_grad(x):
    u = x * x
    t = jnp.tanh(x * (_GELU_C + (_GELU_C * _GELU_K) * u))
    hp = 0.5 + 0.5 * t
    dg = hp + x * (0.5 - 0.5 * (t * t)) * (_GELU_C + (3.0 * _GELU_C * _GELU_K) * u)
    return x * hp, dg


def _gelu(x):
    return 0.5 * x * (1.0 + jnp.tanh(_GELU_C * (x + _GELU_K * x * x * x)))


def _ln_stats(z):
    mu = jnp.mean(z, axis=-1, keepdims=True)
    zc = z - mu
    var = jnp.mean(zc * zc, axis=-1, keepdims=True)
    rstd = lax.rsqrt(var + LN_EPS)
    return zc * rstd, rstd


def _ln_bwd(dy, xhat, rstd, g):
    dxh = dy * g
    m1 = jnp.mean(dxh, axis=-1, keepdims=True)
    m2 = jnp.mean(dxh * xhat, axis=-1, keepdims=True)
    return rstd * (dxh - m1 - xhat * m2)


def _softplus_neg(lam):
    u = jnp.exp(-jnp.abs(lam))
    w = 1.0 + u
    d = w - 1.0
    log1p_u = jnp.where(d == 0.0, u, jnp.log(w) * (u / jnp.where(d == 0.0, 1.0, d)))
    return jnp.maximum(-lam, 0.0) + log1p_u


def _shift_down(x, halo, s):
    xs = pltpu.roll(x, s, 0)
    hs = pltpu.roll(halo, s, 0)
    row8 = lax.broadcasted_iota(jnp.int32, hs.shape, 0)
    first = jnp.where(row8 < s, hs, xs[:8])
    return jnp.concatenate([first, xs[8:]], axis=0)


def _shift_up(x, halo, s):
    n = x.shape[0]
    xs = pltpu.roll(x, n - s, 0)
    hs = pltpu.roll(halo, 8 - s, 0)
    row8 = lax.broadcasted_iota(jnp.int32, hs.shape, 0)
    last = jnp.where(row8 >= 8 - s, hs, xs[n - 8:])
    return jnp.concatenate([xs[:n - 8], last], axis=0)


def _row_sum(x):
    return jnp.sum(x, axis=0, keepdims=True)


def _put_rows(acc_ref, rows):
    row8 = lax.broadcasted_iota(jnp.int32, acc_ref.shape, 0)
    upd = jnp.zeros(acc_ref.shape, F32)
    for r, vec in enumerate(rows):
        upd = jnp.where(row8 == r, vec, upd)
    acc_ref[...] += upd


def _place():
    return lax.axis_index("x"), lax.axis_index("y"), lax.axis_index("c")


def _dev_index(px, py, pc):
    return 4 * px + 2 * py + pc


_ANY = pl.BlockSpec(memory_space=pl.ANY)


class _Gather:
    def __init__(self, arrays):
        self.arrays = list(arrays)
        self.n = len(self.arrays)

    def out_shape(self):
        return [jax.ShapeDtypeStruct((N_DEV,) + s.shape, s.dtype) for s in self.arrays]

    def scratch(self):
        return [pltpu.SemaphoreType.DMA((self.n, 7)), pltpu.SemaphoreType.DMA((self.n, 7)),
                pltpu.SemaphoreType.DMA((self.n,))]

    def _parts(self, ins, outs, sems):
        send_sems, recv_sems, local_sems = sems
        x, y, c = _place()
        me, sibling = (x, y, c), (x, y, 1 - c)
        chips = [(1 - x, y), (x, 1 - y), (1 - x, 1 - y)]

        def copy(a, k, block, to, src=None):
            rows = outs[a].at[_dev_index(*block)]
            return pltpu.make_async_remote_copy(
                src_ref=rows if src is None else src, dst_ref=rows, send_sem=send_sems.at[a, k],
                recv_sem=recv_sems.at[a, k], device_id=to, device_id_type=MESH)

        rng = range(self.n)
        mine = [pltpu.make_async_copy(ins[a], outs[a].at[_dev_index(*me)], local_sems.at[a]) for a in rng]
        first = [copy(a, 0, me, sibling, src=ins[a]) for a in rng]
        first += [copy(a, 1 + j, me, (*chip, c), src=ins[a]) for j, chip in enumerate(chips) for a in rng]
        landed = [copy(a, 1 + j, (*chip, c), me) for j, chip in enumerate(chips) for a in rng]
        passed = [copy(a, 4 + j, (*chip, c), sibling) for j, chip in enumerate(chips) for a in rng]
        from_sibling = [copy(a, 0, sibling, me) for a in rng]
        from_sibling += [copy(a, 4 + j, (*chip, 1 - c), me) for j, chip in enumerate(chips) for a in rng]
        return mine, first, landed, passed, from_sibling

    def start(self, ins, outs, sems):
        mine, first, _, _, _ = self._parts(ins, outs, sems)
        for cp in mine + first:
            cp.start()

    def forward(self, ins, outs, sems):
        _, _, landed, passed, _ = self._parts(ins, outs, sems)
        for got, fwd in zip(landed, passed):
            got.wait_recv()
            fwd.start()

    def finish(self, ins, outs, sems):
        mine, first, _, passed, from_sibling = self._parts(ins, outs, sems)
        for cp in from_sibling:
            cp.wait_recv()
        for cp in first + passed:
            cp.wait_send()
        for cp in mine:
            cp.wait()

    def before(self, ins, outs, sems, step, nsteps):
        pl.when(step == 0)(lambda: self.start(ins, outs, sems))
        pl.when(step == (7 * nsteps) // 8)(lambda: self.forward(ins, outs, sems))

    def after(self, ins, outs, sems, step, nsteps):
        pl.when(step == nsteps - 1)(lambda: self.finish(ins, outs, sems))


class _Exchange:
    def __init__(self, arrays):
        self.arrays = list(arrays)
        self.n = len(self.arrays)

    def out_shape(self):
        return [jax.ShapeDtypeStruct(b.shape, b.dtype) for b in self.arrays]

    def scratch(self):
        return [pltpu.SemaphoreType.DMA((self.n, 7)), pltpu.SemaphoreType.DMA((self.n, 7)),
                pltpu.SemaphoreType.DMA((self.n,))]

    def _parts(self, ins, outs, sems):
        send_sems, recv_sems, local_sems = sems
        x, y, c = _place()
        me = _dev_index(x, y, c)
        peers = [(x ^ (k >> 2), y ^ ((k >> 1) & 1), c ^ (k & 1)) for k in range(1, N_DEV)]
        rng = range(self.n)
        mine = [pltpu.make_async_copy(ins[a].at[me], outs[a].at[me], local_sems.at[a]) for a in rng]
        sent = [pltpu.make_async_remote_copy(
            src_ref=ins[a].at[_dev_index(*to)], dst_ref=outs[a].at[me], send_sem=send_sems.at[a, k],
            recv_sem=recv_sems.at[a, k], device_id=to, device_id_type=MESH) for k, to in enumerate(peers) for a in rng]
        arrivals = [pltpu.make_async_remote_copy(
            src_ref=ins[a].at[me], dst_ref=outs[a].at[_dev_index(*frm)], send_sem=send_sems.at[a, k],
            recv_sem=recv_sems.at[a, k], device_id=frm, device_id_type=MESH) for k, frm in enumerate(peers) for a in rng]
        return mine, sent, arrivals

    def start(self, ins, outs, sems):
        mine, sent, _ = self._parts(ins, outs, sems)
        for cp in mine + sent:
            cp.start()

    def finish(self, ins, outs, sems):
        mine, sent, arrivals = self._parts(ins, outs, sems)
        for cp in arrivals:
            cp.wait_recv()
        for cp in sent:
            cp.wait_send()
        for cp in mine:
            cp.wait()

    def before(self, ins, outs, sems, step, nsteps):
        pl.when(step == 0)(lambda: self.start(ins, outs, sems))

    def after(self, ins, outs, sems, step, nsteps):
        pl.when(step == nsteps - 1)(lambda: self.finish(ins, outs, sems))


class _Bcast(_Exchange):
    def out_shape(self):
        return [jax.ShapeDtypeStruct((N_DEV,) + s.shape, s.dtype) for s in self.arrays]

    def _parts(self, ins, outs, sems):
        send_sems, recv_sems, local_sems = sems
        x, y, c = _place()
        me = _dev_index(x, y, c)
        peers = [(x ^ (k >> 2), y ^ ((k >> 1) & 1), c ^ (k & 1)) for k in range(1, N_DEV)]
        rng = range(self.n)
        mine = [pltpu.make_async_copy(ins[a], outs[a].at[me], local_sems.at[a]) for a in rng]
        sent = [pltpu.make_async_remote_copy(
            src_ref=ins[a], dst_ref=outs[a].at[me], send_sem=send_sems.at[a, k], recv_sem=recv_sems.at[a, k],
            device_id=to, device_id_type=MESH) for k, to in enumerate(peers) for a in rng]
        arrivals = [pltpu.make_async_remote_copy(
            src_ref=ins[a], dst_ref=outs[a].at[_dev_index(*frm)], send_sem=send_sems.at[a, k],
            recv_sem=recv_sems.at[a, k], device_id=frm, device_id_type=MESH) for k, frm in enumerate(peers) for a in rng]
        return mine, sent, arrivals


class _Multi:
    def __init__(self, comms):
        self.comms = list(comms)
        self.arrays = [arr for c in self.comms for arr in c.arrays]
        self.n = len(self.arrays)

    def out_shape(self):
        return [s for c in self.comms for s in c.out_shape()]

    def scratch(self):
        return [s for c in self.comms for s in c.scratch()]

    def _each(self, ins, outs, sems):
        a = 0
        for j, c in enumerate(self.comms):
            yield c, ins[a:a + c.n], outs[a:a + c.n], sems[3 * j:3 * j + 3]
            a += c.n

    def before(self, ins, outs, sems, step, nsteps):
        for c, ci, co, cs in self._each(ins, outs, sems):
            c.before(ci, co, cs, step, nsteps)

    def after(self, ins, outs, sems, step, nsteps):
        for c, ci, co, cs in self._each(ins, outs, sems):
            c.after(ci, co, cs, step, nsteps)


def _comm_call(comms, name):
    ns = [c.n for c in comms]
    n = sum(ns)

    def body(*refs):
        parts, a, s = [], 0, 2 * n
        for c in comms:
            parts.append((c, refs[a:a + c.n], refs[n + a:n + a + c.n], refs[s:s + 3]))
            a, s = a + c.n, s + 3
        for c, ins, outs, sems in parts:
            c.start(ins, outs, sems)
        for c, ins, outs, sems in parts:
            if isinstance(c, _Gather):
                c.forward(ins, outs, sems)
        for c, ins, outs, sems in parts:
            c.finish(ins, outs, sems)

    res = pl.pallas_call(
        body, name=name, in_specs=[_ANY] * n, out_specs=[_ANY] * n,
        out_shape=[s for c in comms for s in c.out_shape()], scratch_shapes=[s for c in comms for s in c.scratch()],
    )(*[arr for c in comms for arr in c.arrays])
    out, a = [], 0
    for k in ns:
        out.append(res[a:a + k])
        a += k
    return out


def _pcall(body, args, *, name, grid, in_specs, out_specs, out_shape, scratch_shapes=(), sem="parallel", comm=None,
           step_axis=0):
    sem = (sem,) * len(grid) if isinstance(sem, str) else sem
    if comm is None:
        res = pl.pallas_call(body, name=name, grid=grid, in_specs=in_specs, out_specs=out_specs, out_shape=out_shape,
                             scratch_shapes=list(scratch_shapes), compiler_params=_params(*sem))(*args)
        return res, []
    n_in, n_out, n_scr, n = len(in_specs), len(out_specs), len(scratch_shapes), comm.n
    nsteps = grid[step_axis]
    assert all(g == 1 for ax, g in enumerate(grid) if ax != step_axis)

    def hosted(*refs):
        ins, cin = refs[:n_in], refs[n_in:n_in + n]
        o0 = n_in + n
        outs, cout = refs[o0:o0 + n_out], refs[o0 + n_out:o0 + n_out + n]
        s0 = o0 + n_out + n
        scr, sems = refs[s0:s0 + n_scr], refs[s0 + n_scr:]
        step = pl.program_id(step_axis)
        comm.before(cin, cout, sems, step, nsteps)
        body(*ins, *outs, *scr)
        comm.after(cin, cout, sems, step, nsteps)

    res = pl.pallas_call(
        hosted, name=name, grid=grid, in_specs=list(in_specs) + [_ANY] * n, out_specs=list(out_specs) + [_ANY] * n,
        out_shape=list(out_shape) + comm.out_shape(), scratch_shapes=list(scratch_shapes) + comm.scratch(),
        compiler_params=_params(*(("arbitrary",) * len(grid))))(*args, *comm.arrays)
    return res[:n_out], res[n_out:]


def _in_proj(x, w_in_t, comm=None):
    S = x.shape[0]
    tb = min(512, S)

    def body(x_ref, w_ref, q_ref, k_ref, v_ref, xr_ref, gr_ref):
        u = _dot_nt(x_ref[...].astype(BF16), w_ref[...])
        q_ref[...] = (u[:, :D_ATT] * (HEAD_DIM ** -0.5)).astype(BF16)
        k_ref[...] = u[:, D_ATT:D_ATT + D_KV].astype(BF16)
        v_ref[...] = u[:, D_ATT + D_KV:D_ATT + 2 * D_KV].astype(BF16)
        xr_ref[...] = u[:, D_ATT + 2 * D_KV:D_ATT + 2 * D_KV + D_RNN]
        gr_ref[...] = u[:, D_ATT + 2 * D_KV + D_RNN:]

    return _pcall(
        body, (x, w_in_t), name="in_proj", grid=(S // tb,), comm=comm,
        in_specs=[_rows(tb, D_MODEL), _resident((D_IN, D_MODEL))],
        out_specs=[_rows(tb, D_ATT), _rows(tb, D_KV), _rows(tb, D_KV), _rows(tb, D_RNN), _rows(tb, D_RNN)],
        out_shape=[jax.ShapeDtypeStruct((S, D_ATT), BF16), jax.ShapeDtypeStruct((S, D_KV), BF16),
                   jax.ShapeDtypeStruct((S, D_KV), BF16), jax.ShapeDtypeStruct((S, D_RNN), F32),
                   jax.ShapeDtypeStruct((S, D_RNN), F32)])


GROUP = N_HEADS // N_KV


def _band_mask(i):
    qi = lax.broadcasted_iota(jnp.int32, (GROUP * QBLK, 2 * QBLK), 0) & (QBLK - 1)
    sj = lax.broadcasted_iota(jnp.int32, (GROUP * QBLK, 2 * QBLK), 1)
    return (sj > qi) & (sj <= qi + QBLK) & ((sj >= QBLK) | (i > 0))


def _stack_heads(x, g):
    return jnp.concatenate([x[:, (g * GROUP + hh) * HEAD_DIM:(g * GROUP + hh + 1) * HEAD_DIM] for hh in range(GROUP)],
                           axis=0)


def _unstack_heads(x4):
    return [x4[hh * QBLK:(hh + 1) * QBLK] for hh in range(GROUP)]


def _sink_column(sink_ref, g):
    head = lax.broadcasted_iota(jnp.int32, (GROUP * QBLK, 1), 0) // QBLK
    col = jnp.full((GROUP * QBLK, 1), sink_ref[g * GROUP], F32)
    for hh in range(1, GROUP):
        col = jnp.where(head == hh, sink_ref[g * GROUP + hh], col)
    return col


ATT_STEP = 4
IN_GRAD_PARTS = 2


def _attn_specs(nq=1):
    cur = lambda i: (i, 0)
    prev = lambda i: (jnp.maximum(nq * i - 1, 0), 0)
    return [pl.BlockSpec((nq * QBLK, D_KV), cur), pl.BlockSpec((QBLK, D_KV), prev),
            pl.BlockSpec((nq * QBLK, D_KV), cur), pl.BlockSpec((QBLK, D_KV), prev)]


def _attn_fwd(q, k, v, sinks, comm=None):
    S = q.shape[0]
    nq = min(ATT_STEP, S // QBLK)

    def body(sink_ref, q_ref, kc_ref, kp_ref, vc_ref, vp_ref, o_ref, lse_ref):
        first = pl.program_id(0) * nq
        kall = jnp.concatenate([kp_ref[...], kc_ref[...]], axis=0)
        vall = jnp.concatenate([vp_ref[...], vc_ref[...]], axis=0)
        for b in range(nq):
            valid = _band_mask(first + b)
            rows = slice(b * QBLK, (b + 1) * QBLK)
            keys = slice(b * QBLK, (b + 2) * QBLK)
            qv = q_ref[rows, :]
            outs = []
            for g in range(N_KV):
                kcat = kall[keys, g * HEAD_DIM:(g + 1) * HEAD_DIM]
                vcat = vall[keys, g * HEAD_DIM:(g + 1) * HEAD_DIM]
                s = jnp.where(valid, _dot_nt(_stack_heads(qv, g), kcat), -1e30)
                sink = _sink_column(sink_ref, g)
                m = jnp.maximum(jnp.max(s, axis=1, keepdims=True), sink)
                p = jnp.exp(s - m)
                l = jnp.sum(p, axis=1, keepdims=True) + jnp.exp(sink - m)
                outs += _unstack_heads(_dot(p.astype(BF16), vcat) / l)
                lse_ref[(b * N_KV + g) * GROUP * QBLK:(b * N_KV + g + 1) * GROUP * QBLK, :] = m + jnp.log(l)
            o_ref[rows, :] = jnp.concatenate(outs, axis=1).astype(BF16)

    lse_rows = nq * N_HEADS * QBLK
    return _pcall(
        body, (sinks, q, k, k, v, v), name="attn_fwd", grid=(S // (nq * QBLK),), comm=comm,
        in_specs=[pl.BlockSpec(memory_space=pltpu.SMEM), _rows(nq * QBLK, D_ATT)] + _attn_specs(nq),
        out_specs=[_rows(nq * QBLK, D_ATT), _rows(lse_rows, 1)],
        out_shape=[jax.ShapeDtypeStruct((S, D_ATT), BF16), jax.ShapeDtypeStruct((S * N_HEADS, 1), F32)])


def _w_rows(w_ref):
    return [w_ref[k:k + 1, :] for k in range(w_ref.shape[0])]


def _conv4(x, halo, w, b):
    y = b + w[3] * x
    for s in (1, 2, 3):
        y = y + w[3 - s] * _shift_down(x, halo, s)
    return y


def _rnn_gates(xc, wa, wx, ba, bx, sp):
    xcb = xc.astype(BF16)
    r = _sigmoid(_dot(xcb, wa) + ba)
    ig = _sigmoid(_dot(xcb, wx) + bx)
    la = -LRU_C * r * sp
    a = jnp.exp(la)
    t = jnp.tanh(la)
    f = jnp.sqrt(-2.0 * t / (1.0 - t))
    return r, ig, a, f


def _rnn_fwd(xr, gr, conv_w, conv_b, wa, wx, ba, bx, lam, comm=None):
    S = xr.shape[0]
    tb = min(512, S)

    def body(xr_ref, gr_ref, cw_ref, cb_ref, wa_ref, wx_ref, ba_ref, bx_ref, lam_ref, rec_ref, h_ref,
             xc_ref, r_ref, ig_ref, a_ref, f_ref, halo_s, hc_s, a_s, b_s):
        @pl.when(pl.program_id(0) == 0)
        def _():
            halo_s[...] = jnp.zeros_like(halo_s)
            hc_s[...] = jnp.zeros_like(hc_s)

        x = xr_ref[...]
        xc = _conv4(x, halo_s[...], _w_rows(cw_ref), cb_ref[...])
        halo_s[...] = x[tb - 8:]
        r, ig, a, f = _rnn_gates(xc, wa_ref[...], wx_ref[...], ba_ref[...], bx_ref[...], _softplus_neg(lam_ref[...]))
        xc_ref[...] = xc
        r_ref[...] = r
        ig_ref[...] = ig
        a_ref[...] = a
        f_ref[...] = f
        a_s[...] = a
        b_s[...] = f * ig * xc
        row8 = lax.broadcasted_iota(jnp.int32, (8, D_RNN), 0)

        def tile(t, hc):
            o = pl.multiple_of(t * 8, 8)
            at = a_s[pl.ds(o, 8), :]
            bt = b_s[pl.ds(o, 8), :]
            for s in (1, 2, 4):
                keep = row8 >= s
                a_sh = jnp.where(keep, pltpu.roll(at, s, 0), 1.0)
                b_sh = jnp.where(keep, pltpu.roll(bt, s, 0), 0.0)
                bt = at * b_sh + bt
                at = at * a_sh
            ht = at * hc + bt
            b_s[pl.ds(o, 8), :] = ht
            return _row_sum(jnp.where(row8 == 7, ht, 0.0))

        hc_s[0:1, :] = lax.fori_loop(0, tb // 8, tile, hc_s[0:1, :], unroll=2)
        h = b_s[...]
        h_ref[...] = h
        rec_ref[...] = (h * _gelu(gr_ref[...])).astype(BF16)

    vec = _resident((1, D_RNN))
    kept = jax.ShapeDtypeStruct((S, D_RNN), F32)
    return _pcall(
        body, (xr, gr, conv_w, conv_b, wa, wx, ba, bx, lam), name="rnn_fwd", grid=(S // tb,), sem="arbitrary", comm=comm,
        in_specs=[_rows(tb, D_RNN), _rows(tb, D_RNN), _resident((4, D_RNN)), vec,
                  _resident((D_RNN, D_RNN)), _resident((D_RNN, D_RNN)), vec, vec, vec],
        out_specs=[_rows(tb, D_RNN)] * 7,
        out_shape=[jax.ShapeDtypeStruct((S, D_RNN), BF16), kept, kept, kept, kept, kept, kept],
        scratch_shapes=[pltpu.VMEM((8, D_RNN), F32), pltpu.VMEM((8, D_RNN), F32),
                        pltpu.VMEM((tb, D_RNN), F32), pltpu.VMEM((tb, D_RNN), F32)])


def _mix_ln1_up(x, att, rec, w_out, ln1_g, ln1_b, w_up, fcw, fcb, comm=None):
    S = x.shape[0]
    tb = min(256, S)
    nblk, _, wblk = w_up.shape
    half = nblk // 2

    def body(x_ref, att_ref, rec_ref, wo_ref, g_ref, b_ref, wu_ref, fcw_ref, fcb_ref,
             z1_ref, h1b_ref, gate_ref, act_ref, gl_ref, vdgl_ref, halo_s):
        @pl.when(pl.program_id(0) == 0)
        def _():
            halo_s[...] = jnp.zeros_like(halo_s)

        z1 = ALPHA * x_ref[...] + _dot(att_ref[...], wo_ref[:D_ATT, :]) + _dot(rec_ref[...], wo_ref[D_ATT:, :])
        z1_ref[...] = z1
        xhat, _ = _ln_stats(z1)
        h1b = (xhat * g_ref[...] + b_ref[...]).astype(BF16)
        h1b_ref[...] = h1b
        for jj in range(half):
            cols = slice(jj * wblk, (jj + 1) * wblk)
            gate = _dot(h1b, wu_ref[jj])
            val = _dot(h1b, wu_ref[jj + half])
            halo = halo_s[:, cols]
            conv = (fcb_ref[:, cols] + fcw_ref[2:3, cols] * gate + fcw_ref[1:2, cols] * _shift_down(gate, halo, 1)
                    + fcw_ref[0:1, cols] * _shift_down(gate, halo, 2))
            halo_s[:, cols] = gate[tb - 8:]
            gl, dgl = _gelu_and_grad(conv)
            gate_ref[:, cols] = gate.astype(BF16)
            act_ref[:, cols] = (gl * val).astype(BF16)
            gl_ref[:, cols] = gl.astype(BF16)
            vdgl_ref[:, cols] = (val * dgl).astype(BF16)

    vec = _resident((1, D_MODEL))
    wide = jax.ShapeDtypeStruct((S, D_FF), BF16)
    return _pcall(
        body, (x, att, rec, w_out, ln1_g, ln1_b, w_up, fcw, fcb), name="mix_ln1_up", grid=(S // tb,),
        sem="arbitrary", comm=comm,
        in_specs=[_rows(tb, D_MODEL), _rows(tb, D_ATT), _rows(tb, D_RNN), _resident((D_MODEL, D_MODEL)), vec, vec,
                  _resident(w_up.shape), _resident((3, D_FF)), _resident((1, D_FF))],
        out_specs=[_rows(tb, D_MODEL), _rows(tb, D_MODEL)] + [_rows(tb, D_FF)] * 4,
        out_shape=[jax.ShapeDtypeStruct((S, D_MODEL), F32), jax.ShapeDtypeStruct((S, D_MODEL), BF16), wide, wide, wide, wide],
        scratch_shapes=[pltpu.VMEM((8, D_FF), F32)])


def _tail(act, gl, vdgl, z1, h1b, p, tgt, w_down, w_pg, b_pg, w_pp, ln1_g, ln1_b, ln2_g, ln2_b):
    S = z1.shape[0]
    tb = min(256, S)

    def body(act_ref, gl_ref, vdgl_ref, z1_ref, h1b_ref, p_ref, t_ref, wd_ref, wpg_ref, bpg_ref, wpp_ref,
             g1_ref, b1_ref, g2_ref, b2_ref, dz2_ref, dpre_ref, dpp_ref, dgc_ref, dval_ref, dh1_ref, acc_ref):
        i = pl.program_id(0)

        @pl.when(i == 0)
        def _():
            acc_ref[...] = jnp.zeros_like(acc_ref)

        ffn = _dot(act_ref[...], wd_ref[...])
        xhat1, _ = _ln_stats(z1_ref[...])
        h1 = xhat1 * g1_ref[...] + b1_ref[...]
        sg = _sigmoid(_dot(h1b_ref[...], wpg_ref[...]) + bpg_ref[...])
        pp = _dot(p_ref[...].astype(BF16), wpp_ref[...])
        z2 = ALPHA * h1 + ffn + sg * pp
        xhat2, rstd2 = _ln_stats(z2)
        y = xhat2 * g2_ref[...] + b2_ref[...]
        err = y - t_ref[...]
        dy = err * (1.0 / D_MODEL)
        loss = 0.5 * jnp.sum(jnp.sum(err * err, axis=1, keepdims=True), axis=0, keepdims=True) * (1.0 / D_MODEL)
        dz2 = _ln_bwd(dy, xhat2, rstd2, g2_ref[...])
        dz2b = dz2.astype(BF16)
        dz2_ref[...] = dz2b
        dpre = dz2 * pp * sg * (1.0 - sg)
        dpreb = dpre.astype(BF16)
        dpre_ref[...] = dpreb
        dpp_ref[...] = (dz2 * sg).astype(BF16)
        dh1_ref[...] = ALPHA * dz2 + _dot_nt(dpreb, wpg_ref[...])
        dactb = _dot_nt(dz2b, wd_ref[...]).astype(BF16)
        dval_ref[...] = dactb * gl_ref[...]
        dgc_ref[...] = dactb * vdgl_ref[...]
        _put_rows(acc_ref, [_row_sum(dy * xhat2), _row_sum(dy), _row_sum(dpre),
                            jnp.broadcast_to(loss, (1, D_MODEL))])

    vec = _resident((1, D_MODEL))
    return pl.pallas_call(
        body, name="tail", grid=(S // tb,),
        in_specs=[_rows(tb, D_FF), _rows(tb, D_FF), _rows(tb, D_FF), _rows(tb, D_MODEL), _rows(tb, D_MODEL),
                  _rows(tb, PLE_DIM), _rows(tb, D_MODEL), _resident((D_FF, D_MODEL)), _resident((D_MODEL, D_MODEL)), vec,
                  _resident((PLE_DIM, D_MODEL)), vec, vec, vec, vec],
        out_specs=[_rows(tb, D_MODEL), _rows(tb, D_MODEL), _rows(tb, D_MODEL), _rows(tb, D_FF),
                   _rows(tb, D_FF), _rows(tb, D_MODEL), _acc((8, D_MODEL))],
        out_shape=[jax.ShapeDtypeStruct((S, D_MODEL), BF16),
                   jax.ShapeDtypeStruct((S, D_MODEL), BF16), jax.ShapeDtypeStruct((S, D_MODEL), BF16),
                   jax.ShapeDtypeStruct((S, D_FF), BF16), jax.ShapeDtypeStruct((S, D_FF), BF16),
                   jax.ShapeDtypeStruct((S, D_MODEL), F32), jax.ShapeDtypeStruct((8, D_MODEL), F32)],
        compiler_params=_params("arbitrary"),
    )(act, gl, vdgl, z1, h1b, p, tgt, w_down, w_pg, b_pg, w_pp, ln1_g, ln1_b, ln2_g, ln2_b)


def _weight_grad(a_list, b_list, name, layout, ts=512, comm=None, b_window=None, halves=False):
    S = a_list[0].shape[0]
    ms = [a.shape[1] for a in a_list]
    M, nb = sum(ms), len(b_list)
    win, Nb = b_window if b_window else (0, b_list[0].shape[1])
    ts = min(ts, S)
    nk = S // ts
    per_b = N_DEV // nb
    na = len(a_list)

    n_out = 2 if halves else 1
    assert layout == "cols" or not halves

    def body(*refs):
        a_refs, b_refs, o_refs, acc_ref = refs[:na], refs[na:na + nb], refs[na + nb:na + nb + n_out], refs[-1]
        o_ref = o_refs[0]
        j, k = pl.program_id(0), pl.program_id(1)

        @pl.when(k == 0)
        def _():
            acc_ref[...] = jnp.zeros_like(acc_ref)

        for jj in range(nb):
            @pl.when(j == jj)
            def _():
                b = b_refs[jj][...].astype(BF16)
                off = 0
                for a_ref, m in zip(a_refs, ms):
                    acc_ref[off:off + m, :] += _dot_tn(a_ref[...].astype(BF16), b)
                    off += m

        @pl.when(k == nk - 1)
        def _():
            for d in range(per_b):
                if layout == "rows":
                    o_ref[d] = acc_ref[d * (M // N_DEV):(d + 1) * (M // N_DEV), :].astype(BF16)
                elif layout == "cols" and halves:
                    for o_half, r0 in zip(o_refs, (0, M // 2)):
                        o_half[d] = acc_ref[r0:r0 + M // 2, d * (Nb // per_b):(d + 1) * (Nb // per_b)].astype(BF16)
                elif layout == "cols":
                    o_ref[d] = acc_ref[:, d * (Nb // per_b):(d + 1) * (Nb // per_b)].astype(BF16)
                else:
                    o_ref[d] = acc_ref[:, d * (Nb // per_b):(d + 1) * (Nb // per_b)].T.astype(BF16)

    def b_index(jj):
        return lambda j, k: (jnp.where(j == jj, k, jnp.where(j < jj, 0, nk - 1)), win)

    if layout == "rows":
        assert nb == 1
        blk = (N_DEV, M // N_DEV, Nb)
    elif layout == "cols":
        blk = (per_b, M // n_out, Nb // per_b)
    else:
        blk = (per_b, Nb // per_b, M)
    res, comm_res = _pcall(
        body, (*a_list, *b_list), name=name, grid=(nb, nk), sem="arbitrary", comm=comm, step_axis=1,
        in_specs=[pl.BlockSpec((ts, m), lambda j, k: (k, 0)) for m in ms]
        + [pl.BlockSpec((ts, Nb), b_index(jj)) for jj in range(nb)],
        out_specs=[pl.BlockSpec(blk, lambda j, k: (j, 0, 0))] * n_out,
        out_shape=[jax.ShapeDtypeStruct((N_DEV,) + blk[1:], BF16)] * n_out,
        scratch_shapes=[pltpu.VMEM((M, Nb), F32)])
    res = res if halves else res[0]
    return (res, comm_res) if comm is not None else res


def _up_bwd(dgc, gate, dval, dh1p, z1, w_up, fcw, w_out, ln1_g, comm=None):
    S = z1.shape[0]
    tb = min(256, S)
    t16 = tb // 16
    n16 = S // 16
    nblk, _, wblk = w_up.shape
    half = nblk // 2
    nsteps = S // tb

    def body(dgc_ref, dgn_ref, gc_ref, dval_ref, dh1p_ref, z1_ref, wu_ref, fcw_ref, wo_ref, g1_ref,
             dgate_ref, dz1_ref, dz1b_ref, datt_ref, drec_ref, accf_ref, accd_ref):
        i = pl.program_id(0)

        @pl.when(i == 0)
        def _():
            accf_ref[...] = jnp.zeros_like(accf_ref)
            accd_ref[...] = jnp.zeros_like(accd_ref)

        dg = dgc_ref[...].astype(F32)
        nxt = jnp.where(i < nsteps - 1, dgn_ref[...].astype(F32)[0:8], 0.0)
        w = _w_rows(fcw_ref)
        up1, up2 = _shift_up(dg, nxt, 1), _shift_up(dg, nxt, 2)
        dgate = (w[2] * dg + w[1] * up1 + w[0] * up2).astype(BF16)
        dgate_ref[...] = dgate
        gate = gc_ref[...].astype(F32)
        _put_rows(accf_ref, [_row_sum(up2 * gate), _row_sum(up1 * gate), _row_sum(dg * gate), _row_sum(dg)])

        dh1 = dh1p_ref[...]
        for j in range(nblk):
            src = dgate if j < half else dval_ref[...]
            jj = j % half
            dh1 = dh1 + _dot_nt(src[:, jj * wblk:(jj + 1) * wblk], wu_ref[j])
        xhat1, rstd1 = _ln_stats(z1_ref[...])
        dz1 = _ln_bwd(dh1, xhat1, rstd1, g1_ref[...])
        dz1_ref[...] = dz1
        dz1b = dz1.astype(BF16)
        dz1b_ref[...] = dz1b
        dcat = _dot_nt(dz1b, wo_ref[...])
        datt_ref[...] = dcat[:, :D_ATT].astype(BF16)
        drec_ref[...] = dcat[:, D_ATT:]
        _put_rows(accd_ref, [_row_sum(dh1 * xhat1), _row_sum(dh1)])

    next16 = pl.BlockSpec((16, D_FF), lambda i: (jnp.minimum((i + 1) * t16, n16 - 1), 0))
    return _pcall(
        body, (dgc, dgc, gate, dval, dh1p, z1, w_up, fcw, w_out, ln1_g), name="up_bwd",
        grid=(nsteps,), sem="arbitrary", comm=comm,
        in_specs=[_rows(tb, D_FF), next16, _rows(tb, D_FF), _rows(tb, D_FF), _rows(tb, D_MODEL),
                  _rows(tb, D_MODEL), _resident(w_up.shape), _resident((3, D_FF)),
                  _resident((D_MODEL, D_MODEL)), _resident((1, D_MODEL))],
        out_specs=[_rows(tb, D_FF), _rows(tb, D_MODEL), _rows(tb, D_MODEL), _rows(tb, D_ATT), _rows(tb, D_RNN),
                   _acc((8, D_FF)), _acc((8, D_MODEL))],
        out_shape=[jax.ShapeDtypeStruct((S, D_FF), BF16), jax.ShapeDtypeStruct((S, D_MODEL), F32),
                   jax.ShapeDtypeStruct((S, D_MODEL), BF16), jax.ShapeDtypeStruct((S, D_ATT), BF16),
                   jax.ShapeDtypeStruct((S, D_RNN), F32), jax.ShapeDtypeStruct((8, D_FF), F32),
                   jax.ShapeDtypeStruct((8, D_MODEL), F32)])


def _attn_bwd(q, k, v, lse, do, sinks, comm=None):
    S = q.shape[0]
    grp = N_HEADS // N_KV
    nq = min(ATT_STEP, S // QBLK)

    def body(sink_ref, q_ref, kc_ref, kp_ref, vc_ref, vp_ref, do_ref, lse_ref, dq_ref, dkc_ref, dkp_ref, dvc_ref, dvp_ref,
             ds_ref):
        i = pl.program_id(0)

        @pl.when(i == 0)
        def _():
            ds_ref[...] = jnp.zeros_like(ds_ref)

        row8 = lax.broadcasted_iota(jnp.int32, (8, 128), 0)
        lane8 = lax.broadcasted_iota(jnp.int32, (8, 128), 1)
        dsink = jnp.zeros((8, 128), F32)
        kall = jnp.concatenate([kp_ref[...], kc_ref[...]], axis=0)
        vall = jnp.concatenate([vp_ref[...], vc_ref[...]], axis=0)
        dk_t = [jnp.zeros((D_KV, QBLK), F32) for _ in range(nq + 1)]
        dv_t = [jnp.zeros((D_KV, QBLK), F32) for _ in range(nq + 1)]
        for b in range(nq):
            valid = _band_mask(i * nq + b)
            rows = slice(b * QBLK, (b + 1) * QBLK)
            keys = slice(b * QBLK, (b + 2) * QBLK)
            qv, dov = q_ref[rows, :], do_ref[rows, :]
            dqs, dks, dvs = [], [], []
            for g in range(N_KV):
                kcat = kall[keys, g * HEAD_DIM:(g + 1) * HEAD_DIM]
                vcat = vall[keys, g * HEAD_DIM:(g + 1) * HEAD_DIM]
                q4, do4 = _stack_heads(qv, g), _stack_heads(dov, g)
                s = jnp.where(valid, _dot_nt(q4, kcat), -1e30)
                lse = lse_ref[(b * N_KV + g) * GROUP * QBLK:(b * N_KV + g + 1) * GROUP * QBLK, :]
                p = jnp.exp(s - lse)
                p_sink = jnp.exp(_sink_column(sink_ref, g) - lse)
                dp = _dot_nt(do4, vcat)
                delta = jnp.sum(p * dp, axis=1, keepdims=True)
                dsc = (p * (dp - delta)).astype(BF16)
                dqs += _unstack_heads(_dot(dsc, kcat) * (HEAD_DIM ** -0.5))
                dks.append(_dot_tn(q4, dsc))
                dvs.append(_dot_tn(do4, p.astype(BF16)))
                for hh, part in enumerate(_unstack_heads(-p_sink * delta)):
                    here = (row8 == 0) & (lane8 == g * grp + hh)
                    dsink = dsink + jnp.where(here, jnp.sum(part, axis=0, keepdims=True), 0.0)
            dq_ref[rows, :] = jnp.concatenate(dqs, axis=1).astype(BF16)
            dk2, dv2 = jnp.concatenate(dks, axis=0), jnp.concatenate(dvs, axis=0)
            dk_t[b], dk_t[b + 1] = dk_t[b] + dk2[:, :QBLK], dk_t[b + 1] + dk2[:, QBLK:]
            dv_t[b], dv_t[b + 1] = dv_t[b] + dv2[:, :QBLK], dv_t[b + 1] + dv2[:, QBLK:]
        dkp_ref[...] = dk_t[0].T
        dvp_ref[...] = dv_t[0].T
        for b in range(nq):
            dkc_ref[b * QBLK:(b + 1) * QBLK, :] = dk_t[b + 1].T
            dvc_ref[b * QBLK:(b + 1) * QBLK, :] = dv_t[b + 1].T
        ds_ref[...] += dsink

    nsteps = S // (nq * QBLK)
    cur = jax.ShapeDtypeStruct((S, D_KV), F32)
    prev = jax.ShapeDtypeStruct((nsteps * QBLK, D_KV), F32)
    big = _rows(nq * QBLK, D_ATT)
    return _pcall(
        body, (sinks, q, k, k, v, v, do, lse), name="attn_bwd", grid=(nsteps,), sem="arbitrary", comm=comm,
        in_specs=[pl.BlockSpec(memory_space=pltpu.SMEM), big] + _attn_specs(nq) + [big, _rows(nq * N_HEADS * QBLK, 1)],
        out_specs=[big, _rows(nq * QBLK, D_KV), _rows(QBLK, D_KV), _rows(nq * QBLK, D_KV), _rows(QBLK, D_KV),
                   _acc((8, 128))],
        out_shape=[jax.ShapeDtypeStruct((S, D_ATT), BF16), cur, prev, cur, prev, jax.ShapeDtypeStruct((8, 128), F32)])


def _rnn_bwd(xr, gr, h, kept, drec, conv_w, wa, wx, lam, comm=None):
    S = xr.shape[0]
    tb = min(512, S)
    t8 = tb // 8
    nsteps = S // tb

    def body(xr_ref, xp_ref, gr_ref, h_ref, hp_ref, xc_ref, r_ref, ig_ref, a_ref, f_ref, drec_ref, cw_ref, wa_ref, wx_ref,
             lam_ref, dxr_ref, dgr_ref, gwa_ref, gwx_ref, acc_ref, carry_s, dxc_halo_s, d_s, gwa_s, gwx_s):
        i = pl.program_id(0)
        blk = nsteps - 1 - i

        @pl.when(i == 0)
        def _():
            gwa_s[...] = jnp.zeros_like(gwa_s)
            gwx_s[...] = jnp.zeros_like(gwx_s)
            acc_ref[...] = jnp.zeros_like(acc_ref)
            carry_s[...] = jnp.zeros_like(carry_s)
            dxc_halo_s[...] = jnp.zeros_like(dxc_halo_s)

        x = xr_ref[...]
        xhalo = jnp.where(blk > 0, xp_ref[...], 0.0)
        cw = _w_rows(cw_ref)
        xs = [_shift_down(x, xhalo, 3), _shift_down(x, xhalo, 2), _shift_down(x, xhalo, 1), x]
        xc, r, ig, a, f = xc_ref[...], r_ref[...], ig_ref[...], a_ref[...], f_ref[...]
        sp = _softplus_neg(lam_ref[...])
        hcur = h_ref[...]
        hprev = _shift_down(hcur, jnp.where(blk > 0, hp_ref[...], 0.0), 1)
        gl, dgl = _gelu_and_grad(gr_ref[...])
        drec = drec_ref[...]
        dgr_ref[...] = (drec * hcur * dgl).astype(BF16)
        d_s[...] = drec * gl
        row8 = lax.broadcasted_iota(jnp.int32, (8, D_RNN), 0)

        def tile(t, c):
            o = pl.multiple_of((t8 - 1 - t) * 8, 8)
            a8 = a_ref[pl.ds(o, 8), :]
            dt = d_s[pl.ds(o, 8), :]
            at = jnp.where(row8 == 7, 1.0, pltpu.roll(a8, 7, 0))
            for s in (1, 2, 4):
                keep = row8 < 8 - s
                a_sh = jnp.where(keep, pltpu.roll(at, 8 - s, 0), 1.0)
                d_sh = jnp.where(keep, pltpu.roll(dt, 8 - s, 0), 0.0)
                dt = at * d_sh + dt
                at = at * a_sh
            lt = at * c + dt
            d_s[pl.ds(o, 8), :] = lt
            return _row_sum(jnp.where(row8 == 0, a8 * lt, 0.0))

        carry_s[0:1, :] = lax.fori_loop(0, t8, tile, carry_s[0:1, :], unroll=2)
        lmb = d_s[...]
        a2 = a * a
        dla = lmb * hprev * a - lmb * ig * xc * (a2 / f)
        di = lmb * f * xc
        dr = dla * (-LRU_C) * sp
        dpa = dr * r * (1.0 - r)
        dpx = di * ig * (1.0 - ig)
        dpab = dpa.astype(BF16)
        dpxb = dpx.astype(BF16)
        xcb = xc.astype(BF16)
        gwa_s[...] += _dot_tn(xcb, dpab)
        gwx_s[...] += _dot_tn(xcb, dpxb)

        @pl.when(i == nsteps - 1)
        def _():
            for dense, out in ((gwa_s[...], gwa_ref), (gwx_s[...], gwx_ref)):
                for b in range(RNN_BLOCKS):
                    rows = slice(b * HEAD_DIM, (b + 1) * HEAD_DIM)
                    out[rows, :] = dense[rows, b * HEAD_DIM:(b + 1) * HEAD_DIM]

        dxc = lmb * f * ig + _dot_nt(dpab, wa_ref[...]) + _dot_nt(dpxb, wx_ref[...])
        nxt = dxc_halo_s[...]
        dxr = cw[3] * dxc
        for s in (1, 2, 3):
            dxr = dxr + cw[3 - s] * _shift_up(dxc, nxt, s)
        dxr_ref[...] = dxr.astype(BF16)
        dxc_halo_s[...] = dxc[:8]
        dlam = _row_sum(dla * (-LRU_C) * r) * (-1.0 / (1.0 + jnp.exp(lam_ref[...])))
        _put_rows(acc_ref, [_row_sum(dxc * xs[0]), _row_sum(dxc * xs[1]), _row_sum(dxc * xs[2]), _row_sum(dxc * xs[3]),
                            _row_sum(dxc), _row_sum(dpa), _row_sum(dpx), dlam])

    rev = lambda i: (nsteps - 1 - i, 0)
    prev8 = lambda i: (jnp.maximum((nsteps - 1 - i) * t8 - 1, 0), 0)
    blkspec = pl.BlockSpec((tb, D_RNN), rev)
    halo8 = pl.BlockSpec((8, D_RNN), prev8)
    vec = _resident((1, D_RNN))
    return _pcall(
        body, (xr, xr, gr, h, h, *kept, drec, conv_w, wa, wx, lam), name="rnn_bwd", grid=(nsteps,),
        sem="arbitrary", comm=comm,
        in_specs=[blkspec, halo8, blkspec, blkspec, halo8] + [blkspec] * 6
        + [_resident((4, D_RNN)), _resident((D_RNN, D_RNN)), _resident((D_RNN, D_RNN)), vec],
        out_specs=[blkspec, blkspec, _acc((D_RNN, HEAD_DIM)), _acc((D_RNN, HEAD_DIM)), _acc((8, D_RNN))],
        out_shape=[jax.ShapeDtypeStruct((S, D_RNN), BF16), jax.ShapeDtypeStruct((S, D_RNN), BF16),
                   jax.ShapeDtypeStruct((D_RNN, HEAD_DIM), F32), jax.ShapeDtypeStruct((D_RNN, HEAD_DIM), F32),
                   jax.ShapeDtypeStruct((8, D_RNN), F32)],
        scratch_shapes=[pltpu.VMEM((8, D_RNN), F32), pltpu.VMEM((8, D_RNN), F32), pltpu.VMEM((tb, D_RNN), F32),
                        pltpu.VMEM((D_RNN, D_RNN), F32), pltpu.VMEM((D_RNN, D_RNN), F32)])


def _in_bwd(dq, dkc, dkp, dvc, dvp, dxr, dgr, dz1, w_in, comm=None):
    S = dz1.shape[0]
    tb = min(ATT_STEP * QBLK, S)
    nsteps = S // tb

    def body(dq_ref, dkc_ref, dkn_ref, dvc_ref, dvn_ref, dxr_ref, dgr_ref, dz1_ref, w_ref, dkv_ref, dx_ref):
        last = pl.program_id(0) == nsteps - 1

        def total(cur_ref, next_ref):
            nxt = jnp.where(last, 0.0, next_ref[...])
            tail = cur_ref[tb - QBLK:, :] + nxt
            return jnp.concatenate([cur_ref[:tb - QBLK, :], tail], axis=0) if tb > QBLK else tail

        dkv = jnp.concatenate([total(dkc_ref, dkn_ref), total(dvc_ref, dvn_ref)], axis=1).astype(BF16)
        dkv_ref[...] = dkv
        du = jnp.concatenate([dq_ref[...], dkv, dxr_ref[...], dgr_ref[...]], axis=1)
        dx_ref[...] = ALPHA * dz1_ref[...] + _dot(du, w_ref[...])

    nextp = pl.BlockSpec((QBLK, D_KV), lambda i: (jnp.minimum(i + 1, nsteps - 1), 0))
    return _pcall(
        body, (dq, dkc, dkp, dvc, dvp, dxr, dgr, dz1, w_in), name="in_bwd", grid=(nsteps,), comm=comm,
        in_specs=[_rows(tb, D_ATT), _rows(tb, D_KV), nextp, _rows(tb, D_KV), nextp,
                  _rows(tb, D_RNN), _rows(tb, D_RNN), _rows(tb, D_MODEL), _resident((D_IN, D_MODEL))],
        out_specs=[_rows(tb, 2 * D_KV), _rows(tb, D_MODEL)],
        out_shape=[jax.ShapeDtypeStruct((S, 2 * D_KV), BF16), jax.ShapeDtypeStruct((S, D_MODEL), F32)])


def _block_diag(w):
    eye = jnp.eye(RNN_BLOCKS, dtype=w.dtype)
    return (w[:, :, None, :] * eye[:, None, :, None]).reshape(D_RNN, D_RNN).astype(BF16)


def _adamw(w, g, m, v):
    m = ADAM_B1 * m + (1.0 - ADAM_B1) * g
    v = ADAM_B2 * v + (1.0 - ADAM_B2) * (g * g)
    m_hat = m / (1.0 - ADAM_B1 ** ADAM_STEP)
    v_hat = v / (1.0 - ADAM_B2 ** ADAM_STEP)
    delta = -ADAM_LR * (m_hat / (jnp.sqrt(v_hat) + ADAM_EPS) + ADAM_WD * w)
    return delta, m, v


def _sum_adamw(parts, w, m, v, name):
    parts = parts if isinstance(parts, (list, tuple)) else [parts]
    R, C = w.shape
    rb = R if R <= 256 else (256 if parts[0].shape[1] % 256 == 0 else 128)
    per = parts[0].shape[1] // rb
    assert R % rb == 0 and parts[0].shape[1] % rb == 0
    n = len(parts)

    def body(*refs):
        p_refs = refs[:n]
        w_ref, m_ref, v_ref, g_out, d_out, m_out, v_out = refs[n:]
        which = pl.program_id(0) // per

        def total(p_ref):
            g = p_ref[0].astype(F32)
            for d in range(1, N_DEV):
                g = g + p_ref[d].astype(F32)
            return g

        g = total(p_refs[0])
        for j in range(1, n):
            g = jnp.where(which == j, total(p_refs[j]), g)
        delta, mn, vn = _adamw(w_ref[...], g, m_ref[...], v_ref[...])
        g_out[...] = g
        d_out[...] = delta
        m_out[...] = mn
        v_out[...] = vn

    def part_spec(j):
        return pl.BlockSpec((N_DEV, rb, C), lambda i: (0, jnp.clip(i - j * per, 0, per - 1), 0))

    blk = _rows(rb, C)
    out = jax.ShapeDtypeStruct((R, C), F32)
    return pl.pallas_call(
        body, name=name, grid=(R // rb,),
        in_specs=[part_spec(j) for j in range(n)] + [blk, blk, blk],
        out_specs=[blk, blk, blk, blk], out_shape=[out, out, out, out],
        compiler_params=_params("parallel"),
    )(*parts, w, m, v)


_SMALL = [("attn_sinks", "s", 0, 1, None), ("rnn_conv_w", "r", 0, 4, "cols"), ("rnn_conv_b", "r", 4, 1, None),
          ("gate_a_w", "a", 0, D_RNN, None), ("gate_a_b", "r", 5, 1, None), ("gate_x_w", "x", 0, D_RNN, None),
          ("gate_x_b", "r", 6, 1, None), ("lru_lambda", "r", 7, 1, None), ("ln1_g", "d", 0, 1, None),
          ("ln1_b", "d", 1, 1, None), ("ffn_conv_w", "f", 0, 3, "cols"), ("ffn_conv_b", "f", 3, 1, None),
          ("ple_gate_b", "t", 2, 1, None), ("ln2_g", "t", 0, 1, None), ("ln2_b", "t", 1, 1, None)]
_LOSS_ROW = 3


_ACC_COLS = {"t": (0, D_MODEL), "f": (D_MODEL, D_FF), "d": (D_MODEL + D_FF, D_MODEL), "s": (2 * D_MODEL + D_FF, 128),
             "r": (2 * D_MODEL + D_FF + 128, D_RNN)}
_ACC_WIDTH = 2 * D_MODEL + D_FF + 128 + D_RNN


def _small_update(rows_all, gates_all, params):
    flat = [arr for triple in params for arr in triple]
    n_par = len(_SMALL)

    def body(*refs):
        rows_ref, gates_ref = refs[:2]
        p_refs = refs[2:2 + 3 * n_par]
        loss_ref = refs[2 + 3 * n_par]
        o_refs = refs[3 + 3 * n_par:3 + 7 * n_par]
        rows_s, tmp_r, tmp_f = refs[3 + 7 * n_par:]
        me = _dev_index(*_place())
        rows_sum, gates_sum = rows_ref[0], gates_ref[0]
        for d in range(1, N_DEV):
            rows_sum = rows_sum + rows_ref[d]
            gates_sum = gates_sum + gates_ref[d]
        rows_s[...] = rows_sum
        t0 = _ACC_COLS["t"][0]
        loss_ref[...] = rows_s[_LOSS_ROW:_LOSS_ROW + 1, t0:t0 + 128]
        for i, (name, key, row, rows, how) in enumerate(_SMALL):
            w_ref, m_ref, v_ref = p_refs[3 * i:3 * i + 3]
            g_out, d_out, m_out, v_out = o_refs[4 * i:4 * i + 4]
            if key == "a":
                g = gates_sum[:, :HEAD_DIM]
            elif key == "x":
                g = gates_sum[:, HEAD_DIM:]
            elif how == "cols":
                c0, width = _ACC_COLS[key]
                full = rows_s[:, c0:c0 + width]
                shard = width // N_DEV
                mine = full[:, :shard]
                for d in range(1, N_DEV):
                    mine = jnp.where(me == d, full[:, d * shard:(d + 1) * shard], mine)
                tmp = tmp_r if key == "r" else tmp_f
                tmp[...] = mine
                g = tmp[row:row + rows, :]
            else:
                c0, width = _ACC_COLS[key]
                g = rows_s[row:row + rows, c0:c0 + width][:, :w_ref.shape[1]]
            delta, mn, vn = _adamw(w_ref[...], g, m_ref[...], v_ref[...])
            g_out[...] = g
            d_out[...] = delta
            m_out[...] = mn
            v_out[...] = vn

    outs = [jax.ShapeDtypeStruct((1, 128), F32)]
    for w, _, _ in params:
        outs += [jax.ShapeDtypeStruct(w.shape, F32)] * 4
    scratch = [pltpu.VMEM((8, _ACC_WIDTH), F32), pltpu.VMEM((8, D_RNN // N_DEV), F32), pltpu.VMEM((8, D_FF // N_DEV), F32)]
    res = pl.pallas_call(body, name="small_update", out_shape=outs, scratch_shapes=scratch)(rows_all, gates_all, *flat)
    return res[0], [res[1 + 4 * i:5 + 4 * i] for i in range(n_par)]


def kernel(x, p, w_in, attn_sinks, rnn_conv_w, rnn_conv_b, gate_a_w, gate_a_b, gate_x_w, gate_x_b, lru_lambda, w_out, ln1_g, ln1_b, w_ffn_up, ffn_conv_w, ffn_conv_b, w_ffn_down, ple_gate_w, ple_gate_b, ple_proj, ln2_g, ln2_b, loss_target, m_w_in, m_attn_sinks, m_rnn_conv_w, m_rnn_conv_b, m_gate_a_w, m_gate_a_b, m_gate_x_w, m_gate_x_b, m_lru_lambda, m_w_out, m_ln1_g, m_ln1_b, m_w_ffn_up, m_ffn_conv_w, m_ffn_conv_b, m_w_ffn_down, m_ple_gate_w, m_ple_gate_b, m_ple_proj, m_ln2_g, m_ln2_b, v_w_in, v_attn_sinks, v_rnn_conv_w, v_rnn_conv_b, v_gate_a_w, v_gate_a_b, v_gate_x_w, v_gate_x_b, v_lru_lambda, v_w_out, v_ln1_g, v_ln1_b, v_w_ffn_up, v_ffn_conv_w, v_ffn_conv_b, v_w_ffn_down, v_ple_gate_w, v_ple_gate_b, v_ple_proj, v_ln2_g, v_ln2_b):
    from_col_blocks = lambda g: g.transpose(1, 0, 2).reshape(g.shape[1], N_DEV * g.shape[2])

    xs, ps, tgt, sinks = x[0], p[0, 0], loss_target[0], attn_sinks[0]
    wa, wx = _block_diag(gate_a_w[0]), _block_diag(gate_x_w[0])

    conv_cols = jnp.concatenate([rnn_conv_w[0].reshape(1, -1), ffn_conv_w[0].reshape(1, -1)], axis=1)
    n_rc, n_fc = 4 * D_RNN // N_DEV, 3 * D_FF // N_DEV
    ((g_in,),) = _comm_call([_Gather([w_in[0].T.astype(BF16)])], "gather_w_in")
    w_in_full = g_in.reshape(D_IN, D_MODEL)

    (q, k, v, xr, gr), _ = _in_proj(xs, w_in_full)
    (att, lse), (g_out, g_conv) = _attn_fwd(
        q, k, v, sinks,
        comm=_Multi([_Gather([w_out[0].astype(BF16)]), _Bcast([jnp.broadcast_to(conv_cols, (8, n_rc + n_fc))])]))
    rcw = from_col_blocks(g_conv[:, 0, :n_rc].reshape(N_DEV, 4, D_RNN // N_DEV))
    fcw = from_col_blocks(g_conv[:, 0, n_rc:].reshape(N_DEV, 3, D_FF // N_DEV))
    (rec, h, *kept), (w_up,) = _rnn_fwd(xr, gr, rcw, rnn_conv_b, wa, wx, gate_a_b, gate_x_b, lru_lambda,
                                        comm=_Gather([w_ffn_up[0].astype(BF16)]))
    w_out_full = g_out.reshape(D_MODEL, D_MODEL)
    (z1, h1b, gate, act, gl, vdgl), (g_down, g_pg, g_pp) = _mix_ln1_up(
        xs, att, rec, w_out_full, ln1_g, ln1_b, w_up, fcw, ffn_conv_b,
        comm=_Gather([w_ffn_down[0].astype(BF16), ple_gate_w[0].astype(BF16), ple_proj[0].astype(BF16)]))
    dz2b, dpreb, dppb, dgc, dval, dh1p, acc_t = _tail(
        act, gl, vdgl, z1, h1b, ps, tgt, g_down.reshape(D_FF, D_MODEL), g_pg.reshape(D_MODEL, D_MODEL), ple_gate_b,
        from_col_blocks(g_pp), ln1_g, ln1_b, ln2_g, ln2_b)

    gd_down = _weight_grad([dz2b], [act], "down_grad", "rows_t", ts=1024)
    gd_pg = _weight_grad([h1b], [dpreb], "pg_grad", "rows", ts=1024)
    gd_pp = _weight_grad([ps], [dppb], "pp_grad", "cols", ts=1024)
    (dgate, dz1, dz1b, datt, drec, acc_f, acc_d), (r_down, r_pg, r_pp) = _up_bwd(
        dgc, gate, dval, dh1p, z1, w_up, fcw, w_out_full, ln1_g, comm=_Exchange([gd_down, gd_pg, gd_pp]))
    gd_up_top, gd_up_bot = _weight_grad([h1b], [dgate, dval], "up_grad", "cols", halves=True)
    gd_out = _weight_grad([att, rec], [dz1b], "out_grad", "rows", ts=1024)
    (dq, dkc, dkp, dvc, dvp, acc_s), (r_up_top,) = _attn_bwd(q, k, v, lse, datt, sinks, comm=_Exchange([gd_up_top]))
    early = jnp.concatenate([acc_t, acc_f, acc_d], axis=1)
    (dxr, dgr, g_wa, g_wx, acc_r), (r_up_bot, r_out, early_all) = _rnn_bwd(
        xr, gr, h, kept, drec, rcw, wa, wx, lru_lambda, comm=_Multi([_Exchange([gd_up_bot, gd_out]), _Bcast([early])]))
    (dkv, dx), _ = _in_bwd(dq, dkc, dkp, dvc, dvp, dxr, dgr, dz1, w_in_full)
    du_parts = [dq, dkv, dxr, dgr]
    lanes = D_RNN // 128
    late = jnp.concatenate([g_wa, g_wx], axis=1)
    late = jnp.concatenate([late, acc_s, acc_r.reshape(8, lanes, 128).transpose(1, 0, 2).reshape(8 * lanes, 128)], axis=0)
    width = D_MODEL // IN_GRAD_PARTS
    comm, r_parts = _Bcast([late]), []
    for part in range(IN_GRAD_PARTS):
        gd_part, got = _weight_grad(du_parts, [xs], f"in_grad_{part}", "rows", ts=1024, b_window=(part, width), comm=comm)
        if part == 0:
            (late_all,) = got
        else:
            r_parts += got
        comm = _Exchange([gd_part])
    r_parts += _comm_call([comm], "exchange_w_in")[0]
    r_in = jnp.concatenate(r_parts, axis=2)
    acc_r_all = late_all[:, D_RNN + 8:].reshape(N_DEV, lanes, 8, 128).transpose(0, 2, 1, 3).reshape(N_DEV, 8, D_RNN)
    small_parts = (jnp.concatenate([early_all, late_all[:, D_RNN:D_RNN + 8], acc_r_all], axis=2),
                   late_all[:, :D_RNN])

    outs = {}
    res = _sum_adamw(r_in, w_in[0].T, m_w_in[0].T, v_w_in[0].T, "adamw_w_in")
    outs["w_in"] = [r.T[None] for r in res]
    for name, parts, w, m, v in [("w_out", r_out, w_out, m_w_out, v_w_out),
                                 ("w_ffn_up", [r_up_top, r_up_bot], w_ffn_up, m_w_ffn_up, v_w_ffn_up),
                                 ("w_ffn_down", r_down, w_ffn_down, m_w_ffn_down, v_w_ffn_down),
                                 ("ple_gate_w", r_pg, ple_gate_w, m_ple_gate_w, v_ple_gate_w),
                                 ("ple_proj", r_pp, ple_proj, m_ple_proj, v_ple_proj)]:
        res = _sum_adamw(parts, w[0], m[0], v[0], "adamw_" + name)
        outs[name] = [r[None] for r in res]

    given = dict(attn_sinks=(attn_sinks, m_attn_sinks, v_attn_sinks), rnn_conv_w=(rnn_conv_w, m_rnn_conv_w, v_rnn_conv_w),
                 rnn_conv_b=(rnn_conv_b, m_rnn_conv_b, v_rnn_conv_b), gate_a_w=(gate_a_w, m_gate_a_w, v_gate_a_w),
                 gate_a_b=(gate_a_b, m_gate_a_b, v_gate_a_b), gate_x_w=(gate_x_w, m_gate_x_w, v_gate_x_w),
                 gate_x_b=(gate_x_b, m_gate_x_b, v_gate_x_b), lru_lambda=(lru_lambda, m_lru_lambda, v_lru_lambda),
                 ln1_g=(ln1_g, m_ln1_g, v_ln1_g), ln1_b=(ln1_b, m_ln1_b, v_ln1_b),
                 ffn_conv_w=(ffn_conv_w, m_ffn_conv_w, v_ffn_conv_w), ffn_conv_b=(ffn_conv_b, m_ffn_conv_b, v_ffn_conv_b),
                 ple_gate_b=(ple_gate_b, m_ple_gate_b, v_ple_gate_b), ln2_g=(ln2_g, m_ln2_g, v_ln2_g),
                 ln2_b=(ln2_b, m_ln2_b, v_ln2_b))
    as_2d = lambda a: a.reshape(-1, a.shape[-1])
    loss_row, small_res = _small_update(*small_parts, [tuple(as_2d(a) for a in given[n]) for n, *_ in _SMALL])
    loss = loss_row[0, 0]
    for (n, *_), res in zip(_SMALL, small_res):
        outs[n] = [r.reshape(given[n][0].shape) for r in res]

    order = ["w_in", "attn_sinks", "rnn_conv_w", "rnn_conv_b", "gate_a_w", "gate_a_b", "gate_x_w", "gate_x_b",
             "lru_lambda", "w_out", "ln1_g", "ln1_b", "w_ffn_up", "ffn_conv_w", "ffn_conv_b", "w_ffn_down",
             "ple_gate_w", "ple_gate_b", "ple_proj", "ln2_g", "ln2_b"]
    return (loss, dx[None], *[outs[n][0] for n in order], *[outs[n][1] for n in order],
            *[outs[n][2] for n in order], *[outs[n][3] for n in order])
```

```python
import jax
import jax.numpy as jnp
from jax import lax
from jax.experimental import pallas as pl
from jax.experimental.pallas import tpu as pltpu

F32 = jnp.float32
BF16 = jnp.bfloat16

D_MODEL = 1024
D_ATT = 512
D_KV = 128
HEAD_DIM = 64
N_HEADS = 8
N_KV = 2
D_RNN = 512
RNN_BLOCKS = 8
D_IN = 1792
D_FF = 3072
PLE_DIM = 256
QBLK = 128
N_DEV = 8
ALPHA = float(2 ** 0.25)
LN_EPS = 1e-5
LRU_C = 8.0
ADAM_LR, ADAM_B1, ADAM_B2, ADAM_EPS, ADAM_WD, ADAM_STEP = 0.001, 0.9, 0.999, 1e-08, 0.01, 10

V7X_VMEM_LIMIT = 56 * 1024 * 1024
MESH = pl.DeviceIdType.MESH


def _params(*sem, vmem=V7X_VMEM_LIMIT):
    return pltpu.CompilerParams(dimension_semantics=sem or None, vmem_limit_bytes=vmem)


def _resident(shape):
    return pl.BlockSpec(shape, lambda *_: (0,) * len(shape), pipeline_mode=pl.Buffered(1))


def _rows(tb, cols):
    return pl.BlockSpec((tb, cols), lambda i: (i, 0))


def _acc(shape):
    return pl.BlockSpec(shape, lambda *_: (0,) * len(shape))


def _dot(a, b):
    return jnp.dot(a, b, preferred_element_type=F32)


def _dot_nt(a, b):
    return lax.dot_general(a, b, (((1,), (1,)), ((), ())), preferred_element_type=F32)


def _dot_tn(a, b):
    return lax.dot_general(a, b, (((0,), (0,)), ((), ())), preferred_element_type=F32)


def _sigmoid(x):
    return 1.0 / (1.0 + jnp.exp(-x))


_GELU_C = 0.7978845608028654
_GELU_K = 0.044715


def _gelu_and_grad(x):
    u = x * x
    t = jnp.tanh(x * (_GELU_C + (_GELU_C * _GELU_K) * u))
    hp = 0.5 + 0.5 * t
    dg = hp + x * (0.5 - 0.5 * (t * t)) * (_GELU_C + (3.0 * _GELU_C * _GELU_K) * u)
    return x * hp, dg


def _gelu(x):
    return 0.5 * x * (1.0 + jnp.tanh(_GELU_C * (x + _GELU_K * x * x * x)))


def _ln_stats(z):
    mu = jnp.mean(z, axis=-1, keepdims=True)
    zc = z - mu
    var = jnp.mean(zc * zc, axis=-1, keepdims=True)
    rstd = lax.rsqrt(var + LN_EPS)
    return zc * rstd, rstd


def _ln_bwd(dy, xhat, rstd, g):
    dxh = dy * g
    m1 = jnp.mean(dxh, axis=-1, keepdims=True)
    m2 = jnp.mean(dxh * xhat, axis=-1, keepdims=True)
    return rstd * (dxh - m1 - xhat * m2)


def _softplus_neg(lam):
    u = jnp.exp(-jnp.abs(lam))
    w = 1.0 + u
    d = w - 1.0
    log1p_u = jnp.where(d == 0.0, u, jnp.log(w) * (u / jnp.where(d == 0.0, 1.0, d)))
    return jnp.maximum(-lam, 0.0) + log1p_u


def _shift_down(x, halo, s):
    xs = pltpu.roll(x, s, 0)
    hs = pltpu.roll(halo, s, 0)
    row8 = lax.broadcasted_iota(jnp.int32, hs.shape, 0)
    first = jnp.where(row8 < s, hs, xs[:8])
    return jnp.concatenate([first, xs[8:]], axis=0)


def _shift_up(x, halo, s):
    n = x.shape[0]
    xs = pltpu.roll(x, n - s, 0)
    hs = pltpu.roll(halo, 8 - s, 0)
    row8 = lax.broadcasted_iota(jnp.int32, hs.shape, 0)
    last = jnp.where(row8 >= 8 - s, hs, xs[n - 8:])
    return jnp.concatenate([xs[:n - 8], last], axis=0)


def _row_sum(x):
    return jnp.sum(x, axis=0, keepdims=True)


def _put_rows(acc_ref, rows):
    row8 = lax.broadcasted_iota(jnp.int32, acc_ref.shape, 0)
    upd = jnp.zeros(acc_ref.shape, F32)
    for r, vec in enumerate(rows):
        upd = jnp.where(row8 == r, vec, upd)
    acc_ref[...] += upd


def _place():
    return lax.axis_index("x"), lax.axis_index("y"), lax.axis_index("c")


def _dev_index(px, py, pc):
    return 4 * px + 2 * py + pc


_ANY = pl.BlockSpec(memory_space=pl.ANY)


class _Gather:
    def __init__(self, arrays):
        self.arrays = list(arrays)
        self.n = len(self.arrays)

    def out_shape(self):
        return [jax.ShapeDtypeStruct((N_DEV,) + s.shape, s.dtype) for s in self.arrays]

    def scratch(self):
        return [pltpu.SemaphoreType.DMA((self.n, 7)), pltpu.SemaphoreType.DMA((self.n, 7)),
                pltpu.SemaphoreType.DMA((self.n,))]

    def _parts(self, ins, outs, sems):
        send_sems, recv_sems, local_sems = sems
        x, y, c = _place()
        me, sibling = (x, y, c), (x, y, 1 - c)
        chips = [(1 - x, y), (x, 1 - y), (1 - x, 1 - y)]

        def copy(a, k, block, to, src=None):
            rows = outs[a].at[_dev_index(*block)]
            return pltpu.make_async_remote_copy(
                src_ref=rows if src is None else src, dst_ref=rows, send_sem=send_sems.at[a, k],
                recv_sem=recv_sems.at[a, k], device_id=to, device_id_type=MESH)

        rng = range(self.n)
        mine = [pltpu.make_async_copy(ins[a], outs[a].at[_dev_index(*me)], local_sems.at[a]) for a in rng]
        first = [copy(a, 0, me, sibling, src=ins[a]) for a in rng]
        first += [copy(a, 1 + j, me, (*chip, c), src=ins[a]) for j, chip in enumerate(chips) for a in rng]
        landed = [copy(a, 1 + j, (*chip, c), me) for j, chip in enumerate(chips) for a in rng]
        passed = [copy(a, 4 + j, (*chip, c), sibling) for j, chip in enumerate(chips) for a in rng]
        from_sibling = [copy(a, 0, sibling, me) for a in rng]
        from_sibling += [copy(a, 4 + j, (*chip, 1 - c), me) for j, chip in enumerate(chips) for a in rng]
        return mine, first, landed, passed, from_sibling

    def start(self, ins, outs, sems):
        mine, first, _, _, _ = self._parts(ins, outs, sems)
        for cp in mine + first:
            cp.start()

    def forward(self, ins, outs, sems):
        _, _, landed, passed, _ = self._parts(ins, outs, sems)
        for got, fwd in zip(landed, passed):
            got.wait_recv()
            fwd.start()

    def finish(self, ins, outs, sems):
        mine, first, _, passed, from_sibling = self._parts(ins, outs, sems)
        for cp in from_sibling:
            cp.wait_recv()
        for cp in first + passed:
            cp.wait_send()
        for cp in mine:
            cp.wait()

    def before(self, ins, outs, sems, step, nsteps):
        pl.when(step == 0)(lambda: self.start(ins, outs, sems))
        pl.when(step == (7 * nsteps) // 8)(lambda: self.forward(ins, outs, sems))

    def after(self, ins, outs, sems, step, nsteps):
        pl.when(step == nsteps - 1)(lambda: self.finish(ins, outs, sems))


class _Exchange:
    def __init__(self, arrays):
        self.arrays = list(arrays)
        self.n = len(self.arrays)

    def out_shape(self):
        return [jax.ShapeDtypeStruct(b.shape, b.dtype) for b in self.arrays]

    def scratch(self):
        return [pltpu.SemaphoreType.DMA((self.n, 7)), pltpu.SemaphoreType.DMA((self.n, 7)),
                pltpu.SemaphoreType.DMA((self.n,))]

    def _parts(self, ins, outs, sems):
        send_sems, recv_sems, local_sems = sems
        x, y, c = _place()
        me = _dev_index(x, y, c)
        peers = [(x ^ (k >> 2), y ^ ((k >> 1) & 1), c ^ (k & 1)) for k in range(1, N_DEV)]
        rng = range(self.n)
        mine = [pltpu.make_async_copy(ins[a].at[me], outs[a].at[me], local_sems.at[a]) for a in rng]
        sent = [pltpu.make_async_remote_copy(
            src_ref=ins[a].at[_dev_index(*to)], dst_ref=outs[a].at[me], send_sem=send_sems.at[a, k],
            recv_sem=recv_sems.at[a, k], device_id=to, device_id_type=MESH) for k, to in enumerate(peers) for a in rng]
        arrivals = [pltpu.make_async_remote_copy(
            src_ref=ins[a].at[me], dst_ref=outs[a].at[_dev_index(*frm)], send_sem=send_sems.at[a, k],
            recv_sem=recv_sems.at[a, k], device_id=frm, device_id_type=MESH) for k, frm in enumerate(peers) for a in rng]
        return mine, sent, arrivals

    def start(self, ins, outs, sems):
        mine, sent, _ = self._parts(ins, outs, sems)
        for cp in mine + sent:
            cp.start()

    def finish(self, ins, outs, sems):
        mine, sent, arrivals = self._parts(ins, outs, sems)
        for cp in arrivals:
            cp.wait_recv()
        for cp in sent:
            cp.wait_send()
        for cp in mine:
            cp.wait()

    def before(self, ins, outs, sems, step, nsteps):
        pl.when(step == 0)(lambda: self.start(ins, outs, sems))

    def after(self, ins, outs, sems, step, nsteps):
        pl.when(step == nsteps - 1)(lambda: self.finish(ins, outs, sems))


class _Bcast(_Exchange):
    def out_shape(self):
        return [jax.ShapeDtypeStruct((N_DEV,) + s.shape, s.dtype) for s in self.arrays]

    def _parts(self, ins, outs, sems):
        send_sems, recv_sems, local_sems = sems
        x, y, c = _place()
        me = _dev_index(x, y, c)
        peers = [(x ^ (k >> 2), y ^ ((k >> 1) & 1), c ^ (k & 1)) for k in range(1, N_DEV)]
        rng = range(self.n)
        mine = [pltpu.make_async_copy(ins[a], outs[a].at[me], local_sems.at[a]) for a in rng]
        sent = [pltpu.make_async_remote_copy(
            src_ref=ins[a], dst_ref=outs[a].at[me], send_sem=send_sems.at[a, k], recv_sem=recv_sems.at[a, k],
            device_id=to, device_id_type=MESH) for k, to in enumerate(peers) for a in rng]
        arrivals = [pltpu.make_async_remote_copy(
            src_ref=ins[a], dst_ref=outs[a].at[_dev_index(*frm)], send_sem=send_sems.at[a, k],
            recv_sem=recv_sems.at[a, k], device_id=frm, device_id_type=MESH) for k, frm in enumerate(peers) for a in rng]
        return mine, sent, arrivals


class _Multi:
    def __init__(self, comms):
        self.comms = list(comms)
        self.arrays = [arr for c in self.comms for arr in c.arrays]
        self.n = len(self.arrays)

    def out_shape(self):
        return [s for c in self.comms for s in c.out_shape()]

    def scratch(self):
        return [s for c in self.comms for s in c.scratch()]

    def _each(self, ins, outs, sems):
        a = 0
        for j, c in enumerate(self.comms):
            yield c, ins[a:a + c.n], outs[a:a + c.n], sems[3 * j:3 * j + 3]
            a += c.n

    def before(self, ins, outs, sems, step, nsteps):
        for c, ci, co, cs in self._each(ins, outs, sems):
            c.before(ci, co, cs, step, nsteps)

    def after(self, ins, outs, sems, step, nsteps):
        for c, ci, co, cs in self._each(ins, outs, sems):
            c.after(ci, co, cs, step, nsteps)


def _comm_call(comms, name):
    ns = [c.n for c in comms]
    n = sum(ns)

    def body(*refs):
        parts, a, s = [], 0, 2 * n
        for c in comms:
            parts.append((c, refs[a:a + c.n], refs[n + a:n + a + c.n], refs[s:s + 3]))
            a, s = a + c.n, s + 3
        for c, ins, outs, sems in parts:
            c.start(ins, outs, sems)
        for c, ins, outs, sems in parts:
            if isinstance(c, _Gather):
                c.forward(ins, outs, sems)
        for c, ins, outs, sems in parts:
            c.finish(ins, outs, sems)

    res = pl.pallas_call(
        body, name=name, in_specs=[_ANY] * n, out_specs=[_ANY] * n,
        out_shape=[s for c in comms for s in c.out_shape()], scratch_shapes=[s for c in comms for s in c.scratch()],
    )(*[arr for c in comms for arr in c.arrays])
    out, a = [], 0
    for k in ns:
        out.append(res[a:a + k])
        a += k
    return out


def _pcall(body, args, *, name, grid, in_specs, out_specs, out_shape, scratch_shapes=(), sem="parallel", comm=None,
           step_axis=0):
    sem = (sem,) * len(grid) if isinstance(sem, str) else sem
    if comm is None:
        res = pl.pallas_call(body, name=name, grid=grid, in_specs=in_specs, out_specs=out_specs, out_shape=out_shape,
                             scratch_shapes=list(scratch_shapes), compiler_params=_params(*sem))(*args)
        return res, []
    n_in, n_out, n_scr, n = len(in_specs), len(out_specs), len(scratch_shapes), comm.n
    nsteps = grid[step_axis]
    assert all(g == 1 for ax, g in enumerate(grid) if ax != step_axis)

    def hosted(*refs):
        ins, cin = refs[:n_in], refs[n_in:n_in + n]
        o0 = n_in + n
        outs, cout = refs[o0:o0 + n_out], refs[o0 + n_out:o0 + n_out + n]
        s0 = o0 + n_out + n
        scr, sems = refs[s0:s0 + n_scr], refs[s0 + n_scr:]
        step = pl.program_id(step_axis)
        comm.before(cin, cout, sems, step, nsteps)
        body(*ins, *outs, *scr)
        comm.after(cin, cout, sems, step, nsteps)

    res = pl.pallas_call(
        hosted, name=name, grid=grid, in_specs=list(in_specs) + [_ANY] * n, out_specs=list(out_specs) + [_ANY] * n,
        out_shape=list(out_shape) + comm.out_shape(), scratch_shapes=list(scratch_shapes) + comm.scratch(),
        compiler_params=_params(*(("arbitrary",) * len(grid))))(*args, *comm.arrays)
    return res[:n_out], res[n_out:]


def _in_proj(x, w_in_t, comm=None):
    S = x.shape[0]
    tb = min(1024, S)

    def body(x_ref, w_ref, q_ref, k_ref, v_ref, xr_ref, gr_ref):
        u = _dot_nt(x_ref[...].astype(BF16), w_ref[...])
        q_ref[...] = (u[:, :D_ATT] * (HEAD_DIM ** -0.5)).astype(BF16)
        k_ref[...] = u[:, D_ATT:D_ATT + D_KV].astype(BF16)
        v_ref[...] = u[:, D_ATT + D_KV:D_ATT + 2 * D_KV].astype(BF16)
        xr_ref[...] = u[:, D_ATT + 2 * D_KV:D_ATT + 2 * D_KV + D_RNN]
        gr_ref[...] = u[:, D_ATT + 2 * D_KV + D_RNN:]

    return _pcall(
        body, (x, w_in_t), name="in_proj", grid=(S // tb,), comm=comm,
        in_specs=[_rows(tb, D_MODEL), _resident((D_IN, D_MODEL))],
        out_specs=[_rows(tb, D_ATT), _rows(tb, D_KV), _rows(tb, D_KV), _rows(tb, D_RNN), _rows(tb, D_RNN)],
        out_shape=[jax.ShapeDtypeStruct((S, D_ATT), BF16), jax.ShapeDtypeStruct((S, D_KV), BF16),
                   jax.ShapeDtypeStruct((S, D_KV), BF16), jax.ShapeDtypeStruct((S, D_RNN), F32),
                   jax.ShapeDtypeStruct((S, D_RNN), F32)])


GROUP = N_HEADS // N_KV


def _band_mask(i):
    qi = lax.broadcasted_iota(jnp.int32, (GROUP * QBLK, 2 * QBLK), 0) & (QBLK - 1)
    sj = lax.broadcasted_iota(jnp.int32, (GROUP * QBLK, 2 * QBLK), 1)
    return (sj > qi) & (sj <= qi + QBLK) & ((sj >= QBLK) | (i > 0))


def _stack_heads(x, g):
    return jnp.concatenate([x[:, (g * GROUP + hh) * HEAD_DIM:(g * GROUP + hh + 1) * HEAD_DIM] for hh in range(GROUP)],
                           axis=0)


def _unstack_heads(x4):
    return [x4[hh * QBLK:(hh + 1) * QBLK] for hh in range(GROUP)]


def _sink_column(sink_ref, g):
    head = lax.broadcasted_iota(jnp.int32, (GROUP * QBLK, 1), 0) // QBLK
    col = jnp.full((GROUP * QBLK, 1), sink_ref[g * GROUP], F32)
    for hh in range(1, GROUP):
        col = jnp.where(head == hh, sink_ref[g * GROUP + hh], col)
    return col


ATT_STEP = 4
IN_GRAD_PARTS = 2


def _attn_specs(nq=1):
    cur = lambda i: (i, 0)
    prev = lambda i: (jnp.maximum(nq * i - 1, 0), 0)
    return [pl.BlockSpec((nq * QBLK, D_KV), cur), pl.BlockSpec((QBLK, D_KV), prev),
            pl.BlockSpec((nq * QBLK, D_KV), cur), pl.BlockSpec((QBLK, D_KV), prev)]


def _attn_fwd(q, k, v, sinks, comm=None):
    S = q.shape[0]
    nq = min(ATT_STEP, S // QBLK)

    def body(sink_ref, q_ref, kc_ref, kp_ref, vc_ref, vp_ref, o_ref, lse_ref):
        first = pl.program_id(0) * nq
        kall = jnp.concatenate([kp_ref[...], kc_ref[...]], axis=0)
        vall = jnp.concatenate([vp_ref[...], vc_ref[...]], axis=0)
        for b in range(nq):
            valid = _band_mask(first + b)
            rows = slice(b * QBLK, (b + 1) * QBLK)
            keys = slice(b * QBLK, (b + 2) * QBLK)
            qv = q_ref[rows, :]
            outs = []
            for g in range(N_KV):
                kcat = kall[keys, g * HEAD_DIM:(g + 1) * HEAD_DIM]
                vcat = vall[keys, g * HEAD_DIM:(g + 1) * HEAD_DIM]
                s = jnp.where(valid, _dot_nt(_stack_heads(qv, g), kcat), -1e30)
                sink = _sink_column(sink_ref, g)
                m = jnp.maximum(jnp.max(s, axis=1, keepdims=True), sink)
                p = jnp.exp(s - m)
                l = jnp.sum(p, axis=1, keepdims=True) + jnp.exp(sink - m)
                outs += _unstack_heads(_dot(p.astype(BF16), vcat) / l)
                lse_ref[(b * N_KV + g) * GROUP * QBLK:(b * N_KV + g + 1) * GROUP * QBLK, :] = m + jnp.log(l)
            o_ref[rows, :] = jnp.concatenate(outs, axis=1).astype(BF16)

    lse_rows = nq * N_HEADS * QBLK
    return _pcall(
        body, (sinks, q, k, k, v, v), name="attn_fwd", grid=(S // (nq * QBLK),), comm=comm,
        in_specs=[pl.BlockSpec(memory_space=pltpu.SMEM), _rows(nq * QBLK, D_ATT)] + _attn_specs(nq),
        out_specs=[_rows(nq * QBLK, D_ATT), _rows(lse_rows, 1)],
        out_shape=[jax.ShapeDtypeStruct((S, D_ATT), BF16), jax.ShapeDtypeStruct((S * N_HEADS, 1), F32)])


def _w_rows(w_ref):
    return [w_ref[k:k + 1, :] for k in range(w_ref.shape[0])]


def _conv4(x, halo, w, b):
    y = b + w[3] * x
    for s in (1, 2, 3):
        y = y + w[3 - s] * _shift_down(x, halo, s)
    return y


def _rnn_gates(xc, wa, wx, ba, bx, sp):
    xcb = xc.astype(BF16)
    r = _sigmoid(_dot(xcb, wa) + ba)
    ig = _sigmoid(_dot(xcb, wx) + bx)
    la = -LRU_C * r * sp
    a = jnp.exp(la)
    t = jnp.tanh(la)
    f = jnp.sqrt(-2.0 * t / (1.0 - t))
    return r, ig, a, f


def _rnn_fwd(xr, gr, conv_w, conv_b, wa, wx, ba, bx, lam, comm=None):
    S = xr.shape[0]
    tb = min(512, S)

    def body(xr_ref, gr_ref, cw_ref, cb_ref, wa_ref, wx_ref, ba_ref, bx_ref, lam_ref, rec_ref, h_ref,
             xc_ref, r_ref, ig_ref, a_ref, f_ref, halo_s, hc_s, a_s, b_s):
        @pl.when(pl.program_id(0) == 0)
        def _():
            halo_s[...] = jnp.zeros_like(halo_s)
            hc_s[...] = jnp.zeros_like(hc_s)

        x = xr_ref[...]
        xc = _conv4(x, halo_s[...], _w_rows(cw_ref), cb_ref[...])
        halo_s[...] = x[tb - 8:]
        r, ig, a, f = _rnn_gates(xc, wa_ref[...], wx_ref[...], ba_ref[...], bx_ref[...], _softplus_neg(lam_ref[...]))
        xc_ref[...] = xc
        r_ref[...] = r
        ig_ref[...] = ig
        a_ref[...] = a
        f_ref[...] = f
        a_s[...] = a
        b_s[...] = f * ig * xc
        row8 = lax.broadcasted_iota(jnp.int32, (8, D_RNN), 0)

        def tile(t, hc):
            o = pl.multiple_of(t * 8, 8)
            at = a_s[pl.ds(o, 8), :]
            bt = b_s[pl.ds(o, 8), :]
            for s in (1, 2, 4):
                keep = row8 >= s
                a_sh = jnp.where(keep, pltpu.roll(at, s, 0), 1.0)
                b_sh = jnp.where(keep, pltpu.roll(bt, s, 0), 0.0)
                bt = at * b_sh + bt
                at = at * a_sh
            ht = at * hc + bt
            b_s[pl.ds(o, 8), :] = ht
            return _row_sum(jnp.where(row8 == 7, ht, 0.0))

        hc_s[0:1, :] = lax.fori_loop(0, tb // 8, tile, hc_s[0:1, :], unroll=2)
        h = b_s[...]
        h_ref[...] = h
        rec_ref[...] = (h * _gelu(gr_ref[...])).astype(BF16)

    vec = _resident((1, D_RNN))
    kept = jax.ShapeDtypeStruct((S, D_RNN), F32)
    return _pcall(
        body, (xr, gr, conv_w, conv_b, wa, wx, ba, bx, lam), name="rnn_fwd", grid=(S // tb,), sem="arbitrary", comm=comm,
        in_specs=[_rows(tb, D_RNN), _rows(tb, D_RNN), _resident((4, D_RNN)), vec,
                  _resident((D_RNN, D_RNN)), _resident((D_RNN, D_RNN)), vec, vec, vec],
        out_specs=[_rows(tb, D_RNN)] * 7,
        out_shape=[jax.ShapeDtypeStruct((S, D_RNN), BF16), kept, kept, kept, kept, kept, kept],
        scratch_shapes=[pltpu.VMEM((8, D_RNN), F32), pltpu.VMEM((8, D_RNN), F32),
                        pltpu.VMEM((tb, D_RNN), F32), pltpu.VMEM((tb, D_RNN), F32)])


def _mix_ln1_up(x, att, rec, w_out, ln1_g, ln1_b, w_up, fcw, fcb, comm=None):
    S = x.shape[0]
    tb = min(256, S)
    nblk, _, wblk = w_up.shape
    half = nblk // 2

    def body(x_ref, att_ref, rec_ref, wo_ref, g_ref, b_ref, wu_ref, fcw_ref, fcb_ref,
             z1_ref, h1b_ref, gate_ref, act_ref, gl_ref, vdgl_ref, halo_s):
        @pl.when(pl.program_id(0) == 0)
        def _():
            halo_s[...] = jnp.zeros_like(halo_s)

        z1 = ALPHA * x_ref[...] + _dot(att_ref[...], wo_ref[:D_ATT, :]) + _dot(rec_ref[...], wo_ref[D_ATT:, :])
        z1_ref[...] = z1
        xhat, _ = _ln_stats(z1)
        h1b = (xhat * g_ref[...] + b_ref[...]).astype(BF16)
        h1b_ref[...] = h1b
        for jj in range(half):
            cols = slice(jj * wblk, (jj + 1) * wblk)
            gate = _dot(h1b, wu_ref[jj])
            val = _dot(h1b, wu_ref[jj + half])
            halo = halo_s[:, cols]
            conv = (fcb_ref[:, cols] + fcw_ref[2:3, cols] * gate + fcw_ref[1:2, cols] * _shift_down(gate, halo, 1)
                    + fcw_ref[0:1, cols] * _shift_down(gate, halo, 2))
            halo_s[:, cols] = gate[tb - 8:]
            gl, dgl = _gelu_and_grad(conv)
            gate_ref[:, cols] = gate.astype(BF16)
            act_ref[:, cols] = (gl * val).astype(BF16)
            gl_ref[:, cols] = gl.astype(BF16)
            vdgl_ref[:, cols] = (val * dgl).astype(BF16)

    vec = _resident((1, D_MODEL))
    wide = jax.ShapeDtypeStruct((S, D_FF), BF16)
    return _pcall(
        body, (x, att, rec, w_out, ln1_g, ln1_b, w_up, fcw, fcb), name="mix_ln1_up", grid=(S // tb,),
        sem="arbitrary", comm=comm,
        in_specs=[_rows(tb, D_MODEL), _rows(tb, D_ATT), _rows(tb, D_RNN), _resident((D_MODEL, D_MODEL)), vec, vec,
                  _resident(w_up.shape), _resident((3, D_FF)), _resident((1, D_FF))],
        out_specs=[_rows(tb, D_MODEL), _rows(tb, D_MODEL)] + [_rows(tb, D_FF)] * 4,
        out_shape=[jax.ShapeDtypeStruct((S, D_MODEL), F32), jax.ShapeDtypeStruct((S, D_MODEL), BF16), wide, wide, wide, wide],
        scratch_shapes=[pltpu.VMEM((8, D_FF), F32)])


def _tail(act, gl, vdgl, z1, h1b, p, tgt, w_down, w_pg, b_pg, w_pp, ln1_g, ln1_b, ln2_g, ln2_b):
    S = z1.shape[0]
    tb = min(256, S)

    def body(act_ref, gl_ref, vdgl_ref, z1_ref, h1b_ref, p_ref, t_ref, wd_ref, wpg_ref, bpg_ref, wpp_ref,
             g1_ref, b1_ref, g2_ref, b2_ref, dz2_ref, dpre_ref, dpp_ref, dgc_ref, dval_ref, dh1_ref, acc_ref):
        i = pl.program_id(0)

        @pl.when(i == 0)
        def _():
            acc_ref[...] = jnp.zeros_like(acc_ref)

        ffn = _dot(act_ref[...], wd_ref[...])
        xhat1, _ = _ln_stats(z1_ref[...])
        h1 = xhat1 * g1_ref[...] + b1_ref[...]
        sg = _sigmoid(_dot(h1b_ref[...], wpg_ref[...]) + bpg_ref[...])
        pp = _dot(p_ref[...].astype(BF16), wpp_ref[...])
        z2 = ALPHA * h1 + ffn + sg * pp
        xhat2, rstd2 = _ln_stats(z2)
        y = xhat2 * g2_ref[...] + b2_ref[...]
        err = y - t_ref[...]
        dy = err * (1.0 / D_MODEL)
        loss = 0.5 * jnp.sum(jnp.sum(err * err, axis=1, keepdims=True), axis=0, keepdims=True) * (1.0 / D_MODEL)
        dz2 = _ln_bwd(dy, xhat2, rstd2, g2_ref[...])
        dz2b = dz2.astype(BF16)
        dz2_ref[...] = dz2b
        dpre = dz2 * pp * sg * (1.0 - sg)
        dpreb = dpre.astype(BF16)
        dpre_ref[...] = dpreb
        dpp_ref[...] = (dz2 * sg).astype(BF16)
        dh1_ref[...] = ALPHA * dz2 + _dot_nt(dpreb, wpg_ref[...])
        dactb = _dot_nt(dz2b, wd_ref[...]).astype(BF16)
        dval_ref[...] = dactb * gl_ref[...]
        dgc_ref[...] = dactb * vdgl_ref[...]
        _put_rows(acc_ref, [_row_sum(dy * xhat2), _row_sum(dy), _row_sum(dpre),
                            jnp.broadcast_to(loss, (1, D_MODEL))])

    vec = _resident((1, D_MODEL))
    return pl.pallas_call(
        body, name="tail", grid=(S // tb,),
        in_specs=[_rows(tb, D_FF), _rows(tb, D_FF), _rows(tb, D_FF), _rows(tb, D_MODEL), _rows(tb, D_MODEL),
                  _rows(tb, PLE_DIM), _rows(tb, D_MODEL), _resident((D_FF, D_MODEL)), _resident((D_MODEL, D_MODEL)), vec,
                  _resident((PLE_DIM, D_MODEL)), vec, vec, vec, vec],
        out_specs=[_rows(tb, D_MODEL), _rows(tb, D_MODEL), _rows(tb, D_MODEL), _rows(tb, D_FF),
                   _rows(tb, D_FF), _rows(tb, D_MODEL), _acc((8, D_MODEL))],
        out_shape=[jax.ShapeDtypeStruct((S, D_MODEL), BF16),
                   jax.ShapeDtypeStruct((S, D_MODEL), BF16), jax.ShapeDtypeStruct((S, D_MODEL), BF16),
                   jax.ShapeDtypeStruct((S, D_FF), BF16), jax.ShapeDtypeStruct((S, D_FF), BF16),
                   jax.ShapeDtypeStruct((S, D_MODEL), F32), jax.ShapeDtypeStruct((8, D_MODEL), F32)],
        compiler_params=_params("arbitrary"),
    )(act, gl, vdgl, z1, h1b, p, tgt, w_down, w_pg, b_pg, w_pp, ln1_g, ln1_b, ln2_g, ln2_b)


def _weight_grad(a_list, b_list, name, layout, ts=512, comm=None, b_window=None, halves=False):
    S = a_list[0].shape[0]
    ms = [a.shape[1] for a in a_list]
    M, nb = sum(ms), len(b_list)
    win, Nb = b_window if b_window else (0, b_list[0].shape[1])
    ts = min(ts, S)
    nk = S // ts
    per_b = N_DEV // nb
    na = len(a_list)

    n_out = 2 if halves else 1
    assert layout == "cols" or not halves

    def body(*refs):
        a_refs, b_refs, o_refs, acc_ref = refs[:na], refs[na:na + nb], refs[na + nb:na + nb + n_out], refs[-1]
        o_ref = o_refs[0]
        j, k = pl.program_id(0), pl.program_id(1)

        @pl.when(k == 0)
        def _():
            acc_ref[...] = jnp.zeros_like(acc_ref)

        for jj in range(nb):
            @pl.when(j == jj)
            def _():
                b = b_refs[jj][...].astype(BF16)
                off = 0
                for a_ref, m in zip(a_refs, ms):
                    acc_ref[off:off + m, :] += _dot_tn(a_ref[...].astype(BF16), b)
                    off += m

        @pl.when(k == nk - 1)
        def _():
            for d in range(per_b):
                if layout == "rows":
                    o_ref[d] = acc_ref[d * (M // N_DEV):(d + 1) * (M // N_DEV), :].astype(BF16)
                elif layout == "cols" and halves:
                    for o_half, r0 in zip(o_refs, (0, M // 2)):
                        o_half[d] = acc_ref[r0:r0 + M // 2, d * (Nb // per_b):(d + 1) * (Nb // per_b)].astype(BF16)
                elif layout == "cols":
                    o_ref[d] = acc_ref[:, d * (Nb // per_b):(d + 1) * (Nb // per_b)].astype(BF16)
                else:
                    o_ref[d] = acc_ref[:, d * (Nb // per_b):(d + 1) * (Nb // per_b)].T.astype(BF16)

    def b_index(jj):
        return lambda j, k: (jnp.where(j == jj, k, jnp.where(j < jj, 0, nk - 1)), win)

    if layout == "rows":
        assert nb == 1
        blk = (N_DEV, M // N_DEV, Nb)
    elif layout == "cols":
        blk = (per_b, M // n_out, Nb // per_b)
    else:
        blk = (per_b, Nb // per_b, M)
    res, comm_res = _pcall(
        body, (*a_list, *b_list), name=name, grid=(nb, nk), sem="arbitrary", comm=comm, step_axis=1,
        in_specs=[pl.BlockSpec((ts, m), lambda j, k: (k, 0)) for m in ms]
        + [pl.BlockSpec((ts, Nb), b_index(jj)) for jj in range(nb)],
        out_specs=[pl.BlockSpec(blk, lambda j, k: (j, 0, 0))] * n_out,
        out_shape=[jax.ShapeDtypeStruct((N_DEV,) + blk[1:], BF16)] * n_out,
        scratch_shapes=[pltpu.VMEM((M, Nb), F32)])
    res = res if halves else res[0]
    return (res, comm_res) if comm is not None else res


def _up_bwd(dgc, gate, dval, dh1p, z1, w_up, fcw, w_out, ln1_g, comm=None):
    S = z1.shape[0]
    tb = min(256, S)
    t16 = tb // 16
    n16 = S // 16
    nblk, _, wblk = w_up.shape
    half = nblk // 2
    nsteps = S // tb

    def body(dgc_ref, dgn_ref, gc_ref, dval_ref, dh1p_ref, z1_ref, wu_ref, fcw_ref, wo_ref, g1_ref,
             dgate_ref, dz1_ref, dz1b_ref, datt_ref, drec_ref, accf_ref, accd_ref):
        i = pl.program_id(0)

        @pl.when(i == 0)
        def _():
            accf_ref[...] = jnp.zeros_like(accf_ref)
            accd_ref[...] = jnp.zeros_like(accd_ref)

        dg = dgc_ref[...].astype(F32)
        nxt = jnp.where(i < nsteps - 1, dgn_ref[...].astype(F32)[0:8], 0.0)
        w = _w_rows(fcw_ref)
        up1, up2 = _shift_up(dg, nxt, 1), _shift_up(dg, nxt, 2)
        dgate = (w[2] * dg + w[1] * up1 + w[0] * up2).astype(BF16)
        dgate_ref[...] = dgate
        gate = gc_ref[...].astype(F32)
        _put_rows(accf_ref, [_row_sum(up2 * gate), _row_sum(up1 * gate), _row_sum(dg * gate), _row_sum(dg)])

        dh1 = dh1p_ref[...]
        for j in range(nblk):
            src = dgate if j < half else dval_ref[...]
            jj = j % half
            dh1 = dh1 + _dot_nt(src[:, jj * wblk:(jj + 1) * wblk], wu_ref[j])
        xhat1, rstd1 = _ln_stats(z1_ref[...])
        dz1 = _ln_bwd(dh1, xhat1, rstd1, g1_ref[...])
        dz1_ref[...] = dz1
        dz1b = dz1.astype(BF16)
        dz1b_ref[...] = dz1b
        dcat = _dot_nt(dz1b, wo_ref[...])
        datt_ref[...] = dcat[:, :D_ATT].astype(BF16)
        drec_ref[...] = dcat[:, D_ATT:]
        _put_rows(accd_ref, [_row_sum(dh1 * xhat1), _row_sum(dh1)])

    next16 = pl.BlockSpec((16, D_FF), lambda i: (jnp.minimum((i + 1) * t16, n16 - 1), 0))
    return _pcall(
        body, (dgc, dgc, gate, dval, dh1p, z1, w_up, fcw, w_out, ln1_g), name="up_bwd",
        grid=(nsteps,), sem="arbitrary", comm=comm,
        in_specs=[_rows(tb, D_FF), next16, _rows(tb, D_FF), _rows(tb, D_FF), _rows(tb, D_MODEL),
                  _rows(tb, D_MODEL), _resident(w_up.shape), _resident((3, D_FF)),
                  _resident((D_MODEL, D_MODEL)), _resident((1, D_MODEL))],
        out_specs=[_rows(tb, D_FF), _rows(tb, D_MODEL), _rows(tb, D_MODEL), _rows(tb, D_ATT), _rows(tb, D_RNN),
                   _acc((8, D_FF)), _acc((8, D_MODEL))],
        out_shape=[jax.ShapeDtypeStruct((S, D_FF), BF16), jax.ShapeDtypeStruct((S, D_MODEL), F32),
                   jax.ShapeDtypeStruct((S, D_MODEL), BF16), jax.ShapeDtypeStruct((S, D_ATT), BF16),
                   jax.ShapeDtypeStruct((S, D_RNN), F32), jax.ShapeDtypeStruct((8, D_FF), F32),
                   jax.ShapeDtypeStruct((8, D_MODEL), F32)])


def _attn_bwd(q, k, v, lse, do, sinks, comm=None):
    S = q.shape[0]
    grp = N_HEADS // N_KV
    nq = min(ATT_STEP, S // QBLK)

    def body(sink_ref, q_ref, kc_ref, kp_ref, vc_ref, vp_ref, do_ref, lse_ref, dq_ref, dkc_ref, dkp_ref, dvc_ref, dvp_ref,
             ds_ref):
        i = pl.program_id(0)

        @pl.when(i == 0)
        def _():
            ds_ref[...] = jnp.zeros_like(ds_ref)

        row8 = lax.broadcasted_iota(jnp.int32, (8, 128), 0)
        lane8 = lax.broadcasted_iota(jnp.int32, (8, 128), 1)
        dsink = jnp.zeros((8, 128), F32)
        kall = jnp.concatenate([kp_ref[...], kc_ref[...]], axis=0)
        vall = jnp.concatenate([vp_ref[...], vc_ref[...]], axis=0)
        dk_t = [jnp.zeros((D_KV, QBLK), F32) for _ in range(nq + 1)]
        dv_t = [jnp.zeros((D_KV, QBLK), F32) for _ in range(nq + 1)]
        for b in range(nq):
            valid = _band_mask(i * nq + b)
            rows = slice(b * QBLK, (b + 1) * QBLK)
            keys = slice(b * QBLK, (b + 2) * QBLK)
            qv, dov = q_ref[rows, :], do_ref[rows, :]
            dqs, dks, dvs = [], [], []
            for g in range(N_KV):
                kcat = kall[keys, g * HEAD_DIM:(g + 1) * HEAD_DIM]
                vcat = vall[keys, g * HEAD_DIM:(g + 1) * HEAD_DIM]
                q4, do4 = _stack_heads(qv, g), _stack_heads(dov, g)
                s = jnp.where(valid, _dot_nt(q4, kcat), -1e30)
                lse = lse_ref[(b * N_KV + g) * GROUP * QBLK:(b * N_KV + g + 1) * GROUP * QBLK, :]
                p = jnp.exp(s - lse)
                p_sink = jnp.exp(_sink_column(sink_ref, g) - lse)
                dp = _dot_nt(do4, vcat)
                delta = jnp.sum(p * dp, axis=1, keepdims=True)
                dsc = (p * (dp - delta)).astype(BF16)
                dqs += _unstack_heads(_dot(dsc, kcat) * (HEAD_DIM ** -0.5))
                dks.append(_dot_tn(q4, dsc))
                dvs.append(_dot_tn(do4, p.astype(BF16)))
                for hh, part in enumerate(_unstack_heads(-p_sink * delta)):
                    here = (row8 == 0) & (lane8 == g * grp + hh)
                    dsink = dsink + jnp.where(here, jnp.sum(part, axis=0, keepdims=True), 0.0)
            dq_ref[rows, :] = jnp.concatenate(dqs, axis=1).astype(BF16)
            dk2, dv2 = jnp.concatenate(dks, axis=0), jnp.concatenate(dvs, axis=0)
            dk_t[b], dk_t[b + 1] = dk_t[b] + dk2[:, :QBLK], dk_t[b + 1] + dk2[:, QBLK:]
            dv_t[b], dv_t[b + 1] = dv_t[b] + dv2[:, :QBLK], dv_t[b + 1] + dv2[:, QBLK:]
        dkp_ref[...] = dk_t[0].T
        dvp_ref[...] = dv_t[0].T
        for b in range(nq):
            dkc_ref[b * QBLK:(b + 1) * QBLK, :] = dk_t[b + 1].T
            dvc_ref[b * QBLK:(b + 1) * QBLK, :] = dv_t[b + 1].T
        ds_ref[...] += dsink

    nsteps = S // (nq * QBLK)
    cur = jax.ShapeDtypeStruct((S, D_KV), F32)
    prev = jax.ShapeDtypeStruct((nsteps * QBLK, D_KV), F32)
    big = _rows(nq * QBLK, D_ATT)
    return _pcall(
        body, (sinks, q, k, k, v, v, do, lse), name="attn_bwd", grid=(nsteps,), sem="arbitrary", comm=comm,
        in_specs=[pl.BlockSpec(memory_space=pltpu.SMEM), big] + _attn_specs(nq) + [big, _rows(nq * N_HEADS * QBLK, 1)],
        out_specs=[big, _rows(nq * QBLK, D_KV), _rows(QBLK, D_KV), _rows(nq * QBLK, D_KV), _rows(QBLK, D_KV),
                   _acc((8, 128))],
        out_shape=[jax.ShapeDtypeStruct((S, D_ATT), BF16), cur, prev, cur, prev, jax.ShapeDtypeStruct((8, 128), F32)])


def _rnn_bwd(xr, gr, h, kept, drec, conv_w, wa, wx, lam, comm=None):
    S = xr.shape[0]
    tb = min(512, S)
    t8 = tb // 8
    nsteps = S // tb

    def body(xr_ref, xp_ref, gr_ref, h_ref, hp_ref, xc_ref, r_ref, ig_ref, a_ref, f_ref, drec_ref, cw_ref, wa_ref, wx_ref,
             lam_ref, dxr_ref, dgr_ref, gwa_ref, gwx_ref, acc_ref, carry_s, dxc_halo_s, d_s, gwa_s, gwx_s):
        i = pl.program_id(0)
        blk = nsteps - 1 - i

        @pl.when(i == 0)
        def _():
            gwa_s[...] = jnp.zeros_like(gwa_s)
            gwx_s[...] = jnp.zeros_like(gwx_s)
            acc_ref[...] = jnp.zeros_like(acc_ref)
            carry_s[...] = jnp.zeros_like(carry_s)
            dxc_halo_s[...] = jnp.zeros_like(dxc_halo_s)

        x = xr_ref[...]
        xhalo = jnp.where(blk > 0, xp_ref[...], 0.0)
        cw = _w_rows(cw_ref)
        xs = [_shift_down(x, xhalo, 3), _shift_down(x, xhalo, 2), _shift_down(x, xhalo, 1), x]
        xc, r, ig, a, f = xc_ref[...], r_ref[...], ig_ref[...], a_ref[...], f_ref[...]
        sp = _softplus_neg(lam_ref[...])
        hcur = h_ref[...]
        hprev = _shift_down(hcur, jnp.where(blk > 0, hp_ref[...], 0.0), 1)
        gl, dgl = _gelu_and_grad(gr_ref[...])
        drec = drec_ref[...]
        dgr_ref[...] = (drec * hcur * dgl).astype(BF16)
        d_s[...] = drec * gl
        row8 = lax.broadcasted_iota(jnp.int32, (8, D_RNN), 0)

        def tile(t, c):
            o = pl.multiple_of((t8 - 1 - t) * 8, 8)
            a8 = a_ref[pl.ds(o, 8), :]
            dt = d_s[pl.ds(o, 8), :]
            at = jnp.where(row8 == 7, 1.0, pltpu.roll(a8, 7, 0))
            for s in (1, 2, 4):
                keep = row8 < 8 - s
                a_sh = jnp.where(keep, pltpu.roll(at, 8 - s, 0), 1.0)
                d_sh = jnp.where(keep, pltpu.roll(dt, 8 - s, 0), 0.0)
                dt = at * d_sh + dt
                at = at * a_sh
            lt = at * c + dt
            d_s[pl.ds(o, 8), :] = lt
            return _row_sum(jnp.where(row8 == 0, a8 * lt, 0.0))

        carry_s[0:1, :] = lax.fori_loop(0, t8, tile, carry_s[0:1, :], unroll=2)
        lmb = d_s[...]
        a2 = a * a
        dla = lmb * hprev * a - lmb * ig * xc * (a2 / f)
        di = lmb * f * xc
        dr = dla * (-LRU_C) * sp
        dpa = dr * r * (1.0 - r)
        dpx = di * ig * (1.0 - ig)
        dpab = dpa.astype(BF16)
        dpxb = dpx.astype(BF16)
        xcb = xc.astype(BF16)
        gwa_s[...] += _dot_tn(xcb, dpab)
        gwx_s[...] += _dot_tn(xcb, dpxb)

        @pl.when(i == nsteps - 1)
        def _():
            for dense, out in ((gwa_s[...], gwa_ref), (gwx_s[...], gwx_ref)):
                for b in range(RNN_BLOCKS):
                    rows = slice(b * HEAD_DIM, (b + 1) * HEAD_DIM)
                    out[rows, :] = dense[rows, b * HEAD_DIM:(b + 1) * HEAD_DIM]

        dxc = lmb * f * ig + _dot_nt(dpab, wa_ref[...]) + _dot_nt(dpxb, wx_ref[...])
        nxt = dxc_halo_s[...]
        dxr = cw[3] * dxc
        for s in (1, 2, 3):
            dxr = dxr + cw[3 - s] * _shift_up(dxc, nxt, s)
        dxr_ref[...] = dxr.astype(BF16)
        dxc_halo_s[...] = dxc[:8]
        dlam = _row_sum(dla * (-LRU_C) * r) * (-1.0 / (1.0 + jnp.exp(lam_ref[...])))
        _put_rows(acc_ref, [_row_sum(dxc * xs[0]), _row_sum(dxc * xs[1]), _row_sum(dxc * xs[2]), _row_sum(dxc * xs[3]),
                            _row_sum(dxc), _row_sum(dpa), _row_sum(dpx), dlam])

    rev = lambda i: (nsteps - 1 - i, 0)
    prev8 = lambda i: (jnp.maximum((nsteps - 1 - i) * t8 - 1, 0), 0)
    blkspec = pl.BlockSpec((tb, D_RNN), rev)
    halo8 = pl.BlockSpec((8, D_RNN), prev8)
    vec = _resident((1, D_RNN))
    return _pcall(
        body, (xr, xr, gr, h, h, *kept, drec, conv_w, wa, wx, lam), name="rnn_bwd", grid=(nsteps,),
        sem="arbitrary", comm=comm,
        in_specs=[blkspec, halo8, blkspec, blkspec, halo8] + [blkspec] * 6
        + [_resident((4, D_RNN)), _resident((D_RNN, D_RNN)), _resident((D_RNN, D_RNN)), vec],
        out_specs=[blkspec, blkspec, _acc((D_RNN, HEAD_DIM)), _acc((D_RNN, HEAD_DIM)), _acc((8, D_RNN))],
        out_shape=[jax.ShapeDtypeStruct((S, D_RNN), BF16), jax.ShapeDtypeStruct((S, D_RNN), BF16),
                   jax.ShapeDtypeStruct((D_RNN, HEAD_DIM), F32), jax.ShapeDtypeStruct((D_RNN, HEAD_DIM), F32),
                   jax.ShapeDtypeStruct((8, D_RNN), F32)],
        scratch_shapes=[pltpu.VMEM((8, D_RNN), F32), pltpu.VMEM((8, D_RNN), F32), pltpu.VMEM((tb, D_RNN), F32),
                        pltpu.VMEM((D_RNN, D_RNN), F32), pltpu.VMEM((D_RNN, D_RNN), F32)])


def _in_bwd(dq, dkc, dkp, dvc, dvp, dxr, dgr, dz1, w_in, comm=None):
    S = dz1.shape[0]
    tb = min(ATT_STEP * QBLK, S)
    nsteps = S // tb

    def body(dq_ref, dkc_ref, dkn_ref, dvc_ref, dvn_ref, dxr_ref, dgr_ref, dz1_ref, w_ref, dkv_ref, dx_ref):
        last = pl.program_id(0) == nsteps - 1

        def total(cur_ref, next_ref):
            nxt = jnp.where(last, 0.0, next_ref[...])
            tail = cur_ref[tb - QBLK:, :] + nxt
            return jnp.concatenate([cur_ref[:tb - QBLK, :], tail], axis=0) if tb > QBLK else tail

        dkv = jnp.concatenate([total(dkc_ref, dkn_ref), total(dvc_ref, dvn_ref)], axis=1).astype(BF16)
        dkv_ref[...] = dkv
        du = jnp.concatenate([dq_ref[...], dkv, dxr_ref[...], dgr_ref[...]], axis=1)
        dx_ref[...] = ALPHA * dz1_ref[...] + _dot(du, w_ref[...])

    nextp = pl.BlockSpec((QBLK, D_KV), lambda i: (jnp.minimum(i + 1, nsteps - 1), 0))
    return _pcall(
        body, (dq, dkc, dkp, dvc, dvp, dxr, dgr, dz1, w_in), name="in_bwd", grid=(nsteps,), comm=comm,
        in_specs=[_rows(tb, D_ATT), _rows(tb, D_KV), nextp, _rows(tb, D_KV), nextp,
                  _rows(tb, D_RNN), _rows(tb, D_RNN), _rows(tb, D_MODEL), _resident((D_IN, D_MODEL))],
        out_specs=[_rows(tb, 2 * D_KV), _rows(tb, D_MODEL)],
        out_shape=[jax.ShapeDtypeStruct((S, 2 * D_KV), BF16), jax.ShapeDtypeStruct((S, D_MODEL), F32)])


def _block_diag(w):
    eye = jnp.eye(RNN_BLOCKS, dtype=w.dtype)
    return (w[:, :, None, :] * eye[:, None, :, None]).reshape(D_RNN, D_RNN).astype(BF16)


def _adamw(w, g, m, v):
    m = ADAM_B1 * m + (1.0 - ADAM_B1) * g
    v = ADAM_B2 * v + (1.0 - ADAM_B2) * (g * g)
    m_hat = m / (1.0 - ADAM_B1 ** ADAM_STEP)
    v_hat = v / (1.0 - ADAM_B2 ** ADAM_STEP)
    delta = -ADAM_LR * (m_hat / (jnp.sqrt(v_hat) + ADAM_EPS) + ADAM_WD * w)
    return delta, m, v


def _sum_adamw(parts, w, m, v, name):
    parts = parts if isinstance(parts, (list, tuple)) else [parts]
    R, C = w.shape
    rb = R if R <= 256 else (256 if parts[0].shape[1] % 256 == 0 else 128)
    per = parts[0].shape[1] // rb
    assert R % rb == 0 and parts[0].shape[1] % rb == 0
    n = len(parts)

    def body(*refs):
        p_refs = refs[:n]
        w_ref, m_ref, v_ref, g_out, d_out, m_out, v_out = refs[n:]
        which = pl.program_id(0) // per

        def total(p_ref):
            g = p_ref[0].astype(F32)
            for d in range(1, N_DEV):
                g = g + p_ref[d].astype(F32)
            return g

        g = total(p_refs[0])
        for j in range(1, n):
            g = jnp.where(which == j, total(p_refs[j]), g)
        delta, mn, vn = _adamw(w_ref[...], g, m_ref[...], v_ref[...])
        g_out[...] = g
        d_out[...] = delta
        m_out[...] = mn
        v_out[...] = vn

    def part_spec(j):
        return pl.BlockSpec((N_DEV, rb, C), lambda i: (0, jnp.clip(i - j * per, 0, per - 1), 0))

    blk = _rows(rb, C)
    out = jax.ShapeDtypeStruct((R, C), F32)
    return pl.pallas_call(
        body, name=name, grid=(R // rb,),
        in_specs=[part_spec(j) for j in range(n)] + [blk, blk, blk],
        out_specs=[blk, blk, blk, blk], out_shape=[out, out, out, out],
        compiler_params=_params("parallel"),
    )(*parts, w, m, v)


_SMALL = [("attn_sinks", "s", 0, 1, None), ("rnn_conv_w", "r", 0, 4, "cols"), ("rnn_conv_b", "r", 4, 1, None),
          ("gate_a_w", "a", 0, D_RNN, None), ("gate_a_b", "r", 5, 1, None), ("gate_x_w", "x", 0, D_RNN, None),
          ("gate_x_b", "r", 6, 1, None), ("lru_lambda", "r", 7, 1, None), ("ln1_g", "d", 0, 1, None),
          ("ln1_b", "d", 1, 1, None), ("ffn_conv_w", "f", 0, 3, "cols"), ("ffn_conv_b", "f", 3, 1, None),
          ("ple_gate_b", "t", 2, 1, None), ("ln2_g", "t", 0, 1, None), ("ln2_b", "t", 1, 1, None)]
_LOSS_ROW = 3


_ACC_COLS = {"t": (0, D_MODEL), "f": (D_MODEL, D_FF), "d": (D_MODEL + D_FF, D_MODEL), "s": (2 * D_MODEL + D_FF, 128),
             "r": (2 * D_MODEL + D_FF + 128, D_RNN)}
_ACC_WIDTH = 2 * D_MODEL + D_FF + 128 + D_RNN


def _small_update(rows_all, gates_all, params):
    flat = [arr for triple in params for arr in triple]
    n_par = len(_SMALL)

    def body(*refs):
        rows_ref, gates_ref = refs[:2]
        p_refs = refs[2:2 + 3 * n_par]
        loss_ref = refs[2 + 3 * n_par]
        o_refs = refs[3 + 3 * n_par:3 + 7 * n_par]
        rows_s, tmp_r, tmp_f = refs[3 + 7 * n_par:]
        me = _dev_index(*_place())
        rows_sum, gates_sum = rows_ref[0], gates_ref[0]
        for d in range(1, N_DEV):
            rows_sum = rows_sum + rows_ref[d]
            gates_sum = gates_sum + gates_ref[d]
        rows_s[...] = rows_sum
        t0 = _ACC_COLS["t"][0]
        loss_ref[...] = rows_s[_LOSS_ROW:_LOSS_ROW + 1, t0:t0 + 128]
        for i, (name, key, row, rows, how) in enumerate(_SMALL):
            w_ref, m_ref, v_ref = p_refs[3 * i:3 * i + 3]
            g_out, d_out, m_out, v_out = o_refs[4 * i:4 * i + 4]
            if key == "a":
                g = gates_sum[:, :HEAD_DIM]
            elif key == "x":
                g = gates_sum[:, HEAD_DIM:]
            elif how == "cols":
                c0, width = _ACC_COLS[key]
                full = rows_s[:, c0:c0 + width]
                shard = width // N_DEV
                mine = full[:, :shard]
                for d in range(1, N_DEV):
                    mine = jnp.where(me == d, full[:, d * shard:(d + 1) * shard], mine)
                tmp = tmp_r if key == "r" else tmp_f
                tmp[...] = mine
                g = tmp[row:row + rows, :]
            else:
                c0, width = _ACC_COLS[key]
                g = rows_s[row:row + rows, c0:c0 + width][:, :w_ref.shape[1]]
            delta, mn, vn = _adamw(w_ref[...], g, m_ref[...], v_ref[...])
            g_out[...] = g
            d_out[...] = delta
            m_out[...] = mn
            v_out[...] = vn

    outs = [jax.ShapeDtypeStruct((1, 128), F32)]
    for w, _, _ in params:
        outs += [jax.ShapeDtypeStruct(w.shape, F32)] * 4
    scratch = [pltpu.VMEM((8, _ACC_WIDTH), F32), pltpu.VMEM((8, D_RNN // N_DEV), F32), pltpu.VMEM((8, D_FF // N_DEV), F32)]
    res = pl.pallas_call(body, name="small_update", out_shape=outs, scratch_shapes=scratch)(rows_all, gates_all, *flat)
    return res[0], [res[1 + 4 * i:5 + 4 * i] for i in range(n_par)]


def kernel(x, p, w_in, attn_sinks, rnn_conv_w, rnn_conv_b, gate_a_w, gate_a_b, gate_x_w, gate_x_b, lru_lambda, w_out, ln1_g, ln1_b, w_ffn_up, ffn_conv_w, ffn_conv_b, w_ffn_down, ple_gate_w, ple_gate_b, ple_proj, ln2_g, ln2_b, loss_target, m_w_in, m_attn_sinks, m_rnn_conv_w, m_rnn_conv_b, m_gate_a_w, m_gate_a_b, m_gate_x_w, m_gate_x_b, m_lru_lambda, m_w_out, m_ln1_g, m_ln1_b, m_w_ffn_up, m_ffn_conv_w, m_ffn_conv_b, m_w_ffn_down, m_ple_gate_w, m_ple_gate_b, m_ple_proj, m_ln2_g, m_ln2_b, v_w_in, v_attn_sinks, v_rnn_conv_w, v_rnn_conv_b, v_gate_a_w, v_gate_a_b, v_gate_x_w, v_gate_x_b, v_lru_lambda, v_w_out, v_ln1_g, v_ln1_b, v_w_ffn_up, v_ffn_conv_w, v_ffn_conv_b, v_w_ffn_down, v_ple_gate_w, v_ple_gate_b, v_ple_proj, v_ln2_g, v_ln2_b):
    from_col_blocks = lambda g: g.transpose(1, 0, 2).reshape(g.shape[1], N_DEV * g.shape[2])

    xs, ps, tgt, sinks = x[0], p[0, 0], loss_target[0], attn_sinks[0]
    wa, wx = _block_diag(gate_a_w[0]), _block_diag(gate_x_w[0])

    conv_cols = jnp.concatenate([rnn_conv_w[0].reshape(1, -1), ffn_conv_w[0].reshape(1, -1)], axis=1)
    n_rc, n_fc = 4 * D_RNN // N_DEV, 3 * D_FF // N_DEV
    ((g_in,),) = _comm_call([_Gather([w_in[0].T.astype(BF16)])], "gather_w_in")
    w_in_full = g_in.reshape(D_IN, D_MODEL)

    (q, k, v, xr, gr), _ = _in_proj(xs, w_in_full)
    (att, lse), (g_out, g_conv) = _attn_fwd(
        q, k, v, sinks,
        comm=_Multi([_Gather([w_out[0].astype(BF16)]), _Bcast([jnp.broadcast_to(conv_cols, (8, n_rc + n_fc))])]))
    rcw = from_col_blocks(g_conv[:, 0, :n_rc].reshape(N_DEV, 4, D_RNN // N_DEV))
    fcw = from_col_blocks(g_conv[:, 0, n_rc:].reshape(N_DEV, 3, D_FF // N_DEV))
    (rec, h, *kept), (w_up,) = _rnn_fwd(xr, gr, rcw, rnn_conv_b, wa, wx, gate_a_b, gate_x_b, lru_lambda,
                                        comm=_Gather([w_ffn_up[0].astype(BF16)]))
    w_out_full = g_out.reshape(D_MODEL, D_MODEL)
    (z1, h1b, gate, act, gl, vdgl), (g_down, g_pg, g_pp) = _mix_ln1_up(
        xs, att, rec, w_out_full, ln1_g, ln1_b, w_up, fcw, ffn_conv_b,
        comm=_Gather([w_ffn_down[0].astype(BF16), ple_gate_w[0].astype(BF16), ple_proj[0].astype(BF16)]))
    dz2b, dpreb, dppb, dgc, dval, dh1p, acc_t = _tail(
        act, gl, vdgl, z1, h1b, ps, tgt, g_down.reshape(D_FF, D_MODEL), g_pg.reshape(D_MODEL, D_MODEL), ple_gate_b,
        from_col_blocks(g_pp), ln1_g, ln1_b, ln2_g, ln2_b)

    gd_down = _weight_grad([dz2b], [act], "down_grad", "rows_t", ts=1024)
    gd_pg = _weight_grad([h1b], [dpreb], "pg_grad", "rows", ts=1024)
    gd_pp = _weight_grad([ps], [dppb], "pp_grad", "cols", ts=1024)
    (dgate, dz1, dz1b, datt, drec, acc_f, acc_d), (r_down, r_pg, r_pp) = _up_bwd(
        dgc, gate, dval, dh1p, z1, w_up, fcw, w_out_full, ln1_g, comm=_Exchange([gd_down, gd_pg, gd_pp]))
    gd_up_top, gd_up_bot = _weight_grad([h1b], [dgate, dval], "up_grad", "cols", halves=True)
    gd_out = _weight_grad([att, rec], [dz1b], "out_grad", "rows", ts=1024)
    (dq, dkc, dkp, dvc, dvp, acc_s), (r_up_top,) = _attn_bwd(q, k, v, lse, datt, sinks, comm=_Exchange([gd_up_top]))
    early = jnp.concatenate([acc_t, acc_f, acc_d], axis=1)
    (dxr, dgr, g_wa, g_wx, acc_r), (r_up_bot, r_out, early_all) = _rnn_bwd(
        xr, gr, h, kept, drec, rcw, wa, wx, lru_lambda, comm=_Multi([_Exchange([gd_up_bot, gd_out]), _Bcast([early])]))
    (dkv, dx), _ = _in_bwd(dq, dkc, dkp, dvc, dvp, dxr, dgr, dz1, w_in_full)
    du_parts = [dq, dkv, dxr, dgr]
    lanes = D_RNN // 128
    late = jnp.concatenate([g_wa, g_wx], axis=1)
    late = jnp.concatenate([late, acc_s, acc_r.reshape(8, lanes, 128).transpose(1, 0, 2).reshape(8 * lanes, 128)], axis=0)
    width = D_MODEL // IN_GRAD_PARTS
    comm, r_parts = _Bcast([late]), []
    for part in range(IN_GRAD_PARTS):
        gd_part, got = _weight_grad(du_parts, [xs], f"in_grad_{part}", "rows", ts=1024, b_window=(part, width), comm=comm)
        if part == 0:
            (late_all,) = got
        else:
            r_parts += got
        comm = _Exchange([gd_part])
    r_parts += _comm_call([comm], "exchange_w_in")[0]
    r_in = jnp.concatenate(r_parts, axis=2)
    acc_r_all = late_all[:, D_RNN + 8:].reshape(N_DEV, lanes, 8, 128).transpose(0, 2, 1, 3).reshape(N_DEV, 8, D_RNN)
    small_parts = (jnp.concatenate([early_all, late_all[:, D_RNN:D_RNN + 8], acc_r_all], axis=2),
                   late_all[:, :D_RNN])

    outs = {}
    res = _sum_adamw(r_in, w_in[0].T, m_w_in[0].T, v_w_in[0].T, "adamw_w_in")
    outs["w_in"] = [r.T[None] for r in res]
    for name, parts, w, m, v in [("w_out", r_out, w_out, m_w_out, v_w_out),
                                 ("w_ffn_up", [r_up_top, r_up_bot], w_ffn_up, m_w_ffn_up, v_w_ffn_up),
                                 ("w_ffn_down", r_down, w_ffn_down, m_w_ffn_down, v_w_ffn_down),
                                 ("ple_gate_w", r_pg, ple_gate_w, m_ple_gate_w, v_ple_gate_w),
                                 ("ple_proj", r_pp, ple_proj, m_ple_proj, v_ple_proj)]:
        res = _sum_adamw(parts, w[0], m[0], v[0], "adamw_" + name)
        outs[name] = [r[None] for r in res]

    given = dict(attn_sinks=(attn_sinks, m_attn_sinks, v_attn_sinks), rnn_conv_w=(rnn_conv_w, m_rnn_conv_w, v_rnn_conv_w),
                 rnn_conv_b=(rnn_conv_b, m_rnn_conv_b, v_rnn_conv_b), gate_a_w=(gate_a_w, m_gate_a_w, v_gate_a_w),
                 gate_a_b=(gate_a_b, m_gate_a_b, v_gate_a_b), gate_x_w=(gate_x_w, m_gate_x_w, v_gate_x_w),
                 gate_x_b=(gate_x_b, m_gate_x_b, v_gate_x_b), lru_lambda=(lru_lambda, m_lru_lambda, v_lru_lambda),
                 ln1_g=(ln1_g, m_ln1_g, v_ln1_g), ln1_b=(ln1_b, m_ln1_b, v_ln1_b),
                 ffn_conv_w=(ffn_conv_w, m_ffn_conv_w, v_ffn_conv_w), ffn_conv_b=(ffn_conv_b, m_ffn_conv_b, v_ffn_conv_b),
                 ple_gate_b=(ple_gate_b, m_ple_gate_b, v_ple_gate_b), ln2_g=(ln2_g, m_ln2_g, v_ln2_g),
                 ln2_b=(ln2_b, m_ln2_b, v_ln2_b))
    as_2d = lambda a: a.reshape(-1, a.shape[-1])
    loss_row, small_res = _small_update(*small_parts, [tuple(as_2d(a) for a in given[n]) for n, *_ in _SMALL])
    loss = loss_row[0, 0]
    for (n, *_), res in zip(_SMALL, small_res):
        outs[n] = [r.reshape(given[n][0].shape) for r in res]

    order = ["w_in", "attn_sinks", "rnn_conv_w", "rnn_conv_b", "gate_a_w", "gate_a_b", "gate_x_w", "gate_x_b",
             "lru_lambda", "w_out", "ln1_g", "ln1_b", "w_ffn_up", "ffn_conv_w", "ffn_conv_b", "w_ffn_down",
             "ple_gate_w", "ple_gate_b", "ple_proj", "ln2_g", "ln2_b"]
    return (loss, dx[None], *[outs[n][0] for n in order], *[outs[n][1] for n in order],
            *[outs[n][2] for n in order], *[outs[n][3] for n in order])
```

```python
import jax
import jax.numpy as jnp
from jax import lax
from jax.experimental import pallas as pl
from jax.experimental.pallas import tpu as pltpu

F32 = jnp.float32
BF16 = jnp.bfloat16

D_MODEL = 1024
D_ATT = 512
D_KV = 128
HEAD_DIM = 64
N_HEADS = 8
N_KV = 2
D_RNN = 512
RNN_BLOCKS = 8
D_IN = 1792
D_FF = 3072
PLE_DIM = 256
QBLK = 128
N_DEV = 8
ALPHA = float(2 ** 0.25)
LN_EPS = 1e-5
LRU_C = 8.0
ADAM_LR, ADAM_B1, ADAM_B2, ADAM_EPS, ADAM_WD, ADAM_STEP = 0.001, 0.9, 0.999, 1e-08, 0.01, 10

V7X_VMEM_LIMIT = 56 * 1024 * 1024
MESH = pl.DeviceIdType.MESH


def _params(*sem, vmem=V7X_VMEM_LIMIT):
    return pltpu.CompilerParams(dimension_semantics=sem or None, vmem_limit_bytes=vmem)


def _resident(shape):
    return pl.BlockSpec(shape, lambda *_: (0,) * len(shape), pipeline_mode=pl.Buffered(1))


def _rows(tb, cols):
    return pl.BlockSpec((tb, cols), lambda i: (i, 0))


def _acc(shape):
    return pl.BlockSpec(shape, lambda *_: (0,) * len(shape))


def _dot(a, b):
    return jnp.dot(a, b, preferred_element_type=F32)


def _dot_nt(a, b):
    return lax.dot_general(a, b, (((1,), (1,)), ((), ())), preferred_element_type=F32)


def _dot_tn(a, b):
    return lax.dot_general(a, b, (((0,), (0,)), ((), ())), preferred_element_type=F32)


def _sigmoid(x):
    return 1.0 / (1.0 + jnp.exp(-x))


_GELU_C = 0.7978845608028654
_GELU_K = 0.044715


def _gelu_and_grad(x):
    u = x * x
    t = jnp.tanh(x * (_GELU_C + (_GELU_C * _GELU_K) * u))
    hp = 0.5 + 0.5 * t
    dg = hp + x * (0.5 - 0.5 * (t * t)) * (_GELU_C + (3.0 * _GELU_C * _GELU_K) * u)
    return x * hp, dg


def _gelu(x):
    return 0.5 * x * (1.0 + jnp.tanh(_GELU_C * (x + _GELU_K * x * x * x)))


def _ln_stats(z):
    mu = jnp.mean(z, axis=-1, keepdims=True)
    zc = z - mu
    var = jnp.mean(zc * zc, axis=-1, keepdims=True)
    rstd = lax.rsqrt(var + LN_EPS)
    return zc * rstd, rstd


def _ln_bwd(dy, xhat, rstd, g):
    dxh = dy * g
    m1 = jnp.mean(dxh, axis=-1, keepdims=True)
    m2 = jnp.mean(dxh * xhat, axis=-1, keepdims=True)
    return rstd * (dxh - m1 - xhat * m2)


def _softplus_neg(lam):
    u = jnp.exp(-jnp.abs(lam))
    w = 1.0 + u
    d = w - 1.0
    log1p_u = jnp.where(d == 0.0, u, jnp.log(w) * (u / jnp.where(d == 0.0, 1.0, d)))
    return jnp.maximum(-lam, 0.0) + log1p_u


def _shift_down(x, halo, s):
    xs = pltpu.roll(x, s, 0)
    hs = pltpu.roll(halo, s, 0)
    row8 = lax.broadcasted_iota(jnp.int32, hs.shape, 0)
    first = jnp.where(row8 < s, hs, xs[:8])
    return jnp.concatenate([first, xs[8:]], axis=0)


def _shift_up(x, halo, s):
    n = x.shape[0]
    xs = pltpu.roll(x, n - s, 0)
    hs = pltpu.roll(halo, 8 - s, 0)
    row8 = lax.broadcasted_iota(jnp.int32, hs.shape, 0)
    last = jnp.where(row8 >= 8 - s, hs, xs[n - 8:])
    return jnp.concatenate([xs[:n - 8], last], axis=0)


def _row_sum(x):
    return jnp.sum(x, axis=0, keepdims=True)


def _put_rows(acc_ref, rows):
    row8 = lax.broadcasted_iota(jnp.int32, acc_ref.shape, 0)
    upd = jnp.zeros(acc_ref.shape, F32)
    for r, vec in enumerate(rows):
        upd = jnp.where(row8 == r, vec, upd)
    acc_ref[...] += upd


def _place():
    return lax.axis_index("x"), lax.axis_index("y"), lax.axis_index("c")


def _dev_index(px, py, pc):
    return 4 * px + 2 * py + pc


_ANY = pl.BlockSpec(memory_space=pl.ANY)


class _Gather:
    def __init__(self, arrays):
        self.arrays = list(arrays)
        self.n = len(self.arrays)

    def out_shape(self):
        return [jax.ShapeDtypeStruct((N_DEV,) + s.shape, s.dtype) for s in self.arrays]

    def scratch(self):
        return [pltpu.SemaphoreType.DMA((self.n, 7)), pltpu.SemaphoreType.DMA((self.n, 7)),
                pltpu.SemaphoreType.DMA((self.n,))]

    def _parts(self, ins, outs, sems):
        send_sems, recv_sems, local_sems = sems
        x, y, c = _place()
        me, sibling = (x, y, c), (x, y, 1 - c)
        chips = [(1 - x, y), (x, 1 - y), (1 - x, 1 - y)]

        def copy(a, k, block, to, src=None):
            rows = outs[a].at[_dev_index(*block)]
            return pltpu.make_async_remote_copy(
                src_ref=rows if src is None else src, dst_ref=rows, send_sem=send_sems.at[a, k],
                recv_sem=recv_sems.at[a, k], device_id=to, device_id_type=MESH)

        rng = range(self.n)
        mine = [pltpu.make_async_copy(ins[a], outs[a].at[_dev_index(*me)], local_sems.at[a]) for a in rng]
        first = [copy(a, 0, me, sibling, src=ins[a]) for a in rng]
        first += [copy(a, 1 + j, me, (*chip, c), src=ins[a]) for j, chip in enumerate(chips) for a in rng]
        landed = [copy(a, 1 + j, (*chip, c), me) for j, chip in enumerate(chips) for a in rng]
        passed = [copy(a, 4 + j, (*chip, c), sibling) for j, chip in enumerate(chips) for a in rng]
        from_sibling = [copy(a, 0, sibling, me) for a in rng]
        from_sibling += [copy(a, 4 + j, (*chip, 1 - c), me) for j, chip in enumerate(chips) for a in rng]
        return mine, first, landed, passed, from_sibling

    def start(self, ins, outs, sems):
        mine, first, _, _, _ = self._parts(ins, outs, sems)
        for cp in mine + first:
            cp.start()

    def forward(self, ins, outs, sems):
        _, _, landed, passed, _ = self._parts(ins, outs, sems)
        for got, fwd in zip(landed, passed):
            got.wait_recv()
            fwd.start()

    def finish(self, ins, outs, sems):
        mine, first, _, passed, from_sibling = self._parts(ins, outs, sems)
        for cp in from_sibling:
            cp.wait_recv()
        for cp in first + passed:
            cp.wait_send()
        for cp in mine:
            cp.wait()

    def before(self, ins, outs, sems, step, nsteps):
        pl.when(step == 0)(lambda: self.start(ins, outs, sems))
        pl.when(step == (7 * nsteps) // 8)(lambda: self.forward(ins, outs, sems))

    def after(self, ins, outs, sems, step, nsteps):
        pl.when(step == nsteps - 1)(lambda: self.finish(ins, outs, sems))


class _Exchange:
    def __init__(self, arrays):
        self.arrays = list(arrays)
        self.n = len(self.arrays)

    def out_shape(self):
        return [jax.ShapeDtypeStruct(b.shape, b.dtype) for b in self.arrays]

    def scratch(self):
        return [pltpu.SemaphoreType.DMA((self.n, 7)), pltpu.SemaphoreType.DMA((self.n, 7)),
                pltpu.SemaphoreType.DMA((self.n,))]

    def _parts(self, ins, outs, sems):
        send_sems, recv_sems, local_sems = sems
        x, y, c = _place()
        me = _dev_index(x, y, c)
        peers = [(x ^ (k >> 2), y ^ ((k >> 1) & 1), c ^ (k & 1)) for k in range(1, N_DEV)]
        rng = range(self.n)
        mine = [pltpu.make_async_copy(ins[a].at[me], outs[a].at[me], local_sems.at[a]) for a in rng]
        sent = [pltpu.make_async_remote_copy(
            src_ref=ins[a].at[_dev_index(*to)], dst_ref=outs[a].at[me], send_sem=send_sems.at[a, k],
            recv_sem=recv_sems.at[a, k], device_id=to, device_id_type=MESH) for k, to in enumerate(peers) for a in rng]
        arrivals = [pltpu.make_async_remote_copy(
            src_ref=ins[a].at[me], dst_ref=outs[a].at[_dev_index(*frm)], send_sem=send_sems.at[a, k],
            recv_sem=recv_sems.at[a, k], device_id=frm, device_id_type=MESH) for k, frm in enumerate(peers) for a in rng]
        return mine, sent, arrivals

    def start(self, ins, outs, sems):
        mine, sent, _ = self._parts(ins, outs, sems)
        for cp in mine + sent:
            cp.start()

    def finish(self, ins, outs, sems):
        mine, sent, arrivals = self._parts(ins, outs, sems)
        for cp in arrivals:
            cp.wait_recv()
        for cp in sent:
            cp.wait_send()
        for cp in mine:
            cp.wait()

    def before(self, ins, outs, sems, step, nsteps):
        pl.when(step == 0)(lambda: self.start(ins, outs, sems))

    def after(self, ins, outs, sems, step, nsteps):
        pl.when(step == nsteps - 1)(lambda: self.finish(ins, outs, sems))


class _Bcast(_Exchange):
    def out_shape(self):
        return [jax.ShapeDtypeStruct((N_DEV,) + s.shape, s.dtype) for s in self.arrays]

    def _parts(self, ins, outs, sems):
        send_sems, recv_sems, local_sems = sems
        x, y, c = _place()
        me = _dev_index(x, y, c)
        peers = [(x ^ (k >> 2), y ^ ((k >> 1) & 1), c ^ (k & 1)) for k in range(1, N_DEV)]
        rng = range(self.n)
        mine = [pltpu.make_async_copy(ins[a], outs[a].at[me], local_sems.at[a]) for a in rng]
        sent = [pltpu.make_async_remote_copy(
            src_ref=ins[a], dst_ref=outs[a].at[me], send_sem=send_sems.at[a, k], recv_sem=recv_sems.at[a, k],
            device_id=to, device_id_type=MESH) for k, to in enumerate(peers) for a in rng]
        arrivals = [pltpu.make_async_remote_copy(
            src_ref=ins[a], dst_ref=outs[a].at[_dev_index(*frm)], send_sem=send_sems.at[a, k],
            recv_sem=recv_sems.at[a, k], device_id=frm, device_id_type=MESH) for k, frm in enumerate(peers) for a in rng]
        return mine, sent, arrivals


class _Multi:
    def __init__(self, comms):
        self.comms = list(comms)
        self.arrays = [arr for c in self.comms for arr in c.arrays]
        self.n = len(self.arrays)

    def out_shape(self):
        return [s for c in self.comms for s in c.out_shape()]

    def scratch(self):
        return [s for c in self.comms for s in c.scratch()]

    def _each(self, ins, outs, sems):
        a = 0
        for j, c in enumerate(self.comms):
            yield c, ins[a:a + c.n], outs[a:a + c.n], sems[3 * j:3 * j + 3]
            a += c.n

    def before(self, ins, outs, sems, step, nsteps):
        for c, ci, co, cs in self._each(ins, outs, sems):
            c.before(ci, co, cs, step, nsteps)

    def after(self, ins, outs, sems, step, nsteps):
        for c, ci, co, cs in self._each(ins, outs, sems):
            c.after(ci, co, cs, step, nsteps)


def _comm_call(comms, name):
    ns = [c.n for c in comms]
    n = sum(ns)

    def body(*refs):
        parts, a, s = [], 0, 2 * n
        for c in comms:
            parts.append((c, refs[a:a + c.n], refs[n + a:n + a + c.n], refs[s:s + 3]))
            a, s = a + c.n, s + 3
        for c, ins, outs, sems in parts:
            c.start(ins, outs, sems)
        for c, ins, outs, sems in parts:
            if isinstance(c, _Gather):
                c.forward(ins, outs, sems)
        for c, ins, outs, sems in parts:
            c.finish(ins, outs, sems)

    res = pl.pallas_call(
        body, name=name, in_specs=[_ANY] * n, out_specs=[_ANY] * n,
        out_shape=[s for c in comms for s in c.out_shape()], scratch_shapes=[s for c in comms for s in c.scratch()],
    )(*[arr for c in comms for arr in c.arrays])
    out, a = [], 0
    for k in ns:
        out.append(res[a:a + k])
        a += k
    return out


def _pcall(body, args, *, name, grid, in_specs, out_specs, out_shape, scratch_shapes=(), sem="parallel", comm=None,
           step_axis=0):
    sem = (sem,) * len(grid) if isinstance(sem, str) else sem
    if comm is None:
        res = pl.pallas_call(body, name=name, grid=grid, in_specs=in_specs, out_specs=out_specs, out_shape=out_shape,
                             scratch_shapes=list(scratch_shapes), compiler_params=_params(*sem))(*args)
        return res, []
    n_in, n_out, n_scr, n = len(in_specs), len(out_specs), len(scratch_shapes), comm.n
    nsteps = grid[step_axis]
    assert all(g == 1 for ax, g in enumerate(grid) if ax != step_axis)

    def hosted(*refs):
        ins, cin = refs[:n_in], refs[n_in:n_in + n]
        o0 = n_in + n
        outs, cout = refs[o0:o0 + n_out], refs[o0 + n_out:o0 + n_out + n]
        s0 = o0 + n_out + n
        scr, sems = refs[s0:s0 + n_scr], refs[s0 + n_scr:]
        step = pl.program_id(step_axis)
        comm.before(cin, cout, sems, step, nsteps)
        body(*ins, *outs, *scr)
        comm.after(cin, cout, sems, step, nsteps)

    res = pl.pallas_call(
        hosted, name=name, grid=grid, in_specs=list(in_specs) + [_ANY] * n, out_specs=list(out_specs) + [_ANY] * n,
        out_shape=list(out_shape) + comm.out_shape(), scratch_shapes=list(scratch_shapes) + comm.scratch(),
        compiler_params=_params(*(("arbitrary",) * len(grid))))(*args, *comm.arrays)
    return res[:n_out], res[n_out:]


def _load_row_halves(top_hbm, bot_hbm, full_s, sems):
    r = top_hbm.shape[1]
    copies = [pltpu.make_async_copy(top_hbm, full_s.at[:, :r, :], sems.at[0]),
              pltpu.make_async_copy(bot_hbm, full_s.at[:, r:, :], sems.at[1])]
    for cp in copies:
        cp.start()
    for cp in copies:
        cp.wait()


def _in_proj(x, w_in_t, comm=None):
    S = x.shape[0]
    tb = min(1024, S)

    def body(x_ref, w_ref, q_ref, k_ref, v_ref, xr_ref, gr_ref):
        u = _dot_nt(x_ref[...].astype(BF16), w_ref[...])
        q_ref[...] = (u[:, :D_ATT] * (HEAD_DIM ** -0.5)).astype(BF16)
        k_ref[...] = u[:, D_ATT:D_ATT + D_KV].astype(BF16)
        v_ref[...] = u[:, D_ATT + D_KV:D_ATT + 2 * D_KV].astype(BF16)
        xr_ref[...] = u[:, D_ATT + 2 * D_KV:D_ATT + 2 * D_KV + D_RNN]
        gr_ref[...] = u[:, D_ATT + 2 * D_KV + D_RNN:]

    return _pcall(
        body, (x, w_in_t), name="in_proj", grid=(S // tb,), comm=comm,
        in_specs=[_rows(tb, D_MODEL), _resident((D_IN, D_MODEL))],
        out_specs=[_rows(tb, D_ATT), _rows(tb, D_KV), _rows(tb, D_KV), _rows(tb, D_RNN), _rows(tb, D_RNN)],
        out_shape=[jax.ShapeDtypeStruct((S, D_ATT), BF16), jax.ShapeDtypeStruct((S, D_KV), BF16),
                   jax.ShapeDtypeStruct((S, D_KV), BF16), jax.ShapeDtypeStruct((S, D_RNN), F32),
                   jax.ShapeDtypeStruct((S, D_RNN), F32)])


GROUP = N_HEADS // N_KV


def _band_mask(i):
    qi = lax.broadcasted_iota(jnp.int32, (GROUP * QBLK, 2 * QBLK), 0) & (QBLK - 1)
    sj = lax.broadcasted_iota(jnp.int32, (GROUP * QBLK, 2 * QBLK), 1)
    return (sj > qi) & (sj <= qi + QBLK) & ((sj >= QBLK) | (i > 0))


def _stack_heads(x, g):
    return jnp.concatenate([x[:, (g * GROUP + hh) * HEAD_DIM:(g * GROUP + hh + 1) * HEAD_DIM] for hh in range(GROUP)],
                           axis=0)


def _unstack_heads(x4):
    return [x4[hh * QBLK:(hh + 1) * QBLK] for hh in range(GROUP)]


def _sink_column(sink_ref, g):
    head = lax.broadcasted_iota(jnp.int32, (GROUP * QBLK, 1), 0) // QBLK
    col = jnp.full((GROUP * QBLK, 1), sink_ref[g * GROUP], F32)
    for hh in range(1, GROUP):
        col = jnp.where(head == hh, sink_ref[g * GROUP + hh], col)
    return col


ATT_STEP = 4
IN_GRAD_PARTS = 2


def _attn_specs(nq=1):
    cur = lambda i: (i, 0)
    prev = lambda i: (jnp.maximum(nq * i - 1, 0), 0)
    return [pl.BlockSpec((nq * QBLK, D_KV), cur), pl.BlockSpec((QBLK, D_KV), prev),
            pl.BlockSpec((nq * QBLK, D_KV), cur), pl.BlockSpec((QBLK, D_KV), prev)]


def _attn_fwd(q, k, v, sinks, comm=None):
    S = q.shape[0]
    nq = min(ATT_STEP, S // QBLK)

    def body(sink_ref, q_ref, kc_ref, kp_ref, vc_ref, vp_ref, o_ref, lse_ref):
        first = pl.program_id(0) * nq
        kall = jnp.concatenate([kp_ref[...], kc_ref[...]], axis=0)
        vall = jnp.concatenate([vp_ref[...], vc_ref[...]], axis=0)
        for b in range(nq):
            valid = _band_mask(first + b)
            rows = slice(b * QBLK, (b + 1) * QBLK)
            keys = slice(b * QBLK, (b + 2) * QBLK)
            qv = q_ref[rows, :]
            outs = []
            for g in range(N_KV):
                kcat = kall[keys, g * HEAD_DIM:(g + 1) * HEAD_DIM]
                vcat = vall[keys, g * HEAD_DIM:(g + 1) * HEAD_DIM]
                s = jnp.where(valid, _dot_nt(_stack_heads(qv, g), kcat), -1e30)
                sink = _sink_column(sink_ref, g)
                m = jnp.maximum(jnp.max(s, axis=1, keepdims=True), sink)
                p = jnp.exp(s - m)
                l = jnp.sum(p, axis=1, keepdims=True) + jnp.exp(sink - m)
                outs += _unstack_heads(_dot(p.astype(BF16), vcat) / l)
                lse_ref[(b * N_KV + g) * GROUP * QBLK:(b * N_KV + g + 1) * GROUP * QBLK, :] = m + jnp.log(l)
            o_ref[rows, :] = jnp.concatenate(outs, axis=1).astype(BF16)

    lse_rows = nq * N_HEADS * QBLK
    return _pcall(
        body, (sinks, q, k, k, v, v), name="attn_fwd", grid=(S // (nq * QBLK),), comm=comm,
        in_specs=[pl.BlockSpec(memory_space=pltpu.SMEM), _rows(nq * QBLK, D_ATT)] + _attn_specs(nq),
        out_specs=[_rows(nq * QBLK, D_ATT), _rows(lse_rows, 1)],
        out_shape=[jax.ShapeDtypeStruct((S, D_ATT), BF16), jax.ShapeDtypeStruct((S * N_HEADS, 1), F32)])


def _w_rows(w_ref):
    return [w_ref[k:k + 1, :] for k in range(w_ref.shape[0])]


def _conv4(x, halo, w, b):
    y = b + w[3] * x
    for s in (1, 2, 3):
        y = y + w[3 - s] * _shift_down(x, halo, s)
    return y


def _rnn_gates(xc, wa, wx, ba, bx, sp):
    xcb = xc.astype(BF16)
    r = _sigmoid(_dot(xcb, wa) + ba)
    ig = _sigmoid(_dot(xcb, wx) + bx)
    la = -LRU_C * r * sp
    a = jnp.exp(la)
    t = jnp.tanh(la)
    f = jnp.sqrt(-2.0 * t / (1.0 - t))
    return r, ig, a, f


def _rnn_fwd(xr, gr, conv_w, conv_b, wa, wx, ba, bx, lam, comm=None):
    S = xr.shape[0]
    tb = min(512, S)

    def body(xr_ref, gr_ref, cw_ref, cb_ref, wa_ref, wx_ref, ba_ref, bx_ref, lam_ref, rec_ref, h_ref,
             xc_ref, r_ref, ig_ref, a_ref, f_ref, halo_s, hc_s, a_s, b_s):
        @pl.when(pl.program_id(0) == 0)
        def _():
            halo_s[...] = jnp.zeros_like(halo_s)
            hc_s[...] = jnp.zeros_like(hc_s)

        x = xr_ref[...]
        xc = _conv4(x, halo_s[...], _w_rows(cw_ref), cb_ref[...])
        halo_s[...] = x[tb - 8:]
        r, ig, a, f = _rnn_gates(xc, wa_ref[...], wx_ref[...], ba_ref[...], bx_ref[...], _softplus_neg(lam_ref[...]))
        xc_ref[...] = xc
        r_ref[...] = r
        ig_ref[...] = ig
        a_ref[...] = a
        f_ref[...] = f
        a_s[...] = a
        b_s[...] = f * ig * xc
        row8 = lax.broadcasted_iota(jnp.int32, (8, D_RNN), 0)

        def tile(t, hc):
            o = pl.multiple_of(t * 8, 8)
            at = a_s[pl.ds(o, 8), :]
            bt = b_s[pl.ds(o, 8), :]
            for s in (1, 2, 4):
                keep = row8 >= s
                a_sh = jnp.where(keep, pltpu.roll(at, s, 0), 1.0)
                b_sh = jnp.where(keep, pltpu.roll(bt, s, 0), 0.0)
                bt = at * b_sh + bt
                at = at * a_sh
            ht = at * hc + bt
            b_s[pl.ds(o, 8), :] = ht
            return _row_sum(jnp.where(row8 == 7, ht, 0.0))

        hc_s[0:1, :] = lax.fori_loop(0, tb // 8, tile, hc_s[0:1, :], unroll=2)
        h = b_s[...]
        h_ref[...] = h
        rec_ref[...] = (h * _gelu(gr_ref[...])).astype(BF16)

    vec = _resident((1, D_RNN))
    kept = jax.ShapeDtypeStruct((S, D_RNN), F32)
    return _pcall(
        body, (xr, gr, conv_w, conv_b, wa, wx, ba, bx, lam), name="rnn_fwd", grid=(S // tb,), sem="arbitrary", comm=comm,
        in_specs=[_rows(tb, D_RNN), _rows(tb, D_RNN), _resident((4, D_RNN)), vec,
                  _resident((D_RNN, D_RNN)), _resident((D_RNN, D_RNN)), vec, vec, vec],
        out_specs=[_rows(tb, D_RNN)] * 7,
        out_shape=[jax.ShapeDtypeStruct((S, D_RNN), BF16), kept, kept, kept, kept, kept, kept],
        scratch_shapes=[pltpu.VMEM((8, D_RNN), F32), pltpu.VMEM((8, D_RNN), F32),
                        pltpu.VMEM((tb, D_RNN), F32), pltpu.VMEM((tb, D_RNN), F32)])


def _mix_ln1_up(x, att, rec, w_out, ln1_g, ln1_b, w_up_top, w_up_bot, fcw, fcb, comm=None):
    S = x.shape[0]
    tb = min(256, S)
    nblk, kh, wblk = w_up_top.shape
    half = nblk // 2

    def body(x_ref, att_ref, rec_ref, wo_ref, g_ref, b_ref, wt_hbm, wb_hbm, fcw_ref, fcb_ref,
             z1_ref, h1b_ref, gate_ref, act_ref, gl_ref, vdgl_ref, halo_s, wu_s, wu_sems):
        @pl.when(pl.program_id(0) == 0)
        def _():
            halo_s[...] = jnp.zeros_like(halo_s)
            _load_row_halves(wt_hbm, wb_hbm, wu_s, wu_sems)

        z1 = ALPHA * x_ref[...] + _dot(att_ref[...], wo_ref[:D_ATT, :]) + _dot(rec_ref[...], wo_ref[D_ATT:, :])
        z1_ref[...] = z1
        xhat, _ = _ln_stats(z1)
        h1b = (xhat * g_ref[...] + b_ref[...]).astype(BF16)
        h1b_ref[...] = h1b
        for jj in range(half):
            cols = slice(jj * wblk, (jj + 1) * wblk)
            gate = _dot(h1b, wu_s[jj])
            val = _dot(h1b, wu_s[jj + half])
            halo = halo_s[:, cols]
            conv = (fcb_ref[:, cols] + fcw_ref[2:3, cols] * gate + fcw_ref[1:2, cols] * _shift_down(gate, halo, 1)
                    + fcw_ref[0:1, cols] * _shift_down(gate, halo, 2))
            halo_s[:, cols] = gate[tb - 8:]
            gl, dgl = _gelu_and_grad(conv)
            gate_ref[:, cols] = gate.astype(BF16)
            act_ref[:, cols] = (gl * val).astype(BF16)
            gl_ref[:, cols] = gl.astype(BF16)
            vdgl_ref[:, cols] = (val * dgl).astype(BF16)

    vec = _resident((1, D_MODEL))
    wide = jax.ShapeDtypeStruct((S, D_FF), BF16)
    return _pcall(
        body, (x, att, rec, w_out, ln1_g, ln1_b, w_up_top, w_up_bot, fcw, fcb), name="mix_ln1_up", grid=(S // tb,),
        sem="arbitrary", comm=comm,
        in_specs=[_rows(tb, D_MODEL), _rows(tb, D_ATT), _rows(tb, D_RNN), _resident((D_MODEL, D_MODEL)), vec, vec,
                  _ANY, _ANY, _resident((3, D_FF)), _resident((1, D_FF))],
        out_specs=[_rows(tb, D_MODEL), _rows(tb, D_MODEL)] + [_rows(tb, D_FF)] * 4,
        out_shape=[jax.ShapeDtypeStruct((S, D_MODEL), F32), jax.ShapeDtypeStruct((S, D_MODEL), BF16), wide, wide, wide, wide],
        scratch_shapes=[pltpu.VMEM((8, D_FF), F32), pltpu.VMEM((nblk, 2 * kh, wblk), BF16),
                        pltpu.SemaphoreType.DMA((2,))])


def _tail(act, gl, vdgl, z1, h1b, p, tgt, w_down, w_pg, b_pg, w_pp, ln1_g, ln1_b, ln2_g, ln2_b):
    S = z1.shape[0]
    tb = min(256, S)

    def body(act_ref, gl_ref, vdgl_ref, z1_ref, h1b_ref, p_ref, t_ref, wd_ref, wpg_ref, bpg_ref, wpp_ref,
             g1_ref, b1_ref, g2_ref, b2_ref, dz2_ref, dpre_ref, dpp_ref, dgc_ref, dval_ref, dh1_ref, acc_ref):
        i = pl.program_id(0)

        @pl.when(i == 0)
        def _():
            acc_ref[...] = jnp.zeros_like(acc_ref)

        ffn = _dot(act_ref[...], wd_ref[...])
        xhat1, _ = _ln_stats(z1_ref[...])
        h1 = xhat1 * g1_ref[...] + b1_ref[...]
        sg = _sigmoid(_dot(h1b_ref[...], wpg_ref[...]) + bpg_ref[...])
        pp = _dot(p_ref[...].astype(BF16), wpp_ref[...])
        z2 = ALPHA * h1 + ffn + sg * pp
        xhat2, rstd2 = _ln_stats(z2)
        y = xhat2 * g2_ref[...] + b2_ref[...]
        err = y - t_ref[...]
        dy = err * (1.0 / D_MODEL)
        loss = 0.5 * jnp.sum(jnp.sum(err * err, axis=1, keepdims=True), axis=0, keepdims=True) * (1.0 / D_MODEL)
        dz2 = _ln_bwd(dy, xhat2, rstd2, g2_ref[...])
        dz2b = dz2.astype(BF16)
        dz2_ref[...] = dz2b
        dpre = dz2 * pp * sg * (1.0 - sg)
        dpreb = dpre.astype(BF16)
        dpre_ref[...] = dpreb
        dpp_ref[...] = (dz2 * sg).astype(BF16)
        dh1_ref[...] = ALPHA * dz2 + _dot_nt(dpreb, wpg_ref[...])
        dactb = _dot_nt(dz2b, wd_ref[...]).astype(BF16)
        dval_ref[...] = dactb * gl_ref[...]
        dgc_ref[...] = dactb * vdgl_ref[...]
        _put_rows(acc_ref, [_row_sum(dy * xhat2), _row_sum(dy), _row_sum(dpre),
                            jnp.broadcast_to(loss, (1, D_MODEL))])

    vec = _resident((1, D_MODEL))
    return pl.pallas_call(
        body, name="tail", grid=(S // tb,),
        in_specs=[_rows(tb, D_FF), _rows(tb, D_FF), _rows(tb, D_FF), _rows(tb, D_MODEL), _rows(tb, D_MODEL),
                  _rows(tb, PLE_DIM), _rows(tb, D_MODEL), _resident((D_FF, D_MODEL)), _resident((D_MODEL, D_MODEL)), vec,
                  _resident((PLE_DIM, D_MODEL)), vec, vec, vec, vec],
        out_specs=[_rows(tb, D_MODEL), _rows(tb, D_MODEL), _rows(tb, D_MODEL), _rows(tb, D_FF),
                   _rows(tb, D_FF), _rows(tb, D_MODEL), _acc((8, D_MODEL))],
        out_shape=[jax.ShapeDtypeStruct((S, D_MODEL), BF16),
                   jax.ShapeDtypeStruct((S, D_MODEL), BF16), jax.ShapeDtypeStruct((S, D_MODEL), BF16),
                   jax.ShapeDtypeStruct((S, D_FF), BF16), jax.ShapeDtypeStruct((S, D_FF), BF16),
                   jax.ShapeDtypeStruct((S, D_MODEL), F32), jax.ShapeDtypeStruct((8, D_MODEL), F32)],
        compiler_params=_params("arbitrary"),
    )(act, gl, vdgl, z1, h1b, p, tgt, w_down, w_pg, b_pg, w_pp, ln1_g, ln1_b, ln2_g, ln2_b)


def _weight_grad(a_list, b_list, name, layout, ts=512, comm=None, b_window=None, halves=False):
    S = a_list[0].shape[0]
    ms = [a.shape[1] for a in a_list]
    M, nb = sum(ms), len(b_list)
    win, Nb = b_window if b_window else (0, b_list[0].shape[1])
    ts = min(ts, S)
    nk = S // ts
    per_b = N_DEV // nb
    na = len(a_list)

    n_out = 2 if halves else 1
    assert layout == "cols" or not halves

    def body(*refs):
        a_refs, b_refs, o_refs, acc_ref = refs[:na], refs[na:na + nb], refs[na + nb:na + nb + n_out], refs[-1]
        o_ref = o_refs[0]
        j, k = pl.program_id(0), pl.program_id(1)

        @pl.when(k == 0)
        def _():
            acc_ref[...] = jnp.zeros_like(acc_ref)

        for jj in range(nb):
            @pl.when(j == jj)
            def _():
                b = b_refs[jj][...].astype(BF16)
                off = 0
                for a_ref, m in zip(a_refs, ms):
                    acc_ref[off:off + m, :] += _dot_tn(a_ref[...].astype(BF16), b)
                    off += m

        @pl.when(k == nk - 1)
        def _():
            for d in range(per_b):
                if layout == "rows":
                    o_ref[d] = acc_ref[d * (M // N_DEV):(d + 1) * (M // N_DEV), :].astype(BF16)
                elif layout == "cols" and halves:
                    for o_half, r0 in zip(o_refs, (0, M // 2)):
                        o_half[d] = acc_ref[r0:r0 + M // 2, d * (Nb // per_b):(d + 1) * (Nb // per_b)].astype(BF16)
                elif layout == "cols":
                    o_ref[d] = acc_ref[:, d * (Nb // per_b):(d + 1) * (Nb // per_b)].astype(BF16)
                else:
                    o_ref[d] = acc_ref[:, d * (Nb // per_b):(d + 1) * (Nb // per_b)].T.astype(BF16)

    def b_index(jj):
        return lambda j, k: (jnp.where(j == jj, k, jnp.where(j < jj, 0, nk - 1)), win)

    if layout == "rows":
        assert nb == 1
        blk = (N_DEV, M // N_DEV, Nb)
    elif layout == "cols":
        blk = (per_b, M // n_out, Nb // per_b)
    else:
        blk = (per_b, Nb // per_b, M)
    res, comm_res = _pcall(
        body, (*a_list, *b_list), name=name, grid=(nb, nk), sem="arbitrary", comm=comm, step_axis=1,
        in_specs=[pl.BlockSpec((ts, m), lambda j, k: (k, 0)) for m in ms]
        + [pl.BlockSpec((ts, Nb), b_index(jj)) for jj in range(nb)],
        out_specs=[pl.BlockSpec(blk, lambda j, k: (j, 0, 0))] * n_out,
        out_shape=[jax.ShapeDtypeStruct((N_DEV,) + blk[1:], BF16)] * n_out,
        scratch_shapes=[pltpu.VMEM((M, Nb), F32)])
    res = res if halves else res[0]
    return (res, comm_res) if comm is not None else res


def _up_bwd(dgc, gate, dval, dh1p, z1, w_up_top, w_up_bot, fcw, w_out, ln1_g, comm=None):
    S = z1.shape[0]
    tb = min(256, S)
    t16 = tb // 16
    n16 = S // 16
    nblk, kh, wblk = w_up_top.shape
    half = nblk // 2
    nsteps = S // tb

    def body(dgc_ref, dgn_ref, gc_ref, dval_ref, dh1p_ref, z1_ref, wt_hbm, wb_hbm, fcw_ref, wo_ref, g1_ref,
             dgate_ref, dz1_ref, dz1b_ref, datt_ref, drec_ref, accf_ref, accd_ref, wu_s, wu_sems):
        i = pl.program_id(0)

        @pl.when(i == 0)
        def _():
            accf_ref[...] = jnp.zeros_like(accf_ref)
            accd_ref[...] = jnp.zeros_like(accd_ref)
            _load_row_halves(wt_hbm, wb_hbm, wu_s, wu_sems)

        dg = dgc_ref[...].astype(F32)
        nxt = jnp.where(i < nsteps - 1, dgn_ref[...].astype(F32)[0:8], 0.0)
        w = _w_rows(fcw_ref)
        up1, up2 = _shift_up(dg, nxt, 1), _shift_up(dg, nxt, 2)
        dgate = (w[2] * dg + w[1] * up1 + w[0] * up2).astype(BF16)
        dgate_ref[...] = dgate
        gate = gc_ref[...].astype(F32)
        _put_rows(accf_ref, [_row_sum(up2 * gate), _row_sum(up1 * gate), _row_sum(dg * gate), _row_sum(dg)])

        dh1 = dh1p_ref[...]
        for j in range(nblk):
            src = dgate if j < half else dval_ref[...]
            jj = j % half
            dh1 = dh1 + _dot_nt(src[:, jj * wblk:(jj + 1) * wblk], wu_s[j])
        xhat1, rstd1 = _ln_stats(z1_ref[...])
        dz1 = _ln_bwd(dh1, xhat1, rstd1, g1_ref[...])
        dz1_ref[...] = dz1
        dz1b = dz1.astype(BF16)
        dz1b_ref[...] = dz1b
        dcat = _dot_nt(dz1b, wo_ref[...])
        datt_ref[...] = dcat[:, :D_ATT].astype(BF16)
        drec_ref[...] = dcat[:, D_ATT:]
        _put_rows(accd_ref, [_row_sum(dh1 * xhat1), _row_sum(dh1)])

    next16 = pl.BlockSpec((16, D_FF), lambda i: (jnp.minimum((i + 1) * t16, n16 - 1), 0))
    return _pcall(
        body, (dgc, dgc, gate, dval, dh1p, z1, w_up_top, w_up_bot, fcw, w_out, ln1_g), name="up_bwd",
        grid=(nsteps,), sem="arbitrary", comm=comm,
        in_specs=[_rows(tb, D_FF), next16, _rows(tb, D_FF), _rows(tb, D_FF), _rows(tb, D_MODEL),
                  _rows(tb, D_MODEL), _ANY, _ANY, _resident((3, D_FF)),
                  _resident((D_MODEL, D_MODEL)), _resident((1, D_MODEL))],
        scratch_shapes=[pltpu.VMEM((nblk, 2 * kh, wblk), BF16), pltpu.SemaphoreType.DMA((2,))],
        out_specs=[_rows(tb, D_FF), _rows(tb, D_MODEL), _rows(tb, D_MODEL), _rows(tb, D_ATT), _rows(tb, D_RNN),
                   _acc((8, D_FF)), _acc((8, D_MODEL))],
        out_shape=[jax.ShapeDtypeStruct((S, D_FF), BF16), jax.ShapeDtypeStruct((S, D_MODEL), F32),
                   jax.ShapeDtypeStruct((S, D_MODEL), BF16), jax.ShapeDtypeStruct((S, D_ATT), BF16),
                   jax.ShapeDtypeStruct((S, D_RNN), F32), jax.ShapeDtypeStruct((8, D_FF), F32),
                   jax.ShapeDtypeStruct((8, D_MODEL), F32)])


def _attn_bwd(q, k, v, lse, do, sinks, comm=None):
    S = q.shape[0]
    grp = N_HEADS // N_KV
    nq = min(ATT_STEP, S // QBLK)

    def body(sink_ref, q_ref, kc_ref, kp_ref, vc_ref, vp_ref, do_ref, lse_ref, dq_ref, dkc_ref, dkp_ref, dvc_ref, dvp_ref,
             ds_ref):
        i = pl.program_id(0)

        @pl.when(i == 0)
        def _():
            ds_ref[...] = jnp.zeros_like(ds_ref)

        row8 = lax.broadcasted_iota(jnp.int32, (8, 128), 0)
        lane8 = lax.broadcasted_iota(jnp.int32, (8, 128), 1)
        dsink = jnp.zeros((8, 128), F32)
        kall = jnp.concatenate([kp_ref[...], kc_ref[...]], axis=0)
        vall = jnp.concatenate([vp_ref[...], vc_ref[...]], axis=0)
        dk_t = [jnp.zeros((D_KV, QBLK), F32) for _ in range(nq + 1)]
        dv_t = [jnp.zeros((D_KV, QBLK), F32) for _ in range(nq + 1)]
        for b in range(nq):
            valid = _band_mask(i * nq + b)
            rows = slice(b * QBLK, (b + 1) * QBLK)
            keys = slice(b * QBLK, (b + 2) * QBLK)
            qv, dov = q_ref[rows, :], do_ref[rows, :]
            dqs, dks, dvs = [], [], []
            for g in range(N_KV):
                kcat = kall[keys, g * HEAD_DIM:(g + 1) * HEAD_DIM]
                vcat = vall[keys, g * HEAD_DIM:(g + 1) * HEAD_DIM]
                q4, do4 = _stack_heads(qv, g), _stack_heads(dov, g)
                s = jnp.where(valid, _dot_nt(q4, kcat), -1e30)
                lse = lse_ref[(b * N_KV + g) * GROUP * QBLK:(b * N_KV + g + 1) * GROUP * QBLK, :]
                p = jnp.exp(s - lse)
                p_sink = jnp.exp(_sink_column(sink_ref, g) - lse)
                dp = _dot_nt(do4, vcat)
                delta = jnp.sum(p * dp, axis=1, keepdims=True)
                dsc = (p * (dp - delta)).astype(BF16)
                dqs += _unstack_heads(_dot(dsc, kcat) * (HEAD_DIM ** -0.5))
                dks.append(_dot_tn(q4, dsc))
                dvs.append(_dot_tn(do4, p.astype(BF16)))
                for hh, part in enumerate(_unstack_heads(-p_sink * delta)):
                    here = (row8 == 0) & (lane8 == g * grp + hh)
                    dsink = dsink + jnp.where(here, jnp.sum(part, axis=0, keepdims=True), 0.0)
            dq_ref[rows, :] = jnp.concatenate(dqs, axis=1).astype(BF16)
            dk2, dv2 = jnp.concatenate(dks, axis=0), jnp.concatenate(dvs, axis=0)
            dk_t[b], dk_t[b + 1] = dk_t[b] + dk2[:, :QBLK], dk_t[b + 1] + dk2[:, QBLK:]
            dv_t[b], dv_t[b + 1] = dv_t[b] + dv2[:, :QBLK], dv_t[b + 1] + dv2[:, QBLK:]
        dkp_ref[...] = dk_t[0].T
        dvp_ref[...] = dv_t[0].T
        for b in range(nq):
            dkc_ref[b * QBLK:(b + 1) * QBLK, :] = dk_t[b + 1].T
            dvc_ref[b * QBLK:(b + 1) * QBLK, :] = dv_t[b + 1].T
        ds_ref[...] += dsink

    nsteps = S // (nq * QBLK)
    cur = jax.ShapeDtypeStruct((S, D_KV), F32)
    prev = jax.ShapeDtypeStruct((nsteps * QBLK, D_KV), F32)
    big = _rows(nq * QBLK, D_ATT)
    return _pcall(
        body, (sinks, q, k, k, v, v, do, lse), name="attn_bwd", grid=(nsteps,), sem="arbitrary", comm=comm,
        in_specs=[pl.BlockSpec(memory_space=pltpu.SMEM), big] + _attn_specs(nq) + [big, _rows(nq * N_HEADS * QBLK, 1)],
        out_specs=[big, _rows(nq * QBLK, D_KV), _rows(QBLK, D_KV), _rows(nq * QBLK, D_KV), _rows(QBLK, D_KV),
                   _acc((8, 128))],
        out_shape=[jax.ShapeDtypeStruct((S, D_ATT), BF16), cur, prev, cur, prev, jax.ShapeDtypeStruct((8, 128), F32)])


def _rnn_bwd(xr, gr, h, kept, drec, conv_w, wa, wx, lam, comm=None):
    S = xr.shape[0]
    tb = min(512, S)
    t8 = tb // 8
    nsteps = S // tb

    def body(xr_ref, xp_ref, gr_ref, h_ref, hp_ref, xc_ref, r_ref, ig_ref, a_ref, f_ref, drec_ref, cw_ref, wa_ref, wx_ref,
             lam_ref, dxr_ref, dgr_ref, gwa_ref, gwx_ref, acc_ref, carry_s, dxc_halo_s, d_s, gwa_s, gwx_s):
        i = pl.program_id(0)
        blk = nsteps - 1 - i

        @pl.when(i == 0)
        def _():
            gwa_s[...] = jnp.zeros_like(gwa_s)
            gwx_s[...] = jnp.zeros_like(gwx_s)
            acc_ref[...] = jnp.zeros_like(acc_ref)
            carry_s[...] = jnp.zeros_like(carry_s)
            dxc_halo_s[...] = jnp.zeros_like(dxc_halo_s)

        x = xr_ref[...]
        xhalo = jnp.where(blk > 0, xp_ref[...], 0.0)
        cw = _w_rows(cw_ref)
        xs = [_shift_down(x, xhalo, 3), _shift_down(x, xhalo, 2), _shift_down(x, xhalo, 1), x]
        xc, r, ig, a, f = xc_ref[...], r_ref[...], ig_ref[...], a_ref[...], f_ref[...]
        sp = _softplus_neg(lam_ref[...])
        hcur = h_ref[...]
        hprev = _shift_down(hcur, jnp.where(blk > 0, hp_ref[...], 0.0), 1)
        gl, dgl = _gelu_and_grad(gr_ref[...])
        drec = drec_ref[...]
        dgr_ref[...] = (drec * hcur * dgl).astype(BF16)
        d_s[...] = drec * gl
        row8 = lax.broadcasted_iota(jnp.int32, (8, D_RNN), 0)

        def tile(t, c):
            o = pl.multiple_of((t8 - 1 - t) * 8, 8)
            a8 = a_ref[pl.ds(o, 8), :]
            dt = d_s[pl.ds(o, 8), :]
            at = jnp.where(row8 == 7, 1.0, pltpu.roll(a8, 7, 0))
            for s in (1, 2, 4):
                keep = row8 < 8 - s
                a_sh = jnp.where(keep, pltpu.roll(at, 8 - s, 0), 1.0)
                d_sh = jnp.where(keep, pltpu.roll(dt, 8 - s, 0), 0.0)
                dt = at * d_sh + dt
                at = at * a_sh
            lt = at * c + dt
            d_s[pl.ds(o, 8), :] = lt
            return _row_sum(jnp.where(row8 == 0, a8 * lt, 0.0))

        carry_s[0:1, :] = lax.fori_loop(0, t8, tile, carry_s[0:1, :], unroll=2)
        lmb = d_s[...]
        a2 = a * a
        dla = lmb * hprev * a - lmb * ig * xc * (a2 / f)
        di = lmb * f * xc
        dr = dla * (-LRU_C) * sp
        dpa = dr * r * (1.0 - r)
        dpx = di * ig * (1.0 - ig)
        dpab = dpa.astype(BF16)
        dpxb = dpx.astype(BF16)
        xcb = xc.astype(BF16)
        gwa_s[...] += _dot_tn(xcb, dpab)
        gwx_s[...] += _dot_tn(xcb, dpxb)

        @pl.when(i == nsteps - 1)
        def _():
            for dense, out in ((gwa_s[...], gwa_ref), (gwx_s[...], gwx_ref)):
                for b in range(RNN_BLOCKS):
                    rows = slice(b * HEAD_DIM, (b + 1) * HEAD_DIM)
                    out[rows, :] = dense[rows, b * HEAD_DIM:(b + 1) * HEAD_DIM]

        dxc = lmb * f * ig + _dot_nt(dpab, wa_ref[...]) + _dot_nt(dpxb, wx_ref[...])
        nxt = dxc_halo_s[...]
        dxr = cw[3] * dxc
        for s in (1, 2, 3):
            dxr = dxr + cw[3 - s] * _shift_up(dxc, nxt, s)
        dxr_ref[...] = dxr.astype(BF16)
        dxc_halo_s[...] = dxc[:8]
        dlam = _row_sum(dla * (-LRU_C) * r) * (-1.0 / (1.0 + jnp.exp(lam_ref[...])))
        _put_rows(acc_ref, [_row_sum(dxc * xs[0]), _row_sum(dxc * xs[1]), _row_sum(dxc * xs[2]), _row_sum(dxc * xs[3]),
                            _row_sum(dxc), _row_sum(dpa), _row_sum(dpx), dlam])

    rev = lambda i: (nsteps - 1 - i, 0)
    prev8 = lambda i: (jnp.maximum((nsteps - 1 - i) * t8 - 1, 0), 0)
    blkspec = pl.BlockSpec((tb, D_RNN), rev)
    halo8 = pl.BlockSpec((8, D_RNN), prev8)
    vec = _resident((1, D_RNN))
    return _pcall(
        body, (xr, xr, gr, h, h, *kept, drec, conv_w, wa, wx, lam), name="rnn_bwd", grid=(nsteps,),
        sem="arbitrary", comm=comm,
        in_specs=[blkspec, halo8, blkspec, blkspec, halo8] + [blkspec] * 6
        + [_resident((4, D_RNN)), _resident((D_RNN, D_RNN)), _resident((D_RNN, D_RNN)), vec],
        out_specs=[blkspec, blkspec, _acc((D_RNN, HEAD_DIM)), _acc((D_RNN, HEAD_DIM)), _acc((8, D_RNN))],
        out_shape=[jax.ShapeDtypeStruct((S, D_RNN), BF16), jax.ShapeDtypeStruct((S, D_RNN), BF16),
                   jax.ShapeDtypeStruct((D_RNN, HEAD_DIM), F32), jax.ShapeDtypeStruct((D_RNN, HEAD_DIM), F32),
                   jax.ShapeDtypeStruct((8, D_RNN), F32)],
        scratch_shapes=[pltpu.VMEM((8, D_RNN), F32), pltpu.VMEM((8, D_RNN), F32), pltpu.VMEM((tb, D_RNN), F32),
                        pltpu.VMEM((D_RNN, D_RNN), F32), pltpu.VMEM((D_RNN, D_RNN), F32)])


def _in_bwd(dq, dkc, dkp, dvc, dvp, dxr, dgr, dz1, w_in, comm=None):
    S = dz1.shape[0]
    tb = min(ATT_STEP * QBLK, S)
    nsteps = S // tb

    def body(dq_ref, dkc_ref, dkn_ref, dvc_ref, dvn_ref, dxr_ref, dgr_ref, dz1_ref, w_ref, dkv_ref, dx_ref):
        last = pl.program_id(0) == nsteps - 1

        def total(cur_ref, next_ref):
            nxt = jnp.where(last, 0.0, next_ref[...])
            tail = cur_ref[tb - QBLK:, :] + nxt
            return jnp.concatenate([cur_ref[:tb - QBLK, :], tail], axis=0) if tb > QBLK else tail

        dkv = jnp.concatenate([total(dkc_ref, dkn_ref), total(dvc_ref, dvn_ref)], axis=1).astype(BF16)
        dkv_ref[...] = dkv
        du = jnp.concatenate([dq_ref[...], dkv, dxr_ref[...], dgr_ref[...]], axis=1)
        dx_ref[...] = ALPHA * dz1_ref[...] + _dot(du, w_ref[...])

    nextp = pl.BlockSpec((QBLK, D_KV), lambda i: (jnp.minimum(i + 1, nsteps - 1), 0))
    return _pcall(
        body, (dq, dkc, dkp, dvc, dvp, dxr, dgr, dz1, w_in), name="in_bwd", grid=(nsteps,), comm=comm,
        in_specs=[_rows(tb, D_ATT), _rows(tb, D_KV), nextp, _rows(tb, D_KV), nextp,
                  _rows(tb, D_RNN), _rows(tb, D_RNN), _rows(tb, D_MODEL), _resident((D_IN, D_MODEL))],
        out_specs=[_rows(tb, 2 * D_KV), _rows(tb, D_MODEL)],
        out_shape=[jax.ShapeDtypeStruct((S, 2 * D_KV), BF16), jax.ShapeDtypeStruct((S, D_MODEL), F32)])


def _block_diag(w):
    eye = jnp.eye(RNN_BLOCKS, dtype=w.dtype)
    return (w[:, :, None, :] * eye[:, None, :, None]).reshape(D_RNN, D_RNN).astype(BF16)


def _adamw(w, g, m, v):
    m = ADAM_B1 * m + (1.0 - ADAM_B1) * g
    v = ADAM_B2 * v + (1.0 - ADAM_B2) * (g * g)
    m_hat = m / (1.0 - ADAM_B1 ** ADAM_STEP)
    v_hat = v / (1.0 - ADAM_B2 ** ADAM_STEP)
    delta = -ADAM_LR * (m_hat / (jnp.sqrt(v_hat) + ADAM_EPS) + ADAM_WD * w)
    return delta, m, v


def _sum_adamw(parts, w, m, v, name):
    parts = parts if isinstance(parts, (list, tuple)) else [parts]
    R, C = w.shape
    rb = R if R <= 256 else (256 if parts[0].shape[1] % 256 == 0 else 128)
    per = parts[0].shape[1] // rb
    assert R % rb == 0 and parts[0].shape[1] % rb == 0
    n = len(parts)

    def body(*refs):
        p_refs = refs[:n]
        w_ref, m_ref, v_ref, g_out, d_out, m_out, v_out = refs[n:]
        which = pl.program_id(0) // per

        def total(p_ref):
            g = p_ref[0].astype(F32)
            for d in range(1, N_DEV):
                g = g + p_ref[d].astype(F32)
            return g

        g = total(p_refs[0])
        for j in range(1, n):
            g = jnp.where(which == j, total(p_refs[j]), g)
        delta, mn, vn = _adamw(w_ref[...], g, m_ref[...], v_ref[...])
        g_out[...] = g
        d_out[...] = delta
        m_out[...] = mn
        v_out[...] = vn

    def part_spec(j):
        return pl.BlockSpec((N_DEV, rb, C), lambda i: (0, jnp.clip(i - j * per, 0, per - 1), 0))

    blk = _rows(rb, C)
    out = jax.ShapeDtypeStruct((R, C), F32)
    return pl.pallas_call(
        body, name=name, grid=(R // rb,),
        in_specs=[part_spec(j) for j in range(n)] + [blk, blk, blk],
        out_specs=[blk, blk, blk, blk], out_shape=[out, out, out, out],
        compiler_params=_params("parallel"),
    )(*parts, w, m, v)


_SMALL = [("attn_sinks", "s", 0, 1, None), ("rnn_conv_w", "r", 0, 4, "cols"), ("rnn_conv_b", "r", 4, 1, None),
          ("gate_a_w", "a", 0, D_RNN, None), ("gate_a_b", "r", 5, 1, None), ("gate_x_w", "x", 0, D_RNN, None),
          ("gate_x_b", "r", 6, 1, None), ("lru_lambda", "r", 7, 1, None), ("ln1_g", "d", 0, 1, None),
          ("ln1_b", "d", 1, 1, None), ("ffn_conv_w", "f", 0, 3, "cols"), ("ffn_conv_b", "f", 3, 1, None),
          ("ple_gate_b", "t", 2, 1, None), ("ln2_g", "t", 0, 1, None), ("ln2_b", "t", 1, 1, None)]
_LOSS_ROW = 3


_ACC_COLS = {"t": (0, D_MODEL), "f": (D_MODEL, D_FF), "d": (D_MODEL + D_FF, D_MODEL), "s": (2 * D_MODEL + D_FF, 128),
             "r": (2 * D_MODEL + D_FF + 128, D_RNN)}
_ACC_WIDTH = 2 * D_MODEL + D_FF + 128 + D_RNN


def _small_update(rows_all, gates_all, params):
    flat = [arr for triple in params for arr in triple]
    n_par = len(_SMALL)

    def body(*refs):
        rows_ref, gates_ref = refs[:2]
        p_refs = refs[2:2 + 3 * n_par]
        loss_ref = refs[2 + 3 * n_par]
        o_refs = refs[3 + 3 * n_par:3 + 7 * n_par]
        rows_s, tmp_r, tmp_f = refs[3 + 7 * n_par:]
        me = _dev_index(*_place())
        rows_sum, gates_sum = rows_ref[0], gates_ref[0]
        for d in range(1, N_DEV):
            rows_sum = rows_sum + rows_ref[d]
            gates_sum = gates_sum + gates_ref[d]
        rows_s[...] = rows_sum
        t0 = _ACC_COLS["t"][0]
        loss_ref[...] = rows_s[_LOSS_ROW:_LOSS_ROW + 1, t0:t0 + 128]
        for i, (name, key, row, rows, how) in enumerate(_SMALL):
            w_ref, m_ref, v_ref = p_refs[3 * i:3 * i + 3]
            g_out, d_out, m_out, v_out = o_refs[4 * i:4 * i + 4]
            if key == "a":
                g = gates_sum[:, :HEAD_DIM]
            elif key == "x":
                g = gates_sum[:, HEAD_DIM:]
            elif how == "cols":
                c0, width = _ACC_COLS[key]
                full = rows_s[:, c0:c0 + width]
                shard = width // N_DEV
                mine = full[:, :shard]
                for d in range(1, N_DEV):
                    mine = jnp.where(me == d, full[:, d * shard:(d + 1) * shard], mine)
                tmp = tmp_r if key == "r" else tmp_f
                tmp[...] = mine
                g = tmp[row:row + rows, :]
            else:
                c0, width = _ACC_COLS[key]
                g = rows_s[row:row + rows, c0:c0 + width][:, :w_ref.shape[1]]
            delta, mn, vn = _adamw(w_ref[...], g, m_ref[...], v_ref[...])
            g_out[...] = g
            d_out[...] = delta
            m_out[...] = mn
            v_out[...] = vn

    outs = [jax.ShapeDtypeStruct((1, 128), F32)]
    for w, _, _ in params:
        outs += [jax.ShapeDtypeStruct(w.shape, F32)] * 4
    scratch = [pltpu.VMEM((8, _ACC_WIDTH), F32), pltpu.VMEM((8, D_RNN // N_DEV), F32), pltpu.VMEM((8, D_FF // N_DEV), F32)]
    res = pl.pallas_call(body, name="small_update", out_shape=outs, scratch_shapes=scratch)(rows_all, gates_all, *flat)
    return res[0], [res[1 + 4 * i:5 + 4 * i] for i in range(n_par)]


def kernel(x, p, w_in, attn_sinks, rnn_conv_w, rnn_conv_b, gate_a_w, gate_a_b, gate_x_w, gate_x_b, lru_lambda, w_out, ln1_g, ln1_b, w_ffn_up, ffn_conv_w, ffn_conv_b, w_ffn_down, ple_gate_w, ple_gate_b, ple_proj, ln2_g, ln2_b, loss_target, m_w_in, m_attn_sinks, m_rnn_conv_w, m_rnn_conv_b, m_gate_a_w, m_gate_a_b, m_gate_x_w, m_gate_x_b, m_lru_lambda, m_w_out, m_ln1_g, m_ln1_b, m_w_ffn_up, m_ffn_conv_w, m_ffn_conv_b, m_w_ffn_down, m_ple_gate_w, m_ple_gate_b, m_ple_proj, m_ln2_g, m_ln2_b, v_w_in, v_attn_sinks, v_rnn_conv_w, v_rnn_conv_b, v_gate_a_w, v_gate_a_b, v_gate_x_w, v_gate_x_b, v_lru_lambda, v_w_out, v_ln1_g, v_ln1_b, v_w_ffn_up, v_ffn_conv_w, v_ffn_conv_b, v_w_ffn_down, v_ple_gate_w, v_ple_gate_b, v_ple_proj, v_ln2_g, v_ln2_b):
    from_col_blocks = lambda g: g.transpose(1, 0, 2).reshape(g.shape[1], N_DEV * g.shape[2])

    xs, ps, tgt, sinks = x[0], p[0, 0], loss_target[0], attn_sinks[0]
    wa, wx = _block_diag(gate_a_w[0]), _block_diag(gate_x_w[0])

    conv_cols = jnp.concatenate([rnn_conv_w[0].reshape(1, -1), ffn_conv_w[0].reshape(1, -1)], axis=1)
    n_rc, n_fc = 4 * D_RNN // N_DEV, 3 * D_FF // N_DEV
    ((g_in,),) = _comm_call([_Gather([w_in[0].T.astype(BF16)])], "gather_w_in")
    w_in_full = g_in.reshape(D_IN, D_MODEL)

    (q, k, v, xr, gr), _ = _in_proj(xs, w_in_full)
    w_up_shard = w_ffn_up[0].astype(BF16)
    (att, lse), (g_out, w_up_top, g_conv) = _attn_fwd(
        q, k, v, sinks,
        comm=_Multi([_Gather([w_out[0].astype(BF16), w_up_shard[:D_MODEL // 2]]),
                     _Bcast([jnp.broadcast_to(conv_cols, (8, n_rc + n_fc))])]))
    rcw = from_col_blocks(g_conv[:, 0, :n_rc].reshape(N_DEV, 4, D_RNN // N_DEV))
    fcw = from_col_blocks(g_conv[:, 0, n_rc:].reshape(N_DEV, 3, D_FF // N_DEV))
    (rec, h, *kept), (w_up_bot,) = _rnn_fwd(xr, gr, rcw, rnn_conv_b, wa, wx, gate_a_b, gate_x_b, lru_lambda,
                                            comm=_Gather([w_up_shard[D_MODEL // 2:]]))
    w_out_full = g_out.reshape(D_MODEL, D_MODEL)
    (z1, h1b, gate, act, gl, vdgl), (g_down, g_pg, g_pp) = _mix_ln1_up(
        xs, att, rec, w_out_full, ln1_g, ln1_b, w_up_top, w_up_bot, fcw, ffn_conv_b,
        comm=_Gather([w_ffn_down[0].astype(BF16), ple_gate_w[0].astype(BF16), ple_proj[0].astype(BF16)]))
    dz2b, dpreb, dppb, dgc, dval, dh1p, acc_t = _tail(
        act, gl, vdgl, z1, h1b, ps, tgt, g_down.reshape(D_FF, D_MODEL), g_pg.reshape(D_MODEL, D_MODEL), ple_gate_b,
        from_col_blocks(g_pp), ln1_g, ln1_b, ln2_g, ln2_b)

    gd_down = _weight_grad([dz2b], [act], "down_grad", "rows_t", ts=1024)
    gd_pg = _weight_grad([h1b], [dpreb], "pg_grad", "rows", ts=1024)
    gd_pp = _weight_grad([ps], [dppb], "pp_grad", "cols", ts=1024)
    (dgate, dz1, dz1b, datt, drec, acc_f, acc_d), (r_down, r_pg, r_pp) = _up_bwd(
        dgc, gate, dval, dh1p, z1, w_up_top, w_up_bot, fcw, w_out_full, ln1_g, comm=_Exchange([gd_down, gd_pg, gd_pp]))
    gd_up_top, gd_up_bot = _weight_grad([h1b], [dgate, dval], "up_grad", "cols", halves=True)
    gd_out = _weight_grad([att, rec], [dz1b], "out_grad", "rows", ts=1024)
    (dq, dkc, dkp, dvc, dvp, acc_s), (r_up_top,) = _attn_bwd(q, k, v, lse, datt, sinks, comm=_Exchange([gd_up_top]))
    early = jnp.concatenate([acc_t, acc_f, acc_d], axis=1)
    (dxr, dgr, g_wa, g_wx, acc_r), (r_up_bot, r_out, early_all) = _rnn_bwd(
        xr, gr, h, kept, drec, rcw, wa, wx, lru_lambda, comm=_Multi([_Exchange([gd_up_bot, gd_out]), _Bcast([early])]))
    (dkv, dx), _ = _in_bwd(dq, dkc, dkp, dvc, dvp, dxr, dgr, dz1, w_in_full)
    du_parts = [dq, dkv, dxr, dgr]
    lanes = D_RNN // 128
    late = jnp.concatenate([g_wa, g_wx], axis=1)
    late = jnp.concatenate([late, acc_s, acc_r.reshape(8, lanes, 128).transpose(1, 0, 2).reshape(8 * lanes, 128)], axis=0)
    width = D_MODEL // IN_GRAD_PARTS
    comm, r_parts = _Bcast([late]), []
    for part in range(IN_GRAD_PARTS):
        gd_part, got = _weight_grad(du_parts, [xs], f"in_grad_{part}", "rows", ts=1024, b_window=(part, width), comm=comm)
        if part == 0:
            (late_all,) = got
        else:
            r_parts += got
        comm = _Exchange([gd_part])
    r_parts += _comm_call([comm], "exchange_w_in")[0]
    r_in = jnp.concatenate(r_parts, axis=2)
    acc_r_all = late_all[:, D_RNN + 8:].reshape(N_DEV, lanes, 8, 128).transpose(0, 2, 1, 3).reshape(N_DEV, 8, D_RNN)
    small_parts = (jnp.concatenate([early_all, late_all[:, D_RNN:D_RNN + 8], acc_r_all], axis=2),
                   late_all[:, :D_RNN])

    outs = {}
    res = _sum_adamw(r_in, w_in[0].T, m_w_in[0].T, v_w_in[0].T, "adamw_w_in")
    outs["w_in"] = [r.T[None] for r in res]
    for name, parts, w, m, v in [("w_out", r_out, w_out, m_w_out, v_w_out),
                                 ("w_ffn_up", [r_up_top, r_up_bot], w_ffn_up, m_w_ffn_up, v_w_ffn_up),
                                 ("w_ffn_down", r_down, w_ffn_down, m_w_ffn_down, v_w_ffn_down),
                                 ("ple_gate_w", r_pg, ple_gate_w, m_ple_gate_w, v_ple_gate_w),
                                 ("ple_proj", r_pp, ple_proj, m_ple_proj, v_ple_proj)]:
        res = _sum_adamw(parts, w[0], m[0], v[0], "adamw_" + name)
        outs[name] = [r[None] for r in res]

    given = dict(attn_sinks=(attn_sinks, m_attn_sinks, v_attn_sinks), rnn_conv_w=(rnn_conv_w, m_rnn_conv_w, v_rnn_conv_w),
                 rnn_conv_b=(rnn_conv_b, m_rnn_conv_b, v_rnn_conv_b), gate_a_w=(gate_a_w, m_gate_a_w, v_gate_a_w),
                 gate_a_b=(gate_a_b, m_gate_a_b, v_gate_a_b), gate_x_w=(gate_x_w, m_gate_x_w, v_gate_x_w),
                 gate_x_b=(gate_x_b, m_gate_x_b, v_gate_x_b), lru_lambda=(lru_lambda, m_lru_lambda, v_lru_lambda),
                 ln1_g=(ln1_g, m_ln1_g, v_ln1_g), ln1_b=(ln1_b, m_ln1_b, v_ln1_b),
                 ffn_conv_w=(ffn_conv_w, m_ffn_conv_w, v_ffn_conv_w), ffn_conv_b=(ffn_conv_b, m_ffn_conv_b, v_ffn_conv_b),
                 ple_gate_b=(ple_gate_b, m_ple_gate_b, v_ple_gate_b), ln2_g=(ln2_g, m_ln2_g, v_ln2_g),
                 ln2_b=(ln2_b, m_ln2_b, v_ln2_b))
    as_2d = lambda a: a.reshape(-1, a.shape[-1])
    loss_row, small_res = _small_update(*small_parts, [tuple(as_2d(a) for a in given[n]) for n, *_ in _SMALL])
    loss = loss_row[0, 0]
    for (n, *_), res in zip(_SMALL, small_res):
        outs[n] = [r.reshape(given[n][0].shape) for r in res]

    order = ["w_in", "attn_sinks", "rnn_conv_w", "rnn_conv_b", "gate_a_w", "gate_a_b", "gate_x_w", "gate_x_b",
             "lru_lambda", "w_out", "ln1_g", "ln1_b", "w_ffn_up", "ffn_conv_w", "ffn_conv_b", "w_ffn_down",
             "ple_gate_w", "ple_gate_b", "ple_proj", "ln2_g", "ln2_b"]
    return (loss, dx[None], *[outs[n][0] for n in order], *[outs[n][1] for n in order],
            *[outs[n][2] for n in order], *[outs[n][3] for n in order])
```

```python
import jax
import jax.numpy as jnp
from jax import lax
from jax.experimental import pallas as pl
from jax.experimental.pallas import tpu as pltpu

F32 = jnp.float32
BF16 = jnp.bfloat16

D_MODEL = 1024
D_ATT = 512
D_KV = 128
HEAD_DIM = 64
N_HEADS = 8
N_KV = 2
D_RNN = 512
RNN_BLOCKS = 8
D_IN = 1792
D_FF = 3072
PLE_DIM = 256
QBLK = 128
N_DEV = 8
ALPHA = float(2 ** 0.25)
LN_EPS = 1e-5
LRU_C = 8.0
ADAM_LR, ADAM_B1, ADAM_B2, ADAM_EPS, ADAM_WD, ADAM_STEP = 0.001, 0.9, 0.999, 1e-08, 0.01, 10

V7X_VMEM_LIMIT = 56 * 1024 * 1024
MESH = pl.DeviceIdType.MESH


def _params(*sem, vmem=V7X_VMEM_LIMIT):
    return pltpu.CompilerParams(dimension_semantics=sem or None, vmem_limit_bytes=vmem)


def _resident(shape):
    return pl.BlockSpec(shape, lambda *_: (0,) * len(shape), pipeline_mode=pl.Buffered(1))


def _rows(tb, cols):
    return pl.BlockSpec((tb, cols), lambda i: (i, 0))


def _acc(shape):
    return pl.BlockSpec(shape, lambda *_: (0,) * len(shape))


def _dot(a, b):
    return jnp.dot(a, b, preferred_element_type=F32)


def _dot_nt(a, b):
    return lax.dot_general(a, b, (((1,), (1,)), ((), ())), preferred_element_type=F32)


def _dot_tn(a, b):
    return lax.dot_general(a, b, (((0,), (0,)), ((), ())), preferred_element_type=F32)


def _sigmoid(x):
    return 1.0 / (1.0 + jnp.exp(-x))


_GELU_C = 0.7978845608028654
_GELU_K = 0.044715


def _gelu_and_grad(x):
    u = x * x
    t = jnp.tanh(x * (_GELU_C + (_GELU_C * _GELU_K) * u))
    hp = 0.5 + 0.5 * t
    dg = hp + x * (0.5 - 0.5 * (t * t)) * (_GELU_C + (3.0 * _GELU_C * _GELU_K) * u)
    return x * hp, dg


def _gelu(x):
    return 0.5 * x * (1.0 + jnp.tanh(_GELU_C * (x + _GELU_K * x * x * x)))


def _ln_stats(z):
    mu = jnp.mean(z, axis=-1, keepdims=True)
    zc = z - mu
    var = jnp.mean(zc * zc, axis=-1, keepdims=True)
    rstd = lax.rsqrt(var + LN_EPS)
    return zc * rstd, rstd


def _ln_bwd(dy, xhat, rstd, g):
    dxh = dy * g
    m1 = jnp.mean(dxh, axis=-1, keepdims=True)
    m2 = jnp.mean(dxh * xhat, axis=-1, keepdims=True)
    return rstd * (dxh - m1 - xhat * m2)


def _softplus_neg(lam):
    u = jnp.exp(-jnp.abs(lam))
    w = 1.0 + u
    d = w - 1.0
    log1p_u = jnp.where(d == 0.0, u, jnp.log(w) * (u / jnp.where(d == 0.0, 1.0, d)))
    return jnp.maximum(-lam, 0.0) + log1p_u


def _shift_down(x, halo, s):
    xs = pltpu.roll(x, s, 0)
    hs = pltpu.roll(halo, s, 0)
    row8 = lax.broadcasted_iota(jnp.int32, hs.shape, 0)
    first = jnp.where(row8 < s, hs, xs[:8])
    return jnp.concatenate([first, xs[8:]], axis=0)


def _shift_up(x, halo, s):
    n = x.shape[0]
    xs = pltpu.roll(x, n - s, 0)
    hs = pltpu.roll(halo, 8 - s, 0)
    row8 = lax.broadcasted_iota(jnp.int32, hs.shape, 0)
    last = jnp.where(row8 >= 8 - s, hs, xs[n - 8:])
    return jnp.concatenate([xs[:n - 8], last], axis=0)


def _row_sum(x):
    return jnp.sum(x, axis=0, keepdims=True)


def _put_rows(acc_ref, rows):
    row8 = lax.broadcasted_iota(jnp.int32, acc_ref.shape, 0)
    upd = jnp.zeros(acc_ref.shape, F32)
    for r, vec in enumerate(rows):
        upd = jnp.where(row8 == r, vec, upd)
    acc_ref[...] += upd


def _place():
    return lax.axis_index("x"), lax.axis_index("y"), lax.axis_index("c")


def _dev_index(px, py, pc):
    return 4 * px + 2 * py + pc


_ANY = pl.BlockSpec(memory_space=pl.ANY)


class _Gather:
    def __init__(self, arrays):
        self.arrays = list(arrays)
        self.n = len(self.arrays)

    def out_shape(self):
        return [jax.ShapeDtypeStruct((N_DEV,) + s.shape, s.dtype) for s in self.arrays]

    def scratch(self):
        return [pltpu.SemaphoreType.DMA((self.n, 7)), pltpu.SemaphoreType.DMA((self.n, 7)),
                pltpu.SemaphoreType.DMA((self.n,))]

    def _parts(self, ins, outs, sems):
        send_sems, recv_sems, local_sems = sems
        x, y, c = _place()
        me, sibling = (x, y, c), (x, y, 1 - c)
        chips = [(1 - x, y), (x, 1 - y), (1 - x, 1 - y)]

        def copy(a, k, block, to, src=None):
            rows = outs[a].at[_dev_index(*block)]
            return pltpu.make_async_remote_copy(
                src_ref=rows if src is None else src, dst_ref=rows, send_sem=send_sems.at[a, k],
                recv_sem=recv_sems.at[a, k], device_id=to, device_id_type=MESH)

        rng = range(self.n)
        mine = [pltpu.make_async_copy(ins[a], outs[a].at[_dev_index(*me)], local_sems.at[a]) for a in rng]
        first = [copy(a, 0, me, sibling, src=ins[a]) for a in rng]
        first += [copy(a, 1 + j, me, (*chip, c), src=ins[a]) for j, chip in enumerate(chips) for a in rng]
        landed = [copy(a, 1 + j, (*chip, c), me) for j, chip in enumerate(chips) for a in rng]
        passed = [copy(a, 4 + j, (*chip, c), sibling) for j, chip in enumerate(chips) for a in rng]
        from_sibling = [copy(a, 0, sibling, me) for a in rng]
        from_sibling += [copy(a, 4 + j, (*chip, 1 - c), me) for j, chip in enumerate(chips) for a in rng]
        return mine, first, landed, passed, from_sibling

    def start(self, ins, outs, sems):
        mine, first, _, _, _ = self._parts(ins, outs, sems)
        for cp in mine + first:
            cp.start()

    def forward(self, ins, outs, sems):
        _, _, landed, passed, _ = self._parts(ins, outs, sems)
        for got, fwd in zip(landed, passed):
            got.wait_recv()
            fwd.start()

    def finish(self, ins, outs, sems):
        mine, first, _, passed, from_sibling = self._parts(ins, outs, sems)
        for cp in from_sibling:
            cp.wait_recv()
        for cp in first + passed:
            cp.wait_send()
        for cp in mine:
            cp.wait()

    def before(self, ins, outs, sems, step, nsteps):
        pl.when(step == 0)(lambda: self.start(ins, outs, sems))
        pl.when(step == (7 * nsteps) // 8)(lambda: self.forward(ins, outs, sems))

    def after(self, ins, outs, sems, step, nsteps):
        pl.when(step == nsteps - 1)(lambda: self.finish(ins, outs, sems))


class _Exchange:
    def __init__(self, arrays):
        self.arrays = list(arrays)
        self.n = len(self.arrays)

    def out_shape(self):
        return [jax.ShapeDtypeStruct(b.shape, b.dtype) for b in self.arrays]

    def scratch(self):
        return [pltpu.SemaphoreType.DMA((self.n, 7)), pltpu.SemaphoreType.DMA((self.n, 7)),
                pltpu.SemaphoreType.DMA((self.n,))]

    def _parts(self, ins, outs, sems):
        send_sems, recv_sems, local_sems = sems
        x, y, c = _place()
        me = _dev_index(x, y, c)
        peers = [(x ^ (k >> 2), y ^ ((k >> 1) & 1), c ^ (k & 1)) for k in range(1, N_DEV)]
        rng = range(self.n)
        mine = [pltpu.make_async_copy(ins[a].at[me], outs[a].at[me], local_sems.at[a]) for a in rng]
        sent = [pltpu.make_async_remote_copy(
            src_ref=ins[a].at[_dev_index(*to)], dst_ref=outs[a].at[me], send_sem=send_sems.at[a, k],
            recv_sem=recv_sems.at[a, k], device_id=to, device_id_type=MESH) for k, to in enumerate(peers) for a in rng]
        arrivals = [pltpu.make_async_remote_copy(
            src_ref=ins[a].at[me], dst_ref=outs[a].at[_dev_index(*frm)], send_sem=send_sems.at[a, k],
            recv_sem=recv_sems.at[a, k], device_id=frm, device_id_type=MESH) for k, frm in enumerate(peers) for a in rng]
        return mine, sent, arrivals

    def start(self, ins, outs, sems):
        mine, sent, _ = self._parts(ins, outs, sems)
        for cp in mine + sent:
            cp.start()

    def finish(self, ins, outs, sems):
        mine, sent, arrivals = self._parts(ins, outs, sems)
        for cp in arrivals:
            cp.wait_recv()
        for cp in sent:
            cp.wait_send()
        for cp in mine:
            cp.wait()

    def before(self, ins, outs, sems, step, nsteps):
        pl.when(step == 0)(lambda: self.start(ins, outs, sems))

    def after(self, ins, outs, sems, step, nsteps):
        pl.when(step == nsteps - 1)(lambda: self.finish(ins, outs, sems))


class _Bcast(_Exchange):
    def out_shape(self):
        return [jax.ShapeDtypeStruct((N_DEV,) + s.shape, s.dtype) for s in self.arrays]

    def _parts(self, ins, outs, sems):
        send_sems, recv_sems, local_sems = sems
        x, y, c = _place()
        me = _dev_index(x, y, c)
        peers = [(x ^ (k >> 2), y ^ ((k >> 1) & 1), c ^ (k & 1)) for k in range(1, N_DEV)]
        rng = range(self.n)
        mine = [pltpu.make_async_copy(ins[a], outs[a].at[me], local_sems.at[a]) for a in rng]
        sent = [pltpu.make_async_remote_copy(
            src_ref=ins[a], dst_ref=outs[a].at[me], send_sem=send_sems.at[a, k], recv_sem=recv_sems.at[a, k],
            device_id=to, device_id_type=MESH) for k, to in enumerate(peers) for a in rng]
        arrivals = [pltpu.make_async_remote_copy(
            src_ref=ins[a], dst_ref=outs[a].at[_dev_index(*frm)], send_sem=send_sems.at[a, k],
            recv_sem=recv_sems.at[a, k], device_id=frm, device_id_type=MESH) for k, frm in enumerate(peers) for a in rng]
        return mine, sent, arrivals


class _Multi:
    def __init__(self, comms):
        self.comms = list(comms)
        self.arrays = [arr for c in self.comms for arr in c.arrays]
        self.n = len(self.arrays)

    def out_shape(self):
        return [s for c in self.comms for s in c.out_shape()]

    def scratch(self):
        return [s for c in self.comms for s in c.scratch()]

    def _each(self, ins, outs, sems):
        a = 0
        for j, c in enumerate(self.comms):
            yield c, ins[a:a + c.n], outs[a:a + c.n], sems[3 * j:3 * j + 3]
            a += c.n

    def before(self, ins, outs, sems, step, nsteps):
        for c, ci, co, cs in self._each(ins, outs, sems):
            c.before(ci, co, cs, step, nsteps)

    def after(self, ins, outs, sems, step, nsteps):
        for c, ci, co, cs in self._each(ins, outs, sems):
            c.after(ci, co, cs, step, nsteps)


def _comm_call(comms, name):
    ns = [c.n for c in comms]
    n = sum(ns)

    def body(*refs):
        parts, a, s = [], 0, 2 * n
        for c in comms:
            parts.append((c, refs[a:a + c.n], refs[n + a:n + a + c.n], refs[s:s + 3]))
            a, s = a + c.n, s + 3
        for c, ins, outs, sems in parts:
            c.start(ins, outs, sems)
        for c, ins, outs, sems in parts:
            if isinstance(c, _Gather):
                c.forward(ins, outs, sems)
        for c, ins, outs, sems in parts:
            c.finish(ins, outs, sems)

    res = pl.pallas_call(
        body, name=name, in_specs=[_ANY] * n, out_specs=[_ANY] * n,
        out_shape=[s for c in comms for s in c.out_shape()], scratch_shapes=[s for c in comms for s in c.scratch()],
    )(*[arr for c in comms for arr in c.arrays])
    out, a = [], 0
    for k in ns:
        out.append(res[a:a + k])
        a += k
    return out


def _pcall(body, args, *, name, grid, in_specs, out_specs, out_shape, scratch_shapes=(), sem="parallel", comm=None,
           step_axis=0):
    sem = (sem,) * len(grid) if isinstance(sem, str) else sem
    if comm is None:
        res = pl.pallas_call(body, name=name, grid=grid, in_specs=in_specs, out_specs=out_specs, out_shape=out_shape,
                             scratch_shapes=list(scratch_shapes), compiler_params=_params(*sem))(*args)
        return res, []
    n_in, n_out, n_scr, n = len(in_specs), len(out_specs), len(scratch_shapes), comm.n
    nsteps = grid[step_axis]
    assert all(g == 1 for ax, g in enumerate(grid) if ax != step_axis)

    def hosted(*refs):
        ins, cin = refs[:n_in], refs[n_in:n_in + n]
        o0 = n_in + n
        outs, cout = refs[o0:o0 + n_out], refs[o0 + n_out:o0 + n_out + n]
        s0 = o0 + n_out + n
        scr, sems = refs[s0:s0 + n_scr], refs[s0 + n_scr:]
        step = pl.program_id(step_axis)
        comm.before(cin, cout, sems, step, nsteps)
        body(*ins, *outs, *scr)
        comm.after(cin, cout, sems, step, nsteps)

    res = pl.pallas_call(
        hosted, name=name, grid=grid, in_specs=list(in_specs) + [_ANY] * n, out_specs=list(out_specs) + [_ANY] * n,
        out_shape=list(out_shape) + comm.out_shape(), scratch_shapes=list(scratch_shapes) + comm.scratch(),
        compiler_params=_params(*(("arbitrary",) * len(grid))))(*args, *comm.arrays)
    return res[:n_out], res[n_out:]


def _load_row_halves(top_hbm, bot_hbm, full_s, sems):
    r = top_hbm.shape[1]
    copies = [pltpu.make_async_copy(top_hbm, full_s.at[:, :r, :], sems.at[0]),
              pltpu.make_async_copy(bot_hbm, full_s.at[:, r:, :], sems.at[1])]
    for cp in copies:
        cp.start()
    for cp in copies:
        cp.wait()


def _in_proj(x, w_in_t, comm=None):
    S = x.shape[0]
    tb = min(1024, S)

    def body(x_ref, w_ref, q_ref, k_ref, v_ref, xr_ref, gr_ref):
        u = _dot_nt(x_ref[...].astype(BF16), w_ref[...])
        q_ref[...] = (u[:, :D_ATT] * (HEAD_DIM ** -0.5)).astype(BF16)
        k_ref[...] = u[:, D_ATT:D_ATT + D_KV].astype(BF16)
        v_ref[...] = u[:, D_ATT + D_KV:D_ATT + 2 * D_KV].astype(BF16)
        xr_ref[...] = u[:, D_ATT + 2 * D_KV:D_ATT + 2 * D_KV + D_RNN]
        gr_ref[...] = u[:, D_ATT + 2 * D_KV + D_RNN:]

    return _pcall(
        body, (x, w_in_t), name="in_proj", grid=(S // tb,), comm=comm,
        in_specs=[_rows(tb, D_MODEL), _resident((D_IN, D_MODEL))],
        out_specs=[_rows(tb, D_ATT), _rows(tb, D_KV), _rows(tb, D_KV), _rows(tb, D_RNN), _rows(tb, D_RNN)],
        out_shape=[jax.ShapeDtypeStruct((S, D_ATT), BF16), jax.ShapeDtypeStruct((S, D_KV), BF16),
                   jax.ShapeDtypeStruct((S, D_KV), BF16), jax.ShapeDtypeStruct((S, D_RNN), F32),
                   jax.ShapeDtypeStruct((S, D_RNN), F32)])


GROUP = N_HEADS // N_KV


def _band_mask(i):
    qi = lax.broadcasted_iota(jnp.int32, (GROUP * QBLK, 2 * QBLK), 0) & (QBLK - 1)
    sj = lax.broadcasted_iota(jnp.int32, (GROUP * QBLK, 2 * QBLK), 1)
    return (sj > qi) & (sj <= qi + QBLK) & ((sj >= QBLK) | (i > 0))


def _stack_heads(x, g):
    return jnp.concatenate([x[:, (g * GROUP + hh) * HEAD_DIM:(g * GROUP + hh + 1) * HEAD_DIM] for hh in range(GROUP)],
                           axis=0)


def _unstack_heads(x4):
    return [x4[hh * QBLK:(hh + 1) * QBLK] for hh in range(GROUP)]


def _sink_column(sink_ref, g):
    head = lax.broadcasted_iota(jnp.int32, (GROUP * QBLK, 1), 0) // QBLK
    col = jnp.full((GROUP * QBLK, 1), sink_ref[g * GROUP], F32)
    for hh in range(1, GROUP):
        col = jnp.where(head == hh, sink_ref[g * GROUP + hh], col)
    return col


ATT_STEP = 4
IN_GRAD_PARTS = 2


def _attn_specs(nq=1):
    cur = lambda i: (i, 0)
    prev = lambda i: (jnp.maximum(nq * i - 1, 0), 0)
    return [pl.BlockSpec((nq * QBLK, D_KV), cur), pl.BlockSpec((QBLK, D_KV), prev),
            pl.BlockSpec((nq * QBLK, D_KV), cur), pl.BlockSpec((QBLK, D_KV), prev)]


def _attn_fwd(q, k, v, sinks, comm=None):
    S = q.shape[0]
    nq = min(ATT_STEP, S // QBLK)

    def body(sink_ref, q_ref, kc_ref, kp_ref, vc_ref, vp_ref, o_ref, lse_ref):
        first = pl.program_id(0) * nq
        kall = jnp.concatenate([kp_ref[...], kc_ref[...]], axis=0)
        vall = jnp.concatenate([vp_ref[...], vc_ref[...]], axis=0)
        for b in range(nq):
            valid = _band_mask(first + b)
            rows = slice(b * QBLK, (b + 1) * QBLK)
            keys = slice(b * QBLK, (b + 2) * QBLK)
            qv = q_ref[rows, :]
            outs = []
            for g in range(N_KV):
                kcat = kall[keys, g * HEAD_DIM:(g + 1) * HEAD_DIM]
                vcat = vall[keys, g * HEAD_DIM:(g + 1) * HEAD_DIM]
                s = jnp.where(valid, _dot_nt(_stack_heads(qv, g), kcat), -1e30)
                sink = _sink_column(sink_ref, g)
                m = jnp.maximum(jnp.max(s, axis=1, keepdims=True), sink)
                p = jnp.exp(s - m)
                l = jnp.sum(p, axis=1, keepdims=True) + jnp.exp(sink - m)
                outs += _unstack_heads(_dot(p.astype(BF16), vcat) / l)
                lse_ref[(b * N_KV + g) * GROUP * QBLK:(b * N_KV + g + 1) * GROUP * QBLK, :] = m + jnp.log(l)
            o_ref[rows, :] = jnp.concatenate(outs, axis=1).astype(BF16)

    lse_rows = nq * N_HEADS * QBLK
    return _pcall(
        body, (sinks, q, k, k, v, v), name="attn_fwd", grid=(S // (nq * QBLK),), comm=comm,
        in_specs=[pl.BlockSpec(memory_space=pltpu.SMEM), _rows(nq * QBLK, D_ATT)] + _attn_specs(nq),
        out_specs=[_rows(nq * QBLK, D_ATT), _rows(lse_rows, 1)],
        out_shape=[jax.ShapeDtypeStruct((S, D_ATT), BF16), jax.ShapeDtypeStruct((S * N_HEADS, 1), F32)])


def _w_rows(w_ref):
    return [w_ref[k:k + 1, :] for k in range(w_ref.shape[0])]


def _conv4(x, halo, w, b):
    y = b + w[3] * x
    for s in (1, 2, 3):
        y = y + w[3 - s] * _shift_down(x, halo, s)
    return y


def _rnn_gates(xc, wa, wx, ba, bx, sp):
    xcb = xc.astype(BF16)
    r = _sigmoid(_dot(xcb, wa) + ba)
    ig = _sigmoid(_dot(xcb, wx) + bx)
    la = -LRU_C * r * sp
    a = jnp.exp(la)
    t = jnp.tanh(la)
    f = jnp.sqrt(-2.0 * t / (1.0 - t))
    return r, ig, a, f


def _rnn_fwd(xr, gr, conv_w, conv_b, wa, wx, ba, bx, lam, comm=None):
    S = xr.shape[0]
    tb = min(512, S)

    def body(xr_ref, gr_ref, cw_ref, cb_ref, wa_ref, wx_ref, ba_ref, bx_ref, lam_ref, rec_ref, h_ref,
             xc_ref, r_ref, ig_ref, a_ref, f_ref, halo_s, hc_s, a_s, b_s):
        @pl.when(pl.program_id(0) == 0)
        def _():
            halo_s[...] = jnp.zeros_like(halo_s)
            hc_s[...] = jnp.zeros_like(hc_s)

        x = xr_ref[...]
        xc = _conv4(x, halo_s[...], _w_rows(cw_ref), cb_ref[...])
        halo_s[...] = x[tb - 8:]
        r, ig, a, f = _rnn_gates(xc, wa_ref[...], wx_ref[...], ba_ref[...], bx_ref[...], _softplus_neg(lam_ref[...]))
        xc_ref[...] = xc
        r_ref[...] = r
        ig_ref[...] = ig
        a_ref[...] = a
        f_ref[...] = f
        a_s[...] = a
        b_s[...] = f * ig * xc
        row8 = lax.broadcasted_iota(jnp.int32, (8, D_RNN), 0)

        def tile(t, hc):
            o = pl.multiple_of(t * 8, 8)
            at = a_s[pl.ds(o, 8), :]
            bt = b_s[pl.ds(o, 8), :]
            for s in (1, 2, 4):
                keep = row8 >= s
                a_sh = jnp.where(keep, pltpu.roll(at, s, 0), 1.0)
                b_sh = jnp.where(keep, pltpu.roll(bt, s, 0), 0.0)
                bt = at * b_sh + bt
                at = at * a_sh
            ht = at * hc + bt
            b_s[pl.ds(o, 8), :] = ht
            return _row_sum(jnp.where(row8 == 7, ht, 0.0))

        hc_s[0:1, :] = lax.fori_loop(0, tb // 8, tile, hc_s[0:1, :], unroll=2)
        h = b_s[...]
        h_ref[...] = h
        rec_ref[...] = (h * _gelu(gr_ref[...])).astype(BF16)

    vec = _resident((1, D_RNN))
    kept = jax.ShapeDtypeStruct((S, D_RNN), F32)
    return _pcall(
        body, (xr, gr, conv_w, conv_b, wa, wx, ba, bx, lam), name="rnn_fwd", grid=(S // tb,), sem="arbitrary", comm=comm,
        in_specs=[_rows(tb, D_RNN), _rows(tb, D_RNN), _resident((4, D_RNN)), vec,
                  _resident((D_RNN, D_RNN)), _resident((D_RNN, D_RNN)), vec, vec, vec],
        out_specs=[_rows(tb, D_RNN)] * 7,
        out_shape=[jax.ShapeDtypeStruct((S, D_RNN), BF16), kept, kept, kept, kept, kept, kept],
        scratch_shapes=[pltpu.VMEM((8, D_RNN), F32), pltpu.VMEM((8, D_RNN), F32),
                        pltpu.VMEM((tb, D_RNN), F32), pltpu.VMEM((tb, D_RNN), F32)])


def _mix_ln1_up(x, att, rec, w_out, ln1_g, ln1_b, w_up_top, w_up_bot, fcw, fcb, comm=None):
    S = x.shape[0]
    tb = min(256, S)
    nblk, kh, wblk = w_up_top.shape
    half = nblk // 2

    def body(x_ref, att_ref, rec_ref, wo_ref, g_ref, b_ref, wt_hbm, wb_hbm, fcw_ref, fcb_ref,
             z1_ref, h1b_ref, gate_ref, act_ref, gl_ref, vdgl_ref, halo_s, wu_s, wu_sems):
        @pl.when(pl.program_id(0) == 0)
        def _():
            halo_s[...] = jnp.zeros_like(halo_s)
            _load_row_halves(wt_hbm, wb_hbm, wu_s, wu_sems)

        z1 = ALPHA * x_ref[...] + _dot(att_ref[...], wo_ref[:D_ATT, :]) + _dot(rec_ref[...], wo_ref[D_ATT:, :])
        z1_ref[...] = z1
        xhat, _ = _ln_stats(z1)
        h1b = (xhat * g_ref[...] + b_ref[...]).astype(BF16)
        h1b_ref[...] = h1b
        for jj in range(half):
            cols = slice(jj * wblk, (jj + 1) * wblk)
            gate = _dot(h1b, wu_s[jj])
            val = _dot(h1b, wu_s[jj + half])
            halo = halo_s[:, cols]
            conv = (fcb_ref[:, cols] + fcw_ref[2:3, cols] * gate + fcw_ref[1:2, cols] * _shift_down(gate, halo, 1)
                    + fcw_ref[0:1, cols] * _shift_down(gate, halo, 2))
            halo_s[:, cols] = gate[tb - 8:]
            gl, dgl = _gelu_and_grad(conv)
            gate_ref[:, cols] = gate.astype(BF16)
            act_ref[:, cols] = (gl * val).astype(BF16)
            gl_ref[:, cols] = gl.astype(BF16)
            vdgl_ref[:, cols] = (val * dgl).astype(BF16)

    vec = _resident((1, D_MODEL))
    wide = jax.ShapeDtypeStruct((S, D_FF), BF16)
    return _pcall(
        body, (x, att, rec, w_out, ln1_g, ln1_b, w_up_top, w_up_bot, fcw, fcb), name="mix_ln1_up", grid=(S // tb,),
        sem="arbitrary", comm=comm,
        in_specs=[_rows(tb, D_MODEL), _rows(tb, D_ATT), _rows(tb, D_RNN), _resident((D_MODEL, D_MODEL)), vec, vec,
                  _ANY, _ANY, _resident((3, D_FF)), _resident((1, D_FF))],
        out_specs=[_rows(tb, D_MODEL), _rows(tb, D_MODEL)] + [_rows(tb, D_FF)] * 4,
        out_shape=[jax.ShapeDtypeStruct((S, D_MODEL), F32), jax.ShapeDtypeStruct((S, D_MODEL), BF16), wide, wide, wide, wide],
        scratch_shapes=[pltpu.VMEM((8, D_FF), F32), pltpu.VMEM((nblk, 2 * kh, wblk), BF16),
                        pltpu.SemaphoreType.DMA((2,))])


def _tail(act, gl, vdgl, z1, h1b, p, tgt, w_down, w_pg, b_pg, w_pp, ln1_g, ln1_b, ln2_g, ln2_b):
    S = z1.shape[0]
    tb = min(256, S)

    def body(act_ref, gl_ref, vdgl_ref, z1_ref, h1b_ref, p_ref, t_ref, wd_ref, wpg_ref, bpg_ref, wpp_ref,
             g1_ref, b1_ref, g2_ref, b2_ref, dz2_ref, dpre_ref, dpp_ref, dgc_ref, dval_ref, dh1_ref, acc_ref):
        i = pl.program_id(0)

        @pl.when(i == 0)
        def _():
            acc_ref[...] = jnp.zeros_like(acc_ref)

        ffn = _dot(act_ref[...], wd_ref[...])
        xhat1, _ = _ln_stats(z1_ref[...])
        h1 = xhat1 * g1_ref[...] + b1_ref[...]
        sg = _sigmoid(_dot(h1b_ref[...], wpg_ref[...]) + bpg_ref[...])
        pp = _dot(p_ref[...].astype(BF16), wpp_ref[...])
        z2 = ALPHA * h1 + ffn + sg * pp
        xhat2, rstd2 = _ln_stats(z2)
        y = xhat2 * g2_ref[...] + b2_ref[...]
        err = y - t_ref[...]
        dy = err * (1.0 / D_MODEL)
        loss = 0.5 * jnp.sum(jnp.sum(err * err, axis=1, keepdims=True), axis=0, keepdims=True) * (1.0 / D_MODEL)
        dz2 = _ln_bwd(dy, xhat2, rstd2, g2_ref[...])
        dz2b = dz2.astype(BF16)
        dz2_ref[...] = dz2b
        dpre = dz2 * pp * sg * (1.0 - sg)
        dpreb = dpre.astype(BF16)
        dpre_ref[...] = dpreb
        dpp_ref[...] = (dz2 * sg).astype(BF16)
        dh1_ref[...] = ALPHA * dz2 + _dot_nt(dpreb, wpg_ref[...])
        dactb = _dot_nt(dz2b, wd_ref[...]).astype(BF16)
        dval_ref[...] = dactb * gl_ref[...]
        dgc_ref[...] = dactb * vdgl_ref[...]
        _put_rows(acc_ref, [_row_sum(dy * xhat2), _row_sum(dy), _row_sum(dpre),
                            jnp.broadcast_to(loss, (1, D_MODEL))])

    vec = _resident((1, D_MODEL))
    return pl.pallas_call(
        body, name="tail", grid=(S // tb,),
        in_specs=[_rows(tb, D_FF), _rows(tb, D_FF), _rows(tb, D_FF), _rows(tb, D_MODEL), _rows(tb, D_MODEL),
                  _rows(tb, PLE_DIM), _rows(tb, D_MODEL), _resident((D_FF, D_MODEL)), _resident((D_MODEL, D_MODEL)), vec,
                  _resident((PLE_DIM, D_MODEL)), vec, vec, vec, vec],
        out_specs=[_rows(tb, D_MODEL), _rows(tb, D_MODEL), _rows(tb, D_MODEL), _rows(tb, D_FF),
                   _rows(tb, D_FF), _rows(tb, D_MODEL), _acc((8, D_MODEL))],
        out_shape=[jax.ShapeDtypeStruct((S, D_MODEL), BF16),
                   jax.ShapeDtypeStruct((S, D_MODEL), BF16), jax.ShapeDtypeStruct((S, D_MODEL), BF16),
                   jax.ShapeDtypeStruct((S, D_FF), BF16), jax.ShapeDtypeStruct((S, D_FF), BF16),
                   jax.ShapeDtypeStruct((S, D_MODEL), F32), jax.ShapeDtypeStruct((8, D_MODEL), F32)],
        compiler_params=_params("arbitrary"),
    )(act, gl, vdgl, z1, h1b, p, tgt, w_down, w_pg, b_pg, w_pp, ln1_g, ln1_b, ln2_g, ln2_b)


def _weight_grad(a_list, b_list, name, layout, ts=512, comm=None, b_window=None, halves=False):
    S = a_list[0].shape[0]
    ms = [a.shape[1] for a in a_list]
    M, nb = sum(ms), len(b_list)
    win, Nb = b_window if b_window else (0, b_list[0].shape[1])
    ts = min(ts, S)
    nk = S // ts
    per_b = N_DEV // nb
    na = len(a_list)

    n_out = 2 if halves else 1
    assert layout == "cols" or not halves

    def body(*refs):
        a_refs, b_refs, o_refs, acc_ref = refs[:na], refs[na:na + nb], refs[na + nb:na + nb + n_out], refs[-1]
        o_ref = o_refs[0]
        j, k = pl.program_id(0), pl.program_id(1)

        @pl.when(k == 0)
        def _():
            acc_ref[...] = jnp.zeros_like(acc_ref)

        for jj in range(nb):
            @pl.when(j == jj)
            def _():
                b = b_refs[jj][...].astype(BF16)
                off = 0
                for a_ref, m in zip(a_refs, ms):
                    acc_ref[off:off + m, :] += _dot_tn(a_ref[...].astype(BF16), b)
                    off += m

        @pl.when(k == nk - 1)
        def _():
            for d in range(per_b):
                if layout == "rows":
                    o_ref[d] = acc_ref[d * (M // N_DEV):(d + 1) * (M // N_DEV), :].astype(BF16)
                elif layout == "cols" and halves:
                    for o_half, r0 in zip(o_refs, (0, M // 2)):
                        o_half[d] = acc_ref[r0:r0 + M // 2, d * (Nb // per_b):(d + 1) * (Nb // per_b)].astype(BF16)
                elif layout == "cols":
                    o_ref[d] = acc_ref[:, d * (Nb // per_b):(d + 1) * (Nb // per_b)].astype(BF16)
                else:
                    o_ref[d] = acc_ref[:, d * (Nb // per_b):(d + 1) * (Nb // per_b)].T.astype(BF16)

    def b_index(jj):
        return lambda j, k: (jnp.where(j == jj, k, jnp.where(j < jj, 0, nk - 1)), win)

    if layout == "rows":
        assert nb == 1
        blk = (N_DEV, M // N_DEV, Nb)
    elif layout == "cols":
        blk = (per_b, M // n_out, Nb // per_b)
    else:
        blk = (per_b, Nb // per_b, M)
    res, comm_res = _pcall(
        body, (*a_list, *b_list), name=name, grid=(nb, nk), sem="arbitrary", comm=comm, step_axis=1,
        in_specs=[pl.BlockSpec((ts, m), lambda j, k: (k, 0)) for m in ms]
        + [pl.BlockSpec((ts, Nb), b_index(jj)) for jj in range(nb)],
        out_specs=[pl.BlockSpec(blk, lambda j, k: (j, 0, 0))] * n_out,
        out_shape=[jax.ShapeDtypeStruct((N_DEV,) + blk[1:], BF16)] * n_out,
        scratch_shapes=[pltpu.VMEM((M, Nb), F32)])
    res = res if halves else res[0]
    return (res, comm_res) if comm is not None else res


def _up_bwd(dgc, gate, dval, dh1p, z1, w_up_top, w_up_bot, fcw, w_out, ln1_g, comm=None):
    S = z1.shape[0]
    tb = min(256, S)
    t16 = tb // 16
    n16 = S // 16
    nblk, kh, wblk = w_up_top.shape
    half = nblk // 2
    nsteps = S // tb

    def body(dgc_ref, dgn_ref, gc_ref, dval_ref, dh1p_ref, z1_ref, wt_hbm, wb_hbm, fcw_ref, wo_ref, g1_ref,
             dgate_ref, dz1_ref, dz1b_ref, datt_ref, drec_ref, accf_ref, accd_ref, wu_s, wu_sems):
        i = pl.program_id(0)

        @pl.when(i == 0)
        def _():
            accf_ref[...] = jnp.zeros_like(accf_ref)
            accd_ref[...] = jnp.zeros_like(accd_ref)
            _load_row_halves(wt_hbm, wb_hbm, wu_s, wu_sems)

        dg = dgc_ref[...].astype(F32)
        nxt = jnp.where(i < nsteps - 1, dgn_ref[...].astype(F32)[0:8], 0.0)
        w = _w_rows(fcw_ref)
        up1, up2 = _shift_up(dg, nxt, 1), _shift_up(dg, nxt, 2)
        dgate = (w[2] * dg + w[1] * up1 + w[0] * up2).astype(BF16)
        dgate_ref[...] = dgate
        gate = gc_ref[...].astype(F32)
        _put_rows(accf_ref, [_row_sum(up2 * gate), _row_sum(up1 * gate), _row_sum(dg * gate), _row_sum(dg)])

        dh1 = dh1p_ref[...]
        for j in range(nblk):
            src = dgate if j < half else dval_ref[...]
            jj = j % half
            dh1 = dh1 + _dot_nt(src[:, jj * wblk:(jj + 1) * wblk], wu_s[j])
        xhat1, rstd1 = _ln_stats(z1_ref[...])
        dz1 = _ln_bwd(dh1, xhat1, rstd1, g1_ref[...])
        dz1_ref[...] = dz1
        dz1b = dz1.astype(BF16)
        dz1b_ref[...] = dz1b
        dcat = _dot_nt(dz1b, wo_ref[...])
        datt_ref[...] = dcat[:, :D_ATT].astype(BF16)
        drec_ref[...] = dcat[:, D_ATT:]
        _put_rows(accd_ref, [_row_sum(dh1 * xhat1), _row_sum(dh1)])

    next16 = pl.BlockSpec((16, D_FF), lambda i: (jnp.minimum((i + 1) * t16, n16 - 1), 0))
    return _pcall(
        body, (dgc, dgc, gate, dval, dh1p, z1, w_up_top, w_up_bot, fcw, w_out, ln1_g), name="up_bwd",
        grid=(nsteps,), sem="arbitrary", comm=comm,
        in_specs=[_rows(tb, D_FF), next16, _rows(tb, D_FF), _rows(tb, D_FF), _rows(tb, D_MODEL),
                  _rows(tb, D_MODEL), _ANY, _ANY, _resident((3, D_FF)),
                  _resident((D_MODEL, D_MODEL)), _resident((1, D_MODEL))],
        scratch_shapes=[pltpu.VMEM((nblk, 2 * kh, wblk), BF16), pltpu.SemaphoreType.DMA((2,))],
        out_specs=[_rows(tb, D_FF), _rows(tb, D_MODEL), _rows(tb, D_MODEL), _rows(tb, D_ATT), _rows(tb, D_RNN),
                   _acc((8, D_FF)), _acc((8, D_MODEL))],
        out_shape=[jax.ShapeDtypeStruct((S, D_FF), BF16), jax.ShapeDtypeStruct((S, D_MODEL), F32),
                   jax.ShapeDtypeStruct((S, D_MODEL), BF16), jax.ShapeDtypeStruct((S, D_ATT), BF16),
                   jax.ShapeDtypeStruct((S, D_RNN), F32), jax.ShapeDtypeStruct((8, D_FF), F32),
                   jax.ShapeDtypeStruct((8, D_MODEL), F32)])


def _attn_bwd(q, k, v, lse, do, sinks, comm=None):
    S = q.shape[0]
    grp = N_HEADS // N_KV
    nq = min(ATT_STEP, S // QBLK)

    def body(sink_ref, q_ref, kc_ref, kp_ref, vc_ref, vp_ref, do_ref, lse_ref, dq_ref, dkc_ref, dkp_ref, dvc_ref, dvp_ref,
             ds_ref):
        i = pl.program_id(0)

        @pl.when(i == 0)
        def _():
            ds_ref[...] = jnp.zeros_like(ds_ref)

        row8 = lax.broadcasted_iota(jnp.int32, (8, 128), 0)
        lane8 = lax.broadcasted_iota(jnp.int32, (8, 128), 1)
        dsink = jnp.zeros((8, 128), F32)
        kall = jnp.concatenate([kp_ref[...], kc_ref[...]], axis=0)
        vall = jnp.concatenate([vp_ref[...], vc_ref[...]], axis=0)
        dk_t = [jnp.zeros((D_KV, QBLK), F32) for _ in range(nq + 1)]
        dv_t = [jnp.zeros((D_KV, QBLK), F32) for _ in range(nq + 1)]
        for b in range(nq):
            valid = _band_mask(i * nq + b)
            rows = slice(b * QBLK, (b + 1) * QBLK)
            keys = slice(b * QBLK, (b + 2) * QBLK)
            qv, dov = q_ref[rows, :], do_ref[rows, :]
            dqs, dks, dvs = [], [], []
            for g in range(N_KV):
                kcat = kall[keys, g * HEAD_DIM:(g + 1) * HEAD_DIM]
                vcat = vall[keys, g * HEAD_DIM:(g + 1) * HEAD_DIM]
                q4, do4 = _stack_heads(qv, g), _stack_heads(dov, g)
                s = jnp.where(valid, _dot_nt(q4, kcat), -1e30)
                lse = lse_ref[(b * N_KV + g) * GROUP * QBLK:(b * N_KV + g + 1) * GROUP * QBLK, :]
                p = jnp.exp(s - lse)
                p_sink = jnp.exp(_sink_column(sink_ref, g) - lse)
                dp = _dot_nt(do4, vcat)
                delta = jnp.sum(p * dp, axis=1, keepdims=True)
                dsc = (p * (dp - delta)).astype(BF16)
                dqs += _unstack_heads(_dot(dsc, kcat) * (HEAD_DIM ** -0.5))
                dks.append(_dot_tn(q4, dsc))
                dvs.append(_dot_tn(do4, p.astype(BF16)))
                for hh, part in enumerate(_unstack_heads(-p_sink * delta)):
                    here = (row8 == 0) & (lane8 == g * grp + hh)
                    dsink = dsink + jnp.where(here, jnp.sum(part, axis=0, keepdims=True), 0.0)
            dq_ref[rows, :] = jnp.concatenate(dqs, axis=1).astype(BF16)
            dk2, dv2 = jnp.concatenate(dks, axis=0), jnp.concatenate(dvs, axis=0)
            dk_t[b], dk_t[b + 1] = dk_t[b] + dk2[:, :QBLK], dk_t[b + 1] + dk2[:, QBLK:]
            dv_t[b], dv_t[b + 1] = dv_t[b] + dv2[:, :QBLK], dv_t[b + 1] + dv2[:, QBLK:]
        dkp_ref[...] = dk_t[0].T
        dvp_ref[...] = dv_t[0].T
        for b in range(nq):
            dkc_ref[b * QBLK:(b + 1) * QBLK, :] = dk_t[b + 1].T
            dvc_ref[b * QBLK:(b + 1) * QBLK, :] = dv_t[b + 1].T
        ds_ref[...] += dsink

    nsteps = S // (nq * QBLK)
    cur = jax.ShapeDtypeStruct((S, D_KV), F32)
    prev = jax.ShapeDtypeStruct((nsteps * QBLK, D_KV), F32)
    big = _rows(nq * QBLK, D_ATT)
    return _pcall(
        body, (sinks, q, k, k, v, v, do, lse), name="attn_bwd", grid=(nsteps,), sem="arbitrary", comm=comm,
        in_specs=[pl.BlockSpec(memory_space=pltpu.SMEM), big] + _attn_specs(nq) + [big, _rows(nq * N_HEADS * QBLK, 1)],
        out_specs=[big, _rows(nq * QBLK, D_KV), _rows(QBLK, D_KV), _rows(nq * QBLK, D_KV), _rows(QBLK, D_KV),
                   _acc((8, 128))],
        out_shape=[jax.ShapeDtypeStruct((S, D_ATT), BF16), cur, prev, cur, prev, jax.ShapeDtypeStruct((8, 128), F32)])


def _rnn_bwd(xr, gr, h, kept, drec, conv_w, wa, wx, lam, comm=None):
    S = xr.shape[0]
    tb = min(512, S)
    t8 = tb // 8
    nsteps = S // tb

    def body(xr_ref, xp_ref, gr_ref, h_ref, hp_ref, xc_ref, r_ref, ig_ref, a_ref, f_ref, drec_ref, cw_ref, wa_ref, wx_ref,
             lam_ref, dxr_ref, dgr_ref, gwa_ref, gwx_ref, acc_ref, carry_s, dxc_halo_s, d_s, gwa_s, gwx_s):
        i = pl.program_id(0)
        blk = nsteps - 1 - i

        @pl.when(i == 0)
        def _():
            gwa_s[...] = jnp.zeros_like(gwa_s)
            gwx_s[...] = jnp.zeros_like(gwx_s)
            acc_ref[...] = jnp.zeros_like(acc_ref)
            carry_s[...] = jnp.zeros_like(carry_s)
            dxc_halo_s[...] = jnp.zeros_like(dxc_halo_s)

        x = xr_ref[...]
        xhalo = jnp.where(blk > 0, xp_ref[...], 0.0)
        cw = _w_rows(cw_ref)
        xs = [_shift_down(x, xhalo, 3), _shift_down(x, xhalo, 2), _shift_down(x, xhalo, 1), x]
        xc, r, ig, a, f = xc_ref[...], r_ref[...], ig_ref[...], a_ref[...], f_ref[...]
        sp = _softplus_neg(lam_ref[...])
        hcur = h_ref[...]
        hprev = _shift_down(hcur, jnp.where(blk > 0, hp_ref[...], 0.0), 1)
        gl, dgl = _gelu_and_grad(gr_ref[...])
        drec = drec_ref[...]
        dgr_ref[...] = (drec * hcur * dgl).astype(BF16)
        d_s[...] = drec * gl
        row8 = lax.broadcasted_iota(jnp.int32, (8, D_RNN), 0)

        def tile(t, c):
            o = pl.multiple_of((t8 - 1 - t) * 8, 8)
            a8 = a_ref[pl.ds(o, 8), :]
            dt = d_s[pl.ds(o, 8), :]
            at = jnp.where(row8 == 7, 1.0, pltpu.roll(a8, 7, 0))
            for s in (1, 2, 4):
                keep = row8 < 8 - s
                a_sh = jnp.where(keep, pltpu.roll(at, 8 - s, 0), 1.0)
                d_sh = jnp.where(keep, pltpu.roll(dt, 8 - s, 0), 0.0)
                dt = at * d_sh + dt
                at = at * a_sh
            lt = at * c + dt
            d_s[pl.ds(o, 8), :] = lt
            return _row_sum(jnp.where(row8 == 0, a8 * lt, 0.0))

        carry_s[0:1, :] = lax.fori_loop(0, t8, tile, carry_s[0:1, :], unroll=2)
        lmb = d_s[...]
        a2 = a * a
        dla = lmb * hprev * a - lmb * ig * xc * (a2 / f)
        di = lmb * f * xc
        dr = dla * (-LRU_C) * sp
        dpa = dr * r * (1.0 - r)
        dpx = di * ig * (1.0 - ig)
        dpab = dpa.astype(BF16)
        dpxb = dpx.astype(BF16)
        xcb = xc.astype(BF16)
        gwa_s[...] += _dot_tn(xcb, dpab)
        gwx_s[...] += _dot_tn(xcb, dpxb)

        @pl.when(i == nsteps - 1)
        def _():
            for dense, out in ((gwa_s[...], gwa_ref), (gwx_s[...], gwx_ref)):
                for b in range(RNN_BLOCKS):
                    rows = slice(b * HEAD_DIM, (b + 1) * HEAD_DIM)
                    out[rows, :] = dense[rows, b * HEAD_DIM:(b + 1) * HEAD_DIM]

        dxc = lmb * f * ig + _dot_nt(dpab, wa_ref[...]) + _dot_nt(dpxb, wx_ref[...])
        nxt = dxc_halo_s[...]
        dxr = cw[3] * dxc
        for s in (1, 2, 3):
            dxr = dxr + cw[3 - s] * _shift_up(dxc, nxt, s)
        dxr_ref[...] = dxr.astype(BF16)
        dxc_halo_s[...] = dxc[:8]
        dlam = _row_sum(dla * (-LRU_C) * r) * (-1.0 / (1.0 + jnp.exp(lam_ref[...])))
        _put_rows(acc_ref, [_row_sum(dxc * xs[0]), _row_sum(dxc * xs[1]), _row_sum(dxc * xs[2]), _row_sum(dxc * xs[3]),
                            _row_sum(dxc), _row_sum(dpa), _row_sum(dpx), dlam])

    rev = lambda i: (nsteps - 1 - i, 0)
    prev8 = lambda i: (jnp.maximum((nsteps - 1 - i) * t8 - 1, 0), 0)
    blkspec = pl.BlockSpec((tb, D_RNN), rev)
    halo8 = pl.BlockSpec((8, D_RNN), prev8)
    vec = _resident((1, D_RNN))
    return _pcall(
        body, (xr, xr, gr, h, h, *kept, drec, conv_w, wa, wx, lam), name="rnn_bwd", grid=(nsteps,),
        sem="arbitrary", comm=comm,
        in_specs=[blkspec, halo8, blkspec, blkspec, halo8] + [blkspec] * 6
        + [_resident((4, D_RNN)), _resident((D_RNN, D_RNN)), _resident((D_RNN, D_RNN)), vec],
        out_specs=[blkspec, blkspec, _acc((D_RNN, HEAD_DIM)), _acc((D_RNN, HEAD_DIM)), _acc((8, D_RNN))],
        out_shape=[jax.ShapeDtypeStruct((S, D_RNN), BF16), jax.ShapeDtypeStruct((S, D_RNN), BF16),
                   jax.ShapeDtypeStruct((D_RNN, HEAD_DIM), F32), jax.ShapeDtypeStruct((D_RNN, HEAD_DIM), F32),
                   jax.ShapeDtypeStruct((8, D_RNN), F32)],
        scratch_shapes=[pltpu.VMEM((8, D_RNN), F32), pltpu.VMEM((8, D_RNN), F32), pltpu.VMEM((tb, D_RNN), F32),
                        pltpu.VMEM((D_RNN, D_RNN), F32), pltpu.VMEM((D_RNN, D_RNN), F32)])


def _in_bwd(dq, dkc, dkp, dvc, dvp, dxr, dgr, dz1, w_in, comm=None):
    S = dz1.shape[0]
    tb = min(ATT_STEP * QBLK, S)
    nsteps = S // tb

    def body(dq_ref, dkc_ref, dkn_ref, dvc_ref, dvn_ref, dxr_ref, dgr_ref, dz1_ref, w_ref, dkv_ref, dx_ref):
        last = pl.program_id(0) == nsteps - 1

        def total(cur_ref, next_ref):
            nxt = jnp.where(last, 0.0, next_ref[...])
            tail = cur_ref[tb - QBLK:, :] + nxt
            return jnp.concatenate([cur_ref[:tb - QBLK, :], tail], axis=0) if tb > QBLK else tail

        dkv = jnp.concatenate([total(dkc_ref, dkn_ref), total(dvc_ref, dvn_ref)], axis=1).astype(BF16)
        dkv_ref[...] = dkv
        du = jnp.concatenate([dq_ref[...], dkv, dxr_ref[...], dgr_ref[...]], axis=1)
        dx_ref[...] = ALPHA * dz1_ref[...] + _dot(du, w_ref[...])

    nextp = pl.BlockSpec((QBLK, D_KV), lambda i: (jnp.minimum(i + 1, nsteps - 1), 0))
    return _pcall(
        body, (dq, dkc, dkp, dvc, dvp, dxr, dgr, dz1, w_in), name="in_bwd", grid=(nsteps,), comm=comm,
        in_specs=[_rows(tb, D_ATT), _rows(tb, D_KV), nextp, _rows(tb, D_KV), nextp,
                  _rows(tb, D_RNN), _rows(tb, D_RNN), _rows(tb, D_MODEL), _resident((D_IN, D_MODEL))],
        out_specs=[_rows(tb, 2 * D_KV), _rows(tb, D_MODEL)],
        out_shape=[jax.ShapeDtypeStruct((S, 2 * D_KV), BF16), jax.ShapeDtypeStruct((S, D_MODEL), F32)])


def _block_diag(w):
    eye = jnp.eye(RNN_BLOCKS, dtype=w.dtype)
    return (w[:, :, None, :] * eye[:, None, :, None]).reshape(D_RNN, D_RNN).astype(BF16)


def _adamw(w, g, m, v):
    m = ADAM_B1 * m + (1.0 - ADAM_B1) * g
    v = ADAM_B2 * v + (1.0 - ADAM_B2) * (g * g)
    m_hat = m / (1.0 - ADAM_B1 ** ADAM_STEP)
    v_hat = v / (1.0 - ADAM_B2 ** ADAM_STEP)
    delta = -ADAM_LR * (m_hat / (jnp.sqrt(v_hat) + ADAM_EPS) + ADAM_WD * w)
    return delta, m, v


def _sum_adamw(parts, w, m, v, name):
    parts = parts if isinstance(parts, (list, tuple)) else [parts]
    R, C = w.shape
    rb = R if R <= 256 else (256 if parts[0].shape[1] % 256 == 0 else 128)
    per = parts[0].shape[1] // rb
    assert R % rb == 0 and parts[0].shape[1] % rb == 0
    n = len(parts)

    def body(*refs):
        p_refs = refs[:n]
        w_ref, m_ref, v_ref, g_out, d_out, m_out, v_out = refs[n:]
        which = pl.program_id(0) // per

        def total(p_ref):
            g = p_ref[0].astype(F32)
            for d in range(1, N_DEV):
                g = g + p_ref[d].astype(F32)
            return g

        g = total(p_refs[0])
        for j in range(1, n):
            g = jnp.where(which == j, total(p_refs[j]), g)
        delta, mn, vn = _adamw(w_ref[...], g, m_ref[...], v_ref[...])
        g_out[...] = g
        d_out[...] = delta
        m_out[...] = mn
        v_out[...] = vn

    def part_spec(j):
        return pl.BlockSpec((N_DEV, rb, C), lambda i: (0, jnp.clip(i - j * per, 0, per - 1), 0))

    blk = _rows(rb, C)
    out = jax.ShapeDtypeStruct((R, C), F32)
    return pl.pallas_call(
        body, name=name, grid=(R // rb,),
        in_specs=[part_spec(j) for j in range(n)] + [blk, blk, blk],
        out_specs=[blk, blk, blk, blk], out_shape=[out, out, out, out],
        compiler_params=_params("parallel"),
    )(*parts, w, m, v)


_SMALL = [("attn_sinks", "s", 0, 1, None), ("rnn_conv_w", "r", 0, 4, "cols"), ("rnn_conv_b", "r", 4, 1, None),
          ("gate_a_w", "a", 0, D_RNN, None), ("gate_a_b", "r", 5, 1, None), ("gate_x_w", "x", 0, D_RNN, None),
          ("gate_x_b", "r", 6, 1, None), ("lru_lambda", "r", 7, 1, None), ("ln1_g", "d", 0, 1, None),
          ("ln1_b", "d", 1, 1, None), ("ffn_conv_w", "f", 0, 3, "cols"), ("ffn_conv_b", "f", 3, 1, None),
          ("ple_gate_b", "t", 2, 1, None), ("ln2_g", "t", 0, 1, None), ("ln2_b", "t", 1, 1, None)]
_LOSS_ROW = 3


_ACC_COLS = {"t": (0, D_MODEL), "f": (D_MODEL, D_FF), "d": (D_MODEL + D_FF, D_MODEL), "s": (2 * D_MODEL + D_FF, 128),
             "r": (2 * D_MODEL + D_FF + 128, D_RNN)}
_ACC_WIDTH = 2 * D_MODEL + D_FF + 128 + D_RNN


def _small_update(rows_all, gates_all, params):
    flat = [arr for triple in params for arr in triple]
    n_par = len(_SMALL)

    def body(*refs):
        rows_ref, gates_ref = refs[:2]
        p_refs = refs[2:2 + 3 * n_par]
        loss_ref = refs[2 + 3 * n_par]
        o_refs = refs[3 + 3 * n_par:3 + 7 * n_par]
        rows_s, tmp_r, tmp_f = refs[3 + 7 * n_par:]
        me = _dev_index(*_place())
        rows_sum, gates_sum = rows_ref[0], gates_ref[0]
        for d in range(1, N_DEV):
            rows_sum = rows_sum + rows_ref[d]
            gates_sum = gates_sum + gates_ref[d]
        rows_s[...] = rows_sum
        t0 = _ACC_COLS["t"][0]
        loss_ref[...] = rows_s[_LOSS_ROW:_LOSS_ROW + 1, t0:t0 + 128]
        for i, (name, key, row, rows, how) in enumerate(_SMALL):
            w_ref, m_ref, v_ref = p_refs[3 * i:3 * i + 3]
            g_out, d_out, m_out, v_out = o_refs[4 * i:4 * i + 4]
            if key == "a":
                g = gates_sum[:, :HEAD_DIM]
            elif key == "x":
                g = gates_sum[:, HEAD_DIM:]
            elif how == "cols":
                c0, width = _ACC_COLS[key]
                full = rows_s[:, c0:c0 + width]
                shard = width // N_DEV
                mine = full[:, :shard]
                for d in range(1, N_DEV):
                    mine = jnp.where(me == d, full[:, d * shard:(d + 1) * shard], mine)
                tmp = tmp_r if key == "r" else tmp_f
                tmp[...] = mine
                g = tmp[row:row + rows, :]
            else:
                c0, width = _ACC_COLS[key]
                g = rows_s[row:row + rows, c0:c0 + width][:, :w_ref.shape[1]]
            delta, mn, vn = _adamw(w_ref[...], g, m_ref[...], v_ref[...])
            g_out[...] = g
            d_out[...] = delta
            m_out[...] = mn
            v_out[...] = vn

    outs = [jax.ShapeDtypeStruct((1, 128), F32)]
    for w, _, _ in params:
        outs += [jax.ShapeDtypeStruct(w.shape, F32)] * 4
    scratch = [pltpu.VMEM((8, _ACC_WIDTH), F32), pltpu.VMEM((8, D_RNN // N_DEV), F32), pltpu.VMEM((8, D_FF // N_DEV), F32)]
    res = pl.pallas_call(body, name="small_update", out_shape=outs, scratch_shapes=scratch)(rows_all, gates_all, *flat)
    return res[0], [res[1 + 4 * i:5 + 4 * i] for i in range(n_par)]


def kernel(x, p, w_in, attn_sinks, rnn_conv_w, rnn_conv_b, gate_a_w, gate_a_b, gate_x_w, gate_x_b, lru_lambda, w_out, ln1_g, ln1_b, w_ffn_up, ffn_conv_w, ffn_conv_b, w_ffn_down, ple_gate_w, ple_gate_b, ple_proj, ln2_g, ln2_b, loss_target, m_w_in, m_attn_sinks, m_rnn_conv_w, m_rnn_conv_b, m_gate_a_w, m_gate_a_b, m_gate_x_w, m_gate_x_b, m_lru_lambda, m_w_out, m_ln1_g, m_ln1_b, m_w_ffn_up, m_ffn_conv_w, m_ffn_conv_b, m_w_ffn_down, m_ple_gate_w, m_ple_gate_b, m_ple_proj, m_ln2_g, m_ln2_b, v_w_in, v_attn_sinks, v_rnn_conv_w, v_rnn_conv_b, v_gate_a_w, v_gate_a_b, v_gate_x_w, v_gate_x_b, v_lru_lambda, v_w_out, v_ln1_g, v_ln1_b, v_w_ffn_up, v_ffn_conv_w, v_ffn_conv_b, v_w_ffn_down, v_ple_gate_w, v_ple_gate_b, v_ple_proj, v_ln2_g, v_ln2_b):
    from_col_blocks = lambda g: g.transpose(1, 0, 2).reshape(g.shape[1], N_DEV * g.shape[2])

    xs, ps, tgt, sinks = x[0], p[0, 0], loss_target[0], attn_sinks[0]
    wa, wx = _block_diag(gate_a_w[0]), _block_diag(gate_x_w[0])

    conv_cols = jnp.concatenate([rnn_conv_w[0].reshape(1, -1), ffn_conv_w[0].reshape(1, -1)], axis=1)
    n_rc, n_fc = 4 * D_RNN // N_DEV, 3 * D_FF // N_DEV
    ((g_in,),) = _comm_call([_Gather([w_in[0].T.astype(BF16)])], "gather_w_in")
    w_in_full = g_in.reshape(D_IN, D_MODEL)

    (q, k, v, xr, gr), _ = _in_proj(xs, w_in_full)
    w_up_shard = w_ffn_up[0].astype(BF16)
    (att, lse), (g_out, w_up_top, g_conv) = _attn_fwd(
        q, k, v, sinks,
        comm=_Multi([_Gather([w_out[0].astype(BF16), w_up_shard[:D_MODEL // 2]]),
                     _Bcast([jnp.broadcast_to(conv_cols, (8, n_rc + n_fc))])]))
    rcw = from_col_blocks(g_conv[:, 0, :n_rc].reshape(N_DEV, 4, D_RNN // N_DEV))
    fcw = from_col_blocks(g_conv[:, 0, n_rc:].reshape(N_DEV, 3, D_FF // N_DEV))
    (rec, h, *kept), (w_up_bot,) = _rnn_fwd(xr, gr, rcw, rnn_conv_b, wa, wx, gate_a_b, gate_x_b, lru_lambda,
                                            comm=_Gather([w_up_shard[D_MODEL // 2:]]))
    w_out_full = g_out.reshape(D_MODEL, D_MODEL)
    (z1, h1b, gate, act, gl, vdgl), (g_down, g_pg, g_pp) = _mix_ln1_up(
        xs, att, rec, w_out_full, ln1_g, ln1_b, w_up_top, w_up_bot, fcw, ffn_conv_b,
        comm=_Gather([w_ffn_down[0].astype(BF16), ple_gate_w[0].astype(BF16), ple_proj[0].astype(BF16)]))
    dz2b, dpreb, dppb, dgc, dval, dh1p, acc_t = _tail(
        act, gl, vdgl, z1, h1b, ps, tgt, g_down.reshape(D_FF, D_MODEL), g_pg.reshape(D_MODEL, D_MODEL), ple_gate_b,
        from_col_blocks(g_pp), ln1_g, ln1_b, ln2_g, ln2_b)

    gd_down = _weight_grad([dz2b], [act], "down_grad", "rows_t", ts=1024)
    gd_pg = _weight_grad([h1b], [dpreb], "pg_grad", "rows", ts=1024)
    gd_pp = _weight_grad([ps], [dppb], "pp_grad", "cols", ts=1024)
    (dgate, dz1, dz1b, datt, drec, acc_f, acc_d), (r_down, r_pg, r_pp) = _up_bwd(
        dgc, gate, dval, dh1p, z1, w_up_top, w_up_bot, fcw, w_out_full, ln1_g, comm=_Exchange([gd_down, gd_pg, gd_pp]))
    gd_up_top, gd_up_bot = _weight_grad([h1b], [dgate, dval], "up_grad", "cols", halves=True)
    gd_out = _weight_grad([att, rec], [dz1b], "out_grad", "rows", ts=1024)
    (dq, dkc, dkp, dvc, dvp, acc_s), (r_up_top,) = _attn_bwd(q, k, v, lse, datt, sinks, comm=_Exchange([gd_up_top]))
    early = jnp.concatenate([acc_t, acc_f, acc_d], axis=1)
    (dxr, dgr, g_wa, g_wx, acc_r), (r_up_bot, r_out, early_all) = _rnn_bwd(
        xr, gr, h, kept, drec, rcw, wa, wx, lru_lambda, comm=_Multi([_Exchange([gd_up_bot, gd_out]), _Bcast([early])]))
    (dkv, dx), _ = _in_bwd(dq, dkc, dkp, dvc, dvp, dxr, dgr, dz1, w_in_full)
    du_parts = [dq, dkv, dxr, dgr]
    lanes = D_RNN // 128
    late = jnp.concatenate([g_wa, g_wx], axis=1)
    late = jnp.concatenate([late, acc_s, acc_r.reshape(8, lanes, 128).transpose(1, 0, 2).reshape(8 * lanes, 128)], axis=0)
    width = D_MODEL // IN_GRAD_PARTS
    comm, r_parts = _Gather([late]), []
    for part in range(IN_GRAD_PARTS):
        gd_part, got = _weight_grad(du_parts, [xs], f"in_grad_{part}", "rows", ts=1024, b_window=(part, width), comm=comm)
        if part == 0:
            (late_all,) = got
        else:
            r_parts += got
        comm = _Exchange([gd_part])
    r_parts += _comm_call([comm], "exchange_w_in")[0]
    r_in = jnp.concatenate(r_parts, axis=2)
    acc_r_all = late_all[:, D_RNN + 8:].reshape(N_DEV, lanes, 8, 128).transpose(0, 2, 1, 3).reshape(N_DEV, 8, D_RNN)
    small_parts = (jnp.concatenate([early_all, late_all[:, D_RNN:D_RNN + 8], acc_r_all], axis=2),
                   late_all[:, :D_RNN])

    outs = {}
    res = _sum_adamw(r_in, w_in[0].T, m_w_in[0].T, v_w_in[0].T, "adamw_w_in")
    outs["w_in"] = [r.T[None] for r in res]
    for name, parts, w, m, v in [("w_out", r_out, w_out, m_w_out, v_w_out),
                                 ("w_ffn_up", [r_up_top, r_up_bot], w_ffn_up, m_w_ffn_up, v_w_ffn_up),
                                 ("w_ffn_down", r_down, w_ffn_down, m_w_ffn_down, v_w_ffn_down),
                                 ("ple_gate_w", r_pg, ple_gate_w, m_ple_gate_w, v_ple_gate_w),
                                 ("ple_proj", r_pp, ple_proj, m_ple_proj, v_ple_proj)]:
        res = _sum_adamw(parts, w[0], m[0], v[0], "adamw_" + name)
        outs[name] = [r[None] for r in res]

    given = dict(attn_sinks=(attn_sinks, m_attn_sinks, v_attn_sinks), rnn_conv_w=(rnn_conv_w, m_rnn_conv_w, v_rnn_conv_w),
                 rnn_conv_b=(rnn_conv_b, m_rnn_conv_b, v_rnn_conv_b), gate_a_w=(gate_a_w, m_gate_a_w, v_gate_a_w),
                 gate_a_b=(gate_a_b, m_gate_a_b, v_gate_a_b), gate_x_w=(gate_x_w, m_gate_x_w, v_gate_x_w),
                 gate_x_b=(gate_x_b, m_gate_x_b, v_gate_x_b), lru_lambda=(lru_lambda, m_lru_lambda, v_lru_lambda),
                 ln1_g=(ln1_g, m_ln1_g, v_ln1_g), ln1_b=(ln1_b, m_ln1_b, v_ln1_b),
                 ffn_conv_w=(ffn_conv_w, m_ffn_conv_w, v_ffn_conv_w), ffn_conv_b=(ffn_conv_b, m_ffn_conv_b, v_ffn_conv_b),
                 ple_gate_b=(ple_gate_b, m_ple_gate_b, v_ple_gate_b), ln2_g=(ln2_g, m_ln2_g, v_ln2_g),
                 ln2_b=(ln2_b, m_ln2_b, v_ln2_b))
    as_2d = lambda a: a.reshape(-1, a.shape[-1])
    loss_row, small_res = _small_update(*small_parts, [tuple(as_2d(a) for a in given[n]) for n, *_ in _SMALL])
    loss = loss_row[0, 0]
    for (n, *_), res in zip(_SMALL, small_res):
        outs[n] = [r.reshape(given[n][0].shape) for r in res]

    order = ["w_in", "attn_sinks", "rnn_conv_w", "rnn_conv_b", "gate_a_w", "gate_a_b", "gate_x_w", "gate_x_b",
             "lru_lambda", "w_out", "ln1_g", "ln1_b", "w_ffn_up", "ffn_conv_w", "ffn_conv_b", "w_ffn_down",
             "ple_gate_w", "ple_gate_b", "ple_proj", "ln2_g", "ln2_b"]
    return (loss, dx[None], *[outs[n][0] for n in order], *[outs[n][1] for n in order],
            *[outs[n][2] for n in order], *[outs[n][3] for n in order])
```

```python
import jax
import jax.numpy as jnp
from jax import lax
from jax.experimental import pallas as pl
from jax.experimental.pallas import tpu as pltpu

F32 = jnp.float32
BF16 = jnp.bfloat16

D_MODEL = 1024
D_ATT = 512
D_KV = 128
HEAD_DIM = 64
N_HEADS = 8
N_KV = 2
D_RNN = 512
RNN_BLOCKS = 8
D_IN = 1792
D_FF = 3072
PLE_DIM = 256
QBLK = 128
N_DEV = 8
ALPHA = float(2 ** 0.25)
LN_EPS = 1e-5
LRU_C = 8.0
ADAM_LR, ADAM_B1, ADAM_B2, ADAM_EPS, ADAM_WD, ADAM_STEP = 0.001, 0.9, 0.999, 1e-08, 0.01, 10

V7X_VMEM_LIMIT = 56 * 1024 * 1024
MESH = pl.DeviceIdType.MESH


def _params(*sem, vmem=V7X_VMEM_LIMIT):
    return pltpu.CompilerParams(dimension_semantics=sem or None, vmem_limit_bytes=vmem)


def _resident(shape):
    return pl.BlockSpec(shape, lambda *_: (0,) * len(shape), pipeline_mode=pl.Buffered(1))


def _rows(tb, cols):
    return pl.BlockSpec((tb, cols), lambda i: (i, 0))


def _acc(shape):
    return pl.BlockSpec(shape, lambda *_: (0,) * len(shape))


def _dot(a, b):
    return jnp.dot(a, b, preferred_element_type=F32)


def _dot_nt(a, b):
    return lax.dot_general(a, b, (((1,), (1,)), ((), ())), preferred_element_type=F32)


def _dot_tn(a, b):
    return lax.dot_general(a, b, (((0,), (0,)), ((), ())), preferred_element_type=F32)


def _sigmoid(x):
    return 1.0 / (1.0 + jnp.exp(-x))


_GELU_C = 0.7978845608028654
_GELU_K = 0.044715


def _gelu_and_grad(x):
    u = x * x
    t = jnp.tanh(x * (_GELU_C + (_GELU_C * _GELU_K) * u))
    hp = 0.5 + 0.5 * t
    dg = hp + x * (0.5 - 0.5 * (t * t)) * (_GELU_C + (3.0 * _GELU_C * _GELU_K) * u)
    return x * hp, dg


def _gelu(x):
    return 0.5 * x * (1.0 + jnp.tanh(_GELU_C * (x + _GELU_K * x * x * x)))


def _ln_stats(z):
    mu = jnp.mean(z, axis=-1, keepdims=True)
    zc = z - mu
    var = jnp.mean(zc * zc, axis=-1, keepdims=True)
    rstd = lax.rsqrt(var + LN_EPS)
    return zc * rstd, rstd


def _ln_bwd(dy, xhat, rstd, g):
    dxh = dy * g
    m1 = jnp.mean(dxh, axis=-1, keepdims=True)
    m2 = jnp.mean(dxh * xhat, axis=-1, keepdims=True)
    return rstd * (dxh - m1 - xhat * m2)


def _softplus_neg(lam):
    u = jnp.exp(-jnp.abs(lam))
    w = 1.0 + u
    d = w - 1.0
    log1p_u = jnp.where(d == 0.0, u, jnp.log(w) * (u / jnp.where(d == 0.0, 1.0, d)))
    return jnp.maximum(-lam, 0.0) + log1p_u


def _shift_down(x, halo, s):
    xs = pltpu.roll(x, s, 0)
    hs = pltpu.roll(halo, s, 0)
    row8 = lax.broadcasted_iota(jnp.int32, hs.shape, 0)
    first = jnp.where(row8 < s, hs, xs[:8])
    return jnp.concatenate([first, xs[8:]], axis=0)


def _shift_up(x, halo, s):
    n = x.shape[0]
    xs = pltpu.roll(x, n - s, 0)
    hs = pltpu.roll(halo, 8 - s, 0)
    row8 = lax.broadcasted_iota(jnp.int32, hs.shape, 0)
    last = jnp.where(row8 >= 8 - s, hs, xs[n - 8:])
    return jnp.concatenate([xs[:n - 8], last], axis=0)


def _row_sum(x):
    return jnp.sum(x, axis=0, keepdims=True)


def _put_rows(acc_ref, rows):
    row8 = lax.broadcasted_iota(jnp.int32, acc_ref.shape, 0)
    upd = jnp.zeros(acc_ref.shape, F32)
    for r, vec in enumerate(rows):
        upd = jnp.where(row8 == r, vec, upd)
    acc_ref[...] += upd


def _place():
    return lax.axis_index("x"), lax.axis_index("y"), lax.axis_index("c")


def _dev_index(px, py, pc):
    return 4 * px + 2 * py + pc


_ANY = pl.BlockSpec(memory_space=pl.ANY)


class _Gather:
    def __init__(self, arrays):
        self.arrays = list(arrays)
        self.n = len(self.arrays)

    def out_shape(self):
        return [jax.ShapeDtypeStruct((N_DEV,) + s.shape, s.dtype) for s in self.arrays]

    def scratch(self):
        return [pltpu.SemaphoreType.DMA((self.n, 7)), pltpu.SemaphoreType.DMA((self.n, 7)),
                pltpu.SemaphoreType.DMA((self.n,))]

    def _parts(self, ins, outs, sems):
        send_sems, recv_sems, local_sems = sems
        x, y, c = _place()
        me, sibling = (x, y, c), (x, y, 1 - c)
        chips = [(1 - x, y), (x, 1 - y), (1 - x, 1 - y)]

        def copy(a, k, block, to, src=None):
            rows = outs[a].at[_dev_index(*block)]
            return pltpu.make_async_remote_copy(
                src_ref=rows if src is None else src, dst_ref=rows, send_sem=send_sems.at[a, k],
                recv_sem=recv_sems.at[a, k], device_id=to, device_id_type=MESH)

        rng = range(self.n)
        mine = [pltpu.make_async_copy(ins[a], outs[a].at[_dev_index(*me)], local_sems.at[a]) for a in rng]
        first = [copy(a, 0, me, sibling, src=ins[a]) for a in rng]
        first += [copy(a, 1 + j, me, (*chip, c), src=ins[a]) for j, chip in enumerate(chips) for a in rng]
        landed = [copy(a, 1 + j, (*chip, c), me) for j, chip in enumerate(chips) for a in rng]
        passed = [copy(a, 4 + j, (*chip, c), sibling) for j, chip in enumerate(chips) for a in rng]
        from_sibling = [copy(a, 0, sibling, me) for a in rng]
        from_sibling += [copy(a, 4 + j, (*chip, 1 - c), me) for j, chip in enumerate(chips) for a in rng]
        return mine, first, landed, passed, from_sibling

    def start(self, ins, outs, sems):
        mine, first, _, _, _ = self._parts(ins, outs, sems)
        for cp in mine + first:
            cp.start()

    def forward(self, ins, outs, sems):
        _, _, landed, passed, _ = self._parts(ins, outs, sems)
        for got, fwd in zip(landed, passed):
            got.wait_recv()
            fwd.start()

    def finish(self, ins, outs, sems):
        mine, first, _, passed, from_sibling = self._parts(ins, outs, sems)
        for cp in from_sibling:
            cp.wait_recv()
        for cp in first + passed:
            cp.wait_send()
        for cp in mine:
            cp.wait()

    def before(self, ins, outs, sems, step, nsteps):
        pl.when(step == 0)(lambda: self.start(ins, outs, sems))
        pl.when(step == (7 * nsteps) // 8)(lambda: self.forward(ins, outs, sems))

    def after(self, ins, outs, sems, step, nsteps):
        pl.when(step == nsteps - 1)(lambda: self.finish(ins, outs, sems))


class _Exchange:
    def __init__(self, arrays):
        self.arrays = list(arrays)
        self.n = len(self.arrays)

    def out_shape(self):
        return [jax.ShapeDtypeStruct(b.shape, b.dtype) for b in self.arrays]

    def scratch(self):
        return [pltpu.SemaphoreType.DMA((self.n, 7)), pltpu.SemaphoreType.DMA((self.n, 7)),
                pltpu.SemaphoreType.DMA((self.n,))]

    def _parts(self, ins, outs, sems):
        send_sems, recv_sems, local_sems = sems
        x, y, c = _place()
        me = _dev_index(x, y, c)
        peers = [(x ^ (k >> 2), y ^ ((k >> 1) & 1), c ^ (k & 1)) for k in range(1, N_DEV)]
        rng = range(self.n)
        mine = [pltpu.make_async_copy(ins[a].at[me], outs[a].at[me], local_sems.at[a]) for a in rng]
        sent = [pltpu.make_async_remote_copy(
            src_ref=ins[a].at[_dev_index(*to)], dst_ref=outs[a].at[me], send_sem=send_sems.at[a, k],
            recv_sem=recv_sems.at[a, k], device_id=to, device_id_type=MESH) for k, to in enumerate(peers) for a in rng]
        arrivals = [pltpu.make_async_remote_copy(
            src_ref=ins[a].at[me], dst_ref=outs[a].at[_dev_index(*frm)], send_sem=send_sems.at[a, k],
            recv_sem=recv_sems.at[a, k], device_id=frm, device_id_type=MESH) for k, frm in enumerate(peers) for a in rng]
        return mine, sent, arrivals

    def start(self, ins, outs, sems):
        mine, sent, _ = self._parts(ins, outs, sems)
        for cp in mine + sent:
            cp.start()

    def finish(self, ins, outs, sems):
        mine, sent, arrivals = self._parts(ins, outs, sems)
        for cp in arrivals:
            cp.wait_recv()
        for cp in sent:
            cp.wait_send()
        for cp in mine:
            cp.wait()

    def before(self, ins, outs, sems, step, nsteps):
        pl.when(step == 0)(lambda: self.start(ins, outs, sems))

    def after(self, ins, outs, sems, step, nsteps):
        pl.when(step == nsteps - 1)(lambda: self.finish(ins, outs, sems))


class _Bcast(_Exchange):
    def out_shape(self):
        return [jax.ShapeDtypeStruct((N_DEV,) + s.shape, s.dtype) for s in self.arrays]

    def _parts(self, ins, outs, sems):
        send_sems, recv_sems, local_sems = sems
        x, y, c = _place()
        me = _dev_index(x, y, c)
        peers = [(x ^ (k >> 2), y ^ ((k >> 1) & 1), c ^ (k & 1)) for k in range(1, N_DEV)]
        rng = range(self.n)
        mine = [pltpu.make_async_copy(ins[a], outs[a].at[me], local_sems.at[a]) for a in rng]
        sent = [pltpu.make_async_remote_copy(
            src_ref=ins[a], dst_ref=outs[a].at[me], send_sem=send_sems.at[a, k], recv_sem=recv_sems.at[a, k],
            device_id=to, device_id_type=MESH) for k, to in enumerate(peers) for a in rng]
        arrivals = [pltpu.make_async_remote_copy(
            src_ref=ins[a], dst_ref=outs[a].at[_dev_index(*frm)], send_sem=send_sems.at[a, k],
            recv_sem=recv_sems.at[a, k], device_id=frm, device_id_type=MESH) for k, frm in enumerate(peers) for a in rng]
        return mine, sent, arrivals


class _Multi:
    def __init__(self, comms):
        self.comms = list(comms)
        self.arrays = [arr for c in self.comms for arr in c.arrays]
        self.n = len(self.arrays)

    def out_shape(self):
        return [s for c in self.comms for s in c.out_shape()]

    def scratch(self):
        return [s for c in self.comms for s in c.scratch()]

    def _each(self, ins, outs, sems):
        a = 0
        for j, c in enumerate(self.comms):
            yield c, ins[a:a + c.n], outs[a:a + c.n], sems[3 * j:3 * j + 3]
            a += c.n

    def before(self, ins, outs, sems, step, nsteps):
        for c, ci, co, cs in self._each(ins, outs, sems):
            c.before(ci, co, cs, step, nsteps)

    def after(self, ins, outs, sems, step, nsteps):
        for c, ci, co, cs in self._each(ins, outs, sems):
            c.after(ci, co, cs, step, nsteps)


def _comm_call(comms, name):
    ns = [c.n for c in comms]
    n = sum(ns)

    def body(*refs):
        parts, a, s = [], 0, 2 * n
        for c in comms:
            parts.append((c, refs[a:a + c.n], refs[n + a:n + a + c.n], refs[s:s + 3]))
            a, s = a + c.n, s + 3
        for c, ins, outs, sems in parts:
            c.start(ins, outs, sems)
        for c, ins, outs, sems in parts:
            if isinstance(c, _Gather):
                c.forward(ins, outs, sems)
        for c, ins, outs, sems in parts:
            c.finish(ins, outs, sems)

    res = pl.pallas_call(
        body, name=name, in_specs=[_ANY] * n, out_specs=[_ANY] * n,
        out_shape=[s for c in comms for s in c.out_shape()], scratch_shapes=[s for c in comms for s in c.scratch()],
    )(*[arr for c in comms for arr in c.arrays])
    out, a = [], 0
    for k in ns:
        out.append(res[a:a + k])
        a += k
    return out


def _pcall(body, args, *, name, grid, in_specs, out_specs, out_shape, scratch_shapes=(), sem="parallel", comm=None,
           step_axis=0):
    sem = (sem,) * len(grid) if isinstance(sem, str) else sem
    if comm is None:
        res = pl.pallas_call(body, name=name, grid=grid, in_specs=in_specs, out_specs=out_specs, out_shape=out_shape,
                             scratch_shapes=list(scratch_shapes), compiler_params=_params(*sem))(*args)
        return res, []
    n_in, n_out, n_scr, n = len(in_specs), len(out_specs), len(scratch_shapes), comm.n
    nsteps = grid[step_axis]
    assert all(g == 1 for ax, g in enumerate(grid) if ax != step_axis)

    def hosted(*refs):
        ins, cin = refs[:n_in], refs[n_in:n_in + n]
        o0 = n_in + n
        outs, cout = refs[o0:o0 + n_out], refs[o0 + n_out:o0 + n_out + n]
        s0 = o0 + n_out + n
        scr, sems = refs[s0:s0 + n_scr], refs[s0 + n_scr:]
        step = pl.program_id(step_axis)
        comm.before(cin, cout, sems, step, nsteps)
        body(*ins, *outs, *scr)
        comm.after(cin, cout, sems, step, nsteps)

    res = pl.pallas_call(
        hosted, name=name, grid=grid, in_specs=list(in_specs) + [_ANY] * n, out_specs=list(out_specs) + [_ANY] * n,
        out_shape=list(out_shape) + comm.out_shape(), scratch_shapes=list(scratch_shapes) + comm.scratch(),
        compiler_params=_params(*(("arbitrary",) * len(grid))))(*args, *comm.arrays)
    return res[:n_out], res[n_out:]


def _load_row_halves(top_hbm, bot_hbm, full_s, sems):
    r = top_hbm.shape[1]
    copies = [pltpu.make_async_copy(top_hbm, full_s.at[:, :r, :], sems.at[0]),
              pltpu.make_async_copy(bot_hbm, full_s.at[:, r:, :], sems.at[1])]
    for cp in copies:
        cp.start()
    for cp in copies:
        cp.wait()


def _in_proj(x, w_in_t, comm=None):
    S = x.shape[0]
    tb = min(1024, S)

    def body(x_ref, w_ref, q_ref, k_ref, v_ref, xr_ref, gr_ref):
        u = _dot_nt(x_ref[...].astype(BF16), w_ref[...])
        q_ref[...] = (u[:, :D_ATT] * (HEAD_DIM ** -0.5)).astype(BF16)
        k_ref[...] = u[:, D_ATT:D_ATT + D_KV].astype(BF16)
        v_ref[...] = u[:, D_ATT + D_KV:D_ATT + 2 * D_KV].astype(BF16)
        xr_ref[...] = u[:, D_ATT + 2 * D_KV:D_ATT + 2 * D_KV + D_RNN]
        gr_ref[...] = u[:, D_ATT + 2 * D_KV + D_RNN:]

    return _pcall(
        body, (x, w_in_t), name="in_proj", grid=(S // tb,), comm=comm,
        in_specs=[_rows(tb, D_MODEL), _resident((D_IN, D_MODEL))],
        out_specs=[_rows(tb, D_ATT), _rows(tb, D_KV), _rows(tb, D_KV), _rows(tb, D_RNN), _rows(tb, D_RNN)],
        out_shape=[jax.ShapeDtypeStruct((S, D_ATT), BF16), jax.ShapeDtypeStruct((S, D_KV), BF16),
                   jax.ShapeDtypeStruct((S, D_KV), BF16), jax.ShapeDtypeStruct((S, D_RNN), F32),
                   jax.ShapeDtypeStruct((S, D_RNN), F32)])


GROUP = N_HEADS // N_KV


def _band_mask(i):
    qi = lax.broadcasted_iota(jnp.int32, (GROUP * QBLK, 2 * QBLK), 0) & (QBLK - 1)
    sj = lax.broadcasted_iota(jnp.int32, (GROUP * QBLK, 2 * QBLK), 1)
    return (sj > qi) & (sj <= qi + QBLK) & ((sj >= QBLK) | (i > 0))


def _stack_heads(x, g):
    return jnp.concatenate([x[:, (g * GROUP + hh) * HEAD_DIM:(g * GROUP + hh + 1) * HEAD_DIM] for hh in range(GROUP)],
                           axis=0)


def _unstack_heads(x4):
    return [x4[hh * QBLK:(hh + 1) * QBLK] for hh in range(GROUP)]


def _sink_column(sink_ref, g):
    head = lax.broadcasted_iota(jnp.int32, (GROUP * QBLK, 1), 0) // QBLK
    col = jnp.full((GROUP * QBLK, 1), sink_ref[g * GROUP], F32)
    for hh in range(1, GROUP):
        col = jnp.where(head == hh, sink_ref[g * GROUP + hh], col)
    return col


ATT_STEP = 4
IN_GRAD_PARTS = 2


def _attn_specs(nq=1):
    cur = lambda i: (i, 0)
    prev = lambda i: (jnp.maximum(nq * i - 1, 0), 0)
    return [pl.BlockSpec((nq * QBLK, D_KV), cur), pl.BlockSpec((QBLK, D_KV), prev),
            pl.BlockSpec((nq * QBLK, D_KV), cur), pl.BlockSpec((QBLK, D_KV), prev)]


def _attn_fwd(q, k, v, sinks, comm=None):
    S = q.shape[0]
    nq = min(ATT_STEP, S // QBLK)

    def body(sink_ref, q_ref, kc_ref, kp_ref, vc_ref, vp_ref, o_ref, lse_ref):
        first = pl.program_id(0) * nq
        kall = jnp.concatenate([kp_ref[...], kc_ref[...]], axis=0)
        vall = jnp.concatenate([vp_ref[...], vc_ref[...]], axis=0)
        for b in range(nq):
            valid = _band_mask(first + b)
            rows = slice(b * QBLK, (b + 1) * QBLK)
            keys = slice(b * QBLK, (b + 2) * QBLK)
            qv = q_ref[rows, :]
            outs = []
            for g in range(N_KV):
                kcat = kall[keys, g * HEAD_DIM:(g + 1) * HEAD_DIM]
                vcat = vall[keys, g * HEAD_DIM:(g + 1) * HEAD_DIM]
                s = jnp.where(valid, _dot_nt(_stack_heads(qv, g), kcat), -1e30)
                sink = _sink_column(sink_ref, g)
                m = jnp.maximum(jnp.max(s, axis=1, keepdims=True), sink)
                p = jnp.exp(s - m)
                l = jnp.sum(p, axis=1, keepdims=True) + jnp.exp(sink - m)
                outs += _unstack_heads(_dot(p.astype(BF16), vcat) / l)
                lse_ref[(b * N_KV + g) * GROUP * QBLK:(b * N_KV + g + 1) * GROUP * QBLK, :] = m + jnp.log(l)
            o_ref[rows, :] = jnp.concatenate(outs, axis=1).astype(BF16)

    lse_rows = nq * N_HEADS * QBLK
    return _pcall(
        body, (sinks, q, k, k, v, v), name="attn_fwd", grid=(S // (nq * QBLK),), comm=comm,
        in_specs=[pl.BlockSpec(memory_space=pltpu.SMEM), _rows(nq * QBLK, D_ATT)] + _attn_specs(nq),
        out_specs=[_rows(nq * QBLK, D_ATT), _rows(lse_rows, 1)],
        out_shape=[jax.ShapeDtypeStruct((S, D_ATT), BF16), jax.ShapeDtypeStruct((S * N_HEADS, 1), F32)])


def _w_rows(w_ref):
    return [w_ref[k:k + 1, :] for k in range(w_ref.shape[0])]


def _conv4(x, halo, w, b):
    y = b + w[3] * x
    for s in (1, 2, 3):
        y = y + w[3 - s] * _shift_down(x, halo, s)
    return y


def _rnn_gates(xc, wa, wx, ba, bx, sp):
    xcb = xc.astype(BF16)
    r = _sigmoid(_dot(xcb, wa) + ba)
    ig = _sigmoid(_dot(xcb, wx) + bx)
    la = -LRU_C * r * sp
    a = jnp.exp(la)
    t = jnp.tanh(la)
    f = jnp.sqrt(-2.0 * t / (1.0 - t))
    return r, ig, a, f


def _rnn_fwd(xr, gr, conv_w, conv_b, wa, wx, ba, bx, lam, comm=None):
    S = xr.shape[0]
    tb = min(512, S)

    def body(xr_ref, gr_ref, cw_ref, cb_ref, wa_ref, wx_ref, ba_ref, bx_ref, lam_ref, rec_ref, h_ref,
             xc_ref, r_ref, ig_ref, a_ref, f_ref, halo_s, hc_s, a_s, b_s):
        @pl.when(pl.program_id(0) == 0)
        def _():
            halo_s[...] = jnp.zeros_like(halo_s)
            hc_s[...] = jnp.zeros_like(hc_s)

        x = xr_ref[...]
        xc = _conv4(x, halo_s[...], _w_rows(cw_ref), cb_ref[...])
        halo_s[...] = x[tb - 8:]
        r, ig, a, f = _rnn_gates(xc, wa_ref[...], wx_ref[...], ba_ref[...], bx_ref[...], _softplus_neg(lam_ref[...]))
        xc_ref[...] = xc
        r_ref[...] = r
        ig_ref[...] = ig
        a_ref[...] = a
        f_ref[...] = f
        a_s[...] = a
        b_s[...] = f * ig * xc
        row8 = lax.broadcasted_iota(jnp.int32, (8, D_RNN), 0)

        def tile(t, hc):
            o = pl.multiple_of(t * 8, 8)
            at = a_s[pl.ds(o, 8), :]
            bt = b_s[pl.ds(o, 8), :]
            for s in (1, 2, 4):
                keep = row8 >= s
                a_sh = jnp.where(keep, pltpu.roll(at, s, 0), 1.0)
                b_sh = jnp.where(keep, pltpu.roll(bt, s, 0), 0.0)
                bt = at * b_sh + bt
                at = at * a_sh
            ht = at * hc + bt
            b_s[pl.ds(o, 8), :] = ht
            return _row_sum(jnp.where(row8 == 7, ht, 0.0))

        hc_s[0:1, :] = lax.fori_loop(0, tb // 8, tile, hc_s[0:1, :], unroll=2)
        h = b_s[...]
        h_ref[...] = h
        rec_ref[...] = (h * _gelu(gr_ref[...])).astype(BF16)

    vec = _resident((1, D_RNN))
    kept = jax.ShapeDtypeStruct((S, D_RNN), F32)
    return _pcall(
        body, (xr, gr, conv_w, conv_b, wa, wx, ba, bx, lam), name="rnn_fwd", grid=(S // tb,), sem="arbitrary", comm=comm,
        in_specs=[_rows(tb, D_RNN), _rows(tb, D_RNN), _resident((4, D_RNN)), vec,
                  _resident((D_RNN, D_RNN)), _resident((D_RNN, D_RNN)), vec, vec, vec],
        out_specs=[_rows(tb, D_RNN)] * 7,
        out_shape=[jax.ShapeDtypeStruct((S, D_RNN), BF16), kept, kept, kept, kept, kept, kept],
        scratch_shapes=[pltpu.VMEM((8, D_RNN), F32), pltpu.VMEM((8, D_RNN), F32),
                        pltpu.VMEM((tb, D_RNN), F32), pltpu.VMEM((tb, D_RNN), F32)])


def _mix_ln1_up(x, att, rec, w_out, ln1_g, ln1_b, w_up_top, w_up_bot, fcw, fcb, comm=None):
    S = x.shape[0]
    tb = min(256, S)
    nblk, kh, wblk = w_up_top.shape
    half = nblk // 2

    def body(x_ref, att_ref, rec_ref, wo_ref, g_ref, b_ref, wt_hbm, wb_hbm, fcw_ref, fcb_ref,
             z1_ref, h1b_ref, gate_ref, act_ref, gl_ref, vdgl_ref, halo_s, wu_s, wu_sems):
        @pl.when(pl.program_id(0) == 0)
        def _():
            halo_s[...] = jnp.zeros_like(halo_s)
            _load_row_halves(wt_hbm, wb_hbm, wu_s, wu_sems)

        z1 = ALPHA * x_ref[...] + _dot(att_ref[...], wo_ref[:D_ATT, :]) + _dot(rec_ref[...], wo_ref[D_ATT:, :])
        z1_ref[...] = z1
        xhat, _ = _ln_stats(z1)
        h1b = (xhat * g_ref[...] + b_ref[...]).astype(BF16)
        h1b_ref[...] = h1b
        for jj in range(half):
            cols = slice(jj * wblk, (jj + 1) * wblk)
            gate = _dot(h1b, wu_s[jj])
            val = _dot(h1b, wu_s[jj + half])
            halo = halo_s[:, cols]
            conv = (fcb_ref[:, cols] + fcw_ref[2:3, cols] * gate + fcw_ref[1:2, cols] * _shift_down(gate, halo, 1)
                    + fcw_ref[0:1, cols] * _shift_down(gate, halo, 2))
            halo_s[:, cols] = gate[tb - 8:]
            gl, dgl = _gelu_and_grad(conv)
            gate_ref[:, cols] = gate.astype(BF16)
            act_ref[:, cols] = (gl * val).astype(BF16)
            gl_ref[:, cols] = gl.astype(BF16)
            vdgl_ref[:, cols] = (val * dgl).astype(BF16)

    vec = _resident((1, D_MODEL))
    wide = jax.ShapeDtypeStruct((S, D_FF), BF16)
    return _pcall(
        body, (x, att, rec, w_out, ln1_g, ln1_b, w_up_top, w_up_bot, fcw, fcb), name="mix_ln1_up", grid=(S // tb,),
        sem="arbitrary", comm=comm,
        in_specs=[_rows(tb, D_MODEL), _rows(tb, D_ATT), _rows(tb, D_RNN), _resident((D_MODEL, D_MODEL)), vec, vec,
                  _ANY, _ANY, _resident((3, D_FF)), _resident((1, D_FF))],
        out_specs=[_rows(tb, D_MODEL), _rows(tb, D_MODEL)] + [_rows(tb, D_FF)] * 4,
        out_shape=[jax.ShapeDtypeStruct((S, D_MODEL), F32), jax.ShapeDtypeStruct((S, D_MODEL), BF16), wide, wide, wide, wide],
        scratch_shapes=[pltpu.VMEM((8, D_FF), F32), pltpu.VMEM((nblk, 2 * kh, wblk), BF16),
                        pltpu.SemaphoreType.DMA((2,))])


def _tail(act, gl, vdgl, z1, h1b, p, tgt, w_down, w_pg, b_pg, w_pp, ln1_g, ln1_b, ln2_g, ln2_b):
    S = z1.shape[0]
    tb = min(256, S)

    def body(act_ref, gl_ref, vdgl_ref, z1_ref, h1b_ref, p_ref, t_ref, wd_ref, wpg_ref, bpg_ref, wpp_ref,
             g1_ref, b1_ref, g2_ref, b2_ref, dz2_ref, dpre_ref, dpp_ref, dgc_ref, dval_ref, dh1_ref, acc_ref):
        i = pl.program_id(0)

        @pl.when(i == 0)
        def _():
            acc_ref[...] = jnp.zeros_like(acc_ref)

        ffn = _dot(act_ref[...], wd_ref[...])
        xhat1, _ = _ln_stats(z1_ref[...])
        h1 = xhat1 * g1_ref[...] + b1_ref[...]
        sg = _sigmoid(_dot(h1b_ref[...], wpg_ref[...]) + bpg_ref[...])
        pp = _dot(p_ref[...].astype(BF16), wpp_ref[...])
        z2 = ALPHA * h1 + ffn + sg * pp
        xhat2, rstd2 = _ln_stats(z2)
        y = xhat2 * g2_ref[...] + b2_ref[...]
        err = y - t_ref[...]
        dy = err * (1.0 / D_MODEL)
        loss = 0.5 * jnp.sum(jnp.sum(err * err, axis=1, keepdims=True), axis=0, keepdims=True) * (1.0 / D_MODEL)
        dz2 = _ln_bwd(dy, xhat2, rstd2, g2_ref[...])
        dz2b = dz2.astype(BF16)
        dz2_ref[...] = dz2b
        dpre = dz2 * pp * sg * (1.0 - sg)
        dpreb = dpre.astype(BF16)
        dpre_ref[...] = dpreb
        dpp_ref[...] = (dz2 * sg).astype(BF16)
        dh1_ref[...] = ALPHA * dz2 + _dot_nt(dpreb, wpg_ref[...])
        dactb = _dot_nt(dz2b, wd_ref[...]).astype(BF16)
        dval_ref[...] = dactb * gl_ref[...]
        dgc_ref[...] = dactb * vdgl_ref[...]
        _put_rows(acc_ref, [_row_sum(dy * xhat2), _row_sum(dy), _row_sum(dpre),
                            jnp.broadcast_to(loss, (1, D_MODEL))])

    vec = _resident((1, D_MODEL))
    return pl.pallas_call(
        body, name="tail", grid=(S // tb,),
        in_specs=[_rows(tb, D_FF), _rows(tb, D_FF), _rows(tb, D_FF), _rows(tb, D_MODEL), _rows(tb, D_MODEL),
                  _rows(tb, PLE_DIM), _rows(tb, D_MODEL), _resident((D_FF, D_MODEL)), _resident((D_MODEL, D_MODEL)), vec,
                  _resident((PLE_DIM, D_MODEL)), vec, vec, vec, vec],
        out_specs=[_rows(tb, D_MODEL), _rows(tb, D_MODEL), _rows(tb, D_MODEL), _rows(tb, D_FF),
                   _rows(tb, D_FF), _rows(tb, D_MODEL), _acc((8, D_MODEL))],
        out_shape=[jax.ShapeDtypeStruct((S, D_MODEL), BF16),
                   jax.ShapeDtypeStruct((S, D_MODEL), BF16), jax.ShapeDtypeStruct((S, D_MODEL), BF16),
                   jax.ShapeDtypeStruct((S, D_FF), BF16), jax.ShapeDtypeStruct((S, D_FF), BF16),
                   jax.ShapeDtypeStruct((S, D_MODEL), F32), jax.ShapeDtypeStruct((8, D_MODEL), F32)],
        compiler_params=_params("arbitrary"),
    )(act, gl, vdgl, z1, h1b, p, tgt, w_down, w_pg, b_pg, w_pp, ln1_g, ln1_b, ln2_g, ln2_b)


def _weight_grad(a_list, b_list, name, layout, ts=512, comm=None, b_window=None, halves=False):
    S = a_list[0].shape[0]
    ms = [a.shape[1] for a in a_list]
    M, nb = sum(ms), len(b_list)
    win, Nb = b_window if b_window else (0, b_list[0].shape[1])
    ts = min(ts, S)
    nk = S // ts
    per_b = N_DEV // nb
    na = len(a_list)

    n_out = 2 if halves else 1
    assert layout == "cols" or not halves

    def body(*refs):
        a_refs, b_refs, o_refs, acc_ref = refs[:na], refs[na:na + nb], refs[na + nb:na + nb + n_out], refs[-1]
        o_ref = o_refs[0]
        j, k = pl.program_id(0), pl.program_id(1)

        @pl.when(k == 0)
        def _():
            acc_ref[...] = jnp.zeros_like(acc_ref)

        for jj in range(nb):
            @pl.when(j == jj)
            def _():
                b = b_refs[jj][...].astype(BF16)
                off = 0
                for a_ref, m in zip(a_refs, ms):
                    acc_ref[off:off + m, :] += _dot_tn(a_ref[...].astype(BF16), b)
                    off += m

        @pl.when(k == nk - 1)
        def _():
            for d in range(per_b):
                if layout == "rows":
                    o_ref[d] = acc_ref[d * (M // N_DEV):(d + 1) * (M // N_DEV), :].astype(BF16)
                elif layout == "cols" and halves:
                    for o_half, r0 in zip(o_refs, (0, M // 2)):
                        o_half[d] = acc_ref[r0:r0 + M // 2, d * (Nb // per_b):(d + 1) * (Nb // per_b)].astype(BF16)
                elif layout == "cols":
                    o_ref[d] = acc_ref[:, d * (Nb // per_b):(d + 1) * (Nb // per_b)].astype(BF16)
                else:
                    o_ref[d] = acc_ref[:, d * (Nb // per_b):(d + 1) * (Nb // per_b)].T.astype(BF16)

    def b_index(jj):
        return lambda j, k: (jnp.where(j == jj, k, jnp.where(j < jj, 0, nk - 1)), win)

    if layout == "rows":
        assert nb == 1
        blk = (N_DEV, M // N_DEV, Nb)
    elif layout == "cols":
        blk = (per_b, M // n_out, Nb // per_b)
    else:
        blk = (per_b, Nb // per_b, M)
    res, comm_res = _pcall(
        body, (*a_list, *b_list), name=name, grid=(nb, nk), sem="arbitrary", comm=comm, step_axis=1,
        in_specs=[pl.BlockSpec((ts, m), lambda j, k: (k, 0)) for m in ms]
        + [pl.BlockSpec((ts, Nb), b_index(jj)) for jj in range(nb)],
        out_specs=[pl.BlockSpec(blk, lambda j, k: (j, 0, 0))] * n_out,
        out_shape=[jax.ShapeDtypeStruct((N_DEV,) + blk[1:], BF16)] * n_out,
        scratch_shapes=[pltpu.VMEM((M, Nb), F32)])
    res = res if halves else res[0]
    return (res, comm_res) if comm is not None else res


def _up_bwd(dgc, gate, dval, dh1p, z1, w_up_top, w_up_bot, fcw, w_out, ln1_g, comm=None):
    S = z1.shape[0]
    tb = min(256, S)
    t16 = tb // 16
    n16 = S // 16
    nblk, kh, wblk = w_up_top.shape
    half = nblk // 2
    nsteps = S // tb

    def body(dgc_ref, dgn_ref, gc_ref, dval_ref, dh1p_ref, z1_ref, wt_hbm, wb_hbm, fcw_ref, wo_ref, g1_ref,
             dgate_ref, dz1_ref, dz1b_ref, datt_ref, drec_ref, accf_ref, accd_ref, wu_s, wu_sems):
        i = pl.program_id(0)

        @pl.when(i == 0)
        def _():
            accf_ref[...] = jnp.zeros_like(accf_ref)
            accd_ref[...] = jnp.zeros_like(accd_ref)
            _load_row_halves(wt_hbm, wb_hbm, wu_s, wu_sems)

        dg = dgc_ref[...].astype(F32)
        nxt = jnp.where(i < nsteps - 1, dgn_ref[...].astype(F32)[0:8], 0.0)
        w = _w_rows(fcw_ref)
        up1, up2 = _shift_up(dg, nxt, 1), _shift_up(dg, nxt, 2)
        dgate = (w[2] * dg + w[1] * up1 + w[0] * up2).astype(BF16)
        dgate_ref[...] = dgate
        gate = gc_ref[...].astype(F32)
        _put_rows(accf_ref, [_row_sum(up2 * gate), _row_sum(up1 * gate), _row_sum(dg * gate), _row_sum(dg)])

        dh1 = dh1p_ref[...]
        for j in range(nblk):
            src = dgate if j < half else dval_ref[...]
            jj = j % half
            dh1 = dh1 + _dot_nt(src[:, jj * wblk:(jj + 1) * wblk], wu_s[j])
        xhat1, rstd1 = _ln_stats(z1_ref[...])
        dz1 = _ln_bwd(dh1, xhat1, rstd1, g1_ref[...])
        dz1_ref[...] = dz1
        dz1b = dz1.astype(BF16)
        dz1b_ref[...] = dz1b
        dcat = _dot_nt(dz1b, wo_ref[...])
        datt_ref[...] = dcat[:, :D_ATT].astype(BF16)
        drec_ref[...] = dcat[:, D_ATT:]
        _put_rows(accd_ref, [_row_sum(dh1 * xhat1), _row_sum(dh1)])

    next16 = pl.BlockSpec((16, D_FF), lambda i: (jnp.minimum((i + 1) * t16, n16 - 1), 0))
    return _pcall(
        body, (dgc, dgc, gate, dval, dh1p, z1, w_up_top, w_up_bot, fcw, w_out, ln1_g), name="up_bwd",
        grid=(nsteps,), sem="arbitrary", comm=comm,
        in_specs=[_rows(tb, D_FF), next16, _rows(tb, D_FF), _rows(tb, D_FF), _rows(tb, D_MODEL),
                  _rows(tb, D_MODEL), _ANY, _ANY, _resident((3, D_FF)),
                  _resident((D_MODEL, D_MODEL)), _resident((1, D_MODEL))],
        scratch_shapes=[pltpu.VMEM((nblk, 2 * kh, wblk), BF16), pltpu.SemaphoreType.DMA((2,))],
        out_specs=[_rows(tb, D_FF), _rows(tb, D_MODEL), _rows(tb, D_MODEL), _rows(tb, D_ATT), _rows(tb, D_RNN),
                   _acc((8, D_FF)), _acc((8, D_MODEL))],
        out_shape=[jax.ShapeDtypeStruct((S, D_FF), BF16), jax.ShapeDtypeStruct((S, D_MODEL), F32),
                   jax.ShapeDtypeStruct((S, D_MODEL), BF16), jax.ShapeDtypeStruct((S, D_ATT), BF16),
                   jax.ShapeDtypeStruct((S, D_RNN), F32), jax.ShapeDtypeStruct((8, D_FF), F32),
                   jax.ShapeDtypeStruct((8, D_MODEL), F32)])


def _attn_bwd(q, k, v, lse, do, sinks, comm=None):
    S = q.shape[0]
    grp = N_HEADS // N_KV
    nq = min(ATT_STEP, S // QBLK)

    def body(sink_ref, q_ref, kc_ref, kp_ref, vc_ref, vp_ref, do_ref, lse_ref, dq_ref, dkc_ref, dkp_ref, dvc_ref, dvp_ref,
             ds_ref):
        i = pl.program_id(0)

        @pl.when(i == 0)
        def _():
            ds_ref[...] = jnp.zeros_like(ds_ref)

        row8 = lax.broadcasted_iota(jnp.int32, (8, 128), 0)
        lane8 = lax.broadcasted_iota(jnp.int32, (8, 128), 1)
        dsink = jnp.zeros((8, 128), F32)
        kall = jnp.concatenate([kp_ref[...], kc_ref[...]], axis=0)
        vall = jnp.concatenate([vp_ref[...], vc_ref[...]], axis=0)
        dk_t = [jnp.zeros((D_KV, QBLK), F32) for _ in range(nq + 1)]
        dv_t = [jnp.zeros((D_KV, QBLK), F32) for _ in range(nq + 1)]
        for b in range(nq):
            valid = _band_mask(i * nq + b)
            rows = slice(b * QBLK, (b + 1) * QBLK)
            keys = slice(b * QBLK, (b + 2) * QBLK)
            qv, dov = q_ref[rows, :], do_ref[rows, :]
            dqs, dks, dvs = [], [], []
            for g in range(N_KV):
                kcat = kall[keys, g * HEAD_DIM:(g + 1) * HEAD_DIM]
                vcat = vall[keys, g * HEAD_DIM:(g + 1) * HEAD_DIM]
                q4, do4 = _stack_heads(qv, g), _stack_heads(dov, g)
                s = jnp.where(valid, _dot_nt(q4, kcat), -1e30)
                lse = lse_ref[(b * N_KV + g) * GROUP * QBLK:(b * N_KV + g + 1) * GROUP * QBLK, :]
                p = jnp.exp(s - lse)
                p_sink = jnp.exp(_sink_column(sink_ref, g) - lse)
                dp = _dot_nt(do4, vcat)
                delta = jnp.sum(p * dp, axis=1, keepdims=True)
                dsc = (p * (dp - delta)).astype(BF16)
                dqs += _unstack_heads(_dot(dsc, kcat) * (HEAD_DIM ** -0.5))
                dks.append(_dot_tn(q4, dsc))
                dvs.append(_dot_tn(do4, p.astype(BF16)))
                for hh, part in enumerate(_unstack_heads(-p_sink * delta)):
                    here = (row8 == 0) & (lane8 == g * grp + hh)
                    dsink = dsink + jnp.where(here, jnp.sum(part, axis=0, keepdims=True), 0.0)
            dq_ref[rows, :] = jnp.concatenate(dqs, axis=1).astype(BF16)
            dk2, dv2 = jnp.concatenate(dks, axis=0), jnp.concatenate(dvs, axis=0)
            dk_t[b], dk_t[b + 1] = dk_t[b] + dk2[:, :QBLK], dk_t[b + 1] + dk2[:, QBLK:]
            dv_t[b], dv_t[b + 1] = dv_t[b] + dv2[:, :QBLK], dv_t[b + 1] + dv2[:, QBLK:]
        dkp_ref[...] = dk_t[0].T
        dvp_ref[...] = dv_t[0].T
        for b in range(nq):
            dkc_ref[b * QBLK:(b + 1) * QBLK, :] = dk_t[b + 1].T
            dvc_ref[b * QBLK:(b + 1) * QBLK, :] = dv_t[b + 1].T
        ds_ref[...] += dsink

    nsteps = S // (nq * QBLK)
    cur = jax.ShapeDtypeStruct((S, D_KV), F32)
    prev = jax.ShapeDtypeStruct((nsteps * QBLK, D_KV), F32)
    big = _rows(nq * QBLK, D_ATT)
    return _pcall(
        body, (sinks, q, k, k, v, v, do, lse), name="attn_bwd", grid=(nsteps,), sem="arbitrary", comm=comm,
        in_specs=[pl.BlockSpec(memory_space=pltpu.SMEM), big] + _attn_specs(nq) + [big, _rows(nq * N_HEADS * QBLK, 1)],
        out_specs=[big, _rows(nq * QBLK, D_KV), _rows(QBLK, D_KV), _rows(nq * QBLK, D_KV), _rows(QBLK, D_KV),
                   _acc((8, 128))],
        out_shape=[jax.ShapeDtypeStruct((S, D_ATT), BF16), cur, prev, cur, prev, jax.ShapeDtypeStruct((8, 128), F32)])


def _rnn_bwd(xr, gr, h, kept, drec, conv_w, wa, wx, lam, comm=None):
    S = xr.shape[0]
    tb = min(512, S)
    t8 = tb // 8
    nsteps = S // tb

    def body(xr_ref, xp_ref, gr_ref, h_ref, hp_ref, xc_ref, r_ref, ig_ref, a_ref, f_ref, drec_ref, cw_ref, wa_ref, wx_ref,
             lam_ref, dxr_ref, dgr_ref, gwa_ref, gwx_ref, acc_ref, carry_s, dxc_halo_s, d_s, gwa_s, gwx_s):
        i = pl.program_id(0)
        blk = nsteps - 1 - i

        @pl.when(i == 0)
        def _():
            gwa_s[...] = jnp.zeros_like(gwa_s)
            gwx_s[...] = jnp.zeros_like(gwx_s)
            acc_ref[...] = jnp.zeros_like(acc_ref)
            carry_s[...] = jnp.zeros_like(carry_s)
            dxc_halo_s[...] = jnp.zeros_like(dxc_halo_s)

        x = xr_ref[...]
        xhalo = jnp.where(blk > 0, xp_ref[...], 0.0)
        cw = _w_rows(cw_ref)
        xs = [_shift_down(x, xhalo, 3), _shift_down(x, xhalo, 2), _shift_down(x, xhalo, 1), x]
        xc, r, ig, a, f = xc_ref[...], r_ref[...], ig_ref[...], a_ref[...], f_ref[...]
        sp = _softplus_neg(lam_ref[...])
        hcur = h_ref[...]
        hprev = _shift_down(hcur, jnp.where(blk > 0, hp_ref[...], 0.0), 1)
        gl, dgl = _gelu_and_grad(gr_ref[...])
        drec = drec_ref[...]
        dgr_ref[...] = (drec * hcur * dgl).astype(BF16)
        d_s[...] = drec * gl
        row8 = lax.broadcasted_iota(jnp.int32, (8, D_RNN), 0)

        def tile(t, c):
            o = pl.multiple_of((t8 - 1 - t) * 8, 8)
            a8 = a_ref[pl.ds(o, 8), :]
            dt = d_s[pl.ds(o, 8), :]
            at = jnp.where(row8 == 7, 1.0, pltpu.roll(a8, 7, 0))
            for s in (1, 2, 4):
                keep = row8 < 8 - s
                a_sh = jnp.where(keep, pltpu.roll(at, 8 - s, 0), 1.0)
                d_sh = jnp.where(keep, pltpu.roll(dt, 8 - s, 0), 0.0)
                dt = at * d_sh + dt
                at = at * a_sh
            lt = at * c + dt
            d_s[pl.ds(o, 8), :] = lt
            return _row_sum(jnp.where(row8 == 0, a8 * lt, 0.0))

        carry_s[0:1, :] = lax.fori_loop(0, t8, tile, carry_s[0:1, :], unroll=2)
        lmb = d_s[...]
        a2 = a * a
        dla = lmb * hprev * a - lmb * ig * xc * (a2 / f)
        di = lmb * f * xc
        dr = dla * (-LRU_C) * sp
        dpa = dr * r * (1.0 - r)
        dpx = di * ig * (1.0 - ig)
        dpab = dpa.astype(BF16)
        dpxb = dpx.astype(BF16)
        xcb = xc.astype(BF16)
        gwa_s[...] += _dot_tn(xcb, dpab)
        gwx_s[...] += _dot_tn(xcb, dpxb)

        @pl.when(i == nsteps - 1)
        def _():
            for dense, out in ((gwa_s[...], gwa_ref), (gwx_s[...], gwx_ref)):
                for b in range(RNN_BLOCKS):
                    rows = slice(b * HEAD_DIM, (b + 1) * HEAD_DIM)
                    out[rows, :] = dense[rows, b * HEAD_DIM:(b + 1) * HEAD_DIM]

        dxc = lmb * f * ig + _dot_nt(dpab, wa_ref[...]) + _dot_nt(dpxb, wx_ref[...])
        nxt = dxc_halo_s[...]
        dxr = cw[3] * dxc
        for s in (1, 2, 3):
            dxr = dxr + cw[3 - s] * _shift_up(dxc, nxt, s)
        dxr_ref[...] = dxr.astype(BF16)
        dxc_halo_s[...] = dxc[:8]
        dlam = _row_sum(dla * (-LRU_C) * r) * (-1.0 / (1.0 + jnp.exp(lam_ref[...])))
        _put_rows(acc_ref, [_row_sum(dxc * xs[0]), _row_sum(dxc * xs[1]), _row_sum(dxc * xs[2]), _row_sum(dxc * xs[3]),
                            _row_sum(dxc), _row_sum(dpa), _row_sum(dpx), dlam])

    rev = lambda i: (nsteps - 1 - i, 0)
    prev8 = lambda i: (jnp.maximum((nsteps - 1 - i) * t8 - 1, 0), 0)
    blkspec = pl.BlockSpec((tb, D_RNN), rev)
    halo8 = pl.BlockSpec((8, D_RNN), prev8)
    vec = _resident((1, D_RNN))
    return _pcall(
        body, (xr, xr, gr, h, h, *kept, drec, conv_w, wa, wx, lam), name="rnn_bwd", grid=(nsteps,),
        sem="arbitrary", comm=comm,
        in_specs=[blkspec, halo8, blkspec, blkspec, halo8] + [blkspec] * 6
        + [_resident((4, D_RNN)), _resident((D_RNN, D_RNN)), _resident((D_RNN, D_RNN)), vec],
        out_specs=[blkspec, blkspec, _acc((D_RNN, HEAD_DIM)), _acc((D_RNN, HEAD_DIM)), _acc((8, D_RNN))],
        out_shape=[jax.ShapeDtypeStruct((S, D_RNN), BF16), jax.ShapeDtypeStruct((S, D_RNN), BF16),
                   jax.ShapeDtypeStruct((D_RNN, HEAD_DIM), F32), jax.ShapeDtypeStruct((D_RNN, HEAD_DIM), F32),
                   jax.ShapeDtypeStruct((8, D_RNN), F32)],
        scratch_shapes=[pltpu.VMEM((8, D_RNN), F32), pltpu.VMEM((8, D_RNN), F32), pltpu.VMEM((tb, D_RNN), F32),
                        pltpu.VMEM((D_RNN, D_RNN), F32), pltpu.VMEM((D_RNN, D_RNN), F32)])


def _in_bwd(dq, dkc, dkp, dvc, dvp, dxr, dgr, dz1, w_in, comm=None):
    S = dz1.shape[0]
    tb = min(ATT_STEP * QBLK, S)
    nsteps = S // tb

    def body(dq_ref, dkc_ref, dkn_ref, dvc_ref, dvn_ref, dxr_ref, dgr_ref, dz1_ref, w_ref, dkv_ref, dx_ref):
        last = pl.program_id(0) == nsteps - 1

        def total(cur_ref, next_ref):
            nxt = jnp.where(last, 0.0, next_ref[...])
            tail = cur_ref[tb - QBLK:, :] + nxt
            return jnp.concatenate([cur_ref[:tb - QBLK, :], tail], axis=0) if tb > QBLK else tail

        dkv = jnp.concatenate([total(dkc_ref, dkn_ref), total(dvc_ref, dvn_ref)], axis=1).astype(BF16)
        dkv_ref[...] = dkv
        du = jnp.concatenate([dq_ref[...], dkv, dxr_ref[...], dgr_ref[...]], axis=1)
        dx_ref[...] = ALPHA * dz1_ref[...] + _dot(du, w_ref[...])

    nextp = pl.BlockSpec((QBLK, D_KV), lambda i: (jnp.minimum(i + 1, nsteps - 1), 0))
    return _pcall(
        body, (dq, dkc, dkp, dvc, dvp, dxr, dgr, dz1, w_in), name="in_bwd", grid=(nsteps,), comm=comm,
        in_specs=[_rows(tb, D_ATT), _rows(tb, D_KV), nextp, _rows(tb, D_KV), nextp,
                  _rows(tb, D_RNN), _rows(tb, D_RNN), _rows(tb, D_MODEL), _resident((D_IN, D_MODEL))],
        out_specs=[_rows(tb, 2 * D_KV), _rows(tb, D_MODEL)],
        out_shape=[jax.ShapeDtypeStruct((S, 2 * D_KV), BF16), jax.ShapeDtypeStruct((S, D_MODEL), F32)])


def _block_diag(w):
    eye = jnp.eye(RNN_BLOCKS, dtype=w.dtype)
    return (w[:, :, None, :] * eye[:, None, :, None]).reshape(D_RNN, D_RNN).astype(BF16)


def _adamw(w, g, m, v):
    m = ADAM_B1 * m + (1.0 - ADAM_B1) * g
    v = ADAM_B2 * v + (1.0 - ADAM_B2) * (g * g)
    m_hat = m / (1.0 - ADAM_B1 ** ADAM_STEP)
    v_hat = v / (1.0 - ADAM_B2 ** ADAM_STEP)
    delta = -ADAM_LR * (m_hat / (jnp.sqrt(v_hat) + ADAM_EPS) + ADAM_WD * w)
    return delta, m, v


def _sum_adamw(parts, w, m, v, name):
    parts = parts if isinstance(parts, (list, tuple)) else [parts]
    R, C = w.shape
    rb = R if R <= 256 else (256 if parts[0].shape[1] % 256 == 0 else 128)
    per = parts[0].shape[1] // rb
    assert R % rb == 0 and parts[0].shape[1] % rb == 0
    n = len(parts)

    def body(*refs):
        p_refs = refs[:n]
        w_ref, m_ref, v_ref, g_out, d_out, m_out, v_out = refs[n:]
        which = pl.program_id(0) // per

        def total(p_ref):
            g = p_ref[0].astype(F32)
            for d in range(1, N_DEV):
                g = g + p_ref[d].astype(F32)
            return g

        g = total(p_refs[0])
        for j in range(1, n):
            g = jnp.where(which == j, total(p_refs[j]), g)
        delta, mn, vn = _adamw(w_ref[...], g, m_ref[...], v_ref[...])
        g_out[...] = g
        d_out[...] = delta
        m_out[...] = mn
        v_out[...] = vn

    def part_spec(j):
        return pl.BlockSpec((N_DEV, rb, C), lambda i: (0, jnp.clip(i - j * per, 0, per - 1), 0))

    blk = _rows(rb, C)
    out = jax.ShapeDtypeStruct((R, C), F32)
    return pl.pallas_call(
        body, name=name, grid=(R // rb,),
        in_specs=[part_spec(j) for j in range(n)] + [blk, blk, blk],
        out_specs=[blk, blk, blk, blk], out_shape=[out, out, out, out],
        compiler_params=_params("parallel"),
    )(*parts, w, m, v)


def _sum_adamw_t(col_parts, w, m, v, name):
    R, C = w.shape
    n = len(col_parts)
    c_pad = -C % 128

    def body(*refs):
        p_refs = refs[:n]
        w_ref, m_ref, v_ref, g_out, d_out, m_out, v_out = refs[n:]

        def total(p_ref):
            g = p_ref[0].astype(F32)
            for d in range(1, N_DEV):
                g = g + p_ref[d].astype(F32)
            return g

        g_t = jnp.concatenate([total(p_ref) for p_ref in p_refs], axis=1)
        if c_pad:
            g_t = jnp.concatenate([g_t, jnp.zeros((c_pad, R), F32)], axis=0)
        g = g_t.T[:, :C]
        delta, mn, vn = _adamw(w_ref[...], g, m_ref[...], v_ref[...])
        g_out[...] = g
        d_out[...] = delta
        m_out[...] = mn
        v_out[...] = vn

    out = jax.ShapeDtypeStruct((R, C), F32)
    return pl.pallas_call(body, name=name, out_shape=[out, out, out, out], compiler_params=_params())(*col_parts, w, m, v)


_SMALL = [("attn_sinks", "s", 0, 1, None), ("rnn_conv_w", "r", 0, 4, "cols"), ("rnn_conv_b", "r", 4, 1, None),
          ("gate_a_w", "a", 0, D_RNN, None), ("gate_a_b", "r", 5, 1, None), ("gate_x_w", "x", 0, D_RNN, None),
          ("gate_x_b", "r", 6, 1, None), ("lru_lambda", "r", 7, 1, None), ("ln1_g", "d", 0, 1, None),
          ("ln1_b", "d", 1, 1, None), ("ffn_conv_w", "f", 0, 3, "cols"), ("ffn_conv_b", "f", 3, 1, None),
          ("ple_gate_b", "t", 2, 1, None), ("ln2_g", "t", 0, 1, None), ("ln2_b", "t", 1, 1, None)]
_LOSS_ROW = 3


_ACC_COLS = {"t": (0, D_MODEL), "f": (D_MODEL, D_FF), "d": (D_MODEL + D_FF, D_MODEL), "s": (2 * D_MODEL + D_FF, 128),
             "r": (2 * D_MODEL + D_FF + 128, D_RNN)}
_ACC_WIDTH = 2 * D_MODEL + D_FF + 128 + D_RNN


def _small_update(rows_all, gates_all, params):
    flat = [arr for triple in params for arr in triple]
    n_par = len(_SMALL)

    def body(*refs):
        rows_ref, gates_ref = refs[:2]
        p_refs = refs[2:2 + 3 * n_par]
        loss_ref = refs[2 + 3 * n_par]
        o_refs = refs[3 + 3 * n_par:3 + 7 * n_par]
        rows_s, tmp_r, tmp_f = refs[3 + 7 * n_par:]
        me = _dev_index(*_place())
        rows_sum, gates_sum = rows_ref[0], gates_ref[0]
        for d in range(1, N_DEV):
            rows_sum = rows_sum + rows_ref[d]
            gates_sum = gates_sum + gates_ref[d]
        rows_s[...] = rows_sum
        t0 = _ACC_COLS["t"][0]
        loss_ref[...] = rows_s[_LOSS_ROW:_LOSS_ROW + 1, t0:t0 + 128]
        for i, (name, key, row, rows, how) in enumerate(_SMALL):
            w_ref, m_ref, v_ref = p_refs[3 * i:3 * i + 3]
            g_out, d_out, m_out, v_out = o_refs[4 * i:4 * i + 4]
            if key == "a":
                g = gates_sum[:, :HEAD_DIM]
            elif key == "x":
                g = gates_sum[:, HEAD_DIM:]
            elif how == "cols":
                c0, width = _ACC_COLS[key]
                full = rows_s[:, c0:c0 + width]
                shard = width // N_DEV
                mine = full[:, :shard]
                for d in range(1, N_DEV):
                    mine = jnp.where(me == d, full[:, d * shard:(d + 1) * shard], mine)
                tmp = tmp_r if key == "r" else tmp_f
                tmp[...] = mine
                g = tmp[row:row + rows, :]
            else:
                c0, width = _ACC_COLS[key]
                g = rows_s[row:row + rows, c0:c0 + width][:, :w_ref.shape[1]]
            delta, mn, vn = _adamw(w_ref[...], g, m_ref[...], v_ref[...])
            g_out[...] = g
            d_out[...] = delta
            m_out[...] = mn
            v_out[...] = vn

    outs = [jax.ShapeDtypeStruct((1, 128), F32)]
    for w, _, _ in params:
        outs += [jax.ShapeDtypeStruct(w.shape, F32)] * 4
    scratch = [pltpu.VMEM((8, _ACC_WIDTH), F32), pltpu.VMEM((8, D_RNN // N_DEV), F32), pltpu.VMEM((8, D_FF // N_DEV), F32)]
    res = pl.pallas_call(body, name="small_update", out_shape=outs, scratch_shapes=scratch)(rows_all, gates_all, *flat)
    return res[0], [res[1 + 4 * i:5 + 4 * i] for i in range(n_par)]


def kernel(x, p, w_in, attn_sinks, rnn_conv_w, rnn_conv_b, gate_a_w, gate_a_b, gate_x_w, gate_x_b, lru_lambda, w_out, ln1_g, ln1_b, w_ffn_up, ffn_conv_w, ffn_conv_b, w_ffn_down, ple_gate_w, ple_gate_b, ple_proj, ln2_g, ln2_b, loss_target, m_w_in, m_attn_sinks, m_rnn_conv_w, m_rnn_conv_b, m_gate_a_w, m_gate_a_b, m_gate_x_w, m_gate_x_b, m_lru_lambda, m_w_out, m_ln1_g, m_ln1_b, m_w_ffn_up, m_ffn_conv_w, m_ffn_conv_b, m_w_ffn_down, m_ple_gate_w, m_ple_gate_b, m_ple_proj, m_ln2_g, m_ln2_b, v_w_in, v_attn_sinks, v_rnn_conv_w, v_rnn_conv_b, v_gate_a_w, v_gate_a_b, v_gate_x_w, v_gate_x_b, v_lru_lambda, v_w_out, v_ln1_g, v_ln1_b, v_w_ffn_up, v_ffn_conv_w, v_ffn_conv_b, v_w_ffn_down, v_ple_gate_w, v_ple_gate_b, v_ple_proj, v_ln2_g, v_ln2_b):
    from_col_blocks = lambda g: g.transpose(1, 0, 2).reshape(g.shape[1], N_DEV * g.shape[2])

    xs, ps, tgt, sinks = x[0], p[0, 0], loss_target[0], attn_sinks[0]
    wa, wx = _block_diag(gate_a_w[0]), _block_diag(gate_x_w[0])

    conv_cols = jnp.concatenate([rnn_conv_w[0].reshape(1, -1), ffn_conv_w[0].reshape(1, -1)], axis=1)
    n_rc, n_fc = 4 * D_RNN // N_DEV, 3 * D_FF // N_DEV
    ((g_in,),) = _comm_call([_Gather([w_in[0].T.astype(BF16)])], "gather_w_in")
    w_in_full = g_in.reshape(D_IN, D_MODEL)

    (q, k, v, xr, gr), _ = _in_proj(xs, w_in_full)
    w_up_shard = w_ffn_up[0].astype(BF16)
    (att, lse), (g_out, w_up_top, g_conv) = _attn_fwd(
        q, k, v, sinks,
        comm=_Multi([_Gather([w_out[0].astype(BF16), w_up_shard[:D_MODEL // 2]]),
                     _Bcast([jnp.broadcast_to(conv_cols, (8, n_rc + n_fc))])]))
    rcw = from_col_blocks(g_conv[:, 0, :n_rc].reshape(N_DEV, 4, D_RNN // N_DEV))
    fcw = from_col_blocks(g_conv[:, 0, n_rc:].reshape(N_DEV, 3, D_FF // N_DEV))
    (rec, h, *kept), (w_up_bot,) = _rnn_fwd(xr, gr, rcw, rnn_conv_b, wa, wx, gate_a_b, gate_x_b, lru_lambda,
                                            comm=_Gather([w_up_shard[D_MODEL // 2:]]))
    w_out_full = g_out.reshape(D_MODEL, D_MODEL)
    (z1, h1b, gate, act, gl, vdgl), (g_down, g_pg, g_pp) = _mix_ln1_up(
        xs, att, rec, w_out_full, ln1_g, ln1_b, w_up_top, w_up_bot, fcw, ffn_conv_b,
        comm=_Gather([w_ffn_down[0].astype(BF16), ple_gate_w[0].astype(BF16), ple_proj[0].astype(BF16)]))
    dz2b, dpreb, dppb, dgc, dval, dh1p, acc_t = _tail(
        act, gl, vdgl, z1, h1b, ps, tgt, g_down.reshape(D_FF, D_MODEL), g_pg.reshape(D_MODEL, D_MODEL), ple_gate_b,
        from_col_blocks(g_pp), ln1_g, ln1_b, ln2_g, ln2_b)

    gd_down = _weight_grad([dz2b], [act], "down_grad", "rows_t", ts=1024)
    gd_pg = _weight_grad([h1b], [dpreb], "pg_grad", "rows", ts=1024)
    gd_pp = _weight_grad([ps], [dppb], "pp_grad", "cols", ts=1024)
    (dgate, dz1, dz1b, datt, drec, acc_f, acc_d), (r_down, r_pg, r_pp) = _up_bwd(
        dgc, gate, dval, dh1p, z1, w_up_top, w_up_bot, fcw, w_out_full, ln1_g, comm=_Exchange([gd_down, gd_pg, gd_pp]))
    gd_up_top, gd_up_bot = _weight_grad([h1b], [dgate, dval], "up_grad", "cols", halves=True)
    gd_out = _weight_grad([att, rec], [dz1b], "out_grad", "rows", ts=1024)
    (dq, dkc, dkp, dvc, dvp, acc_s), (r_up_top,) = _attn_bwd(q, k, v, lse, datt, sinks, comm=_Exchange([gd_up_top]))
    early = jnp.concatenate([acc_t, acc_f, acc_d], axis=1)
    (dxr, dgr, g_wa, g_wx, acc_r), (r_up_bot, r_out, early_all) = _rnn_bwd(
        xr, gr, h, kept, drec, rcw, wa, wx, lru_lambda, comm=_Multi([_Exchange([gd_up_bot, gd_out]), _Bcast([early])]))
    (dkv, dx), _ = _in_bwd(dq, dkc, dkp, dvc, dvp, dxr, dgr, dz1, w_in_full)
    du_parts = [dq, dkv, dxr, dgr]
    lanes = D_RNN // 128
    late = jnp.concatenate([g_wa, g_wx], axis=1)
    late = jnp.concatenate([late, acc_s, acc_r.reshape(8, lanes, 128).transpose(1, 0, 2).reshape(8 * lanes, 128)], axis=0)
    width = D_MODEL // IN_GRAD_PARTS
    comm, r_parts = _Gather([late]), []
    for part in range(IN_GRAD_PARTS):
        gd_part, got = _weight_grad(du_parts, [xs], f"in_grad_{part}", "rows", ts=1024, b_window=(part, width), comm=comm)
        if part == 0:
            (late_all,) = got
        else:
            r_parts += got
        comm = _Exchange([gd_part])
    r_parts += _comm_call([comm], "exchange_w_in")[0]
    acc_r_all = late_all[:, D_RNN + 8:].reshape(N_DEV, lanes, 8, 128).transpose(0, 2, 1, 3).reshape(N_DEV, 8, D_RNN)
    small_parts = (jnp.concatenate([early_all, late_all[:, D_RNN:D_RNN + 8], acc_r_all], axis=2),
                   late_all[:, :D_RNN])

    outs = {}
    res = _sum_adamw_t(r_parts, w_in[0], m_w_in[0], v_w_in[0], "adamw_w_in")
    outs["w_in"] = [r[None] for r in res]
    for name, parts, w, m, v in [("w_out", r_out, w_out, m_w_out, v_w_out),
                                 ("w_ffn_up", [r_up_top, r_up_bot], w_ffn_up, m_w_ffn_up, v_w_ffn_up),
                                 ("w_ffn_down", r_down, w_ffn_down, m_w_ffn_down, v_w_ffn_down),
                                 ("ple_gate_w", r_pg, ple_gate_w, m_ple_gate_w, v_ple_gate_w),
                                 ("ple_proj", r_pp, ple_proj, m_ple_proj, v_ple_proj)]:
        res = _sum_adamw(parts, w[0], m[0], v[0], "adamw_" + name)
        outs[name] = [r[None] for r in res]

    given = dict(attn_sinks=(attn_sinks, m_attn_sinks, v_attn_sinks), rnn_conv_w=(rnn_conv_w, m_rnn_conv_w, v_rnn_conv_w),
                 rnn_conv_b=(rnn_conv_b, m_rnn_conv_b, v_rnn_conv_b), gate_a_w=(gate_a_w, m_gate_a_w, v_gate_a_w),
                 gate_a_b=(gate_a_b, m_gate_a_b, v_gate_a_b), gate_x_w=(gate_x_w, m_gate_x_w, v_gate_x_w),
                 gate_x_b=(gate_x_b, m_gate_x_b, v_gate_x_b), lru_lambda=(lru_lambda, m_lru_lambda, v_lru_lambda),
                 ln1_g=(ln1_g, m_ln1_g, v_ln1_g), ln1_b=(ln1_b, m_ln1_b, v_ln1_b),
                 ffn_conv_w=(ffn_conv_w, m_ffn_conv_w, v_ffn_conv_w), ffn_conv_b=(ffn_conv_b, m_ffn_conv_b, v_ffn_conv_b),
                 ple_gate_b=(ple_gate_b, m_ple_gate_b, v_ple_gate_b), ln2_g=(ln2_g, m_ln2_g, v_ln2_g),
                 ln2_b=(ln2_b, m_ln2_b, v_ln2_b))
    as_2d = lambda a: a.reshape(-1, a.shape[-1])
    loss_row, small_res = _small_update(*small_parts, [tuple(as_2d(a) for a in given[n]) for n, *_ in _SMALL])
    loss = loss_row[0, 0]
    for (n, *_), res in zip(_SMALL, small_res):
        outs[n] = [r.reshape(given[n][0].shape) for r in res]

    order = ["w_in", "attn_sinks", "rnn_conv_w", "rnn_conv_b", "gate_a_w", "gate_a_b", "gate_x_w", "gate_x_b",
             "lru_lambda", "w_out", "ln1_g", "ln1_b", "w_ffn_up", "ffn_conv_w", "ffn_conv_b", "w_ffn_down",
             "ple_gate_w", "ple_gate_b", "ple_proj", "ln2_g", "ln2_b"]
    return (loss, dx[None], *[outs[n][0] for n in order], *[outs[n][1] for n in order],
            *[outs[n][2] for n in order], *[outs[n][3] for n in order])
```

```python
import jax
import jax.numpy as jnp
from jax import lax
from jax.experimental import pallas as pl
from jax.experimental.pallas import tpu as pltpu

F32 = jnp.float32
BF16 = jnp.bfloat16

D_MODEL = 1024
D_ATT = 512
D_KV = 128
HEAD_DIM = 64
N_HEADS = 8
N_KV = 2
D_RNN = 512
RNN_BLOCKS = 8
D_IN = 1792
D_FF = 3072
PLE_DIM = 256
QBLK = 128
N_DEV = 8
ALPHA = float(2 ** 0.25)
LN_EPS = 1e-5
LRU_C = 8.0
ADAM_LR, ADAM_B1, ADAM_B2, ADAM_EPS, ADAM_WD, ADAM_STEP = 0.001, 0.9, 0.999, 1e-08, 0.01, 10

V7X_VMEM_LIMIT = 56 * 1024 * 1024
MESH = pl.DeviceIdType.MESH


def _params(*sem, vmem=V7X_VMEM_LIMIT):
    return pltpu.CompilerParams(dimension_semantics=sem or None, vmem_limit_bytes=vmem)


def _resident(shape):
    return pl.BlockSpec(shape, lambda *_: (0,) * len(shape), pipeline_mode=pl.Buffered(1))


def _rows(tb, cols):
    return pl.BlockSpec((tb, cols), lambda i: (i, 0))


def _acc(shape):
    return pl.BlockSpec(shape, lambda *_: (0,) * len(shape))


def _dot(a, b):
    return jnp.dot(a, b, preferred_element_type=F32)


def _dot_nt(a, b):
    return lax.dot_general(a, b, (((1,), (1,)), ((), ())), preferred_element_type=F32)


def _dot_tn(a, b):
    return lax.dot_general(a, b, (((0,), (0,)), ((), ())), preferred_element_type=F32)


def _sigmoid(x):
    return 1.0 / (1.0 + jnp.exp(-x))


_GELU_C = 0.7978845608028654
_GELU_K = 0.044715


def _gelu_and_grad(x):
    u = x * x
    t = jnp.tanh(x * (_GELU_C + (_GELU_C * _GELU_K) * u))
    hp = 0.5 + 0.5 * t
    dg = hp + x * (0.5 - 0.5 * (t * t)) * (_GELU_C + (3.0 * _GELU_C * _GELU_K) * u)
    return x * hp, dg


def _gelu(x):
    return 0.5 * x * (1.0 + jnp.tanh(_GELU_C * (x + _GELU_K * x * x * x)))


def _ln_stats(z):
    mu = jnp.mean(z, axis=-1, keepdims=True)
    zc = z - mu
    var = jnp.mean(zc * zc, axis=-1, keepdims=True)
    rstd = lax.rsqrt(var + LN_EPS)
    return zc * rstd, rstd


def _ln_bwd(dy, xhat, rstd, g):
    dxh = dy * g
    m1 = jnp.mean(dxh, axis=-1, keepdims=True)
    m2 = jnp.mean(dxh * xhat, axis=-1, keepdims=True)
    return rstd * (dxh - m1 - xhat * m2)


def _softplus_neg(lam):
    u = jnp.exp(-jnp.abs(lam))
    w = 1.0 + u
    d = w - 1.0
    log1p_u = jnp.where(d == 0.0, u, jnp.log(w) * (u / jnp.where(d == 0.0, 1.0, d)))
    return jnp.maximum(-lam, 0.0) + log1p_u


def _shift_down(x, halo, s):
    xs = pltpu.roll(x, s, 0)
    hs = pltpu.roll(halo, s, 0)
    row8 = lax.broadcasted_iota(jnp.int32, hs.shape, 0)
    first = jnp.where(row8 < s, hs, xs[:8])
    return jnp.concatenate([first, xs[8:]], axis=0)


def _shift_up(x, halo, s):
    n = x.shape[0]
    xs = pltpu.roll(x, n - s, 0)
    hs = pltpu.roll(halo, 8 - s, 0)
    row8 = lax.broadcasted_iota(jnp.int32, hs.shape, 0)
    last = jnp.where(row8 >= 8 - s, hs, xs[n - 8:])
    return jnp.concatenate([xs[:n - 8], last], axis=0)


def _row_sum(x):
    return jnp.sum(x, axis=0, keepdims=True)


def _put_rows(acc_ref, rows):
    row8 = lax.broadcasted_iota(jnp.int32, acc_ref.shape, 0)
    upd = jnp.zeros(acc_ref.shape, F32)
    for r, vec in enumerate(rows):
        upd = jnp.where(row8 == r, vec, upd)
    acc_ref[...] += upd


def _place():
    return lax.axis_index("x"), lax.axis_index("y"), lax.axis_index("c")


def _dev_index(px, py, pc):
    return 4 * px + 2 * py + pc


_ANY = pl.BlockSpec(memory_space=pl.ANY)


class _Gather:
    def __init__(self, arrays):
        self.arrays = list(arrays)
        self.n = len(self.arrays)

    def out_shape(self):
        return [jax.ShapeDtypeStruct((N_DEV,) + s.shape, s.dtype) for s in self.arrays]

    def scratch(self):
        return [pltpu.SemaphoreType.DMA((self.n, 7)), pltpu.SemaphoreType.DMA((self.n, 7)),
                pltpu.SemaphoreType.DMA((self.n,))]

    def _parts(self, ins, outs, sems):
        send_sems, recv_sems, local_sems = sems
        x, y, c = _place()
        me, sibling = (x, y, c), (x, y, 1 - c)
        chips = [(1 - x, y), (x, 1 - y), (1 - x, 1 - y)]

        def copy(a, k, block, to, src=None):
            rows = outs[a].at[_dev_index(*block)]
            return pltpu.make_async_remote_copy(
                src_ref=rows if src is None else src, dst_ref=rows, send_sem=send_sems.at[a, k],
                recv_sem=recv_sems.at[a, k], device_id=to, device_id_type=MESH)

        rng = range(self.n)
        mine = [pltpu.make_async_copy(ins[a], outs[a].at[_dev_index(*me)], local_sems.at[a]) for a in rng]
        first = [copy(a, 0, me, sibling, src=ins[a]) for a in rng]
        first += [copy(a, 1 + j, me, (*chip, c), src=ins[a]) for j, chip in enumerate(chips) for a in rng]
        landed = [copy(a, 1 + j, (*chip, c), me) for j, chip in enumerate(chips) for a in rng]
        passed = [copy(a, 4 + j, (*chip, c), sibling) for j, chip in enumerate(chips) for a in rng]
        from_sibling = [copy(a, 0, sibling, me) for a in rng]
        from_sibling += [copy(a, 4 + j, (*chip, 1 - c), me) for j, chip in enumerate(chips) for a in rng]
        return mine, first, landed, passed, from_sibling

    def start(self, ins, outs, sems):
        mine, first, _, _, _ = self._parts(ins, outs, sems)
        for cp in mine + first:
            cp.start()

    def forward(self, ins, outs, sems):
        _, _, landed, passed, _ = self._parts(ins, outs, sems)
        for got, fwd in zip(landed, passed):
            got.wait_recv()
            fwd.start()

    def finish(self, ins, outs, sems):
        mine, first, _, passed, from_sibling = self._parts(ins, outs, sems)
        for cp in from_sibling:
            cp.wait_recv()
        for cp in first + passed:
            cp.wait_send()
        for cp in mine:
            cp.wait()

    def before(self, ins, outs, sems, step, nsteps):
        pl.when(step == 0)(lambda: self.start(ins, outs, sems))
        pl.when(step == (7 * nsteps) // 8)(lambda: self.forward(ins, outs, sems))

    def after(self, ins, outs, sems, step, nsteps):
        pl.when(step == nsteps - 1)(lambda: self.finish(ins, outs, sems))


class _Exchange:
    def __init__(self, arrays):
        self.arrays = list(arrays)
        self.n = len(self.arrays)

    def out_shape(self):
        return [jax.ShapeDtypeStruct(b.shape, b.dtype) for b in self.arrays]

    def scratch(self):
        return [pltpu.SemaphoreType.DMA((self.n, 7)), pltpu.SemaphoreType.DMA((self.n, 7)),
                pltpu.SemaphoreType.DMA((self.n,))]

    def _parts(self, ins, outs, sems):
        send_sems, recv_sems, local_sems = sems
        x, y, c = _place()
        me = _dev_index(x, y, c)
        peers = [(x ^ (k >> 2), y ^ ((k >> 1) & 1), c ^ (k & 1)) for k in range(1, N_DEV)]
        rng = range(self.n)
        mine = [pltpu.make_async_copy(ins[a].at[me], outs[a].at[me], local_sems.at[a]) for a in rng]
        sent = [pltpu.make_async_remote_copy(
            src_ref=ins[a].at[_dev_index(*to)], dst_ref=outs[a].at[me], send_sem=send_sems.at[a, k],
            recv_sem=recv_sems.at[a, k], device_id=to, device_id_type=MESH) for k, to in enumerate(peers) for a in rng]
        arrivals = [pltpu.make_async_remote_copy(
            src_ref=ins[a].at[me], dst_ref=outs[a].at[_dev_index(*frm)], send_sem=send_sems.at[a, k],
            recv_sem=recv_sems.at[a, k], device_id=frm, device_id_type=MESH) for k, frm in enumerate(peers) for a in rng]
        return mine, sent, arrivals

    def start(self, ins, outs, sems):
        mine, sent, _ = self._parts(ins, outs, sems)
        for cp in mine + sent:
            cp.start()

    def finish(self, ins, outs, sems):
        mine, sent, arrivals = self._parts(ins, outs, sems)
        for cp in arrivals:
            cp.wait_recv()
        for cp in sent:
            cp.wait_send()
        for cp in mine:
            cp.wait()

    def before(self, ins, outs, sems, step, nsteps):
        pl.when(step == 0)(lambda: self.start(ins, outs, sems))

    def after(self, ins, outs, sems, step, nsteps):
        pl.when(step == nsteps - 1)(lambda: self.finish(ins, outs, sems))


class _Bcast(_Exchange):
    def out_shape(self):
        return [jax.ShapeDtypeStruct((N_DEV,) + s.shape, s.dtype) for s in self.arrays]

    def _parts(self, ins, outs, sems):
        send_sems, recv_sems, local_sems = sems
        x, y, c = _place()
        me = _dev_index(x, y, c)
        peers = [(x ^ (k >> 2), y ^ ((k >> 1) & 1), c ^ (k & 1)) for k in range(1, N_DEV)]
        rng = range(self.n)
        mine = [pltpu.make_async_copy(ins[a], outs[a].at[me], local_sems.at[a]) for a in rng]
        sent = [pltpu.make_async_remote_copy(
            src_ref=ins[a], dst_ref=outs[a].at[me], send_sem=send_sems.at[a, k], recv_sem=recv_sems.at[a, k],
            device_id=to, device_id_type=MESH) for k, to in enumerate(peers) for a in rng]
        arrivals = [pltpu.make_async_remote_copy(
            src_ref=ins[a], dst_ref=outs[a].at[_dev_index(*frm)], send_sem=send_sems.at[a, k],
            recv_sem=recv_sems.at[a, k], device_id=frm, device_id_type=MESH) for k, frm in enumerate(peers) for a in rng]
        return mine, sent, arrivals


class _Multi:
    def __init__(self, comms):
        self.comms = list(comms)
        self.arrays = [arr for c in self.comms for arr in c.arrays]
        self.n = len(self.arrays)

    def out_shape(self):
        return [s for c in self.comms for s in c.out_shape()]

    def scratch(self):
        return [s for c in self.comms for s in c.scratch()]

    def _each(self, ins, outs, sems):
        a = 0
        for j, c in enumerate(self.comms):
            yield c, ins[a:a + c.n], outs[a:a + c.n], sems[3 * j:3 * j + 3]
            a += c.n

    def before(self, ins, outs, sems, step, nsteps):
        for c, ci, co, cs in self._each(ins, outs, sems):
            c.before(ci, co, cs, step, nsteps)

    def after(self, ins, outs, sems, step, nsteps):
        for c, ci, co, cs in self._each(ins, outs, sems):
            c.after(ci, co, cs, step, nsteps)


def _comm_call(comms, name):
    ns = [c.n for c in comms]
    n = sum(ns)

    def body(*refs):
        parts, a, s = [], 0, 2 * n
        for c in comms:
            parts.append((c, refs[a:a + c.n], refs[n + a:n + a + c.n], refs[s:s + 3]))
            a, s = a + c.n, s + 3
        for c, ins, outs, sems in parts:
            c.start(ins, outs, sems)
        for c, ins, outs, sems in parts:
            if isinstance(c, _Gather):
                c.forward(ins, outs, sems)
        for c, ins, outs, sems in parts:
            c.finish(ins, outs, sems)

    res = pl.pallas_call(
        body, name=name, in_specs=[_ANY] * n, out_specs=[_ANY] * n,
        out_shape=[s for c in comms for s in c.out_shape()], scratch_shapes=[s for c in comms for s in c.scratch()],
    )(*[arr for c in comms for arr in c.arrays])
    out, a = [], 0
    for k in ns:
        out.append(res[a:a + k])
        a += k
    return out


def _pcall(body, args, *, name, grid, in_specs, out_specs, out_shape, scratch_shapes=(), sem="parallel", comm=None,
           step_axis=0):
    sem = (sem,) * len(grid) if isinstance(sem, str) else sem
    if comm is None:
        res = pl.pallas_call(body, name=name, grid=grid, in_specs=in_specs, out_specs=out_specs, out_shape=out_shape,
                             scratch_shapes=list(scratch_shapes), compiler_params=_params(*sem))(*args)
        return res, []
    n_in, n_out, n_scr, n = len(in_specs), len(out_specs), len(scratch_shapes), comm.n
    nsteps = grid[step_axis]
    assert all(g == 1 for ax, g in enumerate(grid) if ax != step_axis)

    def hosted(*refs):
        ins, cin = refs[:n_in], refs[n_in:n_in + n]
        o0 = n_in + n
        outs, cout = refs[o0:o0 + n_out], refs[o0 + n_out:o0 + n_out + n]
        s0 = o0 + n_out + n
        scr, sems = refs[s0:s0 + n_scr], refs[s0 + n_scr:]
        step = pl.program_id(step_axis)
        comm.before(cin, cout, sems, step, nsteps)
        body(*ins, *outs, *scr)
        comm.after(cin, cout, sems, step, nsteps)

    res = pl.pallas_call(
        hosted, name=name, grid=grid, in_specs=list(in_specs) + [_ANY] * n, out_specs=list(out_specs) + [_ANY] * n,
        out_shape=list(out_shape) + comm.out_shape(), scratch_shapes=list(scratch_shapes) + comm.scratch(),
        compiler_params=_params(*(("arbitrary",) * len(grid))))(*args, *comm.arrays)
    return res[:n_out], res[n_out:]


def _load_row_halves(top_hbm, bot_hbm, full_s, sems):
    r = top_hbm.shape[1]
    copies = [pltpu.make_async_copy(top_hbm, full_s.at[:, :r, :], sems.at[0]),
              pltpu.make_async_copy(bot_hbm, full_s.at[:, r:, :], sems.at[1])]
    for cp in copies:
        cp.start()
    for cp in copies:
        cp.wait()


def _in_proj(x, w_in_t, comm=None):
    S = x.shape[0]
    tb = min(1024, S)

    def body(x_ref, w_ref, q_ref, k_ref, v_ref, xr_ref, gr_ref):
        u = _dot_nt(x_ref[...].astype(BF16), w_ref[...])
        q_ref[...] = (u[:, :D_ATT] * (HEAD_DIM ** -0.5)).astype(BF16)
        k_ref[...] = u[:, D_ATT:D_ATT + D_KV].astype(BF16)
        v_ref[...] = u[:, D_ATT + D_KV:D_ATT + 2 * D_KV].astype(BF16)
        xr_ref[...] = u[:, D_ATT + 2 * D_KV:D_ATT + 2 * D_KV + D_RNN]
        gr_ref[...] = u[:, D_ATT + 2 * D_KV + D_RNN:]

    return _pcall(
        body, (x, w_in_t), name="in_proj", grid=(S // tb,), comm=comm,
        in_specs=[_rows(tb, D_MODEL), _resident((D_IN, D_MODEL))],
        out_specs=[_rows(tb, D_ATT), _rows(tb, D_KV), _rows(tb, D_KV), _rows(tb, D_RNN), _rows(tb, D_RNN)],
        out_shape=[jax.ShapeDtypeStruct((S, D_ATT), BF16), jax.ShapeDtypeStruct((S, D_KV), BF16),
                   jax.ShapeDtypeStruct((S, D_KV), BF16), jax.ShapeDtypeStruct((S, D_RNN), F32),
                   jax.ShapeDtypeStruct((S, D_RNN), F32)])


GROUP = N_HEADS // N_KV


def _band_mask(i):
    qi = lax.broadcasted_iota(jnp.int32, (GROUP * QBLK, 2 * QBLK), 0) & (QBLK - 1)
    sj = lax.broadcasted_iota(jnp.int32, (GROUP * QBLK, 2 * QBLK), 1)
    return (sj > qi) & (sj <= qi + QBLK) & ((sj >= QBLK) | (i > 0))


def _stack_heads(x, g):
    return jnp.concatenate([x[:, (g * GROUP + hh) * HEAD_DIM:(g * GROUP + hh + 1) * HEAD_DIM] for hh in range(GROUP)],
                           axis=0)


def _unstack_heads(x4):
    return [x4[hh * QBLK:(hh + 1) * QBLK] for hh in range(GROUP)]


def _sink_column(sink_ref, g):
    head = lax.broadcasted_iota(jnp.int32, (GROUP * QBLK, 1), 0) // QBLK
    col = jnp.full((GROUP * QBLK, 1), sink_ref[g * GROUP], F32)
    for hh in range(1, GROUP):
        col = jnp.where(head == hh, sink_ref[g * GROUP + hh], col)
    return col


ATT_STEP = 4
IN_GRAD_PARTS = 2


def _attn_specs(nq=1):
    cur = lambda i: (i, 0)
    prev = lambda i: (jnp.maximum(nq * i - 1, 0), 0)
    return [pl.BlockSpec((nq * QBLK, D_KV), cur), pl.BlockSpec((QBLK, D_KV), prev),
            pl.BlockSpec((nq * QBLK, D_KV), cur), pl.BlockSpec((QBLK, D_KV), prev)]


def _attn_fwd(q, k, v, sinks, comm=None):
    S = q.shape[0]
    nq = min(ATT_STEP, S // QBLK)

    def body(sink_ref, q_ref, kc_ref, kp_ref, vc_ref, vp_ref, o_ref, lse_ref):
        first = pl.program_id(0) * nq
        kall = jnp.concatenate([kp_ref[...], kc_ref[...]], axis=0)
        vall = jnp.concatenate([vp_ref[...], vc_ref[...]], axis=0)
        for b in range(nq):
            valid = _band_mask(first + b)
            rows = slice(b * QBLK, (b + 1) * QBLK)
            keys = slice(b * QBLK, (b + 2) * QBLK)
            qv = q_ref[rows, :]
            outs = []
            for g in range(N_KV):
                kcat = kall[keys, g * HEAD_DIM:(g + 1) * HEAD_DIM]
                vcat = vall[keys, g * HEAD_DIM:(g + 1) * HEAD_DIM]
                s = jnp.where(valid, _dot_nt(_stack_heads(qv, g), kcat), -1e30)
                sink = _sink_column(sink_ref, g)
                m = jnp.maximum(jnp.max(s, axis=1, keepdims=True), sink)
                p = jnp.exp(s - m)
                l = jnp.sum(p, axis=1, keepdims=True) + jnp.exp(sink - m)
                outs += _unstack_heads(_dot(p.astype(BF16), vcat) / l)
                lse_ref[(b * N_KV + g) * GROUP * QBLK:(b * N_KV + g + 1) * GROUP * QBLK, :] = m + jnp.log(l)
            o_ref[rows, :] = jnp.concatenate(outs, axis=1).astype(BF16)

    lse_rows = nq * N_HEADS * QBLK
    return _pcall(
        body, (sinks, q, k, k, v, v), name="attn_fwd", grid=(S // (nq * QBLK),), comm=comm,
        in_specs=[pl.BlockSpec(memory_space=pltpu.SMEM), _rows(nq * QBLK, D_ATT)] + _attn_specs(nq),
        out_specs=[_rows(nq * QBLK, D_ATT), _rows(lse_rows, 1)],
        out_shape=[jax.ShapeDtypeStruct((S, D_ATT), BF16), jax.ShapeDtypeStruct((S * N_HEADS, 1), F32)])


def _w_rows(w_ref):
    return [w_ref[k:k + 1, :] for k in range(w_ref.shape[0])]


def _conv4(x, halo, w, b):
    y = b + w[3] * x
    for s in (1, 2, 3):
        y = y + w[3 - s] * _shift_down(x, halo, s)
    return y


def _rnn_gates(xc, wa, wx, ba, bx, sp):
    xcb = xc.astype(BF16)
    r = _sigmoid(_dot(xcb, wa) + ba)
    ig = _sigmoid(_dot(xcb, wx) + bx)
    la = -LRU_C * r * sp
    a = jnp.exp(la)
    t = jnp.tanh(la)
    f = jnp.sqrt(-2.0 * t / (1.0 - t))
    return r, ig, a, f


def _rnn_fwd(xr, gr, conv_w, conv_b, wa, wx, ba, bx, lam, comm=None):
    S = xr.shape[0]
    tb = min(1024, S)

    def body(xr_ref, gr_ref, cw_ref, cb_ref, wa_ref, wx_ref, ba_ref, bx_ref, lam_ref, rec_ref, h_ref,
             xc_ref, r_ref, ig_ref, a_ref, f_ref, halo_s, hc_s, a_s, b_s):
        @pl.when(pl.program_id(0) == 0)
        def _():
            halo_s[...] = jnp.zeros_like(halo_s)
            hc_s[...] = jnp.zeros_like(hc_s)

        x = xr_ref[...]
        xc = _conv4(x, halo_s[...], _w_rows(cw_ref), cb_ref[...])
        halo_s[...] = x[tb - 8:]
        r, ig, a, f = _rnn_gates(xc, wa_ref[...], wx_ref[...], ba_ref[...], bx_ref[...], _softplus_neg(lam_ref[...]))
        xc_ref[...] = xc
        r_ref[...] = r
        ig_ref[...] = ig
        a_ref[...] = a
        f_ref[...] = f
        a_s[...] = a
        b_s[...] = f * ig * xc
        row8 = lax.broadcasted_iota(jnp.int32, (8, D_RNN), 0)

        def tile(t, hc):
            o = pl.multiple_of(t * 8, 8)
            at = a_s[pl.ds(o, 8), :]
            bt = b_s[pl.ds(o, 8), :]
            for s in (1, 2, 4):
                keep = row8 >= s
                a_sh = jnp.where(keep, pltpu.roll(at, s, 0), 1.0)
                b_sh = jnp.where(keep, pltpu.roll(bt, s, 0), 0.0)
                bt = at * b_sh + bt
                at = at * a_sh
            ht = at * hc + bt
            b_s[pl.ds(o, 8), :] = ht
            return _row_sum(jnp.where(row8 == 7, ht, 0.0))

        hc_s[0:1, :] = lax.fori_loop(0, tb // 8, tile, hc_s[0:1, :], unroll=2)
        h = b_s[...]
        h_ref[...] = h
        rec_ref[...] = (h * _gelu(gr_ref[...])).astype(BF16)

    vec = _resident((1, D_RNN))
    kept = jax.ShapeDtypeStruct((S, D_RNN), F32)
    return _pcall(
        body, (xr, gr, conv_w, conv_b, wa, wx, ba, bx, lam), name="rnn_fwd", grid=(S // tb,), sem="arbitrary", comm=comm,
        in_specs=[_rows(tb, D_RNN), _rows(tb, D_RNN), _resident((4, D_RNN)), vec,
                  _resident((D_RNN, D_RNN)), _resident((D_RNN, D_RNN)), vec, vec, vec],
        out_specs=[_rows(tb, D_RNN)] * 7,
        out_shape=[jax.ShapeDtypeStruct((S, D_RNN), BF16), kept, kept, kept, kept, kept, kept],
        scratch_shapes=[pltpu.VMEM((8, D_RNN), F32), pltpu.VMEM((8, D_RNN), F32),
                        pltpu.VMEM((tb, D_RNN), F32), pltpu.VMEM((tb, D_RNN), F32)])


def _mix_ln1_up(x, att, rec, w_out, ln1_g, ln1_b, w_up_top, w_up_bot, fcw, fcb, comm=None):
    S = x.shape[0]
    tb = min(256, S)
    nblk, kh, wblk = w_up_top.shape
    half = nblk // 2

    def body(x_ref, att_ref, rec_ref, wo_ref, g_ref, b_ref, wt_hbm, wb_hbm, fcw_ref, fcb_ref,
             z1_ref, h1b_ref, gate_ref, act_ref, gl_ref, vdgl_ref, halo_s, wu_s, wu_sems):
        @pl.when(pl.program_id(0) == 0)
        def _():
            halo_s[...] = jnp.zeros_like(halo_s)
            _load_row_halves(wt_hbm, wb_hbm, wu_s, wu_sems)

        z1 = ALPHA * x_ref[...] + _dot(att_ref[...], wo_ref[:D_ATT, :]) + _dot(rec_ref[...], wo_ref[D_ATT:, :])
        z1_ref[...] = z1
        xhat, _ = _ln_stats(z1)
        h1b = (xhat * g_ref[...] + b_ref[...]).astype(BF16)
        h1b_ref[...] = h1b
        for jj in range(half):
            cols = slice(jj * wblk, (jj + 1) * wblk)
            gate = _dot(h1b, wu_s[jj])
            val = _dot(h1b, wu_s[jj + half])
            halo = halo_s[:, cols]
            conv = (fcb_ref[:, cols] + fcw_ref[2:3, cols] * gate + fcw_ref[1:2, cols] * _shift_down(gate, halo, 1)
                    + fcw_ref[0:1, cols] * _shift_down(gate, halo, 2))
            halo_s[:, cols] = gate[tb - 8:]
            gl, dgl = _gelu_and_grad(conv)
            gate_ref[:, cols] = gate.astype(BF16)
            act_ref[:, cols] = (gl * val).astype(BF16)
            gl_ref[:, cols] = gl.astype(BF16)
            vdgl_ref[:, cols] = (val * dgl).astype(BF16)

    vec = _resident((1, D_MODEL))
    wide = jax.ShapeDtypeStruct((S, D_FF), BF16)
    return _pcall(
        body, (x, att, rec, w_out, ln1_g, ln1_b, w_up_top, w_up_bot, fcw, fcb), name="mix_ln1_up", grid=(S // tb,),
        sem="arbitrary", comm=comm,
        in_specs=[_rows(tb, D_MODEL), _rows(tb, D_ATT), _rows(tb, D_RNN), _resident((D_MODEL, D_MODEL)), vec, vec,
                  _ANY, _ANY, _resident((3, D_FF)), _resident((1, D_FF))],
        out_specs=[_rows(tb, D_MODEL), _rows(tb, D_MODEL)] + [_rows(tb, D_FF)] * 4,
        out_shape=[jax.ShapeDtypeStruct((S, D_MODEL), F32), jax.ShapeDtypeStruct((S, D_MODEL), BF16), wide, wide, wide, wide],
        scratch_shapes=[pltpu.VMEM((8, D_FF), F32), pltpu.VMEM((nblk, 2 * kh, wblk), BF16),
                        pltpu.SemaphoreType.DMA((2,))])


def _tail(act, gl, vdgl, z1, h1b, p, tgt, w_down, w_pg, b_pg, w_pp, ln1_g, ln1_b, ln2_g, ln2_b):
    S = z1.shape[0]
    tb = min(256, S)

    def body(act_ref, gl_ref, vdgl_ref, z1_ref, h1b_ref, p_ref, t_ref, wd_ref, wpg_ref, bpg_ref, wpp_ref,
             g1_ref, b1_ref, g2_ref, b2_ref, dz2_ref, dpre_ref, dpp_ref, dgc_ref, dval_ref, dh1_ref, acc_ref):
        i = pl.program_id(0)

        @pl.when(i == 0)
        def _():
            acc_ref[...] = jnp.zeros_like(acc_ref)

        ffn = _dot(act_ref[...], wd_ref[...])
        xhat1, _ = _ln_stats(z1_ref[...])
        h1 = xhat1 * g1_ref[...] + b1_ref[...]
        sg = _sigmoid(_dot(h1b_ref[...], wpg_ref[...]) + bpg_ref[...])
        pp = _dot(p_ref[...].astype(BF16), wpp_ref[...])
        z2 = ALPHA * h1 + ffn + sg * pp
        xhat2, rstd2 = _ln_stats(z2)
        y = xhat2 * g2_ref[...] + b2_ref[...]
        err = y - t_ref[...]
        dy = err * (1.0 / D_MODEL)
        loss = 0.5 * jnp.sum(jnp.sum(err * err, axis=1, keepdims=True), axis=0, keepdims=True) * (1.0 / D_MODEL)
        dz2 = _ln_bwd(dy, xhat2, rstd2, g2_ref[...])
        dz2b = dz2.astype(BF16)
        dz2_ref[...] = dz2b
        dpre = dz2 * pp * sg * (1.0 - sg)
        dpreb = dpre.astype(BF16)
        dpre_ref[...] = dpreb
        dpp_ref[...] = (dz2 * sg).astype(BF16)
        dh1_ref[...] = ALPHA * dz2 + _dot_nt(dpreb, wpg_ref[...])
        dactb = _dot_nt(dz2b, wd_ref[...]).astype(BF16)
        dval_ref[...] = dactb * gl_ref[...]
        dgc_ref[...] = dactb * vdgl_ref[...]
        _put_rows(acc_ref, [_row_sum(dy * xhat2), _row_sum(dy), _row_sum(dpre),
                            jnp.broadcast_to(loss, (1, D_MODEL))])

    vec = _resident((1, D_MODEL))
    return pl.pallas_call(
        body, name="tail", grid=(S // tb,),
        in_specs=[_rows(tb, D_FF), _rows(tb, D_FF), _rows(tb, D_FF), _rows(tb, D_MODEL), _rows(tb, D_MODEL),
                  _rows(tb, PLE_DIM), _rows(tb, D_MODEL), _resident((D_FF, D_MODEL)), _resident((D_MODEL, D_MODEL)), vec,
                  _resident((PLE_DIM, D_MODEL)), vec, vec, vec, vec],
        out_specs=[_rows(tb, D_MODEL), _rows(tb, D_MODEL), _rows(tb, D_MODEL), _rows(tb, D_FF),
                   _rows(tb, D_FF), _rows(tb, D_MODEL), _acc((8, D_MODEL))],
        out_shape=[jax.ShapeDtypeStruct((S, D_MODEL), BF16),
                   jax.ShapeDtypeStruct((S, D_MODEL), BF16), jax.ShapeDtypeStruct((S, D_MODEL), BF16),
                   jax.ShapeDtypeStruct((S, D_FF), BF16), jax.ShapeDtypeStruct((S, D_FF), BF16),
                   jax.ShapeDtypeStruct((S, D_MODEL), F32), jax.ShapeDtypeStruct((8, D_MODEL), F32)],
        compiler_params=_params("arbitrary"),
    )(act, gl, vdgl, z1, h1b, p, tgt, w_down, w_pg, b_pg, w_pp, ln1_g, ln1_b, ln2_g, ln2_b)


def _weight_grad(a_list, b_list, name, layout, ts=512, comm=None, b_window=None, halves=False):
    S = a_list[0].shape[0]
    ms = [a.shape[1] for a in a_list]
    M, nb = sum(ms), len(b_list)
    win, Nb = b_window if b_window else (0, b_list[0].shape[1])
    ts = min(ts, S)
    nk = S // ts
    per_b = N_DEV // nb
    na = len(a_list)

    n_out = 2 if halves else 1
    assert layout == "cols" or not halves

    def body(*refs):
        a_refs, b_refs, o_refs, acc_ref = refs[:na], refs[na:na + nb], refs[na + nb:na + nb + n_out], refs[-1]
        o_ref = o_refs[0]
        j, k = pl.program_id(0), pl.program_id(1)

        @pl.when(k == 0)
        def _():
            acc_ref[...] = jnp.zeros_like(acc_ref)

        for jj in range(nb):
            @pl.when(j == jj)
            def _():
                b = b_refs[jj][...].astype(BF16)
                off = 0
                for a_ref, m in zip(a_refs, ms):
                    acc_ref[off:off + m, :] += _dot_tn(a_ref[...].astype(BF16), b)
                    off += m

        @pl.when(k == nk - 1)
        def _():
            for d in range(per_b):
                if layout == "rows":
                    o_ref[d] = acc_ref[d * (M // N_DEV):(d + 1) * (M // N_DEV), :].astype(BF16)
                elif layout == "cols" and halves:
                    for o_half, r0 in zip(o_refs, (0, M // 2)):
                        o_half[d] = acc_ref[r0:r0 + M // 2, d * (Nb // per_b):(d + 1) * (Nb // per_b)].astype(BF16)
                elif layout == "cols":
                    o_ref[d] = acc_ref[:, d * (Nb // per_b):(d + 1) * (Nb // per_b)].astype(BF16)
                else:
                    o_ref[d] = acc_ref[:, d * (Nb // per_b):(d + 1) * (Nb // per_b)].T.astype(BF16)

    def b_index(jj):
        return lambda j, k: (jnp.where(j == jj, k, jnp.where(j < jj, 0, nk - 1)), win)

    if layout == "rows":
        assert nb == 1
        blk = (N_DEV, M // N_DEV, Nb)
    elif layout == "cols":
        blk = (per_b, M // n_out, Nb // per_b)
    else:
        blk = (per_b, Nb // per_b, M)
    res, comm_res = _pcall(
        body, (*a_list, *b_list), name=name, grid=(nb, nk), sem="arbitrary", comm=comm, step_axis=1,
        in_specs=[pl.BlockSpec((ts, m), lambda j, k: (k, 0)) for m in ms]
        + [pl.BlockSpec((ts, Nb), b_index(jj)) for jj in range(nb)],
        out_specs=[pl.BlockSpec(blk, lambda j, k: (j, 0, 0))] * n_out,
        out_shape=[jax.ShapeDtypeStruct((N_DEV,) + blk[1:], BF16)] * n_out,
        scratch_shapes=[pltpu.VMEM((M, Nb), F32)])
    res = res if halves else res[0]
    return (res, comm_res) if comm is not None else res


def _up_bwd(dgc, gate, dval, dh1p, z1, w_up_top, w_up_bot, fcw, w_out, ln1_g, comm=None):
    S = z1.shape[0]
    tb = min(256, S)
    t16 = tb // 16
    n16 = S // 16
    nblk, kh, wblk = w_up_top.shape
    half = nblk // 2
    nsteps = S // tb

    def body(dgc_ref, dgn_ref, gc_ref, dval_ref, dh1p_ref, z1_ref, wt_hbm, wb_hbm, fcw_ref, wo_ref, g1_ref,
             dgate_ref, dz1_ref, dz1b_ref, datt_ref, drec_ref, accf_ref, accd_ref, wu_s, wu_sems):
        i = pl.program_id(0)

        @pl.when(i == 0)
        def _():
            accf_ref[...] = jnp.zeros_like(accf_ref)
            accd_ref[...] = jnp.zeros_like(accd_ref)
            _load_row_halves(wt_hbm, wb_hbm, wu_s, wu_sems)

        dg = dgc_ref[...].astype(F32)
        nxt = jnp.where(i < nsteps - 1, dgn_ref[...].astype(F32)[0:8], 0.0)
        w = _w_rows(fcw_ref)
        up1, up2 = _shift_up(dg, nxt, 1), _shift_up(dg, nxt, 2)
        dgate = (w[2] * dg + w[1] * up1 + w[0] * up2).astype(BF16)
        dgate_ref[...] = dgate
        gate = gc_ref[...].astype(F32)
        _put_rows(accf_ref, [_row_sum(up2 * gate), _row_sum(up1 * gate), _row_sum(dg * gate), _row_sum(dg)])

        dh1 = dh1p_ref[...]
        for j in range(nblk):
            src = dgate if j < half else dval_ref[...]
            jj = j % half
            dh1 = dh1 + _dot_nt(src[:, jj * wblk:(jj + 1) * wblk], wu_s[j])
        xhat1, rstd1 = _ln_stats(z1_ref[...])
        dz1 = _ln_bwd(dh1, xhat1, rstd1, g1_ref[...])
        dz1_ref[...] = dz1
        dz1b = dz1.astype(BF16)
        dz1b_ref[...] = dz1b
        dcat = _dot_nt(dz1b, wo_ref[...])
        datt_ref[...] = dcat[:, :D_ATT].astype(BF16)
        drec_ref[...] = dcat[:, D_ATT:]
        _put_rows(accd_ref, [_row_sum(dh1 * xhat1), _row_sum(dh1)])

    next16 = pl.BlockSpec((16, D_FF), lambda i: (jnp.minimum((i + 1) * t16, n16 - 1), 0))
    return _pcall(
        body, (dgc, dgc, gate, dval, dh1p, z1, w_up_top, w_up_bot, fcw, w_out, ln1_g), name="up_bwd",
        grid=(nsteps,), sem="arbitrary", comm=comm,
        in_specs=[_rows(tb, D_FF), next16, _rows(tb, D_FF), _rows(tb, D_FF), _rows(tb, D_MODEL),
                  _rows(tb, D_MODEL), _ANY, _ANY, _resident((3, D_FF)),
                  _resident((D_MODEL, D_MODEL)), _resident((1, D_MODEL))],
        scratch_shapes=[pltpu.VMEM((nblk, 2 * kh, wblk), BF16), pltpu.SemaphoreType.DMA((2,))],
        out_specs=[_rows(tb, D_FF), _rows(tb, D_MODEL), _rows(tb, D_MODEL), _rows(tb, D_ATT), _rows(tb, D_RNN),
                   _acc((8, D_FF)), _acc((8, D_MODEL))],
        out_shape=[jax.ShapeDtypeStruct((S, D_FF), BF16), jax.ShapeDtypeStruct((S, D_MODEL), F32),
                   jax.ShapeDtypeStruct((S, D_MODEL), BF16), jax.ShapeDtypeStruct((S, D_ATT), BF16),
                   jax.ShapeDtypeStruct((S, D_RNN), F32), jax.ShapeDtypeStruct((8, D_FF), F32),
                   jax.ShapeDtypeStruct((8, D_MODEL), F32)])


def _attn_bwd(q, k, v, lse, do, sinks, comm=None):
    S = q.shape[0]
    grp = N_HEADS // N_KV
    nq = min(ATT_STEP, S // QBLK)

    def body(sink_ref, q_ref, kc_ref, kp_ref, vc_ref, vp_ref, do_ref, lse_ref, dq_ref, dkc_ref, dkp_ref, dvc_ref, dvp_ref,
             ds_ref):
        i = pl.program_id(0)

        @pl.when(i == 0)
        def _():
            ds_ref[...] = jnp.zeros_like(ds_ref)

        row8 = lax.broadcasted_iota(jnp.int32, (8, 128), 0)
        lane8 = lax.broadcasted_iota(jnp.int32, (8, 128), 1)
        dsink = jnp.zeros((8, 128), F32)
        kall = jnp.concatenate([kp_ref[...], kc_ref[...]], axis=0)
        vall = jnp.concatenate([vp_ref[...], vc_ref[...]], axis=0)
        dk_t = [jnp.zeros((D_KV, QBLK), F32) for _ in range(nq + 1)]
        dv_t = [jnp.zeros((D_KV, QBLK), F32) for _ in range(nq + 1)]
        for b in range(nq):
            valid = _band_mask(i * nq + b)
            rows = slice(b * QBLK, (b + 1) * QBLK)
            keys = slice(b * QBLK, (b + 2) * QBLK)
            qv, dov = q_ref[rows, :], do_ref[rows, :]
            dqs, dks, dvs = [], [], []
            for g in range(N_KV):
                kcat = kall[keys, g * HEAD_DIM:(g + 1) * HEAD_DIM]
                vcat = vall[keys, g * HEAD_DIM:(g + 1) * HEAD_DIM]
                q4, do4 = _stack_heads(qv, g), _stack_heads(dov, g)
                s = jnp.where(valid, _dot_nt(q4, kcat), -1e30)
                lse = lse_ref[(b * N_KV + g) * GROUP * QBLK:(b * N_KV + g + 1) * GROUP * QBLK, :]
                p = jnp.exp(s - lse)
                p_sink = jnp.exp(_sink_column(sink_ref, g) - lse)
                dp = _dot_nt(do4, vcat)
                delta = jnp.sum(p * dp, axis=1, keepdims=True)
                dsc = (p * (dp - delta)).astype(BF16)
                dqs += _unstack_heads(_dot(dsc, kcat) * (HEAD_DIM ** -0.5))
                dks.append(_dot_tn(q4, dsc))
                dvs.append(_dot_tn(do4, p.astype(BF16)))
                for hh, part in enumerate(_unstack_heads(-p_sink * delta)):
                    here = (row8 == 0) & (lane8 == g * grp + hh)
                    dsink = dsink + jnp.where(here, jnp.sum(part, axis=0, keepdims=True), 0.0)
            dq_ref[rows, :] = jnp.concatenate(dqs, axis=1).astype(BF16)
            dk2, dv2 = jnp.concatenate(dks, axis=0), jnp.concatenate(dvs, axis=0)
            dk_t[b], dk_t[b + 1] = dk_t[b] + dk2[:, :QBLK], dk_t[b + 1] + dk2[:, QBLK:]
            dv_t[b], dv_t[b + 1] = dv_t[b] + dv2[:, :QBLK], dv_t[b + 1] + dv2[:, QBLK:]
        dkp_ref[...] = dk_t[0].T
        dvp_ref[...] = dv_t[0].T
        for b in range(nq):
            dkc_ref[b * QBLK:(b + 1) * QBLK, :] = dk_t[b + 1].T
            dvc_ref[b * QBLK:(b + 1) * QBLK, :] = dv_t[b + 1].T
        ds_ref[...] += dsink

    nsteps = S // (nq * QBLK)
    cur = jax.ShapeDtypeStruct((S, D_KV), F32)
    prev = jax.ShapeDtypeStruct((nsteps * QBLK, D_KV), F32)
    big = _rows(nq * QBLK, D_ATT)
    return _pcall(
        body, (sinks, q, k, k, v, v, do, lse), name="attn_bwd", grid=(nsteps,), sem="arbitrary", comm=comm,
        in_specs=[pl.BlockSpec(memory_space=pltpu.SMEM), big] + _attn_specs(nq) + [big, _rows(nq * N_HEADS * QBLK, 1)],
        out_specs=[big, _rows(nq * QBLK, D_KV), _rows(QBLK, D_KV), _rows(nq * QBLK, D_KV), _rows(QBLK, D_KV),
                   _acc((8, 128))],
        out_shape=[jax.ShapeDtypeStruct((S, D_ATT), BF16), cur, prev, cur, prev, jax.ShapeDtypeStruct((8, 128), F32)])


def _rnn_bwd(xr, gr, h, kept, drec, conv_w, wa, wx, lam, comm=None):
    S = xr.shape[0]
    tb = min(512, S)
    t8 = tb // 8
    nsteps = S // tb

    def body(xr_ref, xp_ref, gr_ref, h_ref, hp_ref, xc_ref, r_ref, ig_ref, a_ref, f_ref, drec_ref, cw_ref, wa_ref, wx_ref,
             lam_ref, dxr_ref, dgr_ref, gwa_ref, gwx_ref, acc_ref, carry_s, dxc_halo_s, d_s, gwa_s, gwx_s):
        i = pl.program_id(0)
        blk = nsteps - 1 - i

        @pl.when(i == 0)
        def _():
            gwa_s[...] = jnp.zeros_like(gwa_s)
            gwx_s[...] = jnp.zeros_like(gwx_s)
            acc_ref[...] = jnp.zeros_like(acc_ref)
            carry_s[...] = jnp.zeros_like(carry_s)
            dxc_halo_s[...] = jnp.zeros_like(dxc_halo_s)

        x = xr_ref[...]
        xhalo = jnp.where(blk > 0, xp_ref[...], 0.0)
        cw = _w_rows(cw_ref)
        xs = [_shift_down(x, xhalo, 3), _shift_down(x, xhalo, 2), _shift_down(x, xhalo, 1), x]
        xc, r, ig, a, f = xc_ref[...], r_ref[...], ig_ref[...], a_ref[...], f_ref[...]
        sp = _softplus_neg(lam_ref[...])
        hcur = h_ref[...]
        hprev = _shift_down(hcur, jnp.where(blk > 0, hp_ref[...], 0.0), 1)
        gl, dgl = _gelu_and_grad(gr_ref[...])
        drec = drec_ref[...]
        dgr_ref[...] = (drec * hcur * dgl).astype(BF16)
        d_s[...] = drec * gl
        row8 = lax.broadcasted_iota(jnp.int32, (8, D_RNN), 0)

        def tile(t, c):
            o = pl.multiple_of((t8 - 1 - t) * 8, 8)
            a8 = a_ref[pl.ds(o, 8), :]
            dt = d_s[pl.ds(o, 8), :]
            at = jnp.where(row8 == 7, 1.0, pltpu.roll(a8, 7, 0))
            for s in (1, 2, 4):
                keep = row8 < 8 - s
                a_sh = jnp.where(keep, pltpu.roll(at, 8 - s, 0), 1.0)
                d_sh = jnp.where(keep, pltpu.roll(dt, 8 - s, 0), 0.0)
                dt = at * d_sh + dt
                at = at * a_sh
            lt = at * c + dt
            d_s[pl.ds(o, 8), :] = lt
            return _row_sum(jnp.where(row8 == 0, a8 * lt, 0.0))

        carry_s[0:1, :] = lax.fori_loop(0, t8, tile, carry_s[0:1, :], unroll=2)
        lmb = d_s[...]
        a2 = a * a
        dla = lmb * hprev * a - lmb * ig * xc * (a2 / f)
        di = lmb * f * xc
        dr = dla * (-LRU_C) * sp
        dpa = dr * r * (1.0 - r)
        dpx = di * ig * (1.0 - ig)
        dpab = dpa.astype(BF16)
        dpxb = dpx.astype(BF16)
        xcb = xc.astype(BF16)
        gwa_s[...] += _dot_tn(xcb, dpab)
        gwx_s[...] += _dot_tn(xcb, dpxb)

        @pl.when(i == nsteps - 1)
        def _():
            for dense, out in ((gwa_s[...], gwa_ref), (gwx_s[...], gwx_ref)):
                for b in range(RNN_BLOCKS):
                    rows = slice(b * HEAD_DIM, (b + 1) * HEAD_DIM)
                    out[rows, :] = dense[rows, b * HEAD_DIM:(b + 1) * HEAD_DIM]

        dxc = lmb * f * ig + _dot_nt(dpab, wa_ref[...]) + _dot_nt(dpxb, wx_ref[...])
        nxt = dxc_halo_s[...]
        dxr = cw[3] * dxc
        for s in (1, 2, 3):
            dxr = dxr + cw[3 - s] * _shift_up(dxc, nxt, s)
        dxr_ref[...] = dxr.astype(BF16)
        dxc_halo_s[...] = dxc[:8]
        dlam = _row_sum(dla * (-LRU_C) * r) * (-1.0 / (1.0 + jnp.exp(lam_ref[...])))
        _put_rows(acc_ref, [_row_sum(dxc * xs[0]), _row_sum(dxc * xs[1]), _row_sum(dxc * xs[2]), _row_sum(dxc * xs[3]),
                            _row_sum(dxc), _row_sum(dpa), _row_sum(dpx), dlam])

    rev = lambda i: (nsteps - 1 - i, 0)
    prev8 = lambda i: (jnp.maximum((nsteps - 1 - i) * t8 - 1, 0), 0)
    blkspec = pl.BlockSpec((tb, D_RNN), rev)
    halo8 = pl.BlockSpec((8, D_RNN), prev8)
    vec = _resident((1, D_RNN))
    return _pcall(
        body, (xr, xr, gr, h, h, *kept, drec, conv_w, wa, wx, lam), name="rnn_bwd", grid=(nsteps,),
        sem="arbitrary", comm=comm,
        in_specs=[blkspec, halo8, blkspec, blkspec, halo8] + [blkspec] * 6
        + [_resident((4, D_RNN)), _resident((D_RNN, D_RNN)), _resident((D_RNN, D_RNN)), vec],
        out_specs=[blkspec, blkspec, _acc((D_RNN, HEAD_DIM)), _acc((D_RNN, HEAD_DIM)), _acc((8, D_RNN))],
        out_shape=[jax.ShapeDtypeStruct((S, D_RNN), BF16), jax.ShapeDtypeStruct((S, D_RNN), BF16),
                   jax.ShapeDtypeStruct((D_RNN, HEAD_DIM), F32), jax.ShapeDtypeStruct((D_RNN, HEAD_DIM), F32),
                   jax.ShapeDtypeStruct((8, D_RNN), F32)],
        scratch_shapes=[pltpu.VMEM((8, D_RNN), F32), pltpu.VMEM((8, D_RNN), F32), pltpu.VMEM((tb, D_RNN), F32),
                        pltpu.VMEM((D_RNN, D_RNN), F32), pltpu.VMEM((D_RNN, D_RNN), F32)])


def _in_bwd(dq, dkc, dkp, dvc, dvp, dxr, dgr, dz1, w_in, comm=None):
    S = dz1.shape[0]
    tb = min(ATT_STEP * QBLK, S)
    nsteps = S // tb

    def body(dq_ref, dkc_ref, dkn_ref, dvc_ref, dvn_ref, dxr_ref, dgr_ref, dz1_ref, w_ref, dkv_ref, dx_ref):
        last = pl.program_id(0) == nsteps - 1

        def total(cur_ref, next_ref):
            nxt = jnp.where(last, 0.0, next_ref[...])
            tail = cur_ref[tb - QBLK:, :] + nxt
            return jnp.concatenate([cur_ref[:tb - QBLK, :], tail], axis=0) if tb > QBLK else tail

        dkv = jnp.concatenate([total(dkc_ref, dkn_ref), total(dvc_ref, dvn_ref)], axis=1).astype(BF16)
        dkv_ref[...] = dkv
        du = jnp.concatenate([dq_ref[...], dkv, dxr_ref[...], dgr_ref[...]], axis=1)
        dx_ref[...] = ALPHA * dz1_ref[...] + _dot(du, w_ref[...])

    nextp = pl.BlockSpec((QBLK, D_KV), lambda i: (jnp.minimum(i + 1, nsteps - 1), 0))
    return _pcall(
        body, (dq, dkc, dkp, dvc, dvp, dxr, dgr, dz1, w_in), name="in_bwd", grid=(nsteps,), comm=comm,
        in_specs=[_rows(tb, D_ATT), _rows(tb, D_KV), nextp, _rows(tb, D_KV), nextp,
                  _rows(tb, D_RNN), _rows(tb, D_RNN), _rows(tb, D_MODEL), _resident((D_IN, D_MODEL))],
        out_specs=[_rows(tb, 2 * D_KV), _rows(tb, D_MODEL)],
        out_shape=[jax.ShapeDtypeStruct((S, 2 * D_KV), BF16), jax.ShapeDtypeStruct((S, D_MODEL), F32)])


def _block_diag(w):
    eye = jnp.eye(RNN_BLOCKS, dtype=w.dtype)
    return (w[:, :, None, :] * eye[:, None, :, None]).reshape(D_RNN, D_RNN).astype(BF16)


def _adamw(w, g, m, v):
    m = ADAM_B1 * m + (1.0 - ADAM_B1) * g
    v = ADAM_B2 * v + (1.0 - ADAM_B2) * (g * g)
    m_hat = m / (1.0 - ADAM_B1 ** ADAM_STEP)
    v_hat = v / (1.0 - ADAM_B2 ** ADAM_STEP)
    delta = -ADAM_LR * (m_hat / (jnp.sqrt(v_hat) + ADAM_EPS) + ADAM_WD * w)
    return delta, m, v


def _sum_adamw(parts, w, m, v, name):
    parts = parts if isinstance(parts, (list, tuple)) else [parts]
    R, C = w.shape
    rb = R if R <= 256 else (256 if parts[0].shape[1] % 256 == 0 else 128)
    per = parts[0].shape[1] // rb
    assert R % rb == 0 and parts[0].shape[1] % rb == 0
    n = len(parts)

    def body(*refs):
        p_refs = refs[:n]
        w_ref, m_ref, v_ref, g_out, d_out, m_out, v_out = refs[n:]
        which = pl.program_id(0) // per

        def total(p_ref):
            g = p_ref[0].astype(F32)
            for d in range(1, N_DEV):
                g = g + p_ref[d].astype(F32)
            return g

        g = total(p_refs[0])
        for j in range(1, n):
            g = jnp.where(which == j, total(p_refs[j]), g)
        delta, mn, vn = _adamw(w_ref[...], g, m_ref[...], v_ref[...])
        g_out[...] = g
        d_out[...] = delta
        m_out[...] = mn
        v_out[...] = vn

    def part_spec(j):
        return pl.BlockSpec((N_DEV, rb, C), lambda i: (0, jnp.clip(i - j * per, 0, per - 1), 0))

    blk = _rows(rb, C)
    out = jax.ShapeDtypeStruct((R, C), F32)
    return pl.pallas_call(
        body, name=name, grid=(R // rb,),
        in_specs=[part_spec(j) for j in range(n)] + [blk, blk, blk],
        out_specs=[blk, blk, blk, blk], out_shape=[out, out, out, out],
        compiler_params=_params("parallel"),
    )(*parts, w, m, v)


_SMALL = [("attn_sinks", "s", 0, 1, None), ("rnn_conv_w", "r", 0, 4, "cols"), ("rnn_conv_b", "r", 4, 1, None),
          ("gate_a_w", "a", 0, D_RNN, None), ("gate_a_b", "r", 5, 1, None), ("gate_x_w", "x", 0, D_RNN, None),
          ("gate_x_b", "r", 6, 1, None), ("lru_lambda", "r", 7, 1, None), ("ln1_g", "d", 0, 1, None),
          ("ln1_b", "d", 1, 1, None), ("ffn_conv_w", "f", 0, 3, "cols"), ("ffn_conv_b", "f", 3, 1, None),
          ("ple_gate_b", "t", 2, 1, None), ("ln2_g", "t", 0, 1, None), ("ln2_b", "t", 1, 1, None)]
_LOSS_ROW = 3


_ACC_COLS = {"t": (0, D_MODEL), "f": (D_MODEL, D_FF), "d": (D_MODEL + D_FF, D_MODEL), "s": (2 * D_MODEL + D_FF, 128),
             "r": (2 * D_MODEL + D_FF + 128, D_RNN)}
_ACC_WIDTH = 2 * D_MODEL + D_FF + 128 + D_RNN


def _small_update(rows_all, gates_all, params):
    flat = [arr for triple in params for arr in triple]
    n_par = len(_SMALL)

    def body(*refs):
        rows_ref, gates_ref = refs[:2]
        p_refs = refs[2:2 + 3 * n_par]
        loss_ref = refs[2 + 3 * n_par]
        o_refs = refs[3 + 3 * n_par:3 + 7 * n_par]
        rows_s, tmp_r, tmp_f = refs[3 + 7 * n_par:]
        me = _dev_index(*_place())
        rows_sum, gates_sum = rows_ref[0], gates_ref[0]
        for d in range(1, N_DEV):
            rows_sum = rows_sum + rows_ref[d]
            gates_sum = gates_sum + gates_ref[d]
        rows_s[...] = rows_sum
        t0 = _ACC_COLS["t"][0]
        loss_ref[...] = rows_s[_LOSS_ROW:_LOSS_ROW + 1, t0:t0 + 128]
        for i, (name, key, row, rows, how) in enumerate(_SMALL):
            w_ref, m_ref, v_ref = p_refs[3 * i:3 * i + 3]
            g_out, d_out, m_out, v_out = o_refs[4 * i:4 * i + 4]
            if key == "a":
                g = gates_sum[:, :HEAD_DIM]
            elif key == "x":
                g = gates_sum[:, HEAD_DIM:]
            elif how == "cols":
                c0, width = _ACC_COLS[key]
                full = rows_s[:, c0:c0 + width]
                shard = width // N_DEV
                mine = full[:, :shard]
                for d in range(1, N_DEV):
                    mine = jnp.where(me == d, full[:, d * shard:(d + 1) * shard], mine)
                tmp = tmp_r if key == "r" else tmp_f
                tmp[...] = mine
                g = tmp[row:row + rows, :]
            else:
                c0, width = _ACC_COLS[key]
                g = rows_s[row:row + rows, c0:c0 + width][:, :w_ref.shape[1]]
            delta, mn, vn = _adamw(w_ref[...], g, m_ref[...], v_ref[...])
            g_out[...] = g
            d_out[...] = delta
            m_out[...] = mn
            v_out[...] = vn

    outs = [jax.ShapeDtypeStruct((1, 128), F32)]
    for w, _, _ in params:
        outs += [jax.ShapeDtypeStruct(w.shape, F32)] * 4
    scratch = [pltpu.VMEM((8, _ACC_WIDTH), F32), pltpu.VMEM((8, D_RNN // N_DEV), F32), pltpu.VMEM((8, D_FF // N_DEV), F32)]
    res = pl.pallas_call(body, name="small_update", out_shape=outs, scratch_shapes=scratch)(rows_all, gates_all, *flat)
    return res[0], [res[1 + 4 * i:5 + 4 * i] for i in range(n_par)]


def kernel(x, p, w_in, attn_sinks, rnn_conv_w, rnn_conv_b, gate_a_w, gate_a_b, gate_x_w, gate_x_b, lru_lambda, w_out, ln1_g, ln1_b, w_ffn_up, ffn_conv_w, ffn_conv_b, w_ffn_down, ple_gate_w, ple_gate_b, ple_proj, ln2_g, ln2_b, loss_target, m_w_in, m_attn_sinks, m_rnn_conv_w, m_rnn_conv_b, m_gate_a_w, m_gate_a_b, m_gate_x_w, m_gate_x_b, m_lru_lambda, m_w_out, m_ln1_g, m_ln1_b, m_w_ffn_up, m_ffn_conv_w, m_ffn_conv_b, m_w_ffn_down, m_ple_gate_w, m_ple_gate_b, m_ple_proj, m_ln2_g, m_ln2_b, v_w_in, v_attn_sinks, v_rnn_conv_w, v_rnn_conv_b, v_gate_a_w, v_gate_a_b, v_gate_x_w, v_gate_x_b, v_lru_lambda, v_w_out, v_ln1_g, v_ln1_b, v_w_ffn_up, v_ffn_conv_w, v_ffn_conv_b, v_w_ffn_down, v_ple_gate_w, v_ple_gate_b, v_ple_proj, v_ln2_g, v_ln2_b):
    from_col_blocks = lambda g: g.transpose(1, 0, 2).reshape(g.shape[1], N_DEV * g.shape[2])

    xs, ps, tgt, sinks = x[0], p[0, 0], loss_target[0], attn_sinks[0]
    wa, wx = _block_diag(gate_a_w[0]), _block_diag(gate_x_w[0])

    conv_cols = jnp.concatenate([rnn_conv_w[0].reshape(1, -1), ffn_conv_w[0].reshape(1, -1)], axis=1)
    n_rc, n_fc = 4 * D_RNN // N_DEV, 3 * D_FF // N_DEV
    ((g_in,),) = _comm_call([_Gather([w_in[0].T.astype(BF16)])], "gather_w_in")
    w_in_full = g_in.reshape(D_IN, D_MODEL)

    (q, k, v, xr, gr), _ = _in_proj(xs, w_in_full)
    w_up_shard = w_ffn_up[0].astype(BF16)
    (att, lse), (g_out, w_up_top, g_conv) = _attn_fwd(
        q, k, v, sinks,
        comm=_Multi([_Gather([w_out[0].astype(BF16), w_up_shard[:D_MODEL // 2]]),
                     _Bcast([jnp.broadcast_to(conv_cols, (8, n_rc + n_fc))])]))
    rcw = from_col_blocks(g_conv[:, 0, :n_rc].reshape(N_DEV, 4, D_RNN // N_DEV))
    fcw = from_col_blocks(g_conv[:, 0, n_rc:].reshape(N_DEV, 3, D_FF // N_DEV))
    (rec, h, *kept), (w_up_bot,) = _rnn_fwd(xr, gr, rcw, rnn_conv_b, wa, wx, gate_a_b, gate_x_b, lru_lambda,
                                            comm=_Gather([w_up_shard[D_MODEL // 2:]]))
    w_out_full = g_out.reshape(D_MODEL, D_MODEL)
    (z1, h1b, gate, act, gl, vdgl), (g_down, g_pg, g_pp) = _mix_ln1_up(
        xs, att, rec, w_out_full, ln1_g, ln1_b, w_up_top, w_up_bot, fcw, ffn_conv_b,
        comm=_Gather([w_ffn_down[0].astype(BF16), ple_gate_w[0].astype(BF16), ple_proj[0].astype(BF16)]))
    dz2b, dpreb, dppb, dgc, dval, dh1p, acc_t = _tail(
        act, gl, vdgl, z1, h1b, ps, tgt, g_down.reshape(D_FF, D_MODEL), g_pg.reshape(D_MODEL, D_MODEL), ple_gate_b,
        from_col_blocks(g_pp), ln1_g, ln1_b, ln2_g, ln2_b)

    gd_down = _weight_grad([dz2b], [act], "down_grad", "rows_t", ts=1024)
    gd_pg = _weight_grad([h1b], [dpreb], "pg_grad", "rows", ts=1024)
    gd_pp = _weight_grad([ps], [dppb], "pp_grad", "cols", ts=1024)
    (dgate, dz1, dz1b, datt, drec, acc_f, acc_d), (r_down, r_pg, r_pp) = _up_bwd(
        dgc, gate, dval, dh1p, z1, w_up_top, w_up_bot, fcw, w_out_full, ln1_g, comm=_Exchange([gd_down, gd_pg, gd_pp]))
    gd_up_top, gd_up_bot = _weight_grad([h1b], [dgate, dval], "up_grad", "cols", halves=True)
    gd_out = _weight_grad([att, rec], [dz1b], "out_grad", "rows", ts=1024)
    (dq, dkc, dkp, dvc, dvp, acc_s), (r_up_top,) = _attn_bwd(q, k, v, lse, datt, sinks, comm=_Exchange([gd_up_top]))
    early = jnp.concatenate([acc_t, acc_f, acc_d], axis=1)
    (dxr, dgr, g_wa, g_wx, acc_r), (r_up_bot, r_out, early_all) = _rnn_bwd(
        xr, gr, h, kept, drec, rcw, wa, wx, lru_lambda, comm=_Multi([_Exchange([gd_up_bot, gd_out]), _Bcast([early])]))
    (dkv, dx), _ = _in_bwd(dq, dkc, dkp, dvc, dvp, dxr, dgr, dz1, w_in_full)
    du_parts = [dq, dkv, dxr, dgr]
    lanes = D_RNN // 128
    late = jnp.concatenate([g_wa, g_wx], axis=1)
    late = jnp.concatenate([late, acc_s, acc_r.reshape(8, lanes, 128).transpose(1, 0, 2).reshape(8 * lanes, 128)], axis=0)
    width = D_MODEL // IN_GRAD_PARTS
    comm, r_parts = _Gather([late]), []
    for part in range(IN_GRAD_PARTS):
        gd_part, got = _weight_grad(du_parts, [xs], f"in_grad_{part}", "rows", ts=1024, b_window=(part, width), comm=comm)
        if part == 0:
            (late_all,) = got
        else:
            r_parts += got
        comm = _Exchange([gd_part])
    r_parts += _comm_call([comm], "exchange_w_in")[0]
    r_in = jnp.concatenate(r_parts, axis=2)
    acc_r_all = late_all[:, D_RNN + 8:].reshape(N_DEV, lanes, 8, 128).transpose(0, 2, 1, 3).reshape(N_DEV, 8, D_RNN)
    small_parts = (jnp.concatenate([early_all, late_all[:, D_RNN:D_RNN + 8], acc_r_all], axis=2),
                   late_all[:, :D_RNN])

    outs = {}
    res = _sum_adamw(r_in, w_in[0].T, m_w_in[0].T, v_w_in[0].T, "adamw_w_in")
    outs["w_in"] = [r.T[None] for r in res]
    for name, parts, w, m, v in [("w_out", r_out, w_out, m_w_out, v_w_out),
                                 ("w_ffn_up", [r_up_top, r_up_bot], w_ffn_up, m_w_ffn_up, v_w_ffn_up),
                                 ("w_ffn_down", r_down, w_ffn_down, m_w_ffn_down, v_w_ffn_down),
                                 ("ple_gate_w", r_pg, ple_gate_w, m_ple_gate_w, v_ple_gate_w),
                                 ("ple_proj", r_pp, ple_proj, m_ple_proj, v_ple_proj)]:
        res = _sum_adamw(parts, w[0], m[0], v[0], "adamw_" + name)
        outs[name] = [r[None] for r in res]

    given = dict(attn_sinks=(attn_sinks, m_attn_sinks, v_attn_sinks), rnn_conv_w=(rnn_conv_w, m_rnn_conv_w, v_rnn_conv_w),
                 rnn_conv_b=(rnn_conv_b, m_rnn_conv_b, v_rnn_conv_b), gate_a_w=(gate_a_w, m_gate_a_w, v_gate_a_w),
                 gate_a_b=(gate_a_b, m_gate_a_b, v_gate_a_b), gate_x_w=(gate_x_w, m_gate_x_w, v_gate_x_w),
                 gate_x_b=(gate_x_b, m_gate_x_b, v_gate_x_b), lru_lambda=(lru_lambda, m_lru_lambda, v_lru_lambda),
                 ln1_g=(ln1_g, m_ln1_g, v_ln1_g), ln1_b=(ln1_b, m_ln1_b, v_ln1_b),
                 ffn_conv_w=(ffn_conv_w, m_ffn_conv_w, v_ffn_conv_w), ffn_conv_b=(ffn_conv_b, m_ffn_conv_b, v_ffn_conv_b),
                 ple_gate_b=(ple_gate_b, m_ple_gate_b, v_ple_gate_b), ln2_g=(ln2_g, m_ln2_g, v_ln2_g),
                 ln2_b=(ln2_b, m_ln2_b, v_ln2_b))
    as_2d = lambda a: a.reshape(-1, a.shape[-1])
    loss_row, small_res = _small_update(*small_parts, [tuple(as_2d(a) for a in given[n]) for n, *_ in _SMALL])
    loss = loss_row[0, 0]
    for (n, *_), res in zip(_SMALL, small_res):
        outs[n] = [r.reshape(given[n][0].shape) for r in res]

    order = ["w_in", "attn_sinks", "rnn_conv_w", "rnn_conv_b", "gate_a_w", "gate_a_b", "gate_x_w", "gate_x_b",
             "lru_lambda", "w_out", "ln1_g", "ln1_b", "w_ffn_up", "ffn_conv_w", "ffn_conv_b", "w_ffn_down",
             "ple_gate_w", "ple_gate_b", "ple_proj", "ln2_g", "ln2_b"]
    return (loss, dx[None], *[outs[n][0] for n in order], *[outs[n][1] for n in order],
            *[outs[n][2] for n in order], *[outs[n][3] for n in order])
```

```python
import jax
import jax.numpy as jnp
from jax import lax
from jax.experimental import pallas as pl
from jax.experimental.pallas import tpu as pltpu

F32 = jnp.float32
BF16 = jnp.bfloat16

D_MODEL = 1024
D_ATT = 512
D_KV = 128
HEAD_DIM = 64
N_HEADS = 8
N_KV = 2
D_RNN = 512
RNN_BLOCKS = 8
D_IN = 1792
D_FF = 3072
PLE_DIM = 256
QBLK = 128
N_DEV = 8
ALPHA = float(2 ** 0.25)
LN_EPS = 1e-5
LRU_C = 8.0
ADAM_LR, ADAM_B1, ADAM_B2, ADAM_EPS, ADAM_WD, ADAM_STEP = 0.001, 0.9, 0.999, 1e-08, 0.01, 10

V7X_VMEM_LIMIT = 56 * 1024 * 1024
MESH = pl.DeviceIdType.MESH


def _params(*sem, vmem=V7X_VMEM_LIMIT):
    return pltpu.CompilerParams(dimension_semantics=sem or None, vmem_limit_bytes=vmem)


def _resident(shape):
    return pl.BlockSpec(shape, lambda *_: (0,) * len(shape), pipeline_mode=pl.Buffered(1))


def _rows(tb, cols):
    return pl.BlockSpec((tb, cols), lambda i: (i, 0))


def _acc(shape):
    return pl.BlockSpec(shape, lambda *_: (0,) * len(shape))


def _dot(a, b):
    return jnp.dot(a, b, preferred_element_type=F32)


def _dot_nt(a, b):
    return lax.dot_general(a, b, (((1,), (1,)), ((), ())), preferred_element_type=F32)


def _dot_tn(a, b):
    return lax.dot_general(a, b, (((0,), (0,)), ((), ())), preferred_element_type=F32)


def _sigmoid(x):
    return 1.0 / (1.0 + jnp.exp(-x))


_GELU_C = 0.7978845608028654
_GELU_K = 0.044715


def _gelu_and_grad(x):
    u = x * x
    t = jnp.tanh(x * (_GELU_C + (_GELU_C * _GELU_K) * u))
    hp = 0.5 + 0.5 * t
    dg = hp + x * (0.5 - 0.5 * (t * t)) * (_GELU_C + (3.0 * _GELU_C * _GELU_K) * u)
    return x * hp, dg


def _gelu(x):
    return 0.5 * x * (1.0 + jnp.tanh(_GELU_C * (x + _GELU_K * x * x * x)))


def _ln_stats(z):
    mu = jnp.mean(z, axis=-1, keepdims=True)
    zc = z - mu
    var = jnp.mean(zc * zc, axis=-1, keepdims=True)
    rstd = lax.rsqrt(var + LN_EPS)
    return zc * rstd, rstd


def _ln_bwd(dy, xhat, rstd, g):
    dxh = dy * g
    m1 = jnp.mean(dxh, axis=-1, keepdims=True)
    m2 = jnp.mean(dxh * xhat, axis=-1, keepdims=True)
    return rstd * (dxh - m1 - xhat * m2)


def _softplus_neg(lam):
    u = jnp.exp(-jnp.abs(lam))
    w = 1.0 + u
    d = w - 1.0
    log1p_u = jnp.where(d == 0.0, u, jnp.log(w) * (u / jnp.where(d == 0.0, 1.0, d)))
    return jnp.maximum(-lam, 0.0) + log1p_u


def _shift_down(x, halo, s):
    xs = pltpu.roll(x, s, 0)
    hs = pltpu.roll(halo, s, 0)
    row8 = lax.broadcasted_iota(jnp.int32, hs.shape, 0)
    first = jnp.where(row8 < s, hs, xs[:8])
    return jnp.concatenate([first, xs[8:]], axis=0)


def _shift_up(x, halo, s):
    n = x.shape[0]
    xs = pltpu.roll(x, n - s, 0)
    hs = pltpu.roll(halo, 8 - s, 0)
    row8 = lax.broadcasted_iota(jnp.int32, hs.shape, 0)
    last = jnp.where(row8 >= 8 - s, hs, xs[n - 8:])
    return jnp.concatenate([xs[:n - 8], last], axis=0)


def _row_sum(x):
    return jnp.sum(x, axis=0, keepdims=True)


def _put_rows(acc_ref, rows):
    row8 = lax.broadcasted_iota(jnp.int32, acc_ref.shape, 0)
    upd = jnp.zeros(acc_ref.shape, F32)
    for r, vec in enumerate(rows):
        upd = jnp.where(row8 == r, vec, upd)
    acc_ref[...] += upd


def _place():
    return lax.axis_index("x"), lax.axis_index("y"), lax.axis_index("c")


def _dev_index(px, py, pc):
    return 4 * px + 2 * py + pc


_ANY = pl.BlockSpec(memory_space=pl.ANY)


class _Gather:
    def __init__(self, arrays):
        self.arrays = list(arrays)
        self.n = len(self.arrays)

    def out_shape(self):
        return [jax.ShapeDtypeStruct((N_DEV,) + s.shape, s.dtype) for s in self.arrays]

    def scratch(self):
        return [pltpu.SemaphoreType.DMA((self.n, 7)), pltpu.SemaphoreType.DMA((self.n, 7)),
                pltpu.SemaphoreType.DMA((self.n,))]

    def _parts(self, ins, outs, sems):
        send_sems, recv_sems, local_sems = sems
        x, y, c = _place()
        me, sibling = (x, y, c), (x, y, 1 - c)
        chips = [(1 - x, y), (x, 1 - y), (1 - x, 1 - y)]

        def copy(a, k, block, to, src=None):
            rows = outs[a].at[_dev_index(*block)]
            return pltpu.make_async_remote_copy(
                src_ref=rows if src is None else src, dst_ref=rows, send_sem=send_sems.at[a, k],
                recv_sem=recv_sems.at[a, k], device_id=to, device_id_type=MESH)

        rng = range(self.n)
        mine = [pltpu.make_async_copy(ins[a], outs[a].at[_dev_index(*me)], local_sems.at[a]) for a in rng]
        first = [copy(a, 0, me, sibling, src=ins[a]) for a in rng]
        first += [copy(a, 1 + j, me, (*chip, c), src=ins[a]) for j, chip in enumerate(chips) for a in rng]
        landed = [copy(a, 1 + j, (*chip, c), me) for j, chip in enumerate(chips) for a in rng]
        passed = [copy(a, 4 + j, (*chip, c), sibling) for j, chip in enumerate(chips) for a in rng]
        from_sibling = [copy(a, 0, sibling, me) for a in rng]
        from_sibling += [copy(a, 4 + j, (*chip, 1 - c), me) for j, chip in enumerate(chips) for a in rng]
        return mine, first, landed, passed, from_sibling

    def start(self, ins, outs, sems):
        mine, first, _, _, _ = self._parts(ins, outs, sems)
        for cp in mine + first:
            cp.start()

    def forward(self, ins, outs, sems):
        _, _, landed, passed, _ = self._parts(ins, outs, sems)
        for got, fwd in zip(landed, passed):
            got.wait_recv()
            fwd.start()

    def finish(self, ins, outs, sems):
        mine, first, _, passed, from_sibling = self._parts(ins, outs, sems)
        for cp in from_sibling:
            cp.wait_recv()
        for cp in first + passed:
            cp.wait_send()
        for cp in mine:
            cp.wait()

    def before(self, ins, outs, sems, step, nsteps):
        pl.when(step == 0)(lambda: self.start(ins, outs, sems))
        pl.when(step == (7 * nsteps) // 8)(lambda: self.forward(ins, outs, sems))

    def after(self, ins, outs, sems, step, nsteps):
        pl.when(step == nsteps - 1)(lambda: self.finish(ins, outs, sems))


class _Exchange:
    def __init__(self, arrays):
        self.arrays = list(arrays)
        self.n = len(self.arrays)

    def out_shape(self):
        return [jax.ShapeDtypeStruct(b.shape, b.dtype) for b in self.arrays]

    def scratch(self):
        return [pltpu.SemaphoreType.DMA((self.n, 7)), pltpu.SemaphoreType.DMA((self.n, 7)),
                pltpu.SemaphoreType.DMA((self.n,))]

    def _parts(self, ins, outs, sems):
        send_sems, recv_sems, local_sems = sems
        x, y, c = _place()
        me = _dev_index(x, y, c)
        peers = [(x ^ (k >> 2), y ^ ((k >> 1) & 1), c ^ (k & 1)) for k in range(1, N_DEV)]
        rng = range(self.n)
        mine = [pltpu.make_async_copy(ins[a].at[me], outs[a].at[me], local_sems.at[a]) for a in rng]
        sent = [pltpu.make_async_remote_copy(
            src_ref=ins[a].at[_dev_index(*to)], dst_ref=outs[a].at[me], send_sem=send_sems.at[a, k],
            recv_sem=recv_sems.at[a, k], device_id=to, device_id_type=MESH) for k, to in enumerate(peers) for a in rng]
        arrivals = [pltpu.make_async_remote_copy(
            src_ref=ins[a].at[me], dst_ref=outs[a].at[_dev_index(*frm)], send_sem=send_sems.at[a, k],
            recv_sem=recv_sems.at[a, k], device_id=frm, device_id_type=MESH) for k, frm in enumerate(peers) for a in rng]
        return mine, sent, arrivals

    def start(self, ins, outs, sems):
        mine, sent, _ = self._parts(ins, outs, sems)
        for cp in mine + sent:
            cp.start()

    def finish(self, ins, outs, sems):
        mine, sent, arrivals = self._parts(ins, outs, sems)
        for cp in arrivals:
            cp.wait_recv()
        for cp in sent:
            cp.wait_send()
        for cp in mine:
            cp.wait()

    def before(self, ins, outs, sems, step, nsteps):
        pl.when(step == 0)(lambda: self.start(ins, outs, sems))

    def after(self, ins, outs, sems, step, nsteps):
        pl.when(step == nsteps - 1)(lambda: self.finish(ins, outs, sems))


class _Bcast(_Exchange):
    def out_shape(self):
        return [jax.ShapeDtypeStruct((N_DEV,) + s.shape, s.dtype) for s in self.arrays]

    def _parts(self, ins, outs, sems):
        send_sems, recv_sems, local_sems = sems
        x, y, c = _place()
        me = _dev_index(x, y, c)
        peers = [(x ^ (k >> 2), y ^ ((k >> 1) & 1), c ^ (k & 1)) for k in range(1, N_DEV)]
        rng = range(self.n)
        mine = [pltpu.make_async_copy(ins[a], outs[a].at[me], local_sems.at[a]) for a in rng]
        sent = [pltpu.make_async_remote_copy(
            src_ref=ins[a], dst_ref=outs[a].at[me], send_sem=send_sems.at[a, k], recv_sem=recv_sems.at[a, k],
            device_id=to, device_id_type=MESH) for k, to in enumerate(peers) for a in rng]
        arrivals = [pltpu.make_async_remote_copy(
            src_ref=ins[a], dst_ref=outs[a].at[_dev_index(*frm)], send_sem=send_sems.at[a, k],
            recv_sem=recv_sems.at[a, k], device_id=frm, device_id_type=MESH) for k, frm in enumerate(peers) for a in rng]
        return mine, sent, arrivals


class _Multi:
    def __init__(self, comms):
        self.comms = list(comms)
        self.arrays = [arr for c in self.comms for arr in c.arrays]
        self.n = len(self.arrays)

    def out_shape(self):
        return [s for c in self.comms for s in c.out_shape()]

    def scratch(self):
        return [s for c in self.comms for s in c.scratch()]

    def _each(self, ins, outs, sems):
        a = 0
        for j, c in enumerate(self.comms):
            yield c, ins[a:a + c.n], outs[a:a + c.n], sems[3 * j:3 * j + 3]
            a += c.n

    def before(self, ins, outs, sems, step, nsteps):
        for c, ci, co, cs in self._each(ins, outs, sems):
            c.before(ci, co, cs, step, nsteps)

    def after(self, ins, outs, sems, step, nsteps):
        for c, ci, co, cs in self._each(ins, outs, sems):
            c.after(ci, co, cs, step, nsteps)


def _comm_call(comms, name):
    ns = [c.n for c in comms]
    n = sum(ns)

    def body(*refs):
        parts, a, s = [], 0, 2 * n
        for c in comms:
            parts.append((c, refs[a:a + c.n], refs[n + a:n + a + c.n], refs[s:s + 3]))
            a, s = a + c.n, s + 3
        for c, ins, outs, sems in parts:
            c.start(ins, outs, sems)
        for c, ins, outs, sems in parts:
            if isinstance(c, _Gather):
                c.forward(ins, outs, sems)
        for c, ins, outs, sems in parts:
            c.finish(ins, outs, sems)

    res = pl.pallas_call(
        body, name=name, in_specs=[_ANY] * n, out_specs=[_ANY] * n,
        out_shape=[s for c in comms for s in c.out_shape()], scratch_shapes=[s for c in comms for s in c.scratch()],
    )(*[arr for c in comms for arr in c.arrays])
    out, a = [], 0
    for k in ns:
        out.append(res[a:a + k])
        a += k
    return out


def _pcall(body, args, *, name, grid, in_specs, out_specs, out_shape, scratch_shapes=(), sem="parallel", comm=None,
           step_axis=0):
    sem = (sem,) * len(grid) if isinstance(sem, str) else sem
    if comm is None:
        res = pl.pallas_call(body, name=name, grid=grid, in_specs=in_specs, out_specs=out_specs, out_shape=out_shape,
                             scratch_shapes=list(scratch_shapes), compiler_params=_params(*sem))(*args)
        return res, []
    n_in, n_out, n_scr, n = len(in_specs), len(out_specs), len(scratch_shapes), comm.n
    nsteps = grid[step_axis]
    assert all(g == 1 for ax, g in enumerate(grid) if ax != step_axis)

    def hosted(*refs):
        ins, cin = refs[:n_in], refs[n_in:n_in + n]
        o0 = n_in + n
        outs, cout = refs[o0:o0 + n_out], refs[o0 + n_out:o0 + n_out + n]
        s0 = o0 + n_out + n
        scr, sems = refs[s0:s0 + n_scr], refs[s0 + n_scr:]
        step = pl.program_id(step_axis)
        comm.before(cin, cout, sems, step, nsteps)
        body(*ins, *outs, *scr)
        comm.after(cin, cout, sems, step, nsteps)

    res = pl.pallas_call(
        hosted, name=name, grid=grid, in_specs=list(in_specs) + [_ANY] * n, out_specs=list(out_specs) + [_ANY] * n,
        out_shape=list(out_shape) + comm.out_shape(), scratch_shapes=list(scratch_shapes) + comm.scratch(),
        compiler_params=_params(*(("arbitrary",) * len(grid))))(*args, *comm.arrays)
    return res[:n_out], res[n_out:]


def _load_row_halves(top_hbm, bot_hbm, full_s, sems):
    r = top_hbm.shape[1]
    copies = [pltpu.make_async_copy(top_hbm, full_s.at[:, :r, :], sems.at[0]),
              pltpu.make_async_copy(bot_hbm, full_s.at[:, r:, :], sems.at[1])]
    for cp in copies:
        cp.start()
    for cp in copies:
        cp.wait()


def _in_proj(x, w_in_t, comm=None):
    S = x.shape[0]
    tb = min(1024, S)

    def body(x_ref, w_ref, q_ref, k_ref, v_ref, xr_ref, gr_ref):
        u = _dot_nt(x_ref[...].astype(BF16), w_ref[...])
        q_ref[...] = (u[:, :D_ATT] * (HEAD_DIM ** -0.5)).astype(BF16)
        k_ref[...] = u[:, D_ATT:D_ATT + D_KV].astype(BF16)
        v_ref[...] = u[:, D_ATT + D_KV:D_ATT + 2 * D_KV].astype(BF16)
        xr_ref[...] = u[:, D_ATT + 2 * D_KV:D_ATT + 2 * D_KV + D_RNN]
        gr_ref[...] = u[:, D_ATT + 2 * D_KV + D_RNN:]

    return _pcall(
        body, (x, w_in_t), name="in_proj", grid=(S // tb,), comm=comm,
        in_specs=[_rows(tb, D_MODEL), _resident((D_IN, D_MODEL))],
        out_specs=[_rows(tb, D_ATT), _rows(tb, D_KV), _rows(tb, D_KV), _rows(tb, D_RNN), _rows(tb, D_RNN)],
        out_shape=[jax.ShapeDtypeStruct((S, D_ATT), BF16), jax.ShapeDtypeStruct((S, D_KV), BF16),
                   jax.ShapeDtypeStruct((S, D_KV), BF16), jax.ShapeDtypeStruct((S, D_RNN), F32),
                   jax.ShapeDtypeStruct((S, D_RNN), F32)])


GROUP = N_HEADS // N_KV


def _band_mask(i):
    qi = lax.broadcasted_iota(jnp.int32, (GROUP * QBLK, 2 * QBLK), 0) & (QBLK - 1)
    sj = lax.broadcasted_iota(jnp.int32, (GROUP * QBLK, 2 * QBLK), 1)
    return (sj > qi) & (sj <= qi + QBLK) & ((sj >= QBLK) | (i > 0))


def _stack_heads(x, g):
    return jnp.concatenate([x[:, (g * GROUP + hh) * HEAD_DIM:(g * GROUP + hh + 1) * HEAD_DIM] for hh in range(GROUP)],
                           axis=0)


def _unstack_heads(x4):
    return [x4[hh * QBLK:(hh + 1) * QBLK] for hh in range(GROUP)]


def _sink_column(sink_ref, g):
    head = lax.broadcasted_iota(jnp.int32, (GROUP * QBLK, 1), 0) // QBLK
    col = jnp.full((GROUP * QBLK, 1), sink_ref[g * GROUP], F32)
    for hh in range(1, GROUP):
        col = jnp.where(head == hh, sink_ref[g * GROUP + hh], col)
    return col


ATT_STEP = 4
IN_GRAD_PARTS = 2


def _attn_specs(nq=1):
    cur = lambda i: (i, 0)
    prev = lambda i: (jnp.maximum(nq * i - 1, 0), 0)
    return [pl.BlockSpec((nq * QBLK, D_KV), cur), pl.BlockSpec((QBLK, D_KV), prev),
            pl.BlockSpec((nq * QBLK, D_KV), cur), pl.BlockSpec((QBLK, D_KV), prev)]


def _attn_fwd(q, k, v, sinks, comm=None):
    S = q.shape[0]
    nq = min(ATT_STEP, S // QBLK)

    def body(sink_ref, q_ref, kc_ref, kp_ref, vc_ref, vp_ref, o_ref, lse_ref):
        first = pl.program_id(0) * nq
        kall = jnp.concatenate([kp_ref[...], kc_ref[...]], axis=0)
        vall = jnp.concatenate([vp_ref[...], vc_ref[...]], axis=0)
        for b in range(nq):
            valid = _band_mask(first + b)
            rows = slice(b * QBLK, (b + 1) * QBLK)
            keys = slice(b * QBLK, (b + 2) * QBLK)
            qv = q_ref[rows, :]
            outs = []
            for g in range(N_KV):
                kcat = kall[keys, g * HEAD_DIM:(g + 1) * HEAD_DIM]
                vcat = vall[keys, g * HEAD_DIM:(g + 1) * HEAD_DIM]
                s = jnp.where(valid, _dot_nt(_stack_heads(qv, g), kcat), -1e30)
                sink = _sink_column(sink_ref, g)
                m = jnp.maximum(jnp.max(s, axis=1, keepdims=True), sink)
                p = jnp.exp(s - m)
                l = jnp.sum(p, axis=1, keepdims=True) + jnp.exp(sink - m)
                outs += _unstack_heads(_dot(p.astype(BF16), vcat) / l)
                lse_ref[(b * N_KV + g) * GROUP * QBLK:(b * N_KV + g + 1) * GROUP * QBLK, :] = m + jnp.log(l)
            o_ref[rows, :] = jnp.concatenate(outs, axis=1).astype(BF16)

    lse_rows = nq * N_HEADS * QBLK
    return _pcall(
        body, (sinks, q, k, k, v, v), name="attn_fwd", grid=(S // (nq * QBLK),), comm=comm,
        in_specs=[pl.BlockSpec(memory_space=pltpu.SMEM), _rows(nq * QBLK, D_ATT)] + _attn_specs(nq),
        out_specs=[_rows(nq * QBLK, D_ATT), _rows(lse_rows, 1)],
        out_shape=[jax.ShapeDtypeStruct((S, D_ATT), BF16), jax.ShapeDtypeStruct((S * N_HEADS, 1), F32)])


def _w_rows(w_ref):
    return [w_ref[k:k + 1, :] for k in range(w_ref.shape[0])]


def _conv4(x, halo, w, b):
    y = b + w[3] * x
    for s in (1, 2, 3):
        y = y + w[3 - s] * _shift_down(x, halo, s)
    return y


def _rnn_gates(xc, wa, wx, ba, bx, sp):
    xcb = xc.astype(BF16)
    r = _sigmoid(_dot(xcb, wa) + ba)
    ig = _sigmoid(_dot(xcb, wx) + bx)
    la = -LRU_C * r * sp
    a = jnp.exp(la)
    t = jnp.tanh(la)
    f = jnp.sqrt(-2.0 * t / (1.0 - t))
    return r, ig, a, f


def _rnn_fwd(xr, gr, conv_w, conv_b, wa, wx, ba, bx, lam, comm=None):
    S = xr.shape[0]
    tb = min(512, S)

    def body(xr_ref, gr_ref, cw_ref, cb_ref, wa_ref, wx_ref, ba_ref, bx_ref, lam_ref, rec_ref, h_ref,
             xc_ref, r_ref, ig_ref, a_ref, f_ref, halo_s, hc_s, a_s, b_s):
        @pl.when(pl.program_id(0) == 0)
        def _():
            halo_s[...] = jnp.zeros_like(halo_s)
            hc_s[...] = jnp.zeros_like(hc_s)

        x = xr_ref[...]
        xc = _conv4(x, halo_s[...], _w_rows(cw_ref), cb_ref[...])
        halo_s[...] = x[tb - 8:]
        r, ig, a, f = _rnn_gates(xc, wa_ref[...], wx_ref[...], ba_ref[...], bx_ref[...], _softplus_neg(lam_ref[...]))
        xc_ref[...] = xc
        r_ref[...] = r
        ig_ref[...] = ig
        a_ref[...] = a
        f_ref[...] = f
        a_s[...] = a
        b_s[...] = f * ig * xc
        row8 = lax.broadcasted_iota(jnp.int32, (8, D_RNN), 0)

        def tile(t, hc):
            o = pl.multiple_of(t * 8, 8)
            at = a_s[pl.ds(o, 8), :]
            bt = b_s[pl.ds(o, 8), :]
            for s in (1, 2, 4):
                keep = row8 >= s
                a_sh = jnp.where(keep, pltpu.roll(at, s, 0), 1.0)
                b_sh = jnp.where(keep, pltpu.roll(bt, s, 0), 0.0)
                bt = at * b_sh + bt
                at = at * a_sh
            ht = at * hc + bt
            b_s[pl.ds(o, 8), :] = ht
            return _row_sum(jnp.where(row8 == 7, ht, 0.0))

        hc_s[0:1, :] = lax.fori_loop(0, tb // 8, tile, hc_s[0:1, :], unroll=2)
        h = b_s[...]
        h_ref[...] = h
        rec_ref[...] = (h * _gelu(gr_ref[...])).astype(BF16)

    vec = _resident((1, D_RNN))
    kept = jax.ShapeDtypeStruct((S, D_RNN), F32)
    return _pcall(
        body, (xr, gr, conv_w, conv_b, wa, wx, ba, bx, lam), name="rnn_fwd", grid=(S // tb,), sem="arbitrary", comm=comm,
        in_specs=[_rows(tb, D_RNN), _rows(tb, D_RNN), _resident((4, D_RNN)), vec,
                  _resident((D_RNN, D_RNN)), _resident((D_RNN, D_RNN)), vec, vec, vec],
        out_specs=[_rows(tb, D_RNN)] * 7,
        out_shape=[jax.ShapeDtypeStruct((S, D_RNN), BF16), kept, kept, kept, kept, kept, kept],
        scratch_shapes=[pltpu.VMEM((8, D_RNN), F32), pltpu.VMEM((8, D_RNN), F32),
                        pltpu.VMEM((tb, D_RNN), F32), pltpu.VMEM((tb, D_RNN), F32)])


def _mix_ln1_up(x, att, rec, w_out, ln1_g, ln1_b, w_up_top, w_up_bot, fcw, fcb, comm=None):
    S = x.shape[0]
    tb = min(256, S)
    nblk, kh, wblk = w_up_top.shape
    half = nblk // 2

    def body(x_ref, att_ref, rec_ref, wo_ref, g_ref, b_ref, wt_hbm, wb_hbm, fcw_ref, fcb_ref,
             z1_ref, h1b_ref, gate_ref, act_ref, gl_ref, vdgl_ref, halo_s, wu_s, wu_sems):
        @pl.when(pl.program_id(0) == 0)
        def _():
            halo_s[...] = jnp.zeros_like(halo_s)
            _load_row_halves(wt_hbm, wb_hbm, wu_s, wu_sems)

        z1 = ALPHA * x_ref[...] + _dot(att_ref[...], wo_ref[:D_ATT, :]) + _dot(rec_ref[...], wo_ref[D_ATT:, :])
        z1_ref[...] = z1
        xhat, _ = _ln_stats(z1)
        h1b = (xhat * g_ref[...] + b_ref[...]).astype(BF16)
        h1b_ref[...] = h1b
        for jj in range(half):
            cols = slice(jj * wblk, (jj + 1) * wblk)
            gate = _dot(h1b, wu_s[jj])
            val = _dot(h1b, wu_s[jj + half])
            halo = halo_s[:, cols]
            conv = (fcb_ref[:, cols] + fcw_ref[2:3, cols] * gate + fcw_ref[1:2, cols] * _shift_down(gate, halo, 1)
                    + fcw_ref[0:1, cols] * _shift_down(gate, halo, 2))
            halo_s[:, cols] = gate[tb - 8:]
            gl, dgl = _gelu_and_grad(conv)
            gate_ref[:, cols] = gate.astype(BF16)
            act_ref[:, cols] = (gl * val).astype(BF16)
            gl_ref[:, cols] = gl.astype(BF16)
            vdgl_ref[:, cols] = (val * dgl).astype(BF16)

    vec = _resident((1, D_MODEL))
    wide = jax.ShapeDtypeStruct((S, D_FF), BF16)
    return _pcall(
        body, (x, att, rec, w_out, ln1_g, ln1_b, w_up_top, w_up_bot, fcw, fcb), name="mix_ln1_up", grid=(S // tb,),
        sem="arbitrary", comm=comm,
        in_specs=[_rows(tb, D_MODEL), _rows(tb, D_ATT), _rows(tb, D_RNN), _resident((D_MODEL, D_MODEL)), vec, vec,
                  _ANY, _ANY, _resident((3, D_FF)), _resident((1, D_FF))],
        out_specs=[_rows(tb, D_MODEL), _rows(tb, D_MODEL)] + [_rows(tb, D_FF)] * 4,
        out_shape=[jax.ShapeDtypeStruct((S, D_MODEL), F32), jax.ShapeDtypeStruct((S, D_MODEL), BF16), wide, wide, wide, wide],
        scratch_shapes=[pltpu.VMEM((8, D_FF), F32), pltpu.VMEM((nblk, 2 * kh, wblk), BF16),
                        pltpu.SemaphoreType.DMA((2,))])


def _tail(act, gl, vdgl, z1, h1b, p, tgt, w_down, w_pg, b_pg, w_pp, ln1_g, ln1_b, ln2_g, ln2_b):
    S = z1.shape[0]
    tb = min(256, S)

    def body(act_ref, gl_ref, vdgl_ref, z1_ref, h1b_ref, p_ref, t_ref, wd_ref, wpg_ref, bpg_ref, wpp_ref,
             g1_ref, b1_ref, g2_ref, b2_ref, dz2_ref, dpre_ref, dpp_ref, dgc_ref, dval_ref, dh1_ref, acc_ref):
        i = pl.program_id(0)

        @pl.when(i == 0)
        def _():
            acc_ref[...] = jnp.zeros_like(acc_ref)

        ffn = _dot(act_ref[...], wd_ref[...])
        xhat1, _ = _ln_stats(z1_ref[...])
        h1 = xhat1 * g1_ref[...] + b1_ref[...]
        sg = _sigmoid(_dot(h1b_ref[...], wpg_ref[...]) + bpg_ref[...])
        pp = _dot(p_ref[...].astype(BF16), wpp_ref[...])
        z2 = ALPHA * h1 + ffn + sg * pp
        xhat2, rstd2 = _ln_stats(z2)
        y = xhat2 * g2_ref[...] + b2_ref[...]
        err = y - t_ref[...]
        dy = err * (1.0 / D_MODEL)
        loss = 0.5 * jnp.sum(jnp.sum(err * err, axis=1, keepdims=True), axis=0, keepdims=True) * (1.0 / D_MODEL)
        dz2 = _ln_bwd(dy, xhat2, rstd2, g2_ref[...])
        dz2b = dz2.astype(BF16)
        dz2_ref[...] = dz2b
        dpre = dz2 * pp * sg * (1.0 - sg)
        dpreb = dpre.astype(BF16)
        dpre_ref[...] = dpreb
        dpp_ref[...] = (dz2 * sg).astype(BF16)
        dh1_ref[...] = ALPHA * dz2 + _dot_nt(dpreb, wpg_ref[...])
        dactb = _dot_nt(dz2b, wd_ref[...]).astype(BF16)
        dval_ref[...] = dactb * gl_ref[...]
        dgc_ref[...] = dactb * vdgl_ref[...]
        _put_rows(acc_ref, [_row_sum(dy * xhat2), _row_sum(dy), _row_sum(dpre),
                            jnp.broadcast_to(loss, (1, D_MODEL))])

    vec = _resident((1, D_MODEL))
    return pl.pallas_call(
        body, name="tail", grid=(S // tb,),
        in_specs=[_rows(tb, D_FF), _rows(tb, D_FF), _rows(tb, D_FF), _rows(tb, D_MODEL), _rows(tb, D_MODEL),
                  _rows(tb, PLE_DIM), _rows(tb, D_MODEL), _resident((D_FF, D_MODEL)), _resident((D_MODEL, D_MODEL)), vec,
                  _resident((PLE_DIM, D_MODEL)), vec, vec, vec, vec],
        out_specs=[_rows(tb, D_MODEL), _rows(tb, D_MODEL), _rows(tb, D_MODEL), _rows(tb, D_FF),
                   _rows(tb, D_FF), _rows(tb, D_MODEL), _acc((8, D_MODEL))],
        out_shape=[jax.ShapeDtypeStruct((S, D_MODEL), BF16),
                   jax.ShapeDtypeStruct((S, D_MODEL), BF16), jax.ShapeDtypeStruct((S, D_MODEL), BF16),
                   jax.ShapeDtypeStruct((S, D_FF), BF16), jax.ShapeDtypeStruct((S, D_FF), BF16),
                   jax.ShapeDtypeStruct((S, D_MODEL), F32), jax.ShapeDtypeStruct((8, D_MODEL), F32)],
        compiler_params=_params("arbitrary"),
    )(act, gl, vdgl, z1, h1b, p, tgt, w_down, w_pg, b_pg, w_pp, ln1_g, ln1_b, ln2_g, ln2_b)


def _weight_grad(a_list, b_list, name, layout, ts=512, comm=None, b_window=None, halves=False):
    S = a_list[0].shape[0]
    ms = [a.shape[1] for a in a_list]
    M, nb = sum(ms), len(b_list)
    win, Nb = b_window if b_window else (0, b_list[0].shape[1])
    ts = min(ts, S)
    nk = S // ts
    per_b = N_DEV // nb
    na = len(a_list)

    n_out = 2 if halves else 1
    assert layout == "cols" or not halves

    def body(*refs):
        a_refs, b_refs, o_refs, acc_ref = refs[:na], refs[na:na + nb], refs[na + nb:na + nb + n_out], refs[-1]
        o_ref = o_refs[0]
        j, k = pl.program_id(0), pl.program_id(1)

        @pl.when(k == 0)
        def _():
            acc_ref[...] = jnp.zeros_like(acc_ref)

        for jj in range(nb):
            @pl.when(j == jj)
            def _():
                b = b_refs[jj][...].astype(BF16)
                off = 0
                for a_ref, m in zip(a_refs, ms):
                    acc_ref[off:off + m, :] += _dot_tn(a_ref[...].astype(BF16), b)
                    off += m

        @pl.when(k == nk - 1)
        def _():
            for d in range(per_b):
                if layout == "rows":
                    o_ref[d] = acc_ref[d * (M // N_DEV):(d + 1) * (M // N_DEV), :].astype(BF16)
                elif layout == "cols" and halves:
                    for o_half, r0 in zip(o_refs, (0, M // 2)):
                        o_half[d] = acc_ref[r0:r0 + M // 2, d * (Nb // per_b):(d + 1) * (Nb // per_b)].astype(BF16)
                elif layout == "cols":
                    o_ref[d] = acc_ref[:, d * (Nb // per_b):(d + 1) * (Nb // per_b)].astype(BF16)
                else:
                    o_ref[d] = acc_ref[:, d * (Nb // per_b):(d + 1) * (Nb // per_b)].T.astype(BF16)

    def b_index(jj):
        return lambda j, k: (jnp.where(j == jj, k, jnp.where(j < jj, 0, nk - 1)), win)

    if layout == "rows":
        assert nb == 1
        blk = (N_DEV, M // N_DEV, Nb)
    elif layout == "cols":
        blk = (per_b, M // n_out, Nb // per_b)
    else:
        blk = (per_b, Nb // per_b, M)
    res, comm_res = _pcall(
        body, (*a_list, *b_list), name=name, grid=(nb, nk), sem="arbitrary", comm=comm, step_axis=1,
        in_specs=[pl.BlockSpec((ts, m), lambda j, k: (k, 0)) for m in ms]
        + [pl.BlockSpec((ts, Nb), b_index(jj)) for jj in range(nb)],
        out_specs=[pl.BlockSpec(blk, lambda j, k: (j, 0, 0))] * n_out,
        out_shape=[jax.ShapeDtypeStruct((N_DEV,) + blk[1:], BF16)] * n_out,
        scratch_shapes=[pltpu.VMEM((M, Nb), F32)])
    res = res if halves else res[0]
    return (res, comm_res) if comm is not None else res


def _up_bwd(dgc, gate, dval, dh1p, z1, w_up_top, w_up_bot, fcw, w_out, ln1_g, comm=None):
    S = z1.shape[0]
    tb = min(256, S)
    t16 = tb // 16
    n16 = S // 16
    nblk, kh, wblk = w_up_top.shape
    half = nblk // 2
    nsteps = S // tb

    def body(dgc_ref, dgn_ref, gc_ref, dval_ref, dh1p_ref, z1_ref, wt_hbm, wb_hbm, fcw_ref, wo_ref, g1_ref,
             dgate_ref, dz1_ref, dz1b_ref, datt_ref, drec_ref, accf_ref, accd_ref, wu_s, wu_sems):
        i = pl.program_id(0)

        @pl.when(i == 0)
        def _():
            accf_ref[...] = jnp.zeros_like(accf_ref)
            accd_ref[...] = jnp.zeros_like(accd_ref)
            _load_row_halves(wt_hbm, wb_hbm, wu_s, wu_sems)

        dg = dgc_ref[...].astype(F32)
        nxt = jnp.where(i < nsteps - 1, dgn_ref[...].astype(F32)[0:8], 0.0)
        w = _w_rows(fcw_ref)
        up1, up2 = _shift_up(dg, nxt, 1), _shift_up(dg, nxt, 2)
        dgate = (w[2] * dg + w[1] * up1 + w[0] * up2).astype(BF16)
        dgate_ref[...] = dgate
        gate = gc_ref[...].astype(F32)
        _put_rows(accf_ref, [_row_sum(up2 * gate), _row_sum(up1 * gate), _row_sum(dg * gate), _row_sum(dg)])

        dh1 = dh1p_ref[...]
        for j in range(nblk):
            src = dgate if j < half else dval_ref[...]
            jj = j % half
            dh1 = dh1 + _dot_nt(src[:, jj * wblk:(jj + 1) * wblk], wu_s[j])
        xhat1, rstd1 = _ln_stats(z1_ref[...])
        dz1 = _ln_bwd(dh1, xhat1, rstd1, g1_ref[...])
        dz1_ref[...] = dz1
        dz1b = dz1.astype(BF16)
        dz1b_ref[...] = dz1b
        dcat = _dot_nt(dz1b, wo_ref[...])
        datt_ref[...] = dcat[:, :D_ATT].astype(BF16)
        drec_ref[...] = dcat[:, D_ATT:]
        _put_rows(accd_ref, [_row_sum(dh1 * xhat1), _row_sum(dh1)])

    next16 = pl.BlockSpec((16, D_FF), lambda i: (jnp.minimum((i + 1) * t16, n16 - 1), 0))
    return _pcall(
        body, (dgc, dgc, gate, dval, dh1p, z1, w_up_top, w_up_bot, fcw, w_out, ln1_g), name="up_bwd",
        grid=(nsteps,), sem="arbitrary", comm=comm,
        in_specs=[_rows(tb, D_FF), next16, _rows(tb, D_FF), _rows(tb, D_FF), _rows(tb, D_MODEL),
                  _rows(tb, D_MODEL), _ANY, _ANY, _resident((3, D_FF)),
                  _resident((D_MODEL, D_MODEL)), _resident((1, D_MODEL))],
        scratch_shapes=[pltpu.VMEM((nblk, 2 * kh, wblk), BF16), pltpu.SemaphoreType.DMA((2,))],
        out_specs=[_rows(tb, D_FF), _rows(tb, D_MODEL), _rows(tb, D_MODEL), _rows(tb, D_ATT), _rows(tb, D_RNN),
                   _acc((8, D_FF)), _acc((8, D_MODEL))],
        out_shape=[jax.ShapeDtypeStruct((S, D_FF), BF16), jax.ShapeDtypeStruct((S, D_MODEL), F32),
                   jax.ShapeDtypeStruct((S, D_MODEL), BF16), jax.ShapeDtypeStruct((S, D_ATT), BF16),
                   jax.ShapeDtypeStruct((S, D_RNN), F32), jax.ShapeDtypeStruct((8, D_FF), F32),
                   jax.ShapeDtypeStruct((8, D_MODEL), F32)])


def _attn_bwd(q, k, v, lse, do, sinks, comm=None):
    S = q.shape[0]
    grp = N_HEADS // N_KV
    nq = min(ATT_STEP, S // QBLK)

    def body(sink_ref, q_ref, kc_ref, kp_ref, vc_ref, vp_ref, do_ref, lse_ref, dq_ref, dkc_ref, dkp_ref, dvc_ref, dvp_ref,
             ds_ref):
        i = pl.program_id(0)

        @pl.when(i == 0)
        def _():
            ds_ref[...] = jnp.zeros_like(ds_ref)

        row8 = lax.broadcasted_iota(jnp.int32, (8, 128), 0)
        lane8 = lax.broadcasted_iota(jnp.int32, (8, 128), 1)
        dsink = jnp.zeros((8, 128), F32)
        kall = jnp.concatenate([kp_ref[...], kc_ref[...]], axis=0)
        vall = jnp.concatenate([vp_ref[...], vc_ref[...]], axis=0)
        dk_t = [jnp.zeros((D_KV, QBLK), F32) for _ in range(nq + 1)]
        dv_t = [jnp.zeros((D_KV, QBLK), F32) for _ in range(nq + 1)]
        for b in range(nq):
            valid = _band_mask(i * nq + b)
            rows = slice(b * QBLK, (b + 1) * QBLK)
            keys = slice(b * QBLK, (b + 2) * QBLK)
            qv, dov = q_ref[rows, :], do_ref[rows, :]
            dqs, dks, dvs = [], [], []
            for g in range(N_KV):
                kcat = kall[keys, g * HEAD_DIM:(g + 1) * HEAD_DIM]
                vcat = vall[keys, g * HEAD_DIM:(g + 1) * HEAD_DIM]
                q4, do4 = _stack_heads(qv, g), _stack_heads(dov, g)
                s = jnp.where(valid, _dot_nt(q4, kcat), -1e30)
                lse = lse_ref[(b * N_KV + g) * GROUP * QBLK:(b * N_KV + g + 1) * GROUP * QBLK, :]
                p = jnp.exp(s - lse)
                p_sink = jnp.exp(_sink_column(sink_ref, g) - lse)
                dp = _dot_nt(do4, vcat)
                delta = jnp.sum(p * dp, axis=1, keepdims=True)
                dsc = (p * (dp - delta)).astype(BF16)
                dqs += _unstack_heads(_dot(dsc, kcat) * (HEAD_DIM ** -0.5))
                dks.append(_dot_tn(q4, dsc))
                dvs.append(_dot_tn(do4, p.astype(BF16)))
                for hh, part in enumerate(_unstack_heads(-p_sink * delta)):
                    here = (row8 == 0) & (lane8 == g * grp + hh)
                    dsink = dsink + jnp.where(here, jnp.sum(part, axis=0, keepdims=True), 0.0)
            dq_ref[rows, :] = jnp.concatenate(dqs, axis=1).astype(BF16)
            dk2, dv2 = jnp.concatenate(dks, axis=0), jnp.concatenate(dvs, axis=0)
            dk_t[b], dk_t[b + 1] = dk_t[b] + dk2[:, :QBLK], dk_t[b + 1] + dk2[:, QBLK:]
            dv_t[b], dv_t[b + 1] = dv_t[b] + dv2[:, :QBLK], dv_t[b + 1] + dv2[:, QBLK:]
        dkp_ref[...] = dk_t[0].T
        dvp_ref[...] = dv_t[0].T
        for b in range(nq):
            dkc_ref[b * QBLK:(b + 1) * QBLK, :] = dk_t[b + 1].T
            dvc_ref[b * QBLK:(b + 1) * QBLK, :] = dv_t[b + 1].T
        ds_ref[...] += dsink

    nsteps = S // (nq * QBLK)
    cur = jax.ShapeDtypeStruct((S, D_KV), F32)
    prev = jax.ShapeDtypeStruct((nsteps * QBLK, D_KV), F32)
    big = _rows(nq * QBLK, D_ATT)
    return _pcall(
        body, (sinks, q, k, k, v, v, do, lse), name="attn_bwd", grid=(nsteps,), sem="arbitrary", comm=comm,
        in_specs=[pl.BlockSpec(memory_space=pltpu.SMEM), big] + _attn_specs(nq) + [big, _rows(nq * N_HEADS * QBLK, 1)],
        out_specs=[big, _rows(nq * QBLK, D_KV), _rows(QBLK, D_KV), _rows(nq * QBLK, D_KV), _rows(QBLK, D_KV),
                   _acc((8, 128))],
        out_shape=[jax.ShapeDtypeStruct((S, D_ATT), BF16), cur, prev, cur, prev, jax.ShapeDtypeStruct((8, 128), F32)])


def _rnn_bwd(xr, gr, h, kept, drec, conv_w, wa, wx, lam, comm=None):
    S = xr.shape[0]
    tb = min(512, S)
    t8 = tb // 8
    nsteps = S // tb

    def body(xr_ref, xp_ref, gr_ref, h_ref, hp_ref, xc_ref, r_ref, ig_ref, a_ref, f_ref, drec_ref, cw_ref, wa_ref, wx_ref,
             lam_ref, dxr_ref, dgr_ref, gwa_ref, gwx_ref, acc_ref, carry_s, dxc_halo_s, d_s, gwa_s, gwx_s):
        i = pl.program_id(0)
        blk = nsteps - 1 - i

        @pl.when(i == 0)
        def _():
            gwa_s[...] = jnp.zeros_like(gwa_s)
            gwx_s[...] = jnp.zeros_like(gwx_s)
            acc_ref[...] = jnp.zeros_like(acc_ref)
            carry_s[...] = jnp.zeros_like(carry_s)
            dxc_halo_s[...] = jnp.zeros_like(dxc_halo_s)

        x = xr_ref[...]
        xhalo = jnp.where(blk > 0, xp_ref[...], 0.0)
        cw = _w_rows(cw_ref)
        xs = [_shift_down(x, xhalo, 3), _shift_down(x, xhalo, 2), _shift_down(x, xhalo, 1), x]
        xc, r, ig, a, f = xc_ref[...], r_ref[...], ig_ref[...], a_ref[...], f_ref[...]
        sp = _softplus_neg(lam_ref[...])
        hcur = h_ref[...]
        hprev = _shift_down(hcur, jnp.where(blk > 0, hp_ref[...], 0.0), 1)
        gl, dgl = _gelu_and_grad(gr_ref[...])
        drec = drec_ref[...]
        dgr_ref[...] = (drec * hcur * dgl).astype(BF16)
        d_s[...] = drec * gl
        row8 = lax.broadcasted_iota(jnp.int32, (8, D_RNN), 0)

        def tile(t, c):
            o = pl.multiple_of((t8 - 1 - t) * 8, 8)
            a8 = a_ref[pl.ds(o, 8), :]
            dt = d_s[pl.ds(o, 8), :]
            at = jnp.where(row8 == 7, 1.0, pltpu.roll(a8, 7, 0))
            for s in (1, 2, 4):
                keep = row8 < 8 - s
                a_sh = jnp.where(keep, pltpu.roll(at, 8 - s, 0), 1.0)
                d_sh = jnp.where(keep, pltpu.roll(dt, 8 - s, 0), 0.0)
                dt = at * d_sh + dt
                at = at * a_sh
            lt = at * c + dt
            d_s[pl.ds(o, 8), :] = lt
            return _row_sum(jnp.where(row8 == 0, a8 * lt, 0.0))

        carry_s[0:1, :] = lax.fori_loop(0, t8, tile, carry_s[0:1, :], unroll=2)
        lmb = d_s[...]
        a2 = a * a
        dla = lmb * hprev * a - lmb * ig * xc * (a2 / f)
        di = lmb * f * xc
        dr = dla * (-LRU_C) * sp
        dpa = dr * r * (1.0 - r)
        dpx = di * ig * (1.0 - ig)
        dpab = dpa.astype(BF16)
        dpxb = dpx.astype(BF16)
        xcb = xc.astype(BF16)
        gwa_s[...] += _dot_tn(xcb, dpab)
        gwx_s[...] += _dot_tn(xcb, dpxb)

        @pl.when(i == nsteps - 1)
        def _():
            for dense, out in ((gwa_s[...], gwa_ref), (gwx_s[...], gwx_ref)):
                for b in range(RNN_BLOCKS):
                    rows = slice(b * HEAD_DIM, (b + 1) * HEAD_DIM)
                    out[rows, :] = dense[rows, b * HEAD_DIM:(b + 1) * HEAD_DIM]

        dxc = lmb * f * ig + _dot_nt(dpab, wa_ref[...]) + _dot_nt(dpxb, wx_ref[...])
        nxt = dxc_halo_s[...]
        dxr = cw[3] * dxc
        for s in (1, 2, 3):
            dxr = dxr + cw[3 - s] * _shift_up(dxc, nxt, s)
        dxr_ref[...] = dxr.astype(BF16)
        dxc_halo_s[...] = dxc[:8]
        dlam = _row_sum(dla * (-LRU_C) * r) * (-1.0 / (1.0 + jnp.exp(lam_ref[...])))
        _put_rows(acc_ref, [_row_sum(dxc * xs[0]), _row_sum(dxc * xs[1]), _row_sum(dxc * xs[2]), _row_sum(dxc * xs[3]),
                            _row_sum(dxc), _row_sum(dpa), _row_sum(dpx), dlam])

    rev = lambda i: (nsteps - 1 - i, 0)
    prev8 = lambda i: (jnp.maximum((nsteps - 1 - i) * t8 - 1, 0), 0)
    blkspec = pl.BlockSpec((tb, D_RNN), rev)
    halo8 = pl.BlockSpec((8, D_RNN), prev8)
    vec = _resident((1, D_RNN))
    return _pcall(
        body, (xr, xr, gr, h, h, *kept, drec, conv_w, wa, wx, lam), name="rnn_bwd", grid=(nsteps,),
        sem="arbitrary", comm=comm,
        in_specs=[blkspec, halo8, blkspec, blkspec, halo8] + [blkspec] * 6
        + [_resident((4, D_RNN)), _resident((D_RNN, D_RNN)), _resident((D_RNN, D_RNN)), vec],
        out_specs=[blkspec, blkspec, _acc((D_RNN, HEAD_DIM)), _acc((D_RNN, HEAD_DIM)), _acc((8, D_RNN))],
        out_shape=[jax.ShapeDtypeStruct((S, D_RNN), BF16), jax.ShapeDtypeStruct((S, D_RNN), BF16),
                   jax.ShapeDtypeStruct((D_RNN, HEAD_DIM), F32), jax.ShapeDtypeStruct((D_RNN, HEAD_DIM), F32),
                   jax.ShapeDtypeStruct((8, D_RNN), F32)],
        scratch_shapes=[pltpu.VMEM((8, D_RNN), F32), pltpu.VMEM((8, D_RNN), F32), pltpu.VMEM((tb, D_RNN), F32),
                        pltpu.VMEM((D_RNN, D_RNN), F32), pltpu.VMEM((D_RNN, D_RNN), F32)])


def _in_bwd(dq, dkc, dkp, dvc, dvp, dxr, dgr, dz1, w_in, comm=None):
    S = dz1.shape[0]
    tb = min(ATT_STEP * QBLK, S)
    nsteps = S // tb
    ring = 3

    def body(dq_ref, dkc_ref, dkn_ref, dvc_ref, dvn_ref, dxr_ref, dgr_ref, dz1_hbm, w_ref, dkv_ref, dx_ref,
             ring_s, ring_sems):
        i = pl.program_id(0)
        last = i == nsteps - 1

        def fetch(step):
            slot = step % ring
            rows = pl.ds(pl.multiple_of(step * tb, tb), tb)
            return pltpu.make_async_copy(dz1_hbm.at[rows, :], ring_s.at[slot], ring_sems.at[slot])

        @pl.when(i == 0)
        def _():
            for step in range(min(ring - 1, nsteps)):
                fetch(step).start()

        @pl.when(i + ring - 1 < nsteps)
        def _():
            fetch(i + ring - 1).start()

        def total(cur_ref, next_ref):
            nxt = jnp.where(last, 0.0, next_ref[...])
            tail = cur_ref[tb - QBLK:, :] + nxt
            return jnp.concatenate([cur_ref[:tb - QBLK, :], tail], axis=0) if tb > QBLK else tail

        dkv = jnp.concatenate([total(dkc_ref, dkn_ref), total(dvc_ref, dvn_ref)], axis=1).astype(BF16)
        dkv_ref[...] = dkv
        du = jnp.concatenate([dq_ref[...], dkv, dxr_ref[...], dgr_ref[...]], axis=1)
        prod = _dot(du, w_ref[...])
        fetch(i).wait()
        dx_ref[...] = ALPHA * ring_s[i % ring] + prod

    nextp = pl.BlockSpec((QBLK, D_KV), lambda i: (jnp.minimum(i + 1, nsteps - 1), 0))
    return _pcall(
        body, (dq, dkc, dkp, dvc, dvp, dxr, dgr, dz1, w_in), name="in_bwd", grid=(nsteps,), sem="arbitrary", comm=comm,
        in_specs=[_rows(tb, D_ATT), _rows(tb, D_KV), nextp, _rows(tb, D_KV), nextp,
                  _rows(tb, D_RNN), _rows(tb, D_RNN), _ANY, _resident((D_IN, D_MODEL))],
        out_specs=[_rows(tb, 2 * D_KV), _rows(tb, D_MODEL)],
        out_shape=[jax.ShapeDtypeStruct((S, 2 * D_KV), BF16), jax.ShapeDtypeStruct((S, D_MODEL), F32)],
        scratch_shapes=[pltpu.VMEM((ring, tb, D_MODEL), F32), pltpu.SemaphoreType.DMA((ring,))])


def _block_diag(w):
    eye = jnp.eye(RNN_BLOCKS, dtype=w.dtype)
    return (w[:, :, None, :] * eye[:, None, :, None]).reshape(D_RNN, D_RNN).astype(BF16)


def _adamw(w, g, m, v):
    m = ADAM_B1 * m + (1.0 - ADAM_B1) * g
    v = ADAM_B2 * v + (1.0 - ADAM_B2) * (g * g)
    m_hat = m / (1.0 - ADAM_B1 ** ADAM_STEP)
    v_hat = v / (1.0 - ADAM_B2 ** ADAM_STEP)
    delta = -ADAM_LR * (m_hat / (jnp.sqrt(v_hat) + ADAM_EPS) + ADAM_WD * w)
    return delta, m, v


def _sum_adamw(parts, w, m, v, name):
    parts = parts if isinstance(parts, (list, tuple)) else [parts]
    R, C = w.shape
    rb = R if R <= 256 else (256 if parts[0].shape[1] % 256 == 0 else 128)
    per = parts[0].shape[1] // rb
    assert R % rb == 0 and parts[0].shape[1] % rb == 0
    n = len(parts)

    def body(*refs):
        p_refs = refs[:n]
        w_ref, m_ref, v_ref, g_out, d_out, m_out, v_out = refs[n:]
        which = pl.program_id(0) // per

        def total(p_ref):
            g = p_ref[0].astype(F32)
            for d in range(1, N_DEV):
                g = g + p_ref[d].astype(F32)
            return g

        g = total(p_refs[0])
        for j in range(1, n):
            g = jnp.where(which == j, total(p_refs[j]), g)
        delta, mn, vn = _adamw(w_ref[...], g, m_ref[...], v_ref[...])
        g_out[...] = g
        d_out[...] = delta
        m_out[...] = mn
        v_out[...] = vn

    def part_spec(j):
        return pl.BlockSpec((N_DEV, rb, C), lambda i: (0, jnp.clip(i - j * per, 0, per - 1), 0))

    blk = _rows(rb, C)
    out = jax.ShapeDtypeStruct((R, C), F32)
    return pl.pallas_call(
        body, name=name, grid=(R // rb,),
        in_specs=[part_spec(j) for j in range(n)] + [blk, blk, blk],
        out_specs=[blk, blk, blk, blk], out_shape=[out, out, out, out],
        compiler_params=_params("parallel"),
    )(*parts, w, m, v)


_SMALL = [("attn_sinks", "s", 0, 1, None), ("rnn_conv_w", "r", 0, 4, "cols"), ("rnn_conv_b", "r", 4, 1, None),
          ("gate_a_w", "a", 0, D_RNN, None), ("gate_a_b", "r", 5, 1, None), ("gate_x_w", "x", 0, D_RNN, None),
          ("gate_x_b", "r", 6, 1, None), ("lru_lambda", "r", 7, 1, None), ("ln1_g", "d", 0, 1, None),
          ("ln1_b", "d", 1, 1, None), ("ffn_conv_w", "f", 0, 3, "cols"), ("ffn_conv_b", "f", 3, 1, None),
          ("ple_gate_b", "t", 2, 1, None), ("ln2_g", "t", 0, 1, None), ("ln2_b", "t", 1, 1, None)]
_LOSS_ROW = 3


_ACC_COLS = {"t": (0, D_MODEL), "f": (D_MODEL, D_FF), "d": (D_MODEL + D_FF, D_MODEL), "s": (2 * D_MODEL + D_FF, 128),
             "r": (2 * D_MODEL + D_FF + 128, D_RNN)}
_ACC_WIDTH = 2 * D_MODEL + D_FF + 128 + D_RNN


def _small_update(rows_all, gates_all, params):
    flat = [arr for triple in params for arr in triple]
    n_par = len(_SMALL)

    def body(*refs):
        rows_ref, gates_ref = refs[:2]
        p_refs = refs[2:2 + 3 * n_par]
        loss_ref = refs[2 + 3 * n_par]
        o_refs = refs[3 + 3 * n_par:3 + 7 * n_par]
        rows_s, tmp_r, tmp_f = refs[3 + 7 * n_par:]
        me = _dev_index(*_place())
        rows_sum, gates_sum = rows_ref[0], gates_ref[0]
        for d in range(1, N_DEV):
            rows_sum = rows_sum + rows_ref[d]
            gates_sum = gates_sum + gates_ref[d]
        rows_s[...] = rows_sum
        t0 = _ACC_COLS["t"][0]
        loss_ref[...] = rows_s[_LOSS_ROW:_LOSS_ROW + 1, t0:t0 + 128]
        for i, (name, key, row, rows, how) in enumerate(_SMALL):
            w_ref, m_ref, v_ref = p_refs[3 * i:3 * i + 3]
            g_out, d_out, m_out, v_out = o_refs[4 * i:4 * i + 4]
            if key == "a":
                g = gates_sum[:, :HEAD_DIM]
            elif key == "x":
                g = gates_sum[:, HEAD_DIM:]
            elif how == "cols":
                c0, width = _ACC_COLS[key]
                full = rows_s[:, c0:c0 + width]
                shard = width // N_DEV
                mine = full[:, :shard]
                for d in range(1, N_DEV):
                    mine = jnp.where(me == d, full[:, d * shard:(d + 1) * shard], mine)
                tmp = tmp_r if key == "r" else tmp_f
                tmp[...] = mine
                g = tmp[row:row + rows, :]
            else:
                c0, width = _ACC_COLS[key]
                g = rows_s[row:row + rows, c0:c0 + width][:, :w_ref.shape[1]]
            delta, mn, vn = _adamw(w_ref[...], g, m_ref[...], v_ref[...])
            g_out[...] = g
            d_out[...] = delta
            m_out[...] = mn
            v_out[...] = vn

    outs = [jax.ShapeDtypeStruct((1, 128), F32)]
    for w, _, _ in params:
        outs += [jax.ShapeDtypeStruct(w.shape, F32)] * 4
    scratch = [pltpu.VMEM((8, _ACC_WIDTH), F32), pltpu.VMEM((8, D_RNN // N_DEV), F32), pltpu.VMEM((8, D_FF // N_DEV), F32)]
    res = pl.pallas_call(body, name="small_update", out_shape=outs, scratch_shapes=scratch)(rows_all, gates_all, *flat)
    return res[0], [res[1 + 4 * i:5 + 4 * i] for i in range(n_par)]


def kernel(x, p, w_in, attn_sinks, rnn_conv_w, rnn_conv_b, gate_a_w, gate_a_b, gate_x_w, gate_x_b, lru_lambda, w_out, ln1_g, ln1_b, w_ffn_up, ffn_conv_w, ffn_conv_b, w_ffn_down, ple_gate_w, ple_gate_b, ple_proj, ln2_g, ln2_b, loss_target, m_w_in, m_attn_sinks, m_rnn_conv_w, m_rnn_conv_b, m_gate_a_w, m_gate_a_b, m_gate_x_w, m_gate_x_b, m_lru_lambda, m_w_out, m_ln1_g, m_ln1_b, m_w_ffn_up, m_ffn_conv_w, m_ffn_conv_b, m_w_ffn_down, m_ple_gate_w, m_ple_gate_b, m_ple_proj, m_ln2_g, m_ln2_b, v_w_in, v_attn_sinks, v_rnn_conv_w, v_rnn_conv_b, v_gate_a_w, v_gate_a_b, v_gate_x_w, v_gate_x_b, v_lru_lambda, v_w_out, v_ln1_g, v_ln1_b, v_w_ffn_up, v_ffn_conv_w, v_ffn_conv_b, v_w_ffn_down, v_ple_gate_w, v_ple_gate_b, v_ple_proj, v_ln2_g, v_ln2_b):
    from_col_blocks = lambda g: g.transpose(1, 0, 2).reshape(g.shape[1], N_DEV * g.shape[2])

    xs, ps, tgt, sinks = x[0], p[0, 0], loss_target[0], attn_sinks[0]
    wa, wx = _block_diag(gate_a_w[0]), _block_diag(gate_x_w[0])

    conv_cols = jnp.concatenate([rnn_conv_w[0].reshape(1, -1), ffn_conv_w[0].reshape(1, -1)], axis=1)
    n_rc, n_fc = 4 * D_RNN // N_DEV, 3 * D_FF // N_DEV
    ((g_in,),) = _comm_call([_Gather([w_in[0].T.astype(BF16)])], "gather_w_in")
    w_in_full = g_in.reshape(D_IN, D_MODEL)

    (q, k, v, xr, gr), _ = _in_proj(xs, w_in_full)
    w_up_shard = w_ffn_up[0].astype(BF16)
    (att, lse), (g_out, w_up_top, g_conv) = _attn_fwd(
        q, k, v, sinks,
        comm=_Multi([_Gather([w_out[0].astype(BF16), w_up_shard[:D_MODEL // 2]]),
                     _Bcast([jnp.broadcast_to(conv_cols, (8, n_rc + n_fc))])]))
    rcw = from_col_blocks(g_conv[:, 0, :n_rc].reshape(N_DEV, 4, D_RNN // N_DEV))
    fcw = from_col_blocks(g_conv[:, 0, n_rc:].reshape(N_DEV, 3, D_FF // N_DEV))
    (rec, h, *kept), (w_up_bot,) = _rnn_fwd(xr, gr, rcw, rnn_conv_b, wa, wx, gate_a_b, gate_x_b, lru_lambda,
                                            comm=_Gather([w_up_shard[D_MODEL // 2:]]))
    w_out_full = g_out.reshape(D_MODEL, D_MODEL)
    (z1, h1b, gate, act, gl, vdgl), (g_down, g_pg, g_pp) = _mix_ln1_up(
        xs, att, rec, w_out_full, ln1_g, ln1_b, w_up_top, w_up_bot, fcw, ffn_conv_b,
        comm=_Gather([w_ffn_down[0].astype(BF16), ple_gate_w[0].astype(BF16), ple_proj[0].astype(BF16)]))
    dz2b, dpreb, dppb, dgc, dval, dh1p, acc_t = _tail(
        act, gl, vdgl, z1, h1b, ps, tgt, g_down.reshape(D_FF, D_MODEL), g_pg.reshape(D_MODEL, D_MODEL), ple_gate_b,
        from_col_blocks(g_pp), ln1_g, ln1_b, ln2_g, ln2_b)

    gd_down = _weight_grad([dz2b], [act], "down_grad", "rows_t", ts=1024)
    gd_pg = _weight_grad([h1b], [dpreb], "pg_grad", "rows", ts=1024)
    gd_pp = _weight_grad([ps], [dppb], "pp_grad", "cols", ts=1024)
    (dgate, dz1, dz1b, datt, drec, acc_f, acc_d), (r_down, r_pg, r_pp) = _up_bwd(
        dgc, gate, dval, dh1p, z1, w_up_top, w_up_bot, fcw, w_out_full, ln1_g, comm=_Exchange([gd_down, gd_pg, gd_pp]))
    gd_up_top, gd_up_bot = _weight_grad([h1b], [dgate, dval], "up_grad", "cols", halves=True)
    gd_out = _weight_grad([att, rec], [dz1b], "out_grad", "rows", ts=1024)
    (dq, dkc, dkp, dvc, dvp, acc_s), (r_up_top,) = _attn_bwd(q, k, v, lse, datt, sinks, comm=_Exchange([gd_up_top]))
    early = jnp.concatenate([acc_t, acc_f, acc_d], axis=1)
    (dxr, dgr, g_wa, g_wx, acc_r), (r_up_bot, r_out, early_all) = _rnn_bwd(
        xr, gr, h, kept, drec, rcw, wa, wx, lru_lambda, comm=_Multi([_Exchange([gd_up_bot, gd_out]), _Bcast([early])]))
    (dkv, dx), _ = _in_bwd(dq, dkc, dkp, dvc, dvp, dxr, dgr, dz1, w_in_full)
    du_parts = [dq, dkv, dxr, dgr]
    lanes = D_RNN // 128
    late = jnp.concatenate([g_wa, g_wx], axis=1)
    late = jnp.concatenate([late, acc_s, acc_r.reshape(8, lanes, 128).transpose(1, 0, 2).reshape(8 * lanes, 128)], axis=0)
    width = D_MODEL // IN_GRAD_PARTS
    comm, r_parts = _Gather([late]), []
    for part in range(IN_GRAD_PARTS):
        gd_part, got = _weight_grad(du_parts, [xs], f"in_grad_{part}", "rows", ts=1024, b_window=(part, width), comm=comm)
        if part == 0:
            (late_all,) = got
        else:
            r_parts += got
        comm = _Exchange([gd_part])
    r_parts += _comm_call([comm], "exchange_w_in")[0]
    r_in = jnp.concatenate(r_parts, axis=2)
    acc_r_all = late_all[:, D_RNN + 8:].reshape(N_DEV, lanes, 8, 128).transpose(0, 2, 1, 3).reshape(N_DEV, 8, D_RNN)
    small_parts = (jnp.concatenate([early_all, late_all[:, D_RNN:D_RNN + 8], acc_r_all], axis=2),
                   late_all[:, :D_RNN])

    outs = {}
    res = _sum_adamw(r_in, w_in[0].T, m_w_in[0].T, v_w_in[0].T, "adamw_w_in")
    outs["w_in"] = [r.T[None] for r in res]
    for name, parts, w, m, v in [("w_out", r_out, w_out, m_w_out, v_w_out),
                                 ("w_ffn_up", [r_up_top, r_up_bot], w_ffn_up, m_w_ffn_up, v_w_ffn_up),
                                 ("w_ffn_down", r_down, w_ffn_down, m_w_ffn_down, v_w_ffn_down),
                                 ("ple_gate_w", r_pg, ple_gate_w, m_ple_gate_w, v_ple_gate_w),
                                 ("ple_proj", r_pp, ple_proj, m_ple_proj, v_ple_proj)]:
        res = _sum_adamw(parts, w[0], m[0], v[0], "adamw_" + name)
        outs[name] = [r[None] for r in res]

    given = dict(attn_sinks=(attn_sinks, m_attn_sinks, v_attn_sinks), rnn_conv_w=(rnn_conv_w, m_rnn_conv_w, v_rnn_conv_w),
                 rnn_conv_b=(rnn_conv_b, m_rnn_conv_b, v_rnn_conv_b), gate_a_w=(gate_a_w, m_gate_a_w, v_gate_a_w),
                 gate_a_b=(gate_a_b, m_gate_a_b, v_gate_a_b), gate_x_w=(gate_x_w, m_gate_x_w, v_gate_x_w),
                 gate_x_b=(gate_x_b, m_gate_x_b, v_gate_x_b), lru_lambda=(lru_lambda, m_lru_lambda, v_lru_lambda),
                 ln1_g=(ln1_g, m_ln1_g, v_ln1_g), ln1_b=(ln1_b, m_ln1_b, v_ln1_b),
                 ffn_conv_w=(ffn_conv_w, m_ffn_conv_w, v_ffn_conv_w), ffn_conv_b=(ffn_conv_b, m_ffn_conv_b, v_ffn_conv_b),
                 ple_gate_b=(ple_gate_b, m_ple_gate_b, v_ple_gate_b), ln2_g=(ln2_g, m_ln2_g, v_ln2_g),
                 ln2_b=(ln2_b, m_ln2_b, v_ln2_b))
    as_2d = lambda a: a.reshape(-1, a.shape[-1])
    loss_row, small_res = _small_update(*small_parts, [tuple(as_2d(a) for a in given[n]) for n, *_ in _SMALL])
    loss = loss_row[0, 0]
    for (n, *_), res in zip(_SMALL, small_res):
        outs[n] = [r.reshape(given[n][0].shape) for r in res]

    order = ["w_in", "attn_sinks", "rnn_conv_w", "rnn_conv_b", "gate_a_w", "gate_a_b", "gate_x_w", "gate_x_b",
             "lru_lambda", "w_out", "ln1_g", "ln1_b", "w_ffn_up", "ffn_conv_w", "ffn_conv_b", "w_ffn_down",
             "ple_gate_w", "ple_gate_b", "ple_proj", "ln2_g", "ln2_b"]
    return (loss, dx[None], *[outs[n][0] for n in order], *[outs[n][1] for n in order],
            *[outs[n][2] for n in order], *[outs[n][3] for n in order])
```

```python
import jax
import jax.numpy as jnp
from jax import lax
from jax.experimental import pallas as pl
from jax.experimental.pallas import tpu as pltpu

F32 = jnp.float32
BF16 = jnp.bfloat16

D_MODEL = 1024
D_ATT = 512
D_KV = 128
HEAD_DIM = 64
N_HEADS = 8
N_KV = 2
D_RNN = 512
RNN_BLOCKS = 8
D_IN = 1792
D_FF = 3072
PLE_DIM = 256
QBLK = 128
N_DEV = 8
ALPHA = float(2 ** 0.25)
LN_EPS = 1e-5
LRU_C = 8.0
ADAM_LR, ADAM_B1, ADAM_B2, ADAM_EPS, ADAM_WD, ADAM_STEP = 0.001, 0.9, 0.999, 1e-08, 0.01, 10

V7X_VMEM_LIMIT = 56 * 1024 * 1024
MESH = pl.DeviceIdType.MESH


def _params(*sem, vmem=V7X_VMEM_LIMIT):
    return pltpu.CompilerParams(dimension_semantics=sem or None, vmem_limit_bytes=vmem)


def _resident(shape):
    return pl.BlockSpec(shape, lambda *_: (0,) * len(shape), pipeline_mode=pl.Buffered(1))


def _rows(tb, cols):
    return pl.BlockSpec((tb, cols), lambda i: (i, 0))


def _acc(shape):
    return pl.BlockSpec(shape, lambda *_: (0,) * len(shape))


def _dot(a, b):
    return jnp.dot(a, b, preferred_element_type=F32)


def _dot_nt(a, b):
    return lax.dot_general(a, b, (((1,), (1,)), ((), ())), preferred_element_type=F32)


def _dot_tn(a, b):
    return lax.dot_general(a, b, (((0,), (0,)), ((), ())), preferred_element_type=F32)


def _sigmoid(x):
    return 1.0 / (1.0 + jnp.exp(-x))


_GELU_C = 0.7978845608028654
_GELU_K = 0.044715


def _gelu_and_grad(x):
    u = x * x
    t = jnp.tanh(x * (_GELU_C + (_GELU_C * _GELU_K) * u))
    hp = 0.5 + 0.5 * t
    dg = hp + x * (0.5 - 0.5 * (t * t)) * (_GELU_C + (3.0 * _GELU_C * _GELU_K) * u)
    return x * hp, dg


def _gelu(x):
    return 0.5 * x * (1.0 + jnp.tanh(_GELU_C * (x + _GELU_K * x * x * x)))


def _ln_stats(z):
    mu = jnp.mean(z, axis=-1, keepdims=True)
    zc = z - mu
    var = jnp.mean(zc * zc, axis=-1, keepdims=True)
    rstd = lax.rsqrt(var + LN_EPS)
    return zc * rstd, rstd


def _ln_bwd(dy, xhat, rstd, g):
    dxh = dy * g
    m1 = jnp.mean(dxh, axis=-1, keepdims=True)
    m2 = jnp.mean(dxh * xhat, axis=-1, keepdims=True)
    return rstd * (dxh - m1 - xhat * m2)


def _softplus_neg(lam):
    u = jnp.exp(-jnp.abs(lam))
    w = 1.0 + u
    d = w - 1.0
    log1p_u = jnp.where(d == 0.0, u, jnp.log(w) * (u / jnp.where(d == 0.0, 1.0, d)))
    return jnp.maximum(-lam, 0.0) + log1p_u


def _shift_down(x, halo, s):
    xs = pltpu.roll(x, s, 0)
    hs = pltpu.roll(halo, s, 0)
    row8 = lax.broadcasted_iota(jnp.int32, hs.shape, 0)
    first = jnp.where(row8 < s, hs, xs[:8])
    return jnp.concatenate([first, xs[8:]], axis=0)


def _shift_up(x, halo, s):
    n = x.shape[0]
    xs = pltpu.roll(x, n - s, 0)
    hs = pltpu.roll(halo, 8 - s, 0)
    row8 = lax.broadcasted_iota(jnp.int32, hs.shape, 0)
    last = jnp.where(row8 >= 8 - s, hs, xs[n - 8:])
    return jnp.concatenate([xs[:n - 8], last], axis=0)


def _row_sum(x):
    return jnp.sum(x, axis=0, keepdims=True)


def _put_rows(acc_ref, rows):
    row8 = lax.broadcasted_iota(jnp.int32, acc_ref.shape, 0)
    upd = jnp.zeros(acc_ref.shape, F32)
    for r, vec in enumerate(rows):
        upd = jnp.where(row8 == r, vec, upd)
    acc_ref[...] += upd


def _place():
    return lax.axis_index("x"), lax.axis_index("y"), lax.axis_index("c")


def _dev_index(px, py, pc):
    return 4 * px + 2 * py + pc


_ANY = pl.BlockSpec(memory_space=pl.ANY)


class _Gather:
    def __init__(self, arrays):
        self.arrays = list(arrays)
        self.n = len(self.arrays)

    def out_shape(self):
        return [jax.ShapeDtypeStruct((N_DEV,) + s.shape, s.dtype) for s in self.arrays]

    def scratch(self):
        return [pltpu.SemaphoreType.DMA((self.n, 7)), pltpu.SemaphoreType.DMA((self.n, 7)),
                pltpu.SemaphoreType.DMA((self.n,))]

    def _parts(self, ins, outs, sems):
        send_sems, recv_sems, local_sems = sems
        x, y, c = _place()
        me, sibling = (x, y, c), (x, y, 1 - c)
        chips = [(1 - x, y), (x, 1 - y), (1 - x, 1 - y)]

        def copy(a, k, block, to, src=None):
            rows = outs[a].at[_dev_index(*block)]
            return pltpu.make_async_remote_copy(
                src_ref=rows if src is None else src, dst_ref=rows, send_sem=send_sems.at[a, k],
                recv_sem=recv_sems.at[a, k], device_id=to, device_id_type=MESH)

        rng = range(self.n)
        mine = [pltpu.make_async_copy(ins[a], outs[a].at[_dev_index(*me)], local_sems.at[a]) for a in rng]
        first = [copy(a, 0, me, sibling, src=ins[a]) for a in rng]
        first += [copy(a, 1 + j, me, (*chip, c), src=ins[a]) for j, chip in enumerate(chips) for a in rng]
        landed = [copy(a, 1 + j, (*chip, c), me) for j, chip in enumerate(chips) for a in rng]
        passed = [copy(a, 4 + j, (*chip, c), sibling) for j, chip in enumerate(chips) for a in rng]
        from_sibling = [copy(a, 0, sibling, me) for a in rng]
        from_sibling += [copy(a, 4 + j, (*chip, 1 - c), me) for j, chip in enumerate(chips) for a in rng]
        return mine, first, landed, passed, from_sibling

    def start(self, ins, outs, sems):
        mine, first, _, _, _ = self._parts(ins, outs, sems)
        for cp in mine + first:
            cp.start()

    def forward(self, ins, outs, sems):
        _, _, landed, passed, _ = self._parts(ins, outs, sems)
        for got, fwd in zip(landed, passed):
            got.wait_recv()
            fwd.start()

    def finish(self, ins, outs, sems):
        mine, first, _, passed, from_sibling = self._parts(ins, outs, sems)
        for cp in from_sibling:
            cp.wait_recv()
        for cp in first + passed:
            cp.wait_send()
        for cp in mine:
            cp.wait()

    def before(self, ins, outs, sems, step, nsteps):
        pl.when(step == 0)(lambda: self.start(ins, outs, sems))
        pl.when(step == (7 * nsteps) // 8)(lambda: self.forward(ins, outs, sems))

    def after(self, ins, outs, sems, step, nsteps):
        pl.when(step == nsteps - 1)(lambda: self.finish(ins, outs, sems))


class _Exchange:
    def __init__(self, arrays):
        self.arrays = list(arrays)
        self.n = len(self.arrays)

    def out_shape(self):
        return [jax.ShapeDtypeStruct(b.shape, b.dtype) for b in self.arrays]

    def scratch(self):
        return [pltpu.SemaphoreType.DMA((self.n, 7)), pltpu.SemaphoreType.DMA((self.n, 7)),
                pltpu.SemaphoreType.DMA((self.n,))]

    def _parts(self, ins, outs, sems):
        send_sems, recv_sems, local_sems = sems
        x, y, c = _place()
        me = _dev_index(x, y, c)
        peers = [(x ^ (k >> 2), y ^ ((k >> 1) & 1), c ^ (k & 1)) for k in range(1, N_DEV)]
        rng = range(self.n)
        mine = [pltpu.make_async_copy(ins[a].at[me], outs[a].at[me], local_sems.at[a]) for a in rng]
        sent = [pltpu.make_async_remote_copy(
            src_ref=ins[a].at[_dev_index(*to)], dst_ref=outs[a].at[me], send_sem=send_sems.at[a, k],
            recv_sem=recv_sems.at[a, k], device_id=to, device_id_type=MESH) for k, to in enumerate(peers) for a in rng]
        arrivals = [pltpu.make_async_remote_copy(
            src_ref=ins[a].at[me], dst_ref=outs[a].at[_dev_index(*frm)], send_sem=send_sems.at[a, k],
            recv_sem=recv_sems.at[a, k], device_id=frm, device_id_type=MESH) for k, frm in enumerate(peers) for a in rng]
        return mine, sent, arrivals

    def start(self, ins, outs, sems):
        mine, sent, _ = self._parts(ins, outs, sems)
        for cp in mine + sent:
            cp.start()

    def finish(self, ins, outs, sems):
        mine, sent, arrivals = self._parts(ins, outs, sems)
        for cp in arrivals:
            cp.wait_recv()
        for cp in sent:
            cp.wait_send()
        for cp in mine:
            cp.wait()

    def before(self, ins, outs, sems, step, nsteps):
        pl.when(step == 0)(lambda: self.start(ins, outs, sems))

    def after(self, ins, outs, sems, step, nsteps):
        pl.when(step == nsteps - 1)(lambda: self.finish(ins, outs, sems))


class _Bcast(_Exchange):
    def out_shape(self):
        return [jax.ShapeDtypeStruct((N_DEV,) + s.shape, s.dtype) for s in self.arrays]

    def _parts(self, ins, outs, sems):
        send_sems, recv_sems, local_sems = sems
        x, y, c = _place()
        me = _dev_index(x, y, c)
        peers = [(x ^ (k >> 2), y ^ ((k >> 1) & 1), c ^ (k & 1)) for k in range(1, N_DEV)]
        rng = range(self.n)
        mine = [pltpu.make_async_copy(ins[a], outs[a].at[me], local_sems.at[a]) for a in rng]
        sent = [pltpu.make_async_remote_copy(
            src_ref=ins[a], dst_ref=outs[a].at[me], send_sem=send_sems.at[a, k], recv_sem=recv_sems.at[a, k],
            device_id=to, device_id_type=MESH) for k, to in enumerate(peers) for a in rng]
        arrivals = [pltpu.make_async_remote_copy(
            src_ref=ins[a], dst_ref=outs[a].at[_dev_index(*frm)], send_sem=send_sems.at[a, k],
            recv_sem=recv_sems.at[a, k], device_id=frm, device_id_type=MESH) for k, frm in enumerate(peers) for a in rng]
        return mine, sent, arrivals


class _Multi:
    def __init__(self, comms):
        self.comms = list(comms)
        self.arrays = [arr for c in self.comms for arr in c.arrays]
        self.n = len(self.arrays)

    def out_shape(self):
        return [s for c in self.comms for s in c.out_shape()]

    def scratch(self):
        return [s for c in self.comms for s in c.scratch()]

    def _each(self, ins, outs, sems):
        a = 0
        for j, c in enumerate(self.comms):
            yield c, ins[a:a + c.n], outs[a:a + c.n], sems[3 * j:3 * j + 3]
            a += c.n

    def before(self, ins, outs, sems, step, nsteps):
        for c, ci, co, cs in self._each(ins, outs, sems):
            c.before(ci, co, cs, step, nsteps)

    def after(self, ins, outs, sems, step, nsteps):
        for c, ci, co, cs in self._each(ins, outs, sems):
            c.after(ci, co, cs, step, nsteps)


def _comm_call(comms, name):
    ns = [c.n for c in comms]
    n = sum(ns)

    def body(*refs):
        parts, a, s = [], 0, 2 * n
        for c in comms:
            parts.append((c, refs[a:a + c.n], refs[n + a:n + a + c.n], refs[s:s + 3]))
            a, s = a + c.n, s + 3
        for c, ins, outs, sems in parts:
            c.start(ins, outs, sems)
        for c, ins, outs, sems in parts:
            if isinstance(c, _Gather):
                c.forward(ins, outs, sems)
        for c, ins, outs, sems in parts:
            c.finish(ins, outs, sems)

    res = pl.pallas_call(
        body, name=name, in_specs=[_ANY] * n, out_specs=[_ANY] * n,
        out_shape=[s for c in comms for s in c.out_shape()], scratch_shapes=[s for c in comms for s in c.scratch()],
    )(*[arr for c in comms for arr in c.arrays])
    out, a = [], 0
    for k in ns:
        out.append(res[a:a + k])
        a += k
    return out


def _pcall(body, args, *, name, grid, in_specs, out_specs, out_shape, scratch_shapes=(), sem="parallel", comm=None,
           step_axis=0):
    sem = (sem,) * len(grid) if isinstance(sem, str) else sem
    if comm is None:
        res = pl.pallas_call(body, name=name, grid=grid, in_specs=in_specs, out_specs=out_specs, out_shape=out_shape,
                             scratch_shapes=list(scratch_shapes), compiler_params=_params(*sem))(*args)
        return res, []
    n_in, n_out, n_scr, n = len(in_specs), len(out_specs), len(scratch_shapes), comm.n
    nsteps = grid[step_axis]
    assert all(g == 1 for ax, g in enumerate(grid) if ax != step_axis)

    def hosted(*refs):
        ins, cin = refs[:n_in], refs[n_in:n_in + n]
        o0 = n_in + n
        outs, cout = refs[o0:o0 + n_out], refs[o0 + n_out:o0 + n_out + n]
        s0 = o0 + n_out + n
        scr, sems = refs[s0:s0 + n_scr], refs[s0 + n_scr:]
        step = pl.program_id(step_axis)
        comm.before(cin, cout, sems, step, nsteps)
        body(*ins, *outs, *scr)
        comm.after(cin, cout, sems, step, nsteps)

    res = pl.pallas_call(
        hosted, name=name, grid=grid, in_specs=list(in_specs) + [_ANY] * n, out_specs=list(out_specs) + [_ANY] * n,
        out_shape=list(out_shape) + comm.out_shape(), scratch_shapes=list(scratch_shapes) + comm.scratch(),
        compiler_params=_params(*(("arbitrary",) * len(grid))))(*args, *comm.arrays)
    return res[:n_out], res[n_out:]


def _load_row_halves(top_hbm, bot_hbm, full_s, sems):
    r = top_hbm.shape[1]
    copies = [pltpu.make_async_copy(top_hbm, full_s.at[:, :r, :], sems.at[0]),
              pltpu.make_async_copy(bot_hbm, full_s.at[:, r:, :], sems.at[1])]
    for cp in copies:
        cp.start()
    for cp in copies:
        cp.wait()


def _in_proj(x, w_in_t, comm=None):
    S = x.shape[0]
    tb = min(1024, S)

    def body(x_ref, w_ref, q_ref, k_ref, v_ref, xr_ref, gr_ref):
        u = _dot_nt(x_ref[...].astype(BF16), w_ref[...])
        q_ref[...] = (u[:, :D_ATT] * (HEAD_DIM ** -0.5)).astype(BF16)
        k_ref[...] = u[:, D_ATT:D_ATT + D_KV].astype(BF16)
        v_ref[...] = u[:, D_ATT + D_KV:D_ATT + 2 * D_KV].astype(BF16)
        xr_ref[...] = u[:, D_ATT + 2 * D_KV:D_ATT + 2 * D_KV + D_RNN]
        gr_ref[...] = u[:, D_ATT + 2 * D_KV + D_RNN:]

    return _pcall(
        body, (x, w_in_t), name="in_proj", grid=(S // tb,), comm=comm,
        in_specs=[_rows(tb, D_MODEL), _resident((D_IN, D_MODEL))],
        out_specs=[_rows(tb, D_ATT), _rows(tb, D_KV), _rows(tb, D_KV), _rows(tb, D_RNN), _rows(tb, D_RNN)],
        out_shape=[jax.ShapeDtypeStruct((S, D_ATT), BF16), jax.ShapeDtypeStruct((S, D_KV), BF16),
                   jax.ShapeDtypeStruct((S, D_KV), BF16), jax.ShapeDtypeStruct((S, D_RNN), F32),
                   jax.ShapeDtypeStruct((S, D_RNN), F32)])


GROUP = N_HEADS // N_KV


def _band_mask(i):
    qi = lax.broadcasted_iota(jnp.int32, (GROUP * QBLK, 2 * QBLK), 0) & (QBLK - 1)
    sj = lax.broadcasted_iota(jnp.int32, (GROUP * QBLK, 2 * QBLK), 1)
    return (sj > qi) & (sj <= qi + QBLK) & ((sj >= QBLK) | (i > 0))


def _stack_heads(x, g):
    return jnp.concatenate([x[:, (g * GROUP + hh) * HEAD_DIM:(g * GROUP + hh + 1) * HEAD_DIM] for hh in range(GROUP)],
                           axis=0)


def _unstack_heads(x4):
    return [x4[hh * QBLK:(hh + 1) * QBLK] for hh in range(GROUP)]


def _sink_column(sink_ref, g):
    head = lax.broadcasted_iota(jnp.int32, (GROUP * QBLK, 1), 0) // QBLK
    col = jnp.full((GROUP * QBLK, 1), sink_ref[g * GROUP], F32)
    for hh in range(1, GROUP):
        col = jnp.where(head == hh, sink_ref[g * GROUP + hh], col)
    return col


ATT_STEP = 4
IN_GRAD_PARTS = 2


def _attn_specs(nq=1):
    cur = lambda i: (i, 0)
    prev = lambda i: (jnp.maximum(nq * i - 1, 0), 0)
    return [pl.BlockSpec((nq * QBLK, D_KV), cur), pl.BlockSpec((QBLK, D_KV), prev),
            pl.BlockSpec((nq * QBLK, D_KV), cur), pl.BlockSpec((QBLK, D_KV), prev)]


def _attn_fwd(q, k, v, sinks, comm=None):
    S = q.shape[0]
    nq = min(ATT_STEP, S // QBLK)

    def body(sink_ref, q_ref, kc_ref, kp_ref, vc_ref, vp_ref, o_ref, lse_ref):
        first = pl.program_id(0) * nq
        kall = jnp.concatenate([kp_ref[...], kc_ref[...]], axis=0)
        vall = jnp.concatenate([vp_ref[...], vc_ref[...]], axis=0)
        for b in range(nq):
            valid = _band_mask(first + b)
            rows = slice(b * QBLK, (b + 1) * QBLK)
            keys = slice(b * QBLK, (b + 2) * QBLK)
            qv = q_ref[rows, :]
            outs = []
            for g in range(N_KV):
                kcat = kall[keys, g * HEAD_DIM:(g + 1) * HEAD_DIM]
                vcat = vall[keys, g * HEAD_DIM:(g + 1) * HEAD_DIM]
                s = jnp.where(valid, _dot_nt(_stack_heads(qv, g), kcat), -1e30)
                sink = _sink_column(sink_ref, g)
                m = jnp.maximum(jnp.max(s, axis=1, keepdims=True), sink)
                p = jnp.exp(s - m)
                l = jnp.sum(p, axis=1, keepdims=True) + jnp.exp(sink - m)
                outs += _unstack_heads(_dot(p.astype(BF16), vcat) / l)
                lse_ref[(b * N_KV + g) * GROUP * QBLK:(b * N_KV + g + 1) * GROUP * QBLK, :] = m + jnp.log(l)
            o_ref[rows, :] = jnp.concatenate(outs, axis=1).astype(BF16)

    lse_rows = nq * N_HEADS * QBLK
    return _pcall(
        body, (sinks, q, k, k, v, v), name="attn_fwd", grid=(S // (nq * QBLK),), comm=comm,
        in_specs=[pl.BlockSpec(memory_space=pltpu.SMEM), _rows(nq * QBLK, D_ATT)] + _attn_specs(nq),
        out_specs=[_rows(nq * QBLK, D_ATT), _rows(lse_rows, 1)],
        out_shape=[jax.ShapeDtypeStruct((S, D_ATT), BF16), jax.ShapeDtypeStruct((S * N_HEADS, 1), F32)])


def _w_rows(w_ref):
    return [w_ref[k:k + 1, :] for k in range(w_ref.shape[0])]


def _conv4(x, halo, w, b):
    y = b + w[3] * x
    for s in (1, 2, 3):
        y = y + w[3 - s] * _shift_down(x, halo, s)
    return y


def _rnn_gates(xc, wa, wx, ba, bx, sp):
    xcb = xc.astype(BF16)
    r = _sigmoid(_dot(xcb, wa) + ba)
    ig = _sigmoid(_dot(xcb, wx) + bx)
    la = -LRU_C * r * sp
    a = jnp.exp(la)
    t = jnp.tanh(la)
    f = jnp.sqrt(-2.0 * t / (1.0 - t))
    return r, ig, a, f


def _rnn_fwd(xr, gr, conv_w, conv_b, wa, wx, ba, bx, lam, comm=None):
    S = xr.shape[0]
    tb = min(512, S)

    def body(xr_ref, gr_ref, cw_ref, cb_ref, wa_ref, wx_ref, ba_ref, bx_ref, lam_ref, rec_ref, h_ref,
             xc_ref, r_ref, ig_ref, a_ref, f_ref, halo_s, hc_s, a_s, b_s):
        @pl.when(pl.program_id(0) == 0)
        def _():
            halo_s[...] = jnp.zeros_like(halo_s)
            hc_s[...] = jnp.zeros_like(hc_s)

        x = xr_ref[...]
        xc = _conv4(x, halo_s[...], _w_rows(cw_ref), cb_ref[...])
        halo_s[...] = x[tb - 8:]
        r, ig, a, f = _rnn_gates(xc, wa_ref[...], wx_ref[...], ba_ref[...], bx_ref[...], _softplus_neg(lam_ref[...]))
        xc_ref[...] = xc
        r_ref[...] = r
        ig_ref[...] = ig
        a_ref[...] = a
        f_ref[...] = f
        a_s[...] = a
        b_s[...] = f * ig * xc
        row8 = lax.broadcasted_iota(jnp.int32, (8, D_RNN), 0)

        def tile(t, hc):
            o = pl.multiple_of(t * 8, 8)
            at = a_s[pl.ds(o, 8), :]
            bt = b_s[pl.ds(o, 8), :]
            for s in (1, 2, 4):
                keep = row8 >= s
                a_sh = jnp.where(keep, pltpu.roll(at, s, 0), 1.0)
                b_sh = jnp.where(keep, pltpu.roll(bt, s, 0), 0.0)
                bt = at * b_sh + bt
                at = at * a_sh
            ht = at * hc + bt
            b_s[pl.ds(o, 8), :] = ht
            return _row_sum(jnp.where(row8 == 7, ht, 0.0))

        hc_s[0:1, :] = lax.fori_loop(0, tb // 8, tile, hc_s[0:1, :], unroll=2)
        h = b_s[...]
        h_ref[...] = h
        rec_ref[...] = (h * _gelu(gr_ref[...])).astype(BF16)

    vec = _resident((1, D_RNN))
    kept = jax.ShapeDtypeStruct((S, D_RNN), F32)
    return _pcall(
        body, (xr, gr, conv_w, conv_b, wa, wx, ba, bx, lam), name="rnn_fwd", grid=(S // tb,), sem="arbitrary", comm=comm,
        in_specs=[_rows(tb, D_RNN), _rows(tb, D_RNN), _resident((4, D_RNN)), vec,
                  _resident((D_RNN, D_RNN)), _resident((D_RNN, D_RNN)), vec, vec, vec],
        out_specs=[_rows(tb, D_RNN)] * 7,
        out_shape=[jax.ShapeDtypeStruct((S, D_RNN), BF16), kept, kept, kept, kept, kept, kept],
        scratch_shapes=[pltpu.VMEM((8, D_RNN), F32), pltpu.VMEM((8, D_RNN), F32),
                        pltpu.VMEM((tb, D_RNN), F32), pltpu.VMEM((tb, D_RNN), F32)])


def _mix_ln1_up(x, att, rec, w_out, ln1_g, ln1_b, w_up_top, w_up_bot, fcw, fcb, comm=None):
    S = x.shape[0]
    tb = min(256, S)
    nblk, kh, wblk = w_up_top.shape
    half = nblk // 2

    def body(x_ref, att_ref, rec_ref, wo_ref, g_ref, b_ref, wt_hbm, wb_hbm, fcw_ref, fcb_ref,
             z1_ref, h1b_ref, gate_ref, act_ref, gl_ref, vdgl_ref, halo_s, wu_s, wu_sems):
        @pl.when(pl.program_id(0) == 0)
        def _():
            halo_s[...] = jnp.zeros_like(halo_s)
            _load_row_halves(wt_hbm, wb_hbm, wu_s, wu_sems)

        z1 = ALPHA * x_ref[...] + _dot(att_ref[...], wo_ref[:D_ATT, :]) + _dot(rec_ref[...], wo_ref[D_ATT:, :])
        z1_ref[...] = z1
        xhat, _ = _ln_stats(z1)
        h1b = (xhat * g_ref[...] + b_ref[...]).astype(BF16)
        h1b_ref[...] = h1b
        for jj in range(half):
            cols = slice(jj * wblk, (jj + 1) * wblk)
            gate = _dot(h1b, wu_s[jj])
            val = _dot(h1b, wu_s[jj + half])
            halo = halo_s[:, cols]
            conv = (fcb_ref[:, cols] + fcw_ref[2:3, cols] * gate + fcw_ref[1:2, cols] * _shift_down(gate, halo, 1)
                    + fcw_ref[0:1, cols] * _shift_down(gate, halo, 2))
            halo_s[:, cols] = gate[tb - 8:]
            gl, dgl = _gelu_and_grad(conv)
            gate_ref[:, cols] = gate.astype(BF16)
            act_ref[:, cols] = (gl * val).astype(BF16)
            gl_ref[:, cols] = gl.astype(BF16)
            vdgl_ref[:, cols] = (val * dgl).astype(BF16)

    vec = _resident((1, D_MODEL))
    wide = jax.ShapeDtypeStruct((S, D_FF), BF16)
    return _pcall(
        body, (x, att, rec, w_out, ln1_g, ln1_b, w_up_top, w_up_bot, fcw, fcb), name="mix_ln1_up", grid=(S // tb,),
        sem="arbitrary", comm=comm,
        in_specs=[_rows(tb, D_MODEL), _rows(tb, D_ATT), _rows(tb, D_RNN), _resident((D_MODEL, D_MODEL)), vec, vec,
                  _ANY, _ANY, _resident((3, D_FF)), _resident((1, D_FF))],
        out_specs=[_rows(tb, D_MODEL), _rows(tb, D_MODEL)] + [_rows(tb, D_FF)] * 4,
        out_shape=[jax.ShapeDtypeStruct((S, D_MODEL), F32), jax.ShapeDtypeStruct((S, D_MODEL), BF16), wide, wide, wide, wide],
        scratch_shapes=[pltpu.VMEM((8, D_FF), F32), pltpu.VMEM((nblk, 2 * kh, wblk), BF16),
                        pltpu.SemaphoreType.DMA((2,))])


def _tail(act, gl, vdgl, z1, h1b, p, tgt, w_down, w_pg, b_pg, w_pp, ln1_g, ln1_b, ln2_g, ln2_b):
    S = z1.shape[0]
    tb = min(256, S)

    def body(act_ref, gl_ref, vdgl_ref, z1_ref, h1b_ref, p_ref, t_ref, wd_ref, wpg_ref, bpg_ref, wpp_ref,
             g1_ref, b1_ref, g2_ref, b2_ref, dz2_ref, dpre_ref, dpp_ref, dgc_ref, dval_ref, dh1_ref, acc_ref):
        i = pl.program_id(0)

        @pl.when(i == 0)
        def _():
            acc_ref[...] = jnp.zeros_like(acc_ref)

        ffn = _dot(act_ref[...], wd_ref[...])
        xhat1, _ = _ln_stats(z1_ref[...])
        h1 = xhat1 * g1_ref[...] + b1_ref[...]
        sg = _sigmoid(_dot(h1b_ref[...], wpg_ref[...]) + bpg_ref[...])
        pp = _dot(p_ref[...].astype(BF16), wpp_ref[...])
        z2 = ALPHA * h1 + ffn + sg * pp
        xhat2, rstd2 = _ln_stats(z2)
        y = xhat2 * g2_ref[...] + b2_ref[...]
        err = y - t_ref[...]
        dy = err * (1.0 / D_MODEL)
        loss = 0.5 * jnp.sum(jnp.sum(err * err, axis=1, keepdims=True), axis=0, keepdims=True) * (1.0 / D_MODEL)
        dz2 = _ln_bwd(dy, xhat2, rstd2, g2_ref[...])
        dz2b = dz2.astype(BF16)
        dz2_ref[...] = dz2b
        dpre = dz2 * pp * sg * (1.0 - sg)
        dpreb = dpre.astype(BF16)
        dpre_ref[...] = dpreb
        dpp_ref[...] = (dz2 * sg).astype(BF16)
        dh1_ref[...] = ALPHA * dz2 + _dot_nt(dpreb, wpg_ref[...])
        dactb = _dot_nt(dz2b, wd_ref[...]).astype(BF16)
        dval_ref[...] = dactb * gl_ref[...]
        dgc_ref[...] = dactb * vdgl_ref[...]
        _put_rows(acc_ref, [_row_sum(dy * xhat2), _row_sum(dy), _row_sum(dpre),
                            jnp.broadcast_to(loss, (1, D_MODEL))])

    vec = _resident((1, D_MODEL))
    return pl.pallas_call(
        body, name="tail", grid=(S // tb,),
        in_specs=[_rows(tb, D_FF), _rows(tb, D_FF), _rows(tb, D_FF), _rows(tb, D_MODEL), _rows(tb, D_MODEL),
                  _rows(tb, PLE_DIM), _rows(tb, D_MODEL), _resident((D_FF, D_MODEL)), _resident((D_MODEL, D_MODEL)), vec,
                  _resident((PLE_DIM, D_MODEL)), vec, vec, vec, vec],
        out_specs=[_rows(tb, D_MODEL), _rows(tb, D_MODEL), _rows(tb, D_MODEL), _rows(tb, D_FF),
                   _rows(tb, D_FF), _rows(tb, D_MODEL), _acc((8, D_MODEL))],
        out_shape=[jax.ShapeDtypeStruct((S, D_MODEL), BF16),
                   jax.ShapeDtypeStruct((S, D_MODEL), BF16), jax.ShapeDtypeStruct((S, D_MODEL), BF16),
                   jax.ShapeDtypeStruct((S, D_FF), BF16), jax.ShapeDtypeStruct((S, D_FF), BF16),
                   jax.ShapeDtypeStruct((S, D_MODEL), F32), jax.ShapeDtypeStruct((8, D_MODEL), F32)],
        compiler_params=_params("arbitrary"),
    )(act, gl, vdgl, z1, h1b, p, tgt, w_down, w_pg, b_pg, w_pp, ln1_g, ln1_b, ln2_g, ln2_b)


def _weight_grad(a_list, b_list, name, layout, ts=512, comm=None, b_window=None, halves=False):
    S = a_list[0].shape[0]
    ms = [a.shape[1] for a in a_list]
    M, nb = sum(ms), len(b_list)
    win, Nb = b_window if b_window else (0, b_list[0].shape[1])
    ts = min(ts, S)
    nk = S // ts
    per_b = N_DEV // nb
    na = len(a_list)

    n_out = 2 if halves else 1
    assert layout == "cols" or not halves

    def body(*refs):
        a_refs, b_refs, o_refs, acc_ref = refs[:na], refs[na:na + nb], refs[na + nb:na + nb + n_out], refs[-1]
        o_ref = o_refs[0]
        j, k = pl.program_id(0), pl.program_id(1)

        @pl.when(k == 0)
        def _():
            acc_ref[...] = jnp.zeros_like(acc_ref)

        for jj in range(nb):
            @pl.when(j == jj)
            def _():
                b = b_refs[jj][...].astype(BF16)
                off = 0
                for a_ref, m in zip(a_refs, ms):
                    acc_ref[off:off + m, :] += _dot_tn(a_ref[...].astype(BF16), b)
                    off += m

        @pl.when(k == nk - 1)
        def _():
            for d in range(per_b):
                if layout == "rows":
                    o_ref[d] = acc_ref[d * (M // N_DEV):(d + 1) * (M // N_DEV), :].astype(BF16)
                elif layout == "cols" and halves:
                    for o_half, r0 in zip(o_refs, (0, M // 2)):
                        o_half[d] = acc_ref[r0:r0 + M // 2, d * (Nb // per_b):(d + 1) * (Nb // per_b)].astype(BF16)
                elif layout == "cols":
                    o_ref[d] = acc_ref[:, d * (Nb // per_b):(d + 1) * (Nb // per_b)].astype(BF16)
                else:
                    o_ref[d] = acc_ref[:, d * (Nb // per_b):(d + 1) * (Nb // per_b)].T.astype(BF16)

    def b_index(jj):
        return lambda j, k: (jnp.where(j == jj, k, jnp.where(j < jj, 0, nk - 1)), win)

    if layout == "rows":
        assert nb == 1
        blk = (N_DEV, M // N_DEV, Nb)
    elif layout == "cols":
        blk = (per_b, M // n_out, Nb // per_b)
    else:
        blk = (per_b, Nb // per_b, M)
    res, comm_res = _pcall(
        body, (*a_list, *b_list), name=name, grid=(nb, nk), sem="arbitrary", comm=comm, step_axis=1,
        in_specs=[pl.BlockSpec((ts, m), lambda j, k: (k, 0)) for m in ms]
        + [pl.BlockSpec((ts, Nb), b_index(jj)) for jj in range(nb)],
        out_specs=[pl.BlockSpec(blk, lambda j, k: (j, 0, 0))] * n_out,
        out_shape=[jax.ShapeDtypeStruct((N_DEV,) + blk[1:], BF16)] * n_out,
        scratch_shapes=[pltpu.VMEM((M, Nb), F32)])
    res = res if halves else res[0]
    return (res, comm_res) if comm is not None else res


def _pg_pp_grad(h1b, dpreb, p, dppb, ts=1024):
    S = h1b.shape[0]
    ts = min(ts, S)
    nk = S // ts
    rows, cols = D_MODEL // N_DEV, D_MODEL // N_DEV

    def body(h_ref, dpre_ref, p_ref, dpp_ref, gpg_ref, gpp_ref, acc_pg, acc_pp):
        k = pl.program_id(0)

        @pl.when(k == 0)
        def _():
            acc_pg[...] = jnp.zeros_like(acc_pg)
            acc_pp[...] = jnp.zeros_like(acc_pp)

        acc_pg[...] += _dot_tn(h_ref[...], dpre_ref[...])
        acc_pp[...] += _dot_tn(p_ref[...].astype(BF16), dpp_ref[...])

        @pl.when(k == nk - 1)
        def _():
            for d in range(N_DEV):
                gpg_ref[d] = acc_pg[d * rows:(d + 1) * rows, :].astype(BF16)
                gpp_ref[d] = acc_pp[:, d * cols:(d + 1) * cols].astype(BF16)

    return pl.pallas_call(
        body, name="pg_pp_grad", grid=(nk,),
        in_specs=[_rows(ts, D_MODEL), _rows(ts, D_MODEL), _rows(ts, PLE_DIM), _rows(ts, D_MODEL)],
        out_specs=[_acc((N_DEV, rows, D_MODEL)), _acc((N_DEV, PLE_DIM, cols))],
        out_shape=[jax.ShapeDtypeStruct((N_DEV, rows, D_MODEL), BF16), jax.ShapeDtypeStruct((N_DEV, PLE_DIM, cols), BF16)],
        scratch_shapes=[pltpu.VMEM((D_MODEL, D_MODEL), F32), pltpu.VMEM((PLE_DIM, D_MODEL), F32)],
        compiler_params=_params("arbitrary"))(h1b, dpreb, p, dppb)


def _up_bwd(dgc, gate, dval, dh1p, z1, w_up_top, w_up_bot, fcw, w_out, ln1_g, comm=None):
    S = z1.shape[0]
    tb = min(256, S)
    t16 = tb // 16
    n16 = S // 16
    nblk, kh, wblk = w_up_top.shape
    half = nblk // 2
    nsteps = S // tb

    def body(dgc_ref, dgn_ref, gc_ref, dval_ref, dh1p_ref, z1_ref, wt_hbm, wb_hbm, fcw_ref, wo_ref, g1_ref,
             dgate_ref, dz1_ref, dz1b_ref, datt_ref, drec_ref, accf_ref, accd_ref, wu_s, wu_sems):
        i = pl.program_id(0)

        @pl.when(i == 0)
        def _():
            accf_ref[...] = jnp.zeros_like(accf_ref)
            accd_ref[...] = jnp.zeros_like(accd_ref)
            _load_row_halves(wt_hbm, wb_hbm, wu_s, wu_sems)

        dg = dgc_ref[...].astype(F32)
        nxt = jnp.where(i < nsteps - 1, dgn_ref[...].astype(F32)[0:8], 0.0)
        w = _w_rows(fcw_ref)
        up1, up2 = _shift_up(dg, nxt, 1), _shift_up(dg, nxt, 2)
        dgate = (w[2] * dg + w[1] * up1 + w[0] * up2).astype(BF16)
        dgate_ref[...] = dgate
        gate = gc_ref[...].astype(F32)
        _put_rows(accf_ref, [_row_sum(up2 * gate), _row_sum(up1 * gate), _row_sum(dg * gate), _row_sum(dg)])

        dh1 = dh1p_ref[...]
        for j in range(nblk):
            src = dgate if j < half else dval_ref[...]
            jj = j % half
            dh1 = dh1 + _dot_nt(src[:, jj * wblk:(jj + 1) * wblk], wu_s[j])
        xhat1, rstd1 = _ln_stats(z1_ref[...])
        dz1 = _ln_bwd(dh1, xhat1, rstd1, g1_ref[...])
        dz1_ref[...] = dz1
        dz1b = dz1.astype(BF16)
        dz1b_ref[...] = dz1b
        dcat = _dot_nt(dz1b, wo_ref[...])
        datt_ref[...] = dcat[:, :D_ATT].astype(BF16)
        drec_ref[...] = dcat[:, D_ATT:]
        _put_rows(accd_ref, [_row_sum(dh1 * xhat1), _row_sum(dh1)])

    next16 = pl.BlockSpec((16, D_FF), lambda i: (jnp.minimum((i + 1) * t16, n16 - 1), 0))
    return _pcall(
        body, (dgc, dgc, gate, dval, dh1p, z1, w_up_top, w_up_bot, fcw, w_out, ln1_g), name="up_bwd",
        grid=(nsteps,), sem="arbitrary", comm=comm,
        in_specs=[_rows(tb, D_FF), next16, _rows(tb, D_FF), _rows(tb, D_FF), _rows(tb, D_MODEL),
                  _rows(tb, D_MODEL), _ANY, _ANY, _resident((3, D_FF)),
                  _resident((D_MODEL, D_MODEL)), _resident((1, D_MODEL))],
        scratch_shapes=[pltpu.VMEM((nblk, 2 * kh, wblk), BF16), pltpu.SemaphoreType.DMA((2,))],
        out_specs=[_rows(tb, D_FF), _rows(tb, D_MODEL), _rows(tb, D_MODEL), _rows(tb, D_ATT), _rows(tb, D_RNN),
                   _acc((8, D_FF)), _acc((8, D_MODEL))],
        out_shape=[jax.ShapeDtypeStruct((S, D_FF), BF16), jax.ShapeDtypeStruct((S, D_MODEL), F32),
                   jax.ShapeDtypeStruct((S, D_MODEL), BF16), jax.ShapeDtypeStruct((S, D_ATT), BF16),
                   jax.ShapeDtypeStruct((S, D_RNN), F32), jax.ShapeDtypeStruct((8, D_FF), F32),
                   jax.ShapeDtypeStruct((8, D_MODEL), F32)])


def _attn_bwd(q, k, v, lse, do, sinks, comm=None):
    S = q.shape[0]
    grp = N_HEADS // N_KV
    nq = min(ATT_STEP, S // QBLK)

    def body(sink_ref, q_ref, kc_ref, kp_ref, vc_ref, vp_ref, do_ref, lse_ref, dq_ref, dkc_ref, dkp_ref, dvc_ref, dvp_ref,
             ds_ref):
        i = pl.program_id(0)

        @pl.when(i == 0)
        def _():
            ds_ref[...] = jnp.zeros_like(ds_ref)

        row8 = lax.broadcasted_iota(jnp.int32, (8, 128), 0)
        lane8 = lax.broadcasted_iota(jnp.int32, (8, 128), 1)
        dsink = jnp.zeros((8, 128), F32)
        kall = jnp.concatenate([kp_ref[...], kc_ref[...]], axis=0)
        vall = jnp.concatenate([vp_ref[...], vc_ref[...]], axis=0)
        dk_t = [jnp.zeros((D_KV, QBLK), F32) for _ in range(nq + 1)]
        dv_t = [jnp.zeros((D_KV, QBLK), F32) for _ in range(nq + 1)]
        for b in range(nq):
            valid = _band_mask(i * nq + b)
            rows = slice(b * QBLK, (b + 1) * QBLK)
            keys = slice(b * QBLK, (b + 2) * QBLK)
            qv, dov = q_ref[rows, :], do_ref[rows, :]
            dqs, dks, dvs = [], [], []
            for g in range(N_KV):
                kcat = kall[keys, g * HEAD_DIM:(g + 1) * HEAD_DIM]
                vcat = vall[keys, g * HEAD_DIM:(g + 1) * HEAD_DIM]
                q4, do4 = _stack_heads(qv, g), _stack_heads(dov, g)
                s = jnp.where(valid, _dot_nt(q4, kcat), -1e30)
                lse = lse_ref[(b * N_KV + g) * GROUP * QBLK:(b * N_KV + g + 1) * GROUP * QBLK, :]
                p = jnp.exp(s - lse)
                p_sink = jnp.exp(_sink_column(sink_ref, g) - lse)
                dp = _dot_nt(do4, vcat)
                delta = jnp.sum(p * dp, axis=1, keepdims=True)
                dsc = (p * (dp - delta)).astype(BF16)
                dqs += _unstack_heads(_dot(dsc, kcat) * (HEAD_DIM ** -0.5))
                dks.append(_dot_tn(q4, dsc))
                dvs.append(_dot_tn(do4, p.astype(BF16)))
                for hh, part in enumerate(_unstack_heads(-p_sink * delta)):
                    here = (row8 == 0) & (lane8 == g * grp + hh)
                    dsink = dsink + jnp.where(here, jnp.sum(part, axis=0, keepdims=True), 0.0)
            dq_ref[rows, :] = jnp.concatenate(dqs, axis=1).astype(BF16)
            dk2, dv2 = jnp.concatenate(dks, axis=0), jnp.concatenate(dvs, axis=0)
            dk_t[b], dk_t[b + 1] = dk_t[b] + dk2[:, :QBLK], dk_t[b + 1] + dk2[:, QBLK:]
            dv_t[b], dv_t[b + 1] = dv_t[b] + dv2[:, :QBLK], dv_t[b + 1] + dv2[:, QBLK:]
        dkp_ref[...] = dk_t[0].T
        dvp_ref[...] = dv_t[0].T
        for b in range(nq):
            dkc_ref[b * QBLK:(b + 1) * QBLK, :] = dk_t[b + 1].T
            dvc_ref[b * QBLK:(b + 1) * QBLK, :] = dv_t[b + 1].T
        ds_ref[...] += dsink

    nsteps = S // (nq * QBLK)
    cur = jax.ShapeDtypeStruct((S, D_KV), F32)
    prev = jax.ShapeDtypeStruct((nsteps * QBLK, D_KV), F32)
    big = _rows(nq * QBLK, D_ATT)
    return _pcall(
        body, (sinks, q, k, k, v, v, do, lse), name="attn_bwd", grid=(nsteps,), sem="arbitrary", comm=comm,
        in_specs=[pl.BlockSpec(memory_space=pltpu.SMEM), big] + _attn_specs(nq) + [big, _rows(nq * N_HEADS * QBLK, 1)],
        out_specs=[big, _rows(nq * QBLK, D_KV), _rows(QBLK, D_KV), _rows(nq * QBLK, D_KV), _rows(QBLK, D_KV),
                   _acc((8, 128))],
        out_shape=[jax.ShapeDtypeStruct((S, D_ATT), BF16), cur, prev, cur, prev, jax.ShapeDtypeStruct((8, 128), F32)])


def _rnn_bwd(xr, gr, h, kept, drec, conv_w, wa, wx, lam, comm=None):
    S = xr.shape[0]
    tb = min(512, S)
    t8 = tb // 8
    nsteps = S // tb

    def body(xr_ref, xp_ref, gr_ref, h_ref, hp_ref, xc_ref, r_ref, ig_ref, a_ref, f_ref, drec_ref, cw_ref, wa_ref, wx_ref,
             lam_ref, dxr_ref, dgr_ref, gwa_ref, gwx_ref, acc_ref, carry_s, dxc_halo_s, d_s, gwa_s, gwx_s):
        i = pl.program_id(0)
        blk = nsteps - 1 - i

        @pl.when(i == 0)
        def _():
            gwa_s[...] = jnp.zeros_like(gwa_s)
            gwx_s[...] = jnp.zeros_like(gwx_s)
            acc_ref[...] = jnp.zeros_like(acc_ref)
            carry_s[...] = jnp.zeros_like(carry_s)
            dxc_halo_s[...] = jnp.zeros_like(dxc_halo_s)

        x = xr_ref[...]
        xhalo = jnp.where(blk > 0, xp_ref[...], 0.0)
        cw = _w_rows(cw_ref)
        xs = [_shift_down(x, xhalo, 3), _shift_down(x, xhalo, 2), _shift_down(x, xhalo, 1), x]
        xc, r, ig, a, f = xc_ref[...], r_ref[...], ig_ref[...], a_ref[...], f_ref[...]
        sp = _softplus_neg(lam_ref[...])
        hcur = h_ref[...]
        hprev = _shift_down(hcur, jnp.where(blk > 0, hp_ref[...], 0.0), 1)
        gl, dgl = _gelu_and_grad(gr_ref[...])
        drec = drec_ref[...]
        dgr_ref[...] = (drec * hcur * dgl).astype(BF16)
        d_s[...] = drec * gl
        row8 = lax.broadcasted_iota(jnp.int32, (8, D_RNN), 0)

        def tile(t, c):
            o = pl.multiple_of((t8 - 1 - t) * 8, 8)
            a8 = a_ref[pl.ds(o, 8), :]
            dt = d_s[pl.ds(o, 8), :]
            at = jnp.where(row8 == 7, 1.0, pltpu.roll(a8, 7, 0))
            for s in (1, 2, 4):
                keep = row8 < 8 - s
                a_sh = jnp.where(keep, pltpu.roll(at, 8 - s, 0), 1.0)
                d_sh = jnp.where(keep, pltpu.roll(dt, 8 - s, 0), 0.0)
                dt = at * d_sh + dt
                at = at * a_sh
            lt = at * c + dt
            d_s[pl.ds(o, 8), :] = lt
            return _row_sum(jnp.where(row8 == 0, a8 * lt, 0.0))

        carry_s[0:1, :] = lax.fori_loop(0, t8, tile, carry_s[0:1, :], unroll=2)
        lmb = d_s[...]
        a2 = a * a
        dla = lmb * hprev * a - lmb * ig * xc * (a2 / f)
        di = lmb * f * xc
        dr = dla * (-LRU_C) * sp
        dpa = dr * r * (1.0 - r)
        dpx = di * ig * (1.0 - ig)
        dpab = dpa.astype(BF16)
        dpxb = dpx.astype(BF16)
        xcb = xc.astype(BF16)
        gwa_s[...] += _dot_tn(xcb, dpab)
        gwx_s[...] += _dot_tn(xcb, dpxb)

        @pl.when(i == nsteps - 1)
        def _():
            for dense, out in ((gwa_s[...], gwa_ref), (gwx_s[...], gwx_ref)):
                for b in range(RNN_BLOCKS):
                    rows = slice(b * HEAD_DIM, (b + 1) * HEAD_DIM)
                    out[rows, :] = dense[rows, b * HEAD_DIM:(b + 1) * HEAD_DIM]

        dxc = lmb * f * ig + _dot_nt(dpab, wa_ref[...]) + _dot_nt(dpxb, wx_ref[...])
        nxt = dxc_halo_s[...]
        dxr = cw[3] * dxc
        for s in (1, 2, 3):
            dxr = dxr + cw[3 - s] * _shift_up(dxc, nxt, s)
        dxr_ref[...] = dxr.astype(BF16)
        dxc_halo_s[...] = dxc[:8]
        dlam = _row_sum(dla * (-LRU_C) * r) * (-1.0 / (1.0 + jnp.exp(lam_ref[...])))
        _put_rows(acc_ref, [_row_sum(dxc * xs[0]), _row_sum(dxc * xs[1]), _row_sum(dxc * xs[2]), _row_sum(dxc * xs[3]),
                            _row_sum(dxc), _row_sum(dpa), _row_sum(dpx), dlam])

    rev = lambda i: (nsteps - 1 - i, 0)
    prev8 = lambda i: (jnp.maximum((nsteps - 1 - i) * t8 - 1, 0), 0)
    blkspec = pl.BlockSpec((tb, D_RNN), rev)
    halo8 = pl.BlockSpec((8, D_RNN), prev8)
    vec = _resident((1, D_RNN))
    return _pcall(
        body, (xr, xr, gr, h, h, *kept, drec, conv_w, wa, wx, lam), name="rnn_bwd", grid=(nsteps,),
        sem="arbitrary", comm=comm,
        in_specs=[blkspec, halo8, blkspec, blkspec, halo8] + [blkspec] * 6
        + [_resident((4, D_RNN)), _resident((D_RNN, D_RNN)), _resident((D_RNN, D_RNN)), vec],
        out_specs=[blkspec, blkspec, _acc((D_RNN, HEAD_DIM)), _acc((D_RNN, HEAD_DIM)), _acc((8, D_RNN))],
        out_shape=[jax.ShapeDtypeStruct((S, D_RNN), BF16), jax.ShapeDtypeStruct((S, D_RNN), BF16),
                   jax.ShapeDtypeStruct((D_RNN, HEAD_DIM), F32), jax.ShapeDtypeStruct((D_RNN, HEAD_DIM), F32),
                   jax.ShapeDtypeStruct((8, D_RNN), F32)],
        scratch_shapes=[pltpu.VMEM((8, D_RNN), F32), pltpu.VMEM((8, D_RNN), F32), pltpu.VMEM((tb, D_RNN), F32),
                        pltpu.VMEM((D_RNN, D_RNN), F32), pltpu.VMEM((D_RNN, D_RNN), F32)])


def _in_bwd(dq, dkc, dkp, dvc, dvp, dxr, dgr, dz1, w_in, comm=None):
    S = dz1.shape[0]
    tb = min(ATT_STEP * QBLK, S)
    nsteps = S // tb
    ring = 3

    def body(dq_ref, dkc_ref, dkn_ref, dvc_ref, dvn_ref, dxr_ref, dgr_ref, dz1_hbm, w_ref, dkv_ref, dx_ref,
             ring_s, ring_sems):
        i = pl.program_id(0)
        last = i == nsteps - 1

        def fetch(step):
            slot = step % ring
            rows = pl.ds(pl.multiple_of(step * tb, tb), tb)
            return pltpu.make_async_copy(dz1_hbm.at[rows, :], ring_s.at[slot], ring_sems.at[slot])

        @pl.when(i == 0)
        def _():
            for step in range(min(ring - 1, nsteps)):
                fetch(step).start()

        @pl.when(i + ring - 1 < nsteps)
        def _():
            fetch(i + ring - 1).start()

        def total(cur_ref, next_ref):
            nxt = jnp.where(last, 0.0, next_ref[...])
            tail = cur_ref[tb - QBLK:, :] + nxt
            return jnp.concatenate([cur_ref[:tb - QBLK, :], tail], axis=0) if tb > QBLK else tail

        dkv = jnp.concatenate([total(dkc_ref, dkn_ref), total(dvc_ref, dvn_ref)], axis=1).astype(BF16)
        dkv_ref[...] = dkv
        du = jnp.concatenate([dq_ref[...], dkv, dxr_ref[...], dgr_ref[...]], axis=1)
        prod = _dot(du, w_ref[...])
        fetch(i).wait()
        dx_ref[...] = ALPHA * ring_s[i % ring] + prod

    nextp = pl.BlockSpec((QBLK, D_KV), lambda i: (jnp.minimum(i + 1, nsteps - 1), 0))
    return _pcall(
        body, (dq, dkc, dkp, dvc, dvp, dxr, dgr, dz1, w_in), name="in_bwd", grid=(nsteps,), sem="arbitrary", comm=comm,
        in_specs=[_rows(tb, D_ATT), _rows(tb, D_KV), nextp, _rows(tb, D_KV), nextp,
                  _rows(tb, D_RNN), _rows(tb, D_RNN), _ANY, _resident((D_IN, D_MODEL))],
        out_specs=[_rows(tb, 2 * D_KV), _rows(tb, D_MODEL)],
        out_shape=[jax.ShapeDtypeStruct((S, 2 * D_KV), BF16), jax.ShapeDtypeStruct((S, D_MODEL), F32)],
        scratch_shapes=[pltpu.VMEM((ring, tb, D_MODEL), F32), pltpu.SemaphoreType.DMA((ring,))])


def _block_diag(w):
    eye = jnp.eye(RNN_BLOCKS, dtype=w.dtype)
    return (w[:, :, None, :] * eye[:, None, :, None]).reshape(D_RNN, D_RNN).astype(BF16)


def _adamw(w, g, m, v):
    m = ADAM_B1 * m + (1.0 - ADAM_B1) * g
    v = ADAM_B2 * v + (1.0 - ADAM_B2) * (g * g)
    m_hat = m / (1.0 - ADAM_B1 ** ADAM_STEP)
    v_hat = v / (1.0 - ADAM_B2 ** ADAM_STEP)
    delta = -ADAM_LR * (m_hat / (jnp.sqrt(v_hat) + ADAM_EPS) + ADAM_WD * w)
    return delta, m, v


def _sum_adamw(parts, w, m, v, name):
    parts = parts if isinstance(parts, (list, tuple)) else [parts]
    R, C = w.shape
    rb = R if R <= 256 else (256 if parts[0].shape[1] % 256 == 0 else 128)
    per = parts[0].shape[1] // rb
    assert R % rb == 0 and parts[0].shape[1] % rb == 0
    n = len(parts)

    def body(*refs):
        p_refs = refs[:n]
        w_ref, m_ref, v_ref, g_out, d_out, m_out, v_out = refs[n:]
        which = pl.program_id(0) // per

        def total(p_ref):
            g = p_ref[0].astype(F32)
            for d in range(1, N_DEV):
                g = g + p_ref[d].astype(F32)
            return g

        g = total(p_refs[0])
        for j in range(1, n):
            g = jnp.where(which == j, total(p_refs[j]), g)
        delta, mn, vn = _adamw(w_ref[...], g, m_ref[...], v_ref[...])
        g_out[...] = g
        d_out[...] = delta
        m_out[...] = mn
        v_out[...] = vn

    def part_spec(j):
        return pl.BlockSpec((N_DEV, rb, C), lambda i: (0, jnp.clip(i - j * per, 0, per - 1), 0))

    blk = _rows(rb, C)
    out = jax.ShapeDtypeStruct((R, C), F32)
    return pl.pallas_call(
        body, name=name, grid=(R // rb,),
        in_specs=[part_spec(j) for j in range(n)] + [blk, blk, blk],
        out_specs=[blk, blk, blk, blk], out_shape=[out, out, out, out],
        compiler_params=_params("parallel"),
    )(*parts, w, m, v)


_SMALL = [("attn_sinks", "s", 0, 1, None), ("rnn_conv_w", "r", 0, 4, "cols"), ("rnn_conv_b", "r", 4, 1, None),
          ("gate_a_w", "a", 0, D_RNN, None), ("gate_a_b", "r", 5, 1, None), ("gate_x_w", "x", 0, D_RNN, None),
          ("gate_x_b", "r", 6, 1, None), ("lru_lambda", "r", 7, 1, None), ("ln1_g", "d", 0, 1, None),
          ("ln1_b", "d", 1, 1, None), ("ffn_conv_w", "f", 0, 3, "cols"), ("ffn_conv_b", "f", 3, 1, None),
          ("ple_gate_b", "t", 2, 1, None), ("ln2_g", "t", 0, 1, None), ("ln2_b", "t", 1, 1, None)]
_LOSS_ROW = 3


_ACC_COLS = {"t": (0, D_MODEL), "f": (D_MODEL, D_FF), "d": (D_MODEL + D_FF, D_MODEL), "s": (2 * D_MODEL + D_FF, 128),
             "r": (2 * D_MODEL + D_FF + 128, D_RNN)}
_ACC_WIDTH = 2 * D_MODEL + D_FF + 128 + D_RNN


def _small_update(rows_all, gates_all, params):
    flat = [arr for triple in params for arr in triple]
    n_par = len(_SMALL)

    def body(*refs):
        rows_ref, gates_ref = refs[:2]
        p_refs = refs[2:2 + 3 * n_par]
        loss_ref = refs[2 + 3 * n_par]
        o_refs = refs[3 + 3 * n_par:3 + 7 * n_par]
        rows_s, tmp_r, tmp_f = refs[3 + 7 * n_par:]
        me = _dev_index(*_place())
        rows_sum, gates_sum = rows_ref[0], gates_ref[0]
        for d in range(1, N_DEV):
            rows_sum = rows_sum + rows_ref[d]
            gates_sum = gates_sum + gates_ref[d]
        rows_s[...] = rows_sum
        t0 = _ACC_COLS["t"][0]
        loss_ref[...] = rows_s[_LOSS_ROW:_LOSS_ROW + 1, t0:t0 + 128]
        for i, (name, key, row, rows, how) in enumerate(_SMALL):
            w_ref, m_ref, v_ref = p_refs[3 * i:3 * i + 3]
            g_out, d_out, m_out, v_out = o_refs[4 * i:4 * i + 4]
            if key == "a":
                g = gates_sum[:, :HEAD_DIM]
            elif key == "x":
                g = gates_sum[:, HEAD_DIM:]
            elif how == "cols":
                c0, width = _ACC_COLS[key]
                full = rows_s[:, c0:c0 + width]
                shard = width // N_DEV
                mine = full[:, :shard]
                for d in range(1, N_DEV):
                    mine = jnp.where(me == d, full[:, d * shard:(d + 1) * shard], mine)
                tmp = tmp_r if key == "r" else tmp_f
                tmp[...] = mine
                g = tmp[row:row + rows, :]
            else:
                c0, width = _ACC_COLS[key]
                g = rows_s[row:row + rows, c0:c0 + width][:, :w_ref.shape[1]]
            delta, mn, vn = _adamw(w_ref[...], g, m_ref[...], v_ref[...])
            g_out[...] = g
            d_out[...] = delta
            m_out[...] = mn
            v_out[...] = vn

    outs = [jax.ShapeDtypeStruct((1, 128), F32)]
    for w, _, _ in params:
        outs += [jax.ShapeDtypeStruct(w.shape, F32)] * 4
    scratch = [pltpu.VMEM((8, _ACC_WIDTH), F32), pltpu.VMEM((8, D_RNN // N_DEV), F32), pltpu.VMEM((8, D_FF // N_DEV), F32)]
    res = pl.pallas_call(body, name="small_update", out_shape=outs, scratch_shapes=scratch)(rows_all, gates_all, *flat)
    return res[0], [res[1 + 4 * i:5 + 4 * i] for i in range(n_par)]


def kernel(x, p, w_in, attn_sinks, rnn_conv_w, rnn_conv_b, gate_a_w, gate_a_b, gate_x_w, gate_x_b, lru_lambda, w_out, ln1_g, ln1_b, w_ffn_up, ffn_conv_w, ffn_conv_b, w_ffn_down, ple_gate_w, ple_gate_b, ple_proj, ln2_g, ln2_b, loss_target, m_w_in, m_attn_sinks, m_rnn_conv_w, m_rnn_conv_b, m_gate_a_w, m_gate_a_b, m_gate_x_w, m_gate_x_b, m_lru_lambda, m_w_out, m_ln1_g, m_ln1_b, m_w_ffn_up, m_ffn_conv_w, m_ffn_conv_b, m_w_ffn_down, m_ple_gate_w, m_ple_gate_b, m_ple_proj, m_ln2_g, m_ln2_b, v_w_in, v_attn_sinks, v_rnn_conv_w, v_rnn_conv_b, v_gate_a_w, v_gate_a_b, v_gate_x_w, v_gate_x_b, v_lru_lambda, v_w_out, v_ln1_g, v_ln1_b, v_w_ffn_up, v_ffn_conv_w, v_ffn_conv_b, v_w_ffn_down, v_ple_gate_w, v_ple_gate_b, v_ple_proj, v_ln2_g, v_ln2_b):
    from_col_blocks = lambda g: g.transpose(1, 0, 2).reshape(g.shape[1], N_DEV * g.shape[2])

    xs, ps, tgt, sinks = x[0], p[0, 0], loss_target[0], attn_sinks[0]
    wa, wx = _block_diag(gate_a_w[0]), _block_diag(gate_x_w[0])

    conv_cols = jnp.concatenate([rnn_conv_w[0].reshape(1, -1), ffn_conv_w[0].reshape(1, -1)], axis=1)
    n_rc, n_fc = 4 * D_RNN // N_DEV, 3 * D_FF // N_DEV
    ((g_in,),) = _comm_call([_Gather([w_in[0].T.astype(BF16)])], "gather_w_in")
    w_in_full = g_in.reshape(D_IN, D_MODEL)

    (q, k, v, xr, gr), _ = _in_proj(xs, w_in_full)
    w_up_shard = w_ffn_up[0].astype(BF16)
    (att, lse), (g_out, w_up_top, g_conv) = _attn_fwd(
        q, k, v, sinks,
        comm=_Multi([_Gather([w_out[0].astype(BF16), w_up_shard[:D_MODEL // 2]]),
                     _Bcast([jnp.broadcast_to(conv_cols, (8, n_rc + n_fc))])]))
    rcw = from_col_blocks(g_conv[:, 0, :n_rc].reshape(N_DEV, 4, D_RNN // N_DEV))
    fcw = from_col_blocks(g_conv[:, 0, n_rc:].reshape(N_DEV, 3, D_FF // N_DEV))
    (rec, h, *kept), (w_up_bot,) = _rnn_fwd(xr, gr, rcw, rnn_conv_b, wa, wx, gate_a_b, gate_x_b, lru_lambda,
                                            comm=_Gather([w_up_shard[D_MODEL // 2:]]))
    w_out_full = g_out.reshape(D_MODEL, D_MODEL)
    (z1, h1b, gate, act, gl, vdgl), (g_down, g_pg, g_pp) = _mix_ln1_up(
        xs, att, rec, w_out_full, ln1_g, ln1_b, w_up_top, w_up_bot, fcw, ffn_conv_b,
        comm=_Gather([w_ffn_down[0].astype(BF16), ple_gate_w[0].astype(BF16), ple_proj[0].astype(BF16)]))
    dz2b, dpreb, dppb, dgc, dval, dh1p, acc_t = _tail(
        act, gl, vdgl, z1, h1b, ps, tgt, g_down.reshape(D_FF, D_MODEL), g_pg.reshape(D_MODEL, D_MODEL), ple_gate_b,
        from_col_blocks(g_pp), ln1_g, ln1_b, ln2_g, ln2_b)

    gd_down = _weight_grad([dz2b], [act], "down_grad", "rows_t", ts=1024)
    gd_pg, gd_pp = _pg_pp_grad(h1b, dpreb, ps, dppb)
    (dgate, dz1, dz1b, datt, drec, acc_f, acc_d), (r_down, r_pg, r_pp) = _up_bwd(
        dgc, gate, dval, dh1p, z1, w_up_top, w_up_bot, fcw, w_out_full, ln1_g, comm=_Exchange([gd_down, gd_pg, gd_pp]))
    gd_up_top, gd_up_bot = _weight_grad([h1b], [dgate, dval], "up_grad", "cols", halves=True)
    gd_out = _weight_grad([att, rec], [dz1b], "out_grad", "rows", ts=1024)
    (dq, dkc, dkp, dvc, dvp, acc_s), (r_up_top,) = _attn_bwd(q, k, v, lse, datt, sinks, comm=_Exchange([gd_up_top]))
    early = jnp.concatenate([acc_t, acc_f, acc_d], axis=1)
    (dxr, dgr, g_wa, g_wx, acc_r), (r_up_bot, r_out, early_all) = _rnn_bwd(
        xr, gr, h, kept, drec, rcw, wa, wx, lru_lambda, comm=_Multi([_Exchange([gd_up_bot, gd_out]), _Bcast([early])]))
    (dkv, dx), _ = _in_bwd(dq, dkc, dkp, dvc, dvp, dxr, dgr, dz1, w_in_full)
    du_parts = [dq, dkv, dxr, dgr]
    lanes = D_RNN // 128
    late = jnp.concatenate([g_wa, g_wx], axis=1)
    late = jnp.concatenate([late, acc_s, acc_r.reshape(8, lanes, 128).transpose(1, 0, 2).reshape(8 * lanes, 128)], axis=0)
    width = D_MODEL // IN_GRAD_PARTS
    comm, r_parts = _Gather([late]), []
    for part in range(IN_GRAD_PARTS):
        gd_part, got = _weight_grad(du_parts, [xs], f"in_grad_{part}", "rows", ts=1024, b_window=(part, width), comm=comm)
        if part == 0:
            (late_all,) = got
        else:
            r_parts += got
        comm = _Exchange([gd_part])
    r_parts += _comm_call([comm], "exchange_w_in")[0]
    r_in = jnp.concatenate(r_parts, axis=2)
    acc_r_all = late_all[:, D_RNN + 8:].reshape(N_DEV, lanes, 8, 128).transpose(0, 2, 1, 3).reshape(N_DEV, 8, D_RNN)
    small_parts = (jnp.concatenate([early_all, late_all[:, D_RNN:D_RNN + 8], acc_r_all], axis=2),
                   late_all[:, :D_RNN])

    outs = {}
    res = _sum_adamw(r_in, w_in[0].T, m_w_in[0].T, v_w_in[0].T, "adamw_w_in")
    outs["w_in"] = [r.T[None] for r in res]
    for name, parts, w, m, v in [("w_out", r_out, w_out, m_w_out, v_w_out),
                                 ("w_ffn_up", [r_up_top, r_up_bot], w_ffn_up, m_w_ffn_up, v_w_ffn_up),
                                 ("w_ffn_down", r_down, w_ffn_down, m_w_ffn_down, v_w_ffn_down),
                                 ("ple_gate_w", r_pg, ple_gate_w, m_ple_gate_w, v_ple_gate_w),
                                 ("ple_proj", r_pp, ple_proj, m_ple_proj, v_ple_proj)]:
        res = _sum_adamw(parts, w[0], m[0], v[0], "adamw_" + name)
        outs[name] = [r[None] for r in res]

    given = dict(attn_sinks=(attn_sinks, m_attn_sinks, v_attn_sinks), rnn_conv_w=(rnn_conv_w, m_rnn_conv_w, v_rnn_conv_w),
                 rnn_conv_b=(rnn_conv_b, m_rnn_conv_b, v_rnn_conv_b), gate_a_w=(gate_a_w, m_gate_a_w, v_gate_a_w),
                 gate_a_b=(gate_a_b, m_gate_a_b, v_gate_a_b), gate_x_w=(gate_x_w, m_gate_x_w, v_gate_x_w),
                 gate_x_b=(gate_x_b, m_gate_x_b, v_gate_x_b), lru_lambda=(lru_lambda, m_lru_lambda, v_lru_lambda),
                 ln1_g=(ln1_g, m_ln1_g, v_ln1_g), ln1_b=(ln1_b, m_ln1_b, v_ln1_b),
                 ffn_conv_w=(ffn_conv_w, m_ffn_conv_w, v_ffn_conv_w), ffn_conv_b=(ffn_conv_b, m_ffn_conv_b, v_ffn_conv_b),
                 ple_gate_b=(ple_gate_b, m_ple_gate_b, v_ple_gate_b), ln2_g=(ln2_g, m_ln2_g, v_ln2_g),
                 ln2_b=(ln2_b, m_ln2_b, v_ln2_b))
    as_2d = lambda a: a.reshape(-1, a.shape[-1])
    loss_row, small_res = _small_update(*small_parts, [tuple(as_2d(a) for a in given[n]) for n, *_ in _SMALL])
    loss = loss_row[0, 0]
    for (n, *_), res in zip(_SMALL, small_res):
        outs[n] = [r.reshape(given[n][0].shape) for r in res]

    order = ["w_in", "attn_sinks", "rnn_conv_w", "rnn_conv_b", "gate_a_w", "gate_a_b", "gate_x_w", "gate_x_b",
             "lru_lambda", "w_out", "ln1_g", "ln1_b", "w_ffn_up", "ffn_conv_w", "ffn_conv_b", "w_ffn_down",
             "ple_gate_w", "ple_gate_b", "ple_proj", "ln2_g", "ln2_b"]
    return (loss, dx[None], *[outs[n][0] for n in order], *[outs[n][1] for n in order],
            *[outs[n][2] for n in order], *[outs[n][3] for n in order])
```

```python
import jax
import jax.numpy as jnp
from jax import lax
from jax.experimental import pallas as pl
from jax.experimental.pallas import tpu as pltpu

F32 = jnp.float32
BF16 = jnp.bfloat16

D_MODEL = 1024
D_ATT = 512
D_KV = 128
HEAD_DIM = 64
N_HEADS = 8
N_KV = 2
D_RNN = 512
RNN_BLOCKS = 8
D_IN = 1792
D_FF = 3072
PLE_DIM = 256
QBLK = 128
N_DEV = 8
ALPHA = float(2 ** 0.25)
LN_EPS = 1e-5
LRU_C = 8.0
ADAM_LR, ADAM_B1, ADAM_B2, ADAM_EPS, ADAM_WD, ADAM_STEP = 0.001, 0.9, 0.999, 1e-08, 0.01, 10

V7X_VMEM_LIMIT = 56 * 1024 * 1024
MESH = pl.DeviceIdType.MESH


def _params(*sem, vmem=V7X_VMEM_LIMIT):
    return pltpu.CompilerParams(dimension_semantics=sem or None, vmem_limit_bytes=vmem)


def _resident(shape):
    return pl.BlockSpec(shape, lambda *_: (0,) * len(shape), pipeline_mode=pl.Buffered(1))


def _rows(tb, cols):
    return pl.BlockSpec((tb, cols), lambda i: (i, 0))


def _acc(shape):
    return pl.BlockSpec(shape, lambda *_: (0,) * len(shape))


def _dot(a, b):
    return jnp.dot(a, b, preferred_element_type=F32)


def _dot_nt(a, b):
    return lax.dot_general(a, b, (((1,), (1,)), ((), ())), preferred_element_type=F32)


def _dot_tn(a, b):
    return lax.dot_general(a, b, (((0,), (0,)), ((), ())), preferred_element_type=F32)


def _sigmoid(x):
    return 1.0 / (1.0 + jnp.exp(-x))


_GELU_C = 0.7978845608028654
_GELU_K = 0.044715


def _gelu_and_grad(x):
    u = x * x
    t = jnp.tanh(x * (_GELU_C + (_GELU_C * _GELU_K) * u))
    hp = 0.5 + 0.5 * t
    dg = hp + x * (0.5 - 0.5 * (t * t)) * (_GELU_C + (3.0 * _GELU_C * _GELU_K) * u)
    return x * hp, dg


def _gelu(x):
    return 0.5 * x * (1.0 + jnp.tanh(_GELU_C * (x + _GELU_K * x * x * x)))


def _ln_stats(z):
    mu = jnp.mean(z, axis=-1, keepdims=True)
    zc = z - mu
    var = jnp.mean(zc * zc, axis=-1, keepdims=True)
    rstd = lax.rsqrt(var + LN_EPS)
    return zc * rstd, rstd


def _ln_bwd(dy, xhat, rstd, g):
    dxh = dy * g
    m1 = jnp.mean(dxh, axis=-1, keepdims=True)
    m2 = jnp.mean(dxh * xhat, axis=-1, keepdims=True)
    return rstd * (dxh - m1 - xhat * m2)


def _softplus_neg(lam):
    u = jnp.exp(-jnp.abs(lam))
    w = 1.0 + u
    d = w - 1.0
    log1p_u = jnp.where(d == 0.0, u, jnp.log(w) * (u / jnp.where(d == 0.0, 1.0, d)))
    return jnp.maximum(-lam, 0.0) + log1p_u


def _shift_down(x, halo, s):
    xs = pltpu.roll(x, s, 0)
    hs = pltpu.roll(halo, s, 0)
    row8 = lax.broadcasted_iota(jnp.int32, hs.shape, 0)
    first = jnp.where(row8 < s, hs, xs[:8])
    return jnp.concatenate([first, xs[8:]], axis=0)


def _shift_up(x, halo, s):
    n = x.shape[0]
    xs = pltpu.roll(x, n - s, 0)
    hs = pltpu.roll(halo, 8 - s, 0)
    row8 = lax.broadcasted_iota(jnp.int32, hs.shape, 0)
    last = jnp.where(row8 >= 8 - s, hs, xs[n - 8:])
    return jnp.concatenate([xs[:n - 8], last], axis=0)


def _row_sum(x):
    return jnp.sum(x, axis=0, keepdims=True)


def _put_rows(acc_ref, rows):
    row8 = lax.broadcasted_iota(jnp.int32, acc_ref.shape, 0)
    upd = jnp.zeros(acc_ref.shape, F32)
    for r, vec in enumerate(rows):
        upd = jnp.where(row8 == r, vec, upd)
    acc_ref[...] += upd


def _place():
    return lax.axis_index("x"), lax.axis_index("y"), lax.axis_index("c")


def _dev_index(px, py, pc):
    return 4 * px + 2 * py + pc


_ANY = pl.BlockSpec(memory_space=pl.ANY)


class _Gather:
    def __init__(self, arrays):
        self.arrays = list(arrays)
        self.n = len(self.arrays)

    def out_shape(self):
        return [jax.ShapeDtypeStruct((N_DEV,) + s.shape, s.dtype) for s in self.arrays]

    def scratch(self):
        return [pltpu.SemaphoreType.DMA((self.n, 7)), pltpu.SemaphoreType.DMA((self.n, 7)),
                pltpu.SemaphoreType.DMA((self.n,))]

    def _parts(self, ins, outs, sems):
        send_sems, recv_sems, local_sems = sems
        x, y, c = _place()
        me, sibling = (x, y, c), (x, y, 1 - c)
        chips = [(1 - x, y), (x, 1 - y), (1 - x, 1 - y)]

        def copy(a, k, block, to, src=None):
            rows = outs[a].at[_dev_index(*block)]
            return pltpu.make_async_remote_copy(
                src_ref=rows if src is None else src, dst_ref=rows, send_sem=send_sems.at[a, k],
                recv_sem=recv_sems.at[a, k], device_id=to, device_id_type=MESH)

        rng = range(self.n)
        mine = [pltpu.make_async_copy(ins[a], outs[a].at[_dev_index(*me)], local_sems.at[a]) for a in rng]
        first = [copy(a, 0, me, sibling, src=ins[a]) for a in rng]
        first += [copy(a, 1 + j, me, (*chip, c), src=ins[a]) for j, chip in enumerate(chips) for a in rng]
        landed = [copy(a, 1 + j, (*chip, c), me) for j, chip in enumerate(chips) for a in rng]
        passed = [copy(a, 4 + j, (*chip, c), sibling) for j, chip in enumerate(chips) for a in rng]
        from_sibling = [copy(a, 0, sibling, me) for a in rng]
        from_sibling += [copy(a, 4 + j, (*chip, 1 - c), me) for j, chip in enumerate(chips) for a in rng]
        return mine, first, landed, passed, from_sibling

    def start(self, ins, outs, sems):
        mine, first, _, _, _ = self._parts(ins, outs, sems)
        for cp in mine + first:
            cp.start()

    def forward(self, ins, outs, sems):
        _, _, landed, passed, _ = self._parts(ins, outs, sems)
        for got, fwd in zip(landed, passed):
            got.wait_recv()
            fwd.start()

    def finish(self, ins, outs, sems):
        mine, first, _, passed, from_sibling = self._parts(ins, outs, sems)
        for cp in from_sibling:
            cp.wait_recv()
        for cp in first + passed:
            cp.wait_send()
        for cp in mine:
            cp.wait()

    def before(self, ins, outs, sems, step, nsteps):
        pl.when(step == 0)(lambda: self.start(ins, outs, sems))
        pl.when(step == (7 * nsteps) // 8)(lambda: self.forward(ins, outs, sems))

    def after(self, ins, outs, sems, step, nsteps):
        pl.when(step == nsteps - 1)(lambda: self.finish(ins, outs, sems))


class _Exchange:
    def __init__(self, arrays):
        self.arrays = list(arrays)
        self.n = len(self.arrays)

    def out_shape(self):
        return [jax.ShapeDtypeStruct(b.shape, b.dtype) for b in self.arrays]

    def scratch(self):
        return [pltpu.SemaphoreType.DMA((self.n, 7)), pltpu.SemaphoreType.DMA((self.n, 7)),
                pltpu.SemaphoreType.DMA((self.n,))]

    def _parts(self, ins, outs, sems):
        send_sems, recv_sems, local_sems = sems
        x, y, c = _place()
        me = _dev_index(x, y, c)
        peers = [(x ^ (k >> 2), y ^ ((k >> 1) & 1), c ^ (k & 1)) for k in range(1, N_DEV)]
        rng = range(self.n)
        mine = [pltpu.make_async_copy(ins[a].at[me], outs[a].at[me], local_sems.at[a]) for a in rng]
        sent = [pltpu.make_async_remote_copy(
            src_ref=ins[a].at[_dev_index(*to)], dst_ref=outs[a].at[me], send_sem=send_sems.at[a, k],
            recv_sem=recv_sems.at[a, k], device_id=to, device_id_type=MESH) for k, to in enumerate(peers) for a in rng]
        arrivals = [pltpu.make_async_remote_copy(
            src_ref=ins[a].at[me], dst_ref=outs[a].at[_dev_index(*frm)], send_sem=send_sems.at[a, k],
            recv_sem=recv_sems.at[a, k], device_id=frm, device_id_type=MESH) for k, frm in enumerate(peers) for a in rng]
        return mine, sent, arrivals

    def start(self, ins, outs, sems):
        mine, sent, _ = self._parts(ins, outs, sems)
        for cp in mine + sent:
            cp.start()

    def finish(self, ins, outs, sems):
        mine, sent, arrivals = self._parts(ins, outs, sems)
        for cp in arrivals:
            cp.wait_recv()
        for cp in sent:
            cp.wait_send()
        for cp in mine:
            cp.wait()

    def before(self, ins, outs, sems, step, nsteps):
        pl.when(step == 0)(lambda: self.start(ins, outs, sems))

    def after(self, ins, outs, sems, step, nsteps):
        pl.when(step == nsteps - 1)(lambda: self.finish(ins, outs, sems))


class _Bcast(_Exchange):
    def out_shape(self):
        return [jax.ShapeDtypeStruct((N_DEV,) + s.shape, s.dtype) for s in self.arrays]

    def _parts(self, ins, outs, sems):
        send_sems, recv_sems, local_sems = sems
        x, y, c = _place()
        me = _dev_index(x, y, c)
        peers = [(x ^ (k >> 2), y ^ ((k >> 1) & 1), c ^ (k & 1)) for k in range(1, N_DEV)]
        rng = range(self.n)
        mine = [pltpu.make_async_copy(ins[a], outs[a].at[me], local_sems.at[a]) for a in rng]
        sent = [pltpu.make_async_remote_copy(
            src_ref=ins[a], dst_ref=outs[a].at[me], send_sem=send_sems.at[a, k], recv_sem=recv_sems.at[a, k],
            device_id=to, device_id_type=MESH) for k, to in enumerate(peers) for a in rng]
        arrivals = [pltpu.make_async_remote_copy(
            src_ref=ins[a], dst_ref=outs[a].at[_dev_index(*frm)], send_sem=send_sems.at[a, k],
            recv_sem=recv_sems.at[a, k], device_id=frm, device_id_type=MESH) for k, frm in enumerate(peers) for a in rng]
        return mine, sent, arrivals


class _Multi:
    def __init__(self, comms):
        self.comms = list(comms)
        self.arrays = [arr for c in self.comms for arr in c.arrays]
        self.n = len(self.arrays)

    def out_shape(self):
        return [s for c in self.comms for s in c.out_shape()]

    def scratch(self):
        return [s for c in self.comms for s in c.scratch()]

    def _each(self, ins, outs, sems):
        a = 0
        for j, c in enumerate(self.comms):
            yield c, ins[a:a + c.n], outs[a:a + c.n], sems[3 * j:3 * j + 3]
            a += c.n

    def before(self, ins, outs, sems, step, nsteps):
        for c, ci, co, cs in self._each(ins, outs, sems):
            c.before(ci, co, cs, step, nsteps)

    def after(self, ins, outs, sems, step, nsteps):
        for c, ci, co, cs in self._each(ins, outs, sems):
            c.after(ci, co, cs, step, nsteps)


def _comm_call(comms, name):
    ns = [c.n for c in comms]
    n = sum(ns)

    def body(*refs):
        parts, a, s = [], 0, 2 * n
        for c in comms:
            parts.append((c, refs[a:a + c.n], refs[n + a:n + a + c.n], refs[s:s + 3]))
            a, s = a + c.n, s + 3
        for c, ins, outs, sems in parts:
            c.start(ins, outs, sems)
        for c, ins, outs, sems in parts:
            if isinstance(c, _Gather):
                c.forward(ins, outs, sems)
        for c, ins, outs, sems in parts:
            c.finish(ins, outs, sems)

    res = pl.pallas_call(
        body, name=name, in_specs=[_ANY] * n, out_specs=[_ANY] * n,
        out_shape=[s for c in comms for s in c.out_shape()], scratch_shapes=[s for c in comms for s in c.scratch()],
    )(*[arr for c in comms for arr in c.arrays])
    out, a = [], 0
    for k in ns:
        out.append(res[a:a + k])
        a += k
    return out


def _pcall(body, args, *, name, grid, in_specs, out_specs, out_shape, scratch_shapes=(), sem="parallel", comm=None,
           step_axis=0):
    sem = (sem,) * len(grid) if isinstance(sem, str) else sem
    if comm is None:
        res = pl.pallas_call(body, name=name, grid=grid, in_specs=in_specs, out_specs=out_specs, out_shape=out_shape,
                             scratch_shapes=list(scratch_shapes), compiler_params=_params(*sem))(*args)
        return res, []
    n_in, n_out, n_scr, n = len(in_specs), len(out_specs), len(scratch_shapes), comm.n
    nsteps = grid[step_axis]
    assert all(g == 1 for ax, g in enumerate(grid) if ax != step_axis)

    def hosted(*refs):
        ins, cin = refs[:n_in], refs[n_in:n_in + n]
        o0 = n_in + n
        outs, cout = refs[o0:o0 + n_out], refs[o0 + n_out:o0 + n_out + n]
        s0 = o0 + n_out + n
        scr, sems = refs[s0:s0 + n_scr], refs[s0 + n_scr:]
        step = pl.program_id(step_axis)
        comm.before(cin, cout, sems, step, nsteps)
        body(*ins, *outs, *scr)
        comm.after(cin, cout, sems, step, nsteps)

    res = pl.pallas_call(
        hosted, name=name, grid=grid, in_specs=list(in_specs) + [_ANY] * n, out_specs=list(out_specs) + [_ANY] * n,
        out_shape=list(out_shape) + comm.out_shape(), scratch_shapes=list(scratch_shapes) + comm.scratch(),
        compiler_params=_params(*(("arbitrary",) * len(grid))))(*args, *comm.arrays)
    return res[:n_out], res[n_out:]


def _load_row_halves(top_hbm, bot_hbm, full_s, sems):
    r = top_hbm.shape[1]
    copies = [pltpu.make_async_copy(top_hbm, full_s.at[:, :r, :], sems.at[0]),
              pltpu.make_async_copy(bot_hbm, full_s.at[:, r:, :], sems.at[1])]
    for cp in copies:
        cp.start()
    for cp in copies:
        cp.wait()


def _in_proj(x, w_in_t, comm=None):
    S = x.shape[0]
    tb = min(1024, S)

    def body(x_ref, w_ref, q_ref, k_ref, v_ref, xr_ref, gr_ref):
        u = _dot_nt(x_ref[...].astype(BF16), w_ref[...])
        q_ref[...] = (u[:, :D_ATT] * (HEAD_DIM ** -0.5)).astype(BF16)
        k_ref[...] = u[:, D_ATT:D_ATT + D_KV].astype(BF16)
        v_ref[...] = u[:, D_ATT + D_KV:D_ATT + 2 * D_KV].astype(BF16)
        xr_ref[...] = u[:, D_ATT + 2 * D_KV:D_ATT + 2 * D_KV + D_RNN]
        gr_ref[...] = u[:, D_ATT + 2 * D_KV + D_RNN:]

    return _pcall(
        body, (x, w_in_t), name="in_proj", grid=(S // tb,), comm=comm,
        in_specs=[_rows(tb, D_MODEL), _resident((D_IN, D_MODEL))],
        out_specs=[_rows(tb, D_ATT), _rows(tb, D_KV), _rows(tb, D_KV), _rows(tb, D_RNN), _rows(tb, D_RNN)],
        out_shape=[jax.ShapeDtypeStruct((S, D_ATT), BF16), jax.ShapeDtypeStruct((S, D_KV), BF16),
                   jax.ShapeDtypeStruct((S, D_KV), BF16), jax.ShapeDtypeStruct((S, D_RNN), F32),
                   jax.ShapeDtypeStruct((S, D_RNN), F32)])


GROUP = N_HEADS // N_KV


def _band_mask(i):
    qi = lax.broadcasted_iota(jnp.int32, (GROUP * QBLK, 2 * QBLK), 0) & (QBLK - 1)
    sj = lax.broadcasted_iota(jnp.int32, (GROUP * QBLK, 2 * QBLK), 1)
    return (sj > qi) & (sj <= qi + QBLK) & ((sj >= QBLK) | (i > 0))


def _stack_heads(x, g):
    return jnp.concatenate([x[:, (g * GROUP + hh) * HEAD_DIM:(g * GROUP + hh + 1) * HEAD_DIM] for hh in range(GROUP)],
                           axis=0)


def _unstack_heads(x4):
    return [x4[hh * QBLK:(hh + 1) * QBLK] for hh in range(GROUP)]


def _sink_column(sink_ref, g):
    head = lax.broadcasted_iota(jnp.int32, (GROUP * QBLK, 1), 0) // QBLK
    col = jnp.full((GROUP * QBLK, 1), sink_ref[g * GROUP], F32)
    for hh in range(1, GROUP):
        col = jnp.where(head == hh, sink_ref[g * GROUP + hh], col)
    return col


ATT_STEP = 4
IN_GRAD_PARTS = 2


def _attn_specs(nq=1):
    cur = lambda i: (i, 0)
    prev = lambda i: (jnp.maximum(nq * i - 1, 0), 0)
    return [pl.BlockSpec((nq * QBLK, D_KV), cur), pl.BlockSpec((QBLK, D_KV), prev),
            pl.BlockSpec((nq * QBLK, D_KV), cur), pl.BlockSpec((QBLK, D_KV), prev)]


def _attn_fwd(q, k, v, sinks, comm=None):
    S = q.shape[0]
    nq = min(ATT_STEP, S // QBLK)

    def body(sink_ref, q_ref, kc_ref, kp_ref, vc_ref, vp_ref, o_ref, lse_ref):
        first = pl.program_id(0) * nq
        kall = jnp.concatenate([kp_ref[...], kc_ref[...]], axis=0)
        vall = jnp.concatenate([vp_ref[...], vc_ref[...]], axis=0)
        for b in range(nq):
            valid = _band_mask(first + b)
            rows = slice(b * QBLK, (b + 1) * QBLK)
            keys = slice(b * QBLK, (b + 2) * QBLK)
            qv = q_ref[rows, :]
            outs = []
            for g in range(N_KV):
                kcat = kall[keys, g * HEAD_DIM:(g + 1) * HEAD_DIM]
                vcat = vall[keys, g * HEAD_DIM:(g + 1) * HEAD_DIM]
                s = jnp.where(valid, _dot_nt(_stack_heads(qv, g), kcat), -1e30)
                sink = _sink_column(sink_ref, g)
                m = jnp.maximum(jnp.max(s, axis=1, keepdims=True), sink)
                p = jnp.exp(s - m)
                l = jnp.sum(p, axis=1, keepdims=True) + jnp.exp(sink - m)
                outs += _unstack_heads(_dot(p.astype(BF16), vcat) / l)
                lse_ref[(b * N_KV + g) * GROUP * QBLK:(b * N_KV + g + 1) * GROUP * QBLK, :] = m + jnp.log(l)
            o_ref[rows, :] = jnp.concatenate(outs, axis=1).astype(BF16)

    lse_rows = nq * N_HEADS * QBLK
    return _pcall(
        body, (sinks, q, k, k, v, v), name="attn_fwd", grid=(S // (nq * QBLK),), comm=comm,
        in_specs=[pl.BlockSpec(memory_space=pltpu.SMEM), _rows(nq * QBLK, D_ATT)] + _attn_specs(nq),
        out_specs=[_rows(nq * QBLK, D_ATT), _rows(lse_rows, 1)],
        out_shape=[jax.ShapeDtypeStruct((S, D_ATT), BF16), jax.ShapeDtypeStruct((S * N_HEADS, 1), F32)])


def _w_rows(w_ref):
    return [w_ref[k:k + 1, :] for k in range(w_ref.shape[0])]


def _conv4(x, halo, w, b):
    y = b + w[3] * x
    for s in (1, 2, 3):
        y = y + w[3 - s] * _shift_down(x, halo, s)
    return y


def _rnn_gates(xc, wa, wx, ba, bx, sp):
    xcb = xc.astype(BF16)
    r = _sigmoid(_dot(xcb, wa) + ba)
    ig = _sigmoid(_dot(xcb, wx) + bx)
    la = -LRU_C * r * sp
    a = jnp.exp(la)
    t = jnp.tanh(la)
    f = jnp.sqrt(-2.0 * t / (1.0 - t))
    return r, ig, a, f


def _rnn_fwd(xr, gr, conv_w, conv_b, wa, wx, ba, bx, lam, comm=None):
    S = xr.shape[0]
    tb = min(512, S)

    def body(xr_ref, gr_ref, cw_ref, cb_ref, wa_ref, wx_ref, ba_ref, bx_ref, lam_ref, rec_ref, h_ref,
             xc_ref, r_ref, ig_ref, a_ref, f_ref, halo_s, hc_s, a_s, b_s):
        @pl.when(pl.program_id(0) == 0)
        def _():
            halo_s[...] = jnp.zeros_like(halo_s)
            hc_s[...] = jnp.zeros_like(hc_s)

        x = xr_ref[...]
        xc = _conv4(x, halo_s[...], _w_rows(cw_ref), cb_ref[...])
        halo_s[...] = x[tb - 8:]
        r, ig, a, f = _rnn_gates(xc, wa_ref[...], wx_ref[...], ba_ref[...], bx_ref[...], _softplus_neg(lam_ref[...]))
        xc_ref[...] = xc
        r_ref[...] = r
        ig_ref[...] = ig
        a_ref[...] = a
        f_ref[...] = f
        a_s[...] = a
        b_s[...] = f * ig * xc
        row8 = lax.broadcasted_iota(jnp.int32, (8, D_RNN), 0)

        def tile(t, hc):
            o = pl.multiple_of(t * 8, 8)
            at = a_s[pl.ds(o, 8), :]
            bt = b_s[pl.ds(o, 8), :]
            for s in (1, 2, 4):
                keep = row8 >= s
                a_sh = jnp.where(keep, pltpu.roll(at, s, 0), 1.0)
                b_sh = jnp.where(keep, pltpu.roll(bt, s, 0), 0.0)
                bt = at * b_sh + bt
                at = at * a_sh
            ht = at * hc + bt
            b_s[pl.ds(o, 8), :] = ht
            return _row_sum(jnp.where(row8 == 7, ht, 0.0))

        hc_s[0:1, :] = lax.fori_loop(0, tb // 8, tile, hc_s[0:1, :], unroll=2)
        h = b_s[...]
        h_ref[...] = h
        rec_ref[...] = (h * _gelu(gr_ref[...])).astype(BF16)

    vec = _resident((1, D_RNN))
    kept = jax.ShapeDtypeStruct((S, D_RNN), F32)
    return _pcall(
        body, (xr, gr, conv_w, conv_b, wa, wx, ba, bx, lam), name="rnn_fwd", grid=(S // tb,), sem="arbitrary", comm=comm,
        in_specs=[_rows(tb, D_RNN), _rows(tb, D_RNN), _resident((4, D_RNN)), vec,
                  _resident((D_RNN, D_RNN)), _resident((D_RNN, D_RNN)), vec, vec, vec],
        out_specs=[_rows(tb, D_RNN)] * 7,
        out_shape=[jax.ShapeDtypeStruct((S, D_RNN), BF16), kept, kept, kept, kept, kept, kept],
        scratch_shapes=[pltpu.VMEM((8, D_RNN), F32), pltpu.VMEM((8, D_RNN), F32),
                        pltpu.VMEM((tb, D_RNN), F32), pltpu.VMEM((tb, D_RNN), F32)])


def _mix_ln1_up(x, att, rec, w_out, ln1_g, ln1_b, w_up_top, w_up_bot, fcw, fcb, comm=None):
    S = x.shape[0]
    tb = min(256, S)
    nblk, kh, wblk = w_up_top.shape
    half = nblk // 2

    def body(x_ref, att_ref, rec_ref, wo_ref, g_ref, b_ref, wt_hbm, wb_hbm, fcw_ref, fcb_ref,
             z1_ref, h1b_ref, gate_ref, act_ref, gl_ref, vdgl_ref, halo_s, wu_s, wu_sems):
        @pl.when(pl.program_id(0) == 0)
        def _():
            halo_s[...] = jnp.zeros_like(halo_s)
            _load_row_halves(wt_hbm, wb_hbm, wu_s, wu_sems)

        z1 = ALPHA * x_ref[...] + _dot(att_ref[...], wo_ref[:D_ATT, :]) + _dot(rec_ref[...], wo_ref[D_ATT:, :])
        z1_ref[...] = z1
        xhat, _ = _ln_stats(z1)
        h1b = (xhat * g_ref[...] + b_ref[...]).astype(BF16)
        h1b_ref[...] = h1b
        for jj in range(half):
            cols = slice(jj * wblk, (jj + 1) * wblk)
            gate = _dot(h1b, wu_s[jj])
            val = _dot(h1b, wu_s[jj + half])
            halo = halo_s[:, cols]
            conv = (fcb_ref[:, cols] + fcw_ref[2:3, cols] * gate + fcw_ref[1:2, cols] * _shift_down(gate, halo, 1)
                    + fcw_ref[0:1, cols] * _shift_down(gate, halo, 2))
            halo_s[:, cols] = gate[tb - 8:]
            gl, dgl = _gelu_and_grad(conv)
            gate_ref[:, cols] = gate.astype(BF16)
            act_ref[:, cols] = (gl * val).astype(BF16)
            gl_ref[:, cols] = gl.astype(BF16)
            vdgl_ref[:, cols] = (val * dgl).astype(BF16)

    vec = _resident((1, D_MODEL))
    wide = jax.ShapeDtypeStruct((S, D_FF), BF16)
    return _pcall(
        body, (x, att, rec, w_out, ln1_g, ln1_b, w_up_top, w_up_bot, fcw, fcb), name="mix_ln1_up", grid=(S // tb,),
        sem="arbitrary", comm=comm,
        in_specs=[_rows(tb, D_MODEL), _rows(tb, D_ATT), _rows(tb, D_RNN), _resident((D_MODEL, D_MODEL)), vec, vec,
                  _ANY, _ANY, _resident((3, D_FF)), _resident((1, D_FF))],
        out_specs=[_rows(tb, D_MODEL), _rows(tb, D_MODEL)] + [_rows(tb, D_FF)] * 4,
        out_shape=[jax.ShapeDtypeStruct((S, D_MODEL), F32), jax.ShapeDtypeStruct((S, D_MODEL), BF16), wide, wide, wide, wide],
        scratch_shapes=[pltpu.VMEM((8, D_FF), F32), pltpu.VMEM((nblk, 2 * kh, wblk), BF16),
                        pltpu.SemaphoreType.DMA((2,))])


def _tail(act, gl, vdgl, z1, h1b, p, tgt, w_down, w_pg, b_pg, w_pp, ln1_g, ln1_b, ln2_g, ln2_b):
    S = z1.shape[0]
    tb = min(256, S)

    def body(act_ref, gl_ref, vdgl_ref, z1_ref, h1b_ref, p_ref, t_ref, wd_ref, wpg_ref, bpg_ref, wpp_ref,
             g1_ref, b1_ref, g2_ref, b2_ref, dz2_ref, dpre_ref, dpp_ref, dgc_ref, dval_ref, dh1_ref, acc_ref):
        i = pl.program_id(0)

        @pl.when(i == 0)
        def _():
            acc_ref[...] = jnp.zeros_like(acc_ref)

        ffn = _dot(act_ref[...], wd_ref[...])
        xhat1, _ = _ln_stats(z1_ref[...])
        h1 = xhat1 * g1_ref[...] + b1_ref[...]
        sg = _sigmoid(_dot(h1b_ref[...], wpg_ref[...]) + bpg_ref[...])
        pp = _dot(p_ref[...].astype(BF16), wpp_ref[...])
        z2 = ALPHA * h1 + ffn + sg * pp
        xhat2, rstd2 = _ln_stats(z2)
        y = xhat2 * g2_ref[...] + b2_ref[...]
        err = y - t_ref[...]
        dy = err * (1.0 / D_MODEL)
        loss = 0.5 * jnp.sum(jnp.sum(err * err, axis=1, keepdims=True), axis=0, keepdims=True) * (1.0 / D_MODEL)
        dz2 = _ln_bwd(dy, xhat2, rstd2, g2_ref[...])
        dz2b = dz2.astype(BF16)
        dz2_ref[...] = dz2b
        dpre = dz2 * pp * sg * (1.0 - sg)
        dpreb = dpre.astype(BF16)
        dpre_ref[...] = dpreb
        dpp_ref[...] = (dz2 * sg).astype(BF16)
        dh1_ref[...] = ALPHA * dz2 + _dot_nt(dpreb, wpg_ref[...])
        dactb = _dot_nt(dz2b, wd_ref[...]).astype(BF16)
        dval_ref[...] = dactb * gl_ref[...]
        dgc_ref[...] = dactb * vdgl_ref[...]
        _put_rows(acc_ref, [_row_sum(dy * xhat2), _row_sum(dy), _row_sum(dpre),
                            jnp.broadcast_to(loss, (1, D_MODEL))])

    vec = _resident((1, D_MODEL))
    return pl.pallas_call(
        body, name="tail", grid=(S // tb,),
        in_specs=[_rows(tb, D_FF), _rows(tb, D_FF), _rows(tb, D_FF), _rows(tb, D_MODEL), _rows(tb, D_MODEL),
                  _rows(tb, PLE_DIM), _rows(tb, D_MODEL), _resident((D_FF, D_MODEL)), _resident((D_MODEL, D_MODEL)), vec,
                  _resident((PLE_DIM, D_MODEL)), vec, vec, vec, vec],
        out_specs=[_rows(tb, D_MODEL), _rows(tb, D_MODEL), _rows(tb, D_MODEL), _rows(tb, D_FF),
                   _rows(tb, D_FF), _rows(tb, D_MODEL), _acc((8, D_MODEL))],
        out_shape=[jax.ShapeDtypeStruct((S, D_MODEL), BF16),
                   jax.ShapeDtypeStruct((S, D_MODEL), BF16), jax.ShapeDtypeStruct((S, D_MODEL), BF16),
                   jax.ShapeDtypeStruct((S, D_FF), BF16), jax.ShapeDtypeStruct((S, D_FF), BF16),
                   jax.ShapeDtypeStruct((S, D_MODEL), F32), jax.ShapeDtypeStruct((8, D_MODEL), F32)],
        compiler_params=_params("arbitrary"),
    )(act, gl, vdgl, z1, h1b, p, tgt, w_down, w_pg, b_pg, w_pp, ln1_g, ln1_b, ln2_g, ln2_b)


def _weight_grad(a_list, b_list, name, layout, ts=512, comm=None, b_window=None, halves=False):
    S = a_list[0].shape[0]
    ms = [a.shape[1] for a in a_list]
    M, nb = sum(ms), len(b_list)
    win, Nb = b_window if b_window else (0, b_list[0].shape[1])
    ts = min(ts, S)
    nk = S // ts
    per_b = N_DEV // nb
    na = len(a_list)

    n_out = 2 if halves else 1
    assert layout == "cols" or not halves

    def body(*refs):
        a_refs, b_refs, o_refs, acc_ref = refs[:na], refs[na:na + nb], refs[na + nb:na + nb + n_out], refs[-1]
        o_ref = o_refs[0]
        j, k = pl.program_id(0), pl.program_id(1)

        @pl.when(k == 0)
        def _():
            acc_ref[...] = jnp.zeros_like(acc_ref)

        for jj in range(nb):
            @pl.when(j == jj)
            def _():
                b = b_refs[jj][...].astype(BF16)
                off = 0
                for a_ref, m in zip(a_refs, ms):
                    acc_ref[off:off + m, :] += _dot_tn(a_ref[...].astype(BF16), b)
                    off += m

        @pl.when(k == nk - 1)
        def _():
            for d in range(per_b):
                if layout == "rows":
                    o_ref[d] = acc_ref[d * (M // N_DEV):(d + 1) * (M // N_DEV), :].astype(BF16)
                elif layout == "cols" and halves:
                    for o_half, r0 in zip(o_refs, (0, M // 2)):
                        o_half[d] = acc_ref[r0:r0 + M // 2, d * (Nb // per_b):(d + 1) * (Nb // per_b)].astype(BF16)
                elif layout == "cols":
                    o_ref[d] = acc_ref[:, d * (Nb // per_b):(d + 1) * (Nb // per_b)].astype(BF16)
                else:
                    o_ref[d] = acc_ref[:, d * (Nb // per_b):(d + 1) * (Nb // per_b)].T.astype(BF16)

    def b_index(jj):
        return lambda j, k: (jnp.where(j == jj, k, jnp.where(j < jj, 0, nk - 1)), win)

    if layout == "rows":
        assert nb == 1
        blk = (N_DEV, M // N_DEV, Nb)
    elif layout == "cols":
        blk = (per_b, M // n_out, Nb // per_b)
    else:
        blk = (per_b, Nb // per_b, M)
    res, comm_res = _pcall(
        body, (*a_list, *b_list), name=name, grid=(nb, nk), sem="arbitrary", comm=comm, step_axis=1,
        in_specs=[pl.BlockSpec((ts, m), lambda j, k: (k, 0)) for m in ms]
        + [pl.BlockSpec((ts, Nb), b_index(jj)) for jj in range(nb)],
        out_specs=[pl.BlockSpec(blk, lambda j, k: (j, 0, 0))] * n_out,
        out_shape=[jax.ShapeDtypeStruct((N_DEV,) + blk[1:], BF16)] * n_out,
        scratch_shapes=[pltpu.VMEM((M, Nb), F32)])
    res = res if halves else res[0]
    return (res, comm_res) if comm is not None else res


def _pg_pp_grad(h1b, dpreb, p, dppb, ts=1024):
    S = h1b.shape[0]
    ts = min(ts, S)
    nk = S // ts
    rows, cols = D_MODEL // N_DEV, D_MODEL // N_DEV

    def body(h_ref, dpre_ref, p_ref, dpp_ref, gpg_ref, gpp_ref, acc_pg, acc_pp):
        k = pl.program_id(0)

        @pl.when(k == 0)
        def _():
            acc_pg[...] = jnp.zeros_like(acc_pg)
            acc_pp[...] = jnp.zeros_like(acc_pp)

        acc_pg[...] += _dot_tn(h_ref[...], dpre_ref[...])
        acc_pp[...] += _dot_tn(p_ref[...].astype(BF16), dpp_ref[...])

        @pl.when(k == nk - 1)
        def _():
            for d in range(N_DEV):
                gpg_ref[d] = acc_pg[d * rows:(d + 1) * rows, :].astype(BF16)
                gpp_ref[d] = acc_pp[:, d * cols:(d + 1) * cols].astype(BF16)

    return pl.pallas_call(
        body, name="pg_pp_grad", grid=(nk,),
        in_specs=[_rows(ts, D_MODEL), _rows(ts, D_MODEL), _rows(ts, PLE_DIM), _rows(ts, D_MODEL)],
        out_specs=[_acc((N_DEV, rows, D_MODEL)), _acc((N_DEV, PLE_DIM, cols))],
        out_shape=[jax.ShapeDtypeStruct((N_DEV, rows, D_MODEL), BF16), jax.ShapeDtypeStruct((N_DEV, PLE_DIM, cols), BF16)],
        scratch_shapes=[pltpu.VMEM((D_MODEL, D_MODEL), F32), pltpu.VMEM((PLE_DIM, D_MODEL), F32)],
        compiler_params=_params("arbitrary"))(h1b, dpreb, p, dppb)


def _up_bwd(dgc, gate, dval, dh1p, z1, w_up_top, w_up_bot, fcw, w_out, ln1_g, comm=None):
    S = z1.shape[0]
    tb = min(256, S)
    t16 = tb // 16
    n16 = S // 16
    nblk, kh, wblk = w_up_top.shape
    half = nblk // 2
    nsteps = S // tb

    def body(dgc_ref, dgn_ref, gc_ref, dval_ref, dh1p_ref, z1_ref, wt_hbm, wb_hbm, fcw_ref, wo_ref, g1_ref,
             dgate_ref, dz1_ref, dz1b_ref, datt_ref, drec_ref, accf_ref, accd_ref, wu_s, wu_sems):
        i = pl.program_id(0)

        @pl.when(i == 0)
        def _():
            accf_ref[...] = jnp.zeros_like(accf_ref)
            accd_ref[...] = jnp.zeros_like(accd_ref)
            _load_row_halves(wt_hbm, wb_hbm, wu_s, wu_sems)

        dg = dgc_ref[...].astype(F32)
        nxt = jnp.where(i < nsteps - 1, dgn_ref[...].astype(F32)[0:8], 0.0)
        w = _w_rows(fcw_ref)
        up1, up2 = _shift_up(dg, nxt, 1), _shift_up(dg, nxt, 2)
        dgate = (w[2] * dg + w[1] * up1 + w[0] * up2).astype(BF16)
        dgate_ref[...] = dgate
        gate = gc_ref[...].astype(F32)
        _put_rows(accf_ref, [_row_sum(up2 * gate), _row_sum(up1 * gate), _row_sum(dg * gate), _row_sum(dg)])

        dh1 = dh1p_ref[...]
        for j in range(nblk):
            src = dgate if j < half else dval_ref[...]
            jj = j % half
            dh1 = dh1 + _dot_nt(src[:, jj * wblk:(jj + 1) * wblk], wu_s[j])
        xhat1, rstd1 = _ln_stats(z1_ref[...])
        dz1 = _ln_bwd(dh1, xhat1, rstd1, g1_ref[...])
        dz1_ref[...] = dz1
        dz1b = dz1.astype(BF16)
        dz1b_ref[...] = dz1b
        dcat = _dot_nt(dz1b, wo_ref[...])
        datt_ref[...] = dcat[:, :D_ATT].astype(BF16)
        drec_ref[...] = dcat[:, D_ATT:]
        _put_rows(accd_ref, [_row_sum(dh1 * xhat1), _row_sum(dh1)])

    next16 = pl.BlockSpec((16, D_FF), lambda i: (jnp.minimum((i + 1) * t16, n16 - 1), 0))
    return _pcall(
        body, (dgc, dgc, gate, dval, dh1p, z1, w_up_top, w_up_bot, fcw, w_out, ln1_g), name="up_bwd",
        grid=(nsteps,), sem="arbitrary", comm=comm,
        in_specs=[_rows(tb, D_FF), next16, _rows(tb, D_FF), _rows(tb, D_FF), _rows(tb, D_MODEL),
                  _rows(tb, D_MODEL), _ANY, _ANY, _resident((3, D_FF)),
                  _resident((D_MODEL, D_MODEL)), _resident((1, D_MODEL))],
        scratch_shapes=[pltpu.VMEM((nblk, 2 * kh, wblk), BF16), pltpu.SemaphoreType.DMA((2,))],
        out_specs=[_rows(tb, D_FF), _rows(tb, D_MODEL), _rows(tb, D_MODEL), _rows(tb, D_ATT), _rows(tb, D_RNN),
                   _acc((8, D_FF)), _acc((8, D_MODEL))],
        out_shape=[jax.ShapeDtypeStruct((S, D_FF), BF16), jax.ShapeDtypeStruct((S, D_MODEL), F32),
                   jax.ShapeDtypeStruct((S, D_MODEL), BF16), jax.ShapeDtypeStruct((S, D_ATT), BF16),
                   jax.ShapeDtypeStruct((S, D_RNN), F32), jax.ShapeDtypeStruct((8, D_FF), F32),
                   jax.ShapeDtypeStruct((8, D_MODEL), F32)])


def _attn_bwd(q, k, v, lse, do, sinks, comm=None):
    S = q.shape[0]
    grp = N_HEADS // N_KV
    nq = min(ATT_STEP, S // QBLK)

    def body(sink_ref, q_ref, kc_ref, kp_ref, vc_ref, vp_ref, do_ref, lse_ref, dq_ref, dkc_ref, dkp_ref, dvc_ref, dvp_ref,
             ds_ref):
        i = pl.program_id(0)

        @pl.when(i == 0)
        def _():
            ds_ref[...] = jnp.zeros_like(ds_ref)

        row8 = lax.broadcasted_iota(jnp.int32, (8, 128), 0)
        lane8 = lax.broadcasted_iota(jnp.int32, (8, 128), 1)
        dsink = jnp.zeros((8, 128), F32)
        kall = jnp.concatenate([kp_ref[...], kc_ref[...]], axis=0)
        vall = jnp.concatenate([vp_ref[...], vc_ref[...]], axis=0)
        dk_t = [jnp.zeros((D_KV, QBLK), F32) for _ in range(nq + 1)]
        dv_t = [jnp.zeros((D_KV, QBLK), F32) for _ in range(nq + 1)]
        for b in range(nq):
            valid = _band_mask(i * nq + b)
            rows = slice(b * QBLK, (b + 1) * QBLK)
            keys = slice(b * QBLK, (b + 2) * QBLK)
            qv, dov = q_ref[rows, :], do_ref[rows, :]
            dqs, dks, dvs = [], [], []
            for g in range(N_KV):
                kcat = kall[keys, g * HEAD_DIM:(g + 1) * HEAD_DIM]
                vcat = vall[keys, g * HEAD_DIM:(g + 1) * HEAD_DIM]
                q4, do4 = _stack_heads(qv, g), _stack_heads(dov, g)
                s = jnp.where(valid, _dot_nt(q4, kcat), -1e30)
                lse = lse_ref[(b * N_KV + g) * GROUP * QBLK:(b * N_KV + g + 1) * GROUP * QBLK, :]
                p = jnp.exp(s - lse)
                p_sink = jnp.exp(_sink_column(sink_ref, g) - lse)
                dp = _dot_nt(do4, vcat)
                delta = jnp.sum(p * dp, axis=1, keepdims=True)
                dsc = (p * (dp - delta)).astype(BF16)
                dqs += _unstack_heads(_dot(dsc, kcat) * (HEAD_DIM ** -0.5))
                dks.append(_dot_tn(q4, dsc))
                dvs.append(_dot_tn(do4, p.astype(BF16)))
                for hh, part in enumerate(_unstack_heads(-p_sink * delta)):
                    here = (row8 == 0) & (lane8 == g * grp + hh)
                    dsink = dsink + jnp.where(here, jnp.sum(part, axis=0, keepdims=True), 0.0)
            dq_ref[rows, :] = jnp.concatenate(dqs, axis=1).astype(BF16)
            dk2, dv2 = jnp.concatenate(dks, axis=0), jnp.concatenate(dvs, axis=0)
            dk_t[b], dk_t[b + 1] = dk_t[b] + dk2[:, :QBLK], dk_t[b + 1] + dk2[:, QBLK:]
            dv_t[b], dv_t[b + 1] = dv_t[b] + dv2[:, :QBLK], dv_t[b + 1] + dv2[:, QBLK:]
        dkp_ref[...] = dk_t[0].T
        dvp_ref[...] = dv_t[0].T
        for b in range(nq):
            dkc_ref[b * QBLK:(b + 1) * QBLK, :] = dk_t[b + 1].T
            dvc_ref[b * QBLK:(b + 1) * QBLK, :] = dv_t[b + 1].T
        ds_ref[...] += dsink

    nsteps = S // (nq * QBLK)
    cur = jax.ShapeDtypeStruct((S, D_KV), F32)
    prev = jax.ShapeDtypeStruct((nsteps * QBLK, D_KV), F32)
    big = _rows(nq * QBLK, D_ATT)
    return _pcall(
        body, (sinks, q, k, k, v, v, do, lse), name="attn_bwd", grid=(nsteps,), sem="arbitrary", comm=comm,
        in_specs=[pl.BlockSpec(memory_space=pltpu.SMEM), big] + _attn_specs(nq) + [big, _rows(nq * N_HEADS * QBLK, 1)],
        out_specs=[big, _rows(nq * QBLK, D_KV), _rows(QBLK, D_KV), _rows(nq * QBLK, D_KV), _rows(QBLK, D_KV),
                   _acc((8, 128))],
        out_shape=[jax.ShapeDtypeStruct((S, D_ATT), BF16), cur, prev, cur, prev, jax.ShapeDtypeStruct((8, 128), F32)])


def _rnn_bwd(xr, gr, h, kept, drec, conv_w, wa, wx, lam, comm=None):
    S = xr.shape[0]
    tb = min(512, S)
    t8 = tb // 8
    nsteps = S // tb

    def body(xr_ref, xp_ref, gr_ref, h_ref, hp_ref, xc_ref, r_ref, ig_ref, a_ref, f_ref, drec_ref, cw_ref, wa_ref, wx_ref,
             lam_ref, dxr_ref, dgr_ref, gwa_ref, gwx_ref, acc_ref, carry_s, dxc_halo_s, d_s, gwa_s, gwx_s):
        i = pl.program_id(0)
        blk = nsteps - 1 - i

        @pl.when(i == 0)
        def _():
            gwa_s[...] = jnp.zeros_like(gwa_s)
            gwx_s[...] = jnp.zeros_like(gwx_s)
            acc_ref[...] = jnp.zeros_like(acc_ref)
            carry_s[...] = jnp.zeros_like(carry_s)
            dxc_halo_s[...] = jnp.zeros_like(dxc_halo_s)

        x = xr_ref[...]
        xhalo = jnp.where(blk > 0, xp_ref[...], 0.0)
        cw = _w_rows(cw_ref)
        xs = [_shift_down(x, xhalo, 3), _shift_down(x, xhalo, 2), _shift_down(x, xhalo, 1), x]
        xc, r, ig, a, f = xc_ref[...], r_ref[...], ig_ref[...], a_ref[...], f_ref[...]
        sp = _softplus_neg(lam_ref[...])
        hcur = h_ref[...]
        hprev = _shift_down(hcur, jnp.where(blk > 0, hp_ref[...], 0.0), 1)
        gl, dgl = _gelu_and_grad(gr_ref[...])
        drec = drec_ref[...]
        dgr_ref[...] = (drec * hcur * dgl).astype(BF16)
        d_s[...] = drec * gl
        row8 = lax.broadcasted_iota(jnp.int32, (8, D_RNN), 0)

        def tile(t, c):
            o = pl.multiple_of((t8 - 1 - t) * 8, 8)
            a8 = a_ref[pl.ds(o, 8), :]
            dt = d_s[pl.ds(o, 8), :]
            at = jnp.where(row8 == 7, 1.0, pltpu.roll(a8, 7, 0))
            for s in (1, 2, 4):
                keep = row8 < 8 - s
                a_sh = jnp.where(keep, pltpu.roll(at, 8 - s, 0), 1.0)
                d_sh = jnp.where(keep, pltpu.roll(dt, 8 - s, 0), 0.0)
                dt = at * d_sh + dt
                at = at * a_sh
            lt = at * c + dt
            d_s[pl.ds(o, 8), :] = lt
            return _row_sum(jnp.where(row8 == 0, a8 * lt, 0.0))

        carry_s[0:1, :] = lax.fori_loop(0, t8, tile, carry_s[0:1, :], unroll=2)
        lmb = d_s[...]
        a2 = a * a
        dla = lmb * hprev * a - lmb * ig * xc * (a2 / f)
        di = lmb * f * xc
        dr = dla * (-LRU_C) * sp
        dpa = dr * r * (1.0 - r)
        dpx = di * ig * (1.0 - ig)
        dpab = dpa.astype(BF16)
        dpxb = dpx.astype(BF16)
        xcb = xc.astype(BF16)
        gwa_s[...] += _dot_tn(xcb, dpab)
        gwx_s[...] += _dot_tn(xcb, dpxb)

        @pl.when(i == nsteps - 1)
        def _():
            for dense, out in ((gwa_s[...], gwa_ref), (gwx_s[...], gwx_ref)):
                for b in range(RNN_BLOCKS):
                    rows = slice(b * HEAD_DIM, (b + 1) * HEAD_DIM)
                    out[rows, :] = dense[rows, b * HEAD_DIM:(b + 1) * HEAD_DIM]

        dxc = lmb * f * ig + _dot_nt(dpab, wa_ref[...]) + _dot_nt(dpxb, wx_ref[...])
        nxt = dxc_halo_s[...]
        dxr = cw[3] * dxc
        for s in (1, 2, 3):
            dxr = dxr + cw[3 - s] * _shift_up(dxc, nxt, s)
        dxr_ref[...] = dxr.astype(BF16)
        dxc_halo_s[...] = dxc[:8]
        dlam = _row_sum(dla * (-LRU_C) * r) * (-1.0 / (1.0 + jnp.exp(lam_ref[...])))
        _put_rows(acc_ref, [_row_sum(dxc * xs[0]), _row_sum(dxc * xs[1]), _row_sum(dxc * xs[2]), _row_sum(dxc * xs[3]),
                            _row_sum(dxc), _row_sum(dpa), _row_sum(dpx), dlam])

    rev = lambda i: (nsteps - 1 - i, 0)
    prev8 = lambda i: (jnp.maximum((nsteps - 1 - i) * t8 - 1, 0), 0)
    blkspec = pl.BlockSpec((tb, D_RNN), rev)
    halo8 = pl.BlockSpec((8, D_RNN), prev8)
    vec = _resident((1, D_RNN))
    return _pcall(
        body, (xr, xr, gr, h, h, *kept, drec, conv_w, wa, wx, lam), name="rnn_bwd", grid=(nsteps,),
        sem="arbitrary", comm=comm,
        in_specs=[blkspec, halo8, blkspec, blkspec, halo8] + [blkspec] * 6
        + [_resident((4, D_RNN)), _resident((D_RNN, D_RNN)), _resident((D_RNN, D_RNN)), vec],
        out_specs=[blkspec, blkspec, _acc((D_RNN, HEAD_DIM)), _acc((D_RNN, HEAD_DIM)), _acc((8, D_RNN))],
        out_shape=[jax.ShapeDtypeStruct((S, D_RNN), BF16), jax.ShapeDtypeStruct((S, D_RNN), BF16),
                   jax.ShapeDtypeStruct((D_RNN, HEAD_DIM), F32), jax.ShapeDtypeStruct((D_RNN, HEAD_DIM), F32),
                   jax.ShapeDtypeStruct((8, D_RNN), F32)],
        scratch_shapes=[pltpu.VMEM((8, D_RNN), F32), pltpu.VMEM((8, D_RNN), F32), pltpu.VMEM((tb, D_RNN), F32),
                        pltpu.VMEM((D_RNN, D_RNN), F32), pltpu.VMEM((D_RNN, D_RNN), F32)])


def _in_bwd(dq, dkc, dkp, dvc, dvp, dxr, dgr, dz1, w_in, comm=None):
    S = dz1.shape[0]
    tb = min(ATT_STEP * QBLK, S)
    nsteps = S // tb
    ring = 3

    def body(dq_ref, dkc_ref, dkn_ref, dvc_ref, dvn_ref, dxr_ref, dgr_ref, dz1_hbm, w_ref, dkv_ref, dx_ref,
             ring_s, ring_sems):
        i = pl.program_id(0)
        last = i == nsteps - 1

        def fetch(step):
            slot = step % ring
            rows = pl.ds(pl.multiple_of(step * tb, tb), tb)
            return pltpu.make_async_copy(dz1_hbm.at[rows, :], ring_s.at[slot], ring_sems.at[slot])

        @pl.when(i == 0)
        def _():
            for step in range(min(ring - 1, nsteps)):
                fetch(step).start()

        @pl.when(i + ring - 1 < nsteps)
        def _():
            fetch(i + ring - 1).start()

        def total(cur_ref, next_ref):
            nxt = jnp.where(last, 0.0, next_ref[...])
            tail = cur_ref[tb - QBLK:, :] + nxt
            return jnp.concatenate([cur_ref[:tb - QBLK, :], tail], axis=0) if tb > QBLK else tail

        dkv = jnp.concatenate([total(dkc_ref, dkn_ref), total(dvc_ref, dvn_ref)], axis=1).astype(BF16)
        dkv_ref[...] = dkv
        du = jnp.concatenate([dq_ref[...], dkv, dxr_ref[...], dgr_ref[...]], axis=1)
        prod = _dot(du, w_ref[...])
        fetch(i).wait()
        dx_ref[...] = ALPHA * ring_s[i % ring] + prod

    nextp = pl.BlockSpec((QBLK, D_KV), lambda i: (jnp.minimum(i + 1, nsteps - 1), 0))
    return _pcall(
        body, (dq, dkc, dkp, dvc, dvp, dxr, dgr, dz1, w_in), name="in_bwd", grid=(nsteps,), sem="arbitrary", comm=comm,
        in_specs=[_rows(tb, D_ATT), _rows(tb, D_KV), nextp, _rows(tb, D_KV), nextp,
                  _rows(tb, D_RNN), _rows(tb, D_RNN), _ANY, _resident((D_IN, D_MODEL))],
        out_specs=[_rows(tb, 2 * D_KV), _rows(tb, D_MODEL)],
        out_shape=[jax.ShapeDtypeStruct((S, 2 * D_KV), BF16), jax.ShapeDtypeStruct((S, D_MODEL), F32)],
        scratch_shapes=[pltpu.VMEM((ring, tb, D_MODEL), F32), pltpu.SemaphoreType.DMA((ring,))])


def _block_diag(w):
    eye = jnp.eye(RNN_BLOCKS, dtype=w.dtype)
    return (w[:, :, None, :] * eye[:, None, :, None]).reshape(D_RNN, D_RNN).astype(BF16)


def _adamw(w, g, m, v):
    m = ADAM_B1 * m + (1.0 - ADAM_B1) * g
    v = ADAM_B2 * v + (1.0 - ADAM_B2) * (g * g)
    m_hat = m / (1.0 - ADAM_B1 ** ADAM_STEP)
    v_hat = v / (1.0 - ADAM_B2 ** ADAM_STEP)
    delta = -ADAM_LR * (m_hat / (jnp.sqrt(v_hat) + ADAM_EPS) + ADAM_WD * w)
    return delta, m, v


def _sum_adamw(parts, w, m, v, name):
    parts = parts if isinstance(parts, (list, tuple)) else [parts]
    R, C = w.shape
    rb = R if R <= 256 else (256 if parts[0].shape[1] % 256 == 0 else 128)
    per = parts[0].shape[1] // rb
    assert R % rb == 0 and parts[0].shape[1] % rb == 0
    n = len(parts)

    def body(*refs):
        p_refs = refs[:n]
        w_ref, m_ref, v_ref, g_out, d_out, m_out, v_out = refs[n:]
        which = pl.program_id(0) // per

        def total(p_ref):
            g = p_ref[0].astype(F32)
            for d in range(1, N_DEV):
                g = g + p_ref[d].astype(F32)
            return g

        g = total(p_refs[0])
        for j in range(1, n):
            g = jnp.where(which == j, total(p_refs[j]), g)
        delta, mn, vn = _adamw(w_ref[...], g, m_ref[...], v_ref[...])
        g_out[...] = g
        d_out[...] = delta
        m_out[...] = mn
        v_out[...] = vn

    def part_spec(j):
        return pl.BlockSpec((N_DEV, rb, C), lambda i: (0, jnp.clip(i - j * per, 0, per - 1), 0))

    blk = _rows(rb, C)
    out = jax.ShapeDtypeStruct((R, C), F32)
    return pl.pallas_call(
        body, name=name, grid=(R // rb,),
        in_specs=[part_spec(j) for j in range(n)] + [blk, blk, blk],
        out_specs=[blk, blk, blk, blk], out_shape=[out, out, out, out],
        compiler_params=_params("parallel"),
    )(*parts, w, m, v)


def _sum_adamw_group(items, name):
    n = len(items)

    def body(*refs):
        ins, outs = refs[:4 * n], refs[4 * n:]
        for j in range(n):
            p_ref, w_ref, m_ref, v_ref = ins[4 * j:4 * j + 4]
            g = p_ref[0].astype(F32)
            for d in range(1, N_DEV):
                g = g + p_ref[d].astype(F32)
            delta, mn, vn = _adamw(w_ref[...], g, m_ref[...], v_ref[...])
            for o_ref, val in zip(outs[4 * j:4 * j + 4], (g, delta, mn, vn)):
                o_ref[...] = val

    out_shape = [jax.ShapeDtypeStruct(w.shape, F32) for _, w, _, _ in items for _ in range(4)]
    res = pl.pallas_call(body, name=name, out_shape=out_shape, compiler_params=_params())(
        *[a for item in items for a in item])
    return [res[4 * j:4 * j + 4] for j in range(n)]


_SMALL = [("attn_sinks", "s", 0, 1, None), ("rnn_conv_w", "r", 0, 4, "cols"), ("rnn_conv_b", "r", 4, 1, None),
          ("gate_a_w", "a", 0, D_RNN, None), ("gate_a_b", "r", 5, 1, None), ("gate_x_w", "x", 0, D_RNN, None),
          ("gate_x_b", "r", 6, 1, None), ("lru_lambda", "r", 7, 1, None), ("ln1_g", "d", 0, 1, None),
          ("ln1_b", "d", 1, 1, None), ("ffn_conv_w", "f", 0, 3, "cols"), ("ffn_conv_b", "f", 3, 1, None),
          ("ple_gate_b", "t", 2, 1, None), ("ln2_g", "t", 0, 1, None), ("ln2_b", "t", 1, 1, None)]
_LOSS_ROW = 3


_ACC_COLS = {"t": (0, D_MODEL), "f": (D_MODEL, D_FF), "d": (D_MODEL + D_FF, D_MODEL), "s": (2 * D_MODEL + D_FF, 128),
             "r": (2 * D_MODEL + D_FF + 128, D_RNN)}
_ACC_WIDTH = 2 * D_MODEL + D_FF + 128 + D_RNN


def _small_update(rows_all, gates_all, params):
    flat = [arr for triple in params for arr in triple]
    n_par = len(_SMALL)

    def body(*refs):
        rows_ref, gates_ref = refs[:2]
        p_refs = refs[2:2 + 3 * n_par]
        loss_ref = refs[2 + 3 * n_par]
        o_refs = refs[3 + 3 * n_par:3 + 7 * n_par]
        rows_s, tmp_r, tmp_f = refs[3 + 7 * n_par:]
        me = _dev_index(*_place())
        rows_sum, gates_sum = rows_ref[0], gates_ref[0]
        for d in range(1, N_DEV):
            rows_sum = rows_sum + rows_ref[d]
            gates_sum = gates_sum + gates_ref[d]
        rows_s[...] = rows_sum
        t0 = _ACC_COLS["t"][0]
        loss_ref[...] = rows_s[_LOSS_ROW:_LOSS_ROW + 1, t0:t0 + 128]
        for i, (name, key, row, rows, how) in enumerate(_SMALL):
            w_ref, m_ref, v_ref = p_refs[3 * i:3 * i + 3]
            g_out, d_out, m_out, v_out = o_refs[4 * i:4 * i + 4]
            if key == "a":
                g = gates_sum[:, :HEAD_DIM]
            elif key == "x":
                g = gates_sum[:, HEAD_DIM:]
            elif how == "cols":
                c0, width = _ACC_COLS[key]
                full = rows_s[:, c0:c0 + width]
                shard = width // N_DEV
                mine = full[:, :shard]
                for d in range(1, N_DEV):
                    mine = jnp.where(me == d, full[:, d * shard:(d + 1) * shard], mine)
                tmp = tmp_r if key == "r" else tmp_f
                tmp[...] = mine
                g = tmp[row:row + rows, :]
            else:
                c0, width = _ACC_COLS[key]
                g = rows_s[row:row + rows, c0:c0 + width][:, :w_ref.shape[1]]
            delta, mn, vn = _adamw(w_ref[...], g, m_ref[...], v_ref[...])
            g_out[...] = g
            d_out[...] = delta
            m_out[...] = mn
            v_out[...] = vn

    outs = [jax.ShapeDtypeStruct((1, 128), F32)]
    for w, _, _ in params:
        outs += [jax.ShapeDtypeStruct(w.shape, F32)] * 4
    scratch = [pltpu.VMEM((8, _ACC_WIDTH), F32), pltpu.VMEM((8, D_RNN // N_DEV), F32), pltpu.VMEM((8, D_FF // N_DEV), F32)]
    res = pl.pallas_call(body, name="small_update", out_shape=outs, scratch_shapes=scratch)(rows_all, gates_all, *flat)
    return res[0], [res[1 + 4 * i:5 + 4 * i] for i in range(n_par)]


def kernel(x, p, w_in, attn_sinks, rnn_conv_w, rnn_conv_b, gate_a_w, gate_a_b, gate_x_w, gate_x_b, lru_lambda, w_out, ln1_g, ln1_b, w_ffn_up, ffn_conv_w, ffn_conv_b, w_ffn_down, ple_gate_w, ple_gate_b, ple_proj, ln2_g, ln2_b, loss_target, m_w_in, m_attn_sinks, m_rnn_conv_w, m_rnn_conv_b, m_gate_a_w, m_gate_a_b, m_gate_x_w, m_gate_x_b, m_lru_lambda, m_w_out, m_ln1_g, m_ln1_b, m_w_ffn_up, m_ffn_conv_w, m_ffn_conv_b, m_w_ffn_down, m_ple_gate_w, m_ple_gate_b, m_ple_proj, m_ln2_g, m_ln2_b, v_w_in, v_attn_sinks, v_rnn_conv_w, v_rnn_conv_b, v_gate_a_w, v_gate_a_b, v_gate_x_w, v_gate_x_b, v_lru_lambda, v_w_out, v_ln1_g, v_ln1_b, v_w_ffn_up, v_ffn_conv_w, v_ffn_conv_b, v_w_ffn_down, v_ple_gate_w, v_ple_gate_b, v_ple_proj, v_ln2_g, v_ln2_b):
    from_col_blocks = lambda g: g.transpose(1, 0, 2).reshape(g.shape[1], N_DEV * g.shape[2])

    xs, ps, tgt, sinks = x[0], p[0, 0], loss_target[0], attn_sinks[0]
    wa, wx = _block_diag(gate_a_w[0]), _block_diag(gate_x_w[0])

    conv_cols = jnp.concatenate([rnn_conv_w[0].reshape(1, -1), ffn_conv_w[0].reshape(1, -1)], axis=1)
    n_rc, n_fc = 4 * D_RNN // N_DEV, 3 * D_FF // N_DEV
    ((g_in,),) = _comm_call([_Gather([w_in[0].T.astype(BF16)])], "gather_w_in")
    w_in_full = g_in.reshape(D_IN, D_MODEL)

    (q, k, v, xr, gr), _ = _in_proj(xs, w_in_full)
    w_up_shard = w_ffn_up[0].astype(BF16)
    (att, lse), (g_out, w_up_top, g_conv) = _attn_fwd(
        q, k, v, sinks,
        comm=_Multi([_Gather([w_out[0].astype(BF16), w_up_shard[:D_MODEL // 2]]),
                     _Bcast([jnp.broadcast_to(conv_cols, (8, n_rc + n_fc))])]))
    rcw = from_col_blocks(g_conv[:, 0, :n_rc].reshape(N_DEV, 4, D_RNN // N_DEV))
    fcw = from_col_blocks(g_conv[:, 0, n_rc:].reshape(N_DEV, 3, D_FF // N_DEV))
    (rec, h, *kept), (w_up_bot,) = _rnn_fwd(xr, gr, rcw, rnn_conv_b, wa, wx, gate_a_b, gate_x_b, lru_lambda,
                                            comm=_Gather([w_up_shard[D_MODEL // 2:]]))
    w_out_full = g_out.reshape(D_MODEL, D_MODEL)
    (z1, h1b, gate, act, gl, vdgl), (g_down, g_pg, g_pp) = _mix_ln1_up(
        xs, att, rec, w_out_full, ln1_g, ln1_b, w_up_top, w_up_bot, fcw, ffn_conv_b,
        comm=_Gather([w_ffn_down[0].astype(BF16), ple_gate_w[0].astype(BF16), ple_proj[0].astype(BF16)]))
    dz2b, dpreb, dppb, dgc, dval, dh1p, acc_t = _tail(
        act, gl, vdgl, z1, h1b, ps, tgt, g_down.reshape(D_FF, D_MODEL), g_pg.reshape(D_MODEL, D_MODEL), ple_gate_b,
        from_col_blocks(g_pp), ln1_g, ln1_b, ln2_g, ln2_b)

    gd_down = _weight_grad([dz2b], [act], "down_grad", "rows_t", ts=1024)
    gd_pg, gd_pp = _pg_pp_grad(h1b, dpreb, ps, dppb)
    (dgate, dz1, dz1b, datt, drec, acc_f, acc_d), (r_down, r_pg, r_pp) = _up_bwd(
        dgc, gate, dval, dh1p, z1, w_up_top, w_up_bot, fcw, w_out_full, ln1_g, comm=_Exchange([gd_down, gd_pg, gd_pp]))
    gd_up_top, gd_up_bot = _weight_grad([h1b], [dgate, dval], "up_grad", "cols", halves=True)
    gd_out = _weight_grad([att, rec], [dz1b], "out_grad", "rows", ts=1024)
    (dq, dkc, dkp, dvc, dvp, acc_s), (r_up_top,) = _attn_bwd(q, k, v, lse, datt, sinks, comm=_Exchange([gd_up_top]))
    early = jnp.concatenate([acc_t, acc_f, acc_d], axis=1)
    (dxr, dgr, g_wa, g_wx, acc_r), (r_up_bot, r_out, early_all) = _rnn_bwd(
        xr, gr, h, kept, drec, rcw, wa, wx, lru_lambda, comm=_Multi([_Exchange([gd_up_bot, gd_out]), _Bcast([early])]))
    (dkv, dx), _ = _in_bwd(dq, dkc, dkp, dvc, dvp, dxr, dgr, dz1, w_in_full)
    du_parts = [dq, dkv, dxr, dgr]
    lanes = D_RNN // 128
    late = jnp.concatenate([g_wa, g_wx], axis=1)
    late = jnp.concatenate([late, acc_s, acc_r.reshape(8, lanes, 128).transpose(1, 0, 2).reshape(8 * lanes, 128)], axis=0)
    width = D_MODEL // IN_GRAD_PARTS
    comm, r_parts = _Gather([late]), []
    for part in range(IN_GRAD_PARTS):
        gd_part, got = _weight_grad(du_parts, [xs], f"in_grad_{part}", "rows", ts=1024, b_window=(part, width), comm=comm)
        if part == 0:
            (late_all,) = got
        else:
            r_parts += got
        comm = _Exchange([gd_part])
    r_parts += _comm_call([comm], "exchange_w_in")[0]
    r_in = jnp.concatenate(r_parts, axis=2)
    acc_r_all = late_all[:, D_RNN + 8:].reshape(N_DEV, lanes, 8, 128).transpose(0, 2, 1, 3).reshape(N_DEV, 8, D_RNN)
    small_parts = (jnp.concatenate([early_all, late_all[:, D_RNN:D_RNN + 8], acc_r_all], axis=2),
                   late_all[:, :D_RNN])

    outs = {}
    res = _sum_adamw(r_in, w_in[0].T, m_w_in[0].T, v_w_in[0].T, "adamw_w_in")
    outs["w_in"] = [r.T[None] for r in res]
    for name, parts, w, m, v in [("w_ffn_up", [r_up_top, r_up_bot], w_ffn_up, m_w_ffn_up, v_w_ffn_up),
                                 ("w_ffn_down", r_down, w_ffn_down, m_w_ffn_down, v_w_ffn_down)]:
        res = _sum_adamw(parts, w[0], m[0], v[0], "adamw_" + name)
        outs[name] = [r[None] for r in res]
    small_shards = [("w_out", r_out, w_out, m_w_out, v_w_out), ("ple_gate_w", r_pg, ple_gate_w, m_ple_gate_w, v_ple_gate_w),
                    ("ple_proj", r_pp, ple_proj, m_ple_proj, v_ple_proj)]
    group = _sum_adamw_group([(parts, w[0], m[0], v[0]) for _, parts, w, m, v in small_shards], "adamw_small_shards")
    for (name, *_), res in zip(small_shards, group):
        outs[name] = [r[None] for r in res]

    given = dict(attn_sinks=(attn_sinks, m_attn_sinks, v_attn_sinks), rnn_conv_w=(rnn_conv_w, m_rnn_conv_w, v_rnn_conv_w),
                 rnn_conv_b=(rnn_conv_b, m_rnn_conv_b, v_rnn_conv_b), gate_a_w=(gate_a_w, m_gate_a_w, v_gate_a_w),
                 gate_a_b=(gate_a_b, m_gate_a_b, v_gate_a_b), gate_x_w=(gate_x_w, m_gate_x_w, v_gate_x_w),
                 gate_x_b=(gate_x_b, m_gate_x_b, v_gate_x_b), lru_lambda=(lru_lambda, m_lru_lambda, v_lru_lambda),
                 ln1_g=(ln1_g, m_ln1_g, v_ln1_g), ln1_b=(ln1_b, m_ln1_b, v_ln1_b),
                 ffn_conv_w=(ffn_conv_w, m_ffn_conv_w, v_ffn_conv_w), ffn_conv_b=(ffn_conv_b, m_ffn_conv_b, v_ffn_conv_b),
                 ple_gate_b=(ple_gate_b, m_ple_gate_b, v_ple_gate_b), ln2_g=(ln2_g, m_ln2_g, v_ln2_g),
                 ln2_b=(ln2_b, m_ln2_b, v_ln2_b))
    as_2d = lambda a: a.reshape(-1, a.shape[-1])
    loss_row, small_res = _small_update(*small_parts, [tuple(as_2d(a) for a in given[n]) for n, *_ in _SMALL])
    loss = loss_row[0, 0]
    for (n, *_), res in zip(_SMALL, small_res):
        outs[n] = [r.reshape(given[n][0].shape) for r in res]

    order = ["w_in", "attn_sinks", "rnn_conv_w", "rnn_conv_b", "gate_a_w", "gate_a_b", "gate_x_w", "gate_x_b",
             "lru_lambda", "w_out", "ln1_g", "ln1_b", "w_ffn_up", "ffn_conv_w", "ffn_conv_b", "w_ffn_down",
             "ple_gate_w", "ple_gate_b", "ple_proj", "ln2_g", "ln2_b"]
    return (loss, dx[None], *[outs[n][0] for n in order], *[outs[n][1] for n in order],
            *[outs[n][2] for n in order], *[outs[n][3] for n in order])
```

```python
import jax
import jax.numpy as jnp
from jax import lax
from jax.experimental import pallas as pl
from jax.experimental.pallas import tpu as pltpu

F32 = jnp.float32
BF16 = jnp.bfloat16

D_MODEL = 1024
D_ATT = 512
D_KV = 128
HEAD_DIM = 64
N_HEADS = 8
N_KV = 2
D_RNN = 512
RNN_BLOCKS = 8
D_IN = 1792
D_FF = 3072
PLE_DIM = 256
QBLK = 128
N_DEV = 8
ALPHA = float(2 ** 0.25)
LN_EPS = 1e-5
LRU_C = 8.0
ADAM_LR, ADAM_B1, ADAM_B2, ADAM_EPS, ADAM_WD, ADAM_STEP = 0.001, 0.9, 0.999, 1e-08, 0.01, 10

V7X_VMEM_LIMIT = 56 * 1024 * 1024
MESH = pl.DeviceIdType.MESH


def _params(*sem, vmem=V7X_VMEM_LIMIT):
    return pltpu.CompilerParams(dimension_semantics=sem or None, vmem_limit_bytes=vmem)


def _resident(shape):
    return pl.BlockSpec(shape, lambda *_: (0,) * len(shape), pipeline_mode=pl.Buffered(1))


def _rows(tb, cols):
    return pl.BlockSpec((tb, cols), lambda i: (i, 0))


def _acc(shape):
    return pl.BlockSpec(shape, lambda *_: (0,) * len(shape))


def _dot(a, b):
    return jnp.dot(a, b, preferred_element_type=F32)


def _dot_nt(a, b):
    return lax.dot_general(a, b, (((1,), (1,)), ((), ())), preferred_element_type=F32)


def _dot_tn(a, b):
    return lax.dot_general(a, b, (((0,), (0,)), ((), ())), preferred_element_type=F32)


def _sigmoid(x):
    return 1.0 / (1.0 + jnp.exp(-x))


_GELU_C = 0.7978845608028654
_GELU_K = 0.044715


def _gelu_and_grad(x):
    u = x * x
    t = jnp.tanh(x * (_GELU_C + (_GELU_C * _GELU_K) * u))
    hp = 0.5 + 0.5 * t
    dg = hp + x * (0.5 - 0.5 * (t * t)) * (_GELU_C + (3.0 * _GELU_C * _GELU_K) * u)
    return x * hp, dg


def _gelu(x):
    return 0.5 * x * (1.0 + jnp.tanh(_GELU_C * (x + _GELU_K * x * x * x)))


def _ln_stats(z):
    mu = jnp.mean(z, axis=-1, keepdims=True)
    zc = z - mu
    var = jnp.mean(zc * zc, axis=-1, keepdims=True)
    rstd = lax.rsqrt(var + LN_EPS)
    return zc * rstd, rstd


def _ln_bwd(dy, xhat, rstd, g):
    dxh = dy * g
    m1 = jnp.mean(dxh, axis=-1, keepdims=True)
    m2 = jnp.mean(dxh * xhat, axis=-1, keepdims=True)
    return rstd * (dxh - m1 - xhat * m2)


def _softplus_neg(lam):
    u = jnp.exp(-jnp.abs(lam))
    w = 1.0 + u
    d = w - 1.0
    log1p_u = jnp.where(d == 0.0, u, jnp.log(w) * (u / jnp.where(d == 0.0, 1.0, d)))
    return jnp.maximum(-lam, 0.0) + log1p_u


def _shift_down(x, halo, s):
    xs = pltpu.roll(x, s, 0)
    hs = pltpu.roll(halo, s, 0)
    row8 = lax.broadcasted_iota(jnp.int32, hs.shape, 0)
    first = jnp.where(row8 < s, hs, xs[:8])
    return jnp.concatenate([first, xs[8:]], axis=0)


def _shift_up(x, halo, s):
    n = x.shape[0]
    xs = pltpu.roll(x, n - s, 0)
    hs = pltpu.roll(halo, 8 - s, 0)
    row8 = lax.broadcasted_iota(jnp.int32, hs.shape, 0)
    last = jnp.where(row8 >= 8 - s, hs, xs[n - 8:])
    return jnp.concatenate([xs[:n - 8], last], axis=0)


def _row_sum(x):
    return jnp.sum(x, axis=0, keepdims=True)


def _put_rows(acc_ref, rows):
    row8 = lax.broadcasted_iota(jnp.int32, acc_ref.shape, 0)
    upd = jnp.zeros(acc_ref.shape, F32)
    for r, vec in enumerate(rows):
        upd = jnp.where(row8 == r, vec, upd)
    acc_ref[...] += upd


def _place():
    return lax.axis_index("x"), lax.axis_index("y"), lax.axis_index("c")


def _dev_index(px, py, pc):
    return 4 * px + 2 * py + pc


_ANY = pl.BlockSpec(memory_space=pl.ANY)


class _Gather:
    def __init__(self, arrays):
        self.arrays = list(arrays)
        self.n = len(self.arrays)

    def out_shape(self):
        return [jax.ShapeDtypeStruct((N_DEV,) + s.shape, s.dtype) for s in self.arrays]

    def scratch(self):
        return [pltpu.SemaphoreType.DMA((self.n, 7)), pltpu.SemaphoreType.DMA((self.n, 7)),
                pltpu.SemaphoreType.DMA((self.n,))]

    def _parts(self, ins, outs, sems):
        send_sems, recv_sems, local_sems = sems
        x, y, c = _place()
        me, sibling = (x, y, c), (x, y, 1 - c)
        chips = [(1 - x, y), (x, 1 - y), (1 - x, 1 - y)]

        def copy(a, k, block, to, src=None):
            rows = outs[a].at[_dev_index(*block)]
            return pltpu.make_async_remote_copy(
                src_ref=rows if src is None else src, dst_ref=rows, send_sem=send_sems.at[a, k],
                recv_sem=recv_sems.at[a, k], device_id=to, device_id_type=MESH)

        rng = range(self.n)
        mine = [pltpu.make_async_copy(ins[a], outs[a].at[_dev_index(*me)], local_sems.at[a]) for a in rng]
        first = [copy(a, 0, me, sibling, src=ins[a]) for a in rng]
        first += [copy(a, 1 + j, me, (*chip, c), src=ins[a]) for j, chip in enumerate(chips) for a in rng]
        landed = [copy(a, 1 + j, (*chip, c), me) for j, chip in enumerate(chips) for a in rng]
        passed = [copy(a, 4 + j, (*chip, c), sibling) for j, chip in enumerate(chips) for a in rng]
        from_sibling = [copy(a, 0, sibling, me) for a in rng]
        from_sibling += [copy(a, 4 + j, (*chip, 1 - c), me) for j, chip in enumerate(chips) for a in rng]
        return mine, first, landed, passed, from_sibling

    def start(self, ins, outs, sems):
        mine, first, _, _, _ = self._parts(ins, outs, sems)
        for cp in mine + first:
            cp.start()

    def forward(self, ins, outs, sems):
        _, _, landed, passed, _ = self._parts(ins, outs, sems)
        for got, fwd in zip(landed, passed):
            got.wait_recv()
            fwd.start()

    def finish(self, ins, outs, sems):
        mine, first, _, passed, from_sibling = self._parts(ins, outs, sems)
        for cp in from_sibling:
            cp.wait_recv()
        for cp in first + passed:
            cp.wait_send()
        for cp in mine:
            cp.wait()

    def before(self, ins, outs, sems, step, nsteps):
        pl.when(step == 0)(lambda: self.start(ins, outs, sems))
        pl.when(step == (7 * nsteps) // 8)(lambda: self.forward(ins, outs, sems))

    def after(self, ins, outs, sems, step, nsteps):
        pl.when(step == nsteps - 1)(lambda: self.finish(ins, outs, sems))


class _Exchange:
    def __init__(self, arrays):
        self.arrays = list(arrays)
        self.n = len(self.arrays)

    def out_shape(self):
        return [jax.ShapeDtypeStruct(b.shape, b.dtype) for b in self.arrays]

    def scratch(self):
        return [pltpu.SemaphoreType.DMA((self.n, 7)), pltpu.SemaphoreType.DMA((self.n, 7)),
                pltpu.SemaphoreType.DMA((self.n,))]

    def _parts(self, ins, outs, sems):
        send_sems, recv_sems, local_sems = sems
        x, y, c = _place()
        me = _dev_index(x, y, c)
        peers = [(x ^ (k >> 2), y ^ ((k >> 1) & 1), c ^ (k & 1)) for k in range(1, N_DEV)]
        rng = range(self.n)
        mine = [pltpu.make_async_copy(ins[a].at[me], outs[a].at[me], local_sems.at[a]) for a in rng]
        sent = [pltpu.make_async_remote_copy(
            src_ref=ins[a].at[_dev_index(*to)], dst_ref=outs[a].at[me], send_sem=send_sems.at[a, k],
            recv_sem=recv_sems.at[a, k], device_id=to, device_id_type=MESH) for k, to in enumerate(peers) for a in rng]
        arrivals = [pltpu.make_async_remote_copy(
            src_ref=ins[a].at[me], dst_ref=outs[a].at[_dev_index(*frm)], send_sem=send_sems.at[a, k],
            recv_sem=recv_sems.at[a, k], device_id=frm, device_id_type=MESH) for k, frm in enumerate(peers) for a in rng]
        return mine, sent, arrivals

    def start(self, ins, outs, sems):
        mine, sent, _ = self._parts(ins, outs, sems)
        for cp in mine + sent:
            cp.start()

    def finish(self, ins, outs, sems):
        mine, sent, arrivals = self._parts(ins, outs, sems)
        for cp in arrivals:
            cp.wait_recv()
        for cp in sent:
            cp.wait_send()
        for cp in mine:
            cp.wait()

    def before(self, ins, outs, sems, step, nsteps):
        pl.when(step == 0)(lambda: self.start(ins, outs, sems))

    def after(self, ins, outs, sems, step, nsteps):
        pl.when(step == nsteps - 1)(lambda: self.finish(ins, outs, sems))


class _Bcast(_Exchange):
    def out_shape(self):
        return [jax.ShapeDtypeStruct((N_DEV,) + s.shape, s.dtype) for s in self.arrays]

    def _parts(self, ins, outs, sems):
        send_sems, recv_sems, local_sems = sems
        x, y, c = _place()
        me = _dev_index(x, y, c)
        peers = [(x ^ (k >> 2), y ^ ((k >> 1) & 1), c ^ (k & 1)) for k in range(1, N_DEV)]
        rng = range(self.n)
        mine = [pltpu.make_async_copy(ins[a], outs[a].at[me], local_sems.at[a]) for a in rng]
        sent = [pltpu.make_async_remote_copy(
            src_ref=ins[a], dst_ref=outs[a].at[me], send_sem=send_sems.at[a, k], recv_sem=recv_sems.at[a, k],
            device_id=to, device_id_type=MESH) for k, to in enumerate(peers) for a in rng]
        arrivals = [pltpu.make_async_remote_copy(
            src_ref=ins[a], dst_ref=outs[a].at[_dev_index(*frm)], send_sem=send_sems.at[a, k],
            recv_sem=recv_sems.at[a, k], device_id=frm, device_id_type=MESH) for k, frm in enumerate(peers) for a in rng]
        return mine, sent, arrivals


class _Multi:
    def __init__(self, comms):
        self.comms = list(comms)
        self.arrays = [arr for c in self.comms for arr in c.arrays]
        self.n = len(self.arrays)

    def out_shape(self):
        return [s for c in self.comms for s in c.out_shape()]

    def scratch(self):
        return [s for c in self.comms for s in c.scratch()]

    def _each(self, ins, outs, sems):
        a = 0
        for j, c in enumerate(self.comms):
            yield c, ins[a:a + c.n], outs[a:a + c.n], sems[3 * j:3 * j + 3]
            a += c.n

    def before(self, ins, outs, sems, step, nsteps):
        for c, ci, co, cs in self._each(ins, outs, sems):
            c.before(ci, co, cs, step, nsteps)

    def after(self, ins, outs, sems, step, nsteps):
        for c, ci, co, cs in self._each(ins, outs, sems):
            c.after(ci, co, cs, step, nsteps)


def _comm_call(comms, name):
    ns = [c.n for c in comms]
    n = sum(ns)

    def body(*refs):
        parts, a, s = [], 0, 2 * n
        for c in comms:
            parts.append((c, refs[a:a + c.n], refs[n + a:n + a + c.n], refs[s:s + 3]))
            a, s = a + c.n, s + 3
        for c, ins, outs, sems in parts:
            c.start(ins, outs, sems)
        for c, ins, outs, sems in parts:
            if isinstance(c, _Gather):
                c.forward(ins, outs, sems)
        for c, ins, outs, sems in parts:
            c.finish(ins, outs, sems)

    res = pl.pallas_call(
        body, name=name, in_specs=[_ANY] * n, out_specs=[_ANY] * n,
        out_shape=[s for c in comms for s in c.out_shape()], scratch_shapes=[s for c in comms for s in c.scratch()],
    )(*[arr for c in comms for arr in c.arrays])
    out, a = [], 0
    for k in ns:
        out.append(res[a:a + k])
        a += k
    return out


def _pcall(body, args, *, name, grid, in_specs, out_specs, out_shape, scratch_shapes=(), sem="parallel", comm=None,
           step_axis=0):
    sem = (sem,) * len(grid) if isinstance(sem, str) else sem
    if comm is None:
        res = pl.pallas_call(body, name=name, grid=grid, in_specs=in_specs, out_specs=out_specs, out_shape=out_shape,
                             scratch_shapes=list(scratch_shapes), compiler_params=_params(*sem))(*args)
        return res, []
    n_in, n_out, n_scr, n = len(in_specs), len(out_specs), len(scratch_shapes), comm.n
    nsteps = grid[step_axis]
    assert all(g == 1 for ax, g in enumerate(grid) if ax != step_axis)

    def hosted(*refs):
        ins, cin = refs[:n_in], refs[n_in:n_in + n]
        o0 = n_in + n
        outs, cout = refs[o0:o0 + n_out], refs[o0 + n_out:o0 + n_out + n]
        s0 = o0 + n_out + n
        scr, sems = refs[s0:s0 + n_scr], refs[s0 + n_scr:]
        step = pl.program_id(step_axis)
        comm.before(cin, cout, sems, step, nsteps)
        body(*ins, *outs, *scr)
        comm.after(cin, cout, sems, step, nsteps)

    res = pl.pallas_call(
        hosted, name=name, grid=grid, in_specs=list(in_specs) + [_ANY] * n, out_specs=list(out_specs) + [_ANY] * n,
        out_shape=list(out_shape) + comm.out_shape(), scratch_shapes=list(scratch_shapes) + comm.scratch(),
        compiler_params=_params(*(("arbitrary",) * len(grid))))(*args, *comm.arrays)
    return res[:n_out], res[n_out:]


def _load_row_halves(top_hbm, bot_hbm, full_s, sems):
    r = top_hbm.shape[1]
    copies = [pltpu.make_async_copy(top_hbm, full_s.at[:, :r, :], sems.at[0]),
              pltpu.make_async_copy(bot_hbm, full_s.at[:, r:, :], sems.at[1])]
    for cp in copies:
        cp.start()
    for cp in copies:
        cp.wait()


def _in_proj(x, w_in_t, comm=None):
    S = x.shape[0]
    tb = min(1024, S)

    def body(x_ref, w_ref, q_ref, k_ref, v_ref, xr_ref, gr_ref):
        u = _dot_nt(x_ref[...].astype(BF16), w_ref[...])
        q_ref[...] = (u[:, :D_ATT] * (HEAD_DIM ** -0.5)).astype(BF16)
        k_ref[...] = u[:, D_ATT:D_ATT + D_KV].astype(BF16)
        v_ref[...] = u[:, D_ATT + D_KV:D_ATT + 2 * D_KV].astype(BF16)
        xr_ref[...] = u[:, D_ATT + 2 * D_KV:D_ATT + 2 * D_KV + D_RNN]
        gr_ref[...] = u[:, D_ATT + 2 * D_KV + D_RNN:]

    return _pcall(
        body, (x, w_in_t), name="in_proj", grid=(S // tb,), comm=comm,
        in_specs=[_rows(tb, D_MODEL), _resident((D_IN, D_MODEL))],
        out_specs=[_rows(tb, D_ATT), _rows(tb, D_KV), _rows(tb, D_KV), _rows(tb, D_RNN), _rows(tb, D_RNN)],
        out_shape=[jax.ShapeDtypeStruct((S, D_ATT), BF16), jax.ShapeDtypeStruct((S, D_KV), BF16),
                   jax.ShapeDtypeStruct((S, D_KV), BF16), jax.ShapeDtypeStruct((S, D_RNN), F32),
                   jax.ShapeDtypeStruct((S, D_RNN), F32)])


GROUP = N_HEADS // N_KV


def _band_mask(i):
    qi = lax.broadcasted_iota(jnp.int32, (GROUP * QBLK, 2 * QBLK), 0) & (QBLK - 1)
    sj = lax.broadcasted_iota(jnp.int32, (GROUP * QBLK, 2 * QBLK), 1)
    return (sj > qi) & (sj <= qi + QBLK) & ((sj >= QBLK) | (i > 0))


def _stack_heads(x, g):
    return jnp.concatenate([x[:, (g * GROUP + hh) * HEAD_DIM:(g * GROUP + hh + 1) * HEAD_DIM] for hh in range(GROUP)],
                           axis=0)


def _unstack_heads(x4):
    return [x4[hh * QBLK:(hh + 1) * QBLK] for hh in range(GROUP)]


def _sink_column(sink_ref, g):
    head = lax.broadcasted_iota(jnp.int32, (GROUP * QBLK, 1), 0) // QBLK
    col = jnp.full((GROUP * QBLK, 1), sink_ref[g * GROUP], F32)
    for hh in range(1, GROUP):
        col = jnp.where(head == hh, sink_ref[g * GROUP + hh], col)
    return col


ATT_STEP = 4
IN_GRAD_PARTS = 2


def _attn_specs(nq=1):
    cur = lambda i: (i, 0)
    prev = lambda i: (jnp.maximum(nq * i - 1, 0), 0)
    return [pl.BlockSpec((nq * QBLK, D_KV), cur), pl.BlockSpec((QBLK, D_KV), prev),
            pl.BlockSpec((nq * QBLK, D_KV), cur), pl.BlockSpec((QBLK, D_KV), prev)]


def _attn_fwd(q, k, v, sinks, comm=None):
    S = q.shape[0]
    nq = min(ATT_STEP, S // QBLK)

    def body(sink_ref, q_ref, kc_ref, kp_ref, vc_ref, vp_ref, o_ref, lse_ref):
        first = pl.program_id(0) * nq
        kall = jnp.concatenate([kp_ref[...], kc_ref[...]], axis=0)
        vall = jnp.concatenate([vp_ref[...], vc_ref[...]], axis=0)
        for b in range(nq):
            valid = _band_mask(first + b)
            rows = slice(b * QBLK, (b + 1) * QBLK)
            keys = slice(b * QBLK, (b + 2) * QBLK)
            qv = q_ref[rows, :]
            outs = []
            for g in range(N_KV):
                kcat = kall[keys, g * HEAD_DIM:(g + 1) * HEAD_DIM]
                vcat = vall[keys, g * HEAD_DIM:(g + 1) * HEAD_DIM]
                s = jnp.where(valid, _dot_nt(_stack_heads(qv, g), kcat), -1e30)
                sink = _sink_column(sink_ref, g)
                m = jnp.maximum(jnp.max(s, axis=1, keepdims=True), sink)
                p = jnp.exp(s - m)
                l = jnp.sum(p, axis=1, keepdims=True) + jnp.exp(sink - m)
                outs += _unstack_heads(_dot(p.astype(BF16), vcat) / l)
                lse_ref[(b * N_KV + g) * GROUP * QBLK:(b * N_KV + g + 1) * GROUP * QBLK, :] = m + jnp.log(l)
            o_ref[rows, :] = jnp.concatenate(outs, axis=1).astype(BF16)

    lse_rows = nq * N_HEADS * QBLK
    return _pcall(
        body, (sinks, q, k, k, v, v), name="attn_fwd", grid=(S // (nq * QBLK),), comm=comm,
        in_specs=[pl.BlockSpec(memory_space=pltpu.SMEM), _rows(nq * QBLK, D_ATT)] + _attn_specs(nq),
        out_specs=[_rows(nq * QBLK, D_ATT), _rows(lse_rows, 1)],
        out_shape=[jax.ShapeDtypeStruct((S, D_ATT), BF16), jax.ShapeDtypeStruct((S * N_HEADS, 1), F32)])


def _w_rows(w_ref):
    return [w_ref[k:k + 1, :] for k in range(w_ref.shape[0])]


def _conv4(x, halo, w, b):
    y = b + w[3] * x
    for s in (1, 2, 3):
        y = y + w[3 - s] * _shift_down(x, halo, s)
    return y


def _rnn_gates(xc, wa, wx, ba, bx, sp):
    xcb = xc.astype(BF16)
    r = _sigmoid(_dot(xcb, wa) + ba)
    ig = _sigmoid(_dot(xcb, wx) + bx)
    la = -LRU_C * r * sp
    a = jnp.exp(la)
    t = jnp.tanh(la)
    f = jnp.sqrt(-2.0 * t / (1.0 - t))
    return r, ig, a, f


def _rnn_fwd(xr, gr, conv_w, conv_b, wa, wx, ba, bx, lam, comm=None):
    S = xr.shape[0]
    tb = min(512, S)

    def body(xr_ref, gr_ref, cw_ref, cb_ref, wa_ref, wx_ref, ba_ref, bx_ref, lam_ref, rec_ref, h_ref,
             xc_ref, r_ref, ig_ref, a_ref, f_ref, halo_s, hc_s, a_s, b_s):
        @pl.when(pl.program_id(0) == 0)
        def _():
            halo_s[...] = jnp.zeros_like(halo_s)
            hc_s[...] = jnp.zeros_like(hc_s)

        x = xr_ref[...]
        xc = _conv4(x, halo_s[...], _w_rows(cw_ref), cb_ref[...])
        halo_s[...] = x[tb - 8:]
        r, ig, a, f = _rnn_gates(xc, wa_ref[...], wx_ref[...], ba_ref[...], bx_ref[...], _softplus_neg(lam_ref[...]))
        xc_ref[...] = xc
        r_ref[...] = r
        ig_ref[...] = ig
        a_ref[...] = a
        f_ref[...] = f
        a_s[...] = a
        b_s[...] = f * ig * xc
        row8 = lax.broadcasted_iota(jnp.int32, (8, D_RNN), 0)

        def tile(t, hc):
            o = pl.multiple_of(t * 8, 8)
            at = a_s[pl.ds(o, 8), :]
            bt = b_s[pl.ds(o, 8), :]
            for s in (1, 2, 4):
                keep = row8 >= s
                a_sh = jnp.where(keep, pltpu.roll(at, s, 0), 1.0)
                b_sh = jnp.where(keep, pltpu.roll(bt, s, 0), 0.0)
                bt = at * b_sh + bt
                at = at * a_sh
            ht = at * hc + bt
            b_s[pl.ds(o, 8), :] = ht
            return _row_sum(jnp.where(row8 == 7, ht, 0.0))

        hc_s[0:1, :] = lax.fori_loop(0, tb // 8, tile, hc_s[0:1, :], unroll=2)
        h = b_s[...]
        h_ref[...] = h
        rec_ref[...] = (h * _gelu(gr_ref[...])).astype(BF16)

    vec = _resident((1, D_RNN))
    kept = jax.ShapeDtypeStruct((S, D_RNN), F32)
    return _pcall(
        body, (xr, gr, conv_w, conv_b, wa, wx, ba, bx, lam), name="rnn_fwd", grid=(S // tb,), sem="arbitrary", comm=comm,
        in_specs=[_rows(tb, D_RNN), _rows(tb, D_RNN), _resident((4, D_RNN)), vec,
                  _resident((D_RNN, D_RNN)), _resident((D_RNN, D_RNN)), vec, vec, vec],
        out_specs=[_rows(tb, D_RNN)] * 7,
        out_shape=[jax.ShapeDtypeStruct((S, D_RNN), BF16), kept, kept, kept, kept, kept, kept],
        scratch_shapes=[pltpu.VMEM((8, D_RNN), F32), pltpu.VMEM((8, D_RNN), F32),
                        pltpu.VMEM((tb, D_RNN), F32), pltpu.VMEM((tb, D_RNN), F32)])


def _mix_ln1_up(x, att, rec, w_out, ln1_g, ln1_b, w_up_top, w_up_bot, fcw, fcb, comm=None):
    S = x.shape[0]
    tb = min(256, S)
    nblk, kh, wblk = w_up_top.shape
    half = nblk // 2

    def body(x_ref, att_ref, rec_ref, wo_ref, g_ref, b_ref, wt_hbm, wb_hbm, fcw_ref, fcb_ref,
             z1_ref, h1b_ref, gate_ref, act_ref, gl_ref, vdgl_ref, halo_s, wu_s, wu_sems):
        @pl.when(pl.program_id(0) == 0)
        def _():
            halo_s[...] = jnp.zeros_like(halo_s)
            _load_row_halves(wt_hbm, wb_hbm, wu_s, wu_sems)

        z1 = ALPHA * x_ref[...] + _dot(att_ref[...], wo_ref[:D_ATT, :]) + _dot(rec_ref[...], wo_ref[D_ATT:, :])
        z1_ref[...] = z1
        xhat, _ = _ln_stats(z1)
        h1b = (xhat * g_ref[...] + b_ref[...]).astype(BF16)
        h1b_ref[...] = h1b
        for jj in range(half):
            cols = slice(jj * wblk, (jj + 1) * wblk)
            gate = _dot(h1b, wu_s[jj])
            val = _dot(h1b, wu_s[jj + half])
            halo = halo_s[:, cols]
            conv = (fcb_ref[:, cols] + fcw_ref[2:3, cols] * gate + fcw_ref[1:2, cols] * _shift_down(gate, halo, 1)
                    + fcw_ref[0:1, cols] * _shift_down(gate, halo, 2))
            halo_s[:, cols] = gate[tb - 8:]
            gl, dgl = _gelu_and_grad(conv)
            gate_ref[:, cols] = gate.astype(BF16)
            act_ref[:, cols] = (gl * val).astype(BF16)
            gl_ref[:, cols] = gl.astype(BF16)
            vdgl_ref[:, cols] = (val * dgl).astype(BF16)

    vec = _resident((1, D_MODEL))
    wide = jax.ShapeDtypeStruct((S, D_FF), BF16)
    return _pcall(
        body, (x, att, rec, w_out, ln1_g, ln1_b, w_up_top, w_up_bot, fcw, fcb), name="mix_ln1_up", grid=(S // tb,),
        sem="arbitrary", comm=comm,
        in_specs=[_rows(tb, D_MODEL), _rows(tb, D_ATT), _rows(tb, D_RNN), _resident((D_MODEL, D_MODEL)), vec, vec,
                  _ANY, _ANY, _resident((3, D_FF)), _resident((1, D_FF))],
        out_specs=[_rows(tb, D_MODEL), _rows(tb, D_MODEL)] + [_rows(tb, D_FF)] * 4,
        out_shape=[jax.ShapeDtypeStruct((S, D_MODEL), F32), jax.ShapeDtypeStruct((S, D_MODEL), BF16), wide, wide, wide, wide],
        scratch_shapes=[pltpu.VMEM((8, D_FF), F32), pltpu.VMEM((nblk, 2 * kh, wblk), BF16),
                        pltpu.SemaphoreType.DMA((2,))])


def _tail(act, gl, vdgl, z1, h1b, p, tgt, w_down, w_pg, b_pg, w_pp, ln1_g, ln1_b, ln2_g, ln2_b):
    S = z1.shape[0]
    tb = min(256, S)

    def body(act_ref, gl_ref, vdgl_ref, z1_ref, h1b_ref, p_ref, t_ref, wd_ref, wpg_ref, bpg_ref, wpp_ref,
             g1_ref, b1_ref, g2_ref, b2_ref, dz2_ref, dpre_ref, dpp_ref, dgc_ref, dval_ref, dh1_ref, acc_ref):
        i = pl.program_id(0)

        @pl.when(i == 0)
        def _():
            acc_ref[...] = jnp.zeros_like(acc_ref)

        ffn = _dot(act_ref[...], wd_ref[...])
        xhat1, _ = _ln_stats(z1_ref[...])
        h1 = xhat1 * g1_ref[...] + b1_ref[...]
        sg = _sigmoid(_dot(h1b_ref[...], wpg_ref[...]) + bpg_ref[...])
        pp = _dot(p_ref[...].astype(BF16), wpp_ref[...])
        z2 = ALPHA * h1 + ffn + sg * pp
        xhat2, rstd2 = _ln_stats(z2)
        y = xhat2 * g2_ref[...] + b2_ref[...]
        err = y - t_ref[...]
        dy = err * (1.0 / D_MODEL)
        loss = 0.5 * jnp.sum(jnp.sum(err * err, axis=1, keepdims=True), axis=0, keepdims=True) * (1.0 / D_MODEL)
        dz2 = _ln_bwd(dy, xhat2, rstd2, g2_ref[...])
        dz2b = dz2.astype(BF16)
        dz2_ref[...] = dz2b
        dpre = dz2 * pp * sg * (1.0 - sg)
        dpreb = dpre.astype(BF16)
        dpre_ref[...] = dpreb
        dpp_ref[...] = (dz2 * sg).astype(BF16)
        dh1_ref[...] = ALPHA * dz2 + _dot_nt(dpreb, wpg_ref[...])
        dactb = _dot_nt(dz2b, wd_ref[...]).astype(BF16)
        dval_ref[...] = dactb * gl_ref[...]
        dgc_ref[...] = dactb * vdgl_ref[...]
        _put_rows(acc_ref, [_row_sum(dy * xhat2), _row_sum(dy), _row_sum(dpre),
                            jnp.broadcast_to(loss, (1, D_MODEL))])

    vec = _resident((1, D_MODEL))
    return pl.pallas_call(
        body, name="tail", grid=(S // tb,),
        in_specs=[_rows(tb, D_FF), _rows(tb, D_FF), _rows(tb, D_FF), _rows(tb, D_MODEL), _rows(tb, D_MODEL),
                  _rows(tb, PLE_DIM), _rows(tb, D_MODEL), _resident((D_FF, D_MODEL)), _resident((D_MODEL, D_MODEL)), vec,
                  _resident((PLE_DIM, D_MODEL)), vec, vec, vec, vec],
        out_specs=[_rows(tb, D_MODEL), _rows(tb, D_MODEL), _rows(tb, D_MODEL), _rows(tb, D_FF),
                   _rows(tb, D_FF), _rows(tb, D_MODEL), _acc((8, D_MODEL))],
        out_shape=[jax.ShapeDtypeStruct((S, D_MODEL), BF16),
                   jax.ShapeDtypeStruct((S, D_MODEL), BF16), jax.ShapeDtypeStruct((S, D_MODEL), BF16),
                   jax.ShapeDtypeStruct((S, D_FF), BF16), jax.ShapeDtypeStruct((S, D_FF), BF16),
                   jax.ShapeDtypeStruct((S, D_MODEL), F32), jax.ShapeDtypeStruct((8, D_MODEL), F32)],
        compiler_params=_params("arbitrary"),
    )(act, gl, vdgl, z1, h1b, p, tgt, w_down, w_pg, b_pg, w_pp, ln1_g, ln1_b, ln2_g, ln2_b)


def _weight_grad(a_list, b_list, name, layout, ts=512, comm=None, b_window=None, halves=False):
    S = a_list[0].shape[0]
    ms = [a.shape[1] for a in a_list]
    M, nb = sum(ms), len(b_list)
    win, Nb = b_window if b_window else (0, b_list[0].shape[1])
    ts = min(ts, S)
    nk = S // ts
    per_b = N_DEV // nb
    na = len(a_list)

    n_out = 2 if halves else 1
    assert layout == "cols" or not halves

    def body(*refs):
        a_refs, b_refs, o_refs, acc_ref = refs[:na], refs[na:na + nb], refs[na + nb:na + nb + n_out], refs[-1]
        o_ref = o_refs[0]
        j, k = pl.program_id(0), pl.program_id(1)

        @pl.when(k == 0)
        def _():
            acc_ref[...] = jnp.zeros_like(acc_ref)

        for jj in range(nb):
            @pl.when(j == jj)
            def _():
                b = b_refs[jj][...].astype(BF16)
                off = 0
                for a_ref, m in zip(a_refs, ms):
                    acc_ref[off:off + m, :] += _dot_tn(a_ref[...].astype(BF16), b)
                    off += m

        @pl.when(k == nk - 1)
        def _():
            for d in range(per_b):
                if layout == "rows":
                    o_ref[d] = acc_ref[d * (M // N_DEV):(d + 1) * (M // N_DEV), :].astype(BF16)
                elif layout == "cols" and halves:
                    for o_half, r0 in zip(o_refs, (0, M // 2)):
                        o_half[d] = acc_ref[r0:r0 + M // 2, d * (Nb // per_b):(d + 1) * (Nb // per_b)].astype(BF16)
                elif layout == "cols":
                    o_ref[d] = acc_ref[:, d * (Nb // per_b):(d + 1) * (Nb // per_b)].astype(BF16)
                else:
                    o_ref[d] = acc_ref[:, d * (Nb // per_b):(d + 1) * (Nb // per_b)].T.astype(BF16)

    def b_index(jj):
        return lambda j, k: (jnp.where(j == jj, k, jnp.where(j < jj, 0, nk - 1)), win)

    if layout == "rows":
        assert nb == 1
        blk = (N_DEV, M // N_DEV, Nb)
    elif layout == "cols":
        blk = (per_b, M // n_out, Nb // per_b)
    else:
        blk = (per_b, Nb // per_b, M)
    res, comm_res = _pcall(
        body, (*a_list, *b_list), name=name, grid=(nb, nk), sem="arbitrary", comm=comm, step_axis=1,
        in_specs=[pl.BlockSpec((ts, m), lambda j, k: (k, 0)) for m in ms]
        + [pl.BlockSpec((ts, Nb), b_index(jj)) for jj in range(nb)],
        out_specs=[pl.BlockSpec(blk, lambda j, k: (j, 0, 0))] * n_out,
        out_shape=[jax.ShapeDtypeStruct((N_DEV,) + blk[1:], BF16)] * n_out,
        scratch_shapes=[pltpu.VMEM((M, Nb), F32)])
    res = res if halves else res[0]
    return (res, comm_res) if comm is not None else res


def _pg_pp_grad(h1b, dpreb, p, dppb, ts=1024):
    S = h1b.shape[0]
    ts = min(ts, S)
    nk = S // ts
    rows, cols = D_MODEL // N_DEV, D_MODEL // N_DEV

    def body(h_ref, dpre_ref, p_ref, dpp_ref, gpg_ref, gpp_ref, acc_pg, acc_pp):
        k = pl.program_id(0)

        @pl.when(k == 0)
        def _():
            acc_pg[...] = jnp.zeros_like(acc_pg)
            acc_pp[...] = jnp.zeros_like(acc_pp)

        acc_pg[...] += _dot_tn(h_ref[...], dpre_ref[...])
        acc_pp[...] += _dot_tn(p_ref[...].astype(BF16), dpp_ref[...])

        @pl.when(k == nk - 1)
        def _():
            for d in range(N_DEV):
                gpg_ref[d] = acc_pg[d * rows:(d + 1) * rows, :].astype(BF16)
                gpp_ref[d] = acc_pp[:, d * cols:(d + 1) * cols].astype(BF16)

    return pl.pallas_call(
        body, name="pg_pp_grad", grid=(nk,),
        in_specs=[_rows(ts, D_MODEL), _rows(ts, D_MODEL), _rows(ts, PLE_DIM), _rows(ts, D_MODEL)],
        out_specs=[_acc((N_DEV, rows, D_MODEL)), _acc((N_DEV, PLE_DIM, cols))],
        out_shape=[jax.ShapeDtypeStruct((N_DEV, rows, D_MODEL), BF16), jax.ShapeDtypeStruct((N_DEV, PLE_DIM, cols), BF16)],
        scratch_shapes=[pltpu.VMEM((D_MODEL, D_MODEL), F32), pltpu.VMEM((PLE_DIM, D_MODEL), F32)],
        compiler_params=_params("arbitrary"))(h1b, dpreb, p, dppb)


def _up_bwd(dgc, gate, dval, dh1p, z1, w_up_top, w_up_bot, fcw, w_out, ln1_g, comm=None):
    S = z1.shape[0]
    tb = min(256, S)
    t16 = tb // 16
    n16 = S // 16
    nblk, kh, wblk = w_up_top.shape
    half = nblk // 2
    nsteps = S // tb

    def body(dgc_ref, dgn_ref, gc_ref, dval_ref, dh1p_ref, z1_ref, wt_hbm, wb_hbm, fcw_ref, wo_ref, g1_ref,
             dgate_ref, dz1_ref, dz1b_ref, datt_ref, drec_ref, accf_ref, accd_ref, wu_s, wu_sems):
        i = pl.program_id(0)

        @pl.when(i == 0)
        def _():
            accf_ref[...] = jnp.zeros_like(accf_ref)
            accd_ref[...] = jnp.zeros_like(accd_ref)
            _load_row_halves(wt_hbm, wb_hbm, wu_s, wu_sems)

        dg = dgc_ref[...].astype(F32)
        nxt = jnp.where(i < nsteps - 1, dgn_ref[...].astype(F32)[0:8], 0.0)
        w = _w_rows(fcw_ref)
        up1, up2 = _shift_up(dg, nxt, 1), _shift_up(dg, nxt, 2)
        dgate = (w[2] * dg + w[1] * up1 + w[0] * up2).astype(BF16)
        dgate_ref[...] = dgate
        gate = gc_ref[...].astype(F32)
        _put_rows(accf_ref, [_row_sum(up2 * gate), _row_sum(up1 * gate), _row_sum(dg * gate), _row_sum(dg)])

        dh1 = dh1p_ref[...]
        for j in range(nblk):
            src = dgate if j < half else dval_ref[...]
            jj = j % half
            dh1 = dh1 + _dot_nt(src[:, jj * wblk:(jj + 1) * wblk], wu_s[j])
        xhat1, rstd1 = _ln_stats(z1_ref[...])
        dz1 = _ln_bwd(dh1, xhat1, rstd1, g1_ref[...])
        dz1_ref[...] = dz1
        dz1b = dz1.astype(BF16)
        dz1b_ref[...] = dz1b
        dcat = _dot_nt(dz1b, wo_ref[...])
        datt_ref[...] = dcat[:, :D_ATT].astype(BF16)
        drec_ref[...] = dcat[:, D_ATT:]
        _put_rows(accd_ref, [_row_sum(dh1 * xhat1), _row_sum(dh1)])

    next16 = pl.BlockSpec((16, D_FF), lambda i: (jnp.minimum((i + 1) * t16, n16 - 1), 0))
    return _pcall(
        body, (dgc, dgc, gate, dval, dh1p, z1, w_up_top, w_up_bot, fcw, w_out, ln1_g), name="up_bwd",
        grid=(nsteps,), sem="arbitrary", comm=comm,
        in_specs=[_rows(tb, D_FF), next16, _rows(tb, D_FF), _rows(tb, D_FF), _rows(tb, D_MODEL),
                  _rows(tb, D_MODEL), _ANY, _ANY, _resident((3, D_FF)),
                  _resident((D_MODEL, D_MODEL)), _resident((1, D_MODEL))],
        scratch_shapes=[pltpu.VMEM((nblk, 2 * kh, wblk), BF16), pltpu.SemaphoreType.DMA((2,))],
        out_specs=[_rows(tb, D_FF), _rows(tb, D_MODEL), _rows(tb, D_MODEL), _rows(tb, D_ATT), _rows(tb, D_RNN),
                   _acc((8, D_FF)), _acc((8, D_MODEL))],
        out_shape=[jax.ShapeDtypeStruct((S, D_FF), BF16), jax.ShapeDtypeStruct((S, D_MODEL), F32),
                   jax.ShapeDtypeStruct((S, D_MODEL), BF16), jax.ShapeDtypeStruct((S, D_ATT), BF16),
                   jax.ShapeDtypeStruct((S, D_RNN), F32), jax.ShapeDtypeStruct((8, D_FF), F32),
                   jax.ShapeDtypeStruct((8, D_MODEL), F32)])


def _attn_bwd(q, k, v, lse, do, sinks, comm=None):
    S = q.shape[0]
    grp = N_HEADS // N_KV
    nq = min(ATT_STEP, S // QBLK)

    def body(sink_ref, q_ref, kc_ref, kp_ref, vc_ref, vp_ref, do_ref, lse_ref, dq_ref, dkc_ref, dkp_ref, dvc_ref, dvp_ref,
             ds_ref):
        i = pl.program_id(0)

        @pl.when(i == 0)
        def _():
            ds_ref[...] = jnp.zeros_like(ds_ref)

        row8 = lax.broadcasted_iota(jnp.int32, (8, 128), 0)
        lane8 = lax.broadcasted_iota(jnp.int32, (8, 128), 1)
        dsink = jnp.zeros((8, 128), F32)
        kall = jnp.concatenate([kp_ref[...], kc_ref[...]], axis=0)
        vall = jnp.concatenate([vp_ref[...], vc_ref[...]], axis=0)
        dk_t = [jnp.zeros((D_KV, QBLK), F32) for _ in range(nq + 1)]
        dv_t = [jnp.zeros((D_KV, QBLK), F32) for _ in range(nq + 1)]
        for b in range(nq):
            valid = _band_mask(i * nq + b)
            rows = slice(b * QBLK, (b + 1) * QBLK)
            keys = slice(b * QBLK, (b + 2) * QBLK)
            qv, dov = q_ref[rows, :], do_ref[rows, :]
            dqs, dks, dvs = [], [], []
            for g in range(N_KV):
                kcat = kall[keys, g * HEAD_DIM:(g + 1) * HEAD_DIM]
                vcat = vall[keys, g * HEAD_DIM:(g + 1) * HEAD_DIM]
                q4, do4 = _stack_heads(qv, g), _stack_heads(dov, g)
                s = jnp.where(valid, _dot_nt(q4, kcat), -1e30)
                lse = lse_ref[(b * N_KV + g) * GROUP * QBLK:(b * N_KV + g + 1) * GROUP * QBLK, :]
                p = jnp.exp(s - lse)
                p_sink = jnp.exp(_sink_column(sink_ref, g) - lse)
                dp = _dot_nt(do4, vcat)
                delta = jnp.sum(p * dp, axis=1, keepdims=True)
                dsc = (p * (dp - delta)).astype(BF16)
                dqs += _unstack_heads(_dot(dsc, kcat) * (HEAD_DIM ** -0.5))
                dks.append(_dot_tn(q4, dsc))
                dvs.append(_dot_tn(do4, p.astype(BF16)))
                for hh, part in enumerate(_unstack_heads(-p_sink * delta)):
                    here = (row8 == 0) & (lane8 == g * grp + hh)
                    dsink = dsink + jnp.where(here, jnp.sum(part, axis=0, keepdims=True), 0.0)
            dq_ref[rows, :] = jnp.concatenate(dqs, axis=1).astype(BF16)
            dk2, dv2 = jnp.concatenate(dks, axis=0), jnp.concatenate(dvs, axis=0)
            dk_t[b], dk_t[b + 1] = dk_t[b] + dk2[:, :QBLK], dk_t[b + 1] + dk2[:, QBLK:]
            dv_t[b], dv_t[b + 1] = dv_t[b] + dv2[:, :QBLK], dv_t[b + 1] + dv2[:, QBLK:]
        dkp_ref[...] = dk_t[0].T
        dvp_ref[...] = dv_t[0].T
        for b in range(nq):
            dkc_ref[b * QBLK:(b + 1) * QBLK, :] = dk_t[b + 1].T
            dvc_ref[b * QBLK:(b + 1) * QBLK, :] = dv_t[b + 1].T
        ds_ref[...] += dsink

    nsteps = S // (nq * QBLK)
    cur = jax.ShapeDtypeStruct((S, D_KV), F32)
    prev = jax.ShapeDtypeStruct((nsteps * QBLK, D_KV), F32)
    big = _rows(nq * QBLK, D_ATT)
    return _pcall(
        body, (sinks, q, k, k, v, v, do, lse), name="attn_bwd", grid=(nsteps,), sem="arbitrary", comm=comm,
        in_specs=[pl.BlockSpec(memory_space=pltpu.SMEM), big] + _attn_specs(nq) + [big, _rows(nq * N_HEADS * QBLK, 1)],
        out_specs=[big, _rows(nq * QBLK, D_KV), _rows(QBLK, D_KV), _rows(nq * QBLK, D_KV), _rows(QBLK, D_KV),
                   _acc((8, 128))],
        out_shape=[jax.ShapeDtypeStruct((S, D_ATT), BF16), cur, prev, cur, prev, jax.ShapeDtypeStruct((8, 128), F32)])


def _rnn_bwd(xr, gr, h, kept, drec, conv_w, wa, wx, lam, comm=None):
    S = xr.shape[0]
    tb = min(512, S)
    t8 = tb // 8
    nsteps = S // tb

    def body(xr_ref, xp_ref, gr_ref, h_ref, hp_ref, xc_ref, r_ref, ig_ref, a_ref, f_ref, drec_ref, cw_ref, wa_ref, wx_ref,
             lam_ref, dxr_ref, dgr_ref, gwa_ref, gwx_ref, acc_ref, carry_s, dxc_halo_s, d_s, gwa_s, gwx_s):
        i = pl.program_id(0)
        blk = nsteps - 1 - i

        @pl.when(i == 0)
        def _():
            gwa_s[...] = jnp.zeros_like(gwa_s)
            gwx_s[...] = jnp.zeros_like(gwx_s)
            acc_ref[...] = jnp.zeros_like(acc_ref)
            carry_s[...] = jnp.zeros_like(carry_s)
            dxc_halo_s[...] = jnp.zeros_like(dxc_halo_s)

        x = xr_ref[...]
        xhalo = jnp.where(blk > 0, xp_ref[...], 0.0)
        cw = _w_rows(cw_ref)
        xs = [_shift_down(x, xhalo, 3), _shift_down(x, xhalo, 2), _shift_down(x, xhalo, 1), x]
        xc, r, ig, a, f = xc_ref[...], r_ref[...], ig_ref[...], a_ref[...], f_ref[...]
        sp = _softplus_neg(lam_ref[...])
        hcur = h_ref[...]
        hprev = _shift_down(hcur, jnp.where(blk > 0, hp_ref[...], 0.0), 1)
        gl, dgl = _gelu_and_grad(gr_ref[...])
        drec = drec_ref[...]
        dgr_ref[...] = (drec * hcur * dgl).astype(BF16)
        d_s[...] = drec * gl
        row8 = lax.broadcasted_iota(jnp.int32, (8, D_RNN), 0)

        def tile(t, c):
            o = pl.multiple_of((t8 - 1 - t) * 8, 8)
            a8 = a_ref[pl.ds(o, 8), :]
            dt = d_s[pl.ds(o, 8), :]
            at = jnp.where(row8 == 7, 1.0, pltpu.roll(a8, 7, 0))
            for s in (1, 2, 4):
                keep = row8 < 8 - s
                a_sh = jnp.where(keep, pltpu.roll(at, 8 - s, 0), 1.0)
                d_sh = jnp.where(keep, pltpu.roll(dt, 8 - s, 0), 0.0)
                dt = at * d_sh + dt
                at = at * a_sh
            lt = at * c + dt
            d_s[pl.ds(o, 8), :] = lt
            return _row_sum(jnp.where(row8 == 0, a8 * lt, 0.0))

        carry_s[0:1, :] = lax.fori_loop(0, t8, tile, carry_s[0:1, :], unroll=2)
        lmb = d_s[...]
        a2 = a * a
        dla = lmb * hprev * a - lmb * ig * xc * (a2 / f)
        di = lmb * f * xc
        dr = dla * (-LRU_C) * sp
        dpa = dr * r * (1.0 - r)
        dpx = di * ig * (1.0 - ig)
        dpab = dpa.astype(BF16)
        dpxb = dpx.astype(BF16)
        xcb = xc.astype(BF16)
        gwa_s[...] += _dot_tn(xcb, dpab)
        gwx_s[...] += _dot_tn(xcb, dpxb)

        @pl.when(i == nsteps - 1)
        def _():
            for dense, out in ((gwa_s[...], gwa_ref), (gwx_s[...], gwx_ref)):
                for b in range(RNN_BLOCKS):
                    rows = slice(b * HEAD_DIM, (b + 1) * HEAD_DIM)
                    out[rows, :] = dense[rows, b * HEAD_DIM:(b + 1) * HEAD_DIM]

        dxc = lmb * f * ig + _dot_nt(dpab, wa_ref[...]) + _dot_nt(dpxb, wx_ref[...])
        nxt = dxc_halo_s[...]
        dxr = cw[3] * dxc
        for s in (1, 2, 3):
            dxr = dxr + cw[3 - s] * _shift_up(dxc, nxt, s)
        dxr_ref[...] = dxr.astype(BF16)
        dxc_halo_s[...] = dxc[:8]
        dlam = _row_sum(dla * (-LRU_C) * r) * (-1.0 / (1.0 + jnp.exp(lam_ref[...])))
        _put_rows(acc_ref, [_row_sum(dxc * xs[0]), _row_sum(dxc * xs[1]), _row_sum(dxc * xs[2]), _row_sum(dxc * xs[3]),
                            _row_sum(dxc), _row_sum(dpa), _row_sum(dpx), dlam])

    rev = lambda i: (nsteps - 1 - i, 0)
    prev8 = lambda i: (jnp.maximum((nsteps - 1 - i) * t8 - 1, 0), 0)
    blkspec = pl.BlockSpec((tb, D_RNN), rev)
    halo8 = pl.BlockSpec((8, D_RNN), prev8)
    vec = _resident((1, D_RNN))
    return _pcall(
        body, (xr, xr, gr, h, h, *kept, drec, conv_w, wa, wx, lam), name="rnn_bwd", grid=(nsteps,),
        sem="arbitrary", comm=comm,
        in_specs=[blkspec, halo8, blkspec, blkspec, halo8] + [blkspec] * 6
        + [_resident((4, D_RNN)), _resident((D_RNN, D_RNN)), _resident((D_RNN, D_RNN)), vec],
        out_specs=[blkspec, blkspec, _acc((D_RNN, HEAD_DIM)), _acc((D_RNN, HEAD_DIM)), _acc((8, D_RNN))],
        out_shape=[jax.ShapeDtypeStruct((S, D_RNN), BF16), jax.ShapeDtypeStruct((S, D_RNN), BF16),
                   jax.ShapeDtypeStruct((D_RNN, HEAD_DIM), F32), jax.ShapeDtypeStruct((D_RNN, HEAD_DIM), F32),
                   jax.ShapeDtypeStruct((8, D_RNN), F32)],
        scratch_shapes=[pltpu.VMEM((8, D_RNN), F32), pltpu.VMEM((8, D_RNN), F32), pltpu.VMEM((tb, D_RNN), F32),
                        pltpu.VMEM((D_RNN, D_RNN), F32), pltpu.VMEM((D_RNN, D_RNN), F32)])


def _in_bwd(dq, dkc, dkp, dvc, dvp, dxr, dgr, dz1, w_in, comm=None):
    S = dz1.shape[0]
    tb = min(ATT_STEP * QBLK, S)
    nsteps = S // tb
    ring = 3

    def body(dq_ref, dkc_ref, dkn_ref, dvc_ref, dvn_ref, dxr_ref, dgr_ref, dz1_hbm, w_ref, dkv_ref, dx_ref,
             ring_s, ring_sems):
        i = pl.program_id(0)
        last = i == nsteps - 1

        def fetch(step):
            slot = step % ring
            rows = pl.ds(pl.multiple_of(step * tb, tb), tb)
            return pltpu.make_async_copy(dz1_hbm.at[rows, :], ring_s.at[slot], ring_sems.at[slot])

        @pl.when(i == 0)
        def _():
            for step in range(min(ring - 1, nsteps)):
                fetch(step).start()

        @pl.when(i + ring - 1 < nsteps)
        def _():
            fetch(i + ring - 1).start()

        def total(cur_ref, next_ref):
            nxt = jnp.where(last, 0.0, next_ref[...])
            tail = cur_ref[tb - QBLK:, :] + nxt
            return jnp.concatenate([cur_ref[:tb - QBLK, :], tail], axis=0) if tb > QBLK else tail

        dkv = jnp.concatenate([total(dkc_ref, dkn_ref), total(dvc_ref, dvn_ref)], axis=1).astype(BF16)
        dkv_ref[...] = dkv
        du = jnp.concatenate([dq_ref[...], dkv, dxr_ref[...], dgr_ref[...]], axis=1)
        prod = _dot(du, w_ref[...])
        fetch(i).wait()
        dx_ref[...] = ALPHA * ring_s[i % ring] + prod

    nextp = pl.BlockSpec((QBLK, D_KV), lambda i: (jnp.minimum(i + 1, nsteps - 1), 0))
    return _pcall(
        body, (dq, dkc, dkp, dvc, dvp, dxr, dgr, dz1, w_in), name="in_bwd", grid=(nsteps,), sem="arbitrary", comm=comm,
        in_specs=[_rows(tb, D_ATT), _rows(tb, D_KV), nextp, _rows(tb, D_KV), nextp,
                  _rows(tb, D_RNN), _rows(tb, D_RNN), _ANY, _resident((D_IN, D_MODEL))],
        out_specs=[_rows(tb, 2 * D_KV), _rows(tb, D_MODEL)],
        out_shape=[jax.ShapeDtypeStruct((S, 2 * D_KV), BF16), jax.ShapeDtypeStruct((S, D_MODEL), F32)],
        scratch_shapes=[pltpu.VMEM((ring, tb, D_MODEL), F32), pltpu.SemaphoreType.DMA((ring,))])


def _block_diag(w):
    eye = jnp.eye(RNN_BLOCKS, dtype=w.dtype)
    return (w[:, :, None, :] * eye[:, None, :, None]).reshape(D_RNN, D_RNN).astype(BF16)


def _adamw(w, g, m, v):
    m = ADAM_B1 * m + (1.0 - ADAM_B1) * g
    v = ADAM_B2 * v + (1.0 - ADAM_B2) * (g * g)
    m_hat = m / (1.0 - ADAM_B1 ** ADAM_STEP)
    v_hat = v / (1.0 - ADAM_B2 ** ADAM_STEP)
    delta = -ADAM_LR * (m_hat / (jnp.sqrt(v_hat) + ADAM_EPS) + ADAM_WD * w)
    return delta, m, v


def _sum_adamw(parts, w, m, v, name):
    parts = parts if isinstance(parts, (list, tuple)) else [parts]
    R, C = w.shape
    rb = R if R <= 256 else (256 if parts[0].shape[1] % 256 == 0 else 128)
    per = parts[0].shape[1] // rb
    assert R % rb == 0 and parts[0].shape[1] % rb == 0
    n = len(parts)

    def body(*refs):
        p_refs = refs[:n]
        w_ref, m_ref, v_ref, g_out, d_out, m_out, v_out = refs[n:]
        which = pl.program_id(0) // per

        def total(p_ref):
            g = p_ref[0].astype(F32)
            for d in range(1, N_DEV):
                g = g + p_ref[d].astype(F32)
            return g

        g = total(p_refs[0])
        for j in range(1, n):
            g = jnp.where(which == j, total(p_refs[j]), g)
        delta, mn, vn = _adamw(w_ref[...], g, m_ref[...], v_ref[...])
        g_out[...] = g
        d_out[...] = delta
        m_out[...] = mn
        v_out[...] = vn

    def part_spec(j):
        return pl.BlockSpec((N_DEV, rb, C), lambda i: (0, jnp.clip(i - j * per, 0, per - 1), 0))

    blk = _rows(rb, C)
    out = jax.ShapeDtypeStruct((R, C), F32)
    return pl.pallas_call(
        body, name=name, grid=(R // rb,),
        in_specs=[part_spec(j) for j in range(n)] + [blk, blk, blk],
        out_specs=[blk, blk, blk, blk], out_shape=[out, out, out, out],
        compiler_params=_params("parallel"),
    )(*parts, w, m, v)


def _sum_adamw_group(items, name):
    n = len(items)

    def body(*refs):
        ins, outs = refs[:4 * n], refs[4 * n:]
        for j in range(n):
            p_ref, w_ref, m_ref, v_ref = ins[4 * j:4 * j + 4]
            g = p_ref[0].astype(F32)
            for d in range(1, N_DEV):
                g = g + p_ref[d].astype(F32)
            delta, mn, vn = _adamw(w_ref[...], g, m_ref[...], v_ref[...])
            for o_ref, val in zip(outs[4 * j:4 * j + 4], (g, delta, mn, vn)):
                o_ref[...] = val

    out_shape = [jax.ShapeDtypeStruct(w.shape, F32) for _, w, _, _ in items for _ in range(4)]
    res = pl.pallas_call(body, name=name, out_shape=out_shape, compiler_params=_params())(
        *[a for item in items for a in item])
    return [res[4 * j:4 * j + 4] for j in range(n)]


_SMALL = [("attn_sinks", "s", 0, 1, None), ("rnn_conv_w", "r", 0, 4, "cols"), ("rnn_conv_b", "r", 4, 1, None),
          ("gate_a_w", "a", 0, D_RNN, None), ("gate_a_b", "r", 5, 1, None), ("gate_x_w", "x", 0, D_RNN, None),
          ("gate_x_b", "r", 6, 1, None), ("lru_lambda", "r", 7, 1, None), ("ln1_g", "d", 0, 1, None),
          ("ln1_b", "d", 1, 1, None), ("ffn_conv_w", "f", 0, 3, "cols"), ("ffn_conv_b", "f", 3, 1, None),
          ("ple_gate_b", "t", 2, 1, None), ("ln2_g", "t", 0, 1, None), ("ln2_b", "t", 1, 1, None)]
_LOSS_ROW = 3


_ACC_COLS = {"t": (0, D_MODEL), "f": (D_MODEL, D_FF), "d": (D_MODEL + D_FF, D_MODEL), "s": (2 * D_MODEL + D_FF, 128),
             "r": (2 * D_MODEL + D_FF + 128, D_RNN)}
_ACC_WIDTH = 2 * D_MODEL + D_FF + 128 + D_RNN


def _small_update(rows_all, gates_all, params):
    flat = [arr for triple in params for arr in triple]
    n_par = len(_SMALL)

    def body(*refs):
        rows_ref, gates_ref = refs[:2]
        p_refs = refs[2:2 + 3 * n_par]
        loss_ref = refs[2 + 3 * n_par]
        o_refs = refs[3 + 3 * n_par:3 + 7 * n_par]
        rows_s, tmp_r, tmp_f = refs[3 + 7 * n_par:]
        me = _dev_index(*_place())
        rows_sum, gates_sum = rows_ref[0], gates_ref[0]
        for d in range(1, N_DEV):
            rows_sum = rows_sum + rows_ref[d]
            gates_sum = gates_sum + gates_ref[d]
        rows_s[...] = rows_sum
        t0 = _ACC_COLS["t"][0]
        loss_ref[...] = rows_s[_LOSS_ROW:_LOSS_ROW + 1, t0:t0 + 128]
        for i, (name, key, row, rows, how) in enumerate(_SMALL):
            w_ref, m_ref, v_ref = p_refs[3 * i:3 * i + 3]
            g_out, d_out, m_out, v_out = o_refs[4 * i:4 * i + 4]
            if key == "a":
                g = gates_sum[:, :HEAD_DIM]
            elif key == "x":
                g = gates_sum[:, HEAD_DIM:]
            elif how == "cols":
                c0, width = _ACC_COLS[key]
                full = rows_s[:, c0:c0 + width]
                shard = width // N_DEV
                mine = full[:, :shard]
                for d in range(1, N_DEV):
                    mine = jnp.where(me == d, full[:, d * shard:(d + 1) * shard], mine)
                tmp = tmp_r if key == "r" else tmp_f
                tmp[...] = mine
                g = tmp[row:row + rows, :]
            else:
                c0, width = _ACC_COLS[key]
                g = rows_s[row:row + rows, c0:c0 + width][:, :w_ref.shape[1]]
            delta, mn, vn = _adamw(w_ref[...], g, m_ref[...], v_ref[...])
            g_out[...] = g
            d_out[...] = delta
            m_out[...] = mn
            v_out[...] = vn

    outs = [jax.ShapeDtypeStruct((1, 128), F32)]
    for w, _, _ in params:
        outs += [jax.ShapeDtypeStruct(w.shape, F32)] * 4
    scratch = [pltpu.VMEM((8, _ACC_WIDTH), F32), pltpu.VMEM((8, D_RNN // N_DEV), F32), pltpu.VMEM((8, D_FF // N_DEV), F32)]
    res = pl.pallas_call(body, name="small_update", out_shape=outs, scratch_shapes=scratch)(rows_all, gates_all, *flat)
    return res[0], [res[1 + 4 * i:5 + 4 * i] for i in range(n_par)]


def kernel(x, p, w_in, attn_sinks, rnn_conv_w, rnn_conv_b, gate_a_w, gate_a_b, gate_x_w, gate_x_b, lru_lambda, w_out, ln1_g, ln1_b, w_ffn_up, ffn_conv_w, ffn_conv_b, w_ffn_down, ple_gate_w, ple_gate_b, ple_proj, ln2_g, ln2_b, loss_target, m_w_in, m_attn_sinks, m_rnn_conv_w, m_rnn_conv_b, m_gate_a_w, m_gate_a_b, m_gate_x_w, m_gate_x_b, m_lru_lambda, m_w_out, m_ln1_g, m_ln1_b, m_w_ffn_up, m_ffn_conv_w, m_ffn_conv_b, m_w_ffn_down, m_ple_gate_w, m_ple_gate_b, m_ple_proj, m_ln2_g, m_ln2_b, v_w_in, v_attn_sinks, v_rnn_conv_w, v_rnn_conv_b, v_gate_a_w, v_gate_a_b, v_gate_x_w, v_gate_x_b, v_lru_lambda, v_w_out, v_ln1_g, v_ln1_b, v_w_ffn_up, v_ffn_conv_w, v_ffn_conv_b, v_w_ffn_down, v_ple_gate_w, v_ple_gate_b, v_ple_proj, v_ln2_g, v_ln2_b):
    from_col_blocks = lambda g: g.transpose(1, 0, 2).reshape(g.shape[1], N_DEV * g.shape[2])

    xs, ps, tgt, sinks = x[0], p[0, 0], loss_target[0], attn_sinks[0]
    wa, wx = _block_diag(gate_a_w[0]), _block_diag(gate_x_w[0])

    conv_cols = jnp.concatenate([rnn_conv_w[0].reshape(1, -1), ffn_conv_w[0].reshape(1, -1)], axis=1)
    n_rc, n_fc = 4 * D_RNN // N_DEV, 3 * D_FF // N_DEV
    ((g_in,),) = _comm_call([_Gather([w_in[0].T.astype(BF16)])], "gather_w_in")
    w_in_full = g_in.reshape(D_IN, D_MODEL)

    (q, k, v, xr, gr), _ = _in_proj(xs, w_in_full)
    w_up_shard = w_ffn_up[0].astype(BF16)
    (att, lse), (g_out, w_up_top, g_conv) = _attn_fwd(
        q, k, v, sinks,
        comm=_Multi([_Gather([w_out[0].astype(BF16), w_up_shard[:D_MODEL // 2]]),
                     _Bcast([jnp.broadcast_to(conv_cols, (8, n_rc + n_fc))])]))
    rcw = from_col_blocks(g_conv[:, 0, :n_rc].reshape(N_DEV, 4, D_RNN // N_DEV))
    fcw = from_col_blocks(g_conv[:, 0, n_rc:].reshape(N_DEV, 3, D_FF // N_DEV))
    (rec, h, *kept), (w_up_bot,) = _rnn_fwd(xr, gr, rcw, rnn_conv_b, wa, wx, gate_a_b, gate_x_b, lru_lambda,
                                            comm=_Gather([w_up_shard[D_MODEL // 2:]]))
    w_out_full = g_out.reshape(D_MODEL, D_MODEL)
    (z1, h1b, gate, act, gl, vdgl), (g_down, g_pg, g_pp) = _mix_ln1_up(
        xs, att, rec, w_out_full, ln1_g, ln1_b, w_up_top, w_up_bot, fcw, ffn_conv_b,
        comm=_Gather([w_ffn_down[0].astype(BF16), ple_gate_w[0].astype(BF16), ple_proj[0].astype(BF16)]))
    dz2b, dpreb, dppb, dgc, dval, dh1p, acc_t = _tail(
        act, gl, vdgl, z1, h1b, ps, tgt, g_down.reshape(D_FF, D_MODEL), g_pg.reshape(D_MODEL, D_MODEL), ple_gate_b,
        from_col_blocks(g_pp), ln1_g, ln1_b, ln2_g, ln2_b)

    gd_down = _weight_grad([dz2b], [act], "down_grad", "rows_t", ts=1024)
    gd_pg, gd_pp = _pg_pp_grad(h1b, dpreb, ps, dppb)
    (dgate, dz1, dz1b, datt, drec, acc_f, acc_d), (r_down, r_pg, r_pp) = _up_bwd(
        dgc, gate, dval, dh1p, z1, w_up_top, w_up_bot, fcw, w_out_full, ln1_g, comm=_Exchange([gd_down, gd_pg, gd_pp]))
    gd_up_top, gd_up_bot = _weight_grad([h1b], [dgate, dval], "up_grad", "cols", halves=True)
    gd_out = _weight_grad([att, rec], [dz1b], "out_grad", "rows", ts=1024)
    (dq, dkc, dkp, dvc, dvp, acc_s), (r_up_top,) = _attn_bwd(q, k, v, lse, datt, sinks, comm=_Exchange([gd_up_top]))
    early = jnp.concatenate([acc_t, acc_f, acc_d], axis=1)
    (dxr, dgr, g_wa, g_wx, acc_r), (r_up_bot, r_out, early_all) = _rnn_bwd(
        xr, gr, h, kept, drec, rcw, wa, wx, lru_lambda, comm=_Multi([_Exchange([gd_up_bot, gd_out]), _Bcast([early])]))
    (dkv, dx), _ = _in_bwd(dq, dkc, dkp, dvc, dvp, dxr, dgr, dz1, w_in_full)
    du_parts = [dq, dkv, dxr, dgr]
    lanes = D_RNN // 128
    late = jnp.concatenate([g_wa, g_wx], axis=1)
    late = jnp.concatenate([late, acc_s, acc_r.reshape(8, lanes, 128).transpose(1, 0, 2).reshape(8 * lanes, 128)], axis=0)
    width = D_MODEL // IN_GRAD_PARTS
    comm, r_parts = _Gather([late]), []
    for part in range(IN_GRAD_PARTS):
        gd_part, got = _weight_grad(du_parts, [xs], f"in_grad_{part}", "rows", ts=1024, b_window=(part, width), comm=comm)
        if part == 0:
            (late_all,) = got
        else:
            r_parts += got
        comm = _Exchange([gd_part])
    r_parts += _comm_call([comm], "exchange_w_in")[0]
    r_in = jnp.concatenate(r_parts, axis=2)
    acc_r_all = late_all[:, D_RNN + 8:].reshape(N_DEV, lanes, 8, 128).transpose(0, 2, 1, 3).reshape(N_DEV, 8, D_RNN)
    small_parts = (jnp.concatenate([early_all, late_all[:, D_RNN:D_RNN + 8], acc_r_all], axis=2),
                   late_all[:, :D_RNN])

    outs = {}
    for name, parts, w, m, v in [("w_ffn_up", [r_up_top, r_up_bot], w_ffn_up, m_w_ffn_up, v_w_ffn_up),
                                 ("w_ffn_down", r_down, w_ffn_down, m_w_ffn_down, v_w_ffn_down)]:
        res = _sum_adamw(parts, w[0], m[0], v[0], "adamw_" + name)
        outs[name] = [r[None] for r in res]
    small_shards = [("w_out", r_out, w_out, m_w_out, v_w_out), ("ple_gate_w", r_pg, ple_gate_w, m_ple_gate_w, v_ple_gate_w),
                    ("ple_proj", r_pp, ple_proj, m_ple_proj, v_ple_proj)]
    group = _sum_adamw_group([(parts, w[0], m[0], v[0]) for _, parts, w, m, v in small_shards]
                             + [(r_in, w_in[0].T, m_w_in[0].T, v_w_in[0].T)], "adamw_small_shards")
    for (name, *_), res in zip(small_shards, group):
        outs[name] = [r[None] for r in res]
    outs["w_in"] = [r.T[None] for r in group[-1]]

    given = dict(attn_sinks=(attn_sinks, m_attn_sinks, v_attn_sinks), rnn_conv_w=(rnn_conv_w, m_rnn_conv_w, v_rnn_conv_w),
                 rnn_conv_b=(rnn_conv_b, m_rnn_conv_b, v_rnn_conv_b), gate_a_w=(gate_a_w, m_gate_a_w, v_gate_a_w),
                 gate_a_b=(gate_a_b, m_gate_a_b, v_gate_a_b), gate_x_w=(gate_x_w, m_gate_x_w, v_gate_x_w),
                 gate_x_b=(gate_x_b, m_gate_x_b, v_gate_x_b), lru_lambda=(lru_lambda, m_lru_lambda, v_lru_lambda),
                 ln1_g=(ln1_g, m_ln1_g, v_ln1_g), ln1_b=(ln1_b, m_ln1_b, v_ln1_b),
                 ffn_conv_w=(ffn_conv_w, m_ffn_conv_w, v_ffn_conv_w), ffn_conv_b=(ffn_conv_b, m_ffn_conv_b, v_ffn_conv_b),
                 ple_gate_b=(ple_gate_b, m_ple_gate_b, v_ple_gate_b), ln2_g=(ln2_g, m_ln2_g, v_ln2_g),
                 ln2_b=(ln2_b, m_ln2_b, v_ln2_b))
    as_2d = lambda a: a.reshape(-1, a.shape[-1])
    loss_row, small_res = _small_update(*small_parts, [tuple(as_2d(a) for a in given[n]) for n, *_ in _SMALL])
    loss = loss_row[0, 0]
    for (n, *_), res in zip(_SMALL, small_res):
        outs[n] = [r.reshape(given[n][0].shape) for r in res]

    order = ["w_in", "attn_sinks", "rnn_conv_w", "rnn_conv_b", "gate_a_w", "gate_a_b", "gate_x_w", "gate_x_b",
             "lru_lambda", "w_out", "ln1_g", "ln1_b", "w_ffn_up", "ffn_conv_w", "ffn_conv_b", "w_ffn_down",
             "ple_gate_w", "ple_gate_b", "ple_proj", "ln2_g", "ln2_b"]
    return (loss, dx[None], *[outs[n][0] for n in order], *[outs[n][1] for n in order],
            *[outs[n][2] for n in order], *[outs[n][3] for n in order])
```

```python
import jax
import jax.numpy as jnp
from jax import lax
from jax.experimental import pallas as pl
from jax.experimental.pallas import tpu as pltpu

F32 = jnp.float32
BF16 = jnp.bfloat16

D_MODEL = 1024
D_ATT = 512
D_KV = 128
HEAD_DIM = 64
N_HEADS = 8
N_KV = 2
D_RNN = 512
RNN_BLOCKS = 8
D_IN = 1792
D_FF = 3072
PLE_DIM = 256
QBLK = 128
N_DEV = 8
ALPHA = float(2 ** 0.25)
LN_EPS = 1e-5
LRU_C = 8.0
ADAM_LR, ADAM_B1, ADAM_B2, ADAM_EPS, ADAM_WD, ADAM_STEP = 0.001, 0.9, 0.999, 1e-08, 0.01, 10

V7X_VMEM_LIMIT = 56 * 1024 * 1024
MESH = pl.DeviceIdType.MESH


def _params(*sem, vmem=V7X_VMEM_LIMIT):
    return pltpu.CompilerParams(dimension_semantics=sem or None, vmem_limit_bytes=vmem)


def _resident(shape):
    return pl.BlockSpec(shape, lambda *_: (0,) * len(shape), pipeline_mode=pl.Buffered(1))


def _rows(tb, cols):
    return pl.BlockSpec((tb, cols), lambda i: (i, 0))


def _acc(shape):
    return pl.BlockSpec(shape, lambda *_: (0,) * len(shape))


def _dot(a, b):
    return jnp.dot(a, b, preferred_element_type=F32)


def _dot_nt(a, b):
    return lax.dot_general(a, b, (((1,), (1,)), ((), ())), preferred_element_type=F32)


def _dot_tn(a, b):
    return lax.dot_general(a, b, (((0,), (0,)), ((), ())), preferred_element_type=F32)


def _sigmoid(x):
    return 1.0 / (1.0 + jnp.exp(-x))


_GELU_C = 0.7978845608028654
_GELU_K = 0.044715


def _gelu_and_grad(x):
    u = x * x
    t = jnp.tanh(x * (_GELU_C + (_GELU_C * _GELU_K) * u))
    hp = 0.5 + 0.5 * t
    dg = hp + x * (0.5 - 0.5 * (t * t)) * (_GELU_C + (3.0 * _GELU_C * _GELU_K) * u)
    return x * hp, dg


def _gelu(x):
    return 0.5 * x * (1.0 + jnp.tanh(_GELU_C * (x + _GELU_K * x * x * x)))


def _ln_stats(z):
    mu = jnp.mean(z, axis=-1, keepdims=True)
    zc = z - mu
    var = jnp.mean(zc * zc, axis=-1, keepdims=True)
    rstd = lax.rsqrt(var + LN_EPS)
    return zc * rstd, rstd


def _ln_bwd(dy, xhat, rstd, g):
    dxh = dy * g
    m1 = jnp.mean(dxh, axis=-1, keepdims=True)
    m2 = jnp.mean(dxh * xhat, axis=-1, keepdims=True)
    return rstd * (dxh - m1 - xhat * m2)


def _softplus_neg(lam):
    u = jnp.exp(-jnp.abs(lam))
    w = 1.0 + u
    d = w - 1.0
    log1p_u = jnp.where(d == 0.0, u, jnp.log(w) * (u / jnp.where(d == 0.0, 1.0, d)))
    return jnp.maximum(-lam, 0.0) + log1p_u


def _shift_down(x, halo, s):
    xs = pltpu.roll(x, s, 0)
    hs = pltpu.roll(halo, s, 0)
    row8 = lax.broadcasted_iota(jnp.int32, hs.shape, 0)
    first = jnp.where(row8 < s, hs, xs[:8])
    return jnp.concatenate([first, xs[8:]], axis=0)


def _shift_up(x, halo, s):
    n = x.shape[0]
    xs = pltpu.roll(x, n - s, 0)
    hs = pltpu.roll(halo, 8 - s, 0)
    row8 = lax.broadcasted_iota(jnp.int32, hs.shape, 0)
    last = jnp.where(row8 >= 8 - s, hs, xs[n - 8:])
    return jnp.concatenate([xs[:n - 8], last], axis=0)


def _row_sum(x):
    return jnp.sum(x, axis=0, keepdims=True)


def _put_rows(acc_ref, rows):
    row8 = lax.broadcasted_iota(jnp.int32, acc_ref.shape, 0)
    upd = jnp.zeros(acc_ref.shape, F32)
    for r, vec in enumerate(rows):
        upd = jnp.where(row8 == r, vec, upd)
    acc_ref[...] += upd


def _place():
    return lax.axis_index("x"), lax.axis_index("y"), lax.axis_index("c")


def _dev_index(px, py, pc):
    return 4 * px + 2 * py + pc


_ANY = pl.BlockSpec(memory_space=pl.ANY)


class _Gather:
    def __init__(self, arrays):
        self.arrays = list(arrays)
        self.n = len(self.arrays)

    def out_shape(self):
        return [jax.ShapeDtypeStruct((N_DEV,) + s.shape, s.dtype) for s in self.arrays]

    def scratch(self):
        return [pltpu.SemaphoreType.DMA((self.n, 7)), pltpu.SemaphoreType.DMA((self.n, 7)),
                pltpu.SemaphoreType.DMA((self.n,))]

    def _parts(self, ins, outs, sems):
        send_sems, recv_sems, local_sems = sems
        x, y, c = _place()
        me, sibling = (x, y, c), (x, y, 1 - c)
        chips = [(1 - x, y), (x, 1 - y), (1 - x, 1 - y)]

        def copy(a, k, block, to, src=None):
            rows = outs[a].at[_dev_index(*block)]
            return pltpu.make_async_remote_copy(
                src_ref=rows if src is None else src, dst_ref=rows, send_sem=send_sems.at[a, k],
                recv_sem=recv_sems.at[a, k], device_id=to, device_id_type=MESH)

        rng = range(self.n)
        mine = [pltpu.make_async_copy(ins[a], outs[a].at[_dev_index(*me)], local_sems.at[a]) for a in rng]
        first = [copy(a, 0, me, sibling, src=ins[a]) for a in rng]
        first += [copy(a, 1 + j, me, (*chip, c), src=ins[a]) for j, chip in enumerate(chips) for a in rng]
        landed = [copy(a, 1 + j, (*chip, c), me) for j, chip in enumerate(chips) for a in rng]
        passed = [copy(a, 4 + j, (*chip, c), sibling) for j, chip in enumerate(chips) for a in rng]
        from_sibling = [copy(a, 0, sibling, me) for a in rng]
        from_sibling += [copy(a, 4 + j, (*chip, 1 - c), me) for j, chip in enumerate(chips) for a in rng]
        return mine, first, landed, passed, from_sibling

    def start(self, ins, outs, sems):
        mine, first, _, _, _ = self._parts(ins, outs, sems)
        for cp in mine + first:
            cp.start()

    def forward(self, ins, outs, sems):
        _, _, landed, passed, _ = self._parts(ins, outs, sems)
        for got, fwd in zip(landed, passed):
            got.wait_recv()
            fwd.start()

    def finish(self, ins, outs, sems):
        mine, first, _, passed, from_sibling = self._parts(ins, outs, sems)
        for cp in from_sibling:
            cp.wait_recv()
        for cp in first + passed:
            cp.wait_send()
        for cp in mine:
            cp.wait()

    def before(self, ins, outs, sems, step, nsteps):
        pl.when(step == 0)(lambda: self.start(ins, outs, sems))
        pl.when(step == (7 * nsteps) // 8)(lambda: self.forward(ins, outs, sems))

    def after(self, ins, outs, sems, step, nsteps):
        pl.when(step == nsteps - 1)(lambda: self.finish(ins, outs, sems))


class _Exchange:
    def __init__(self, arrays):
        self.arrays = list(arrays)
        self.n = len(self.arrays)

    def out_shape(self):
        return [jax.ShapeDtypeStruct(b.shape, b.dtype) for b in self.arrays]

    def scratch(self):
        return [pltpu.SemaphoreType.DMA((self.n, 7)), pltpu.SemaphoreType.DMA((self.n, 7)),
                pltpu.SemaphoreType.DMA((self.n,))]

    def _parts(self, ins, outs, sems):
        send_sems, recv_sems, local_sems = sems
        x, y, c = _place()
        me = _dev_index(x, y, c)
        peers = [(x ^ (k >> 2), y ^ ((k >> 1) & 1), c ^ (k & 1)) for k in range(1, N_DEV)]
        rng = range(self.n)
        mine = [pltpu.make_async_copy(ins[a].at[me], outs[a].at[me], local_sems.at[a]) for a in rng]
        sent = [pltpu.make_async_remote_copy(
            src_ref=ins[a].at[_dev_index(*to)], dst_ref=outs[a].at[me], send_sem=send_sems.at[a, k],
            recv_sem=recv_sems.at[a, k], device_id=to, device_id_type=MESH) for k, to in enumerate(peers) for a in rng]
        arrivals = [pltpu.make_async_remote_copy(
            src_ref=ins[a].at[me], dst_ref=outs[a].at[_dev_index(*frm)], send_sem=send_sems.at[a, k],
            recv_sem=recv_sems.at[a, k], device_id=frm, device_id_type=MESH) for k, frm in enumerate(peers) for a in rng]
        return mine, sent, arrivals

    def start(self, ins, outs, sems):
        mine, sent, _ = self._parts(ins, outs, sems)
        for cp in mine + sent:
            cp.start()

    def finish(self, ins, outs, sems):
        mine, sent, arrivals = self._parts(ins, outs, sems)
        for cp in arrivals:
            cp.wait_recv()
        for cp in sent:
            cp.wait_send()
        for cp in mine:
            cp.wait()

    def before(self, ins, outs, sems, step, nsteps):
        pl.when(step == 0)(lambda: self.start(ins, outs, sems))

    def after(self, ins, outs, sems, step, nsteps):
        pl.when(step == nsteps - 1)(lambda: self.finish(ins, outs, sems))


class _Bcast(_Exchange):
    def out_shape(self):
        return [jax.ShapeDtypeStruct((N_DEV,) + s.shape, s.dtype) for s in self.arrays]

    def _parts(self, ins, outs, sems):
        send_sems, recv_sems, local_sems = sems
        x, y, c = _place()
        me = _dev_index(x, y, c)
        peers = [(x ^ (k >> 2), y ^ ((k >> 1) & 1), c ^ (k & 1)) for k in range(1, N_DEV)]
        rng = range(self.n)
        mine = [pltpu.make_async_copy(ins[a], outs[a].at[me], local_sems.at[a]) for a in rng]
        sent = [pltpu.make_async_remote_copy(
            src_ref=ins[a], dst_ref=outs[a].at[me], send_sem=send_sems.at[a, k], recv_sem=recv_sems.at[a, k],
            device_id=to, device_id_type=MESH) for k, to in enumerate(peers) for a in rng]
        arrivals = [pltpu.make_async_remote_copy(
            src_ref=ins[a], dst_ref=outs[a].at[_dev_index(*frm)], send_sem=send_sems.at[a, k],
            recv_sem=recv_sems.at[a, k], device_id=frm, device_id_type=MESH) for k, frm in enumerate(peers) for a in rng]
        return mine, sent, arrivals


class _Multi:
    def __init__(self, comms):
        self.comms = list(comms)
        self.arrays = [arr for c in self.comms for arr in c.arrays]
        self.n = len(self.arrays)

    def out_shape(self):
        return [s for c in self.comms for s in c.out_shape()]

    def scratch(self):
        return [s for c in self.comms for s in c.scratch()]

    def _each(self, ins, outs, sems):
        a = 0
        for j, c in enumerate(self.comms):
            yield c, ins[a:a + c.n], outs[a:a + c.n], sems[3 * j:3 * j + 3]
            a += c.n

    def before(self, ins, outs, sems, step, nsteps):
        for c, ci, co, cs in self._each(ins, outs, sems):
            c.before(ci, co, cs, step, nsteps)

    def after(self, ins, outs, sems, step, nsteps):
        for c, ci, co, cs in self._each(ins, outs, sems):
            c.after(ci, co, cs, step, nsteps)


def _comm_call(comms, name):
    ns = [c.n for c in comms]
    n = sum(ns)

    def body(*refs):
        parts, a, s = [], 0, 2 * n
        for c in comms:
            parts.append((c, refs[a:a + c.n], refs[n + a:n + a + c.n], refs[s:s + 3]))
            a, s = a + c.n, s + 3
        for c, ins, outs, sems in parts:
            c.start(ins, outs, sems)
        for c, ins, outs, sems in parts:
            if isinstance(c, _Gather):
                c.forward(ins, outs, sems)
        for c, ins, outs, sems in parts:
            c.finish(ins, outs, sems)

    res = pl.pallas_call(
        body, name=name, in_specs=[_ANY] * n, out_specs=[_ANY] * n,
        out_shape=[s for c in comms for s in c.out_shape()], scratch_shapes=[s for c in comms for s in c.scratch()],
    )(*[arr for c in comms for arr in c.arrays])
    out, a = [], 0
    for k in ns:
        out.append(res[a:a + k])
        a += k
    return out


def _pcall(body, args, *, name, grid, in_specs, out_specs, out_shape, scratch_shapes=(), sem="parallel", comm=None,
           step_axis=0):
    sem = (sem,) * len(grid) if isinstance(sem, str) else sem
    if comm is None:
        res = pl.pallas_call(body, name=name, grid=grid, in_specs=in_specs, out_specs=out_specs, out_shape=out_shape,
                             scratch_shapes=list(scratch_shapes), compiler_params=_params(*sem))(*args)
        return res, []
    n_in, n_out, n_scr, n = len(in_specs), len(out_specs), len(scratch_shapes), comm.n
    nsteps = grid[step_axis]
    assert all(g == 1 for ax, g in enumerate(grid) if ax != step_axis)

    def hosted(*refs):
        ins, cin = refs[:n_in], refs[n_in:n_in + n]
        o0 = n_in + n
        outs, cout = refs[o0:o0 + n_out], refs[o0 + n_out:o0 + n_out + n]
        s0 = o0 + n_out + n
        scr, sems = refs[s0:s0 + n_scr], refs[s0 + n_scr:]
        step = pl.program_id(step_axis)
        comm.before(cin, cout, sems, step, nsteps)
        body(*ins, *outs, *scr)
        comm.after(cin, cout, sems, step, nsteps)

    res = pl.pallas_call(
        hosted, name=name, grid=grid, in_specs=list(in_specs) + [_ANY] * n, out_specs=list(out_specs) + [_ANY] * n,
        out_shape=list(out_shape) + comm.out_shape(), scratch_shapes=list(scratch_shapes) + comm.scratch(),
        compiler_params=_params(*(("arbitrary",) * len(grid))))(*args, *comm.arrays)
    return res[:n_out], res[n_out:]


def _load_row_halves(top_hbm, bot_hbm, full_s, sems):
    r = top_hbm.shape[1]
    copies = [pltpu.make_async_copy(top_hbm, full_s.at[:, :r, :], sems.at[0]),
              pltpu.make_async_copy(bot_hbm, full_s.at[:, r:, :], sems.at[1])]
    for cp in copies:
        cp.start()
    for cp in copies:
        cp.wait()


def _in_proj(x, w_in_t, comm=None):
    S = x.shape[0]
    tb = min(1024, S)

    def body(x_ref, w_ref, q_ref, k_ref, v_ref, xr_ref, gr_ref):
        u = _dot_nt(x_ref[...].astype(BF16), w_ref[...])
        q_ref[...] = (u[:, :D_ATT] * (HEAD_DIM ** -0.5)).astype(BF16)
        k_ref[...] = u[:, D_ATT:D_ATT + D_KV].astype(BF16)
        v_ref[...] = u[:, D_ATT + D_KV:D_ATT + 2 * D_KV].astype(BF16)
        xr_ref[...] = u[:, D_ATT + 2 * D_KV:D_ATT + 2 * D_KV + D_RNN]
        gr_ref[...] = u[:, D_ATT + 2 * D_KV + D_RNN:]

    return _pcall(
        body, (x, w_in_t), name="in_proj", grid=(S // tb,), comm=comm,
        in_specs=[_rows(tb, D_MODEL), _resident((D_IN, D_MODEL))],
        out_specs=[_rows(tb, D_ATT), _rows(tb, D_KV), _rows(tb, D_KV), _rows(tb, D_RNN), _rows(tb, D_RNN)],
        out_shape=[jax.ShapeDtypeStruct((S, D_ATT), BF16), jax.ShapeDtypeStruct((S, D_KV), BF16),
                   jax.ShapeDtypeStruct((S, D_KV), BF16), jax.ShapeDtypeStruct((S, D_RNN), F32),
                   jax.ShapeDtypeStruct((S, D_RNN), F32)])


GROUP = N_HEADS // N_KV


def _band_mask(i):
    qi = lax.broadcasted_iota(jnp.int32, (GROUP * QBLK, 2 * QBLK), 0) & (QBLK - 1)
    sj = lax.broadcasted_iota(jnp.int32, (GROUP * QBLK, 2 * QBLK), 1)
    return (sj > qi) & (sj <= qi + QBLK) & ((sj >= QBLK) | (i > 0))


def _stack_heads(x, g):
    return jnp.concatenate([x[:, (g * GROUP + hh) * HEAD_DIM:(g * GROUP + hh + 1) * HEAD_DIM] for hh in range(GROUP)],
                           axis=0)


def _unstack_heads(x4):
    return [x4[hh * QBLK:(hh + 1) * QBLK] for hh in range(GROUP)]


def _sink_column(sink_ref, g):
    head = lax.broadcasted_iota(jnp.int32, (GROUP * QBLK, 1), 0) // QBLK
    col = jnp.full((GROUP * QBLK, 1), sink_ref[g * GROUP], F32)
    for hh in range(1, GROUP):
        col = jnp.where(head == hh, sink_ref[g * GROUP + hh], col)
    return col


ATT_STEP = 4
IN_GRAD_PARTS = 2


def _attn_specs(nq=1):
    cur = lambda i: (i, 0)
    prev = lambda i: (jnp.maximum(nq * i - 1, 0), 0)
    return [pl.BlockSpec((nq * QBLK, D_KV), cur), pl.BlockSpec((QBLK, D_KV), prev),
            pl.BlockSpec((nq * QBLK, D_KV), cur), pl.BlockSpec((QBLK, D_KV), prev)]


def _attn_fwd(q, k, v, sinks, comm=None):
    S = q.shape[0]
    nq = min(ATT_STEP, S // QBLK)

    def body(sink_ref, q_ref, kc_ref, kp_ref, vc_ref, vp_ref, o_ref, lse_ref):
        first = pl.program_id(0) * nq
        kall = jnp.concatenate([kp_ref[...], kc_ref[...]], axis=0)
        vall = jnp.concatenate([vp_ref[...], vc_ref[...]], axis=0)
        for b in range(nq):
            valid = _band_mask(first + b)
            rows = slice(b * QBLK, (b + 1) * QBLK)
            keys = slice(b * QBLK, (b + 2) * QBLK)
            qv = q_ref[rows, :]
            outs = []
            for g in range(N_KV):
                kcat = kall[keys, g * HEAD_DIM:(g + 1) * HEAD_DIM]
                vcat = vall[keys, g * HEAD_DIM:(g + 1) * HEAD_DIM]
                s = jnp.where(valid, _dot_nt(_stack_heads(qv, g), kcat), -1e30)
                sink = _sink_column(sink_ref, g)
                m = jnp.maximum(jnp.max(s, axis=1, keepdims=True), sink)
                p = jnp.exp(s - m)
                l = jnp.sum(p, axis=1, keepdims=True) + jnp.exp(sink - m)
                outs += _unstack_heads(_dot(p.astype(BF16), vcat) / l)
                lse_ref[(b * N_KV + g) * GROUP * QBLK:(b * N_KV + g + 1) * GROUP * QBLK, :] = m + jnp.log(l)
            o_ref[rows, :] = jnp.concatenate(outs, axis=1).astype(BF16)

    lse_rows = nq * N_HEADS * QBLK
    return _pcall(
        body, (sinks, q, k, k, v, v), name="attn_fwd", grid=(S // (nq * QBLK),), comm=comm,
        in_specs=[pl.BlockSpec(memory_space=pltpu.SMEM), _rows(nq * QBLK, D_ATT)] + _attn_specs(nq),
        out_specs=[_rows(nq * QBLK, D_ATT), _rows(lse_rows, 1)],
        out_shape=[jax.ShapeDtypeStruct((S, D_ATT), BF16), jax.ShapeDtypeStruct((S * N_HEADS, 1), F32)])


def _w_rows(w_ref):
    return [w_ref[k:k + 1, :] for k in range(w_ref.shape[0])]


def _conv4(x, halo, w, b):
    y = b + w[3] * x
    for s in (1, 2, 3):
        y = y + w[3 - s] * _shift_down(x, halo, s)
    return y


def _rnn_gates(xc, wa, wx, ba, bx, sp):
    xcb = xc.astype(BF16)
    r = _sigmoid(_dot(xcb, wa) + ba)
    ig = _sigmoid(_dot(xcb, wx) + bx)
    la = -LRU_C * r * sp
    a = jnp.exp(la)
    t = jnp.tanh(la)
    f = jnp.sqrt(-2.0 * t / (1.0 - t))
    return r, ig, a, f


def _rnn_fwd(xr, gr, conv_w, conv_b, wa, wx, ba, bx, lam, comm=None):
    S = xr.shape[0]
    tb = min(512, S)

    def body(xr_ref, gr_ref, cw_ref, cb_ref, wa_ref, wx_ref, ba_ref, bx_ref, lam_ref, rec_ref, h_ref,
             xc_ref, r_ref, ig_ref, a_ref, f_ref, halo_s, hc_s, a_s, b_s):
        @pl.when(pl.program_id(0) == 0)
        def _():
            halo_s[...] = jnp.zeros_like(halo_s)
            hc_s[...] = jnp.zeros_like(hc_s)

        x = xr_ref[...]
        xc = _conv4(x, halo_s[...], _w_rows(cw_ref), cb_ref[...])
        halo_s[...] = x[tb - 8:]
        r, ig, a, f = _rnn_gates(xc, wa_ref[...], wx_ref[...], ba_ref[...], bx_ref[...], _softplus_neg(lam_ref[...]))
        xc_ref[...] = xc
        r_ref[...] = r
        ig_ref[...] = ig
        a_ref[...] = a
        f_ref[...] = f
        a_s[...] = a
        b_s[...] = f * ig * xc
        row8 = lax.broadcasted_iota(jnp.int32, (8, D_RNN), 0)

        def tile(t, hc):
            o = pl.multiple_of(t * 8, 8)
            at = a_s[pl.ds(o, 8), :]
            bt = b_s[pl.ds(o, 8), :]
            for s in (1, 2, 4):
                keep = row8 >= s
                a_sh = jnp.where(keep, pltpu.roll(at, s, 0), 1.0)
                b_sh = jnp.where(keep, pltpu.roll(bt, s, 0), 0.0)
                bt = at * b_sh + bt
                at = at * a_sh
            ht = at * hc + bt
            b_s[pl.ds(o, 8), :] = ht
            return _row_sum(jnp.where(row8 == 7, ht, 0.0))

        hc_s[0:1, :] = lax.fori_loop(0, tb // 8, tile, hc_s[0:1, :], unroll=2)
        h = b_s[...]
        h_ref[...] = h
        rec_ref[...] = (h * _gelu(gr_ref[...])).astype(BF16)

    vec = _resident((1, D_RNN))
    kept = jax.ShapeDtypeStruct((S, D_RNN), F32)
    return _pcall(
        body, (xr, gr, conv_w, conv_b, wa, wx, ba, bx, lam), name="rnn_fwd", grid=(S // tb,), sem="arbitrary", comm=comm,
        in_specs=[_rows(tb, D_RNN), _rows(tb, D_RNN), _resident((4, D_RNN)), vec,
                  _resident((D_RNN, D_RNN)), _resident((D_RNN, D_RNN)), vec, vec, vec],
        out_specs=[_rows(tb, D_RNN)] * 7,
        out_shape=[jax.ShapeDtypeStruct((S, D_RNN), BF16), kept, kept, kept, kept, kept, kept],
        scratch_shapes=[pltpu.VMEM((8, D_RNN), F32), pltpu.VMEM((8, D_RNN), F32),
                        pltpu.VMEM((tb, D_RNN), F32), pltpu.VMEM((tb, D_RNN), F32)])


def _mix_ln1_up(x, att, rec, w_out, ln1_g, ln1_b, w_up_top, w_up_bot, fcw, fcb, comm=None):
    S = x.shape[0]
    tb = min(256, S)
    nblk, kh, wblk = w_up_top.shape
    half = nblk // 2

    def body(x_ref, att_ref, rec_ref, wo_ref, g_ref, b_ref, wt_hbm, wb_hbm, fcw_ref, fcb_ref,
             z1_ref, h1b_ref, gate_ref, act_ref, gl_ref, vdgl_ref, halo_s, wu_s, wu_sems):
        @pl.when(pl.program_id(0) == 0)
        def _():
            halo_s[...] = jnp.zeros_like(halo_s)
            _load_row_halves(wt_hbm, wb_hbm, wu_s, wu_sems)

        z1 = ALPHA * x_ref[...] + _dot(att_ref[...], wo_ref[:D_ATT, :]) + _dot(rec_ref[...], wo_ref[D_ATT:, :])
        z1_ref[...] = z1
        xhat, _ = _ln_stats(z1)
        h1b = (xhat * g_ref[...] + b_ref[...]).astype(BF16)
        h1b_ref[...] = h1b
        for jj in range(half):
            cols = slice(jj * wblk, (jj + 1) * wblk)
            gate = _dot(h1b, wu_s[jj])
            val = _dot(h1b, wu_s[jj + half])
            halo = halo_s[:, cols]
            conv = (fcb_ref[:, cols] + fcw_ref[2:3, cols] * gate + fcw_ref[1:2, cols] * _shift_down(gate, halo, 1)
                    + fcw_ref[0:1, cols] * _shift_down(gate, halo, 2))
            halo_s[:, cols] = gate[tb - 8:]
            gl, dgl = _gelu_and_grad(conv)
            gate_ref[:, cols] = gate.astype(BF16)
            act_ref[:, cols] = (gl * val).astype(BF16)
            gl_ref[:, cols] = gl.astype(BF16)
            vdgl_ref[:, cols] = (val * dgl).astype(BF16)

    vec = _resident((1, D_MODEL))
    wide = jax.ShapeDtypeStruct((S, D_FF), BF16)
    return _pcall(
        body, (x, att, rec, w_out, ln1_g, ln1_b, w_up_top, w_up_bot, fcw, fcb), name="mix_ln1_up", grid=(S // tb,),
        sem="arbitrary", comm=comm,
        in_specs=[_rows(tb, D_MODEL), _rows(tb, D_ATT), _rows(tb, D_RNN), _resident((D_MODEL, D_MODEL)), vec, vec,
                  _ANY, _ANY, _resident((3, D_FF)), _resident((1, D_FF))],
        out_specs=[_rows(tb, D_MODEL), _rows(tb, D_MODEL)] + [_rows(tb, D_FF)] * 4,
        out_shape=[jax.ShapeDtypeStruct((S, D_MODEL), F32), jax.ShapeDtypeStruct((S, D_MODEL), BF16), wide, wide, wide, wide],
        scratch_shapes=[pltpu.VMEM((8, D_FF), F32), pltpu.VMEM((nblk, 2 * kh, wblk), BF16),
                        pltpu.SemaphoreType.DMA((2,))])


def _tail(act, gl, vdgl, z1, h1b, p, tgt, w_down, w_pg, b_pg, w_pp, ln1_g, ln1_b, ln2_g, ln2_b):
    S = z1.shape[0]
    tb = min(256, S)

    def body(act_ref, gl_ref, vdgl_ref, z1_ref, h1b_ref, p_ref, t_ref, wd_ref, wpg_ref, bpg_ref, wpp_ref,
             g1_ref, b1_ref, g2_ref, b2_ref, dz2_ref, dpre_ref, dpp_ref, dgc_ref, dval_ref, dh1_ref, acc_ref):
        i = pl.program_id(0)

        @pl.when(i == 0)
        def _():
            acc_ref[...] = jnp.zeros_like(acc_ref)

        ffn = _dot(act_ref[...], wd_ref[...])
        xhat1, _ = _ln_stats(z1_ref[...])
        h1 = xhat1 * g1_ref[...] + b1_ref[...]
        sg = _sigmoid(_dot(h1b_ref[...], wpg_ref[...]) + bpg_ref[...])
        pp = _dot(p_ref[...].astype(BF16), wpp_ref[...])
        z2 = ALPHA * h1 + ffn + sg * pp
        xhat2, rstd2 = _ln_stats(z2)
        y = xhat2 * g2_ref[...] + b2_ref[...]
        err = y - t_ref[...]
        dy = err * (1.0 / D_MODEL)
        loss = 0.5 * jnp.sum(jnp.sum(err * err, axis=1, keepdims=True), axis=0, keepdims=True) * (1.0 / D_MODEL)
        dz2 = _ln_bwd(dy, xhat2, rstd2, g2_ref[...])
        dz2b = dz2.astype(BF16)
        dz2_ref[...] = dz2b
        dpre = dz2 * pp * sg * (1.0 - sg)
        dpreb = dpre.astype(BF16)
        dpre_ref[...] = dpreb
        dpp_ref[...] = (dz2 * sg).astype(BF16)
        dh1_ref[...] = ALPHA * dz2 + _dot_nt(dpreb, wpg_ref[...])
        dactb = _dot_nt(dz2b, wd_ref[...]).astype(BF16)
        dval_ref[...] = dactb * gl_ref[...]
        dgc_ref[...] = dactb * vdgl_ref[...]
        _put_rows(acc_ref, [_row_sum(dy * xhat2), _row_sum(dy), _row_sum(dpre),
                            jnp.broadcast_to(loss, (1, D_MODEL))])

    vec = _resident((1, D_MODEL))
    return pl.pallas_call(
        body, name="tail", grid=(S // tb,),
        in_specs=[_rows(tb, D_FF), _rows(tb, D_FF), _rows(tb, D_FF), _rows(tb, D_MODEL), _rows(tb, D_MODEL),
                  _rows(tb, PLE_DIM), _rows(tb, D_MODEL), _resident((D_FF, D_MODEL)), _resident((D_MODEL, D_MODEL)), vec,
                  _resident((PLE_DIM, D_MODEL)), vec, vec, vec, vec],
        out_specs=[_rows(tb, D_MODEL), _rows(tb, D_MODEL), _rows(tb, D_MODEL), _rows(tb, D_FF),
                   _rows(tb, D_FF), _rows(tb, D_MODEL), _acc((8, D_MODEL))],
        out_shape=[jax.ShapeDtypeStruct((S, D_MODEL), BF16),
                   jax.ShapeDtypeStruct((S, D_MODEL), BF16), jax.ShapeDtypeStruct((S, D_MODEL), BF16),
                   jax.ShapeDtypeStruct((S, D_FF), BF16), jax.ShapeDtypeStruct((S, D_FF), BF16),
                   jax.ShapeDtypeStruct((S, D_MODEL), F32), jax.ShapeDtypeStruct((8, D_MODEL), F32)],
        compiler_params=_params("arbitrary"),
    )(act, gl, vdgl, z1, h1b, p, tgt, w_down, w_pg, b_pg, w_pp, ln1_g, ln1_b, ln2_g, ln2_b)


def _weight_grad(a_list, b_list, name, layout, ts=512, comm=None, b_window=None, halves=False):
    S = a_list[0].shape[0]
    ms = [a.shape[1] for a in a_list]
    M, nb = sum(ms), len(b_list)
    win, Nb = b_window if b_window else (0, b_list[0].shape[1])
    ts = min(ts, S)
    nk = S // ts
    per_b = N_DEV // nb
    na = len(a_list)

    n_out = 2 if halves else 1
    assert layout == "cols" or not halves

    def body(*refs):
        a_refs, b_refs, o_refs, acc_ref = refs[:na], refs[na:na + nb], refs[na + nb:na + nb + n_out], refs[-1]
        o_ref = o_refs[0]
        j, k = pl.program_id(0), pl.program_id(1)

        @pl.when(k == 0)
        def _():
            acc_ref[...] = jnp.zeros_like(acc_ref)

        for jj in range(nb):
            @pl.when(j == jj)
            def _():
                b = b_refs[jj][...].astype(BF16)
                off = 0
                for a_ref, m in zip(a_refs, ms):
                    acc_ref[off:off + m, :] += _dot_tn(a_ref[...].astype(BF16), b)
                    off += m

        @pl.when(k == nk - 1)
        def _():
            for d in range(per_b):
                if layout == "rows":
                    o_ref[d] = acc_ref[d * (M // N_DEV):(d + 1) * (M // N_DEV), :].astype(BF16)
                elif layout == "cols" and halves:
                    for o_half, r0 in zip(o_refs, (0, M // 2)):
                        o_half[d] = acc_ref[r0:r0 + M // 2, d * (Nb // per_b):(d + 1) * (Nb // per_b)].astype(BF16)
                elif layout == "cols":
                    o_ref[d] = acc_ref[:, d * (Nb // per_b):(d + 1) * (Nb // per_b)].astype(BF16)
                else:
                    o_ref[d] = acc_ref[:, d * (Nb // per_b):(d + 1) * (Nb // per_b)].T.astype(BF16)

    def b_index(jj):
        return lambda j, k: (jnp.where(j == jj, k, jnp.where(j < jj, 0, nk - 1)), win)

    if layout == "rows":
        assert nb == 1
        blk = (N_DEV, M // N_DEV, Nb)
    elif layout == "cols":
        blk = (per_b, M // n_out, Nb // per_b)
    else:
        blk = (per_b, Nb // per_b, M)
    res, comm_res = _pcall(
        body, (*a_list, *b_list), name=name, grid=(nb, nk), sem="arbitrary", comm=comm, step_axis=1,
        in_specs=[pl.BlockSpec((ts, m), lambda j, k: (k, 0)) for m in ms]
        + [pl.BlockSpec((ts, Nb), b_index(jj)) for jj in range(nb)],
        out_specs=[pl.BlockSpec(blk, lambda j, k: (j, 0, 0))] * n_out,
        out_shape=[jax.ShapeDtypeStruct((N_DEV,) + blk[1:], BF16)] * n_out,
        scratch_shapes=[pltpu.VMEM((M, Nb), F32)])
    res = res if halves else res[0]
    return (res, comm_res) if comm is not None else res


def _pg_pp_grad(h1b, dpreb, p, dppb, ts=1024):
    S = h1b.shape[0]
    ts = min(ts, S)
    nk = S // ts
    rows, cols = D_MODEL // N_DEV, D_MODEL // N_DEV

    def body(h_ref, dpre_ref, p_ref, dpp_ref, gpg_ref, gpp_ref, acc_pg, acc_pp):
        k = pl.program_id(0)

        @pl.when(k == 0)
        def _():
            acc_pg[...] = jnp.zeros_like(acc_pg)
            acc_pp[...] = jnp.zeros_like(acc_pp)

        acc_pg[...] += _dot_tn(h_ref[...], dpre_ref[...])
        acc_pp[...] += _dot_tn(p_ref[...].astype(BF16), dpp_ref[...])

        @pl.when(k == nk - 1)
        def _():
            for d in range(N_DEV):
                gpg_ref[d] = acc_pg[d * rows:(d + 1) * rows, :].astype(BF16)
                gpp_ref[d] = acc_pp[:, d * cols:(d + 1) * cols].astype(BF16)

    return pl.pallas_call(
        body, name="pg_pp_grad", grid=(nk,),
        in_specs=[_rows(ts, D_MODEL), _rows(ts, D_MODEL), _rows(ts, PLE_DIM), _rows(ts, D_MODEL)],
        out_specs=[_acc((N_DEV, rows, D_MODEL)), _acc((N_DEV, PLE_DIM, cols))],
        out_shape=[jax.ShapeDtypeStruct((N_DEV, rows, D_MODEL), BF16), jax.ShapeDtypeStruct((N_DEV, PLE_DIM, cols), BF16)],
        scratch_shapes=[pltpu.VMEM((D_MODEL, D_MODEL), F32), pltpu.VMEM((PLE_DIM, D_MODEL), F32)],
        compiler_params=_params("arbitrary"))(h1b, dpreb, p, dppb)


def _up_bwd(dgc, gate, dval, dh1p, z1, w_up_top, w_up_bot, fcw, w_out, ln1_g, comm=None):
    S = z1.shape[0]
    tb = min(256, S)
    t16 = tb // 16
    n16 = S // 16
    nblk, kh, wblk = w_up_top.shape
    half = nblk // 2
    nsteps = S // tb

    def body(dgc_ref, dgn_ref, gc_ref, dval_ref, dh1p_ref, z1_ref, wt_hbm, wb_hbm, fcw_ref, wo_ref, g1_ref,
             dgate_ref, dz1_ref, dz1b_ref, datt_ref, drec_ref, accf_ref, accd_ref, wu_s, wu_sems):
        i = pl.program_id(0)

        @pl.when(i == 0)
        def _():
            accf_ref[...] = jnp.zeros_like(accf_ref)
            accd_ref[...] = jnp.zeros_like(accd_ref)
            _load_row_halves(wt_hbm, wb_hbm, wu_s, wu_sems)

        dg = dgc_ref[...].astype(F32)
        nxt = jnp.where(i < nsteps - 1, dgn_ref[...].astype(F32)[0:8], 0.0)
        w = _w_rows(fcw_ref)
        up1, up2 = _shift_up(dg, nxt, 1), _shift_up(dg, nxt, 2)
        dgate = (w[2] * dg + w[1] * up1 + w[0] * up2).astype(BF16)
        dgate_ref[...] = dgate
        gate = gc_ref[...].astype(F32)
        _put_rows(accf_ref, [_row_sum(up2 * gate), _row_sum(up1 * gate), _row_sum(dg * gate), _row_sum(dg)])

        dh1 = dh1p_ref[...]
        for j in range(nblk):
            src = dgate if j < half else dval_ref[...]
            jj = j % half
            dh1 = dh1 + _dot_nt(src[:, jj * wblk:(jj + 1) * wblk], wu_s[j])
        xhat1, rstd1 = _ln_stats(z1_ref[...])
        dz1 = _ln_bwd(dh1, xhat1, rstd1, g1_ref[...])
        dz1_ref[...] = dz1
        dz1b = dz1.astype(BF16)
        dz1b_ref[...] = dz1b
        dcat = _dot_nt(dz1b, wo_ref[...])
        datt_ref[...] = dcat[:, :D_ATT].astype(BF16)
        drec_ref[...] = dcat[:, D_ATT:]
        _put_rows(accd_ref, [_row_sum(dh1 * xhat1), _row_sum(dh1)])

    next16 = pl.BlockSpec((16, D_FF), lambda i: (jnp.minimum((i + 1) * t16, n16 - 1), 0))
    return _pcall(
        body, (dgc, dgc, gate, dval, dh1p, z1, w_up_top, w_up_bot, fcw, w_out, ln1_g), name="up_bwd",
        grid=(nsteps,), sem="arbitrary", comm=comm,
        in_specs=[_rows(tb, D_FF), next16, _rows(tb, D_FF), _rows(tb, D_FF), _rows(tb, D_MODEL),
                  _rows(tb, D_MODEL), _ANY, _ANY, _resident((3, D_FF)),
                  _resident((D_MODEL, D_MODEL)), _resident((1, D_MODEL))],
        scratch_shapes=[pltpu.VMEM((nblk, 2 * kh, wblk), BF16), pltpu.SemaphoreType.DMA((2,))],
        out_specs=[_rows(tb, D_FF), _rows(tb, D_MODEL), _rows(tb, D_MODEL), _rows(tb, D_ATT), _rows(tb, D_RNN),
                   _acc((8, D_FF)), _acc((8, D_MODEL))],
        out_shape=[jax.ShapeDtypeStruct((S, D_FF), BF16), jax.ShapeDtypeStruct((S, D_MODEL), F32),
                   jax.ShapeDtypeStruct((S, D_MODEL), BF16), jax.ShapeDtypeStruct((S, D_ATT), BF16),
                   jax.ShapeDtypeStruct((S, D_RNN), F32), jax.ShapeDtypeStruct((8, D_FF), F32),
                   jax.ShapeDtypeStruct((8, D_MODEL), F32)])


def _attn_bwd(q, k, v, lse, do, sinks, comm=None):
    S = q.shape[0]
    grp = N_HEADS // N_KV
    nq = min(ATT_STEP, S // QBLK)

    def body(sink_ref, q_ref, kc_ref, kp_ref, vc_ref, vp_ref, do_ref, lse_ref, dq_ref, dkc_ref, dkp_ref, dvc_ref, dvp_ref,
             ds_ref):
        i = pl.program_id(0)

        @pl.when(i == 0)
        def _():
            ds_ref[...] = jnp.zeros_like(ds_ref)

        row8 = lax.broadcasted_iota(jnp.int32, (8, 128), 0)
        lane8 = lax.broadcasted_iota(jnp.int32, (8, 128), 1)
        dsink = jnp.zeros((8, 128), F32)
        kall = jnp.concatenate([kp_ref[...], kc_ref[...]], axis=0)
        vall = jnp.concatenate([vp_ref[...], vc_ref[...]], axis=0)
        dk_t = [jnp.zeros((D_KV, QBLK), F32) for _ in range(nq + 1)]
        dv_t = [jnp.zeros((D_KV, QBLK), F32) for _ in range(nq + 1)]
        for b in range(nq):
            valid = _band_mask(i * nq + b)
            rows = slice(b * QBLK, (b + 1) * QBLK)
            keys = slice(b * QBLK, (b + 2) * QBLK)
            qv, dov = q_ref[rows, :], do_ref[rows, :]
            dqs, dks, dvs = [], [], []
            for g in range(N_KV):
                kcat = kall[keys, g * HEAD_DIM:(g + 1) * HEAD_DIM]
                vcat = vall[keys, g * HEAD_DIM:(g + 1) * HEAD_DIM]
                q4, do4 = _stack_heads(qv, g), _stack_heads(dov, g)
                s = jnp.where(valid, _dot_nt(q4, kcat), -1e30)
                lse = lse_ref[(b * N_KV + g) * GROUP * QBLK:(b * N_KV + g + 1) * GROUP * QBLK, :]
                p = jnp.exp(s - lse)
                p_sink = jnp.exp(_sink_column(sink_ref, g) - lse)
                dp = _dot_nt(do4, vcat)
                delta = jnp.sum(p * dp, axis=1, keepdims=True)
                dsc = (p * (dp - delta)).astype(BF16)
                dqs += _unstack_heads(_dot(dsc, kcat) * (HEAD_DIM ** -0.5))
                dks.append(_dot_tn(q4, dsc))
                dvs.append(_dot_tn(do4, p.astype(BF16)))
                for hh, part in enumerate(_unstack_heads(-p_sink * delta)):
                    here = (row8 == 0) & (lane8 == g * grp + hh)
                    dsink = dsink + jnp.where(here, jnp.sum(part, axis=0, keepdims=True), 0.0)
            dq_ref[rows, :] = jnp.concatenate(dqs, axis=1).astype(BF16)
            dk2, dv2 = jnp.concatenate(dks, axis=0), jnp.concatenate(dvs, axis=0)
            dk_t[b], dk_t[b + 1] = dk_t[b] + dk2[:, :QBLK], dk_t[b + 1] + dk2[:, QBLK:]
            dv_t[b], dv_t[b + 1] = dv_t[b] + dv2[:, :QBLK], dv_t[b + 1] + dv2[:, QBLK:]
        dkp_ref[...] = dk_t[0].T
        dvp_ref[...] = dv_t[0].T
        for b in range(nq):
            dkc_ref[b * QBLK:(b + 1) * QBLK, :] = dk_t[b + 1].T
            dvc_ref[b * QBLK:(b + 1) * QBLK, :] = dv_t[b + 1].T
        ds_ref[...] += dsink

    nsteps = S // (nq * QBLK)
    cur = jax.ShapeDtypeStruct((S, D_KV), F32)
    prev = jax.ShapeDtypeStruct((nsteps * QBLK, D_KV), F32)
    big = _rows(nq * QBLK, D_ATT)
    return _pcall(
        body, (sinks, q, k, k, v, v, do, lse), name="attn_bwd", grid=(nsteps,), sem="arbitrary", comm=comm,
        in_specs=[pl.BlockSpec(memory_space=pltpu.SMEM), big] + _attn_specs(nq) + [big, _rows(nq * N_HEADS * QBLK, 1)],
        out_specs=[big, _rows(nq * QBLK, D_KV), _rows(QBLK, D_KV), _rows(nq * QBLK, D_KV), _rows(QBLK, D_KV),
                   _acc((8, 128))],
        out_shape=[jax.ShapeDtypeStruct((S, D_ATT), BF16), cur, prev, cur, prev, jax.ShapeDtypeStruct((8, 128), F32)])


def _rnn_bwd(xr, gr, h, kept, drec, conv_w, wa, wx, lam, comm=None):
    S = xr.shape[0]
    tb = min(512, S)
    t8 = tb // 8
    nsteps = S // tb

    def body(xr_ref, xp_ref, gr_ref, h_ref, hp_ref, xc_ref, r_ref, ig_ref, a_ref, f_ref, drec_ref, cw_ref, wa_ref, wx_ref,
             lam_ref, dxr_ref, dgr_ref, gwa_ref, gwx_ref, acc_ref, carry_s, dxc_halo_s, d_s, gwa_s, gwx_s):
        i = pl.program_id(0)
        blk = nsteps - 1 - i

        @pl.when(i == 0)
        def _():
            gwa_s[...] = jnp.zeros_like(gwa_s)
            gwx_s[...] = jnp.zeros_like(gwx_s)
            acc_ref[...] = jnp.zeros_like(acc_ref)
            carry_s[...] = jnp.zeros_like(carry_s)
            dxc_halo_s[...] = jnp.zeros_like(dxc_halo_s)

        x = xr_ref[...]
        xhalo = jnp.where(blk > 0, xp_ref[...], 0.0)
        cw = _w_rows(cw_ref)
        xs = [_shift_down(x, xhalo, 3), _shift_down(x, xhalo, 2), _shift_down(x, xhalo, 1), x]
        xc, r, ig, a, f = xc_ref[...], r_ref[...], ig_ref[...], a_ref[...], f_ref[...]
        sp = _softplus_neg(lam_ref[...])
        hcur = h_ref[...]
        hprev = _shift_down(hcur, jnp.where(blk > 0, hp_ref[...], 0.0), 1)
        gl, dgl = _gelu_and_grad(gr_ref[...])
        drec = drec_ref[...]
        dgr_ref[...] = (drec * hcur * dgl).astype(BF16)
        d_s[...] = drec * gl
        row8 = lax.broadcasted_iota(jnp.int32, (8, D_RNN), 0)

        def tile(t, c):
            o = pl.multiple_of((t8 - 1 - t) * 8, 8)
            a8 = a_ref[pl.ds(o, 8), :]
            dt = d_s[pl.ds(o, 8), :]
            at = jnp.where(row8 == 7, 1.0, pltpu.roll(a8, 7, 0))
            for s in (1, 2, 4):
                keep = row8 < 8 - s
                a_sh = jnp.where(keep, pltpu.roll(at, 8 - s, 0), 1.0)
                d_sh = jnp.where(keep, pltpu.roll(dt, 8 - s, 0), 0.0)
                dt = at * d_sh + dt
                at = at * a_sh
            lt = at * c + dt
            d_s[pl.ds(o, 8), :] = lt
            return _row_sum(jnp.where(row8 == 0, a8 * lt, 0.0))

        carry_s[0:1, :] = lax.fori_loop(0, t8, tile, carry_s[0:1, :], unroll=2)
        lmb = d_s[...]
        a2 = a * a
        dla = lmb * hprev * a - lmb * ig * xc * (a2 / f)
        di = lmb * f * xc
        dr = dla * (-LRU_C) * sp
        dpa = dr * r * (1.0 - r)
        dpx = di * ig * (1.0 - ig)
        dpab = dpa.astype(BF16)
        dpxb = dpx.astype(BF16)
        xcb = xc.astype(BF16)
        gwa_s[...] += _dot_tn(xcb, dpab)
        gwx_s[...] += _dot_tn(xcb, dpxb)

        @pl.when(i == nsteps - 1)
        def _():
            for dense, out in ((gwa_s[...], gwa_ref), (gwx_s[...], gwx_ref)):
                for b in range(RNN_BLOCKS):
                    rows = slice(b * HEAD_DIM, (b + 1) * HEAD_DIM)
                    out[rows, :] = dense[rows, b * HEAD_DIM:(b + 1) * HEAD_DIM]

        dxc = lmb * f * ig + _dot_nt(dpab, wa_ref[...]) + _dot_nt(dpxb, wx_ref[...])
        nxt = dxc_halo_s[...]
        dxr = cw[3] * dxc
        for s in (1, 2, 3):
            dxr = dxr + cw[3 - s] * _shift_up(dxc, nxt, s)
        dxr_ref[...] = dxr.astype(BF16)
        dxc_halo_s[...] = dxc[:8]
        dlam = _row_sum(dla * (-LRU_C) * r) * (-1.0 / (1.0 + jnp.exp(lam_ref[...])))
        _put_rows(acc_ref, [_row_sum(dxc * xs[0]), _row_sum(dxc * xs[1]), _row_sum(dxc * xs[2]), _row_sum(dxc * xs[3]),
                            _row_sum(dxc), _row_sum(dpa), _row_sum(dpx), dlam])

    rev = lambda i: (nsteps - 1 - i, 0)
    prev8 = lambda i: (jnp.maximum((nsteps - 1 - i) * t8 - 1, 0), 0)
    blkspec = pl.BlockSpec((tb, D_RNN), rev)
    halo8 = pl.BlockSpec((8, D_RNN), prev8)
    vec = _resident((1, D_RNN))
    return _pcall(
        body, (xr, xr, gr, h, h, *kept, drec, conv_w, wa, wx, lam), name="rnn_bwd", grid=(nsteps,),
        sem="arbitrary", comm=comm,
        in_specs=[blkspec, halo8, blkspec, blkspec, halo8] + [blkspec] * 6
        + [_resident((4, D_RNN)), _resident((D_RNN, D_RNN)), _resident((D_RNN, D_RNN)), vec],
        out_specs=[blkspec, blkspec, _acc((D_RNN, HEAD_DIM)), _acc((D_RNN, HEAD_DIM)), _acc((8, D_RNN))],
        out_shape=[jax.ShapeDtypeStruct((S, D_RNN), BF16), jax.ShapeDtypeStruct((S, D_RNN), BF16),
                   jax.ShapeDtypeStruct((D_RNN, HEAD_DIM), F32), jax.ShapeDtypeStruct((D_RNN, HEAD_DIM), F32),
                   jax.ShapeDtypeStruct((8, D_RNN), F32)],
        scratch_shapes=[pltpu.VMEM((8, D_RNN), F32), pltpu.VMEM((8, D_RNN), F32), pltpu.VMEM((tb, D_RNN), F32),
                        pltpu.VMEM((D_RNN, D_RNN), F32), pltpu.VMEM((D_RNN, D_RNN), F32)])


def _in_bwd(dq, dkc, dkp, dvc, dvp, dxr, dgr, dz1, w_in, comm=None):
    S = dz1.shape[0]
    tb = min(ATT_STEP * QBLK, S)
    nsteps = S // tb
    ring = 3

    def body(dq_ref, dkc_ref, dkn_ref, dvc_ref, dvn_ref, dxr_ref, dgr_ref, dz1_hbm, w_ref, dkv_ref, dx_ref,
             ring_s, ring_sems):
        i = pl.program_id(0)
        last = i == nsteps - 1

        def fetch(step):
            slot = step % ring
            rows = pl.ds(pl.multiple_of(step * tb, tb), tb)
            return pltpu.make_async_copy(dz1_hbm.at[rows, :], ring_s.at[slot], ring_sems.at[slot])

        @pl.when(i == 0)
        def _():
            for step in range(min(ring - 1, nsteps)):
                fetch(step).start()

        @pl.when(i + ring - 1 < nsteps)
        def _():
            fetch(i + ring - 1).start()

        def total(cur_ref, next_ref):
            nxt = jnp.where(last, 0.0, next_ref[...])
            tail = cur_ref[tb - QBLK:, :] + nxt
            return jnp.concatenate([cur_ref[:tb - QBLK, :], tail], axis=0) if tb > QBLK else tail

        dkv = jnp.concatenate([total(dkc_ref, dkn_ref), total(dvc_ref, dvn_ref)], axis=1).astype(BF16)
        dkv_ref[...] = dkv
        du = jnp.concatenate([dq_ref[...], dkv, dxr_ref[...], dgr_ref[...]], axis=1)
        prod = _dot(du, w_ref[...])
        fetch(i).wait()
        dx_ref[...] = ALPHA * ring_s[i % ring] + prod

    nextp = pl.BlockSpec((QBLK, D_KV), lambda i: (jnp.minimum(i + 1, nsteps - 1), 0))
    return _pcall(
        body, (dq, dkc, dkp, dvc, dvp, dxr, dgr, dz1, w_in), name="in_bwd", grid=(nsteps,), sem="arbitrary", comm=comm,
        in_specs=[_rows(tb, D_ATT), _rows(tb, D_KV), nextp, _rows(tb, D_KV), nextp,
                  _rows(tb, D_RNN), _rows(tb, D_RNN), _ANY, _resident((D_IN, D_MODEL))],
        out_specs=[_rows(tb, 2 * D_KV), _rows(tb, D_MODEL)],
        out_shape=[jax.ShapeDtypeStruct((S, 2 * D_KV), BF16), jax.ShapeDtypeStruct((S, D_MODEL), F32)],
        scratch_shapes=[pltpu.VMEM((ring, tb, D_MODEL), F32), pltpu.SemaphoreType.DMA((ring,))])


def _block_diag(w):
    eye = jnp.eye(RNN_BLOCKS, dtype=w.dtype)
    return (w[:, :, None, :] * eye[:, None, :, None]).reshape(D_RNN, D_RNN).astype(BF16)


def _adamw(w, g, m, v):
    m = ADAM_B1 * m + (1.0 - ADAM_B1) * g
    v = ADAM_B2 * v + (1.0 - ADAM_B2) * (g * g)
    m_hat = m / (1.0 - ADAM_B1 ** ADAM_STEP)
    v_hat = v / (1.0 - ADAM_B2 ** ADAM_STEP)
    delta = -ADAM_LR * (m_hat / (jnp.sqrt(v_hat) + ADAM_EPS) + ADAM_WD * w)
    return delta, m, v


def _sum_adamw(parts, w, m, v, name):
    parts = parts if isinstance(parts, (list, tuple)) else [parts]
    R, C = w.shape
    rb = R if R <= 256 else (256 if parts[0].shape[1] % 256 == 0 else 128)
    per = parts[0].shape[1] // rb
    assert R % rb == 0 and parts[0].shape[1] % rb == 0
    n = len(parts)

    def body(*refs):
        p_refs = refs[:n]
        w_ref, m_ref, v_ref, g_out, d_out, m_out, v_out = refs[n:]
        which = pl.program_id(0) // per

        def total(p_ref):
            g = p_ref[0].astype(F32)
            for d in range(1, N_DEV):
                g = g + p_ref[d].astype(F32)
            return g

        g = total(p_refs[0])
        for j in range(1, n):
            g = jnp.where(which == j, total(p_refs[j]), g)
        delta, mn, vn = _adamw(w_ref[...], g, m_ref[...], v_ref[...])
        g_out[...] = g
        d_out[...] = delta
        m_out[...] = mn
        v_out[...] = vn

    def part_spec(j):
        return pl.BlockSpec((N_DEV, rb, C), lambda i: (0, jnp.clip(i - j * per, 0, per - 1), 0))

    blk = _rows(rb, C)
    out = jax.ShapeDtypeStruct((R, C), F32)
    return pl.pallas_call(
        body, name=name, grid=(R // rb,),
        in_specs=[part_spec(j) for j in range(n)] + [blk, blk, blk],
        out_specs=[blk, blk, blk, blk], out_shape=[out, out, out, out],
        compiler_params=_params("parallel"),
    )(*parts, w, m, v)


def _sum_adamw_group(items, name):
    n = len(items)

    def body(*refs):
        ins, outs = refs[:4 * n], refs[4 * n:]
        for j in range(n):
            p_ref, w_ref, m_ref, v_ref = ins[4 * j:4 * j + 4]
            g = p_ref[0].astype(F32)
            for d in range(1, N_DEV):
                g = g + p_ref[d].astype(F32)
            delta, mn, vn = _adamw(w_ref[...], g, m_ref[...], v_ref[...])
            for o_ref, val in zip(outs[4 * j:4 * j + 4], (g, delta, mn, vn)):
                o_ref[...] = val

    out_shape = [jax.ShapeDtypeStruct(w.shape, F32) for _, w, _, _ in items for _ in range(4)]
    res = pl.pallas_call(body, name=name, out_shape=out_shape, compiler_params=_params())(
        *[a for item in items for a in item])
    return [res[4 * j:4 * j + 4] for j in range(n)]


_SMALL = [("attn_sinks", "s", 0, 1, None), ("rnn_conv_w", "r", 0, 4, "cols"), ("rnn_conv_b", "r", 4, 1, None),
          ("gate_a_w", "a", 0, D_RNN, None), ("gate_a_b", "r", 5, 1, None), ("gate_x_w", "x", 0, D_RNN, None),
          ("gate_x_b", "r", 6, 1, None), ("lru_lambda", "r", 7, 1, None), ("ln1_g", "d", 0, 1, None),
          ("ln1_b", "d", 1, 1, None), ("ffn_conv_w", "f", 0, 3, "cols"), ("ffn_conv_b", "f", 3, 1, None),
          ("ple_gate_b", "t", 2, 1, None), ("ln2_g", "t", 0, 1, None), ("ln2_b", "t", 1, 1, None)]
_LOSS_ROW = 3


_ACC_COLS = {"t": (0, D_MODEL), "f": (D_MODEL, D_FF), "d": (D_MODEL + D_FF, D_MODEL), "s": (2 * D_MODEL + D_FF, 128),
             "r": (2 * D_MODEL + D_FF + 128, D_RNN)}
_ACC_WIDTH = 2 * D_MODEL + D_FF + 128 + D_RNN


def _small_update(rows_all, gates_all, params):
    flat = [arr for triple in params for arr in triple]
    n_par = len(_SMALL)

    def body(*refs):
        rows_ref, gates_ref = refs[:2]
        p_refs = refs[2:2 + 3 * n_par]
        loss_ref = refs[2 + 3 * n_par]
        o_refs = refs[3 + 3 * n_par:3 + 7 * n_par]
        rows_s, tmp_r, tmp_f = refs[3 + 7 * n_par:]
        me = _dev_index(*_place())
        rows_sum, gates_sum = rows_ref[0], gates_ref[0]
        for d in range(1, N_DEV):
            rows_sum = rows_sum + rows_ref[d]
            gates_sum = gates_sum + gates_ref[d]
        rows_s[...] = rows_sum
        t0 = _ACC_COLS["t"][0]
        loss_ref[...] = rows_s[_LOSS_ROW:_LOSS_ROW + 1, t0:t0 + 128]
        for i, (name, key, row, rows, how) in enumerate(_SMALL):
            w_ref, m_ref, v_ref = p_refs[3 * i:3 * i + 3]
            g_out, d_out, m_out, v_out = o_refs[4 * i:4 * i + 4]
            if key == "a":
                g = gates_sum[:, :HEAD_DIM]
            elif key == "x":
                g = gates_sum[:, HEAD_DIM:]
            elif how == "cols":
                c0, width = _ACC_COLS[key]
                full = rows_s[:, c0:c0 + width]
                shard = width // N_DEV
                mine = full[:, :shard]
                for d in range(1, N_DEV):
                    mine = jnp.where(me == d, full[:, d * shard:(d + 1) * shard], mine)
                tmp = tmp_r if key == "r" else tmp_f
                tmp[...] = mine
                g = tmp[row:row + rows, :]
            else:
                c0, width = _ACC_COLS[key]
                g = rows_s[row:row + rows, c0:c0 + width][:, :w_ref.shape[1]]
            delta, mn, vn = _adamw(w_ref[...], g, m_ref[...], v_ref[...])
            g_out[...] = g
            d_out[...] = delta
            m_out[...] = mn
            v_out[...] = vn

    outs = [jax.ShapeDtypeStruct((1, 128), F32)]
    for w, _, _ in params:
        outs += [jax.ShapeDtypeStruct(w.shape, F32)] * 4
    scratch = [pltpu.VMEM((8, _ACC_WIDTH), F32), pltpu.VMEM((8, D_RNN // N_DEV), F32), pltpu.VMEM((8, D_FF // N_DEV), F32)]
    res = pl.pallas_call(body, name="small_update", out_shape=outs, scratch_shapes=scratch)(rows_all, gates_all, *flat)
    return res[0], [res[1 + 4 * i:5 + 4 * i] for i in range(n_par)]


def kernel(x, p, w_in, attn_sinks, rnn_conv_w, rnn_conv_b, gate_a_w, gate_a_b, gate_x_w, gate_x_b, lru_lambda, w_out, ln1_g, ln1_b, w_ffn_up, ffn_conv_w, ffn_conv_b, w_ffn_down, ple_gate_w, ple_gate_b, ple_proj, ln2_g, ln2_b, loss_target, m_w_in, m_attn_sinks, m_rnn_conv_w, m_rnn_conv_b, m_gate_a_w, m_gate_a_b, m_gate_x_w, m_gate_x_b, m_lru_lambda, m_w_out, m_ln1_g, m_ln1_b, m_w_ffn_up, m_ffn_conv_w, m_ffn_conv_b, m_w_ffn_down, m_ple_gate_w, m_ple_gate_b, m_ple_proj, m_ln2_g, m_ln2_b, v_w_in, v_attn_sinks, v_rnn_conv_w, v_rnn_conv_b, v_gate_a_w, v_gate_a_b, v_gate_x_w, v_gate_x_b, v_lru_lambda, v_w_out, v_ln1_g, v_ln1_b, v_w_ffn_up, v_ffn_conv_w, v_ffn_conv_b, v_w_ffn_down, v_ple_gate_w, v_ple_gate_b, v_ple_proj, v_ln2_g, v_ln2_b):
    from_col_blocks = lambda g: g.transpose(1, 0, 2).reshape(g.shape[1], N_DEV * g.shape[2])

    xs, ps, tgt, sinks = x[0], p[0, 0], loss_target[0], attn_sinks[0]
    wa, wx = _block_diag(gate_a_w[0]), _block_diag(gate_x_w[0])

    conv_cols = jnp.concatenate([rnn_conv_w[0].reshape(1, -1), ffn_conv_w[0].reshape(1, -1)], axis=1)
    n_rc, n_fc = 4 * D_RNN // N_DEV, 3 * D_FF // N_DEV
    ((g_in,),) = _comm_call([_Gather([w_in[0].T.astype(BF16)])], "gather_w_in")
    w_in_full = g_in.reshape(D_IN, D_MODEL)

    (q, k, v, xr, gr), _ = _in_proj(xs, w_in_full)
    w_up_shard = w_ffn_up[0].astype(BF16)
    (att, lse), (g_out, w_up_top, g_conv) = _attn_fwd(
        q, k, v, sinks,
        comm=_Multi([_Gather([w_out[0].astype(BF16), w_up_shard[:D_MODEL // 2]]),
                     _Bcast([jnp.broadcast_to(conv_cols, (8, n_rc + n_fc))])]))
    rcw = from_col_blocks(g_conv[:, 0, :n_rc].reshape(N_DEV, 4, D_RNN // N_DEV))
    fcw = from_col_blocks(g_conv[:, 0, n_rc:].reshape(N_DEV, 3, D_FF // N_DEV))
    (rec, h, *kept), (w_up_bot,) = _rnn_fwd(xr, gr, rcw, rnn_conv_b, wa, wx, gate_a_b, gate_x_b, lru_lambda,
                                            comm=_Gather([w_up_shard[D_MODEL // 2:]]))
    w_out_full = g_out.reshape(D_MODEL, D_MODEL)
    (z1, h1b, gate, act, gl, vdgl), (g_down, g_pg, g_pp) = _mix_ln1_up(
        xs, att, rec, w_out_full, ln1_g, ln1_b, w_up_top, w_up_bot, fcw, ffn_conv_b,
        comm=_Gather([w_ffn_down[0].astype(BF16), ple_gate_w[0].astype(BF16), ple_proj[0].astype(BF16)]))
    dz2b, dpreb, dppb, dgc, dval, dh1p, acc_t = _tail(
        act, gl, vdgl, z1, h1b, ps, tgt, g_down.reshape(D_FF, D_MODEL), g_pg.reshape(D_MODEL, D_MODEL), ple_gate_b,
        from_col_blocks(g_pp), ln1_g, ln1_b, ln2_g, ln2_b)

    gd_down = _weight_grad([dz2b], [act], "down_grad", "rows_t", ts=1024)
    gd_pg, gd_pp = _pg_pp_grad(h1b, dpreb, ps, dppb)
    (dgate, dz1, dz1b, datt, drec, acc_f, acc_d), (r_down, r_pg, r_pp) = _up_bwd(
        dgc, gate, dval, dh1p, z1, w_up_top, w_up_bot, fcw, w_out_full, ln1_g, comm=_Exchange([gd_down, gd_pg, gd_pp]))
    gd_up_top, gd_up_bot = _weight_grad([h1b], [dgate, dval], "up_grad", "cols", halves=True)
    gd_out = _weight_grad([att, rec], [dz1b], "out_grad", "rows", ts=1024)
    (dq, dkc, dkp, dvc, dvp, acc_s), (r_up_top,) = _attn_bwd(q, k, v, lse, datt, sinks, comm=_Exchange([gd_up_top]))
    early = jnp.concatenate([acc_t, acc_f, acc_d], axis=1)
    (dxr, dgr, g_wa, g_wx, acc_r), (r_up_bot, r_out, early_all) = _rnn_bwd(
        xr, gr, h, kept, drec, rcw, wa, wx, lru_lambda, comm=_Multi([_Exchange([gd_up_bot, gd_out]), _Bcast([early])]))
    (dkv, dx), _ = _in_bwd(dq, dkc, dkp, dvc, dvp, dxr, dgr, dz1, w_in_full)
    du_parts = [dq, dkv, dxr, dgr]
    lanes = D_RNN // 128
    late = jnp.concatenate([g_wa, g_wx], axis=1)
    late = jnp.concatenate([late, acc_s, acc_r.reshape(8, lanes, 128).transpose(1, 0, 2).reshape(8 * lanes, 128)], axis=0)
    width = D_MODEL // IN_GRAD_PARTS
    comm, r_parts = _Gather([late]), []
    for part in range(IN_GRAD_PARTS):
        gd_part, got = _weight_grad(du_parts, [xs], f"in_grad_{part}", "rows", ts=1024, b_window=(part, width), comm=comm)
        if part == 0:
            (late_all,) = got
        else:
            r_parts += got
        comm = _Exchange([gd_part])
    r_parts += _comm_call([comm], "exchange_w_in")[0]
    r_in = jnp.concatenate(r_parts, axis=2)
    acc_r_all = late_all[:, D_RNN + 8:].reshape(N_DEV, lanes, 8, 128).transpose(0, 2, 1, 3).reshape(N_DEV, 8, D_RNN)
    small_parts = (jnp.concatenate([early_all, late_all[:, D_RNN:D_RNN + 8], acc_r_all], axis=2),
                   late_all[:, :D_RNN])

    outs = {}
    res = _sum_adamw([r_up_top, r_up_bot], w_ffn_up[0], m_w_ffn_up[0], v_w_ffn_up[0], "adamw_w_ffn_up")
    outs["w_ffn_up"] = [r[None] for r in res]
    small_shards = [("w_out", r_out, w_out, m_w_out, v_w_out), ("ple_gate_w", r_pg, ple_gate_w, m_ple_gate_w, v_ple_gate_w),
                    ("ple_proj", r_pp, ple_proj, m_ple_proj, v_ple_proj),
                    ("w_ffn_down", r_down, w_ffn_down, m_w_ffn_down, v_w_ffn_down)]
    group = _sum_adamw_group([(parts, w[0], m[0], v[0]) for _, parts, w, m, v in small_shards]
                             + [(r_in, w_in[0].T, m_w_in[0].T, v_w_in[0].T)], "adamw_small_shards")
    for (name, *_), res in zip(small_shards, group):
        outs[name] = [r[None] for r in res]
    outs["w_in"] = [r.T[None] for r in group[-1]]

    given = dict(attn_sinks=(attn_sinks, m_attn_sinks, v_attn_sinks), rnn_conv_w=(rnn_conv_w, m_rnn_conv_w, v_rnn_conv_w),
                 rnn_conv_b=(rnn_conv_b, m_rnn_conv_b, v_rnn_conv_b), gate_a_w=(gate_a_w, m_gate_a_w, v_gate_a_w),
                 gate_a_b=(gate_a_b, m_gate_a_b, v_gate_a_b), gate_x_w=(gate_x_w, m_gate_x_w, v_gate_x_w),
                 gate_x_b=(gate_x_b, m_gate_x_b, v_gate_x_b), lru_lambda=(lru_lambda, m_lru_lambda, v_lru_lambda),
                 ln1_g=(ln1_g, m_ln1_g, v_ln1_g), ln1_b=(ln1_b, m_ln1_b, v_ln1_b),
                 ffn_conv_w=(ffn_conv_w, m_ffn_conv_w, v_ffn_conv_w), ffn_conv_b=(ffn_conv_b, m_ffn_conv_b, v_ffn_conv_b),
                 ple_gate_b=(ple_gate_b, m_ple_gate_b, v_ple_gate_b), ln2_g=(ln2_g, m_ln2_g, v_ln2_g),
                 ln2_b=(ln2_b, m_ln2_b, v_ln2_b))
    as_2d = lambda a: a.reshape(-1, a.shape[-1])
    loss_row, small_res = _small_update(*small_parts, [tuple(as_2d(a) for a in given[n]) for n, *_ in _SMALL])
    loss = loss_row[0, 0]
    for (n, *_), res in zip(_SMALL, small_res):
        outs[n] = [r.reshape(given[n][0].shape) for r in res]

    order = ["w_in", "attn_sinks", "rnn_conv_w", "rnn_conv_b", "gate_a_w", "gate_a_b", "gate_x_w", "gate_x_b",
             "lru_lambda", "w_out", "ln1_g", "ln1_b", "w_ffn_up", "ffn_conv_w", "ffn_conv_b", "w_ffn_down",
             "ple_gate_w", "ple_gate_b", "ple_proj", "ln2_g", "ln2_b"]
    return (loss, dx[None], *[outs[n][0] for n in order], *[outs[n][1] for n in order],
            *[outs[n][2] for n in order], *[outs[n][3] for n in order])
```

```python
import jax
import jax.numpy as jnp
from jax import lax
from jax.experimental import pallas as pl
from jax.experimental.pallas import tpu as pltpu

F32 = jnp.float32
BF16 = jnp.bfloat16

D_MODEL = 1024
D_ATT = 512
D_KV = 128
HEAD_DIM = 64
N_HEADS = 8
N_KV = 2
D_RNN = 512
RNN_BLOCKS = 8
D_IN = 1792
D_FF = 3072
PLE_DIM = 256
QBLK = 128
N_DEV = 8
ALPHA = float(2 ** 0.25)
LN_EPS = 1e-5
LRU_C = 8.0
ADAM_LR, ADAM_B1, ADAM_B2, ADAM_EPS, ADAM_WD, ADAM_STEP = 0.001, 0.9, 0.999, 1e-08, 0.01, 10

V7X_VMEM_LIMIT = 56 * 1024 * 1024
MESH = pl.DeviceIdType.MESH


def _params(*sem, vmem=V7X_VMEM_LIMIT):
    return pltpu.CompilerParams(dimension_semantics=sem or None, vmem_limit_bytes=vmem)


def _resident(shape):
    return pl.BlockSpec(shape, lambda *_: (0,) * len(shape), pipeline_mode=pl.Buffered(1))


def _rows(tb, cols):
    return pl.BlockSpec((tb, cols), lambda i: (i, 0))


def _acc(shape):
    return pl.BlockSpec(shape, lambda *_: (0,) * len(shape))


def _dot(a, b):
    return jnp.dot(a, b, preferred_element_type=F32)


def _dot_nt(a, b):
    return lax.dot_general(a, b, (((1,), (1,)), ((), ())), preferred_element_type=F32)


def _dot_tn(a, b):
    return lax.dot_general(a, b, (((0,), (0,)), ((), ())), preferred_element_type=F32)


def _sigmoid(x):
    return 1.0 / (1.0 + jnp.exp(-x))


_GELU_C = 0.7978845608028654
_GELU_K = 0.044715


def _gelu_and_grad(x):
    u = x * x
    t = jnp.tanh(x * (_GELU_C + (_GELU_C * _GELU_K) * u))
    hp = 0.5 + 0.5 * t
    dg = hp + x * (0.5 - 0.5 * (t * t)) * (_GELU_C + (3.0 * _GELU_C * _GELU_K) * u)
    return x * hp, dg


def _gelu(x):
    return 0.5 * x * (1.0 + jnp.tanh(_GELU_C * (x + _GELU_K * x * x * x)))


def _ln_stats(z):
    mu = jnp.mean(z, axis=-1, keepdims=True)
    zc = z - mu
    var = jnp.mean(zc * zc, axis=-1, keepdims=True)
    rstd = lax.rsqrt(var + LN_EPS)
    return zc * rstd, rstd


def _ln_bwd(dy, xhat, rstd, g):
    dxh = dy * g
    m1 = jnp.mean(dxh, axis=-1, keepdims=True)
    m2 = jnp.mean(dxh * xhat, axis=-1, keepdims=True)
    return rstd * (dxh - m1 - xhat * m2)


def _softplus_neg(lam):
    u = jnp.exp(-jnp.abs(lam))
    w = 1.0 + u
    d = w - 1.0
    log1p_u = jnp.where(d == 0.0, u, jnp.log(w) * (u / jnp.where(d == 0.0, 1.0, d)))
    return jnp.maximum(-lam, 0.0) + log1p_u


def _shift_down(x, halo, s):
    xs = pltpu.roll(x, s, 0)
    hs = pltpu.roll(halo, s, 0)
    row8 = lax.broadcasted_iota(jnp.int32, hs.shape, 0)
    first = jnp.where(row8 < s, hs, xs[:8])
    return jnp.concatenate([first, xs[8:]], axis=0)


def _shift_up(x, halo, s):
    n = x.shape[0]
    xs = pltpu.roll(x, n - s, 0)
    hs = pltpu.roll(halo, 8 - s, 0)
    row8 = lax.broadcasted_iota(jnp.int32, hs.shape, 0)
    last = jnp.where(row8 >= 8 - s, hs, xs[n - 8:])
    return jnp.concatenate([xs[:n - 8], last], axis=0)


def _row_sum(x):
    return jnp.sum(x, axis=0, keepdims=True)


def _put_rows(acc_ref, rows):
    row8 = lax.broadcasted_iota(jnp.int32, acc_ref.shape, 0)
    upd = jnp.zeros(acc_ref.shape, F32)
    for r, vec in enumerate(rows):
        upd = jnp.where(row8 == r, vec, upd)
    acc_ref[...] += upd


def _place():
    return lax.axis_index("x"), lax.axis_index("y"), lax.axis_index("c")


def _dev_index(px, py, pc):
    return 4 * px + 2 * py + pc


_ANY = pl.BlockSpec(memory_space=pl.ANY)


class _Gather:
    def __init__(self, arrays):
        self.arrays = list(arrays)
        self.n = len(self.arrays)

    def out_shape(self):
        return [jax.ShapeDtypeStruct((N_DEV,) + s.shape, s.dtype) for s in self.arrays]

    def scratch(self):
        return [pltpu.SemaphoreType.DMA((self.n, 7)), pltpu.SemaphoreType.DMA((self.n, 7)),
                pltpu.SemaphoreType.DMA((self.n,))]

    def _parts(self, ins, outs, sems):
        send_sems, recv_sems, local_sems = sems
        x, y, c = _place()
        me, sibling = (x, y, c), (x, y, 1 - c)
        chips = [(1 - x, y), (x, 1 - y), (1 - x, 1 - y)]

        def copy(a, k, block, to, src=None):
            rows = outs[a].at[_dev_index(*block)]
            return pltpu.make_async_remote_copy(
                src_ref=rows if src is None else src, dst_ref=rows, send_sem=send_sems.at[a, k],
                recv_sem=recv_sems.at[a, k], device_id=to, device_id_type=MESH)

        rng = range(self.n)
        mine = [pltpu.make_async_copy(ins[a], outs[a].at[_dev_index(*me)], local_sems.at[a]) for a in rng]
        first = [copy(a, 0, me, sibling, src=ins[a]) for a in rng]
        first += [copy(a, 1 + j, me, (*chip, c), src=ins[a]) for j, chip in enumerate(chips) for a in rng]
        landed = [copy(a, 1 + j, (*chip, c), me) for j, chip in enumerate(chips) for a in rng]
        passed = [copy(a, 4 + j, (*chip, c), sibling) for j, chip in enumerate(chips) for a in rng]
        from_sibling = [copy(a, 0, sibling, me) for a in rng]
        from_sibling += [copy(a, 4 + j, (*chip, 1 - c), me) for j, chip in enumerate(chips) for a in rng]
        return mine, first, landed, passed, from_sibling

    def start(self, ins, outs, sems):
        mine, first, _, _, _ = self._parts(ins, outs, sems)
        for cp in mine + first:
            cp.start()

    def forward(self, ins, outs, sems):
        _, _, landed, passed, _ = self._parts(ins, outs, sems)
        for got, fwd in zip(landed, passed):
            got.wait_recv()
            fwd.start()

    def finish(self, ins, outs, sems):
        mine, first, _, passed, from_sibling = self._parts(ins, outs, sems)
        for cp in from_sibling:
            cp.wait_recv()
        for cp in first + passed:
            cp.wait_send()
        for cp in mine:
            cp.wait()

    def before(self, ins, outs, sems, step, nsteps):
        pl.when(step == 0)(lambda: self.start(ins, outs, sems))
        pl.when(step == (7 * nsteps) // 8)(lambda: self.forward(ins, outs, sems))

    def after(self, ins, outs, sems, step, nsteps):
        pl.when(step == nsteps - 1)(lambda: self.finish(ins, outs, sems))


class _Exchange:
    def __init__(self, arrays):
        self.arrays = list(arrays)
        self.n = len(self.arrays)

    def out_shape(self):
        return [jax.ShapeDtypeStruct(b.shape, b.dtype) for b in self.arrays]

    def scratch(self):
        return [pltpu.SemaphoreType.DMA((self.n, 7)), pltpu.SemaphoreType.DMA((self.n, 7)),
                pltpu.SemaphoreType.DMA((self.n,))]

    def _parts(self, ins, outs, sems):
        send_sems, recv_sems, local_sems = sems
        x, y, c = _place()
        me = _dev_index(x, y, c)
        peers = [(x ^ (k >> 2), y ^ ((k >> 1) & 1), c ^ (k & 1)) for k in range(1, N_DEV)]
        rng = range(self.n)
        mine = [pltpu.make_async_copy(ins[a].at[me], outs[a].at[me], local_sems.at[a]) for a in rng]
        sent = [pltpu.make_async_remote_copy(
            src_ref=ins[a].at[_dev_index(*to)], dst_ref=outs[a].at[me], send_sem=send_sems.at[a, k],
            recv_sem=recv_sems.at[a, k], device_id=to, device_id_type=MESH) for k, to in enumerate(peers) for a in rng]
        arrivals = [pltpu.make_async_remote_copy(
            src_ref=ins[a].at[me], dst_ref=outs[a].at[_dev_index(*frm)], send_sem=send_sems.at[a, k],
            recv_sem=recv_sems.at[a, k], device_id=frm, device_id_type=MESH) for k, frm in enumerate(peers) for a in rng]
        return mine, sent, arrivals

    def start(self, ins, outs, sems):
        mine, sent, _ = self._parts(ins, outs, sems)
        for cp in mine + sent:
            cp.start()

    def finish(self, ins, outs, sems):
        mine, sent, arrivals = self._parts(ins, outs, sems)
        for cp in arrivals:
            cp.wait_recv()
        for cp in sent:
            cp.wait_send()
        for cp in mine:
            cp.wait()

    def before(self, ins, outs, sems, step, nsteps):
        pl.when(step == 0)(lambda: self.start(ins, outs, sems))

    def after(self, ins, outs, sems, step, nsteps):
        pl.when(step == nsteps - 1)(lambda: self.finish(ins, outs, sems))


class _Bcast(_Exchange):
    def out_shape(self):
        return [jax.ShapeDtypeStruct((N_DEV,) + s.shape, s.dtype) for s in self.arrays]

    def _parts(self, ins, outs, sems):
        send_sems, recv_sems, local_sems = sems
        x, y, c = _place()
        me = _dev_index(x, y, c)
        peers = [(x ^ (k >> 2), y ^ ((k >> 1) & 1), c ^ (k & 1)) for k in range(1, N_DEV)]
        rng = range(self.n)
        mine = [pltpu.make_async_copy(ins[a], outs[a].at[me], local_sems.at[a]) for a in rng]
        sent = [pltpu.make_async_remote_copy(
            src_ref=ins[a], dst_ref=outs[a].at[me], send_sem=send_sems.at[a, k], recv_sem=recv_sems.at[a, k],
            device_id=to, device_id_type=MESH) for k, to in enumerate(peers) for a in rng]
        arrivals = [pltpu.make_async_remote_copy(
            src_ref=ins[a], dst_ref=outs[a].at[_dev_index(*frm)], send_sem=send_sems.at[a, k],
            recv_sem=recv_sems.at[a, k], device_id=frm, device_id_type=MESH) for k, frm in enumerate(peers) for a in rng]
        return mine, sent, arrivals


class _Multi:
    def __init__(self, comms):
        self.comms = list(comms)
        self.arrays = [arr for c in self.comms for arr in c.arrays]
        self.n = len(self.arrays)

    def out_shape(self):
        return [s for c in self.comms for s in c.out_shape()]

    def scratch(self):
        return [s for c in self.comms for s in c.scratch()]

    def _each(self, ins, outs, sems):
        a = 0
        for j, c in enumerate(self.comms):
            yield c, ins[a:a + c.n], outs[a:a + c.n], sems[3 * j:3 * j + 3]
            a += c.n

    def before(self, ins, outs, sems, step, nsteps):
        for c, ci, co, cs in self._each(ins, outs, sems):
            c.before(ci, co, cs, step, nsteps)

    def after(self, ins, outs, sems, step, nsteps):
        for c, ci, co, cs in self._each(ins, outs, sems):
            c.after(ci, co, cs, step, nsteps)


def _comm_call(comms, name):
    ns = [c.n for c in comms]
    n = sum(ns)

    def body(*refs):
        parts, a, s = [], 0, 2 * n
        for c in comms:
            parts.append((c, refs[a:a + c.n], refs[n + a:n + a + c.n], refs[s:s + 3]))
            a, s = a + c.n, s + 3
        for c, ins, outs, sems in parts:
            c.start(ins, outs, sems)
        for c, ins, outs, sems in parts:
            if isinstance(c, _Gather):
                c.forward(ins, outs, sems)
        for c, ins, outs, sems in parts:
            c.finish(ins, outs, sems)

    res = pl.pallas_call(
        body, name=name, in_specs=[_ANY] * n, out_specs=[_ANY] * n,
        out_shape=[s for c in comms for s in c.out_shape()], scratch_shapes=[s for c in comms for s in c.scratch()],
    )(*[arr for c in comms for arr in c.arrays])
    out, a = [], 0
    for k in ns:
        out.append(res[a:a + k])
        a += k
    return out


def _pcall(body, args, *, name, grid, in_specs, out_specs, out_shape, scratch_shapes=(), sem="parallel", comm=None,
           step_axis=0):
    sem = (sem,) * len(grid) if isinstance(sem, str) else sem
    if comm is None:
        res = pl.pallas_call(body, name=name, grid=grid, in_specs=in_specs, out_specs=out_specs, out_shape=out_shape,
                             scratch_shapes=list(scratch_shapes), compiler_params=_params(*sem))(*args)
        return res, []
    n_in, n_out, n_scr, n = len(in_specs), len(out_specs), len(scratch_shapes), comm.n
    nsteps = grid[step_axis]
    assert all(g == 1 for ax, g in enumerate(grid) if ax != step_axis)

    def hosted(*refs):
        ins, cin = refs[:n_in], refs[n_in:n_in + n]
        o0 = n_in + n
        outs, cout = refs[o0:o0 + n_out], refs[o0 + n_out:o0 + n_out + n]
        s0 = o0 + n_out + n
        scr, sems = refs[s0:s0 + n_scr], refs[s0 + n_scr:]
        step = pl.program_id(step_axis)
        comm.before(cin, cout, sems, step, nsteps)
        body(*ins, *outs, *scr)
        comm.after(cin, cout, sems, step, nsteps)

    res = pl.pallas_call(
        hosted, name=name, grid=grid, in_specs=list(in_specs) + [_ANY] * n, out_specs=list(out_specs) + [_ANY] * n,
        out_shape=list(out_shape) + comm.out_shape(), scratch_shapes=list(scratch_shapes) + comm.scratch(),
        compiler_params=_params(*(("arbitrary",) * len(grid))))(*args, *comm.arrays)
    return res[:n_out], res[n_out:]


def _load_row_halves(top_hbm, bot_hbm, full_s, sems):
    r = top_hbm.shape[1]
    copies = [pltpu.make_async_copy(top_hbm, full_s.at[:, :r, :], sems.at[0]),
              pltpu.make_async_copy(bot_hbm, full_s.at[:, r:, :], sems.at[1])]
    for cp in copies:
        cp.start()
    for cp in copies:
        cp.wait()


def _in_proj(x, w_in_t, comm=None):
    S = x.shape[0]
    tb = min(1024, S)

    def body(x_ref, w_ref, q_ref, k_ref, v_ref, xr_ref, gr_ref):
        u = _dot_nt(x_ref[...].astype(BF16), w_ref[...])
        q_ref[...] = (u[:, :D_ATT] * (HEAD_DIM ** -0.5)).astype(BF16)
        k_ref[...] = u[:, D_ATT:D_ATT + D_KV].astype(BF16)
        v_ref[...] = u[:, D_ATT + D_KV:D_ATT + 2 * D_KV].astype(BF16)
        xr_ref[...] = u[:, D_ATT + 2 * D_KV:D_ATT + 2 * D_KV + D_RNN]
        gr_ref[...] = u[:, D_ATT + 2 * D_KV + D_RNN:]

    return _pcall(
        body, (x, w_in_t), name="in_proj", grid=(S // tb,), comm=comm,
        in_specs=[_rows(tb, D_MODEL), _resident((D_IN, D_MODEL))],
        out_specs=[_rows(tb, D_ATT), _rows(tb, D_KV), _rows(tb, D_KV), _rows(tb, D_RNN), _rows(tb, D_RNN)],
        out_shape=[jax.ShapeDtypeStruct((S, D_ATT), BF16), jax.ShapeDtypeStruct((S, D_KV), BF16),
                   jax.ShapeDtypeStruct((S, D_KV), BF16), jax.ShapeDtypeStruct((S, D_RNN), F32),
                   jax.ShapeDtypeStruct((S, D_RNN), F32)])


GROUP = N_HEADS // N_KV


def _band_mask(i):
    qi = lax.broadcasted_iota(jnp.int32, (GROUP * QBLK, 2 * QBLK), 0) & (QBLK - 1)
    sj = lax.broadcasted_iota(jnp.int32, (GROUP * QBLK, 2 * QBLK), 1)
    return (sj > qi) & (sj <= qi + QBLK) & ((sj >= QBLK) | (i > 0))


def _stack_heads(x, g):
    return jnp.concatenate([x[:, (g * GROUP + hh) * HEAD_DIM:(g * GROUP + hh + 1) * HEAD_DIM] for hh in range(GROUP)],
                           axis=0)


def _unstack_heads(x4):
    return [x4[hh * QBLK:(hh + 1) * QBLK] for hh in range(GROUP)]


def _sink_column(sink_ref, g):
    head = lax.broadcasted_iota(jnp.int32, (GROUP * QBLK, 1), 0) // QBLK
    col = jnp.full((GROUP * QBLK, 1), sink_ref[g * GROUP], F32)
    for hh in range(1, GROUP):
        col = jnp.where(head == hh, sink_ref[g * GROUP + hh], col)
    return col


ATT_STEP = 4
IN_GRAD_PARTS = 2


def _attn_specs(nq=1):
    cur = lambda i: (i, 0)
    prev = lambda i: (jnp.maximum(nq * i - 1, 0), 0)
    return [pl.BlockSpec((nq * QBLK, D_KV), cur), pl.BlockSpec((QBLK, D_KV), prev),
            pl.BlockSpec((nq * QBLK, D_KV), cur), pl.BlockSpec((QBLK, D_KV), prev)]


def _attn_fwd(q, k, v, sinks, comm=None):
    S = q.shape[0]
    nq = min(ATT_STEP, S // QBLK)

    def body(sink_ref, q_ref, kc_ref, kp_ref, vc_ref, vp_ref, o_ref, lse_ref):
        first = pl.program_id(0) * nq
        kall = jnp.concatenate([kp_ref[...], kc_ref[...]], axis=0)
        vall = jnp.concatenate([vp_ref[...], vc_ref[...]], axis=0)
        for b in range(nq):
            valid = _band_mask(first + b)
            rows = slice(b * QBLK, (b + 1) * QBLK)
            keys = slice(b * QBLK, (b + 2) * QBLK)
            qv = q_ref[rows, :]
            outs = []
            for g in range(N_KV):
                kcat = kall[keys, g * HEAD_DIM:(g + 1) * HEAD_DIM]
                vcat = vall[keys, g * HEAD_DIM:(g + 1) * HEAD_DIM]
                s = jnp.where(valid, _dot_nt(_stack_heads(qv, g), kcat), -1e30)
                sink = _sink_column(sink_ref, g)
                m = jnp.maximum(jnp.max(s, axis=1, keepdims=True), sink)
                p = jnp.exp(s - m)
                l = jnp.sum(p, axis=1, keepdims=True) + jnp.exp(sink - m)
                outs += _unstack_heads(_dot(p.astype(BF16), vcat) / l)
                lse_ref[(b * N_KV + g) * GROUP * QBLK:(b * N_KV + g + 1) * GROUP * QBLK, :] = m + jnp.log(l)
            o_ref[rows, :] = jnp.concatenate(outs, axis=1).astype(BF16)

    lse_rows = nq * N_HEADS * QBLK
    return _pcall(
        body, (sinks, q, k, k, v, v), name="attn_fwd", grid=(S // (nq * QBLK),), comm=comm,
        in_specs=[pl.BlockSpec(memory_space=pltpu.SMEM), _rows(nq * QBLK, D_ATT)] + _attn_specs(nq),
        out_specs=[_rows(nq * QBLK, D_ATT), _rows(lse_rows, 1)],
        out_shape=[jax.ShapeDtypeStruct((S, D_ATT), BF16), jax.ShapeDtypeStruct((S * N_HEADS, 1), F32)])


def _w_rows(w_ref):
    return [w_ref[k:k + 1, :] for k in range(w_ref.shape[0])]


def _conv4(x, halo, w, b):
    y = b + w[3] * x
    for s in (1, 2, 3):
        y = y + w[3 - s] * _shift_down(x, halo, s)
    return y


def _rnn_gates(xc, wa, wx, ba, bx, sp):
    xcb = xc.astype(BF16)
    r = _sigmoid(_dot(xcb, wa) + ba)
    ig = _sigmoid(_dot(xcb, wx) + bx)
    la = -LRU_C * r * sp
    a = jnp.exp(la)
    t = jnp.tanh(la)
    f = jnp.sqrt(-2.0 * t / (1.0 - t))
    return r, ig, a, f


def _rnn_fwd(xr, gr, conv_w, conv_b, wa, wx, ba, bx, lam, comm=None):
    S = xr.shape[0]
    tb = min(512, S)

    def body(xr_ref, gr_ref, cw_ref, cb_ref, wa_ref, wx_ref, ba_ref, bx_ref, lam_ref, rec_ref, h_ref,
             xc_ref, r_ref, ig_ref, a_ref, f_ref, halo_s, hc_s, a_s, b_s):
        @pl.when(pl.program_id(0) == 0)
        def _():
            halo_s[...] = jnp.zeros_like(halo_s)
            hc_s[...] = jnp.zeros_like(hc_s)

        x = xr_ref[...]
        xc = _conv4(x, halo_s[...], _w_rows(cw_ref), cb_ref[...])
        halo_s[...] = x[tb - 8:]
        r, ig, a, f = _rnn_gates(xc, wa_ref[...], wx_ref[...], ba_ref[...], bx_ref[...], _softplus_neg(lam_ref[...]))
        xc_ref[...] = xc
        r_ref[...] = r
        ig_ref[...] = ig
        a_ref[...] = a
        f_ref[...] = f
        a_s[...] = a
        b_s[...] = f * ig * xc
        row8 = lax.broadcasted_iota(jnp.int32, (8, D_RNN), 0)

        def tile(t, hc):
            o = pl.multiple_of(t * 8, 8)
            at = a_s[pl.ds(o, 8), :]
            bt = b_s[pl.ds(o, 8), :]
            for s in (1, 2, 4):
                keep = row8 >= s
                a_sh = jnp.where(keep, pltpu.roll(at, s, 0), 1.0)
                b_sh = jnp.where(keep, pltpu.roll(bt, s, 0), 0.0)
                bt = at * b_sh + bt
                at = at * a_sh
            ht = at * hc + bt
            b_s[pl.ds(o, 8), :] = ht
            return _row_sum(jnp.where(row8 == 7, ht, 0.0))

        hc_s[0:1, :] = lax.fori_loop(0, tb // 8, tile, hc_s[0:1, :], unroll=2)
        h = b_s[...]
        h_ref[...] = h
        rec_ref[...] = (h * _gelu(gr_ref[...])).astype(BF16)

    vec = _resident((1, D_RNN))
    kept = jax.ShapeDtypeStruct((S, D_RNN), F32)
    return _pcall(
        body, (xr, gr, conv_w, conv_b, wa, wx, ba, bx, lam), name="rnn_fwd", grid=(S // tb,), sem="arbitrary", comm=comm,
        in_specs=[_rows(tb, D_RNN), _rows(tb, D_RNN), _resident((4, D_RNN)), vec,
                  _resident((D_RNN, D_RNN)), _resident((D_RNN, D_RNN)), vec, vec, vec],
        out_specs=[_rows(tb, D_RNN)] * 7,
        out_shape=[jax.ShapeDtypeStruct((S, D_RNN), BF16), kept, kept, kept, kept, kept, kept],
        scratch_shapes=[pltpu.VMEM((8, D_RNN), F32), pltpu.VMEM((8, D_RNN), F32),
                        pltpu.VMEM((tb, D_RNN), F32), pltpu.VMEM((tb, D_RNN), F32)])


def _mix_ln1_up(x, att, rec, w_out, ln1_g, ln1_b, w_up_top, w_up_bot, fcw, fcb, comm=None):
    S = x.shape[0]
    tb = min(256, S)
    nblk, kh, wblk = w_up_top.shape
    half = nblk // 2

    def body(x_ref, att_ref, rec_ref, wo_ref, g_ref, b_ref, wt_hbm, wb_hbm, fcw_ref, fcb_ref,
             z1_ref, h1b_ref, gate_ref, act_ref, gl_ref, vdgl_ref, halo_s, wu_s, wu_sems):
        @pl.when(pl.program_id(0) == 0)
        def _():
            halo_s[...] = jnp.zeros_like(halo_s)
            _load_row_halves(wt_hbm, wb_hbm, wu_s, wu_sems)

        z1 = ALPHA * x_ref[...] + _dot(att_ref[...], wo_ref[:D_ATT, :]) + _dot(rec_ref[...], wo_ref[D_ATT:, :])
        z1_ref[...] = z1
        xhat, _ = _ln_stats(z1)
        h1b = (xhat * g_ref[...] + b_ref[...]).astype(BF16)
        h1b_ref[...] = h1b
        for jj in range(half):
            cols = slice(jj * wblk, (jj + 1) * wblk)
            gate = _dot(h1b, wu_s[jj])
            val = _dot(h1b, wu_s[jj + half])
            halo = halo_s[:, cols]
            conv = (fcb_ref[:, cols] + fcw_ref[2:3, cols] * gate + fcw_ref[1:2, cols] * _shift_down(gate, halo, 1)
                    + fcw_ref[0:1, cols] * _shift_down(gate, halo, 2))
            halo_s[:, cols] = gate[tb - 8:]
            gl, dgl = _gelu_and_grad(conv)
            gate_ref[:, cols] = gate.astype(BF16)
            act_ref[:, cols] = (gl * val).astype(BF16)
            gl_ref[:, cols] = gl.astype(BF16)
            vdgl_ref[:, cols] = (val * dgl).astype(BF16)

    vec = _resident((1, D_MODEL))
    wide = jax.ShapeDtypeStruct((S, D_FF), BF16)
    return _pcall(
        body, (x, att, rec, w_out, ln1_g, ln1_b, w_up_top, w_up_bot, fcw, fcb), name="mix_ln1_up", grid=(S // tb,),
        sem="arbitrary", comm=comm,
        in_specs=[_rows(tb, D_MODEL), _rows(tb, D_ATT), _rows(tb, D_RNN), _resident((D_MODEL, D_MODEL)), vec, vec,
                  _ANY, _ANY, _resident((3, D_FF)), _resident((1, D_FF))],
        out_specs=[_rows(tb, D_MODEL), _rows(tb, D_MODEL)] + [_rows(tb, D_FF)] * 4,
        out_shape=[jax.ShapeDtypeStruct((S, D_MODEL), F32), jax.ShapeDtypeStruct((S, D_MODEL), BF16), wide, wide, wide, wide],
        scratch_shapes=[pltpu.VMEM((8, D_FF), F32), pltpu.VMEM((nblk, 2 * kh, wblk), BF16),
                        pltpu.SemaphoreType.DMA((2,))])


def _tail(act, gl, vdgl, z1, h1b, p, tgt, w_down, w_pg, b_pg, w_pp, ln1_g, ln1_b, ln2_g, ln2_b):
    S = z1.shape[0]
    tb = min(256, S)

    def body(act_ref, gl_ref, vdgl_ref, z1_ref, h1b_ref, p_ref, t_ref, wd_ref, wpg_ref, bpg_ref, wpp_ref,
             g1_ref, b1_ref, g2_ref, b2_ref, dz2_ref, dpre_ref, dpp_ref, dgc_ref, dval_ref, dh1_ref, acc_ref):
        i = pl.program_id(0)

        @pl.when(i == 0)
        def _():
            acc_ref[...] = jnp.zeros_like(acc_ref)

        ffn = _dot(act_ref[...], wd_ref[...])
        xhat1, _ = _ln_stats(z1_ref[...])
        h1 = xhat1 * g1_ref[...] + b1_ref[...]
        sg = _sigmoid(_dot(h1b_ref[...], wpg_ref[...]) + bpg_ref[...])
        pp = _dot(p_ref[...].astype(BF16), wpp_ref[...])
        z2 = ALPHA * h1 + ffn + sg * pp
        xhat2, rstd2 = _ln_stats(z2)
        y = xhat2 * g2_ref[...] + b2_ref[...]
        err = y - t_ref[...]
        dy = err * (1.0 / D_MODEL)
        loss = 0.5 * jnp.sum(jnp.sum(err * err, axis=1, keepdims=True), axis=0, keepdims=True) * (1.0 / D_MODEL)
        dz2 = _ln_bwd(dy, xhat2, rstd2, g2_ref[...])
        dz2b = dz2.astype(BF16)
        dz2_ref[...] = dz2b
        dpre = dz2 * pp * sg * (1.0 - sg)
        dpreb = dpre.astype(BF16)
        dpre_ref[...] = dpreb
        dpp_ref[...] = (dz2 * sg).astype(BF16)
        dh1_ref[...] = ALPHA * dz2 + _dot_nt(dpreb, wpg_ref[...])
        dactb = _dot_nt(dz2b, wd_ref[...]).astype(BF16)
        dval_ref[...] = dactb * gl_ref[...]
        dgc_ref[...] = dactb * vdgl_ref[...]
        _put_rows(acc_ref, [_row_sum(dy * xhat2), _row_sum(dy), _row_sum(dpre),
                            jnp.broadcast_to(loss, (1, D_MODEL))])

    vec = _resident((1, D_MODEL))
    return pl.pallas_call(
        body, name="tail", grid=(S // tb,),
        in_specs=[_rows(tb, D_FF), _rows(tb, D_FF), _rows(tb, D_FF), _rows(tb, D_MODEL), _rows(tb, D_MODEL),
                  _rows(tb, PLE_DIM), _rows(tb, D_MODEL), _resident((D_FF, D_MODEL)), _resident((D_MODEL, D_MODEL)), vec,
                  _resident((PLE_DIM, D_MODEL)), vec, vec, vec, vec],
        out_specs=[_rows(tb, D_MODEL), _rows(tb, D_MODEL), _rows(tb, D_MODEL), _rows(tb, D_FF),
                   _rows(tb, D_FF), _rows(tb, D_MODEL), _acc((8, D_MODEL))],
        out_shape=[jax.ShapeDtypeStruct((S, D_MODEL), BF16),
                   jax.ShapeDtypeStruct((S, D_MODEL), BF16), jax.ShapeDtypeStruct((S, D_MODEL), BF16),
                   jax.ShapeDtypeStruct((S, D_FF), BF16), jax.ShapeDtypeStruct((S, D_FF), BF16),
                   jax.ShapeDtypeStruct((S, D_MODEL), F32), jax.ShapeDtypeStruct((8, D_MODEL), F32)],
        compiler_params=_params("arbitrary"),
    )(act, gl, vdgl, z1, h1b, p, tgt, w_down, w_pg, b_pg, w_pp, ln1_g, ln1_b, ln2_g, ln2_b)


def _weight_grad(a_list, b_list, name, layout, ts=512, comm=None, b_window=None, halves=False):
    S = a_list[0].shape[0]
    ms = [a.shape[1] for a in a_list]
    M, nb = sum(ms), len(b_list)
    win, Nb = b_window if b_window else (0, b_list[0].shape[1])
    ts = min(ts, S)
    nk = S // ts
    per_b = N_DEV // nb
    na = len(a_list)

    n_out = 2 if halves else 1
    assert layout == "cols" or not halves

    def body(*refs):
        a_refs, b_refs, o_refs, acc_ref = refs[:na], refs[na:na + nb], refs[na + nb:na + nb + n_out], refs[-1]
        o_ref = o_refs[0]
        j, k = pl.program_id(0), pl.program_id(1)

        @pl.when(k == 0)
        def _():
            acc_ref[...] = jnp.zeros_like(acc_ref)

        for jj in range(nb):
            @pl.when(j == jj)
            def _():
                b = b_refs[jj][...].astype(BF16)
                off = 0
                for a_ref, m in zip(a_refs, ms):
                    acc_ref[off:off + m, :] += _dot_tn(a_ref[...].astype(BF16), b)
                    off += m

        @pl.when(k == nk - 1)
        def _():
            for d in range(per_b):
                if layout == "rows":
                    o_ref[d] = acc_ref[d * (M // N_DEV):(d + 1) * (M // N_DEV), :].astype(BF16)
                elif layout == "cols" and halves:
                    for o_half, r0 in zip(o_refs, (0, M // 2)):
                        o_half[d] = acc_ref[r0:r0 + M // 2, d * (Nb // per_b):(d + 1) * (Nb // per_b)].astype(BF16)
                elif layout == "cols":
                    o_ref[d] = acc_ref[:, d * (Nb // per_b):(d + 1) * (Nb // per_b)].astype(BF16)
                else:
                    o_ref[d] = acc_ref[:, d * (Nb // per_b):(d + 1) * (Nb // per_b)].T.astype(BF16)

    def b_index(jj):
        return lambda j, k: (jnp.where(j == jj, k, jnp.where(j < jj, 0, nk - 1)), win)

    if layout == "rows":
        assert nb == 1
        blk = (N_DEV, M // N_DEV, Nb)
    elif layout == "cols":
        blk = (per_b, M // n_out, Nb // per_b)
    else:
        blk = (per_b, Nb // per_b, M)
    res, comm_res = _pcall(
        body, (*a_list, *b_list), name=name, grid=(nb, nk), sem="arbitrary", comm=comm, step_axis=1,
        in_specs=[pl.BlockSpec((ts, m), lambda j, k: (k, 0)) for m in ms]
        + [pl.BlockSpec((ts, Nb), b_index(jj)) for jj in range(nb)],
        out_specs=[pl.BlockSpec(blk, lambda j, k: (j, 0, 0))] * n_out,
        out_shape=[jax.ShapeDtypeStruct((N_DEV,) + blk[1:], BF16)] * n_out,
        scratch_shapes=[pltpu.VMEM((M, Nb), F32)])
    res = res if halves else res[0]
    return (res, comm_res) if comm is not None else res


def _pg_pp_grad(h1b, dpreb, p, dppb, ts=1024):
    S = h1b.shape[0]
    ts = min(ts, S)
    nk = S // ts
    rows, cols = D_MODEL // N_DEV, D_MODEL // N_DEV

    def body(h_ref, dpre_ref, p_ref, dpp_ref, gpg_ref, gpp_ref, acc_pg, acc_pp):
        k = pl.program_id(0)

        @pl.when(k == 0)
        def _():
            acc_pg[...] = jnp.zeros_like(acc_pg)
            acc_pp[...] = jnp.zeros_like(acc_pp)

        acc_pg[...] += _dot_tn(h_ref[...], dpre_ref[...])
        acc_pp[...] += _dot_tn(p_ref[...].astype(BF16), dpp_ref[...])

        @pl.when(k == nk - 1)
        def _():
            for d in range(N_DEV):
                gpg_ref[d] = acc_pg[d * rows:(d + 1) * rows, :].astype(BF16)
                gpp_ref[d] = acc_pp[:, d * cols:(d + 1) * cols].astype(BF16)

    return pl.pallas_call(
        body, name="pg_pp_grad", grid=(nk,),
        in_specs=[_rows(ts, D_MODEL), _rows(ts, D_MODEL), _rows(ts, PLE_DIM), _rows(ts, D_MODEL)],
        out_specs=[_acc((N_DEV, rows, D_MODEL)), _acc((N_DEV, PLE_DIM, cols))],
        out_shape=[jax.ShapeDtypeStruct((N_DEV, rows, D_MODEL), BF16), jax.ShapeDtypeStruct((N_DEV, PLE_DIM, cols), BF16)],
        scratch_shapes=[pltpu.VMEM((D_MODEL, D_MODEL), F32), pltpu.VMEM((PLE_DIM, D_MODEL), F32)],
        compiler_params=_params("arbitrary"))(h1b, dpreb, p, dppb)


def _up_bwd(dgc, gate, dval, dh1p, z1, w_up_top, w_up_bot, fcw, w_out, ln1_g, comm=None):
    S = z1.shape[0]
    tb = min(256, S)
    t16 = tb // 16
    n16 = S // 16
    nblk, kh, wblk = w_up_top.shape
    half = nblk // 2
    nsteps = S // tb

    def body(dgc_ref, dgn_ref, gc_ref, dval_ref, dh1p_ref, z1_ref, wt_hbm, wb_hbm, fcw_ref, wo_ref, g1_ref,
             dgate_ref, dz1_ref, dz1b_ref, datt_ref, drec_ref, accf_ref, accd_ref, wu_s, wu_sems):
        i = pl.program_id(0)

        @pl.when(i == 0)
        def _():
            accf_ref[...] = jnp.zeros_like(accf_ref)
            accd_ref[...] = jnp.zeros_like(accd_ref)
            _load_row_halves(wt_hbm, wb_hbm, wu_s, wu_sems)

        dg = dgc_ref[...].astype(F32)
        nxt = jnp.where(i < nsteps - 1, dgn_ref[...].astype(F32)[0:8], 0.0)
        w = _w_rows(fcw_ref)
        up1, up2 = _shift_up(dg, nxt, 1), _shift_up(dg, nxt, 2)
        dgate = (w[2] * dg + w[1] * up1 + w[0] * up2).astype(BF16)
        dgate_ref[...] = dgate
        gate = gc_ref[...].astype(F32)
        _put_rows(accf_ref, [_row_sum(up2 * gate), _row_sum(up1 * gate), _row_sum(dg * gate), _row_sum(dg)])

        dh1 = dh1p_ref[...]
        for j in range(nblk):
            src = dgate if j < half else dval_ref[...]
            jj = j % half
            dh1 = dh1 + _dot_nt(src[:, jj * wblk:(jj + 1) * wblk], wu_s[j])
        xhat1, rstd1 = _ln_stats(z1_ref[...])
        dz1 = _ln_bwd(dh1, xhat1, rstd1, g1_ref[...])
        dz1_ref[...] = dz1
        dz1b = dz1.astype(BF16)
        dz1b_ref[...] = dz1b
        dcat = _dot_nt(dz1b, wo_ref[...])
        datt_ref[...] = dcat[:, :D_ATT].astype(BF16)
        drec_ref[...] = dcat[:, D_ATT:]
        _put_rows(accd_ref, [_row_sum(dh1 * xhat1), _row_sum(dh1)])

    next16 = pl.BlockSpec((16, D_FF), lambda i: (jnp.minimum((i + 1) * t16, n16 - 1), 0))
    return _pcall(
        body, (dgc, dgc, gate, dval, dh1p, z1, w_up_top, w_up_bot, fcw, w_out, ln1_g), name="up_bwd",
        grid=(nsteps,), sem="arbitrary", comm=comm,
        in_specs=[_rows(tb, D_FF), next16, _rows(tb, D_FF), _rows(tb, D_FF), _rows(tb, D_MODEL),
                  _rows(tb, D_MODEL), _ANY, _ANY, _resident((3, D_FF)),
                  _resident((D_MODEL, D_MODEL)), _resident((1, D_MODEL))],
        scratch_shapes=[pltpu.VMEM((nblk, 2 * kh, wblk), BF16), pltpu.SemaphoreType.DMA((2,))],
        out_specs=[_rows(tb, D_FF), _rows(tb, D_MODEL), _rows(tb, D_MODEL), _rows(tb, D_ATT), _rows(tb, D_RNN),
                   _acc((8, D_FF)), _acc((8, D_MODEL))],
        out_shape=[jax.ShapeDtypeStruct((S, D_FF), BF16), jax.ShapeDtypeStruct((S, D_MODEL), F32),
                   jax.ShapeDtypeStruct((S, D_MODEL), BF16), jax.ShapeDtypeStruct((S, D_ATT), BF16),
                   jax.ShapeDtypeStruct((S, D_RNN), F32), jax.ShapeDtypeStruct((8, D_FF), F32),
                   jax.ShapeDtypeStruct((8, D_MODEL), F32)])


def _attn_bwd(q, k, v, lse, do, sinks, comm=None):
    S = q.shape[0]
    grp = N_HEADS // N_KV
    nq = min(ATT_STEP, S // QBLK)

    def body(sink_ref, q_ref, kc_ref, kp_ref, vc_ref, vp_ref, do_ref, lse_ref, dq_ref, dkc_ref, dkp_ref, dvc_ref, dvp_ref,
             ds_ref):
        i = pl.program_id(0)

        @pl.when(i == 0)
        def _():
            ds_ref[...] = jnp.zeros_like(ds_ref)

        row8 = lax.broadcasted_iota(jnp.int32, (8, 128), 0)
        lane8 = lax.broadcasted_iota(jnp.int32, (8, 128), 1)
        dsink = jnp.zeros((8, 128), F32)
        kall = jnp.concatenate([kp_ref[...], kc_ref[...]], axis=0)
        vall = jnp.concatenate([vp_ref[...], vc_ref[...]], axis=0)
        dk_t = [jnp.zeros((D_KV, QBLK), F32) for _ in range(nq + 1)]
        dv_t = [jnp.zeros((D_KV, QBLK), F32) for _ in range(nq + 1)]
        for b in range(nq):
            valid = _band_mask(i * nq + b)
            rows = slice(b * QBLK, (b + 1) * QBLK)
            keys = slice(b * QBLK, (b + 2) * QBLK)
            qv, dov = q_ref[rows, :], do_ref[rows, :]
            dqs, dks, dvs = [], [], []
            for g in range(N_KV):
                kcat = kall[keys, g * HEAD_DIM:(g + 1) * HEAD_DIM]
                vcat = vall[keys, g * HEAD_DIM:(g + 1) * HEAD_DIM]
                q4, do4 = _stack_heads(qv, g), _stack_heads(dov, g)
                s = jnp.where(valid, _dot_nt(q4, kcat), -1e30)
                lse = lse_ref[(b * N_KV + g) * GROUP * QBLK:(b * N_KV + g + 1) * GROUP * QBLK, :]
                p = jnp.exp(s - lse)
                p_sink = jnp.exp(_sink_column(sink_ref, g) - lse)
                dp = _dot_nt(do4, vcat)
                delta = jnp.sum(p * dp, axis=1, keepdims=True)
                dsc = (p * (dp - delta)).astype(BF16)
                dqs += _unstack_heads(_dot(dsc, kcat) * (HEAD_DIM ** -0.5))
                dks.append(_dot_tn(q4, dsc))
                dvs.append(_dot_tn(do4, p.astype(BF16)))
                for hh, part in enumerate(_unstack_heads(-p_sink * delta)):
                    here = (row8 == 0) & (lane8 == g * grp + hh)
                    dsink = dsink + jnp.where(here, jnp.sum(part, axis=0, keepdims=True), 0.0)
            dq_ref[rows, :] = jnp.concatenate(dqs, axis=1).astype(BF16)
            dk2, dv2 = jnp.concatenate(dks, axis=0), jnp.concatenate(dvs, axis=0)
            dk_t[b], dk_t[b + 1] = dk_t[b] + dk2[:, :QBLK], dk_t[b + 1] + dk2[:, QBLK:]
            dv_t[b], dv_t[b + 1] = dv_t[b] + dv2[:, :QBLK], dv_t[b + 1] + dv2[:, QBLK:]
        dkp_ref[...] = dk_t[0].T
        dvp_ref[...] = dv_t[0].T
        for b in range(nq):
            dkc_ref[b * QBLK:(b + 1) * QBLK, :] = dk_t[b + 1].T
            dvc_ref[b * QBLK:(b + 1) * QBLK, :] = dv_t[b + 1].T
        ds_ref[...] += dsink

    nsteps = S // (nq * QBLK)
    cur = jax.ShapeDtypeStruct((S, D_KV), F32)
    prev = jax.ShapeDtypeStruct((nsteps * QBLK, D_KV), F32)
    big = _rows(nq * QBLK, D_ATT)
    return _pcall(
        body, (sinks, q, k, k, v, v, do, lse), name="attn_bwd", grid=(nsteps,), sem="arbitrary", comm=comm,
        in_specs=[pl.BlockSpec(memory_space=pltpu.SMEM), big] + _attn_specs(nq) + [big, _rows(nq * N_HEADS * QBLK, 1)],
        out_specs=[big, _rows(nq * QBLK, D_KV), _rows(QBLK, D_KV), _rows(nq * QBLK, D_KV), _rows(QBLK, D_KV),
                   _acc((8, 128))],
        out_shape=[jax.ShapeDtypeStruct((S, D_ATT), BF16), cur, prev, cur, prev, jax.ShapeDtypeStruct((8, 128), F32)])


def _rnn_bwd(xr, gr, h, kept, drec, conv_w, wa, wx, lam, comm=None):
    S = xr.shape[0]
    tb = min(512, S)
    t8 = tb // 8
    nsteps = S // tb

    def body(xr_ref, xp_ref, gr_ref, h_ref, hp_ref, xc_ref, r_ref, ig_ref, a_ref, f_ref, drec_ref, cw_ref, wa_ref, wx_ref,
             lam_ref, dxr_ref, dgr_ref, gwa_ref, gwx_ref, acc_ref, carry_s, dxc_halo_s, d_s, gwa_s, gwx_s):
        i = pl.program_id(0)
        blk = nsteps - 1 - i

        @pl.when(i == 0)
        def _():
            gwa_s[...] = jnp.zeros_like(gwa_s)
            gwx_s[...] = jnp.zeros_like(gwx_s)
            acc_ref[...] = jnp.zeros_like(acc_ref)
            carry_s[...] = jnp.zeros_like(carry_s)
            dxc_halo_s[...] = jnp.zeros_like(dxc_halo_s)

        x = xr_ref[...]
        xhalo = jnp.where(blk > 0, xp_ref[...], 0.0)
        cw = _w_rows(cw_ref)
        xs = [_shift_down(x, xhalo, 3), _shift_down(x, xhalo, 2), _shift_down(x, xhalo, 1), x]
        xc, r, ig, a, f = xc_ref[...], r_ref[...], ig_ref[...], a_ref[...], f_ref[...]
        sp = _softplus_neg(lam_ref[...])
        hcur = h_ref[...]
        hprev = _shift_down(hcur, jnp.where(blk > 0, hp_ref[...], 0.0), 1)
        gl, dgl = _gelu_and_grad(gr_ref[...])
        drec = drec_ref[...]
        dgr_ref[...] = (drec * hcur * dgl).astype(BF16)
        d_s[...] = drec * gl
        row8 = lax.broadcasted_iota(jnp.int32, (8, D_RNN), 0)

        def tile(t, c):
            o = pl.multiple_of((t8 - 1 - t) * 8, 8)
            a8 = a_ref[pl.ds(o, 8), :]
            dt = d_s[pl.ds(o, 8), :]
            at = jnp.where(row8 == 7, 1.0, pltpu.roll(a8, 7, 0))
            for s in (1, 2, 4):
                keep = row8 < 8 - s
                a_sh = jnp.where(keep, pltpu.roll(at, 8 - s, 0), 1.0)
                d_sh = jnp.where(keep, pltpu.roll(dt, 8 - s, 0), 0.0)
                dt = at * d_sh + dt
                at = at * a_sh
            lt = at * c + dt
            d_s[pl.ds(o, 8), :] = lt
            return _row_sum(jnp.where(row8 == 0, a8 * lt, 0.0))

        carry_s[0:1, :] = lax.fori_loop(0, t8, tile, carry_s[0:1, :], unroll=2)
        lmb = d_s[...]
        a2 = a * a
        dla = lmb * hprev * a - lmb * ig * xc * (a2 / f)
        di = lmb * f * xc
        dr = dla * (-LRU_C) * sp
        dpa = dr * r * (1.0 - r)
        dpx = di * ig * (1.0 - ig)
        dpab = dpa.astype(BF16)
        dpxb = dpx.astype(BF16)
        xcb = xc.astype(BF16)
        gwa_s[...] += _dot_tn(xcb, dpab)
        gwx_s[...] += _dot_tn(xcb, dpxb)

        @pl.when(i == nsteps - 1)
        def _():
            for dense, out in ((gwa_s[...], gwa_ref), (gwx_s[...], gwx_ref)):
                for b in range(RNN_BLOCKS):
                    rows = slice(b * HEAD_DIM, (b + 1) * HEAD_DIM)
                    out[rows, :] = dense[rows, b * HEAD_DIM:(b + 1) * HEAD_DIM]

        dxc = lmb * f * ig + _dot_nt(dpab, wa_ref[...]) + _dot_nt(dpxb, wx_ref[...])
        nxt = dxc_halo_s[...]
        dxr = cw[3] * dxc
        for s in (1, 2, 3):
            dxr = dxr + cw[3 - s] * _shift_up(dxc, nxt, s)
        dxr_ref[...] = dxr.astype(BF16)
        dxc_halo_s[...] = dxc[:8]
        dlam = _row_sum(dla * (-LRU_C) * r) * (-1.0 / (1.0 + jnp.exp(lam_ref[...])))
        _put_rows(acc_ref, [_row_sum(dxc * xs[0]), _row_sum(dxc * xs[1]), _row_sum(dxc * xs[2]), _row_sum(dxc * xs[3]),
                            _row_sum(dxc), _row_sum(dpa), _row_sum(dpx), dlam])

    rev = lambda i: (nsteps - 1 - i, 0)
    prev8 = lambda i: (jnp.maximum((nsteps - 1 - i) * t8 - 1, 0), 0)
    blkspec = pl.BlockSpec((tb, D_RNN), rev)
    halo8 = pl.BlockSpec((8, D_RNN), prev8)
    vec = _resident((1, D_RNN))
    return _pcall(
        body, (xr, xr, gr, h, h, *kept, drec, conv_w, wa, wx, lam), name="rnn_bwd", grid=(nsteps,),
        sem="arbitrary", comm=comm,
        in_specs=[blkspec, halo8, blkspec, blkspec, halo8] + [blkspec] * 6
        + [_resident((4, D_RNN)), _resident((D_RNN, D_RNN)), _resident((D_RNN, D_RNN)), vec],
        out_specs=[blkspec, blkspec, _acc((D_RNN, HEAD_DIM)), _acc((D_RNN, HEAD_DIM)), _acc((8, D_RNN))],
        out_shape=[jax.ShapeDtypeStruct((S, D_RNN), BF16), jax.ShapeDtypeStruct((S, D_RNN), BF16),
                   jax.ShapeDtypeStruct((D_RNN, HEAD_DIM), F32), jax.ShapeDtypeStruct((D_RNN, HEAD_DIM), F32),
                   jax.ShapeDtypeStruct((8, D_RNN), F32)],
        scratch_shapes=[pltpu.VMEM((8, D_RNN), F32), pltpu.VMEM((8, D_RNN), F32), pltpu.VMEM((tb, D_RNN), F32),
                        pltpu.VMEM((D_RNN, D_RNN), F32), pltpu.VMEM((D_RNN, D_RNN), F32)])


def _in_bwd(dq, dkc, dkp, dvc, dvp, dxr, dgr, dz1, w_in, comm=None):
    S = dz1.shape[0]
    tb = min(ATT_STEP * QBLK, S)
    nsteps = S // tb
    ring = 3

    def body(dq_ref, dkc_ref, dkn_ref, dvc_ref, dvn_ref, dxr_ref, dgr_ref, dz1_hbm, w_ref, dkv_ref, dx_ref,
             ring_s, ring_sems):
        i = pl.program_id(0)
        last = i == nsteps - 1

        def fetch(step):
            slot = step % ring
            rows = pl.ds(pl.multiple_of(step * tb, tb), tb)
            return pltpu.make_async_copy(dz1_hbm.at[rows, :], ring_s.at[slot], ring_sems.at[slot])

        @pl.when(i == 0)
        def _():
            for step in range(min(ring - 1, nsteps)):
                fetch(step).start()

        @pl.when(i + ring - 1 < nsteps)
        def _():
            fetch(i + ring - 1).start()

        def total(cur_ref, next_ref):
            nxt = jnp.where(last, 0.0, next_ref[...])
            tail = cur_ref[tb - QBLK:, :] + nxt
            return jnp.concatenate([cur_ref[:tb - QBLK, :], tail], axis=0) if tb > QBLK else tail

        dkv = jnp.concatenate([total(dkc_ref, dkn_ref), total(dvc_ref, dvn_ref)], axis=1).astype(BF16)
        dkv_ref[...] = dkv
        du = jnp.concatenate([dq_ref[...], dkv, dxr_ref[...], dgr_ref[...]], axis=1)
        prod = _dot(du, w_ref[...])
        fetch(i).wait()
        dx_ref[...] = ALPHA * ring_s[i % ring] + prod

    nextp = pl.BlockSpec((QBLK, D_KV), lambda i: (jnp.minimum(i + 1, nsteps - 1), 0))
    return _pcall(
        body, (dq, dkc, dkp, dvc, dvp, dxr, dgr, dz1, w_in), name="in_bwd", grid=(nsteps,), sem="arbitrary", comm=comm,
        in_specs=[_rows(tb, D_ATT), _rows(tb, D_KV), nextp, _rows(tb, D_KV), nextp,
                  _rows(tb, D_RNN), _rows(tb, D_RNN), _ANY, _resident((D_IN, D_MODEL))],
        out_specs=[_rows(tb, 2 * D_KV), _rows(tb, D_MODEL)],
        out_shape=[jax.ShapeDtypeStruct((S, 2 * D_KV), BF16), jax.ShapeDtypeStruct((S, D_MODEL), F32)],
        scratch_shapes=[pltpu.VMEM((ring, tb, D_MODEL), F32), pltpu.SemaphoreType.DMA((ring,))])


def _block_diag(w):
    eye = jnp.eye(RNN_BLOCKS, dtype=w.dtype)
    return (w[:, :, None, :] * eye[:, None, :, None]).reshape(D_RNN, D_RNN).astype(BF16)


def _adamw(w, g, m, v):
    m = ADAM_B1 * m + (1.0 - ADAM_B1) * g
    v = ADAM_B2 * v + (1.0 - ADAM_B2) * (g * g)
    m_hat = m / (1.0 - ADAM_B1 ** ADAM_STEP)
    v_hat = v / (1.0 - ADAM_B2 ** ADAM_STEP)
    delta = -ADAM_LR * (m_hat / (jnp.sqrt(v_hat) + ADAM_EPS) + ADAM_WD * w)
    return delta, m, v


def _sum_adamw(parts, w, m, v, name):
    parts = parts if isinstance(parts, (list, tuple)) else [parts]
    R, C = w.shape
    rb = R if R <= 256 else (256 if parts[0].shape[1] % 256 == 0 else 128)
    per = parts[0].shape[1] // rb
    assert R % rb == 0 and parts[0].shape[1] % rb == 0
    n = len(parts)

    def body(*refs):
        p_refs = refs[:n]
        w_ref, m_ref, v_ref, g_out, d_out, m_out, v_out = refs[n:]
        which = pl.program_id(0) // per

        def total(p_ref):
            g = p_ref[0].astype(F32)
            for d in range(1, N_DEV):
                g = g + p_ref[d].astype(F32)
            return g

        g = total(p_refs[0])
        for j in range(1, n):
            g = jnp.where(which == j, total(p_refs[j]), g)
        delta, mn, vn = _adamw(w_ref[...], g, m_ref[...], v_ref[...])
        g_out[...] = g
        d_out[...] = delta
        m_out[...] = mn
        v_out[...] = vn

    def part_spec(j):
        return pl.BlockSpec((N_DEV, rb, C), lambda i: (0, jnp.clip(i - j * per, 0, per - 1), 0))

    blk = _rows(rb, C)
    out = jax.ShapeDtypeStruct((R, C), F32)
    return pl.pallas_call(
        body, name=name, grid=(R // rb,),
        in_specs=[part_spec(j) for j in range(n)] + [blk, blk, blk],
        out_specs=[blk, blk, blk, blk], out_shape=[out, out, out, out],
        compiler_params=_params("parallel"),
    )(*parts, w, m, v)


def _sum_adamw_group(items, name):
    n = len(items)

    def body(*refs):
        ins, outs = refs[:4 * n], refs[4 * n:]
        for j in range(n):
            p_ref, w_ref, m_ref, v_ref = ins[4 * j:4 * j + 4]
            g = p_ref[0].astype(F32)
            for d in range(1, N_DEV):
                g = g + p_ref[d].astype(F32)
            delta, mn, vn = _adamw(w_ref[...], g, m_ref[...], v_ref[...])
            for o_ref, val in zip(outs[4 * j:4 * j + 4], (g, delta, mn, vn)):
                o_ref[...] = val

    out_shape = [jax.ShapeDtypeStruct(w.shape, F32) for _, w, _, _ in items for _ in range(4)]
    res = pl.pallas_call(body, name=name, out_shape=out_shape, compiler_params=_params())(
        *[a for item in items for a in item])
    return [res[4 * j:4 * j + 4] for j in range(n)]


_SMALL = [("attn_sinks", "s", 0, 1, None), ("rnn_conv_w", "r", 0, 4, "cols"), ("rnn_conv_b", "r", 4, 1, None),
          ("gate_a_w", "a", 0, D_RNN, None), ("gate_a_b", "r", 5, 1, None), ("gate_x_w", "x", 0, D_RNN, None),
          ("gate_x_b", "r", 6, 1, None), ("lru_lambda", "r", 7, 1, None), ("ln1_g", "d", 0, 1, None),
          ("ln1_b", "d", 1, 1, None), ("ffn_conv_w", "f", 0, 3, "cols"), ("ffn_conv_b", "f", 3, 1, None),
          ("ple_gate_b", "t", 2, 1, None), ("ln2_g", "t", 0, 1, None), ("ln2_b", "t", 1, 1, None)]
_LOSS_ROW = 3


_ACC_COLS = {"t": (0, D_MODEL), "f": (D_MODEL, D_FF), "d": (D_MODEL + D_FF, D_MODEL), "s": (2 * D_MODEL + D_FF, 128),
             "r": (2 * D_MODEL + D_FF + 128, D_RNN)}
_ACC_WIDTH = 2 * D_MODEL + D_FF + 128 + D_RNN


def _small_update(rows_all, gates_all, params):
    flat = [arr for triple in params for arr in triple]
    n_par = len(_SMALL)

    def body(*refs):
        rows_ref, gates_ref = refs[:2]
        p_refs = refs[2:2 + 3 * n_par]
        loss_ref = refs[2 + 3 * n_par]
        o_refs = refs[3 + 3 * n_par:3 + 7 * n_par]
        rows_s, tmp_r, tmp_f = refs[3 + 7 * n_par:]
        me = _dev_index(*_place())
        rows_sum, gates_sum = rows_ref[0], gates_ref[0]
        for d in range(1, N_DEV):
            rows_sum = rows_sum + rows_ref[d]
            gates_sum = gates_sum + gates_ref[d]
        rows_s[...] = rows_sum
        t0 = _ACC_COLS["t"][0]
        loss_ref[...] = rows_s[_LOSS_ROW:_LOSS_ROW + 1, t0:t0 + 128]
        for i, (name, key, row, rows, how) in enumerate(_SMALL):
            w_ref, m_ref, v_ref = p_refs[3 * i:3 * i + 3]
            g_out, d_out, m_out, v_out = o_refs[4 * i:4 * i + 4]
            if key == "a":
                g = gates_sum[:, :HEAD_DIM]
            elif key == "x":
                g = gates_sum[:, HEAD_DIM:]
            elif how == "cols":
                c0, width = _ACC_COLS[key]
                full = rows_s[:, c0:c0 + width]
                shard = width // N_DEV
                mine = full[:, :shard]
                for d in range(1, N_DEV):
                    mine = jnp.where(me == d, full[:, d * shard:(d + 1) * shard], mine)
                tmp = tmp_r if key == "r" else tmp_f
                tmp[...] = mine
                g = tmp[row:row + rows, :]
            else:
                c0, width = _ACC_COLS[key]
                g = rows_s[row:row + rows, c0:c0 + width][:, :w_ref.shape[1]]
            delta, mn, vn = _adamw(w_ref[...], g, m_ref[...], v_ref[...])
            g_out[...] = g
            d_out[...] = delta
            m_out[...] = mn
            v_out[...] = vn

    outs = [jax.ShapeDtypeStruct((1, 128), F32)]
    for w, _, _ in params:
        outs += [jax.ShapeDtypeStruct(w.shape, F32)] * 4
    scratch = [pltpu.VMEM((8, _ACC_WIDTH), F32), pltpu.VMEM((8, D_RNN // N_DEV), F32), pltpu.VMEM((8, D_FF // N_DEV), F32)]
    res = pl.pallas_call(body, name="small_update", out_shape=outs, scratch_shapes=scratch)(rows_all, gates_all, *flat)
    return res[0], [res[1 + 4 * i:5 + 4 * i] for i in range(n_par)]


def kernel(x, p, w_in, attn_sinks, rnn_conv_w, rnn_conv_b, gate_a_w, gate_a_b, gate_x_w, gate_x_b, lru_lambda, w_out, ln1_g, ln1_b, w_ffn_up, ffn_conv_w, ffn_conv_b, w_ffn_down, ple_gate_w, ple_gate_b, ple_proj, ln2_g, ln2_b, loss_target, m_w_in, m_attn_sinks, m_rnn_conv_w, m_rnn_conv_b, m_gate_a_w, m_gate_a_b, m_gate_x_w, m_gate_x_b, m_lru_lambda, m_w_out, m_ln1_g, m_ln1_b, m_w_ffn_up, m_ffn_conv_w, m_ffn_conv_b, m_w_ffn_down, m_ple_gate_w, m_ple_gate_b, m_ple_proj, m_ln2_g, m_ln2_b, v_w_in, v_attn_sinks, v_rnn_conv_w, v_rnn_conv_b, v_gate_a_w, v_gate_a_b, v_gate_x_w, v_gate_x_b, v_lru_lambda, v_w_out, v_ln1_g, v_ln1_b, v_w_ffn_up, v_ffn_conv_w, v_ffn_conv_b, v_w_ffn_down, v_ple_gate_w, v_ple_gate_b, v_ple_proj, v_ln2_g, v_ln2_b):
    from_col_blocks = lambda g: g.transpose(1, 0, 2).reshape(g.shape[1], N_DEV * g.shape[2])

    xs, ps, tgt, sinks = x[0], p[0, 0], loss_target[0], attn_sinks[0]
    wa, wx = _block_diag(gate_a_w[0]), _block_diag(gate_x_w[0])

    conv_cols = jnp.concatenate([rnn_conv_w[0].reshape(1, -1), ffn_conv_w[0].reshape(1, -1)], axis=1)
    n_rc, n_fc = 4 * D_RNN // N_DEV, 3 * D_FF // N_DEV
    ((g_in,),) = _comm_call([_Gather([w_in[0].T.astype(BF16)])], "gather_w_in")
    w_in_full = g_in.reshape(D_IN, D_MODEL)

    (q, k, v, xr, gr), _ = _in_proj(xs, w_in_full)
    w_up_shard = w_ffn_up[0].astype(BF16)
    (att, lse), (g_out, w_up_top, g_conv) = _attn_fwd(
        q, k, v, sinks,
        comm=_Multi([_Gather([w_out[0].astype(BF16), w_up_shard[:D_MODEL // 2]]),
                     _Bcast([jnp.broadcast_to(conv_cols, (8, n_rc + n_fc))])]))
    rcw = from_col_blocks(g_conv[:, 0, :n_rc].reshape(N_DEV, 4, D_RNN // N_DEV))
    fcw = from_col_blocks(g_conv[:, 0, n_rc:].reshape(N_DEV, 3, D_FF // N_DEV))
    (rec, h, *kept), (w_up_bot,) = _rnn_fwd(xr, gr, rcw, rnn_conv_b, wa, wx, gate_a_b, gate_x_b, lru_lambda,
                                            comm=_Gather([w_up_shard[D_MODEL // 2:]]))
    w_out_full = g_out.reshape(D_MODEL, D_MODEL)
    (z1, h1b, gate, act, gl, vdgl), (g_down, g_pg, g_pp) = _mix_ln1_up(
        xs, att, rec, w_out_full, ln1_g, ln1_b, w_up_top, w_up_bot, fcw, ffn_conv_b,
        comm=_Gather([w_ffn_down[0].astype(BF16), ple_gate_w[0].astype(BF16), ple_proj[0].astype(BF16)]))
    dz2b, dpreb, dppb, dgc, dval, dh1p, acc_t = _tail(
        act, gl, vdgl, z1, h1b, ps, tgt, g_down.reshape(D_FF, D_MODEL), g_pg.reshape(D_MODEL, D_MODEL), ple_gate_b,
        from_col_blocks(g_pp), ln1_g, ln1_b, ln2_g, ln2_b)

    gd_down = _weight_grad([dz2b], [act], "down_grad", "rows_t", ts=1024)
    gd_pg, gd_pp = _pg_pp_grad(h1b, dpreb, ps, dppb)
    (dgate, dz1, dz1b, datt, drec, acc_f, acc_d), (r_down, r_pg, r_pp) = _up_bwd(
        dgc, gate, dval, dh1p, z1, w_up_top, w_up_bot, fcw, w_out_full, ln1_g, comm=_Exchange([gd_down, gd_pg, gd_pp]))
    gd_up_top, gd_up_bot = _weight_grad([h1b], [dgate, dval], "up_grad", "cols", halves=True)
    gd_out = _weight_grad([att, rec], [dz1b], "out_grad", "rows", ts=1024)
    (dq, dkc, dkp, dvc, dvp, acc_s), (r_up_top,) = _attn_bwd(q, k, v, lse, datt, sinks, comm=_Exchange([gd_up_top]))
    early = jnp.concatenate([acc_t, acc_f, acc_d], axis=1)
    (dxr, dgr, g_wa, g_wx, acc_r), (r_up_bot, r_out, early_all) = _rnn_bwd(
        xr, gr, h, kept, drec, rcw, wa, wx, lru_lambda, comm=_Multi([_Exchange([gd_up_bot, gd_out]), _Gather([early])]))
    (dkv, dx), _ = _in_bwd(dq, dkc, dkp, dvc, dvp, dxr, dgr, dz1, w_in_full)
    du_parts = [dq, dkv, dxr, dgr]
    lanes = D_RNN // 128
    late = jnp.concatenate([g_wa, g_wx], axis=1)
    late = jnp.concatenate([late, acc_s, acc_r.reshape(8, lanes, 128).transpose(1, 0, 2).reshape(8 * lanes, 128)], axis=0)
    width = D_MODEL // IN_GRAD_PARTS
    comm, r_parts = _Gather([late]), []
    for part in range(IN_GRAD_PARTS):
        gd_part, got = _weight_grad(du_parts, [xs], f"in_grad_{part}", "rows", ts=1024, b_window=(part, width), comm=comm)
        if part == 0:
            (late_all,) = got
        else:
            r_parts += got
        comm = _Exchange([gd_part])
    r_parts += _comm_call([comm], "exchange_w_in")[0]
    r_in = jnp.concatenate(r_parts, axis=2)
    acc_r_all = late_all[:, D_RNN + 8:].reshape(N_DEV, lanes, 8, 128).transpose(0, 2, 1, 3).reshape(N_DEV, 8, D_RNN)
    small_parts = (jnp.concatenate([early_all, late_all[:, D_RNN:D_RNN + 8], acc_r_all], axis=2),
                   late_all[:, :D_RNN])

    outs = {}
    res = _sum_adamw([r_up_top, r_up_bot], w_ffn_up[0], m_w_ffn_up[0], v_w_ffn_up[0], "adamw_w_ffn_up")
    outs["w_ffn_up"] = [r[None] for r in res]
    small_shards = [("w_out", r_out, w_out, m_w_out, v_w_out), ("ple_gate_w", r_pg, ple_gate_w, m_ple_gate_w, v_ple_gate_w),
                    ("ple_proj", r_pp, ple_proj, m_ple_proj, v_ple_proj),
                    ("w_ffn_down", r_down, w_ffn_down, m_w_ffn_down, v_w_ffn_down)]
    group = _sum_adamw_group([(parts, w[0], m[0], v[0]) for _, parts, w, m, v in small_shards]
                             + [(r_in, w_in[0].T, m_w_in[0].T, v_w_in[0].T)], "adamw_small_shards")
    for (name, *_), res in zip(small_shards, group):
        outs[name] = [r[None] for r in res]
    outs["w_in"] = [r.T[None] for r in group[-1]]

    given = dict(attn_sinks=(attn_sinks, m_attn_sinks, v_attn_sinks), rnn_conv_w=(rnn_conv_w, m_rnn_conv_w, v_rnn_conv_w),
                 rnn_conv_b=(rnn_conv_b, m_rnn_conv_b, v_rnn_conv_b), gate_a_w=(gate_a_w, m_gate_a_w, v_gate_a_w),
                 gate_a_b=(gate_a_b, m_gate_a_b, v_gate_a_b), gate_x_w=(gate_x_w, m_gate_x_w, v_gate_x_w),
                 gate_x_b=(gate_x_b, m_gate_x_b, v_gate_x_b), lru_lambda=(lru_lambda, m_lru_lambda, v_lru_lambda),
                 ln1_g=(ln1_g, m_ln1_g, v_ln1_g), ln1_b=(ln1_b, m_ln1_b, v_ln1_b),
                 ffn_conv_w=(ffn_conv_w, m_ffn_conv_w, v_ffn_conv_w), ffn_conv_b=(ffn_conv_b, m_ffn_conv_b, v_ffn_conv_b),
                 ple_gate_b=(ple_gate_b, m_ple_gate_b, v_ple_gate_b), ln2_g=(ln2_g, m_ln2_g, v_ln2_g),
                 ln2_b=(ln2_b, m_ln2_b, v_ln2_b))
    as_2d = lambda a: a.reshape(-1, a.shape[-1])
    loss_row, small_res = _small_update(*small_parts, [tuple(as_2d(a) for a in given[n]) for n, *_ in _SMALL])
    loss = loss_row[0, 0]
    for (n, *_), res in zip(_SMALL, small_res):
        outs[n] = [r.reshape(given[n][0].shape) for r in res]

    order = ["w_in", "attn_sinks", "rnn_conv_w", "rnn_conv_b", "gate_a_w", "gate_a_b", "gate_x_w", "gate_x_b",
             "lru_lambda", "w_out", "ln1_g", "ln1_b", "w_ffn_up", "ffn_conv_w", "ffn_conv_b", "w_ffn_down",
             "ple_gate_w", "ple_gate_b", "ple_proj", "ln2_g", "ln2_b"]
    return (loss, dx[None], *[outs[n][0] for n in order], *[outs[n][1] for n in order],
            *[outs[n][2] for n in order], *[outs[n][3] for n in order])
```
